```python
import jax
import jax.numpy as jnp
from jax import lax
import numpy as np

D_MODEL = 1024
BATCH = 8
SEQ = 4096
DEPTH = 1

HEAD_DIM = 64
ATTN_WIDTH = D_MODEL // 2
N_ATTN_HEADS = ATTN_WIDTH // HEAD_DIM
N_KV_HEADS = 2
KV_WIDTH = N_KV_HEADS * HEAD_DIM
WINDOW = 128
BLOCK = 128
ROPE_THETA = 500000.0
ROTARY_DIM = HEAD_DIM // 4
POOL_WINDOWS = (2, 4, 8, 16)
N_POOL_GROUPS = len(POOL_WINDOWS)
POOL_WIDTH = D_MODEL // 2
POOL_GROUP_WIDTH = POOL_WIDTH // N_POOL_GROUPS
MIX_WIDTH = ATTN_WIDTH + POOL_WIDTH
IN_WIDTH = ATTN_WIDTH + 2 * KV_WIDTH + POOL_WIDTH
D_FF = 2816
EPS = 1e-6

kernel_name = 'hybrid_window_gqa_multiscale_pool_macaron'


def rms_norm(x, g):
    xf = x.astype(jnp.float32)
    y = xf * lax.rsqrt(jnp.mean(xf * xf, axis=-1, keepdims=True) + EPS)
    return (y * g.astype(jnp.float32)).astype(x.dtype)


def swiglu(x, w_gate, w_up, w_down):
    return (jax.nn.silu(x @ w_gate) * (x @ w_up)) @ w_down


def rope_tables(positions):
    inv_freq = ROPE_THETA ** (-jnp.arange(0, ROTARY_DIM, 2, dtype=jnp.float32) / ROTARY_DIM)
    ang = positions.astype(jnp.float32)[:, None] * inv_freq[None, :]
    emb = jnp.concatenate([ang, ang], axis=-1)
    return jnp.cos(emb)[None, :, None, :], jnp.sin(emb)[None, :, None, :]


def apply_partial_rope(t, cos, sin):
    tf = t.astype(jnp.float32)
    rot, rest = tf[..., :ROTARY_DIM], tf[..., ROTARY_DIM:]
    half = ROTARY_DIM // 2
    rot_half = jnp.concatenate([-rot[..., half:], rot[..., :half]], axis=-1)
    rot = rot * cos + rot_half * sin
    return jnp.concatenate([rot, rest], axis=-1).astype(t.dtype)


def window_attention(q, k, v, sink):
    B, S, H, D = q.shape
    nb = S // BLOCK
    G = H // N_KV_HEADS
    qb = q.reshape(B, nb, BLOCK, N_KV_HEADS, G, D)

    def bands(t):
        tp = jnp.pad(t, ((0, 0), (BLOCK, BLOCK), (0, 0), (0, 0)))
        tb = tp.reshape(B, nb + 2, BLOCK, N_KV_HEADS, D)
        return jnp.concatenate([tb[:, :-2], tb[:, 1:-1], tb[:, 2:]], axis=2)

    kb, vb = bands(k), bands(v)
    scores = jnp.einsum('bnqkgd,bnskd->bnkgqs', qb, kb,
                        preferred_element_type=jnp.float32) * (D ** -0.5)
    qpos = jnp.arange(nb)[:, None] * BLOCK + jnp.arange(BLOCK)[None, :]
    kpos = jnp.arange(nb)[:, None] * BLOCK + jnp.arange(3 * BLOCK)[None, :] - BLOCK
    dist = qpos[:, :, None] - kpos[:, None, :]
    valid = (jnp.abs(dist) <= WINDOW) & (kpos[:, None, :] >= 0) & (kpos[:, None, :] < S)
    scores = jnp.where(valid[None, :, None, None], scores, -1e30)
    sink_l = sink.astype(jnp.float32).reshape(N_KV_HEADS, G)[None, None, :, :, None, None]
    m = jnp.maximum(jnp.max(scores, axis=-1, keepdims=True), sink_l)
    p = jnp.exp(scores - m)
    denom = jnp.sum(p, axis=-1, keepdims=True) + jnp.exp(sink_l - m)
    p = (p / denom).astype(v.dtype)
    out = jnp.einsum('bnkgqs,bnskd->bnqkgd', p, vb)
    return out.reshape(B, S, H * D)


def multiscale_pool(u, w_pool, pool_scale):
    B, S, _ = u.shape
    uf = u.astype(jnp.float32)
    csum = jnp.concatenate([jnp.zeros((B, 1, POOL_WIDTH), jnp.float32),
                            jnp.cumsum(uf, axis=1)], axis=1)
    t = jnp.arange(S)
    means = []
    for gi, w in enumerate(POOL_WINDOWS):
        cg = csum[..., gi * POOL_GROUP_WIDTH:(gi + 1) * POOL_GROUP_WIDTH]
        half = w // 2

        def win_mean(lo, hi):
            a = jnp.clip(lo, 0, S)
            b = jnp.clip(hi + 1, 0, S)
            s = jnp.take(cg, b, axis=1) - jnp.take(cg, a, axis=1)
            return s / (b - a).astype(jnp.float32)[None, :, None]

        means.append(0.5 * (win_mean(t - half, t + half - 1) + win_mean(t - half + 1, t + half)))
    mean = jnp.concatenate(means, axis=-1)
    d = (mean - uf).reshape(B, S, N_POOL_GROUPS, POOL_GROUP_WIDTH)
    y = jnp.einsum('bsgc,gcd->bsgd', d, w_pool.astype(jnp.float32)).reshape(B, S, POOL_WIDTH)
    return (y * pool_scale.astype(jnp.float32)).astype(u.dtype)


def _fwd_setup_inputs(seed: int = 0) -> dict:
    key = jax.random.key(seed)
    ks = jax.random.split(key, 18)
    f32 = jnp.float32
    L = DEPTH

    def nrm(k, shape, scale):
        return jax.random.normal(k, shape, f32) * scale

    def gain(k, shape, s=0.05):
        return 1.0 + s * jax.random.normal(k, shape, f32)

    return {
        'x': nrm(ks[0], (BATCH, SEQ, D_MODEL), 1.0),
        'ffn1_norm': gain(ks[1], (L, D_MODEL)),
        'ffn1_w_gate': nrm(ks[2], (L, D_MODEL, D_FF), D_MODEL ** -0.5),
        'ffn1_w_up': nrm(ks[3], (L, D_MODEL, D_FF), D_MODEL ** -0.5),
        'ffn1_w_down': nrm(ks[4], (L, D_FF, D_MODEL), D_FF ** -0.5),
        'mix_norm': gain(ks[5], (L, D_MODEL)),
        'w_in': nrm(ks[6], (L, D_MODEL, IN_WIDTH), D_MODEL ** -0.5),
        'sink_logits': nrm(ks[7], (L, N_ATTN_HEADS), 0.5),
        'pool_w': nrm(ks[8], (L, N_POOL_GROUPS, POOL_GROUP_WIDTH, POOL_GROUP_WIDTH), POOL_GROUP_WIDTH ** -0.5),
        'pool_scale': gain(ks[9], (L, POOL_WIDTH), 0.1),
        'w_out': nrm(ks[10], (L, MIX_WIDTH, D_MODEL), MIX_WIDTH ** -0.5),
        'ffn2_norm': gain(ks[11], (L, D_MODEL)),
        'ffn2_w_gate': nrm(ks[12], (L, D_MODEL, D_FF), D_MODEL ** -0.5),
        'ffn2_w_up': nrm(ks[13], (L, D_MODEL, D_FF), D_MODEL ** -0.5),
        'ffn2_w_down': nrm(ks[14], (L, D_FF, D_MODEL), D_FF ** -0.5),
        'final_norm': gain(ks[15], (D_MODEL,)),
    }


def _fwd_reference(x, ffn1_norm, ffn1_w_gate, ffn1_w_up, ffn1_w_down, mix_norm, w_in,
              sink_logits, pool_w, pool_scale, w_out, ffn2_norm, ffn2_w_gate,
              ffn2_w_up, ffn2_w_down, final_norm):
    B, S, _ = x.shape
    positions = jnp.arange(S, dtype=jnp.int32)
    cos, sin = rope_tables(positions)
    h = x
    for l in range(DEPTH):
        h = h + 0.5 * swiglu(rms_norm(h, ffn1_norm[l]), ffn1_w_gate[l], ffn1_w_up[l], ffn1_w_down[l])
        u = rms_norm(h, mix_norm[l]) @ w_in[l]
        q, k, v, pc = jnp.split(u, [ATTN_WIDTH, ATTN_WIDTH + KV_WIDTH, ATTN_WIDTH + 2 * KV_WIDTH], axis=-1)
        q = apply_partial_rope(q.reshape(B, S, N_ATTN_HEADS, HEAD_DIM), cos, sin)
        k = apply_partial_rope(k.reshape(B, S, N_KV_HEADS, HEAD_DIM), cos, sin)
        v = v.reshape(B, S, N_KV_HEADS, HEAD_DIM)
        a = window_attention(q, k, v, sink_logits[l])
        p = multiscale_pool(pc, pool_w[l], pool_scale[l])
        h = h + jnp.concatenate([a, p], axis=-1) @ w_out[l]
        h = h + 0.5 * swiglu(rms_norm(h, ffn2_norm[l]), ffn2_w_gate[l], ffn2_w_up[l], ffn2_w_down[l])
    return rms_norm(h, final_norm)


import jax as _jax
import jax.numpy as _jnp

TWIN_FORMAT = 'train_step'
FWD_PARAMS = ['x', 'ffn1_norm', 'ffn1_w_gate', 'ffn1_w_up', 'ffn1_w_down', 'mix_norm', 'w_in', 'sink_logits', 'pool_w', 'pool_scale', 'w_out', 'ffn2_norm', 'ffn2_w_gate', 'ffn2_w_up', 'ffn2_w_down', 'final_norm']
TWIN_WEIGHTS = ['ffn1_norm', 'ffn1_w_gate', 'ffn1_w_up', 'ffn1_w_down', 'mix_norm', 'w_in', 'sink_logits', 'pool_w', 'pool_scale', 'w_out', 'ffn2_norm', 'ffn2_w_gate', 'ffn2_w_up', 'ffn2_w_down', 'final_norm']
TWIN_DIFF_INPUT = 'x'
TWIN_INPUTS = ['x', 'ffn1_norm', 'ffn1_w_gate', 'ffn1_w_up', 'ffn1_w_down', 'mix_norm', 'w_in', 'sink_logits', 'pool_w', 'pool_scale', 'w_out', 'ffn2_norm', 'ffn2_w_gate', 'ffn2_w_up', 'ffn2_w_down', 'final_norm', 'loss_target', 'm_ffn1_norm', 'm_ffn1_w_gate', 'm_ffn1_w_up', 'm_ffn1_w_down', 'm_mix_norm', 'm_w_in', 'm_sink_logits', 'm_pool_w', 'm_pool_scale', 'm_w_out', 'm_ffn2_norm', 'm_ffn2_w_gate', 'm_ffn2_w_up', 'm_ffn2_w_down', 'm_final_norm', 'v_ffn1_norm', 'v_ffn1_w_gate', 'v_ffn1_w_up', 'v_ffn1_w_down', 'v_mix_norm', 'v_w_in', 'v_sink_logits', 'v_pool_w', 'v_pool_scale', 'v_w_out', 'v_ffn2_norm', 'v_ffn2_w_gate', 'v_ffn2_w_up', 'v_ffn2_w_down', 'v_final_norm']
TWIN_OUTPUTS = ['loss', 'grad_x', 'grad_ffn1_norm', 'grad_ffn1_w_gate', 'grad_ffn1_w_up', 'grad_ffn1_w_down', 'grad_mix_norm', 'grad_w_in', 'grad_sink_logits', 'grad_pool_w', 'grad_pool_scale', 'grad_w_out', 'grad_ffn2_norm', 'grad_ffn2_w_gate', 'grad_ffn2_w_up', 'grad_ffn2_w_down', 'grad_final_norm', 'delta_ffn1_norm', 'delta_ffn1_w_gate', 'delta_ffn1_w_up', 'delta_ffn1_w_down', 'delta_mix_norm', 'delta_w_in', 'delta_sink_logits', 'delta_pool_w', 'delta_pool_scale', 'delta_w_out', 'delta_ffn2_norm', 'delta_ffn2_w_gate', 'delta_ffn2_w_up', 'delta_ffn2_w_down', 'delta_final_norm', 'new_m_ffn1_norm', 'new_m_ffn1_w_gate', 'new_m_ffn1_w_up', 'new_m_ffn1_w_down', 'new_m_mix_norm', 'new_m_w_in', 'new_m_sink_logits', 'new_m_pool_w', 'new_m_pool_scale', 'new_m_w_out', 'new_m_ffn2_norm', 'new_m_ffn2_w_gate', 'new_m_ffn2_w_up', 'new_m_ffn2_w_down', 'new_m_final_norm', 'new_v_ffn1_norm', 'new_v_ffn1_w_gate', 'new_v_ffn1_w_up', 'new_v_ffn1_w_down', 'new_v_mix_norm', 'new_v_w_in', 'new_v_sink_logits', 'new_v_pool_w', 'new_v_pool_scale', 'new_v_w_out', 'new_v_ffn2_norm', 'new_v_ffn2_w_gate', 'new_v_ffn2_w_up', 'new_v_ffn2_w_down', 'new_v_final_norm']
TWIN_LEAF_KINDS = {'loss': 'loss', 'grad_x': 'grad_x', 'grad_ffn1_norm': 'grad_w', 'grad_ffn1_w_gate': 'grad_w', 'grad_ffn1_w_up': 'grad_w', 'grad_ffn1_w_down': 'grad_w', 'grad_mix_norm': 'grad_w', 'grad_w_in': 'grad_w', 'grad_sink_logits': 'grad_w', 'grad_pool_w': 'grad_w', 'grad_pool_scale': 'grad_w', 'grad_w_out': 'grad_w', 'grad_ffn2_norm': 'grad_w', 'grad_ffn2_w_gate': 'grad_w', 'grad_ffn2_w_up': 'grad_w', 'grad_ffn2_w_down': 'grad_w', 'grad_final_norm': 'grad_w', 'delta_ffn1_norm': 'delta_w', 'delta_ffn1_w_gate': 'delta_w', 'delta_ffn1_w_up': 'delta_w', 'delta_ffn1_w_down': 'delta_w', 'delta_mix_norm': 'delta_w', 'delta_w_in': 'delta_w', 'delta_sink_logits': 'delta_w', 'delta_pool_w': 'delta_w', 'delta_pool_scale': 'delta_w', 'delta_w_out': 'delta_w', 'delta_ffn2_norm': 'delta_w', 'delta_ffn2_w_gate': 'delta_w', 'delta_ffn2_w_up': 'delta_w', 'delta_ffn2_w_down': 'delta_w', 'delta_final_norm': 'delta_w', 'new_m_ffn1_norm': 'new_m', 'new_m_ffn1_w_gate': 'new_m', 'new_m_ffn1_w_up': 'new_m', 'new_m_ffn1_w_down': 'new_m', 'new_m_mix_norm': 'new_m', 'new_m_w_in': 'new_m', 'new_m_sink_logits': 'new_m', 'new_m_pool_w': 'new_m', 'new_m_pool_scale': 'new_m', 'new_m_w_out': 'new_m', 'new_m_ffn2_norm': 'new_m', 'new_m_ffn2_w_gate': 'new_m', 'new_m_ffn2_w_up': 'new_m', 'new_m_ffn2_w_down': 'new_m', 'new_m_final_norm': 'new_m', 'new_v_ffn1_norm': 'new_v', 'new_v_ffn1_w_gate': 'new_v', 'new_v_ffn1_w_up': 'new_v', 'new_v_ffn1_w_down': 'new_v', 'new_v_mix_norm': 'new_v', 'new_v_w_in': 'new_v', 'new_v_sink_logits': 'new_v', 'new_v_pool_w': 'new_v', 'new_v_pool_scale': 'new_v', 'new_v_w_out': 'new_v', 'new_v_ffn2_norm': 'new_v', 'new_v_ffn2_w_gate': 'new_v', 'new_v_ffn2_w_up': 'new_v', 'new_v_ffn2_w_down': 'new_v', 'new_v_final_norm': 'new_v'}


def _forward(args):
    return _fwd_reference(*[args[k] for k in FWD_PARAMS])


def _output_shape():
    out = _jax.eval_shape(lambda: _forward(_fwd_setup_inputs(0)))
    return out.shape, out.dtype

N_MICROBATCH = 1
ADAM_LR = 0.001
ADAM_B1 = 0.9
ADAM_B2 = 0.999
ADAM_EPS = 1e-08
ADAM_WD = 0.01
ADAM_STEP = 10
PER_EXAMPLE_BATCH_AXIS = {'x': 0, 'loss_target': 0}
SHARED_INPUTS = []
_WEIGHT_DTYPES = {'ffn1_norm': _jnp.float32, 'ffn1_w_gate': _jnp.float32, 'ffn1_w_up': _jnp.float32, 'ffn1_w_down': _jnp.float32, 'mix_norm': _jnp.float32, 'w_in': _jnp.float32, 'sink_logits': _jnp.float32, 'pool_w': _jnp.float32, 'pool_scale': _jnp.float32, 'w_out': _jnp.float32, 'ffn2_norm': _jnp.float32, 'ffn2_w_gate': _jnp.float32, 'ffn2_w_up': _jnp.float32, 'ffn2_w_down': _jnp.float32, 'final_norm': _jnp.float32}
MOMENT_SCALE = {'ffn1_norm': 7.791649e-02, 'ffn1_w_gate': 3.289584e-02, 'ffn1_w_up': 3.179732e-02, 'ffn1_w_down': 5.275280e-02, 'mix_norm': 1.028763e-01, 'w_in': 8.581091e-02, 'sink_logits': 3.249127e-04, 'pool_w': 1.392236e-01, 'pool_scale': 1.547805e-01, 'w_out': 9.823800e-02, 'ffn2_norm': 5.962857e-02, 'ffn2_w_gate': 2.668429e-02, 'ffn2_w_up': 2.602400e-02, 'ffn2_w_down': 4.326444e-02, 'final_norm': 3.199146e+01}


def _to_microbatches(a, axis):
    t = _jnp.moveaxis(a, axis, 0)
    t = t.reshape((N_MICROBATCH, t.shape[0] // N_MICROBATCH) + t.shape[1:])
    return _jnp.moveaxis(t, 1, axis + 1)


def setup_inputs(seed: int = 0) -> dict:
    inp = _fwd_setup_inputs(seed)
    key = _jax.random.fold_in(_jax.random.key(seed), 7919)
    shape, _ = _output_shape()
    out = dict(inp)
    out["loss_target"] = _jax.random.normal(_jax.random.fold_in(key, 0), shape, _jnp.float32)
    for i, name in enumerate(TWIN_WEIGHTS):
        w = inp[name].astype(_jnp.float32)
        if MOMENT_SCALE is None:
            s = _jnp.sqrt(_jnp.mean(_jnp.square(w)) + 1e-30)
        else:
            s = MOMENT_SCALE[name]
        km, kv = _jax.random.split(_jax.random.fold_in(key, i + 1))
        out[name] = w
        out["m_" + name] = s * _jax.random.normal(km, w.shape, _jnp.float32)
        out["v_" + name] = (s * s) * _jax.random.uniform(kv, w.shape, _jnp.float32, 0.5, 1.5)
    if N_MICROBATCH > 1:
        for name, axis in PER_EXAMPLE_BATCH_AXIS.items():
            out[name] = _to_microbatches(out[name], axis)
    return {'x': out['x'], 'ffn1_norm': out['ffn1_norm'], 'ffn1_w_gate': out['ffn1_w_gate'], 'ffn1_w_up': out['ffn1_w_up'], 'ffn1_w_down': out['ffn1_w_down'], 'mix_norm': out['mix_norm'], 'w_in': out['w_in'], 'sink_logits': out['sink_logits'], 'pool_w': out['pool_w'], 'pool_scale': out['pool_scale'], 'w_out': out['w_out'], 'ffn2_norm': out['ffn2_norm'], 'ffn2_w_gate': out['ffn2_w_gate'], 'ffn2_w_up': out['ffn2_w_up'], 'ffn2_w_down': out['ffn2_w_down'], 'final_norm': out['final_norm'], 'loss_target': out['loss_target'], 'm_ffn1_norm': out['m_ffn1_norm'], 'm_ffn1_w_gate': out['m_ffn1_w_gate'], 'm_ffn1_w_up': out['m_ffn1_w_up'], 'm_ffn1_w_down': out['m_ffn1_w_down'], 'm_mix_norm': out['m_mix_norm'], 'm_w_in': out['m_w_in'], 'm_sink_logits': out['m_sink_logits'], 'm_pool_w': out['m_pool_w'], 'm_pool_scale': out['m_pool_scale'], 'm_w_out': out['m_w_out'], 'm_ffn2_norm': out['m_ffn2_norm'], 'm_ffn2_w_gate': out['m_ffn2_w_gate'], 'm_ffn2_w_up': out['m_ffn2_w_up'], 'm_ffn2_w_down': out['m_ffn2_w_down'], 'm_final_norm': out['m_final_norm'], 'v_ffn1_norm': out['v_ffn1_norm'], 'v_ffn1_w_gate': out['v_ffn1_w_gate'], 'v_ffn1_w_up': out['v_ffn1_w_up'], 'v_ffn1_w_down': out['v_ffn1_w_down'], 'v_mix_norm': out['v_mix_norm'], 'v_w_in': out['v_w_in'], 'v_sink_logits': out['v_sink_logits'], 'v_pool_w': out['v_pool_w'], 'v_pool_scale': out['v_pool_scale'], 'v_w_out': out['v_w_out'], 'v_ffn2_norm': out['v_ffn2_norm'], 'v_ffn2_w_gate': out['v_ffn2_w_gate'], 'v_ffn2_w_up': out['v_ffn2_w_up'], 'v_ffn2_w_down': out['v_ffn2_w_down'], 'v_final_norm': out['v_final_norm']}


def _loss(weights, diff, rest, loss_target):
    with _jax.named_scope("forward"):
        args = {**rest, TWIN_DIFF_INPUT: diff, **{k: w.astype(_WEIGHT_DTYPES[k]) for k, w in weights.items()}}
        y = _forward(args)
    with _jax.named_scope("loss_head"):
        err = _jnp.square(y.astype(_jnp.float32) - loss_target)
        return 0.5 * _jnp.sum(_jnp.mean(err, axis=-1)) if err.ndim else 0.5 * err


def _adamw(w, g, m, v):
    m = ADAM_B1 * m + (1.0 - ADAM_B1) * g
    v = ADAM_B2 * v + (1.0 - ADAM_B2) * _jnp.square(g)
    m_hat = m / (1.0 - ADAM_B1 ** ADAM_STEP)
    v_hat = v / (1.0 - ADAM_B2 ** ADAM_STEP)
    delta = -ADAM_LR * (m_hat / (_jnp.sqrt(v_hat) + ADAM_EPS) + ADAM_WD * w)
    return delta, m, v


def reference(x, ffn1_norm, ffn1_w_gate, ffn1_w_up, ffn1_w_down, mix_norm, w_in, sink_logits, pool_w, pool_scale, w_out, ffn2_norm, ffn2_w_gate, ffn2_w_up, ffn2_w_down, final_norm, loss_target, m_ffn1_norm, m_ffn1_w_gate, m_ffn1_w_up, m_ffn1_w_down, m_mix_norm, m_w_in, m_sink_logits, m_pool_w, m_pool_scale, m_w_out, m_ffn2_norm, m_ffn2_w_gate, m_ffn2_w_up, m_ffn2_w_down, m_final_norm, v_ffn1_norm, v_ffn1_w_gate, v_ffn1_w_up, v_ffn1_w_down, v_mix_norm, v_w_in, v_sink_logits, v_pool_w, v_pool_scale, v_w_out, v_ffn2_norm, v_ffn2_w_gate, v_ffn2_w_up, v_ffn2_w_down, v_final_norm):
    given = dict(x=x, ffn1_norm=ffn1_norm, ffn1_w_gate=ffn1_w_gate, ffn1_w_up=ffn1_w_up, ffn1_w_down=ffn1_w_down, mix_norm=mix_norm, w_in=w_in, sink_logits=sink_logits, pool_w=pool_w, pool_scale=pool_scale, w_out=w_out, ffn2_norm=ffn2_norm, ffn2_w_gate=ffn2_w_gate, ffn2_w_up=ffn2_w_up, ffn2_w_down=ffn2_w_down, final_norm=final_norm, loss_target=loss_target, m_ffn1_norm=m_ffn1_norm, m_ffn1_w_gate=m_ffn1_w_gate, m_ffn1_w_up=m_ffn1_w_up, m_ffn1_w_down=m_ffn1_w_down, m_mix_norm=m_mix_norm, m_w_in=m_w_in, m_sink_logits=m_sink_logits, m_pool_w=m_pool_w, m_pool_scale=m_pool_scale, m_w_out=m_w_out, m_ffn2_norm=m_ffn2_norm, m_ffn2_w_gate=m_ffn2_w_gate, m_ffn2_w_up=m_ffn2_w_up, m_ffn2_w_down=m_ffn2_w_down, m_final_norm=m_final_norm, v_ffn1_norm=v_ffn1_norm, v_ffn1_w_gate=v_ffn1_w_gate, v_ffn1_w_up=v_ffn1_w_up, v_ffn1_w_down=v_ffn1_w_down, v_mix_norm=v_mix_norm, v_w_in=v_w_in, v_sink_logits=v_sink_logits, v_pool_w=v_pool_w, v_pool_scale=v_pool_scale, v_w_out=v_w_out, v_ffn2_norm=v_ffn2_norm, v_ffn2_w_gate=v_ffn2_w_gate, v_ffn2_w_up=v_ffn2_w_up, v_ffn2_w_down=v_ffn2_w_down, v_final_norm=v_final_norm)
    weights = {n: given[n] for n in TWIN_WEIGHTS}
    shared = {n: given[n] for n in SHARED_INPUTS}
    per_example = {n: given[n] for n in ['x']}
    grad_fn = _jax.value_and_grad(_loss, argnums=(0, 1))

    def one_microbatch(ex, loss_target):
        ex = dict(ex)
        diff = ex.pop(TWIN_DIFF_INPUT)
        return grad_fn(weights, diff, {**shared, **ex}, loss_target)

    if N_MICROBATCH == 1:
        loss, (grad_w, grad_x) = one_microbatch(per_example, given["loss_target"])
    else:
        def body(carry, xs):
            loss_sum, grad_sum = carry
            l_k, (gw_k, gx_k) = one_microbatch(xs[0], xs[1])
            with _jax.named_scope("update"):
                return (loss_sum + l_k, _jax.tree.map(_jnp.add, grad_sum, gw_k)), gx_k

        init = (_jnp.zeros((), _jnp.float32), _jax.tree.map(_jnp.zeros_like, weights))
        (loss, grad_w), grad_x = _jax.lax.scan(body, init, (per_example, given["loss_target"]))
    with _jax.named_scope("update"):
        delta_w, new_m, new_v = {}, {}, {}
        for n in TWIN_WEIGHTS:
            delta_w[n], new_m[n], new_v[n] = _adamw(weights[n], grad_w[n], given["m_" + n], given["v_" + n])
    return (loss, grad_x, *[grad_w[n] for n in TWIN_WEIGHTS], *[delta_w[n] for n in TWIN_WEIGHTS],
            *[new_m[n] for n in TWIN_WEIGHTS], *[new_v[n] for n in TWIN_WEIGHTS])
```

```python
import functools
import math

import jax
import jax.numpy as jnp
from jax import lax
from jax.experimental import pallas as pl
from jax.experimental.pallas import tpu as pltpu

F32, BF16 = jnp.float32, jnp.bfloat16
MESH = pl.DeviceIdType.MESH

D_MODEL = 1024
D_FF = 2816
N_CHIPS = 4
FF_CHUNK = D_FF // N_CHIPS
HEAD_DIM = 64
N_HEADS = 8
N_KV = 2
Q_PER_KV = N_HEADS // N_KV
KV_WIDTH = N_KV * HEAD_DIM
ATTN_WIDTH = N_HEADS * HEAD_DIM
POOL_WINDOWS = (2, 4, 8, 16)
N_POOL = len(POOL_WINDOWS)
POOL_GROUP = 128
POOL_WIDTH = N_POOL * POOL_GROUP
IN_WIDTH = ATTN_WIDTH + 2 * KV_WIDTH + POOL_WIDTH
WINDOW = 128
BLOCK = 128
BAND = 3 * BLOCK
ROPE_THETA = 500000.0
ROTARY_DIM = HEAD_DIM // 4
EPS = 1e-6
LANES = 128
Q_PAD = N_HEADS * LANES
U_PAD = Q_PAD + 2 * KV_WIDTH + POOL_WIDTH
SCALE = HEAD_DIM ** -0.5
NEG = -1e30

ADAM_LR, ADAM_B1, ADAM_B2, ADAM_EPS, ADAM_WD, ADAM_STEP = 0.001, 0.9, 0.999, 1e-08, 0.01, 10

V7X_VMEM_BYTES = 64 * 1024 * 1024
TOK_TILE = 512
BWD_TILE = 256
SMALL_ROWS = 552


def _params(sem, vmem_mb):
    return pltpu.CompilerParams(dimension_semantics=sem, vmem_limit_bytes=vmem_mb * 1024 * 1024)


def _dot(a, b):
    return lax.dot_general(a, b, (((1,), (0,)), ((), ())), preferred_element_type=F32)


def _dot_nt(a, b):
    return lax.dot_general(a, b, (((1,), (1,)), ((), ())), preferred_element_type=F32)


def _dot_tn(a, b):
    return lax.dot_general(a, b, (((0,), (0,)), ((), ())), preferred_element_type=F32)


def _rms_stats(h):
    r = lax.rsqrt(jnp.mean(h * h, axis=-1, keepdims=True) + EPS)
    return r, h * r


def _rms_bwd(dn, g, r, xh):
    gd = dn * g
    dh = r * (gd - xh * jnp.mean(gd * xh, axis=-1, keepdims=True))
    return dh, jnp.sum(dn * xh, axis=0, keepdims=True)


def _rope(x, c, s1, s2):
    return x * c + pltpu.roll(x, LANES - ROTARY_DIM // 2, 1) * s1 + pltpu.roll(x, ROTARY_DIM // 2, 1) * s2


def _rope_bwd(d, c, s1, s2):
    return d * c + pltpu.roll(d * s1, ROTARY_DIM // 2, 1) + pltpu.roll(d * s2, LANES - ROTARY_DIM // 2, 1)


def _full(shape):
    nd = len(shape)
    return pl.BlockSpec(shape, lambda *_: (0,) * nd)


def _rows(tile, cols):
    return pl.BlockSpec((tile, cols), lambda i, *_: (i, 0))


def _ffn_fwd(h, gain, wg, wu, wd, name):
    S = h.shape[0]
    tile = min(TOK_TILE, S)
    nt = S // tile

    def body(h_ref, g_ref, wg_ref, wu_ref, wd_ref, ho_ref, n_ref, gate_ref, up_ref, acc):
        j = pl.program_id(1)

        @pl.when(j == 0)
        def _():
            _, xh = _rms_stats(h_ref[...])
            n_ref[...] = (xh * g_ref[...]).astype(BF16)
            acc[...] = jnp.zeros_like(acc)

        n = n_ref[...]
        gate = _dot(n, wg_ref[0])
        up = _dot(n, wu_ref[0])
        gate_ref[0] = gate.astype(BF16)
        up_ref[0] = up.astype(BF16)
        act = (gate * jax.nn.sigmoid(gate) * up).astype(BF16)
        acc[...] += _dot(act, wd_ref[0])

        @pl.when(j == N_CHIPS - 1)
        def _():
            ho_ref[...] = h_ref[...] + 0.5 * acc[...]

    tok = pl.BlockSpec((tile, D_MODEL), lambda i, j: (i, 0))
    hid = pl.BlockSpec((1, tile, FF_CHUNK), lambda i, j: (j, i, 0))
    return pl.pallas_call(
        body, name=name, grid=(nt, N_CHIPS),
        in_specs=[tok, pl.BlockSpec((1, D_MODEL), lambda i, j: (0, 0)),
                  pl.BlockSpec((1, D_MODEL, FF_CHUNK), lambda i, j: (j, 0, 0)),
                  pl.BlockSpec((1, D_MODEL, FF_CHUNK), lambda i, j: (j, 0, 0)),
                  pl.BlockSpec((1, FF_CHUNK, D_MODEL), lambda i, j: (j, 0, 0))],
        out_specs=[tok, tok, hid, hid],
        out_shape=[jax.ShapeDtypeStruct((S, D_MODEL), F32), jax.ShapeDtypeStruct((S, D_MODEL), BF16),
                   jax.ShapeDtypeStruct((N_CHIPS, S, FF_CHUNK), BF16), jax.ShapeDtypeStruct((N_CHIPS, S, FF_CHUNK), BF16)],
        scratch_shapes=[pltpu.VMEM((tile, D_MODEL), F32)],
        compiler_params=_params(("parallel", "arbitrary"), 40),
    )(h, gain, wg, wu, wd)


def _ffn_bwd(d_out, n, gate, up, wg, wu, wd, name):
    S = n.shape[0]
    tile = min(BWD_TILE, S)
    nt = S // tile
    hr_in, hr_ff = D_MODEL // 2, FF_CHUNK // 2

    def body(do_ref, n_ref, gate_ref, up_ref, wg_ref, wu_ref, wd_ref, dn_ref, dwg_ref, dwu_ref, dwd_ref,
             dn_scr, acc_g, acc_u, acc_d):
        j, i = pl.program_id(0), pl.program_id(1)

        @pl.when(i == 0)
        def _():
            acc_g[...] = jnp.zeros_like(acc_g)
            acc_u[...] = jnp.zeros_like(acc_u)
            acc_d[...] = jnp.zeros_like(acc_d)

        do = do_ref[...]
        nn = n_ref[...]
        g = gate_ref[0].astype(F32)
        u = up_ref[0].astype(F32)
        d_act = _dot_nt(do, wd_ref[0])
        sig = jax.nn.sigmoid(g)
        silu = g * sig
        d_up = (d_act * silu).astype(BF16)
        d_gate = (d_act * u * (sig * (1.0 + g * (1.0 - sig)))).astype(BF16)
        act = (silu * u).astype(BF16)
        part = _dot_nt(d_gate, wg_ref[0]) + _dot_nt(d_up, wu_ref[0])
        rows = pl.ds(pl.multiple_of(i * tile, tile), tile)

        @pl.when(j == 0)
        def _():
            dn_scr[rows, :] = part

        @pl.when(j > 0)
        def _():
            dn_scr[rows, :] += part

        acc_g[...] += _dot_tn(nn, d_gate)
        acc_u[...] += _dot_tn(nn, d_up)
        acc_d[...] += _dot_tn(act, do)

        @pl.when(i == nt - 1)
        def _():
            for half in range(2):
                dwg_ref[half, 0] = acc_g[half * hr_in:(half + 1) * hr_in, :].astype(BF16)
                dwu_ref[half, 0] = acc_u[half * hr_in:(half + 1) * hr_in, :].astype(BF16)
                dwd_ref[half, 0] = acc_d[half * hr_ff:(half + 1) * hr_ff, :].astype(BF16)

        @pl.when(j == N_CHIPS - 1)
        def _():
            dn_ref[...] = dn_scr[rows, :]

    tok = pl.BlockSpec((tile, D_MODEL), lambda j, i: (i, 0))
    hid = pl.BlockSpec((1, tile, FF_CHUNK), lambda j, i: (j, i, 0))
    w_in = pl.BlockSpec((1, D_MODEL, FF_CHUNK), lambda j, i: (j, 0, 0))
    w_dn = pl.BlockSpec((1, FF_CHUNK, D_MODEL), lambda j, i: (j, 0, 0))
    g_in = pl.BlockSpec((2, 1, hr_in, FF_CHUNK), lambda j, i: (0, j, 0, 0))
    g_dn = pl.BlockSpec((2, 1, hr_ff, D_MODEL), lambda j, i: (0, j, 0, 0))
    dn_spec = pl.BlockSpec((tile, D_MODEL), lambda j, i: (jnp.where(j == N_CHIPS - 1, i, 0), 0))
    return pl.pallas_call(
        body, name=name, grid=(N_CHIPS, nt),
        in_specs=[tok, tok, hid, hid, w_in, w_in, w_dn],
        out_specs=[dn_spec, g_in, g_in, g_dn],
        out_shape=[jax.ShapeDtypeStruct((S, D_MODEL), F32),
                   jax.ShapeDtypeStruct((2, N_CHIPS, hr_in, FF_CHUNK), BF16),
                   jax.ShapeDtypeStruct((2, N_CHIPS, hr_in, FF_CHUNK), BF16),
                   jax.ShapeDtypeStruct((2, N_CHIPS, hr_ff, D_MODEL), BF16)],
        scratch_shapes=[pltpu.VMEM((S, D_MODEL), F32), pltpu.VMEM((D_MODEL, FF_CHUNK), F32),
                        pltpu.VMEM((D_MODEL, FF_CHUNK), F32), pltpu.VMEM((FF_CHUNK, D_MODEL), F32)],
        compiler_params=_params(("arbitrary", "arbitrary"), 60),
    )(d_out, n, gate, up, wg, wu, wd)


def _mix_in(h, gain, w_in, rc, rs1, rs2, name):
    S = h.shape[0]
    tile = min(TOK_TILE, S)

    def body(h_ref, g_ref, w_ref, c_ref, s1_ref, s2_ref, n_ref, q_ref, k_ref, v_ref, pc_ref):
        _, xh = _rms_stats(h_ref[...])
        n = (xh * g_ref[...]).astype(BF16)
        n_ref[...] = n
        u = _dot(n, w_ref[...])
        c, s1, s2 = c_ref[...], s1_ref[...], s2_ref[...]
        for hd in range(N_HEADS):
            q_ref[:, hd * LANES:(hd + 1) * LANES] = _rope(u[:, hd * LANES:(hd + 1) * LANES], c, s1, s2).astype(BF16)
        k_ref[...] = _rope(u[:, Q_PAD:Q_PAD + KV_WIDTH], c, s1, s2).astype(BF16)
        v_ref[...] = u[:, Q_PAD + KV_WIDTH:Q_PAD + 2 * KV_WIDTH].astype(BF16)
        pc_ref[...] = u[:, Q_PAD + 2 * KV_WIDTH:]

    return pl.pallas_call(
        body, name=name, grid=(S // tile,),
        in_specs=[_rows(tile, D_MODEL), _full((1, D_MODEL)), _full((D_MODEL, U_PAD)),
                  _rows(tile, LANES), _rows(tile, LANES), _rows(tile, LANES)],
        out_specs=[_rows(tile, D_MODEL), _rows(tile, Q_PAD), _rows(tile, KV_WIDTH), _rows(tile, KV_WIDTH),
                   _rows(tile, POOL_WIDTH)],
        out_shape=[jax.ShapeDtypeStruct((S, D_MODEL), BF16), jax.ShapeDtypeStruct((S, Q_PAD), BF16),
                   jax.ShapeDtypeStruct((S, KV_WIDTH), BF16), jax.ShapeDtypeStruct((S, KV_WIDTH), BF16),
                   jax.ShapeDtypeStruct((S, POOL_WIDTH), F32)],
        compiler_params=_params(("parallel",), 40),
    )(h, gain, w_in, rc, rs1, rs2)


def _band_start(i, S):
    return pl.multiple_of(jnp.clip((i - 1) * BLOCK, 0, S - BAND), BLOCK)


def _probs(qh, kb, off, sink_h):
    s = _dot_nt(qh, kb) * SCALE
    r = lax.broadcasted_iota(jnp.int32, (BLOCK, 1), 0)
    c = lax.broadcasted_iota(jnp.int32, (1, BAND), 1)
    s = jnp.where(jnp.abs(off + r - c) <= WINDOW, s, NEG)
    m = jnp.maximum(jnp.max(s, axis=-1, keepdims=True), sink_h)
    p = jnp.exp(s - m)
    es = jnp.exp(sink_h - m)
    inv = 1.0 / (jnp.sum(p, axis=-1, keepdims=True) + es)
    return p * inv, es * inv


def _pool_matrix(t0, start, S, w):
    r = lax.broadcasted_iota(jnp.int32, (BLOCK, 1), 0) + t0
    c = lax.broadcasted_iota(jnp.int32, (1, BAND), 1) + start
    half = w // 2

    def window(lo, hi):
        a = jnp.maximum(lo, 0)
        b = jnp.minimum(hi + 1, S)
        return jnp.where((c >= a) & (c < b), 1.0 / (b - a).astype(F32), 0.0)

    return (0.5 * (window(r - half, r + half - 1) + window(r - half + 1, r + half))).astype(BF16)


def _mix_core_fwd(q, k, v, pc, sink, pool_w, pool_scale, name):
    S = q.shape[0]

    def body(sink_ref, q_ref, k_ref, v_ref, pc_ref, pw_ref, ps_ref, a_ref, p_ref):
        i = pl.program_id(0)
        start = _band_start(i, S)
        band = pl.ds(start, BAND)
        off = i * BLOCK - start
        kb, vb = k_ref[band, :], v_ref[band, :]
        for hd in range(N_HEADS):
            prob, _ = _probs(q_ref[:, hd * LANES:(hd + 1) * LANES], kb, off, sink_ref[0, hd])
            a_ref[:, hd * LANES:(hd + 1) * LANES] = _dot(prob.astype(BF16), vb).astype(BF16)
        centre = pl.ds(pl.multiple_of(i * BLOCK, BLOCK), BLOCK)
        for g, w in enumerate(POOL_WINDOWS):
            cols = slice(g * POOL_GROUP, (g + 1) * POOL_GROUP)
            mean = _dot(_pool_matrix(i * BLOCK, start, S, w), pc_ref[band, cols].astype(BF16))
            dev = (mean - pc_ref[centre, cols]).astype(BF16)
            p_ref[:, cols] = (_dot(dev, pw_ref[g]) * ps_ref[:, cols]).astype(BF16)

    return pl.pallas_call(
        body, name=name, grid=(S // BLOCK,),
        in_specs=[pl.BlockSpec(memory_space=pltpu.SMEM), _rows(BLOCK, Q_PAD), _full((S, KV_WIDTH)), _full((S, KV_WIDTH)),
                  _full((S, POOL_WIDTH)), _full((N_POOL, POOL_GROUP, POOL_GROUP)), _full((1, POOL_WIDTH))],
        out_specs=[_rows(BLOCK, Q_PAD), _rows(BLOCK, POOL_WIDTH)],
        out_shape=[jax.ShapeDtypeStruct((S, Q_PAD), BF16), jax.ShapeDtypeStruct((S, POOL_WIDTH), BF16)],
        compiler_params=_params(("parallel",), 40),
    )(sink, q, k, v, pc, pool_w, pool_scale)


def _mix_core_bwd(q, k, v, pc, da, dp, sink, pool_w, pool_scale, rc, rs1, rs2, name):
    S = q.shape[0]
    nb = S // BLOCK

    def body(sink_ref, q_ref, k_ref, v_ref, pc_ref, da_ref, dp_ref, pw_ref, ps_ref, c_ref, s1_ref, s2_ref,
             dq_ref, dk_ref, dv_ref, dpc_ref, dsink_ref, dpw_ref, dps_ref):
        i = pl.program_id(0)

        @pl.when(i == 0)
        def _():
            dk_ref[...] = jnp.zeros_like(dk_ref)
            dv_ref[...] = jnp.zeros_like(dv_ref)
            dpc_ref[...] = jnp.zeros_like(dpc_ref)
            dsink_ref[...] = jnp.zeros_like(dsink_ref)
            dpw_ref[...] = jnp.zeros_like(dpw_ref)
            dps_ref[...] = jnp.zeros_like(dps_ref)

        start = _band_start(i, S)
        band = pl.ds(start, BAND)
        off = i * BLOCK - start
        kb, vb = k_ref[band, :], v_ref[band, :]
        c, s1, s2 = c_ref[...], s1_ref[...], s2_ref[...]
        lane = lax.broadcasted_iota(jnp.int32, (1, LANES), 1)
        dk_blk = jnp.zeros((BAND, KV_WIDTH), F32)
        dv_blk = jnp.zeros((BAND, KV_WIDTH), F32)
        dsink = jnp.zeros((1, LANES), F32)
        for hd in range(N_HEADS):
            cols = slice(hd * LANES, (hd + 1) * LANES)
            qh = q_ref[:, cols]
            dah = da_ref[:, cols]
            prob, p_sink = _probs(qh, kb, off, sink_ref[0, hd])
            d_prob = _dot_nt(dah, vb)
            delta = jnp.sum(prob * d_prob, axis=-1, keepdims=True)
            d_s = (prob * (d_prob - delta) * SCALE).astype(BF16)
            dq_ref[:, cols] = _rope_bwd(_dot(d_s, kb), c, s1, s2).astype(BF16)
            dk_blk += _dot_tn(d_s, qh)
            dv_blk += _dot_tn(prob.astype(BF16), dah)
            dsink += jnp.where(lane == hd, -jnp.sum(p_sink * delta, axis=0, keepdims=True), 0.0)
        dk_ref[band, :] += dk_blk
        dv_ref[band, :] += dv_blk
        dsink_ref[...] += dsink

        centre = pl.ds(pl.multiple_of(i * BLOCK, BLOCK), BLOCK)
        for g, w in enumerate(POOL_WINDOWS):
            cols = slice(g * POOL_GROUP, (g + 1) * POOL_GROUP)
            pm = _pool_matrix(i * BLOCK, start, S, w)
            dev = (_dot(pm, pc_ref[band, cols].astype(BF16)) - pc_ref[centre, cols]).astype(BF16)
            dy = dp_ref[:, cols].astype(F32)
            z = _dot(dev, pw_ref[g])
            dps_ref[:, cols] += jnp.sum(dy * z, axis=0, keepdims=True)
            dz = (dy * ps_ref[:, cols]).astype(BF16)
            dpw_ref[g] += _dot_tn(dev, dz)
            d_dev = _dot_nt(dz, pw_ref[g])
            dpc_ref[band, cols] += _dot_tn(pm, d_dev.astype(BF16))
            dpc_ref[centre, cols] -= d_dev

    return pl.pallas_call(
        body, name=name, grid=(nb,),
        in_specs=[pl.BlockSpec(memory_space=pltpu.SMEM), _rows(BLOCK, Q_PAD), _full((S, KV_WIDTH)), _full((S, KV_WIDTH)),
                  _full((S, POOL_WIDTH)), _rows(BLOCK, Q_PAD), _rows(BLOCK, POOL_WIDTH),
                  _full((N_POOL, POOL_GROUP, POOL_GROUP)), _full((1, POOL_WIDTH)),
                  _rows(BLOCK, LANES), _rows(BLOCK, LANES), _rows(BLOCK, LANES)],
        out_specs=[_rows(BLOCK, Q_PAD), _full((S, KV_WIDTH)), _full((S, KV_WIDTH)), _full((S, POOL_WIDTH)),
                   _full((1, LANES)), _full((N_POOL, POOL_GROUP, POOL_GROUP)), _full((1, POOL_WIDTH))],
        out_shape=[jax.ShapeDtypeStruct((S, Q_PAD), BF16), jax.ShapeDtypeStruct((S, KV_WIDTH), F32),
                   jax.ShapeDtypeStruct((S, KV_WIDTH), F32), jax.ShapeDtypeStruct((S, POOL_WIDTH), F32),
                   jax.ShapeDtypeStruct((1, LANES), F32), jax.ShapeDtypeStruct((N_POOL, POOL_GROUP, POOL_GROUP), F32),
                   jax.ShapeDtypeStruct((1, POOL_WIDTH), F32)],
        compiler_params=_params(("arbitrary",), 56),
    )(sink, q, k, v, pc, da, dp, pool_w, pool_scale, rc, rs1, rs2)


def _mix_out(h, a, p, wa, wp, name):
    S = h.shape[0]
    tile = min(TOK_TILE, S)

    def body(h_ref, a_ref, p_ref, wa_ref, wp_ref, o_ref):
        o_ref[...] = h_ref[...] + _dot(a_ref[...], wa_ref[...]) + _dot(p_ref[...], wp_ref[...])

    return pl.pallas_call(
        body, name=name, grid=(S // tile,),
        in_specs=[_rows(tile, D_MODEL), _rows(tile, Q_PAD), _rows(tile, POOL_WIDTH),
                  _full((Q_PAD, D_MODEL)), _full((POOL_WIDTH, D_MODEL))],
        out_specs=_rows(tile, D_MODEL),
        out_shape=jax.ShapeDtypeStruct((S, D_MODEL), F32),
        compiler_params=_params(("parallel",), 40),
    )(h, a, p, wa, wp)


def _loss_head(h, target, gain, name):
    S = h.shape[0]
    tile = min(TOK_TILE, S)

    def body(h_ref, t_ref, g_ref, dh_ref, dhalf_ref, loss_ref, dg_ref):
        @pl.when(pl.program_id(0) == 0)
        def _():
            loss_ref[...] = jnp.zeros_like(loss_ref)
            dg_ref[...] = jnp.zeros_like(dg_ref)

        g = g_ref[...]
        r, xh = _rms_stats(h_ref[...])
        err = xh * g - t_ref[...]
        loss_ref[...] += (0.5 / D_MODEL) * jnp.sum(err * err, axis=0, keepdims=True)
        dh, dg = _rms_bwd(err * (1.0 / D_MODEL), g, r, xh)
        dg_ref[...] += dg
        dh_ref[...] = dh
        dhalf_ref[...] = (0.5 * dh).astype(BF16)

    return pl.pallas_call(
        body, name=name, grid=(S // tile,),
        in_specs=[_rows(tile, D_MODEL), _rows(tile, D_MODEL), _full((1, D_MODEL))],
        out_specs=[_rows(tile, D_MODEL), _rows(tile, D_MODEL), _full((1, D_MODEL)), _full((1, D_MODEL))],
        out_shape=[jax.ShapeDtypeStruct((S, D_MODEL), F32), jax.ShapeDtypeStruct((S, D_MODEL), BF16),
                   jax.ShapeDtypeStruct((1, D_MODEL), F32), jax.ShapeDtypeStruct((1, D_MODEL), F32)],
        compiler_params=_params(("arbitrary",), 40),
    )(h, target, gain)


def _mix_out_bwd(dh_out, dn, h, gain, a, p, wa, wp, name):
    S = h.shape[0]
    tile = min(TOK_TILE, S)

    def body(do_ref, dn_ref, h_ref, g_ref, a_ref, p_ref, wa_ref, wp_ref, dh_ref, da_ref, dp_ref, dwa_ref, dwp_ref, dg_ref):
        @pl.when(pl.program_id(0) == 0)
        def _():
            dwa_ref[...] = jnp.zeros_like(dwa_ref)
            dwp_ref[...] = jnp.zeros_like(dwp_ref)
            dg_ref[...] = jnp.zeros_like(dg_ref)

        r, xh = _rms_stats(h_ref[...])
        dnorm, dg = _rms_bwd(dn_ref[...], g_ref[...], r, xh)
        dh = do_ref[...] + dnorm
        dg_ref[...] += dg
        dh_ref[...] = dh
        dhb = dh.astype(BF16)
        da_ref[...] = _dot_nt(dhb, wa_ref[...]).astype(BF16)
        dp_ref[...] = _dot_nt(dhb, wp_ref[...]).astype(BF16)
        dwa_ref[...] += _dot_tn(a_ref[...], dhb)
        dwp_ref[...] += _dot_tn(p_ref[...], dhb)

    return pl.pallas_call(
        body, name=name, grid=(S // tile,),
        in_specs=[_rows(tile, D_MODEL), _rows(tile, D_MODEL), _rows(tile, D_MODEL), _full((1, D_MODEL)),
                  _rows(tile, Q_PAD), _rows(tile, POOL_WIDTH), _full((Q_PAD, D_MODEL)), _full((POOL_WIDTH, D_MODEL))],
        out_specs=[_rows(tile, D_MODEL), _rows(tile, Q_PAD), _rows(tile, POOL_WIDTH),
                   _full((Q_PAD, D_MODEL)), _full((POOL_WIDTH, D_MODEL)), _full((1, D_MODEL))],
        out_shape=[jax.ShapeDtypeStruct((S, D_MODEL), F32), jax.ShapeDtypeStruct((S, Q_PAD), BF16),
                   jax.ShapeDtypeStruct((S, POOL_WIDTH), BF16), jax.ShapeDtypeStruct((Q_PAD, D_MODEL), F32),
                   jax.ShapeDtypeStruct((POOL_WIDTH, D_MODEL), F32), jax.ShapeDtypeStruct((1, D_MODEL), F32)],
        compiler_params=_params(("arbitrary",), 48),
    )(dh_out, dn, h, gain, a, p, wa, wp)


def _mix_in_bwd(dh_out, h, gain, n, dq, dk, dv, dpc, rc, rs1, rs2, w_in, name):
    S = h.shape[0]
    tile = min(TOK_TILE, S)

    def body(do_ref, h_ref, g_ref, n_ref, dq_ref, dk_ref, dv_ref, dpc_ref, c_ref, s1_ref, s2_ref, w_ref,
             dh_ref, dhalf_ref, dw_ref, dg_ref):
        @pl.when(pl.program_id(0) == 0)
        def _():
            dw_ref[...] = jnp.zeros_like(dw_ref)
            dg_ref[...] = jnp.zeros_like(dg_ref)

        dk = _rope_bwd(dk_ref[...], c_ref[...], s1_ref[...], s2_ref[...]).astype(BF16)
        du = jnp.concatenate([dq_ref[...], dk, dv_ref[...].astype(BF16), dpc_ref[...].astype(BF16)], axis=1)
        dn = _dot_nt(du, w_ref[...])
        dw_ref[...] += _dot_tn(n_ref[...], du)
        r, xh = _rms_stats(h_ref[...])
        dnorm, dg = _rms_bwd(dn, g_ref[...], r, xh)
        dh = do_ref[...] + dnorm
        dg_ref[...] += dg
        dh_ref[...] = dh
        dhalf_ref[...] = (0.5 * dh).astype(BF16)

    return pl.pallas_call(
        body, name=name, grid=(S // tile,),
        in_specs=[_rows(tile, D_MODEL), _rows(tile, D_MODEL), _full((1, D_MODEL)), _rows(tile, D_MODEL),
                  _rows(tile, Q_PAD), _rows(tile, KV_WIDTH), _rows(tile, KV_WIDTH), _rows(tile, POOL_WIDTH),
                  _rows(tile, LANES), _rows(tile, LANES), _rows(tile, LANES), _full((D_MODEL, U_PAD))],
        out_specs=[_rows(tile, D_MODEL), _rows(tile, D_MODEL), _full((D_MODEL, U_PAD)), _full((1, D_MODEL))],
        out_shape=[jax.ShapeDtypeStruct((S, D_MODEL), F32), jax.ShapeDtypeStruct((S, D_MODEL), BF16),
                   jax.ShapeDtypeStruct((D_MODEL, U_PAD), F32), jax.ShapeDtypeStruct((1, D_MODEL), F32)],
        compiler_params=_params(("arbitrary",), 56),
    )(dh_out, h, gain, n, dq, dk, dv, dpc, rc, rs1, rs2, w_in)


def _norm_bwd(dh_out, dn, h, gain, name):
    S = h.shape[0]
    tile = min(TOK_TILE, S)

    def body(do_ref, dn_ref, h_ref, g_ref, dh_ref, dg_ref):
        @pl.when(pl.program_id(0) == 0)
        def _():
            dg_ref[...] = jnp.zeros_like(dg_ref)

        r, xh = _rms_stats(h_ref[...])
        dnorm, dg = _rms_bwd(dn_ref[...], g_ref[...], r, xh)
        dg_ref[...] += dg
        dh_ref[...] = do_ref[...] + dnorm

    return pl.pallas_call(
        body, name=name, grid=(S // tile,),
        in_specs=[_rows(tile, D_MODEL), _rows(tile, D_MODEL), _rows(tile, D_MODEL), _full((1, D_MODEL))],
        out_specs=[_rows(tile, D_MODEL), _full((1, D_MODEL))],
        out_shape=[jax.ShapeDtypeStruct((S, D_MODEL), F32), jax.ShapeDtypeStruct((1, D_MODEL), F32)],
        compiler_params=_params(("arbitrary",), 40),
    )(dh_out, dn, h, gain)


def _rope_tables(S):
    half = ROTARY_DIM // 2
    inv_freq = ROPE_THETA ** (-jnp.arange(0, ROTARY_DIM, 2, dtype=F32) / ROTARY_DIM)
    ang = jnp.arange(S, dtype=F32)[:, None] * inv_freq[None, :]
    cos, sin, zero, one = jnp.cos(ang), jnp.sin(ang), jnp.zeros((S, half), F32), jnp.ones((S, HEAD_DIM - ROTARY_DIM), F32)
    rest = jnp.zeros((S, HEAD_DIM - ROTARY_DIM), F32)
    c = jnp.concatenate([cos, cos, one], axis=1)
    s1 = jnp.concatenate([-sin, zero, rest], axis=1)
    s2 = jnp.concatenate([zero, sin, rest], axis=1)
    return tuple(jnp.concatenate([t, t], axis=1) for t in (c, s1, s2))


def _pad_heads(w, axis):
    w = jnp.moveaxis(w, axis, 0)
    heads = w.reshape((N_HEADS, HEAD_DIM) + w.shape[1:])
    zero = jnp.zeros_like(heads)
    first = (jnp.arange(N_HEADS) < Q_PER_KV).reshape((N_HEADS, 1) + (1,) * (w.ndim - 1))
    lo = jnp.where(first, heads, zero)
    hi = jnp.where(first, zero, heads)
    padded = jnp.concatenate([lo, hi], axis=1).reshape((Q_PAD,) + w.shape[1:])
    return jnp.moveaxis(padded, 0, axis)


def _unpad_heads(w, axis):
    w = jnp.moveaxis(w, axis, 0)
    groups = w.reshape((N_HEADS, 2, HEAD_DIM) + w.shape[1:])
    first = (jnp.arange(N_HEADS) < Q_PER_KV).reshape((N_HEADS, 1) + (1,) * (w.ndim - 1))
    heads = jnp.where(first, groups[:, 0], groups[:, 1]).reshape((ATTN_WIDTH,) + w.shape[1:])
    return jnp.moveaxis(heads, 0, axis)


def _local_step(x, target, w, small):
    S = x.shape[0]
    rc, rs1, rs2 = _rope_tables(S)
    w_in_full = jnp.transpose(w["w_in"], (1, 0, 2)).reshape(D_MODEL, IN_WIDTH)
    w_in_pad = jnp.concatenate([_pad_heads(w_in_full[:, :ATTN_WIDTH], 1), w_in_full[:, ATTN_WIDTH:]], axis=1)
    w_out_full = w["w_out"].reshape(ATTN_WIDTH + POOL_WIDTH, D_MODEL)
    wa = _pad_heads(w_out_full[:ATTN_WIDTH], 0)
    wp = w_out_full[ATTN_WIDTH:]
    pool_w = small["pool_w"].astype(BF16)

    h1, n1, gate1, up1 = _ffn_fwd(x, small["ffn1_norm"], w["ffn1_w_gate"], w["ffn1_w_up"], w["ffn1_w_down"], "ffn1_fwd")
    n2, q, k, v, pc = _mix_in(h1, small["mix_norm"], w_in_pad, rc, rs1, rs2, "mix_in")
    a, p = _mix_core_fwd(q, k, v, pc, small["sink_logits"], pool_w, small["pool_scale"], "mix_core_fwd")
    h2 = _mix_out(h1, a, p, wa, wp, "mix_out")
    h3, n3, gate2, up2 = _ffn_fwd(h2, small["ffn2_norm"], w["ffn2_w_gate"], w["ffn2_w_up"], w["ffn2_w_down"], "ffn2_fwd")
    dh3, dhalf3, loss_lanes, d_final = _loss_head(h3, target, small["final_norm"], "loss_head")

    dn3, dwg2, dwu2, dwd2 = _ffn_bwd(dhalf3, n3, gate2, up2, w["ffn2_w_gate"], w["ffn2_w_up"], w["ffn2_w_down"], "ffn2_bwd")
    dh2, da, dp, dwa, dwp, d_ffn2_norm = _mix_out_bwd(dh3, dn3, h2, small["ffn2_norm"], a, p, wa, wp, "mix_out_bwd")
    dq, dk, dv, dpc, dsink, dpool_w, dpool_scale = _mix_core_bwd(
        q, k, v, pc, da, dp, small["sink_logits"], pool_w, small["pool_scale"], rc, rs1, rs2, "mix_core_bwd")
    dh1, dhalf1, dw_in_pad, d_mix_norm = _mix_in_bwd(dh2, h1, small["mix_norm"], n2, dq, dk, dv, dpc, rc, rs1, rs2,
                                                     w_in_pad, "mix_in_bwd")
    dn1, dwg1, dwu1, dwd1 = _ffn_bwd(dhalf1, n1, gate1, up1, w["ffn1_w_gate"], w["ffn1_w_up"], w["ffn1_w_down"], "ffn1_bwd")
    grad_x, d_ffn1_norm = _norm_bwd(dh1, dn1, x, small["ffn1_norm"], "norm1_bwd")

    dw_in = jnp.concatenate([_unpad_heads(dw_in_pad[:, :Q_PAD], 1), dw_in_pad[:, Q_PAD:]], axis=1)
    cw = IN_WIDTH // N_CHIPS
    dw_in = jnp.transpose(dw_in.reshape(2, D_MODEL // 2, N_CHIPS, cw), (0, 2, 1, 3)).astype(BF16)
    dw_out = jnp.concatenate([_unpad_heads(dwa, 0), dwp], axis=0)
    rw = (ATTN_WIDTH + POOL_WIDTH) // N_CHIPS
    dw_out = jnp.transpose(dw_out.reshape(N_CHIPS, 2, rw // 2, D_MODEL), (1, 0, 2, 3)).astype(BF16)
    big = {"ffn1_w_gate": dwg1, "ffn1_w_up": dwu1, "ffn1_w_down": dwd1, "w_in": dw_in, "w_out": dw_out,
           "ffn2_w_gate": dwg2, "ffn2_w_up": dwu2, "ffn2_w_down": dwd2}
    small_g = {"ffn1_norm": d_ffn1_norm, "mix_norm": d_mix_norm, "ffn2_norm": d_ffn2_norm, "final_norm": d_final,
               "pool_scale": dpool_scale, "sink_logits": dsink[:, :N_HEADS], "pool_w": dpool_w}
    return loss_lanes, grad_x, big, small_g


HBM_SPEC = pl.BlockSpec(memory_space=pltpu.HBM)
BIG = ("ffn1_w_gate", "ffn1_w_up", "ffn1_w_down", "w_in", "w_out", "ffn2_w_gate", "ffn2_w_up", "ffn2_w_down")


def _place():
    x, y, c = lax.axis_index("x"), lax.axis_index("y"), lax.axis_index("c")
    chips = [(1 - x, y), (x, 1 - y), (1 - x, 1 - y)]
    return x, y, c, chips


def _remote(src, dst, send_sem, recv_sem, to):
    return pltpu.make_async_remote_copy(src_ref=src, dst_ref=dst, send_sem=send_sem, recv_sem=recv_sem,
                                        device_id=to, device_id_type=MESH)


def _allgather(shards, name):
    n = len(shards)

    def body(*refs):
        ins, outs = refs[:n], refs[n:2 * n]
        send_sems, recv_sems, local_sems = refs[2 * n:]
        x, y, c, chips = _place()
        me = 2 * x + y
        sibling = (x, y, 1 - c)
        slots = [2 * cx + cy for cx, cy in chips]
        local = [pltpu.make_async_copy(ins[a], outs[a].at[me], local_sems.at[a]) for a in range(n)]
        for cp in local:
            cp.start()
        first = [[_remote(ins[a].at[c], outs[a].at[me, c], send_sems.at[6 * a + k], recv_sems.at[6 * a + k], (*chips[k], c))
                  for k in range(3)] for a in range(n)]
        for a in range(n):
            for cp in first[a]:
                cp.start()
        passed = [[_remote(outs[a].at[slots[k], c], outs[a].at[slots[k], c], send_sems.at[6 * a + 3 + k],
                           recv_sems.at[6 * a + 3 + k], sibling) for k in range(3)] for a in range(n)]
        for a in range(n):
            for k in range(3):
                _remote(ins[a].at[c], outs[a].at[slots[k], c], send_sems.at[6 * a + k], recv_sems.at[6 * a + k],
                        (*chips[k], c)).wait_recv()
                passed[a][k].start()
        for a in range(n):
            for k in range(3):
                _remote(ins[a].at[c], outs[a].at[slots[k], 1 - c], send_sems.at[6 * a + 3 + k],
                        recv_sems.at[6 * a + 3 + k], sibling).wait_recv()
        for a in range(n):
            for cp in first[a] + passed[a]:
                cp.wait_send()
        for cp in local:
            cp.wait()

    return pl.pallas_call(
        body, name=name,
        in_specs=[HBM_SPEC] * n, out_specs=[HBM_SPEC] * n,
        out_shape=[jax.ShapeDtypeStruct((N_CHIPS,) + s.shape, s.dtype) for s in shards],
        scratch_shapes=[pltpu.SemaphoreType.DMA((6 * n,)), pltpu.SemaphoreType.DMA((6 * n,)), pltpu.SemaphoreType.DMA((n,))],
    )(*shards)


def _sibling_exchange(parts, small, name):
    n = len(parts)
    flips = [(fx, fy, fc) for fx in range(2) for fy in range(2) for fc in range(2)][1:]

    def body(*refs):
        ins, small_ref = refs[:n], refs[n]
        outs, gather_ref = refs[n + 1:2 * n + 1], refs[2 * n + 1]
        send_sems, recv_sems, local_sem = refs[2 * n + 2:]
        x, y, c, _ = _place()
        me = 4 * x + 2 * y + c
        sibling = (x, y, 1 - c)
        own = pltpu.make_async_copy(small_ref, gather_ref.at[me], local_sem)
        own.start()
        peers = [((1 - x) if fx else x, (1 - y) if fy else y, (1 - c) if fc else c) for fx, fy, fc in flips]
        copies = [_remote(ins[a].at[1 - c], outs[a], send_sems.at[a], recv_sems.at[a], sibling) for a in range(n)]
        copies += [_remote(small_ref, gather_ref.at[me], send_sems.at[n + k], recv_sems.at[n + k], peer)
                   for k, peer in enumerate(peers)]
        for cp in copies:
            cp.start()
        for a in range(n):
            _remote(ins[a].at[1 - c], outs[a], send_sems.at[a], recv_sems.at[a], sibling).wait_recv()
        for k, (px, py, pc) in enumerate(peers):
            _remote(small_ref, gather_ref.at[4 * px + 2 * py + pc], send_sems.at[n + k], recv_sems.at[n + k],
                    (px, py, pc)).wait_recv()
        for cp in copies:
            cp.wait_send()
        own.wait()

    return pl.pallas_call(
        body, name=name,
        in_specs=[HBM_SPEC] * (n + 1), out_specs=[HBM_SPEC] * (n + 1),
        out_shape=[jax.ShapeDtypeStruct(p.shape[1:], p.dtype) for p in parts]
        + [jax.ShapeDtypeStruct((2 * N_CHIPS,) + small.shape, small.dtype)],
        scratch_shapes=[pltpu.SemaphoreType.DMA((n + 7,)), pltpu.SemaphoreType.DMA((n + 7,)), pltpu.SemaphoreType.DMA],
    )(*parts, small)


def _scatter(sums, name):
    n = len(sums)

    def body(*refs):
        ins, outs = refs[:n], refs[n:2 * n]
        send_sems, recv_sems, local_sems = refs[2 * n:]
        x, y, c, chips = _place()
        me = 2 * x + y
        slots = [2 * cx + cy for cx, cy in chips]
        local = [pltpu.make_async_copy(ins[a].at[me], outs[a].at[me], local_sems.at[a]) for a in range(n)]
        for cp in local:
            cp.start()
        copies = [_remote(ins[a].at[slots[k]], outs[a].at[me], send_sems.at[3 * a + k], recv_sems.at[3 * a + k], (*chips[k], c))
                  for a in range(n) for k in range(3)]
        for cp in copies:
            cp.start()
        for a in range(n):
            for k in range(3):
                _remote(ins[a].at[slots[k]], outs[a].at[slots[k]], send_sems.at[3 * a + k], recv_sems.at[3 * a + k],
                        (*chips[k], c)).wait_recv()
        for cp in copies:
            cp.wait_send()
        for cp in local:
            cp.wait()

    return pl.pallas_call(
        body, name=name,
        in_specs=[HBM_SPEC] * n, out_specs=[HBM_SPEC] * n,
        out_shape=[jax.ShapeDtypeStruct(s.shape, s.dtype) for s in sums],
        scratch_shapes=[pltpu.SemaphoreType.DMA((3 * n,)), pltpu.SemaphoreType.DMA((3 * n,)), pltpu.SemaphoreType.DMA((n,))],
    )(*sums)


def _sibling_share(halves, name):
    n = len(halves)

    def body(*refs):
        ins, outs = refs[:n], refs[n:2 * n]
        send_sems, recv_sems, local_sems = refs[2 * n:]
        x, y, c, _ = _place()
        sibling = (x, y, 1 - c)
        local = [pltpu.make_async_copy(ins[a], outs[a].at[c], local_sems.at[a]) for a in range(n)]
        for cp in local:
            cp.start()
        copies = [_remote(ins[a], outs[a].at[c], send_sems.at[a], recv_sems.at[a], sibling) for a in range(n)]
        for cp in copies:
            cp.start()
        for a in range(n):
            _remote(ins[a], outs[a].at[1 - c], send_sems.at[a], recv_sems.at[a], sibling).wait_recv()
        for cp in copies:
            cp.wait_send()
        for cp in local:
            cp.wait()

    return pl.pallas_call(
        body, name=name,
        in_specs=[HBM_SPEC] * n, out_specs=[HBM_SPEC] * n,
        out_shape=[jax.ShapeDtypeStruct((2,) + h.shape, h.dtype) for h in halves],
        scratch_shapes=[pltpu.SemaphoreType.DMA((n,)), pltpu.SemaphoreType.DMA((n,)), pltpu.SemaphoreType.DMA((n,))],
    )(*halves)


def _pair_sum(core, part, received, name):
    _, _, rh, cols = part.shape

    def body(core_ref, p_ref, r_ref, o_ref):
        o_ref[...] = (p_ref[0].astype(F32) + r_ref[...].astype(F32)).astype(BF16)

    return pl.pallas_call(
        body, name=name,
        grid_spec=pltpu.PrefetchScalarGridSpec(
            num_scalar_prefetch=1, grid=(N_CHIPS,),
            in_specs=[pl.BlockSpec((1, 1, rh, cols), lambda j, core_ref: (core_ref[0], j, 0, 0)),
                      pl.BlockSpec((1, rh, cols), lambda j, core_ref: (j, 0, 0))],
            out_specs=pl.BlockSpec((1, rh, cols), lambda j, core_ref: (j, 0, 0))),
        out_shape=jax.ShapeDtypeStruct((N_CHIPS, rh, cols), BF16),
        compiler_params=_params(("parallel",), 32),
    )(core, part, received)


def _sum_leading(stack, steps, name):
    k, rows, cols = stack.shape
    tile = rows // steps

    def body(s_ref, o_ref):
        total = s_ref[0].astype(F32)
        for d in range(1, k):
            total = total + s_ref[d].astype(F32)
        o_ref[...] = total

    return pl.pallas_call(
        body, name=name, grid=(steps,),
        in_specs=[pl.BlockSpec((k, tile, cols), lambda i: (0, i, 0))],
        out_specs=pl.BlockSpec((tile, cols), lambda i: (i, 0)),
        out_shape=jax.ShapeDtypeStruct((rows, cols), F32),
        compiler_params=_params(("parallel",), 32),
    )(stack)


def _adamw(w, g, m, v, steps, name):
    rows, cols = w.shape
    tile = rows // steps
    bc1 = 1.0 - ADAM_B1 ** ADAM_STEP
    bc2 = 1.0 - ADAM_B2 ** ADAM_STEP

    def body(w_ref, g_ref, m_ref, v_ref, d_ref, mo_ref, vo_ref):
        g = g_ref[...]
        m_new = ADAM_B1 * m_ref[...] + (1.0 - ADAM_B1) * g
        v_new = ADAM_B2 * v_ref[...] + (1.0 - ADAM_B2) * (g * g)
        d_ref[...] = -ADAM_LR * ((m_new / bc1) / (jnp.sqrt(v_new / bc2) + ADAM_EPS) + ADAM_WD * w_ref[...])
        mo_ref[...] = m_new
        vo_ref[...] = v_new

    spec = pl.BlockSpec((tile, cols), lambda i: (i, 0))
    return pl.pallas_call(
        body, name=name, grid=(steps,),
        in_specs=[spec] * 4, out_specs=[spec] * 3,
        out_shape=[jax.ShapeDtypeStruct((rows, cols), F32)] * 3,
        compiler_params=_params(("parallel",), 32),
    )(w, g, m, v)


SMALL = ("ffn1_norm", "mix_norm", "ffn2_norm", "final_norm", "pool_scale", "sink_logits", "pool_w")


def _pack_small(d):
    sink = jnp.pad(d["sink_logits"].reshape(1, N_HEADS), ((0, 0), (0, LANES - N_HEADS)))
    rows = [d[n].reshape(-1, LANES) for n in SMALL[:5]] + [sink, d["pool_w"].reshape(-1, LANES)]
    packed = jnp.concatenate(rows, axis=0)
    return jnp.pad(packed, ((0, SMALL_ROWS - packed.shape[0]), (0, 0)))


def _unpack_small(packed, like):
    out, row = {}, 0
    for n in SMALL:
        size = LANES if n == "sink_logits" else math.prod(like[n].shape)
        chunk = packed[row:row + size // LANES].reshape(-1)
        out[n] = (chunk[:N_HEADS] if n == "sink_logits" else chunk).reshape(like[n].shape)
        row += size // LANES
    return out


def kernel(x, ffn1_norm, ffn1_w_gate, ffn1_w_up, ffn1_w_down, mix_norm, w_in, sink_logits, pool_w, pool_scale, w_out, ffn2_norm, ffn2_w_gate, ffn2_w_up, ffn2_w_down, final_norm, loss_target, m_ffn1_norm, m_ffn1_w_gate, m_ffn1_w_up, m_ffn1_w_down, m_mix_norm, m_w_in, m_sink_logits, m_pool_w, m_pool_scale, m_w_out, m_ffn2_norm, m_ffn2_w_gate, m_ffn2_w_up, m_ffn2_w_down, m_final_norm, v_ffn1_norm, v_ffn1_w_gate, v_ffn1_w_up, v_ffn1_w_down, v_mix_norm, v_w_in, v_sink_logits, v_pool_w, v_pool_scale, v_w_out, v_ffn2_norm, v_ffn2_w_gate, v_ffn2_w_up, v_ffn2_w_down, v_final_norm):
    names = ("ffn1_norm", "ffn1_w_gate", "ffn1_w_up", "ffn1_w_down", "mix_norm", "w_in", "sink_logits", "pool_w",
             "pool_scale", "w_out", "ffn2_norm", "ffn2_w_gate", "ffn2_w_up", "ffn2_w_down", "final_norm")
    weights = dict(zip(names, (ffn1_norm, ffn1_w_gate, ffn1_w_up, ffn1_w_down, mix_norm, w_in, sink_logits, pool_w,
                               pool_scale, w_out, ffn2_norm, ffn2_w_gate, ffn2_w_up, ffn2_w_down, final_norm)))
    mom1 = dict(zip(names, (m_ffn1_norm, m_ffn1_w_gate, m_ffn1_w_up, m_ffn1_w_down, m_mix_norm, m_w_in, m_sink_logits,
                            m_pool_w, m_pool_scale, m_w_out, m_ffn2_norm, m_ffn2_w_gate, m_ffn2_w_up, m_ffn2_w_down,
                            m_final_norm)))
    mom2 = dict(zip(names, (v_ffn1_norm, v_ffn1_w_gate, v_ffn1_w_up, v_ffn1_w_down, v_mix_norm, v_w_in, v_sink_logits,
                            v_pool_w, v_pool_scale, v_w_out, v_ffn2_norm, v_ffn2_w_gate, v_ffn2_w_up, v_ffn2_w_down,
                            v_final_norm)))
    core = lax.axis_index("c").astype(jnp.int32).reshape(1)

    shards = [weights[n][0] for n in BIG]
    halves = [s.astype(BF16).reshape(2, s.shape[0] // 2, s.shape[1]) for s in shards]
    gathered = _allgather(halves, "weight_allgather")
    full = {n: g.reshape(N_CHIPS, s.shape[0], s.shape[1]) for n, g, s in zip(BIG, gathered, shards)}

    small_w = {"ffn1_norm": ffn1_norm, "mix_norm": mix_norm, "ffn2_norm": ffn2_norm,
               "final_norm": final_norm.reshape(1, D_MODEL), "pool_scale": pool_scale, "sink_logits": sink_logits,
               "pool_w": pool_w[0]}
    loss_lanes, grad_x, big_g, small_g = _local_step(x[0], loss_target[0], full, small_w)

    received = _sibling_exchange([big_g[n] for n in BIG], _pack_small(small_g), "grad_sibling_exchange")
    chip_sums = [_pair_sum(core, big_g[n], r, "grad_pair_sum_" + n) for n, r in zip(BIG, received[:-1])]
    small_sum = _sum_leading(received[-1], 1, "small_grad_sum")
    stacks = _scatter(chip_sums, "grad_scatter")
    reduced = [_sum_leading(s, 2, "grad_chip_sum_" + n) for n, s in zip(BIG, stacks)]
    both = _sibling_share(reduced, "grad_sibling_share")
    grads = {n: b.reshape(s.shape) for n, b, s in zip(BIG, both, shards)}

    out_g, out_d, out_m, out_v = {}, {}, {}, {}
    for n, s in zip(BIG, shards):
        d, mn, vn = _adamw(s, grads[n], mom1[n][0], mom2[n][0], 4, "adamw_" + n)
        out_g[n], out_d[n], out_m[n], out_v[n] = (t.reshape(weights[n].shape) for t in (grads[n], d, mn, vn))
    sd, sm, sv = _adamw(_pack_small(weights), small_sum, _pack_small(mom1), _pack_small(mom2), 1, "adamw_small")
    for dst, packed in ((out_g, small_sum), (out_d, sd), (out_m, sm), (out_v, sv)):
        dst.update(_unpack_small(packed, weights))

    loss = lax.psum(jnp.sum(loss_lanes), ("x", "y", "c"))
    return (loss, grad_x.reshape(x.shape), *[out_g[n] for n in names], *[out_d[n] for n in names],
            *[out_m[n] for n in names], *[out_v[n] for n in names])
```

```python
import functools
import math

import jax
import jax.numpy as jnp
from jax import lax
from jax.experimental import pallas as pl
from jax.experimental.pallas import tpu as pltpu

F32, BF16 = jnp.float32, jnp.bfloat16
MESH = pl.DeviceIdType.MESH

D_MODEL = 1024
D_FF = 2816
N_CHIPS = 4
FF_CHUNK = D_FF // N_CHIPS
HEAD_DIM = 64
N_HEADS = 8
N_KV = 2
Q_PER_KV = N_HEADS // N_KV
KV_WIDTH = N_KV * HEAD_DIM
ATTN_WIDTH = N_HEADS * HEAD_DIM
POOL_WINDOWS = (2, 4, 8, 16)
N_POOL = len(POOL_WINDOWS)
POOL_GROUP = 128
POOL_WIDTH = N_POOL * POOL_GROUP
IN_WIDTH = ATTN_WIDTH + 2 * KV_WIDTH + POOL_WIDTH
WINDOW = 128
BLOCK = 128
BAND = 3 * BLOCK
ROPE_THETA = 500000.0
ROTARY_DIM = HEAD_DIM // 4
EPS = 1e-6
LANES = 128
Q_PAD = N_HEADS * LANES
U_PAD = Q_PAD + 2 * KV_WIDTH + POOL_WIDTH
SCALE = HEAD_DIM ** -0.5
NEG = -1e30

ADAM_LR, ADAM_B1, ADAM_B2, ADAM_EPS, ADAM_WD, ADAM_STEP = 0.001, 0.9, 0.999, 1e-08, 0.01, 10

V7X_VMEM_BYTES = 64 * 1024 * 1024
TOK_TILE = 512
BWD_TILE = 256
SMALL_ROWS = 552


def _params(sem, vmem_mb):
    return pltpu.CompilerParams(dimension_semantics=sem, vmem_limit_bytes=vmem_mb * 1024 * 1024)


def _dot(a, b):
    return lax.dot_general(a, b, (((1,), (0,)), ((), ())), preferred_element_type=F32)


def _dot_nt(a, b):
    return lax.dot_general(a, b, (((1,), (1,)), ((), ())), preferred_element_type=F32)


def _dot_tn(a, b):
    return lax.dot_general(a, b, (((0,), (0,)), ((), ())), preferred_element_type=F32)


def _rms_stats(h):
    r = lax.rsqrt(jnp.mean(h * h, axis=-1, keepdims=True) + EPS)
    return r, h * r


def _rms_bwd(dn, g, r, xh):
    gd = dn * g
    dh = r * (gd - xh * jnp.mean(gd * xh, axis=-1, keepdims=True))
    return dh, jnp.sum(dn * xh, axis=0, keepdims=True)


def _rope(x, c, s1, s2):
    return x * c + pltpu.roll(x, LANES - ROTARY_DIM // 2, 1) * s1 + pltpu.roll(x, ROTARY_DIM // 2, 1) * s2


def _rope_bwd(d, c, s1, s2):
    return d * c + pltpu.roll(d * s1, ROTARY_DIM // 2, 1) + pltpu.roll(d * s2, LANES - ROTARY_DIM // 2, 1)


def _sum_chunks(ref):
    total = ref[0].astype(F32)
    for j in range(1, N_CHIPS):
        total = total + ref[j].astype(F32)
    return total


def _chunk_rows(tile):
    return pl.BlockSpec((N_CHIPS, tile, D_MODEL), lambda i, *_: (0, i, 0))


def _full(shape):
    nd = len(shape)
    return pl.BlockSpec(shape, lambda *_: (0,) * nd)


def _rows(tile, cols):
    return pl.BlockSpec((tile, cols), lambda i, *_: (i, 0))


def _ffn_fwd(h, gain, group, name):
    S = h.shape[0]
    tile = min(TOK_TILE, S)
    nt = S // tile

    def body(h_ref, g_ref, wg_ref, wu_ref, wd_ref, ho_ref, n_ref, gate_ref, up_ref, acc):
        j = pl.program_id(1)

        @pl.when(j == 0)
        def _():
            _, xh = _rms_stats(h_ref[...])
            n_ref[...] = (xh * g_ref[...]).astype(BF16)
            acc[...] = jnp.zeros_like(acc)

        n = n_ref[...]
        gate = _dot_nt(n, wg_ref[0])
        up = _dot_nt(n, wu_ref[0])
        gate_ref[0] = gate.astype(BF16)
        up_ref[0] = up.astype(BF16)
        act = (gate * jax.nn.sigmoid(gate) * up).astype(BF16)
        acc[...] += _dot(act, wd_ref[0])

        @pl.when(j == N_CHIPS - 1)
        def _():
            ho_ref[...] = h_ref[...] + 0.5 * acc[...]

    tok = pl.BlockSpec((tile, D_MODEL), lambda i, j: (i, 0))
    hid = pl.BlockSpec((1, tile, FF_CHUNK), lambda i, j: (j, i, 0))
    return pl.pallas_call(
        body, name=name, grid=(nt, N_CHIPS),
        in_specs=[tok, pl.BlockSpec((1, D_MODEL), lambda i, j: (0, 0))]
        + [pl.BlockSpec((1, FF_CHUNK, D_MODEL), functools.partial(lambda i, j, part: (j, part, 0), part=part))
           for part in range(3)],
        out_specs=[tok, tok, hid, hid],
        out_shape=[jax.ShapeDtypeStruct((S, D_MODEL), F32), jax.ShapeDtypeStruct((S, D_MODEL), BF16),
                   jax.ShapeDtypeStruct((N_CHIPS, S, FF_CHUNK), BF16), jax.ShapeDtypeStruct((N_CHIPS, S, FF_CHUNK), BF16)],
        scratch_shapes=[pltpu.VMEM((tile, D_MODEL), F32)],
        compiler_params=_params(("parallel", "arbitrary"), 40),
    )(h, gain, group, group, group)


def _ffn_bwd(d_out, n, gate, up, group, name):
    S = n.shape[0]
    tile = min(TOK_TILE, S)
    nt = S // tile
    half_rows = 3 * FF_CHUNK // 2
    cut = FF_CHUNK // 2

    def body(do_ref, n_ref, gate_ref, up_ref, wg_ref, wu_ref, wd_ref, dn_ref, dw_ref, acc_g, acc_u, acc_d):
        j, i = pl.program_id(0), pl.program_id(1)

        @pl.when(i == 0)
        def _():
            acc_g[...] = jnp.zeros_like(acc_g)
            acc_u[...] = jnp.zeros_like(acc_u)
            acc_d[...] = jnp.zeros_like(acc_d)

        do = do_ref[...]
        nn = n_ref[...]
        g = gate_ref[0].astype(F32)
        u = up_ref[0].astype(F32)
        d_act = _dot_nt(do, wd_ref[0])
        sig = jax.nn.sigmoid(g)
        silu = g * sig
        d_up = (d_act * silu).astype(BF16)
        d_gate = (d_act * u * (sig * (1.0 + g * (1.0 - sig)))).astype(BF16)
        act = (silu * u).astype(BF16)
        dn_ref[0] = (_dot(d_gate, wg_ref[0]) + _dot(d_up, wu_ref[0])).astype(BF16)
        acc_g[...] += _dot_tn(d_gate, nn)
        acc_u[...] += _dot_tn(d_up, nn)
        acc_d[...] += _dot_tn(act, do)

        @pl.when(i == nt - 1)
        def _():
            dw_ref[0, 0, :FF_CHUNK, :] = acc_g[...].astype(BF16)
            dw_ref[0, 0, FF_CHUNK:, :] = acc_u[:cut, :].astype(BF16)
            dw_ref[1, 0, :cut, :] = acc_u[cut:, :].astype(BF16)
            dw_ref[1, 0, cut:, :] = acc_d[...].astype(BF16)

    tok = pl.BlockSpec((tile, D_MODEL), lambda j, i: (i, 0))
    hid = pl.BlockSpec((1, tile, FF_CHUNK), lambda j, i: (j, i, 0))
    return pl.pallas_call(
        body, name=name, grid=(N_CHIPS, nt),
        in_specs=[tok, tok, hid, hid]
        + [pl.BlockSpec((1, FF_CHUNK, D_MODEL), functools.partial(lambda j, i, part: (j, part, 0), part=part))
           for part in range(3)],
        out_specs=[pl.BlockSpec((1, tile, D_MODEL), lambda j, i: (j, i, 0)),
                   pl.BlockSpec((2, 1, half_rows, D_MODEL), lambda j, i: (0, j, 0, 0))],
        out_shape=[jax.ShapeDtypeStruct((N_CHIPS, S, D_MODEL), BF16),
                   jax.ShapeDtypeStruct((2, N_CHIPS, half_rows, D_MODEL), BF16)],
        scratch_shapes=[pltpu.VMEM((FF_CHUNK, D_MODEL), F32)] * 3,
        compiler_params=_params(("arbitrary", "arbitrary"), 56),
    )(d_out, n, gate, up, group, group, group)


def _mix_in(h, gain, w_in, rc, rs1, rs2, name):
    S = h.shape[0]
    tile = min(TOK_TILE, S)

    def body(h_ref, g_ref, w_ref, c_ref, s1_ref, s2_ref, n_ref, q_ref, k_ref, v_ref, pc_ref):
        _, xh = _rms_stats(h_ref[...])
        n = (xh * g_ref[...]).astype(BF16)
        n_ref[...] = n
        u = _dot_nt(n, w_ref[...])
        c, s1, s2 = c_ref[...], s1_ref[...], s2_ref[...]
        for hd in range(N_HEADS):
            q_ref[:, hd * LANES:(hd + 1) * LANES] = _rope(u[:, hd * LANES:(hd + 1) * LANES], c, s1, s2).astype(BF16)
        k_ref[...] = _rope(u[:, Q_PAD:Q_PAD + KV_WIDTH], c, s1, s2).astype(BF16)
        v_ref[...] = u[:, Q_PAD + KV_WIDTH:Q_PAD + 2 * KV_WIDTH].astype(BF16)
        pc_ref[...] = u[:, Q_PAD + 2 * KV_WIDTH:]

    return pl.pallas_call(
        body, name=name, grid=(S // tile,),
        in_specs=[_rows(tile, D_MODEL), _full((1, D_MODEL)), _full((U_PAD, D_MODEL)),
                  _rows(tile, LANES), _rows(tile, LANES), _rows(tile, LANES)],
        out_specs=[_rows(tile, D_MODEL), _rows(tile, Q_PAD), _rows(tile, KV_WIDTH), _rows(tile, KV_WIDTH),
                   _rows(tile, POOL_WIDTH)],
        out_shape=[jax.ShapeDtypeStruct((S, D_MODEL), BF16), jax.ShapeDtypeStruct((S, Q_PAD), BF16),
                   jax.ShapeDtypeStruct((S, KV_WIDTH), BF16), jax.ShapeDtypeStruct((S, KV_WIDTH), BF16),
                   jax.ShapeDtypeStruct((S, POOL_WIDTH), F32)],
        compiler_params=_params(("parallel",), 40),
    )(h, gain, w_in, rc, rs1, rs2)


def _band_start(i, S):
    return pl.multiple_of(jnp.clip((i - 1) * BLOCK, 0, S - BAND), BLOCK)


def _probs(qh, kb, off, sink_h):
    s = _dot_nt(qh, kb) * SCALE
    r = lax.broadcasted_iota(jnp.int32, (BLOCK, 1), 0)
    c = lax.broadcasted_iota(jnp.int32, (1, BAND), 1)
    s = jnp.where(jnp.abs(off + r - c) <= WINDOW, s, NEG)
    m = jnp.maximum(jnp.max(s, axis=-1, keepdims=True), sink_h)
    p = jnp.exp(s - m)
    es = jnp.exp(sink_h - m)
    inv = 1.0 / (jnp.sum(p, axis=-1, keepdims=True) + es)
    return p * inv, es * inv


def _pool_matrix(t0, start, S, w):
    r = lax.broadcasted_iota(jnp.int32, (BLOCK, 1), 0) + t0
    c = lax.broadcasted_iota(jnp.int32, (1, BAND), 1) + start
    half = w // 2

    def window(lo, hi):
        a = jnp.maximum(lo, 0)
        b = jnp.minimum(hi + 1, S)
        return jnp.where((c >= a) & (c < b), 1.0 / (b - a).astype(F32), 0.0)

    return (0.5 * (window(r - half, r + half - 1) + window(r - half + 1, r + half))).astype(BF16)


def _mix_core_fwd(q, k, v, pc, sink, pool_w, pool_scale, name):
    S = q.shape[0]

    def body(sink_ref, q_ref, k_ref, v_ref, pc_ref, pw_ref, ps_ref, a_ref, p_ref):
        i = pl.program_id(0)
        start = _band_start(i, S)
        band = pl.ds(start, BAND)
        off = i * BLOCK - start
        kb, vb = k_ref[band, :], v_ref[band, :]
        for hd in range(N_HEADS):
            prob, _ = _probs(q_ref[:, hd * LANES:(hd + 1) * LANES], kb, off, sink_ref[0, hd])
            a_ref[:, hd * LANES:(hd + 1) * LANES] = _dot(prob.astype(BF16), vb).astype(BF16)
        centre = pl.ds(pl.multiple_of(i * BLOCK, BLOCK), BLOCK)
        for g, w in enumerate(POOL_WINDOWS):
            cols = slice(g * POOL_GROUP, (g + 1) * POOL_GROUP)
            mean = _dot(_pool_matrix(i * BLOCK, start, S, w), pc_ref[band, cols].astype(BF16))
            dev = (mean - pc_ref[centre, cols]).astype(BF16)
            p_ref[:, cols] = (_dot(dev, pw_ref[g]) * ps_ref[:, cols]).astype(BF16)

    return pl.pallas_call(
        body, name=name, grid=(S // BLOCK,),
        in_specs=[pl.BlockSpec(memory_space=pltpu.SMEM), _rows(BLOCK, Q_PAD), _full((S, KV_WIDTH)), _full((S, KV_WIDTH)),
                  _full((S, POOL_WIDTH)), _full((N_POOL, POOL_GROUP, POOL_GROUP)), _full((1, POOL_WIDTH))],
        out_specs=[_rows(BLOCK, Q_PAD), _rows(BLOCK, POOL_WIDTH)],
        out_shape=[jax.ShapeDtypeStruct((S, Q_PAD), BF16), jax.ShapeDtypeStruct((S, POOL_WIDTH), BF16)],
        compiler_params=_params(("parallel",), 40),
    )(sink, q, k, v, pc, pool_w, pool_scale)


def _mix_core_bwd(q, k, v, pc, da, dp, sink, pool_w, pool_scale, rc, rs1, rs2, name):
    S = q.shape[0]
    nb = S // BLOCK

    def body(sink_ref, q_ref, k_ref, v_ref, pc_ref, da_ref, dp_ref, pw_ref, ps_ref, c_ref, s1_ref, s2_ref,
             dq_ref, dk_ref, dv_ref, dpc_ref, dsink_ref, dpw_ref, dps_ref):
        i = pl.program_id(0)

        @pl.when(i == 0)
        def _():
            dk_ref[...] = jnp.zeros_like(dk_ref)
            dv_ref[...] = jnp.zeros_like(dv_ref)
            dpc_ref[...] = jnp.zeros_like(dpc_ref)
            dsink_ref[...] = jnp.zeros_like(dsink_ref)
            dpw_ref[...] = jnp.zeros_like(dpw_ref)
            dps_ref[...] = jnp.zeros_like(dps_ref)

        start = _band_start(i, S)
        band = pl.ds(start, BAND)
        off = i * BLOCK - start
        kb, vb = k_ref[band, :], v_ref[band, :]
        c, s1, s2 = c_ref[...], s1_ref[...], s2_ref[...]
        lane = lax.broadcasted_iota(jnp.int32, (1, LANES), 1)
        dk_blk = jnp.zeros((BAND, KV_WIDTH), F32)
        dv_blk = jnp.zeros((BAND, KV_WIDTH), F32)
        dsink = jnp.zeros((1, LANES), F32)
        for hd in range(N_HEADS):
            cols = slice(hd * LANES, (hd + 1) * LANES)
            qh = q_ref[:, cols]
            dah = da_ref[:, cols]
            prob, p_sink = _probs(qh, kb, off, sink_ref[0, hd])
            d_prob = _dot_nt(dah, vb)
            delta = jnp.sum(prob * d_prob, axis=-1, keepdims=True)
            d_s = (prob * (d_prob - delta) * SCALE).astype(BF16)
            dq_ref[:, cols] = _rope_bwd(_dot(d_s, kb), c, s1, s2).astype(BF16)
            dk_blk += _dot_tn(d_s, qh)
            dv_blk += _dot_tn(prob.astype(BF16), dah)
            dsink += jnp.where(lane == hd, -jnp.sum(p_sink * delta, axis=0, keepdims=True), 0.0)
        dk_ref[band, :] += dk_blk
        dv_ref[band, :] += dv_blk
        dsink_ref[...] += dsink

        centre = pl.ds(pl.multiple_of(i * BLOCK, BLOCK), BLOCK)
        for g, w in enumerate(POOL_WINDOWS):
            cols = slice(g * POOL_GROUP, (g + 1) * POOL_GROUP)
            pm = _pool_matrix(i * BLOCK, start, S, w)
            dev = (_dot(pm, pc_ref[band, cols].astype(BF16)) - pc_ref[centre, cols]).astype(BF16)
            dy = dp_ref[:, cols].astype(F32)
            z = _dot(dev, pw_ref[g])
            dps_ref[:, cols] += jnp.sum(dy * z, axis=0, keepdims=True)
            dz = (dy * ps_ref[:, cols]).astype(BF16)
            dpw_ref[g] += _dot_tn(dev, dz)
            d_dev = _dot_nt(dz, pw_ref[g])
            dpc_ref[band, cols] += _dot_tn(pm, d_dev.astype(BF16))
            dpc_ref[centre, cols] -= d_dev

    return pl.pallas_call(
        body, name=name, grid=(nb,),
        in_specs=[pl.BlockSpec(memory_space=pltpu.SMEM), _rows(BLOCK, Q_PAD), _full((S, KV_WIDTH)), _full((S, KV_WIDTH)),
                  _full((S, POOL_WIDTH)), _rows(BLOCK, Q_PAD), _rows(BLOCK, POOL_WIDTH),
                  _full((N_POOL, POOL_GROUP, POOL_GROUP)), _full((1, POOL_WIDTH)),
                  _rows(BLOCK, LANES), _rows(BLOCK, LANES), _rows(BLOCK, LANES)],
        out_specs=[_rows(BLOCK, Q_PAD), _full((S, KV_WIDTH)), _full((S, KV_WIDTH)), _full((S, POOL_WIDTH)),
                   _full((1, LANES)), _full((N_POOL, POOL_GROUP, POOL_GROUP)), _full((1, POOL_WIDTH))],
        out_shape=[jax.ShapeDtypeStruct((S, Q_PAD), BF16), jax.ShapeDtypeStruct((S, KV_WIDTH), F32),
                   jax.ShapeDtypeStruct((S, KV_WIDTH), F32), jax.ShapeDtypeStruct((S, POOL_WIDTH), F32),
                   jax.ShapeDtypeStruct((1, LANES), F32), jax.ShapeDtypeStruct((N_POOL, POOL_GROUP, POOL_GROUP), F32),
                   jax.ShapeDtypeStruct((1, POOL_WIDTH), F32)],
        compiler_params=_params(("arbitrary",), 56),
    )(sink, q, k, v, pc, da, dp, pool_w, pool_scale, rc, rs1, rs2)


def _mix_out(h, a, p, wa, wp, name):
    S = h.shape[0]
    tile = min(TOK_TILE, S)

    def body(h_ref, a_ref, p_ref, wa_ref, wp_ref, o_ref):
        o_ref[...] = h_ref[...] + _dot(a_ref[...], wa_ref[...]) + _dot(p_ref[...], wp_ref[...])

    return pl.pallas_call(
        body, name=name, grid=(S // tile,),
        in_specs=[_rows(tile, D_MODEL), _rows(tile, Q_PAD), _rows(tile, POOL_WIDTH),
                  _full((Q_PAD, D_MODEL)), _full((POOL_WIDTH, D_MODEL))],
        out_specs=_rows(tile, D_MODEL),
        out_shape=jax.ShapeDtypeStruct((S, D_MODEL), F32),
        compiler_params=_params(("parallel",), 40),
    )(h, a, p, wa, wp)


def _loss_head(h, target, gain, name):
    S = h.shape[0]
    tile = min(TOK_TILE, S)

    def body(h_ref, t_ref, g_ref, dh_ref, dhalf_ref, loss_ref, dg_ref):
        @pl.when(pl.program_id(0) == 0)
        def _():
            loss_ref[...] = jnp.zeros_like(loss_ref)
            dg_ref[...] = jnp.zeros_like(dg_ref)

        g = g_ref[...]
        r, xh = _rms_stats(h_ref[...])
        err = xh * g - t_ref[...]
        loss_ref[...] += (0.5 / D_MODEL) * jnp.sum(err * err, axis=0, keepdims=True)
        dh, dg = _rms_bwd(err * (1.0 / D_MODEL), g, r, xh)
        dg_ref[...] += dg
        dh_ref[...] = dh
        dhalf_ref[...] = (0.5 * dh).astype(BF16)

    return pl.pallas_call(
        body, name=name, grid=(S // tile,),
        in_specs=[_rows(tile, D_MODEL), _rows(tile, D_MODEL), _full((1, D_MODEL))],
        out_specs=[_rows(tile, D_MODEL), _rows(tile, D_MODEL), _full((1, D_MODEL)), _full((1, D_MODEL))],
        out_shape=[jax.ShapeDtypeStruct((S, D_MODEL), F32), jax.ShapeDtypeStruct((S, D_MODEL), BF16),
                   jax.ShapeDtypeStruct((1, D_MODEL), F32), jax.ShapeDtypeStruct((1, D_MODEL), F32)],
        compiler_params=_params(("arbitrary",), 40),
    )(h, target, gain)


def _mix_out_bwd(dh_out, dn, h, gain, a, p, wa, wp, name):
    S = h.shape[0]
    tile = min(TOK_TILE, S)

    def body(do_ref, dn_ref, h_ref, g_ref, a_ref, p_ref, wa_ref, wp_ref, dh_ref, da_ref, dp_ref, dwa_ref, dwp_ref, dg_ref):
        @pl.when(pl.program_id(0) == 0)
        def _():
            dwa_ref[...] = jnp.zeros_like(dwa_ref)
            dwp_ref[...] = jnp.zeros_like(dwp_ref)
            dg_ref[...] = jnp.zeros_like(dg_ref)

        r, xh = _rms_stats(h_ref[...])
        dnorm, dg = _rms_bwd(_sum_chunks(dn_ref), g_ref[...], r, xh)
        dh = do_ref[...] + dnorm
        dg_ref[...] += dg
        dh_ref[...] = dh
        dhb = dh.astype(BF16)
        da_ref[...] = _dot_nt(dhb, wa_ref[...]).astype(BF16)
        dp_ref[...] = _dot_nt(dhb, wp_ref[...]).astype(BF16)
        dwa_ref[...] += _dot_tn(a_ref[...], dhb)
        dwp_ref[...] += _dot_tn(p_ref[...], dhb)

    return pl.pallas_call(
        body, name=name, grid=(S // tile,),
        in_specs=[_rows(tile, D_MODEL), _chunk_rows(tile), _rows(tile, D_MODEL), _full((1, D_MODEL)),
                  _rows(tile, Q_PAD), _rows(tile, POOL_WIDTH), _full((Q_PAD, D_MODEL)), _full((POOL_WIDTH, D_MODEL))],
        out_specs=[_rows(tile, D_MODEL), _rows(tile, Q_PAD), _rows(tile, POOL_WIDTH),
                   _full((Q_PAD, D_MODEL)), _full((POOL_WIDTH, D_MODEL)), _full((1, D_MODEL))],
        out_shape=[jax.ShapeDtypeStruct((S, D_MODEL), F32), jax.ShapeDtypeStruct((S, Q_PAD), BF16),
                   jax.ShapeDtypeStruct((S, POOL_WIDTH), BF16), jax.ShapeDtypeStruct((Q_PAD, D_MODEL), F32),
                   jax.ShapeDtypeStruct((POOL_WIDTH, D_MODEL), F32), jax.ShapeDtypeStruct((1, D_MODEL), F32)],
        compiler_params=_params(("arbitrary",), 48),
    )(dh_out, dn, h, gain, a, p, wa, wp)


def _mix_in_bwd(dh_out, h, gain, n, dq, dk, dv, dpc, rc, rs1, rs2, w_in, name):
    S = h.shape[0]
    tile = min(TOK_TILE, S)

    def body(do_ref, h_ref, g_ref, n_ref, dq_ref, dk_ref, dv_ref, dpc_ref, c_ref, s1_ref, s2_ref, w_ref,
             dh_ref, dhalf_ref, dw_ref, dg_ref):
        @pl.when(pl.program_id(0) == 0)
        def _():
            dw_ref[...] = jnp.zeros_like(dw_ref)
            dg_ref[...] = jnp.zeros_like(dg_ref)

        dk = _rope_bwd(dk_ref[...], c_ref[...], s1_ref[...], s2_ref[...]).astype(BF16)
        du = jnp.concatenate([dq_ref[...], dk, dv_ref[...].astype(BF16), dpc_ref[...].astype(BF16)], axis=1)
        dn = _dot(du, w_ref[...])
        dw_ref[...] += _dot_tn(du, n_ref[...])
        r, xh = _rms_stats(h_ref[...])
        dnorm, dg = _rms_bwd(dn, g_ref[...], r, xh)
        dh = do_ref[...] + dnorm
        dg_ref[...] += dg
        dh_ref[...] = dh
        dhalf_ref[...] = (0.5 * dh).astype(BF16)

    return pl.pallas_call(
        body, name=name, grid=(S // tile,),
        in_specs=[_rows(tile, D_MODEL), _rows(tile, D_MODEL), _full((1, D_MODEL)), _rows(tile, D_MODEL),
                  _rows(tile, Q_PAD), _rows(tile, KV_WIDTH), _rows(tile, KV_WIDTH), _rows(tile, POOL_WIDTH),
                  _rows(tile, LANES), _rows(tile, LANES), _rows(tile, LANES), _full((U_PAD, D_MODEL))],
        out_specs=[_rows(tile, D_MODEL), _rows(tile, D_MODEL), _full((U_PAD, D_MODEL)), _full((1, D_MODEL))],
        out_shape=[jax.ShapeDtypeStruct((S, D_MODEL), F32), jax.ShapeDtypeStruct((S, D_MODEL), BF16),
                   jax.ShapeDtypeStruct((U_PAD, D_MODEL), F32), jax.ShapeDtypeStruct((1, D_MODEL), F32)],
        compiler_params=_params(("arbitrary",), 56),
    )(dh_out, h, gain, n, dq, dk, dv, dpc, rc, rs1, rs2, w_in)


def _norm_bwd(dh_out, dn, h, gain, name):
    S = h.shape[0]
    tile = min(TOK_TILE, S)

    def body(do_ref, dn_ref, h_ref, g_ref, dh_ref, dg_ref):
        @pl.when(pl.program_id(0) == 0)
        def _():
            dg_ref[...] = jnp.zeros_like(dg_ref)

        r, xh = _rms_stats(h_ref[...])
        dnorm, dg = _rms_bwd(_sum_chunks(dn_ref), g_ref[...], r, xh)
        dg_ref[...] += dg
        dh_ref[...] = do_ref[...] + dnorm

    return pl.pallas_call(
        body, name=name, grid=(S // tile,),
        in_specs=[_rows(tile, D_MODEL), _chunk_rows(tile), _rows(tile, D_MODEL), _full((1, D_MODEL))],
        out_specs=[_rows(tile, D_MODEL), _full((1, D_MODEL))],
        out_shape=[jax.ShapeDtypeStruct((S, D_MODEL), F32), jax.ShapeDtypeStruct((1, D_MODEL), F32)],
        compiler_params=_params(("arbitrary",), 40),
    )(dh_out, dn, h, gain)


def _rope_tables(S):
    half = ROTARY_DIM // 2
    inv_freq = ROPE_THETA ** (-jnp.arange(0, ROTARY_DIM, 2, dtype=F32) / ROTARY_DIM)
    dim = jnp.arange(LANES) % HEAD_DIM
    ang = jnp.arange(S, dtype=F32)[:, None] * inv_freq[dim % half][None, :]
    lo, hi = (dim < half)[None, :], ((dim >= half) & (dim < ROTARY_DIM))[None, :]
    c = jnp.where(lo | hi, jnp.cos(ang), 1.0)
    s1 = jnp.where(lo, -jnp.sin(ang), 0.0)
    s2 = jnp.where(hi, jnp.sin(ang), 0.0)
    return c, s1, s2


def _pad_heads(w, axis):
    w = jnp.moveaxis(w, axis, 0)
    heads = w.reshape((N_HEADS, HEAD_DIM) + w.shape[1:])
    zero = jnp.zeros_like(heads)
    first = (jnp.arange(N_HEADS) < Q_PER_KV).reshape((N_HEADS, 1) + (1,) * (w.ndim - 1))
    lo = jnp.where(first, heads, zero)
    hi = jnp.where(first, zero, heads)
    padded = jnp.concatenate([lo, hi], axis=1).reshape((Q_PAD,) + w.shape[1:])
    return jnp.moveaxis(padded, 0, axis)


def _unpad_heads(w, axis):
    w = jnp.moveaxis(w, axis, 0)
    groups = w.reshape((N_HEADS, 2, HEAD_DIM) + w.shape[1:])
    first = (jnp.arange(N_HEADS) < Q_PER_KV).reshape((N_HEADS, 1) + (1,) * (w.ndim - 1))
    heads = jnp.where(first, groups[:, 0], groups[:, 1]).reshape((ATTN_WIDTH,) + w.shape[1:])
    return jnp.moveaxis(heads, 0, axis)


IN_ROWS = IN_WIDTH // N_CHIPS
OUT_ROWS = (ATTN_WIDTH + POOL_WIDTH) // N_CHIPS
MIX_ROWS = IN_ROWS + OUT_ROWS
FFN_ROWS = 3 * FF_CHUNK


def _local_step(x, target, ffn1, mix, ffn2, small):
    S = x.shape[0]
    rc, rs1, rs2 = _rope_tables(S)
    w_in_t = mix[:, :IN_ROWS].reshape(IN_WIDTH, D_MODEL)
    w_in_pad = jnp.concatenate([_pad_heads(w_in_t[:ATTN_WIDTH], 0), w_in_t[ATTN_WIDTH:]], axis=0)
    w_out = mix[:, IN_ROWS:].reshape(ATTN_WIDTH + POOL_WIDTH, D_MODEL)
    wa = _pad_heads(w_out[:ATTN_WIDTH], 0)
    wp = w_out[ATTN_WIDTH:]
    pool_w = small["pool_w"].astype(BF16)

    h1, n1, gate1, up1 = _ffn_fwd(x, small["ffn1_norm"], ffn1, "ffn1_fwd")
    n2, q, k, v, pc = _mix_in(h1, small["mix_norm"], w_in_pad, rc, rs1, rs2, "mix_in")
    a, p = _mix_core_fwd(q, k, v, pc, small["sink_logits"], pool_w, small["pool_scale"], "mix_core_fwd")
    h2 = _mix_out(h1, a, p, wa, wp, "mix_out")
    h3, n3, gate2, up2 = _ffn_fwd(h2, small["ffn2_norm"], ffn2, "ffn2_fwd")
    dh3, dhalf3, loss_lanes, d_final = _loss_head(h3, target, small["final_norm"], "loss_head")

    dn3, d_ffn2 = _ffn_bwd(dhalf3, n3, gate2, up2, ffn2, "ffn2_bwd")
    dh2, da, dp, dwa, dwp, d_ffn2_norm = _mix_out_bwd(dh3, dn3, h2, small["ffn2_norm"], a, p, wa, wp, "mix_out_bwd")
    dq, dk, dv, dpc, dsink, dpool_w, dpool_scale = _mix_core_bwd(
        q, k, v, pc, da, dp, small["sink_logits"], pool_w, small["pool_scale"], rc, rs1, rs2, "mix_core_bwd")
    dh1, dhalf1, dw_in_pad, d_mix_norm = _mix_in_bwd(dh2, h1, small["mix_norm"], n2, dq, dk, dv, dpc, rc, rs1, rs2,
                                                     w_in_pad, "mix_in_bwd")
    dn1, d_ffn1 = _ffn_bwd(dhalf1, n1, gate1, up1, ffn1, "ffn1_bwd")
    grad_x, d_ffn1_norm = _norm_bwd(dh1, dn1, x, small["ffn1_norm"], "norm1_bwd")

    dw_in_t = jnp.concatenate([_unpad_heads(dw_in_pad[:Q_PAD], 0), dw_in_pad[Q_PAD:]], axis=0)
    dw_out = jnp.concatenate([_unpad_heads(dwa, 0), dwp], axis=0)
    d_mix = jnp.concatenate([dw_in_t.reshape(N_CHIPS, IN_ROWS, D_MODEL), dw_out.reshape(N_CHIPS, OUT_ROWS, D_MODEL)], axis=1)
    d_mix = jnp.transpose(d_mix.reshape(N_CHIPS, 2, MIX_ROWS // 2, D_MODEL), (1, 0, 2, 3)).astype(BF16)
    small_g = {"ffn1_norm": d_ffn1_norm, "mix_norm": d_mix_norm, "ffn2_norm": d_ffn2_norm, "final_norm": d_final,
               "pool_scale": dpool_scale, "sink_logits": dsink[:, :N_HEADS], "pool_w": dpool_w}
    return loss_lanes, grad_x, (d_ffn1, d_mix, d_ffn2), small_g


HBM_SPEC = pl.BlockSpec(memory_space=pltpu.HBM)
GROUPS = (("ffn1_w_gate", "ffn1_w_up", "ffn1_w_down"), ("w_in", "w_out"), ("ffn2_w_gate", "ffn2_w_up", "ffn2_w_down"))
TRANSPOSED = ("ffn1_w_gate", "ffn1_w_up", "w_in", "ffn2_w_gate", "ffn2_w_up")


def _place():
    x, y, c = lax.axis_index("x"), lax.axis_index("y"), lax.axis_index("c")
    chips = [(1 - x, y), (x, 1 - y), (1 - x, 1 - y)]
    return x, y, c, chips


def _remote(src, dst, send_sem, recv_sem, to):
    return pltpu.make_async_remote_copy(src_ref=src, dst_ref=dst, send_sem=send_sem, recv_sem=recv_sem,
                                        device_id=to, device_id_type=MESH)


def _allgather(shards, name):
    n = len(shards)

    def body(*refs):
        ins, outs = refs[:n], refs[n:2 * n]
        send_sems, recv_sems, local_sems = refs[2 * n:]
        x, y, c, chips = _place()
        me = 2 * x + y
        sibling = (x, y, 1 - c)
        slots = [2 * cx + cy for cx, cy in chips]
        local = [pltpu.make_async_copy(ins[a], outs[a].at[me], local_sems.at[a]) for a in range(n)]
        for cp in local:
            cp.start()
        first = [[_remote(ins[a].at[c], outs[a].at[me, c], send_sems.at[6 * a + k], recv_sems.at[6 * a + k], (*chips[k], c))
                  for k in range(3)] for a in range(n)]
        for a in range(n):
            for cp in first[a]:
                cp.start()
        passed = [[_remote(outs[a].at[slots[k], c], outs[a].at[slots[k], c], send_sems.at[6 * a + 3 + k],
                           recv_sems.at[6 * a + 3 + k], sibling) for k in range(3)] for a in range(n)]
        for a in range(n):
            for k in range(3):
                _remote(ins[a].at[c], outs[a].at[slots[k], c], send_sems.at[6 * a + k], recv_sems.at[6 * a + k],
                        (*chips[k], c)).wait_recv()
                passed[a][k].start()
        for a in range(n):
            for k in range(3):
                _remote(ins[a].at[c], outs[a].at[slots[k], 1 - c], send_sems.at[6 * a + 3 + k],
                        recv_sems.at[6 * a + 3 + k], sibling).wait_recv()
        for a in range(n):
            for cp in first[a] + passed[a]:
                cp.wait_send()
        for cp in local:
            cp.wait()

    return pl.pallas_call(
        body, name=name,
        in_specs=[HBM_SPEC] * n, out_specs=[HBM_SPEC] * n,
        out_shape=[jax.ShapeDtypeStruct((N_CHIPS,) + s.shape, s.dtype) for s in shards],
        scratch_shapes=[pltpu.SemaphoreType.DMA((6 * n,)), pltpu.SemaphoreType.DMA((6 * n,)), pltpu.SemaphoreType.DMA((n,))],
    )(*shards)


def _sibling_exchange(parts, small, name):
    n = len(parts)
    flips = [(fx, fy, fc) for fx in range(2) for fy in range(2) for fc in range(2)][1:]

    def body(*refs):
        ins, small_ref = refs[:n], refs[n]
        outs, gather_ref = refs[n + 1:2 * n + 1], refs[2 * n + 1]
        send_sems, recv_sems, local_sem = refs[2 * n + 2:]
        x, y, c, _ = _place()
        me = 4 * x + 2 * y + c
        sibling = (x, y, 1 - c)
        own = pltpu.make_async_copy(small_ref, gather_ref.at[me], local_sem)
        own.start()
        peers = [((1 - x) if fx else x, (1 - y) if fy else y, (1 - c) if fc else c) for fx, fy, fc in flips]
        copies = [_remote(ins[a].at[1 - c], outs[a], send_sems.at[a], recv_sems.at[a], sibling) for a in range(n)]
        copies += [_remote(small_ref, gather_ref.at[me], send_sems.at[n + k], recv_sems.at[n + k], peer)
                   for k, peer in enumerate(peers)]
        for cp in copies:
            cp.start()
        for a in range(n):
            _remote(ins[a].at[1 - c], outs[a], send_sems.at[a], recv_sems.at[a], sibling).wait_recv()
        for k, (px, py, pc) in enumerate(peers):
            _remote(small_ref, gather_ref.at[4 * px + 2 * py + pc], send_sems.at[n + k], recv_sems.at[n + k],
                    (px, py, pc)).wait_recv()
        for cp in copies:
            cp.wait_send()
        own.wait()

    return pl.pallas_call(
        body, name=name,
        in_specs=[HBM_SPEC] * (n + 1), out_specs=[HBM_SPEC] * (n + 1),
        out_shape=[jax.ShapeDtypeStruct(p.shape[1:], p.dtype) for p in parts]
        + [jax.ShapeDtypeStruct((2 * N_CHIPS,) + small.shape, small.dtype)],
        scratch_shapes=[pltpu.SemaphoreType.DMA((n + 7,)), pltpu.SemaphoreType.DMA((n + 7,)), pltpu.SemaphoreType.DMA],
    )(*parts, small)


def _scatter(sums, name):
    n = len(sums)

    def body(*refs):
        ins, outs = refs[:n], refs[n:2 * n]
        send_sems, recv_sems, local_sems = refs[2 * n:]
        x, y, c, chips = _place()
        me = 2 * x + y
        slots = [2 * cx + cy for cx, cy in chips]
        local = [pltpu.make_async_copy(ins[a].at[me], outs[a].at[me], local_sems.at[a]) for a in range(n)]
        for cp in local:
            cp.start()
        copies = [_remote(ins[a].at[slots[k]], outs[a].at[me], send_sems.at[3 * a + k], recv_sems.at[3 * a + k], (*chips[k], c))
                  for a in range(n) for k in range(3)]
        for cp in copies:
            cp.start()
        for a in range(n):
            for k in range(3):
                _remote(ins[a].at[slots[k]], outs[a].at[slots[k]], send_sems.at[3 * a + k], recv_sems.at[3 * a + k],
                        (*chips[k], c)).wait_recv()
        for cp in copies:
            cp.wait_send()
        for cp in local:
            cp.wait()

    return pl.pallas_call(
        body, name=name,
        in_specs=[HBM_SPEC] * n, out_specs=[HBM_SPEC] * n,
        out_shape=[jax.ShapeDtypeStruct(s.shape, s.dtype) for s in sums],
        scratch_shapes=[pltpu.SemaphoreType.DMA((3 * n,)), pltpu.SemaphoreType.DMA((3 * n,)), pltpu.SemaphoreType.DMA((n,))],
    )(*sums)


def _sibling_share(halves, name):
    n = len(halves)

    def body(*refs):
        ins, outs = refs[:n], refs[n:2 * n]
        send_sems, recv_sems, local_sems = refs[2 * n:]
        x, y, c, _ = _place()
        sibling = (x, y, 1 - c)
        local = [pltpu.make_async_copy(ins[a], outs[a].at[c], local_sems.at[a]) for a in range(n)]
        for cp in local:
            cp.start()
        copies = [_remote(ins[a], outs[a].at[c], send_sems.at[a], recv_sems.at[a], sibling) for a in range(n)]
        for cp in copies:
            cp.start()
        for a in range(n):
            _remote(ins[a], outs[a].at[1 - c], send_sems.at[a], recv_sems.at[a], sibling).wait_recv()
        for cp in copies:
            cp.wait_send()
        for cp in local:
            cp.wait()

    return pl.pallas_call(
        body, name=name,
        in_specs=[HBM_SPEC] * n, out_specs=[HBM_SPEC] * n,
        out_shape=[jax.ShapeDtypeStruct((2,) + h.shape, h.dtype) for h in halves],
        scratch_shapes=[pltpu.SemaphoreType.DMA((n,)), pltpu.SemaphoreType.DMA((n,)), pltpu.SemaphoreType.DMA((n,))],
    )(*halves)


def _pair_sum(core, part, received, name):
    _, _, rh, cols = part.shape

    def body(core_ref, p_ref, r_ref, o_ref):
        o_ref[...] = (p_ref[0].astype(F32) + r_ref[...].astype(F32)).astype(BF16)

    return pl.pallas_call(
        body, name=name,
        grid_spec=pltpu.PrefetchScalarGridSpec(
            num_scalar_prefetch=1, grid=(N_CHIPS,),
            in_specs=[pl.BlockSpec((1, 1, rh, cols), lambda j, core_ref: (core_ref[0], j, 0, 0)),
                      pl.BlockSpec((1, rh, cols), lambda j, core_ref: (j, 0, 0))],
            out_specs=pl.BlockSpec((1, rh, cols), lambda j, core_ref: (j, 0, 0))),
        out_shape=jax.ShapeDtypeStruct((N_CHIPS, rh, cols), BF16),
        compiler_params=_params(("parallel",), 32),
    )(core, part, received)


def _sum_leading(stack, steps, name):
    k, rows, cols = stack.shape
    tile = rows // steps

    def body(s_ref, o_ref):
        total = s_ref[0].astype(F32)
        for d in range(1, k):
            total = total + s_ref[d].astype(F32)
        o_ref[...] = total

    return pl.pallas_call(
        body, name=name, grid=(steps,),
        in_specs=[pl.BlockSpec((k, tile, cols), lambda i: (0, i, 0))],
        out_specs=pl.BlockSpec((tile, cols), lambda i: (i, 0)),
        out_shape=jax.ShapeDtypeStruct((rows, cols), F32),
        compiler_params=_params(("parallel",), 32),
    )(stack)


def _adamw(w, g, row0, m, v, tile, name):
    rows, cols = w.shape
    first = row0 // tile
    assert rows % tile == 0 and row0 % tile == 0
    bc1 = 1.0 - ADAM_B1 ** ADAM_STEP
    bc2 = 1.0 - ADAM_B2 ** ADAM_STEP

    def body(w_ref, g_ref, m_ref, v_ref, go_ref, d_ref, mo_ref, vo_ref):
        g = g_ref[...]
        m_new = ADAM_B1 * m_ref[...] + (1.0 - ADAM_B1) * g
        v_new = ADAM_B2 * v_ref[...] + (1.0 - ADAM_B2) * (g * g)
        go_ref[...] = g
        d_ref[...] = -ADAM_LR * ((m_new / bc1) / (jnp.sqrt(v_new / bc2) + ADAM_EPS) + ADAM_WD * w_ref[...])
        mo_ref[...] = m_new
        vo_ref[...] = v_new

    spec = pl.BlockSpec((tile, cols), lambda i: (i, 0))
    g_spec = pl.BlockSpec((tile, cols), lambda i: (first + i, 0))
    return pl.pallas_call(
        body, name=name, grid=(rows // tile,),
        in_specs=[spec, g_spec, spec, spec], out_specs=[spec] * 4,
        out_shape=[jax.ShapeDtypeStruct((rows, cols), F32)] * 4,
        compiler_params=_params(("parallel",), 32),
    )(w, g, m, v)


SMALL = ("ffn1_norm", "mix_norm", "ffn2_norm", "final_norm", "pool_scale", "sink_logits", "pool_w")


def _pack_small(d):
    sink = jnp.pad(d["sink_logits"].reshape(1, N_HEADS), ((0, 0), (0, LANES - N_HEADS)))
    rows = [d[n].reshape(-1, LANES) for n in SMALL[:5]] + [sink, d["pool_w"].reshape(-1, LANES)]
    packed = jnp.concatenate(rows, axis=0)
    return jnp.pad(packed, ((0, SMALL_ROWS - packed.shape[0]), (0, 0)))


def _unpack_small(packed, like):
    out, row = {}, 0
    for n in SMALL:
        size = LANES if n == "sink_logits" else math.prod(like[n].shape)
        chunk = packed[row:row + size // LANES].reshape(-1)
        out[n] = (chunk[:N_HEADS] if n == "sink_logits" else chunk).reshape(like[n].shape)
        row += size // LANES
    return out


def kernel(x, ffn1_norm, ffn1_w_gate, ffn1_w_up, ffn1_w_down, mix_norm, w_in, sink_logits, pool_w, pool_scale, w_out, ffn2_norm, ffn2_w_gate, ffn2_w_up, ffn2_w_down, final_norm, loss_target, m_ffn1_norm, m_ffn1_w_gate, m_ffn1_w_up, m_ffn1_w_down, m_mix_norm, m_w_in, m_sink_logits, m_pool_w, m_pool_scale, m_w_out, m_ffn2_norm, m_ffn2_w_gate, m_ffn2_w_up, m_ffn2_w_down, m_final_norm, v_ffn1_norm, v_ffn1_w_gate, v_ffn1_w_up, v_ffn1_w_down, v_mix_norm, v_w_in, v_sink_logits, v_pool_w, v_pool_scale, v_w_out, v_ffn2_norm, v_ffn2_w_gate, v_ffn2_w_up, v_ffn2_w_down, v_final_norm):
    names = ("ffn1_norm", "ffn1_w_gate", "ffn1_w_up", "ffn1_w_down", "mix_norm", "w_in", "sink_logits", "pool_w",
             "pool_scale", "w_out", "ffn2_norm", "ffn2_w_gate", "ffn2_w_up", "ffn2_w_down", "final_norm")
    weights = dict(zip(names, (ffn1_norm, ffn1_w_gate, ffn1_w_up, ffn1_w_down, mix_norm, w_in, sink_logits, pool_w,
                               pool_scale, w_out, ffn2_norm, ffn2_w_gate, ffn2_w_up, ffn2_w_down, final_norm)))
    mom1 = dict(zip(names, (m_ffn1_norm, m_ffn1_w_gate, m_ffn1_w_up, m_ffn1_w_down, m_mix_norm, m_w_in, m_sink_logits,
                            m_pool_w, m_pool_scale, m_w_out, m_ffn2_norm, m_ffn2_w_gate, m_ffn2_w_up, m_ffn2_w_down,
                            m_final_norm)))
    mom2 = dict(zip(names, (v_ffn1_norm, v_ffn1_w_gate, v_ffn1_w_up, v_ffn1_w_down, v_mix_norm, v_w_in, v_sink_logits,
                            v_pool_w, v_pool_scale, v_w_out, v_ffn2_norm, v_ffn2_w_gate, v_ffn2_w_up, v_ffn2_w_down,
                            v_final_norm)))
    core = lax.axis_index("c").astype(jnp.int32).reshape(1)

    def rows_of(t, n):
        return jnp.swapaxes(t[n][0], 0, 1) if n in TRANSPOSED else t[n][0]

    def group(t, members):
        both = jnp.concatenate([rows_of(t, n) for n in members], axis=0).astype(BF16)
        return both.reshape(2, both.shape[0] // 2, D_MODEL)

    gathered = _allgather([group(weights, members) for members in GROUPS], "weight_allgather")
    ffn1, mix, ffn2 = (g.reshape(N_CHIPS, -1, D_MODEL) for g in gathered)

    small_w = {"ffn1_norm": ffn1_norm, "mix_norm": mix_norm, "ffn2_norm": ffn2_norm,
               "final_norm": final_norm.reshape(1, D_MODEL), "pool_scale": pool_scale, "sink_logits": sink_logits,
               "pool_w": pool_w[0]}
    loss_lanes, grad_x, group_g, small_g = _local_step(x[0], loss_target[0], ffn1, mix, ffn2, small_w)

    tags = ("ffn1", "mix", "ffn2")
    received = _sibling_exchange(list(group_g), _pack_small(small_g), "grad_sibling_exchange")
    chip_sums = [_pair_sum(core, g, r, "grad_pair_sum_" + t) for t, g, r in zip(tags, group_g, received[:-1])]
    small_sum = _sum_leading(received[-1], 1, "small_grad_sum")
    stacks = _scatter(chip_sums, "grad_scatter")
    reduced = [_sum_leading(s, 2, "grad_chip_sum_" + t) for t, s in zip(tags, stacks)]
    both = _sibling_share(reduced, "grad_sibling_share")
    group_grads = [b.reshape(-1, D_MODEL) for b in both]

    out_g, out_d, out_m, out_v = {}, {}, {}, {}
    for members, g in zip(GROUPS, group_grads):
        row0 = 0
        for n in members:
            w = rows_of(weights, n)
            tile = FF_CHUNK // 4 if w.shape[0] == FF_CHUNK else math.gcd(IN_ROWS, OUT_ROWS)
            outs = _adamw(w, g, row0, rows_of(mom1, n), rows_of(mom2, n), tile, "adamw_" + n)
            row0 += w.shape[0]
            for dst, t in zip((out_g, out_d, out_m, out_v), outs):
                dst[n] = (jnp.swapaxes(t, 0, 1) if n in TRANSPOSED else t).reshape(weights[n].shape)
    small_outs = _adamw(_pack_small(weights), small_sum, 0, _pack_small(mom1), _pack_small(mom2), SMALL_ROWS, "adamw_small")
    for dst, packed in zip((out_g, out_d, out_m, out_v), small_outs):
        dst.update(_unpack_small(packed, weights))

    loss = lax.psum(jnp.sum(loss_lanes), ("x", "y", "c"))
    return (loss, grad_x.reshape(x.shape), *[out_g[n] for n in names], *[out_d[n] for n in names],
            *[out_m[n] for n in names], *[out_v[n] for n in names])
```

```python
import functools
import math

import jax
import jax.numpy as jnp
from jax import lax
from jax.experimental import pallas as pl
from jax.experimental.pallas import tpu as pltpu

F32, BF16 = jnp.float32, jnp.bfloat16
MESH = pl.DeviceIdType.MESH

D_MODEL = 1024
D_FF = 2816
N_CHIPS = 4
FF_CHUNK = D_FF // N_CHIPS
HEAD_DIM = 64
N_HEADS = 8
N_KV = 2
Q_PER_KV = N_HEADS // N_KV
KV_WIDTH = N_KV * HEAD_DIM
ATTN_WIDTH = N_HEADS * HEAD_DIM
POOL_WINDOWS = (2, 4, 8, 16)
N_POOL = len(POOL_WINDOWS)
POOL_GROUP = 128
POOL_WIDTH = N_POOL * POOL_GROUP
IN_WIDTH = ATTN_WIDTH + 2 * KV_WIDTH + POOL_WIDTH
WINDOW = 128
BLOCK = 128
BAND = 3 * BLOCK
ROPE_THETA = 500000.0
ROTARY_DIM = HEAD_DIM // 4
EPS = 1e-6
LANES = 128
Q_PAD = N_HEADS * LANES
U_PAD = Q_PAD + 2 * KV_WIDTH + POOL_WIDTH
SCALE = HEAD_DIM ** -0.5
NEG = -1e30

ADAM_LR, ADAM_B1, ADAM_B2, ADAM_EPS, ADAM_WD, ADAM_STEP = 0.001, 0.9, 0.999, 1e-08, 0.01, 10

V7X_VMEM_BYTES = 64 * 1024 * 1024
TOK_TILE = 512
BWD_TILE = 256
SMALL_ROWS = 552


def _params(sem, vmem_mb):
    return pltpu.CompilerParams(dimension_semantics=sem, vmem_limit_bytes=vmem_mb * 1024 * 1024)


def _dot(a, b):
    return lax.dot_general(a, b, (((1,), (0,)), ((), ())), preferred_element_type=F32)


def _dot_nt(a, b):
    return lax.dot_general(a, b, (((1,), (1,)), ((), ())), preferred_element_type=F32)


def _dot_tn(a, b):
    return lax.dot_general(a, b, (((0,), (0,)), ((), ())), preferred_element_type=F32)


def _rms_stats(h):
    r = lax.rsqrt(jnp.mean(h * h, axis=-1, keepdims=True) + EPS)
    return r, h * r


def _rms_bwd(dn, g, r, xh):
    gd = dn * g
    dh = r * (gd - xh * jnp.mean(gd * xh, axis=-1, keepdims=True))
    return dh, jnp.sum(dn * xh, axis=0, keepdims=True)


def _rope(x, c, s1, s2):
    return x * c + pltpu.roll(x, LANES - ROTARY_DIM // 2, 1) * s1 + pltpu.roll(x, ROTARY_DIM // 2, 1) * s2


def _rope_bwd(d, c, s1, s2):
    return d * c + pltpu.roll(d * s1, ROTARY_DIM // 2, 1) + pltpu.roll(d * s2, LANES - ROTARY_DIM // 2, 1)


def _sum_chunks(ref):
    total = ref[0].astype(F32)
    for j in range(1, N_CHIPS):
        total = total + ref[j].astype(F32)
    return total


def _chunk_rows(tile):
    return pl.BlockSpec((N_CHIPS, tile, D_MODEL), lambda i, *_: (0, i, 0))


def _full(shape):
    nd = len(shape)
    return pl.BlockSpec(shape, lambda *_: (0,) * nd)


def _rows(tile, cols):
    return pl.BlockSpec((tile, cols), lambda i, *_: (i, 0))


def _ffn_fwd(h, gain, group, name):
    S = h.shape[0]
    tile = min(TOK_TILE, S)
    nt = S // tile

    def body(h_ref, g_ref, wg_ref, wu_ref, wd_ref, ho_ref, n_ref, gate_ref, up_ref, acc):
        j = pl.program_id(1)

        @pl.when(j == 0)
        def _():
            _, xh = _rms_stats(h_ref[...])
            n_ref[...] = (xh * g_ref[...]).astype(BF16)
            acc[...] = jnp.zeros_like(acc)

        n = n_ref[...]
        gate = _dot_nt(n, wg_ref[0])
        up = _dot_nt(n, wu_ref[0])
        gate_ref[0] = gate.astype(BF16)
        up_ref[0] = up.astype(BF16)
        act = (gate * jax.nn.sigmoid(gate) * up).astype(BF16)
        acc[...] += _dot(act, wd_ref[0])

        @pl.when(j == N_CHIPS - 1)
        def _():
            ho_ref[...] = h_ref[...] + 0.5 * acc[...]

    tok = pl.BlockSpec((tile, D_MODEL), lambda i, j: (i, 0))
    hid = pl.BlockSpec((1, tile, FF_CHUNK), lambda i, j: (j, i, 0))
    return pl.pallas_call(
        body, name=name, grid=(nt, N_CHIPS),
        in_specs=[tok, pl.BlockSpec((1, D_MODEL), lambda i, j: (0, 0))]
        + [pl.BlockSpec((1, FF_CHUNK, D_MODEL), functools.partial(lambda i, j, part: (j, part, 0), part=part))
           for part in range(3)],
        out_specs=[tok, tok, hid, hid],
        out_shape=[jax.ShapeDtypeStruct((S, D_MODEL), F32), jax.ShapeDtypeStruct((S, D_MODEL), BF16),
                   jax.ShapeDtypeStruct((N_CHIPS, S, FF_CHUNK), BF16), jax.ShapeDtypeStruct((N_CHIPS, S, FF_CHUNK), BF16)],
        scratch_shapes=[pltpu.VMEM((tile, D_MODEL), F32)],
        compiler_params=_params(("parallel", "arbitrary"), 40),
    )(h, gain, group, group, group)


def _ffn_bwd(d_out, n, gate, up, group, name):
    S = n.shape[0]
    tile = min(TOK_TILE, S)
    nt = S // tile
    half_rows = 3 * FF_CHUNK // 2
    cut = FF_CHUNK // 2

    def body(do_ref, n_ref, gate_ref, up_ref, wg_ref, wu_ref, wd_ref, dn_ref, dw_ref, acc_g, acc_u, acc_d):
        j, i = pl.program_id(0), pl.program_id(1)

        @pl.when(i == 0)
        def _():
            acc_g[...] = jnp.zeros_like(acc_g)
            acc_u[...] = jnp.zeros_like(acc_u)
            acc_d[...] = jnp.zeros_like(acc_d)

        do = do_ref[...]
        nn = n_ref[...]
        g = gate_ref[0].astype(F32)
        u = up_ref[0].astype(F32)
        d_act = _dot_nt(do, wd_ref[0])
        sig = jax.nn.sigmoid(g)
        silu = g * sig
        d_up = (d_act * silu).astype(BF16)
        d_gate = (d_act * u * (sig * (1.0 + g * (1.0 - sig)))).astype(BF16)
        act = (silu * u).astype(BF16)
        dn_ref[0] = (_dot(d_gate, wg_ref[0]) + _dot(d_up, wu_ref[0])).astype(BF16)
        acc_g[...] += _dot_tn(d_gate, nn)
        acc_u[...] += _dot_tn(d_up, nn)
        acc_d[...] += _dot_tn(act, do)

        @pl.when(i == nt - 1)
        def _():
            dw_ref[0, 0, :FF_CHUNK, :] = acc_g[...].astype(BF16)
            dw_ref[0, 0, FF_CHUNK:, :] = acc_u[:cut, :].astype(BF16)
            dw_ref[1, 0, :cut, :] = acc_u[cut:, :].astype(BF16)
            dw_ref[1, 0, cut:, :] = acc_d[...].astype(BF16)

    tok = pl.BlockSpec((tile, D_MODEL), lambda j, i: (i, 0))
    hid = pl.BlockSpec((1, tile, FF_CHUNK), lambda j, i: (j, i, 0))
    return pl.pallas_call(
        body, name=name, grid=(N_CHIPS, nt),
        in_specs=[tok, tok, hid, hid]
        + [pl.BlockSpec((1, FF_CHUNK, D_MODEL), functools.partial(lambda j, i, part: (j, part, 0), part=part))
           for part in range(3)],
        out_specs=[pl.BlockSpec((1, tile, D_MODEL), lambda j, i: (j, i, 0)),
                   pl.BlockSpec((2, 1, half_rows, D_MODEL), lambda j, i: (0, j, 0, 0))],
        out_shape=[jax.ShapeDtypeStruct((N_CHIPS, S, D_MODEL), BF16),
                   jax.ShapeDtypeStruct((2, N_CHIPS, half_rows, D_MODEL), BF16)],
        scratch_shapes=[pltpu.VMEM((FF_CHUNK, D_MODEL), F32)] * 3,
        compiler_params=_params(("arbitrary", "arbitrary"), 56),
    )(d_out, n, gate, up, group, group, group)


def _mix_in(h, gain, w_in, rc, rs1, rs2, name):
    S = h.shape[0]
    tile = min(TOK_TILE, S)

    def body(h_ref, g_ref, w_ref, c_ref, s1_ref, s2_ref, n_ref, q_ref, k_ref, v_ref, pc_ref):
        _, xh = _rms_stats(h_ref[...])
        n = (xh * g_ref[...]).astype(BF16)
        n_ref[...] = n
        u = _dot_nt(n, w_ref[...])
        c, s1, s2 = c_ref[...], s1_ref[...], s2_ref[...]
        for hd in range(N_HEADS):
            q_ref[:, hd * LANES:(hd + 1) * LANES] = _rope(u[:, hd * LANES:(hd + 1) * LANES], c, s1, s2).astype(BF16)
        k_ref[...] = _rope(u[:, Q_PAD:Q_PAD + KV_WIDTH], c, s1, s2).astype(BF16)
        v_ref[...] = u[:, Q_PAD + KV_WIDTH:Q_PAD + 2 * KV_WIDTH].astype(BF16)
        pc_ref[...] = u[:, Q_PAD + 2 * KV_WIDTH:]

    return pl.pallas_call(
        body, name=name, grid=(S // tile,),
        in_specs=[_rows(tile, D_MODEL), _full((1, D_MODEL)), _full((U_PAD, D_MODEL)),
                  _rows(tile, LANES), _rows(tile, LANES), _rows(tile, LANES)],
        out_specs=[_rows(tile, D_MODEL), _rows(tile, Q_PAD), _rows(tile, KV_WIDTH), _rows(tile, KV_WIDTH),
                   _rows(tile, POOL_WIDTH)],
        out_shape=[jax.ShapeDtypeStruct((S, D_MODEL), BF16), jax.ShapeDtypeStruct((S, Q_PAD), BF16),
                   jax.ShapeDtypeStruct((S, KV_WIDTH), BF16), jax.ShapeDtypeStruct((S, KV_WIDTH), BF16),
                   jax.ShapeDtypeStruct((S, POOL_WIDTH), F32)],
        compiler_params=_params(("parallel",), 40),
    )(h, gain, w_in, rc, rs1, rs2)


def _band_start(i, S):
    return pl.multiple_of(jnp.clip((i - 1) * BLOCK, 0, S - BAND), BLOCK)


def _probs(qh, kb, off, sink_h):
    s = _dot_nt(qh, kb) * SCALE
    r = lax.broadcasted_iota(jnp.int32, (BLOCK, 1), 0)
    c = lax.broadcasted_iota(jnp.int32, (1, BAND), 1)
    s = jnp.where(jnp.abs(off + r - c) <= WINDOW, s, NEG)
    m = jnp.maximum(jnp.max(s, axis=-1, keepdims=True), sink_h)
    p = jnp.exp(s - m)
    es = jnp.exp(sink_h - m)
    inv = 1.0 / (jnp.sum(p, axis=-1, keepdims=True) + es)
    return p * inv, es * inv


def _pool_matrix(t0, start, S, w):
    r = lax.broadcasted_iota(jnp.int32, (BLOCK, 1), 0) + t0
    c = lax.broadcasted_iota(jnp.int32, (1, BAND), 1) + start
    half = w // 2

    def window(lo, hi):
        a = jnp.maximum(lo, 0)
        b = jnp.minimum(hi + 1, S)
        return jnp.where((c >= a) & (c < b), 1.0 / (b - a).astype(F32), 0.0)

    return (0.5 * (window(r - half, r + half - 1) + window(r - half + 1, r + half))).astype(BF16)


def _mix_core_fwd(q, k, v, pc, sink, pool_w, pool_scale, name):
    S = q.shape[0]

    def body(sink_ref, q_ref, k_ref, v_ref, pc_ref, pw_ref, ps_ref, a_ref, p_ref):
        i = pl.program_id(0)
        start = _band_start(i, S)
        band = pl.ds(start, BAND)
        off = i * BLOCK - start
        kb, vb = k_ref[band, :], v_ref[band, :]
        for hd in range(N_HEADS):
            prob, _ = _probs(q_ref[:, hd * LANES:(hd + 1) * LANES], kb, off, sink_ref[0, hd])
            a_ref[:, hd * LANES:(hd + 1) * LANES] = _dot(prob.astype(BF16), vb).astype(BF16)
        centre = pl.ds(pl.multiple_of(i * BLOCK, BLOCK), BLOCK)
        for g, w in enumerate(POOL_WINDOWS):
            cols = slice(g * POOL_GROUP, (g + 1) * POOL_GROUP)
            mean = _dot(_pool_matrix(i * BLOCK, start, S, w), pc_ref[band, cols].astype(BF16))
            dev = (mean - pc_ref[centre, cols]).astype(BF16)
            p_ref[:, cols] = (_dot(dev, pw_ref[g]) * ps_ref[:, cols]).astype(BF16)

    return pl.pallas_call(
        body, name=name, grid=(S // BLOCK,),
        in_specs=[pl.BlockSpec(memory_space=pltpu.SMEM), _rows(BLOCK, Q_PAD), _full((S, KV_WIDTH)), _full((S, KV_WIDTH)),
                  _full((S, POOL_WIDTH)), _full((N_POOL, POOL_GROUP, POOL_GROUP)), _full((1, POOL_WIDTH))],
        out_specs=[_rows(BLOCK, Q_PAD), _rows(BLOCK, POOL_WIDTH)],
        out_shape=[jax.ShapeDtypeStruct((S, Q_PAD), BF16), jax.ShapeDtypeStruct((S, POOL_WIDTH), BF16)],
        compiler_params=_params(("parallel",), 40),
    )(sink, q, k, v, pc, pool_w, pool_scale)


def _mix_core_bwd(q, k, v, pc, da, dp, sink, pool_w, pool_scale, rc, rs1, rs2, name):
    S = q.shape[0]
    nb = S // BLOCK

    def body(sink_ref, q_ref, k_ref, v_ref, pc_ref, da_ref, dp_ref, pw_ref, ps_ref, c_ref, s1_ref, s2_ref,
             dq_ref, dk_ref, dv_ref, dpc_ref, dsink_ref, dpw_ref, dps_ref):
        i = pl.program_id(0)

        @pl.when(i == 0)
        def _():
            dk_ref[...] = jnp.zeros_like(dk_ref)
            dv_ref[...] = jnp.zeros_like(dv_ref)
            dpc_ref[...] = jnp.zeros_like(dpc_ref)
            dsink_ref[...] = jnp.zeros_like(dsink_ref)
            dpw_ref[...] = jnp.zeros_like(dpw_ref)
            dps_ref[...] = jnp.zeros_like(dps_ref)

        start = _band_start(i, S)
        band = pl.ds(start, BAND)
        off = i * BLOCK - start
        kb, vb = k_ref[band, :], v_ref[band, :]
        c, s1, s2 = c_ref[...], s1_ref[...], s2_ref[...]
        lane = lax.broadcasted_iota(jnp.int32, (1, LANES), 1)
        dk_blk = jnp.zeros((BAND, KV_WIDTH), F32)
        dv_blk = jnp.zeros((BAND, KV_WIDTH), F32)
        dsink = jnp.zeros((1, LANES), F32)
        for hd in range(N_HEADS):
            cols = slice(hd * LANES, (hd + 1) * LANES)
            qh = q_ref[:, cols]
            dah = da_ref[:, cols]
            prob, p_sink = _probs(qh, kb, off, sink_ref[0, hd])
            d_prob = _dot_nt(dah, vb)
            delta = jnp.sum(prob * d_prob, axis=-1, keepdims=True)
            d_s = (prob * (d_prob - delta) * SCALE).astype(BF16)
            dq_ref[:, cols] = _rope_bwd(_dot(d_s, kb), c, s1, s2).astype(BF16)
            dk_blk += _dot_tn(d_s, qh)
            dv_blk += _dot_tn(prob.astype(BF16), dah)
            dsink += jnp.where(lane == hd, -jnp.sum(p_sink * delta, axis=0, keepdims=True), 0.0)
        dk_ref[band, :] += dk_blk
        dv_ref[band, :] += dv_blk
        dsink_ref[...] += dsink

        centre = pl.ds(pl.multiple_of(i * BLOCK, BLOCK), BLOCK)
        for g, w in enumerate(POOL_WINDOWS):
            cols = slice(g * POOL_GROUP, (g + 1) * POOL_GROUP)
            pm = _pool_matrix(i * BLOCK, start, S, w)
            dev = (_dot(pm, pc_ref[band, cols].astype(BF16)) - pc_ref[centre, cols]).astype(BF16)
            dy = dp_ref[:, cols].astype(F32)
            z = _dot(dev, pw_ref[g])
            dps_ref[:, cols] += jnp.sum(dy * z, axis=0, keepdims=True)
            dz = (dy * ps_ref[:, cols]).astype(BF16)
            dpw_ref[g] += _dot_tn(dev, dz)
            d_dev = _dot_nt(dz, pw_ref[g])
            dpc_ref[band, cols] += _dot_tn(pm, d_dev.astype(BF16))
            dpc_ref[centre, cols] -= d_dev

    return pl.pallas_call(
        body, name=name, grid=(nb,),
        in_specs=[pl.BlockSpec(memory_space=pltpu.SMEM), _rows(BLOCK, Q_PAD), _full((S, KV_WIDTH)), _full((S, KV_WIDTH)),
                  _full((S, POOL_WIDTH)), _rows(BLOCK, Q_PAD), _rows(BLOCK, POOL_WIDTH),
                  _full((N_POOL, POOL_GROUP, POOL_GROUP)), _full((1, POOL_WIDTH)),
                  _rows(BLOCK, LANES), _rows(BLOCK, LANES), _rows(BLOCK, LANES)],
        out_specs=[_rows(BLOCK, Q_PAD), _full((S, KV_WIDTH)), _full((S, KV_WIDTH)), _full((S, POOL_WIDTH)),
                   _full((1, LANES)), _full((N_POOL, POOL_GROUP, POOL_GROUP)), _full((1, POOL_WIDTH))],
        out_shape=[jax.ShapeDtypeStruct((S, Q_PAD), BF16), jax.ShapeDtypeStruct((S, KV_WIDTH), F32),
                   jax.ShapeDtypeStruct((S, KV_WIDTH), F32), jax.ShapeDtypeStruct((S, POOL_WIDTH), F32),
                   jax.ShapeDtypeStruct((1, LANES), F32), jax.ShapeDtypeStruct((N_POOL, POOL_GROUP, POOL_GROUP), F32),
                   jax.ShapeDtypeStruct((1, POOL_WIDTH), F32)],
        compiler_params=_params(("arbitrary",), 56),
    )(sink, q, k, v, pc, da, dp, pool_w, pool_scale, rc, rs1, rs2)


def _mix_out(h, a, p, wa, wp, name):
    S = h.shape[0]
    tile = min(TOK_TILE, S)

    def body(h_ref, a_ref, p_ref, wa_ref, wp_ref, o_ref):
        o_ref[...] = h_ref[...] + _dot(a_ref[...], wa_ref[...]) + _dot(p_ref[...], wp_ref[...])

    return pl.pallas_call(
        body, name=name, grid=(S // tile,),
        in_specs=[_rows(tile, D_MODEL), _rows(tile, Q_PAD), _rows(tile, POOL_WIDTH),
                  _full((Q_PAD, D_MODEL)), _full((POOL_WIDTH, D_MODEL))],
        out_specs=_rows(tile, D_MODEL),
        out_shape=jax.ShapeDtypeStruct((S, D_MODEL), F32),
        compiler_params=_params(("parallel",), 40),
    )(h, a, p, wa, wp)


def _loss_head(h, target, gain, name):
    S = h.shape[0]
    tile = min(TOK_TILE, S)

    def body(h_ref, t_ref, g_ref, dh_ref, dhalf_ref, loss_ref, dg_ref):
        @pl.when(pl.program_id(0) == 0)
        def _():
            loss_ref[...] = jnp.zeros_like(loss_ref)
            dg_ref[...] = jnp.zeros_like(dg_ref)

        g = g_ref[...]
        r, xh = _rms_stats(h_ref[...])
        err = xh * g - t_ref[...]
        loss_ref[...] += (0.5 / D_MODEL) * jnp.sum(err * err, axis=0, keepdims=True)
        dh, dg = _rms_bwd(err * (1.0 / D_MODEL), g, r, xh)
        dg_ref[...] += dg
        dh_ref[...] = dh
        dhalf_ref[...] = (0.5 * dh).astype(BF16)

    return pl.pallas_call(
        body, name=name, grid=(S // tile,),
        in_specs=[_rows(tile, D_MODEL), _rows(tile, D_MODEL), _full((1, D_MODEL))],
        out_specs=[_rows(tile, D_MODEL), _rows(tile, D_MODEL), _full((1, D_MODEL)), _full((1, D_MODEL))],
        out_shape=[jax.ShapeDtypeStruct((S, D_MODEL), F32), jax.ShapeDtypeStruct((S, D_MODEL), BF16),
                   jax.ShapeDtypeStruct((1, D_MODEL), F32), jax.ShapeDtypeStruct((1, D_MODEL), F32)],
        compiler_params=_params(("arbitrary",), 40),
    )(h, target, gain)


def _mix_out_bwd(dh_out, dn, h, gain, a, p, wa, wp, name):
    S = h.shape[0]
    tile = min(TOK_TILE, S)

    def body(do_ref, dn_ref, h_ref, g_ref, a_ref, p_ref, wa_ref, wp_ref, dh_ref, da_ref, dp_ref, dwa_ref, dwp_ref, dg_ref):
        @pl.when(pl.program_id(0) == 0)
        def _():
            dwa_ref[...] = jnp.zeros_like(dwa_ref)
            dwp_ref[...] = jnp.zeros_like(dwp_ref)
            dg_ref[...] = jnp.zeros_like(dg_ref)

        r, xh = _rms_stats(h_ref[...])
        dnorm, dg = _rms_bwd(_sum_chunks(dn_ref), g_ref[...], r, xh)
        dh = do_ref[...] + dnorm
        dg_ref[...] += dg
        dh_ref[...] = dh
        dhb = dh.astype(BF16)
        da_ref[...] = _dot_nt(dhb, wa_ref[...]).astype(BF16)
        dp_ref[...] = _dot_nt(dhb, wp_ref[...]).astype(BF16)
        dwa_ref[...] += _dot_tn(a_ref[...], dhb)
        dwp_ref[...] += _dot_tn(p_ref[...], dhb)

    return pl.pallas_call(
        body, name=name, grid=(S // tile,),
        in_specs=[_rows(tile, D_MODEL), _chunk_rows(tile), _rows(tile, D_MODEL), _full((1, D_MODEL)),
                  _rows(tile, Q_PAD), _rows(tile, POOL_WIDTH), _full((Q_PAD, D_MODEL)), _full((POOL_WIDTH, D_MODEL))],
        out_specs=[_rows(tile, D_MODEL), _rows(tile, Q_PAD), _rows(tile, POOL_WIDTH),
                   _full((Q_PAD, D_MODEL)), _full((POOL_WIDTH, D_MODEL)), _full((1, D_MODEL))],
        out_shape=[jax.ShapeDtypeStruct((S, D_MODEL), F32), jax.ShapeDtypeStruct((S, Q_PAD), BF16),
                   jax.ShapeDtypeStruct((S, POOL_WIDTH), BF16), jax.ShapeDtypeStruct((Q_PAD, D_MODEL), F32),
                   jax.ShapeDtypeStruct((POOL_WIDTH, D_MODEL), F32), jax.ShapeDtypeStruct((1, D_MODEL), F32)],
        compiler_params=_params(("arbitrary",), 48),
    )(dh_out, dn, h, gain, a, p, wa, wp)


def _mix_in_bwd(dh_out, h, gain, n, dq, dk, dv, dpc, rc, rs1, rs2, w_in, name):
    S = h.shape[0]
    tile = min(TOK_TILE, S)

    def body(do_ref, h_ref, g_ref, n_ref, dq_ref, dk_ref, dv_ref, dpc_ref, c_ref, s1_ref, s2_ref, w_ref,
             dh_ref, dhalf_ref, dw_ref, dg_ref):
        @pl.when(pl.program_id(0) == 0)
        def _():
            dw_ref[...] = jnp.zeros_like(dw_ref)
            dg_ref[...] = jnp.zeros_like(dg_ref)

        dk = _rope_bwd(dk_ref[...], c_ref[...], s1_ref[...], s2_ref[...]).astype(BF16)
        du = jnp.concatenate([dq_ref[...], dk, dv_ref[...].astype(BF16), dpc_ref[...].astype(BF16)], axis=1)
        dn = _dot(du, w_ref[...])
        dw_ref[...] += _dot_tn(du, n_ref[...])
        r, xh = _rms_stats(h_ref[...])
        dnorm, dg = _rms_bwd(dn, g_ref[...], r, xh)
        dh = do_ref[...] + dnorm
        dg_ref[...] += dg
        dh_ref[...] = dh
        dhalf_ref[...] = (0.5 * dh).astype(BF16)

    return pl.pallas_call(
        body, name=name, grid=(S // tile,),
        in_specs=[_rows(tile, D_MODEL), _rows(tile, D_MODEL), _full((1, D_MODEL)), _rows(tile, D_MODEL),
                  _rows(tile, Q_PAD), _rows(tile, KV_WIDTH), _rows(tile, KV_WIDTH), _rows(tile, POOL_WIDTH),
                  _rows(tile, LANES), _rows(tile, LANES), _rows(tile, LANES), _full((U_PAD, D_MODEL))],
        out_specs=[_rows(tile, D_MODEL), _rows(tile, D_MODEL), _full((U_PAD, D_MODEL)), _full((1, D_MODEL))],
        out_shape=[jax.ShapeDtypeStruct((S, D_MODEL), F32), jax.ShapeDtypeStruct((S, D_MODEL), BF16),
                   jax.ShapeDtypeStruct((U_PAD, D_MODEL), F32), jax.ShapeDtypeStruct((1, D_MODEL), F32)],
        compiler_params=_params(("arbitrary",), 56),
    )(dh_out, h, gain, n, dq, dk, dv, dpc, rc, rs1, rs2, w_in)


def _norm_bwd(dh_out, dn, h, gain, name):
    S = h.shape[0]
    tile = min(TOK_TILE, S)

    def body(do_ref, dn_ref, h_ref, g_ref, dh_ref, dg_ref):
        @pl.when(pl.program_id(0) == 0)
        def _():
            dg_ref[...] = jnp.zeros_like(dg_ref)

        r, xh = _rms_stats(h_ref[...])
        dnorm, dg = _rms_bwd(_sum_chunks(dn_ref), g_ref[...], r, xh)
        dg_ref[...] += dg
        dh_ref[...] = do_ref[...] + dnorm

    return pl.pallas_call(
        body, name=name, grid=(S // tile,),
        in_specs=[_rows(tile, D_MODEL), _chunk_rows(tile), _rows(tile, D_MODEL), _full((1, D_MODEL))],
        out_specs=[_rows(tile, D_MODEL), _full((1, D_MODEL))],
        out_shape=[jax.ShapeDtypeStruct((S, D_MODEL), F32), jax.ShapeDtypeStruct((1, D_MODEL), F32)],
        compiler_params=_params(("arbitrary",), 40),
    )(dh_out, dn, h, gain)


def _rope_tables(S):
    half = ROTARY_DIM // 2
    inv_freq = ROPE_THETA ** (-jnp.arange(0, ROTARY_DIM, 2, dtype=F32) / ROTARY_DIM)
    dim = jnp.arange(LANES) % HEAD_DIM
    ang = jnp.arange(S, dtype=F32)[:, None] * inv_freq[dim % half][None, :]
    lo, hi = (dim < half)[None, :], ((dim >= half) & (dim < ROTARY_DIM))[None, :]
    c = jnp.where(lo | hi, jnp.cos(ang), 1.0)
    s1 = jnp.where(lo, -jnp.sin(ang), 0.0)
    s2 = jnp.where(hi, jnp.sin(ang), 0.0)
    return c, s1, s2


def _pad_heads(w, axis):
    w = jnp.moveaxis(w, axis, 0)
    heads = w.reshape((N_HEADS, HEAD_DIM) + w.shape[1:])
    zero = jnp.zeros_like(heads)
    first = (jnp.arange(N_HEADS) < Q_PER_KV).reshape((N_HEADS, 1) + (1,) * (w.ndim - 1))
    lo = jnp.where(first, heads, zero)
    hi = jnp.where(first, zero, heads)
    padded = jnp.concatenate([lo, hi], axis=1).reshape((Q_PAD,) + w.shape[1:])
    return jnp.moveaxis(padded, 0, axis)


def _unpad_heads(w, axis):
    w = jnp.moveaxis(w, axis, 0)
    groups = w.reshape((N_HEADS, 2, HEAD_DIM) + w.shape[1:])
    first = (jnp.arange(N_HEADS) < Q_PER_KV).reshape((N_HEADS, 1) + (1,) * (w.ndim - 1))
    heads = jnp.where(first, groups[:, 0], groups[:, 1]).reshape((ATTN_WIDTH,) + w.shape[1:])
    return jnp.moveaxis(heads, 0, axis)


IN_ROWS = IN_WIDTH // N_CHIPS
OUT_ROWS = (ATTN_WIDTH + POOL_WIDTH) // N_CHIPS
MIX_ROWS = IN_ROWS + OUT_ROWS
FFN_ROWS = 3 * FF_CHUNK


def _local_step(x, target, ffn1, mix, ffn2, small):
    S = x.shape[0]
    rc, rs1, rs2 = _rope_tables(S)
    w_in_t = mix[:, :IN_ROWS].reshape(IN_WIDTH, D_MODEL)
    w_in_pad = jnp.concatenate([_pad_heads(w_in_t[:ATTN_WIDTH], 0), w_in_t[ATTN_WIDTH:]], axis=0)
    w_out = mix[:, IN_ROWS:].reshape(ATTN_WIDTH + POOL_WIDTH, D_MODEL)
    wa = _pad_heads(w_out[:ATTN_WIDTH], 0)
    wp = w_out[ATTN_WIDTH:]
    pool_w = small["pool_w"].astype(BF16)

    h1, n1, gate1, up1 = _ffn_fwd(x, small["ffn1_norm"], ffn1, "ffn1_fwd")
    n2, q, k, v, pc = _mix_in(h1, small["mix_norm"], w_in_pad, rc, rs1, rs2, "mix_in")
    a, p = _mix_core_fwd(q, k, v, pc, small["sink_logits"], pool_w, small["pool_scale"], "mix_core_fwd")
    h2 = _mix_out(h1, a, p, wa, wp, "mix_out")
    h3, n3, gate2, up2 = _ffn_fwd(h2, small["ffn2_norm"], ffn2, "ffn2_fwd")
    dh3, dhalf3, loss_lanes, d_final = _loss_head(h3, target, small["final_norm"], "loss_head")

    dn3, d_ffn2 = _ffn_bwd(dhalf3, n3, gate2, up2, ffn2, "ffn2_bwd")
    dh2, da, dp, dwa, dwp, d_ffn2_norm = _mix_out_bwd(dh3, dn3, h2, small["ffn2_norm"], a, p, wa, wp, "mix_out_bwd")
    dq, dk, dv, dpc, dsink, dpool_w, dpool_scale = _mix_core_bwd(
        q, k, v, pc, da, dp, small["sink_logits"], pool_w, small["pool_scale"], rc, rs1, rs2, "mix_core_bwd")
    dh1, dhalf1, dw_in_pad, d_mix_norm = _mix_in_bwd(dh2, h1, small["mix_norm"], n2, dq, dk, dv, dpc, rc, rs1, rs2,
                                                     w_in_pad, "mix_in_bwd")
    dn1, d_ffn1 = _ffn_bwd(dhalf1, n1, gate1, up1, ffn1, "ffn1_bwd")
    grad_x, d_ffn1_norm = _norm_bwd(dh1, dn1, x, small["ffn1_norm"], "norm1_bwd")

    dw_in_t = jnp.concatenate([_unpad_heads(dw_in_pad[:Q_PAD], 0), dw_in_pad[Q_PAD:]], axis=0)
    dw_out = jnp.concatenate([_unpad_heads(dwa, 0), dwp], axis=0)
    d_mix = jnp.concatenate([dw_in_t.reshape(N_CHIPS, IN_ROWS, D_MODEL), dw_out.reshape(N_CHIPS, OUT_ROWS, D_MODEL)], axis=1)
    d_mix = jnp.transpose(d_mix.reshape(N_CHIPS, 2, MIX_ROWS // 2, D_MODEL), (1, 0, 2, 3)).astype(BF16)
    small_g = {"ffn1_norm": d_ffn1_norm, "mix_norm": d_mix_norm, "ffn2_norm": d_ffn2_norm, "final_norm": d_final,
               "pool_scale": dpool_scale, "sink_logits": dsink[:, :N_HEADS], "pool_w": dpool_w}
    return loss_lanes, grad_x, (d_ffn1, d_mix, d_ffn2), small_g


HBM_SPEC = pl.BlockSpec(memory_space=pltpu.HBM)
GROUPS = (("ffn1_w_gate", "ffn1_w_up", "ffn1_w_down"), ("w_in", "w_out"), ("ffn2_w_gate", "ffn2_w_up", "ffn2_w_down"))
TRANSPOSED = ("ffn1_w_gate", "ffn1_w_up", "w_in", "ffn2_w_gate", "ffn2_w_up")


def _place():
    x, y, c = lax.axis_index("x"), lax.axis_index("y"), lax.axis_index("c")
    chips = [(1 - x, y), (x, 1 - y), (1 - x, 1 - y)]
    return x, y, c, chips


def _remote(src, dst, send_sem, recv_sem, to):
    return pltpu.make_async_remote_copy(src_ref=src, dst_ref=dst, send_sem=send_sem, recv_sem=recv_sem,
                                        device_id=to, device_id_type=MESH)


def _pack(chip, members, name):
    rows = members[0].shape[0]
    n = len(members)

    def body(chip_ref, *refs):
        out_ref = refs[n]
        for k in range(n):
            @pl.when(pl.program_id(0) == k)
            def _(k=k):
                out_ref[0] = refs[k][...].astype(BF16)

    return pl.pallas_call(
        body, name=name,
        grid_spec=pltpu.PrefetchScalarGridSpec(
            num_scalar_prefetch=1, grid=(n,),
            in_specs=[pl.BlockSpec((rows, D_MODEL), lambda k, chip_ref: (0, 0))] * n,
            out_specs=pl.BlockSpec((1, rows, D_MODEL), lambda k, chip_ref: (chip_ref[0], k, 0))),
        out_shape=jax.ShapeDtypeStruct((N_CHIPS, n * rows, D_MODEL), BF16),
        compiler_params=_params(("arbitrary",), 40),
    )(chip, *members)


def _allgather(bufs, name):
    n = len(bufs)

    def body(*refs):
        outs = refs[n:2 * n]
        send_sems, recv_sems = refs[2 * n:]
        x, y, c, chips = _place()
        me = 2 * x + y
        sibling = (x, y, 1 - c)
        slots = [2 * cx + cy for cx, cy in chips]
        first = [[_remote(outs[a].at[me, c], outs[a].at[me, c], send_sems.at[6 * a + k], recv_sems.at[6 * a + k], (*chips[k], c))
                  for k in range(3)] for a in range(n)]
        for a in range(n):
            for cp in first[a]:
                cp.start()
        passed = [[_remote(outs[a].at[slots[k], c], outs[a].at[slots[k], c], send_sems.at[6 * a + 3 + k],
                           recv_sems.at[6 * a + 3 + k], sibling) for k in range(3)] for a in range(n)]
        for a in range(n):
            for k in range(3):
                _remote(outs[a].at[me, c], outs[a].at[slots[k], c], send_sems.at[6 * a + k], recv_sems.at[6 * a + k],
                        (*chips[k], c)).wait_recv()
                passed[a][k].start()
        for a in range(n):
            for k in range(3):
                _remote(outs[a].at[me, c], outs[a].at[slots[k], 1 - c], send_sems.at[6 * a + 3 + k],
                        recv_sems.at[6 * a + 3 + k], sibling).wait_recv()
        for a in range(n):
            for cp in first[a] + passed[a]:
                cp.wait_send()

    return pl.pallas_call(
        body, name=name,
        in_specs=[HBM_SPEC] * n, out_specs=[HBM_SPEC] * n,
        out_shape=[jax.ShapeDtypeStruct(b.shape, b.dtype) for b in bufs],
        input_output_aliases={a: a for a in range(n)},
        scratch_shapes=[pltpu.SemaphoreType.DMA((6 * n,)), pltpu.SemaphoreType.DMA((6 * n,))],
    )(*bufs)


def _sibling_exchange(parts, small, name):
    n = len(parts)
    flips = [(fx, fy, fc) for fx in range(2) for fy in range(2) for fc in range(2)][1:]

    def body(*refs):
        ins, small_ref = refs[:n], refs[n]
        outs, gather_ref = refs[n + 1:2 * n + 1], refs[2 * n + 1]
        send_sems, recv_sems, local_sem = refs[2 * n + 2:]
        x, y, c, _ = _place()
        me = 4 * x + 2 * y + c
        sibling = (x, y, 1 - c)
        own = pltpu.make_async_copy(small_ref, gather_ref.at[me], local_sem)
        own.start()
        peers = [((1 - x) if fx else x, (1 - y) if fy else y, (1 - c) if fc else c) for fx, fy, fc in flips]
        copies = [_remote(ins[a].at[1 - c], outs[a], send_sems.at[a], recv_sems.at[a], sibling) for a in range(n)]
        copies += [_remote(small_ref, gather_ref.at[me], send_sems.at[n + k], recv_sems.at[n + k], peer)
                   for k, peer in enumerate(peers)]
        for cp in copies:
            cp.start()
        for a in range(n):
            _remote(ins[a].at[1 - c], outs[a], send_sems.at[a], recv_sems.at[a], sibling).wait_recv()
        for k, (px, py, pc) in enumerate(peers):
            _remote(small_ref, gather_ref.at[4 * px + 2 * py + pc], send_sems.at[n + k], recv_sems.at[n + k],
                    (px, py, pc)).wait_recv()
        for cp in copies:
            cp.wait_send()
        own.wait()

    return pl.pallas_call(
        body, name=name,
        in_specs=[HBM_SPEC] * (n + 1), out_specs=[HBM_SPEC] * (n + 1),
        out_shape=[jax.ShapeDtypeStruct(p.shape[1:], p.dtype) for p in parts]
        + [jax.ShapeDtypeStruct((2 * N_CHIPS,) + small.shape, small.dtype)],
        scratch_shapes=[pltpu.SemaphoreType.DMA((n + 7,)), pltpu.SemaphoreType.DMA((n + 7,)), pltpu.SemaphoreType.DMA],
    )(*parts, small)


def _scatter(sums, name):
    n = len(sums)

    def body(*refs):
        ins, outs = refs[:n], refs[n:2 * n]
        send_sems, recv_sems, local_sems = refs[2 * n:]
        x, y, c, chips = _place()
        me = 2 * x + y
        slots = [2 * cx + cy for cx, cy in chips]
        local = [pltpu.make_async_copy(ins[a].at[me], outs[a].at[me], local_sems.at[a]) for a in range(n)]
        for cp in local:
            cp.start()
        copies = [_remote(ins[a].at[slots[k]], outs[a].at[me], send_sems.at[3 * a + k], recv_sems.at[3 * a + k], (*chips[k], c))
                  for a in range(n) for k in range(3)]
        for cp in copies:
            cp.start()
        for a in range(n):
            for k in range(3):
                _remote(ins[a].at[slots[k]], outs[a].at[slots[k]], send_sems.at[3 * a + k], recv_sems.at[3 * a + k],
                        (*chips[k], c)).wait_recv()
        for cp in copies:
            cp.wait_send()
        for cp in local:
            cp.wait()

    return pl.pallas_call(
        body, name=name,
        in_specs=[HBM_SPEC] * n, out_specs=[HBM_SPEC] * n,
        out_shape=[jax.ShapeDtypeStruct(s.shape, s.dtype) for s in sums],
        scratch_shapes=[pltpu.SemaphoreType.DMA((3 * n,)), pltpu.SemaphoreType.DMA((3 * n,)), pltpu.SemaphoreType.DMA((n,))],
    )(*sums)


def _sibling_share(bufs, name):
    n = len(bufs)

    def body(*refs):
        outs = refs[n:2 * n]
        send_sems, recv_sems = refs[2 * n:]
        x, y, c, _ = _place()
        sibling = (x, y, 1 - c)
        copies = [_remote(outs[a].at[c], outs[a].at[c], send_sems.at[a], recv_sems.at[a], sibling) for a in range(n)]
        for cp in copies:
            cp.start()
        for a in range(n):
            _remote(outs[a].at[c], outs[a].at[1 - c], send_sems.at[a], recv_sems.at[a], sibling).wait_recv()
        for cp in copies:
            cp.wait_send()

    return pl.pallas_call(
        body, name=name,
        in_specs=[HBM_SPEC] * n, out_specs=[HBM_SPEC] * n,
        out_shape=[jax.ShapeDtypeStruct(b.shape, b.dtype) for b in bufs],
        input_output_aliases={a: a for a in range(n)},
        scratch_shapes=[pltpu.SemaphoreType.DMA((n,)), pltpu.SemaphoreType.DMA((n,))],
    )(*bufs)


def _pair_sum(core, part, received, name):
    _, _, rh, cols = part.shape

    def body(core_ref, p_ref, r_ref, o_ref):
        o_ref[...] = (p_ref[0].astype(F32) + r_ref[...].astype(F32)).astype(BF16)

    return pl.pallas_call(
        body, name=name,
        grid_spec=pltpu.PrefetchScalarGridSpec(
            num_scalar_prefetch=1, grid=(N_CHIPS,),
            in_specs=[pl.BlockSpec((1, 1, rh, cols), lambda j, core_ref: (core_ref[0], j, 0, 0)),
                      pl.BlockSpec((1, rh, cols), lambda j, core_ref: (j, 0, 0))],
            out_specs=pl.BlockSpec((1, rh, cols), lambda j, core_ref: (j, 0, 0))),
        out_shape=jax.ShapeDtypeStruct((N_CHIPS, rh, cols), BF16),
        compiler_params=_params(("parallel",), 32),
    )(core, part, received)


def _sum_leading(stack, steps, name):
    k, rows, cols = stack.shape
    tile = rows // steps

    def body(s_ref, o_ref):
        total = s_ref[0].astype(F32)
        for d in range(1, k):
            total = total + s_ref[d].astype(F32)
        o_ref[...] = total

    return pl.pallas_call(
        body, name=name, grid=(steps,),
        in_specs=[pl.BlockSpec((k, tile, cols), lambda i: (0, i, 0))],
        out_specs=pl.BlockSpec((tile, cols), lambda i: (i, 0)),
        out_shape=jax.ShapeDtypeStruct((rows, cols), F32),
        compiler_params=_params(("parallel",), 32),
    )(stack)


def _chip_sum(core, stack, steps, name):
    k, rows, cols = stack.shape
    tile = rows // steps

    def body(core_ref, s_ref, o_ref):
        total = s_ref[0].astype(F32)
        for d in range(1, k):
            total = total + s_ref[d].astype(F32)
        o_ref[0] = total

    return pl.pallas_call(
        body, name=name,
        grid_spec=pltpu.PrefetchScalarGridSpec(
            num_scalar_prefetch=1, grid=(steps,),
            in_specs=[pl.BlockSpec((k, tile, cols), lambda i, core_ref: (0, i, 0))],
            out_specs=pl.BlockSpec((1, tile, cols), lambda i, core_ref: (core_ref[0], i, 0))),
        out_shape=jax.ShapeDtypeStruct((2, rows, cols), F32),
        compiler_params=_params(("parallel",), 32),
    )(core, stack)


def _adamw(w, g, row0, m, v, tile, name):
    rows, cols = w.shape
    first = row0 // tile
    assert rows % tile == 0 and row0 % tile == 0
    bc1 = 1.0 - ADAM_B1 ** ADAM_STEP
    bc2 = 1.0 - ADAM_B2 ** ADAM_STEP

    def body(w_ref, g_ref, m_ref, v_ref, go_ref, d_ref, mo_ref, vo_ref):
        g = g_ref[...]
        m_new = ADAM_B1 * m_ref[...] + (1.0 - ADAM_B1) * g
        v_new = ADAM_B2 * v_ref[...] + (1.0 - ADAM_B2) * (g * g)
        go_ref[...] = g
        d_ref[...] = -ADAM_LR * ((m_new / bc1) / (jnp.sqrt(v_new / bc2) + ADAM_EPS) + ADAM_WD * w_ref[...])
        mo_ref[...] = m_new
        vo_ref[...] = v_new

    spec = pl.BlockSpec((tile, cols), lambda i: (i, 0))
    g_spec = pl.BlockSpec((tile, cols), lambda i: (first + i, 0))
    return pl.pallas_call(
        body, name=name, grid=(rows // tile,),
        in_specs=[spec, g_spec, spec, spec], out_specs=[spec] * 4,
        out_shape=[jax.ShapeDtypeStruct((rows, cols), F32)] * 4,
        compiler_params=_params(("parallel",), 32),
    )(w, g, m, v)


SMALL = ("ffn1_norm", "mix_norm", "ffn2_norm", "final_norm", "pool_scale", "sink_logits", "pool_w")


def _pack_small(d):
    sink = jnp.pad(d["sink_logits"].reshape(1, N_HEADS), ((0, 0), (0, LANES - N_HEADS)))
    rows = [d[n].reshape(-1, LANES) for n in SMALL[:5]] + [sink, d["pool_w"].reshape(-1, LANES)]
    packed = jnp.concatenate(rows, axis=0)
    return jnp.pad(packed, ((0, SMALL_ROWS - packed.shape[0]), (0, 0)))


def _unpack_small(packed, like):
    out, row = {}, 0
    for n in SMALL:
        size = LANES if n == "sink_logits" else math.prod(like[n].shape)
        chunk = packed[row:row + size // LANES].reshape(-1)
        out[n] = (chunk[:N_HEADS] if n == "sink_logits" else chunk).reshape(like[n].shape)
        row += size // LANES
    return out


def kernel(x, ffn1_norm, ffn1_w_gate, ffn1_w_up, ffn1_w_down, mix_norm, w_in, sink_logits, pool_w, pool_scale, w_out, ffn2_norm, ffn2_w_gate, ffn2_w_up, ffn2_w_down, final_norm, loss_target, m_ffn1_norm, m_ffn1_w_gate, m_ffn1_w_up, m_ffn1_w_down, m_mix_norm, m_w_in, m_sink_logits, m_pool_w, m_pool_scale, m_w_out, m_ffn2_norm, m_ffn2_w_gate, m_ffn2_w_up, m_ffn2_w_down, m_final_norm, v_ffn1_norm, v_ffn1_w_gate, v_ffn1_w_up, v_ffn1_w_down, v_mix_norm, v_w_in, v_sink_logits, v_pool_w, v_pool_scale, v_w_out, v_ffn2_norm, v_ffn2_w_gate, v_ffn2_w_up, v_ffn2_w_down, v_final_norm):
    names = ("ffn1_norm", "ffn1_w_gate", "ffn1_w_up", "ffn1_w_down", "mix_norm", "w_in", "sink_logits", "pool_w",
             "pool_scale", "w_out", "ffn2_norm", "ffn2_w_gate", "ffn2_w_up", "ffn2_w_down", "final_norm")
    weights = dict(zip(names, (ffn1_norm, ffn1_w_gate, ffn1_w_up, ffn1_w_down, mix_norm, w_in, sink_logits, pool_w,
                               pool_scale, w_out, ffn2_norm, ffn2_w_gate, ffn2_w_up, ffn2_w_down, final_norm)))
    mom1 = dict(zip(names, (m_ffn1_norm, m_ffn1_w_gate, m_ffn1_w_up, m_ffn1_w_down, m_mix_norm, m_w_in, m_sink_logits,
                            m_pool_w, m_pool_scale, m_w_out, m_ffn2_norm, m_ffn2_w_gate, m_ffn2_w_up, m_ffn2_w_down,
                            m_final_norm)))
    mom2 = dict(zip(names, (v_ffn1_norm, v_ffn1_w_gate, v_ffn1_w_up, v_ffn1_w_down, v_mix_norm, v_w_in, v_sink_logits,
                            v_pool_w, v_pool_scale, v_w_out, v_ffn2_norm, v_ffn2_w_gate, v_ffn2_w_up, v_ffn2_w_down,
                            v_final_norm)))
    core = lax.axis_index("c").astype(jnp.int32).reshape(1)

    def rows_of(t, n):
        return jnp.swapaxes(t[n][0], 0, 1) if n in TRANSPOSED else t[n][0]

    chip = (2 * lax.axis_index("x") + lax.axis_index("y")).astype(jnp.int32).reshape(1)
    packed = [_pack(chip, [rows_of(weights, n) for n in GROUPS[0]], "pack_ffn1"),
              _pack(chip, [jnp.concatenate([rows_of(weights, n) for n in GROUPS[1]], axis=0)], "pack_mix"),
              _pack(chip, [rows_of(weights, n) for n in GROUPS[2]], "pack_ffn2")]
    gathered = _allgather([p.reshape(N_CHIPS, 2, p.shape[1] // 2, D_MODEL) for p in packed], "weight_allgather")
    ffn1, mix, ffn2 = (g.reshape(N_CHIPS, -1, D_MODEL) for g in gathered)

    small_w = {"ffn1_norm": ffn1_norm, "mix_norm": mix_norm, "ffn2_norm": ffn2_norm,
               "final_norm": final_norm.reshape(1, D_MODEL), "pool_scale": pool_scale, "sink_logits": sink_logits,
               "pool_w": pool_w[0]}
    loss_lanes, grad_x, group_g, small_g = _local_step(x[0], loss_target[0], ffn1, mix, ffn2, small_w)

    tags = ("ffn1", "mix", "ffn2")
    received = _sibling_exchange(list(group_g), _pack_small(small_g), "grad_sibling_exchange")
    chip_sums = [_pair_sum(core, g, r, "grad_pair_sum_" + t) for t, g, r in zip(tags, group_g, received[:-1])]
    small_sum = _sum_leading(received[-1], 1, "small_grad_sum")
    stacks = _scatter(chip_sums, "grad_scatter")
    reduced = [_chip_sum(core, s, 2, "grad_chip_sum_" + t) for t, s in zip(tags, stacks)]
    both = _sibling_share(reduced, "grad_sibling_share")
    group_grads = [b.reshape(-1, D_MODEL) for b in both]

    out_g, out_d, out_m, out_v = {}, {}, {}, {}
    for members, g in zip(GROUPS, group_grads):
        row0 = 0
        for n in members:
            w = rows_of(weights, n)
            tile = FF_CHUNK // 4 if w.shape[0] == FF_CHUNK else math.gcd(IN_ROWS, OUT_ROWS)
            outs = _adamw(w, g, row0, rows_of(mom1, n), rows_of(mom2, n), tile, "adamw_" + n)
            row0 += w.shape[0]
            for dst, t in zip((out_g, out_d, out_m, out_v), outs):
                dst[n] = (jnp.swapaxes(t, 0, 1) if n in TRANSPOSED else t).reshape(weights[n].shape)
    small_outs = _adamw(_pack_small(weights), small_sum, 0, _pack_small(mom1), _pack_small(mom2), SMALL_ROWS, "adamw_small")
    for dst, packed in zip((out_g, out_d, out_m, out_v), small_outs):
        dst.update(_unpack_small(packed, weights))

    loss = lax.psum(jnp.sum(loss_lanes), ("x", "y", "c"))
    return (loss, grad_x.reshape(x.shape), *[out_g[n] for n in names], *[out_d[n] for n in names],
            *[out_m[n] for n in names], *[out_v[n] for n in names])
```

```python
import functools
import math

import jax
import jax.numpy as jnp
from jax import lax
from jax.experimental import pallas as pl
from jax.experimental.pallas import tpu as pltpu

F32, BF16 = jnp.float32, jnp.bfloat16
MESH = pl.DeviceIdType.MESH

D_MODEL = 1024
D_FF = 2816
N_CHIPS = 4
FF_CHUNK = D_FF // N_CHIPS
HEAD_DIM = 64
N_HEADS = 8
N_KV = 2
Q_PER_KV = N_HEADS // N_KV
KV_WIDTH = N_KV * HEAD_DIM
ATTN_WIDTH = N_HEADS * HEAD_DIM
POOL_WINDOWS = (2, 4, 8, 16)
N_POOL = len(POOL_WINDOWS)
POOL_GROUP = 128
POOL_WIDTH = N_POOL * POOL_GROUP
IN_WIDTH = ATTN_WIDTH + 2 * KV_WIDTH + POOL_WIDTH
WINDOW = 128
BLOCK = 128
BAND = 3 * BLOCK
ROPE_THETA = 500000.0
ROTARY_DIM = HEAD_DIM // 4
EPS = 1e-6
LANES = 128
Q_PAD = N_HEADS * LANES
U_PAD = Q_PAD + 2 * KV_WIDTH + POOL_WIDTH
SCALE = HEAD_DIM ** -0.5
NEG = -1e30

ADAM_LR, ADAM_B1, ADAM_B2, ADAM_EPS, ADAM_WD, ADAM_STEP = 0.001, 0.9, 0.999, 1e-08, 0.01, 10

V7X_VMEM_BYTES = 64 * 1024 * 1024
TOK_TILE = 512
BWD_TILE = 256
SMALL_ROWS = 552


def _params(sem, vmem_mb):
    return pltpu.CompilerParams(dimension_semantics=sem, vmem_limit_bytes=vmem_mb * 1024 * 1024)


def _dot(a, b):
    return lax.dot_general(a, b, (((1,), (0,)), ((), ())), preferred_element_type=F32)


def _dot_nt(a, b):
    return lax.dot_general(a, b, (((1,), (1,)), ((), ())), preferred_element_type=F32)


def _dot_tn(a, b):
    return lax.dot_general(a, b, (((0,), (0,)), ((), ())), preferred_element_type=F32)


def _rms_stats(h):
    r = lax.rsqrt(jnp.mean(h * h, axis=-1, keepdims=True) + EPS)
    return r, h * r


def _rms_bwd(dn, g, r, xh):
    gd = dn * g
    dh = r * (gd - xh * jnp.mean(gd * xh, axis=-1, keepdims=True))
    return dh, jnp.sum(dn * xh, axis=0, keepdims=True)


def _rope(x, c, s1, s2):
    return x * c + pltpu.roll(x, LANES - ROTARY_DIM // 2, 1) * s1 + pltpu.roll(x, ROTARY_DIM // 2, 1) * s2


def _rope_bwd(d, c, s1, s2):
    return d * c + pltpu.roll(d * s1, ROTARY_DIM // 2, 1) + pltpu.roll(d * s2, LANES - ROTARY_DIM // 2, 1)


def _sum_chunks(ref):
    total = ref[0].astype(F32)
    for j in range(1, N_CHIPS):
        total = total + ref[j].astype(F32)
    return total


def _chunk_rows(tile):
    return pl.BlockSpec((N_CHIPS, tile, D_MODEL), lambda i, *_: (0, i, 0))


def _full(shape):
    nd = len(shape)
    return pl.BlockSpec(shape, lambda *_: (0,) * nd)


def _rows(tile, cols):
    return pl.BlockSpec((tile, cols), lambda i, *_: (i, 0))


def _ffn_fwd(h, gain, group, name):
    S = h.shape[0]
    tile = min(TOK_TILE, S)
    nt = S // tile

    def body(h_ref, g_ref, wg_ref, wu_ref, wd_ref, ho_ref, n_ref, gate_ref, up_ref, acc):
        j = pl.program_id(1)

        @pl.when(j == 0)
        def _():
            _, xh = _rms_stats(h_ref[...])
            n_ref[...] = (xh * g_ref[...]).astype(BF16)
            acc[...] = jnp.zeros_like(acc)

        n = n_ref[...]
        gate = _dot_nt(n, wg_ref[0])
        up = _dot_nt(n, wu_ref[0])
        gate_ref[0] = gate.astype(BF16)
        up_ref[0] = up.astype(BF16)
        act = (gate * jax.nn.sigmoid(gate) * up).astype(BF16)
        acc[...] += _dot(act, wd_ref[0])

        @pl.when(j == N_CHIPS - 1)
        def _():
            ho_ref[...] = h_ref[...] + 0.5 * acc[...]

    tok = pl.BlockSpec((tile, D_MODEL), lambda i, j: (i, 0))
    hid = pl.BlockSpec((1, tile, FF_CHUNK), lambda i, j: (j, i, 0))
    return pl.pallas_call(
        body, name=name, grid=(nt, N_CHIPS),
        in_specs=[tok, pl.BlockSpec((1, D_MODEL), lambda i, j: (0, 0))]
        + [pl.BlockSpec((1, FF_CHUNK, D_MODEL), functools.partial(lambda i, j, part: (j, part, 0), part=part))
           for part in range(3)],
        out_specs=[tok, tok, hid, hid],
        out_shape=[jax.ShapeDtypeStruct((S, D_MODEL), F32), jax.ShapeDtypeStruct((S, D_MODEL), BF16),
                   jax.ShapeDtypeStruct((N_CHIPS, S, FF_CHUNK), BF16), jax.ShapeDtypeStruct((N_CHIPS, S, FF_CHUNK), BF16)],
        scratch_shapes=[pltpu.VMEM((tile, D_MODEL), F32)],
        compiler_params=_params(("parallel", "arbitrary"), 40),
    )(h, gain, group, group, group)


def _ffn_bwd(d_out, n, gate, up, group, name):
    S = n.shape[0]
    tile = min(TOK_TILE, S)
    nt = S // tile
    half_rows = 3 * FF_CHUNK // 2
    cut = FF_CHUNK // 2

    def body(do_ref, n_ref, gate_ref, up_ref, wg_ref, wu_ref, wd_ref, dn_ref, dw_ref, acc_g, acc_u, acc_d):
        j, i = pl.program_id(0), pl.program_id(1)

        @pl.when(i == 0)
        def _():
            acc_g[...] = jnp.zeros_like(acc_g)
            acc_u[...] = jnp.zeros_like(acc_u)
            acc_d[...] = jnp.zeros_like(acc_d)

        do = do_ref[...]
        nn = n_ref[...]
        g = gate_ref[0].astype(F32)
        u = up_ref[0].astype(F32)
        d_act = _dot_nt(do, wd_ref[0])
        sig = jax.nn.sigmoid(g)
        silu = g * sig
        d_up = (d_act * silu).astype(BF16)
        d_gate = (d_act * u * (sig * (1.0 + g * (1.0 - sig)))).astype(BF16)
        act = (silu * u).astype(BF16)
        dn_ref[0] = (_dot(d_gate, wg_ref[0]) + _dot(d_up, wu_ref[0])).astype(BF16)
        acc_g[...] += _dot_tn(d_gate, nn)
        acc_u[...] += _dot_tn(d_up, nn)
        acc_d[...] += _dot_tn(act, do)

        @pl.when(i == nt - 1)
        def _():
            dw_ref[0, 0, :FF_CHUNK, :] = acc_g[...].astype(BF16)
            dw_ref[0, 0, FF_CHUNK:, :] = acc_u[:cut, :].astype(BF16)
            dw_ref[1, 0, :cut, :] = acc_u[cut:, :].astype(BF16)
            dw_ref[1, 0, cut:, :] = acc_d[...].astype(BF16)

    tok = pl.BlockSpec((tile, D_MODEL), lambda j, i: (i, 0))
    hid = pl.BlockSpec((1, tile, FF_CHUNK), lambda j, i: (j, i, 0))
    return pl.pallas_call(
        body, name=name, grid=(N_CHIPS, nt),
        in_specs=[tok, tok, hid, hid]
        + [pl.BlockSpec((1, FF_CHUNK, D_MODEL), functools.partial(lambda j, i, part: (j, part, 0), part=part))
           for part in range(3)],
        out_specs=[pl.BlockSpec((1, tile, D_MODEL), lambda j, i: (j, i, 0)),
                   pl.BlockSpec((2, 1, half_rows, D_MODEL), lambda j, i: (0, j, 0, 0))],
        out_shape=[jax.ShapeDtypeStruct((N_CHIPS, S, D_MODEL), BF16),
                   jax.ShapeDtypeStruct((2, N_CHIPS, half_rows, D_MODEL), BF16)],
        scratch_shapes=[pltpu.VMEM((FF_CHUNK, D_MODEL), F32)] * 3,
        compiler_params=_params(("arbitrary", "arbitrary"), 56),
    )(d_out, n, gate, up, group, group, group)


def _mix_in(h, gain, w_in, rc, rs1, rs2, name):
    S = h.shape[0]
    tile = min(TOK_TILE, S)

    def body(h_ref, g_ref, w_ref, c_ref, s1_ref, s2_ref, n_ref, q_ref, k_ref, v_ref, pc_ref):
        _, xh = _rms_stats(h_ref[...])
        n = (xh * g_ref[...]).astype(BF16)
        n_ref[...] = n
        u = _dot_nt(n, w_ref[...])
        c, s1, s2 = c_ref[...], s1_ref[...], s2_ref[...]
        q_ref[...] = jnp.concatenate([(_rope(u[:, hd * LANES:(hd + 1) * LANES], c, s1, s2) * SCALE).astype(BF16)
                                      for hd in range(N_HEADS)], axis=1)
        k_ref[...] = _rope(u[:, Q_PAD:Q_PAD + KV_WIDTH], c, s1, s2).astype(BF16)
        v_ref[...] = u[:, Q_PAD + KV_WIDTH:Q_PAD + 2 * KV_WIDTH].astype(BF16)
        pc_ref[...] = u[:, Q_PAD + 2 * KV_WIDTH:]

    return pl.pallas_call(
        body, name=name, grid=(S // tile,),
        in_specs=[_rows(tile, D_MODEL), _full((1, D_MODEL)), _full((U_PAD, D_MODEL)),
                  _rows(tile, LANES), _rows(tile, LANES), _rows(tile, LANES)],
        out_specs=[_rows(tile, D_MODEL), _rows(tile, Q_PAD), _rows(tile, KV_WIDTH), _rows(tile, KV_WIDTH),
                   _rows(tile, POOL_WIDTH)],
        out_shape=[jax.ShapeDtypeStruct((S, D_MODEL), BF16), jax.ShapeDtypeStruct((S, Q_PAD), BF16),
                   jax.ShapeDtypeStruct((S, KV_WIDTH), BF16), jax.ShapeDtypeStruct((S, KV_WIDTH), BF16),
                   jax.ShapeDtypeStruct((S, POOL_WIDTH), F32)],
        compiler_params=_params(("parallel",), 40),
    )(h, gain, w_in, rc, rs1, rs2)


def _band_start(i, S):
    return pl.multiple_of(jnp.clip((i - 1) * BLOCK, 0, S - BAND), BLOCK)


def _window_bias(off):
    r = lax.broadcasted_iota(jnp.int32, (BLOCK, 1), 0)
    c = lax.broadcasted_iota(jnp.int32, (1, BAND), 1)
    return jnp.where(jnp.abs(off + r - c) <= WINDOW, 0.0, NEG).astype(F32)


def _softmax_parts(qh, kb, bias, sink_h):
    s = _dot_nt(qh, kb) + bias
    m = jnp.maximum(jnp.max(s, axis=-1, keepdims=True), sink_h)
    p = jnp.exp(s - m)
    es = jnp.exp(sink_h - m)
    return p, es, 1.0 / (jnp.sum(p, axis=-1, keepdims=True) + es)


def _pool_matrix(t0, start, S, w):
    r = lax.broadcasted_iota(jnp.int32, (BLOCK, 1), 0) + t0
    c = lax.broadcasted_iota(jnp.int32, (1, BAND), 1) + start
    half = w // 2

    def window(lo, hi):
        a = jnp.maximum(lo, 0)
        b = jnp.minimum(hi + 1, S)
        return jnp.where((c >= a) & (c < b), 1.0 / (b - a).astype(F32), 0.0)

    return (0.5 * (window(r - half, r + half - 1) + window(r - half + 1, r + half))).astype(BF16)


def _pool_matrices(S):
    blocks = ((0, 0), (BLOCK, 0), (S - BLOCK, S - BAND))
    return jnp.stack([jnp.stack([_pool_matrix(t0, start, S, w) for w in POOL_WINDOWS]) for t0, start in blocks])


def _pool_spec(nb):
    return pl.BlockSpec((1, N_POOL, BLOCK, BAND), lambda i, *_: (jnp.where(i == 0, 0, jnp.where(i == nb - 1, 2, 1)), 0, 0, 0))


def _mix_core_fwd(q, k, v, pc, sink, pool_m, pool_w, pool_scale, name):
    S = q.shape[0]
    nb = S // BLOCK

    def body(sink_ref, q_ref, k_ref, v_ref, pc_ref, pm_ref, pw_ref, ps_ref, a_ref, p_ref):
        i = pl.program_id(0)
        start = _band_start(i, S)
        band = pl.ds(start, BAND)
        bias = _window_bias(i * BLOCK - start)
        kb, vb = k_ref[band, :], v_ref[band, :]
        hs = range(N_HEADS)
        ss = [_dot_nt(q_ref[:, hd * LANES:(hd + 1) * LANES], kb) + bias for hd in hs]
        ms = [jnp.maximum(jnp.max(ss[hd], axis=-1, keepdims=True), sink_ref[0, hd]) for hd in hs]
        ps = [jnp.exp(ss[hd] - ms[hd]) for hd in hs]
        invs = [1.0 / (jnp.sum(ps[hd], axis=-1, keepdims=True) + jnp.exp(sink_ref[0, hd] - ms[hd])) for hd in hs]
        outs = [_dot(ps[hd].astype(BF16), vb) for hd in hs]
        a_ref[...] = jnp.concatenate([(outs[hd] * invs[hd]).astype(BF16) for hd in hs], axis=1)
        centre = pl.ds(pl.multiple_of(i * BLOCK, BLOCK), BLOCK)
        gs = range(N_POOL)
        sl = [slice(g * POOL_GROUP, (g + 1) * POOL_GROUP) for g in gs]
        means = [_dot(pm_ref[0, g], pc_ref[band, sl[g]].astype(BF16)) for g in gs]
        devs = [(means[g] - pc_ref[centre, sl[g]]).astype(BF16) for g in gs]
        p_ref[...] = (jnp.concatenate([_dot(devs[g], pw_ref[g]) for g in gs], axis=1) * ps_ref[...]).astype(BF16)

    return pl.pallas_call(
        body, name=name, grid=(nb,),
        in_specs=[pl.BlockSpec(memory_space=pltpu.SMEM), _rows(BLOCK, Q_PAD), _full((S, KV_WIDTH)), _full((S, KV_WIDTH)),
                  _full((S, POOL_WIDTH)), _pool_spec(nb), _full((N_POOL, POOL_GROUP, POOL_GROUP)), _full((1, POOL_WIDTH))],
        out_specs=[_rows(BLOCK, Q_PAD), _rows(BLOCK, POOL_WIDTH)],
        out_shape=[jax.ShapeDtypeStruct((S, Q_PAD), BF16), jax.ShapeDtypeStruct((S, POOL_WIDTH), BF16)],
        compiler_params=_params(("parallel",), 40),
    )(sink, q, k, v, pc, pool_m, pool_w, pool_scale)


def _mix_core_bwd(q, k, v, pc, da, dp, sink, pool_m, pool_w, pool_scale, rc, rs1, rs2, name):
    S = q.shape[0]
    nb = S // BLOCK

    def body(sink_ref, q_ref, k_ref, v_ref, pc_ref, da_ref, dp_ref, pm_ref, pw_ref, ps_ref, c_ref, s1_ref, s2_ref,
             dq_ref, dk_ref, dv_ref, dpc_ref, dsink_ref, dpw_ref, dps_ref):
        i = pl.program_id(0)

        @pl.when(i == 0)
        def _():
            dk_ref[...] = jnp.zeros_like(dk_ref)
            dv_ref[...] = jnp.zeros_like(dv_ref)
            dpc_ref[...] = jnp.zeros_like(dpc_ref)
            dsink_ref[...] = jnp.zeros_like(dsink_ref)
            dpw_ref[...] = jnp.zeros_like(dpw_ref)
            dps_ref[...] = jnp.zeros_like(dps_ref)

        start = _band_start(i, S)
        band = pl.ds(start, BAND)
        bias = _window_bias(i * BLOCK - start)
        kb, vb = k_ref[band, :], v_ref[band, :]
        c, s1, s2 = c_ref[...], s1_ref[...], s2_ref[...]
        lane = lax.broadcasted_iota(jnp.int32, (1, LANES), 1)
        hs = range(N_HEADS)
        qs = [q_ref[:, hd * LANES:(hd + 1) * LANES] for hd in hs]
        das = [da_ref[:, hd * LANES:(hd + 1) * LANES] for hd in hs]
        ss = [_dot_nt(qs[hd], kb) + bias for hd in hs]
        d_probs = [_dot_nt(das[hd], vb) for hd in hs]
        ms = [jnp.maximum(jnp.max(ss[hd], axis=-1, keepdims=True), sink_ref[0, hd]) for hd in hs]
        ps = [jnp.exp(ss[hd] - ms[hd]) for hd in hs]
        ess = [jnp.exp(sink_ref[0, hd] - ms[hd]) for hd in hs]
        invs = [1.0 / (jnp.sum(ps[hd], axis=-1, keepdims=True) + ess[hd]) for hd in hs]
        probs = [ps[hd] * invs[hd] for hd in hs]
        deltas = [jnp.sum(probs[hd] * d_probs[hd], axis=-1, keepdims=True) for hd in hs]
        d_ss = [(probs[hd] * (d_probs[hd] - deltas[hd])).astype(BF16) for hd in hs]
        dqs = [_dot(d_ss[hd], kb) for hd in hs]
        dq_ref[...] = jnp.concatenate([_rope_bwd(dqs[hd] * SCALE, c, s1, s2).astype(BF16) for hd in hs], axis=1)
        dks = [_dot_tn(d_ss[hd], qs[hd]) for hd in hs]
        dvs = [_dot_tn(probs[hd].astype(BF16), das[hd]) for hd in hs]
        dk_ref[band, :] += functools.reduce(lambda a, b: a + b, dks)
        dv_ref[band, :] += functools.reduce(lambda a, b: a + b, dvs)
        dsink_ref[...] += functools.reduce(lambda a, b: a + b, [
            jnp.where(lane == hd, -jnp.sum(ess[hd] * invs[hd] * deltas[hd], axis=0, keepdims=True), 0.0) for hd in hs])

        centre = pl.ds(pl.multiple_of(i * BLOCK, BLOCK), BLOCK)
        gs = range(N_POOL)
        sl = [slice(g * POOL_GROUP, (g + 1) * POOL_GROUP) for g in gs]
        devs = [(_dot(pm_ref[0, g], pc_ref[band, sl[g]].astype(BF16)) - pc_ref[centre, sl[g]]).astype(BF16) for g in gs]
        dys = [dp_ref[:, sl[g]].astype(F32) for g in gs]
        zs = [_dot(devs[g], pw_ref[g]) for g in gs]
        dzs = [(dys[g] * ps_ref[:, sl[g]]).astype(BF16) for g in gs]
        d_devs = [_dot_nt(dzs[g], pw_ref[g]) for g in gs]
        dps_ref[...] += jnp.concatenate([jnp.sum(dys[g] * zs[g], axis=0, keepdims=True) for g in gs], axis=1)
        for g in gs:
            dpw_ref[g] += _dot_tn(devs[g], dzs[g])
        dpc_ref[band, :] += jnp.concatenate([_dot_tn(pm_ref[0, g], d_devs[g].astype(BF16)) for g in gs], axis=1)
        dpc_ref[centre, :] -= jnp.concatenate(d_devs, axis=1)

    return pl.pallas_call(
        body, name=name, grid=(nb,),
        in_specs=[pl.BlockSpec(memory_space=pltpu.SMEM), _rows(BLOCK, Q_PAD), _full((S, KV_WIDTH)), _full((S, KV_WIDTH)),
                  _full((S, POOL_WIDTH)), _rows(BLOCK, Q_PAD), _rows(BLOCK, POOL_WIDTH), _pool_spec(nb),
                  _full((N_POOL, POOL_GROUP, POOL_GROUP)), _full((1, POOL_WIDTH)),
                  _rows(BLOCK, LANES), _rows(BLOCK, LANES), _rows(BLOCK, LANES)],
        out_specs=[_rows(BLOCK, Q_PAD), _full((S, KV_WIDTH)), _full((S, KV_WIDTH)), _full((S, POOL_WIDTH)),
                   _full((1, LANES)), _full((N_POOL, POOL_GROUP, POOL_GROUP)), _full((1, POOL_WIDTH))],
        out_shape=[jax.ShapeDtypeStruct((S, Q_PAD), BF16), jax.ShapeDtypeStruct((S, KV_WIDTH), F32),
                   jax.ShapeDtypeStruct((S, KV_WIDTH), F32), jax.ShapeDtypeStruct((S, POOL_WIDTH), F32),
                   jax.ShapeDtypeStruct((1, LANES), F32), jax.ShapeDtypeStruct((N_POOL, POOL_GROUP, POOL_GROUP), F32),
                   jax.ShapeDtypeStruct((1, POOL_WIDTH), F32)],
        compiler_params=_params(("arbitrary",), 56),
    )(sink, q, k, v, pc, da, dp, pool_m, pool_w, pool_scale, rc, rs1, rs2)


def _mix_out(h, a, p, wa, wp, name):
    S = h.shape[0]
    tile = min(TOK_TILE, S)

    def body(h_ref, a_ref, p_ref, wa_ref, wp_ref, o_ref):
        o_ref[...] = h_ref[...] + _dot(a_ref[...], wa_ref[...]) + _dot(p_ref[...], wp_ref[...])

    return pl.pallas_call(
        body, name=name, grid=(S // tile,),
        in_specs=[_rows(tile, D_MODEL), _rows(tile, Q_PAD), _rows(tile, POOL_WIDTH),
                  _full((Q_PAD, D_MODEL)), _full((POOL_WIDTH, D_MODEL))],
        out_specs=_rows(tile, D_MODEL),
        out_shape=jax.ShapeDtypeStruct((S, D_MODEL), F32),
        compiler_params=_params(("parallel",), 40),
    )(h, a, p, wa, wp)


def _loss_head(h, target, gain, name):
    S = h.shape[0]
    tile = min(TOK_TILE, S)

    def body(h_ref, t_ref, g_ref, dh_ref, dhalf_ref, loss_ref, dg_ref):
        @pl.when(pl.program_id(0) == 0)
        def _():
            loss_ref[...] = jnp.zeros_like(loss_ref)
            dg_ref[...] = jnp.zeros_like(dg_ref)

        g = g_ref[...]
        r, xh = _rms_stats(h_ref[...])
        err = xh * g - t_ref[...]
        loss_ref[...] += (0.5 / D_MODEL) * jnp.sum(err * err, axis=0, keepdims=True)
        dh, dg = _rms_bwd(err * (1.0 / D_MODEL), g, r, xh)
        dg_ref[...] += dg
        dh_ref[...] = dh
        dhalf_ref[...] = (0.5 * dh).astype(BF16)

    return pl.pallas_call(
        body, name=name, grid=(S // tile,),
        in_specs=[_rows(tile, D_MODEL), _rows(tile, D_MODEL), _full((1, D_MODEL))],
        out_specs=[_rows(tile, D_MODEL), _rows(tile, D_MODEL), _full((1, D_MODEL)), _full((1, D_MODEL))],
        out_shape=[jax.ShapeDtypeStruct((S, D_MODEL), F32), jax.ShapeDtypeStruct((S, D_MODEL), BF16),
                   jax.ShapeDtypeStruct((1, D_MODEL), F32), jax.ShapeDtypeStruct((1, D_MODEL), F32)],
        compiler_params=_params(("arbitrary",), 40),
    )(h, target, gain)


def _mix_out_bwd(dh_out, dn, h, gain, a, p, wa, wp, name):
    S = h.shape[0]
    tile = min(TOK_TILE, S)

    def body(do_ref, dn_ref, h_ref, g_ref, a_ref, p_ref, wa_ref, wp_ref, dh_ref, da_ref, dp_ref, dwa_ref, dwp_ref, dg_ref):
        @pl.when(pl.program_id(0) == 0)
        def _():
            dwa_ref[...] = jnp.zeros_like(dwa_ref)
            dwp_ref[...] = jnp.zeros_like(dwp_ref)
            dg_ref[...] = jnp.zeros_like(dg_ref)

        r, xh = _rms_stats(h_ref[...])
        dnorm, dg = _rms_bwd(_sum_chunks(dn_ref), g_ref[...], r, xh)
        dh = do_ref[...] + dnorm
        dg_ref[...] += dg
        dh_ref[...] = dh
        dhb = dh.astype(BF16)
        da_ref[...] = _dot_nt(dhb, wa_ref[...]).astype(BF16)
        dp_ref[...] = _dot_nt(dhb, wp_ref[...]).astype(BF16)
        dwa_ref[...] += _dot_tn(a_ref[...], dhb)
        dwp_ref[...] += _dot_tn(p_ref[...], dhb)

    return pl.pallas_call(
        body, name=name, grid=(S // tile,),
        in_specs=[_rows(tile, D_MODEL), _chunk_rows(tile), _rows(tile, D_MODEL), _full((1, D_MODEL)),
                  _rows(tile, Q_PAD), _rows(tile, POOL_WIDTH), _full((Q_PAD, D_MODEL)), _full((POOL_WIDTH, D_MODEL))],
        out_specs=[_rows(tile, D_MODEL), _rows(tile, Q_PAD), _rows(tile, POOL_WIDTH),
                   _full((Q_PAD, D_MODEL)), _full((POOL_WIDTH, D_MODEL)), _full((1, D_MODEL))],
        out_shape=[jax.ShapeDtypeStruct((S, D_MODEL), F32), jax.ShapeDtypeStruct((S, Q_PAD), BF16),
                   jax.ShapeDtypeStruct((S, POOL_WIDTH), BF16), jax.ShapeDtypeStruct((Q_PAD, D_MODEL), F32),
                   jax.ShapeDtypeStruct((POOL_WIDTH, D_MODEL), F32), jax.ShapeDtypeStruct((1, D_MODEL), F32)],
        compiler_params=_params(("arbitrary",), 48),
    )(dh_out, dn, h, gain, a, p, wa, wp)


def _mix_in_bwd(dh_out, h, gain, n, dq, dk, dv, dpc, rc, rs1, rs2, w_in, name):
    S = h.shape[0]
    tile = min(TOK_TILE, S)

    def body(do_ref, h_ref, g_ref, n_ref, dq_ref, dk_ref, dv_ref, dpc_ref, c_ref, s1_ref, s2_ref, w_ref,
             dh_ref, dhalf_ref, dw_ref, dg_ref):
        @pl.when(pl.program_id(0) == 0)
        def _():
            dw_ref[...] = jnp.zeros_like(dw_ref)
            dg_ref[...] = jnp.zeros_like(dg_ref)

        dk = _rope_bwd(dk_ref[...], c_ref[...], s1_ref[...], s2_ref[...]).astype(BF16)
        du = jnp.concatenate([dq_ref[...], dk, dv_ref[...].astype(BF16), dpc_ref[...].astype(BF16)], axis=1)
        dn = _dot(du, w_ref[...])
        dw_ref[...] += _dot_tn(du, n_ref[...])
        r, xh = _rms_stats(h_ref[...])
        dnorm, dg = _rms_bwd(dn, g_ref[...], r, xh)
        dh = do_ref[...] + dnorm
        dg_ref[...] += dg
        dh_ref[...] = dh
        dhalf_ref[...] = (0.5 * dh).astype(BF16)

    return pl.pallas_call(
        body, name=name, grid=(S // tile,),
        in_specs=[_rows(tile, D_MODEL), _rows(tile, D_MODEL), _full((1, D_MODEL)), _rows(tile, D_MODEL),
                  _rows(tile, Q_PAD), _rows(tile, KV_WIDTH), _rows(tile, KV_WIDTH), _rows(tile, POOL_WIDTH),
                  _rows(tile, LANES), _rows(tile, LANES), _rows(tile, LANES), _full((U_PAD, D_MODEL))],
        out_specs=[_rows(tile, D_MODEL), _rows(tile, D_MODEL), _full((U_PAD, D_MODEL)), _full((1, D_MODEL))],
        out_shape=[jax.ShapeDtypeStruct((S, D_MODEL), F32), jax.ShapeDtypeStruct((S, D_MODEL), BF16),
                   jax.ShapeDtypeStruct((U_PAD, D_MODEL), F32), jax.ShapeDtypeStruct((1, D_MODEL), F32)],
        compiler_params=_params(("arbitrary",), 56),
    )(dh_out, h, gain, n, dq, dk, dv, dpc, rc, rs1, rs2, w_in)


def _norm_bwd(dh_out, dn, h, gain, name):
    S = h.shape[0]
    tile = min(TOK_TILE, S)

    def body(do_ref, dn_ref, h_ref, g_ref, dh_ref, dg_ref):
        @pl.when(pl.program_id(0) == 0)
        def _():
            dg_ref[...] = jnp.zeros_like(dg_ref)

        r, xh = _rms_stats(h_ref[...])
        dnorm, dg = _rms_bwd(_sum_chunks(dn_ref), g_ref[...], r, xh)
        dg_ref[...] += dg
        dh_ref[...] = do_ref[...] + dnorm

    return pl.pallas_call(
        body, name=name, grid=(S // tile,),
        in_specs=[_rows(tile, D_MODEL), _chunk_rows(tile), _rows(tile, D_MODEL), _full((1, D_MODEL))],
        out_specs=[_rows(tile, D_MODEL), _full((1, D_MODEL))],
        out_shape=[jax.ShapeDtypeStruct((S, D_MODEL), F32), jax.ShapeDtypeStruct((1, D_MODEL), F32)],
        compiler_params=_params(("arbitrary",), 40),
    )(dh_out, dn, h, gain)


def _rope_tables(S):
    half = ROTARY_DIM // 2
    inv_freq = ROPE_THETA ** (-jnp.arange(0, ROTARY_DIM, 2, dtype=F32) / ROTARY_DIM)
    dim = jnp.arange(LANES) % HEAD_DIM
    ang = jnp.arange(S, dtype=F32)[:, None] * inv_freq[dim % half][None, :]
    lo, hi = (dim < half)[None, :], ((dim >= half) & (dim < ROTARY_DIM))[None, :]
    c = jnp.where(lo | hi, jnp.cos(ang), 1.0)
    s1 = jnp.where(lo, -jnp.sin(ang), 0.0)
    s2 = jnp.where(hi, jnp.sin(ang), 0.0)
    return c, s1, s2


def _pad_heads(w, axis):
    w = jnp.moveaxis(w, axis, 0)
    heads = w.reshape((N_HEADS, HEAD_DIM) + w.shape[1:])
    zero = jnp.zeros_like(heads)
    first = (jnp.arange(N_HEADS) < Q_PER_KV).reshape((N_HEADS, 1) + (1,) * (w.ndim - 1))
    lo = jnp.where(first, heads, zero)
    hi = jnp.where(first, zero, heads)
    padded = jnp.concatenate([lo, hi], axis=1).reshape((Q_PAD,) + w.shape[1:])
    return jnp.moveaxis(padded, 0, axis)


def _unpad_heads(w, axis):
    w = jnp.moveaxis(w, axis, 0)
    groups = w.reshape((N_HEADS, 2, HEAD_DIM) + w.shape[1:])
    first = (jnp.arange(N_HEADS) < Q_PER_KV).reshape((N_HEADS, 1) + (1,) * (w.ndim - 1))
    heads = jnp.where(first, groups[:, 0], groups[:, 1]).reshape((ATTN_WIDTH,) + w.shape[1:])
    return jnp.moveaxis(heads, 0, axis)


IN_ROWS = IN_WIDTH // N_CHIPS
OUT_ROWS = (ATTN_WIDTH + POOL_WIDTH) // N_CHIPS
MIX_ROWS = IN_ROWS + OUT_ROWS
FFN_ROWS = 3 * FF_CHUNK


def _local_step(x, target, ffn1, mix, ffn2, small):
    S = x.shape[0]
    rc, rs1, rs2 = _rope_tables(S)
    w_in_t = mix[:, :IN_ROWS].reshape(IN_WIDTH, D_MODEL)
    w_in_pad = jnp.concatenate([_pad_heads(w_in_t[:ATTN_WIDTH], 0), w_in_t[ATTN_WIDTH:]], axis=0)
    w_out = mix[:, IN_ROWS:].reshape(ATTN_WIDTH + POOL_WIDTH, D_MODEL)
    wa = _pad_heads(w_out[:ATTN_WIDTH], 0)
    wp = w_out[ATTN_WIDTH:]
    pool_w = small["pool_w"].astype(BF16)

    h1, n1, gate1, up1 = _ffn_fwd(x, small["ffn1_norm"], ffn1, "ffn1_fwd")
    n2, q, k, v, pc = _mix_in(h1, small["mix_norm"], w_in_pad, rc, rs1, rs2, "mix_in")
    pool_m = _pool_matrices(S)
    a, p = _mix_core_fwd(q, k, v, pc, small["sink_logits"], pool_m, pool_w, small["pool_scale"], "mix_core_fwd")
    h2 = _mix_out(h1, a, p, wa, wp, "mix_out")
    h3, n3, gate2, up2 = _ffn_fwd(h2, small["ffn2_norm"], ffn2, "ffn2_fwd")
    dh3, dhalf3, loss_lanes, d_final = _loss_head(h3, target, small["final_norm"], "loss_head")

    dn3, d_ffn2 = _ffn_bwd(dhalf3, n3, gate2, up2, ffn2, "ffn2_bwd")
    dh2, da, dp, dwa, dwp, d_ffn2_norm = _mix_out_bwd(dh3, dn3, h2, small["ffn2_norm"], a, p, wa, wp, "mix_out_bwd")
    dq, dk, dv, dpc, dsink, dpool_w, dpool_scale = _mix_core_bwd(
        q, k, v, pc, da, dp, small["sink_logits"], pool_m, pool_w, small["pool_scale"], rc, rs1, rs2, "mix_core_bwd")
    dh1, dhalf1, dw_in_pad, d_mix_norm = _mix_in_bwd(dh2, h1, small["mix_norm"], n2, dq, dk, dv, dpc, rc, rs1, rs2,
                                                     w_in_pad, "mix_in_bwd")
    dn1, d_ffn1 = _ffn_bwd(dhalf1, n1, gate1, up1, ffn1, "ffn1_bwd")
    grad_x, d_ffn1_norm = _norm_bwd(dh1, dn1, x, small["ffn1_norm"], "norm1_bwd")

    dw_in_t = jnp.concatenate([_unpad_heads(dw_in_pad[:Q_PAD], 0), dw_in_pad[Q_PAD:]], axis=0)
    dw_out = jnp.concatenate([_unpad_heads(dwa, 0), dwp], axis=0)
    d_mix = jnp.concatenate([dw_in_t.reshape(N_CHIPS, IN_ROWS, D_MODEL), dw_out.reshape(N_CHIPS, OUT_ROWS, D_MODEL)], axis=1)
    d_mix = jnp.transpose(d_mix.reshape(N_CHIPS, 2, MIX_ROWS // 2, D_MODEL), (1, 0, 2, 3)).astype(BF16)
    small_g = {"ffn1_norm": d_ffn1_norm, "mix_norm": d_mix_norm, "ffn2_norm": d_ffn2_norm, "final_norm": d_final,
               "pool_scale": dpool_scale, "sink_logits": dsink[:, :N_HEADS], "pool_w": dpool_w}
    return loss_lanes, grad_x, (d_ffn1, d_mix, d_ffn2), small_g


HBM_SPEC = pl.BlockSpec(memory_space=pltpu.HBM)
GROUPS = (("ffn1_w_gate", "ffn1_w_up", "ffn1_w_down"), ("w_in", "w_out"), ("ffn2_w_gate", "ffn2_w_up", "ffn2_w_down"))
TRANSPOSED = ("ffn1_w_gate", "ffn1_w_up", "w_in", "ffn2_w_gate", "ffn2_w_up")


def _place():
    x, y, c = lax.axis_index("x"), lax.axis_index("y"), lax.axis_index("c")
    chips = [(1 - x, y), (x, 1 - y), (1 - x, 1 - y)]
    return x, y, c, chips


def _remote(src, dst, send_sem, recv_sem, to):
    return pltpu.make_async_remote_copy(src_ref=src, dst_ref=dst, send_sem=send_sem, recv_sem=recv_sem,
                                        device_id=to, device_id_type=MESH)


def _pack(chip, members, name):
    rows = members[0].shape[0]
    n = len(members)

    def body(chip_ref, *refs):
        out_ref = refs[n]
        for k in range(n):
            @pl.when(pl.program_id(0) == k)
            def _(k=k):
                out_ref[0] = refs[k][...].astype(BF16)

    return pl.pallas_call(
        body, name=name,
        grid_spec=pltpu.PrefetchScalarGridSpec(
            num_scalar_prefetch=1, grid=(n,),
            in_specs=[pl.BlockSpec((rows, D_MODEL), lambda k, chip_ref: (0, 0))] * n,
            out_specs=pl.BlockSpec((1, rows, D_MODEL), lambda k, chip_ref: (chip_ref[0], k, 0))),
        out_shape=jax.ShapeDtypeStruct((N_CHIPS, n * rows, D_MODEL), BF16),
        compiler_params=_params(("arbitrary",), 40),
    )(chip, *members)


def _allgather(bufs, name):
    n = len(bufs)

    def body(*refs):
        outs = refs[n:2 * n]
        send_sems, recv_sems = refs[2 * n:]
        x, y, c, chips = _place()
        me = 2 * x + y
        sibling = (x, y, 1 - c)
        slots = [2 * cx + cy for cx, cy in chips]
        first = [[_remote(outs[a].at[me, c], outs[a].at[me, c], send_sems.at[6 * a + k], recv_sems.at[6 * a + k], (*chips[k], c))
                  for k in range(3)] for a in range(n)]
        for a in range(n):
            for cp in first[a]:
                cp.start()
        passed = [[_remote(outs[a].at[slots[k], c], outs[a].at[slots[k], c], send_sems.at[6 * a + 3 + k],
                           recv_sems.at[6 * a + 3 + k], sibling) for k in range(3)] for a in range(n)]
        for a in range(n):
            for k in range(3):
                _remote(outs[a].at[me, c], outs[a].at[slots[k], c], send_sems.at[6 * a + k], recv_sems.at[6 * a + k],
                        (*chips[k], c)).wait_recv()
                passed[a][k].start()
        for a in range(n):
            for k in range(3):
                _remote(outs[a].at[me, c], outs[a].at[slots[k], 1 - c], send_sems.at[6 * a + 3 + k],
                        recv_sems.at[6 * a + 3 + k], sibling).wait_recv()
        for a in range(n):
            for cp in first[a] + passed[a]:
                cp.wait_send()

    return pl.pallas_call(
        body, name=name,
        in_specs=[HBM_SPEC] * n, out_specs=[HBM_SPEC] * n,
        out_shape=[jax.ShapeDtypeStruct(b.shape, b.dtype) for b in bufs],
        input_output_aliases={a: a for a in range(n)},
        scratch_shapes=[pltpu.SemaphoreType.DMA((6 * n,)), pltpu.SemaphoreType.DMA((6 * n,))],
    )(*bufs)


def _sibling_exchange(parts, small, name):
    n = len(parts)
    flips = [(fx, fy, fc) for fx in range(2) for fy in range(2) for fc in range(2)][1:]

    def body(*refs):
        ins, small_ref = refs[:n], refs[n]
        outs, gather_ref = refs[n + 1:2 * n + 1], refs[2 * n + 1]
        send_sems, recv_sems, local_sem = refs[2 * n + 2:]
        x, y, c, _ = _place()
        me = 4 * x + 2 * y + c
        sibling = (x, y, 1 - c)
        own = pltpu.make_async_copy(small_ref, gather_ref.at[me], local_sem)
        own.start()
        peers = [((1 - x) if fx else x, (1 - y) if fy else y, (1 - c) if fc else c) for fx, fy, fc in flips]
        copies = [_remote(ins[a].at[1 - c], outs[a], send_sems.at[a], recv_sems.at[a], sibling) for a in range(n)]
        copies += [_remote(small_ref, gather_ref.at[me], send_sems.at[n + k], recv_sems.at[n + k], peer)
                   for k, peer in enumerate(peers)]
        for cp in copies:
            cp.start()
        for a in range(n):
            _remote(ins[a].at[1 - c], outs[a], send_sems.at[a], recv_sems.at[a], sibling).wait_recv()
        for k, (px, py, pc) in enumerate(peers):
            _remote(small_ref, gather_ref.at[4 * px + 2 * py + pc], send_sems.at[n + k], recv_sems.at[n + k],
                    (px, py, pc)).wait_recv()
        for cp in copies:
            cp.wait_send()
        own.wait()

    return pl.pallas_call(
        body, name=name,
        in_specs=[HBM_SPEC] * (n + 1), out_specs=[HBM_SPEC] * (n + 1),
        out_shape=[jax.ShapeDtypeStruct(p.shape[1:], p.dtype) for p in parts]
        + [jax.ShapeDtypeStruct((2 * N_CHIPS,) + small.shape, small.dtype)],
        scratch_shapes=[pltpu.SemaphoreType.DMA((n + 7,)), pltpu.SemaphoreType.DMA((n + 7,)), pltpu.SemaphoreType.DMA],
    )(*parts, small)


def _scatter(sums, name):
    n = len(sums)

    def body(*refs):
        ins, outs = refs[:n], refs[n:2 * n]
        send_sems, recv_sems, local_sems = refs[2 * n:]
        x, y, c, chips = _place()
        me = 2 * x + y
        slots = [2 * cx + cy for cx, cy in chips]
        local = [pltpu.make_async_copy(ins[a].at[me], outs[a].at[me], local_sems.at[a]) for a in range(n)]
        for cp in local:
            cp.start()
        copies = [_remote(ins[a].at[slots[k]], outs[a].at[me], send_sems.at[3 * a + k], recv_sems.at[3 * a + k], (*chips[k], c))
                  for a in range(n) for k in range(3)]
        for cp in copies:
            cp.start()
        for a in range(n):
            for k in range(3):
                _remote(ins[a].at[slots[k]], outs[a].at[slots[k]], send_sems.at[3 * a + k], recv_sems.at[3 * a + k],
                        (*chips[k], c)).wait_recv()
        for cp in copies:
            cp.wait_send()
        for cp in local:
            cp.wait()

    return pl.pallas_call(
        body, name=name,
        in_specs=[HBM_SPEC] * n, out_specs=[HBM_SPEC] * n,
        out_shape=[jax.ShapeDtypeStruct(s.shape, s.dtype) for s in sums],
        scratch_shapes=[pltpu.SemaphoreType.DMA((3 * n,)), pltpu.SemaphoreType.DMA((3 * n,)), pltpu.SemaphoreType.DMA((n,))],
    )(*sums)


def _sibling_share(bufs, name):
    n = len(bufs)

    def body(*refs):
        outs = refs[n:2 * n]
        send_sems, recv_sems = refs[2 * n:]
        x, y, c, _ = _place()
        sibling = (x, y, 1 - c)
        copies = [_remote(outs[a].at[c], outs[a].at[c], send_sems.at[a], recv_sems.at[a], sibling) for a in range(n)]
        for cp in copies:
            cp.start()
        for a in range(n):
            _remote(outs[a].at[c], outs[a].at[1 - c], send_sems.at[a], recv_sems.at[a], sibling).wait_recv()
        for cp in copies:
            cp.wait_send()

    return pl.pallas_call(
        body, name=name,
        in_specs=[HBM_SPEC] * n, out_specs=[HBM_SPEC] * n,
        out_shape=[jax.ShapeDtypeStruct(b.shape, b.dtype) for b in bufs],
        input_output_aliases={a: a for a in range(n)},
        scratch_shapes=[pltpu.SemaphoreType.DMA((n,)), pltpu.SemaphoreType.DMA((n,))],
    )(*bufs)


def _pair_sum(core, part, received, name):
    _, _, rh, cols = part.shape

    def body(core_ref, p_ref, r_ref, o_ref):
        o_ref[...] = (p_ref[0].astype(F32) + r_ref[...].astype(F32)).astype(BF16)

    return pl.pallas_call(
        body, name=name,
        grid_spec=pltpu.PrefetchScalarGridSpec(
            num_scalar_prefetch=1, grid=(N_CHIPS,),
            in_specs=[pl.BlockSpec((1, 1, rh, cols), lambda j, core_ref: (core_ref[0], j, 0, 0)),
                      pl.BlockSpec((1, rh, cols), lambda j, core_ref: (j, 0, 0))],
            out_specs=pl.BlockSpec((1, rh, cols), lambda j, core_ref: (j, 0, 0))),
        out_shape=jax.ShapeDtypeStruct((N_CHIPS, rh, cols), BF16),
        compiler_params=_params(("parallel",), 32),
    )(core, part, received)


def _sum_leading(stack, steps, name):
    k, rows, cols = stack.shape
    tile = rows // steps

    def body(s_ref, o_ref):
        total = s_ref[0].astype(F32)
        for d in range(1, k):
            total = total + s_ref[d].astype(F32)
        o_ref[...] = total

    return pl.pallas_call(
        body, name=name, grid=(steps,),
        in_specs=[pl.BlockSpec((k, tile, cols), lambda i: (0, i, 0))],
        out_specs=pl.BlockSpec((tile, cols), lambda i: (i, 0)),
        out_shape=jax.ShapeDtypeStruct((rows, cols), F32),
        compiler_params=_params(("parallel",), 32),
    )(stack)


def _chip_sum(core, stack, steps, name):
    k, rows, cols = stack.shape
    tile = rows // steps

    def body(core_ref, s_ref, o_ref):
        total = s_ref[0].astype(F32)
        for d in range(1, k):
            total = total + s_ref[d].astype(F32)
        o_ref[0] = total

    return pl.pallas_call(
        body, name=name,
        grid_spec=pltpu.PrefetchScalarGridSpec(
            num_scalar_prefetch=1, grid=(steps,),
            in_specs=[pl.BlockSpec((k, tile, cols), lambda i, core_ref: (0, i, 0))],
            out_specs=pl.BlockSpec((1, tile, cols), lambda i, core_ref: (core_ref[0], i, 0))),
        out_shape=jax.ShapeDtypeStruct((2, rows, cols), F32),
        compiler_params=_params(("parallel",), 32),
    )(core, stack)


def _adamw(w, g, row0, m, v, tile, name):
    rows, cols = w.shape
    first = row0 // tile
    assert rows % tile == 0 and row0 % tile == 0
    bc1 = 1.0 - ADAM_B1 ** ADAM_STEP
    bc2 = 1.0 - ADAM_B2 ** ADAM_STEP

    def body(w_ref, g_ref, m_ref, v_ref, go_ref, d_ref, mo_ref, vo_ref):
        g = g_ref[...]
        m_new = ADAM_B1 * m_ref[...] + (1.0 - ADAM_B1) * g
        v_new = ADAM_B2 * v_ref[...] + (1.0 - ADAM_B2) * (g * g)
        go_ref[...] = g
        d_ref[...] = -ADAM_LR * ((m_new / bc1) / (jnp.sqrt(v_new / bc2) + ADAM_EPS) + ADAM_WD * w_ref[...])
        mo_ref[...] = m_new
        vo_ref[...] = v_new

    spec = pl.BlockSpec((tile, cols), lambda i: (i, 0))
    g_spec = pl.BlockSpec((tile, cols), lambda i: (first + i, 0))
    return pl.pallas_call(
        body, name=name, grid=(rows // tile,),
        in_specs=[spec, g_spec, spec, spec], out_specs=[spec] * 4,
        out_shape=[jax.ShapeDtypeStruct((rows, cols), F32)] * 4,
        compiler_params=_params(("parallel",), 32),
    )(w, g, m, v)


SMALL = ("ffn1_norm", "mix_norm", "ffn2_norm", "final_norm", "pool_scale", "sink_logits", "pool_w")


def _pack_small(d):
    sink = jnp.pad(d["sink_logits"].reshape(1, N_HEADS), ((0, 0), (0, LANES - N_HEADS)))
    rows = [d[n].reshape(-1, LANES) for n in SMALL[:5]] + [sink, d["pool_w"].reshape(-1, LANES)]
    packed = jnp.concatenate(rows, axis=0)
    return jnp.pad(packed, ((0, SMALL_ROWS - packed.shape[0]), (0, 0)))


def _unpack_small(packed, like):
    out, row = {}, 0
    for n in SMALL:
        size = LANES if n == "sink_logits" else math.prod(like[n].shape)
        chunk = packed[row:row + size // LANES].reshape(-1)
        out[n] = (chunk[:N_HEADS] if n == "sink_logits" else chunk).reshape(like[n].shape)
        row += size // LANES
    return out


def kernel(x, ffn1_norm, ffn1_w_gate, ffn1_w_up, ffn1_w_down, mix_norm, w_in, sink_logits, pool_w, pool_scale, w_out, ffn2_norm, ffn2_w_gate, ffn2_w_up, ffn2_w_down, final_norm, loss_target, m_ffn1_norm, m_ffn1_w_gate, m_ffn1_w_up, m_ffn1_w_down, m_mix_norm, m_w_in, m_sink_logits, m_pool_w, m_pool_scale, m_w_out, m_ffn2_norm, m_ffn2_w_gate, m_ffn2_w_up, m_ffn2_w_down, m_final_norm, v_ffn1_norm, v_ffn1_w_gate, v_ffn1_w_up, v_ffn1_w_down, v_mix_norm, v_w_in, v_sink_logits, v_pool_w, v_pool_scale, v_w_out, v_ffn2_norm, v_ffn2_w_gate, v_ffn2_w_up, v_ffn2_w_down, v_final_norm):
    names = ("ffn1_norm", "ffn1_w_gate", "ffn1_w_up", "ffn1_w_down", "mix_norm", "w_in", "sink_logits", "pool_w",
             "pool_scale", "w_out", "ffn2_norm", "ffn2_w_gate", "ffn2_w_up", "ffn2_w_down", "final_norm")
    weights = dict(zip(names, (ffn1_norm, ffn1_w_gate, ffn1_w_up, ffn1_w_down, mix_norm, w_in, sink_logits, pool_w,
                               pool_scale, w_out, ffn2_norm, ffn2_w_gate, ffn2_w_up, ffn2_w_down, final_norm)))
    mom1 = dict(zip(names, (m_ffn1_norm, m_ffn1_w_gate, m_ffn1_w_up, m_ffn1_w_down, m_mix_norm, m_w_in, m_sink_logits,
                            m_pool_w, m_pool_scale, m_w_out, m_ffn2_norm, m_ffn2_w_gate, m_ffn2_w_up, m_ffn2_w_down,
                            m_final_norm)))
    mom2 = dict(zip(names, (v_ffn1_norm, v_ffn1_w_gate, v_ffn1_w_up, v_ffn1_w_down, v_mix_norm, v_w_in, v_sink_logits,
                            v_pool_w, v_pool_scale, v_w_out, v_ffn2_norm, v_ffn2_w_gate, v_ffn2_w_up, v_ffn2_w_down,
                            v_final_norm)))
    core = lax.axis_index("c").astype(jnp.int32).reshape(1)

    def rows_of(t, n):
        return jnp.swapaxes(t[n][0], 0, 1) if n in TRANSPOSED else t[n][0]

    chip = (2 * lax.axis_index("x") + lax.axis_index("y")).astype(jnp.int32).reshape(1)
    packed = [_pack(chip, [rows_of(weights, n) for n in GROUPS[0]], "pack_ffn1"),
              _pack(chip, [jnp.concatenate([rows_of(weights, n) for n in GROUPS[1]], axis=0)], "pack_mix"),
              _pack(chip, [rows_of(weights, n) for n in GROUPS[2]], "pack_ffn2")]
    gathered = _allgather([p.reshape(N_CHIPS, 2, p.shape[1] // 2, D_MODEL) for p in packed], "weight_allgather")
    ffn1, mix, ffn2 = (g.reshape(N_CHIPS, -1, D_MODEL) for g in gathered)

    small_w = {"ffn1_norm": ffn1_norm, "mix_norm": mix_norm, "ffn2_norm": ffn2_norm,
               "final_norm": final_norm.reshape(1, D_MODEL), "pool_scale": pool_scale, "sink_logits": sink_logits,
               "pool_w": pool_w[0]}
    loss_lanes, grad_x, group_g, small_g = _local_step(x[0], loss_target[0], ffn1, mix, ffn2, small_w)

    tags = ("ffn1", "mix", "ffn2")
    received = _sibling_exchange(list(group_g), _pack_small(small_g), "grad_sibling_exchange")
    chip_sums = [_pair_sum(core, g, r, "grad_pair_sum_" + t) for t, g, r in zip(tags, group_g, received[:-1])]
    small_sum = _sum_leading(received[-1], 1, "small_grad_sum")
    stacks = _scatter(chip_sums, "grad_scatter")
    reduced = [_chip_sum(core, s, 2, "grad_chip_sum_" + t) for t, s in zip(tags, stacks)]
    both = _sibling_share(reduced, "grad_sibling_share")
    group_grads = [b.reshape(-1, D_MODEL) for b in both]

    out_g, out_d, out_m, out_v = {}, {}, {}, {}
    for members, g in zip(GROUPS, group_grads):
        row0 = 0
        for n in members:
            w = rows_of(weights, n)
            tile = FF_CHUNK // 4 if w.shape[0] == FF_CHUNK else math.gcd(IN_ROWS, OUT_ROWS)
            outs = _adamw(w, g, row0, rows_of(mom1, n), rows_of(mom2, n), tile, "adamw_" + n)
            row0 += w.shape[0]
            for dst, t in zip((out_g, out_d, out_m, out_v), outs):
                dst[n] = (jnp.swapaxes(t, 0, 1) if n in TRANSPOSED else t).reshape(weights[n].shape)
    small_outs = _adamw(_pack_small(weights), small_sum, 0, _pack_small(mom1), _pack_small(mom2), SMALL_ROWS, "adamw_small")
    for dst, packed in zip((out_g, out_d, out_m, out_v), small_outs):
        dst.update(_unpack_small(packed, weights))

    loss = lax.psum(jnp.sum(loss_lanes), ("x", "y", "c"))
    return (loss, grad_x.reshape(x.shape), *[out_g[n] for n in names], *[out_d[n] for n in names],
            *[out_m[n] for n in names], *[out_v[n] for n in names])
```

```python
import collections
import functools
import math

import jax
import jax.numpy as jnp
from jax import lax
from jax.experimental import pallas as pl
from jax.experimental.pallas import tpu as pltpu

F32, BF16 = jnp.float32, jnp.bfloat16
MESH = pl.DeviceIdType.MESH

D_MODEL = 1024
D_FF = 2816
N_CHIPS = 4
FF_CHUNK = D_FF // N_CHIPS
HEAD_DIM = 64
N_HEADS = 8
N_KV = 2
Q_PER_KV = N_HEADS // N_KV
KV_WIDTH = N_KV * HEAD_DIM
ATTN_WIDTH = N_HEADS * HEAD_DIM
POOL_WINDOWS = (2, 4, 8, 16)
N_POOL = len(POOL_WINDOWS)
POOL_GROUP = 128
POOL_WIDTH = N_POOL * POOL_GROUP
IN_WIDTH = ATTN_WIDTH + 2 * KV_WIDTH + POOL_WIDTH
WINDOW = 128
BLOCK = 128
BAND = 3 * BLOCK
ROPE_THETA = 500000.0
ROTARY_DIM = HEAD_DIM // 4
EPS = 1e-6
LANES = 128
Q_PAD = N_HEADS * LANES
U_PAD = Q_PAD + 2 * KV_WIDTH + POOL_WIDTH
SCALE = HEAD_DIM ** -0.5
NEG = -1e30

ADAM_LR, ADAM_B1, ADAM_B2, ADAM_EPS, ADAM_WD, ADAM_STEP = 0.001, 0.9, 0.999, 1e-08, 0.01, 10

V7X_VMEM_BYTES = 64 * 1024 * 1024
TOK_TILE = 512
BWD_TILE = 256
SMALL_ROWS = 552


def _params(sem, vmem_mb):
    return pltpu.CompilerParams(dimension_semantics=sem, vmem_limit_bytes=vmem_mb * 1024 * 1024)


def _dot(a, b):
    return lax.dot_general(a, b, (((1,), (0,)), ((), ())), preferred_element_type=F32)


def _dot_nt(a, b):
    return lax.dot_general(a, b, (((1,), (1,)), ((), ())), preferred_element_type=F32)


def _dot_tn(a, b):
    return lax.dot_general(a, b, (((0,), (0,)), ((), ())), preferred_element_type=F32)


def _rms_stats(h):
    r = lax.rsqrt(jnp.mean(h * h, axis=-1, keepdims=True) + EPS)
    return r, h * r


def _rms_bwd(dn, g, r, xh):
    gd = dn * g
    dh = r * (gd - xh * jnp.mean(gd * xh, axis=-1, keepdims=True))
    return dh, jnp.sum(dn * xh, axis=0, keepdims=True)


def _rope(x, c, s1, s2):
    return x * c + pltpu.roll(x, LANES - ROTARY_DIM // 2, 1) * s1 + pltpu.roll(x, ROTARY_DIM // 2, 1) * s2


def _rope_bwd(d, c, s1, s2):
    return d * c + pltpu.roll(d * s1, ROTARY_DIM // 2, 1) + pltpu.roll(d * s2, LANES - ROTARY_DIM // 2, 1)


def _sum_chunks(ref):
    total = ref[0].astype(F32)
    for j in range(1, N_CHIPS):
        total = total + ref[j].astype(F32)
    return total


def _chunk_rows(tile):
    return pl.BlockSpec((N_CHIPS, tile, D_MODEL), lambda i, *_: (0, i, 0))


def _full(shape):
    nd = len(shape)
    return pl.BlockSpec(shape, lambda *_: (0,) * nd)


def _rows(tile, cols):
    return pl.BlockSpec((tile, cols), lambda i, *_: (i, 0))


HBM_SPEC = pl.BlockSpec(memory_space=pltpu.HBM)

_Rider = collections.namedtuple("_Rider", "operands out_shapes aliases scratch start finish")


def _call(body, name, grid, in_specs, out_specs, out_shape, scratch, vmem_mb, args, rider=None):
    if rider is None:
        return pl.pallas_call(body, name=name, grid=grid, in_specs=in_specs, out_specs=out_specs, out_shape=out_shape,
                              scratch_shapes=scratch, compiler_params=_params(("arbitrary",) * len(grid), vmem_mb))(*args)
    n_in, n_out, n_scr, r_in, r_out = len(in_specs), len(out_specs), len(scratch), len(rider.operands), len(rider.out_shapes)

    def fused(*refs):
        ins, refs = refs[:n_in], refs[n_in:]
        r_ins, refs = refs[:r_in], refs[r_in:]
        outs, refs = refs[:n_out], refs[n_out:]
        r_outs, refs = refs[:r_out], refs[r_out:]
        scr, r_scr = refs[:n_scr], refs[n_scr:]
        ids = [pl.program_id(d) for d in range(len(grid))]
        first = functools.reduce(jnp.logical_and, [i == 0 for i in ids])
        last = functools.reduce(jnp.logical_and, [i == g - 1 for i, g in zip(ids, grid)])

        @pl.when(first)
        def _():
            rider.start(r_ins, r_outs, r_scr)

        body(*ins, *outs, *scr)

        @pl.when(last)
        def _():
            rider.finish(r_ins, r_outs, r_scr)

    return pl.pallas_call(
        fused, name=name, grid=grid,
        in_specs=list(in_specs) + [HBM_SPEC] * r_in, out_specs=list(out_specs) + [HBM_SPEC] * r_out,
        out_shape=list(out_shape) + list(rider.out_shapes),
        input_output_aliases={n_in + k: n_out + v for k, v in rider.aliases.items()},
        scratch_shapes=list(scratch) + list(rider.scratch),
        compiler_params=_params(("arbitrary",) * len(grid), vmem_mb),
    )(*args, *rider.operands)


def _comm_call(rider, name):
    r_in, r_out = len(rider.operands), len(rider.out_shapes)

    def body(*refs):
        r_ins, r_outs, r_scr = refs[:r_in], refs[r_in:r_in + r_out], refs[r_in + r_out:]
        rider.start(r_ins, r_outs, r_scr)
        rider.finish(r_ins, r_outs, r_scr)

    return pl.pallas_call(
        body, name=name, in_specs=[HBM_SPEC] * r_in, out_specs=[HBM_SPEC] * r_out, out_shape=list(rider.out_shapes),
        input_output_aliases=dict(rider.aliases), scratch_shapes=list(rider.scratch),
    )(*rider.operands)


def _ffn_fwd(h, gain, group, name, rider=None):
    S = h.shape[0]
    tile = min(TOK_TILE, S)
    nt = S // tile

    def body(h_ref, g_ref, wg_ref, wu_ref, wd_ref, ho_ref, n_ref, gate_ref, up_ref, acc):
        j = pl.program_id(1)

        @pl.when(j == 0)
        def _():
            _, xh = _rms_stats(h_ref[...])
            n_ref[...] = (xh * g_ref[...]).astype(BF16)
            acc[...] = jnp.zeros_like(acc)

        n = n_ref[...]
        gate = _dot_nt(n, wg_ref[0])
        up = _dot_nt(n, wu_ref[0])
        gate_ref[0] = gate.astype(BF16)
        up_ref[0] = up.astype(BF16)
        act = (gate * jax.nn.sigmoid(gate) * up).astype(BF16)
        acc[...] += _dot(act, wd_ref[0])

        @pl.when(j == N_CHIPS - 1)
        def _():
            ho_ref[...] = h_ref[...] + 0.5 * acc[...]

    tok = pl.BlockSpec((tile, D_MODEL), lambda i, j: (i, 0))
    hid = pl.BlockSpec((1, tile, FF_CHUNK), lambda i, j: (j, i, 0))
    return _call(
        body, name, (nt, N_CHIPS),
        [tok, pl.BlockSpec((1, D_MODEL), lambda i, j: (0, 0))]
        + [pl.BlockSpec((1, FF_CHUNK, D_MODEL), functools.partial(lambda i, j, part: (j, part, 0), part=part))
           for part in range(3)],
        [tok, tok, hid, hid],
        [jax.ShapeDtypeStruct((S, D_MODEL), F32), jax.ShapeDtypeStruct((S, D_MODEL), BF16),
         jax.ShapeDtypeStruct((N_CHIPS, S, FF_CHUNK), BF16), jax.ShapeDtypeStruct((N_CHIPS, S, FF_CHUNK), BF16)],
        [pltpu.VMEM((tile, D_MODEL), F32)], 40, (h, gain, group, group, group), rider)


def _ffn_bwd(d_out, n, gate, up, group, name):
    S = n.shape[0]
    tile = min(TOK_TILE, S)
    nt = S // tile
    half_rows = 3 * FF_CHUNK // 2
    cut = FF_CHUNK // 2

    def body(do_ref, n_ref, gate_ref, up_ref, wg_ref, wu_ref, wd_ref, dn_ref, dw_ref, acc_g, acc_u, acc_d):
        j, i = pl.program_id(0), pl.program_id(1)

        @pl.when(i == 0)
        def _():
            acc_g[...] = jnp.zeros_like(acc_g)
            acc_u[...] = jnp.zeros_like(acc_u)
            acc_d[...] = jnp.zeros_like(acc_d)

        do = do_ref[...]
        nn = n_ref[...]
        g = gate_ref[0].astype(F32)
        u = up_ref[0].astype(F32)
        d_act = _dot_nt(do, wd_ref[0])
        sig = jax.nn.sigmoid(g)
        silu = g * sig
        d_up = (d_act * silu).astype(BF16)
        d_gate = (d_act * u * (sig * (1.0 + g * (1.0 - sig)))).astype(BF16)
        act = (silu * u).astype(BF16)
        dn_ref[0] = (_dot(d_gate, wg_ref[0]) + _dot(d_up, wu_ref[0])).astype(BF16)
        acc_g[...] += _dot_tn(d_gate, nn)
        acc_u[...] += _dot_tn(d_up, nn)
        acc_d[...] += _dot_tn(act, do)

        @pl.when(i == nt - 1)
        def _():
            dw_ref[0, 0, :FF_CHUNK, :] = acc_g[...].astype(BF16)
            dw_ref[0, 0, FF_CHUNK:, :] = acc_u[:cut, :].astype(BF16)
            dw_ref[1, 0, :cut, :] = acc_u[cut:, :].astype(BF16)
            dw_ref[1, 0, cut:, :] = acc_d[...].astype(BF16)

    tok = pl.BlockSpec((tile, D_MODEL), lambda j, i: (i, 0))
    hid = pl.BlockSpec((1, tile, FF_CHUNK), lambda j, i: (j, i, 0))
    return pl.pallas_call(
        body, name=name, grid=(N_CHIPS, nt),
        in_specs=[tok, tok, hid, hid]
        + [pl.BlockSpec((1, FF_CHUNK, D_MODEL), functools.partial(lambda j, i, part: (j, part, 0), part=part))
           for part in range(3)],
        out_specs=[pl.BlockSpec((1, tile, D_MODEL), lambda j, i: (j, i, 0)),
                   pl.BlockSpec((2, 1, half_rows, D_MODEL), lambda j, i: (0, j, 0, 0))],
        out_shape=[jax.ShapeDtypeStruct((N_CHIPS, S, D_MODEL), BF16),
                   jax.ShapeDtypeStruct((2, N_CHIPS, half_rows, D_MODEL), BF16)],
        scratch_shapes=[pltpu.VMEM((FF_CHUNK, D_MODEL), F32)] * 3,
        compiler_params=_params(("arbitrary", "arbitrary"), 56),
    )(d_out, n, gate, up, group, group, group)


def _mix_in(h, gain, w_in, rc, rs1, rs2, name):
    S = h.shape[0]
    tile = min(TOK_TILE, S)

    def body(h_ref, g_ref, w_ref, c_ref, s1_ref, s2_ref, n_ref, q_ref, k_ref, v_ref, pc_ref):
        _, xh = _rms_stats(h_ref[...])
        n = (xh * g_ref[...]).astype(BF16)
        n_ref[...] = n
        u = _dot_nt(n, w_ref[...])
        c, s1, s2 = c_ref[...], s1_ref[...], s2_ref[...]
        q_ref[...] = jnp.concatenate([(_rope(u[:, hd * LANES:(hd + 1) * LANES], c, s1, s2) * SCALE).astype(BF16)
                                      for hd in range(N_HEADS)], axis=1)
        k_ref[...] = _rope(u[:, Q_PAD:Q_PAD + KV_WIDTH], c, s1, s2).astype(BF16)
        v_ref[...] = u[:, Q_PAD + KV_WIDTH:Q_PAD + 2 * KV_WIDTH].astype(BF16)
        pc_ref[...] = u[:, Q_PAD + 2 * KV_WIDTH:]

    return pl.pallas_call(
        body, name=name, grid=(S // tile,),
        in_specs=[_rows(tile, D_MODEL), _full((1, D_MODEL)), _full((U_PAD, D_MODEL)),
                  _rows(tile, LANES), _rows(tile, LANES), _rows(tile, LANES)],
        out_specs=[_rows(tile, D_MODEL), _rows(tile, Q_PAD), _rows(tile, KV_WIDTH), _rows(tile, KV_WIDTH),
                   _rows(tile, POOL_WIDTH)],
        out_shape=[jax.ShapeDtypeStruct((S, D_MODEL), BF16), jax.ShapeDtypeStruct((S, Q_PAD), BF16),
                   jax.ShapeDtypeStruct((S, KV_WIDTH), BF16), jax.ShapeDtypeStruct((S, KV_WIDTH), BF16),
                   jax.ShapeDtypeStruct((S, POOL_WIDTH), F32)],
        compiler_params=_params(("parallel",), 40),
    )(h, gain, w_in, rc, rs1, rs2)


def _band_start(i, S):
    return pl.multiple_of(jnp.clip((i - 1) * BLOCK, 0, S - BAND), BLOCK)


def _window_bias(off):
    r = lax.broadcasted_iota(jnp.int32, (BLOCK, 1), 0)
    c = lax.broadcasted_iota(jnp.int32, (1, BAND), 1)
    return jnp.where(jnp.abs(off + r - c) <= WINDOW, 0.0, NEG).astype(F32)


def _softmax_parts(qh, kb, bias, sink_h):
    s = _dot_nt(qh, kb) + bias
    m = jnp.maximum(jnp.max(s, axis=-1, keepdims=True), sink_h)
    p = jnp.exp(s - m)
    es = jnp.exp(sink_h - m)
    return p, es, 1.0 / (jnp.sum(p, axis=-1, keepdims=True) + es)


def _pool_matrix(t0, start, S, w):
    r = lax.broadcasted_iota(jnp.int32, (BLOCK, 1), 0) + t0
    c = lax.broadcasted_iota(jnp.int32, (1, BAND), 1) + start
    half = w // 2

    def window(lo, hi):
        a = jnp.maximum(lo, 0)
        b = jnp.minimum(hi + 1, S)
        return jnp.where((c >= a) & (c < b), 1.0 / (b - a).astype(F32), 0.0)

    return (0.5 * (window(r - half, r + half - 1) + window(r - half + 1, r + half))).astype(BF16)


def _pool_matrices(S):
    blocks = ((0, 0), (BLOCK, 0), (S - BLOCK, S - BAND))
    return jnp.stack([jnp.stack([_pool_matrix(t0, start, S, w) for w in POOL_WINDOWS]) for t0, start in blocks])


def _pool_spec(nb):
    return pl.BlockSpec((1, N_POOL, BLOCK, BAND), lambda i, *_: (jnp.where(i == 0, 0, jnp.where(i == nb - 1, 2, 1)), 0, 0, 0))


def _mix_core_fwd(q, k, v, pc, sink, pool_m, pool_w, pool_scale, name):
    S = q.shape[0]
    nb = S // BLOCK

    def body(sink_ref, q_ref, k_ref, v_ref, pc_ref, pm_ref, pw_ref, ps_ref, a_ref, p_ref):
        i = pl.program_id(0)
        start = _band_start(i, S)
        band = pl.ds(start, BAND)
        bias = _window_bias(i * BLOCK - start)
        kb, vb = k_ref[band, :], v_ref[band, :]
        hs = range(N_HEADS)
        ss = [_dot_nt(q_ref[:, hd * LANES:(hd + 1) * LANES], kb) + bias for hd in hs]
        ms = [jnp.maximum(jnp.max(ss[hd], axis=-1, keepdims=True), sink_ref[0, hd]) for hd in hs]
        ps = [jnp.exp(ss[hd] - ms[hd]) for hd in hs]
        invs = [1.0 / (jnp.sum(ps[hd], axis=-1, keepdims=True) + jnp.exp(sink_ref[0, hd] - ms[hd])) for hd in hs]
        outs = [_dot(ps[hd].astype(BF16), vb) for hd in hs]
        a_ref[...] = jnp.concatenate([(outs[hd] * invs[hd]).astype(BF16) for hd in hs], axis=1)
        centre = pl.ds(pl.multiple_of(i * BLOCK, BLOCK), BLOCK)
        gs = range(N_POOL)
        sl = [slice(g * POOL_GROUP, (g + 1) * POOL_GROUP) for g in gs]
        means = [_dot(pm_ref[0, g], pc_ref[band, sl[g]].astype(BF16)) for g in gs]
        devs = [(means[g] - pc_ref[centre, sl[g]]).astype(BF16) for g in gs]
        p_ref[...] = (jnp.concatenate([_dot(devs[g], pw_ref[g]) for g in gs], axis=1) * ps_ref[...]).astype(BF16)

    return pl.pallas_call(
        body, name=name, grid=(nb,),
        in_specs=[pl.BlockSpec(memory_space=pltpu.SMEM), _rows(BLOCK, Q_PAD), _full((S, KV_WIDTH)), _full((S, KV_WIDTH)),
                  _full((S, POOL_WIDTH)), _pool_spec(nb), _full((N_POOL, POOL_GROUP, POOL_GROUP)), _full((1, POOL_WIDTH))],
        out_specs=[_rows(BLOCK, Q_PAD), _rows(BLOCK, POOL_WIDTH)],
        out_shape=[jax.ShapeDtypeStruct((S, Q_PAD), BF16), jax.ShapeDtypeStruct((S, POOL_WIDTH), BF16)],
        compiler_params=_params(("parallel",), 40),
    )(sink, q, k, v, pc, pool_m, pool_w, pool_scale)


def _mix_core_bwd(q, k, v, pc, da, dp, sink, pool_m, pool_w, pool_scale, rc, rs1, rs2, name, rider=None):
    S = q.shape[0]
    nb = S // BLOCK

    def body(sink_ref, q_ref, k_ref, v_ref, pc_ref, da_ref, dp_ref, pm_ref, pw_ref, ps_ref, c_ref, s1_ref, s2_ref,
             dq_ref, dk_ref, dv_ref, dpc_ref, dsink_ref, dpw_ref, dps_ref):
        i = pl.program_id(0)

        @pl.when(i == 0)
        def _():
            dk_ref[...] = jnp.zeros_like(dk_ref)
            dv_ref[...] = jnp.zeros_like(dv_ref)
            dpc_ref[...] = jnp.zeros_like(dpc_ref)
            dsink_ref[...] = jnp.zeros_like(dsink_ref)
            dpw_ref[...] = jnp.zeros_like(dpw_ref)
            dps_ref[...] = jnp.zeros_like(dps_ref)

        start = _band_start(i, S)
        band = pl.ds(start, BAND)
        bias = _window_bias(i * BLOCK - start)
        kb, vb = k_ref[band, :], v_ref[band, :]
        c, s1, s2 = c_ref[...], s1_ref[...], s2_ref[...]
        lane = lax.broadcasted_iota(jnp.int32, (1, LANES), 1)
        hs = range(N_HEADS)
        qs = [q_ref[:, hd * LANES:(hd + 1) * LANES] for hd in hs]
        das = [da_ref[:, hd * LANES:(hd + 1) * LANES] for hd in hs]
        ss = [_dot_nt(qs[hd], kb) + bias for hd in hs]
        d_probs = [_dot_nt(das[hd], vb) for hd in hs]
        ms = [jnp.maximum(jnp.max(ss[hd], axis=-1, keepdims=True), sink_ref[0, hd]) for hd in hs]
        ps = [jnp.exp(ss[hd] - ms[hd]) for hd in hs]
        ess = [jnp.exp(sink_ref[0, hd] - ms[hd]) for hd in hs]
        invs = [1.0 / (jnp.sum(ps[hd], axis=-1, keepdims=True) + ess[hd]) for hd in hs]
        probs = [ps[hd] * invs[hd] for hd in hs]
        deltas = [jnp.sum(probs[hd] * d_probs[hd], axis=-1, keepdims=True) for hd in hs]
        d_ss = [(probs[hd] * (d_probs[hd] - deltas[hd])).astype(BF16) for hd in hs]
        dqs = [_dot(d_ss[hd], kb) for hd in hs]
        dq_ref[...] = jnp.concatenate([_rope_bwd(dqs[hd] * SCALE, c, s1, s2).astype(BF16) for hd in hs], axis=1)
        dks = [_dot_tn(d_ss[hd], qs[hd]) for hd in hs]
        dvs = [_dot_tn(probs[hd].astype(BF16), das[hd]) for hd in hs]
        dk_ref[band, :] += functools.reduce(lambda a, b: a + b, dks)
        dv_ref[band, :] += functools.reduce(lambda a, b: a + b, dvs)
        dsink_ref[...] += functools.reduce(lambda a, b: a + b, [
            jnp.where(lane == hd, -jnp.sum(ess[hd] * invs[hd] * deltas[hd], axis=0, keepdims=True), 0.0) for hd in hs])

        centre = pl.ds(pl.multiple_of(i * BLOCK, BLOCK), BLOCK)
        gs = range(N_POOL)
        sl = [slice(g * POOL_GROUP, (g + 1) * POOL_GROUP) for g in gs]
        devs = [(_dot(pm_ref[0, g], pc_ref[band, sl[g]].astype(BF16)) - pc_ref[centre, sl[g]]).astype(BF16) for g in gs]
        dys = [dp_ref[:, sl[g]].astype(F32) for g in gs]
        zs = [_dot(devs[g], pw_ref[g]) for g in gs]
        dzs = [(dys[g] * ps_ref[:, sl[g]]).astype(BF16) for g in gs]
        d_devs = [_dot_nt(dzs[g], pw_ref[g]) for g in gs]
        dps_ref[...] += jnp.concatenate([jnp.sum(dys[g] * zs[g], axis=0, keepdims=True) for g in gs], axis=1)
        for g in gs:
            dpw_ref[g] += _dot_tn(devs[g], dzs[g])
        dpc_ref[band, :] += jnp.concatenate([_dot_tn(pm_ref[0, g], d_devs[g].astype(BF16)) for g in gs], axis=1)
        dpc_ref[centre, :] -= jnp.concatenate(d_devs, axis=1)

    return _call(
        body, name, (nb,),
        [pl.BlockSpec(memory_space=pltpu.SMEM), _rows(BLOCK, Q_PAD), _full((S, KV_WIDTH)), _full((S, KV_WIDTH)),
         _full((S, POOL_WIDTH)), _rows(BLOCK, Q_PAD), _rows(BLOCK, POOL_WIDTH), _pool_spec(nb),
         _full((N_POOL, POOL_GROUP, POOL_GROUP)), _full((1, POOL_WIDTH)),
         _rows(BLOCK, LANES), _rows(BLOCK, LANES), _rows(BLOCK, LANES)],
        [_rows(BLOCK, Q_PAD), _full((S, KV_WIDTH)), _full((S, KV_WIDTH)), _full((S, POOL_WIDTH)),
         _full((1, LANES)), _full((N_POOL, POOL_GROUP, POOL_GROUP)), _full((1, POOL_WIDTH))],
        [jax.ShapeDtypeStruct((S, Q_PAD), BF16), jax.ShapeDtypeStruct((S, KV_WIDTH), F32),
         jax.ShapeDtypeStruct((S, KV_WIDTH), F32), jax.ShapeDtypeStruct((S, POOL_WIDTH), F32),
         jax.ShapeDtypeStruct((1, LANES), F32), jax.ShapeDtypeStruct((N_POOL, POOL_GROUP, POOL_GROUP), F32),
         jax.ShapeDtypeStruct((1, POOL_WIDTH), F32)],
        [], 56, (sink, q, k, v, pc, da, dp, pool_m, pool_w, pool_scale, rc, rs1, rs2), rider)


def _mix_out(h, a, p, wa, wp, name):
    S = h.shape[0]
    tile = min(TOK_TILE, S)

    def body(h_ref, a_ref, p_ref, wa_ref, wp_ref, o_ref):
        o_ref[...] = h_ref[...] + _dot(a_ref[...], wa_ref[...]) + _dot(p_ref[...], wp_ref[...])

    return pl.pallas_call(
        body, name=name, grid=(S // tile,),
        in_specs=[_rows(tile, D_MODEL), _rows(tile, Q_PAD), _rows(tile, POOL_WIDTH),
                  _full((Q_PAD, D_MODEL)), _full((POOL_WIDTH, D_MODEL))],
        out_specs=_rows(tile, D_MODEL),
        out_shape=jax.ShapeDtypeStruct((S, D_MODEL), F32),
        compiler_params=_params(("parallel",), 40),
    )(h, a, p, wa, wp)


def _loss_head(h, target, gain, name):
    S = h.shape[0]
    tile = min(TOK_TILE, S)

    def body(h_ref, t_ref, g_ref, dh_ref, dhalf_ref, loss_ref, dg_ref):
        @pl.when(pl.program_id(0) == 0)
        def _():
            loss_ref[...] = jnp.zeros_like(loss_ref)
            dg_ref[...] = jnp.zeros_like(dg_ref)

        g = g_ref[...]
        r, xh = _rms_stats(h_ref[...])
        err = xh * g - t_ref[...]
        loss_ref[...] += (0.5 / D_MODEL) * jnp.sum(err * err, axis=0, keepdims=True)
        dh, dg = _rms_bwd(err * (1.0 / D_MODEL), g, r, xh)
        dg_ref[...] += dg
        dh_ref[...] = dh
        dhalf_ref[...] = (0.5 * dh).astype(BF16)

    return pl.pallas_call(
        body, name=name, grid=(S // tile,),
        in_specs=[_rows(tile, D_MODEL), _rows(tile, D_MODEL), _full((1, D_MODEL))],
        out_specs=[_rows(tile, D_MODEL), _rows(tile, D_MODEL), _full((1, D_MODEL)), _full((1, D_MODEL))],
        out_shape=[jax.ShapeDtypeStruct((S, D_MODEL), F32), jax.ShapeDtypeStruct((S, D_MODEL), BF16),
                   jax.ShapeDtypeStruct((1, D_MODEL), F32), jax.ShapeDtypeStruct((1, D_MODEL), F32)],
        compiler_params=_params(("arbitrary",), 40),
    )(h, target, gain)


def _mix_out_bwd(dh_out, dn, h, gain, a, p, wa, wp, name, rider=None):
    S = h.shape[0]
    tile = min(TOK_TILE, S)

    def body(do_ref, dn_ref, h_ref, g_ref, a_ref, p_ref, wa_ref, wp_ref, dh_ref, da_ref, dp_ref, dwa_ref, dwp_ref, dg_ref):
        @pl.when(pl.program_id(0) == 0)
        def _():
            dwa_ref[...] = jnp.zeros_like(dwa_ref)
            dwp_ref[...] = jnp.zeros_like(dwp_ref)
            dg_ref[...] = jnp.zeros_like(dg_ref)

        r, xh = _rms_stats(h_ref[...])
        dnorm, dg = _rms_bwd(_sum_chunks(dn_ref), g_ref[...], r, xh)
        dh = do_ref[...] + dnorm
        dg_ref[...] += dg
        dh_ref[...] = dh
        dhb = dh.astype(BF16)
        da_ref[...] = _dot_nt(dhb, wa_ref[...]).astype(BF16)
        dp_ref[...] = _dot_nt(dhb, wp_ref[...]).astype(BF16)
        dwa_ref[...] += _dot_tn(a_ref[...], dhb)
        dwp_ref[...] += _dot_tn(p_ref[...], dhb)

    return _call(
        body, name, (S // tile,),
        [_rows(tile, D_MODEL), _chunk_rows(tile), _rows(tile, D_MODEL), _full((1, D_MODEL)),
         _rows(tile, Q_PAD), _rows(tile, POOL_WIDTH), _full((Q_PAD, D_MODEL)), _full((POOL_WIDTH, D_MODEL))],
        [_rows(tile, D_MODEL), _rows(tile, Q_PAD), _rows(tile, POOL_WIDTH),
         _full((Q_PAD, D_MODEL)), _full((POOL_WIDTH, D_MODEL)), _full((1, D_MODEL))],
        [jax.ShapeDtypeStruct((S, D_MODEL), F32), jax.ShapeDtypeStruct((S, Q_PAD), BF16),
         jax.ShapeDtypeStruct((S, POOL_WIDTH), BF16), jax.ShapeDtypeStruct((Q_PAD, D_MODEL), F32),
         jax.ShapeDtypeStruct((POOL_WIDTH, D_MODEL), F32), jax.ShapeDtypeStruct((1, D_MODEL), F32)],
        [], 48, (dh_out, dn, h, gain, a, p, wa, wp), rider)


def _mix_in_bwd(dh_out, h, gain, n, dq, dk, dv, dpc, rc, rs1, rs2, w_in, name, rider=None):
    S = h.shape[0]
    tile = min(TOK_TILE, S)

    def body(do_ref, h_ref, g_ref, n_ref, dq_ref, dk_ref, dv_ref, dpc_ref, c_ref, s1_ref, s2_ref, w_ref,
             dh_ref, dhalf_ref, dw_ref, dg_ref):
        @pl.when(pl.program_id(0) == 0)
        def _():
            dw_ref[...] = jnp.zeros_like(dw_ref)
            dg_ref[...] = jnp.zeros_like(dg_ref)

        dk = _rope_bwd(dk_ref[...], c_ref[...], s1_ref[...], s2_ref[...]).astype(BF16)
        du = jnp.concatenate([dq_ref[...], dk, dv_ref[...].astype(BF16), dpc_ref[...].astype(BF16)], axis=1)
        dn = _dot(du, w_ref[...])
        dw_ref[...] += _dot_tn(du, n_ref[...])
        r, xh = _rms_stats(h_ref[...])
        dnorm, dg = _rms_bwd(dn, g_ref[...], r, xh)
        dh = do_ref[...] + dnorm
        dg_ref[...] += dg
        dh_ref[...] = dh
        dhalf_ref[...] = (0.5 * dh).astype(BF16)

    return _call(
        body, name, (S // tile,),
        [_rows(tile, D_MODEL), _rows(tile, D_MODEL), _full((1, D_MODEL)), _rows(tile, D_MODEL),
         _rows(tile, Q_PAD), _rows(tile, KV_WIDTH), _rows(tile, KV_WIDTH), _rows(tile, POOL_WIDTH),
         _rows(tile, LANES), _rows(tile, LANES), _rows(tile, LANES), _full((U_PAD, D_MODEL))],
        [_rows(tile, D_MODEL), _rows(tile, D_MODEL), _full((U_PAD, D_MODEL)), _full((1, D_MODEL))],
        [jax.ShapeDtypeStruct((S, D_MODEL), F32), jax.ShapeDtypeStruct((S, D_MODEL), BF16),
         jax.ShapeDtypeStruct((U_PAD, D_MODEL), F32), jax.ShapeDtypeStruct((1, D_MODEL), F32)],
        [], 56, (dh_out, h, gain, n, dq, dk, dv, dpc, rc, rs1, rs2, w_in), rider)


def _norm_bwd(dh_out, dn, h, gain, name):
    S = h.shape[0]
    tile = min(TOK_TILE, S)

    def body(do_ref, dn_ref, h_ref, g_ref, dh_ref, dg_ref):
        @pl.when(pl.program_id(0) == 0)
        def _():
            dg_ref[...] = jnp.zeros_like(dg_ref)

        r, xh = _rms_stats(h_ref[...])
        dnorm, dg = _rms_bwd(_sum_chunks(dn_ref), g_ref[...], r, xh)
        dg_ref[...] += dg
        dh_ref[...] = do_ref[...] + dnorm

    return pl.pallas_call(
        body, name=name, grid=(S // tile,),
        in_specs=[_rows(tile, D_MODEL), _chunk_rows(tile), _rows(tile, D_MODEL), _full((1, D_MODEL))],
        out_specs=[_rows(tile, D_MODEL), _full((1, D_MODEL))],
        out_shape=[jax.ShapeDtypeStruct((S, D_MODEL), F32), jax.ShapeDtypeStruct((1, D_MODEL), F32)],
        compiler_params=_params(("arbitrary",), 40),
    )(dh_out, dn, h, gain)


def _rope_tables(S):
    half = ROTARY_DIM // 2
    inv_freq = ROPE_THETA ** (-jnp.arange(0, ROTARY_DIM, 2, dtype=F32) / ROTARY_DIM)
    dim = jnp.arange(LANES) % HEAD_DIM
    ang = jnp.arange(S, dtype=F32)[:, None] * inv_freq[dim % half][None, :]
    lo, hi = (dim < half)[None, :], ((dim >= half) & (dim < ROTARY_DIM))[None, :]
    c = jnp.where(lo | hi, jnp.cos(ang), 1.0)
    s1 = jnp.where(lo, -jnp.sin(ang), 0.0)
    s2 = jnp.where(hi, jnp.sin(ang), 0.0)
    return c, s1, s2


def _pad_heads(w, axis):
    w = jnp.moveaxis(w, axis, 0)
    heads = w.reshape((N_HEADS, HEAD_DIM) + w.shape[1:])
    zero = jnp.zeros_like(heads)
    first = (jnp.arange(N_HEADS) < Q_PER_KV).reshape((N_HEADS, 1) + (1,) * (w.ndim - 1))
    lo = jnp.where(first, heads, zero)
    hi = jnp.where(first, zero, heads)
    padded = jnp.concatenate([lo, hi], axis=1).reshape((Q_PAD,) + w.shape[1:])
    return jnp.moveaxis(padded, 0, axis)


def _unpad_heads(w, axis):
    w = jnp.moveaxis(w, axis, 0)
    groups = w.reshape((N_HEADS, 2, HEAD_DIM) + w.shape[1:])
    first = (jnp.arange(N_HEADS) < Q_PER_KV).reshape((N_HEADS, 1) + (1,) * (w.ndim - 1))
    heads = jnp.where(first, groups[:, 0], groups[:, 1]).reshape((ATTN_WIDTH,) + w.shape[1:])
    return jnp.moveaxis(heads, 0, axis)


IN_ROWS = IN_WIDTH // N_CHIPS
OUT_ROWS = (ATTN_WIDTH + POOL_WIDTH) // N_CHIPS
MIX_ROWS = IN_ROWS + OUT_ROWS
FFN_ROWS = 3 * FF_CHUNK


def _reduce_tail(place, parts, small):
    tags = [str(p.shape[2]) for p in parts]
    *received, small_all = _comm_call(_merge(_sibling_exchange(parts), _small_allgather(small)), "grad_exchange_tail")
    pairs = [_pair_sum(place, p, r, "grad_pair_sum_tail_" + t) for t, p, r in zip(tags, parts, received)]
    small_sum = _sum_leading(small_all, 1, "small_grad_sum")
    stacks = _comm_call(_scatter(pairs), "grad_scatter_tail")
    reduced = [_chip_sum(place, p, s, 2, "grad_chip_sum_tail_" + t) for t, p, s in zip(tags, pairs, stacks)]
    return _comm_call(_sibling_share(reduced), "grad_share_tail"), small_sum


def _step(x, target, bufs, small, place):
    S = x.shape[0]
    rc, rs1, rs2 = _rope_tables(S)
    (ffn1,) = _comm_call(_allgather(bufs[:1]), "allgather_ffn1")
    ffn1 = ffn1.reshape(N_CHIPS, FFN_ROWS, D_MODEL)
    h1, n1, gate1, up1, mix, ffn2 = _ffn_fwd(x, small["ffn1_norm"], ffn1, "ffn1_fwd", _allgather(bufs[1:]))
    mix, ffn2 = mix.reshape(N_CHIPS, MIX_ROWS, D_MODEL), ffn2.reshape(N_CHIPS, FFN_ROWS, D_MODEL)
    w_in_t = mix[:, :IN_ROWS].reshape(IN_WIDTH, D_MODEL)
    w_in_pad = jnp.concatenate([_pad_heads(w_in_t[:ATTN_WIDTH], 0), w_in_t[ATTN_WIDTH:]], axis=0)
    w_out = mix[:, IN_ROWS:].reshape(ATTN_WIDTH + POOL_WIDTH, D_MODEL)
    wa = _pad_heads(w_out[:ATTN_WIDTH], 0)
    wp = w_out[ATTN_WIDTH:]
    pool_w = small["pool_w"].astype(BF16)

    n2, q, k, v, pc = _mix_in(h1, small["mix_norm"], w_in_pad, rc, rs1, rs2, "mix_in")
    pool_m = _pool_matrices(S)
    a, p = _mix_core_fwd(q, k, v, pc, small["sink_logits"], pool_m, pool_w, small["pool_scale"], "mix_core_fwd")
    h2 = _mix_out(h1, a, p, wa, wp, "mix_out")
    h3, n3, gate2, up2 = _ffn_fwd(h2, small["ffn2_norm"], ffn2, "ffn2_fwd")
    dh3, dhalf3, loss_lanes, d_final = _loss_head(h3, target, small["final_norm"], "loss_head")

    dn3, d_ffn2 = _ffn_bwd(dhalf3, n3, gate2, up2, ffn2, "ffn2_bwd")
    dh2, da, dp, dwa, dwp, d_ffn2_norm, received = _mix_out_bwd(dh3, dn3, h2, small["ffn2_norm"], a, p, wa, wp, "mix_out_bwd",
                                                                _sibling_exchange([d_ffn2]))
    pair = _pair_sum(place, d_ffn2, received, "grad_pair_sum_ffn2")
    dq, dk, dv, dpc, dsink, dpool_w, dpool_scale, stack = _mix_core_bwd(
        q, k, v, pc, da, dp, small["sink_logits"], pool_m, pool_w, small["pool_scale"], rc, rs1, rs2, "mix_core_bwd",
        _scatter([pair]))
    reduced = _chip_sum(place, pair, stack, 2, "grad_chip_sum_ffn2")
    dh1, dhalf1, dw_in_pad, d_mix_norm, g_ffn2 = _mix_in_bwd(dh2, h1, small["mix_norm"], n2, dq, dk, dv, dpc, rc, rs1, rs2,
                                                             w_in_pad, "mix_in_bwd", _sibling_share([reduced]))
    dn1, d_ffn1 = _ffn_bwd(dhalf1, n1, gate1, up1, ffn1, "ffn1_bwd")
    grad_x, d_ffn1_norm = _norm_bwd(dh1, dn1, x, small["ffn1_norm"], "norm1_bwd")

    dw_in_t = jnp.concatenate([_unpad_heads(dw_in_pad[:Q_PAD], 0), dw_in_pad[Q_PAD:]], axis=0)
    dw_out = jnp.concatenate([_unpad_heads(dwa, 0), dwp], axis=0)
    d_mix = jnp.concatenate([dw_in_t.reshape(N_CHIPS, IN_ROWS, D_MODEL), dw_out.reshape(N_CHIPS, OUT_ROWS, D_MODEL)], axis=1)
    d_mix = jnp.transpose(d_mix.reshape(N_CHIPS, 2, MIX_ROWS // 2, D_MODEL), (1, 0, 2, 3)).astype(BF16)
    small_g = {"ffn1_norm": d_ffn1_norm, "mix_norm": d_mix_norm, "ffn2_norm": d_ffn2_norm, "final_norm": d_final,
               "pool_scale": dpool_scale, "sink_logits": dsink[:, :N_HEADS], "pool_w": dpool_w}
    (g_ffn1, g_mix), small_sum = _reduce_tail(place, [d_ffn1, d_mix], _pack_small(small_g))
    return loss_lanes, grad_x, [g.reshape(-1, D_MODEL) for g in (g_ffn1, g_mix, g_ffn2)], small_sum


GROUPS =(("ffn1_w_gate", "ffn1_w_up", "ffn1_w_down"), ("w_in", "w_out"), ("ffn2_w_gate", "ffn2_w_up", "ffn2_w_down"))
TRANSPOSED = ("ffn1_w_gate", "ffn1_w_up", "w_in", "ffn2_w_gate", "ffn2_w_up")


def _place():
    x, y, c = lax.axis_index("x"), lax.axis_index("y"), lax.axis_index("c")
    chips = [(1 - x, y), (x, 1 - y), (1 - x, 1 - y)]
    return x, y, c, chips


def _remote(src, dst, send_sem, recv_sem, to):
    return pltpu.make_async_remote_copy(src_ref=src, dst_ref=dst, send_sem=send_sem, recv_sem=recv_sem,
                                        device_id=to, device_id_type=MESH)


def _pack(chip, members, name):
    rows = members[0].shape[0]
    n = len(members)

    def body(chip_ref, *refs):
        out_ref = refs[n]
        for k in range(n):
            @pl.when(pl.program_id(0) == k)
            def _(k=k):
                out_ref[0] = refs[k][...].astype(BF16)

    return pl.pallas_call(
        body, name=name,
        grid_spec=pltpu.PrefetchScalarGridSpec(
            num_scalar_prefetch=1, grid=(n,),
            in_specs=[pl.BlockSpec((rows, D_MODEL), lambda k, chip_ref: (0, 0))] * n,
            out_specs=pl.BlockSpec((1, rows, D_MODEL), lambda k, chip_ref: (chip_ref[0], k, 0))),
        out_shape=jax.ShapeDtypeStruct((N_CHIPS, n * rows, D_MODEL), BF16),
        compiler_params=_params(("arbitrary",), 40),
    )(chip, *members)


def _same(arrays):
    return [jax.ShapeDtypeStruct(a.shape, a.dtype) for a in arrays]


def _allgather(bufs):
    n = len(bufs)

    def copies(outs, send_sems, recv_sems, started_only=False):
        x, y, c, chips = _place()
        me = 2 * x + y
        slots = [2 * cx + cy for cx, cy in chips]
        first = [[_remote(outs[a].at[me, c], outs[a].at[me, c], send_sems.at[6 * a + k], recv_sems.at[6 * a + k], (*chips[k], c))
                  for k in range(3)] for a in range(n)]
        if started_only:
            return first
        passed = [[_remote(outs[a].at[slots[k], c], outs[a].at[slots[k], c], send_sems.at[6 * a + 3 + k],
                           recv_sems.at[6 * a + 3 + k], (x, y, 1 - c)) for k in range(3)] for a in range(n)]
        landed = [[_remote(outs[a].at[me, c], outs[a].at[slots[k], c], send_sems.at[6 * a + k], recv_sems.at[6 * a + k],
                           (*chips[k], c)) for k in range(3)] for a in range(n)]
        handed = [[_remote(outs[a].at[me, c], outs[a].at[slots[k], 1 - c], send_sems.at[6 * a + 3 + k],
                           recv_sems.at[6 * a + 3 + k], (x, y, 1 - c)) for k in range(3)] for a in range(n)]
        return first, passed, landed, handed

    def start(ins, outs, sems):
        for per_buf in copies(outs, *sems, started_only=True):
            for cp in per_buf:
                cp.start()

    def finish(ins, outs, sems):
        first, passed, landed, handed = copies(outs, *sems)
        for a in range(n):
            for k in range(3):
                landed[a][k].wait_recv()
                passed[a][k].start()
        for a in range(n):
            for k in range(3):
                handed[a][k].wait_recv()
        for a in range(n):
            for cp in first[a] + passed[a]:
                cp.wait_send()

    return _Rider(list(bufs), _same(bufs), {a: a for a in range(n)},
                  [pltpu.SemaphoreType.DMA((6 * n,)), pltpu.SemaphoreType.DMA((6 * n,))], start, finish)


def _sibling_exchange(parts):
    n = len(parts)

    def copies(ins, outs, send_sems, recv_sems):
        x, y, c, _ = _place()
        return [_remote(ins[a].at[1 - c], outs[a], send_sems.at[a], recv_sems.at[a], (x, y, 1 - c)) for a in range(n)]

    def start(ins, outs, sems):
        for cp in copies(ins, outs, *sems):
            cp.start()

    def finish(ins, outs, sems):
        for cp in copies(ins, outs, *sems):
            cp.wait_recv()
            cp.wait_send()

    return _Rider(list(parts), [jax.ShapeDtypeStruct(p.shape[1:], p.dtype) for p in parts], {},
                  [pltpu.SemaphoreType.DMA((n,)), pltpu.SemaphoreType.DMA((n,))], start, finish)


def _small_allgather(small):
    flips = [(fx, fy, fc) for fx in range(2) for fy in range(2) for fc in range(2)][1:]

    def copies(small_ref, gather_ref, send_sems, recv_sems, local_sem, started_only=False):
        x, y, c, _ = _place()
        me = 4 * x + 2 * y + c
        peers = [((1 - x) if fx else x, (1 - y) if fy else y, (1 - c) if fc else c) for fx, fy, fc in flips]
        own = pltpu.make_async_copy(small_ref, gather_ref.at[me], local_sem)
        sent = [_remote(small_ref, gather_ref.at[me], send_sems.at[k], recv_sems.at[k], peer) for k, peer in enumerate(peers)]
        if started_only:
            return own, sent
        landed = [_remote(small_ref, gather_ref.at[4 * px + 2 * py + pc], send_sems.at[k], recv_sems.at[k], (px, py, pc))
                  for k, (px, py, pc) in enumerate(peers)]
        return own, sent, landed

    def start(ins, outs, sems):
        own, sent = copies(ins[0], outs[0], *sems, started_only=True)
        own.start()
        for cp in sent:
            cp.start()

    def finish(ins, outs, sems):
        own, sent, landed = copies(ins[0], outs[0], *sems)
        for cp in landed:
            cp.wait_recv()
        for cp in sent:
            cp.wait_send()
        own.wait()

    return _Rider([small], [jax.ShapeDtypeStruct((2 * N_CHIPS,) + small.shape, small.dtype)], {},
                  [pltpu.SemaphoreType.DMA((7,)), pltpu.SemaphoreType.DMA((7,)), pltpu.SemaphoreType.DMA], start, finish)


def _merge(a, b):
    na, nao, nas = len(a.operands), len(a.out_shapes), len(a.scratch)

    def start(ins, outs, sems):
        a.start(ins[:na], outs[:nao], sems[:nas])
        b.start(ins[na:], outs[nao:], sems[nas:])

    def finish(ins, outs, sems):
        a.finish(ins[:na], outs[:nao], sems[:nas])
        b.finish(ins[na:], outs[nao:], sems[nas:])

    aliases = {**a.aliases, **{na + k: nao + v for k, v in b.aliases.items()}}
    return _Rider(a.operands + b.operands, a.out_shapes + b.out_shapes, aliases, a.scratch + b.scratch, start, finish)


def _scatter(sums):
    n = len(sums)

    def copies(ins, outs, send_sems, recv_sems, started_only=False):
        x, y, c, chips = _place()
        me = 2 * x + y
        slots = [2 * cx + cy for cx, cy in chips]
        sent = [_remote(ins[a].at[slots[k]], outs[a].at[me], send_sems.at[3 * a + k], recv_sems.at[3 * a + k], (*chips[k], c))
                for a in range(n) for k in range(3)]
        if started_only:
            return sent
        landed = [_remote(ins[a].at[slots[k]], outs[a].at[slots[k]], send_sems.at[3 * a + k], recv_sems.at[3 * a + k],
                          (*chips[k], c)) for a in range(n) for k in range(3)]
        return sent, landed

    def start(ins, outs, sems):
        for cp in copies(ins, outs, *sems, started_only=True):
            cp.start()

    def finish(ins, outs, sems):
        sent, landed = copies(ins, outs, *sems)
        for cp in landed:
            cp.wait_recv()
        for cp in sent:
            cp.wait_send()

    return _Rider(list(sums), _same(sums), {}, [pltpu.SemaphoreType.DMA((3 * n,)), pltpu.SemaphoreType.DMA((3 * n,))],
                  start, finish)


def _sibling_share(bufs):
    n = len(bufs)

    def copies(outs, send_sems, recv_sems, started_only=False):
        x, y, c, _ = _place()
        sent = [_remote(outs[a].at[c], outs[a].at[c], send_sems.at[a], recv_sems.at[a], (x, y, 1 - c)) for a in range(n)]
        if started_only:
            return sent
        landed = [_remote(outs[a].at[c], outs[a].at[1 - c], send_sems.at[a], recv_sems.at[a], (x, y, 1 - c)) for a in range(n)]
        return sent, landed

    def start(ins, outs, sems):
        for cp in copies(outs, *sems, started_only=True):
            cp.start()

    def finish(ins, outs, sems):
        sent, landed = copies(outs, *sems)
        for cp in landed:
            cp.wait_recv()
        for cp in sent:
            cp.wait_send()

    return _Rider(list(bufs), _same(bufs), {a: a for a in range(n)},
                  [pltpu.SemaphoreType.DMA((n,)), pltpu.SemaphoreType.DMA((n,))], start, finish)


def _pair_sum(core, part, received, name):
    _, _, rh, cols = part.shape

    def body(core_ref, p_ref, r_ref, o_ref):
        o_ref[...] = (p_ref[0].astype(F32) + r_ref[...].astype(F32)).astype(BF16)

    return pl.pallas_call(
        body, name=name,
        grid_spec=pltpu.PrefetchScalarGridSpec(
            num_scalar_prefetch=1, grid=(N_CHIPS,),
            in_specs=[pl.BlockSpec((1, 1, rh, cols), lambda j, core_ref: (core_ref[0], j, 0, 0)),
                      pl.BlockSpec((1, rh, cols), lambda j, core_ref: (j, 0, 0))],
            out_specs=pl.BlockSpec((1, rh, cols), lambda j, core_ref: (j, 0, 0))),
        out_shape=jax.ShapeDtypeStruct((N_CHIPS, rh, cols), BF16),
        compiler_params=_params(("parallel",), 32),
    )(core, part, received)


def _sum_leading(stack, steps, name):
    k, rows, cols = stack.shape
    tile = rows // steps

    def body(s_ref, o_ref):
        total = s_ref[0].astype(F32)
        for d in range(1, k):
            total = total + s_ref[d].astype(F32)
        o_ref[...] = total

    return pl.pallas_call(
        body, name=name, grid=(steps,),
        in_specs=[pl.BlockSpec((k, tile, cols), lambda i: (0, i, 0))],
        out_specs=pl.BlockSpec((tile, cols), lambda i: (i, 0)),
        out_shape=jax.ShapeDtypeStruct((rows, cols), F32),
        compiler_params=_params(("parallel",), 32),
    )(stack)


def _chip_sum(place, own, stack, steps, name):
    k, rows, cols = stack.shape
    tile = rows // steps

    def body(place_ref, own_ref, *refs):
        chip = place_ref[1]
        total = None
        for d in range(k):
            term = jnp.where(chip == d, own_ref[0], refs[d][0]).astype(F32)
            total = term if total is None else total + term
        refs[k][0] = total

    def other(d):
        return lambda i, place_ref: (jnp.where(place_ref[1] == d, (d + 1) % k, d), i, 0)

    return pl.pallas_call(
        body, name=name,
        grid_spec=pltpu.PrefetchScalarGridSpec(
            num_scalar_prefetch=1, grid=(steps,),
            in_specs=[pl.BlockSpec((1, tile, cols), lambda i, place_ref: (place_ref[1], i, 0))]
            + [pl.BlockSpec((1, tile, cols), other(d)) for d in range(k)],
            out_specs=pl.BlockSpec((1, tile, cols), lambda i, place_ref: (place_ref[0], i, 0))),
        out_shape=jax.ShapeDtypeStruct((2, rows, cols), F32),
        compiler_params=_params(("arbitrary",), 32),
    )(place, own, *([stack] * k))


def _adamw(w, g, row0, m, v, tile, name):
    rows, cols = w.shape
    first = row0 // tile
    assert rows % tile == 0 and row0 % tile == 0
    bc1 = 1.0 - ADAM_B1 ** ADAM_STEP
    bc2 = 1.0 - ADAM_B2 ** ADAM_STEP

    def body(w_ref, g_ref, m_ref, v_ref, go_ref, d_ref, mo_ref, vo_ref):
        g = g_ref[...]
        m_new = ADAM_B1 * m_ref[...] + (1.0 - ADAM_B1) * g
        v_new = ADAM_B2 * v_ref[...] + (1.0 - ADAM_B2) * (g * g)
        go_ref[...] = g
        d_ref[...] = -ADAM_LR * ((m_new / bc1) / (jnp.sqrt(v_new / bc2) + ADAM_EPS) + ADAM_WD * w_ref[...])
        mo_ref[...] = m_new
        vo_ref[...] = v_new

    spec = pl.BlockSpec((tile, cols), lambda i: (i, 0))
    g_spec = pl.BlockSpec((tile, cols), lambda i: (first + i, 0))
    return pl.pallas_call(
        body, name=name, grid=(rows // tile,),
        in_specs=[spec, g_spec, spec, spec], out_specs=[spec] * 4,
        out_shape=[jax.ShapeDtypeStruct((rows, cols), F32)] * 4,
        compiler_params=_params(("parallel",), 32),
    )(w, g, m, v)


SMALL = ("ffn1_norm", "mix_norm", "ffn2_norm", "final_norm", "pool_scale", "sink_logits", "pool_w")


def _pack_small(d):
    sink = jnp.pad(d["sink_logits"].reshape(1, N_HEADS), ((0, 0), (0, LANES - N_HEADS)))
    rows = [d[n].reshape(-1, LANES) for n in SMALL[:5]] + [sink, d["pool_w"].reshape(-1, LANES)]
    packed = jnp.concatenate(rows, axis=0)
    return jnp.pad(packed, ((0, SMALL_ROWS - packed.shape[0]), (0, 0)))


def _unpack_small(packed, like):
    out, row = {}, 0
    for n in SMALL:
        size = LANES if n == "sink_logits" else math.prod(like[n].shape)
        chunk = packed[row:row + size // LANES].reshape(-1)
        out[n] = (chunk[:N_HEADS] if n == "sink_logits" else chunk).reshape(like[n].shape)
        row += size // LANES
    return out


def kernel(x, ffn1_norm, ffn1_w_gate, ffn1_w_up, ffn1_w_down, mix_norm, w_in, sink_logits, pool_w, pool_scale, w_out, ffn2_norm, ffn2_w_gate, ffn2_w_up, ffn2_w_down, final_norm, loss_target, m_ffn1_norm, m_ffn1_w_gate, m_ffn1_w_up, m_ffn1_w_down, m_mix_norm, m_w_in, m_sink_logits, m_pool_w, m_pool_scale, m_w_out, m_ffn2_norm, m_ffn2_w_gate, m_ffn2_w_up, m_ffn2_w_down, m_final_norm, v_ffn1_norm, v_ffn1_w_gate, v_ffn1_w_up, v_ffn1_w_down, v_mix_norm, v_w_in, v_sink_logits, v_pool_w, v_pool_scale, v_w_out, v_ffn2_norm, v_ffn2_w_gate, v_ffn2_w_up, v_ffn2_w_down, v_final_norm):
    names = ("ffn1_norm", "ffn1_w_gate", "ffn1_w_up", "ffn1_w_down", "mix_norm", "w_in", "sink_logits", "pool_w",
             "pool_scale", "w_out", "ffn2_norm", "ffn2_w_gate", "ffn2_w_up", "ffn2_w_down", "final_norm")
    weights = dict(zip(names, (ffn1_norm, ffn1_w_gate, ffn1_w_up, ffn1_w_down, mix_norm, w_in, sink_logits, pool_w,
                               pool_scale, w_out, ffn2_norm, ffn2_w_gate, ffn2_w_up, ffn2_w_down, final_norm)))
    mom1 = dict(zip(names, (m_ffn1_norm, m_ffn1_w_gate, m_ffn1_w_up, m_ffn1_w_down, m_mix_norm, m_w_in, m_sink_logits,
                            m_pool_w, m_pool_scale, m_w_out, m_ffn2_norm, m_ffn2_w_gate, m_ffn2_w_up, m_ffn2_w_down,
                            m_final_norm)))
    mom2 = dict(zip(names, (v_ffn1_norm, v_ffn1_w_gate, v_ffn1_w_up, v_ffn1_w_down, v_mix_norm, v_w_in, v_sink_logits,
                            v_pool_w, v_pool_scale, v_w_out, v_ffn2_norm, v_ffn2_w_gate, v_ffn2_w_up, v_ffn2_w_down,
                            v_final_norm)))
    chip = (2 * lax.axis_index("x") + lax.axis_index("y")).astype(jnp.int32).reshape(1)
    place = jnp.concatenate([lax.axis_index("c").astype(jnp.int32).reshape(1), chip])

    def rows_of(t, n):
        return jnp.swapaxes(t[n][0], 0, 1) if n in TRANSPOSED else t[n][0]

    packed = [_pack(chip, [rows_of(weights, n) for n in GROUPS[0]], "pack_ffn1"),
              _pack(chip, [jnp.concatenate([rows_of(weights, n) for n in GROUPS[1]], axis=0)], "pack_mix"),
              _pack(chip, [rows_of(weights, n) for n in GROUPS[2]], "pack_ffn2")]
    bufs = [p.reshape(N_CHIPS, 2, p.shape[1] // 2, D_MODEL) for p in packed]

    small_w = {"ffn1_norm": ffn1_norm, "mix_norm": mix_norm, "ffn2_norm": ffn2_norm,
               "final_norm": final_norm.reshape(1, D_MODEL), "pool_scale": pool_scale, "sink_logits": sink_logits,
               "pool_w": pool_w[0]}
    loss_lanes, grad_x, group_grads, small_sum = _step(x[0], loss_target[0], bufs, small_w, place)

    out_g, out_d, out_m, out_v = {}, {}, {}, {}
    for members, g in zip(GROUPS, group_grads):
        row0 = 0
        for n in members:
            w = rows_of(weights, n)
            tile = FF_CHUNK // 4 if w.shape[0] == FF_CHUNK else math.gcd(IN_ROWS, OUT_ROWS)
            outs = _adamw(w, g, row0, rows_of(mom1, n), rows_of(mom2, n), tile, "adamw_" + n)
            row0 += w.shape[0]
            for dst, t in zip((out_g, out_d, out_m, out_v), outs):
                dst[n] = (jnp.swapaxes(t, 0, 1) if n in TRANSPOSED else t).reshape(weights[n].shape)
    small_outs = _adamw(_pack_small(weights), small_sum, 0, _pack_small(mom1), _pack_small(mom2), SMALL_ROWS, "adamw_small")
    for dst, packed in zip((out_g, out_d, out_m, out_v), small_outs):
        dst.update(_unpack_small(packed, weights))

    loss = lax.psum(jnp.sum(loss_lanes), ("x", "y", "c"))
    return (loss, grad_x.reshape(x.shape), *[out_g[n] for n in names], *[out_d[n] for n in names],
            *[out_m[n] for n in names], *[out_v[n] for n in names])
```

```python
import collections
import functools
import math

import jax
import jax.numpy as jnp
from jax import lax
from jax.experimental import pallas as pl
from jax.experimental.pallas import tpu as pltpu

F32, BF16 = jnp.float32, jnp.bfloat16
MESH = pl.DeviceIdType.MESH

D_MODEL = 1024
D_FF = 2816
N_CHIPS = 4
FF_CHUNK = D_FF // N_CHIPS
HEAD_DIM = 64
N_HEADS = 8
N_KV = 2
Q_PER_KV = N_HEADS // N_KV
KV_WIDTH = N_KV * HEAD_DIM
ATTN_WIDTH = N_HEADS * HEAD_DIM
POOL_WINDOWS = (2, 4, 8, 16)
N_POOL = len(POOL_WINDOWS)
POOL_GROUP = 128
POOL_WIDTH = N_POOL * POOL_GROUP
IN_WIDTH = ATTN_WIDTH + 2 * KV_WIDTH + POOL_WIDTH
WINDOW = 128
BLOCK = 128
BAND = 3 * BLOCK
ROPE_THETA = 500000.0
ROTARY_DIM = HEAD_DIM // 4
EPS = 1e-6
LANES = 128
Q_PAD = N_HEADS * LANES
U_PAD = Q_PAD + 2 * KV_WIDTH + POOL_WIDTH
SCALE = HEAD_DIM ** -0.5
NEG = -1e30

ADAM_LR, ADAM_B1, ADAM_B2, ADAM_EPS, ADAM_WD, ADAM_STEP = 0.001, 0.9, 0.999, 1e-08, 0.01, 10

V7X_VMEM_BYTES = 64 * 1024 * 1024
TOK_TILE = 512
BWD_TILE = 256
SMALL_ROWS = 552


def _params(sem, vmem_mb):
    return pltpu.CompilerParams(dimension_semantics=sem, vmem_limit_bytes=vmem_mb * 1024 * 1024)


def _dot(a, b):
    return lax.dot_general(a, b, (((1,), (0,)), ((), ())), preferred_element_type=F32)


def _dot_nt(a, b):
    return lax.dot_general(a, b, (((1,), (1,)), ((), ())), preferred_element_type=F32)


def _dot_tn(a, b):
    return lax.dot_general(a, b, (((0,), (0,)), ((), ())), preferred_element_type=F32)


def _rms_stats(h):
    r = lax.rsqrt(jnp.mean(h * h, axis=-1, keepdims=True) + EPS)
    return r, h * r


def _rms_bwd(dn, g, r, xh):
    gd = dn * g
    dh = r * (gd - xh * jnp.mean(gd * xh, axis=-1, keepdims=True))
    return dh, jnp.sum(dn * xh, axis=0, keepdims=True)


def _rope(x, c, s1, s2):
    return x * c + pltpu.roll(x, LANES - ROTARY_DIM // 2, 1) * s1 + pltpu.roll(x, ROTARY_DIM // 2, 1) * s2


def _rope_bwd(d, c, s1, s2):
    return d * c + pltpu.roll(d * s1, ROTARY_DIM // 2, 1) + pltpu.roll(d * s2, LANES - ROTARY_DIM // 2, 1)


def _sum_chunks(ref):
    total = ref[0].astype(F32)
    for j in range(1, N_CHIPS):
        total = total + ref[j].astype(F32)
    return total


def _chunk_rows(tile):
    return pl.BlockSpec((N_CHIPS, tile, D_MODEL), lambda i, *_: (0, i, 0))


def _full(shape):
    nd = len(shape)
    return pl.BlockSpec(shape, lambda *_: (0,) * nd)


def _rows(tile, cols):
    return pl.BlockSpec((tile, cols), lambda i, *_: (i, 0))


HBM_SPEC = pl.BlockSpec(memory_space=pltpu.HBM)

_Rider = collections.namedtuple("_Rider", "operands out_shapes aliases scratch start finish")


def _call(body, name, grid, in_specs, out_specs, out_shape, scratch, vmem_mb, args, rider=None):
    if rider is None:
        return pl.pallas_call(body, name=name, grid=grid, in_specs=in_specs, out_specs=out_specs, out_shape=out_shape,
                              scratch_shapes=scratch, compiler_params=_params(("arbitrary",) * len(grid), vmem_mb))(*args)
    n_in, n_out, n_scr, r_in, r_out = len(in_specs), len(out_specs), len(scratch), len(rider.operands), len(rider.out_shapes)

    def fused(*refs):
        ins, refs = refs[:n_in], refs[n_in:]
        r_ins, refs = refs[:r_in], refs[r_in:]
        outs, refs = refs[:n_out], refs[n_out:]
        r_outs, refs = refs[:r_out], refs[r_out:]
        scr, r_scr = refs[:n_scr], refs[n_scr:]
        ids = [pl.program_id(d) for d in range(len(grid))]
        first = functools.reduce(jnp.logical_and, [i == 0 for i in ids])
        last = functools.reduce(jnp.logical_and, [i == g - 1 for i, g in zip(ids, grid)])

        @pl.when(first)
        def _():
            rider.start(r_ins, r_outs, r_scr)

        body(*ins, *outs, *scr)

        @pl.when(last)
        def _():
            rider.finish(r_ins, r_outs, r_scr)

    return pl.pallas_call(
        fused, name=name, grid=grid,
        in_specs=list(in_specs) + [HBM_SPEC] * r_in, out_specs=list(out_specs) + [HBM_SPEC] * r_out,
        out_shape=list(out_shape) + list(rider.out_shapes),
        input_output_aliases={n_in + k: n_out + v for k, v in rider.aliases.items()},
        scratch_shapes=list(scratch) + list(rider.scratch),
        compiler_params=_params(("arbitrary",) * len(grid), vmem_mb),
    )(*args, *rider.operands)


def _comm_call(rider, name):
    r_in, r_out = len(rider.operands), len(rider.out_shapes)

    def body(*refs):
        r_ins, r_outs, r_scr = refs[:r_in], refs[r_in:r_in + r_out], refs[r_in + r_out:]
        rider.start(r_ins, r_outs, r_scr)
        rider.finish(r_ins, r_outs, r_scr)

    return pl.pallas_call(
        body, name=name, in_specs=[HBM_SPEC] * r_in, out_specs=[HBM_SPEC] * r_out, out_shape=list(rider.out_shapes),
        input_output_aliases=dict(rider.aliases), scratch_shapes=list(rider.scratch),
    )(*rider.operands)


def _ffn_fwd(h, gain, group, name, rider=None):
    S = h.shape[0]
    tile = min(TOK_TILE, S)
    nt = S // tile

    def body(h_ref, g_ref, wg_ref, wu_ref, wd_ref, ho_ref, n_ref, gate_ref, up_ref, acc):
        j = pl.program_id(1)

        @pl.when(j == 0)
        def _():
            _, xh = _rms_stats(h_ref[...])
            n_ref[...] = (xh * g_ref[...]).astype(BF16)
            acc[...] = jnp.zeros_like(acc)

        n = n_ref[...]
        gate = _dot_nt(n, wg_ref[0])
        up = _dot_nt(n, wu_ref[0])
        gate_ref[0] = gate.astype(BF16)
        up_ref[0] = up.astype(BF16)
        act = (gate * jax.nn.sigmoid(gate) * up).astype(BF16)
        acc[...] += _dot(act, wd_ref[0])

        @pl.when(j == N_CHIPS - 1)
        def _():
            ho_ref[...] = h_ref[...] + 0.5 * acc[...]

    tok = pl.BlockSpec((tile, D_MODEL), lambda i, j: (i, 0))
    hid = pl.BlockSpec((1, tile, FF_CHUNK), lambda i, j: (j, i, 0))
    return _call(
        body, name, (nt, N_CHIPS),
        [tok, pl.BlockSpec((1, D_MODEL), lambda i, j: (0, 0))]
        + [pl.BlockSpec((1, FF_CHUNK, D_MODEL), functools.partial(lambda i, j, part: (j, part, 0), part=part))
           for part in range(3)],
        [tok, tok, hid, hid],
        [jax.ShapeDtypeStruct((S, D_MODEL), F32), jax.ShapeDtypeStruct((S, D_MODEL), BF16),
         jax.ShapeDtypeStruct((N_CHIPS, S, FF_CHUNK), BF16), jax.ShapeDtypeStruct((N_CHIPS, S, FF_CHUNK), BF16)],
        [pltpu.VMEM((tile, D_MODEL), F32)], 40, (h, gain, group, group, group), rider)


def _ffn_bwd(d_out, n, gate, up, group, name):
    S = n.shape[0]
    tile = min(TOK_TILE, S)
    nt = S // tile
    half_rows = 3 * FF_CHUNK // 2
    cut = FF_CHUNK // 2

    def body(do_ref, n_ref, gate_ref, up_ref, wg_ref, wu_ref, wd_ref, dn_ref, dw_ref, acc_g, acc_u, acc_d):
        j, i = pl.program_id(0), pl.program_id(1)

        @pl.when(i == 0)
        def _():
            acc_g[...] = jnp.zeros_like(acc_g)
            acc_u[...] = jnp.zeros_like(acc_u)
            acc_d[...] = jnp.zeros_like(acc_d)

        do = do_ref[...]
        nn = n_ref[...]
        g = gate_ref[0].astype(F32)
        u = up_ref[0].astype(F32)
        d_act = _dot_nt(do, wd_ref[0])
        sig = jax.nn.sigmoid(g)
        silu = g * sig
        d_up = (d_act * silu).astype(BF16)
        d_gate = (d_act * u * (sig * (1.0 + g * (1.0 - sig)))).astype(BF16)
        act = (silu * u).astype(BF16)
        dn_ref[0] = (_dot(d_gate, wg_ref[0]) + _dot(d_up, wu_ref[0])).astype(BF16)
        acc_g[...] += _dot_tn(d_gate, nn)
        acc_u[...] += _dot_tn(d_up, nn)
        acc_d[...] += _dot_tn(act, do)

        @pl.when(i == nt - 1)
        def _():
            dw_ref[0, 0, :FF_CHUNK, :] = acc_g[...].astype(BF16)
            dw_ref[0, 0, FF_CHUNK:, :] = acc_u[:cut, :].astype(BF16)
            dw_ref[1, 0, :cut, :] = acc_u[cut:, :].astype(BF16)
            dw_ref[1, 0, cut:, :] = acc_d[...].astype(BF16)

    tok = pl.BlockSpec((tile, D_MODEL), lambda j, i: (i, 0))
    hid = pl.BlockSpec((1, tile, FF_CHUNK), lambda j, i: (j, i, 0))
    return pl.pallas_call(
        body, name=name, grid=(N_CHIPS, nt),
        in_specs=[tok, tok, hid, hid]
        + [pl.BlockSpec((1, FF_CHUNK, D_MODEL), functools.partial(lambda j, i, part: (j, part, 0), part=part))
           for part in range(3)],
        out_specs=[pl.BlockSpec((1, tile, D_MODEL), lambda j, i: (j, i, 0)),
                   pl.BlockSpec((2, 1, half_rows, D_MODEL), lambda j, i: (0, j, 0, 0))],
        out_shape=[jax.ShapeDtypeStruct((N_CHIPS, S, D_MODEL), BF16),
                   jax.ShapeDtypeStruct((2, N_CHIPS, half_rows, D_MODEL), BF16)],
        scratch_shapes=[pltpu.VMEM((FF_CHUNK, D_MODEL), F32)] * 3,
        compiler_params=_params(("arbitrary", "arbitrary"), 56),
    )(d_out, n, gate, up, group, group, group)


def _mix_in(h, gain, w_in, rc, rs1, rs2, name):
    S = h.shape[0]
    tile = min(TOK_TILE, S)

    def body(h_ref, g_ref, w_ref, c_ref, s1_ref, s2_ref, n_ref, q_ref, k_ref, v_ref, pc_ref):
        _, xh = _rms_stats(h_ref[...])
        n = (xh * g_ref[...]).astype(BF16)
        n_ref[...] = n
        u = _dot_nt(n, w_ref[...])
        c, s1, s2 = c_ref[...], s1_ref[...], s2_ref[...]
        q_ref[...] = jnp.concatenate([(_rope(u[:, hd * LANES:(hd + 1) * LANES], c, s1, s2) * SCALE).astype(BF16)
                                      for hd in range(N_HEADS)], axis=1)
        k_ref[...] = _rope(u[:, Q_PAD:Q_PAD + KV_WIDTH], c, s1, s2).astype(BF16)
        v_ref[...] = u[:, Q_PAD + KV_WIDTH:Q_PAD + 2 * KV_WIDTH].astype(BF16)
        pc_ref[...] = u[:, Q_PAD + 2 * KV_WIDTH:]

    return pl.pallas_call(
        body, name=name, grid=(S // tile,),
        in_specs=[_rows(tile, D_MODEL), _full((1, D_MODEL)), _full((U_PAD, D_MODEL)),
                  _rows(tile, LANES), _rows(tile, LANES), _rows(tile, LANES)],
        out_specs=[_rows(tile, D_MODEL), _rows(tile, Q_PAD), _rows(tile, KV_WIDTH), _rows(tile, KV_WIDTH),
                   _rows(tile, POOL_WIDTH)],
        out_shape=[jax.ShapeDtypeStruct((S, D_MODEL), BF16), jax.ShapeDtypeStruct((S, Q_PAD), BF16),
                   jax.ShapeDtypeStruct((S, KV_WIDTH), BF16), jax.ShapeDtypeStruct((S, KV_WIDTH), BF16),
                   jax.ShapeDtypeStruct((S, POOL_WIDTH), F32)],
        compiler_params=_params(("parallel",), 40),
    )(h, gain, w_in, rc, rs1, rs2)


def _band_start(i, S):
    return pl.multiple_of(jnp.clip((i - 1) * BLOCK, 0, S - BAND), BLOCK)


def _window_bias(off):
    r = lax.broadcasted_iota(jnp.int32, (BLOCK, 1), 0)
    c = lax.broadcasted_iota(jnp.int32, (1, BAND), 1)
    return jnp.where(jnp.abs(off + r - c) <= WINDOW, 0.0, NEG).astype(F32)


def _softmax_parts(qh, kb, bias, sink_h):
    s = _dot_nt(qh, kb) + bias
    m = jnp.maximum(jnp.max(s, axis=-1, keepdims=True), sink_h)
    p = jnp.exp(s - m)
    es = jnp.exp(sink_h - m)
    return p, es, 1.0 / (jnp.sum(p, axis=-1, keepdims=True) + es)


def _pool_matrix(t0, start, S, w):
    r = lax.broadcasted_iota(jnp.int32, (BLOCK, 1), 0) + t0
    c = lax.broadcasted_iota(jnp.int32, (1, BAND), 1) + start
    half = w // 2

    def window(lo, hi):
        a = jnp.maximum(lo, 0)
        b = jnp.minimum(hi + 1, S)
        return jnp.where((c >= a) & (c < b), 1.0 / (b - a).astype(F32), 0.0)

    return (0.5 * (window(r - half, r + half - 1) + window(r - half + 1, r + half))).astype(BF16)


def _pool_matrices(S):
    blocks = ((0, 0), (BLOCK, 0), (S - BLOCK, S - BAND))
    return jnp.stack([jnp.stack([_pool_matrix(t0, start, S, w) for w in POOL_WINDOWS]) for t0, start in blocks])


def _pool_spec(nb):
    return pl.BlockSpec((1, N_POOL, BLOCK, BAND), lambda i, *_: (jnp.where(i == 0, 0, jnp.where(i == nb - 1, 2, 1)), 0, 0, 0))


def _mix_core_fwd(q, k, v, pc, sink, pool_m, pool_w, pool_scale, name):
    S = q.shape[0]
    nb = S // BLOCK

    def body(sink_ref, q_ref, k_ref, v_ref, pc_ref, pm_ref, pw_ref, ps_ref, a_ref, p_ref):
        i = pl.program_id(0)
        start = _band_start(i, S)
        band = pl.ds(start, BAND)
        bias = _window_bias(i * BLOCK - start)
        kb, vb = k_ref[band, :], v_ref[band, :]
        hs = range(N_HEADS)
        ss = [_dot_nt(q_ref[:, hd * LANES:(hd + 1) * LANES], kb) + bias for hd in hs]
        ms = [jnp.maximum(jnp.max(ss[hd], axis=-1, keepdims=True), sink_ref[0, hd]) for hd in hs]
        ps = [jnp.exp(ss[hd] - ms[hd]) for hd in hs]
        invs = [1.0 / (jnp.sum(ps[hd], axis=-1, keepdims=True) + jnp.exp(sink_ref[0, hd] - ms[hd])) for hd in hs]
        outs = [_dot(ps[hd].astype(BF16), vb) for hd in hs]
        a_ref[...] = jnp.concatenate([(outs[hd] * invs[hd]).astype(BF16) for hd in hs], axis=1)
        centre = pl.ds(pl.multiple_of(i * BLOCK, BLOCK), BLOCK)
        gs = range(N_POOL)
        sl = [slice(g * POOL_GROUP, (g + 1) * POOL_GROUP) for g in gs]
        means = [_dot(pm_ref[0, g], pc_ref[band, sl[g]].astype(BF16)) for g in gs]
        devs = [(means[g] - pc_ref[centre, sl[g]]).astype(BF16) for g in gs]
        p_ref[...] = (jnp.concatenate([_dot(devs[g], pw_ref[g]) for g in gs], axis=1) * ps_ref[...]).astype(BF16)

    return pl.pallas_call(
        body, name=name, grid=(nb,),
        in_specs=[pl.BlockSpec(memory_space=pltpu.SMEM), _rows(BLOCK, Q_PAD), _full((S, KV_WIDTH)), _full((S, KV_WIDTH)),
                  _full((S, POOL_WIDTH)), _pool_spec(nb), _full((N_POOL, POOL_GROUP, POOL_GROUP)), _full((1, POOL_WIDTH))],
        out_specs=[_rows(BLOCK, Q_PAD), _rows(BLOCK, POOL_WIDTH)],
        out_shape=[jax.ShapeDtypeStruct((S, Q_PAD), BF16), jax.ShapeDtypeStruct((S, POOL_WIDTH), BF16)],
        compiler_params=_params(("parallel",), 40),
    )(sink, q, k, v, pc, pool_m, pool_w, pool_scale)


def _mix_core_bwd(q, k, v, pc, da, dp, sink, pool_m, pool_w, pool_scale, rc, rs1, rs2, name, rider=None):
    S = q.shape[0]
    nb = S // BLOCK

    def body(sink_ref, q_ref, k_ref, v_ref, pc_ref, da_ref, dp_ref, pm_ref, pw_ref, ps_ref, c_ref, s1_ref, s2_ref,
             dq_ref, dk_ref, dv_ref, dpc_ref, dsink_ref, dpw_ref, dps_ref):
        i = pl.program_id(0)

        @pl.when(i == 0)
        def _():
            dk_ref[...] = jnp.zeros_like(dk_ref)
            dv_ref[...] = jnp.zeros_like(dv_ref)
            dpc_ref[...] = jnp.zeros_like(dpc_ref)
            dsink_ref[...] = jnp.zeros_like(dsink_ref)
            dpw_ref[...] = jnp.zeros_like(dpw_ref)
            dps_ref[...] = jnp.zeros_like(dps_ref)

        start = _band_start(i, S)
        band = pl.ds(start, BAND)
        bias = _window_bias(i * BLOCK - start)
        kb, vb = k_ref[band, :], v_ref[band, :]
        c, s1, s2 = c_ref[...], s1_ref[...], s2_ref[...]
        lane = lax.broadcasted_iota(jnp.int32, (1, LANES), 1)
        hs = range(N_HEADS)
        qs = [q_ref[:, hd * LANES:(hd + 1) * LANES] for hd in hs]
        das = [da_ref[:, hd * LANES:(hd + 1) * LANES] for hd in hs]
        ss = [_dot_nt(qs[hd], kb) + bias for hd in hs]
        d_probs = [_dot_nt(das[hd], vb) for hd in hs]
        ms = [jnp.maximum(jnp.max(ss[hd], axis=-1, keepdims=True), sink_ref[0, hd]) for hd in hs]
        ps = [jnp.exp(ss[hd] - ms[hd]) for hd in hs]
        ess = [jnp.exp(sink_ref[0, hd] - ms[hd]) for hd in hs]
        invs = [1.0 / (jnp.sum(ps[hd], axis=-1, keepdims=True) + ess[hd]) for hd in hs]
        probs = [ps[hd] * invs[hd] for hd in hs]
        deltas = [jnp.sum(probs[hd] * d_probs[hd], axis=-1, keepdims=True) for hd in hs]
        d_ss = [(probs[hd] * (d_probs[hd] - deltas[hd])).astype(BF16) for hd in hs]
        dqs = [_dot(d_ss[hd], kb) for hd in hs]
        dq_ref[...] = jnp.concatenate([_rope_bwd(dqs[hd] * SCALE, c, s1, s2).astype(BF16) for hd in hs], axis=1)
        dks = [_dot_tn(d_ss[hd], qs[hd]) for hd in hs]
        dvs = [_dot_tn(probs[hd].astype(BF16), das[hd]) for hd in hs]
        dk_ref[band, :] += functools.reduce(lambda a, b: a + b, dks)
        dv_ref[band, :] += functools.reduce(lambda a, b: a + b, dvs)
        dsink_ref[...] += functools.reduce(lambda a, b: a + b, [
            jnp.where(lane == hd, -jnp.sum(ess[hd] * invs[hd] * deltas[hd], axis=0, keepdims=True), 0.0) for hd in hs])

        centre = pl.ds(pl.multiple_of(i * BLOCK, BLOCK), BLOCK)
        gs = range(N_POOL)
        sl = [slice(g * POOL_GROUP, (g + 1) * POOL_GROUP) for g in gs]
        devs = [(_dot(pm_ref[0, g], pc_ref[band, sl[g]].astype(BF16)) - pc_ref[centre, sl[g]]).astype(BF16) for g in gs]
        dys = [dp_ref[:, sl[g]].astype(F32) for g in gs]
        zs = [_dot(devs[g], pw_ref[g]) for g in gs]
        dzs = [(dys[g] * ps_ref[:, sl[g]]).astype(BF16) for g in gs]
        d_devs = [_dot_nt(dzs[g], pw_ref[g]) for g in gs]
        dps_ref[...] += jnp.concatenate([jnp.sum(dys[g] * zs[g], axis=0, keepdims=True) for g in gs], axis=1)
        for g in gs:
            dpw_ref[g] += _dot_tn(devs[g], dzs[g])
        dpc_ref[band, :] += jnp.concatenate([_dot_tn(pm_ref[0, g], d_devs[g].astype(BF16)) for g in gs], axis=1)
        dpc_ref[centre, :] -= jnp.concatenate(d_devs, axis=1)

    return _call(
        body, name, (nb,),
        [pl.BlockSpec(memory_space=pltpu.SMEM), _rows(BLOCK, Q_PAD), _full((S, KV_WIDTH)), _full((S, KV_WIDTH)),
         _full((S, POOL_WIDTH)), _rows(BLOCK, Q_PAD), _rows(BLOCK, POOL_WIDTH), _pool_spec(nb),
         _full((N_POOL, POOL_GROUP, POOL_GROUP)), _full((1, POOL_WIDTH)),
         _rows(BLOCK, LANES), _rows(BLOCK, LANES), _rows(BLOCK, LANES)],
        [_rows(BLOCK, Q_PAD), _full((S, KV_WIDTH)), _full((S, KV_WIDTH)), _full((S, POOL_WIDTH)),
         _full((1, LANES)), _full((N_POOL, POOL_GROUP, POOL_GROUP)), _full((1, POOL_WIDTH))],
        [jax.ShapeDtypeStruct((S, Q_PAD), BF16), jax.ShapeDtypeStruct((S, KV_WIDTH), F32),
         jax.ShapeDtypeStruct((S, KV_WIDTH), F32), jax.ShapeDtypeStruct((S, POOL_WIDTH), F32),
         jax.ShapeDtypeStruct((1, LANES), F32), jax.ShapeDtypeStruct((N_POOL, POOL_GROUP, POOL_GROUP), F32),
         jax.ShapeDtypeStruct((1, POOL_WIDTH), F32)],
        [], 56, (sink, q, k, v, pc, da, dp, pool_m, pool_w, pool_scale, rc, rs1, rs2), rider)


def _mix_out(h, a, p, wa, wp, name):
    S = h.shape[0]
    tile = min(TOK_TILE, S)

    def body(h_ref, a_ref, p_ref, wa_ref, wp_ref, o_ref):
        o_ref[...] = h_ref[...] + _dot(a_ref[...], wa_ref[...]) + _dot(p_ref[...], wp_ref[...])

    return pl.pallas_call(
        body, name=name, grid=(S // tile,),
        in_specs=[_rows(tile, D_MODEL), _rows(tile, Q_PAD), _rows(tile, POOL_WIDTH),
                  _full((Q_PAD, D_MODEL)), _full((POOL_WIDTH, D_MODEL))],
        out_specs=_rows(tile, D_MODEL),
        out_shape=jax.ShapeDtypeStruct((S, D_MODEL), F32),
        compiler_params=_params(("parallel",), 40),
    )(h, a, p, wa, wp)


def _loss_head(h, target, gain, name):
    S = h.shape[0]
    tile = min(TOK_TILE, S)

    def body(h_ref, t_ref, g_ref, dh_ref, dhalf_ref, loss_ref, dg_ref):
        @pl.when(pl.program_id(0) == 0)
        def _():
            loss_ref[...] = jnp.zeros_like(loss_ref)
            dg_ref[...] = jnp.zeros_like(dg_ref)

        g = g_ref[...]
        r, xh = _rms_stats(h_ref[...])
        err = xh * g - t_ref[...]
        loss_ref[...] += (0.5 / D_MODEL) * jnp.sum(err * err, axis=0, keepdims=True)
        dh, dg = _rms_bwd(err * (1.0 / D_MODEL), g, r, xh)
        dg_ref[...] += dg
        dh_ref[...] = dh
        dhalf_ref[...] = (0.5 * dh).astype(BF16)

    return pl.pallas_call(
        body, name=name, grid=(S // tile,),
        in_specs=[_rows(tile, D_MODEL), _rows(tile, D_MODEL), _full((1, D_MODEL))],
        out_specs=[_rows(tile, D_MODEL), _rows(tile, D_MODEL), _full((1, D_MODEL)), _full((1, D_MODEL))],
        out_shape=[jax.ShapeDtypeStruct((S, D_MODEL), F32), jax.ShapeDtypeStruct((S, D_MODEL), BF16),
                   jax.ShapeDtypeStruct((1, D_MODEL), F32), jax.ShapeDtypeStruct((1, D_MODEL), F32)],
        compiler_params=_params(("arbitrary",), 40),
    )(h, target, gain)


def _mix_out_bwd(dh_out, dn, h, gain, a, p, wa, wp, name, rider=None):
    S = h.shape[0]
    tile = min(TOK_TILE, S)

    def body(do_ref, dn_ref, h_ref, g_ref, a_ref, p_ref, wa_ref, wp_ref, dh_ref, da_ref, dp_ref, dwa_ref, dwp_ref, dg_ref):
        @pl.when(pl.program_id(0) == 0)
        def _():
            dwa_ref[...] = jnp.zeros_like(dwa_ref)
            dwp_ref[...] = jnp.zeros_like(dwp_ref)
            dg_ref[...] = jnp.zeros_like(dg_ref)

        r, xh = _rms_stats(h_ref[...])
        dnorm, dg = _rms_bwd(_sum_chunks(dn_ref), g_ref[...], r, xh)
        dh = do_ref[...] + dnorm
        dg_ref[...] += dg
        dh_ref[...] = dh
        dhb = dh.astype(BF16)
        da_ref[...] = _dot_nt(dhb, wa_ref[...]).astype(BF16)
        dp_ref[...] = _dot_nt(dhb, wp_ref[...]).astype(BF16)
        dwa_ref[...] += _dot_tn(a_ref[...], dhb)
        dwp_ref[...] += _dot_tn(p_ref[...], dhb)

    return _call(
        body, name, (S // tile,),
        [_rows(tile, D_MODEL), _chunk_rows(tile), _rows(tile, D_MODEL), _full((1, D_MODEL)),
         _rows(tile, Q_PAD), _rows(tile, POOL_WIDTH), _full((Q_PAD, D_MODEL)), _full((POOL_WIDTH, D_MODEL))],
        [_rows(tile, D_MODEL), _rows(tile, Q_PAD), _rows(tile, POOL_WIDTH),
         _full((Q_PAD, D_MODEL)), _full((POOL_WIDTH, D_MODEL)), _full((1, D_MODEL))],
        [jax.ShapeDtypeStruct((S, D_MODEL), F32), jax.ShapeDtypeStruct((S, Q_PAD), BF16),
         jax.ShapeDtypeStruct((S, POOL_WIDTH), BF16), jax.ShapeDtypeStruct((Q_PAD, D_MODEL), F32),
         jax.ShapeDtypeStruct((POOL_WIDTH, D_MODEL), F32), jax.ShapeDtypeStruct((1, D_MODEL), F32)],
        [], 48, (dh_out, dn, h, gain, a, p, wa, wp), rider)


def _mix_in_bwd(dh_out, h, gain, n, dq, dk, dv, dpc, rc, rs1, rs2, w_in, name, rider=None):
    S = h.shape[0]
    tile = min(TOK_TILE, S)

    def body(do_ref, h_ref, g_ref, n_ref, dq_ref, dk_ref, dv_ref, dpc_ref, c_ref, s1_ref, s2_ref, w_ref,
             dh_ref, dhalf_ref, dw_ref, dg_ref):
        @pl.when(pl.program_id(0) == 0)
        def _():
            dw_ref[...] = jnp.zeros_like(dw_ref)
            dg_ref[...] = jnp.zeros_like(dg_ref)

        dk = _rope_bwd(dk_ref[...], c_ref[...], s1_ref[...], s2_ref[...]).astype(BF16)
        du = jnp.concatenate([dq_ref[...], dk, dv_ref[...].astype(BF16), dpc_ref[...].astype(BF16)], axis=1)
        dn = _dot(du, w_ref[...])
        dw_ref[...] += _dot_tn(du, n_ref[...])
        r, xh = _rms_stats(h_ref[...])
        dnorm, dg = _rms_bwd(dn, g_ref[...], r, xh)
        dh = do_ref[...] + dnorm
        dg_ref[...] += dg
        dh_ref[...] = dh
        dhalf_ref[...] = (0.5 * dh).astype(BF16)

    return _call(
        body, name, (S // tile,),
        [_rows(tile, D_MODEL), _rows(tile, D_MODEL), _full((1, D_MODEL)), _rows(tile, D_MODEL),
         _rows(tile, Q_PAD), _rows(tile, KV_WIDTH), _rows(tile, KV_WIDTH), _rows(tile, POOL_WIDTH),
         _rows(tile, LANES), _rows(tile, LANES), _rows(tile, LANES), _full((U_PAD, D_MODEL))],
        [_rows(tile, D_MODEL), _rows(tile, D_MODEL), _full((U_PAD, D_MODEL)), _full((1, D_MODEL))],
        [jax.ShapeDtypeStruct((S, D_MODEL), F32), jax.ShapeDtypeStruct((S, D_MODEL), BF16),
         jax.ShapeDtypeStruct((U_PAD, D_MODEL), F32), jax.ShapeDtypeStruct((1, D_MODEL), F32)],
        [], 56, (dh_out, h, gain, n, dq, dk, dv, dpc, rc, rs1, rs2, w_in), rider)


def _norm_bwd(dh_out, dn, h, gain, name, rider=None):
    S = h.shape[0]
    tile = min(TOK_TILE, S)

    def body(do_ref, dn_ref, h_ref, g_ref, dh_ref, dg_ref):
        @pl.when(pl.program_id(0) == 0)
        def _():
            dg_ref[...] = jnp.zeros_like(dg_ref)

        r, xh = _rms_stats(h_ref[...])
        dnorm, dg = _rms_bwd(_sum_chunks(dn_ref), g_ref[...], r, xh)
        dg_ref[...] += dg
        dh_ref[...] = do_ref[...] + dnorm

    return _call(
        body, name, (S // tile,),
        [_rows(tile, D_MODEL), _chunk_rows(tile), _rows(tile, D_MODEL), _full((1, D_MODEL))],
        [_rows(tile, D_MODEL), _full((1, D_MODEL))],
        [jax.ShapeDtypeStruct((S, D_MODEL), F32), jax.ShapeDtypeStruct((1, D_MODEL), F32)],
        [], 40, (dh_out, dn, h, gain), rider)


def _rope_tables(S):
    half = ROTARY_DIM // 2
    inv_freq = ROPE_THETA ** (-jnp.arange(0, ROTARY_DIM, 2, dtype=F32) / ROTARY_DIM)
    dim = jnp.arange(LANES) % HEAD_DIM
    ang = jnp.arange(S, dtype=F32)[:, None] * inv_freq[dim % half][None, :]
    lo, hi = (dim < half)[None, :], ((dim >= half) & (dim < ROTARY_DIM))[None, :]
    c = jnp.where(lo | hi, jnp.cos(ang), 1.0)
    s1 = jnp.where(lo, -jnp.sin(ang), 0.0)
    s2 = jnp.where(hi, jnp.sin(ang), 0.0)
    return c, s1, s2


def _pad_heads(w, axis):
    w = jnp.moveaxis(w, axis, 0)
    heads = w.reshape((N_HEADS, HEAD_DIM) + w.shape[1:])
    zero = jnp.zeros_like(heads)
    first = (jnp.arange(N_HEADS) < Q_PER_KV).reshape((N_HEADS, 1) + (1,) * (w.ndim - 1))
    lo = jnp.where(first, heads, zero)
    hi = jnp.where(first, zero, heads)
    padded = jnp.concatenate([lo, hi], axis=1).reshape((Q_PAD,) + w.shape[1:])
    return jnp.moveaxis(padded, 0, axis)


def _unpad_heads(w, axis):
    w = jnp.moveaxis(w, axis, 0)
    groups = w.reshape((N_HEADS, 2, HEAD_DIM) + w.shape[1:])
    first = (jnp.arange(N_HEADS) < Q_PER_KV).reshape((N_HEADS, 1) + (1,) * (w.ndim - 1))
    heads = jnp.where(first, groups[:, 0], groups[:, 1]).reshape((ATTN_WIDTH,) + w.shape[1:])
    return jnp.moveaxis(heads, 0, axis)


IN_ROWS = IN_WIDTH // N_CHIPS
OUT_ROWS = (ATTN_WIDTH + POOL_WIDTH) // N_CHIPS
MIX_ROWS = IN_ROWS + OUT_ROWS
FFN_ROWS = 3 * FF_CHUNK


def _reduce_tail(place, parts, received, small):
    tags = [str(p.shape[2]) for p in parts]
    pairs = [_pair_sum(place, p, r, "grad_pair_sum_tail_" + t) for t, p, r in zip(tags, parts, received)]
    *stacks, small_all = _comm_call(_merge(_scatter(pairs), _small_allgather(small)), "grad_scatter_tail")
    reduced = [_chip_sum(place, p, s, 2, "grad_chip_sum_tail_" + t) for t, p, s in zip(tags, pairs, stacks)]
    small_sum = _sum_leading(small_all, 1, "small_grad_sum")
    return _comm_call(_sibling_share(reduced), "grad_share_tail"), small_sum


def _step(x, target, bufs, small, place):
    S = x.shape[0]
    rc, rs1, rs2 = _rope_tables(S)
    (ffn1,) = _comm_call(_allgather(bufs[:1]), "allgather_ffn1")
    ffn1 = ffn1.reshape(N_CHIPS, FFN_ROWS, D_MODEL)
    h1, n1, gate1, up1, mix, ffn2 = _ffn_fwd(x, small["ffn1_norm"], ffn1, "ffn1_fwd", _allgather(bufs[1:]))
    mix, ffn2 = mix.reshape(N_CHIPS, MIX_ROWS, D_MODEL), ffn2.reshape(N_CHIPS, FFN_ROWS, D_MODEL)
    w_in_t = mix[:, :IN_ROWS].reshape(IN_WIDTH, D_MODEL)
    w_in_pad = jnp.concatenate([_pad_heads(w_in_t[:ATTN_WIDTH], 0), w_in_t[ATTN_WIDTH:]], axis=0)
    w_out = mix[:, IN_ROWS:].reshape(ATTN_WIDTH + POOL_WIDTH, D_MODEL)
    wa = _pad_heads(w_out[:ATTN_WIDTH], 0)
    wp = w_out[ATTN_WIDTH:]
    pool_w = small["pool_w"].astype(BF16)

    n2, q, k, v, pc = _mix_in(h1, small["mix_norm"], w_in_pad, rc, rs1, rs2, "mix_in")
    pool_m = _pool_matrices(S)
    a, p = _mix_core_fwd(q, k, v, pc, small["sink_logits"], pool_m, pool_w, small["pool_scale"], "mix_core_fwd")
    h2 = _mix_out(h1, a, p, wa, wp, "mix_out")
    h3, n3, gate2, up2 = _ffn_fwd(h2, small["ffn2_norm"], ffn2, "ffn2_fwd")
    dh3, dhalf3, loss_lanes, d_final = _loss_head(h3, target, small["final_norm"], "loss_head")

    dn3, d_ffn2 = _ffn_bwd(dhalf3, n3, gate2, up2, ffn2, "ffn2_bwd")
    dh2, da, dp, dwa, dwp, d_ffn2_norm, received = _mix_out_bwd(dh3, dn3, h2, small["ffn2_norm"], a, p, wa, wp, "mix_out_bwd",
                                                                _sibling_exchange([d_ffn2]))
    pair = _pair_sum(place, d_ffn2, received, "grad_pair_sum_ffn2")
    dq, dk, dv, dpc, dsink, dpool_w, dpool_scale, stack = _mix_core_bwd(
        q, k, v, pc, da, dp, small["sink_logits"], pool_m, pool_w, small["pool_scale"], rc, rs1, rs2, "mix_core_bwd",
        _scatter([pair]))
    reduced = _chip_sum(place, pair, stack, 2, "grad_chip_sum_ffn2")
    dh1, dhalf1, dw_in_pad, d_mix_norm, g_ffn2 = _mix_in_bwd(dh2, h1, small["mix_norm"], n2, dq, dk, dv, dpc, rc, rs1, rs2,
                                                             w_in_pad, "mix_in_bwd", _sibling_share([reduced]))
    dw_in_t = jnp.concatenate([_unpad_heads(dw_in_pad[:Q_PAD], 0), dw_in_pad[Q_PAD:]], axis=0)
    dw_out = jnp.concatenate([_unpad_heads(dwa, 0), dwp], axis=0)
    d_mix = jnp.concatenate([dw_in_t.reshape(N_CHIPS, IN_ROWS, D_MODEL), dw_out.reshape(N_CHIPS, OUT_ROWS, D_MODEL)], axis=1)
    d_mix = jnp.transpose(d_mix.reshape(N_CHIPS, 2, MIX_ROWS // 2, D_MODEL), (1, 0, 2, 3)).astype(BF16)
    dn1, d_ffn1 = _ffn_bwd(dhalf1, n1, gate1, up1, ffn1, "ffn1_bwd")
    grad_x, d_ffn1_norm, *received = _norm_bwd(dh1, dn1, x, small["ffn1_norm"], "norm1_bwd", _sibling_exchange([d_ffn1, d_mix]))

    small_g = {"ffn1_norm": d_ffn1_norm, "mix_norm": d_mix_norm, "ffn2_norm": d_ffn2_norm, "final_norm": d_final,
               "pool_scale": dpool_scale, "sink_logits": dsink[:, :N_HEADS], "pool_w": dpool_w}
    loss_row = jnp.sum(loss_lanes.reshape(D_MODEL // LANES, LANES), axis=0, keepdims=True)
    (g_ffn1, g_mix), small_sum = _reduce_tail(place, [d_ffn1, d_mix], received, _pack_small(small_g, loss_row))
    return jnp.sum(small_sum[SMALL_ROWS - 1]), grad_x, [g.reshape(-1, D_MODEL) for g in (g_ffn1, g_mix, g_ffn2)], small_sum


GROUPS =(("ffn1_w_gate", "ffn1_w_up", "ffn1_w_down"), ("w_in", "w_out"), ("ffn2_w_gate", "ffn2_w_up", "ffn2_w_down"))
TRANSPOSED = ("ffn1_w_gate", "ffn1_w_up", "w_in", "ffn2_w_gate", "ffn2_w_up")


def _place():
    x, y, c = lax.axis_index("x"), lax.axis_index("y"), lax.axis_index("c")
    chips = [(1 - x, y), (x, 1 - y), (1 - x, 1 - y)]
    return x, y, c, chips


def _remote(src, dst, send_sem, recv_sem, to):
    return pltpu.make_async_remote_copy(src_ref=src, dst_ref=dst, send_sem=send_sem, recv_sem=recv_sem,
                                        device_id=to, device_id_type=MESH)


def _pack(chip, members, name):
    rows = members[0].shape[0]
    n = len(members)

    def body(chip_ref, *refs):
        ins, out_ref, buf, sems = refs[:n], refs[n], refs[n + 1], refs[n + 2]
        copies = [pltpu.make_async_copy(ins[k], buf.at[k], sems.at[k]) for k in range(n)]
        for cp in copies:
            cp.start()
        for k in range(n):
            copies[k].wait()
            out_ref[0, k * rows:(k + 1) * rows, :] = buf[k].astype(BF16)

    return pl.pallas_call(
        body, name=name,
        grid_spec=pltpu.PrefetchScalarGridSpec(
            num_scalar_prefetch=1, grid=(1,),
            in_specs=[HBM_SPEC] * n,
            out_specs=pl.BlockSpec((1, n * rows, D_MODEL), lambda k, chip_ref: (chip_ref[0], 0, 0)),
            scratch_shapes=[pltpu.VMEM((n, rows, D_MODEL), F32), pltpu.SemaphoreType.DMA((n,))]),
        out_shape=jax.ShapeDtypeStruct((N_CHIPS, n * rows, D_MODEL), BF16),
        compiler_params=_params(("arbitrary",), 40),
    )(chip, *members)


def _same(arrays):
    return [jax.ShapeDtypeStruct(a.shape, a.dtype) for a in arrays]


def _allgather(bufs):
    n = len(bufs)

    def copies(outs, send_sems, recv_sems, started_only=False):
        x, y, c, chips = _place()
        me = 2 * x + y
        slots = [2 * cx + cy for cx, cy in chips]
        first = [[_remote(outs[a].at[me, c], outs[a].at[me, c], send_sems.at[6 * a + k], recv_sems.at[6 * a + k], (*chips[k], c))
                  for k in range(3)] for a in range(n)]
        if started_only:
            return first
        passed = [[_remote(outs[a].at[slots[k], c], outs[a].at[slots[k], c], send_sems.at[6 * a + 3 + k],
                           recv_sems.at[6 * a + 3 + k], (x, y, 1 - c)) for k in range(3)] for a in range(n)]
        landed = [[_remote(outs[a].at[me, c], outs[a].at[slots[k], c], send_sems.at[6 * a + k], recv_sems.at[6 * a + k],
                           (*chips[k], c)) for k in range(3)] for a in range(n)]
        handed = [[_remote(outs[a].at[me, c], outs[a].at[slots[k], 1 - c], send_sems.at[6 * a + 3 + k],
                           recv_sems.at[6 * a + 3 + k], (x, y, 1 - c)) for k in range(3)] for a in range(n)]
        return first, passed, landed, handed

    def start(ins, outs, sems):
        for per_buf in copies(outs, *sems, started_only=True):
            for cp in per_buf:
                cp.start()

    def finish(ins, outs, sems):
        first, passed, landed, handed = copies(outs, *sems)
        for a in range(n):
            for k in range(3):
                landed[a][k].wait_recv()
                passed[a][k].start()
        for a in range(n):
            for k in range(3):
                handed[a][k].wait_recv()
        for a in range(n):
            for cp in first[a] + passed[a]:
                cp.wait_send()

    return _Rider(list(bufs), _same(bufs), {a: a for a in range(n)},
                  [pltpu.SemaphoreType.DMA((6 * n,)), pltpu.SemaphoreType.DMA((6 * n,))], start, finish)


def _sibling_exchange(parts):
    n = len(parts)

    def copies(ins, outs, send_sems, recv_sems):
        x, y, c, _ = _place()
        return [_remote(ins[a].at[1 - c], outs[a], send_sems.at[a], recv_sems.at[a], (x, y, 1 - c)) for a in range(n)]

    def start(ins, outs, sems):
        for cp in copies(ins, outs, *sems):
            cp.start()

    def finish(ins, outs, sems):
        for cp in copies(ins, outs, *sems):
            cp.wait_recv()
            cp.wait_send()

    return _Rider(list(parts), [jax.ShapeDtypeStruct(p.shape[1:], p.dtype) for p in parts], {},
                  [pltpu.SemaphoreType.DMA((n,)), pltpu.SemaphoreType.DMA((n,))], start, finish)


def _small_allgather(small):
    flips = [(fx, fy, fc) for fx in range(2) for fy in range(2) for fc in range(2)][1:]

    def copies(small_ref, gather_ref, send_sems, recv_sems, local_sem, started_only=False):
        x, y, c, _ = _place()
        me = 4 * x + 2 * y + c
        peers = [((1 - x) if fx else x, (1 - y) if fy else y, (1 - c) if fc else c) for fx, fy, fc in flips]
        own = pltpu.make_async_copy(small_ref, gather_ref.at[me], local_sem)
        sent = [_remote(small_ref, gather_ref.at[me], send_sems.at[k], recv_sems.at[k], peer) for k, peer in enumerate(peers)]
        if started_only:
            return own, sent
        landed = [_remote(small_ref, gather_ref.at[4 * px + 2 * py + pc], send_sems.at[k], recv_sems.at[k], (px, py, pc))
                  for k, (px, py, pc) in enumerate(peers)]
        return own, sent, landed

    def start(ins, outs, sems):
        own, sent = copies(ins[0], outs[0], *sems, started_only=True)
        own.start()
        for cp in sent:
            cp.start()

    def finish(ins, outs, sems):
        own, sent, landed = copies(ins[0], outs[0], *sems)
        for cp in landed:
            cp.wait_recv()
        for cp in sent:
            cp.wait_send()
        own.wait()

    return _Rider([small], [jax.ShapeDtypeStruct((2 * N_CHIPS,) + small.shape, small.dtype)], {},
                  [pltpu.SemaphoreType.DMA((7,)), pltpu.SemaphoreType.DMA((7,)), pltpu.SemaphoreType.DMA], start, finish)


def _merge(a, b):
    na, nao, nas = len(a.operands), len(a.out_shapes), len(a.scratch)

    def start(ins, outs, sems):
        a.start(ins[:na], outs[:nao], sems[:nas])
        b.start(ins[na:], outs[nao:], sems[nas:])

    def finish(ins, outs, sems):
        a.finish(ins[:na], outs[:nao], sems[:nas])
        b.finish(ins[na:], outs[nao:], sems[nas:])

    aliases = {**a.aliases, **{na + k: nao + v for k, v in b.aliases.items()}}
    return _Rider(a.operands + b.operands, a.out_shapes + b.out_shapes, aliases, a.scratch + b.scratch, start, finish)


def _scatter(sums):
    n = len(sums)

    def copies(ins, outs, send_sems, recv_sems, started_only=False):
        x, y, c, chips = _place()
        me = 2 * x + y
        slots = [2 * cx + cy for cx, cy in chips]
        sent = [_remote(ins[a].at[slots[k]], outs[a].at[me], send_sems.at[3 * a + k], recv_sems.at[3 * a + k], (*chips[k], c))
                for a in range(n) for k in range(3)]
        if started_only:
            return sent
        landed = [_remote(ins[a].at[slots[k]], outs[a].at[slots[k]], send_sems.at[3 * a + k], recv_sems.at[3 * a + k],
                          (*chips[k], c)) for a in range(n) for k in range(3)]
        return sent, landed

    def start(ins, outs, sems):
        for cp in copies(ins, outs, *sems, started_only=True):
            cp.start()

    def finish(ins, outs, sems):
        sent, landed = copies(ins, outs, *sems)
        for cp in landed:
            cp.wait_recv()
        for cp in sent:
            cp.wait_send()

    return _Rider(list(sums), _same(sums), {}, [pltpu.SemaphoreType.DMA((3 * n,)), pltpu.SemaphoreType.DMA((3 * n,))],
                  start, finish)


def _sibling_share(bufs):
    n = len(bufs)

    def copies(outs, send_sems, recv_sems, started_only=False):
        x, y, c, _ = _place()
        sent = [_remote(outs[a].at[c], outs[a].at[c], send_sems.at[a], recv_sems.at[a], (x, y, 1 - c)) for a in range(n)]
        if started_only:
            return sent
        landed = [_remote(outs[a].at[c], outs[a].at[1 - c], send_sems.at[a], recv_sems.at[a], (x, y, 1 - c)) for a in range(n)]
        return sent, landed

    def start(ins, outs, sems):
        for cp in copies(outs, *sems, started_only=True):
            cp.start()

    def finish(ins, outs, sems):
        sent, landed = copies(outs, *sems)
        for cp in landed:
            cp.wait_recv()
        for cp in sent:
            cp.wait_send()

    return _Rider(list(bufs), _same(bufs), {a: a for a in range(n)},
                  [pltpu.SemaphoreType.DMA((n,)), pltpu.SemaphoreType.DMA((n,))], start, finish)


def _pair_sum(core, part, received, name):
    _, _, rh, cols = part.shape

    def body(core_ref, p_ref, r_ref, o_ref):
        o_ref[...] = (p_ref[0].astype(F32) + r_ref[...].astype(F32)).astype(BF16)

    return pl.pallas_call(
        body, name=name,
        grid_spec=pltpu.PrefetchScalarGridSpec(
            num_scalar_prefetch=1, grid=(N_CHIPS,),
            in_specs=[pl.BlockSpec((1, 1, rh, cols), lambda j, core_ref: (core_ref[0], j, 0, 0)),
                      pl.BlockSpec((1, rh, cols), lambda j, core_ref: (j, 0, 0))],
            out_specs=pl.BlockSpec((1, rh, cols), lambda j, core_ref: (j, 0, 0))),
        out_shape=jax.ShapeDtypeStruct((N_CHIPS, rh, cols), BF16),
        compiler_params=_params(("parallel",), 32),
    )(core, part, received)


def _sum_leading(stack, steps, name):
    k, rows, cols = stack.shape
    tile = rows // steps

    def body(s_ref, o_ref):
        total = s_ref[0].astype(F32)
        for d in range(1, k):
            total = total + s_ref[d].astype(F32)
        o_ref[...] = total

    return pl.pallas_call(
        body, name=name, grid=(steps,),
        in_specs=[pl.BlockSpec((k, tile, cols), lambda i: (0, i, 0))],
        out_specs=pl.BlockSpec((tile, cols), lambda i: (i, 0)),
        out_shape=jax.ShapeDtypeStruct((rows, cols), F32),
        compiler_params=_params(("parallel",), 32),
    )(stack)


def _chip_sum(place, own, stack, steps, name):
    k, rows, cols = stack.shape
    tile = rows // steps

    def body(place_ref, own_ref, *refs):
        chip = place_ref[1]
        total = None
        for d in range(k):
            term = jnp.where(chip == d, own_ref[0], refs[d][0]).astype(F32)
            total = term if total is None else total + term
        refs[k][0] = total

    def other(d):
        return lambda i, place_ref: (jnp.where(place_ref[1] == d, (d + 1) % k, d), i, 0)

    return pl.pallas_call(
        body, name=name,
        grid_spec=pltpu.PrefetchScalarGridSpec(
            num_scalar_prefetch=1, grid=(steps,),
            in_specs=[pl.BlockSpec((1, tile, cols), lambda i, place_ref: (place_ref[1], i, 0))]
            + [pl.BlockSpec((1, tile, cols), other(d)) for d in range(k)],
            out_specs=pl.BlockSpec((1, tile, cols), lambda i, place_ref: (place_ref[0], i, 0))),
        out_shape=jax.ShapeDtypeStruct((2, rows, cols), F32),
        compiler_params=_params(("arbitrary",), 32),
    )(place, own, *([stack] * k))


def _adamw(w, g, row0, m, v, tile, name):
    rows, cols = w.shape
    first = row0 // tile
    assert rows % tile == 0 and row0 % tile == 0
    bc1 = 1.0 - ADAM_B1 ** ADAM_STEP
    bc2 = 1.0 - ADAM_B2 ** ADAM_STEP

    def body(w_ref, g_ref, m_ref, v_ref, go_ref, d_ref, mo_ref, vo_ref):
        g = g_ref[...]
        m_new = ADAM_B1 * m_ref[...] + (1.0 - ADAM_B1) * g
        v_new = ADAM_B2 * v_ref[...] + (1.0 - ADAM_B2) * (g * g)
        go_ref[...] = g
        d_ref[...] = -ADAM_LR * ((m_new / bc1) / (jnp.sqrt(v_new / bc2) + ADAM_EPS) + ADAM_WD * w_ref[...])
        mo_ref[...] = m_new
        vo_ref[...] = v_new

    spec = pl.BlockSpec((tile, cols), lambda i: (i, 0))
    g_spec = pl.BlockSpec((tile, cols), lambda i: (first + i, 0))
    return pl.pallas_call(
        body, name=name, grid=(rows // tile,),
        in_specs=[spec, g_spec, spec, spec], out_specs=[spec] * 4,
        out_shape=[jax.ShapeDtypeStruct((rows, cols), F32)] * 4,
        compiler_params=_params(("parallel",), 32),
    )(w, g, m, v)


SMALL = ("ffn1_norm", "mix_norm", "ffn2_norm", "final_norm", "pool_scale", "sink_logits", "pool_w")


def _pack_small(d, last_row=None):
    sink = jnp.pad(d["sink_logits"].reshape(1, N_HEADS), ((0, 0), (0, LANES - N_HEADS)))
    rows = [d[n].reshape(-1, LANES) for n in SMALL[:5]] + [sink, d["pool_w"].reshape(-1, LANES)]
    used = sum(r.shape[0] for r in rows)
    last = jnp.zeros((1, LANES), F32) if last_row is None else last_row
    return jnp.concatenate(rows + [jnp.zeros((SMALL_ROWS - used - 1, LANES), F32), last], axis=0)


def _unpack_small(packed, like):
    out, row = {}, 0
    for n in SMALL:
        size = LANES if n == "sink_logits" else math.prod(like[n].shape)
        chunk = packed[row:row + size // LANES].reshape(-1)
        out[n] = (chunk[:N_HEADS] if n == "sink_logits" else chunk).reshape(like[n].shape)
        row += size // LANES
    return out


def kernel(x, ffn1_norm, ffn1_w_gate, ffn1_w_up, ffn1_w_down, mix_norm, w_in, sink_logits, pool_w, pool_scale, w_out, ffn2_norm, ffn2_w_gate, ffn2_w_up, ffn2_w_down, final_norm, loss_target, m_ffn1_norm, m_ffn1_w_gate, m_ffn1_w_up, m_ffn1_w_down, m_mix_norm, m_w_in, m_sink_logits, m_pool_w, m_pool_scale, m_w_out, m_ffn2_norm, m_ffn2_w_gate, m_ffn2_w_up, m_ffn2_w_down, m_final_norm, v_ffn1_norm, v_ffn1_w_gate, v_ffn1_w_up, v_ffn1_w_down, v_mix_norm, v_w_in, v_sink_logits, v_pool_w, v_pool_scale, v_w_out, v_ffn2_norm, v_ffn2_w_gate, v_ffn2_w_up, v_ffn2_w_down, v_final_norm):
    names = ("ffn1_norm", "ffn1_w_gate", "ffn1_w_up", "ffn1_w_down", "mix_norm", "w_in", "sink_logits", "pool_w",
             "pool_scale", "w_out", "ffn2_norm", "ffn2_w_gate", "ffn2_w_up", "ffn2_w_down", "final_norm")
    weights = dict(zip(names, (ffn1_norm, ffn1_w_gate, ffn1_w_up, ffn1_w_down, mix_norm, w_in, sink_logits, pool_w,
                               pool_scale, w_out, ffn2_norm, ffn2_w_gate, ffn2_w_up, ffn2_w_down, final_norm)))
    mom1 = dict(zip(names, (m_ffn1_norm, m_ffn1_w_gate, m_ffn1_w_up, m_ffn1_w_down, m_mix_norm, m_w_in, m_sink_logits,
                            m_pool_w, m_pool_scale, m_w_out, m_ffn2_norm, m_ffn2_w_gate, m_ffn2_w_up, m_ffn2_w_down,
                            m_final_norm)))
    mom2 = dict(zip(names, (v_ffn1_norm, v_ffn1_w_gate, v_ffn1_w_up, v_ffn1_w_down, v_mix_norm, v_w_in, v_sink_logits,
                            v_pool_w, v_pool_scale, v_w_out, v_ffn2_norm, v_ffn2_w_gate, v_ffn2_w_up, v_ffn2_w_down,
                            v_final_norm)))
    chip = (2 * lax.axis_index("x") + lax.axis_index("y")).astype(jnp.int32).reshape(1)
    place = jnp.concatenate([lax.axis_index("c").astype(jnp.int32).reshape(1), chip])

    def rows_of(t, n):
        return jnp.swapaxes(t[n][0], 0, 1) if n in TRANSPOSED else t[n][0]

    packed = [_pack(chip, [rows_of(weights, n) for n in GROUPS[0]], "pack_ffn1"),
              _pack(chip, [jnp.concatenate([rows_of(weights, n) for n in GROUPS[1]], axis=0)], "pack_mix"),
              _pack(chip, [rows_of(weights, n) for n in GROUPS[2]], "pack_ffn2")]
    bufs = [p.reshape(N_CHIPS, 2, p.shape[1] // 2, D_MODEL) for p in packed]

    small_w = {"ffn1_norm": ffn1_norm, "mix_norm": mix_norm, "ffn2_norm": ffn2_norm,
               "final_norm": final_norm.reshape(1, D_MODEL), "pool_scale": pool_scale, "sink_logits": sink_logits,
               "pool_w": pool_w[0]}
    loss, grad_x, group_grads, small_sum = _step(x[0], loss_target[0], bufs, small_w, place)

    out_g, out_d, out_m, out_v = {}, {}, {}, {}
    for members, g in zip(GROUPS, group_grads):
        row0 = 0
        for n in members:
            w = rows_of(weights, n)
            tile = FF_CHUNK // 4 if w.shape[0] == FF_CHUNK else math.gcd(IN_ROWS, OUT_ROWS)
            outs = _adamw(w, g, row0, rows_of(mom1, n), rows_of(mom2, n), tile, "adamw_" + n)
            row0 += w.shape[0]
            for dst, t in zip((out_g, out_d, out_m, out_v), outs):
                dst[n] = (jnp.swapaxes(t, 0, 1) if n in TRANSPOSED else t).reshape(weights[n].shape)
    small_outs = _adamw(_pack_small(weights), small_sum, 0, _pack_small(mom1), _pack_small(mom2), SMALL_ROWS, "adamw_small")
    for dst, packed in zip((out_g, out_d, out_m, out_v), small_outs):
        dst.update(_unpack_small(packed, weights))

    return (loss,grad_x.reshape(x.shape), *[out_g[n] for n in names], *[out_d[n] for n in names],
            *[out_m[n] for n in names], *[out_v[n] for n in names])
```

```python
import collections
import functools
import math

import jax
import jax.numpy as jnp
from jax import lax
from jax.experimental import pallas as pl
from jax.experimental.pallas import tpu as pltpu

F32, BF16 = jnp.float32, jnp.bfloat16
MESH = pl.DeviceIdType.MESH

D_MODEL = 1024
D_FF = 2816
N_CHIPS = 4
FF_CHUNK = D_FF // N_CHIPS
HEAD_DIM = 64
N_HEADS = 8
N_KV = 2
Q_PER_KV = N_HEADS // N_KV
KV_WIDTH = N_KV * HEAD_DIM
ATTN_WIDTH = N_HEADS * HEAD_DIM
POOL_WINDOWS = (2, 4, 8, 16)
N_POOL = len(POOL_WINDOWS)
POOL_GROUP = 128
POOL_WIDTH = N_POOL * POOL_GROUP
IN_WIDTH = ATTN_WIDTH + 2 * KV_WIDTH + POOL_WIDTH
WINDOW = 128
BLOCK = 128
BAND = 3 * BLOCK
ROPE_THETA = 500000.0
ROTARY_DIM = HEAD_DIM // 4
EPS = 1e-6
LANES = 128
Q_PAD = N_HEADS * LANES
U_PAD = Q_PAD + 2 * KV_WIDTH + POOL_WIDTH
SCALE = HEAD_DIM ** -0.5
NEG = -1e30

ADAM_LR, ADAM_B1, ADAM_B2, ADAM_EPS, ADAM_WD, ADAM_STEP = 0.001, 0.9, 0.999, 1e-08, 0.01, 10

V7X_VMEM_BYTES = 64 * 1024 * 1024
TOK_TILE = 512
BWD_TILE = 256
SMALL_ROWS = 552


def _params(sem, vmem_mb):
    return pltpu.CompilerParams(dimension_semantics=sem, vmem_limit_bytes=vmem_mb * 1024 * 1024)


def _dot(a, b):
    return lax.dot_general(a, b, (((1,), (0,)), ((), ())), preferred_element_type=F32)


def _dot_nt(a, b):
    return lax.dot_general(a, b, (((1,), (1,)), ((), ())), preferred_element_type=F32)


def _dot_tn(a, b):
    return lax.dot_general(a, b, (((0,), (0,)), ((), ())), preferred_element_type=F32)


def _rms_stats(h):
    r = lax.rsqrt(jnp.mean(h * h, axis=-1, keepdims=True) + EPS)
    return r, h * r


def _rms_bwd(dn, g, r, xh):
    gd = dn * g
    dh = r * (gd - xh * jnp.mean(gd * xh, axis=-1, keepdims=True))
    return dh, jnp.sum(dn * xh, axis=0, keepdims=True)


def _rope(x, c, s1, s2):
    return x * c + pltpu.roll(x, LANES - ROTARY_DIM // 2, 1) * s1 + pltpu.roll(x, ROTARY_DIM // 2, 1) * s2


def _rope_bwd(d, c, s1, s2):
    return d * c + pltpu.roll(d * s1, ROTARY_DIM // 2, 1) + pltpu.roll(d * s2, LANES - ROTARY_DIM // 2, 1)


def _sum_chunks(refs):
    terms = [ref[j].astype(F32) for ref in refs for j in range(ref.shape[0])]
    return functools.reduce(lambda a, b: a + b, terms)


def _chunk_rows(tile, k):
    return pl.BlockSpec((k, tile, D_MODEL), lambda i, *_: (0, i, 0))


def _full(shape):
    nd = len(shape)
    return pl.BlockSpec(shape, lambda *_: (0,) * nd)


def _rows(tile, cols):
    return pl.BlockSpec((tile, cols), lambda i, *_: (i, 0))


HBM_SPEC = pl.BlockSpec(memory_space=pltpu.HBM)

_Rider = collections.namedtuple("_Rider", "operands out_shapes aliases scratch start finish")


_NO_RIDER = _Rider([], [], {}, [], None, None)


def _call(body, name, grid, in_specs, out_specs, out_shape, scratch, vmem_mb, args, rider=None, prefetch=()):
    rider = rider or _NO_RIDER
    n_pre, n_in, n_out, n_scr = len(prefetch), len(in_specs), len(out_specs), len(scratch)
    r_in, r_out = len(rider.operands), len(rider.out_shapes)

    def fused(*refs):
        pre, refs = refs[:n_pre], refs[n_pre:]
        ins, refs = refs[:n_in], refs[n_in:]
        r_ins, refs = refs[:r_in], refs[r_in:]
        outs, refs = refs[:n_out], refs[n_out:]
        r_outs, refs = refs[:r_out], refs[r_out:]
        scr, r_scr = refs[:n_scr], refs[n_scr:]
        ids = [pl.program_id(d) for d in range(len(grid))]
        if rider.start is not None:
            @pl.when(functools.reduce(jnp.logical_and, [i == 0 for i in ids]))
            def _():
                rider.start(r_ins, r_outs, r_scr)

        body(*pre, *ins, *outs, *scr)

        if rider.finish is not None:
            @pl.when(functools.reduce(jnp.logical_and, [i == g - 1 for i, g in zip(ids, grid)]))
            def _():
                rider.finish(r_ins, r_outs, r_scr)

    return pl.pallas_call(
        fused, name=name,
        grid_spec=pltpu.PrefetchScalarGridSpec(
            num_scalar_prefetch=n_pre, grid=grid,
            in_specs=list(in_specs) + [HBM_SPEC] * r_in, out_specs=list(out_specs) + [HBM_SPEC] * r_out,
            scratch_shapes=list(scratch) + list(rider.scratch)),
        out_shape=list(out_shape) + list(rider.out_shapes),
        input_output_aliases={n_pre + n_in + k: n_out + v for k, v in rider.aliases.items()},
        compiler_params=_params(("arbitrary",) * len(grid), vmem_mb),
    )(*prefetch, *args, *rider.operands)


def _comm_call(rider, name):
    r_in, r_out = len(rider.operands), len(rider.out_shapes)

    def body(*refs):
        r_ins, r_outs, r_scr = refs[:r_in], refs[r_in:r_in + r_out], refs[r_in + r_out:]
        rider.start(r_ins, r_outs, r_scr)
        rider.finish(r_ins, r_outs, r_scr)

    return pl.pallas_call(
        body, name=name, in_specs=[HBM_SPEC] * r_in, out_specs=[HBM_SPEC] * r_out, out_shape=list(rider.out_shapes),
        input_output_aliases=dict(rider.aliases), scratch_shapes=list(rider.scratch),
    )(*rider.operands)


def _ffn_fwd(h, gain, group, name, rider=None):
    S = h.shape[0]
    tile = min(TOK_TILE, S)
    nt = S // tile

    def body(h_ref, g_ref, wg_ref, wu_ref, wd_ref, ho_ref, n_ref, gate_ref, up_ref, acc):
        j = pl.program_id(1)

        @pl.when(j == 0)
        def _():
            _, xh = _rms_stats(h_ref[...])
            n_ref[...] = (xh * g_ref[...]).astype(BF16)
            acc[...] = jnp.zeros_like(acc)

        n = n_ref[...]
        gate = _dot_nt(n, wg_ref[0])
        up = _dot_nt(n, wu_ref[0])
        gate_ref[0] = gate.astype(BF16)
        up_ref[0] = up.astype(BF16)
        act = (gate * jax.nn.sigmoid(gate) * up).astype(BF16)
        acc[...] += _dot(act, wd_ref[0])

        @pl.when(j == N_CHIPS - 1)
        def _():
            ho_ref[...] = h_ref[...] + 0.5 * acc[...]

    tok = pl.BlockSpec((tile, D_MODEL), lambda i, j: (i, 0))
    hid = pl.BlockSpec((1, tile, FF_CHUNK), lambda i, j: (j, i, 0))
    return _call(
        body, name, (nt, N_CHIPS),
        [tok, pl.BlockSpec((1, D_MODEL), lambda i, j: (0, 0))]
        + [pl.BlockSpec((1, FF_CHUNK, D_MODEL), functools.partial(lambda i, j, part: (j, part, 0), part=part))
           for part in range(3)],
        [tok, tok, hid, hid],
        [jax.ShapeDtypeStruct((S, D_MODEL), F32), jax.ShapeDtypeStruct((S, D_MODEL), BF16),
         jax.ShapeDtypeStruct((N_CHIPS, S, FF_CHUNK), BF16), jax.ShapeDtypeStruct((N_CHIPS, S, FF_CHUNK), BF16)],
        [pltpu.VMEM((tile, D_MODEL), F32)], 40, (h, gain, group, group, group), rider)


def _ffn_bwd(chunks, d_out, n, gate, up, group, name, rider=None):
    S = n.shape[0]
    n_chunks = chunks.shape[0]
    tile = min(TOK_TILE, S)
    nt = S // tile
    half_rows = 3 * FF_CHUNK // 2
    cut = FF_CHUNK // 2

    def body(chunks_ref, do_ref, n_ref, gate_ref, up_ref, wg_ref, wu_ref, wd_ref, dn_ref, dw_ref, acc_g, acc_u, acc_d):
        j, i = pl.program_id(0), pl.program_id(1)

        @pl.when(i == 0)
        def _():
            acc_g[...] = jnp.zeros_like(acc_g)
            acc_u[...] = jnp.zeros_like(acc_u)
            acc_d[...] = jnp.zeros_like(acc_d)

        do = do_ref[...]
        nn = n_ref[...]
        g = gate_ref[0].astype(F32)
        u = up_ref[0].astype(F32)
        d_act = _dot_nt(do, wd_ref[0])
        sig = jax.nn.sigmoid(g)
        silu = g * sig
        d_up = (d_act * silu).astype(BF16)
        d_gate = (d_act * u * (sig * (1.0 + g * (1.0 - sig)))).astype(BF16)
        act = (silu * u).astype(BF16)
        dn_ref[0] = (_dot(d_gate, wg_ref[0]) + _dot(d_up, wu_ref[0])).astype(BF16)
        acc_g[...] += _dot_tn(d_gate, nn)
        acc_u[...] += _dot_tn(d_up, nn)
        acc_d[...] += _dot_tn(act, do)

        @pl.when(i == nt - 1)
        def _():
            dw_ref[0, 0, :FF_CHUNK, :] = acc_g[...].astype(BF16)
            dw_ref[0, 0, FF_CHUNK:, :] = acc_u[:cut, :].astype(BF16)
            dw_ref[1, 0, :cut, :] = acc_u[cut:, :].astype(BF16)
            dw_ref[1, 0, cut:, :] = acc_d[...].astype(BF16)

    tok = pl.BlockSpec((tile, D_MODEL), lambda j, i, chunks_ref: (i, 0))
    hid = pl.BlockSpec((1, tile, FF_CHUNK), lambda j, i, chunks_ref: (chunks_ref[j], i, 0))
    return _call(
        body, name, (n_chunks, nt),
        [tok, tok, hid, hid]
        + [pl.BlockSpec((1, FF_CHUNK, D_MODEL), functools.partial(lambda j, i, chunks_ref, part: (chunks_ref[j], part, 0), part=part))
           for part in range(3)],
        [pl.BlockSpec((1, tile, D_MODEL), lambda j, i, chunks_ref: (j, i, 0)),
         pl.BlockSpec((2, 1, half_rows, D_MODEL), lambda j, i, chunks_ref: (0, j, 0, 0))],
        [jax.ShapeDtypeStruct((n_chunks, S, D_MODEL), BF16), jax.ShapeDtypeStruct((2, n_chunks, half_rows, D_MODEL), BF16)],
        [pltpu.VMEM((FF_CHUNK, D_MODEL), F32)] * 3, 56, (d_out, n, gate, up, group, group, group), rider, (chunks,))


def _mix_in(h, gain, w_in, rc, rs1, rs2, name):
    S = h.shape[0]
    tile = min(TOK_TILE, S)

    def body(h_ref, g_ref, w_ref, c_ref, s1_ref, s2_ref, n_ref, q_ref, k_ref, v_ref, pc_ref):
        _, xh = _rms_stats(h_ref[...])
        n = (xh * g_ref[...]).astype(BF16)
        n_ref[...] = n
        u = _dot_nt(n, w_ref[...])
        c, s1, s2 = c_ref[...], s1_ref[...], s2_ref[...]
        q_ref[...] = jnp.concatenate([(_rope(u[:, hd * LANES:(hd + 1) * LANES], c, s1, s2) * SCALE).astype(BF16)
                                      for hd in range(N_HEADS)], axis=1)
        k_ref[...] = _rope(u[:, Q_PAD:Q_PAD + KV_WIDTH], c, s1, s2).astype(BF16)
        v_ref[...] = u[:, Q_PAD + KV_WIDTH:Q_PAD + 2 * KV_WIDTH].astype(BF16)
        pc_ref[...] = u[:, Q_PAD + 2 * KV_WIDTH:]

    return pl.pallas_call(
        body, name=name, grid=(S // tile,),
        in_specs=[_rows(tile, D_MODEL), _full((1, D_MODEL)), _full((U_PAD, D_MODEL)),
                  _rows(tile, LANES), _rows(tile, LANES), _rows(tile, LANES)],
        out_specs=[_rows(tile, D_MODEL), _rows(tile, Q_PAD), _rows(tile, KV_WIDTH), _rows(tile, KV_WIDTH),
                   _rows(tile, POOL_WIDTH)],
        out_shape=[jax.ShapeDtypeStruct((S, D_MODEL), BF16), jax.ShapeDtypeStruct((S, Q_PAD), BF16),
                   jax.ShapeDtypeStruct((S, KV_WIDTH), BF16), jax.ShapeDtypeStruct((S, KV_WIDTH), BF16),
                   jax.ShapeDtypeStruct((S, POOL_WIDTH), F32)],
        compiler_params=_params(("parallel",), 40),
    )(h, gain, w_in, rc, rs1, rs2)


def _band_start(i, S):
    return pl.multiple_of(jnp.clip((i - 1) * BLOCK, 0, S - BAND), BLOCK)


def _window_bias(off):
    r = lax.broadcasted_iota(jnp.int32, (BLOCK, 1), 0)
    c = lax.broadcasted_iota(jnp.int32, (1, BAND), 1)
    return jnp.where(jnp.abs(off + r - c) <= WINDOW, 0.0, NEG).astype(F32)


def _softmax_parts(qh, kb, bias, sink_h):
    s = _dot_nt(qh, kb) + bias
    m = jnp.maximum(jnp.max(s, axis=-1, keepdims=True), sink_h)
    p = jnp.exp(s - m)
    es = jnp.exp(sink_h - m)
    return p, es, 1.0 / (jnp.sum(p, axis=-1, keepdims=True) + es)


def _pool_matrix(t0, start, S, w):
    r = lax.broadcasted_iota(jnp.int32, (BLOCK, 1), 0) + t0
    c = lax.broadcasted_iota(jnp.int32, (1, BAND), 1) + start
    half = w // 2

    def window(lo, hi):
        a = jnp.maximum(lo, 0)
        b = jnp.minimum(hi + 1, S)
        return jnp.where((c >= a) & (c < b), 1.0 / (b - a).astype(F32), 0.0)

    return (0.5 * (window(r - half, r + half - 1) + window(r - half + 1, r + half))).astype(BF16)


def _pool_matrices(S):
    blocks = ((0, 0), (BLOCK, 0), (S - BLOCK, S - BAND))
    return jnp.stack([jnp.stack([_pool_matrix(t0, start, S, w) for w in POOL_WINDOWS]) for t0, start in blocks])


def _pool_spec(nb):
    return pl.BlockSpec((1, N_POOL, BLOCK, BAND), lambda i, *_: (jnp.where(i == 0, 0, jnp.where(i == nb - 1, 2, 1)), 0, 0, 0))


def _mix_core_fwd(q, k, v, pc, sink, pool_m, pool_w, pool_scale, name):
    S = q.shape[0]
    nb = S // BLOCK

    def body(sink_ref, q_ref, k_ref, v_ref, pc_ref, pm_ref, pw_ref, ps_ref, a_ref, p_ref):
        i = pl.program_id(0)
        start = _band_start(i, S)
        band = pl.ds(start, BAND)
        bias = _window_bias(i * BLOCK - start)
        kb, vb = k_ref[band, :], v_ref[band, :]
        hs = range(N_HEADS)
        ss = [_dot_nt(q_ref[:, hd * LANES:(hd + 1) * LANES], kb) + bias for hd in hs]
        ms = [jnp.maximum(jnp.max(ss[hd], axis=-1, keepdims=True), sink_ref[0, hd]) for hd in hs]
        ps = [jnp.exp(ss[hd] - ms[hd]) for hd in hs]
        invs = [1.0 / (jnp.sum(ps[hd], axis=-1, keepdims=True) + jnp.exp(sink_ref[0, hd] - ms[hd])) for hd in hs]
        outs = [_dot(ps[hd].astype(BF16), vb) for hd in hs]
        a_ref[...] = jnp.concatenate([(outs[hd] * invs[hd]).astype(BF16) for hd in hs], axis=1)
        centre = pl.ds(pl.multiple_of(i * BLOCK, BLOCK), BLOCK)
        gs = range(N_POOL)
        sl = [slice(g * POOL_GROUP, (g + 1) * POOL_GROUP) for g in gs]
        means = [_dot(pm_ref[0, g], pc_ref[band, sl[g]].astype(BF16)) for g in gs]
        devs = [(means[g] - pc_ref[centre, sl[g]]).astype(BF16) for g in gs]
        p_ref[...] = (jnp.concatenate([_dot(devs[g], pw_ref[g]) for g in gs], axis=1) * ps_ref[...]).astype(BF16)

    return pl.pallas_call(
        body, name=name, grid=(nb,),
        in_specs=[pl.BlockSpec(memory_space=pltpu.SMEM), _rows(BLOCK, Q_PAD), _full((S, KV_WIDTH)), _full((S, KV_WIDTH)),
                  _full((S, POOL_WIDTH)), _pool_spec(nb), _full((N_POOL, POOL_GROUP, POOL_GROUP)), _full((1, POOL_WIDTH))],
        out_specs=[_rows(BLOCK, Q_PAD), _rows(BLOCK, POOL_WIDTH)],
        out_shape=[jax.ShapeDtypeStruct((S, Q_PAD), BF16), jax.ShapeDtypeStruct((S, POOL_WIDTH), BF16)],
        compiler_params=_params(("parallel",), 40),
    )(sink, q, k, v, pc, pool_m, pool_w, pool_scale)


def _mix_core_bwd(q, k, v, pc, da, dp, sink, pool_m, pool_w, pool_scale, rc, rs1, rs2, name, rider=None):
    S = q.shape[0]
    nb = S // BLOCK

    def body(sink_ref, q_ref, k_ref, v_ref, pc_ref, da_ref, dp_ref, pm_ref, pw_ref, ps_ref, c_ref, s1_ref, s2_ref,
             dq_ref, dk_ref, dv_ref, dpc_ref, dsink_ref, dpw_ref, dps_ref):
        i = pl.program_id(0)

        @pl.when(i == 0)
        def _():
            dk_ref[...] = jnp.zeros_like(dk_ref)
            dv_ref[...] = jnp.zeros_like(dv_ref)
            dpc_ref[...] = jnp.zeros_like(dpc_ref)
            dsink_ref[...] = jnp.zeros_like(dsink_ref)
            dpw_ref[...] = jnp.zeros_like(dpw_ref)
            dps_ref[...] = jnp.zeros_like(dps_ref)

        start = _band_start(i, S)
        band = pl.ds(start, BAND)
        bias = _window_bias(i * BLOCK - start)
        kb, vb = k_ref[band, :], v_ref[band, :]
        c, s1, s2 = c_ref[...], s1_ref[...], s2_ref[...]
        lane = lax.broadcasted_iota(jnp.int32, (1, LANES), 1)
        hs = range(N_HEADS)
        qs = [q_ref[:, hd * LANES:(hd + 1) * LANES] for hd in hs]
        das = [da_ref[:, hd * LANES:(hd + 1) * LANES] for hd in hs]
        ss = [_dot_nt(qs[hd], kb) + bias for hd in hs]
        d_probs = [_dot_nt(das[hd], vb) for hd in hs]
        ms = [jnp.maximum(jnp.max(ss[hd], axis=-1, keepdims=True), sink_ref[0, hd]) for hd in hs]
        ps = [jnp.exp(ss[hd] - ms[hd]) for hd in hs]
        ess = [jnp.exp(sink_ref[0, hd] - ms[hd]) for hd in hs]
        invs = [1.0 / (jnp.sum(ps[hd], axis=-1, keepdims=True) + ess[hd]) for hd in hs]
        probs = [ps[hd] * invs[hd] for hd in hs]
        deltas = [jnp.sum(probs[hd] * d_probs[hd], axis=-1, keepdims=True) for hd in hs]
        d_ss = [(probs[hd] * (d_probs[hd] - deltas[hd])).astype(BF16) for hd in hs]
        dqs = [_dot(d_ss[hd], kb) for hd in hs]
        dq_ref[...] = jnp.concatenate([_rope_bwd(dqs[hd] * SCALE, c, s1, s2).astype(BF16) for hd in hs], axis=1)
        dks = [_dot_tn(d_ss[hd], qs[hd]) for hd in hs]
        dvs = [_dot_tn(probs[hd].astype(BF16), das[hd]) for hd in hs]
        dk_ref[band, :] += functools.reduce(lambda a, b: a + b, dks)
        dv_ref[band, :] += functools.reduce(lambda a, b: a + b, dvs)
        dsink_ref[...] += functools.reduce(lambda a, b: a + b, [
            jnp.where(lane == hd, -jnp.sum(ess[hd] * invs[hd] * deltas[hd], axis=0, keepdims=True), 0.0) for hd in hs])

        centre = pl.ds(pl.multiple_of(i * BLOCK, BLOCK), BLOCK)
        gs = range(N_POOL)
        sl = [slice(g * POOL_GROUP, (g + 1) * POOL_GROUP) for g in gs]
        devs = [(_dot(pm_ref[0, g], pc_ref[band, sl[g]].astype(BF16)) - pc_ref[centre, sl[g]]).astype(BF16) for g in gs]
        dys = [dp_ref[:, sl[g]].astype(F32) for g in gs]
        zs = [_dot(devs[g], pw_ref[g]) for g in gs]
        dzs = [(dys[g] * ps_ref[:, sl[g]]).astype(BF16) for g in gs]
        d_devs = [_dot_nt(dzs[g], pw_ref[g]) for g in gs]
        dps_ref[...] += jnp.concatenate([jnp.sum(dys[g] * zs[g], axis=0, keepdims=True) for g in gs], axis=1)
        for g in gs:
            dpw_ref[g] += _dot_tn(devs[g], dzs[g])
        dpc_ref[band, :] += jnp.concatenate([_dot_tn(pm_ref[0, g], d_devs[g].astype(BF16)) for g in gs], axis=1)
        dpc_ref[centre, :] -= jnp.concatenate(d_devs, axis=1)

    return _call(
        body, name, (nb,),
        [pl.BlockSpec(memory_space=pltpu.SMEM), _rows(BLOCK, Q_PAD), _full((S, KV_WIDTH)), _full((S, KV_WIDTH)),
         _full((S, POOL_WIDTH)), _rows(BLOCK, Q_PAD), _rows(BLOCK, POOL_WIDTH), _pool_spec(nb),
         _full((N_POOL, POOL_GROUP, POOL_GROUP)), _full((1, POOL_WIDTH)),
         _rows(BLOCK, LANES), _rows(BLOCK, LANES), _rows(BLOCK, LANES)],
        [_rows(BLOCK, Q_PAD), _full((S, KV_WIDTH)), _full((S, KV_WIDTH)), _full((S, POOL_WIDTH)),
         _full((1, LANES)), _full((N_POOL, POOL_GROUP, POOL_GROUP)), _full((1, POOL_WIDTH))],
        [jax.ShapeDtypeStruct((S, Q_PAD), BF16), jax.ShapeDtypeStruct((S, KV_WIDTH), F32),
         jax.ShapeDtypeStruct((S, KV_WIDTH), F32), jax.ShapeDtypeStruct((S, POOL_WIDTH), F32),
         jax.ShapeDtypeStruct((1, LANES), F32), jax.ShapeDtypeStruct((N_POOL, POOL_GROUP, POOL_GROUP), F32),
         jax.ShapeDtypeStruct((1, POOL_WIDTH), F32)],
        [], 56, (sink, q, k, v, pc, da, dp, pool_m, pool_w, pool_scale, rc, rs1, rs2), rider)


def _mix_out(h, a, p, wa, wp, name):
    S = h.shape[0]
    tile = min(TOK_TILE, S)

    def body(h_ref, a_ref, p_ref, wa_ref, wp_ref, o_ref):
        o_ref[...] = h_ref[...] + _dot(a_ref[...], wa_ref[...]) + _dot(p_ref[...], wp_ref[...])

    return pl.pallas_call(
        body, name=name, grid=(S // tile,),
        in_specs=[_rows(tile, D_MODEL), _rows(tile, Q_PAD), _rows(tile, POOL_WIDTH),
                  _full((Q_PAD, D_MODEL)), _full((POOL_WIDTH, D_MODEL))],
        out_specs=_rows(tile, D_MODEL),
        out_shape=jax.ShapeDtypeStruct((S, D_MODEL), F32),
        compiler_params=_params(("parallel",), 40),
    )(h, a, p, wa, wp)


def _loss_head(h, target, gain, name):
    S = h.shape[0]
    tile = min(TOK_TILE, S)

    def body(h_ref, t_ref, g_ref, dh_ref, dhalf_ref, loss_ref, dg_ref):
        @pl.when(pl.program_id(0) == 0)
        def _():
            loss_ref[...] = jnp.zeros_like(loss_ref)
            dg_ref[...] = jnp.zeros_like(dg_ref)

        g = g_ref[...]
        r, xh = _rms_stats(h_ref[...])
        err = xh * g - t_ref[...]
        loss_ref[...] += (0.5 / D_MODEL) * jnp.sum(err * err, axis=0, keepdims=True)
        dh, dg = _rms_bwd(err * (1.0 / D_MODEL), g, r, xh)
        dg_ref[...] += dg
        dh_ref[...] = dh
        dhalf_ref[...] = (0.5 * dh).astype(BF16)

    return pl.pallas_call(
        body, name=name, grid=(S // tile,),
        in_specs=[_rows(tile, D_MODEL), _rows(tile, D_MODEL), _full((1, D_MODEL))],
        out_specs=[_rows(tile, D_MODEL), _rows(tile, D_MODEL), _full((1, D_MODEL)), _full((1, D_MODEL))],
        out_shape=[jax.ShapeDtypeStruct((S, D_MODEL), F32), jax.ShapeDtypeStruct((S, D_MODEL), BF16),
                   jax.ShapeDtypeStruct((1, D_MODEL), F32), jax.ShapeDtypeStruct((1, D_MODEL), F32)],
        compiler_params=_params(("arbitrary",), 40),
    )(h, target, gain)


def _mix_out_bwd(dh_out, dn, h, gain, a, p, wa, wp, name, rider=None):
    S = h.shape[0]
    tile = min(TOK_TILE, S)

    def body(do_ref, dn_ref, h_ref, g_ref, a_ref, p_ref, wa_ref, wp_ref, dh_ref, da_ref, dp_ref, dwa_ref, dwp_ref, dg_ref):
        @pl.when(pl.program_id(0) == 0)
        def _():
            dwa_ref[...] = jnp.zeros_like(dwa_ref)
            dwp_ref[...] = jnp.zeros_like(dwp_ref)
            dg_ref[...] = jnp.zeros_like(dg_ref)

        r, xh = _rms_stats(h_ref[...])
        dnorm, dg = _rms_bwd(_sum_chunks([dn_ref]), g_ref[...], r, xh)
        dh = do_ref[...] + dnorm
        dg_ref[...] += dg
        dh_ref[...] = dh
        dhb = dh.astype(BF16)
        da_ref[...] = _dot_nt(dhb, wa_ref[...]).astype(BF16)
        dp_ref[...] = _dot_nt(dhb, wp_ref[...]).astype(BF16)
        dwa_ref[...] += _dot_tn(a_ref[...], dhb)
        dwp_ref[...] += _dot_tn(p_ref[...], dhb)

    return _call(
        body, name, (S // tile,),
        [_rows(tile, D_MODEL), _chunk_rows(tile, dn.shape[0]), _rows(tile, D_MODEL), _full((1, D_MODEL)),
         _rows(tile, Q_PAD), _rows(tile, POOL_WIDTH), _full((Q_PAD, D_MODEL)), _full((POOL_WIDTH, D_MODEL))],
        [_rows(tile, D_MODEL), _rows(tile, Q_PAD), _rows(tile, POOL_WIDTH),
         _full((Q_PAD, D_MODEL)), _full((POOL_WIDTH, D_MODEL)), _full((1, D_MODEL))],
        [jax.ShapeDtypeStruct((S, D_MODEL), F32), jax.ShapeDtypeStruct((S, Q_PAD), BF16),
         jax.ShapeDtypeStruct((S, POOL_WIDTH), BF16), jax.ShapeDtypeStruct((Q_PAD, D_MODEL), F32),
         jax.ShapeDtypeStruct((POOL_WIDTH, D_MODEL), F32), jax.ShapeDtypeStruct((1, D_MODEL), F32)],
        [], 48, (dh_out, dn, h, gain, a, p, wa, wp), rider)


def _mix_in_bwd(dh_out, h, gain, n, dq, dk, dv, dpc, rc, rs1, rs2, w_in, name, rider=None):
    S = h.shape[0]
    tile = min(TOK_TILE, S)

    def body(do_ref, h_ref, g_ref, n_ref, dq_ref, dk_ref, dv_ref, dpc_ref, c_ref, s1_ref, s2_ref, w_ref,
             dh_ref, dhalf_ref, dw_ref, dg_ref):
        @pl.when(pl.program_id(0) == 0)
        def _():
            dw_ref[...] = jnp.zeros_like(dw_ref)
            dg_ref[...] = jnp.zeros_like(dg_ref)

        dk = _rope_bwd(dk_ref[...], c_ref[...], s1_ref[...], s2_ref[...]).astype(BF16)
        du = jnp.concatenate([dq_ref[...], dk, dv_ref[...].astype(BF16), dpc_ref[...].astype(BF16)], axis=1)
        dn = _dot(du, w_ref[...])
        dw_ref[...] += _dot_tn(du, n_ref[...])
        r, xh = _rms_stats(h_ref[...])
        dnorm, dg = _rms_bwd(dn, g_ref[...], r, xh)
        dh = do_ref[...] + dnorm
        dg_ref[...] += dg
        dh_ref[...] = dh
        dhalf_ref[...] = (0.5 * dh).astype(BF16)

    return _call(
        body, name, (S // tile,),
        [_rows(tile, D_MODEL), _rows(tile, D_MODEL), _full((1, D_MODEL)), _rows(tile, D_MODEL),
         _rows(tile, Q_PAD), _rows(tile, KV_WIDTH), _rows(tile, KV_WIDTH), _rows(tile, POOL_WIDTH),
         _rows(tile, LANES), _rows(tile, LANES), _rows(tile, LANES), _full((U_PAD, D_MODEL))],
        [_rows(tile, D_MODEL), _rows(tile, D_MODEL), _full((U_PAD, D_MODEL)), _full((1, D_MODEL))],
        [jax.ShapeDtypeStruct((S, D_MODEL), F32), jax.ShapeDtypeStruct((S, D_MODEL), BF16),
         jax.ShapeDtypeStruct((U_PAD, D_MODEL), F32), jax.ShapeDtypeStruct((1, D_MODEL), F32)],
        [], 56, (dh_out, h, gain, n, dq, dk, dv, dpc, rc, rs1, rs2, w_in), rider)


def _norm_bwd(dh_out, dns, h, gain, name):
    S = h.shape[0]
    tile = min(TOK_TILE, S)
    n = len(dns)

    def body(do_ref, *refs):
        h_ref, g_ref, dh_ref, dg_ref = refs[n:]

        @pl.when(pl.program_id(0) == 0)
        def _():
            dg_ref[...] = jnp.zeros_like(dg_ref)

        r, xh = _rms_stats(h_ref[...])
        dnorm, dg = _rms_bwd(_sum_chunks(refs[:n]), g_ref[...], r, xh)
        dg_ref[...] += dg
        dh_ref[...] = do_ref[...] + dnorm

    return _call(
        body, name, (S // tile,),
        [_rows(tile, D_MODEL)] + [_chunk_rows(tile, dn.shape[0]) for dn in dns] + [_rows(tile, D_MODEL), _full((1, D_MODEL))],
        [_rows(tile, D_MODEL), _full((1, D_MODEL))],
        [jax.ShapeDtypeStruct((S, D_MODEL), F32), jax.ShapeDtypeStruct((1, D_MODEL), F32)],
        [], 40, (dh_out, *dns, h, gain))


def _rope_tables(S):
    half = ROTARY_DIM // 2
    inv_freq = ROPE_THETA ** (-jnp.arange(0, ROTARY_DIM, 2, dtype=F32) / ROTARY_DIM)
    dim = jnp.arange(LANES) % HEAD_DIM
    ang = jnp.arange(S, dtype=F32)[:, None] * inv_freq[dim % half][None, :]
    lo, hi = (dim < half)[None, :], ((dim >= half) & (dim < ROTARY_DIM))[None, :]
    c = jnp.where(lo | hi, jnp.cos(ang), 1.0)
    s1 = jnp.where(lo, -jnp.sin(ang), 0.0)
    s2 = jnp.where(hi, jnp.sin(ang), 0.0)
    return c, s1, s2


def _pad_heads(w, axis):
    w = jnp.moveaxis(w, axis, 0)
    heads = w.reshape((N_HEADS, HEAD_DIM) + w.shape[1:])
    zero = jnp.zeros_like(heads)
    first = (jnp.arange(N_HEADS) < Q_PER_KV).reshape((N_HEADS, 1) + (1,) * (w.ndim - 1))
    lo = jnp.where(first, heads, zero)
    hi = jnp.where(first, zero, heads)
    padded = jnp.concatenate([lo, hi], axis=1).reshape((Q_PAD,) + w.shape[1:])
    return jnp.moveaxis(padded, 0, axis)


def _unpad_heads(w, axis):
    w = jnp.moveaxis(w, axis, 0)
    groups = w.reshape((N_HEADS, 2, HEAD_DIM) + w.shape[1:])
    first = (jnp.arange(N_HEADS) < Q_PER_KV).reshape((N_HEADS, 1) + (1,) * (w.ndim - 1))
    heads = jnp.where(first, groups[:, 0], groups[:, 1]).reshape((ATTN_WIDTH,) + w.shape[1:])
    return jnp.moveaxis(heads, 0, axis)


IN_ROWS = IN_WIDTH // N_CHIPS
OUT_ROWS = (ATTN_WIDTH + POOL_WIDTH) // N_CHIPS
MIX_ROWS = IN_ROWS + OUT_ROWS
FFN_ROWS = 3 * FF_CHUNK


def _step(x, target, bufs, small, place):
    S = x.shape[0]
    rc, rs1, rs2 = _rope_tables(S)
    (ffn1,) = _comm_call(_allgather(bufs[:1]), "allgather_ffn1")
    ffn1 = ffn1.reshape(N_CHIPS, FFN_ROWS, D_MODEL)
    h1, n1, gate1, up1, mix, ffn2 = _ffn_fwd(x, small["ffn1_norm"], ffn1, "ffn1_fwd", _allgather(bufs[1:]))
    mix, ffn2 = mix.reshape(N_CHIPS, MIX_ROWS, D_MODEL), ffn2.reshape(N_CHIPS, FFN_ROWS, D_MODEL)
    w_in_t = mix[:, :IN_ROWS].reshape(IN_WIDTH, D_MODEL)
    w_in_pad = jnp.concatenate([_pad_heads(w_in_t[:ATTN_WIDTH], 0), w_in_t[ATTN_WIDTH:]], axis=0)
    w_out = mix[:, IN_ROWS:].reshape(ATTN_WIDTH + POOL_WIDTH, D_MODEL)
    wa = _pad_heads(w_out[:ATTN_WIDTH], 0)
    wp = w_out[ATTN_WIDTH:]
    pool_w = small["pool_w"].astype(BF16)

    n2, q, k, v, pc = _mix_in(h1, small["mix_norm"], w_in_pad, rc, rs1, rs2, "mix_in")
    pool_m = _pool_matrices(S)
    a, p = _mix_core_fwd(q, k, v, pc, small["sink_logits"], pool_m, pool_w, small["pool_scale"], "mix_core_fwd")
    h2 = _mix_out(h1, a, p, wa, wp, "mix_out")
    h3, n3, gate2, up2 = _ffn_fwd(h2, small["ffn2_norm"], ffn2, "ffn2_fwd")
    dh3, dhalf3, loss_lanes, d_final = _loss_head(h3, target, small["final_norm"], "loss_head")

    dn3, d_ffn2 = _ffn_bwd(jnp.arange(N_CHIPS, dtype=jnp.int32), dhalf3, n3, gate2, up2, ffn2, "ffn2_bwd")
    dh2, da, dp, dwa, dwp, d_ffn2_norm, received = _mix_out_bwd(dh3, dn3, h2, small["ffn2_norm"], a, p, wa, wp, "mix_out_bwd",
                                                                _sibling_exchange([d_ffn2]))
    pair = _pair_sum(place, d_ffn2, received, "grad_pair_sum_ffn2")
    dq, dk, dv, dpc, dsink, dpool_w, dpool_scale, stack = _mix_core_bwd(
        q, k, v, pc, da, dp, small["sink_logits"], pool_m, pool_w, small["pool_scale"], rc, rs1, rs2, "mix_core_bwd",
        _scatter([pair]))
    reduced = _chip_sum(place, pair, stack, 2, "grad_chip_sum_ffn2")
    dh1, dhalf1, dw_in_pad, d_mix_norm, g_ffn2 = _mix_in_bwd(dh2, h1, small["mix_norm"], n2, dq, dk, dv, dpc, rc, rs1, rs2,
                                                             w_in_pad, "mix_in_bwd", _sibling_share([reduced]))
    dw_in_t = jnp.concatenate([_unpad_heads(dw_in_pad[:Q_PAD], 0), dw_in_pad[Q_PAD:]], axis=0)
    dw_out = jnp.concatenate([_unpad_heads(dwa, 0), dwp], axis=0)
    d_mix = jnp.concatenate([dw_in_t.reshape(N_CHIPS, IN_ROWS, D_MODEL), dw_out.reshape(N_CHIPS, OUT_ROWS, D_MODEL)], axis=1)
    d_mix = jnp.transpose(d_mix.reshape(N_CHIPS, 2, MIX_ROWS // 2, D_MODEL), (1, 0, 2, 3)).astype(BF16)
    small_g = {"ffn1_norm": jnp.zeros_like(d_mix_norm), "mix_norm": d_mix_norm, "ffn2_norm": d_ffn2_norm,
               "final_norm": d_final, "pool_scale": dpool_scale, "sink_logits": dsink[:, :N_HEADS], "pool_w": dpool_w}
    loss_row = jnp.sum(loss_lanes.reshape(D_MODEL // LANES, LANES), axis=0, keepdims=True)
    small_early = _pack_small(small_g, loss_row)

    chunk = [(place[1:] + 1 + p) % N_CHIPS for p in range(N_CHIPS)]
    ffn1_bwd = functools.partial(_ffn_bwd, d_out=dhalf1, n=n1, gate=gate1, up=up1, group=ffn1)
    stack = jnp.zeros((N_CHIPS, FFN_ROWS // 2, D_MODEL), BF16)
    dn_a, dw_a, recv_mix, small_all = ffn1_bwd(chunk[0], name="ffn1_bwd_0",
                                               rider=_merge(_sibling_exchange([d_mix]), _small_allgather(small_early)))
    pair_mix = _pair_sum(place, d_mix, recv_mix, "grad_pair_sum_mix")
    dn_b, dw_b, recv_a, stack_mix = ffn1_bwd(chunk[1], name="ffn1_bwd_1",
                                             rider=_merge(_sibling_exchange([dw_a]), _scatter([pair_mix])))
    pair_a = _pair_sum(place, dw_a, recv_a, "grad_pair_sum_ffn1_0")
    reduced_mix = _chip_sum(place, pair_mix, stack_mix, 2, "grad_chip_sum_mix")
    dn_c, dw_c, recv_b, stack, g_mix = ffn1_bwd(
        chunk[2], name="ffn1_bwd_2",
        rider=_merge(_merge(_sibling_exchange([dw_b]), _scatter_step(pair_a, stack, 0)), _sibling_share([reduced_mix])))
    pair_b = _pair_sum(place, dw_b, recv_b, "grad_pair_sum_ffn1_1")
    dn_d, dw_d, recv_c, stack = ffn1_bwd(chunk[3], name="ffn1_bwd_3",
                                         rider=_merge(_sibling_exchange([dw_c]), _scatter_step(pair_b, stack, 1)))
    pair_c = _pair_sum(place, dw_c, recv_c, "grad_pair_sum_ffn1_2")
    grad_x, d_ffn1_norm = _norm_bwd(dh1, [dn_a, dn_b, dn_c, dn_d], x, small["ffn1_norm"], "norm1_bwd")

    recv_d, stack, gains = _comm_call(
        _merge(_merge(_sibling_exchange([dw_d]), _scatter_step(pair_c, stack, 2)),
               _small_allgather(d_ffn1_norm.reshape(-1, LANES))), "grad_tail")
    pair_d = _pair_sum(place, dw_d, recv_d, "grad_pair_sum_ffn1_3")
    reduced_ffn1 = _chip_sum(place, pair_d, stack, 2, "grad_chip_sum_ffn1")
    (g_ffn1,) = _comm_call(_sibling_share([reduced_ffn1]), "grad_share_tail")
    gain_sum = _sum_leading(gains, 1, "gain_grad_sum")
    small_sum = jnp.concatenate([gain_sum, _sum_leading(small_all, 1, "small_grad_sum")[gain_sum.shape[0]:]], axis=0)
    return jnp.sum(small_sum[SMALL_ROWS - 1]), grad_x, [g.reshape(-1, D_MODEL) for g in (g_ffn1, g_mix, g_ffn2)], small_sum


GROUPS =(("ffn1_w_gate", "ffn1_w_up", "ffn1_w_down"), ("w_in", "w_out"), ("ffn2_w_gate", "ffn2_w_up", "ffn2_w_down"))
TRANSPOSED = ("ffn1_w_gate", "ffn1_w_up", "w_in", "ffn2_w_gate", "ffn2_w_up")


def _place():
    x, y, c = lax.axis_index("x"), lax.axis_index("y"), lax.axis_index("c")
    chips = [(1 - x, y), (x, 1 - y), (1 - x, 1 - y)]
    return x, y, c, chips


def _remote(src, dst, send_sem, recv_sem, to):
    return pltpu.make_async_remote_copy(src_ref=src, dst_ref=dst, send_sem=send_sem, recv_sem=recv_sem,
                                        device_id=to, device_id_type=MESH)


def _pack(chip, members, name):
    rows = members[0].shape[0]
    n = len(members)

    def body(chip_ref, *refs):
        ins, out_ref, buf, sems = refs[:n], refs[n], refs[n + 1], refs[n + 2]
        copies = [pltpu.make_async_copy(ins[k], buf.at[k], sems.at[k]) for k in range(n)]
        for cp in copies:
            cp.start()
        for k in range(n):
            copies[k].wait()
            out_ref[0, k * rows:(k + 1) * rows, :] = buf[k].astype(BF16)

    return pl.pallas_call(
        body, name=name,
        grid_spec=pltpu.PrefetchScalarGridSpec(
            num_scalar_prefetch=1, grid=(1,),
            in_specs=[HBM_SPEC] * n,
            out_specs=pl.BlockSpec((1, n * rows, D_MODEL), lambda k, chip_ref: (chip_ref[0], 0, 0)),
            scratch_shapes=[pltpu.VMEM((n, rows, D_MODEL), F32), pltpu.SemaphoreType.DMA((n,))]),
        out_shape=jax.ShapeDtypeStruct((N_CHIPS, n * rows, D_MODEL), BF16),
        compiler_params=_params(("arbitrary",), 40),
    )(chip, *members)


def _same(arrays):
    return [jax.ShapeDtypeStruct(a.shape, a.dtype) for a in arrays]


def _allgather(bufs):
    n = len(bufs)

    def copies(outs, send_sems, recv_sems, started_only=False):
        x, y, c, chips = _place()
        me = 2 * x + y
        slots = [2 * cx + cy for cx, cy in chips]
        first = [[_remote(outs[a].at[me, c], outs[a].at[me, c], send_sems.at[6 * a + k], recv_sems.at[6 * a + k], (*chips[k], c))
                  for k in range(3)] for a in range(n)]
        if started_only:
            return first
        passed = [[_remote(outs[a].at[slots[k], c], outs[a].at[slots[k], c], send_sems.at[6 * a + 3 + k],
                           recv_sems.at[6 * a + 3 + k], (x, y, 1 - c)) for k in range(3)] for a in range(n)]
        landed = [[_remote(outs[a].at[me, c], outs[a].at[slots[k], c], send_sems.at[6 * a + k], recv_sems.at[6 * a + k],
                           (*chips[k], c)) for k in range(3)] for a in range(n)]
        handed = [[_remote(outs[a].at[me, c], outs[a].at[slots[k], 1 - c], send_sems.at[6 * a + 3 + k],
                           recv_sems.at[6 * a + 3 + k], (x, y, 1 - c)) for k in range(3)] for a in range(n)]
        return first, passed, landed, handed

    def start(ins, outs, sems):
        for per_buf in copies(outs, *sems, started_only=True):
            for cp in per_buf:
                cp.start()

    def finish(ins, outs, sems):
        first, passed, landed, handed = copies(outs, *sems)
        for a in range(n):
            for k in range(3):
                landed[a][k].wait_recv()
                passed[a][k].start()
        for a in range(n):
            for k in range(3):
                handed[a][k].wait_recv()
        for a in range(n):
            for cp in first[a] + passed[a]:
                cp.wait_send()

    return _Rider(list(bufs), _same(bufs), {a: a for a in range(n)},
                  [pltpu.SemaphoreType.DMA((6 * n,)), pltpu.SemaphoreType.DMA((6 * n,))], start, finish)


def _sibling_exchange(parts):
    n = len(parts)

    def copies(ins, outs, send_sems, recv_sems):
        x, y, c, _ = _place()
        return [_remote(ins[a].at[1 - c], outs[a], send_sems.at[a], recv_sems.at[a], (x, y, 1 - c)) for a in range(n)]

    def start(ins, outs, sems):
        for cp in copies(ins, outs, *sems):
            cp.start()

    def finish(ins, outs, sems):
        for cp in copies(ins, outs, *sems):
            cp.wait_recv()
            cp.wait_send()

    return _Rider(list(parts), [jax.ShapeDtypeStruct(p.shape[1:], p.dtype) for p in parts], {},
                  [pltpu.SemaphoreType.DMA((n,)), pltpu.SemaphoreType.DMA((n,))], start, finish)


def _small_allgather(small):
    flips = [(fx, fy, fc) for fx in range(2) for fy in range(2) for fc in range(2)][1:]

    def copies(small_ref, gather_ref, send_sems, recv_sems, local_sem, started_only=False):
        x, y, c, _ = _place()
        me = 4 * x + 2 * y + c
        peers = [((1 - x) if fx else x, (1 - y) if fy else y, (1 - c) if fc else c) for fx, fy, fc in flips]
        own = pltpu.make_async_copy(small_ref, gather_ref.at[me], local_sem)
        sent = [_remote(small_ref, gather_ref.at[me], send_sems.at[k], recv_sems.at[k], peer) for k, peer in enumerate(peers)]
        if started_only:
            return own, sent
        landed = [_remote(small_ref, gather_ref.at[4 * px + 2 * py + pc], send_sems.at[k], recv_sems.at[k], (px, py, pc))
                  for k, (px, py, pc) in enumerate(peers)]
        return own, sent, landed

    def start(ins, outs, sems):
        own, sent = copies(ins[0], outs[0], *sems, started_only=True)
        own.start()
        for cp in sent:
            cp.start()

    def finish(ins, outs, sems):
        own, sent, landed = copies(ins[0], outs[0], *sems)
        for cp in landed:
            cp.wait_recv()
        for cp in sent:
            cp.wait_send()
        own.wait()

    return _Rider([small], [jax.ShapeDtypeStruct((2 * N_CHIPS,) + small.shape, small.dtype)], {},
                  [pltpu.SemaphoreType.DMA((7,)), pltpu.SemaphoreType.DMA((7,)), pltpu.SemaphoreType.DMA], start, finish)


def _merge(a, b):
    na, nao, nas = len(a.operands), len(a.out_shapes), len(a.scratch)

    def start(ins, outs, sems):
        a.start(ins[:na], outs[:nao], sems[:nas])
        b.start(ins[na:], outs[nao:], sems[nas:])

    def finish(ins, outs, sems):
        a.finish(ins[:na], outs[:nao], sems[:nas])
        b.finish(ins[na:], outs[nao:], sems[nas:])

    aliases = {**a.aliases, **{na + k: nao + v for k, v in b.aliases.items()}}
    return _Rider(a.operands + b.operands, a.out_shapes + b.out_shapes, aliases, a.scratch + b.scratch, start, finish)


def _scatter_step(pair, stack, step):
    def copies(pair_ref, stack_ref, send_sem, recv_sem, started_only=False):
        x, y, c, _ = _place()
        me = 2 * x + y
        to = (me + 1 + step) % N_CHIPS
        frm = (me + N_CHIPS - 1 - step) % N_CHIPS
        sent = _remote(pair_ref.at[0], stack_ref.at[me], send_sem, recv_sem, (to // 2, to % 2, c))
        if started_only:
            return sent
        landed = _remote(pair_ref.at[0], stack_ref.at[frm], send_sem, recv_sem, (frm // 2, frm % 2, c))
        return sent, landed

    def start(ins, outs, sems):
        copies(ins[0], outs[0], *sems, started_only=True).start()

    def finish(ins, outs, sems):
        sent, landed = copies(ins[0], outs[0], *sems)
        landed.wait_recv()
        sent.wait_send()

    return _Rider([pair, stack], _same([stack]), {1: 0}, [pltpu.SemaphoreType.DMA, pltpu.SemaphoreType.DMA], start, finish)


def _scatter(sums):
    n = len(sums)

    def copies(ins, outs, send_sems, recv_sems, started_only=False):
        x, y, c, chips = _place()
        me = 2 * x + y
        slots = [2 * cx + cy for cx, cy in chips]
        sent = [_remote(ins[a].at[slots[k]], outs[a].at[me], send_sems.at[3 * a + k], recv_sems.at[3 * a + k], (*chips[k], c))
                for a in range(n) for k in range(3)]
        if started_only:
            return sent
        landed = [_remote(ins[a].at[slots[k]], outs[a].at[slots[k]], send_sems.at[3 * a + k], recv_sems.at[3 * a + k],
                          (*chips[k], c)) for a in range(n) for k in range(3)]
        return sent, landed

    def start(ins, outs, sems):
        for cp in copies(ins, outs, *sems, started_only=True):
            cp.start()

    def finish(ins, outs, sems):
        sent, landed = copies(ins, outs, *sems)
        for cp in landed:
            cp.wait_recv()
        for cp in sent:
            cp.wait_send()

    return _Rider(list(sums), _same(sums), {}, [pltpu.SemaphoreType.DMA((3 * n,)), pltpu.SemaphoreType.DMA((3 * n,))],
                  start, finish)


def _sibling_share(bufs):
    n = len(bufs)

    def copies(outs, send_sems, recv_sems, started_only=False):
        x, y, c, _ = _place()
        sent = [_remote(outs[a].at[c], outs[a].at[c], send_sems.at[a], recv_sems.at[a], (x, y, 1 - c)) for a in range(n)]
        if started_only:
            return sent
        landed = [_remote(outs[a].at[c], outs[a].at[1 - c], send_sems.at[a], recv_sems.at[a], (x, y, 1 - c)) for a in range(n)]
        return sent, landed

    def start(ins, outs, sems):
        for cp in copies(outs, *sems, started_only=True):
            cp.start()

    def finish(ins, outs, sems):
        sent, landed = copies(outs, *sems)
        for cp in landed:
            cp.wait_recv()
        for cp in sent:
            cp.wait_send()

    return _Rider(list(bufs), _same(bufs), {a: a for a in range(n)},
                  [pltpu.SemaphoreType.DMA((n,)), pltpu.SemaphoreType.DMA((n,))], start, finish)


def _pair_sum(core, part, received, name):
    _, k, rh, cols = part.shape

    def body(core_ref, p_ref, r_ref, o_ref):
        o_ref[...] = (p_ref[0].astype(F32) + r_ref[...].astype(F32)).astype(BF16)

    return pl.pallas_call(
        body, name=name,
        grid_spec=pltpu.PrefetchScalarGridSpec(
            num_scalar_prefetch=1, grid=(k,),
            in_specs=[pl.BlockSpec((1, 1, rh, cols), lambda j, core_ref: (core_ref[0], j, 0, 0)),
                      pl.BlockSpec((1, rh, cols), lambda j, core_ref: (j, 0, 0))],
            out_specs=pl.BlockSpec((1, rh, cols), lambda j, core_ref: (j, 0, 0))),
        out_shape=jax.ShapeDtypeStruct((k, rh, cols), BF16),
        compiler_params=_params(("parallel",), 32),
    )(core, part, received)


def _sum_leading(stack, steps, name):
    k, rows, cols = stack.shape
    tile = rows // steps

    def body(s_ref, o_ref):
        total = s_ref[0].astype(F32)
        for d in range(1, k):
            total = total + s_ref[d].astype(F32)
        o_ref[...] = total

    return pl.pallas_call(
        body, name=name, grid=(steps,),
        in_specs=[pl.BlockSpec((k, tile, cols), lambda i: (0, i, 0))],
        out_specs=pl.BlockSpec((tile, cols), lambda i: (i, 0)),
        out_shape=jax.ShapeDtypeStruct((rows, cols), F32),
        compiler_params=_params(("parallel",), 32),
    )(stack)


def _chip_sum(place, own, stack, steps, name):
    k, rows, cols = stack.shape
    tile = rows // steps

    def body(place_ref, own_ref, *refs):
        chip = place_ref[1]
        total = None
        for d in range(k):
            term = jnp.where(chip == d, own_ref[0], refs[d][0]).astype(F32)
            total = term if total is None else total + term
        refs[k][0] = total

    def other(d):
        return lambda i, place_ref: (jnp.where(place_ref[1] == d, (d + 1) % k, d), i, 0)

    return pl.pallas_call(
        body, name=name,
        grid_spec=pltpu.PrefetchScalarGridSpec(
            num_scalar_prefetch=1, grid=(steps,),
            in_specs=[pl.BlockSpec((1, tile, cols), lambda i, place_ref: (place_ref[1] % own.shape[0], i, 0))]
            + [pl.BlockSpec((1, tile, cols), other(d)) for d in range(k)],
            out_specs=pl.BlockSpec((1, tile, cols), lambda i, place_ref: (place_ref[0], i, 0))),
        out_shape=jax.ShapeDtypeStruct((2, rows, cols), F32),
        compiler_params=_params(("arbitrary",), 32),
    )(place, own, *([stack] * k))


def _adamw(w, g, row0, m, v, tile, name):
    rows, cols = w.shape
    first = row0 // tile
    assert rows % tile == 0 and row0 % tile == 0
    bc1 = 1.0 - ADAM_B1 ** ADAM_STEP
    bc2 = 1.0 - ADAM_B2 ** ADAM_STEP

    def body(w_ref, g_ref, m_ref, v_ref, go_ref, d_ref, mo_ref, vo_ref):
        g = g_ref[...]
        m_new = ADAM_B1 * m_ref[...] + (1.0 - ADAM_B1) * g
        v_new = ADAM_B2 * v_ref[...] + (1.0 - ADAM_B2) * (g * g)
        go_ref[...] = g
        d_ref[...] = -ADAM_LR * ((m_new / bc1) / (jnp.sqrt(v_new / bc2) + ADAM_EPS) + ADAM_WD * w_ref[...])
        mo_ref[...] = m_new
        vo_ref[...] = v_new

    spec = pl.BlockSpec((tile, cols), lambda i: (i, 0))
    g_spec = pl.BlockSpec((tile, cols), lambda i: (first + i, 0))
    return pl.pallas_call(
        body, name=name, grid=(rows // tile,),
        in_specs=[spec, g_spec, spec, spec], out_specs=[spec] * 4,
        out_shape=[jax.ShapeDtypeStruct((rows, cols), F32)] * 4,
        compiler_params=_params(("parallel",), 32),
    )(w, g, m, v)


SMALL = ("ffn1_norm", "mix_norm", "ffn2_norm", "final_norm", "pool_scale", "sink_logits", "pool_w")


def _pack_small(d, last_row=None):
    sink = jnp.pad(d["sink_logits"].reshape(1, N_HEADS), ((0, 0), (0, LANES - N_HEADS)))
    rows = [d[n].reshape(-1, LANES) for n in SMALL[:5]] + [sink, d["pool_w"].reshape(-1, LANES)]
    used = sum(r.shape[0] for r in rows)
    last = jnp.zeros((1, LANES), F32) if last_row is None else last_row
    return jnp.concatenate(rows + [jnp.zeros((SMALL_ROWS - used - 1, LANES), F32), last], axis=0)


def _unpack_small(packed, like):
    out, row = {}, 0
    for n in SMALL:
        size = LANES if n == "sink_logits" else math.prod(like[n].shape)
        chunk = packed[row:row + size // LANES].reshape(-1)
        out[n] = (chunk[:N_HEADS] if n == "sink_logits" else chunk).reshape(like[n].shape)
        row += size // LANES
    return out


def kernel(x, ffn1_norm, ffn1_w_gate, ffn1_w_up, ffn1_w_down, mix_norm, w_in, sink_logits, pool_w, pool_scale, w_out, ffn2_norm, ffn2_w_gate, ffn2_w_up, ffn2_w_down, final_norm, loss_target, m_ffn1_norm, m_ffn1_w_gate, m_ffn1_w_up, m_ffn1_w_down, m_mix_norm, m_w_in, m_sink_logits, m_pool_w, m_pool_scale, m_w_out, m_ffn2_norm, m_ffn2_w_gate, m_ffn2_w_up, m_ffn2_w_down, m_final_norm, v_ffn1_norm, v_ffn1_w_gate, v_ffn1_w_up, v_ffn1_w_down, v_mix_norm, v_w_in, v_sink_logits, v_pool_w, v_pool_scale, v_w_out, v_ffn2_norm, v_ffn2_w_gate, v_ffn2_w_up, v_ffn2_w_down, v_final_norm):
    names = ("ffn1_norm", "ffn1_w_gate", "ffn1_w_up", "ffn1_w_down", "mix_norm", "w_in", "sink_logits", "pool_w",
             "pool_scale", "w_out", "ffn2_norm", "ffn2_w_gate", "ffn2_w_up", "ffn2_w_down", "final_norm")
    weights = dict(zip(names, (ffn1_norm, ffn1_w_gate, ffn1_w_up, ffn1_w_down, mix_norm, w_in, sink_logits, pool_w,
                               pool_scale, w_out, ffn2_norm, ffn2_w_gate, ffn2_w_up, ffn2_w_down, final_norm)))
    mom1 = dict(zip(names, (m_ffn1_norm, m_ffn1_w_gate, m_ffn1_w_up, m_ffn1_w_down, m_mix_norm, m_w_in, m_sink_logits,
                            m_pool_w, m_pool_scale, m_w_out, m_ffn2_norm, m_ffn2_w_gate, m_ffn2_w_up, m_ffn2_w_down,
                            m_final_norm)))
    mom2 = dict(zip(names, (v_ffn1_norm, v_ffn1_w_gate, v_ffn1_w_up, v_ffn1_w_down, v_mix_norm, v_w_in, v_sink_logits,
                            v_pool_w, v_pool_scale, v_w_out, v_ffn2_norm, v_ffn2_w_gate, v_ffn2_w_up, v_ffn2_w_down,
                            v_final_norm)))
    chip = (2 * lax.axis_index("x") + lax.axis_index("y")).astype(jnp.int32).reshape(1)
    place = jnp.concatenate([lax.axis_index("c").astype(jnp.int32).reshape(1), chip])

    def rows_of(t, n):
        return jnp.swapaxes(t[n][0], 0, 1) if n in TRANSPOSED else t[n][0]

    packed = [_pack(chip, [rows_of(weights, n) for n in GROUPS[0]], "pack_ffn1"),
              _pack(chip, [jnp.concatenate([rows_of(weights, n) for n in GROUPS[1]], axis=0)], "pack_mix"),
              _pack(chip, [rows_of(weights, n) for n in GROUPS[2]], "pack_ffn2")]
    bufs = [p.reshape(N_CHIPS, 2, p.shape[1] // 2, D_MODEL) for p in packed]

    small_w = {"ffn1_norm": ffn1_norm, "mix_norm": mix_norm, "ffn2_norm": ffn2_norm,
               "final_norm": final_norm.reshape(1, D_MODEL), "pool_scale": pool_scale, "sink_logits": sink_logits,
               "pool_w": pool_w[0]}
    loss, grad_x, group_grads, small_sum = _step(x[0], loss_target[0], bufs, small_w, place)

    out_g, out_d, out_m, out_v = {}, {}, {}, {}
    for members, g in zip(GROUPS, group_grads):
        row0 = 0
        for n in members:
            w = rows_of(weights, n)
            tile = FF_CHUNK // 4 if w.shape[0] == FF_CHUNK else math.gcd(IN_ROWS, OUT_ROWS)
            outs = _adamw(w, g, row0, rows_of(mom1, n), rows_of(mom2, n), tile, "adamw_" + n)
            row0 += w.shape[0]
            for dst, t in zip((out_g, out_d, out_m, out_v), outs):
                dst[n] = (jnp.swapaxes(t, 0, 1) if n in TRANSPOSED else t).reshape(weights[n].shape)
    small_outs = _adamw(_pack_small(weights), small_sum, 0, _pack_small(mom1), _pack_small(mom2), SMALL_ROWS, "adamw_small")
    for dst, packed in zip((out_g, out_d, out_m, out_v), small_outs):
        dst.update(_unpack_small(packed, weights))

    return (loss,grad_x.reshape(x.shape), *[out_g[n] for n in names], *[out_d[n] for n in names],
            *[out_m[n] for n in names], *[out_v[n] for n in names])
```

```python
import collections
import functools
import math

import jax
import jax.numpy as jnp
from jax import lax
from jax.experimental import pallas as pl
from jax.experimental.pallas import tpu as pltpu

F32, BF16 = jnp.float32, jnp.bfloat16
MESH = pl.DeviceIdType.MESH

D_MODEL = 1024
D_FF = 2816
N_CHIPS = 4
FF_CHUNK = D_FF // N_CHIPS
HEAD_DIM = 64
N_HEADS = 8
N_KV = 2
Q_PER_KV = N_HEADS // N_KV
KV_WIDTH = N_KV * HEAD_DIM
ATTN_WIDTH = N_HEADS * HEAD_DIM
POOL_WINDOWS = (2, 4, 8, 16)
N_POOL = len(POOL_WINDOWS)
POOL_GROUP = 128
POOL_WIDTH = N_POOL * POOL_GROUP
IN_WIDTH = ATTN_WIDTH + 2 * KV_WIDTH + POOL_WIDTH
WINDOW = 128
BLOCK = 128
BAND = 3 * BLOCK
ROPE_THETA = 500000.0
ROTARY_DIM = HEAD_DIM // 4
EPS = 1e-6
LANES = 128
Q_PAD = N_HEADS * LANES
U_PAD = Q_PAD + 2 * KV_WIDTH + POOL_WIDTH
SCALE = HEAD_DIM ** -0.5
NEG = -1e30

ADAM_LR, ADAM_B1, ADAM_B2, ADAM_EPS, ADAM_WD, ADAM_STEP = 0.001, 0.9, 0.999, 1e-08, 0.01, 10

V7X_VMEM_BYTES = 64 * 1024 * 1024
TOK_TILE = 512
BWD_TILE = 256
SMALL_ROWS = 552


CALL_VMEM_BYTES = 60000 * 1024


def _params(sem, vmem_mb):
    assert vmem_mb * 1024 * 1024 <= CALL_VMEM_BYTES <= V7X_VMEM_BYTES
    return pltpu.CompilerParams(dimension_semantics=sem, vmem_limit_bytes=CALL_VMEM_BYTES)


def _dot(a, b):
    return lax.dot_general(a, b, (((1,), (0,)), ((), ())), preferred_element_type=F32)


def _dot_nt(a, b):
    return lax.dot_general(a, b, (((1,), (1,)), ((), ())), preferred_element_type=F32)


def _dot_tn(a, b):
    return lax.dot_general(a, b, (((0,), (0,)), ((), ())), preferred_element_type=F32)


def _rms_stats(h):
    r = lax.rsqrt(jnp.mean(h * h, axis=-1, keepdims=True) + EPS)
    return r, h * r


def _rms_bwd(dn, g, r, xh):
    gd = dn * g
    dh = r * (gd - xh * jnp.mean(gd * xh, axis=-1, keepdims=True))
    return dh, jnp.sum(dn * xh, axis=0, keepdims=True)


def _rope(x, c, s1, s2):
    return x * c + pltpu.roll(x, LANES - ROTARY_DIM // 2, 1) * s1 + pltpu.roll(x, ROTARY_DIM // 2, 1) * s2


def _rope_bwd(d, c, s1, s2):
    return d * c + pltpu.roll(d * s1, ROTARY_DIM // 2, 1) + pltpu.roll(d * s2, LANES - ROTARY_DIM // 2, 1)


def _sum_chunks(refs):
    terms = [ref[j].astype(F32) for ref in refs for j in range(ref.shape[0])]
    return functools.reduce(lambda a, b: a + b, terms)


def _chunk_rows(tile, k):
    return pl.BlockSpec((k, tile, D_MODEL), lambda i, *_: (0, i, 0))


def _full(shape):
    nd = len(shape)
    return pl.BlockSpec(shape, lambda *_: (0,) * nd)


def _rows(tile, cols):
    return pl.BlockSpec((tile, cols), lambda i, *_: (i, 0))


HBM_SPEC = pl.BlockSpec(memory_space=pltpu.HBM)

_Rider = collections.namedtuple("_Rider", "operands out_shapes aliases scratch start finish")


_NO_RIDER = _Rider([], [], {}, [], None, None)


def _call(body, name, grid, in_specs, out_specs, out_shape, scratch, vmem_mb, args, rider=None, prefetch=()):
    rider = rider or _NO_RIDER
    n_pre, n_in, n_out, n_scr = len(prefetch), len(in_specs), len(out_specs), len(scratch)
    r_in, r_out = len(rider.operands), len(rider.out_shapes)

    def fused(*refs):
        pre, refs = refs[:n_pre], refs[n_pre:]
        ins, refs = refs[:n_in], refs[n_in:]
        r_ins, refs = refs[:r_in], refs[r_in:]
        outs, refs = refs[:n_out], refs[n_out:]
        r_outs, refs = refs[:r_out], refs[r_out:]
        scr, r_scr = refs[:n_scr], refs[n_scr:]
        ids = [pl.program_id(d) for d in range(len(grid))]
        if rider.start is not None:
            @pl.when(functools.reduce(jnp.logical_and, [i == 0 for i in ids]))
            def _():
                rider.start(r_ins, r_outs, r_scr)

        body(*pre, *ins, *outs, *scr)

        if rider.finish is not None:
            @pl.when(functools.reduce(jnp.logical_and, [i == g - 1 for i, g in zip(ids, grid)]))
            def _():
                rider.finish(r_ins, r_outs, r_scr)

    return pl.pallas_call(
        fused, name=name,
        grid_spec=pltpu.PrefetchScalarGridSpec(
            num_scalar_prefetch=n_pre, grid=grid,
            in_specs=list(in_specs) + [HBM_SPEC] * r_in, out_specs=list(out_specs) + [HBM_SPEC] * r_out,
            scratch_shapes=list(scratch) + list(rider.scratch)),
        out_shape=list(out_shape) + list(rider.out_shapes),
        input_output_aliases={n_pre + n_in + k: n_out + v for k, v in rider.aliases.items()},
        compiler_params=_params(("arbitrary",) * len(grid), vmem_mb),
    )(*prefetch, *args, *rider.operands)


def _comm_call(rider, name):
    r_in, r_out = len(rider.operands), len(rider.out_shapes)

    def body(*refs):
        r_ins, r_outs, r_scr = refs[:r_in], refs[r_in:r_in + r_out], refs[r_in + r_out:]
        rider.start(r_ins, r_outs, r_scr)
        rider.finish(r_ins, r_outs, r_scr)

    return pl.pallas_call(
        body, name=name, in_specs=[HBM_SPEC] * r_in, out_specs=[HBM_SPEC] * r_out, out_shape=list(rider.out_shapes),
        input_output_aliases=dict(rider.aliases), scratch_shapes=list(rider.scratch),
    )(*rider.operands)


def _ffn_fwd(h, gain, group, name, rider=None):
    S = h.shape[0]
    tile = min(TOK_TILE, S)
    nt = S // tile

    def body(h_ref, g_ref, wg_ref, wu_ref, wd_ref, ho_ref, n_ref, gate_ref, up_ref, acc):
        j = pl.program_id(1)

        @pl.when(j == 0)
        def _():
            _, xh = _rms_stats(h_ref[...])
            n_ref[...] = (xh * g_ref[...]).astype(BF16)
            acc[...] = jnp.zeros_like(acc)

        n = n_ref[...]
        gate = _dot_nt(n, wg_ref[0])
        up = _dot_nt(n, wu_ref[0])
        gate_ref[0] = gate.astype(BF16)
        up_ref[0] = up.astype(BF16)
        act = (gate * jax.nn.sigmoid(gate) * up).astype(BF16)
        acc[...] += _dot(act, wd_ref[0])

        @pl.when(j == N_CHIPS - 1)
        def _():
            ho_ref[...] = h_ref[...] + 0.5 * acc[...]

    tok = pl.BlockSpec((tile, D_MODEL), lambda i, j: (i, 0))
    hid = pl.BlockSpec((1, tile, FF_CHUNK), lambda i, j: (j, i, 0))
    return _call(
        body, name, (nt, N_CHIPS),
        [tok, pl.BlockSpec((1, D_MODEL), lambda i, j: (0, 0))]
        + [pl.BlockSpec((1, FF_CHUNK, D_MODEL), functools.partial(lambda i, j, part: (j, part, 0), part=part))
           for part in range(3)],
        [tok, tok, hid, hid],
        [jax.ShapeDtypeStruct((S, D_MODEL), F32), jax.ShapeDtypeStruct((S, D_MODEL), BF16),
         jax.ShapeDtypeStruct((N_CHIPS, S, FF_CHUNK), BF16), jax.ShapeDtypeStruct((N_CHIPS, S, FF_CHUNK), BF16)],
        [pltpu.VMEM((tile, D_MODEL), F32)], 40, (h, gain, group, group, group), rider)


def _ffn_bwd(chunks, d_out, n, gate, up, group, name, rider=None):
    S = n.shape[0]
    n_chunks = chunks.shape[0]
    tile = min(TOK_TILE, S)
    nt = S // tile
    half_rows = 3 * FF_CHUNK // 2
    cut = FF_CHUNK // 2

    def body(chunks_ref, do_ref, n_ref, gate_ref, up_ref, wg_ref, wu_ref, wd_ref, dn_ref, dw_ref, acc_g, acc_u, acc_d):
        j, i = pl.program_id(0), pl.program_id(1)

        @pl.when(i == 0)
        def _():
            acc_g[...] = jnp.zeros_like(acc_g)
            acc_u[...] = jnp.zeros_like(acc_u)
            acc_d[...] = jnp.zeros_like(acc_d)

        do = do_ref[...]
        nn = n_ref[...]
        g = gate_ref[0].astype(F32)
        u = up_ref[0].astype(F32)
        d_act = _dot_nt(do, wd_ref[0])
        sig = jax.nn.sigmoid(g)
        silu = g * sig
        d_up = (d_act * silu).astype(BF16)
        d_gate = (d_act * u * (sig * (1.0 + g * (1.0 - sig)))).astype(BF16)
        act = (silu * u).astype(BF16)
        dn_ref[0] = (_dot(d_gate, wg_ref[0]) + _dot(d_up, wu_ref[0])).astype(BF16)
        acc_g[...] += _dot_tn(d_gate, nn)
        acc_u[...] += _dot_tn(d_up, nn)
        acc_d[...] += _dot_tn(act, do)

        @pl.when(i == nt - 1)
        def _():
            dw_ref[0, 0, :FF_CHUNK, :] = acc_g[...].astype(BF16)
            dw_ref[0, 0, FF_CHUNK:, :] = acc_u[:cut, :].astype(BF16)
            dw_ref[1, 0, :cut, :] = acc_u[cut:, :].astype(BF16)
            dw_ref[1, 0, cut:, :] = acc_d[...].astype(BF16)

    tok = pl.BlockSpec((tile, D_MODEL), lambda j, i, chunks_ref: (i, 0))
    hid = pl.BlockSpec((1, tile, FF_CHUNK), lambda j, i, chunks_ref: (chunks_ref[j], i, 0))
    return _call(
        body, name, (n_chunks, nt),
        [tok, tok, hid, hid]
        + [pl.BlockSpec((1, FF_CHUNK, D_MODEL), functools.partial(lambda j, i, chunks_ref, part: (chunks_ref[j], part, 0), part=part))
           for part in range(3)],
        [pl.BlockSpec((1, tile, D_MODEL), lambda j, i, chunks_ref: (j, i, 0)),
         pl.BlockSpec((2, 1, half_rows, D_MODEL), lambda j, i, chunks_ref: (0, j, 0, 0))],
        [jax.ShapeDtypeStruct((n_chunks, S, D_MODEL), BF16), jax.ShapeDtypeStruct((2, n_chunks, half_rows, D_MODEL), BF16)],
        [pltpu.VMEM((FF_CHUNK, D_MODEL), F32)] * 3, 56, (d_out, n, gate, up, group, group, group), rider, (chunks,))


def _mix_in(h, gain, w_in, rc, rs1, rs2, name):
    S = h.shape[0]
    tile = min(TOK_TILE, S)

    def body(h_ref, g_ref, w_ref, c_ref, s1_ref, s2_ref, n_ref, q_ref, k_ref, v_ref, pc_ref):
        _, xh = _rms_stats(h_ref[...])
        n = (xh * g_ref[...]).astype(BF16)
        n_ref[...] = n
        u = _dot_nt(n, w_ref[...])
        c, s1, s2 = c_ref[...], s1_ref[...], s2_ref[...]
        q_ref[...] = jnp.concatenate([(_rope(u[:, hd * LANES:(hd + 1) * LANES], c, s1, s2) * SCALE).astype(BF16)
                                      for hd in range(N_HEADS)], axis=1)
        k_ref[...] = _rope(u[:, Q_PAD:Q_PAD + KV_WIDTH], c, s1, s2).astype(BF16)
        v_ref[...] = u[:, Q_PAD + KV_WIDTH:Q_PAD + 2 * KV_WIDTH].astype(BF16)
        pc_ref[...] = u[:, Q_PAD + 2 * KV_WIDTH:]

    return pl.pallas_call(
        body, name=name, grid=(S // tile,),
        in_specs=[_rows(tile, D_MODEL), _full((1, D_MODEL)), _full((U_PAD, D_MODEL)),
                  _rows(tile, LANES), _rows(tile, LANES), _rows(tile, LANES)],
        out_specs=[_rows(tile, D_MODEL), _rows(tile, Q_PAD), _rows(tile, KV_WIDTH), _rows(tile, KV_WIDTH),
                   _rows(tile, POOL_WIDTH)],
        out_shape=[jax.ShapeDtypeStruct((S, D_MODEL), BF16), jax.ShapeDtypeStruct((S, Q_PAD), BF16),
                   jax.ShapeDtypeStruct((S, KV_WIDTH), BF16), jax.ShapeDtypeStruct((S, KV_WIDTH), BF16),
                   jax.ShapeDtypeStruct((S, POOL_WIDTH), F32)],
        compiler_params=_params(("parallel",), 40),
    )(h, gain, w_in, rc, rs1, rs2)


def _band_start(i, S):
    return pl.multiple_of(jnp.clip((i - 1) * BLOCK, 0, S - BAND), BLOCK)


def _window_bias(off):
    r = lax.broadcasted_iota(jnp.int32, (BLOCK, 1), 0)
    c = lax.broadcasted_iota(jnp.int32, (1, BAND), 1)
    return jnp.where(jnp.abs(off + r - c) <= WINDOW, 0.0, NEG).astype(F32)


def _softmax_parts(qh, kb, bias, sink_h):
    s = _dot_nt(qh, kb) + bias
    m = jnp.maximum(jnp.max(s, axis=-1, keepdims=True), sink_h)
    p = jnp.exp(s - m)
    es = jnp.exp(sink_h - m)
    return p, es, 1.0 / (jnp.sum(p, axis=-1, keepdims=True) + es)


def _pool_matrix(t0, start, S, w):
    r = lax.broadcasted_iota(jnp.int32, (BLOCK, 1), 0) + t0
    c = lax.broadcasted_iota(jnp.int32, (1, BAND), 1) + start
    half = w // 2

    def window(lo, hi):
        a = jnp.maximum(lo, 0)
        b = jnp.minimum(hi + 1, S)
        return jnp.where((c >= a) & (c < b), 1.0 / (b - a).astype(F32), 0.0)

    return (0.5 * (window(r - half, r + half - 1) + window(r - half + 1, r + half))).astype(BF16)


def _pool_matrices(S):
    blocks = ((0, 0), (BLOCK, 0), (S - BLOCK, S - BAND))
    return jnp.stack([jnp.stack([_pool_matrix(t0, start, S, w) for w in POOL_WINDOWS]) for t0, start in blocks])


def _pool_spec(nb):
    return pl.BlockSpec((1, N_POOL, BLOCK, BAND), lambda i, *_: (jnp.where(i == 0, 0, jnp.where(i == nb - 1, 2, 1)), 0, 0, 0))


def _mix_core_fwd(q, k, v, pc, sink, pool_m, pool_w, pool_scale, name):
    S = q.shape[0]
    nb = S // BLOCK

    def body(sink_ref, q_ref, k_ref, v_ref, pc_ref, pm_ref, pw_ref, ps_ref, a_ref, p_ref):
        i = pl.program_id(0)
        start = _band_start(i, S)
        band = pl.ds(start, BAND)
        bias = _window_bias(i * BLOCK - start)
        kb, vb = k_ref[band, :], v_ref[band, :]
        hs = range(N_HEADS)
        ss = [_dot_nt(q_ref[:, hd * LANES:(hd + 1) * LANES], kb) + bias for hd in hs]
        ms = [jnp.maximum(jnp.max(ss[hd], axis=-1, keepdims=True), sink_ref[0, hd]) for hd in hs]
        ps = [jnp.exp(ss[hd] - ms[hd]) for hd in hs]
        invs = [1.0 / (jnp.sum(ps[hd], axis=-1, keepdims=True) + jnp.exp(sink_ref[0, hd] - ms[hd])) for hd in hs]
        outs = [_dot(ps[hd].astype(BF16), vb) for hd in hs]
        a_ref[...] = jnp.concatenate([(outs[hd] * invs[hd]).astype(BF16) for hd in hs], axis=1)
        centre = pl.ds(pl.multiple_of(i * BLOCK, BLOCK), BLOCK)
        gs = range(N_POOL)
        sl = [slice(g * POOL_GROUP, (g + 1) * POOL_GROUP) for g in gs]
        means = [_dot(pm_ref[0, g], pc_ref[band, sl[g]].astype(BF16)) for g in gs]
        devs = [(means[g] - pc_ref[centre, sl[g]]).astype(BF16) for g in gs]
        p_ref[...] = (jnp.concatenate([_dot(devs[g], pw_ref[g]) for g in gs], axis=1) * ps_ref[...]).astype(BF16)

    return pl.pallas_call(
        body, name=name, grid=(nb,),
        in_specs=[pl.BlockSpec(memory_space=pltpu.SMEM), _rows(BLOCK, Q_PAD), _full((S, KV_WIDTH)), _full((S, KV_WIDTH)),
                  _full((S, POOL_WIDTH)), _pool_spec(nb), _full((N_POOL, POOL_GROUP, POOL_GROUP)), _full((1, POOL_WIDTH))],
        out_specs=[_rows(BLOCK, Q_PAD), _rows(BLOCK, POOL_WIDTH)],
        out_shape=[jax.ShapeDtypeStruct((S, Q_PAD), BF16), jax.ShapeDtypeStruct((S, POOL_WIDTH), BF16)],
        compiler_params=_params(("parallel",), 40),
    )(sink, q, k, v, pc, pool_m, pool_w, pool_scale)


def _mix_core_bwd(q, k, v, pc, da, dp, sink, pool_m, pool_w, pool_scale, rc, rs1, rs2, name, rider=None):
    S = q.shape[0]
    nb = S // BLOCK

    def body(sink_ref, q_ref, k_ref, v_ref, pc_ref, da_ref, dp_ref, pm_ref, pw_ref, ps_ref, c_ref, s1_ref, s2_ref,
             dq_ref, dk_ref, dv_ref, dpc_ref, dsink_ref, dpw_ref, dps_ref):
        i = pl.program_id(0)

        @pl.when(i == 0)
        def _():
            dk_ref[...] = jnp.zeros_like(dk_ref)
            dv_ref[...] = jnp.zeros_like(dv_ref)
            dpc_ref[...] = jnp.zeros_like(dpc_ref)
            dsink_ref[...] = jnp.zeros_like(dsink_ref)
            dpw_ref[...] = jnp.zeros_like(dpw_ref)
            dps_ref[...] = jnp.zeros_like(dps_ref)

        start = _band_start(i, S)
        band = pl.ds(start, BAND)
        bias = _window_bias(i * BLOCK - start)
        kb, vb = k_ref[band, :], v_ref[band, :]
        c, s1, s2 = c_ref[...], s1_ref[...], s2_ref[...]
        lane = lax.broadcasted_iota(jnp.int32, (1, LANES), 1)
        hs = range(N_HEADS)
        qs = [q_ref[:, hd * LANES:(hd + 1) * LANES] for hd in hs]
        das = [da_ref[:, hd * LANES:(hd + 1) * LANES] for hd in hs]
        ss = [_dot_nt(qs[hd], kb) + bias for hd in hs]
        d_probs = [_dot_nt(das[hd], vb) for hd in hs]
        ms = [jnp.maximum(jnp.max(ss[hd], axis=-1, keepdims=True), sink_ref[0, hd]) for hd in hs]
        ps = [jnp.exp(ss[hd] - ms[hd]) for hd in hs]
        ess = [jnp.exp(sink_ref[0, hd] - ms[hd]) for hd in hs]
        invs = [1.0 / (jnp.sum(ps[hd], axis=-1, keepdims=True) + ess[hd]) for hd in hs]
        probs = [ps[hd] * invs[hd] for hd in hs]
        deltas = [jnp.sum(probs[hd] * d_probs[hd], axis=-1, keepdims=True) for hd in hs]
        d_ss = [(probs[hd] * (d_probs[hd] - deltas[hd])).astype(BF16) for hd in hs]
        dqs = [_dot(d_ss[hd], kb) for hd in hs]
        dq_ref[...] = jnp.concatenate([_rope_bwd(dqs[hd] * SCALE, c, s1, s2).astype(BF16) for hd in hs], axis=1)
        dks = [_dot_tn(d_ss[hd], qs[hd]) for hd in hs]
        dvs = [_dot_tn(probs[hd].astype(BF16), das[hd]) for hd in hs]
        dk_ref[band, :] += functools.reduce(lambda a, b: a + b, dks)
        dv_ref[band, :] += functools.reduce(lambda a, b: a + b, dvs)
        dsink_ref[...] += functools.reduce(lambda a, b: a + b, [
            jnp.where(lane == hd, -jnp.sum(ess[hd] * invs[hd] * deltas[hd], axis=0, keepdims=True), 0.0) for hd in hs])

        centre = pl.ds(pl.multiple_of(i * BLOCK, BLOCK), BLOCK)
        gs = range(N_POOL)
        sl = [slice(g * POOL_GROUP, (g + 1) * POOL_GROUP) for g in gs]
        devs = [(_dot(pm_ref[0, g], pc_ref[band, sl[g]].astype(BF16)) - pc_ref[centre, sl[g]]).astype(BF16) for g in gs]
        dys = [dp_ref[:, sl[g]].astype(F32) for g in gs]
        zs = [_dot(devs[g], pw_ref[g]) for g in gs]
        dzs = [(dys[g] * ps_ref[:, sl[g]]).astype(BF16) for g in gs]
        d_devs = [_dot_nt(dzs[g], pw_ref[g]) for g in gs]
        dps_ref[...] += jnp.concatenate([jnp.sum(dys[g] * zs[g], axis=0, keepdims=True) for g in gs], axis=1)
        for g in gs:
            dpw_ref[g] += _dot_tn(devs[g], dzs[g])
        dpc_ref[band, :] += jnp.concatenate([_dot_tn(pm_ref[0, g], d_devs[g].astype(BF16)) for g in gs], axis=1)
        dpc_ref[centre, :] -= jnp.concatenate(d_devs, axis=1)

    return _call(
        body, name, (nb,),
        [pl.BlockSpec(memory_space=pltpu.SMEM), _rows(BLOCK, Q_PAD), _full((S, KV_WIDTH)), _full((S, KV_WIDTH)),
         _full((S, POOL_WIDTH)), _rows(BLOCK, Q_PAD), _rows(BLOCK, POOL_WIDTH), _pool_spec(nb),
         _full((N_POOL, POOL_GROUP, POOL_GROUP)), _full((1, POOL_WIDTH)),
         _rows(BLOCK, LANES), _rows(BLOCK, LANES), _rows(BLOCK, LANES)],
        [_rows(BLOCK, Q_PAD), _full((S, KV_WIDTH)), _full((S, KV_WIDTH)), _full((S, POOL_WIDTH)),
         _full((1, LANES)), _full((N_POOL, POOL_GROUP, POOL_GROUP)), _full((1, POOL_WIDTH))],
        [jax.ShapeDtypeStruct((S, Q_PAD), BF16), jax.ShapeDtypeStruct((S, KV_WIDTH), F32),
         jax.ShapeDtypeStruct((S, KV_WIDTH), F32), jax.ShapeDtypeStruct((S, POOL_WIDTH), F32),
         jax.ShapeDtypeStruct((1, LANES), F32), jax.ShapeDtypeStruct((N_POOL, POOL_GROUP, POOL_GROUP), F32),
         jax.ShapeDtypeStruct((1, POOL_WIDTH), F32)],
        [], 56, (sink, q, k, v, pc, da, dp, pool_m, pool_w, pool_scale, rc, rs1, rs2), rider)


def _mix_out(h, a, p, wa, wp, name):
    S = h.shape[0]
    tile = min(TOK_TILE, S)

    def body(h_ref, a_ref, p_ref, wa_ref, wp_ref, o_ref):
        o_ref[...] = h_ref[...] + _dot(a_ref[...], wa_ref[...]) + _dot(p_ref[...], wp_ref[...])

    return pl.pallas_call(
        body, name=name, grid=(S // tile,),
        in_specs=[_rows(tile, D_MODEL), _rows(tile, Q_PAD), _rows(tile, POOL_WIDTH),
                  _full((Q_PAD, D_MODEL)), _full((POOL_WIDTH, D_MODEL))],
        out_specs=_rows(tile, D_MODEL),
        out_shape=jax.ShapeDtypeStruct((S, D_MODEL), F32),
        compiler_params=_params(("parallel",), 40),
    )(h, a, p, wa, wp)


def _loss_head(h, target, gain, name):
    S = h.shape[0]
    tile = min(TOK_TILE, S)

    def body(h_ref, t_ref, g_ref, dh_ref, dhalf_ref, loss_ref, dg_ref):
        @pl.when(pl.program_id(0) == 0)
        def _():
            loss_ref[...] = jnp.zeros_like(loss_ref)
            dg_ref[...] = jnp.zeros_like(dg_ref)

        g = g_ref[...]
        r, xh = _rms_stats(h_ref[...])
        err = xh * g - t_ref[...]
        loss_ref[...] += (0.5 / D_MODEL) * jnp.sum(err * err, axis=0, keepdims=True)
        dh, dg = _rms_bwd(err * (1.0 / D_MODEL), g, r, xh)
        dg_ref[...] += dg
        dh_ref[...] = dh
        dhalf_ref[...] = (0.5 * dh).astype(BF16)

    return pl.pallas_call(
        body, name=name, grid=(S // tile,),
        in_specs=[_rows(tile, D_MODEL), _rows(tile, D_MODEL), _full((1, D_MODEL))],
        out_specs=[_rows(tile, D_MODEL), _rows(tile, D_MODEL), _full((1, D_MODEL)), _full((1, D_MODEL))],
        out_shape=[jax.ShapeDtypeStruct((S, D_MODEL), F32), jax.ShapeDtypeStruct((S, D_MODEL), BF16),
                   jax.ShapeDtypeStruct((1, D_MODEL), F32), jax.ShapeDtypeStruct((1, D_MODEL), F32)],
        compiler_params=_params(("arbitrary",), 40),
    )(h, target, gain)


def _mix_out_bwd(dh_out, dn, h, gain, a, p, wa, wp, name, rider=None):
    S = h.shape[0]
    tile = min(TOK_TILE, S)

    def body(do_ref, dn_ref, h_ref, g_ref, a_ref, p_ref, wa_ref, wp_ref, dh_ref, da_ref, dp_ref, dwa_ref, dwp_ref, dg_ref):
        @pl.when(pl.program_id(0) == 0)
        def _():
            dwa_ref[...] = jnp.zeros_like(dwa_ref)
            dwp_ref[...] = jnp.zeros_like(dwp_ref)
            dg_ref[...] = jnp.zeros_like(dg_ref)

        r, xh = _rms_stats(h_ref[...])
        dnorm, dg = _rms_bwd(_sum_chunks([dn_ref]), g_ref[...], r, xh)
        dh = do_ref[...] + dnorm
        dg_ref[...] += dg
        dh_ref[...] = dh
        dhb = dh.astype(BF16)
        da_ref[...] = _dot_nt(dhb, wa_ref[...]).astype(BF16)
        dp_ref[...] = _dot_nt(dhb, wp_ref[...]).astype(BF16)
        dwa_ref[...] += _dot_tn(a_ref[...], dhb)
        dwp_ref[...] += _dot_tn(p_ref[...], dhb)

    return _call(
        body, name, (S // tile,),
        [_rows(tile, D_MODEL), _chunk_rows(tile, dn.shape[0]), _rows(tile, D_MODEL), _full((1, D_MODEL)),
         _rows(tile, Q_PAD), _rows(tile, POOL_WIDTH), _full((Q_PAD, D_MODEL)), _full((POOL_WIDTH, D_MODEL))],
        [_rows(tile, D_MODEL), _rows(tile, Q_PAD), _rows(tile, POOL_WIDTH),
         _full((Q_PAD, D_MODEL)), _full((POOL_WIDTH, D_MODEL)), _full((1, D_MODEL))],
        [jax.ShapeDtypeStruct((S, D_MODEL), F32), jax.ShapeDtypeStruct((S, Q_PAD), BF16),
         jax.ShapeDtypeStruct((S, POOL_WIDTH), BF16), jax.ShapeDtypeStruct((Q_PAD, D_MODEL), F32),
         jax.ShapeDtypeStruct((POOL_WIDTH, D_MODEL), F32), jax.ShapeDtypeStruct((1, D_MODEL), F32)],
        [], 48, (dh_out, dn, h, gain, a, p, wa, wp), rider)


def _mix_in_bwd(dh_out, h, gain, n, dq, dk, dv, dpc, rc, rs1, rs2, w_in, name, rider=None):
    S = h.shape[0]
    tile = min(TOK_TILE, S)

    def body(do_ref, h_ref, g_ref, n_ref, dq_ref, dk_ref, dv_ref, dpc_ref, c_ref, s1_ref, s2_ref, w_ref,
             dh_ref, dhalf_ref, dw_ref, dg_ref):
        @pl.when(pl.program_id(0) == 0)
        def _():
            dw_ref[...] = jnp.zeros_like(dw_ref)
            dg_ref[...] = jnp.zeros_like(dg_ref)

        dk = _rope_bwd(dk_ref[...], c_ref[...], s1_ref[...], s2_ref[...]).astype(BF16)
        du = jnp.concatenate([dq_ref[...], dk, dv_ref[...].astype(BF16), dpc_ref[...].astype(BF16)], axis=1)
        dn = _dot(du, w_ref[...])
        dw_ref[...] += _dot_tn(du, n_ref[...])
        r, xh = _rms_stats(h_ref[...])
        dnorm, dg = _rms_bwd(dn, g_ref[...], r, xh)
        dh = do_ref[...] + dnorm
        dg_ref[...] += dg
        dh_ref[...] = dh
        dhalf_ref[...] = (0.5 * dh).astype(BF16)

    return _call(
        body, name, (S // tile,),
        [_rows(tile, D_MODEL), _rows(tile, D_MODEL), _full((1, D_MODEL)), _rows(tile, D_MODEL),
         _rows(tile, Q_PAD), _rows(tile, KV_WIDTH), _rows(tile, KV_WIDTH), _rows(tile, POOL_WIDTH),
         _rows(tile, LANES), _rows(tile, LANES), _rows(tile, LANES), _full((U_PAD, D_MODEL))],
        [_rows(tile, D_MODEL), _rows(tile, D_MODEL), _full((U_PAD, D_MODEL)), _full((1, D_MODEL))],
        [jax.ShapeDtypeStruct((S, D_MODEL), F32), jax.ShapeDtypeStruct((S, D_MODEL), BF16),
         jax.ShapeDtypeStruct((U_PAD, D_MODEL), F32), jax.ShapeDtypeStruct((1, D_MODEL), F32)],
        [], 56, (dh_out, h, gain, n, dq, dk, dv, dpc, rc, rs1, rs2, w_in), rider)


def _norm_bwd(dh_out, dns, h, gain, name):
    S = h.shape[0]
    tile = min(TOK_TILE, S)
    n = len(dns)

    def body(do_ref, *refs):
        h_ref, g_ref, dh_ref, dg_ref = refs[n:]

        @pl.when(pl.program_id(0) == 0)
        def _():
            dg_ref[...] = jnp.zeros_like(dg_ref)

        r, xh = _rms_stats(h_ref[...])
        dnorm, dg = _rms_bwd(_sum_chunks(refs[:n]), g_ref[...], r, xh)
        dg_ref[...] += dg
        dh_ref[...] = do_ref[...] + dnorm

    return _call(
        body, name, (S // tile,),
        [_rows(tile, D_MODEL)] + [_chunk_rows(tile, dn.shape[0]) for dn in dns] + [_rows(tile, D_MODEL), _full((1, D_MODEL))],
        [_rows(tile, D_MODEL), _full((1, D_MODEL))],
        [jax.ShapeDtypeStruct((S, D_MODEL), F32), jax.ShapeDtypeStruct((1, D_MODEL), F32)],
        [], 40, (dh_out, *dns, h, gain))


def _rope_tables(S):
    half = ROTARY_DIM // 2
    inv_freq = ROPE_THETA ** (-jnp.arange(0, ROTARY_DIM, 2, dtype=F32) / ROTARY_DIM)
    dim = jnp.arange(LANES) % HEAD_DIM
    ang = jnp.arange(S, dtype=F32)[:, None] * inv_freq[dim % half][None, :]
    lo, hi = (dim < half)[None, :], ((dim >= half) & (dim < ROTARY_DIM))[None, :]
    c = jnp.where(lo | hi, jnp.cos(ang), 1.0)
    s1 = jnp.where(lo, -jnp.sin(ang), 0.0)
    s2 = jnp.where(hi, jnp.sin(ang), 0.0)
    return c, s1, s2


def _pad_heads(w, axis):
    w = jnp.moveaxis(w, axis, 0)
    heads = w.reshape((N_HEADS, HEAD_DIM) + w.shape[1:])
    zero = jnp.zeros_like(heads)
    first = (jnp.arange(N_HEADS) < Q_PER_KV).reshape((N_HEADS, 1) + (1,) * (w.ndim - 1))
    lo = jnp.where(first, heads, zero)
    hi = jnp.where(first, zero, heads)
    padded = jnp.concatenate([lo, hi], axis=1).reshape((Q_PAD,) + w.shape[1:])
    return jnp.moveaxis(padded, 0, axis)


def _unpad_heads(w, axis):
    w = jnp.moveaxis(w, axis, 0)
    groups = w.reshape((N_HEADS, 2, HEAD_DIM) + w.shape[1:])
    first = (jnp.arange(N_HEADS) < Q_PER_KV).reshape((N_HEADS, 1) + (1,) * (w.ndim - 1))
    heads = jnp.where(first, groups[:, 0], groups[:, 1]).reshape((ATTN_WIDTH,) + w.shape[1:])
    return jnp.moveaxis(heads, 0, axis)


IN_ROWS = IN_WIDTH // N_CHIPS
OUT_ROWS = (ATTN_WIDTH + POOL_WIDTH) // N_CHIPS
MIX_ROWS = IN_ROWS + OUT_ROWS
FFN_ROWS = 3 * FF_CHUNK


def _step(x, target, bufs, small, place):
    S = x.shape[0]
    rc, rs1, rs2 = _rope_tables(S)
    (ffn1,) = _comm_call(_allgather(bufs[:1]), "allgather_ffn1")
    ffn1 = ffn1.reshape(N_CHIPS, FFN_ROWS, D_MODEL)
    h1, n1, gate1, up1, mix, ffn2 = _ffn_fwd(x, small["ffn1_norm"], ffn1, "ffn1_fwd", _allgather(bufs[1:]))
    mix, ffn2 = mix.reshape(N_CHIPS, MIX_ROWS, D_MODEL), ffn2.reshape(N_CHIPS, FFN_ROWS, D_MODEL)
    w_in_t = mix[:, :IN_ROWS].reshape(IN_WIDTH, D_MODEL)
    w_in_pad = jnp.concatenate([_pad_heads(w_in_t[:ATTN_WIDTH], 0), w_in_t[ATTN_WIDTH:]], axis=0)
    w_out = mix[:, IN_ROWS:].reshape(ATTN_WIDTH + POOL_WIDTH, D_MODEL)
    wa = _pad_heads(w_out[:ATTN_WIDTH], 0)
    wp = w_out[ATTN_WIDTH:]
    pool_w = small["pool_w"].astype(BF16)

    n2, q, k, v, pc = _mix_in(h1, small["mix_norm"], w_in_pad, rc, rs1, rs2, "mix_in")
    pool_m = _pool_matrices(S)
    a, p = _mix_core_fwd(q, k, v, pc, small["sink_logits"], pool_m, pool_w, small["pool_scale"], "mix_core_fwd")
    h2 = _mix_out(h1, a, p, wa, wp, "mix_out")
    h3, n3, gate2, up2 = _ffn_fwd(h2, small["ffn2_norm"], ffn2, "ffn2_fwd")
    dh3, dhalf3, loss_lanes, d_final = _loss_head(h3, target, small["final_norm"], "loss_head")

    dn3, d_ffn2 = _ffn_bwd(jnp.arange(N_CHIPS, dtype=jnp.int32), dhalf3, n3, gate2, up2, ffn2, "ffn2_bwd")
    dh2, da, dp, dwa, dwp, d_ffn2_norm, received = _mix_out_bwd(dh3, dn3, h2, small["ffn2_norm"], a, p, wa, wp, "mix_out_bwd",
                                                                _sibling_exchange([d_ffn2]))
    pair = _pair_sum(place, d_ffn2, received, "grad_pair_sum_ffn2")
    dq, dk, dv, dpc, dsink, dpool_w, dpool_scale, stack = _mix_core_bwd(
        q, k, v, pc, da, dp, small["sink_logits"], pool_m, pool_w, small["pool_scale"], rc, rs1, rs2, "mix_core_bwd",
        _scatter([pair]))
    reduced = _chip_sum(place, pair, stack, 2, "grad_chip_sum_ffn2")
    dh1, dhalf1, dw_in_pad, d_mix_norm, g_ffn2 = _mix_in_bwd(dh2, h1, small["mix_norm"], n2, dq, dk, dv, dpc, rc, rs1, rs2,
                                                             w_in_pad, "mix_in_bwd", _sibling_share([reduced]))
    dw_in_t = jnp.concatenate([_unpad_heads(dw_in_pad[:Q_PAD], 0), dw_in_pad[Q_PAD:]], axis=0)
    dw_out = jnp.concatenate([_unpad_heads(dwa, 0), dwp], axis=0)
    d_mix = jnp.concatenate([dw_in_t.reshape(N_CHIPS, IN_ROWS, D_MODEL), dw_out.reshape(N_CHIPS, OUT_ROWS, D_MODEL)], axis=1)
    d_mix = jnp.transpose(d_mix.reshape(N_CHIPS, 2, MIX_ROWS // 2, D_MODEL), (1, 0, 2, 3)).astype(BF16)
    small_g = {"ffn1_norm": jnp.zeros_like(d_mix_norm), "mix_norm": d_mix_norm, "ffn2_norm": d_ffn2_norm,
               "final_norm": d_final, "pool_scale": dpool_scale, "sink_logits": dsink[:, :N_HEADS], "pool_w": dpool_w}
    loss_row = jnp.sum(loss_lanes.reshape(D_MODEL // LANES, LANES), axis=0, keepdims=True)
    small_early = _pack_small(small_g, loss_row)

    chunk = [(place[1:] + 1 + p) % N_CHIPS for p in range(N_CHIPS)]
    ffn1_bwd = functools.partial(_ffn_bwd, d_out=dhalf1, n=n1, gate=gate1, up=up1, group=ffn1)
    stack = jnp.zeros((N_CHIPS, FFN_ROWS // 2, D_MODEL), BF16)
    dn_a, dw_a, recv_mix, small_all = ffn1_bwd(chunk[0], name="ffn1_bwd_0",
                                               rider=_merge(_sibling_exchange([d_mix]), _small_allgather(small_early)))
    pair_mix = _pair_sum(place, d_mix, recv_mix, "grad_pair_sum_mix")
    dn_b, dw_b, recv_a, stack_mix = ffn1_bwd(chunk[1], name="ffn1_bwd_1",
                                             rider=_merge(_sibling_exchange([dw_a]), _scatter([pair_mix])))
    pair_a = _pair_sum(place, dw_a, recv_a, "grad_pair_sum_ffn1_0")
    reduced_mix = _chip_sum(place, pair_mix, stack_mix, 2, "grad_chip_sum_mix")
    dn_c, dw_c, recv_b, stack, g_mix = ffn1_bwd(
        chunk[2], name="ffn1_bwd_2",
        rider=_merge(_merge(_sibling_exchange([dw_b]), _scatter_step(pair_a, stack, 0)), _sibling_share([reduced_mix])))
    pair_b = _pair_sum(place, dw_b, recv_b, "grad_pair_sum_ffn1_1")
    dn_d, dw_d, recv_c, stack = ffn1_bwd(chunk[3], name="ffn1_bwd_3",
                                         rider=_merge(_sibling_exchange([dw_c]), _scatter_step(pair_b, stack, 1)))
    pair_c = _pair_sum(place, dw_c, recv_c, "grad_pair_sum_ffn1_2")
    grad_x, d_ffn1_norm = _norm_bwd(dh1, [dn_a, dn_b, dn_c, dn_d], x, small["ffn1_norm"], "norm1_bwd")

    recv_d, stack, gains = _comm_call(
        _merge(_merge(_sibling_exchange([dw_d]), _scatter_step(pair_c, stack, 2)),
               _small_allgather(d_ffn1_norm.reshape(-1, LANES))), "grad_tail")
    pair_d = _pair_sum(place, dw_d, recv_d, "grad_pair_sum_ffn1_3")
    reduced_ffn1 = _chip_sum(place, pair_d, stack, 2, "grad_chip_sum_ffn1")
    (g_ffn1,) = _comm_call(_sibling_share([reduced_ffn1]), "grad_share_tail")
    gain_sum = _sum_leading(gains, 1, "gain_grad_sum")
    small_sum = jnp.concatenate([gain_sum, _sum_leading(small_all, 1, "small_grad_sum")[gain_sum.shape[0]:]], axis=0)
    return jnp.sum(small_sum[SMALL_ROWS - 1]), grad_x, [g.reshape(-1, D_MODEL) for g in (g_ffn1, g_mix, g_ffn2)], small_sum


GROUPS =(("ffn1_w_gate", "ffn1_w_up", "ffn1_w_down"), ("w_in", "w_out"), ("ffn2_w_gate", "ffn2_w_up", "ffn2_w_down"))
TRANSPOSED = ("ffn1_w_gate", "ffn1_w_up", "w_in", "ffn2_w_gate", "ffn2_w_up")


def _place():
    x, y, c = lax.axis_index("x"), lax.axis_index("y"), lax.axis_index("c")
    chips = [(1 - x, y), (x, 1 - y), (1 - x, 1 - y)]
    return x, y, c, chips


def _remote(src, dst, send_sem, recv_sem, to):
    return pltpu.make_async_remote_copy(src_ref=src, dst_ref=dst, send_sem=send_sem, recv_sem=recv_sem,
                                        device_id=to, device_id_type=MESH)


def _pack(chip, members, name):
    rows = members[0].shape[0]
    n = len(members)

    def body(chip_ref, *refs):
        ins, out_ref, buf, sems = refs[:n], refs[n], refs[n + 1], refs[n + 2]
        copies = [pltpu.make_async_copy(ins[k], buf.at[k], sems.at[k]) for k in range(n)]
        for cp in copies:
            cp.start()
        for k in range(n):
            copies[k].wait()
            out_ref[0, k * rows:(k + 1) * rows, :] = buf[k].astype(BF16)

    return pl.pallas_call(
        body, name=name,
        grid_spec=pltpu.PrefetchScalarGridSpec(
            num_scalar_prefetch=1, grid=(1,),
            in_specs=[HBM_SPEC] * n,
            out_specs=pl.BlockSpec((1, n * rows, D_MODEL), lambda k, chip_ref: (chip_ref[0], 0, 0)),
            scratch_shapes=[pltpu.VMEM((n, rows, D_MODEL), F32), pltpu.SemaphoreType.DMA((n,))]),
        out_shape=jax.ShapeDtypeStruct((N_CHIPS, n * rows, D_MODEL), BF16),
        compiler_params=_params(("arbitrary",), 40),
    )(chip, *members)


def _same(arrays):
    return [jax.ShapeDtypeStruct(a.shape, a.dtype) for a in arrays]


def _allgather(bufs):
    n = len(bufs)

    def copies(outs, send_sems, recv_sems, started_only=False):
        x, y, c, chips = _place()
        me = 2 * x + y
        slots = [2 * cx + cy for cx, cy in chips]
        first = [[_remote(outs[a].at[me, c], outs[a].at[me, c], send_sems.at[6 * a + k], recv_sems.at[6 * a + k], (*chips[k], c))
                  for k in range(3)] for a in range(n)]
        if started_only:
            return first
        passed = [[_remote(outs[a].at[slots[k], c], outs[a].at[slots[k], c], send_sems.at[6 * a + 3 + k],
                           recv_sems.at[6 * a + 3 + k], (x, y, 1 - c)) for k in range(3)] for a in range(n)]
        landed = [[_remote(outs[a].at[me, c], outs[a].at[slots[k], c], send_sems.at[6 * a + k], recv_sems.at[6 * a + k],
                           (*chips[k], c)) for k in range(3)] for a in range(n)]
        handed = [[_remote(outs[a].at[me, c], outs[a].at[slots[k], 1 - c], send_sems.at[6 * a + 3 + k],
                           recv_sems.at[6 * a + 3 + k], (x, y, 1 - c)) for k in range(3)] for a in range(n)]
        return first, passed, landed, handed

    def start(ins, outs, sems):
        for per_buf in copies(outs, *sems, started_only=True):
            for cp in per_buf:
                cp.start()

    def finish(ins, outs, sems):
        first, passed, landed, handed = copies(outs, *sems)
        for a in range(n):
            for k in range(3):
                landed[a][k].wait_recv()
                passed[a][k].start()
        for a in range(n):
            for k in range(3):
                handed[a][k].wait_recv()
        for a in range(n):
            for cp in first[a] + passed[a]:
                cp.wait_send()

    return _Rider(list(bufs), _same(bufs), {a: a for a in range(n)},
                  [pltpu.SemaphoreType.DMA((6 * n,)), pltpu.SemaphoreType.DMA((6 * n,))], start, finish)


def _sibling_exchange(parts):
    n = len(parts)

    def copies(ins, outs, send_sems, recv_sems):
        x, y, c, _ = _place()
        return [_remote(ins[a].at[1 - c], outs[a], send_sems.at[a], recv_sems.at[a], (x, y, 1 - c)) for a in range(n)]

    def start(ins, outs, sems):
        for cp in copies(ins, outs, *sems):
            cp.start()

    def finish(ins, outs, sems):
        for cp in copies(ins, outs, *sems):
            cp.wait_recv()
            cp.wait_send()

    return _Rider(list(parts), [jax.ShapeDtypeStruct(p.shape[1:], p.dtype) for p in parts], {},
                  [pltpu.SemaphoreType.DMA((n,)), pltpu.SemaphoreType.DMA((n,))], start, finish)


def _small_allgather(small):
    flips = [(fx, fy, fc) for fx in range(2) for fy in range(2) for fc in range(2)][1:]

    def copies(small_ref, gather_ref, send_sems, recv_sems, local_sem, started_only=False):
        x, y, c, _ = _place()
        me = 4 * x + 2 * y + c
        peers = [((1 - x) if fx else x, (1 - y) if fy else y, (1 - c) if fc else c) for fx, fy, fc in flips]
        own = pltpu.make_async_copy(small_ref, gather_ref.at[me], local_sem)
        sent = [_remote(small_ref, gather_ref.at[me], send_sems.at[k], recv_sems.at[k], peer) for k, peer in enumerate(peers)]
        if started_only:
            return own, sent
        landed = [_remote(small_ref, gather_ref.at[4 * px + 2 * py + pc], send_sems.at[k], recv_sems.at[k], (px, py, pc))
                  for k, (px, py, pc) in enumerate(peers)]
        return own, sent, landed

    def start(ins, outs, sems):
        own, sent = copies(ins[0], outs[0], *sems, started_only=True)
        own.start()
        for cp in sent:
            cp.start()

    def finish(ins, outs, sems):
        own, sent, landed = copies(ins[0], outs[0], *sems)
        for cp in landed:
            cp.wait_recv()
        for cp in sent:
            cp.wait_send()
        own.wait()

    return _Rider([small], [jax.ShapeDtypeStruct((2 * N_CHIPS,) + small.shape, small.dtype)], {},
                  [pltpu.SemaphoreType.DMA((7,)), pltpu.SemaphoreType.DMA((7,)), pltpu.SemaphoreType.DMA], start, finish)


def _merge(a, b):
    na, nao, nas = len(a.operands), len(a.out_shapes), len(a.scratch)

    def start(ins, outs, sems):
        a.start(ins[:na], outs[:nao], sems[:nas])
        b.start(ins[na:], outs[nao:], sems[nas:])

    def finish(ins, outs, sems):
        a.finish(ins[:na], outs[:nao], sems[:nas])
        b.finish(ins[na:], outs[nao:], sems[nas:])

    aliases = {**a.aliases, **{na + k: nao + v for k, v in b.aliases.items()}}
    return _Rider(a.operands + b.operands, a.out_shapes + b.out_shapes, aliases, a.scratch + b.scratch, start, finish)


def _scatter_step(pair, stack, step):
    def copies(pair_ref, stack_ref, send_sem, recv_sem, started_only=False):
        x, y, c, _ = _place()
        me = 2 * x + y
        to = (me + 1 + step) % N_CHIPS
        frm = (me + N_CHIPS - 1 - step) % N_CHIPS
        sent = _remote(pair_ref.at[0], stack_ref.at[me], send_sem, recv_sem, (to // 2, to % 2, c))
        if started_only:
            return sent
        landed = _remote(pair_ref.at[0], stack_ref.at[frm], send_sem, recv_sem, (frm // 2, frm % 2, c))
        return sent, landed

    def start(ins, outs, sems):
        copies(ins[0], outs[0], *sems, started_only=True).start()

    def finish(ins, outs, sems):
        sent, landed = copies(ins[0], outs[0], *sems)
        landed.wait_recv()
        sent.wait_send()

    return _Rider([pair, stack], _same([stack]), {1: 0}, [pltpu.SemaphoreType.DMA, pltpu.SemaphoreType.DMA], start, finish)


def _scatter(sums):
    n = len(sums)

    def copies(ins, outs, send_sems, recv_sems, started_only=False):
        x, y, c, chips = _place()
        me = 2 * x + y
        slots = [2 * cx + cy for cx, cy in chips]
        sent = [_remote(ins[a].at[slots[k]], outs[a].at[me], send_sems.at[3 * a + k], recv_sems.at[3 * a + k], (*chips[k], c))
                for a in range(n) for k in range(3)]
        if started_only:
            return sent
        landed = [_remote(ins[a].at[slots[k]], outs[a].at[slots[k]], send_sems.at[3 * a + k], recv_sems.at[3 * a + k],
                          (*chips[k], c)) for a in range(n) for k in range(3)]
        return sent, landed

    def start(ins, outs, sems):
        for cp in copies(ins, outs, *sems, started_only=True):
            cp.start()

    def finish(ins, outs, sems):
        sent, landed = copies(ins, outs, *sems)
        for cp in landed:
            cp.wait_recv()
        for cp in sent:
            cp.wait_send()

    return _Rider(list(sums), _same(sums), {}, [pltpu.SemaphoreType.DMA((3 * n,)), pltpu.SemaphoreType.DMA((3 * n,))],
                  start, finish)


def _sibling_share(bufs):
    n = len(bufs)

    def copies(outs, send_sems, recv_sems, started_only=False):
        x, y, c, _ = _place()
        sent = [_remote(outs[a].at[c], outs[a].at[c], send_sems.at[a], recv_sems.at[a], (x, y, 1 - c)) for a in range(n)]
        if started_only:
            return sent
        landed = [_remote(outs[a].at[c], outs[a].at[1 - c], send_sems.at[a], recv_sems.at[a], (x, y, 1 - c)) for a in range(n)]
        return sent, landed

    def start(ins, outs, sems):
        for cp in copies(outs, *sems, started_only=True):
            cp.start()

    def finish(ins, outs, sems):
        sent, landed = copies(outs, *sems)
        for cp in landed:
            cp.wait_recv()
        for cp in sent:
            cp.wait_send()

    return _Rider(list(bufs), _same(bufs), {a: a for a in range(n)},
                  [pltpu.SemaphoreType.DMA((n,)), pltpu.SemaphoreType.DMA((n,))], start, finish)


def _pair_sum(core, part, received, name):
    _, k, rh, cols = part.shape

    def body(core_ref, p_ref, r_ref, o_ref):
        o_ref[...] = (p_ref[0].astype(F32) + r_ref[...].astype(F32)).astype(BF16)

    return pl.pallas_call(
        body, name=name,
        grid_spec=pltpu.PrefetchScalarGridSpec(
            num_scalar_prefetch=1, grid=(k,),
            in_specs=[pl.BlockSpec((1, 1, rh, cols), lambda j, core_ref: (core_ref[0], j, 0, 0)),
                      pl.BlockSpec((1, rh, cols), lambda j, core_ref: (j, 0, 0))],
            out_specs=pl.BlockSpec((1, rh, cols), lambda j, core_ref: (j, 0, 0))),
        out_shape=jax.ShapeDtypeStruct((k, rh, cols), BF16),
        compiler_params=_params(("parallel",), 32),
    )(core, part, received)


def _sum_leading(stack, steps, name):
    k, rows, cols = stack.shape
    tile = rows // steps

    def body(s_ref, o_ref):
        total = s_ref[0].astype(F32)
        for d in range(1, k):
            total = total + s_ref[d].astype(F32)
        o_ref[...] = total

    return pl.pallas_call(
        body, name=name, grid=(steps,),
        in_specs=[pl.BlockSpec((k, tile, cols), lambda i: (0, i, 0))],
        out_specs=pl.BlockSpec((tile, cols), lambda i: (i, 0)),
        out_shape=jax.ShapeDtypeStruct((rows, cols), F32),
        compiler_params=_params(("parallel",), 32),
    )(stack)


def _chip_sum(place, own, stack, steps, name):
    k, rows, cols = stack.shape
    tile = rows // steps

    def body(place_ref, own_ref, *refs):
        chip = place_ref[1]
        total = None
        for d in range(k):
            term = jnp.where(chip == d, own_ref[0], refs[d][0]).astype(F32)
            total = term if total is None else total + term
        refs[k][0] = total

    def other(d):
        return lambda i, place_ref: (jnp.where(place_ref[1] == d, (d + 1) % k, d), i, 0)

    return pl.pallas_call(
        body, name=name,
        grid_spec=pltpu.PrefetchScalarGridSpec(
            num_scalar_prefetch=1, grid=(steps,),
            in_specs=[pl.BlockSpec((1, tile, cols), lambda i, place_ref: (place_ref[1] % own.shape[0], i, 0))]
            + [pl.BlockSpec((1, tile, cols), other(d)) for d in range(k)],
            out_specs=pl.BlockSpec((1, tile, cols), lambda i, place_ref: (place_ref[0], i, 0))),
        out_shape=jax.ShapeDtypeStruct((2, rows, cols), F32),
        compiler_params=_params(("arbitrary",), 32),
    )(place, own, *([stack] * k))


def _adamw(w, g, row0, m, v, tile, name):
    rows, cols = w.shape
    first = row0 // tile
    assert rows % tile == 0 and row0 % tile == 0
    bc1 = 1.0 - ADAM_B1 ** ADAM_STEP
    bc2 = 1.0 - ADAM_B2 ** ADAM_STEP

    def body(w_ref, g_ref, m_ref, v_ref, go_ref, d_ref, mo_ref, vo_ref):
        g = g_ref[...]
        m_new = ADAM_B1 * m_ref[...] + (1.0 - ADAM_B1) * g
        v_new = ADAM_B2 * v_ref[...] + (1.0 - ADAM_B2) * (g * g)
        go_ref[...] = g
        d_ref[...] = -ADAM_LR * ((m_new / bc1) / (jnp.sqrt(v_new / bc2) + ADAM_EPS) + ADAM_WD * w_ref[...])
        mo_ref[...] = m_new
        vo_ref[...] = v_new

    spec = pl.BlockSpec((tile, cols), lambda i: (i, 0))
    g_spec = pl.BlockSpec((tile, cols), lambda i: (first + i, 0))
    return pl.pallas_call(
        body, name=name, grid=(rows // tile,),
        in_specs=[spec, g_spec, spec, spec], out_specs=[spec] * 4,
        out_shape=[jax.ShapeDtypeStruct((rows, cols), F32)] * 4,
        compiler_params=_params(("parallel",), 32),
    )(w, g, m, v)


SMALL = ("ffn1_norm", "mix_norm", "ffn2_norm", "final_norm", "pool_scale", "sink_logits", "pool_w")


def _pack_small(d, last_row=None):
    sink = jnp.pad(d["sink_logits"].reshape(1, N_HEADS), ((0, 0), (0, LANES - N_HEADS)))
    rows = [d[n].reshape(-1, LANES) for n in SMALL[:5]] + [sink, d["pool_w"].reshape(-1, LANES)]
    used = sum(r.shape[0] for r in rows)
    last = jnp.zeros((1, LANES), F32) if last_row is None else last_row
    return jnp.concatenate(rows + [jnp.zeros((SMALL_ROWS - used - 1, LANES), F32), last], axis=0)


def _unpack_small(packed, like):
    out, row = {}, 0
    for n in SMALL:
        size = LANES if n == "sink_logits" else math.prod(like[n].shape)
        chunk = packed[row:row + size // LANES].reshape(-1)
        out[n] = (chunk[:N_HEADS] if n == "sink_logits" else chunk).reshape(like[n].shape)
        row += size // LANES
    return out


def kernel(x, ffn1_norm, ffn1_w_gate, ffn1_w_up, ffn1_w_down, mix_norm, w_in, sink_logits, pool_w, pool_scale, w_out, ffn2_norm, ffn2_w_gate, ffn2_w_up, ffn2_w_down, final_norm, loss_target, m_ffn1_norm, m_ffn1_w_gate, m_ffn1_w_up, m_ffn1_w_down, m_mix_norm, m_w_in, m_sink_logits, m_pool_w, m_pool_scale, m_w_out, m_ffn2_norm, m_ffn2_w_gate, m_ffn2_w_up, m_ffn2_w_down, m_final_norm, v_ffn1_norm, v_ffn1_w_gate, v_ffn1_w_up, v_ffn1_w_down, v_mix_norm, v_w_in, v_sink_logits, v_pool_w, v_pool_scale, v_w_out, v_ffn2_norm, v_ffn2_w_gate, v_ffn2_w_up, v_ffn2_w_down, v_final_norm):
    names = ("ffn1_norm", "ffn1_w_gate", "ffn1_w_up", "ffn1_w_down", "mix_norm", "w_in", "sink_logits", "pool_w",
             "pool_scale", "w_out", "ffn2_norm", "ffn2_w_gate", "ffn2_w_up", "ffn2_w_down", "final_norm")
    weights = dict(zip(names, (ffn1_norm, ffn1_w_gate, ffn1_w_up, ffn1_w_down, mix_norm, w_in, sink_logits, pool_w,
                               pool_scale, w_out, ffn2_norm, ffn2_w_gate, ffn2_w_up, ffn2_w_down, final_norm)))
    mom1 = dict(zip(names, (m_ffn1_norm, m_ffn1_w_gate, m_ffn1_w_up, m_ffn1_w_down, m_mix_norm, m_w_in, m_sink_logits,
                            m_pool_w, m_pool_scale, m_w_out, m_ffn2_norm, m_ffn2_w_gate, m_ffn2_w_up, m_ffn2_w_down,
                            m_final_norm)))
    mom2 = dict(zip(names, (v_ffn1_norm, v_ffn1_w_gate, v_ffn1_w_up, v_ffn1_w_down, v_mix_norm, v_w_in, v_sink_logits,
                            v_pool_w, v_pool_scale, v_w_out, v_ffn2_norm, v_ffn2_w_gate, v_ffn2_w_up, v_ffn2_w_down,
                            v_final_norm)))
    chip = (2 * lax.axis_index("x") + lax.axis_index("y")).astype(jnp.int32).reshape(1)
    place = jnp.concatenate([lax.axis_index("c").astype(jnp.int32).reshape(1), chip])

    def rows_of(t, n):
        return jnp.swapaxes(t[n][0], 0, 1) if n in TRANSPOSED else t[n][0]

    packed = [_pack(chip, [rows_of(weights, n) for n in GROUPS[0]], "pack_ffn1"),
              _pack(chip, [jnp.concatenate([rows_of(weights, n) for n in GROUPS[1]], axis=0)], "pack_mix"),
              _pack(chip, [rows_of(weights, n) for n in GROUPS[2]], "pack_ffn2")]
    bufs = [p.reshape(N_CHIPS, 2, p.shape[1] // 2, D_MODEL) for p in packed]

    small_w = {"ffn1_norm": ffn1_norm, "mix_norm": mix_norm, "ffn2_norm": ffn2_norm,
               "final_norm": final_norm.reshape(1, D_MODEL), "pool_scale": pool_scale, "sink_logits": sink_logits,
               "pool_w": pool_w[0]}
    loss, grad_x, group_grads, small_sum = _step(x[0], loss_target[0], bufs, small_w, place)

    out_g, out_d, out_m, out_v = {}, {}, {}, {}
    for members, g in zip(GROUPS, group_grads):
        row0 = 0
        for n in members:
            w = rows_of(weights, n)
            tile = FF_CHUNK // 4 if w.shape[0] == FF_CHUNK else math.gcd(IN_ROWS, OUT_ROWS)
            outs = _adamw(w, g, row0, rows_of(mom1, n), rows_of(mom2, n), tile, "adamw_" + n)
            row0 += w.shape[0]
            for dst, t in zip((out_g, out_d, out_m, out_v), outs):
                dst[n] = (jnp.swapaxes(t, 0, 1) if n in TRANSPOSED else t).reshape(weights[n].shape)
    small_outs = _adamw(_pack_small(weights), small_sum, 0, _pack_small(mom1), _pack_small(mom2), SMALL_ROWS, "adamw_small")
    for dst, packed in zip((out_g, out_d, out_m, out_v), small_outs):
        dst.update(_unpack_small(packed, weights))

    return (loss,grad_x.reshape(x.shape), *[out_g[n] for n in names], *[out_d[n] for n in names],
            *[out_m[n] for n in names], *[out_v[n] for n in names])
```

```python
import collections
import functools
import math

import jax
import jax.numpy as jnp
from jax import lax
from jax.experimental import pallas as pl
from jax.experimental.pallas import tpu as pltpu

F32, BF16 = jnp.float32, jnp.bfloat16
MESH = pl.DeviceIdType.MESH

D_MODEL = 1024
D_FF = 2816
N_CHIPS = 4
FF_CHUNK = D_FF // N_CHIPS
HEAD_DIM = 64
N_HEADS = 8
N_KV = 2
Q_PER_KV = N_HEADS // N_KV
KV_WIDTH = N_KV * HEAD_DIM
ATTN_WIDTH = N_HEADS * HEAD_DIM
POOL_WINDOWS = (2, 4, 8, 16)
N_POOL = len(POOL_WINDOWS)
POOL_GROUP = 128
POOL_WIDTH = N_POOL * POOL_GROUP
IN_WIDTH = ATTN_WIDTH + 2 * KV_WIDTH + POOL_WIDTH
WINDOW = 128
BLOCK = 128
BAND = 3 * BLOCK
ROPE_THETA = 500000.0
ROTARY_DIM = HEAD_DIM // 4
EPS = 1e-6
LANES = 128
Q_PAD = N_HEADS * LANES
U_PAD = Q_PAD + 2 * KV_WIDTH + POOL_WIDTH
SCALE = HEAD_DIM ** -0.5
NEG = -1e30

ADAM_LR, ADAM_B1, ADAM_B2, ADAM_EPS, ADAM_WD, ADAM_STEP = 0.001, 0.9, 0.999, 1e-08, 0.01, 10

V7X_VMEM_BYTES = 64 * 1024 * 1024
TOK_TILE = 512
BWD_TILE = 256
SMALL_ROWS = 552


def _params(sem, vmem_mb):
    assert vmem_mb * 1024 * 1024 <= V7X_VMEM_BYTES
    return pltpu.CompilerParams(dimension_semantics=sem, vmem_limit_bytes=vmem_mb * 1024 * 1024)


def _dot(a, b):
    return lax.dot_general(a, b, (((1,), (0,)), ((), ())), preferred_element_type=F32)


def _dot_nt(a, b):
    return lax.dot_general(a, b, (((1,), (1,)), ((), ())), preferred_element_type=F32)


def _dot_tn(a, b):
    return lax.dot_general(a, b, (((0,), (0,)), ((), ())), preferred_element_type=F32)


def _rms_stats(h):
    r = lax.rsqrt(jnp.mean(h * h, axis=-1, keepdims=True) + EPS)
    return r, h * r


def _rms_bwd(dn, g, r, xh):
    gd = dn * g
    dh = r * (gd - xh * jnp.mean(gd * xh, axis=-1, keepdims=True))
    return dh, jnp.sum(dn * xh, axis=0, keepdims=True)


def _rope(x, c, s1, s2):
    return x * c + pltpu.roll(x, LANES - ROTARY_DIM // 2, 1) * s1 + pltpu.roll(x, ROTARY_DIM // 2, 1) * s2


def _rope_bwd(d, c, s1, s2):
    return d * c + pltpu.roll(d * s1, ROTARY_DIM // 2, 1) + pltpu.roll(d * s2, LANES - ROTARY_DIM // 2, 1)


def _sum_chunks(refs):
    terms = [ref[j].astype(F32) for ref in refs for j in range(ref.shape[0])]
    return functools.reduce(lambda a, b: a + b, terms)


def _chunk_rows(tile, k):
    return pl.BlockSpec((k, tile, D_MODEL), lambda i, *_: (0, i, 0))


def _full(shape):
    nd = len(shape)
    return pl.BlockSpec(shape, lambda *_: (0,) * nd)


def _rows(tile, cols):
    return pl.BlockSpec((tile, cols), lambda i, *_: (i, 0))


HBM_SPEC = pl.BlockSpec(memory_space=pltpu.HBM)

_Rider = collections.namedtuple("_Rider", "operands out_shapes aliases scratch start finish")


_NO_RIDER = _Rider([], [], {}, [], None, None)


def _call(body, name, grid, in_specs, out_specs, out_shape, scratch, vmem_mb, args, rider=None, prefetch=()):
    rider = rider or _NO_RIDER
    n_pre, n_in, n_out, n_scr = len(prefetch), len(in_specs), len(out_specs), len(scratch)
    r_in, r_out = len(rider.operands), len(rider.out_shapes)

    def fused(*refs):
        pre, refs = refs[:n_pre], refs[n_pre:]
        ins, refs = refs[:n_in], refs[n_in:]
        r_ins, refs = refs[:r_in], refs[r_in:]
        outs, refs = refs[:n_out], refs[n_out:]
        r_outs, refs = refs[:r_out], refs[r_out:]
        scr, r_scr = refs[:n_scr], refs[n_scr:]
        ids = [pl.program_id(d) for d in range(len(grid))]
        if rider.start is not None:
            @pl.when(functools.reduce(jnp.logical_and, [i == 0 for i in ids]))
            def _():
                rider.start(r_ins, r_outs, r_scr)

        body(*pre, *ins, *outs, *scr)

        if rider.finish is not None:
            @pl.when(functools.reduce(jnp.logical_and, [i == g - 1 for i, g in zip(ids, grid)]))
            def _():
                rider.finish(r_ins, r_outs, r_scr)

    return pl.pallas_call(
        fused, name=name,
        grid_spec=pltpu.PrefetchScalarGridSpec(
            num_scalar_prefetch=n_pre, grid=grid,
            in_specs=list(in_specs) + [HBM_SPEC] * r_in, out_specs=list(out_specs) + [HBM_SPEC] * r_out,
            scratch_shapes=list(scratch) + list(rider.scratch)),
        out_shape=list(out_shape) + list(rider.out_shapes),
        input_output_aliases={n_pre + n_in + k: n_out + v for k, v in rider.aliases.items()},
        compiler_params=_params(("arbitrary",) * len(grid), vmem_mb),
    )(*prefetch, *args, *rider.operands)


def _comm_call(rider, name):
    r_in, r_out = len(rider.operands), len(rider.out_shapes)

    def body(*refs):
        r_ins, r_outs, r_scr = refs[:r_in], refs[r_in:r_in + r_out], refs[r_in + r_out:]
        rider.start(r_ins, r_outs, r_scr)
        rider.finish(r_ins, r_outs, r_scr)

    return pl.pallas_call(
        body, name=name, in_specs=[HBM_SPEC] * r_in, out_specs=[HBM_SPEC] * r_out, out_shape=list(rider.out_shapes),
        input_output_aliases=dict(rider.aliases), scratch_shapes=list(rider.scratch),
    )(*rider.operands)


def _ffn_fwd(h, gain, group, name, rider=None):
    S = h.shape[0]
    tile = min(TOK_TILE, S)
    nt = S // tile

    def body(h_ref, g_ref, wg_ref, wu_ref, wd_ref, ho_ref, n_ref, gate_ref, up_ref, acc):
        j = pl.program_id(1)

        @pl.when(j == 0)
        def _():
            _, xh = _rms_stats(h_ref[...])
            n_ref[...] = (xh * g_ref[...]).astype(BF16)
            acc[...] = jnp.zeros_like(acc)

        n = n_ref[...]
        gate = _dot_nt(n, wg_ref[0])
        up = _dot_nt(n, wu_ref[0])
        gate_ref[0] = gate.astype(BF16)
        up_ref[0] = up.astype(BF16)
        act = (gate * jax.nn.sigmoid(gate) * up).astype(BF16)
        acc[...] += _dot(act, wd_ref[0])

        @pl.when(j == N_CHIPS - 1)
        def _():
            ho_ref[...] = h_ref[...] + 0.5 * acc[...]

    tok = pl.BlockSpec((tile, D_MODEL), lambda i, j: (i, 0))
    hid = pl.BlockSpec((1, tile, FF_CHUNK), lambda i, j: (j, i, 0))
    return _call(
        body, name, (nt, N_CHIPS),
        [tok, pl.BlockSpec((1, D_MODEL), lambda i, j: (0, 0))]
        + [pl.BlockSpec((1, FF_CHUNK, D_MODEL), functools.partial(lambda i, j, part: (j, part, 0), part=part))
           for part in range(3)],
        [tok, tok, hid, hid],
        [jax.ShapeDtypeStruct((S, D_MODEL), F32), jax.ShapeDtypeStruct((S, D_MODEL), BF16),
         jax.ShapeDtypeStruct((N_CHIPS, S, FF_CHUNK), BF16), jax.ShapeDtypeStruct((N_CHIPS, S, FF_CHUNK), BF16)],
        [pltpu.VMEM((tile, D_MODEL), F32)], 26, (h, gain, group, group, group), rider)


def _ffn_bwd(chunks, d_out, n, gate, up, group, name, rider=None):
    S = n.shape[0]
    n_chunks = chunks.shape[0]
    tile = min(TOK_TILE, S)
    nt = S // tile
    half_rows = 3 * FF_CHUNK // 2
    cut = FF_CHUNK // 2

    def body(chunks_ref, do_ref, n_ref, gate_ref, up_ref, wg_ref, wu_ref, wd_ref, dn_ref, dw_ref, acc_g, acc_u, acc_d):
        j, i = pl.program_id(0), pl.program_id(1)

        @pl.when(i == 0)
        def _():
            acc_g[...] = jnp.zeros_like(acc_g)
            acc_u[...] = jnp.zeros_like(acc_u)
            acc_d[...] = jnp.zeros_like(acc_d)

        do = do_ref[...]
        nn = n_ref[...]
        g = gate_ref[0].astype(F32)
        u = up_ref[0].astype(F32)
        d_act = _dot_nt(do, wd_ref[0])
        sig = jax.nn.sigmoid(g)
        silu = g * sig
        d_up = (d_act * silu).astype(BF16)
        d_gate = (d_act * u * (sig * (1.0 + g * (1.0 - sig)))).astype(BF16)
        act = (silu * u).astype(BF16)
        dn_ref[0] = (_dot(d_gate, wg_ref[0]) + _dot(d_up, wu_ref[0])).astype(BF16)
        acc_g[...] += _dot_tn(d_gate, nn)
        acc_u[...] += _dot_tn(d_up, nn)
        acc_d[...] += _dot_tn(act, do)

        @pl.when(i == nt - 1)
        def _():
            dw_ref[0, 0, :FF_CHUNK, :] = acc_g[...].astype(BF16)
            dw_ref[0, 0, FF_CHUNK:, :] = acc_u[:cut, :].astype(BF16)
            dw_ref[1, 0, :cut, :] = acc_u[cut:, :].astype(BF16)
            dw_ref[1, 0, cut:, :] = acc_d[...].astype(BF16)

    tok = pl.BlockSpec((tile, D_MODEL), lambda j, i, chunks_ref: (i, 0))
    hid = pl.BlockSpec((1, tile, FF_CHUNK), lambda j, i, chunks_ref: (chunks_ref[j], i, 0))
    return _call(
        body, name, (n_chunks, nt),
        [tok, tok, hid, hid]
        + [pl.BlockSpec((1, FF_CHUNK, D_MODEL), functools.partial(lambda j, i, chunks_ref, part: (chunks_ref[j], part, 0), part=part))
           for part in range(3)],
        [pl.BlockSpec((1, tile, D_MODEL), lambda j, i, chunks_ref: (j, i, 0)),
         pl.BlockSpec((2, 1, half_rows, D_MODEL), lambda j, i, chunks_ref: (0, j, 0, 0))],
        [jax.ShapeDtypeStruct((n_chunks, S, D_MODEL), BF16), jax.ShapeDtypeStruct((2, n_chunks, half_rows, D_MODEL), BF16)],
        [pltpu.VMEM((FF_CHUNK, D_MODEL), F32)] * 3, 42, (d_out, n, gate, up, group, group, group), rider, (chunks,))


def _mix_in(h, gain, w_in, rc, rs1, rs2, name):
    S = h.shape[0]
    tile = min(TOK_TILE, S)

    def body(h_ref, g_ref, w_ref, c_ref, s1_ref, s2_ref, n_ref, q_ref, k_ref, v_ref, pc_ref):
        _, xh = _rms_stats(h_ref[...])
        n = (xh * g_ref[...]).astype(BF16)
        n_ref[...] = n
        u = _dot_nt(n, w_ref[...])
        c, s1, s2 = c_ref[...], s1_ref[...], s2_ref[...]
        q_ref[...] = jnp.concatenate([(_rope(u[:, hd * LANES:(hd + 1) * LANES], c, s1, s2) * SCALE).astype(BF16)
                                      for hd in range(N_HEADS)], axis=1)
        k_ref[...] = _rope(u[:, Q_PAD:Q_PAD + KV_WIDTH], c, s1, s2).astype(BF16)
        v_ref[...] = u[:, Q_PAD + KV_WIDTH:Q_PAD + 2 * KV_WIDTH].astype(BF16)
        pc_ref[...] = u[:, Q_PAD + 2 * KV_WIDTH:]

    return pl.pallas_call(
        body, name=name, grid=(S // tile,),
        in_specs=[_rows(tile, D_MODEL), _full((1, D_MODEL)), _full((U_PAD, D_MODEL)),
                  _rows(tile, LANES), _rows(tile, LANES), _rows(tile, LANES)],
        out_specs=[_rows(tile, D_MODEL), _rows(tile, Q_PAD), _rows(tile, KV_WIDTH), _rows(tile, KV_WIDTH),
                   _rows(tile, POOL_WIDTH)],
        out_shape=[jax.ShapeDtypeStruct((S, D_MODEL), BF16), jax.ShapeDtypeStruct((S, Q_PAD), BF16),
                   jax.ShapeDtypeStruct((S, KV_WIDTH), BF16), jax.ShapeDtypeStruct((S, KV_WIDTH), BF16),
                   jax.ShapeDtypeStruct((S, POOL_WIDTH), F32)],
        compiler_params=_params(("parallel",), 14),
    )(h, gain, w_in, rc, rs1, rs2)


def _band_start(i, S):
    return pl.multiple_of(jnp.clip((i - 1) * BLOCK, 0, S - BAND), BLOCK)


def _window_bias(off):
    r = lax.broadcasted_iota(jnp.int32, (BLOCK, 1), 0)
    c = lax.broadcasted_iota(jnp.int32, (1, BAND), 1)
    return jnp.where(jnp.abs(off + r - c) <= WINDOW, 0.0, NEG).astype(F32)


def _softmax_parts(qh, kb, bias, sink_h):
    s = _dot_nt(qh, kb) + bias
    m = jnp.maximum(jnp.max(s, axis=-1, keepdims=True), sink_h)
    p = jnp.exp(s - m)
    es = jnp.exp(sink_h - m)
    return p, es, 1.0 / (jnp.sum(p, axis=-1, keepdims=True) + es)


def _pool_matrix(t0, start, S, w):
    r = lax.broadcasted_iota(jnp.int32, (BLOCK, 1), 0) + t0
    c = lax.broadcasted_iota(jnp.int32, (1, BAND), 1) + start
    half = w // 2

    def window(lo, hi):
        a = jnp.maximum(lo, 0)
        b = jnp.minimum(hi + 1, S)
        return jnp.where((c >= a) & (c < b), 1.0 / (b - a).astype(F32), 0.0)

    return (0.5 * (window(r - half, r + half - 1) + window(r - half + 1, r + half))).astype(BF16)


def _pool_matrices(S):
    blocks = ((0, 0), (BLOCK, 0), (S - BLOCK, S - BAND))
    return jnp.stack([jnp.stack([_pool_matrix(t0, start, S, w) for w in POOL_WINDOWS]) for t0, start in blocks])


def _pool_spec(nb):
    return pl.BlockSpec((1, N_POOL, BLOCK, BAND), lambda i, *_: (jnp.where(i == 0, 0, jnp.where(i == nb - 1, 2, 1)), 0, 0, 0))


def _mix_core_fwd(q, k, v, pc, sink, pool_m, pool_w, pool_scale, name):
    S = q.shape[0]
    nb = S // BLOCK

    def body(sink_ref, q_ref, k_ref, v_ref, pc_ref, pm_ref, pw_ref, ps_ref, a_ref, p_ref):
        i = pl.program_id(0)
        start = _band_start(i, S)
        band = pl.ds(start, BAND)
        bias = _window_bias(i * BLOCK - start)
        kb, vb = k_ref[band, :], v_ref[band, :]
        hs = range(N_HEADS)
        ss = [_dot_nt(q_ref[:, hd * LANES:(hd + 1) * LANES], kb) + bias for hd in hs]
        ms = [jnp.maximum(jnp.max(ss[hd], axis=-1, keepdims=True), sink_ref[0, hd]) for hd in hs]
        ps = [jnp.exp(ss[hd] - ms[hd]) for hd in hs]
        invs = [1.0 / (jnp.sum(ps[hd], axis=-1, keepdims=True) + jnp.exp(sink_ref[0, hd] - ms[hd])) for hd in hs]
        outs = [_dot(ps[hd].astype(BF16), vb) for hd in hs]
        a_ref[...] = jnp.concatenate([(outs[hd] * invs[hd]).astype(BF16) for hd in hs], axis=1)
        centre = pl.ds(pl.multiple_of(i * BLOCK, BLOCK), BLOCK)
        gs = range(N_POOL)
        sl = [slice(g * POOL_GROUP, (g + 1) * POOL_GROUP) for g in gs]
        means = [_dot(pm_ref[0, g], pc_ref[band, sl[g]].astype(BF16)) for g in gs]
        devs = [(means[g] - pc_ref[centre, sl[g]]).astype(BF16) for g in gs]
        p_ref[...] = (jnp.concatenate([_dot(devs[g], pw_ref[g]) for g in gs], axis=1) * ps_ref[...]).astype(BF16)

    return pl.pallas_call(
        body, name=name, grid=(nb,),
        in_specs=[pl.BlockSpec(memory_space=pltpu.SMEM), _rows(BLOCK, Q_PAD), _full((S, KV_WIDTH)), _full((S, KV_WIDTH)),
                  _full((S, POOL_WIDTH)), _pool_spec(nb), _full((N_POOL, POOL_GROUP, POOL_GROUP)), _full((1, POOL_WIDTH))],
        out_specs=[_rows(BLOCK, Q_PAD), _rows(BLOCK, POOL_WIDTH)],
        out_shape=[jax.ShapeDtypeStruct((S, Q_PAD), BF16), jax.ShapeDtypeStruct((S, POOL_WIDTH), BF16)],
        compiler_params=_params(("parallel",), 13),
    )(sink, q, k, v, pc, pool_m, pool_w, pool_scale)


def _mix_core_bwd(q, k, v, pc, da, dp, sink, pool_m, pool_w, pool_scale, rc, rs1, rs2, name, rider=None):
    S = q.shape[0]
    nb = S // BLOCK

    def body(sink_ref, q_ref, k_ref, v_ref, pc_ref, da_ref, dp_ref, pm_ref, pw_ref, ps_ref, c_ref, s1_ref, s2_ref,
             dq_ref, dk_ref, dv_ref, dpc_ref, dsink_ref, dpw_ref, dps_ref):
        i = pl.program_id(0)

        @pl.when(i == 0)
        def _():
            dk_ref[...] = jnp.zeros_like(dk_ref)
            dv_ref[...] = jnp.zeros_like(dv_ref)
            dpc_ref[...] = jnp.zeros_like(dpc_ref)
            dsink_ref[...] = jnp.zeros_like(dsink_ref)
            dpw_ref[...] = jnp.zeros_like(dpw_ref)
            dps_ref[...] = jnp.zeros_like(dps_ref)

        start = _band_start(i, S)
        band = pl.ds(start, BAND)
        bias = _window_bias(i * BLOCK - start)
        kb, vb = k_ref[band, :], v_ref[band, :]
        c, s1, s2 = c_ref[...], s1_ref[...], s2_ref[...]
        lane = lax.broadcasted_iota(jnp.int32, (1, LANES), 1)
        hs = range(N_HEADS)
        qs = [q_ref[:, hd * LANES:(hd + 1) * LANES] for hd in hs]
        das = [da_ref[:, hd * LANES:(hd + 1) * LANES] for hd in hs]
        ss = [_dot_nt(qs[hd], kb) + bias for hd in hs]
        d_probs = [_dot_nt(das[hd], vb) for hd in hs]
        ms = [jnp.maximum(jnp.max(ss[hd], axis=-1, keepdims=True), sink_ref[0, hd]) for hd in hs]
        ps = [jnp.exp(ss[hd] - ms[hd]) for hd in hs]
        ess = [jnp.exp(sink_ref[0, hd] - ms[hd]) for hd in hs]
        invs = [1.0 / (jnp.sum(ps[hd], axis=-1, keepdims=True) + ess[hd]) for hd in hs]
        probs = [ps[hd] * invs[hd] for hd in hs]
        deltas = [jnp.sum(probs[hd] * d_probs[hd], axis=-1, keepdims=True) for hd in hs]
        d_ss = [(probs[hd] * (d_probs[hd] - deltas[hd])).astype(BF16) for hd in hs]
        dqs = [_dot(d_ss[hd], kb) for hd in hs]
        dq_ref[...] = jnp.concatenate([_rope_bwd(dqs[hd] * SCALE, c, s1, s2).astype(BF16) for hd in hs], axis=1)
        dks = [_dot_tn(d_ss[hd], qs[hd]) for hd in hs]
        dvs = [_dot_tn(probs[hd].astype(BF16), das[hd]) for hd in hs]
        dk_ref[band, :] += functools.reduce(lambda a, b: a + b, dks)
        dv_ref[band, :] += functools.reduce(lambda a, b: a + b, dvs)
        dsink_ref[...] += functools.reduce(lambda a, b: a + b, [
            jnp.where(lane == hd, -jnp.sum(ess[hd] * invs[hd] * deltas[hd], axis=0, keepdims=True), 0.0) for hd in hs])

        centre = pl.ds(pl.multiple_of(i * BLOCK, BLOCK), BLOCK)
        gs = range(N_POOL)
        sl = [slice(g * POOL_GROUP, (g + 1) * POOL_GROUP) for g in gs]
        devs = [(_dot(pm_ref[0, g], pc_ref[band, sl[g]].astype(BF16)) - pc_ref[centre, sl[g]]).astype(BF16) for g in gs]
        dys = [dp_ref[:, sl[g]].astype(F32) for g in gs]
        zs = [_dot(devs[g], pw_ref[g]) for g in gs]
        dzs = [(dys[g] * ps_ref[:, sl[g]]).astype(BF16) for g in gs]
        d_devs = [_dot_nt(dzs[g], pw_ref[g]) for g in gs]
        dps_ref[...] += jnp.concatenate([jnp.sum(dys[g] * zs[g], axis=0, keepdims=True) for g in gs], axis=1)
        for g in gs:
            dpw_ref[g] += _dot_tn(devs[g], dzs[g])
        dpc_ref[band, :] += jnp.concatenate([_dot_tn(pm_ref[0, g], d_devs[g].astype(BF16)) for g in gs], axis=1)
        dpc_ref[centre, :] -= jnp.concatenate(d_devs, axis=1)

    return _call(
        body, name, (nb,),
        [pl.BlockSpec(memory_space=pltpu.SMEM), _rows(BLOCK, Q_PAD), _full((S, KV_WIDTH)), _full((S, KV_WIDTH)),
         _full((S, POOL_WIDTH)), _rows(BLOCK, Q_PAD), _rows(BLOCK, POOL_WIDTH), _pool_spec(nb),
         _full((N_POOL, POOL_GROUP, POOL_GROUP)), _full((1, POOL_WIDTH)),
         _rows(BLOCK, LANES), _rows(BLOCK, LANES), _rows(BLOCK, LANES)],
        [_rows(BLOCK, Q_PAD), _full((S, KV_WIDTH)), _full((S, KV_WIDTH)), _full((S, POOL_WIDTH)),
         _full((1, LANES)), _full((N_POOL, POOL_GROUP, POOL_GROUP)), _full((1, POOL_WIDTH))],
        [jax.ShapeDtypeStruct((S, Q_PAD), BF16), jax.ShapeDtypeStruct((S, KV_WIDTH), F32),
         jax.ShapeDtypeStruct((S, KV_WIDTH), F32), jax.ShapeDtypeStruct((S, POOL_WIDTH), F32),
         jax.ShapeDtypeStruct((1, LANES), F32), jax.ShapeDtypeStruct((N_POOL, POOL_GROUP, POOL_GROUP), F32),
         jax.ShapeDtypeStruct((1, POOL_WIDTH), F32)],
        [], 31, (sink, q, k, v, pc, da, dp, pool_m, pool_w, pool_scale, rc, rs1, rs2), rider)


def _mix_out(h, a, p, wa, wp, name):
    S = h.shape[0]
    tile = min(TOK_TILE, S)

    def body(h_ref, a_ref, p_ref, wa_ref, wp_ref, o_ref):
        o_ref[...] = h_ref[...] + _dot(a_ref[...], wa_ref[...]) + _dot(p_ref[...], wp_ref[...])

    return pl.pallas_call(
        body, name=name, grid=(S // tile,),
        in_specs=[_rows(tile, D_MODEL), _rows(tile, Q_PAD), _rows(tile, POOL_WIDTH),
                  _full((Q_PAD, D_MODEL)), _full((POOL_WIDTH, D_MODEL))],
        out_specs=_rows(tile, D_MODEL),
        out_shape=jax.ShapeDtypeStruct((S, D_MODEL), F32),
        compiler_params=_params(("parallel",), 13),
    )(h, a, p, wa, wp)


def _loss_head(h, target, gain, name):
    S = h.shape[0]
    tile = min(TOK_TILE, S)

    def body(h_ref, t_ref, g_ref, dh_ref, dhalf_ref, loss_ref, dg_ref):
        @pl.when(pl.program_id(0) == 0)
        def _():
            loss_ref[...] = jnp.zeros_like(loss_ref)
            dg_ref[...] = jnp.zeros_like(dg_ref)

        g = g_ref[...]
        r, xh = _rms_stats(h_ref[...])
        err = xh * g - t_ref[...]
        loss_ref[...] += (0.5 / D_MODEL) * jnp.sum(err * err, axis=0, keepdims=True)
        dh, dg = _rms_bwd(err * (1.0 / D_MODEL), g, r, xh)
        dg_ref[...] += dg
        dh_ref[...] = dh
        dhalf_ref[...] = (0.5 * dh).astype(BF16)

    return pl.pallas_call(
        body, name=name, grid=(S // tile,),
        in_specs=[_rows(tile, D_MODEL), _rows(tile, D_MODEL), _full((1, D_MODEL))],
        out_specs=[_rows(tile, D_MODEL), _rows(tile, D_MODEL), _full((1, D_MODEL)), _full((1, D_MODEL))],
        out_shape=[jax.ShapeDtypeStruct((S, D_MODEL), F32), jax.ShapeDtypeStruct((S, D_MODEL), BF16),
                   jax.ShapeDtypeStruct((1, D_MODEL), F32), jax.ShapeDtypeStruct((1, D_MODEL), F32)],
        compiler_params=_params(("arbitrary",), 18),
    )(h, target, gain)


def _mix_out_bwd(dh_out, dn, h, gain, a, p, wa, wp, name, rider=None):
    S = h.shape[0]
    tile = min(TOK_TILE, S)

    def body(do_ref, dn_ref, h_ref, g_ref, a_ref, p_ref, wa_ref, wp_ref, dh_ref, da_ref, dp_ref, dwa_ref, dwp_ref, dg_ref):
        @pl.when(pl.program_id(0) == 0)
        def _():
            dwa_ref[...] = jnp.zeros_like(dwa_ref)
            dwp_ref[...] = jnp.zeros_like(dwp_ref)
            dg_ref[...] = jnp.zeros_like(dg_ref)

        r, xh = _rms_stats(h_ref[...])
        dnorm, dg = _rms_bwd(_sum_chunks([dn_ref]), g_ref[...], r, xh)
        dh = do_ref[...] + dnorm
        dg_ref[...] += dg
        dh_ref[...] = dh
        dhb = dh.astype(BF16)
        da_ref[...] = _dot_nt(dhb, wa_ref[...]).astype(BF16)
        dp_ref[...] = _dot_nt(dhb, wp_ref[...]).astype(BF16)
        dwa_ref[...] += _dot_tn(a_ref[...], dhb)
        dwp_ref[...] += _dot_tn(p_ref[...], dhb)

    return _call(
        body, name, (S // tile,),
        [_rows(tile, D_MODEL), _chunk_rows(tile, dn.shape[0]), _rows(tile, D_MODEL), _full((1, D_MODEL)),
         _rows(tile, Q_PAD), _rows(tile, POOL_WIDTH), _full((Q_PAD, D_MODEL)), _full((POOL_WIDTH, D_MODEL))],
        [_rows(tile, D_MODEL), _rows(tile, Q_PAD), _rows(tile, POOL_WIDTH),
         _full((Q_PAD, D_MODEL)), _full((POOL_WIDTH, D_MODEL)), _full((1, D_MODEL))],
        [jax.ShapeDtypeStruct((S, D_MODEL), F32), jax.ShapeDtypeStruct((S, Q_PAD), BF16),
         jax.ShapeDtypeStruct((S, POOL_WIDTH), BF16), jax.ShapeDtypeStruct((Q_PAD, D_MODEL), F32),
         jax.ShapeDtypeStruct((POOL_WIDTH, D_MODEL), F32), jax.ShapeDtypeStruct((1, D_MODEL), F32)],
        [], 36, (dh_out, dn, h, gain, a, p, wa, wp), rider)


def _mix_in_bwd(dh_out, h, gain, n, dq, dk, dv, dpc, rc, rs1, rs2, w_in, name, rider=None):
    S = h.shape[0]
    tile = min(TOK_TILE, S)

    def body(do_ref, h_ref, g_ref, n_ref, dq_ref, dk_ref, dv_ref, dpc_ref, c_ref, s1_ref, s2_ref, w_ref,
             dh_ref, dhalf_ref, dw_ref, dg_ref):
        @pl.when(pl.program_id(0) == 0)
        def _():
            dw_ref[...] = jnp.zeros_like(dw_ref)
            dg_ref[...] = jnp.zeros_like(dg_ref)

        dk = _rope_bwd(dk_ref[...], c_ref[...], s1_ref[...], s2_ref[...]).astype(BF16)
        du = jnp.concatenate([dq_ref[...], dk, dv_ref[...].astype(BF16), dpc_ref[...].astype(BF16)], axis=1)
        dn = _dot(du, w_ref[...])
        dw_ref[...] += _dot_tn(du, n_ref[...])
        r, xh = _rms_stats(h_ref[...])
        dnorm, dg = _rms_bwd(dn, g_ref[...], r, xh)
        dh = do_ref[...] + dnorm
        dg_ref[...] += dg
        dh_ref[...] = dh
        dhalf_ref[...] = (0.5 * dh).astype(BF16)

    return _call(
        body, name, (S // tile,),
        [_rows(tile, D_MODEL), _rows(tile, D_MODEL), _full((1, D_MODEL)), _rows(tile, D_MODEL),
         _rows(tile, Q_PAD), _rows(tile, KV_WIDTH), _rows(tile, KV_WIDTH), _rows(tile, POOL_WIDTH),
         _rows(tile, LANES), _rows(tile, LANES), _rows(tile, LANES), _full((U_PAD, D_MODEL))],
        [_rows(tile, D_MODEL), _rows(tile, D_MODEL), _full((U_PAD, D_MODEL)), _full((1, D_MODEL))],
        [jax.ShapeDtypeStruct((S, D_MODEL), F32), jax.ShapeDtypeStruct((S, D_MODEL), BF16),
         jax.ShapeDtypeStruct((U_PAD, D_MODEL), F32), jax.ShapeDtypeStruct((1, D_MODEL), F32)],
        [], 41, (dh_out, h, gain, n, dq, dk, dv, dpc, rc, rs1, rs2, w_in), rider)


def _norm_bwd(dh_out, dns, h, gain, name):
    S = h.shape[0]
    tile = min(TOK_TILE, S)
    n = len(dns)

    def body(do_ref, *refs):
        h_ref, g_ref, dh_ref, dg_ref = refs[n:]

        @pl.when(pl.program_id(0) == 0)
        def _():
            dg_ref[...] = jnp.zeros_like(dg_ref)

        r, xh = _rms_stats(h_ref[...])
        dnorm, dg = _rms_bwd(_sum_chunks(refs[:n]), g_ref[...], r, xh)
        dg_ref[...] += dg
        dh_ref[...] = do_ref[...] + dnorm

    return _call(
        body, name, (S // tile,),
        [_rows(tile, D_MODEL)] + [_chunk_rows(tile, dn.shape[0]) for dn in dns] + [_rows(tile, D_MODEL), _full((1, D_MODEL))],
        [_rows(tile, D_MODEL), _full((1, D_MODEL))],
        [jax.ShapeDtypeStruct((S, D_MODEL), F32), jax.ShapeDtypeStruct((1, D_MODEL), F32)],
        [], 24, (dh_out, *dns, h, gain))


def _rope_tables(S):
    half = ROTARY_DIM // 2
    inv_freq = ROPE_THETA ** (-jnp.arange(0, ROTARY_DIM, 2, dtype=F32) / ROTARY_DIM)
    dim = jnp.arange(LANES) % HEAD_DIM
    ang = jnp.arange(S, dtype=F32)[:, None] * inv_freq[dim % half][None, :]
    lo, hi = (dim < half)[None, :], ((dim >= half) & (dim < ROTARY_DIM))[None, :]
    c = jnp.where(lo | hi, jnp.cos(ang), 1.0)
    s1 = jnp.where(lo, -jnp.sin(ang), 0.0)
    s2 = jnp.where(hi, jnp.sin(ang), 0.0)
    return c, s1, s2


def _pad_heads(w, axis):
    w = jnp.moveaxis(w, axis, 0)
    heads = w.reshape((N_HEADS, HEAD_DIM) + w.shape[1:])
    zero = jnp.zeros_like(heads)
    first = (jnp.arange(N_HEADS) < Q_PER_KV).reshape((N_HEADS, 1) + (1,) * (w.ndim - 1))
    lo = jnp.where(first, heads, zero)
    hi = jnp.where(first, zero, heads)
    padded = jnp.concatenate([lo, hi], axis=1).reshape((Q_PAD,) + w.shape[1:])
    return jnp.moveaxis(padded, 0, axis)


def _unpad_heads(w, axis):
    w = jnp.moveaxis(w, axis, 0)
    groups = w.reshape((N_HEADS, 2, HEAD_DIM) + w.shape[1:])
    first = (jnp.arange(N_HEADS) < Q_PER_KV).reshape((N_HEADS, 1) + (1,) * (w.ndim - 1))
    heads = jnp.where(first, groups[:, 0], groups[:, 1]).reshape((ATTN_WIDTH,) + w.shape[1:])
    return jnp.moveaxis(heads, 0, axis)


IN_ROWS = IN_WIDTH // N_CHIPS
OUT_ROWS = (ATTN_WIDTH + POOL_WIDTH) // N_CHIPS
MIX_ROWS = IN_ROWS + OUT_ROWS
FFN_ROWS = 3 * FF_CHUNK


def _step(x, target, bufs, small, place):
    S = x.shape[0]
    rc, rs1, rs2 = _rope_tables(S)
    (ffn1,) = _comm_call(_allgather(bufs[:1]), "allgather_ffn1")
    ffn1 = ffn1.reshape(N_CHIPS, FFN_ROWS, D_MODEL)
    h1, n1, gate1, up1, mix, ffn2 = _ffn_fwd(x, small["ffn1_norm"], ffn1, "ffn1_fwd", _allgather(bufs[1:]))
    mix, ffn2 = mix.reshape(N_CHIPS, MIX_ROWS, D_MODEL), ffn2.reshape(N_CHIPS, FFN_ROWS, D_MODEL)
    w_in_t = mix[:, :IN_ROWS].reshape(IN_WIDTH, D_MODEL)
    w_in_pad = jnp.concatenate([_pad_heads(w_in_t[:ATTN_WIDTH], 0), w_in_t[ATTN_WIDTH:]], axis=0)
    w_out = mix[:, IN_ROWS:].reshape(ATTN_WIDTH + POOL_WIDTH, D_MODEL)
    wa = _pad_heads(w_out[:ATTN_WIDTH], 0)
    wp = w_out[ATTN_WIDTH:]
    pool_w = small["pool_w"].astype(BF16)

    n2, q, k, v, pc = _mix_in(h1, small["mix_norm"], w_in_pad, rc, rs1, rs2, "mix_in")
    pool_m = _pool_matrices(S)
    a, p = _mix_core_fwd(q, k, v, pc, small["sink_logits"], pool_m, pool_w, small["pool_scale"], "mix_core_fwd")
    h2 = _mix_out(h1, a, p, wa, wp, "mix_out")
    h3, n3, gate2, up2 = _ffn_fwd(h2, small["ffn2_norm"], ffn2, "ffn2_fwd")
    dh3, dhalf3, loss_lanes, d_final = _loss_head(h3, target, small["final_norm"], "loss_head")

    dn3, d_ffn2 = _ffn_bwd(jnp.arange(N_CHIPS, dtype=jnp.int32), dhalf3, n3, gate2, up2, ffn2, "ffn2_bwd")
    dh2, da, dp, dwa, dwp, d_ffn2_norm, received = _mix_out_bwd(dh3, dn3, h2, small["ffn2_norm"], a, p, wa, wp, "mix_out_bwd",
                                                                _sibling_exchange([d_ffn2]))
    pair = _pair_sum(place, d_ffn2, received, "grad_pair_sum_ffn2")
    dq, dk, dv, dpc, dsink, dpool_w, dpool_scale, stack = _mix_core_bwd(
        q, k, v, pc, da, dp, small["sink_logits"], pool_m, pool_w, small["pool_scale"], rc, rs1, rs2, "mix_core_bwd",
        _scatter([pair]))
    reduced = _chip_sum(place, pair, stack, 2, "grad_chip_sum_ffn2")
    dh1, dhalf1, dw_in_pad, d_mix_norm, g_ffn2 = _mix_in_bwd(dh2, h1, small["mix_norm"], n2, dq, dk, dv, dpc, rc, rs1, rs2,
                                                             w_in_pad, "mix_in_bwd", _sibling_share([reduced]))
    dw_in_t = jnp.concatenate([_unpad_heads(dw_in_pad[:Q_PAD], 0), dw_in_pad[Q_PAD:]], axis=0)
    dw_out = jnp.concatenate([_unpad_heads(dwa, 0), dwp], axis=0)
    d_mix = jnp.concatenate([dw_in_t.reshape(N_CHIPS, IN_ROWS, D_MODEL), dw_out.reshape(N_CHIPS, OUT_ROWS, D_MODEL)], axis=1)
    d_mix = jnp.transpose(d_mix.reshape(N_CHIPS, 2, MIX_ROWS // 2, D_MODEL), (1, 0, 2, 3)).astype(BF16)
    small_g = {"ffn1_norm": jnp.zeros_like(d_mix_norm), "mix_norm": d_mix_norm, "ffn2_norm": d_ffn2_norm,
               "final_norm": d_final, "pool_scale": dpool_scale, "sink_logits": dsink[:, :N_HEADS], "pool_w": dpool_w}
    loss_row = jnp.sum(loss_lanes.reshape(D_MODEL // LANES, LANES), axis=0, keepdims=True)
    small_early = _pack_small(small_g, loss_row)

    chunk = [(place[1:] + 1 + p) % N_CHIPS for p in range(N_CHIPS)]
    ffn1_bwd = functools.partial(_ffn_bwd, d_out=dhalf1, n=n1, gate=gate1, up=up1, group=ffn1)
    stack = jnp.zeros((N_CHIPS, FFN_ROWS // 2, D_MODEL), BF16)
    dn_a, dw_a, recv_mix, small_all = ffn1_bwd(chunk[0], name="ffn1_bwd_0",
                                               rider=_merge(_sibling_exchange([d_mix]), _small_allgather(small_early)))
    pair_mix = _pair_sum(place, d_mix, recv_mix, "grad_pair_sum_mix")
    dn_b, dw_b, recv_a, stack_mix = ffn1_bwd(chunk[1], name="ffn1_bwd_1",
                                             rider=_merge(_sibling_exchange([dw_a]), _scatter([pair_mix])))
    pair_a = _pair_sum(place, dw_a, recv_a, "grad_pair_sum_ffn1_0")
    reduced_mix = _chip_sum(place, pair_mix, stack_mix, 2, "grad_chip_sum_mix")
    dn_c, dw_c, recv_b, stack, g_mix = ffn1_bwd(
        chunk[2], name="ffn1_bwd_2",
        rider=_merge(_merge(_sibling_exchange([dw_b]), _scatter_step(pair_a, stack, 0)), _sibling_share([reduced_mix])))
    pair_b = _pair_sum(place, dw_b, recv_b, "grad_pair_sum_ffn1_1")
    dn_d, dw_d, recv_c, stack = ffn1_bwd(chunk[3], name="ffn1_bwd_3",
                                         rider=_merge(_sibling_exchange([dw_c]), _scatter_step(pair_b, stack, 1)))
    pair_c = _pair_sum(place, dw_c, recv_c, "grad_pair_sum_ffn1_2")
    grad_x, d_ffn1_norm = _norm_bwd(dh1, [dn_a, dn_b, dn_c, dn_d], x, small["ffn1_norm"], "norm1_bwd")

    recv_d, stack, gains = _comm_call(
        _merge(_merge(_sibling_exchange([dw_d]), _scatter_step(pair_c, stack, 2)),
               _small_allgather(d_ffn1_norm.reshape(-1, LANES))), "grad_tail")
    pair_d = _pair_sum(place, dw_d, recv_d, "grad_pair_sum_ffn1_3")
    reduced_ffn1 = _chip_sum(place, pair_d, stack, 2, "grad_chip_sum_ffn1")
    (g_ffn1,) = _comm_call(_sibling_share([reduced_ffn1]), "grad_share_tail")
    gain_sum = _sum_leading(gains, 1, "gain_grad_sum")
    small_sum = jnp.concatenate([gain_sum, _sum_leading(small_all, 1, "small_grad_sum")[gain_sum.shape[0]:]], axis=0)
    return jnp.sum(small_sum[SMALL_ROWS - 1]), grad_x, [g.reshape(-1, D_MODEL) for g in (g_ffn1, g_mix, g_ffn2)], small_sum


GROUPS =(("ffn1_w_gate", "ffn1_w_up", "ffn1_w_down"), ("w_in", "w_out"), ("ffn2_w_gate", "ffn2_w_up", "ffn2_w_down"))
TRANSPOSED = ("ffn1_w_gate", "ffn1_w_up", "w_in", "ffn2_w_gate", "ffn2_w_up")


def _place():
    x, y, c = lax.axis_index("x"), lax.axis_index("y"), lax.axis_index("c")
    chips = [(1 - x, y), (x, 1 - y), (1 - x, 1 - y)]
    return x, y, c, chips


def _remote(src, dst, send_sem, recv_sem, to):
    return pltpu.make_async_remote_copy(src_ref=src, dst_ref=dst, send_sem=send_sem, recv_sem=recv_sem,
                                        device_id=to, device_id_type=MESH)


def _pack(chip, members, name):
    rows = members[0].shape[0]
    n = len(members)

    def body(chip_ref, *refs):
        ins, out_ref, buf, sems = refs[:n], refs[n], refs[n + 1], refs[n + 2]
        copies = [pltpu.make_async_copy(ins[k], buf.at[k], sems.at[k]) for k in range(n)]
        for cp in copies:
            cp.start()
        for k in range(n):
            copies[k].wait()
            out_ref[0, k * rows:(k + 1) * rows, :] = buf[k].astype(BF16)

    return pl.pallas_call(
        body, name=name,
        grid_spec=pltpu.PrefetchScalarGridSpec(
            num_scalar_prefetch=1, grid=(1,),
            in_specs=[HBM_SPEC] * n,
            out_specs=pl.BlockSpec((1, n * rows, D_MODEL), lambda k, chip_ref: (chip_ref[0], 0, 0)),
            scratch_shapes=[pltpu.VMEM((n, rows, D_MODEL), F32), pltpu.SemaphoreType.DMA((n,))]),
        out_shape=jax.ShapeDtypeStruct((N_CHIPS, n * rows, D_MODEL), BF16),
        compiler_params=_params(("arbitrary",), 24),
    )(chip, *members)


def _same(arrays):
    return [jax.ShapeDtypeStruct(a.shape, a.dtype) for a in arrays]


def _allgather(bufs):
    n = len(bufs)

    def copies(outs, send_sems, recv_sems, started_only=False):
        x, y, c, chips = _place()
        me = 2 * x + y
        slots = [2 * cx + cy for cx, cy in chips]
        first = [[_remote(outs[a].at[me, c], outs[a].at[me, c], send_sems.at[6 * a + k], recv_sems.at[6 * a + k], (*chips[k], c))
                  for k in range(3)] for a in range(n)]
        if started_only:
            return first
        passed = [[_remote(outs[a].at[slots[k], c], outs[a].at[slots[k], c], send_sems.at[6 * a + 3 + k],
                           recv_sems.at[6 * a + 3 + k], (x, y, 1 - c)) for k in range(3)] for a in range(n)]
        landed = [[_remote(outs[a].at[me, c], outs[a].at[slots[k], c], send_sems.at[6 * a + k], recv_sems.at[6 * a + k],
                           (*chips[k], c)) for k in range(3)] for a in range(n)]
        handed = [[_remote(outs[a].at[me, c], outs[a].at[slots[k], 1 - c], send_sems.at[6 * a + 3 + k],
                           recv_sems.at[6 * a + 3 + k], (x, y, 1 - c)) for k in range(3)] for a in range(n)]
        return first, passed, landed, handed

    def start(ins, outs, sems):
        for per_buf in copies(outs, *sems, started_only=True):
            for cp in per_buf:
                cp.start()

    def finish(ins, outs, sems):
        first, passed, landed, handed = copies(outs, *sems)
        for a in range(n):
            for k in range(3):
                landed[a][k].wait_recv()
                passed[a][k].start()
        for a in range(n):
            for k in range(3):
                handed[a][k].wait_recv()
        for a in range(n):
            for cp in first[a] + passed[a]:
                cp.wait_send()

    return _Rider(list(bufs), _same(bufs), {a: a for a in range(n)},
                  [pltpu.SemaphoreType.DMA((6 * n,)), pltpu.SemaphoreType.DMA((6 * n,))], start, finish)


def _sibling_exchange(parts):
    n = len(parts)

    def copies(ins, outs, send_sems, recv_sems):
        x, y, c, _ = _place()
        return [_remote(ins[a].at[1 - c], outs[a], send_sems.at[a], recv_sems.at[a], (x, y, 1 - c)) for a in range(n)]

    def start(ins, outs, sems):
        for cp in copies(ins, outs, *sems):
            cp.start()

    def finish(ins, outs, sems):
        for cp in copies(ins, outs, *sems):
            cp.wait_recv()
            cp.wait_send()

    return _Rider(list(parts), [jax.ShapeDtypeStruct(p.shape[1:], p.dtype) for p in parts], {},
                  [pltpu.SemaphoreType.DMA((n,)), pltpu.SemaphoreType.DMA((n,))], start, finish)


def _small_allgather(small):
    flips = [(fx, fy, fc) for fx in range(2) for fy in range(2) for fc in range(2)][1:]

    def copies(small_ref, gather_ref, send_sems, recv_sems, local_sem, started_only=False):
        x, y, c, _ = _place()
        me = 4 * x + 2 * y + c
        peers = [((1 - x) if fx else x, (1 - y) if fy else y, (1 - c) if fc else c) for fx, fy, fc in flips]
        own = pltpu.make_async_copy(small_ref, gather_ref.at[me], local_sem)
        sent = [_remote(small_ref, gather_ref.at[me], send_sems.at[k], recv_sems.at[k], peer) for k, peer in enumerate(peers)]
        if started_only:
            return own, sent
        landed = [_remote(small_ref, gather_ref.at[4 * px + 2 * py + pc], send_sems.at[k], recv_sems.at[k], (px, py, pc))
                  for k, (px, py, pc) in enumerate(peers)]
        return own, sent, landed

    def start(ins, outs, sems):
        own, sent = copies(ins[0], outs[0], *sems, started_only=True)
        own.start()
        for cp in sent:
            cp.start()

    def finish(ins, outs, sems):
        own, sent, landed = copies(ins[0], outs[0], *sems)
        for cp in landed:
            cp.wait_recv()
        for cp in sent:
            cp.wait_send()
        own.wait()

    return _Rider([small], [jax.ShapeDtypeStruct((2 * N_CHIPS,) + small.shape, small.dtype)], {},
                  [pltpu.SemaphoreType.DMA((7,)), pltpu.SemaphoreType.DMA((7,)), pltpu.SemaphoreType.DMA], start, finish)


def _merge(a, b):
    na, nao, nas = len(a.operands), len(a.out_shapes), len(a.scratch)

    def start(ins, outs, sems):
        a.start(ins[:na], outs[:nao], sems[:nas])
        b.start(ins[na:], outs[nao:], sems[nas:])

    def finish(ins, outs, sems):
        a.finish(ins[:na], outs[:nao], sems[:nas])
        b.finish(ins[na:], outs[nao:], sems[nas:])

    aliases = {**a.aliases, **{na + k: nao + v for k, v in b.aliases.items()}}
    return _Rider(a.operands + b.operands, a.out_shapes + b.out_shapes, aliases, a.scratch + b.scratch, start, finish)


def _scatter_step(pair, stack, step):
    def copies(pair_ref, stack_ref, send_sem, recv_sem, started_only=False):
        x, y, c, _ = _place()
        me = 2 * x + y
        to = (me + 1 + step) % N_CHIPS
        frm = (me + N_CHIPS - 1 - step) % N_CHIPS
        sent = _remote(pair_ref.at[0], stack_ref.at[me], send_sem, recv_sem, (to // 2, to % 2, c))
        if started_only:
            return sent
        landed = _remote(pair_ref.at[0], stack_ref.at[frm], send_sem, recv_sem, (frm // 2, frm % 2, c))
        return sent, landed

    def start(ins, outs, sems):
        copies(ins[0], outs[0], *sems, started_only=True).start()

    def finish(ins, outs, sems):
        sent, landed = copies(ins[0], outs[0], *sems)
        landed.wait_recv()
        sent.wait_send()

    return _Rider([pair, stack], _same([stack]), {1: 0}, [pltpu.SemaphoreType.DMA, pltpu.SemaphoreType.DMA], start, finish)


def _scatter(sums):
    n = len(sums)

    def copies(ins, outs, send_sems, recv_sems, started_only=False):
        x, y, c, chips = _place()
        me = 2 * x + y
        slots = [2 * cx + cy for cx, cy in chips]
        sent = [_remote(ins[a].at[slots[k]], outs[a].at[me], send_sems.at[3 * a + k], recv_sems.at[3 * a + k], (*chips[k], c))
                for a in range(n) for k in range(3)]
        if started_only:
            return sent
        landed = [_remote(ins[a].at[slots[k]], outs[a].at[slots[k]], send_sems.at[3 * a + k], recv_sems.at[3 * a + k],
                          (*chips[k], c)) for a in range(n) for k in range(3)]
        return sent, landed

    def start(ins, outs, sems):
        for cp in copies(ins, outs, *sems, started_only=True):
            cp.start()

    def finish(ins, outs, sems):
        sent, landed = copies(ins, outs, *sems)
        for cp in landed:
            cp.wait_recv()
        for cp in sent:
            cp.wait_send()

    return _Rider(list(sums), _same(sums), {}, [pltpu.SemaphoreType.DMA((3 * n,)), pltpu.SemaphoreType.DMA((3 * n,))],
                  start, finish)


def _sibling_share(bufs):
    n = len(bufs)

    def copies(outs, send_sems, recv_sems, started_only=False):
        x, y, c, _ = _place()
        sent = [_remote(outs[a].at[c], outs[a].at[c], send_sems.at[a], recv_sems.at[a], (x, y, 1 - c)) for a in range(n)]
        if started_only:
            return sent
        landed = [_remote(outs[a].at[c], outs[a].at[1 - c], send_sems.at[a], recv_sems.at[a], (x, y, 1 - c)) for a in range(n)]
        return sent, landed

    def start(ins, outs, sems):
        for cp in copies(outs, *sems, started_only=True):
            cp.start()

    def finish(ins, outs, sems):
        sent, landed = copies(outs, *sems)
        for cp in landed:
            cp.wait_recv()
        for cp in sent:
            cp.wait_send()

    return _Rider(list(bufs), _same(bufs), {a: a for a in range(n)},
                  [pltpu.SemaphoreType.DMA((n,)), pltpu.SemaphoreType.DMA((n,))], start, finish)


def _pair_sum(core, part, received, name):
    _, k, rh, cols = part.shape

    def body(core_ref, p_ref, r_ref, o_ref):
        o_ref[...] = (p_ref[0].astype(F32) + r_ref[...].astype(F32)).astype(BF16)

    return pl.pallas_call(
        body, name=name,
        grid_spec=pltpu.PrefetchScalarGridSpec(
            num_scalar_prefetch=1, grid=(k,),
            in_specs=[pl.BlockSpec((1, 1, rh, cols), lambda j, core_ref: (core_ref[0], j, 0, 0)),
                      pl.BlockSpec((1, rh, cols), lambda j, core_ref: (j, 0, 0))],
            out_specs=pl.BlockSpec((1, rh, cols), lambda j, core_ref: (j, 0, 0))),
        out_shape=jax.ShapeDtypeStruct((k, rh, cols), BF16),
        compiler_params=_params(("parallel",), 16),
    )(core, part, received)


def _sum_leading(stack, steps, name):
    k, rows, cols = stack.shape
    tile = rows // steps

    def body(s_ref, o_ref):
        total = s_ref[0].astype(F32)
        for d in range(1, k):
            total = total + s_ref[d].astype(F32)
        o_ref[...] = total

    return pl.pallas_call(
        body, name=name, grid=(steps,),
        in_specs=[pl.BlockSpec((k, tile, cols), lambda i: (0, i, 0))],
        out_specs=pl.BlockSpec((tile, cols), lambda i: (i, 0)),
        out_shape=jax.ShapeDtypeStruct((rows, cols), F32),
        compiler_params=_params(("parallel",), 16),
    )(stack)


def _chip_sum(place, own, stack, steps, name):
    k, rows, cols = stack.shape
    tile = rows // steps

    def body(place_ref, own_ref, *refs):
        chip = place_ref[1]
        total = None
        for d in range(k):
            term = jnp.where(chip == d, own_ref[0], refs[d][0]).astype(F32)
            total = term if total is None else total + term
        refs[k][0] = total

    def other(d):
        return lambda i, place_ref: (jnp.where(place_ref[1] == d, (d + 1) % k, d), i, 0)

    return pl.pallas_call(
        body, name=name,
        grid_spec=pltpu.PrefetchScalarGridSpec(
            num_scalar_prefetch=1, grid=(steps,),
            in_specs=[pl.BlockSpec((1, tile, cols), lambda i, place_ref: (place_ref[1] % own.shape[0], i, 0))]
            + [pl.BlockSpec((1, tile, cols), other(d)) for d in range(k)],
            out_specs=pl.BlockSpec((1, tile, cols), lambda i, place_ref: (place_ref[0], i, 0))),
        out_shape=jax.ShapeDtypeStruct((2, rows, cols), F32),
        compiler_params=_params(("arbitrary",), 16),
    )(place, own, *([stack] * k))


def _adamw(w, g, row0, m, v, tile, name):
    rows, cols = w.shape
    first = row0 // tile
    assert rows % tile == 0 and row0 % tile == 0
    bc1 = 1.0 - ADAM_B1 ** ADAM_STEP
    bc2 = 1.0 - ADAM_B2 ** ADAM_STEP

    def body(w_ref, g_ref, m_ref, v_ref, go_ref, d_ref, mo_ref, vo_ref):
        g = g_ref[...]
        m_new = ADAM_B1 * m_ref[...] + (1.0 - ADAM_B1) * g
        v_new = ADAM_B2 * v_ref[...] + (1.0 - ADAM_B2) * (g * g)
        go_ref[...] = g
        d_ref[...] = -ADAM_LR * ((m_new / bc1) / (jnp.sqrt(v_new / bc2) + ADAM_EPS) + ADAM_WD * w_ref[...])
        mo_ref[...] = m_new
        vo_ref[...] = v_new

    spec = pl.BlockSpec((tile, cols), lambda i: (i, 0))
    g_spec = pl.BlockSpec((tile, cols), lambda i: (first + i, 0))
    return pl.pallas_call(
        body, name=name, grid=(rows // tile,),
        in_specs=[spec, g_spec, spec, spec], out_specs=[spec] * 4,
        out_shape=[jax.ShapeDtypeStruct((rows, cols), F32)] * 4,
        compiler_params=_params(("parallel",), 16),
    )(w, g, m, v)


SMALL = ("ffn1_norm", "mix_norm", "ffn2_norm", "final_norm", "pool_scale", "sink_logits", "pool_w")


def _pack_small(d, last_row=None):
    sink = jnp.pad(d["sink_logits"].reshape(1, N_HEADS), ((0, 0), (0, LANES - N_HEADS)))
    rows = [d[n].reshape(-1, LANES) for n in SMALL[:5]] + [sink, d["pool_w"].reshape(-1, LANES)]
    used = sum(r.shape[0] for r in rows)
    last = jnp.zeros((1, LANES), F32) if last_row is None else last_row
    return jnp.concatenate(rows + [jnp.zeros((SMALL_ROWS - used - 1, LANES), F32), last], axis=0)


def _unpack_small(packed, like):
    out, row = {}, 0
    for n in SMALL:
        size = LANES if n == "sink_logits" else math.prod(like[n].shape)
        chunk = packed[row:row + size // LANES].reshape(-1)
        out[n] = (chunk[:N_HEADS] if n == "sink_logits" else chunk).reshape(like[n].shape)
        row += size // LANES
    return out


def kernel(x, ffn1_norm, ffn1_w_gate, ffn1_w_up, ffn1_w_down, mix_norm, w_in, sink_logits, pool_w, pool_scale, w_out, ffn2_norm, ffn2_w_gate, ffn2_w_up, ffn2_w_down, final_norm, loss_target, m_ffn1_norm, m_ffn1_w_gate, m_ffn1_w_up, m_ffn1_w_down, m_mix_norm, m_w_in, m_sink_logits, m_pool_w, m_pool_scale, m_w_out, m_ffn2_norm, m_ffn2_w_gate, m_ffn2_w_up, m_ffn2_w_down, m_final_norm, v_ffn1_norm, v_ffn1_w_gate, v_ffn1_w_up, v_ffn1_w_down, v_mix_norm, v_w_in, v_sink_logits, v_pool_w, v_pool_scale, v_w_out, v_ffn2_norm, v_ffn2_w_gate, v_ffn2_w_up, v_ffn2_w_down, v_final_norm):
    names = ("ffn1_norm", "ffn1_w_gate", "ffn1_w_up", "ffn1_w_down", "mix_norm", "w_in", "sink_logits", "pool_w",
             "pool_scale", "w_out", "ffn2_norm", "ffn2_w_gate", "ffn2_w_up", "ffn2_w_down", "final_norm")
    weights = dict(zip(names, (ffn1_norm, ffn1_w_gate, ffn1_w_up, ffn1_w_down, mix_norm, w_in, sink_logits, pool_w,
                               pool_scale, w_out, ffn2_norm, ffn2_w_gate, ffn2_w_up, ffn2_w_down, final_norm)))
    mom1 = dict(zip(names, (m_ffn1_norm, m_ffn1_w_gate, m_ffn1_w_up, m_ffn1_w_down, m_mix_norm, m_w_in, m_sink_logits,
                            m_pool_w, m_pool_scale, m_w_out, m_ffn2_norm, m_ffn2_w_gate, m_ffn2_w_up, m_ffn2_w_down,
                            m_final_norm)))
    mom2 = dict(zip(names, (v_ffn1_norm, v_ffn1_w_gate, v_ffn1_w_up, v_ffn1_w_down, v_mix_norm, v_w_in, v_sink_logits,
                            v_pool_w, v_pool_scale, v_w_out, v_ffn2_norm, v_ffn2_w_gate, v_ffn2_w_up, v_ffn2_w_down,
                            v_final_norm)))
    chip = (2 * lax.axis_index("x") + lax.axis_index("y")).astype(jnp.int32).reshape(1)
    place = jnp.concatenate([lax.axis_index("c").astype(jnp.int32).reshape(1), chip])

    def rows_of(t, n):
        return jnp.swapaxes(t[n][0], 0, 1) if n in TRANSPOSED else t[n][0]

    packed = [_pack(chip, [rows_of(weights, n) for n in GROUPS[0]], "pack_ffn1"),
              _pack(chip, [jnp.concatenate([rows_of(weights, n) for n in GROUPS[1]], axis=0)], "pack_mix"),
              _pack(chip, [rows_of(weights, n) for n in GROUPS[2]], "pack_ffn2")]
    bufs = [p.reshape(N_CHIPS, 2, p.shape[1] // 2, D_MODEL) for p in packed]

    small_w = {"ffn1_norm": ffn1_norm, "mix_norm": mix_norm, "ffn2_norm": ffn2_norm,
               "final_norm": final_norm.reshape(1, D_MODEL), "pool_scale": pool_scale, "sink_logits": sink_logits,
               "pool_w": pool_w[0]}
    loss, grad_x, group_grads, small_sum = _step(x[0], loss_target[0], bufs, small_w, place)

    out_g, out_d, out_m, out_v = {}, {}, {}, {}
    for members, g in zip(GROUPS, group_grads):
        row0 = 0
        for n in members:
            w = rows_of(weights, n)
            tile = FF_CHUNK // 4 if w.shape[0] == FF_CHUNK else math.gcd(IN_ROWS, OUT_ROWS)
            outs = _adamw(w, g, row0, rows_of(mom1, n), rows_of(mom2, n), tile, "adamw_" + n)
            row0 += w.shape[0]
            for dst, t in zip((out_g, out_d, out_m, out_v), outs):
                dst[n] = (jnp.swapaxes(t, 0, 1) if n in TRANSPOSED else t).reshape(weights[n].shape)
    small_outs = _adamw(_pack_small(weights), small_sum, 0, _pack_small(mom1), _pack_small(mom2), SMALL_ROWS, "adamw_small")
    for dst, packed in zip((out_g, out_d, out_m, out_v), small_outs):
        dst.update(_unpack_small(packed, weights))

    return (loss,grad_x.reshape(x.shape), *[out_g[n] for n in names], *[out_d[n] for n in names],
            *[out_m[n] for n in names], *[out_v[n] for n in names])
```

```python
import collections
import functools
import math

import jax
import jax.numpy as jnp
from jax import lax
from jax.experimental import pallas as pl
from jax.experimental.pallas import tpu as pltpu

F32, BF16 = jnp.float32, jnp.bfloat16
MESH = pl.DeviceIdType.MESH

D_MODEL = 1024
D_FF = 2816
N_CHIPS = 4
FF_CHUNK = D_FF // N_CHIPS
HEAD_DIM = 64
N_HEADS = 8
N_KV = 2
Q_PER_KV = N_HEADS // N_KV
KV_WIDTH = N_KV * HEAD_DIM
ATTN_WIDTH = N_HEADS * HEAD_DIM
POOL_WINDOWS = (2, 4, 8, 16)
N_POOL = len(POOL_WINDOWS)
POOL_GROUP = 128
POOL_WIDTH = N_POOL * POOL_GROUP
IN_WIDTH = ATTN_WIDTH + 2 * KV_WIDTH + POOL_WIDTH
WINDOW = 128
BLOCK = 128
BAND = 3 * BLOCK
ROPE_THETA = 500000.0
ROTARY_DIM = HEAD_DIM // 4
EPS = 1e-6
LANES = 128
Q_PAD = N_HEADS * LANES
U_PAD = Q_PAD + 2 * KV_WIDTH + POOL_WIDTH
SCALE = HEAD_DIM ** -0.5
NEG = -1e30

ADAM_LR, ADAM_B1, ADAM_B2, ADAM_EPS, ADAM_WD, ADAM_STEP = 0.001, 0.9, 0.999, 1e-08, 0.01, 10

V7X_VMEM_BYTES = 64 * 1024 * 1024
TOK_TILE = 512
FFN_FWD_TILE = 1024
SMALL_ROWS = 552


def _params(sem, vmem_mb):
    assert vmem_mb * 1024 * 1024 <= V7X_VMEM_BYTES
    return pltpu.CompilerParams(dimension_semantics=sem, vmem_limit_bytes=vmem_mb * 1024 * 1024)


def _dot(a, b):
    return lax.dot_general(a, b, (((1,), (0,)), ((), ())), preferred_element_type=F32)


def _dot_nt(a, b):
    return lax.dot_general(a, b, (((1,), (1,)), ((), ())), preferred_element_type=F32)


def _dot_tn(a, b):
    return lax.dot_general(a, b, (((0,), (0,)), ((), ())), preferred_element_type=F32)


def _rms_stats(h):
    r = lax.rsqrt(jnp.mean(h * h, axis=-1, keepdims=True) + EPS)
    return r, h * r


def _rms_bwd(dn, g, r, xh):
    gd = dn * g
    dh = r * (gd - xh * jnp.mean(gd * xh, axis=-1, keepdims=True))
    return dh, jnp.sum(dn * xh, axis=0, keepdims=True)


def _rope(x, c, s1, s2):
    return x * c + pltpu.roll(x, LANES - ROTARY_DIM // 2, 1) * s1 + pltpu.roll(x, ROTARY_DIM // 2, 1) * s2


def _rope_bwd(d, c, s1, s2):
    return d * c + pltpu.roll(d * s1, ROTARY_DIM // 2, 1) + pltpu.roll(d * s2, LANES - ROTARY_DIM // 2, 1)


def _sum_chunks(refs):
    terms = [ref[j].astype(F32) for ref in refs for j in range(ref.shape[0])]
    return functools.reduce(lambda a, b: a + b, terms)


def _chunk_rows(tile, k):
    return pl.BlockSpec((k, tile, D_MODEL), lambda i, *_: (0, i, 0))


def _full(shape):
    nd = len(shape)
    return pl.BlockSpec(shape, lambda *_: (0,) * nd)


def _rows(tile, cols):
    return pl.BlockSpec((tile, cols), lambda i, *_: (i, 0))


HBM_SPEC = pl.BlockSpec(memory_space=pltpu.HBM)

_Rider = collections.namedtuple("_Rider", "operands out_shapes aliases scratch start finish")


_NO_RIDER = _Rider([], [], {}, [], None, None)


def _call(body, name, grid, in_specs, out_specs, out_shape, scratch, vmem_mb, args, rider=None, prefetch=()):
    rider = rider or _NO_RIDER
    n_pre, n_in, n_out, n_scr = len(prefetch), len(in_specs), len(out_specs), len(scratch)
    r_in, r_out = len(rider.operands), len(rider.out_shapes)

    def fused(*refs):
        pre, refs = refs[:n_pre], refs[n_pre:]
        ins, refs = refs[:n_in], refs[n_in:]
        r_ins, refs = refs[:r_in], refs[r_in:]
        outs, refs = refs[:n_out], refs[n_out:]
        r_outs, refs = refs[:r_out], refs[r_out:]
        scr, r_scr = refs[:n_scr], refs[n_scr:]
        ids = [pl.program_id(d) for d in range(len(grid))]
        if rider.start is not None:
            @pl.when(functools.reduce(jnp.logical_and, [i == 0 for i in ids]))
            def _():
                rider.start(r_ins, r_outs, r_scr)

        body(*pre, *ins, *outs, *scr)

        if rider.finish is not None:
            @pl.when(functools.reduce(jnp.logical_and, [i == g - 1 for i, g in zip(ids, grid)]))
            def _():
                rider.finish(r_ins, r_outs, r_scr)

    return pl.pallas_call(
        fused, name=name,
        grid_spec=pltpu.PrefetchScalarGridSpec(
            num_scalar_prefetch=n_pre, grid=grid,
            in_specs=list(in_specs) + [HBM_SPEC] * r_in, out_specs=list(out_specs) + [HBM_SPEC] * r_out,
            scratch_shapes=list(scratch) + list(rider.scratch)),
        out_shape=list(out_shape) + list(rider.out_shapes),
        input_output_aliases={n_pre + n_in + k: n_out + v for k, v in rider.aliases.items()},
        compiler_params=_params(("arbitrary",) * len(grid), vmem_mb),
    )(*prefetch, *args, *rider.operands)


def _comm_call(rider, name):
    r_in, r_out = len(rider.operands), len(rider.out_shapes)

    def body(*refs):
        r_ins, r_outs, r_scr = refs[:r_in], refs[r_in:r_in + r_out], refs[r_in + r_out:]
        rider.start(r_ins, r_outs, r_scr)
        rider.finish(r_ins, r_outs, r_scr)

    return pl.pallas_call(
        body, name=name, in_specs=[HBM_SPEC] * r_in, out_specs=[HBM_SPEC] * r_out, out_shape=list(rider.out_shapes),
        input_output_aliases=dict(rider.aliases), scratch_shapes=list(rider.scratch),
    )(*rider.operands)


def _ffn_fwd(h, gain, group, name, rider=None):
    S = h.shape[0]
    tile = min(FFN_FWD_TILE, S)
    nt = S // tile

    def body(h_ref, g_ref, wg_ref, wu_ref, wd_ref, ho_ref, n_ref, gate_ref, up_ref, acc):
        j = pl.program_id(1)

        @pl.when(j == 0)
        def _():
            _, xh = _rms_stats(h_ref[...])
            n_ref[...] = (xh * g_ref[...]).astype(BF16)
            acc[...] = jnp.zeros_like(acc)

        halves = [pl.ds(s * (tile // 2), tile // 2) for s in range(2)]
        ns = [n_ref[rows, :] for rows in halves]
        gates = [_dot_nt(n, wg_ref[0]) for n in ns]
        ups = [_dot_nt(n, wu_ref[0]) for n in ns]
        acts = [(g * jax.nn.sigmoid(g) * u).astype(BF16) for g, u in zip(gates, ups)]
        for rows, g, u, act in zip(halves, gates, ups, acts):
            gate_ref[0, rows, :] = g.astype(BF16)
            up_ref[0, rows, :] = u.astype(BF16)
            acc[rows, :] += _dot(act, wd_ref[0])

        @pl.when(j == N_CHIPS - 1)
        def _():
            ho_ref[...] = h_ref[...] + 0.5 * acc[...]

    tok = pl.BlockSpec((tile, D_MODEL), lambda i, j: (i, 0))
    hid = pl.BlockSpec((1, tile, FF_CHUNK), lambda i, j: (j, i, 0))
    return _call(
        body, name, (nt, N_CHIPS),
        [tok, pl.BlockSpec((1, D_MODEL), lambda i, j: (0, 0))]
        + [pl.BlockSpec((1, FF_CHUNK, D_MODEL), functools.partial(lambda i, j, part: (j, part, 0), part=part))
           for part in range(3)],
        [tok, tok, hid, hid],
        [jax.ShapeDtypeStruct((S, D_MODEL), F32), jax.ShapeDtypeStruct((S, D_MODEL), BF16),
         jax.ShapeDtypeStruct((N_CHIPS, S, FF_CHUNK), BF16), jax.ShapeDtypeStruct((N_CHIPS, S, FF_CHUNK), BF16)],
        [pltpu.VMEM((tile, D_MODEL), F32)], 56, (h, gain, group, group, group), rider)


def _ffn_bwd(chunks, d_out, n, gate, up, group, name, rider=None):
    S = n.shape[0]
    n_chunks = chunks.shape[0]
    tile = min(TOK_TILE, S)
    nt = S // tile
    half_rows = 3 * FF_CHUNK // 2
    cut = FF_CHUNK // 2

    def body(chunks_ref, do_ref, n_ref, gate_ref, up_ref, wg_ref, wu_ref, wd_ref, dn_ref, dw_ref, acc_g, acc_u, acc_d):
        j, i = pl.program_id(0), pl.program_id(1)

        @pl.when(i == 0)
        def _():
            acc_g[...] = jnp.zeros_like(acc_g)
            acc_u[...] = jnp.zeros_like(acc_u)
            acc_d[...] = jnp.zeros_like(acc_d)

        do = do_ref[...]
        nn = n_ref[...]
        g = gate_ref[0].astype(F32)
        u = up_ref[0].astype(F32)
        d_act = _dot_nt(do, wd_ref[0])
        sig = jax.nn.sigmoid(g)
        silu = g * sig
        d_up = (d_act * silu).astype(BF16)
        d_gate = (d_act * u * (sig * (1.0 + g * (1.0 - sig)))).astype(BF16)
        act = (silu * u).astype(BF16)
        dn_ref[0] = (_dot(d_gate, wg_ref[0]) + _dot(d_up, wu_ref[0])).astype(BF16)
        acc_g[...] += _dot_tn(d_gate, nn)
        acc_u[...] += _dot_tn(d_up, nn)
        acc_d[...] += _dot_tn(act, do)

        @pl.when(i == nt - 1)
        def _():
            dw_ref[0, 0, :FF_CHUNK, :] = acc_g[...].astype(BF16)
            dw_ref[0, 0, FF_CHUNK:, :] = acc_u[:cut, :].astype(BF16)
            dw_ref[1, 0, :cut, :] = acc_u[cut:, :].astype(BF16)
            dw_ref[1, 0, cut:, :] = acc_d[...].astype(BF16)

    tok = pl.BlockSpec((tile, D_MODEL), lambda j, i, chunks_ref: (i, 0))
    hid = pl.BlockSpec((1, tile, FF_CHUNK), lambda j, i, chunks_ref: (chunks_ref[j], i, 0))
    return _call(
        body, name, (n_chunks, nt),
        [tok, tok, hid, hid]
        + [pl.BlockSpec((1, FF_CHUNK, D_MODEL), functools.partial(lambda j, i, chunks_ref, part: (chunks_ref[j], part, 0), part=part))
           for part in range(3)],
        [pl.BlockSpec((1, tile, D_MODEL), lambda j, i, chunks_ref: (j, i, 0)),
         pl.BlockSpec((2, 1, half_rows, D_MODEL), lambda j, i, chunks_ref: (0, j, 0, 0))],
        [jax.ShapeDtypeStruct((n_chunks, S, D_MODEL), BF16), jax.ShapeDtypeStruct((2, n_chunks, half_rows, D_MODEL), BF16)],
        [pltpu.VMEM((FF_CHUNK, D_MODEL), F32)] * 3, 56, (d_out, n, gate, up, group, group, group), rider, (chunks,))


def _mix_in(h, gain, w_in, rc, rs1, rs2, name):
    S = h.shape[0]
    tile = min(TOK_TILE, S)

    def body(h_ref, g_ref, w_ref, c_ref, s1_ref, s2_ref, n_ref, q_ref, k_ref, v_ref, pc_ref):
        _, xh = _rms_stats(h_ref[...])
        n = (xh * g_ref[...]).astype(BF16)
        n_ref[...] = n
        u = _dot_nt(n, w_ref[...])
        c, s1, s2 = c_ref[...], s1_ref[...], s2_ref[...]
        q_ref[...] = jnp.concatenate([(_rope(u[:, hd * LANES:(hd + 1) * LANES], c, s1, s2) * SCALE).astype(BF16)
                                      for hd in range(N_HEADS)], axis=1)
        k_ref[...] = _rope(u[:, Q_PAD:Q_PAD + KV_WIDTH], c, s1, s2).astype(BF16)
        v_ref[...] = u[:, Q_PAD + KV_WIDTH:Q_PAD + 2 * KV_WIDTH].astype(BF16)
        pc_ref[...] = u[:, Q_PAD + 2 * KV_WIDTH:]

    return pl.pallas_call(
        body, name=name, grid=(S // tile,),
        in_specs=[_rows(tile, D_MODEL), _full((1, D_MODEL)), _full((U_PAD, D_MODEL)),
                  _rows(tile, LANES), _rows(tile, LANES), _rows(tile, LANES)],
        out_specs=[_rows(tile, D_MODEL), _rows(tile, Q_PAD), _rows(tile, KV_WIDTH), _rows(tile, KV_WIDTH),
                   _rows(tile, POOL_WIDTH)],
        out_shape=[jax.ShapeDtypeStruct((S, D_MODEL), BF16), jax.ShapeDtypeStruct((S, Q_PAD), BF16),
                   jax.ShapeDtypeStruct((S, KV_WIDTH), BF16), jax.ShapeDtypeStruct((S, KV_WIDTH), BF16),
                   jax.ShapeDtypeStruct((S, POOL_WIDTH), F32)],
        compiler_params=_params(("parallel",), 40),
    )(h, gain, w_in, rc, rs1, rs2)


def _band_start(i, S):
    return pl.multiple_of(jnp.clip((i - 1) * BLOCK, 0, S - BAND), BLOCK)


def _window_bias(off):
    r = lax.broadcasted_iota(jnp.int32, (BLOCK, 1), 0)
    c = lax.broadcasted_iota(jnp.int32, (1, BAND), 1)
    return jnp.where(jnp.abs(off + r - c) <= WINDOW, 0.0, NEG).astype(F32)


def _softmax_parts(qh, kb, bias, sink_h):
    s = _dot_nt(qh, kb) + bias
    m = jnp.maximum(jnp.max(s, axis=-1, keepdims=True), sink_h)
    p = jnp.exp(s - m)
    es = jnp.exp(sink_h - m)
    return p, es, 1.0 / (jnp.sum(p, axis=-1, keepdims=True) + es)


def _pool_matrix(t0, start, S, w):
    r = lax.broadcasted_iota(jnp.int32, (BLOCK, 1), 0) + t0
    c = lax.broadcasted_iota(jnp.int32, (1, BAND), 1) + start
    half = w // 2

    def window(lo, hi):
        a = jnp.maximum(lo, 0)
        b = jnp.minimum(hi + 1, S)
        return jnp.where((c >= a) & (c < b), 1.0 / (b - a).astype(F32), 0.0)

    return (0.5 * (window(r - half, r + half - 1) + window(r - half + 1, r + half))).astype(BF16)


def _pool_matrices(S):
    blocks = ((0, 0), (BLOCK, 0), (S - BLOCK, S - BAND))
    return jnp.stack([jnp.stack([_pool_matrix(t0, start, S, w) for w in POOL_WINDOWS]) for t0, start in blocks])


def _pool_spec(nb):
    return pl.BlockSpec((1, N_POOL, BLOCK, BAND), lambda i, *_: (jnp.where(i == 0, 0, jnp.where(i == nb - 1, 2, 1)), 0, 0, 0))


def _mix_core_fwd(q, k, v, pc, sink, pool_m, pool_w, pool_scale, name):
    S = q.shape[0]
    nb = S // BLOCK

    def body(sink_ref, q_ref, k_ref, v_ref, pc_ref, pm_ref, pw_ref, ps_ref, a_ref, p_ref):
        i = pl.program_id(0)
        start = _band_start(i, S)
        band = pl.ds(start, BAND)
        bias = _window_bias(i * BLOCK - start)
        kb, vb = k_ref[band, :], v_ref[band, :]
        hs = range(N_HEADS)
        ss = [_dot_nt(q_ref[:, hd * LANES:(hd + 1) * LANES], kb) + bias for hd in hs]
        ms = [jnp.maximum(jnp.max(ss[hd], axis=-1, keepdims=True), sink_ref[0, hd]) for hd in hs]
        ps = [jnp.exp(ss[hd] - ms[hd]) for hd in hs]
        invs = [1.0 / (jnp.sum(ps[hd], axis=-1, keepdims=True) + jnp.exp(sink_ref[0, hd] - ms[hd])) for hd in hs]
        outs = [_dot(ps[hd].astype(BF16), vb) for hd in hs]
        a_ref[...] = jnp.concatenate([(outs[hd] * invs[hd]).astype(BF16) for hd in hs], axis=1)
        centre = pl.ds(pl.multiple_of(i * BLOCK, BLOCK), BLOCK)
        gs = range(N_POOL)
        sl = [slice(g * POOL_GROUP, (g + 1) * POOL_GROUP) for g in gs]
        means = [_dot(pm_ref[0, g], pc_ref[band, sl[g]].astype(BF16)) for g in gs]
        devs = [(means[g] - pc_ref[centre, sl[g]]).astype(BF16) for g in gs]
        p_ref[...] = (jnp.concatenate([_dot(devs[g], pw_ref[g]) for g in gs], axis=1) * ps_ref[...]).astype(BF16)

    return pl.pallas_call(
        body, name=name, grid=(nb,),
        in_specs=[pl.BlockSpec(memory_space=pltpu.SMEM), _rows(BLOCK, Q_PAD), _full((S, KV_WIDTH)), _full((S, KV_WIDTH)),
                  _full((S, POOL_WIDTH)), _pool_spec(nb), _full((N_POOL, POOL_GROUP, POOL_GROUP)), _full((1, POOL_WIDTH))],
        out_specs=[_rows(BLOCK, Q_PAD), _rows(BLOCK, POOL_WIDTH)],
        out_shape=[jax.ShapeDtypeStruct((S, Q_PAD), BF16), jax.ShapeDtypeStruct((S, POOL_WIDTH), BF16)],
        compiler_params=_params(("parallel",), 40),
    )(sink, q, k, v, pc, pool_m, pool_w, pool_scale)


def _mix_core_bwd(q, k, v, pc, da, dp, sink, pool_m, pool_w, pool_scale, rc, rs1, rs2, name, rider=None):
    S = q.shape[0]
    nb = S // BLOCK

    def body(sink_ref, q_ref, k_ref, v_ref, pc_ref, da_ref, dp_ref, pm_ref, pw_ref, ps_ref, c_ref, s1_ref, s2_ref,
             dq_ref, dk_ref, dv_ref, dpc_ref, dsink_ref, dpw_ref, dps_ref):
        i = pl.program_id(0)

        @pl.when(i == 0)
        def _():
            dk_ref[...] = jnp.zeros_like(dk_ref)
            dv_ref[...] = jnp.zeros_like(dv_ref)
            dpc_ref[...] = jnp.zeros_like(dpc_ref)
            dsink_ref[...] = jnp.zeros_like(dsink_ref)
            dpw_ref[...] = jnp.zeros_like(dpw_ref)
            dps_ref[...] = jnp.zeros_like(dps_ref)

        start = _band_start(i, S)
        band = pl.ds(start, BAND)
        bias = _window_bias(i * BLOCK - start)
        kb, vb = k_ref[band, :], v_ref[band, :]
        c, s1, s2 = c_ref[...], s1_ref[...], s2_ref[...]
        lane = lax.broadcasted_iota(jnp.int32, (1, LANES), 1)
        hs = range(N_HEADS)
        qs = [q_ref[:, hd * LANES:(hd + 1) * LANES] for hd in hs]
        das = [da_ref[:, hd * LANES:(hd + 1) * LANES] for hd in hs]
        ss = [_dot_nt(qs[hd], kb) + bias for hd in hs]
        d_probs = [_dot_nt(das[hd], vb) for hd in hs]
        ms = [jnp.maximum(jnp.max(ss[hd], axis=-1, keepdims=True), sink_ref[0, hd]) for hd in hs]
        ps = [jnp.exp(ss[hd] - ms[hd]) for hd in hs]
        ess = [jnp.exp(sink_ref[0, hd] - ms[hd]) for hd in hs]
        invs = [1.0 / (jnp.sum(ps[hd], axis=-1, keepdims=True) + ess[hd]) for hd in hs]
        probs = [ps[hd] * invs[hd] for hd in hs]
        deltas = [jnp.sum(probs[hd] * d_probs[hd], axis=-1, keepdims=True) for hd in hs]
        d_ss = [(probs[hd] * (d_probs[hd] - deltas[hd])).astype(BF16) for hd in hs]
        dqs = [_dot(d_ss[hd], kb) for hd in hs]
        dq_ref[...] = jnp.concatenate([_rope_bwd(dqs[hd] * SCALE, c, s1, s2).astype(BF16) for hd in hs], axis=1)
        dks = [_dot_tn(d_ss[hd], qs[hd]) for hd in hs]
        dvs = [_dot_tn(probs[hd].astype(BF16), das[hd]) for hd in hs]
        dk_ref[band, :] += functools.reduce(lambda a, b: a + b, dks)
        dv_ref[band, :] += functools.reduce(lambda a, b: a + b, dvs)
        dsink_ref[...] += functools.reduce(lambda a, b: a + b, [
            jnp.where(lane == hd, -jnp.sum(ess[hd] * invs[hd] * deltas[hd], axis=0, keepdims=True), 0.0) for hd in hs])

        centre = pl.ds(pl.multiple_of(i * BLOCK, BLOCK), BLOCK)
        gs = range(N_POOL)
        sl = [slice(g * POOL_GROUP, (g + 1) * POOL_GROUP) for g in gs]
        devs = [(_dot(pm_ref[0, g], pc_ref[band, sl[g]].astype(BF16)) - pc_ref[centre, sl[g]]).astype(BF16) for g in gs]
        dys = [dp_ref[:, sl[g]].astype(F32) for g in gs]
        zs = [_dot(devs[g], pw_ref[g]) for g in gs]
        dzs = [(dys[g] * ps_ref[:, sl[g]]).astype(BF16) for g in gs]
        d_devs = [_dot_nt(dzs[g], pw_ref[g]) for g in gs]
        dps_ref[...] += jnp.concatenate([jnp.sum(dys[g] * zs[g], axis=0, keepdims=True) for g in gs], axis=1)
        for g in gs:
            dpw_ref[g] += _dot_tn(devs[g], dzs[g])
        dpc_ref[band, :] += jnp.concatenate([_dot_tn(pm_ref[0, g], d_devs[g].astype(BF16)) for g in gs], axis=1)
        dpc_ref[centre, :] -= jnp.concatenate(d_devs, axis=1)

    return _call(
        body, name, (nb,),
        [pl.BlockSpec(memory_space=pltpu.SMEM), _rows(BLOCK, Q_PAD), _full((S, KV_WIDTH)), _full((S, KV_WIDTH)),
         _full((S, POOL_WIDTH)), _rows(BLOCK, Q_PAD), _rows(BLOCK, POOL_WIDTH), _pool_spec(nb),
         _full((N_POOL, POOL_GROUP, POOL_GROUP)), _full((1, POOL_WIDTH)),
         _rows(BLOCK, LANES), _rows(BLOCK, LANES), _rows(BLOCK, LANES)],
        [_rows(BLOCK, Q_PAD), _full((S, KV_WIDTH)), _full((S, KV_WIDTH)), _full((S, POOL_WIDTH)),
         _full((1, LANES)), _full((N_POOL, POOL_GROUP, POOL_GROUP)), _full((1, POOL_WIDTH))],
        [jax.ShapeDtypeStruct((S, Q_PAD), BF16), jax.ShapeDtypeStruct((S, KV_WIDTH), F32),
         jax.ShapeDtypeStruct((S, KV_WIDTH), F32), jax.ShapeDtypeStruct((S, POOL_WIDTH), F32),
         jax.ShapeDtypeStruct((1, LANES), F32), jax.ShapeDtypeStruct((N_POOL, POOL_GROUP, POOL_GROUP), F32),
         jax.ShapeDtypeStruct((1, POOL_WIDTH), F32)],
        [], 56, (sink, q, k, v, pc, da, dp, pool_m, pool_w, pool_scale, rc, rs1, rs2), rider)


def _mix_out(h, a, p, wa, wp, name):
    S = h.shape[0]
    tile = min(TOK_TILE, S)

    def body(h_ref, a_ref, p_ref, wa_ref, wp_ref, o_ref):
        o_ref[...] = h_ref[...] + _dot(a_ref[...], wa_ref[...]) + _dot(p_ref[...], wp_ref[...])

    return pl.pallas_call(
        body, name=name, grid=(S // tile,),
        in_specs=[_rows(tile, D_MODEL), _rows(tile, Q_PAD), _rows(tile, POOL_WIDTH),
                  _full((Q_PAD, D_MODEL)), _full((POOL_WIDTH, D_MODEL))],
        out_specs=_rows(tile, D_MODEL),
        out_shape=jax.ShapeDtypeStruct((S, D_MODEL), F32),
        compiler_params=_params(("parallel",), 40),
    )(h, a, p, wa, wp)


def _loss_head(h, target, gain, name):
    S = h.shape[0]
    tile = min(TOK_TILE, S)

    def body(h_ref, t_ref, g_ref, dh_ref, dhalf_ref, loss_ref, dg_ref):
        @pl.when(pl.program_id(0) == 0)
        def _():
            loss_ref[...] = jnp.zeros_like(loss_ref)
            dg_ref[...] = jnp.zeros_like(dg_ref)

        g = g_ref[...]
        r, xh = _rms_stats(h_ref[...])
        err = xh * g - t_ref[...]
        loss_ref[...] += (0.5 / D_MODEL) * jnp.sum(err * err, axis=0, keepdims=True)
        dh, dg = _rms_bwd(err * (1.0 / D_MODEL), g, r, xh)
        dg_ref[...] += dg
        dh_ref[...] = dh
        dhalf_ref[...] = (0.5 * dh).astype(BF16)

    return pl.pallas_call(
        body, name=name, grid=(S // tile,),
        in_specs=[_rows(tile, D_MODEL), _rows(tile, D_MODEL), _full((1, D_MODEL))],
        out_specs=[_rows(tile, D_MODEL), _rows(tile, D_MODEL), _full((1, D_MODEL)), _full((1, D_MODEL))],
        out_shape=[jax.ShapeDtypeStruct((S, D_MODEL), F32), jax.ShapeDtypeStruct((S, D_MODEL), BF16),
                   jax.ShapeDtypeStruct((1, D_MODEL), F32), jax.ShapeDtypeStruct((1, D_MODEL), F32)],
        compiler_params=_params(("arbitrary",), 40),
    )(h, target, gain)


def _mix_out_bwd(dh_out, dn, h, gain, a, p, wa, wp, name, rider=None):
    S = h.shape[0]
    tile = min(TOK_TILE, S)

    def body(do_ref, dn_ref, h_ref, g_ref, a_ref, p_ref, wa_ref, wp_ref, dh_ref, da_ref, dp_ref, dwa_ref, dwp_ref, dg_ref):
        @pl.when(pl.program_id(0) == 0)
        def _():
            dwa_ref[...] = jnp.zeros_like(dwa_ref)
            dwp_ref[...] = jnp.zeros_like(dwp_ref)
            dg_ref[...] = jnp.zeros_like(dg_ref)

        r, xh = _rms_stats(h_ref[...])
        dnorm, dg = _rms_bwd(_sum_chunks([dn_ref]), g_ref[...], r, xh)
        dh = do_ref[...] + dnorm
        dg_ref[...] += dg
        dh_ref[...] = dh
        dhb = dh.astype(BF16)
        da_ref[...] = _dot_nt(dhb, wa_ref[...]).astype(BF16)
        dp_ref[...] = _dot_nt(dhb, wp_ref[...]).astype(BF16)
        dwa_ref[...] += _dot_tn(a_ref[...], dhb)
        dwp_ref[...] += _dot_tn(p_ref[...], dhb)

    return _call(
        body, name, (S // tile,),
        [_rows(tile, D_MODEL), _chunk_rows(tile, dn.shape[0]), _rows(tile, D_MODEL), _full((1, D_MODEL)),
         _rows(tile, Q_PAD), _rows(tile, POOL_WIDTH), _full((Q_PAD, D_MODEL)), _full((POOL_WIDTH, D_MODEL))],
        [_rows(tile, D_MODEL), _rows(tile, Q_PAD), _rows(tile, POOL_WIDTH),
         _full((Q_PAD, D_MODEL)), _full((POOL_WIDTH, D_MODEL)), _full((1, D_MODEL))],
        [jax.ShapeDtypeStruct((S, D_MODEL), F32), jax.ShapeDtypeStruct((S, Q_PAD), BF16),
         jax.ShapeDtypeStruct((S, POOL_WIDTH), BF16), jax.ShapeDtypeStruct((Q_PAD, D_MODEL), F32),
         jax.ShapeDtypeStruct((POOL_WIDTH, D_MODEL), F32), jax.ShapeDtypeStruct((1, D_MODEL), F32)],
        [], 48, (dh_out, dn, h, gain, a, p, wa, wp), rider)


def _mix_in_bwd(dh_out, h, gain, n, dq, dk, dv, dpc, rc, rs1, rs2, w_in, name, rider=None):
    S = h.shape[0]
    tile = min(TOK_TILE, S)

    def body(do_ref, h_ref, g_ref, n_ref, dq_ref, dk_ref, dv_ref, dpc_ref, c_ref, s1_ref, s2_ref, w_ref,
             dh_ref, dhalf_ref, dw_ref, dg_ref):
        @pl.when(pl.program_id(0) == 0)
        def _():
            dw_ref[...] = jnp.zeros_like(dw_ref)
            dg_ref[...] = jnp.zeros_like(dg_ref)

        dk = _rope_bwd(dk_ref[...], c_ref[...], s1_ref[...], s2_ref[...]).astype(BF16)
        du = jnp.concatenate([dq_ref[...], dk, dv_ref[...].astype(BF16), dpc_ref[...].astype(BF16)], axis=1)
        dn = _dot(du, w_ref[...])
        dw_ref[...] += _dot_tn(du, n_ref[...])
        r, xh = _rms_stats(h_ref[...])
        dnorm, dg = _rms_bwd(dn, g_ref[...], r, xh)
        dh = do_ref[...] + dnorm
        dg_ref[...] += dg
        dh_ref[...] = dh
        dhalf_ref[...] = (0.5 * dh).astype(BF16)

    return _call(
        body, name, (S // tile,),
        [_rows(tile, D_MODEL), _rows(tile, D_MODEL), _full((1, D_MODEL)), _rows(tile, D_MODEL),
         _rows(tile, Q_PAD), _rows(tile, KV_WIDTH), _rows(tile, KV_WIDTH), _rows(tile, POOL_WIDTH),
         _rows(tile, LANES), _rows(tile, LANES), _rows(tile, LANES), _full((U_PAD, D_MODEL))],
        [_rows(tile, D_MODEL), _rows(tile, D_MODEL), _full((U_PAD, D_MODEL)), _full((1, D_MODEL))],
        [jax.ShapeDtypeStruct((S, D_MODEL), F32), jax.ShapeDtypeStruct((S, D_MODEL), BF16),
         jax.ShapeDtypeStruct((U_PAD, D_MODEL), F32), jax.ShapeDtypeStruct((1, D_MODEL), F32)],
        [], 56, (dh_out, h, gain, n, dq, dk, dv, dpc, rc, rs1, rs2, w_in), rider)


def _norm_bwd(dh_out, dns, h, gain, name):
    S = h.shape[0]
    tile = min(TOK_TILE, S)
    n = len(dns)

    def body(do_ref, *refs):
        h_ref, g_ref, dh_ref, dg_ref = refs[n:]

        @pl.when(pl.program_id(0) == 0)
        def _():
            dg_ref[...] = jnp.zeros_like(dg_ref)

        r, xh = _rms_stats(h_ref[...])
        dnorm, dg = _rms_bwd(_sum_chunks(refs[:n]), g_ref[...], r, xh)
        dg_ref[...] += dg
        dh_ref[...] = do_ref[...] + dnorm

    return _call(
        body, name, (S // tile,),
        [_rows(tile, D_MODEL)] + [_chunk_rows(tile, dn.shape[0]) for dn in dns] + [_rows(tile, D_MODEL), _full((1, D_MODEL))],
        [_rows(tile, D_MODEL), _full((1, D_MODEL))],
        [jax.ShapeDtypeStruct((S, D_MODEL), F32), jax.ShapeDtypeStruct((1, D_MODEL), F32)],
        [], 40, (dh_out, *dns, h, gain))


def _rope_tables(S):
    half = ROTARY_DIM // 2
    inv_freq = ROPE_THETA ** (-jnp.arange(0, ROTARY_DIM, 2, dtype=F32) / ROTARY_DIM)
    dim = jnp.arange(LANES) % HEAD_DIM
    ang = jnp.arange(S, dtype=F32)[:, None] * inv_freq[dim % half][None, :]
    lo, hi = (dim < half)[None, :], ((dim >= half) & (dim < ROTARY_DIM))[None, :]
    c = jnp.where(lo | hi, jnp.cos(ang), 1.0)
    s1 = jnp.where(lo, -jnp.sin(ang), 0.0)
    s2 = jnp.where(hi, jnp.sin(ang), 0.0)
    return c, s1, s2


def _pad_heads(w, axis):
    w = jnp.moveaxis(w, axis, 0)
    heads = w.reshape((N_HEADS, HEAD_DIM) + w.shape[1:])
    zero = jnp.zeros_like(heads)
    first = (jnp.arange(N_HEADS) < Q_PER_KV).reshape((N_HEADS, 1) + (1,) * (w.ndim - 1))
    lo = jnp.where(first, heads, zero)
    hi = jnp.where(first, zero, heads)
    padded = jnp.concatenate([lo, hi], axis=1).reshape((Q_PAD,) + w.shape[1:])
    return jnp.moveaxis(padded, 0, axis)


def _unpad_heads(w, axis):
    w = jnp.moveaxis(w, axis, 0)
    groups = w.reshape((N_HEADS, 2, HEAD_DIM) + w.shape[1:])
    first = (jnp.arange(N_HEADS) < Q_PER_KV).reshape((N_HEADS, 1) + (1,) * (w.ndim - 1))
    heads = jnp.where(first, groups[:, 0], groups[:, 1]).reshape((ATTN_WIDTH,) + w.shape[1:])
    return jnp.moveaxis(heads, 0, axis)


IN_ROWS = IN_WIDTH // N_CHIPS
OUT_ROWS = (ATTN_WIDTH + POOL_WIDTH) // N_CHIPS
MIX_ROWS = IN_ROWS + OUT_ROWS
FFN_ROWS = 3 * FF_CHUNK


def _step(x, target, bufs, small, place):
    S = x.shape[0]
    rc, rs1, rs2 = _rope_tables(S)
    (ffn1,) = _comm_call(_allgather(bufs[:1]), "allgather_ffn1")
    ffn1 = ffn1.reshape(N_CHIPS, FFN_ROWS, D_MODEL)
    h1, n1, gate1, up1, mix, ffn2 = _ffn_fwd(x, small["ffn1_norm"], ffn1, "ffn1_fwd", _allgather(bufs[1:]))
    mix, ffn2 = mix.reshape(N_CHIPS, MIX_ROWS, D_MODEL), ffn2.reshape(N_CHIPS, FFN_ROWS, D_MODEL)
    w_in_t = mix[:, :IN_ROWS].reshape(IN_WIDTH, D_MODEL)
    w_in_pad = jnp.concatenate([_pad_heads(w_in_t[:ATTN_WIDTH], 0), w_in_t[ATTN_WIDTH:]], axis=0)
    w_out = mix[:, IN_ROWS:].reshape(ATTN_WIDTH + POOL_WIDTH, D_MODEL)
    wa = _pad_heads(w_out[:ATTN_WIDTH], 0)
    wp = w_out[ATTN_WIDTH:]
    pool_w = small["pool_w"].astype(BF16)

    n2, q, k, v, pc = _mix_in(h1, small["mix_norm"], w_in_pad, rc, rs1, rs2, "mix_in")
    pool_m = _pool_matrices(S)
    a, p = _mix_core_fwd(q, k, v, pc, small["sink_logits"], pool_m, pool_w, small["pool_scale"], "mix_core_fwd")
    h2 = _mix_out(h1, a, p, wa, wp, "mix_out")
    h3, n3, gate2, up2 = _ffn_fwd(h2, small["ffn2_norm"], ffn2, "ffn2_fwd")
    dh3, dhalf3, loss_lanes, d_final = _loss_head(h3, target, small["final_norm"], "loss_head")

    dn3, d_ffn2 = _ffn_bwd(jnp.arange(N_CHIPS, dtype=jnp.int32), dhalf3, n3, gate2, up2, ffn2, "ffn2_bwd")
    dh2, da, dp, dwa, dwp, d_ffn2_norm, received = _mix_out_bwd(dh3, dn3, h2, small["ffn2_norm"], a, p, wa, wp, "mix_out_bwd",
                                                                _sibling_exchange([d_ffn2]))
    pair = _pair_sum(place, d_ffn2, received, "grad_pair_sum_ffn2")
    dq, dk, dv, dpc, dsink, dpool_w, dpool_scale, stack = _mix_core_bwd(
        q, k, v, pc, da, dp, small["sink_logits"], pool_m, pool_w, small["pool_scale"], rc, rs1, rs2, "mix_core_bwd",
        _scatter([pair]))
    reduced = _chip_sum(place, pair, stack, 2, "grad_chip_sum_ffn2")
    dh1, dhalf1, dw_in_pad, d_mix_norm, g_ffn2 = _mix_in_bwd(dh2, h1, small["mix_norm"], n2, dq, dk, dv, dpc, rc, rs1, rs2,
                                                             w_in_pad, "mix_in_bwd", _sibling_share([reduced]))
    dw_in_t = jnp.concatenate([_unpad_heads(dw_in_pad[:Q_PAD], 0), dw_in_pad[Q_PAD:]], axis=0)
    dw_out = jnp.concatenate([_unpad_heads(dwa, 0), dwp], axis=0)
    d_mix = jnp.concatenate([dw_in_t.reshape(N_CHIPS, IN_ROWS, D_MODEL), dw_out.reshape(N_CHIPS, OUT_ROWS, D_MODEL)], axis=1)
    d_mix = jnp.transpose(d_mix.reshape(N_CHIPS, 2, MIX_ROWS // 2, D_MODEL), (1, 0, 2, 3)).astype(BF16)
    small_g = {"ffn1_norm": jnp.zeros_like(d_mix_norm), "mix_norm": d_mix_norm, "ffn2_norm": d_ffn2_norm,
               "final_norm": d_final, "pool_scale": dpool_scale, "sink_logits": dsink[:, :N_HEADS], "pool_w": dpool_w}
    loss_row = jnp.sum(loss_lanes.reshape(D_MODEL // LANES, LANES), axis=0, keepdims=True)
    small_early = _pack_small(small_g, loss_row)

    chunk = [(place[1:] + 1 + p) % N_CHIPS for p in range(N_CHIPS)]
    ffn1_bwd = functools.partial(_ffn_bwd, d_out=dhalf1, n=n1, gate=gate1, up=up1, group=ffn1)
    stack = jnp.zeros((N_CHIPS, FFN_ROWS // 2, D_MODEL), BF16)
    dn_a, dw_a, recv_mix, small_all = ffn1_bwd(chunk[0], name="ffn1_bwd_0",
                                               rider=_merge(_sibling_exchange([d_mix]), _small_allgather(small_early)))
    pair_mix = _pair_sum(place, d_mix, recv_mix, "grad_pair_sum_mix")
    dn_b, dw_b, recv_a, stack_mix = ffn1_bwd(chunk[1], name="ffn1_bwd_1",
                                             rider=_merge(_sibling_exchange([dw_a]), _scatter([pair_mix])))
    pair_a = _pair_sum(place, dw_a, recv_a, "grad_pair_sum_ffn1_0")
    reduced_mix = _chip_sum(place, pair_mix, stack_mix, 2, "grad_chip_sum_mix")
    dn_c, dw_c, recv_b, stack, g_mix = ffn1_bwd(
        chunk[2], name="ffn1_bwd_2",
        rider=_merge(_merge(_sibling_exchange([dw_b]), _scatter_step(pair_a, stack, 0)), _sibling_share([reduced_mix])))
    pair_b = _pair_sum(place, dw_b, recv_b, "grad_pair_sum_ffn1_1")
    dn_d, dw_d, recv_c, stack = ffn1_bwd(chunk[3], name="ffn1_bwd_3",
                                         rider=_merge(_sibling_exchange([dw_c]), _scatter_step(pair_b, stack, 1)))
    pair_c = _pair_sum(place, dw_c, recv_c, "grad_pair_sum_ffn1_2")
    grad_x, d_ffn1_norm = _norm_bwd(dh1, [dn_a, dn_b, dn_c, dn_d], x, small["ffn1_norm"], "norm1_bwd")

    recv_d, stack, gains = _comm_call(
        _merge(_merge(_sibling_exchange([dw_d]), _scatter_step(pair_c, stack, 2)),
               _small_allgather(d_ffn1_norm.reshape(-1, LANES))), "grad_tail")
    pair_d = _pair_sum(place, dw_d, recv_d, "grad_pair_sum_ffn1_3")
    reduced_ffn1 = _chip_sum(place, pair_d, stack, 2, "grad_chip_sum_ffn1")
    (g_ffn1,) = _comm_call(_sibling_share([reduced_ffn1]), "grad_share_tail")
    gain_sum = _sum_leading(gains, 1, "gain_grad_sum")
    small_sum = jnp.concatenate([gain_sum, _sum_leading(small_all, 1, "small_grad_sum")[gain_sum.shape[0]:]], axis=0)
    return jnp.sum(small_sum[SMALL_ROWS - 1]), grad_x, [g.reshape(-1, D_MODEL) for g in (g_ffn1, g_mix, g_ffn2)], small_sum


GROUPS =(("ffn1_w_gate", "ffn1_w_up", "ffn1_w_down"), ("w_in", "w_out"), ("ffn2_w_gate", "ffn2_w_up", "ffn2_w_down"))
TRANSPOSED = ("ffn1_w_gate", "ffn1_w_up", "w_in", "ffn2_w_gate", "ffn2_w_up")


def _place():
    x, y, c = lax.axis_index("x"), lax.axis_index("y"), lax.axis_index("c")
    chips = [(1 - x, y), (x, 1 - y), (1 - x, 1 - y)]
    return x, y, c, chips


def _remote(src, dst, send_sem, recv_sem, to):
    return pltpu.make_async_remote_copy(src_ref=src, dst_ref=dst, send_sem=send_sem, recv_sem=recv_sem,
                                        device_id=to, device_id_type=MESH)


def _pack(chip, members, name):
    rows = members[0].shape[0]
    n = len(members)

    def body(chip_ref, *refs):
        ins, out_ref, buf, sems = refs[:n], refs[n], refs[n + 1], refs[n + 2]
        copies = [pltpu.make_async_copy(ins[k], buf.at[k], sems.at[k]) for k in range(n)]
        for cp in copies:
            cp.start()
        for k in range(n):
            copies[k].wait()
            out_ref[0, k * rows:(k + 1) * rows, :] = buf[k].astype(BF16)

    return pl.pallas_call(
        body, name=name,
        grid_spec=pltpu.PrefetchScalarGridSpec(
            num_scalar_prefetch=1, grid=(1,),
            in_specs=[HBM_SPEC] * n,
            out_specs=pl.BlockSpec((1, n * rows, D_MODEL), lambda k, chip_ref: (chip_ref[0], 0, 0)),
            scratch_shapes=[pltpu.VMEM((n, rows, D_MODEL), F32), pltpu.SemaphoreType.DMA((n,))]),
        out_shape=jax.ShapeDtypeStruct((N_CHIPS, n * rows, D_MODEL), BF16),
        compiler_params=_params(("arbitrary",), 40),
    )(chip, *members)


def _same(arrays):
    return [jax.ShapeDtypeStruct(a.shape, a.dtype) for a in arrays]


def _allgather(bufs):
    n = len(bufs)

    def copies(outs, send_sems, recv_sems, started_only=False):
        x, y, c, chips = _place()
        me = 2 * x + y
        slots = [2 * cx + cy for cx, cy in chips]
        first = [[_remote(outs[a].at[me, c], outs[a].at[me, c], send_sems.at[6 * a + k], recv_sems.at[6 * a + k], (*chips[k], c))
                  for k in range(3)] for a in range(n)]
        if started_only:
            return first
        passed = [[_remote(outs[a].at[slots[k], c], outs[a].at[slots[k], c], send_sems.at[6 * a + 3 + k],
                           recv_sems.at[6 * a + 3 + k], (x, y, 1 - c)) for k in range(3)] for a in range(n)]
        landed = [[_remote(outs[a].at[me, c], outs[a].at[slots[k], c], send_sems.at[6 * a + k], recv_sems.at[6 * a + k],
                           (*chips[k], c)) for k in range(3)] for a in range(n)]
        handed = [[_remote(outs[a].at[me, c], outs[a].at[slots[k], 1 - c], send_sems.at[6 * a + 3 + k],
                           recv_sems.at[6 * a + 3 + k], (x, y, 1 - c)) for k in range(3)] for a in range(n)]
        return first, passed, landed, handed

    def start(ins, outs, sems):
        for per_buf in copies(outs, *sems, started_only=True):
            for cp in per_buf:
                cp.start()

    def finish(ins, outs, sems):
        first, passed, landed, handed = copies(outs, *sems)
        for a in range(n):
            for k in range(3):
                landed[a][k].wait_recv()
                passed[a][k].start()
        for a in range(n):
            for k in range(3):
                handed[a][k].wait_recv()
        for a in range(n):
            for cp in first[a] + passed[a]:
                cp.wait_send()

    return _Rider(list(bufs), _same(bufs), {a: a for a in range(n)},
                  [pltpu.SemaphoreType.DMA((6 * n,)), pltpu.SemaphoreType.DMA((6 * n,))], start, finish)


def _sibling_exchange(parts):
    n = len(parts)

    def copies(ins, outs, send_sems, recv_sems):
        x, y, c, _ = _place()
        return [_remote(ins[a].at[1 - c], outs[a], send_sems.at[a], recv_sems.at[a], (x, y, 1 - c)) for a in range(n)]

    def start(ins, outs, sems):
        for cp in copies(ins, outs, *sems):
            cp.start()

    def finish(ins, outs, sems):
        for cp in copies(ins, outs, *sems):
            cp.wait_recv()
            cp.wait_send()

    return _Rider(list(parts), [jax.ShapeDtypeStruct(p.shape[1:], p.dtype) for p in parts], {},
                  [pltpu.SemaphoreType.DMA((n,)), pltpu.SemaphoreType.DMA((n,))], start, finish)


def _small_allgather(small):
    flips = [(fx, fy, fc) for fx in range(2) for fy in range(2) for fc in range(2)][1:]

    def copies(small_ref, gather_ref, send_sems, recv_sems, local_sem, started_only=False):
        x, y, c, _ = _place()
        me = 4 * x + 2 * y + c
        peers = [((1 - x) if fx else x, (1 - y) if fy else y, (1 - c) if fc else c) for fx, fy, fc in flips]
        own = pltpu.make_async_copy(small_ref, gather_ref.at[me], local_sem)
        sent = [_remote(small_ref, gather_ref.at[me], send_sems.at[k], recv_sems.at[k], peer) for k, peer in enumerate(peers)]
        if started_only:
            return own, sent
        landed = [_remote(small_ref, gather_ref.at[4 * px + 2 * py + pc], send_sems.at[k], recv_sems.at[k], (px, py, pc))
                  for k, (px, py, pc) in enumerate(peers)]
        return own, sent, landed

    def start(ins, outs, sems):
        own, sent = copies(ins[0], outs[0], *sems, started_only=True)
        own.start()
        for cp in sent:
            cp.start()

    def finish(ins, outs, sems):
        own, sent, landed = copies(ins[0], outs[0], *sems)
        for cp in landed:
            cp.wait_recv()
        for cp in sent:
            cp.wait_send()
        own.wait()

    return _Rider([small], [jax.ShapeDtypeStruct((2 * N_CHIPS,) + small.shape, small.dtype)], {},
                  [pltpu.SemaphoreType.DMA((7,)), pltpu.SemaphoreType.DMA((7,)), pltpu.SemaphoreType.DMA], start, finish)


def _merge(a, b):
    na, nao, nas = len(a.operands), len(a.out_shapes), len(a.scratch)

    def start(ins, outs, sems):
        a.start(ins[:na], outs[:nao], sems[:nas])
        b.start(ins[na:], outs[nao:], sems[nas:])

    def finish(ins, outs, sems):
        a.finish(ins[:na], outs[:nao], sems[:nas])
        b.finish(ins[na:], outs[nao:], sems[nas:])

    aliases = {**a.aliases, **{na + k: nao + v for k, v in b.aliases.items()}}
    return _Rider(a.operands + b.operands, a.out_shapes + b.out_shapes, aliases, a.scratch + b.scratch, start, finish)


def _scatter_step(pair, stack, step):
    def copies(pair_ref, stack_ref, send_sem, recv_sem, started_only=False):
        x, y, c, _ = _place()
        me = 2 * x + y
        to = (me + 1 + step) % N_CHIPS
        frm = (me + N_CHIPS - 1 - step) % N_CHIPS
        sent = _remote(pair_ref.at[0], stack_ref.at[me], send_sem, recv_sem, (to // 2, to % 2, c))
        if started_only:
            return sent
        landed = _remote(pair_ref.at[0], stack_ref.at[frm], send_sem, recv_sem, (frm // 2, frm % 2, c))
        return sent, landed

    def start(ins, outs, sems):
        copies(ins[0], outs[0], *sems, started_only=True).start()

    def finish(ins, outs, sems):
        sent, landed = copies(ins[0], outs[0], *sems)
        landed.wait_recv()
        sent.wait_send()

    return _Rider([pair, stack], _same([stack]), {1: 0}, [pltpu.SemaphoreType.DMA, pltpu.SemaphoreType.DMA], start, finish)


def _scatter(sums):
    n = len(sums)

    def copies(ins, outs, send_sems, recv_sems, started_only=False):
        x, y, c, chips = _place()
        me = 2 * x + y
        slots = [2 * cx + cy for cx, cy in chips]
        sent = [_remote(ins[a].at[slots[k]], outs[a].at[me], send_sems.at[3 * a + k], recv_sems.at[3 * a + k], (*chips[k], c))
                for a in range(n) for k in range(3)]
        if started_only:
            return sent
        landed = [_remote(ins[a].at[slots[k]], outs[a].at[slots[k]], send_sems.at[3 * a + k], recv_sems.at[3 * a + k],
                          (*chips[k], c)) for a in range(n) for k in range(3)]
        return sent, landed

    def start(ins, outs, sems):
        for cp in copies(ins, outs, *sems, started_only=True):
            cp.start()

    def finish(ins, outs, sems):
        sent, landed = copies(ins, outs, *sems)
        for cp in landed:
            cp.wait_recv()
        for cp in sent:
            cp.wait_send()

    return _Rider(list(sums), _same(sums), {}, [pltpu.SemaphoreType.DMA((3 * n,)), pltpu.SemaphoreType.DMA((3 * n,))],
                  start, finish)


def _sibling_share(bufs):
    n = len(bufs)

    def copies(outs, send_sems, recv_sems, started_only=False):
        x, y, c, _ = _place()
        sent = [_remote(outs[a].at[c], outs[a].at[c], send_sems.at[a], recv_sems.at[a], (x, y, 1 - c)) for a in range(n)]
        if started_only:
            return sent
        landed = [_remote(outs[a].at[c], outs[a].at[1 - c], send_sems.at[a], recv_sems.at[a], (x, y, 1 - c)) for a in range(n)]
        return sent, landed

    def start(ins, outs, sems):
        for cp in copies(outs, *sems, started_only=True):
            cp.start()

    def finish(ins, outs, sems):
        sent, landed = copies(outs, *sems)
        for cp in landed:
            cp.wait_recv()
        for cp in sent:
            cp.wait_send()

    return _Rider(list(bufs), _same(bufs), {a: a for a in range(n)},
                  [pltpu.SemaphoreType.DMA((n,)), pltpu.SemaphoreType.DMA((n,))], start, finish)


def _pair_sum(core, part, received, name):
    _, k, rh, cols = part.shape

    def body(core_ref, p_ref, r_ref, o_ref):
        o_ref[...] = (p_ref[0].astype(F32) + r_ref[...].astype(F32)).astype(BF16)

    return pl.pallas_call(
        body, name=name,
        grid_spec=pltpu.PrefetchScalarGridSpec(
            num_scalar_prefetch=1, grid=(k,),
            in_specs=[pl.BlockSpec((1, 1, rh, cols), lambda j, core_ref: (core_ref[0], j, 0, 0)),
                      pl.BlockSpec((1, rh, cols), lambda j, core_ref: (j, 0, 0))],
            out_specs=pl.BlockSpec((1, rh, cols), lambda j, core_ref: (j, 0, 0))),
        out_shape=jax.ShapeDtypeStruct((k, rh, cols), BF16),
        compiler_params=_params(("parallel",), 32),
    )(core, part, received)


def _sum_leading(stack, steps, name):
    k, rows, cols = stack.shape
    tile = rows // steps

    def body(s_ref, o_ref):
        total = s_ref[0].astype(F32)
        for d in range(1, k):
            total = total + s_ref[d].astype(F32)
        o_ref[...] = total

    return pl.pallas_call(
        body, name=name, grid=(steps,),
        in_specs=[pl.BlockSpec((k, tile, cols), lambda i: (0, i, 0))],
        out_specs=pl.BlockSpec((tile, cols), lambda i: (i, 0)),
        out_shape=jax.ShapeDtypeStruct((rows, cols), F32),
        compiler_params=_params(("parallel",), 32),
    )(stack)


def _chip_sum(place, own, stack, steps, name):
    k, rows, cols = stack.shape
    tile = rows // steps

    def body(place_ref, own_ref, *refs):
        chip = place_ref[1]
        total = None
        for d in range(k):
            term = jnp.where(chip == d, own_ref[0], refs[d][0]).astype(F32)
            total = term if total is None else total + term
        refs[k][0] = total

    def other(d):
        return lambda i, place_ref: (jnp.where(place_ref[1] == d, (d + 1) % k, d), i, 0)

    return pl.pallas_call(
        body, name=name,
        grid_spec=pltpu.PrefetchScalarGridSpec(
            num_scalar_prefetch=1, grid=(steps,),
            in_specs=[pl.BlockSpec((1, tile, cols), lambda i, place_ref: (place_ref[1] % own.shape[0], i, 0))]
            + [pl.BlockSpec((1, tile, cols), other(d)) for d in range(k)],
            out_specs=pl.BlockSpec((1, tile, cols), lambda i, place_ref: (place_ref[0], i, 0))),
        out_shape=jax.ShapeDtypeStruct((2, rows, cols), F32),
        compiler_params=_params(("arbitrary",), 32),
    )(place, own, *([stack] * k))


def _adamw(w, g, row0, m, v, tile, name):
    rows, cols = w.shape
    first = row0 // tile
    assert rows % tile == 0 and row0 % tile == 0
    bc1 = 1.0 - ADAM_B1 ** ADAM_STEP
    bc2 = 1.0 - ADAM_B2 ** ADAM_STEP

    def body(w_ref, g_ref, m_ref, v_ref, go_ref, d_ref, mo_ref, vo_ref):
        g = g_ref[...]
        m_new = ADAM_B1 * m_ref[...] + (1.0 - ADAM_B1) * g
        v_new = ADAM_B2 * v_ref[...] + (1.0 - ADAM_B2) * (g * g)
        go_ref[...] = g
        d_ref[...] = -ADAM_LR * ((m_new / bc1) / (jnp.sqrt(v_new / bc2) + ADAM_EPS) + ADAM_WD * w_ref[...])
        mo_ref[...] = m_new
        vo_ref[...] = v_new

    spec = pl.BlockSpec((tile, cols), lambda i: (i, 0))
    g_spec = pl.BlockSpec((tile, cols), lambda i: (first + i, 0))
    return pl.pallas_call(
        body, name=name, grid=(rows // tile,),
        in_specs=[spec, g_spec, spec, spec], out_specs=[spec] * 4,
        out_shape=[jax.ShapeDtypeStruct((rows, cols), F32)] * 4,
        compiler_params=_params(("parallel",), 32),
    )(w, g, m, v)


SMALL = ("ffn1_norm", "mix_norm", "ffn2_norm", "final_norm", "pool_scale", "sink_logits", "pool_w")


def _pack_small(d, last_row=None):
    sink = jnp.pad(d["sink_logits"].reshape(1, N_HEADS), ((0, 0), (0, LANES - N_HEADS)))
    rows = [d[n].reshape(-1, LANES) for n in SMALL[:5]] + [sink, d["pool_w"].reshape(-1, LANES)]
    used = sum(r.shape[0] for r in rows)
    last = jnp.zeros((1, LANES), F32) if last_row is None else last_row
    return jnp.concatenate(rows + [jnp.zeros((SMALL_ROWS - used - 1, LANES), F32), last], axis=0)


def _unpack_small(packed, like):
    out, row = {}, 0
    for n in SMALL:
        size = LANES if n == "sink_logits" else math.prod(like[n].shape)
        chunk = packed[row:row + size // LANES].reshape(-1)
        out[n] = (chunk[:N_HEADS] if n == "sink_logits" else chunk).reshape(like[n].shape)
        row += size // LANES
    return out


def kernel(x, ffn1_norm, ffn1_w_gate, ffn1_w_up, ffn1_w_down, mix_norm, w_in, sink_logits, pool_w, pool_scale, w_out, ffn2_norm, ffn2_w_gate, ffn2_w_up, ffn2_w_down, final_norm, loss_target, m_ffn1_norm, m_ffn1_w_gate, m_ffn1_w_up, m_ffn1_w_down, m_mix_norm, m_w_in, m_sink_logits, m_pool_w, m_pool_scale, m_w_out, m_ffn2_norm, m_ffn2_w_gate, m_ffn2_w_up, m_ffn2_w_down, m_final_norm, v_ffn1_norm, v_ffn1_w_gate, v_ffn1_w_up, v_ffn1_w_down, v_mix_norm, v_w_in, v_sink_logits, v_pool_w, v_pool_scale, v_w_out, v_ffn2_norm, v_ffn2_w_gate, v_ffn2_w_up, v_ffn2_w_down, v_final_norm):
    names = ("ffn1_norm", "ffn1_w_gate", "ffn1_w_up", "ffn1_w_down", "mix_norm", "w_in", "sink_logits", "pool_w",
             "pool_scale", "w_out", "ffn2_norm", "ffn2_w_gate", "ffn2_w_up", "ffn2_w_down", "final_norm")
    weights = dict(zip(names, (ffn1_norm, ffn1_w_gate, ffn1_w_up, ffn1_w_down, mix_norm, w_in, sink_logits, pool_w,
                               pool_scale, w_out, ffn2_norm, ffn2_w_gate, ffn2_w_up, ffn2_w_down, final_norm)))
    mom1 = dict(zip(names, (m_ffn1_norm, m_ffn1_w_gate, m_ffn1_w_up, m_ffn1_w_down, m_mix_norm, m_w_in, m_sink_logits,
                            m_pool_w, m_pool_scale, m_w_out, m_ffn2_norm, m_ffn2_w_gate, m_ffn2_w_up, m_ffn2_w_down,
                            m_final_norm)))
    mom2 = dict(zip(names, (v_ffn1_norm, v_ffn1_w_gate, v_ffn1_w_up, v_ffn1_w_down, v_mix_norm, v_w_in, v_sink_logits,
                            v_pool_w, v_pool_scale, v_w_out, v_ffn2_norm, v_ffn2_w_gate, v_ffn2_w_up, v_ffn2_w_down,
                            v_final_norm)))
    chip = (2 * lax.axis_index("x") + lax.axis_index("y")).astype(jnp.int32).reshape(1)
    place = jnp.concatenate([lax.axis_index("c").astype(jnp.int32).reshape(1), chip])

    def rows_of(t, n):
        return jnp.swapaxes(t[n][0], 0, 1) if n in TRANSPOSED else t[n][0]

    packed = [_pack(chip, [rows_of(weights, n) for n in GROUPS[0]], "pack_ffn1"),
              _pack(chip, [jnp.concatenate([rows_of(weights, n) for n in GROUPS[1]], axis=0)], "pack_mix"),
              _pack(chip, [rows_of(weights, n) for n in GROUPS[2]], "pack_ffn2")]
    bufs = [p.reshape(N_CHIPS, 2, p.shape[1] // 2, D_MODEL) for p in packed]

    small_w = {"ffn1_norm": ffn1_norm, "mix_norm": mix_norm, "ffn2_norm": ffn2_norm,
               "final_norm": final_norm.reshape(1, D_MODEL), "pool_scale": pool_scale, "sink_logits": sink_logits,
               "pool_w": pool_w[0]}
    loss, grad_x, group_grads, small_sum = _step(x[0], loss_target[0], bufs, small_w, place)

    out_g, out_d, out_m, out_v = {}, {}, {}, {}
    for members, g in zip(GROUPS, group_grads):
        row0 = 0
        for n in members:
            w = rows_of(weights, n)
            tile = FF_CHUNK // 4 if w.shape[0] == FF_CHUNK else math.gcd(IN_ROWS, OUT_ROWS)
            outs = _adamw(w, g, row0, rows_of(mom1, n), rows_of(mom2, n), tile, "adamw_" + n)
            row0 += w.shape[0]
            for dst, t in zip((out_g, out_d, out_m, out_v), outs):
                dst[n] = (jnp.swapaxes(t, 0, 1) if n in TRANSPOSED else t).reshape(weights[n].shape)
    small_outs = _adamw(_pack_small(weights), small_sum, 0, _pack_small(mom1), _pack_small(mom2), SMALL_ROWS, "adamw_small")
    for dst, packed in zip((out_g, out_d, out_m, out_v), small_outs):
        dst.update(_unpack_small(packed, weights))

    return (loss,grad_x.reshape(x.shape), *[out_g[n] for n in names], *[out_d[n] for n in names],
            *[out_m[n] for n in names], *[out_v[n] for n in names])
```

```python
import collections
import functools
import math

import jax
import jax.numpy as jnp
from jax import lax
from jax.experimental import pallas as pl
from jax.experimental.pallas import tpu as pltpu

F32, BF16 = jnp.float32, jnp.bfloat16
MESH = pl.DeviceIdType.MESH

D_MODEL = 1024
D_FF = 2816
N_CHIPS = 4
FF_CHUNK = D_FF // N_CHIPS
HEAD_DIM = 64
N_HEADS = 8
N_KV = 2
Q_PER_KV = N_HEADS // N_KV
KV_WIDTH = N_KV * HEAD_DIM
ATTN_WIDTH = N_HEADS * HEAD_DIM
POOL_WINDOWS = (2, 4, 8, 16)
N_POOL = len(POOL_WINDOWS)
POOL_GROUP = 128
POOL_WIDTH = N_POOL * POOL_GROUP
IN_WIDTH = ATTN_WIDTH + 2 * KV_WIDTH + POOL_WIDTH
WINDOW = 128
BLOCK = 128
BAND = 3 * BLOCK
ROPE_THETA = 500000.0
ROTARY_DIM = HEAD_DIM // 4
EPS = 1e-6
LANES = 128
Q_PAD = N_HEADS * LANES
U_PAD = Q_PAD + 2 * KV_WIDTH + POOL_WIDTH
SCALE = HEAD_DIM ** -0.5
NEG = -1e30

ADAM_LR, ADAM_B1, ADAM_B2, ADAM_EPS, ADAM_WD, ADAM_STEP = 0.001, 0.9, 0.999, 1e-08, 0.01, 10

V7X_VMEM_BYTES = 64 * 1024 * 1024
TOK_TILE = 512
FFN_FWD_TILE = 1024
SMALL_ROWS = 552


def _params(sem, vmem_mb):
    assert vmem_mb * 1024 * 1024 <= V7X_VMEM_BYTES
    return pltpu.CompilerParams(dimension_semantics=sem, vmem_limit_bytes=vmem_mb * 1024 * 1024)


def _dot(a, b):
    return lax.dot_general(a, b, (((1,), (0,)), ((), ())), preferred_element_type=F32)


def _dot_nt(a, b):
    return lax.dot_general(a, b, (((1,), (1,)), ((), ())), preferred_element_type=F32)


def _dot_tn(a, b):
    return lax.dot_general(a, b, (((0,), (0,)), ((), ())), preferred_element_type=F32)


def _rms_stats(h):
    r = lax.rsqrt(jnp.mean(h * h, axis=-1, keepdims=True) + EPS)
    return r, h * r


def _rms_bwd(dn, g, r, xh):
    gd = dn * g
    dh = r * (gd - xh * jnp.mean(gd * xh, axis=-1, keepdims=True))
    return dh, jnp.sum(dn * xh, axis=0, keepdims=True)


def _rope(x, c, s1, s2):
    return x * c + pltpu.roll(x, LANES - ROTARY_DIM // 2, 1) * s1 + pltpu.roll(x, ROTARY_DIM // 2, 1) * s2


def _rope_bwd(d, c, s1, s2):
    return d * c + pltpu.roll(d * s1, ROTARY_DIM // 2, 1) + pltpu.roll(d * s2, LANES - ROTARY_DIM // 2, 1)


def _sum_chunks(refs):
    terms = [ref[j].astype(F32) for ref in refs for j in range(ref.shape[0])]
    return functools.reduce(lambda a, b: a + b, terms)


def _chunk_rows(tile, k):
    return pl.BlockSpec((k, tile, D_MODEL), lambda i, *_: (0, i, 0))


def _full(shape):
    nd = len(shape)
    return pl.BlockSpec(shape, lambda *_: (0,) * nd)


def _rows(tile, cols):
    return pl.BlockSpec((tile, cols), lambda i, *_: (i, 0))


HBM_SPEC = pl.BlockSpec(memory_space=pltpu.HBM)

_Rider = collections.namedtuple("_Rider", "operands out_shapes aliases scratch start finish")


_NO_RIDER = _Rider([], [], {}, [], None, None)


def _call(body, name, grid, in_specs, out_specs, out_shape, scratch, vmem_mb, args, rider=None, prefetch=()):
    rider = rider or _NO_RIDER
    n_pre, n_in, n_out, n_scr = len(prefetch), len(in_specs), len(out_specs), len(scratch)
    r_in, r_out = len(rider.operands), len(rider.out_shapes)

    def fused(*refs):
        pre, refs = refs[:n_pre], refs[n_pre:]
        ins, refs = refs[:n_in], refs[n_in:]
        r_ins, refs = refs[:r_in], refs[r_in:]
        outs, refs = refs[:n_out], refs[n_out:]
        r_outs, refs = refs[:r_out], refs[r_out:]
        scr, r_scr = refs[:n_scr], refs[n_scr:]
        ids = [pl.program_id(d) for d in range(len(grid))]
        if rider.start is not None:
            @pl.when(functools.reduce(jnp.logical_and, [i == 0 for i in ids]))
            def _():
                rider.start(r_ins, r_outs, r_scr)

        body(*pre, *ins, *outs, *scr)

        if rider.finish is not None:
            @pl.when(functools.reduce(jnp.logical_and, [i == g - 1 for i, g in zip(ids, grid)]))
            def _():
                rider.finish(r_ins, r_outs, r_scr)

    return pl.pallas_call(
        fused, name=name,
        grid_spec=pltpu.PrefetchScalarGridSpec(
            num_scalar_prefetch=n_pre, grid=grid,
            in_specs=list(in_specs) + [HBM_SPEC] * r_in, out_specs=list(out_specs) + [HBM_SPEC] * r_out,
            scratch_shapes=list(scratch) + list(rider.scratch)),
        out_shape=list(out_shape) + list(rider.out_shapes),
        input_output_aliases={n_pre + n_in + k: n_out + v for k, v in rider.aliases.items()},
        compiler_params=_params(("arbitrary",) * len(grid), vmem_mb),
    )(*prefetch, *args, *rider.operands)


def _comm_call(rider, name):
    r_in, r_out = len(rider.operands), len(rider.out_shapes)

    def body(*refs):
        r_ins, r_outs, r_scr = refs[:r_in], refs[r_in:r_in + r_out], refs[r_in + r_out:]
        rider.start(r_ins, r_outs, r_scr)
        rider.finish(r_ins, r_outs, r_scr)

    return pl.pallas_call(
        body, name=name, in_specs=[HBM_SPEC] * r_in, out_specs=[HBM_SPEC] * r_out, out_shape=list(rider.out_shapes),
        input_output_aliases=dict(rider.aliases), scratch_shapes=list(rider.scratch),
    )(*rider.operands)


def _ffn_fwd(h, gain, group, name, rider=None):
    S = h.shape[0]
    tile = min(FFN_FWD_TILE, S)
    nt = S // tile

    def body(h_ref, g_ref, wg_ref, wu_ref, wd_ref, ho_ref, n_ref, gate_ref, up_ref, acc):
        j = pl.program_id(1)

        @pl.when(j == 0)
        def _():
            _, xh = _rms_stats(h_ref[...])
            n_ref[...] = (xh * g_ref[...]).astype(BF16)
            acc[...] = jnp.zeros_like(acc)

        halves = [pl.ds(s * (tile // 2), tile // 2) for s in range(2)]
        ns = [n_ref[rows, :] for rows in halves]
        gates = [_dot_nt(n, wg_ref[0]) for n in ns]
        ups = [_dot_nt(n, wu_ref[0]) for n in ns]
        acts = [(g * jax.nn.sigmoid(g) * u).astype(BF16) for g, u in zip(gates, ups)]
        for rows, g, u, act in zip(halves, gates, ups, acts):
            gate_ref[0, rows, :] = g.astype(BF16)
            up_ref[0, rows, :] = u.astype(BF16)
            acc[rows, :] += _dot(act, wd_ref[0])

        @pl.when(j == N_CHIPS - 1)
        def _():
            ho_ref[...] = h_ref[...] + 0.5 * acc[...]

    tok = pl.BlockSpec((tile, D_MODEL), lambda i, j: (i, 0))
    hid = pl.BlockSpec((1, tile, FF_CHUNK), lambda i, j: (j, i, 0))
    return _call(
        body, name, (nt, N_CHIPS),
        [tok, pl.BlockSpec((1, D_MODEL), lambda i, j: (0, 0))]
        + [pl.BlockSpec((1, FF_CHUNK, D_MODEL), functools.partial(lambda i, j, part: (j, part, 0), part=part))
           for part in range(3)],
        [tok, tok, hid, hid],
        [jax.ShapeDtypeStruct((S, D_MODEL), F32), jax.ShapeDtypeStruct((S, D_MODEL), BF16),
         jax.ShapeDtypeStruct((N_CHIPS, S, FF_CHUNK), BF16), jax.ShapeDtypeStruct((N_CHIPS, S, FF_CHUNK), BF16)],
        [pltpu.VMEM((tile, D_MODEL), F32)], 56, (h, gain, group, group, group), rider)


def _ffn_bwd(chunks, d_out, n, gate, up, group, name, rider=None):
    S = n.shape[0]
    n_chunks = chunks.shape[0]
    tile = min(TOK_TILE, S)
    nt = S // tile
    half_rows = 3 * FF_CHUNK // 2
    cut = FF_CHUNK // 2

    def body(chunks_ref, do_ref, n_ref, gate_ref, up_ref, wg_ref, wu_ref, wd_ref, dn_ref, dw_ref, acc_g, acc_u, acc_d):
        j, i = pl.program_id(0), pl.program_id(1)

        @pl.when(i == 0)
        def _():
            acc_g[...] = jnp.zeros_like(acc_g)
            acc_u[...] = jnp.zeros_like(acc_u)
            acc_d[...] = jnp.zeros_like(acc_d)

        halves = [pl.ds(s * (tile // 2), tile // 2) for s in range(2)]
        dos = [do_ref[rows, :] for rows in halves]
        d_acts = [_dot_nt(do, wd_ref[0]) for do in dos]
        gs = [gate_ref[0, rows, :].astype(F32) for rows in halves]
        us = [up_ref[0, rows, :].astype(F32) for rows in halves]
        sigs = [jax.nn.sigmoid(g) for g in gs]
        silus = [g * sig for g, sig in zip(gs, sigs)]
        d_ups = [(d_act * silu).astype(BF16) for d_act, silu in zip(d_acts, silus)]
        d_gates = [(d_act * u * (sig * (1.0 + g * (1.0 - sig)))).astype(BF16) for d_act, u, sig, g in zip(d_acts, us, sigs, gs)]
        for rows, d_gate, d_up in zip(halves, d_gates, d_ups):
            dn_ref[0, rows, :] = (_dot(d_gate, wg_ref[0]) + _dot(d_up, wu_ref[0])).astype(BF16)
        d_gate, d_up = jnp.concatenate(d_gates, axis=0), jnp.concatenate(d_ups, axis=0)
        act = jnp.concatenate([(silu * u).astype(BF16) for silu, u in zip(silus, us)], axis=0)
        nn = n_ref[...]
        acc_g[...] += _dot_tn(d_gate, nn)
        acc_u[...] += _dot_tn(d_up, nn)
        acc_d[...] += _dot_tn(act, do_ref[...])

        @pl.when(i == nt - 1)
        def _():
            dw_ref[0, 0, :FF_CHUNK, :] = acc_g[...].astype(BF16)
            dw_ref[0, 0, FF_CHUNK:, :] = acc_u[:cut, :].astype(BF16)
            dw_ref[1, 0, :cut, :] = acc_u[cut:, :].astype(BF16)
            dw_ref[1, 0, cut:, :] = acc_d[...].astype(BF16)

    tok = pl.BlockSpec((tile, D_MODEL), lambda j, i, chunks_ref: (i, 0))
    hid = pl.BlockSpec((1, tile, FF_CHUNK), lambda j, i, chunks_ref: (chunks_ref[j], i, 0))
    return _call(
        body, name, (n_chunks, nt),
        [tok, tok, hid, hid]
        + [pl.BlockSpec((1, FF_CHUNK, D_MODEL), functools.partial(lambda j, i, chunks_ref, part: (chunks_ref[j], part, 0), part=part))
           for part in range(3)],
        [pl.BlockSpec((1, tile, D_MODEL), lambda j, i, chunks_ref: (j, i, 0)),
         pl.BlockSpec((2, 1, half_rows, D_MODEL), lambda j, i, chunks_ref: (0, j, 0, 0))],
        [jax.ShapeDtypeStruct((n_chunks, S, D_MODEL), BF16), jax.ShapeDtypeStruct((2, n_chunks, half_rows, D_MODEL), BF16)],
        [pltpu.VMEM((FF_CHUNK, D_MODEL), F32)] * 3, 56, (d_out, n, gate, up, group, group, group), rider, (chunks,))


def _mix_in(h, gain, w_in, rc, rs1, rs2, name):
    S = h.shape[0]
    tile = min(TOK_TILE, S)

    def body(h_ref, g_ref, w_ref, c_ref, s1_ref, s2_ref, n_ref, q_ref, k_ref, v_ref, pc_ref):
        _, xh = _rms_stats(h_ref[...])
        n = (xh * g_ref[...]).astype(BF16)
        n_ref[...] = n
        u = _dot_nt(n, w_ref[...])
        c, s1, s2 = c_ref[...], s1_ref[...], s2_ref[...]
        q_ref[...] = jnp.concatenate([(_rope(u[:, hd * LANES:(hd + 1) * LANES], c, s1, s2) * SCALE).astype(BF16)
                                      for hd in range(N_HEADS)], axis=1)
        k_ref[...] = _rope(u[:, Q_PAD:Q_PAD + KV_WIDTH], c, s1, s2).astype(BF16)
        v_ref[...] = u[:, Q_PAD + KV_WIDTH:Q_PAD + 2 * KV_WIDTH].astype(BF16)
        pc_ref[...] = u[:, Q_PAD + 2 * KV_WIDTH:]

    return pl.pallas_call(
        body, name=name, grid=(S // tile,),
        in_specs=[_rows(tile, D_MODEL), _full((1, D_MODEL)), _full((U_PAD, D_MODEL)),
                  _rows(tile, LANES), _rows(tile, LANES), _rows(tile, LANES)],
        out_specs=[_rows(tile, D_MODEL), _rows(tile, Q_PAD), _rows(tile, KV_WIDTH), _rows(tile, KV_WIDTH),
                   _rows(tile, POOL_WIDTH)],
        out_shape=[jax.ShapeDtypeStruct((S, D_MODEL), BF16), jax.ShapeDtypeStruct((S, Q_PAD), BF16),
                   jax.ShapeDtypeStruct((S, KV_WIDTH), BF16), jax.ShapeDtypeStruct((S, KV_WIDTH), BF16),
                   jax.ShapeDtypeStruct((S, POOL_WIDTH), F32)],
        compiler_params=_params(("parallel",), 40),
    )(h, gain, w_in, rc, rs1, rs2)


def _band_start(i, S):
    return pl.multiple_of(jnp.clip((i - 1) * BLOCK, 0, S - BAND), BLOCK)


def _window_bias(off):
    r = lax.broadcasted_iota(jnp.int32, (BLOCK, 1), 0)
    c = lax.broadcasted_iota(jnp.int32, (1, BAND), 1)
    return jnp.where(jnp.abs(off + r - c) <= WINDOW, 0.0, NEG).astype(F32)


def _softmax_parts(qh, kb, bias, sink_h):
    s = _dot_nt(qh, kb) + bias
    m = jnp.maximum(jnp.max(s, axis=-1, keepdims=True), sink_h)
    p = jnp.exp(s - m)
    es = jnp.exp(sink_h - m)
    return p, es, 1.0 / (jnp.sum(p, axis=-1, keepdims=True) + es)


def _pool_matrix(t0, start, S, w):
    r = lax.broadcasted_iota(jnp.int32, (BLOCK, 1), 0) + t0
    c = lax.broadcasted_iota(jnp.int32, (1, BAND), 1) + start
    half = w // 2

    def window(lo, hi):
        a = jnp.maximum(lo, 0)
        b = jnp.minimum(hi + 1, S)
        return jnp.where((c >= a) & (c < b), 1.0 / (b - a).astype(F32), 0.0)

    return (0.5 * (window(r - half, r + half - 1) + window(r - half + 1, r + half))).astype(BF16)


def _pool_matrices(S):
    blocks = ((0, 0), (BLOCK, 0), (S - BLOCK, S - BAND))
    return jnp.stack([jnp.stack([_pool_matrix(t0, start, S, w) for w in POOL_WINDOWS]) for t0, start in blocks])


def _pool_spec(nb):
    return pl.BlockSpec((1, N_POOL, BLOCK, BAND), lambda i, *_: (jnp.where(i == 0, 0, jnp.where(i == nb - 1, 2, 1)), 0, 0, 0))


def _mix_core_fwd(q, k, v, pc, sink, pool_m, pool_w, pool_scale, name, rider=None):
    S = q.shape[0]
    nb = S // BLOCK

    def body(sink_ref, q_ref, k_ref, v_ref, pc_ref, pm_ref, pw_ref, ps_ref, a_ref, p_ref):
        i = pl.program_id(0)
        start = _band_start(i, S)
        band = pl.ds(start, BAND)
        bias = _window_bias(i * BLOCK - start)
        kb, vb = k_ref[band, :], v_ref[band, :]
        hs = range(N_HEADS)
        ss = [_dot_nt(q_ref[:, hd * LANES:(hd + 1) * LANES], kb) + bias for hd in hs]
        ms = [jnp.maximum(jnp.max(ss[hd], axis=-1, keepdims=True), sink_ref[0, hd]) for hd in hs]
        ps = [jnp.exp(ss[hd] - ms[hd]) for hd in hs]
        invs = [1.0 / (jnp.sum(ps[hd], axis=-1, keepdims=True) + jnp.exp(sink_ref[0, hd] - ms[hd])) for hd in hs]
        outs = [_dot(ps[hd].astype(BF16), vb) for hd in hs]
        a_ref[...] = jnp.concatenate([(outs[hd] * invs[hd]).astype(BF16) for hd in hs], axis=1)
        centre = pl.ds(pl.multiple_of(i * BLOCK, BLOCK), BLOCK)
        gs = range(N_POOL)
        sl = [slice(g * POOL_GROUP, (g + 1) * POOL_GROUP) for g in gs]
        means = [_dot(pm_ref[0, g], pc_ref[band, sl[g]].astype(BF16)) for g in gs]
        devs = [(means[g] - pc_ref[centre, sl[g]]).astype(BF16) for g in gs]
        p_ref[...] = (jnp.concatenate([_dot(devs[g], pw_ref[g]) for g in gs], axis=1) * ps_ref[...]).astype(BF16)

    return _call(
        body, name, (nb,),
        [pl.BlockSpec(memory_space=pltpu.SMEM), _rows(BLOCK, Q_PAD), _full((S, KV_WIDTH)), _full((S, KV_WIDTH)),
         _full((S, POOL_WIDTH)), _pool_spec(nb), _full((N_POOL, POOL_GROUP, POOL_GROUP)), _full((1, POOL_WIDTH))],
        [_rows(BLOCK, Q_PAD), _rows(BLOCK, POOL_WIDTH)],
        [jax.ShapeDtypeStruct((S, Q_PAD), BF16), jax.ShapeDtypeStruct((S, POOL_WIDTH), BF16)],
        [], 40, (sink, q, k, v, pc, pool_m, pool_w, pool_scale), rider)


def _mix_core_bwd(q, k, v, pc, da, dp, sink, pool_m, pool_w, pool_scale, rc, rs1, rs2, name, rider=None):
    S = q.shape[0]
    nb = S // BLOCK

    def body(sink_ref, q_ref, k_ref, v_ref, pc_ref, da_ref, dp_ref, pm_ref, pw_ref, ps_ref, c_ref, s1_ref, s2_ref,
             dq_ref, dk_ref, dv_ref, dpc_ref, dsink_ref, dpw_ref, dps_ref):
        i = pl.program_id(0)

        @pl.when(i == 0)
        def _():
            dk_ref[...] = jnp.zeros_like(dk_ref)
            dv_ref[...] = jnp.zeros_like(dv_ref)
            dpc_ref[...] = jnp.zeros_like(dpc_ref)
            dsink_ref[...] = jnp.zeros_like(dsink_ref)
            dpw_ref[...] = jnp.zeros_like(dpw_ref)
            dps_ref[...] = jnp.zeros_like(dps_ref)

        start = _band_start(i, S)
        band = pl.ds(start, BAND)
        bias = _window_bias(i * BLOCK - start)
        kb, vb = k_ref[band, :], v_ref[band, :]
        c, s1, s2 = c_ref[...], s1_ref[...], s2_ref[...]
        lane = lax.broadcasted_iota(jnp.int32, (1, LANES), 1)
        hs = range(N_HEADS)
        qs = [q_ref[:, hd * LANES:(hd + 1) * LANES] for hd in hs]
        das = [da_ref[:, hd * LANES:(hd + 1) * LANES] for hd in hs]
        ss = [_dot_nt(qs[hd], kb) + bias for hd in hs]
        d_probs = [_dot_nt(das[hd], vb) for hd in hs]
        ms = [jnp.maximum(jnp.max(ss[hd], axis=-1, keepdims=True), sink_ref[0, hd]) for hd in hs]
        ps = [jnp.exp(ss[hd] - ms[hd]) for hd in hs]
        ess = [jnp.exp(sink_ref[0, hd] - ms[hd]) for hd in hs]
        invs = [1.0 / (jnp.sum(ps[hd], axis=-1, keepdims=True) + ess[hd]) for hd in hs]
        probs = [ps[hd] * invs[hd] for hd in hs]
        deltas = [jnp.sum(probs[hd] * d_probs[hd], axis=-1, keepdims=True) for hd in hs]
        d_ss = [(probs[hd] * (d_probs[hd] - deltas[hd])).astype(BF16) for hd in hs]
        dqs = [_dot(d_ss[hd], kb) for hd in hs]
        dq_ref[...] = jnp.concatenate([_rope_bwd(dqs[hd] * SCALE, c, s1, s2).astype(BF16) for hd in hs], axis=1)
        dks = [_dot_tn(d_ss[hd], qs[hd]) for hd in hs]
        dvs = [_dot_tn(probs[hd].astype(BF16), das[hd]) for hd in hs]
        dk_ref[band, :] += functools.reduce(lambda a, b: a + b, dks)
        dv_ref[band, :] += functools.reduce(lambda a, b: a + b, dvs)
        dsink_ref[...] += functools.reduce(lambda a, b: a + b, [
            jnp.where(lane == hd, -jnp.sum(ess[hd] * invs[hd] * deltas[hd], axis=0, keepdims=True), 0.0) for hd in hs])

        centre = pl.ds(pl.multiple_of(i * BLOCK, BLOCK), BLOCK)
        gs = range(N_POOL)
        sl = [slice(g * POOL_GROUP, (g + 1) * POOL_GROUP) for g in gs]
        devs = [(_dot(pm_ref[0, g], pc_ref[band, sl[g]].astype(BF16)) - pc_ref[centre, sl[g]]).astype(BF16) for g in gs]
        dys = [dp_ref[:, sl[g]].astype(F32) for g in gs]
        zs = [_dot(devs[g], pw_ref[g]) for g in gs]
        dzs = [(dys[g] * ps_ref[:, sl[g]]).astype(BF16) for g in gs]
        d_devs = [_dot_nt(dzs[g], pw_ref[g]) for g in gs]
        dps_ref[...] += jnp.concatenate([jnp.sum(dys[g] * zs[g], axis=0, keepdims=True) for g in gs], axis=1)
        for g in gs:
            dpw_ref[g] += _dot_tn(devs[g], dzs[g])
        dpc_ref[band, :] += jnp.concatenate([_dot_tn(pm_ref[0, g], d_devs[g].astype(BF16)) for g in gs], axis=1)
        dpc_ref[centre, :] -= jnp.concatenate(d_devs, axis=1)

    return _call(
        body, name, (nb,),
        [pl.BlockSpec(memory_space=pltpu.SMEM), _rows(BLOCK, Q_PAD), _full((S, KV_WIDTH)), _full((S, KV_WIDTH)),
         _full((S, POOL_WIDTH)), _rows(BLOCK, Q_PAD), _rows(BLOCK, POOL_WIDTH), _pool_spec(nb),
         _full((N_POOL, POOL_GROUP, POOL_GROUP)), _full((1, POOL_WIDTH)),
         _rows(BLOCK, LANES), _rows(BLOCK, LANES), _rows(BLOCK, LANES)],
        [_rows(BLOCK, Q_PAD), _full((S, KV_WIDTH)), _full((S, KV_WIDTH)), _full((S, POOL_WIDTH)),
         _full((1, LANES)), _full((N_POOL, POOL_GROUP, POOL_GROUP)), _full((1, POOL_WIDTH))],
        [jax.ShapeDtypeStruct((S, Q_PAD), BF16), jax.ShapeDtypeStruct((S, KV_WIDTH), F32),
         jax.ShapeDtypeStruct((S, KV_WIDTH), F32), jax.ShapeDtypeStruct((S, POOL_WIDTH), F32),
         jax.ShapeDtypeStruct((1, LANES), F32), jax.ShapeDtypeStruct((N_POOL, POOL_GROUP, POOL_GROUP), F32),
         jax.ShapeDtypeStruct((1, POOL_WIDTH), F32)],
        [], 56, (sink, q, k, v, pc, da, dp, pool_m, pool_w, pool_scale, rc, rs1, rs2), rider)


def _mix_out(h, a, p, wa, wp, name):
    S = h.shape[0]
    tile = min(TOK_TILE, S)

    def body(h_ref, a_ref, p_ref, wa_ref, wp_ref, o_ref):
        o_ref[...] = h_ref[...] + _dot(a_ref[...], wa_ref[...]) + _dot(p_ref[...], wp_ref[...])

    return pl.pallas_call(
        body, name=name, grid=(S // tile,),
        in_specs=[_rows(tile, D_MODEL), _rows(tile, Q_PAD), _rows(tile, POOL_WIDTH),
                  _full((Q_PAD, D_MODEL)), _full((POOL_WIDTH, D_MODEL))],
        out_specs=_rows(tile, D_MODEL),
        out_shape=jax.ShapeDtypeStruct((S, D_MODEL), F32),
        compiler_params=_params(("parallel",), 40),
    )(h, a, p, wa, wp)


def _loss_head(h, target, gain, name):
    S = h.shape[0]
    tile = min(TOK_TILE, S)

    def body(h_ref, t_ref, g_ref, dh_ref, dhalf_ref, loss_ref, dg_ref):
        @pl.when(pl.program_id(0) == 0)
        def _():
            loss_ref[...] = jnp.zeros_like(loss_ref)
            dg_ref[...] = jnp.zeros_like(dg_ref)

        g = g_ref[...]
        r, xh = _rms_stats(h_ref[...])
        err = xh * g - t_ref[...]
        loss_ref[...] += (0.5 / D_MODEL) * jnp.sum(err * err, axis=0, keepdims=True)
        dh, dg = _rms_bwd(err * (1.0 / D_MODEL), g, r, xh)
        dg_ref[...] += dg
        dh_ref[...] = dh
        dhalf_ref[...] = (0.5 * dh).astype(BF16)

    return pl.pallas_call(
        body, name=name, grid=(S // tile,),
        in_specs=[_rows(tile, D_MODEL), _rows(tile, D_MODEL), _full((1, D_MODEL))],
        out_specs=[_rows(tile, D_MODEL), _rows(tile, D_MODEL), _full((1, D_MODEL)), _full((1, D_MODEL))],
        out_shape=[jax.ShapeDtypeStruct((S, D_MODEL), F32), jax.ShapeDtypeStruct((S, D_MODEL), BF16),
                   jax.ShapeDtypeStruct((1, D_MODEL), F32), jax.ShapeDtypeStruct((1, D_MODEL), F32)],
        compiler_params=_params(("arbitrary",), 40),
    )(h, target, gain)


def _mix_out_bwd(dh_out, dn, h, gain, a, p, wa, wp, name, rider=None):
    S = h.shape[0]
    tile = min(TOK_TILE, S)

    def body(do_ref, dn_ref, h_ref, g_ref, a_ref, p_ref, wa_ref, wp_ref, dh_ref, da_ref, dp_ref, dwa_ref, dwp_ref, dg_ref):
        @pl.when(pl.program_id(0) == 0)
        def _():
            dwa_ref[...] = jnp.zeros_like(dwa_ref)
            dwp_ref[...] = jnp.zeros_like(dwp_ref)
            dg_ref[...] = jnp.zeros_like(dg_ref)

        r, xh = _rms_stats(h_ref[...])
        dnorm, dg = _rms_bwd(_sum_chunks([dn_ref]), g_ref[...], r, xh)
        dh = do_ref[...] + dnorm
        dg_ref[...] += dg
        dh_ref[...] = dh
        dhb = dh.astype(BF16)
        da_ref[...] = _dot_nt(dhb, wa_ref[...]).astype(BF16)
        dp_ref[...] = _dot_nt(dhb, wp_ref[...]).astype(BF16)
        dwa_ref[...] += _dot_tn(a_ref[...], dhb)
        dwp_ref[...] += _dot_tn(p_ref[...], dhb)

    return _call(
        body, name, (S // tile,),
        [_rows(tile, D_MODEL), _chunk_rows(tile, dn.shape[0]), _rows(tile, D_MODEL), _full((1, D_MODEL)),
         _rows(tile, Q_PAD), _rows(tile, POOL_WIDTH), _full((Q_PAD, D_MODEL)), _full((POOL_WIDTH, D_MODEL))],
        [_rows(tile, D_MODEL), _rows(tile, Q_PAD), _rows(tile, POOL_WIDTH),
         _full((Q_PAD, D_MODEL)), _full((POOL_WIDTH, D_MODEL)), _full((1, D_MODEL))],
        [jax.ShapeDtypeStruct((S, D_MODEL), F32), jax.ShapeDtypeStruct((S, Q_PAD), BF16),
         jax.ShapeDtypeStruct((S, POOL_WIDTH), BF16), jax.ShapeDtypeStruct((Q_PAD, D_MODEL), F32),
         jax.ShapeDtypeStruct((POOL_WIDTH, D_MODEL), F32), jax.ShapeDtypeStruct((1, D_MODEL), F32)],
        [], 48, (dh_out, dn, h, gain, a, p, wa, wp), rider)


def _mix_in_bwd(dh_out, h, gain, n, dq, dk, dv, dpc, rc, rs1, rs2, w_in, name, rider=None):
    S = h.shape[0]
    tile = min(TOK_TILE, S)

    def body(do_ref, h_ref, g_ref, n_ref, dq_ref, dk_ref, dv_ref, dpc_ref, c_ref, s1_ref, s2_ref, w_ref,
             dh_ref, dhalf_ref, dw_ref, dg_ref):
        @pl.when(pl.program_id(0) == 0)
        def _():
            dw_ref[...] = jnp.zeros_like(dw_ref)
            dg_ref[...] = jnp.zeros_like(dg_ref)

        dk = _rope_bwd(dk_ref[...], c_ref[...], s1_ref[...], s2_ref[...]).astype(BF16)
        du = jnp.concatenate([dq_ref[...], dk, dv_ref[...].astype(BF16), dpc_ref[...].astype(BF16)], axis=1)
        dn = _dot(du, w_ref[...])
        dw_ref[...] += _dot_tn(du, n_ref[...])
        r, xh = _rms_stats(h_ref[...])
        dnorm, dg = _rms_bwd(dn, g_ref[...], r, xh)
        dh = do_ref[...] + dnorm
        dg_ref[...] += dg
        dh_ref[...] = dh
        dhalf_ref[...] = (0.5 * dh).astype(BF16)

    return _call(
        body, name, (S // tile,),
        [_rows(tile, D_MODEL), _rows(tile, D_MODEL), _full((1, D_MODEL)), _rows(tile, D_MODEL),
         _rows(tile, Q_PAD), _rows(tile, KV_WIDTH), _rows(tile, KV_WIDTH), _rows(tile, POOL_WIDTH),
         _rows(tile, LANES), _rows(tile, LANES), _rows(tile, LANES), _full((U_PAD, D_MODEL))],
        [_rows(tile, D_MODEL), _rows(tile, D_MODEL), _full((U_PAD, D_MODEL)), _full((1, D_MODEL))],
        [jax.ShapeDtypeStruct((S, D_MODEL), F32), jax.ShapeDtypeStruct((S, D_MODEL), BF16),
         jax.ShapeDtypeStruct((U_PAD, D_MODEL), F32), jax.ShapeDtypeStruct((1, D_MODEL), F32)],
        [], 56, (dh_out, h, gain, n, dq, dk, dv, dpc, rc, rs1, rs2, w_in), rider)


def _norm_bwd(dh_out, dns, h, gain, name):
    S = h.shape[0]
    tile = min(TOK_TILE, S)
    n = len(dns)

    def body(do_ref, *refs):
        h_ref, g_ref, dh_ref, dg_ref = refs[n:]

        @pl.when(pl.program_id(0) == 0)
        def _():
            dg_ref[...] = jnp.zeros_like(dg_ref)

        r, xh = _rms_stats(h_ref[...])
        dnorm, dg = _rms_bwd(_sum_chunks(refs[:n]), g_ref[...], r, xh)
        dg_ref[...] += dg
        dh_ref[...] = do_ref[...] + dnorm

    return _call(
        body, name, (S // tile,),
        [_rows(tile, D_MODEL)] + [_chunk_rows(tile, dn.shape[0]) for dn in dns] + [_rows(tile, D_MODEL), _full((1, D_MODEL))],
        [_rows(tile, D_MODEL), _full((1, D_MODEL))],
        [jax.ShapeDtypeStruct((S, D_MODEL), F32), jax.ShapeDtypeStruct((1, D_MODEL), F32)],
        [], 40, (dh_out, *dns, h, gain))


def _rope_tables(S):
    half = ROTARY_DIM // 2
    inv_freq = ROPE_THETA ** (-jnp.arange(0, ROTARY_DIM, 2, dtype=F32) / ROTARY_DIM)
    dim = jnp.arange(LANES) % HEAD_DIM
    ang = jnp.arange(S, dtype=F32)[:, None] * inv_freq[dim % half][None, :]
    lo, hi = (dim < half)[None, :], ((dim >= half) & (dim < ROTARY_DIM))[None, :]
    c = jnp.where(lo | hi, jnp.cos(ang), 1.0)
    s1 = jnp.where(lo, -jnp.sin(ang), 0.0)
    s2 = jnp.where(hi, jnp.sin(ang), 0.0)
    return c, s1, s2


def _pad_heads(w, axis):
    w = jnp.moveaxis(w, axis, 0)
    heads = w.reshape((N_HEADS, HEAD_DIM) + w.shape[1:])
    zero = jnp.zeros_like(heads)
    first = (jnp.arange(N_HEADS) < Q_PER_KV).reshape((N_HEADS, 1) + (1,) * (w.ndim - 1))
    lo = jnp.where(first, heads, zero)
    hi = jnp.where(first, zero, heads)
    padded = jnp.concatenate([lo, hi], axis=1).reshape((Q_PAD,) + w.shape[1:])
    return jnp.moveaxis(padded, 0, axis)


def _unpad_heads(w, axis):
    w = jnp.moveaxis(w, axis, 0)
    groups = w.reshape((N_HEADS, 2, HEAD_DIM) + w.shape[1:])
    first = (jnp.arange(N_HEADS) < Q_PER_KV).reshape((N_HEADS, 1) + (1,) * (w.ndim - 1))
    heads = jnp.where(first, groups[:, 0], groups[:, 1]).reshape((ATTN_WIDTH,) + w.shape[1:])
    return jnp.moveaxis(heads, 0, axis)


IN_ROWS = IN_WIDTH // N_CHIPS
OUT_ROWS = (ATTN_WIDTH + POOL_WIDTH) // N_CHIPS
MIX_ROWS = IN_ROWS + OUT_ROWS
FFN_ROWS = 3 * FF_CHUNK


def _step(x, target, bufs, small, place):
    S = x.shape[0]
    rc, rs1, rs2 = _rope_tables(S)
    (ffn1,) = _comm_call(_allgather(bufs[:1]), "allgather_ffn1")
    ffn1 = ffn1.reshape(N_CHIPS, FFN_ROWS, D_MODEL)
    h1, n1, gate1, up1, mix, ffn2 = _ffn_fwd(x, small["ffn1_norm"], ffn1, "ffn1_fwd",
                                             _merge(_allgather(bufs[1:2]), _allgather(bufs[2:], peers=X_Y_DIAGONAL[:2])))
    mix = mix.reshape(N_CHIPS, MIX_ROWS, D_MODEL)
    w_in_t = mix[:, :IN_ROWS].reshape(IN_WIDTH, D_MODEL)
    w_in_pad = jnp.concatenate([_pad_heads(w_in_t[:ATTN_WIDTH], 0), w_in_t[ATTN_WIDTH:]], axis=0)
    w_out = mix[:, IN_ROWS:].reshape(ATTN_WIDTH + POOL_WIDTH, D_MODEL)
    wa = _pad_heads(w_out[:ATTN_WIDTH], 0)
    wp = w_out[ATTN_WIDTH:]
    pool_w = small["pool_w"].astype(BF16)

    n2, q, k, v, pc = _mix_in(h1, small["mix_norm"], w_in_pad, rc, rs1, rs2, "mix_in")
    pool_m = _pool_matrices(S)
    a, p, ffn2 = _mix_core_fwd(q, k, v, pc, small["sink_logits"], pool_m, pool_w, small["pool_scale"], "mix_core_fwd",
                               _allgather([ffn2], peers=X_Y_DIAGONAL[2:]))
    ffn2 = ffn2.reshape(N_CHIPS, FFN_ROWS, D_MODEL)
    h2 = _mix_out(h1, a, p, wa, wp, "mix_out")
    h3, n3, gate2, up2 = _ffn_fwd(h2, small["ffn2_norm"], ffn2, "ffn2_fwd")
    dh3, dhalf3, loss_lanes, d_final = _loss_head(h3, target, small["final_norm"], "loss_head")

    dn3, d_ffn2 = _ffn_bwd(jnp.arange(N_CHIPS, dtype=jnp.int32), dhalf3, n3, gate2, up2, ffn2, "ffn2_bwd")
    dh2, da, dp, dwa, dwp, d_ffn2_norm, received = _mix_out_bwd(dh3, dn3, h2, small["ffn2_norm"], a, p, wa, wp, "mix_out_bwd",
                                                                _sibling_exchange([d_ffn2]))
    pair = _pair_sum(place, d_ffn2, received, "grad_pair_sum_ffn2")
    dq, dk, dv, dpc, dsink, dpool_w, dpool_scale, stack = _mix_core_bwd(
        q, k, v, pc, da, dp, small["sink_logits"], pool_m, pool_w, small["pool_scale"], rc, rs1, rs2, "mix_core_bwd",
        _scatter([pair], peers=X_Y_DIAGONAL[:2]))
    dh1, dhalf1, dw_in_pad, d_mix_norm, stack = _mix_in_bwd(dh2, h1, small["mix_norm"], n2, dq, dk, dv, dpc, rc, rs1, rs2,
                                                            w_in_pad, "mix_in_bwd",
                                                            _scatter([pair], peers=X_Y_DIAGONAL[2:], stacks=[stack]))
    reduced_ffn2 = _chip_sum(place, pair, stack, 2, "grad_chip_sum_ffn2")
    dw_in_t = jnp.concatenate([_unpad_heads(dw_in_pad[:Q_PAD], 0), dw_in_pad[Q_PAD:]], axis=0)
    dw_out = jnp.concatenate([_unpad_heads(dwa, 0), dwp], axis=0)
    d_mix = jnp.concatenate([dw_in_t.reshape(N_CHIPS, IN_ROWS, D_MODEL), dw_out.reshape(N_CHIPS, OUT_ROWS, D_MODEL)], axis=1)
    d_mix = jnp.transpose(d_mix.reshape(N_CHIPS, 2, MIX_ROWS // 2, D_MODEL), (1, 0, 2, 3)).astype(BF16)
    small_g = {"ffn1_norm": jnp.zeros_like(d_mix_norm), "mix_norm": d_mix_norm, "ffn2_norm": d_ffn2_norm,
               "final_norm": d_final, "pool_scale": dpool_scale, "sink_logits": dsink[:, :N_HEADS], "pool_w": dpool_w}
    loss_row = jnp.sum(loss_lanes.reshape(D_MODEL // LANES, LANES), axis=0, keepdims=True)
    small_early = _pack_small(small_g, loss_row)

    chunk = [(place[1:] + 1 + p) % N_CHIPS for p in range(N_CHIPS)]
    ffn1_bwd = functools.partial(_ffn_bwd, d_out=dhalf1, n=n1, gate=gate1, up=up1, group=ffn1)
    stack = jnp.zeros((N_CHIPS, FFN_ROWS // 2, D_MODEL), BF16)
    dn_a, dw_a, recv_mix, small_all, g_ffn2 = ffn1_bwd(
        chunk[0], name="ffn1_bwd_0",
        rider=_merge(_merge(_sibling_exchange([d_mix]), _small_allgather(small_early)), _sibling_share([reduced_ffn2])))
    pair_mix = _pair_sum(place, d_mix, recv_mix, "grad_pair_sum_mix")
    dn_b, dw_b, recv_a, stack_mix = ffn1_bwd(chunk[1], name="ffn1_bwd_1",
                                             rider=_merge(_sibling_exchange([dw_a]), _scatter([pair_mix])))
    pair_a = _pair_sum(place, dw_a, recv_a, "grad_pair_sum_ffn1_0")
    reduced_mix = _chip_sum(place, pair_mix, stack_mix, 2, "grad_chip_sum_mix")
    dn_c, dw_c, recv_b, stack, g_mix = ffn1_bwd(
        chunk[2], name="ffn1_bwd_2",
        rider=_merge(_merge(_sibling_exchange([dw_b]), _scatter_step(pair_a, stack, 0)), _sibling_share([reduced_mix])))
    pair_b = _pair_sum(place, dw_b, recv_b, "grad_pair_sum_ffn1_1")
    dn_d, dw_d, recv_c, stack = ffn1_bwd(chunk[3], name="ffn1_bwd_3",
                                         rider=_merge(_sibling_exchange([dw_c]), _scatter_step(pair_b, stack, 1)))
    pair_c = _pair_sum(place, dw_c, recv_c, "grad_pair_sum_ffn1_2")
    grad_x, d_ffn1_norm = _norm_bwd(dh1, [dn_a, dn_b, dn_c, dn_d], x, small["ffn1_norm"], "norm1_bwd")

    recv_d, stack, gains = _comm_call(
        _merge(_merge(_sibling_exchange([dw_d]), _scatter_step(pair_c, stack, 2)),
               _small_allgather(d_ffn1_norm.reshape(-1, LANES))), "grad_tail")
    pair_d = _pair_sum(place, dw_d, recv_d, "grad_pair_sum_ffn1_3")
    reduced_ffn1 = _chip_sum(place, pair_d, stack, 2, "grad_chip_sum_ffn1")
    (g_ffn1,) = _comm_call(_sibling_share([reduced_ffn1]), "grad_share_tail")
    gain_sum = _sum_leading(gains, 1, "gain_grad_sum")
    small_sum = jnp.concatenate([gain_sum, _sum_leading(small_all, 1, "small_grad_sum")[gain_sum.shape[0]:]], axis=0)
    return jnp.sum(small_sum[SMALL_ROWS - 1]), grad_x, [g.reshape(-1, D_MODEL) for g in (g_ffn1, g_mix, g_ffn2)], small_sum


GROUPS =(("ffn1_w_gate", "ffn1_w_up", "ffn1_w_down"), ("w_in", "w_out"), ("ffn2_w_gate", "ffn2_w_up", "ffn2_w_down"))
TRANSPOSED = ("ffn1_w_gate", "ffn1_w_up", "w_in", "ffn2_w_gate", "ffn2_w_up")


def _place():
    x, y, c = lax.axis_index("x"), lax.axis_index("y"), lax.axis_index("c")
    chips = [(1 - x, y), (x, 1 - y), (1 - x, 1 - y)]
    return x, y, c, chips


def _remote(src, dst, send_sem, recv_sem, to):
    return pltpu.make_async_remote_copy(src_ref=src, dst_ref=dst, send_sem=send_sem, recv_sem=recv_sem,
                                        device_id=to, device_id_type=MESH)


def _pack(chip, members, name):
    rows = members[0].shape[0]
    n = len(members)

    def body(chip_ref, *refs):
        ins, out_ref, buf, sems = refs[:n], refs[n], refs[n + 1], refs[n + 2]
        copies = [pltpu.make_async_copy(ins[k], buf.at[k], sems.at[k]) for k in range(n)]
        for cp in copies:
            cp.start()
        for k in range(n):
            copies[k].wait()
            out_ref[0, k * rows:(k + 1) * rows, :] = buf[k].astype(BF16)

    return pl.pallas_call(
        body, name=name,
        grid_spec=pltpu.PrefetchScalarGridSpec(
            num_scalar_prefetch=1, grid=(1,),
            in_specs=[HBM_SPEC] * n,
            out_specs=pl.BlockSpec((1, n * rows, D_MODEL), lambda k, chip_ref: (chip_ref[0], 0, 0)),
            scratch_shapes=[pltpu.VMEM((n, rows, D_MODEL), F32), pltpu.SemaphoreType.DMA((n,))]),
        out_shape=jax.ShapeDtypeStruct((N_CHIPS, n * rows, D_MODEL), BF16),
        compiler_params=_params(("arbitrary",), 40),
    )(chip, *members)


def _same(arrays):
    return [jax.ShapeDtypeStruct(a.shape, a.dtype) for a in arrays]


X_Y_DIAGONAL = (0, 1, 2)


def _allgather(bufs, peers=X_Y_DIAGONAL):
    n = len(bufs)

    def copies(outs, send_sems, recv_sems, started_only=False):
        x, y, c, chips = _place()
        me = 2 * x + y
        slots = [2 * cx + cy for cx, cy in chips]
        first = [[_remote(outs[a].at[me, c], outs[a].at[me, c], send_sems.at[6 * a + k], recv_sems.at[6 * a + k], (*chips[k], c))
                  for k in peers] for a in range(n)]
        if started_only:
            return first
        passed = [[_remote(outs[a].at[slots[k], c], outs[a].at[slots[k], c], send_sems.at[6 * a + 3 + k],
                           recv_sems.at[6 * a + 3 + k], (x, y, 1 - c)) for k in peers] for a in range(n)]
        landed = [[_remote(outs[a].at[me, c], outs[a].at[slots[k], c], send_sems.at[6 * a + k], recv_sems.at[6 * a + k],
                           (*chips[k], c)) for k in peers] for a in range(n)]
        handed = [[_remote(outs[a].at[me, c], outs[a].at[slots[k], 1 - c], send_sems.at[6 * a + 3 + k],
                           recv_sems.at[6 * a + 3 + k], (x, y, 1 - c)) for k in peers] for a in range(n)]
        return first, passed, landed, handed

    def start(ins, outs, sems):
        for per_buf in copies(outs, *sems, started_only=True):
            for cp in per_buf:
                cp.start()

    def finish(ins, outs, sems):
        first, passed, landed, handed = copies(outs, *sems)
        for a in range(n):
            for lan, pas in zip(landed[a], passed[a]):
                lan.wait_recv()
                pas.start()
        for a in range(n):
            for han in handed[a]:
                han.wait_recv()
        for a in range(n):
            for cp in first[a] + passed[a]:
                cp.wait_send()

    return _Rider(list(bufs), _same(bufs), {a: a for a in range(n)},
                  [pltpu.SemaphoreType.DMA((6 * n,)), pltpu.SemaphoreType.DMA((6 * n,))], start, finish)


def _sibling_exchange(parts):
    n = len(parts)

    def copies(ins, outs, send_sems, recv_sems):
        x, y, c, _ = _place()
        return [_remote(ins[a].at[1 - c], outs[a], send_sems.at[a], recv_sems.at[a], (x, y, 1 - c)) for a in range(n)]

    def start(ins, outs, sems):
        for cp in copies(ins, outs, *sems):
            cp.start()

    def finish(ins, outs, sems):
        for cp in copies(ins, outs, *sems):
            cp.wait_recv()
            cp.wait_send()

    return _Rider(list(parts), [jax.ShapeDtypeStruct(p.shape[1:], p.dtype) for p in parts], {},
                  [pltpu.SemaphoreType.DMA((n,)), pltpu.SemaphoreType.DMA((n,))], start, finish)


def _small_allgather(small):
    flips = [(fx, fy, fc) for fx in range(2) for fy in range(2) for fc in range(2)][1:]

    def copies(small_ref, gather_ref, send_sems, recv_sems, local_sem, started_only=False):
        x, y, c, _ = _place()
        me = 4 * x + 2 * y + c
        peers = [((1 - x) if fx else x, (1 - y) if fy else y, (1 - c) if fc else c) for fx, fy, fc in flips]
        own = pltpu.make_async_copy(small_ref, gather_ref.at[me], local_sem)
        sent = [_remote(small_ref, gather_ref.at[me], send_sems.at[k], recv_sems.at[k], peer) for k, peer in enumerate(peers)]
        if started_only:
            return own, sent
        landed = [_remote(small_ref, gather_ref.at[4 * px + 2 * py + pc], send_sems.at[k], recv_sems.at[k], (px, py, pc))
                  for k, (px, py, pc) in enumerate(peers)]
        return own, sent, landed

    def start(ins, outs, sems):
        own, sent = copies(ins[0], outs[0], *sems, started_only=True)
        own.start()
        for cp in sent:
            cp.start()

    def finish(ins, outs, sems):
        own, sent, landed = copies(ins[0], outs[0], *sems)
        for cp in landed:
            cp.wait_recv()
        for cp in sent:
            cp.wait_send()
        own.wait()

    return _Rider([small], [jax.ShapeDtypeStruct((2 * N_CHIPS,) + small.shape, small.dtype)], {},
                  [pltpu.SemaphoreType.DMA((7,)), pltpu.SemaphoreType.DMA((7,)), pltpu.SemaphoreType.DMA], start, finish)


def _merge(a, b):
    na, nao, nas = len(a.operands), len(a.out_shapes), len(a.scratch)

    def start(ins, outs, sems):
        a.start(ins[:na], outs[:nao], sems[:nas])
        b.start(ins[na:], outs[nao:], sems[nas:])

    def finish(ins, outs, sems):
        a.finish(ins[:na], outs[:nao], sems[:nas])
        b.finish(ins[na:], outs[nao:], sems[nas:])

    aliases = {**a.aliases, **{na + k: nao + v for k, v in b.aliases.items()}}
    return _Rider(a.operands + b.operands, a.out_shapes + b.out_shapes, aliases, a.scratch + b.scratch, start, finish)


def _scatter_step(pair, stack, step):
    def copies(pair_ref, stack_ref, send_sem, recv_sem, started_only=False):
        x, y, c, _ = _place()
        me = 2 * x + y
        to = (me + 1 + step) % N_CHIPS
        frm = (me + N_CHIPS - 1 - step) % N_CHIPS
        sent = _remote(pair_ref.at[0], stack_ref.at[me], send_sem, recv_sem, (to // 2, to % 2, c))
        if started_only:
            return sent
        landed = _remote(pair_ref.at[0], stack_ref.at[frm], send_sem, recv_sem, (frm // 2, frm % 2, c))
        return sent, landed

    def start(ins, outs, sems):
        copies(ins[0], outs[0], *sems, started_only=True).start()

    def finish(ins, outs, sems):
        sent, landed = copies(ins[0], outs[0], *sems)
        landed.wait_recv()
        sent.wait_send()

    return _Rider([pair, stack], _same([stack]), {1: 0}, [pltpu.SemaphoreType.DMA, pltpu.SemaphoreType.DMA], start, finish)


def _scatter(sums, peers=X_Y_DIAGONAL, stacks=None):
    n = len(sums)

    def copies(ins, outs, send_sems, recv_sems, started_only=False):
        x, y, c, chips = _place()
        me = 2 * x + y
        slots = [2 * cx + cy for cx, cy in chips]
        sent = [_remote(ins[a].at[slots[k]], outs[a].at[me], send_sems.at[3 * a + k], recv_sems.at[3 * a + k], (*chips[k], c))
                for a in range(n) for k in peers]
        if started_only:
            return sent
        landed = [_remote(ins[a].at[slots[k]], outs[a].at[slots[k]], send_sems.at[3 * a + k], recv_sems.at[3 * a + k],
                          (*chips[k], c)) for a in range(n) for k in peers]
        return sent, landed

    def start(ins, outs, sems):
        for cp in copies(ins, outs, *sems, started_only=True):
            cp.start()

    def finish(ins, outs, sems):
        sent, landed = copies(ins, outs, *sems)
        for cp in landed:
            cp.wait_recv()
        for cp in sent:
            cp.wait_send()

    sems = [pltpu.SemaphoreType.DMA((3 * n,)), pltpu.SemaphoreType.DMA((3 * n,))]
    if stacks is None:
        return _Rider(list(sums), _same(sums), {}, sems, start, finish)
    return _Rider(list(sums) + list(stacks), _same(sums), {n + a: a for a in range(n)}, sems, start, finish)


def _sibling_share(bufs):
    n = len(bufs)

    def copies(outs, send_sems, recv_sems, started_only=False):
        x, y, c, _ = _place()
        sent = [_remote(outs[a].at[c], outs[a].at[c], send_sems.at[a], recv_sems.at[a], (x, y, 1 - c)) for a in range(n)]
        if started_only:
            return sent
        landed = [_remote(outs[a].at[c], outs[a].at[1 - c], send_sems.at[a], recv_sems.at[a], (x, y, 1 - c)) for a in range(n)]
        return sent, landed

    def start(ins, outs, sems):
        for cp in copies(outs, *sems, started_only=True):
            cp.start()

    def finish(ins, outs, sems):
        sent, landed = copies(outs, *sems)
        for cp in landed:
            cp.wait_recv()
        for cp in sent:
            cp.wait_send()

    return _Rider(list(bufs), _same(bufs), {a: a for a in range(n)},
                  [pltpu.SemaphoreType.DMA((n,)), pltpu.SemaphoreType.DMA((n,))], start, finish)


def _pair_sum(core, part, received, name):
    _, k, rh, cols = part.shape

    def body(core_ref, p_ref, r_ref, o_ref):
        o_ref[...] = (p_ref[0].astype(F32) + r_ref[...].astype(F32)).astype(BF16)

    return pl.pallas_call(
        body, name=name,
        grid_spec=pltpu.PrefetchScalarGridSpec(
            num_scalar_prefetch=1, grid=(k,),
            in_specs=[pl.BlockSpec((1, 1, rh, cols), lambda j, core_ref: (core_ref[0], j, 0, 0)),
                      pl.BlockSpec((1, rh, cols), lambda j, core_ref: (j, 0, 0))],
            out_specs=pl.BlockSpec((1, rh, cols), lambda j, core_ref: (j, 0, 0))),
        out_shape=jax.ShapeDtypeStruct((k, rh, cols), BF16),
        compiler_params=_params(("parallel",), 32),
    )(core, part, received)


def _sum_leading(stack, steps, name):
    k, rows, cols = stack.shape
    tile = rows // steps

    def body(s_ref, o_ref):
        total = s_ref[0].astype(F32)
        for d in range(1, k):
            total = total + s_ref[d].astype(F32)
        o_ref[...] = total

    return pl.pallas_call(
        body, name=name, grid=(steps,),
        in_specs=[pl.BlockSpec((k, tile, cols), lambda i: (0, i, 0))],
        out_specs=pl.BlockSpec((tile, cols), lambda i: (i, 0)),
        out_shape=jax.ShapeDtypeStruct((rows, cols), F32),
        compiler_params=_params(("parallel",), 32),
    )(stack)


def _chip_sum(place, own, stack, steps, name):
    k, rows, cols = stack.shape
    tile = rows // steps

    def body(place_ref, own_ref, *refs):
        chip = place_ref[1]
        total = None
        for d in range(k):
            term = jnp.where(chip == d, own_ref[0], refs[d][0]).astype(F32)
            total = term if total is None else total + term
        refs[k][0] = total

    def other(d):
        return lambda i, place_ref: (jnp.where(place_ref[1] == d, (d + 1) % k, d), i, 0)

    return pl.pallas_call(
        body, name=name,
        grid_spec=pltpu.PrefetchScalarGridSpec(
            num_scalar_prefetch=1, grid=(steps,),
            in_specs=[pl.BlockSpec((1, tile, cols), lambda i, place_ref: (place_ref[1] % own.shape[0], i, 0))]
            + [pl.BlockSpec((1, tile, cols), other(d)) for d in range(k)],
            out_specs=pl.BlockSpec((1, tile, cols), lambda i, place_ref: (place_ref[0], i, 0))),
        out_shape=jax.ShapeDtypeStruct((2, rows, cols), F32),
        compiler_params=_params(("arbitrary",), 32),
    )(place, own, *([stack] * k))


def _adamw(w, g, row0, m, v, tile, name):
    rows, cols = w.shape
    first = row0 // tile
    assert rows % tile == 0 and row0 % tile == 0
    bc1 = 1.0 - ADAM_B1 ** ADAM_STEP
    bc2 = 1.0 - ADAM_B2 ** ADAM_STEP

    def body(w_ref, g_ref, m_ref, v_ref, go_ref, d_ref, mo_ref, vo_ref):
        g = g_ref[...]
        m_new = ADAM_B1 * m_ref[...] + (1.0 - ADAM_B1) * g
        v_new = ADAM_B2 * v_ref[...] + (1.0 - ADAM_B2) * (g * g)
        go_ref[...] = g
        d_ref[...] = -ADAM_LR * ((m_new / bc1) / (jnp.sqrt(v_new / bc2) + ADAM_EPS) + ADAM_WD * w_ref[...])
        mo_ref[...] = m_new
        vo_ref[...] = v_new

    spec = pl.BlockSpec((tile, cols), lambda i: (i, 0))
    g_spec = pl.BlockSpec((tile, cols), lambda i: (first + i, 0))
    return pl.pallas_call(
        body, name=name, grid=(rows // tile,),
        in_specs=[spec, g_spec, spec, spec], out_specs=[spec] * 4,
        out_shape=[jax.ShapeDtypeStruct((rows, cols), F32)] * 4,
        compiler_params=_params(("parallel",), 32),
    )(w, g, m, v)


SMALL = ("ffn1_norm", "mix_norm", "ffn2_norm", "final_norm", "pool_scale", "sink_logits", "pool_w")


def _pack_small(d, last_row=None):
    sink = jnp.pad(d["sink_logits"].reshape(1, N_HEADS), ((0, 0), (0, LANES - N_HEADS)))
    rows = [d[n].reshape(-1, LANES) for n in SMALL[:5]] + [sink, d["pool_w"].reshape(-1, LANES)]
    used = sum(r.shape[0] for r in rows)
    last = jnp.zeros((1, LANES), F32) if last_row is None else last_row
    return jnp.concatenate(rows + [jnp.zeros((SMALL_ROWS - used - 1, LANES), F32), last], axis=0)


def _unpack_small(packed, like):
    out, row = {}, 0
    for n in SMALL:
        size = LANES if n == "sink_logits" else math.prod(like[n].shape)
        chunk = packed[row:row + size // LANES].reshape(-1)
        out[n] = (chunk[:N_HEADS] if n == "sink_logits" else chunk).reshape(like[n].shape)
        row += size // LANES
    return out


def kernel(x, ffn1_norm, ffn1_w_gate, ffn1_w_up, ffn1_w_down, mix_norm, w_in, sink_logits, pool_w, pool_scale, w_out, ffn2_norm, ffn2_w_gate, ffn2_w_up, ffn2_w_down, final_norm, loss_target, m_ffn1_norm, m_ffn1_w_gate, m_ffn1_w_up, m_ffn1_w_down, m_mix_norm, m_w_in, m_sink_logits, m_pool_w, m_pool_scale, m_w_out, m_ffn2_norm, m_ffn2_w_gate, m_ffn2_w_up, m_ffn2_w_down, m_final_norm, v_ffn1_norm, v_ffn1_w_gate, v_ffn1_w_up, v_ffn1_w_down, v_mix_norm, v_w_in, v_sink_logits, v_pool_w, v_pool_scale, v_w_out, v_ffn2_norm, v_ffn2_w_gate, v_ffn2_w_up, v_ffn2_w_down, v_final_norm):
    names = ("ffn1_norm", "ffn1_w_gate", "ffn1_w_up", "ffn1_w_down", "mix_norm", "w_in", "sink_logits", "pool_w",
             "pool_scale", "w_out", "ffn2_norm", "ffn2_w_gate", "ffn2_w_up", "ffn2_w_down", "final_norm")
    weights = dict(zip(names, (ffn1_norm, ffn1_w_gate, ffn1_w_up, ffn1_w_down, mix_norm, w_in, sink_logits, pool_w,
                               pool_scale, w_out, ffn2_norm, ffn2_w_gate, ffn2_w_up, ffn2_w_down, final_norm)))
    mom1 = dict(zip(names, (m_ffn1_norm, m_ffn1_w_gate, m_ffn1_w_up, m_ffn1_w_down, m_mix_norm, m_w_in, m_sink_logits,
                            m_pool_w, m_pool_scale, m_w_out, m_ffn2_norm, m_ffn2_w_gate, m_ffn2_w_up, m_ffn2_w_down,
                            m_final_norm)))
    mom2 = dict(zip(names, (v_ffn1_norm, v_ffn1_w_gate, v_ffn1_w_up, v_ffn1_w_down, v_mix_norm, v_w_in, v_sink_logits,
                            v_pool_w, v_pool_scale, v_w_out, v_ffn2_norm, v_ffn2_w_gate, v_ffn2_w_up, v_ffn2_w_down,
                            v_final_norm)))
    chip = (2 * lax.axis_index("x") + lax.axis_index("y")).astype(jnp.int32).reshape(1)
    place = jnp.concatenate([lax.axis_index("c").astype(jnp.int32).reshape(1), chip])

    def rows_of(t, n):
        return jnp.swapaxes(t[n][0], 0, 1) if n in TRANSPOSED else t[n][0]

    packed = [_pack(chip, [rows_of(weights, n) for n in GROUPS[0]], "pack_ffn1"),
              _pack(chip, [jnp.concatenate([rows_of(weights, n) for n in GROUPS[1]], axis=0)], "pack_mix"),
              _pack(chip, [rows_of(weights, n) for n in GROUPS[2]], "pack_ffn2")]
    bufs = [p.reshape(N_CHIPS, 2, p.shape[1] // 2, D_MODEL) for p in packed]

    small_w = {"ffn1_norm": ffn1_norm, "mix_norm": mix_norm, "ffn2_norm": ffn2_norm,
               "final_norm": final_norm.reshape(1, D_MODEL), "pool_scale": pool_scale, "sink_logits": sink_logits,
               "pool_w": pool_w[0]}
    loss, grad_x, group_grads, small_sum = _step(x[0], loss_target[0], bufs, small_w, place)

    out_g, out_d, out_m, out_v = {}, {}, {}, {}
    for members, g in zip(GROUPS, group_grads):
        row0 = 0
        for n in members:
            w = rows_of(weights, n)
            tile = FF_CHUNK // 4 if w.shape[0] == FF_CHUNK else math.gcd(IN_ROWS, OUT_ROWS)
            outs = _adamw(w, g, row0, rows_of(mom1, n), rows_of(mom2, n), tile, "adamw_" + n)
            row0 += w.shape[0]
            for dst, t in zip((out_g, out_d, out_m, out_v), outs):
                dst[n] = (jnp.swapaxes(t, 0, 1) if n in TRANSPOSED else t).reshape(weights[n].shape)
    small_outs = _adamw(_pack_small(weights), small_sum, 0, _pack_small(mom1), _pack_small(mom2), SMALL_ROWS, "adamw_small")
    for dst, packed in zip((out_g, out_d, out_m, out_v), small_outs):
        dst.update(_unpack_small(packed, weights))

    return (loss,grad_x.reshape(x.shape), *[out_g[n] for n in names], *[out_d[n] for n in names],
            *[out_m[n] for n in names], *[out_v[n] for n in names])
```

```python
import collections
import functools
import math

import jax
import jax.numpy as jnp
from jax import lax
from jax.experimental import pallas as pl
from jax.experimental.pallas import tpu as pltpu

F32, BF16 = jnp.float32, jnp.bfloat16
MESH = pl.DeviceIdType.MESH

D_MODEL = 1024
D_FF = 2816
N_CHIPS = 4
FF_CHUNK = D_FF // N_CHIPS
HEAD_DIM = 64
N_HEADS = 8
N_KV = 2
Q_PER_KV = N_HEADS // N_KV
KV_WIDTH = N_KV * HEAD_DIM
ATTN_WIDTH = N_HEADS * HEAD_DIM
POOL_WINDOWS = (2, 4, 8, 16)
N_POOL = len(POOL_WINDOWS)
POOL_GROUP = 128
POOL_WIDTH = N_POOL * POOL_GROUP
IN_WIDTH = ATTN_WIDTH + 2 * KV_WIDTH + POOL_WIDTH
WINDOW = 128
BLOCK = 128
BAND = 3 * BLOCK
ROPE_THETA = 500000.0
ROTARY_DIM = HEAD_DIM // 4
EPS = 1e-6
LANES = 128
Q_PAD = N_HEADS * LANES
U_PAD = Q_PAD + 2 * KV_WIDTH + POOL_WIDTH
SCALE = HEAD_DIM ** -0.5
NEG = -1e30

ADAM_LR, ADAM_B1, ADAM_B2, ADAM_EPS, ADAM_WD, ADAM_STEP = 0.001, 0.9, 0.999, 1e-08, 0.01, 10

V7X_VMEM_BYTES = 64 * 1024 * 1024
TOK_TILE = 512
SMALL_ROWS = 552


def _params(sem, vmem_mb):
    assert vmem_mb * 1024 * 1024 <= V7X_VMEM_BYTES
    return pltpu.CompilerParams(dimension_semantics=sem, vmem_limit_bytes=vmem_mb * 1024 * 1024)


def _dot(a, b):
    return lax.dot_general(a, b, (((1,), (0,)), ((), ())), preferred_element_type=F32)


def _dot_nt(a, b):
    return lax.dot_general(a, b, (((1,), (1,)), ((), ())), preferred_element_type=F32)


def _dot_tn(a, b):
    return lax.dot_general(a, b, (((0,), (0,)), ((), ())), preferred_element_type=F32)


def _rms_stats(h):
    r = lax.rsqrt(jnp.mean(h * h, axis=-1, keepdims=True) + EPS)
    return r, h * r


def _rms_bwd(dn, g, r, xh):
    gd = dn * g
    dh = r * (gd - xh * jnp.mean(gd * xh, axis=-1, keepdims=True))
    return dh, jnp.sum(dn * xh, axis=0, keepdims=True)


def _rope(x, c, s1, s2):
    return x * c + pltpu.roll(x, LANES - ROTARY_DIM // 2, 1) * s1 + pltpu.roll(x, ROTARY_DIM // 2, 1) * s2


def _rope_bwd(d, c, s1, s2):
    return d * c + pltpu.roll(d * s1, ROTARY_DIM // 2, 1) + pltpu.roll(d * s2, LANES - ROTARY_DIM // 2, 1)


def _sum_chunks(refs):
    terms = [ref[j].astype(F32) for ref in refs for j in range(ref.shape[0])]
    return functools.reduce(lambda a, b: a + b, terms)


def _chunk_rows(tile, k):
    return pl.BlockSpec((k, tile, D_MODEL), lambda i, *_: (0, i, 0))


def _full(shape):
    nd = len(shape)
    return pl.BlockSpec(shape, lambda *_: (0,) * nd)


def _rows(tile, cols):
    return pl.BlockSpec((tile, cols), lambda i, *_: (i, 0))


HBM_SPEC = pl.BlockSpec(memory_space=pltpu.HBM)

_Rider = collections.namedtuple("_Rider", "operands out_shapes aliases scratch start finish hooks", defaults=[()])


_NO_RIDER = _Rider([], [], {}, [], None, None)


def _call(body, name, grid, in_specs, out_specs, out_shape, scratch, vmem_mb, args, rider=None, prefetch=(),
          shares_rider_refs=False):
    rider = rider or _NO_RIDER
    n_pre, n_in, n_out, n_scr = len(prefetch), len(in_specs), len(out_specs), len(scratch)
    r_in, r_out = len(rider.operands), len(rider.out_shapes)

    def fused(*refs):
        pre, refs = refs[:n_pre], refs[n_pre:]
        ins, refs = refs[:n_in], refs[n_in:]
        r_ins, refs = refs[:r_in], refs[r_in:]
        outs, refs = refs[:n_out], refs[n_out:]
        r_outs, refs = refs[:r_out], refs[r_out:]
        scr, r_scr = refs[:n_scr], refs[n_scr:]
        ids = [pl.program_id(d) for d in range(len(grid))]
        if rider.start is not None:
            @pl.when(functools.reduce(jnp.logical_and, [i == 0 for i in ids]))
            def _():
                rider.start(r_ins, r_outs, r_scr)

        for at, hook in rider.hooks:
            @pl.when(functools.reduce(jnp.logical_and, [i == a for i, a in zip(ids, at)]))
            def _(hook=hook):
                hook(r_ins, r_outs, r_scr)

        if shares_rider_refs:
            body(*pre, *ins, *outs, *scr, rider_refs=r_outs)
        else:
            body(*pre, *ins, *outs, *scr)

        if rider.finish is not None:
            @pl.when(functools.reduce(jnp.logical_and, [i == g - 1 for i, g in zip(ids, grid)]))
            def _():
                rider.finish(r_ins, r_outs, r_scr)

    return pl.pallas_call(
        fused, name=name,
        grid_spec=pltpu.PrefetchScalarGridSpec(
            num_scalar_prefetch=n_pre, grid=grid,
            in_specs=list(in_specs) + [HBM_SPEC] * r_in, out_specs=list(out_specs) + [HBM_SPEC] * r_out,
            scratch_shapes=list(scratch) + list(rider.scratch)),
        out_shape=list(out_shape) + list(rider.out_shapes),
        input_output_aliases={n_pre + n_in + k: n_out + v for k, v in rider.aliases.items()},
        compiler_params=_params(("arbitrary",) * len(grid), vmem_mb),
    )(*prefetch, *args, *rider.operands)


def _comm_call(rider, name):
    r_in, r_out = len(rider.operands), len(rider.out_shapes)

    def body(*refs):
        r_ins, r_outs, r_scr = refs[:r_in], refs[r_in:r_in + r_out], refs[r_in + r_out:]
        rider.start(r_ins, r_outs, r_scr)
        rider.finish(r_ins, r_outs, r_scr)

    return pl.pallas_call(
        body, name=name, in_specs=[HBM_SPEC] * r_in, out_specs=[HBM_SPEC] * r_out, out_shape=list(rider.out_shapes),
        input_output_aliases=dict(rider.aliases), scratch_shapes=list(rider.scratch),
    )(*rider.operands)


def _ffn_fwd(order, h, gain, name, rider, group_at):
    S = h.shape[0]
    tile = min(TOK_TILE, S)
    nt = S // tile
    last = N_CHIPS - 1

    def body(order_ref, h_ref, g_ref, ho_ref, n_ref, gate_ref, up_ref, w_scr, w_sem, acc, n_scr, rider_refs):
        j, i = pl.program_id(0), pl.program_id(1)

        @pl.when(i == 0)
        def _():
            fetch = pltpu.make_async_copy(rider_refs[group_at].at[order_ref[j]], w_scr, w_sem)
            fetch.start()
            fetch.wait()

        at = pl.multiple_of(i * tile, tile)

        @pl.when(j == 0)
        def _():
            _, xh = _rms_stats(h_ref[...])
            n = (xh * g_ref[...]).astype(BF16)
            n_scr[pl.ds(at, tile), :] = n
            n_ref[...] = n

        half = tile // 2
        wg, wu, wd = (w_scr[part * FF_CHUNK:(part + 1) * FF_CHUNK, :] for part in range(3))
        ns = [n_scr[pl.ds(at + s * half, half), :] for s in range(2)]
        gates = [_dot_nt(n, wg) for n in ns]
        ups = [_dot_nt(n, wu) for n in ns]
        acts = [(g * jax.nn.sigmoid(g) * u).astype(BF16) for g, u in zip(gates, ups)]
        outs = [_dot(act, wd) for act in acts]
        for s in range(2):
            gate_ref[0, s * half:(s + 1) * half, :] = gates[s].astype(BF16)
            up_ref[0, s * half:(s + 1) * half, :] = ups[s].astype(BF16)
        out = jnp.concatenate(outs, axis=0)

        @pl.when(j == 0)
        def _():
            acc[pl.ds(at, tile), :] = out

        @pl.when(jnp.logical_and(j > 0, j < last))
        def _():
            acc[pl.ds(at, tile), :] += out

        @pl.when(j == last)
        def _():
            ho_ref[...] = h_ref[...] + 0.5 * (acc[pl.ds(at, tile), :] + out)

    tok = pl.BlockSpec((tile, D_MODEL), lambda j, i, order_ref: (i, 0))
    hid = pl.BlockSpec((1, tile, FF_CHUNK), lambda j, i, order_ref: (order_ref[j], i, 0))
    return _call(
        body, name, (N_CHIPS, nt),
        [tok, pl.BlockSpec((1, D_MODEL), lambda j, i, order_ref: (0, 0))],
        [pl.BlockSpec((tile, D_MODEL), lambda j, i, order_ref: (jnp.where(j == last, i, 0), 0)),
         pl.BlockSpec((tile, D_MODEL), lambda j, i, order_ref: (jnp.where(j == 0, i, nt - 1), 0)),
         hid, hid],
        [jax.ShapeDtypeStruct((S, D_MODEL), F32), jax.ShapeDtypeStruct((S, D_MODEL), BF16),
         jax.ShapeDtypeStruct((N_CHIPS, S, FF_CHUNK), BF16), jax.ShapeDtypeStruct((N_CHIPS, S, FF_CHUNK), BF16)],
        [pltpu.VMEM((3 * FF_CHUNK, D_MODEL), BF16), pltpu.SemaphoreType.DMA, pltpu.VMEM((S, D_MODEL), F32),
         pltpu.VMEM((S, D_MODEL), BF16)], 58, (h, gain), rider, (order,), shares_rider_refs=True)


def _ffn_bwd(chunks, d_out, n, gate, up, group, name, rider=None):
    S = n.shape[0]
    n_chunks = chunks.shape[0]
    tile = min(TOK_TILE, S)
    nt = S // tile
    half_rows = 3 * FF_CHUNK // 2
    cut = FF_CHUNK // 2

    def body(chunks_ref, do_ref, n_ref, gate_ref, up_ref, wg_ref, wu_ref, wd_ref, dn_ref, dw_ref, acc_g, acc_u, acc_d):
        j, i = pl.program_id(0), pl.program_id(1)

        @pl.when(i == 0)
        def _():
            acc_g[...] = jnp.zeros_like(acc_g)
            acc_u[...] = jnp.zeros_like(acc_u)
            acc_d[...] = jnp.zeros_like(acc_d)

        halves = [pl.ds(s * (tile // 2), tile // 2) for s in range(2)]
        dos = [do_ref[rows, :] for rows in halves]
        d_acts = [_dot_nt(do, wd_ref[0]) for do in dos]
        gs = [gate_ref[0, rows, :].astype(F32) for rows in halves]
        us = [up_ref[0, rows, :].astype(F32) for rows in halves]
        sigs = [jax.nn.sigmoid(g) for g in gs]
        silus = [g * sig for g, sig in zip(gs, sigs)]
        d_ups = [(d_act * silu).astype(BF16) for d_act, silu in zip(d_acts, silus)]
        d_gates = [(d_act * u * (sig * (1.0 + g * (1.0 - sig)))).astype(BF16) for d_act, u, sig, g in zip(d_acts, us, sigs, gs)]
        for rows, d_gate, d_up in zip(halves, d_gates, d_ups):
            dn_ref[0, rows, :] = (_dot(d_gate, wg_ref[0]) + _dot(d_up, wu_ref[0])).astype(BF16)
        d_gate, d_up = jnp.concatenate(d_gates, axis=0), jnp.concatenate(d_ups, axis=0)
        act = jnp.concatenate([(silu * u).astype(BF16) for silu, u in zip(silus, us)], axis=0)
        nn = n_ref[...]
        acc_g[...] += _dot_tn(d_gate, nn)
        acc_u[...] += _dot_tn(d_up, nn)
        acc_d[...] += _dot_tn(act, do_ref[...])

        @pl.when(i == nt - 1)
        def _():
            dw_ref[0, 0, :FF_CHUNK, :] = acc_g[...].astype(BF16)
            dw_ref[0, 0, FF_CHUNK:, :] = acc_u[:cut, :].astype(BF16)
            dw_ref[1, 0, :cut, :] = acc_u[cut:, :].astype(BF16)
            dw_ref[1, 0, cut:, :] = acc_d[...].astype(BF16)

    tok = pl.BlockSpec((tile, D_MODEL), lambda j, i, chunks_ref: (i, 0))
    hid = pl.BlockSpec((1, tile, FF_CHUNK), lambda j, i, chunks_ref: (chunks_ref[j], i, 0))
    return _call(
        body, name, (n_chunks, nt),
        [tok, tok, hid, hid]
        + [pl.BlockSpec((1, FF_CHUNK, D_MODEL), functools.partial(lambda j, i, chunks_ref, part: (chunks_ref[j], part, 0), part=part))
           for part in range(3)],
        [pl.BlockSpec((1, tile, D_MODEL), lambda j, i, chunks_ref: (j, i, 0)),
         pl.BlockSpec((2, 1, half_rows, D_MODEL), lambda j, i, chunks_ref: (0, j, 0, 0))],
        [jax.ShapeDtypeStruct((n_chunks, S, D_MODEL), BF16), jax.ShapeDtypeStruct((2, n_chunks, half_rows, D_MODEL), BF16)],
        [pltpu.VMEM((FF_CHUNK, D_MODEL), F32)] * 3, 56, (d_out, n, gate, up, group, group, group), rider, (chunks,))


def _mix_in(h, gain, w_in, rc, rs1, rs2, name):
    S = h.shape[0]
    tile = min(TOK_TILE, S)

    def body(h_ref, g_ref, w_ref, c_ref, s1_ref, s2_ref, n_ref, q_ref, k_ref, v_ref, pc_ref):
        _, xh = _rms_stats(h_ref[...])
        n = (xh * g_ref[...]).astype(BF16)
        n_ref[...] = n
        u = _dot_nt(n, w_ref[...])
        c, s1, s2 = c_ref[...], s1_ref[...], s2_ref[...]
        q_ref[...] = jnp.concatenate([(_rope(u[:, hd * LANES:(hd + 1) * LANES], c, s1, s2) * SCALE).astype(BF16)
                                      for hd in range(N_HEADS)], axis=1)
        k_ref[...] = _rope(u[:, Q_PAD:Q_PAD + KV_WIDTH], c, s1, s2).astype(BF16)
        v_ref[...] = u[:, Q_PAD + KV_WIDTH:Q_PAD + 2 * KV_WIDTH].astype(BF16)
        pc_ref[...] = u[:, Q_PAD + 2 * KV_WIDTH:]

    return pl.pallas_call(
        body, name=name, grid=(S // tile,),
        in_specs=[_rows(tile, D_MODEL), _full((1, D_MODEL)), _full((U_PAD, D_MODEL)),
                  _rows(tile, LANES), _rows(tile, LANES), _rows(tile, LANES)],
        out_specs=[_rows(tile, D_MODEL), _rows(tile, Q_PAD), _rows(tile, KV_WIDTH), _rows(tile, KV_WIDTH),
                   _rows(tile, POOL_WIDTH)],
        out_shape=[jax.ShapeDtypeStruct((S, D_MODEL), BF16), jax.ShapeDtypeStruct((S, Q_PAD), BF16),
                   jax.ShapeDtypeStruct((S, KV_WIDTH), BF16), jax.ShapeDtypeStruct((S, KV_WIDTH), BF16),
                   jax.ShapeDtypeStruct((S, POOL_WIDTH), F32)],
        compiler_params=_params(("parallel",), 40),
    )(h, gain, w_in, rc, rs1, rs2)


def _band_start(i, S):
    return pl.multiple_of(jnp.clip((i - 1) * BLOCK, 0, S - BAND), BLOCK)


def _window_bias(off):
    r = lax.broadcasted_iota(jnp.int32, (BLOCK, 1), 0)
    c = lax.broadcasted_iota(jnp.int32, (1, BAND), 1)
    return jnp.where(jnp.abs(off + r - c) <= WINDOW, 0.0, NEG).astype(F32)


def _softmax_parts(qh, kb, bias, sink_h):
    s = _dot_nt(qh, kb) + bias
    m = jnp.maximum(jnp.max(s, axis=-1, keepdims=True), sink_h)
    p = jnp.exp(s - m)
    es = jnp.exp(sink_h - m)
    return p, es, 1.0 / (jnp.sum(p, axis=-1, keepdims=True) + es)


def _pool_matrix(t0, start, S, w):
    r = lax.broadcasted_iota(jnp.int32, (BLOCK, 1), 0) + t0
    c = lax.broadcasted_iota(jnp.int32, (1, BAND), 1) + start
    half = w // 2

    def window(lo, hi):
        a = jnp.maximum(lo, 0)
        b = jnp.minimum(hi + 1, S)
        return jnp.where((c >= a) & (c < b), 1.0 / (b - a).astype(F32), 0.0)

    return (0.5 * (window(r - half, r + half - 1) + window(r - half + 1, r + half))).astype(BF16)


def _pool_matrices(S):
    blocks = ((0, 0), (BLOCK, 0), (S - BLOCK, S - BAND))
    return jnp.stack([jnp.stack([_pool_matrix(t0, start, S, w) for w in POOL_WINDOWS]) for t0, start in blocks])


def _pool_spec(nb):
    return pl.BlockSpec((1, N_POOL, BLOCK, BAND), lambda i, *_: (jnp.where(i == 0, 0, jnp.where(i == nb - 1, 2, 1)), 0, 0, 0))


def _mix_core_fwd(q, k, v, pc, sink, pool_m, pool_w, pool_scale, name, rider=None):
    S = q.shape[0]
    nb = S // BLOCK

    def body(sink_ref, q_ref, k_ref, v_ref, pc_ref, pm_ref, pw_ref, ps_ref, a_ref, p_ref):
        i = pl.program_id(0)
        start = _band_start(i, S)
        band = pl.ds(start, BAND)
        bias = _window_bias(i * BLOCK - start)
        kb, vb = k_ref[band, :], v_ref[band, :]
        hs = range(N_HEADS)
        ss = [_dot_nt(q_ref[:, hd * LANES:(hd + 1) * LANES], kb) + bias for hd in hs]
        ms = [jnp.maximum(jnp.max(ss[hd], axis=-1, keepdims=True), sink_ref[0, hd]) for hd in hs]
        ps = [jnp.exp(ss[hd] - ms[hd]) for hd in hs]
        invs = [1.0 / (jnp.sum(ps[hd], axis=-1, keepdims=True) + jnp.exp(sink_ref[0, hd] - ms[hd])) for hd in hs]
        outs = [_dot(ps[hd].astype(BF16), vb) for hd in hs]
        a_ref[...] = jnp.concatenate([(outs[hd] * invs[hd]).astype(BF16) for hd in hs], axis=1)
        centre = pl.ds(pl.multiple_of(i * BLOCK, BLOCK), BLOCK)
        gs = range(N_POOL)
        sl = [slice(g * POOL_GROUP, (g + 1) * POOL_GROUP) for g in gs]
        means = [_dot(pm_ref[0, g], pc_ref[band, sl[g]].astype(BF16)) for g in gs]
        devs = [(means[g] - pc_ref[centre, sl[g]]).astype(BF16) for g in gs]
        p_ref[...] = (jnp.concatenate([_dot(devs[g], pw_ref[g]) for g in gs], axis=1) * ps_ref[...]).astype(BF16)

    return _call(
        body, name, (nb,),
        [pl.BlockSpec(memory_space=pltpu.SMEM), _rows(BLOCK, Q_PAD), _full((S, KV_WIDTH)), _full((S, KV_WIDTH)),
         _full((S, POOL_WIDTH)), _pool_spec(nb), _full((N_POOL, POOL_GROUP, POOL_GROUP)), _full((1, POOL_WIDTH))],
        [_rows(BLOCK, Q_PAD), _rows(BLOCK, POOL_WIDTH)],
        [jax.ShapeDtypeStruct((S, Q_PAD), BF16), jax.ShapeDtypeStruct((S, POOL_WIDTH), BF16)],
        [], 40, (sink, q, k, v, pc, pool_m, pool_w, pool_scale), rider)


def _mix_core_bwd(q, k, v, pc, da, dp, sink, pool_m, pool_w, pool_scale, rc, rs1, rs2, name, rider=None):
    S = q.shape[0]
    nb = S // BLOCK

    def body(sink_ref, q_ref, k_ref, v_ref, pc_ref, da_ref, dp_ref, pm_ref, pw_ref, ps_ref, c_ref, s1_ref, s2_ref,
             dq_ref, dk_ref, dv_ref, dpc_ref, dsink_ref, dpw_ref, dps_ref):
        i = pl.program_id(0)

        @pl.when(i == 0)
        def _():
            dk_ref[...] = jnp.zeros_like(dk_ref)
            dv_ref[...] = jnp.zeros_like(dv_ref)
            dpc_ref[...] = jnp.zeros_like(dpc_ref)
            dsink_ref[...] = jnp.zeros_like(dsink_ref)
            dpw_ref[...] = jnp.zeros_like(dpw_ref)
            dps_ref[...] = jnp.zeros_like(dps_ref)

        start = _band_start(i, S)
        band = pl.ds(start, BAND)
        bias = _window_bias(i * BLOCK - start)
        kb, vb = k_ref[band, :], v_ref[band, :]
        c, s1, s2 = c_ref[...], s1_ref[...], s2_ref[...]
        lane = lax.broadcasted_iota(jnp.int32, (1, LANES), 1)
        hs = range(N_HEADS)
        qs = [q_ref[:, hd * LANES:(hd + 1) * LANES] for hd in hs]
        das = [da_ref[:, hd * LANES:(hd + 1) * LANES] for hd in hs]
        ss = [_dot_nt(qs[hd], kb) + bias for hd in hs]
        d_probs = [_dot_nt(das[hd], vb) for hd in hs]
        ms = [jnp.maximum(jnp.max(ss[hd], axis=-1, keepdims=True), sink_ref[0, hd]) for hd in hs]
        ps = [jnp.exp(ss[hd] - ms[hd]) for hd in hs]
        ess = [jnp.exp(sink_ref[0, hd] - ms[hd]) for hd in hs]
        invs = [1.0 / (jnp.sum(ps[hd], axis=-1, keepdims=True) + ess[hd]) for hd in hs]
        probs = [ps[hd] * invs[hd] for hd in hs]
        deltas = [jnp.sum(probs[hd] * d_probs[hd], axis=-1, keepdims=True) for hd in hs]
        d_ss = [(probs[hd] * (d_probs[hd] - deltas[hd])).astype(BF16) for hd in hs]
        dqs = [_dot(d_ss[hd], kb) for hd in hs]
        dq_ref[...] = jnp.concatenate([_rope_bwd(dqs[hd] * SCALE, c, s1, s2).astype(BF16) for hd in hs], axis=1)
        dks = [_dot_tn(d_ss[hd], qs[hd]) for hd in hs]
        dvs = [_dot_tn(probs[hd].astype(BF16), das[hd]) for hd in hs]
        dk_ref[band, :] += functools.reduce(lambda a, b: a + b, dks)
        dv_ref[band, :] += functools.reduce(lambda a, b: a + b, dvs)
        dsink_ref[...] += functools.reduce(lambda a, b: a + b, [
            jnp.where(lane == hd, -jnp.sum(ess[hd] * invs[hd] * deltas[hd], axis=0, keepdims=True), 0.0) for hd in hs])

        centre = pl.ds(pl.multiple_of(i * BLOCK, BLOCK), BLOCK)
        gs = range(N_POOL)
        sl = [slice(g * POOL_GROUP, (g + 1) * POOL_GROUP) for g in gs]
        devs = [(_dot(pm_ref[0, g], pc_ref[band, sl[g]].astype(BF16)) - pc_ref[centre, sl[g]]).astype(BF16) for g in gs]
        dys = [dp_ref[:, sl[g]].astype(F32) for g in gs]
        zs = [_dot(devs[g], pw_ref[g]) for g in gs]
        dzs = [(dys[g] * ps_ref[:, sl[g]]).astype(BF16) for g in gs]
        d_devs = [_dot_nt(dzs[g], pw_ref[g]) for g in gs]
        dps_ref[...] += jnp.concatenate([jnp.sum(dys[g] * zs[g], axis=0, keepdims=True) for g in gs], axis=1)
        for g in gs:
            dpw_ref[g] += _dot_tn(devs[g], dzs[g])
        dpc_ref[band, :] += jnp.concatenate([_dot_tn(pm_ref[0, g], d_devs[g].astype(BF16)) for g in gs], axis=1)
        dpc_ref[centre, :] -= jnp.concatenate(d_devs, axis=1)

    return _call(
        body, name, (nb,),
        [pl.BlockSpec(memory_space=pltpu.SMEM), _rows(BLOCK, Q_PAD), _full((S, KV_WIDTH)), _full((S, KV_WIDTH)),
         _full((S, POOL_WIDTH)), _rows(BLOCK, Q_PAD), _rows(BLOCK, POOL_WIDTH), _pool_spec(nb),
         _full((N_POOL, POOL_GROUP, POOL_GROUP)), _full((1, POOL_WIDTH)),
         _rows(BLOCK, LANES), _rows(BLOCK, LANES), _rows(BLOCK, LANES)],
        [_rows(BLOCK, Q_PAD), _full((S, KV_WIDTH)), _full((S, KV_WIDTH)), _full((S, POOL_WIDTH)),
         _full((1, LANES)), _full((N_POOL, POOL_GROUP, POOL_GROUP)), _full((1, POOL_WIDTH))],
        [jax.ShapeDtypeStruct((S, Q_PAD), BF16), jax.ShapeDtypeStruct((S, KV_WIDTH), F32),
         jax.ShapeDtypeStruct((S, KV_WIDTH), F32), jax.ShapeDtypeStruct((S, POOL_WIDTH), F32),
         jax.ShapeDtypeStruct((1, LANES), F32), jax.ShapeDtypeStruct((N_POOL, POOL_GROUP, POOL_GROUP), F32),
         jax.ShapeDtypeStruct((1, POOL_WIDTH), F32)],
        [], 56, (sink, q, k, v, pc, da, dp, pool_m, pool_w, pool_scale, rc, rs1, rs2), rider)


def _mix_out(h, a, p, wa, wp, name):
    S = h.shape[0]
    tile = min(TOK_TILE, S)

    def body(h_ref, a_ref, p_ref, wa_ref, wp_ref, o_ref):
        o_ref[...] = h_ref[...] + _dot(a_ref[...], wa_ref[...]) + _dot(p_ref[...], wp_ref[...])

    return pl.pallas_call(
        body, name=name, grid=(S // tile,),
        in_specs=[_rows(tile, D_MODEL), _rows(tile, Q_PAD), _rows(tile, POOL_WIDTH),
                  _full((Q_PAD, D_MODEL)), _full((POOL_WIDTH, D_MODEL))],
        out_specs=_rows(tile, D_MODEL),
        out_shape=jax.ShapeDtypeStruct((S, D_MODEL), F32),
        compiler_params=_params(("parallel",), 40),
    )(h, a, p, wa, wp)


def _loss_head(h, target, gain, name):
    S = h.shape[0]
    tile = min(TOK_TILE, S)

    def body(h_ref, t_ref, g_ref, dh_ref, dhalf_ref, loss_ref, dg_ref):
        @pl.when(pl.program_id(0) == 0)
        def _():
            loss_ref[...] = jnp.zeros_like(loss_ref)
            dg_ref[...] = jnp.zeros_like(dg_ref)

        g = g_ref[...]
        r, xh = _rms_stats(h_ref[...])
        err = xh * g - t_ref[...]
        loss_ref[...] += (0.5 / D_MODEL) * jnp.sum(err * err, axis=0, keepdims=True)
        dh, dg = _rms_bwd(err * (1.0 / D_MODEL), g, r, xh)
        dg_ref[...] += dg
        dh_ref[...] = dh
        dhalf_ref[...] = (0.5 * dh).astype(BF16)

    return pl.pallas_call(
        body, name=name, grid=(S // tile,),
        in_specs=[_rows(tile, D_MODEL), _rows(tile, D_MODEL), _full((1, D_MODEL))],
        out_specs=[_rows(tile, D_MODEL), _rows(tile, D_MODEL), _full((1, D_MODEL)), _full((1, D_MODEL))],
        out_shape=[jax.ShapeDtypeStruct((S, D_MODEL), F32), jax.ShapeDtypeStruct((S, D_MODEL), BF16),
                   jax.ShapeDtypeStruct((1, D_MODEL), F32), jax.ShapeDtypeStruct((1, D_MODEL), F32)],
        compiler_params=_params(("arbitrary",), 40),
    )(h, target, gain)


def _mix_out_bwd(dh_out, dn, h, gain, a, p, wa, wp, name, rider=None):
    S = h.shape[0]
    tile = min(TOK_TILE, S)

    def body(do_ref, dn_ref, h_ref, g_ref, a_ref, p_ref, wa_ref, wp_ref, dh_ref, da_ref, dp_ref, dwa_ref, dwp_ref, dg_ref):
        @pl.when(pl.program_id(0) == 0)
        def _():
            dwa_ref[...] = jnp.zeros_like(dwa_ref)
            dwp_ref[...] = jnp.zeros_like(dwp_ref)
            dg_ref[...] = jnp.zeros_like(dg_ref)

        r, xh = _rms_stats(h_ref[...])
        dnorm, dg = _rms_bwd(_sum_chunks([dn_ref]), g_ref[...], r, xh)
        dh = do_ref[...] + dnorm
        dg_ref[...] += dg
        dh_ref[...] = dh
        dhb = dh.astype(BF16)
        da_ref[...] = _dot_nt(dhb, wa_ref[...]).astype(BF16)
        dp_ref[...] = _dot_nt(dhb, wp_ref[...]).astype(BF16)
        dwa_ref[...] += _dot_tn(a_ref[...], dhb)
        dwp_ref[...] += _dot_tn(p_ref[...], dhb)

    return _call(
        body, name, (S // tile,),
        [_rows(tile, D_MODEL), _chunk_rows(tile, dn.shape[0]), _rows(tile, D_MODEL), _full((1, D_MODEL)),
         _rows(tile, Q_PAD), _rows(tile, POOL_WIDTH), _full((Q_PAD, D_MODEL)), _full((POOL_WIDTH, D_MODEL))],
        [_rows(tile, D_MODEL), _rows(tile, Q_PAD), _rows(tile, POOL_WIDTH),
         _full((Q_PAD, D_MODEL)), _full((POOL_WIDTH, D_MODEL)), _full((1, D_MODEL))],
        [jax.ShapeDtypeStruct((S, D_MODEL), F32), jax.ShapeDtypeStruct((S, Q_PAD), BF16),
         jax.ShapeDtypeStruct((S, POOL_WIDTH), BF16), jax.ShapeDtypeStruct((Q_PAD, D_MODEL), F32),
         jax.ShapeDtypeStruct((POOL_WIDTH, D_MODEL), F32), jax.ShapeDtypeStruct((1, D_MODEL), F32)],
        [], 48, (dh_out, dn, h, gain, a, p, wa, wp), rider)


def _mix_in_bwd(dh_out, h, gain, n, dq, dk, dv, dpc, rc, rs1, rs2, w_in, name, rider=None):
    S = h.shape[0]
    tile = min(TOK_TILE, S)

    def body(do_ref, h_ref, g_ref, n_ref, dq_ref, dk_ref, dv_ref, dpc_ref, c_ref, s1_ref, s2_ref, w_ref,
             dh_ref, dhalf_ref, dw_ref, dg_ref):
        @pl.when(pl.program_id(0) == 0)
        def _():
            dw_ref[...] = jnp.zeros_like(dw_ref)
            dg_ref[...] = jnp.zeros_like(dg_ref)

        dk = _rope_bwd(dk_ref[...], c_ref[...], s1_ref[...], s2_ref[...]).astype(BF16)
        du = jnp.concatenate([dq_ref[...], dk, dv_ref[...].astype(BF16), dpc_ref[...].astype(BF16)], axis=1)
        dn = _dot(du, w_ref[...])
        dw_ref[...] += _dot_tn(du, n_ref[...])
        r, xh = _rms_stats(h_ref[...])
        dnorm, dg = _rms_bwd(dn, g_ref[...], r, xh)
        dh = do_ref[...] + dnorm
        dg_ref[...] += dg
        dh_ref[...] = dh
        dhalf_ref[...] = (0.5 * dh).astype(BF16)

    return _call(
        body, name, (S // tile,),
        [_rows(tile, D_MODEL), _rows(tile, D_MODEL), _full((1, D_MODEL)), _rows(tile, D_MODEL),
         _rows(tile, Q_PAD), _rows(tile, KV_WIDTH), _rows(tile, KV_WIDTH), _rows(tile, POOL_WIDTH),
         _rows(tile, LANES), _rows(tile, LANES), _rows(tile, LANES), _full((U_PAD, D_MODEL))],
        [_rows(tile, D_MODEL), _rows(tile, D_MODEL), _full((U_PAD, D_MODEL)), _full((1, D_MODEL))],
        [jax.ShapeDtypeStruct((S, D_MODEL), F32), jax.ShapeDtypeStruct((S, D_MODEL), BF16),
         jax.ShapeDtypeStruct((U_PAD, D_MODEL), F32), jax.ShapeDtypeStruct((1, D_MODEL), F32)],
        [], 56, (dh_out, h, gain, n, dq, dk, dv, dpc, rc, rs1, rs2, w_in), rider)


def _norm_bwd(dh_out, dns, h, gain, name):
    S = h.shape[0]
    tile = min(TOK_TILE, S)
    n = len(dns)

    def body(do_ref, *refs):
        h_ref, g_ref, dh_ref, dg_ref = refs[n:]

        @pl.when(pl.program_id(0) == 0)
        def _():
            dg_ref[...] = jnp.zeros_like(dg_ref)

        r, xh = _rms_stats(h_ref[...])
        dnorm, dg = _rms_bwd(_sum_chunks(refs[:n]), g_ref[...], r, xh)
        dg_ref[...] += dg
        dh_ref[...] = do_ref[...] + dnorm

    return _call(
        body, name, (S // tile,),
        [_rows(tile, D_MODEL)] + [_chunk_rows(tile, dn.shape[0]) for dn in dns] + [_rows(tile, D_MODEL), _full((1, D_MODEL))],
        [_rows(tile, D_MODEL), _full((1, D_MODEL))],
        [jax.ShapeDtypeStruct((S, D_MODEL), F32), jax.ShapeDtypeStruct((1, D_MODEL), F32)],
        [], 40, (dh_out, *dns, h, gain))


def _rope_tables(S):
    half = ROTARY_DIM // 2
    inv_freq = ROPE_THETA ** (-jnp.arange(0, ROTARY_DIM, 2, dtype=F32) / ROTARY_DIM)
    dim = jnp.arange(LANES) % HEAD_DIM
    ang = jnp.arange(S, dtype=F32)[:, None] * inv_freq[dim % half][None, :]
    lo, hi = (dim < half)[None, :], ((dim >= half) & (dim < ROTARY_DIM))[None, :]
    c = jnp.where(lo | hi, jnp.cos(ang), 1.0)
    s1 = jnp.where(lo, -jnp.sin(ang), 0.0)
    s2 = jnp.where(hi, jnp.sin(ang), 0.0)
    return c, s1, s2


def _pad_heads(w, axis):
    w = jnp.moveaxis(w, axis, 0)
    heads = w.reshape((N_HEADS, HEAD_DIM) + w.shape[1:])
    zero = jnp.zeros_like(heads)
    first = (jnp.arange(N_HEADS) < Q_PER_KV).reshape((N_HEADS, 1) + (1,) * (w.ndim - 1))
    lo = jnp.where(first, heads, zero)
    hi = jnp.where(first, zero, heads)
    padded = jnp.concatenate([lo, hi], axis=1).reshape((Q_PAD,) + w.shape[1:])
    return jnp.moveaxis(padded, 0, axis)


def _unpad_heads(w, axis):
    w = jnp.moveaxis(w, axis, 0)
    groups = w.reshape((N_HEADS, 2, HEAD_DIM) + w.shape[1:])
    first = (jnp.arange(N_HEADS) < Q_PER_KV).reshape((N_HEADS, 1) + (1,) * (w.ndim - 1))
    heads = jnp.where(first, groups[:, 0], groups[:, 1]).reshape((ATTN_WIDTH,) + w.shape[1:])
    return jnp.moveaxis(heads, 0, axis)


IN_ROWS = IN_WIDTH // N_CHIPS
OUT_ROWS = (ATTN_WIDTH + POOL_WIDTH) // N_CHIPS
MIX_ROWS = IN_ROWS + OUT_ROWS
FFN_ROWS = 3 * FF_CHUNK


def _step(x, target, bufs, small, place):
    S = x.shape[0]
    rc, rs1, rs2 = _rope_tables(S)
    mine = place[1]
    order = jnp.stack([mine, mine ^ 2, mine ^ 1, mine ^ 3])
    h1, n1, gate1, up1, ffn1, mix, ffn2 = _ffn_fwd(
        order, x, small["ffn1_norm"], "ffn1_fwd",
        _merge(_merge(_allgather(bufs[:1], in_passes=0), _allgather(bufs[1:2])), _allgather(bufs[2:], peers=X_Y_DIAGONAL[:2])), 0)
    w_in_t = mix[:, :IN_ROWS].reshape(IN_WIDTH, D_MODEL)
    w_in_pad = jnp.concatenate([_pad_heads(w_in_t[:ATTN_WIDTH], 0), w_in_t[ATTN_WIDTH:]], axis=0)
    w_out = mix[:, IN_ROWS:].reshape(ATTN_WIDTH + POOL_WIDTH, D_MODEL)
    wa = _pad_heads(w_out[:ATTN_WIDTH], 0)
    wp = w_out[ATTN_WIDTH:]
    pool_w = small["pool_w"].astype(BF16)

    n2, q, k, v, pc = _mix_in(h1, small["mix_norm"], w_in_pad, rc, rs1, rs2, "mix_in")
    pool_m = _pool_matrices(S)
    a, p = _mix_core_fwd(q, k, v, pc, small["sink_logits"], pool_m, pool_w, small["pool_scale"], "mix_core_fwd")
    h2 = _mix_out(h1, a, p, wa, wp, "mix_out")
    h3, n3, gate2, up2, ffn2 = _ffn_fwd(order, h2, small["ffn2_norm"], "ffn2_fwd",
                                        _allgather([ffn2], peers=X_Y_DIAGONAL[2:], in_passes=0), 0)
    dh3, dhalf3, loss_lanes, d_final = _loss_head(h3, target, small["final_norm"], "loss_head")

    dn3, d_ffn2 = _ffn_bwd(jnp.arange(N_CHIPS, dtype=jnp.int32), dhalf3, n3, gate2, up2, ffn2, "ffn2_bwd")
    dh2, da, dp, dwa, dwp, d_ffn2_norm, received = _mix_out_bwd(dh3, dn3, h2, small["ffn2_norm"], a, p, wa, wp, "mix_out_bwd",
                                                                _sibling_exchange([d_ffn2]))
    pair = _pair_sum(place, d_ffn2, received, "grad_pair_sum_ffn2")
    dq, dk, dv, dpc, dsink, dpool_w, dpool_scale, stack = _mix_core_bwd(
        q, k, v, pc, da, dp, small["sink_logits"], pool_m, pool_w, small["pool_scale"], rc, rs1, rs2, "mix_core_bwd",
        _scatter([pair], peers=X_Y_DIAGONAL[:2]))
    dh1, dhalf1, dw_in_pad, d_mix_norm, stack = _mix_in_bwd(dh2, h1, small["mix_norm"], n2, dq, dk, dv, dpc, rc, rs1, rs2,
                                                            w_in_pad, "mix_in_bwd",
                                                            _scatter([pair], peers=X_Y_DIAGONAL[2:], stacks=[stack]))
    reduced_ffn2 = _chip_sum(place, pair, stack, 2, "grad_chip_sum_ffn2")
    dw_in_t = jnp.concatenate([_unpad_heads(dw_in_pad[:Q_PAD], 0), dw_in_pad[Q_PAD:]], axis=0)
    dw_out = jnp.concatenate([_unpad_heads(dwa, 0), dwp], axis=0)
    d_mix = jnp.concatenate([dw_in_t.reshape(N_CHIPS, IN_ROWS, D_MODEL), dw_out.reshape(N_CHIPS, OUT_ROWS, D_MODEL)], axis=1)
    d_mix = jnp.transpose(d_mix.reshape(N_CHIPS, 2, MIX_ROWS // 2, D_MODEL), (1, 0, 2, 3)).astype(BF16)
    small_g = {"ffn1_norm": jnp.zeros_like(d_mix_norm), "mix_norm": d_mix_norm, "ffn2_norm": d_ffn2_norm,
               "final_norm": d_final, "pool_scale": dpool_scale, "sink_logits": dsink[:, :N_HEADS], "pool_w": dpool_w}
    loss_row = jnp.sum(loss_lanes.reshape(D_MODEL // LANES, LANES), axis=0, keepdims=True)
    small_early = _pack_small(small_g, loss_row)

    chunk = [(place[1:] + 1 + p) % N_CHIPS for p in range(N_CHIPS)]
    ffn1_bwd = functools.partial(_ffn_bwd, d_out=dhalf1, n=n1, gate=gate1, up=up1, group=ffn1)
    stack = jnp.zeros((N_CHIPS, FFN_ROWS // 2, D_MODEL), BF16)
    dn_a, dw_a, recv_mix, small_all, g_ffn2 = ffn1_bwd(
        chunk[0], name="ffn1_bwd_0",
        rider=_merge(_merge(_sibling_exchange([d_mix]), _small_allgather(small_early)), _sibling_share([reduced_ffn2])))
    pair_mix = _pair_sum(place, d_mix, recv_mix, "grad_pair_sum_mix")
    dn_b, dw_b, recv_a, stack_mix = ffn1_bwd(chunk[1], name="ffn1_bwd_1",
                                             rider=_merge(_sibling_exchange([dw_a]), _scatter([pair_mix])))
    pair_a = _pair_sum(place, dw_a, recv_a, "grad_pair_sum_ffn1_0")
    reduced_mix = _chip_sum(place, pair_mix, stack_mix, 2, "grad_chip_sum_mix")
    dn_c, dw_c, recv_b, stack, g_mix = ffn1_bwd(
        chunk[2], name="ffn1_bwd_2",
        rider=_merge(_merge(_sibling_exchange([dw_b]), _scatter_step(pair_a, stack, 0)), _sibling_share([reduced_mix])))
    pair_b = _pair_sum(place, dw_b, recv_b, "grad_pair_sum_ffn1_1")
    dn_d, dw_d, recv_c, stack = ffn1_bwd(chunk[3], name="ffn1_bwd_3",
                                         rider=_merge(_sibling_exchange([dw_c]), _scatter_step(pair_b, stack, 1)))
    pair_c = _pair_sum(place, dw_c, recv_c, "grad_pair_sum_ffn1_2")
    grad_x, d_ffn1_norm = _norm_bwd(dh1, [dn_a, dn_b, dn_c, dn_d], x, small["ffn1_norm"], "norm1_bwd")

    recv_d, stack, gains = _comm_call(
        _merge(_merge(_sibling_exchange([dw_d]), _scatter_step(pair_c, stack, 2)),
               _small_allgather(d_ffn1_norm.reshape(-1, LANES))), "grad_tail")
    pair_d = _pair_sum(place, dw_d, recv_d, "grad_pair_sum_ffn1_3")
    reduced_ffn1 = _chip_sum(place, pair_d, stack, 2, "grad_chip_sum_ffn1")
    (g_ffn1,) = _comm_call(_sibling_share([reduced_ffn1]), "grad_share_tail")
    gain_sum = _sum_leading(gains, 1, "gain_grad_sum")
    small_sum = jnp.concatenate([gain_sum, _sum_leading(small_all, 1, "small_grad_sum")[gain_sum.shape[0]:]], axis=0)
    return jnp.sum(small_sum[SMALL_ROWS - 1]), grad_x, [g.reshape(-1, D_MODEL) for g in (g_ffn1, g_mix, g_ffn2)], small_sum


GROUPS =(("ffn1_w_gate", "ffn1_w_up", "ffn1_w_down"), ("w_in", "w_out"), ("ffn2_w_gate", "ffn2_w_up", "ffn2_w_down"))
TRANSPOSED = ("ffn1_w_gate", "ffn1_w_up", "w_in", "ffn2_w_gate", "ffn2_w_up")


def _place():
    x, y, c = lax.axis_index("x"), lax.axis_index("y"), lax.axis_index("c")
    chips = [(1 - x, y), (x, 1 - y), (1 - x, 1 - y)]
    return x, y, c, chips


def _remote(src, dst, send_sem, recv_sem, to):
    return pltpu.make_async_remote_copy(src_ref=src, dst_ref=dst, send_sem=send_sem, recv_sem=recv_sem,
                                        device_id=to, device_id_type=MESH)


def _pack(chip, members, name):
    rows = members[0].shape[0]
    n = len(members)

    def body(chip_ref, *refs):
        ins, out_ref, buf, sems = refs[:n], refs[n], refs[n + 1], refs[n + 2]
        copies = [pltpu.make_async_copy(ins[k], buf.at[k], sems.at[k]) for k in range(n)]
        for cp in copies:
            cp.start()
        for k in range(n):
            copies[k].wait()
            out_ref[0, k * rows:(k + 1) * rows, :] = buf[k].astype(BF16)

    return pl.pallas_call(
        body, name=name,
        grid_spec=pltpu.PrefetchScalarGridSpec(
            num_scalar_prefetch=1, grid=(1,),
            in_specs=[HBM_SPEC] * n,
            out_specs=pl.BlockSpec((1, n * rows, D_MODEL), lambda k, chip_ref: (chip_ref[0], 0, 0)),
            scratch_shapes=[pltpu.VMEM((n, rows, D_MODEL), F32), pltpu.SemaphoreType.DMA((n,))]),
        out_shape=jax.ShapeDtypeStruct((N_CHIPS, n * rows, D_MODEL), BF16),
        compiler_params=_params(("arbitrary",), 40),
    )(chip, *members)


def _same(arrays):
    return [jax.ShapeDtypeStruct(a.shape, a.dtype) for a in arrays]


X_Y_DIAGONAL = (0, 1, 2)


def _allgather(bufs, peers=X_Y_DIAGONAL, in_passes=None):
    n = len(bufs)

    def copy(kind, outs, send_sems, recv_sems, a, k):
        x, y, c, chips = _place()
        half = bufs[a].shape[1] // 2

        def rows(slot, core):
            return outs[a].at[slot, pl.ds(pl.multiple_of(core * half, 16), half)]

        me, slot = 2 * x + y, 2 * chips[k][0] + chips[k][1]
        over_ici = (send_sems.at[6 * a + k], recv_sems.at[6 * a + k])
        over_d2d = (send_sems.at[6 * a + 3 + k], recv_sems.at[6 * a + 3 + k])
        if kind == "first":
            return _remote(rows(me, c), rows(me, c), *over_ici, (*chips[k], c))
        if kind == "landed":
            return _remote(rows(me, c), rows(slot, c), *over_ici, (*chips[k], c))
        if kind == "passed":
            return _remote(rows(slot, c), rows(slot, c), *over_d2d, (x, y, 1 - c))
        return _remote(rows(me, c), rows(slot, 1 - c), *over_d2d, (x, y, 1 - c))

    def arrive(pairs):
        def hook(ins, outs, sems):
            for a, k in pairs:
                copy("landed", outs, *sems, a, k).wait_recv()
                copy("passed", outs, *sems, a, k).start()
            for a, k in pairs:
                copy("handed", outs, *sems, a, k).wait_recv()
        return hook

    everything = [(a, k) for a in range(n) for k in peers]
    early = [(a, k) for a, k in everything if a == in_passes]

    def start(ins, outs, sems):
        for a, k in everything:
            copy("first", outs, *sems, a, k).start()

    def finish(ins, outs, sems):
        arrive([pair for pair in everything if pair not in early])(ins, outs, sems)
        for a, k in everything:
            copy("first", outs, *sems, a, k).wait_send()
            copy("passed", outs, *sems, a, k).wait_send()

    hooks = tuple(((k + 1, 0), arrive([(a, k)])) for a, k in early)
    return _Rider(list(bufs), _same(bufs), {a: a for a in range(n)},
                  [pltpu.SemaphoreType.DMA((6 * n,)), pltpu.SemaphoreType.DMA((6 * n,))], start, finish, hooks)


def _sibling_exchange(parts):
    n = len(parts)

    def copies(ins, outs, send_sems, recv_sems):
        x, y, c, _ = _place()
        return [_remote(ins[a].at[1 - c], outs[a], send_sems.at[a], recv_sems.at[a], (x, y, 1 - c)) for a in range(n)]

    def start(ins, outs, sems):
        for cp in copies(ins, outs, *sems):
            cp.start()

    def finish(ins, outs, sems):
        for cp in copies(ins, outs, *sems):
            cp.wait_recv()
            cp.wait_send()

    return _Rider(list(parts), [jax.ShapeDtypeStruct(p.shape[1:], p.dtype) for p in parts], {},
                  [pltpu.SemaphoreType.DMA((n,)), pltpu.SemaphoreType.DMA((n,))], start, finish)


def _small_allgather(small):
    flips = [(fx, fy, fc) for fx in range(2) for fy in range(2) for fc in range(2)][1:]

    def copies(small_ref, gather_ref, send_sems, recv_sems, local_sem, started_only=False):
        x, y, c, _ = _place()
        me = 4 * x + 2 * y + c
        peers = [((1 - x) if fx else x, (1 - y) if fy else y, (1 - c) if fc else c) for fx, fy, fc in flips]
        own = pltpu.make_async_copy(small_ref, gather_ref.at[me], local_sem)
        sent = [_remote(small_ref, gather_ref.at[me], send_sems.at[k], recv_sems.at[k], peer) for k, peer in enumerate(peers)]
        if started_only:
            return own, sent
        landed = [_remote(small_ref, gather_ref.at[4 * px + 2 * py + pc], send_sems.at[k], recv_sems.at[k], (px, py, pc))
                  for k, (px, py, pc) in enumerate(peers)]
        return own, sent, landed

    def start(ins, outs, sems):
        own, sent = copies(ins[0], outs[0], *sems, started_only=True)
        own.start()
        for cp in sent:
            cp.start()

    def finish(ins, outs, sems):
        own, sent, landed = copies(ins[0], outs[0], *sems)
        for cp in landed:
            cp.wait_recv()
        for cp in sent:
            cp.wait_send()
        own.wait()

    return _Rider([small], [jax.ShapeDtypeStruct((2 * N_CHIPS,) + small.shape, small.dtype)], {},
                  [pltpu.SemaphoreType.DMA((7,)), pltpu.SemaphoreType.DMA((7,)), pltpu.SemaphoreType.DMA], start, finish)


def _merge(a, b):
    na, nao, nas = len(a.operands), len(a.out_shapes), len(a.scratch)

    def start(ins, outs, sems):
        a.start(ins[:na], outs[:nao], sems[:nas])
        b.start(ins[na:], outs[nao:], sems[nas:])

    def finish(ins, outs, sems):
        a.finish(ins[:na], outs[:nao], sems[:nas])
        b.finish(ins[na:], outs[nao:], sems[nas:])

    def of_a(fn):
        return lambda ins, outs, sems: fn(ins[:na], outs[:nao], sems[:nas])

    def of_b(fn):
        return lambda ins, outs, sems: fn(ins[na:], outs[nao:], sems[nas:])

    aliases = {**a.aliases, **{na + k: nao + v for k, v in b.aliases.items()}}
    hooks = tuple((at, of_a(fn)) for at, fn in a.hooks) + tuple((at, of_b(fn)) for at, fn in b.hooks)
    return _Rider(a.operands + b.operands, a.out_shapes + b.out_shapes, aliases, a.scratch + b.scratch, start, finish, hooks)


def _scatter_step(pair, stack, step):
    def copies(pair_ref, stack_ref, send_sem, recv_sem, started_only=False):
        x, y, c, _ = _place()
        me = 2 * x + y
        to = (me + 1 + step) % N_CHIPS
        frm = (me + N_CHIPS - 1 - step) % N_CHIPS
        sent = _remote(pair_ref.at[0], stack_ref.at[me], send_sem, recv_sem, (to // 2, to % 2, c))
        if started_only:
            return sent
        landed = _remote(pair_ref.at[0], stack_ref.at[frm], send_sem, recv_sem, (frm // 2, frm % 2, c))
        return sent, landed

    def start(ins, outs, sems):
        copies(ins[0], outs[0], *sems, started_only=True).start()

    def finish(ins, outs, sems):
        sent, landed = copies(ins[0], outs[0], *sems)
        landed.wait_recv()
        sent.wait_send()

    return _Rider([pair, stack], _same([stack]), {1: 0}, [pltpu.SemaphoreType.DMA, pltpu.SemaphoreType.DMA], start, finish)


def _scatter(sums, peers=X_Y_DIAGONAL, stacks=None):
    n = len(sums)

    def copies(ins, outs, send_sems, recv_sems, started_only=False):
        x, y, c, chips = _place()
        me = 2 * x + y
        slots = [2 * cx + cy for cx, cy in chips]
        sent = [_remote(ins[a].at[slots[k]], outs[a].at[me], send_sems.at[3 * a + k], recv_sems.at[3 * a + k], (*chips[k], c))
                for a in range(n) for k in peers]
        if started_only:
            return sent
        landed = [_remote(ins[a].at[slots[k]], outs[a].at[slots[k]], send_sems.at[3 * a + k], recv_sems.at[3 * a + k],
                          (*chips[k], c)) for a in range(n) for k in peers]
        return sent, landed

    def start(ins, outs, sems):
        for cp in copies(ins, outs, *sems, started_only=True):
            cp.start()

    def finish(ins, outs, sems):
        sent, landed = copies(ins, outs, *sems)
        for cp in landed:
            cp.wait_recv()
        for cp in sent:
            cp.wait_send()

    sems = [pltpu.SemaphoreType.DMA((3 * n,)), pltpu.SemaphoreType.DMA((3 * n,))]
    if stacks is None:
        return _Rider(list(sums), _same(sums), {}, sems, start, finish)
    return _Rider(list(sums) + list(stacks), _same(sums), {n + a: a for a in range(n)}, sems, start, finish)


def _sibling_share(bufs):
    n = len(bufs)

    def copies(outs, send_sems, recv_sems, started_only=False):
        x, y, c, _ = _place()
        sent = [_remote(outs[a].at[c], outs[a].at[c], send_sems.at[a], recv_sems.at[a], (x, y, 1 - c)) for a in range(n)]
        if started_only:
            return sent
        landed = [_remote(outs[a].at[c], outs[a].at[1 - c], send_sems.at[a], recv_sems.at[a], (x, y, 1 - c)) for a in range(n)]
        return sent, landed

    def start(ins, outs, sems):
        for cp in copies(outs, *sems, started_only=True):
            cp.start()

    def finish(ins, outs, sems):
        sent, landed = copies(outs, *sems)
        for cp in landed:
            cp.wait_recv()
        for cp in sent:
            cp.wait_send()

    return _Rider(list(bufs), _same(bufs), {a: a for a in range(n)},
                  [pltpu.SemaphoreType.DMA((n,)), pltpu.SemaphoreType.DMA((n,))], start, finish)


def _pair_sum(core, part, received, name):
    _, k, rh, cols = part.shape

    def body(core_ref, p_ref, r_ref, o_ref):
        o_ref[...] = (p_ref[0].astype(F32) + r_ref[...].astype(F32)).astype(BF16)

    return pl.pallas_call(
        body, name=name,
        grid_spec=pltpu.PrefetchScalarGridSpec(
            num_scalar_prefetch=1, grid=(k,),
            in_specs=[pl.BlockSpec((1, 1, rh, cols), lambda j, core_ref: (core_ref[0], j, 0, 0)),
                      pl.BlockSpec((1, rh, cols), lambda j, core_ref: (j, 0, 0))],
            out_specs=pl.BlockSpec((1, rh, cols), lambda j, core_ref: (j, 0, 0))),
        out_shape=jax.ShapeDtypeStruct((k, rh, cols), BF16),
        compiler_params=_params(("parallel",), 32),
    )(core, part, received)


def _sum_leading(stack, steps, name):
    k, rows, cols = stack.shape
    tile = rows // steps

    def body(s_ref, o_ref):
        total = s_ref[0].astype(F32)
        for d in range(1, k):
            total = total + s_ref[d].astype(F32)
        o_ref[...] = total

    return pl.pallas_call(
        body, name=name, grid=(steps,),
        in_specs=[pl.BlockSpec((k, tile, cols), lambda i: (0, i, 0))],
        out_specs=pl.BlockSpec((tile, cols), lambda i: (i, 0)),
        out_shape=jax.ShapeDtypeStruct((rows, cols), F32),
        compiler_params=_params(("parallel",), 32),
    )(stack)


def _chip_sum(place, own, stack, steps, name):
    k, rows, cols = stack.shape
    tile = rows // steps

    def body(place_ref, own_ref, *refs):
        chip = place_ref[1]
        total = None
        for d in range(k):
            term = jnp.where(chip == d, own_ref[0], refs[d][0]).astype(F32)
            total = term if total is None else total + term
        refs[k][0] = total

    def other(d):
        return lambda i, place_ref: (jnp.where(place_ref[1] == d, (d + 1) % k, d), i, 0)

    return pl.pallas_call(
        body, name=name,
        grid_spec=pltpu.PrefetchScalarGridSpec(
            num_scalar_prefetch=1, grid=(steps,),
            in_specs=[pl.BlockSpec((1, tile, cols), lambda i, place_ref: (place_ref[1] % own.shape[0], i, 0))]
            + [pl.BlockSpec((1, tile, cols), other(d)) for d in range(k)],
            out_specs=pl.BlockSpec((1, tile, cols), lambda i, place_ref: (place_ref[0], i, 0))),
        out_shape=jax.ShapeDtypeStruct((2, rows, cols), F32),
        compiler_params=_params(("arbitrary",), 32),
    )(place, own, *([stack] * k))


def _adamw(w, g, row0, m, v, tile, name):
    rows, cols = w.shape
    first = row0 // tile
    assert rows % tile == 0 and row0 % tile == 0
    bc1 = 1.0 - ADAM_B1 ** ADAM_STEP
    bc2 = 1.0 - ADAM_B2 ** ADAM_STEP

    def body(w_ref, g_ref, m_ref, v_ref, go_ref, d_ref, mo_ref, vo_ref):
        g = g_ref[...]
        m_new = ADAM_B1 * m_ref[...] + (1.0 - ADAM_B1) * g
        v_new = ADAM_B2 * v_ref[...] + (1.0 - ADAM_B2) * (g * g)
        go_ref[...] = g
        d_ref[...] = -ADAM_LR * ((m_new / bc1) / (jnp.sqrt(v_new / bc2) + ADAM_EPS) + ADAM_WD * w_ref[...])
        mo_ref[...] = m_new
        vo_ref[...] = v_new

    spec = pl.BlockSpec((tile, cols), lambda i: (i, 0))
    g_spec = pl.BlockSpec((tile, cols), lambda i: (first + i, 0))
    return pl.pallas_call(
        body, name=name, grid=(rows // tile,),
        in_specs=[spec, g_spec, spec, spec], out_specs=[spec] * 4,
        out_shape=[jax.ShapeDtypeStruct((rows, cols), F32)] * 4,
        compiler_params=_params(("parallel",), 32),
    )(w, g, m, v)


SMALL = ("ffn1_norm", "mix_norm", "ffn2_norm", "final_norm", "pool_scale", "sink_logits", "pool_w")


def _pack_small(d, last_row=None):
    sink = jnp.pad(d["sink_logits"].reshape(1, N_HEADS), ((0, 0), (0, LANES - N_HEADS)))
    rows = [d[n].reshape(-1, LANES) for n in SMALL[:5]] + [sink, d["pool_w"].reshape(-1, LANES)]
    used = sum(r.shape[0] for r in rows)
    last = jnp.zeros((1, LANES), F32) if last_row is None else last_row
    return jnp.concatenate(rows + [jnp.zeros((SMALL_ROWS - used - 1, LANES), F32), last], axis=0)


def _unpack_small(packed, like):
    out, row = {}, 0
    for n in SMALL:
        size = LANES if n == "sink_logits" else math.prod(like[n].shape)
        chunk = packed[row:row + size // LANES].reshape(-1)
        out[n] = (chunk[:N_HEADS] if n == "sink_logits" else chunk).reshape(like[n].shape)
        row += size // LANES
    return out


def kernel(x, ffn1_norm, ffn1_w_gate, ffn1_w_up, ffn1_w_down, mix_norm, w_in, sink_logits, pool_w, pool_scale, w_out, ffn2_norm, ffn2_w_gate, ffn2_w_up, ffn2_w_down, final_norm, loss_target, m_ffn1_norm, m_ffn1_w_gate, m_ffn1_w_up, m_ffn1_w_down, m_mix_norm, m_w_in, m_sink_logits, m_pool_w, m_pool_scale, m_w_out, m_ffn2_norm, m_ffn2_w_gate, m_ffn2_w_up, m_ffn2_w_down, m_final_norm, v_ffn1_norm, v_ffn1_w_gate, v_ffn1_w_up, v_ffn1_w_down, v_mix_norm, v_w_in, v_sink_logits, v_pool_w, v_pool_scale, v_w_out, v_ffn2_norm, v_ffn2_w_gate, v_ffn2_w_up, v_ffn2_w_down, v_final_norm):
    names = ("ffn1_norm", "ffn1_w_gate", "ffn1_w_up", "ffn1_w_down", "mix_norm", "w_in", "sink_logits", "pool_w",
             "pool_scale", "w_out", "ffn2_norm", "ffn2_w_gate", "ffn2_w_up", "ffn2_w_down", "final_norm")
    weights = dict(zip(names, (ffn1_norm, ffn1_w_gate, ffn1_w_up, ffn1_w_down, mix_norm, w_in, sink_logits, pool_w,
                               pool_scale, w_out, ffn2_norm, ffn2_w_gate, ffn2_w_up, ffn2_w_down, final_norm)))
    mom1 = dict(zip(names, (m_ffn1_norm, m_ffn1_w_gate, m_ffn1_w_up, m_ffn1_w_down, m_mix_norm, m_w_in, m_sink_logits,
                            m_pool_w, m_pool_scale, m_w_out, m_ffn2_norm, m_ffn2_w_gate, m_ffn2_w_up, m_ffn2_w_down,
                            m_final_norm)))
    mom2 = dict(zip(names, (v_ffn1_norm, v_ffn1_w_gate, v_ffn1_w_up, v_ffn1_w_down, v_mix_norm, v_w_in, v_sink_logits,
                            v_pool_w, v_pool_scale, v_w_out, v_ffn2_norm, v_ffn2_w_gate, v_ffn2_w_up, v_ffn2_w_down,
                            v_final_norm)))
    chip = (2 * lax.axis_index("x") + lax.axis_index("y")).astype(jnp.int32).reshape(1)
    place = jnp.concatenate([lax.axis_index("c").astype(jnp.int32).reshape(1), chip])

    def rows_of(t, n):
        return jnp.swapaxes(t[n][0], 0, 1) if n in TRANSPOSED else t[n][0]

    bufs = [_pack(chip, [rows_of(weights, n) for n in GROUPS[0]], "pack_ffn1"),
            _pack(chip, [jnp.concatenate([rows_of(weights, n) for n in GROUPS[1]], axis=0)], "pack_mix"),
            _pack(chip, [rows_of(weights, n) for n in GROUPS[2]], "pack_ffn2")]

    small_w = {"ffn1_norm": ffn1_norm, "mix_norm": mix_norm, "ffn2_norm": ffn2_norm,
               "final_norm": final_norm.reshape(1, D_MODEL), "pool_scale": pool_scale, "sink_logits": sink_logits,
               "pool_w": pool_w[0]}
    loss, grad_x, group_grads, small_sum = _step(x[0], loss_target[0], bufs, small_w, place)

    out_g, out_d, out_m, out_v = {}, {}, {}, {}
    for members, g in zip(GROUPS, group_grads):
        row0 = 0
        for n in members:
            w = rows_of(weights, n)
            tile = FF_CHUNK // 4 if w.shape[0] == FF_CHUNK else math.gcd(IN_ROWS, OUT_ROWS)
            outs = _adamw(w, g, row0, rows_of(mom1, n), rows_of(mom2, n), tile, "adamw_" + n)
            row0 += w.shape[0]
            for dst, t in zip((out_g, out_d, out_m, out_v), outs):
                dst[n] = (jnp.swapaxes(t, 0, 1) if n in TRANSPOSED else t).reshape(weights[n].shape)
    small_outs = _adamw(_pack_small(weights), small_sum, 0, _pack_small(mom1), _pack_small(mom2), SMALL_ROWS, "adamw_small")
    for dst, packed in zip((out_g, out_d, out_m, out_v), small_outs):
        dst.update(_unpack_small(packed, weights))

    return (loss,grad_x.reshape(x.shape), *[out_g[n] for n in names], *[out_d[n] for n in names],
            *[out_m[n] for n in names], *[out_v[n] for n in names])
```

```python
import collections
import functools
import math

import jax
import jax.numpy as jnp
from jax import lax
from jax.experimental import pallas as pl
from jax.experimental.pallas import tpu as pltpu

F32, BF16 = jnp.float32, jnp.bfloat16
MESH = pl.DeviceIdType.MESH

D_MODEL = 1024
D_FF = 2816
N_CHIPS = 4
FF_CHUNK = D_FF // N_CHIPS
HEAD_DIM = 64
N_HEADS = 8
N_KV = 2
Q_PER_KV = N_HEADS // N_KV
KV_WIDTH = N_KV * HEAD_DIM
ATTN_WIDTH = N_HEADS * HEAD_DIM
POOL_WINDOWS = (2, 4, 8, 16)
N_POOL = len(POOL_WINDOWS)
POOL_GROUP = 128
POOL_WIDTH = N_POOL * POOL_GROUP
IN_WIDTH = ATTN_WIDTH + 2 * KV_WIDTH + POOL_WIDTH
WINDOW = 128
BLOCK = 128
BAND = 3 * BLOCK
ROPE_THETA = 500000.0
ROTARY_DIM = HEAD_DIM // 4
EPS = 1e-6
LANES = 128
Q_PAD = N_HEADS * LANES
U_PAD = Q_PAD + 2 * KV_WIDTH + POOL_WIDTH
SCALE = HEAD_DIM ** -0.5
NEG = -1e30

ADAM_LR, ADAM_B1, ADAM_B2, ADAM_EPS, ADAM_WD, ADAM_STEP = 0.001, 0.9, 0.999, 1e-08, 0.01, 10

V7X_VMEM_BYTES = 64 * 1024 * 1024
TOK_TILE = 512
SMALL_ROWS = 552


def _params(sem, vmem_mb):
    assert vmem_mb * 1024 * 1024 <= V7X_VMEM_BYTES
    return pltpu.CompilerParams(dimension_semantics=sem, vmem_limit_bytes=vmem_mb * 1024 * 1024)


def _dot(a, b):
    return lax.dot_general(a, b, (((1,), (0,)), ((), ())), preferred_element_type=F32)


def _dot_nt(a, b):
    return lax.dot_general(a, b, (((1,), (1,)), ((), ())), preferred_element_type=F32)


def _dot_tn(a, b):
    return lax.dot_general(a, b, (((0,), (0,)), ((), ())), preferred_element_type=F32)


def _rms_stats(h):
    r = lax.rsqrt(jnp.mean(h * h, axis=-1, keepdims=True) + EPS)
    return r, h * r


def _rms_bwd(dn, g, r, xh):
    gd = dn * g
    dh = r * (gd - xh * jnp.mean(gd * xh, axis=-1, keepdims=True))
    return dh, jnp.sum(dn * xh, axis=0, keepdims=True)


def _rope(x, c, s1, s2):
    return x * c + pltpu.roll(x, LANES - ROTARY_DIM // 2, 1) * s1 + pltpu.roll(x, ROTARY_DIM // 2, 1) * s2


def _rope_bwd(d, c, s1, s2):
    return d * c + pltpu.roll(d * s1, ROTARY_DIM // 2, 1) + pltpu.roll(d * s2, LANES - ROTARY_DIM // 2, 1)


def _sum_chunks(refs):
    terms = [ref[j].astype(F32) for ref in refs for j in range(ref.shape[0])]
    return functools.reduce(lambda a, b: a + b, terms)


def _chunk_rows(tile, k):
    return pl.BlockSpec((k, tile, D_MODEL), lambda i, *_: (0, i, 0))


def _full(shape):
    nd = len(shape)
    return pl.BlockSpec(shape, lambda *_: (0,) * nd)


def _rows(tile, cols):
    return pl.BlockSpec((tile, cols), lambda i, *_: (i, 0))


HBM_SPEC = pl.BlockSpec(memory_space=pltpu.HBM)

_Rider = collections.namedtuple("_Rider", "operands out_shapes aliases scratch start finish hooks", defaults=[()])


_NO_RIDER = _Rider([], [], {}, [], None, None)


def _call(body, name, grid, in_specs, out_specs, out_shape, scratch, vmem_mb, args, rider=None, prefetch=(),
          shares_rider_refs=False):
    rider = rider or _NO_RIDER
    n_pre, n_in, n_out, n_scr = len(prefetch), len(in_specs), len(out_specs), len(scratch)
    r_in, r_out = len(rider.operands), len(rider.out_shapes)

    def fused(*refs):
        pre, refs = refs[:n_pre], refs[n_pre:]
        ins, refs = refs[:n_in], refs[n_in:]
        r_ins, refs = refs[:r_in], refs[r_in:]
        outs, refs = refs[:n_out], refs[n_out:]
        r_outs, refs = refs[:r_out], refs[r_out:]
        scr, r_scr = refs[:n_scr], refs[n_scr:]
        ids = [pl.program_id(d) for d in range(len(grid))]
        if rider.start is not None:
            @pl.when(functools.reduce(jnp.logical_and, [i == 0 for i in ids]))
            def _():
                rider.start(r_ins, r_outs, r_scr)

        for at, hook in rider.hooks:
            @pl.when(functools.reduce(jnp.logical_and, [i == a for i, a in zip(ids, at)]))
            def _(hook=hook):
                hook(r_ins, r_outs, r_scr)

        if shares_rider_refs:
            body(*pre, *ins, *outs, *scr, rider_refs=r_outs)
        else:
            body(*pre, *ins, *outs, *scr)

        if rider.finish is not None:
            @pl.when(functools.reduce(jnp.logical_and, [i == g - 1 for i, g in zip(ids, grid)]))
            def _():
                rider.finish(r_ins, r_outs, r_scr)

    return pl.pallas_call(
        fused, name=name,
        grid_spec=pltpu.PrefetchScalarGridSpec(
            num_scalar_prefetch=n_pre, grid=grid,
            in_specs=list(in_specs) + [HBM_SPEC] * r_in, out_specs=list(out_specs) + [HBM_SPEC] * r_out,
            scratch_shapes=list(scratch) + list(rider.scratch)),
        out_shape=list(out_shape) + list(rider.out_shapes),
        input_output_aliases={n_pre + n_in + k: n_out + v for k, v in rider.aliases.items()},
        compiler_params=_params(("arbitrary",) * len(grid), vmem_mb),
    )(*prefetch, *args, *rider.operands)


def _comm_call(rider, name):
    r_in, r_out = len(rider.operands), len(rider.out_shapes)

    def body(*refs):
        r_ins, r_outs, r_scr = refs[:r_in], refs[r_in:r_in + r_out], refs[r_in + r_out:]
        rider.start(r_ins, r_outs, r_scr)
        rider.finish(r_ins, r_outs, r_scr)

    return pl.pallas_call(
        body, name=name, in_specs=[HBM_SPEC] * r_in, out_specs=[HBM_SPEC] * r_out, out_shape=list(rider.out_shapes),
        input_output_aliases=dict(rider.aliases), scratch_shapes=list(rider.scratch),
    )(*rider.operands)


def _ffn_fwd(order, h, gain, name, rider, group_at):
    S = h.shape[0]
    tile = min(TOK_TILE, S)
    nt = S // tile
    last = N_CHIPS - 1

    def body(order_ref, h_ref, g_ref, ho_ref, n_ref, gate_ref, up_ref, w_scr, w_sem, acc, n_scr, rider_refs):
        j, i = pl.program_id(0), pl.program_id(1)

        @pl.when(i == 0)
        def _():
            fetch = pltpu.make_async_copy(rider_refs[group_at].at[order_ref[j]], w_scr, w_sem)
            fetch.start()
            fetch.wait()

        at = pl.multiple_of(i * tile, tile)

        @pl.when(j == 0)
        def _():
            _, xh = _rms_stats(h_ref[...])
            n = (xh * g_ref[...]).astype(BF16)
            n_scr[pl.ds(at, tile), :] = n
            n_ref[...] = n
            acc[pl.ds(at, tile), :] = jnp.zeros((tile, D_MODEL), F32)

        half = tile // 2
        wg, wu, wd = (w_scr[part * FF_CHUNK:(part + 1) * FF_CHUNK, :] for part in range(3))
        ns = [n_scr[pl.ds(at + s * half, half), :] for s in range(2)]
        gates = [_dot_nt(n, wg) for n in ns]
        ups = [_dot_nt(n, wu) for n in ns]
        acts = [(g * jax.nn.sigmoid(g) * u).astype(BF16) for g, u in zip(gates, ups)]
        for s in range(2):
            gate_ref[0, s * half:(s + 1) * half, :] = gates[s].astype(BF16)
            up_ref[0, s * half:(s + 1) * half, :] = ups[s].astype(BF16)
            acc[pl.ds(at + s * half, half), :] += _dot(acts[s], wd)

        @pl.when(j == last)
        def _():
            ho_ref[...] = h_ref[...] + 0.5 * acc[pl.ds(at, tile), :]

    tok = pl.BlockSpec((tile, D_MODEL), lambda j, i, order_ref: (i, 0))
    hid = pl.BlockSpec((1, tile, FF_CHUNK), lambda j, i, order_ref: (order_ref[j], i, 0))
    return _call(
        body, name, (N_CHIPS, nt),
        [tok, pl.BlockSpec((1, D_MODEL), lambda j, i, order_ref: (0, 0))],
        [pl.BlockSpec((tile, D_MODEL), lambda j, i, order_ref: (jnp.where(j == last, i, 0), 0)),
         pl.BlockSpec((tile, D_MODEL), lambda j, i, order_ref: (jnp.where(j == 0, i, nt - 1), 0)),
         hid, hid],
        [jax.ShapeDtypeStruct((S, D_MODEL), F32), jax.ShapeDtypeStruct((S, D_MODEL), BF16),
         jax.ShapeDtypeStruct((N_CHIPS, S, FF_CHUNK), BF16), jax.ShapeDtypeStruct((N_CHIPS, S, FF_CHUNK), BF16)],
        [pltpu.VMEM((3 * FF_CHUNK, D_MODEL), BF16), pltpu.SemaphoreType.DMA, pltpu.VMEM((S, D_MODEL), F32),
         pltpu.VMEM((S, D_MODEL), BF16)], 58, (h, gain), rider, (order,), shares_rider_refs=True)


def _ffn_bwd(chunks, d_out, n, gate, up, group, name, rider=None):
    S = n.shape[0]
    n_chunks = chunks.shape[0]
    tile = min(TOK_TILE, S)
    nt = S // tile
    half_rows = 3 * FF_CHUNK // 2
    cut = FF_CHUNK // 2

    def body(chunks_ref, do_ref, n_ref, gate_ref, up_ref, wg_ref, wu_ref, wd_ref, dn_ref, dw_ref, acc_g, acc_u, acc_d):
        j, i = pl.program_id(0), pl.program_id(1)

        @pl.when(i == 0)
        def _():
            acc_g[...] = jnp.zeros_like(acc_g)
            acc_u[...] = jnp.zeros_like(acc_u)
            acc_d[...] = jnp.zeros_like(acc_d)

        halves = [pl.ds(s * (tile // 2), tile // 2) for s in range(2)]
        dos = [do_ref[rows, :] for rows in halves]
        d_acts = [_dot_nt(do, wd_ref[0]) for do in dos]
        gs = [gate_ref[0, rows, :].astype(F32) for rows in halves]
        us = [up_ref[0, rows, :].astype(F32) for rows in halves]
        sigs = [jax.nn.sigmoid(g) for g in gs]
        silus = [g * sig for g, sig in zip(gs, sigs)]
        d_ups = [(d_act * silu).astype(BF16) for d_act, silu in zip(d_acts, silus)]
        d_gates = [(d_act * u * (sig * (1.0 + g * (1.0 - sig)))).astype(BF16) for d_act, u, sig, g in zip(d_acts, us, sigs, gs)]
        for rows, d_gate, d_up in zip(halves, d_gates, d_ups):
            dn_ref[0, rows, :] = (_dot(d_gate, wg_ref[0]) + _dot(d_up, wu_ref[0])).astype(BF16)
        d_gate, d_up = jnp.concatenate(d_gates, axis=0), jnp.concatenate(d_ups, axis=0)
        act = jnp.concatenate([(silu * u).astype(BF16) for silu, u in zip(silus, us)], axis=0)
        nn = n_ref[...]
        acc_g[...] += _dot_tn(d_gate, nn)
        acc_u[...] += _dot_tn(d_up, nn)
        acc_d[...] += _dot_tn(act, do_ref[...])

        @pl.when(i == nt - 1)
        def _():
            dw_ref[0, 0, :FF_CHUNK, :] = acc_g[...].astype(BF16)
            dw_ref[0, 0, FF_CHUNK:, :] = acc_u[:cut, :].astype(BF16)
            dw_ref[1, 0, :cut, :] = acc_u[cut:, :].astype(BF16)
            dw_ref[1, 0, cut:, :] = acc_d[...].astype(BF16)

    tok = pl.BlockSpec((tile, D_MODEL), lambda j, i, chunks_ref: (i, 0))
    hid = pl.BlockSpec((1, tile, FF_CHUNK), lambda j, i, chunks_ref: (chunks_ref[j], i, 0))
    return _call(
        body, name, (n_chunks, nt),
        [tok, tok, hid, hid]
        + [pl.BlockSpec((1, FF_CHUNK, D_MODEL), functools.partial(lambda j, i, chunks_ref, part: (chunks_ref[j], part, 0), part=part))
           for part in range(3)],
        [pl.BlockSpec((1, tile, D_MODEL), lambda j, i, chunks_ref: (j, i, 0)),
         pl.BlockSpec((2, 1, half_rows, D_MODEL), lambda j, i, chunks_ref: (0, j, 0, 0))],
        [jax.ShapeDtypeStruct((n_chunks, S, D_MODEL), BF16), jax.ShapeDtypeStruct((2, n_chunks, half_rows, D_MODEL), BF16)],
        [pltpu.VMEM((FF_CHUNK, D_MODEL), F32)] * 3, 56, (d_out, n, gate, up, group, group, group), rider, (chunks,))


def _mix_in(h, gain, w_in, rc, rs1, rs2, name):
    S = h.shape[0]
    tile = min(TOK_TILE, S)

    def body(h_ref, g_ref, w_ref, c_ref, s1_ref, s2_ref, n_ref, q_ref, k_ref, v_ref, pc_ref):
        _, xh = _rms_stats(h_ref[...])
        n = (xh * g_ref[...]).astype(BF16)
        n_ref[...] = n
        u = _dot_nt(n, w_ref[...])
        c, s1, s2 = c_ref[...], s1_ref[...], s2_ref[...]
        q_ref[...] = jnp.concatenate([(_rope(u[:, hd * LANES:(hd + 1) * LANES], c, s1, s2) * SCALE).astype(BF16)
                                      for hd in range(N_HEADS)], axis=1)
        k_ref[...] = _rope(u[:, Q_PAD:Q_PAD + KV_WIDTH], c, s1, s2).astype(BF16)
        v_ref[...] = u[:, Q_PAD + KV_WIDTH:Q_PAD + 2 * KV_WIDTH].astype(BF16)
        pc_ref[...] = u[:, Q_PAD + 2 * KV_WIDTH:]

    return pl.pallas_call(
        body, name=name, grid=(S // tile,),
        in_specs=[_rows(tile, D_MODEL), _full((1, D_MODEL)), _full((U_PAD, D_MODEL)),
                  _rows(tile, LANES), _rows(tile, LANES), _rows(tile, LANES)],
        out_specs=[_rows(tile, D_MODEL), _rows(tile, Q_PAD), _rows(tile, KV_WIDTH), _rows(tile, KV_WIDTH),
                   _rows(tile, POOL_WIDTH)],
        out_shape=[jax.ShapeDtypeStruct((S, D_MODEL), BF16), jax.ShapeDtypeStruct((S, Q_PAD), BF16),
                   jax.ShapeDtypeStruct((S, KV_WIDTH), BF16), jax.ShapeDtypeStruct((S, KV_WIDTH), BF16),
                   jax.ShapeDtypeStruct((S, POOL_WIDTH), F32)],
        compiler_params=_params(("parallel",), 40),
    )(h, gain, w_in, rc, rs1, rs2)


def _band_start(i, S):
    return pl.multiple_of(jnp.clip((i - 1) * BLOCK, 0, S - BAND), BLOCK)


def _window_bias(off):
    r = lax.broadcasted_iota(jnp.int32, (BLOCK, 1), 0)
    c = lax.broadcasted_iota(jnp.int32, (1, BAND), 1)
    return jnp.where(jnp.abs(off + r - c) <= WINDOW, 0.0, NEG).astype(F32)


def _softmax_parts(qh, kb, bias, sink_h):
    s = _dot_nt(qh, kb) + bias
    m = jnp.maximum(jnp.max(s, axis=-1, keepdims=True), sink_h)
    p = jnp.exp(s - m)
    es = jnp.exp(sink_h - m)
    return p, es, 1.0 / (jnp.sum(p, axis=-1, keepdims=True) + es)


def _pool_matrix(t0, start, S, w):
    r = lax.broadcasted_iota(jnp.int32, (BLOCK, 1), 0) + t0
    c = lax.broadcasted_iota(jnp.int32, (1, BAND), 1) + start
    half = w // 2

    def window(lo, hi):
        a = jnp.maximum(lo, 0)
        b = jnp.minimum(hi + 1, S)
        return jnp.where((c >= a) & (c < b), 1.0 / (b - a).astype(F32), 0.0)

    return (0.5 * (window(r - half, r + half - 1) + window(r - half + 1, r + half))).astype(BF16)


def _pool_matrices(S):
    blocks = ((0, 0), (BLOCK, 0), (S - BLOCK, S - BAND))
    return jnp.stack([jnp.stack([_pool_matrix(t0, start, S, w) for w in POOL_WINDOWS]) for t0, start in blocks])


def _pool_spec(nb):
    return pl.BlockSpec((1, N_POOL, BLOCK, BAND), lambda i, *_: (jnp.where(i == 0, 0, jnp.where(i == nb - 1, 2, 1)), 0, 0, 0))


def _mix_core_fwd(q, k, v, pc, sink, pool_m, pool_w, pool_scale, name, rider=None):
    S = q.shape[0]
    nb = S // BLOCK

    def body(sink_ref, q_ref, k_ref, v_ref, pc_ref, pm_ref, pw_ref, ps_ref, a_ref, p_ref):
        i = pl.program_id(0)
        start = _band_start(i, S)
        band = pl.ds(start, BAND)
        bias = _window_bias(i * BLOCK - start)
        kb, vb = k_ref[band, :], v_ref[band, :]
        hs = range(N_HEADS)
        ss = [_dot_nt(q_ref[:, hd * LANES:(hd + 1) * LANES], kb) + bias for hd in hs]
        ms = [jnp.maximum(jnp.max(ss[hd], axis=-1, keepdims=True), sink_ref[0, hd]) for hd in hs]
        ps = [jnp.exp(ss[hd] - ms[hd]) for hd in hs]
        invs = [1.0 / (jnp.sum(ps[hd], axis=-1, keepdims=True) + jnp.exp(sink_ref[0, hd] - ms[hd])) for hd in hs]
        outs = [_dot(ps[hd].astype(BF16), vb) for hd in hs]
        a_ref[...] = jnp.concatenate([(outs[hd] * invs[hd]).astype(BF16) for hd in hs], axis=1)
        centre = pl.ds(pl.multiple_of(i * BLOCK, BLOCK), BLOCK)
        gs = range(N_POOL)
        sl = [slice(g * POOL_GROUP, (g + 1) * POOL_GROUP) for g in gs]
        means = [_dot(pm_ref[0, g], pc_ref[band, sl[g]].astype(BF16)) for g in gs]
        devs = [(means[g] - pc_ref[centre, sl[g]]).astype(BF16) for g in gs]
        p_ref[...] = (jnp.concatenate([_dot(devs[g], pw_ref[g]) for g in gs], axis=1) * ps_ref[...]).astype(BF16)

    return _call(
        body, name, (nb,),
        [pl.BlockSpec(memory_space=pltpu.SMEM), _rows(BLOCK, Q_PAD), _full((S, KV_WIDTH)), _full((S, KV_WIDTH)),
         _full((S, POOL_WIDTH)), _pool_spec(nb), _full((N_POOL, POOL_GROUP, POOL_GROUP)), _full((1, POOL_WIDTH))],
        [_rows(BLOCK, Q_PAD), _rows(BLOCK, POOL_WIDTH)],
        [jax.ShapeDtypeStruct((S, Q_PAD), BF16), jax.ShapeDtypeStruct((S, POOL_WIDTH), BF16)],
        [], 40, (sink, q, k, v, pc, pool_m, pool_w, pool_scale), rider)


def _mix_core_bwd(q, k, v, pc, da, dp, sink, pool_m, pool_w, pool_scale, rc, rs1, rs2, name, rider=None):
    S = q.shape[0]
    nb = S // BLOCK

    def body(sink_ref, q_ref, k_ref, v_ref, pc_ref, da_ref, dp_ref, pm_ref, pw_ref, ps_ref, c_ref, s1_ref, s2_ref,
             dq_ref, dk_ref, dv_ref, dpc_ref, dsink_ref, dpw_ref, dps_ref):
        i = pl.program_id(0)

        @pl.when(i == 0)
        def _():
            dk_ref[...] = jnp.zeros_like(dk_ref)
            dv_ref[...] = jnp.zeros_like(dv_ref)
            dpc_ref[...] = jnp.zeros_like(dpc_ref)
            dsink_ref[...] = jnp.zeros_like(dsink_ref)
            dpw_ref[...] = jnp.zeros_like(dpw_ref)
            dps_ref[...] = jnp.zeros_like(dps_ref)

        start = _band_start(i, S)
        band = pl.ds(start, BAND)
        bias = _window_bias(i * BLOCK - start)
        kb, vb = k_ref[band, :], v_ref[band, :]
        c, s1, s2 = c_ref[...], s1_ref[...], s2_ref[...]
        lane = lax.broadcasted_iota(jnp.int32, (1, LANES), 1)
        hs = range(N_HEADS)
        qs = [q_ref[:, hd * LANES:(hd + 1) * LANES] for hd in hs]
        das = [da_ref[:, hd * LANES:(hd + 1) * LANES] for hd in hs]
        ss = [_dot_nt(qs[hd], kb) + bias for hd in hs]
        d_probs = [_dot_nt(das[hd], vb) for hd in hs]
        ms = [jnp.maximum(jnp.max(ss[hd], axis=-1, keepdims=True), sink_ref[0, hd]) for hd in hs]
        ps = [jnp.exp(ss[hd] - ms[hd]) for hd in hs]
        ess = [jnp.exp(sink_ref[0, hd] - ms[hd]) for hd in hs]
        invs = [1.0 / (jnp.sum(ps[hd], axis=-1, keepdims=True) + ess[hd]) for hd in hs]
        probs = [ps[hd] * invs[hd] for hd in hs]
        deltas = [jnp.sum(probs[hd] * d_probs[hd], axis=-1, keepdims=True) for hd in hs]
        d_ss = [(probs[hd] * (d_probs[hd] - deltas[hd])).astype(BF16) for hd in hs]
        dqs = [_dot(d_ss[hd], kb) for hd in hs]
        dq_ref[...] = jnp.concatenate([_rope_bwd(dqs[hd] * SCALE, c, s1, s2).astype(BF16) for hd in hs], axis=1)
        dks = [_dot_tn(d_ss[hd], qs[hd]) for hd in hs]
        dvs = [_dot_tn(probs[hd].astype(BF16), das[hd]) for hd in hs]
        dk_ref[band, :] += functools.reduce(lambda a, b: a + b, dks)
        dv_ref[band, :] += functools.reduce(lambda a, b: a + b, dvs)
        dsink_ref[...] += functools.reduce(lambda a, b: a + b, [
            jnp.where(lane == hd, -jnp.sum(ess[hd] * invs[hd] * deltas[hd], axis=0, keepdims=True), 0.0) for hd in hs])

        centre = pl.ds(pl.multiple_of(i * BLOCK, BLOCK), BLOCK)
        gs = range(N_POOL)
        sl = [slice(g * POOL_GROUP, (g + 1) * POOL_GROUP) for g in gs]
        devs = [(_dot(pm_ref[0, g], pc_ref[band, sl[g]].astype(BF16)) - pc_ref[centre, sl[g]]).astype(BF16) for g in gs]
        dys = [dp_ref[:, sl[g]].astype(F32) for g in gs]
        zs = [_dot(devs[g], pw_ref[g]) for g in gs]
        dzs = [(dys[g] * ps_ref[:, sl[g]]).astype(BF16) for g in gs]
        d_devs = [_dot_nt(dzs[g], pw_ref[g]) for g in gs]
        dps_ref[...] += jnp.concatenate([jnp.sum(dys[g] * zs[g], axis=0, keepdims=True) for g in gs], axis=1)
        for g in gs:
            dpw_ref[g] += _dot_tn(devs[g], dzs[g])
        dpc_ref[band, :] += jnp.concatenate([_dot_tn(pm_ref[0, g], d_devs[g].astype(BF16)) for g in gs], axis=1)
        dpc_ref[centre, :] -= jnp.concatenate(d_devs, axis=1)

    return _call(
        body, name, (nb,),
        [pl.BlockSpec(memory_space=pltpu.SMEM), _rows(BLOCK, Q_PAD), _full((S, KV_WIDTH)), _full((S, KV_WIDTH)),
         _full((S, POOL_WIDTH)), _rows(BLOCK, Q_PAD), _rows(BLOCK, POOL_WIDTH), _pool_spec(nb),
         _full((N_POOL, POOL_GROUP, POOL_GROUP)), _full((1, POOL_WIDTH)),
         _rows(BLOCK, LANES), _rows(BLOCK, LANES), _rows(BLOCK, LANES)],
        [_rows(BLOCK, Q_PAD), _full((S, KV_WIDTH)), _full((S, KV_WIDTH)), _full((S, POOL_WIDTH)),
         _full((1, LANES)), _full((N_POOL, POOL_GROUP, POOL_GROUP)), _full((1, POOL_WIDTH))],
        [jax.ShapeDtypeStruct((S, Q_PAD), BF16), jax.ShapeDtypeStruct((S, KV_WIDTH), F32),
         jax.ShapeDtypeStruct((S, KV_WIDTH), F32), jax.ShapeDtypeStruct((S, POOL_WIDTH), F32),
         jax.ShapeDtypeStruct((1, LANES), F32), jax.ShapeDtypeStruct((N_POOL, POOL_GROUP, POOL_GROUP), F32),
         jax.ShapeDtypeStruct((1, POOL_WIDTH), F32)],
        [], 56, (sink, q, k, v, pc, da, dp, pool_m, pool_w, pool_scale, rc, rs1, rs2), rider)


def _mix_out(h, a, p, wa, wp, name):
    S = h.shape[0]
    tile = min(TOK_TILE, S)

    def body(h_ref, a_ref, p_ref, wa_ref, wp_ref, o_ref):
        o_ref[...] = h_ref[...] + _dot(a_ref[...], wa_ref[...]) + _dot(p_ref[...], wp_ref[...])

    return pl.pallas_call(
        body, name=name, grid=(S // tile,),
        in_specs=[_rows(tile, D_MODEL), _rows(tile, Q_PAD), _rows(tile, POOL_WIDTH),
                  _full((Q_PAD, D_MODEL)), _full((POOL_WIDTH, D_MODEL))],
        out_specs=_rows(tile, D_MODEL),
        out_shape=jax.ShapeDtypeStruct((S, D_MODEL), F32),
        compiler_params=_params(("parallel",), 40),
    )(h, a, p, wa, wp)


def _loss_head(h, target, gain, name):
    S = h.shape[0]
    tile = min(TOK_TILE, S)

    def body(h_ref, t_ref, g_ref, dh_ref, dhalf_ref, loss_ref, dg_ref):
        @pl.when(pl.program_id(0) == 0)
        def _():
            loss_ref[...] = jnp.zeros_like(loss_ref)
            dg_ref[...] = jnp.zeros_like(dg_ref)

        g = g_ref[...]
        r, xh = _rms_stats(h_ref[...])
        err = xh * g - t_ref[...]
        loss_ref[...] += (0.5 / D_MODEL) * jnp.sum(err * err, axis=0, keepdims=True)
        dh, dg = _rms_bwd(err * (1.0 / D_MODEL), g, r, xh)
        dg_ref[...] += dg
        dh_ref[...] = dh
        dhalf_ref[...] = (0.5 * dh).astype(BF16)

    return pl.pallas_call(
        body, name=name, grid=(S // tile,),
        in_specs=[_rows(tile, D_MODEL), _rows(tile, D_MODEL), _full((1, D_MODEL))],
        out_specs=[_rows(tile, D_MODEL), _rows(tile, D_MODEL), _full((1, D_MODEL)), _full((1, D_MODEL))],
        out_shape=[jax.ShapeDtypeStruct((S, D_MODEL), F32), jax.ShapeDtypeStruct((S, D_MODEL), BF16),
                   jax.ShapeDtypeStruct((1, D_MODEL), F32), jax.ShapeDtypeStruct((1, D_MODEL), F32)],
        compiler_params=_params(("arbitrary",), 40),
    )(h, target, gain)


def _mix_out_bwd(dh_out, dn, h, gain, a, p, wa, wp, name, rider=None):
    S = h.shape[0]
    tile = min(TOK_TILE, S)

    def body(do_ref, dn_ref, h_ref, g_ref, a_ref, p_ref, wa_ref, wp_ref, dh_ref, da_ref, dp_ref, dwa_ref, dwp_ref, dg_ref):
        @pl.when(pl.program_id(0) == 0)
        def _():
            dwa_ref[...] = jnp.zeros_like(dwa_ref)
            dwp_ref[...] = jnp.zeros_like(dwp_ref)
            dg_ref[...] = jnp.zeros_like(dg_ref)

        r, xh = _rms_stats(h_ref[...])
        dnorm, dg = _rms_bwd(_sum_chunks([dn_ref]), g_ref[...], r, xh)
        dh = do_ref[...] + dnorm
        dg_ref[...] += dg
        dh_ref[...] = dh
        dhb = dh.astype(BF16)
        da_ref[...] = _dot_nt(dhb, wa_ref[...]).astype(BF16)
        dp_ref[...] = _dot_nt(dhb, wp_ref[...]).astype(BF16)
        dwa_ref[...] += _dot_tn(a_ref[...], dhb)
        dwp_ref[...] += _dot_tn(p_ref[...], dhb)

    return _call(
        body, name, (S // tile,),
        [_rows(tile, D_MODEL), _chunk_rows(tile, dn.shape[0]), _rows(tile, D_MODEL), _full((1, D_MODEL)),
         _rows(tile, Q_PAD), _rows(tile, POOL_WIDTH), _full((Q_PAD, D_MODEL)), _full((POOL_WIDTH, D_MODEL))],
        [_rows(tile, D_MODEL), _rows(tile, Q_PAD), _rows(tile, POOL_WIDTH),
         _full((Q_PAD, D_MODEL)), _full((POOL_WIDTH, D_MODEL)), _full((1, D_MODEL))],
        [jax.ShapeDtypeStruct((S, D_MODEL), F32), jax.ShapeDtypeStruct((S, Q_PAD), BF16),
         jax.ShapeDtypeStruct((S, POOL_WIDTH), BF16), jax.ShapeDtypeStruct((Q_PAD, D_MODEL), F32),
         jax.ShapeDtypeStruct((POOL_WIDTH, D_MODEL), F32), jax.ShapeDtypeStruct((1, D_MODEL), F32)],
        [], 48, (dh_out, dn, h, gain, a, p, wa, wp), rider)


def _mix_in_bwd(dh_out, h, gain, n, dq, dk, dv, dpc, rc, rs1, rs2, w_in, name, rider=None):
    S = h.shape[0]
    tile = min(TOK_TILE, S)

    def body(do_ref, h_ref, g_ref, n_ref, dq_ref, dk_ref, dv_ref, dpc_ref, c_ref, s1_ref, s2_ref, w_ref,
             dh_ref, dhalf_ref, dw_ref, dg_ref):
        @pl.when(pl.program_id(0) == 0)
        def _():
            dw_ref[...] = jnp.zeros_like(dw_ref)
            dg_ref[...] = jnp.zeros_like(dg_ref)

        dk = _rope_bwd(dk_ref[...], c_ref[...], s1_ref[...], s2_ref[...]).astype(BF16)
        du = jnp.concatenate([dq_ref[...], dk, dv_ref[...].astype(BF16), dpc_ref[...].astype(BF16)], axis=1)
        dn = _dot(du, w_ref[...])
        dw_ref[...] += _dot_tn(du, n_ref[...])
        r, xh = _rms_stats(h_ref[...])
        dnorm, dg = _rms_bwd(dn, g_ref[...], r, xh)
        dh = do_ref[...] + dnorm
        dg_ref[...] += dg
        dh_ref[...] = dh
        dhalf_ref[...] = (0.5 * dh).astype(BF16)

    return _call(
        body, name, (S // tile,),
        [_rows(tile, D_MODEL), _rows(tile, D_MODEL), _full((1, D_MODEL)), _rows(tile, D_MODEL),
         _rows(tile, Q_PAD), _rows(tile, KV_WIDTH), _rows(tile, KV_WIDTH), _rows(tile, POOL_WIDTH),
         _rows(tile, LANES), _rows(tile, LANES), _rows(tile, LANES), _full((U_PAD, D_MODEL))],
        [_rows(tile, D_MODEL), _rows(tile, D_MODEL), _full((U_PAD, D_MODEL)), _full((1, D_MODEL))],
        [jax.ShapeDtypeStruct((S, D_MODEL), F32), jax.ShapeDtypeStruct((S, D_MODEL), BF16),
         jax.ShapeDtypeStruct((U_PAD, D_MODEL), F32), jax.ShapeDtypeStruct((1, D_MODEL), F32)],
        [], 56, (dh_out, h, gain, n, dq, dk, dv, dpc, rc, rs1, rs2, w_in), rider)


def _norm_bwd(dh_out, dns, h, gain, name):
    S = h.shape[0]
    tile = min(TOK_TILE, S)
    n = len(dns)

    def body(do_ref, *refs):
        h_ref, g_ref, dh_ref, dg_ref = refs[n:]

        @pl.when(pl.program_id(0) == 0)
        def _():
            dg_ref[...] = jnp.zeros_like(dg_ref)

        r, xh = _rms_stats(h_ref[...])
        dnorm, dg = _rms_bwd(_sum_chunks(refs[:n]), g_ref[...], r, xh)
        dg_ref[...] += dg
        dh_ref[...] = do_ref[...] + dnorm

    return _call(
        body, name, (S // tile,),
        [_rows(tile, D_MODEL)] + [_chunk_rows(tile, dn.shape[0]) for dn in dns] + [_rows(tile, D_MODEL), _full((1, D_MODEL))],
        [_rows(tile, D_MODEL), _full((1, D_MODEL))],
        [jax.ShapeDtypeStruct((S, D_MODEL), F32), jax.ShapeDtypeStruct((1, D_MODEL), F32)],
        [], 40, (dh_out, *dns, h, gain))


def _rope_tables(S):
    half = ROTARY_DIM // 2
    inv_freq = ROPE_THETA ** (-jnp.arange(0, ROTARY_DIM, 2, dtype=F32) / ROTARY_DIM)
    dim = jnp.arange(LANES) % HEAD_DIM
    ang = jnp.arange(S, dtype=F32)[:, None] * inv_freq[dim % half][None, :]
    lo, hi = (dim < half)[None, :], ((dim >= half) & (dim < ROTARY_DIM))[None, :]
    c = jnp.where(lo | hi, jnp.cos(ang), 1.0)
    s1 = jnp.where(lo, -jnp.sin(ang), 0.0)
    s2 = jnp.where(hi, jnp.sin(ang), 0.0)
    return c, s1, s2


def _pad_heads(w, axis):
    w = jnp.moveaxis(w, axis, 0)
    heads = w.reshape((N_HEADS, HEAD_DIM) + w.shape[1:])
    zero = jnp.zeros_like(heads)
    first = (jnp.arange(N_HEADS) < Q_PER_KV).reshape((N_HEADS, 1) + (1,) * (w.ndim - 1))
    lo = jnp.where(first, heads, zero)
    hi = jnp.where(first, zero, heads)
    padded = jnp.concatenate([lo, hi], axis=1).reshape((Q_PAD,) + w.shape[1:])
    return jnp.moveaxis(padded, 0, axis)


def _unpad_heads(w, axis):
    w = jnp.moveaxis(w, axis, 0)
    groups = w.reshape((N_HEADS, 2, HEAD_DIM) + w.shape[1:])
    first = (jnp.arange(N_HEADS) < Q_PER_KV).reshape((N_HEADS, 1) + (1,) * (w.ndim - 1))
    heads = jnp.where(first, groups[:, 0], groups[:, 1]).reshape((ATTN_WIDTH,) + w.shape[1:])
    return jnp.moveaxis(heads, 0, axis)


IN_ROWS = IN_WIDTH // N_CHIPS
OUT_ROWS = (ATTN_WIDTH + POOL_WIDTH) // N_CHIPS
MIX_ROWS = IN_ROWS + OUT_ROWS
FFN_ROWS = 3 * FF_CHUNK


def _step(x, target, bufs, small, place):
    S = x.shape[0]
    rc, rs1, rs2 = _rope_tables(S)
    mine = place[1]
    order = jnp.stack([mine, mine ^ 2, mine ^ 1, mine ^ 3])
    h1, n1, gate1, up1, ffn1, mix, ffn2 = _ffn_fwd(
        order, x, small["ffn1_norm"], "ffn1_fwd",
        _merge(_merge(_allgather(bufs[:1], in_passes=0), _allgather(bufs[1:2])), _allgather(bufs[2:], peers=X_Y_DIAGONAL[:2])), 0)
    w_in_t = mix[:, :IN_ROWS].reshape(IN_WIDTH, D_MODEL)
    w_in_pad = jnp.concatenate([_pad_heads(w_in_t[:ATTN_WIDTH], 0), w_in_t[ATTN_WIDTH:]], axis=0)
    w_out = mix[:, IN_ROWS:].reshape(ATTN_WIDTH + POOL_WIDTH, D_MODEL)
    wa = _pad_heads(w_out[:ATTN_WIDTH], 0)
    wp = w_out[ATTN_WIDTH:]
    pool_w = small["pool_w"].astype(BF16)

    n2, q, k, v, pc = _mix_in(h1, small["mix_norm"], w_in_pad, rc, rs1, rs2, "mix_in")
    pool_m = _pool_matrices(S)
    a, p = _mix_core_fwd(q, k, v, pc, small["sink_logits"], pool_m, pool_w, small["pool_scale"], "mix_core_fwd")
    h2 = _mix_out(h1, a, p, wa, wp, "mix_out")
    h3, n3, gate2, up2, ffn2 = _ffn_fwd(order, h2, small["ffn2_norm"], "ffn2_fwd",
                                        _allgather([ffn2], peers=X_Y_DIAGONAL[2:], in_passes=0), 0)
    dh3, dhalf3, loss_lanes, d_final = _loss_head(h3, target, small["final_norm"], "loss_head")

    dn3, d_ffn2 = _ffn_bwd(jnp.arange(N_CHIPS, dtype=jnp.int32), dhalf3, n3, gate2, up2, ffn2, "ffn2_bwd")
    dh2, da, dp, dwa, dwp, d_ffn2_norm, received = _mix_out_bwd(dh3, dn3, h2, small["ffn2_norm"], a, p, wa, wp, "mix_out_bwd",
                                                                _sibling_exchange([d_ffn2]))
    pair = _pair_sum(place, d_ffn2, received, "grad_pair_sum_ffn2")
    dq, dk, dv, dpc, dsink, dpool_w, dpool_scale, stack = _mix_core_bwd(
        q, k, v, pc, da, dp, small["sink_logits"], pool_m, pool_w, small["pool_scale"], rc, rs1, rs2, "mix_core_bwd",
        _scatter([pair], peers=X_Y_DIAGONAL[:2]))
    dh1, dhalf1, dw_in_pad, d_mix_norm, stack = _mix_in_bwd(dh2, h1, small["mix_norm"], n2, dq, dk, dv, dpc, rc, rs1, rs2,
                                                            w_in_pad, "mix_in_bwd",
                                                            _scatter([pair], peers=X_Y_DIAGONAL[2:], stacks=[stack]))
    reduced_ffn2 = _chip_sum(place, pair, stack, 2, "grad_chip_sum_ffn2")
    dw_in_t = jnp.concatenate([_unpad_heads(dw_in_pad[:Q_PAD], 0), dw_in_pad[Q_PAD:]], axis=0)
    dw_out = jnp.concatenate([_unpad_heads(dwa, 0), dwp], axis=0)
    d_mix = jnp.concatenate([dw_in_t.reshape(N_CHIPS, IN_ROWS, D_MODEL), dw_out.reshape(N_CHIPS, OUT_ROWS, D_MODEL)], axis=1)
    d_mix = jnp.transpose(d_mix.reshape(N_CHIPS, 2, MIX_ROWS // 2, D_MODEL), (1, 0, 2, 3)).astype(BF16)
    small_g = {"ffn1_norm": jnp.zeros_like(d_mix_norm), "mix_norm": d_mix_norm, "ffn2_norm": d_ffn2_norm,
               "final_norm": d_final, "pool_scale": dpool_scale, "sink_logits": dsink[:, :N_HEADS], "pool_w": dpool_w}
    loss_row = jnp.sum(loss_lanes.reshape(D_MODEL // LANES, LANES), axis=0, keepdims=True)
    small_early = _pack_small(small_g, loss_row)

    chunk = [(place[1:] + 1 + p) % N_CHIPS for p in range(N_CHIPS)]
    ffn1_bwd = functools.partial(_ffn_bwd, d_out=dhalf1, n=n1, gate=gate1, up=up1, group=ffn1)
    stack = jnp.zeros((N_CHIPS, FFN_ROWS // 2, D_MODEL), BF16)
    dn_a, dw_a, recv_mix, small_all, g_ffn2 = ffn1_bwd(
        chunk[0], name="ffn1_bwd_0",
        rider=_merge(_merge(_sibling_exchange([d_mix]), _small_allgather(small_early)), _sibling_share([reduced_ffn2])))
    pair_mix = _pair_sum(place, d_mix, recv_mix, "grad_pair_sum_mix")
    dn_b, dw_b, recv_a, stack_mix = ffn1_bwd(chunk[1], name="ffn1_bwd_1",
                                             rider=_merge(_sibling_exchange([dw_a]), _scatter([pair_mix])))
    pair_a = _pair_sum(place, dw_a, recv_a, "grad_pair_sum_ffn1_0")
    reduced_mix = _chip_sum(place, pair_mix, stack_mix, 2, "grad_chip_sum_mix")
    dn_c, dw_c, recv_b, stack, g_mix = ffn1_bwd(
        chunk[2], name="ffn1_bwd_2",
        rider=_merge(_merge(_sibling_exchange([dw_b]), _scatter_step(pair_a, stack, 0)), _sibling_share([reduced_mix])))
    pair_b = _pair_sum(place, dw_b, recv_b, "grad_pair_sum_ffn1_1")
    dn_d, dw_d, recv_c, stack = ffn1_bwd(chunk[3], name="ffn1_bwd_3",
                                         rider=_merge(_sibling_exchange([dw_c]), _scatter_step(pair_b, stack, 1)))
    pair_c = _pair_sum(place, dw_c, recv_c, "grad_pair_sum_ffn1_2")
    grad_x, d_ffn1_norm = _norm_bwd(dh1, [dn_a, dn_b, dn_c, dn_d], x, small["ffn1_norm"], "norm1_bwd")

    recv_d, stack, gains = _comm_call(
        _merge(_merge(_sibling_exchange([dw_d]), _scatter_step(pair_c, stack, 2)),
               _small_allgather(d_ffn1_norm.reshape(-1, LANES))), "grad_tail")
    pair_d = _pair_sum(place, dw_d, recv_d, "grad_pair_sum_ffn1_3")
    reduced_ffn1 = _chip_sum(place, pair_d, stack, 2, "grad_chip_sum_ffn1")
    (g_ffn1,) = _comm_call(_sibling_share([reduced_ffn1]), "grad_share_tail")
    gain_sum = _sum_leading(gains, 1, "gain_grad_sum")
    small_sum = jnp.concatenate([gain_sum, _sum_leading(small_all, 1, "small_grad_sum")[gain_sum.shape[0]:]], axis=0)
    return jnp.sum(small_sum[SMALL_ROWS - 1]), grad_x, [g.reshape(-1, D_MODEL) for g in (g_ffn1, g_mix, g_ffn2)], small_sum


GROUPS =(("ffn1_w_gate", "ffn1_w_up", "ffn1_w_down"), ("w_in", "w_out"), ("ffn2_w_gate", "ffn2_w_up", "ffn2_w_down"))
TRANSPOSED = ("ffn1_w_gate", "ffn1_w_up", "w_in", "ffn2_w_gate", "ffn2_w_up")


def _place():
    x, y, c = lax.axis_index("x"), lax.axis_index("y"), lax.axis_index("c")
    chips = [(1 - x, y), (x, 1 - y), (1 - x, 1 - y)]
    return x, y, c, chips


def _remote(src, dst, send_sem, recv_sem, to):
    return pltpu.make_async_remote_copy(src_ref=src, dst_ref=dst, send_sem=send_sem, recv_sem=recv_sem,
                                        device_id=to, device_id_type=MESH)


def _pack(chip, members, name):
    rows = members[0].shape[0]
    n = len(members)

    def body(chip_ref, *refs):
        ins, out_ref, buf, sems = refs[:n], refs[n], refs[n + 1], refs[n + 2]
        copies = [pltpu.make_async_copy(ins[k], buf.at[k], sems.at[k]) for k in range(n)]
        for cp in copies:
            cp.start()
        for k in range(n):
            copies[k].wait()
            out_ref[0, k * rows:(k + 1) * rows, :] = buf[k].astype(BF16)

    return pl.pallas_call(
        body, name=name,
        grid_spec=pltpu.PrefetchScalarGridSpec(
            num_scalar_prefetch=1, grid=(1,),
            in_specs=[HBM_SPEC] * n,
            out_specs=pl.BlockSpec((1, n * rows, D_MODEL), lambda k, chip_ref: (chip_ref[0], 0, 0)),
            scratch_shapes=[pltpu.VMEM((n, rows, D_MODEL), F32), pltpu.SemaphoreType.DMA((n,))]),
        out_shape=jax.ShapeDtypeStruct((N_CHIPS, n * rows, D_MODEL), BF16),
        compiler_params=_params(("arbitrary",), 40),
    )(chip, *members)


def _same(arrays):
    return [jax.ShapeDtypeStruct(a.shape, a.dtype) for a in arrays]


X_Y_DIAGONAL = (0, 1, 2)


def _allgather(bufs, peers=X_Y_DIAGONAL, in_passes=None):
    n = len(bufs)

    def copy(kind, outs, send_sems, recv_sems, a, k):
        x, y, c, chips = _place()
        half = bufs[a].shape[1] // 2

        def rows(slot, core):
            return outs[a].at[slot, pl.ds(pl.multiple_of(core * half, 16), half)]

        me, slot = 2 * x + y, 2 * chips[k][0] + chips[k][1]
        over_ici = (send_sems.at[6 * a + k], recv_sems.at[6 * a + k])
        over_d2d = (send_sems.at[6 * a + 3 + k], recv_sems.at[6 * a + 3 + k])
        if kind == "first":
            return _remote(rows(me, c), rows(me, c), *over_ici, (*chips[k], c))
        if kind == "landed":
            return _remote(rows(me, c), rows(slot, c), *over_ici, (*chips[k], c))
        if kind == "passed":
            return _remote(rows(slot, c), rows(slot, c), *over_d2d, (x, y, 1 - c))
        return _remote(rows(me, c), rows(slot, 1 - c), *over_d2d, (x, y, 1 - c))

    def arrive(pairs):
        def hook(ins, outs, sems):
            for a, k in pairs:
                copy("landed", outs, *sems, a, k).wait_recv()
                copy("passed", outs, *sems, a, k).start()
            for a, k in pairs:
                copy("handed", outs, *sems, a, k).wait_recv()
        return hook

    everything = [(a, k) for a in range(n) for k in peers]
    early = [(a, k) for a, k in everything if a == in_passes]

    def start(ins, outs, sems):
        for a, k in everything:
            copy("first", outs, *sems, a, k).start()

    def finish(ins, outs, sems):
        arrive([pair for pair in everything if pair not in early])(ins, outs, sems)
        for a, k in everything:
            copy("first", outs, *sems, a, k).wait_send()
            copy("passed", outs, *sems, a, k).wait_send()

    hooks = tuple(((k + 1, 0), arrive([(a, k)])) for a, k in early)
    return _Rider(list(bufs), _same(bufs), {a: a for a in range(n)},
                  [pltpu.SemaphoreType.DMA((6 * n,)), pltpu.SemaphoreType.DMA((6 * n,))], start, finish, hooks)


def _sibling_exchange(parts):
    n = len(parts)

    def copies(ins, outs, send_sems, recv_sems):
        x, y, c, _ = _place()
        return [_remote(ins[a].at[1 - c], outs[a], send_sems.at[a], recv_sems.at[a], (x, y, 1 - c)) for a in range(n)]

    def start(ins, outs, sems):
        for cp in copies(ins, outs, *sems):
            cp.start()

    def finish(ins, outs, sems):
        for cp in copies(ins, outs, *sems):
            cp.wait_recv()
            cp.wait_send()

    return _Rider(list(parts), [jax.ShapeDtypeStruct(p.shape[1:], p.dtype) for p in parts], {},
                  [pltpu.SemaphoreType.DMA((n,)), pltpu.SemaphoreType.DMA((n,))], start, finish)


def _small_allgather(small):
    flips = [(fx, fy, fc) for fx in range(2) for fy in range(2) for fc in range(2)][1:]

    def copies(small_ref, gather_ref, send_sems, recv_sems, local_sem, started_only=False):
        x, y, c, _ = _place()
        me = 4 * x + 2 * y + c
        peers = [((1 - x) if fx else x, (1 - y) if fy else y, (1 - c) if fc else c) for fx, fy, fc in flips]
        own = pltpu.make_async_copy(small_ref, gather_ref.at[me], local_sem)
        sent = [_remote(small_ref, gather_ref.at[me], send_sems.at[k], recv_sems.at[k], peer) for k, peer in enumerate(peers)]
        if started_only:
            return own, sent
        landed = [_remote(small_ref, gather_ref.at[4 * px + 2 * py + pc], send_sems.at[k], recv_sems.at[k], (px, py, pc))
                  for k, (px, py, pc) in enumerate(peers)]
        return own, sent, landed

    def start(ins, outs, sems):
        own, sent = copies(ins[0], outs[0], *sems, started_only=True)
        own.start()
        for cp in sent:
            cp.start()

    def finish(ins, outs, sems):
        own, sent, landed = copies(ins[0], outs[0], *sems)
        for cp in landed:
            cp.wait_recv()
        for cp in sent:
            cp.wait_send()
        own.wait()

    return _Rider([small], [jax.ShapeDtypeStruct((2 * N_CHIPS,) + small.shape, small.dtype)], {},
                  [pltpu.SemaphoreType.DMA((7,)), pltpu.SemaphoreType.DMA((7,)), pltpu.SemaphoreType.DMA], start, finish)


def _merge(a, b):
    na, nao, nas = len(a.operands), len(a.out_shapes), len(a.scratch)

    def start(ins, outs, sems):
        a.start(ins[:na], outs[:nao], sems[:nas])
        b.start(ins[na:], outs[nao:], sems[nas:])

    def finish(ins, outs, sems):
        a.finish(ins[:na], outs[:nao], sems[:nas])
        b.finish(ins[na:], outs[nao:], sems[nas:])

    def of_a(fn):
        return lambda ins, outs, sems: fn(ins[:na], outs[:nao], sems[:nas])

    def of_b(fn):
        return lambda ins, outs, sems: fn(ins[na:], outs[nao:], sems[nas:])

    aliases = {**a.aliases, **{na + k: nao + v for k, v in b.aliases.items()}}
    hooks = tuple((at, of_a(fn)) for at, fn in a.hooks) + tuple((at, of_b(fn)) for at, fn in b.hooks)
    return _Rider(a.operands + b.operands, a.out_shapes + b.out_shapes, aliases, a.scratch + b.scratch, start, finish, hooks)


def _scatter_step(pair, stack, step):
    def copies(pair_ref, stack_ref, send_sem, recv_sem, started_only=False):
        x, y, c, _ = _place()
        me = 2 * x + y
        to = (me + 1 + step) % N_CHIPS
        frm = (me + N_CHIPS - 1 - step) % N_CHIPS
        sent = _remote(pair_ref.at[0], stack_ref.at[me], send_sem, recv_sem, (to // 2, to % 2, c))
        if started_only:
            return sent
        landed = _remote(pair_ref.at[0], stack_ref.at[frm], send_sem, recv_sem, (frm // 2, frm % 2, c))
        return sent, landed

    def start(ins, outs, sems):
        copies(ins[0], outs[0], *sems, started_only=True).start()

    def finish(ins, outs, sems):
        sent, landed = copies(ins[0], outs[0], *sems)
        landed.wait_recv()
        sent.wait_send()

    return _Rider([pair, stack], _same([stack]), {1: 0}, [pltpu.SemaphoreType.DMA, pltpu.SemaphoreType.DMA], start, finish)


def _scatter(sums, peers=X_Y_DIAGONAL, stacks=None):
    n = len(sums)

    def copies(ins, outs, send_sems, recv_sems, started_only=False):
        x, y, c, chips = _place()
        me = 2 * x + y
        slots = [2 * cx + cy for cx, cy in chips]
        sent = [_remote(ins[a].at[slots[k]], outs[a].at[me], send_sems.at[3 * a + k], recv_sems.at[3 * a + k], (*chips[k], c))
                for a in range(n) for k in peers]
        if started_only:
            return sent
        landed = [_remote(ins[a].at[slots[k]], outs[a].at[slots[k]], send_sems.at[3 * a + k], recv_sems.at[3 * a + k],
                          (*chips[k], c)) for a in range(n) for k in peers]
        return sent, landed

    def start(ins, outs, sems):
        for cp in copies(ins, outs, *sems, started_only=True):
            cp.start()

    def finish(ins, outs, sems):
        sent, landed = copies(ins, outs, *sems)
        for cp in landed:
            cp.wait_recv()
        for cp in sent:
            cp.wait_send()

    sems = [pltpu.SemaphoreType.DMA((3 * n,)), pltpu.SemaphoreType.DMA((3 * n,))]
    if stacks is None:
        return _Rider(list(sums), _same(sums), {}, sems, start, finish)
    return _Rider(list(sums) + list(stacks), _same(sums), {n + a: a for a in range(n)}, sems, start, finish)


def _sibling_share(bufs):
    n = len(bufs)

    def copies(outs, send_sems, recv_sems, started_only=False):
        x, y, c, _ = _place()
        sent = [_remote(outs[a].at[c], outs[a].at[c], send_sems.at[a], recv_sems.at[a], (x, y, 1 - c)) for a in range(n)]
        if started_only:
            return sent
        landed = [_remote(outs[a].at[c], outs[a].at[1 - c], send_sems.at[a], recv_sems.at[a], (x, y, 1 - c)) for a in range(n)]
        return sent, landed

    def start(ins, outs, sems):
        for cp in copies(outs, *sems, started_only=True):
            cp.start()

    def finish(ins, outs, sems):
        sent, landed = copies(outs, *sems)
        for cp in landed:
            cp.wait_recv()
        for cp in sent:
            cp.wait_send()

    return _Rider(list(bufs), _same(bufs), {a: a for a in range(n)},
                  [pltpu.SemaphoreType.DMA((n,)), pltpu.SemaphoreType.DMA((n,))], start, finish)


def _pair_sum(core, part, received, name):
    _, k, rh, cols = part.shape

    def body(core_ref, p_ref, r_ref, o_ref):
        o_ref[...] = (p_ref[0].astype(F32) + r_ref[...].astype(F32)).astype(BF16)

    return pl.pallas_call(
        body, name=name,
        grid_spec=pltpu.PrefetchScalarGridSpec(
            num_scalar_prefetch=1, grid=(k,),
            in_specs=[pl.BlockSpec((1, 1, rh, cols), lambda j, core_ref: (core_ref[0], j, 0, 0)),
                      pl.BlockSpec((1, rh, cols), lambda j, core_ref: (j, 0, 0))],
            out_specs=pl.BlockSpec((1, rh, cols), lambda j, core_ref: (j, 0, 0))),
        out_shape=jax.ShapeDtypeStruct((k, rh, cols), BF16),
        compiler_params=_params(("parallel",), 32),
    )(core, part, received)


def _sum_leading(stack, steps, name):
    k, rows, cols = stack.shape
    tile = rows // steps

    def body(s_ref, o_ref):
        total = s_ref[0].astype(F32)
        for d in range(1, k):
            total = total + s_ref[d].astype(F32)
        o_ref[...] = total

    return pl.pallas_call(
        body, name=name, grid=(steps,),
        in_specs=[pl.BlockSpec((k, tile, cols), lambda i: (0, i, 0))],
        out_specs=pl.BlockSpec((tile, cols), lambda i: (i, 0)),
        out_shape=jax.ShapeDtypeStruct((rows, cols), F32),
        compiler_params=_params(("parallel",), 32),
    )(stack)


def _chip_sum(place, own, stack, steps, name):
    k, rows, cols = stack.shape
    tile = rows // steps

    def body(place_ref, own_ref, *refs):
        chip = place_ref[1]
        total = None
        for d in range(k):
            term = jnp.where(chip == d, own_ref[0], refs[d][0]).astype(F32)
            total = term if total is None else total + term
        refs[k][0] = total

    def other(d):
        return lambda i, place_ref: (jnp.where(place_ref[1] == d, (d + 1) % k, d), i, 0)

    return pl.pallas_call(
        body, name=name,
        grid_spec=pltpu.PrefetchScalarGridSpec(
            num_scalar_prefetch=1, grid=(steps,),
            in_specs=[pl.BlockSpec((1, tile, cols), lambda i, place_ref: (place_ref[1] % own.shape[0], i, 0))]
            + [pl.BlockSpec((1, tile, cols), other(d)) for d in range(k)],
            out_specs=pl.BlockSpec((1, tile, cols), lambda i, place_ref: (place_ref[0], i, 0))),
        out_shape=jax.ShapeDtypeStruct((2, rows, cols), F32),
        compiler_params=_params(("arbitrary",), 32),
    )(place, own, *([stack] * k))


def _adamw(w, g, row0, m, v, tile, name):
    rows, cols = w.shape
    first = row0 // tile
    assert rows % tile == 0 and row0 % tile == 0
    bc1 = 1.0 - ADAM_B1 ** ADAM_STEP
    bc2 = 1.0 - ADAM_B2 ** ADAM_STEP

    def body(w_ref, g_ref, m_ref, v_ref, go_ref, d_ref, mo_ref, vo_ref):
        g = g_ref[...]
        m_new = ADAM_B1 * m_ref[...] + (1.0 - ADAM_B1) * g
        v_new = ADAM_B2 * v_ref[...] + (1.0 - ADAM_B2) * (g * g)
        go_ref[...] = g
        d_ref[...] = -ADAM_LR * ((m_new / bc1) / (jnp.sqrt(v_new / bc2) + ADAM_EPS) + ADAM_WD * w_ref[...])
        mo_ref[...] = m_new
        vo_ref[...] = v_new

    spec = pl.BlockSpec((tile, cols), lambda i: (i, 0))
    g_spec = pl.BlockSpec((tile, cols), lambda i: (first + i, 0))
    return pl.pallas_call(
        body, name=name, grid=(rows // tile,),
        in_specs=[spec, g_spec, spec, spec], out_specs=[spec] * 4,
        out_shape=[jax.ShapeDtypeStruct((rows, cols), F32)] * 4,
        compiler_params=_params(("parallel",), 32),
    )(w, g, m, v)


SMALL = ("ffn1_norm", "mix_norm", "ffn2_norm", "final_norm", "pool_scale", "sink_logits", "pool_w")


def _pack_small(d, last_row=None):
    sink = jnp.pad(d["sink_logits"].reshape(1, N_HEADS), ((0, 0), (0, LANES - N_HEADS)))
    rows = [d[n].reshape(-1, LANES) for n in SMALL[:5]] + [sink, d["pool_w"].reshape(-1, LANES)]
    used = sum(r.shape[0] for r in rows)
    last = jnp.zeros((1, LANES), F32) if last_row is None else last_row
    return jnp.concatenate(rows + [jnp.zeros((SMALL_ROWS - used - 1, LANES), F32), last], axis=0)


def _unpack_small(packed, like):
    out, row = {}, 0
    for n in SMALL:
        size = LANES if n == "sink_logits" else math.prod(like[n].shape)
        chunk = packed[row:row + size // LANES].reshape(-1)
        out[n] = (chunk[:N_HEADS] if n == "sink_logits" else chunk).reshape(like[n].shape)
        row += size // LANES
    return out


def kernel(x, ffn1_norm, ffn1_w_gate, ffn1_w_up, ffn1_w_down, mix_norm, w_in, sink_logits, pool_w, pool_scale, w_out, ffn2_norm, ffn2_w_gate, ffn2_w_up, ffn2_w_down, final_norm, loss_target, m_ffn1_norm, m_ffn1_w_gate, m_ffn1_w_up, m_ffn1_w_down, m_mix_norm, m_w_in, m_sink_logits, m_pool_w, m_pool_scale, m_w_out, m_ffn2_norm, m_ffn2_w_gate, m_ffn2_w_up, m_ffn2_w_down, m_final_norm, v_ffn1_norm, v_ffn1_w_gate, v_ffn1_w_up, v_ffn1_w_down, v_mix_norm, v_w_in, v_sink_logits, v_pool_w, v_pool_scale, v_w_out, v_ffn2_norm, v_ffn2_w_gate, v_ffn2_w_up, v_ffn2_w_down, v_final_norm):
    names = ("ffn1_norm", "ffn1_w_gate", "ffn1_w_up", "ffn1_w_down", "mix_norm", "w_in", "sink_logits", "pool_w",
             "pool_scale", "w_out", "ffn2_norm", "ffn2_w_gate", "ffn2_w_up", "ffn2_w_down", "final_norm")
    weights = dict(zip(names, (ffn1_norm, ffn1_w_gate, ffn1_w_up, ffn1_w_down, mix_norm, w_in, sink_logits, pool_w,
                               pool_scale, w_out, ffn2_norm, ffn2_w_gate, ffn2_w_up, ffn2_w_down, final_norm)))
    mom1 = dict(zip(names, (m_ffn1_norm, m_ffn1_w_gate, m_ffn1_w_up, m_ffn1_w_down, m_mix_norm, m_w_in, m_sink_logits,
                            m_pool_w, m_pool_scale, m_w_out, m_ffn2_norm, m_ffn2_w_gate, m_ffn2_w_up, m_ffn2_w_down,
                            m_final_norm)))
    mom2 = dict(zip(names, (v_ffn1_norm, v_ffn1_w_gate, v_ffn1_w_up, v_ffn1_w_down, v_mix_norm, v_w_in, v_sink_logits,
                            v_pool_w, v_pool_scale, v_w_out, v_ffn2_norm, v_ffn2_w_gate, v_ffn2_w_up, v_ffn2_w_down,
                            v_final_norm)))
    chip = (2 * lax.axis_index("x") + lax.axis_index("y")).astype(jnp.int32).reshape(1)
    place = jnp.concatenate([lax.axis_index("c").astype(jnp.int32).reshape(1), chip])

    def rows_of(t, n):
        return jnp.swapaxes(t[n][0], 0, 1) if n in TRANSPOSED else t[n][0]

    bufs = [_pack(chip, [rows_of(weights, n) for n in GROUPS[0]], "pack_ffn1"),
            _pack(chip, [jnp.concatenate([rows_of(weights, n) for n in GROUPS[1]], axis=0)], "pack_mix"),
            _pack(chip, [rows_of(weights, n) for n in GROUPS[2]], "pack_ffn2")]

    small_w = {"ffn1_norm": ffn1_norm, "mix_norm": mix_norm, "ffn2_norm": ffn2_norm,
               "final_norm": final_norm.reshape(1, D_MODEL), "pool_scale": pool_scale, "sink_logits": sink_logits,
               "pool_w": pool_w[0]}
    loss, grad_x, group_grads, small_sum = _step(x[0], loss_target[0], bufs, small_w, place)

    out_g, out_d, out_m, out_v = {}, {}, {}, {}
    for members, g in zip(GROUPS, group_grads):
        row0 = 0
        for n in members:
            w = rows_of(weights, n)
            tile = FF_CHUNK // 4 if w.shape[0] == FF_CHUNK else math.gcd(IN_ROWS, OUT_ROWS)
            outs = _adamw(w, g, row0, rows_of(mom1, n), rows_of(mom2, n), tile, "adamw_" + n)
            row0 += w.shape[0]
            for dst, t in zip((out_g, out_d, out_m, out_v), outs):
                dst[n] = (jnp.swapaxes(t, 0, 1) if n in TRANSPOSED else t).reshape(weights[n].shape)
    small_outs = _adamw(_pack_small(weights), small_sum, 0, _pack_small(mom1), _pack_small(mom2), SMALL_ROWS, "adamw_small")
    for dst, packed in zip((out_g, out_d, out_m, out_v), small_outs):
        dst.update(_unpack_small(packed, weights))

    return (loss,grad_x.reshape(x.shape), *[out_g[n] for n in names], *[out_d[n] for n in names],
            *[out_m[n] for n in names], *[out_v[n] for n in names])
```

```python
import collections
import functools
import math

import jax
import jax.numpy as jnp
from jax import lax
from jax.experimental import pallas as pl
from jax.experimental.pallas import tpu as pltpu

F32, BF16 = jnp.float32, jnp.bfloat16
MESH = pl.DeviceIdType.MESH

D_MODEL = 1024
D_FF = 2816
N_CHIPS = 4
FF_CHUNK = D_FF // N_CHIPS
HEAD_DIM = 64
N_HEADS = 8
N_KV = 2
Q_PER_KV = N_HEADS // N_KV
KV_WIDTH = N_KV * HEAD_DIM
ATTN_WIDTH = N_HEADS * HEAD_DIM
POOL_WINDOWS = (2, 4, 8, 16)
N_POOL = len(POOL_WINDOWS)
POOL_GROUP = 128
POOL_WIDTH = N_POOL * POOL_GROUP
IN_WIDTH = ATTN_WIDTH + 2 * KV_WIDTH + POOL_WIDTH
WINDOW = 128
BLOCK = 128
BAND = 3 * BLOCK
ROPE_THETA = 500000.0
ROTARY_DIM = HEAD_DIM // 4
EPS = 1e-6
LANES = 128
Q_PAD = N_HEADS * LANES
U_PAD = Q_PAD + 2 * KV_WIDTH + POOL_WIDTH
SCALE = HEAD_DIM ** -0.5
NEG = -1e30

ADAM_LR, ADAM_B1, ADAM_B2, ADAM_EPS, ADAM_WD, ADAM_STEP = 0.001, 0.9, 0.999, 1e-08, 0.01, 10

V7X_VMEM_BYTES = 64 * 1024 * 1024
TOK_TILE = 512
SMALL_ROWS = 552


def _params(sem, vmem_mb):
    assert vmem_mb * 1024 * 1024 <= V7X_VMEM_BYTES
    return pltpu.CompilerParams(dimension_semantics=sem, vmem_limit_bytes=vmem_mb * 1024 * 1024)


def _dot(a, b):
    return lax.dot_general(a, b, (((1,), (0,)), ((), ())), preferred_element_type=F32)


def _dot_nt(a, b):
    return lax.dot_general(a, b, (((1,), (1,)), ((), ())), preferred_element_type=F32)


def _dot_tn(a, b):
    return lax.dot_general(a, b, (((0,), (0,)), ((), ())), preferred_element_type=F32)


def _rms_stats(h):
    r = lax.rsqrt(jnp.mean(h * h, axis=-1, keepdims=True) + EPS)
    return r, h * r


def _rms_bwd(dn, g, r, xh):
    gd = dn * g
    dh = r * (gd - xh * jnp.mean(gd * xh, axis=-1, keepdims=True))
    return dh, jnp.sum(dn * xh, axis=0, keepdims=True)


def _rope(x, c, s1, s2):
    return x * c + pltpu.roll(x, LANES - ROTARY_DIM // 2, 1) * s1 + pltpu.roll(x, ROTARY_DIM // 2, 1) * s2


def _rope_bwd(d, c, s1, s2):
    return d * c + pltpu.roll(d * s1, ROTARY_DIM // 2, 1) + pltpu.roll(d * s2, LANES - ROTARY_DIM // 2, 1)


def _sum_chunks(refs):
    terms = [ref[j].astype(F32) for ref in refs for j in range(ref.shape[0])]
    return functools.reduce(lambda a, b: a + b, terms)


def _chunk_rows(tile, k):
    return pl.BlockSpec((k, tile, D_MODEL), lambda i, *_: (0, i, 0))


def _full(shape):
    nd = len(shape)
    return pl.BlockSpec(shape, lambda *_: (0,) * nd)


def _rows(tile, cols):
    return pl.BlockSpec((tile, cols), lambda i, *_: (i, 0))


HBM_SPEC = pl.BlockSpec(memory_space=pltpu.HBM)

_Rider = collections.namedtuple("_Rider", "operands out_shapes aliases scratch start finish hooks", defaults=[()])


_NO_RIDER = _Rider([], [], {}, [], None, None)


def _call(body, name, grid, in_specs, out_specs, out_shape, scratch, vmem_mb, args, rider=None, prefetch=(),
          shares_rider_refs=False):
    rider = rider or _NO_RIDER
    n_pre, n_in, n_out, n_scr = len(prefetch), len(in_specs), len(out_specs), len(scratch)
    r_in, r_out = len(rider.operands), len(rider.out_shapes)

    def fused(*refs):
        pre, refs = refs[:n_pre], refs[n_pre:]
        ins, refs = refs[:n_in], refs[n_in:]
        r_ins, refs = refs[:r_in], refs[r_in:]
        outs, refs = refs[:n_out], refs[n_out:]
        r_outs, refs = refs[:r_out], refs[r_out:]
        scr, r_scr = refs[:n_scr], refs[n_scr:]
        ids = [pl.program_id(d) for d in range(len(grid))]
        if rider.start is not None:
            @pl.when(functools.reduce(jnp.logical_and, [i == 0 for i in ids]))
            def _():
                rider.start(r_ins, r_outs, r_scr)

        for at, hook in rider.hooks:
            @pl.when(functools.reduce(jnp.logical_and, [i == a for i, a in zip(ids, at)]))
            def _(hook=hook):
                hook(r_ins, r_outs, r_scr)

        if shares_rider_refs:
            body(*pre, *ins, *outs, *scr, rider_refs=r_outs)
        else:
            body(*pre, *ins, *outs, *scr)

        if rider.finish is not None:
            @pl.when(functools.reduce(jnp.logical_and, [i == g - 1 for i, g in zip(ids, grid)]))
            def _():
                rider.finish(r_ins, r_outs, r_scr)

    return pl.pallas_call(
        fused, name=name,
        grid_spec=pltpu.PrefetchScalarGridSpec(
            num_scalar_prefetch=n_pre, grid=grid,
            in_specs=list(in_specs) + [HBM_SPEC] * r_in, out_specs=list(out_specs) + [HBM_SPEC] * r_out,
            scratch_shapes=list(scratch) + list(rider.scratch)),
        out_shape=list(out_shape) + list(rider.out_shapes),
        input_output_aliases={n_pre + n_in + k: n_out + v for k, v in rider.aliases.items()},
        compiler_params=_params(("arbitrary",) * len(grid), vmem_mb),
    )(*prefetch, *args, *rider.operands)


def _comm_call(rider, name):
    r_in, r_out = len(rider.operands), len(rider.out_shapes)

    def body(*refs):
        r_ins, r_outs, r_scr = refs[:r_in], refs[r_in:r_in + r_out], refs[r_in + r_out:]
        rider.start(r_ins, r_outs, r_scr)
        rider.finish(r_ins, r_outs, r_scr)

    return pl.pallas_call(
        body, name=name, in_specs=[HBM_SPEC] * r_in, out_specs=[HBM_SPEC] * r_out, out_shape=list(rider.out_shapes),
        input_output_aliases=dict(rider.aliases), scratch_shapes=list(rider.scratch),
    )(*rider.operands)


def _ffn_fwd(order, h, gain, name, rider, group_at):
    S = h.shape[0]
    tile = min(TOK_TILE, S)
    nt = S // tile
    last = N_CHIPS - 1

    def body(order_ref, h_ref, g_ref, ho_ref, n_ref, gate_ref, up_ref, w_scr, w_sem, acc, n_scr, rider_refs):
        j, i = pl.program_id(0), pl.program_id(1)

        @pl.when(i == 0)
        def _():
            fetch = pltpu.make_async_copy(rider_refs[group_at].at[order_ref[j]], w_scr, w_sem)
            fetch.start()
            fetch.wait()

        at = pl.multiple_of(i * tile, tile)

        @pl.when(j == 0)
        def _():
            _, xh = _rms_stats(h_ref[...])
            n = (xh * g_ref[...]).astype(BF16)
            n_scr[pl.ds(at, tile), :] = n
            n_ref[...] = n
            acc[pl.ds(at, tile), :] = jnp.zeros((tile, D_MODEL), F32)

        half = tile // 2
        wg, wu, wd = (w_scr[part * FF_CHUNK:(part + 1) * FF_CHUNK, :] for part in range(3))
        ns = [n_scr[pl.ds(at + s * half, half), :] for s in range(2)]
        gates = [_dot_nt(n, wg) for n in ns]
        ups = [_dot_nt(n, wu) for n in ns]
        acts = [(g * jax.nn.sigmoid(g) * u).astype(BF16) for g, u in zip(gates, ups)]
        for s in range(2):
            gate_ref[0, s * half:(s + 1) * half, :] = gates[s].astype(BF16)
            up_ref[0, s * half:(s + 1) * half, :] = ups[s].astype(BF16)
            acc[pl.ds(at + s * half, half), :] += _dot(acts[s], wd)

        @pl.when(j == last)
        def _():
            ho_ref[...] = h_ref[...] + 0.5 * acc[pl.ds(at, tile), :]

    tok = pl.BlockSpec((tile, D_MODEL), lambda j, i, order_ref: (i, 0))
    hid = pl.BlockSpec((1, tile, FF_CHUNK), lambda j, i, order_ref: (order_ref[j], i, 0))
    return _call(
        body, name, (N_CHIPS, nt),
        [tok, pl.BlockSpec((1, D_MODEL), lambda j, i, order_ref: (0, 0))],
        [pl.BlockSpec((tile, D_MODEL), lambda j, i, order_ref: (jnp.where(j == last, i, 0), 0)),
         pl.BlockSpec((tile, D_MODEL), lambda j, i, order_ref: (jnp.where(j == 0, i, nt - 1), 0)),
         hid, hid],
        [jax.ShapeDtypeStruct((S, D_MODEL), F32), jax.ShapeDtypeStruct((S, D_MODEL), BF16),
         jax.ShapeDtypeStruct((N_CHIPS, S, FF_CHUNK), BF16), jax.ShapeDtypeStruct((N_CHIPS, S, FF_CHUNK), BF16)],
        [pltpu.VMEM((3 * FF_CHUNK, D_MODEL), BF16), pltpu.SemaphoreType.DMA, pltpu.VMEM((S, D_MODEL), F32),
         pltpu.VMEM((S, D_MODEL), BF16)], 58, (h, gain), rider, (order,), shares_rider_refs=True)


def _ffn_bwd(chunks, d_out, n, gate, up, group, name, rider=None):
    S = n.shape[0]
    n_chunks = chunks.shape[0]
    tile = min(TOK_TILE, S)
    nt = S // tile
    half_rows = 3 * FF_CHUNK // 2
    cut = FF_CHUNK // 2

    def body(chunks_ref, do_ref, n_ref, gate_ref, up_ref, wg_ref, wu_ref, wd_ref, dn_ref, dw_ref, acc_g, acc_u, acc_d):
        j, i = pl.program_id(0), pl.program_id(1)

        @pl.when(i == 0)
        def _():
            acc_g[...] = jnp.zeros_like(acc_g)
            acc_u[...] = jnp.zeros_like(acc_u)
            acc_d[...] = jnp.zeros_like(acc_d)

        halves = [pl.ds(s * (tile // 2), tile // 2) for s in range(2)]
        dos = [do_ref[rows, :] for rows in halves]
        d_acts = [_dot_nt(do, wd_ref[0]) for do in dos]
        gs = [gate_ref[0, rows, :].astype(F32) for rows in halves]
        us = [up_ref[0, rows, :].astype(F32) for rows in halves]
        sigs = [jax.nn.sigmoid(g) for g in gs]
        silus = [g * sig for g, sig in zip(gs, sigs)]
        d_ups = [(d_act * silu).astype(BF16) for d_act, silu in zip(d_acts, silus)]
        d_gates = [(d_act * u * (sig * (1.0 + g * (1.0 - sig)))).astype(BF16) for d_act, u, sig, g in zip(d_acts, us, sigs, gs)]
        for rows, d_gate, d_up in zip(halves, d_gates, d_ups):
            dn_ref[0, rows, :] = (_dot(d_gate, wg_ref[0]) + _dot(d_up, wu_ref[0])).astype(BF16)
        d_gate, d_up = jnp.concatenate(d_gates, axis=0), jnp.concatenate(d_ups, axis=0)
        act = jnp.concatenate([(silu * u).astype(BF16) for silu, u in zip(silus, us)], axis=0)
        nn = n_ref[...]
        acc_g[...] += _dot_tn(d_gate, nn)
        acc_u[...] += _dot_tn(d_up, nn)
        acc_d[...] += _dot_tn(act, do_ref[...])

        @pl.when(i == nt - 1)
        def _():
            dw_ref[0, 0, :FF_CHUNK, :] = acc_g[...].astype(BF16)
            dw_ref[0, 0, FF_CHUNK:, :] = acc_u[:cut, :].astype(BF16)
            dw_ref[1, 0, :cut, :] = acc_u[cut:, :].astype(BF16)
            dw_ref[1, 0, cut:, :] = acc_d[...].astype(BF16)

    tok = pl.BlockSpec((tile, D_MODEL), lambda j, i, chunks_ref: (i, 0))
    hid = pl.BlockSpec((1, tile, FF_CHUNK), lambda j, i, chunks_ref: (chunks_ref[j], i, 0))
    return _call(
        body, name, (n_chunks, nt),
        [tok, tok, hid, hid]
        + [pl.BlockSpec((1, FF_CHUNK, D_MODEL), functools.partial(lambda j, i, chunks_ref, part: (chunks_ref[j], part, 0), part=part))
           for part in range(3)],
        [pl.BlockSpec((1, tile, D_MODEL), lambda j, i, chunks_ref: (j, i, 0)),
         pl.BlockSpec((2, 1, half_rows, D_MODEL), lambda j, i, chunks_ref: (0, j, 0, 0))],
        [jax.ShapeDtypeStruct((n_chunks, S, D_MODEL), BF16), jax.ShapeDtypeStruct((2, n_chunks, half_rows, D_MODEL), BF16)],
        [pltpu.VMEM((FF_CHUNK, D_MODEL), F32)] * 3, 56, (d_out, n, gate, up, group, group, group), rider, (chunks,))


def _mix_in(h, gain, w_in, rc, rs1, rs2, name):
    S = h.shape[0]
    tile = min(TOK_TILE, S)

    def body(h_ref, g_ref, w_ref, c_ref, s1_ref, s2_ref, n_ref, q_ref, k_ref, v_ref, pc_ref):
        _, xh = _rms_stats(h_ref[...])
        n = (xh * g_ref[...]).astype(BF16)
        n_ref[...] = n
        u = _dot_nt(n, w_ref[...])
        c, s1, s2 = c_ref[...], s1_ref[...], s2_ref[...]
        q_ref[...] = jnp.concatenate([(_rope(u[:, hd * LANES:(hd + 1) * LANES], c, s1, s2) * SCALE).astype(BF16)
                                      for hd in range(N_HEADS)], axis=1)
        k_ref[...] = _rope(u[:, Q_PAD:Q_PAD + KV_WIDTH], c, s1, s2).astype(BF16)
        v_ref[...] = u[:, Q_PAD + KV_WIDTH:Q_PAD + 2 * KV_WIDTH].astype(BF16)
        pc_ref[...] = u[:, Q_PAD + 2 * KV_WIDTH:]

    return pl.pallas_call(
        body, name=name, grid=(S // tile,),
        in_specs=[_rows(tile, D_MODEL), _full((1, D_MODEL)), _full((U_PAD, D_MODEL)),
                  _rows(tile, LANES), _rows(tile, LANES), _rows(tile, LANES)],
        out_specs=[_rows(tile, D_MODEL), _rows(tile, Q_PAD), _rows(tile, KV_WIDTH), _rows(tile, KV_WIDTH),
                   _rows(tile, POOL_WIDTH)],
        out_shape=[jax.ShapeDtypeStruct((S, D_MODEL), BF16), jax.ShapeDtypeStruct((S, Q_PAD), BF16),
                   jax.ShapeDtypeStruct((S, KV_WIDTH), BF16), jax.ShapeDtypeStruct((S, KV_WIDTH), BF16),
                   jax.ShapeDtypeStruct((S, POOL_WIDTH), F32)],
        compiler_params=_params(("parallel",), 40),
    )(h, gain, w_in, rc, rs1, rs2)


def _band_start(i, S):
    return pl.multiple_of(jnp.clip((i - 1) * BLOCK, 0, S - BAND), BLOCK)


def _window_bias(off):
    r = lax.broadcasted_iota(jnp.int32, (BLOCK, 1), 0)
    c = lax.broadcasted_iota(jnp.int32, (1, BAND), 1)
    return jnp.where(jnp.abs(off + r - c) <= WINDOW, 0.0, NEG).astype(F32)


def _softmax_parts(qh, kb, bias, sink_h):
    s = _dot_nt(qh, kb) + bias
    m = jnp.maximum(jnp.max(s, axis=-1, keepdims=True), sink_h)
    p = jnp.exp(s - m)
    es = jnp.exp(sink_h - m)
    return p, es, 1.0 / (jnp.sum(p, axis=-1, keepdims=True) + es)


def _pool_matrix(t0, start, S, w):
    r = lax.broadcasted_iota(jnp.int32, (BLOCK, 1), 0) + t0
    c = lax.broadcasted_iota(jnp.int32, (1, BAND), 1) + start
    half = w // 2

    def window(lo, hi):
        a = jnp.maximum(lo, 0)
        b = jnp.minimum(hi + 1, S)
        return jnp.where((c >= a) & (c < b), 1.0 / (b - a).astype(F32), 0.0)

    return (0.5 * (window(r - half, r + half - 1) + window(r - half + 1, r + half))).astype(BF16)


def _pool_matrices(S):
    blocks = ((0, 0), (BLOCK, 0), (S - BLOCK, S - BAND))
    return jnp.stack([jnp.stack([_pool_matrix(t0, start, S, w) for w in POOL_WINDOWS]) for t0, start in blocks])


def _pool_spec(nb):
    return pl.BlockSpec((1, N_POOL, BLOCK, BAND), lambda i, *_: (jnp.where(i == 0, 0, jnp.where(i == nb - 1, 2, 1)), 0, 0, 0))


def _mix_core_fwd(q, k, v, pc, sink, pool_m, pool_w, pool_scale, name, rider=None):
    S = q.shape[0]
    nb = S // BLOCK

    def body(sink_ref, q_ref, k_ref, v_ref, pc_ref, pm_ref, pw_ref, ps_ref, a_ref, p_ref):
        i = pl.program_id(0)
        start = _band_start(i, S)
        band = pl.ds(start, BAND)
        bias = _window_bias(i * BLOCK - start)
        kb, vb = k_ref[band, :], v_ref[band, :]
        hs = range(N_HEADS)
        ss = [_dot_nt(q_ref[:, hd * LANES:(hd + 1) * LANES], kb) + bias for hd in hs]
        ms = [jnp.maximum(jnp.max(ss[hd], axis=-1, keepdims=True), sink_ref[0, hd]) for hd in hs]
        ps = [jnp.exp(ss[hd] - ms[hd]) for hd in hs]
        invs = [1.0 / (jnp.sum(ps[hd], axis=-1, keepdims=True) + jnp.exp(sink_ref[0, hd] - ms[hd])) for hd in hs]
        outs = [_dot(ps[hd].astype(BF16), vb) for hd in hs]
        a_ref[...] = jnp.concatenate([(outs[hd] * invs[hd]).astype(BF16) for hd in hs], axis=1)
        centre = pl.ds(pl.multiple_of(i * BLOCK, BLOCK), BLOCK)
        gs = range(N_POOL)
        sl = [slice(g * POOL_GROUP, (g + 1) * POOL_GROUP) for g in gs]
        means = [_dot(pm_ref[0, g], pc_ref[band, sl[g]].astype(BF16)) for g in gs]
        devs = [(means[g] - pc_ref[centre, sl[g]]).astype(BF16) for g in gs]
        p_ref[...] = (jnp.concatenate([_dot(devs[g], pw_ref[g]) for g in gs], axis=1) * ps_ref[...]).astype(BF16)

    return _call(
        body, name, (nb,),
        [pl.BlockSpec(memory_space=pltpu.SMEM), _rows(BLOCK, Q_PAD), _full((S, KV_WIDTH)), _full((S, KV_WIDTH)),
         _full((S, POOL_WIDTH)), _pool_spec(nb), _full((N_POOL, POOL_GROUP, POOL_GROUP)), _full((1, POOL_WIDTH))],
        [_rows(BLOCK, Q_PAD), _rows(BLOCK, POOL_WIDTH)],
        [jax.ShapeDtypeStruct((S, Q_PAD), BF16), jax.ShapeDtypeStruct((S, POOL_WIDTH), BF16)],
        [], 40, (sink, q, k, v, pc, pool_m, pool_w, pool_scale), rider)


def _mix_core_bwd(q, k, v, pc, da, dp, sink, pool_m, pool_w, pool_scale, rc, rs1, rs2, name, rider=None):
    S = q.shape[0]
    nb = S // BLOCK

    def body(sink_ref, q_ref, k_ref, v_ref, pc_ref, da_ref, dp_ref, pm_ref, pw_ref, ps_ref, c_ref, s1_ref, s2_ref,
             dq_ref, dk_ref, dv_ref, dpc_ref, dsink_ref, dpw_ref, dps_ref):
        i = pl.program_id(0)

        @pl.when(i == 0)
        def _():
            dk_ref[...] = jnp.zeros_like(dk_ref)
            dv_ref[...] = jnp.zeros_like(dv_ref)
            dpc_ref[...] = jnp.zeros_like(dpc_ref)
            dsink_ref[...] = jnp.zeros_like(dsink_ref)
            dpw_ref[...] = jnp.zeros_like(dpw_ref)
            dps_ref[...] = jnp.zeros_like(dps_ref)

        start = _band_start(i, S)
        band = pl.ds(start, BAND)
        bias = _window_bias(i * BLOCK - start)
        kb, vb = k_ref[band, :], v_ref[band, :]
        c, s1, s2 = c_ref[...], s1_ref[...], s2_ref[...]
        lane = lax.broadcasted_iota(jnp.int32, (1, LANES), 1)
        hs = range(N_HEADS)
        qs = [q_ref[:, hd * LANES:(hd + 1) * LANES] for hd in hs]
        das = [da_ref[:, hd * LANES:(hd + 1) * LANES] for hd in hs]
        ss = [_dot_nt(qs[hd], kb) + bias for hd in hs]
        d_probs = [_dot_nt(das[hd], vb) for hd in hs]
        ms = [jnp.maximum(jnp.max(ss[hd], axis=-1, keepdims=True), sink_ref[0, hd]) for hd in hs]
        ps = [jnp.exp(ss[hd] - ms[hd]) for hd in hs]
        ess = [jnp.exp(sink_ref[0, hd] - ms[hd]) for hd in hs]
        invs = [1.0 / (jnp.sum(ps[hd], axis=-1, keepdims=True) + ess[hd]) for hd in hs]
        probs = [ps[hd] * invs[hd] for hd in hs]
        deltas = [jnp.sum(probs[hd] * d_probs[hd], axis=-1, keepdims=True) for hd in hs]
        d_ss = [(probs[hd] * (d_probs[hd] - deltas[hd])).astype(BF16) for hd in hs]
        dqs = [_dot(d_ss[hd], kb) for hd in hs]
        dq_ref[...] = jnp.concatenate([_rope_bwd(dqs[hd] * SCALE, c, s1, s2).astype(BF16) for hd in hs], axis=1)
        dks = [_dot_tn(d_ss[hd], qs[hd]) for hd in hs]
        dvs = [_dot_tn(probs[hd].astype(BF16), das[hd]) for hd in hs]
        dk_ref[band, :] += functools.reduce(lambda a, b: a + b, dks)
        dv_ref[band, :] += functools.reduce(lambda a, b: a + b, dvs)
        dsink_ref[...] += functools.reduce(lambda a, b: a + b, [
            jnp.where(lane == hd, -jnp.sum(ess[hd] * invs[hd] * deltas[hd], axis=0, keepdims=True), 0.0) for hd in hs])

        centre = pl.ds(pl.multiple_of(i * BLOCK, BLOCK), BLOCK)
        gs = range(N_POOL)
        sl = [slice(g * POOL_GROUP, (g + 1) * POOL_GROUP) for g in gs]
        devs = [(_dot(pm_ref[0, g], pc_ref[band, sl[g]].astype(BF16)) - pc_ref[centre, sl[g]]).astype(BF16) for g in gs]
        dys = [dp_ref[:, sl[g]].astype(F32) for g in gs]
        zs = [_dot(devs[g], pw_ref[g]) for g in gs]
        dzs = [(dys[g] * ps_ref[:, sl[g]]).astype(BF16) for g in gs]
        d_devs = [_dot_nt(dzs[g], pw_ref[g]) for g in gs]
        dps_ref[...] += jnp.concatenate([jnp.sum(dys[g] * zs[g], axis=0, keepdims=True) for g in gs], axis=1)
        for g in gs:
            dpw_ref[g] += _dot_tn(devs[g], dzs[g])
        dpc_ref[band, :] += jnp.concatenate([_dot_tn(pm_ref[0, g], d_devs[g].astype(BF16)) for g in gs], axis=1)
        dpc_ref[centre, :] -= jnp.concatenate(d_devs, axis=1)

    return _call(
        body, name, (nb,),
        [pl.BlockSpec(memory_space=pltpu.SMEM), _rows(BLOCK, Q_PAD), _full((S, KV_WIDTH)), _full((S, KV_WIDTH)),
         _full((S, POOL_WIDTH)), _rows(BLOCK, Q_PAD), _rows(BLOCK, POOL_WIDTH), _pool_spec(nb),
         _full((N_POOL, POOL_GROUP, POOL_GROUP)), _full((1, POOL_WIDTH)),
         _rows(BLOCK, LANES), _rows(BLOCK, LANES), _rows(BLOCK, LANES)],
        [_rows(BLOCK, Q_PAD), _full((S, KV_WIDTH)), _full((S, KV_WIDTH)), _full((S, POOL_WIDTH)),
         _full((1, LANES)), _full((N_POOL, POOL_GROUP, POOL_GROUP)), _full((1, POOL_WIDTH))],
        [jax.ShapeDtypeStruct((S, Q_PAD), BF16), jax.ShapeDtypeStruct((S, KV_WIDTH), F32),
         jax.ShapeDtypeStruct((S, KV_WIDTH), F32), jax.ShapeDtypeStruct((S, POOL_WIDTH), F32),
         jax.ShapeDtypeStruct((1, LANES), F32), jax.ShapeDtypeStruct((N_POOL, POOL_GROUP, POOL_GROUP), F32),
         jax.ShapeDtypeStruct((1, POOL_WIDTH), F32)],
        [], 56, (sink, q, k, v, pc, da, dp, pool_m, pool_w, pool_scale, rc, rs1, rs2), rider)


def _mix_out(h, a, p, wa, wp, name):
    S = h.shape[0]
    tile = min(TOK_TILE, S)

    def body(h_ref, a_ref, p_ref, wa_ref, wp_ref, o_ref):
        o_ref[...] = h_ref[...] + _dot(a_ref[...], wa_ref[...]) + _dot(p_ref[...], wp_ref[...])

    return pl.pallas_call(
        body, name=name, grid=(S // tile,),
        in_specs=[_rows(tile, D_MODEL), _rows(tile, Q_PAD), _rows(tile, POOL_WIDTH),
                  _full((Q_PAD, D_MODEL)), _full((POOL_WIDTH, D_MODEL))],
        out_specs=_rows(tile, D_MODEL),
        out_shape=jax.ShapeDtypeStruct((S, D_MODEL), F32),
        compiler_params=_params(("parallel",), 40),
    )(h, a, p, wa, wp)


def _loss_head(h, target, gain, name):
    S = h.shape[0]
    tile = min(TOK_TILE, S)

    def body(h_ref, t_ref, g_ref, dh_ref, dhalf_ref, loss_ref, dg_ref):
        @pl.when(pl.program_id(0) == 0)
        def _():
            loss_ref[...] = jnp.zeros_like(loss_ref)
            dg_ref[...] = jnp.zeros_like(dg_ref)

        g = g_ref[...]
        r, xh = _rms_stats(h_ref[...])
        err = xh * g - t_ref[...]
        loss_ref[...] += (0.5 / D_MODEL) * jnp.sum(err * err, axis=0, keepdims=True)
        dh, dg = _rms_bwd(err * (1.0 / D_MODEL), g, r, xh)
        dg_ref[...] += dg
        dh_ref[...] = dh
        dhalf_ref[...] = (0.5 * dh).astype(BF16)

    return pl.pallas_call(
        body, name=name, grid=(S // tile,),
        in_specs=[_rows(tile, D_MODEL), _rows(tile, D_MODEL), _full((1, D_MODEL))],
        out_specs=[_rows(tile, D_MODEL), _rows(tile, D_MODEL), _full((1, D_MODEL)), _full((1, D_MODEL))],
        out_shape=[jax.ShapeDtypeStruct((S, D_MODEL), F32), jax.ShapeDtypeStruct((S, D_MODEL), BF16),
                   jax.ShapeDtypeStruct((1, D_MODEL), F32), jax.ShapeDtypeStruct((1, D_MODEL), F32)],
        compiler_params=_params(("arbitrary",), 40),
    )(h, target, gain)


def _mix_out_bwd(dh_out, dn, h, gain, a, p, wa, wp, name, rider=None):
    S = h.shape[0]
    tile = min(TOK_TILE, S)

    def body(do_ref, dn_ref, h_ref, g_ref, a_ref, p_ref, wa_ref, wp_ref, dh_ref, da_ref, dp_ref, dwa_ref, dwp_ref, dg_ref):
        @pl.when(pl.program_id(0) == 0)
        def _():
            dwa_ref[...] = jnp.zeros_like(dwa_ref)
            dwp_ref[...] = jnp.zeros_like(dwp_ref)
            dg_ref[...] = jnp.zeros_like(dg_ref)

        r, xh = _rms_stats(h_ref[...])
        dnorm, dg = _rms_bwd(_sum_chunks([dn_ref]), g_ref[...], r, xh)
        dh = do_ref[...] + dnorm
        dg_ref[...] += dg
        dh_ref[...] = dh
        dhb = dh.astype(BF16)
        da_ref[...] = _dot_nt(dhb, wa_ref[...]).astype(BF16)
        dp_ref[...] = _dot_nt(dhb, wp_ref[...]).astype(BF16)
        dwa_ref[...] += _dot_tn(a_ref[...], dhb)
        dwp_ref[...] += _dot_tn(p_ref[...], dhb)

    return _call(
        body, name, (S // tile,),
        [_rows(tile, D_MODEL), _chunk_rows(tile, dn.shape[0]), _rows(tile, D_MODEL), _full((1, D_MODEL)),
         _rows(tile, Q_PAD), _rows(tile, POOL_WIDTH), _full((Q_PAD, D_MODEL)), _full((POOL_WIDTH, D_MODEL))],
        [_rows(tile, D_MODEL), _rows(tile, Q_PAD), _rows(tile, POOL_WIDTH),
         _full((Q_PAD, D_MODEL)), _full((POOL_WIDTH, D_MODEL)), _full((1, D_MODEL))],
        [jax.ShapeDtypeStruct((S, D_MODEL), F32), jax.ShapeDtypeStruct((S, Q_PAD), BF16),
         jax.ShapeDtypeStruct((S, POOL_WIDTH), BF16), jax.ShapeDtypeStruct((Q_PAD, D_MODEL), F32),
         jax.ShapeDtypeStruct((POOL_WIDTH, D_MODEL), F32), jax.ShapeDtypeStruct((1, D_MODEL), F32)],
        [], 48, (dh_out, dn, h, gain, a, p, wa, wp), rider)


def _mix_in_bwd(dh_out, h, gain, n, dq, dk, dv, dpc, rc, rs1, rs2, w_in, name, rider=None):
    S = h.shape[0]
    tile = min(TOK_TILE, S)

    def body(do_ref, h_ref, g_ref, n_ref, dq_ref, dk_ref, dv_ref, dpc_ref, c_ref, s1_ref, s2_ref, w_ref,
             dh_ref, dhalf_ref, dw_ref, dg_ref):
        @pl.when(pl.program_id(0) == 0)
        def _():
            dw_ref[...] = jnp.zeros_like(dw_ref)
            dg_ref[...] = jnp.zeros_like(dg_ref)

        dk = _rope_bwd(dk_ref[...], c_ref[...], s1_ref[...], s2_ref[...]).astype(BF16)
        du = jnp.concatenate([dq_ref[...], dk, dv_ref[...].astype(BF16), dpc_ref[...].astype(BF16)], axis=1)
        dn = _dot(du, w_ref[...])
        dw_ref[...] += _dot_tn(du, n_ref[...])
        r, xh = _rms_stats(h_ref[...])
        dnorm, dg = _rms_bwd(dn, g_ref[...], r, xh)
        dh = do_ref[...] + dnorm
        dg_ref[...] += dg
        dh_ref[...] = dh
        dhalf_ref[...] = (0.5 * dh).astype(BF16)

    return _call(
        body, name, (S // tile,),
        [_rows(tile, D_MODEL), _rows(tile, D_MODEL), _full((1, D_MODEL)), _rows(tile, D_MODEL),
         _rows(tile, Q_PAD), _rows(tile, KV_WIDTH), _rows(tile, KV_WIDTH), _rows(tile, POOL_WIDTH),
         _rows(tile, LANES), _rows(tile, LANES), _rows(tile, LANES), _full((U_PAD, D_MODEL))],
        [_rows(tile, D_MODEL), _rows(tile, D_MODEL), _full((U_PAD, D_MODEL)), _full((1, D_MODEL))],
        [jax.ShapeDtypeStruct((S, D_MODEL), F32), jax.ShapeDtypeStruct((S, D_MODEL), BF16),
         jax.ShapeDtypeStruct((U_PAD, D_MODEL), F32), jax.ShapeDtypeStruct((1, D_MODEL), F32)],
        [], 56, (dh_out, h, gain, n, dq, dk, dv, dpc, rc, rs1, rs2, w_in), rider)


def _norm_bwd(dh_out, dns, h, gain, name):
    S = h.shape[0]
    tile = min(TOK_TILE, S)
    n = len(dns)

    def body(do_ref, *refs):
        h_ref, g_ref, dh_ref, dg_ref = refs[n:]

        @pl.when(pl.program_id(0) == 0)
        def _():
            dg_ref[...] = jnp.zeros_like(dg_ref)

        r, xh = _rms_stats(h_ref[...])
        dnorm, dg = _rms_bwd(_sum_chunks(refs[:n]), g_ref[...], r, xh)
        dg_ref[...] += dg
        dh_ref[...] = do_ref[...] + dnorm

    return _call(
        body, name, (S // tile,),
        [_rows(tile, D_MODEL)] + [_chunk_rows(tile, dn.shape[0]) for dn in dns] + [_rows(tile, D_MODEL), _full((1, D_MODEL))],
        [_rows(tile, D_MODEL), _full((1, D_MODEL))],
        [jax.ShapeDtypeStruct((S, D_MODEL), F32), jax.ShapeDtypeStruct((1, D_MODEL), F32)],
        [], 40, (dh_out, *dns, h, gain))


def _rope_tables(S):
    half = ROTARY_DIM // 2
    inv_freq = ROPE_THETA ** (-jnp.arange(0, ROTARY_DIM, 2, dtype=F32) / ROTARY_DIM)
    dim = jnp.arange(LANES) % HEAD_DIM
    ang = jnp.arange(S, dtype=F32)[:, None] * inv_freq[dim % half][None, :]
    lo, hi = (dim < half)[None, :], ((dim >= half) & (dim < ROTARY_DIM))[None, :]
    c = jnp.where(lo | hi, jnp.cos(ang), 1.0)
    s1 = jnp.where(lo, -jnp.sin(ang), 0.0)
    s2 = jnp.where(hi, jnp.sin(ang), 0.0)
    return c, s1, s2


def _pad_heads(w, axis):
    w = jnp.moveaxis(w, axis, 0)
    heads = w.reshape((N_HEADS, HEAD_DIM) + w.shape[1:])
    zero = jnp.zeros_like(heads)
    first = (jnp.arange(N_HEADS) < Q_PER_KV).reshape((N_HEADS, 1) + (1,) * (w.ndim - 1))
    lo = jnp.where(first, heads, zero)
    hi = jnp.where(first, zero, heads)
    padded = jnp.concatenate([lo, hi], axis=1).reshape((Q_PAD,) + w.shape[1:])
    return jnp.moveaxis(padded, 0, axis)


def _unpad_heads(w, axis):
    w = jnp.moveaxis(w, axis, 0)
    groups = w.reshape((N_HEADS, 2, HEAD_DIM) + w.shape[1:])
    first = (jnp.arange(N_HEADS) < Q_PER_KV).reshape((N_HEADS, 1) + (1,) * (w.ndim - 1))
    heads = jnp.where(first, groups[:, 0], groups[:, 1]).reshape((ATTN_WIDTH,) + w.shape[1:])
    return jnp.moveaxis(heads, 0, axis)


IN_ROWS = IN_WIDTH // N_CHIPS
OUT_ROWS = (ATTN_WIDTH + POOL_WIDTH) // N_CHIPS
MIX_ROWS = IN_ROWS + OUT_ROWS
FFN_ROWS = 3 * FF_CHUNK


def _step(x, target, bufs, small, place):
    S = x.shape[0]
    rc, rs1, rs2 = _rope_tables(S)
    mine = place[1]
    order = jnp.stack([mine, mine ^ 2, mine ^ 1, mine ^ 3])
    h1, n1, gate1, up1, ffn1, mix = _ffn_fwd(order, x, small["ffn1_norm"], "ffn1_fwd",
                                             _merge(_allgather(bufs[:1], in_passes=0), _allgather(bufs[1:2])), 0)
    w_in_t = mix[:, :IN_ROWS].reshape(IN_WIDTH, D_MODEL)
    w_in_pad = jnp.concatenate([_pad_heads(w_in_t[:ATTN_WIDTH], 0), w_in_t[ATTN_WIDTH:]], axis=0)
    w_out = mix[:, IN_ROWS:].reshape(ATTN_WIDTH + POOL_WIDTH, D_MODEL)
    wa = _pad_heads(w_out[:ATTN_WIDTH], 0)
    wp = w_out[ATTN_WIDTH:]
    pool_w = small["pool_w"].astype(BF16)

    n2, q, k, v, pc = _mix_in(h1, small["mix_norm"], w_in_pad, rc, rs1, rs2, "mix_in")
    pool_m = _pool_matrices(S)
    a, p, ffn2 = _mix_core_fwd(q, k, v, pc, small["sink_logits"], pool_m, pool_w, small["pool_scale"], "mix_core_fwd",
                               _allgather(bufs[2:], peers=X_Y_DIAGONAL[:2]))
    h2 = _mix_out(h1, a, p, wa, wp, "mix_out")
    h3, n3, gate2, up2, ffn2 = _ffn_fwd(order, h2, small["ffn2_norm"], "ffn2_fwd",
                                        _allgather([ffn2], peers=X_Y_DIAGONAL[2:], in_passes=0), 0)
    dh3, dhalf3, loss_lanes, d_final = _loss_head(h3, target, small["final_norm"], "loss_head")

    dn3, d_ffn2 = _ffn_bwd(jnp.arange(N_CHIPS, dtype=jnp.int32), dhalf3, n3, gate2, up2, ffn2, "ffn2_bwd")
    dh2, da, dp, dwa, dwp, d_ffn2_norm, received = _mix_out_bwd(dh3, dn3, h2, small["ffn2_norm"], a, p, wa, wp, "mix_out_bwd",
                                                                _sibling_exchange([d_ffn2]))
    pair = _pair_sum(place, d_ffn2, received, "grad_pair_sum_ffn2")
    dq, dk, dv, dpc, dsink, dpool_w, dpool_scale, stack = _mix_core_bwd(
        q, k, v, pc, da, dp, small["sink_logits"], pool_m, pool_w, small["pool_scale"], rc, rs1, rs2, "mix_core_bwd",
        _scatter([pair], peers=X_Y_DIAGONAL[:2]))
    dh1, dhalf1, dw_in_pad, d_mix_norm, stack = _mix_in_bwd(dh2, h1, small["mix_norm"], n2, dq, dk, dv, dpc, rc, rs1, rs2,
                                                            w_in_pad, "mix_in_bwd",
                                                            _scatter([pair], peers=X_Y_DIAGONAL[2:], stacks=[stack]))
    reduced_ffn2 = _chip_sum(place, pair, stack, 2, "grad_chip_sum_ffn2")
    dw_in_t = jnp.concatenate([_unpad_heads(dw_in_pad[:Q_PAD], 0), dw_in_pad[Q_PAD:]], axis=0)
    dw_out = jnp.concatenate([_unpad_heads(dwa, 0), dwp], axis=0)
    d_mix = jnp.concatenate([dw_in_t.reshape(N_CHIPS, IN_ROWS, D_MODEL), dw_out.reshape(N_CHIPS, OUT_ROWS, D_MODEL)], axis=1)
    d_mix = jnp.transpose(d_mix.reshape(N_CHIPS, 2, MIX_ROWS // 2, D_MODEL), (1, 0, 2, 3)).astype(BF16)
    small_g = {"ffn1_norm": jnp.zeros_like(d_mix_norm), "mix_norm": d_mix_norm, "ffn2_norm": d_ffn2_norm,
               "final_norm": d_final, "pool_scale": dpool_scale, "sink_logits": dsink[:, :N_HEADS], "pool_w": dpool_w}
    loss_row = jnp.sum(loss_lanes.reshape(D_MODEL // LANES, LANES), axis=0, keepdims=True)
    small_early = _pack_small(small_g, loss_row)

    chunk = [(place[1:] + 1 + p) % N_CHIPS for p in range(N_CHIPS)]
    ffn1_bwd = functools.partial(_ffn_bwd, d_out=dhalf1, n=n1, gate=gate1, up=up1, group=ffn1)
    stack = jnp.zeros((N_CHIPS, FFN_ROWS // 2, D_MODEL), BF16)
    dn_a, dw_a, recv_mix, small_all, g_ffn2 = ffn1_bwd(
        chunk[0], name="ffn1_bwd_0",
        rider=_merge(_merge(_sibling_exchange([d_mix]), _small_allgather(small_early)), _sibling_share([reduced_ffn2])))
    pair_mix = _pair_sum(place, d_mix, recv_mix, "grad_pair_sum_mix")
    dn_b, dw_b, recv_a, stack_mix = ffn1_bwd(chunk[1], name="ffn1_bwd_1",
                                             rider=_merge(_sibling_exchange([dw_a]), _scatter([pair_mix])))
    pair_a = _pair_sum(place, dw_a, recv_a, "grad_pair_sum_ffn1_0")
    reduced_mix = _chip_sum(place, pair_mix, stack_mix, 2, "grad_chip_sum_mix")
    dn_c, dw_c, recv_b, stack, g_mix = ffn1_bwd(
        chunk[2], name="ffn1_bwd_2",
        rider=_merge(_merge(_sibling_exchange([dw_b]), _scatter_step(pair_a, stack, 0)), _sibling_share([reduced_mix])))
    pair_b = _pair_sum(place, dw_b, recv_b, "grad_pair_sum_ffn1_1")
    dn_d, dw_d, recv_c, stack = ffn1_bwd(chunk[3], name="ffn1_bwd_3",
                                         rider=_merge(_sibling_exchange([dw_c]), _scatter_step(pair_b, stack, 1)))
    pair_c = _pair_sum(place, dw_c, recv_c, "grad_pair_sum_ffn1_2")
    grad_x, d_ffn1_norm = _norm_bwd(dh1, [dn_a, dn_b, dn_c, dn_d], x, small["ffn1_norm"], "norm1_bwd")

    recv_d, stack, gains = _comm_call(
        _merge(_merge(_sibling_exchange([dw_d]), _scatter_step(pair_c, stack, 2)),
               _small_allgather(d_ffn1_norm.reshape(-1, LANES))), "grad_tail")
    pair_d = _pair_sum(place, dw_d, recv_d, "grad_pair_sum_ffn1_3")
    reduced_ffn1 = _chip_sum(place, pair_d, stack, 2, "grad_chip_sum_ffn1")
    (g_ffn1,) = _comm_call(_sibling_share([reduced_ffn1]), "grad_share_tail")
    gain_sum = _sum_leading(gains, 1, "gain_grad_sum")
    small_sum = jnp.concatenate([gain_sum, _sum_leading(small_all, 1, "small_grad_sum")[gain_sum.shape[0]:]], axis=0)
    return jnp.sum(small_sum[SMALL_ROWS - 1]), grad_x, [g.reshape(-1, D_MODEL) for g in (g_ffn1, g_mix, g_ffn2)], small_sum


GROUPS =(("ffn1_w_gate", "ffn1_w_up", "ffn1_w_down"), ("w_in", "w_out"), ("ffn2_w_gate", "ffn2_w_up", "ffn2_w_down"))
TRANSPOSED = ("ffn1_w_gate", "ffn1_w_up", "w_in", "ffn2_w_gate", "ffn2_w_up")


def _place():
    x, y, c = lax.axis_index("x"), lax.axis_index("y"), lax.axis_index("c")
    chips = [(1 - x, y), (x, 1 - y), (1 - x, 1 - y)]
    return x, y, c, chips


def _remote(src, dst, send_sem, recv_sem, to):
    return pltpu.make_async_remote_copy(src_ref=src, dst_ref=dst, send_sem=send_sem, recv_sem=recv_sem,
                                        device_id=to, device_id_type=MESH)


def _pack(chip, members, name):
    rows = members[0].shape[0]
    n = len(members)

    def body(chip_ref, *refs):
        ins, out_ref, buf, sems = refs[:n], refs[n], refs[n + 1], refs[n + 2]
        copies = [pltpu.make_async_copy(ins[k], buf.at[k], sems.at[k]) for k in range(n)]
        for cp in copies:
            cp.start()
        for k in range(n):
            copies[k].wait()
            out_ref[0, k * rows:(k + 1) * rows, :] = buf[k].astype(BF16)

    return pl.pallas_call(
        body, name=name,
        grid_spec=pltpu.PrefetchScalarGridSpec(
            num_scalar_prefetch=1, grid=(1,),
            in_specs=[HBM_SPEC] * n,
            out_specs=pl.BlockSpec((1, n * rows, D_MODEL), lambda k, chip_ref: (chip_ref[0], 0, 0)),
            scratch_shapes=[pltpu.VMEM((n, rows, D_MODEL), F32), pltpu.SemaphoreType.DMA((n,))]),
        out_shape=jax.ShapeDtypeStruct((N_CHIPS, n * rows, D_MODEL), BF16),
        compiler_params=_params(("arbitrary",), 40),
    )(chip, *members)


def _same(arrays):
    return [jax.ShapeDtypeStruct(a.shape, a.dtype) for a in arrays]


X_Y_DIAGONAL = (0, 1, 2)


def _allgather(bufs, peers=X_Y_DIAGONAL, in_passes=None):
    n = len(bufs)

    def copy(kind, outs, send_sems, recv_sems, a, k):
        x, y, c, chips = _place()
        half = bufs[a].shape[1] // 2

        def rows(slot, core):
            return outs[a].at[slot, pl.ds(pl.multiple_of(core * half, 16), half)]

        me, slot = 2 * x + y, 2 * chips[k][0] + chips[k][1]
        over_ici = (send_sems.at[6 * a + k], recv_sems.at[6 * a + k])
        over_d2d = (send_sems.at[6 * a + 3 + k], recv_sems.at[6 * a + 3 + k])
        if kind == "first":
            return _remote(rows(me, c), rows(me, c), *over_ici, (*chips[k], c))
        if kind == "landed":
            return _remote(rows(me, c), rows(slot, c), *over_ici, (*chips[k], c))
        if kind == "passed":
            return _remote(rows(slot, c), rows(slot, c), *over_d2d, (x, y, 1 - c))
        return _remote(rows(me, c), rows(slot, 1 - c), *over_d2d, (x, y, 1 - c))

    def arrive(pairs):
        def hook(ins, outs, sems):
            for a, k in pairs:
                copy("landed", outs, *sems, a, k).wait_recv()
                copy("passed", outs, *sems, a, k).start()
            for a, k in pairs:
                copy("handed", outs, *sems, a, k).wait_recv()
        return hook

    everything = [(a, k) for a in range(n) for k in peers]
    early = [(a, k) for a, k in everything if a == in_passes]

    def start(ins, outs, sems):
        for a, k in everything:
            copy("first", outs, *sems, a, k).start()

    def finish(ins, outs, sems):
        arrive([pair for pair in everything if pair not in early])(ins, outs, sems)
        for a, k in everything:
            copy("first", outs, *sems, a, k).wait_send()
            copy("passed", outs, *sems, a, k).wait_send()

    hooks = tuple(((k + 1, 0), arrive([(a, k)])) for a, k in early)
    return _Rider(list(bufs), _same(bufs), {a: a for a in range(n)},
                  [pltpu.SemaphoreType.DMA((6 * n,)), pltpu.SemaphoreType.DMA((6 * n,))], start, finish, hooks)


def _sibling_exchange(parts):
    n = len(parts)

    def copies(ins, outs, send_sems, recv_sems):
        x, y, c, _ = _place()
        return [_remote(ins[a].at[1 - c], outs[a], send_sems.at[a], recv_sems.at[a], (x, y, 1 - c)) for a in range(n)]

    def start(ins, outs, sems):
        for cp in copies(ins, outs, *sems):
            cp.start()

    def finish(ins, outs, sems):
        for cp in copies(ins, outs, *sems):
            cp.wait_recv()
            cp.wait_send()

    return _Rider(list(parts), [jax.ShapeDtypeStruct(p.shape[1:], p.dtype) for p in parts], {},
                  [pltpu.SemaphoreType.DMA((n,)), pltpu.SemaphoreType.DMA((n,))], start, finish)


def _small_allgather(small):
    flips = [(fx, fy, fc) for fx in range(2) for fy in range(2) for fc in range(2)][1:]

    def copies(small_ref, gather_ref, send_sems, recv_sems, local_sem, started_only=False):
        x, y, c, _ = _place()
        me = 4 * x + 2 * y + c
        peers = [((1 - x) if fx else x, (1 - y) if fy else y, (1 - c) if fc else c) for fx, fy, fc in flips]
        own = pltpu.make_async_copy(small_ref, gather_ref.at[me], local_sem)
        sent = [_remote(small_ref, gather_ref.at[me], send_sems.at[k], recv_sems.at[k], peer) for k, peer in enumerate(peers)]
        if started_only:
            return own, sent
        landed = [_remote(small_ref, gather_ref.at[4 * px + 2 * py + pc], send_sems.at[k], recv_sems.at[k], (px, py, pc))
                  for k, (px, py, pc) in enumerate(peers)]
        return own, sent, landed

    def start(ins, outs, sems):
        own, sent = copies(ins[0], outs[0], *sems, started_only=True)
        own.start()
        for cp in sent:
            cp.start()

    def finish(ins, outs, sems):
        own, sent, landed = copies(ins[0], outs[0], *sems)
        for cp in landed:
            cp.wait_recv()
        for cp in sent:
            cp.wait_send()
        own.wait()

    return _Rider([small], [jax.ShapeDtypeStruct((2 * N_CHIPS,) + small.shape, small.dtype)], {},
                  [pltpu.SemaphoreType.DMA((7,)), pltpu.SemaphoreType.DMA((7,)), pltpu.SemaphoreType.DMA], start, finish)


def _merge(a, b):
    na, nao, nas = len(a.operands), len(a.out_shapes), len(a.scratch)

    def start(ins, outs, sems):
        a.start(ins[:na], outs[:nao], sems[:nas])
        b.start(ins[na:], outs[nao:], sems[nas:])

    def finish(ins, outs, sems):
        a.finish(ins[:na], outs[:nao], sems[:nas])
        b.finish(ins[na:], outs[nao:], sems[nas:])

    def of_a(fn):
        return lambda ins, outs, sems: fn(ins[:na], outs[:nao], sems[:nas])

    def of_b(fn):
        return lambda ins, outs, sems: fn(ins[na:], outs[nao:], sems[nas:])

    aliases = {**a.aliases, **{na + k: nao + v for k, v in b.aliases.items()}}
    hooks = tuple((at, of_a(fn)) for at, fn in a.hooks) + tuple((at, of_b(fn)) for at, fn in b.hooks)
    return _Rider(a.operands + b.operands, a.out_shapes + b.out_shapes, aliases, a.scratch + b.scratch, start, finish, hooks)


def _scatter_step(pair, stack, step):
    def copies(pair_ref, stack_ref, send_sem, recv_sem, started_only=False):
        x, y, c, _ = _place()
        me = 2 * x + y
        to = (me + 1 + step) % N_CHIPS
        frm = (me + N_CHIPS - 1 - step) % N_CHIPS
        sent = _remote(pair_ref.at[0], stack_ref.at[me], send_sem, recv_sem, (to // 2, to % 2, c))
        if started_only:
            return sent
        landed = _remote(pair_ref.at[0], stack_ref.at[frm], send_sem, recv_sem, (frm // 2, frm % 2, c))
        return sent, landed

    def start(ins, outs, sems):
        copies(ins[0], outs[0], *sems, started_only=True).start()

    def finish(ins, outs, sems):
        sent, landed = copies(ins[0], outs[0], *sems)
        landed.wait_recv()
        sent.wait_send()

    return _Rider([pair, stack], _same([stack]), {1: 0}, [pltpu.SemaphoreType.DMA, pltpu.SemaphoreType.DMA], start, finish)


def _scatter(sums, peers=X_Y_DIAGONAL, stacks=None):
    n = len(sums)

    def copies(ins, outs, send_sems, recv_sems, started_only=False):
        x, y, c, chips = _place()
        me = 2 * x + y
        slots = [2 * cx + cy for cx, cy in chips]
        sent = [_remote(ins[a].at[slots[k]], outs[a].at[me], send_sems.at[3 * a + k], recv_sems.at[3 * a + k], (*chips[k], c))
                for a in range(n) for k in peers]
        if started_only:
            return sent
        landed = [_remote(ins[a].at[slots[k]], outs[a].at[slots[k]], send_sems.at[3 * a + k], recv_sems.at[3 * a + k],
                          (*chips[k], c)) for a in range(n) for k in peers]
        return sent, landed

    def start(ins, outs, sems):
        for cp in copies(ins, outs, *sems, started_only=True):
            cp.start()

    def finish(ins, outs, sems):
        sent, landed = copies(ins, outs, *sems)
        for cp in landed:
            cp.wait_recv()
        for cp in sent:
            cp.wait_send()

    sems = [pltpu.SemaphoreType.DMA((3 * n,)), pltpu.SemaphoreType.DMA((3 * n,))]
    if stacks is None:
        return _Rider(list(sums), _same(sums), {}, sems, start, finish)
    return _Rider(list(sums) + list(stacks), _same(sums), {n + a: a for a in range(n)}, sems, start, finish)


def _sibling_share(bufs):
    n = len(bufs)

    def copies(outs, send_sems, recv_sems, started_only=False):
        x, y, c, _ = _place()
        sent = [_remote(outs[a].at[c], outs[a].at[c], send_sems.at[a], recv_sems.at[a], (x, y, 1 - c)) for a in range(n)]
        if started_only:
            return sent
        landed = [_remote(outs[a].at[c], outs[a].at[1 - c], send_sems.at[a], recv_sems.at[a], (x, y, 1 - c)) for a in range(n)]
        return sent, landed

    def start(ins, outs, sems):
        for cp in copies(outs, *sems, started_only=True):
            cp.start()

    def finish(ins, outs, sems):
        sent, landed = copies(outs, *sems)
        for cp in landed:
            cp.wait_recv()
        for cp in sent:
            cp.wait_send()

    return _Rider(list(bufs), _same(bufs), {a: a for a in range(n)},
                  [pltpu.SemaphoreType.DMA((n,)), pltpu.SemaphoreType.DMA((n,))], start, finish)


def _pair_sum(core, part, received, name):
    _, k, rh, cols = part.shape

    def body(core_ref, p_ref, r_ref, o_ref):
        o_ref[...] = (p_ref[0].astype(F32) + r_ref[...].astype(F32)).astype(BF16)

    return pl.pallas_call(
        body, name=name,
        grid_spec=pltpu.PrefetchScalarGridSpec(
            num_scalar_prefetch=1, grid=(k,),
            in_specs=[pl.BlockSpec((1, 1, rh, cols), lambda j, core_ref: (core_ref[0], j, 0, 0)),
                      pl.BlockSpec((1, rh, cols), lambda j, core_ref: (j, 0, 0))],
            out_specs=pl.BlockSpec((1, rh, cols), lambda j, core_ref: (j, 0, 0))),
        out_shape=jax.ShapeDtypeStruct((k, rh, cols), BF16),
        compiler_params=_params(("parallel",), 32),
    )(core, part, received)


def _sum_leading(stack, steps, name):
    k, rows, cols = stack.shape
    tile = rows // steps

    def body(s_ref, o_ref):
        total = s_ref[0].astype(F32)
        for d in range(1, k):
            total = total + s_ref[d].astype(F32)
        o_ref[...] = total

    return pl.pallas_call(
        body, name=name, grid=(steps,),
        in_specs=[pl.BlockSpec((k, tile, cols), lambda i: (0, i, 0))],
        out_specs=pl.BlockSpec((tile, cols), lambda i: (i, 0)),
        out_shape=jax.ShapeDtypeStruct((rows, cols), F32),
        compiler_params=_params(("parallel",), 32),
    )(stack)


def _chip_sum(place, own, stack, steps, name):
    k, rows, cols = stack.shape
    tile = rows // steps

    def body(place_ref, own_ref, *refs):
        chip = place_ref[1]
        total = None
        for d in range(k):
            term = jnp.where(chip == d, own_ref[0], refs[d][0]).astype(F32)
            total = term if total is None else total + term
        refs[k][0] = total

    def other(d):
        return lambda i, place_ref: (jnp.where(place_ref[1] == d, (d + 1) % k, d), i, 0)

    return pl.pallas_call(
        body, name=name,
        grid_spec=pltpu.PrefetchScalarGridSpec(
            num_scalar_prefetch=1, grid=(steps,),
            in_specs=[pl.BlockSpec((1, tile, cols), lambda i, place_ref: (place_ref[1] % own.shape[0], i, 0))]
            + [pl.BlockSpec((1, tile, cols), other(d)) for d in range(k)],
            out_specs=pl.BlockSpec((1, tile, cols), lambda i, place_ref: (place_ref[0], i, 0))),
        out_shape=jax.ShapeDtypeStruct((2, rows, cols), F32),
        compiler_params=_params(("arbitrary",), 32),
    )(place, own, *([stack] * k))


def _adamw(w, g, row0, m, v, tile, name):
    rows, cols = w.shape
    first = row0 // tile
    assert rows % tile == 0 and row0 % tile == 0
    bc1 = 1.0 - ADAM_B1 ** ADAM_STEP
    bc2 = 1.0 - ADAM_B2 ** ADAM_STEP

    def body(w_ref, g_ref, m_ref, v_ref, go_ref, d_ref, mo_ref, vo_ref):
        g = g_ref[...]
        m_new = ADAM_B1 * m_ref[...] + (1.0 - ADAM_B1) * g
        v_new = ADAM_B2 * v_ref[...] + (1.0 - ADAM_B2) * (g * g)
        go_ref[...] = g
        d_ref[...] = -ADAM_LR * ((m_new / bc1) / (jnp.sqrt(v_new / bc2) + ADAM_EPS) + ADAM_WD * w_ref[...])
        mo_ref[...] = m_new
        vo_ref[...] = v_new

    spec = pl.BlockSpec((tile, cols), lambda i: (i, 0))
    g_spec = pl.BlockSpec((tile, cols), lambda i: (first + i, 0))
    return pl.pallas_call(
        body, name=name, grid=(rows // tile,),
        in_specs=[spec, g_spec, spec, spec], out_specs=[spec] * 4,
        out_shape=[jax.ShapeDtypeStruct((rows, cols), F32)] * 4,
        compiler_params=_params(("parallel",), 32),
    )(w, g, m, v)


SMALL = ("ffn1_norm", "mix_norm", "ffn2_norm", "final_norm", "pool_scale", "sink_logits", "pool_w")


def _pack_small(d, last_row=None):
    sink = jnp.pad(d["sink_logits"].reshape(1, N_HEADS), ((0, 0), (0, LANES - N_HEADS)))
    rows = [d[n].reshape(-1, LANES) for n in SMALL[:5]] + [sink, d["pool_w"].reshape(-1, LANES)]
    used = sum(r.shape[0] for r in rows)
    last = jnp.zeros((1, LANES), F32) if last_row is None else last_row
    return jnp.concatenate(rows + [jnp.zeros((SMALL_ROWS - used - 1, LANES), F32), last], axis=0)


def _unpack_small(packed, like):
    out, row = {}, 0
    for n in SMALL:
        size = LANES if n == "sink_logits" else math.prod(like[n].shape)
        chunk = packed[row:row + size // LANES].reshape(-1)
        out[n] = (chunk[:N_HEADS] if n == "sink_logits" else chunk).reshape(like[n].shape)
        row += size // LANES
    return out


def kernel(x, ffn1_norm, ffn1_w_gate, ffn1_w_up, ffn1_w_down, mix_norm, w_in, sink_logits, pool_w, pool_scale, w_out, ffn2_norm, ffn2_w_gate, ffn2_w_up, ffn2_w_down, final_norm, loss_target, m_ffn1_norm, m_ffn1_w_gate, m_ffn1_w_up, m_ffn1_w_down, m_mix_norm, m_w_in, m_sink_logits, m_pool_w, m_pool_scale, m_w_out, m_ffn2_norm, m_ffn2_w_gate, m_ffn2_w_up, m_ffn2_w_down, m_final_norm, v_ffn1_norm, v_ffn1_w_gate, v_ffn1_w_up, v_ffn1_w_down, v_mix_norm, v_w_in, v_sink_logits, v_pool_w, v_pool_scale, v_w_out, v_ffn2_norm, v_ffn2_w_gate, v_ffn2_w_up, v_ffn2_w_down, v_final_norm):
    names = ("ffn1_norm", "ffn1_w_gate", "ffn1_w_up", "ffn1_w_down", "mix_norm", "w_in", "sink_logits", "pool_w",
             "pool_scale", "w_out", "ffn2_norm", "ffn2_w_gate", "ffn2_w_up", "ffn2_w_down", "final_norm")
    weights = dict(zip(names, (ffn1_norm, ffn1_w_gate, ffn1_w_up, ffn1_w_down, mix_norm, w_in, sink_logits, pool_w,
                               pool_scale, w_out, ffn2_norm, ffn2_w_gate, ffn2_w_up, ffn2_w_down, final_norm)))
    mom1 = dict(zip(names, (m_ffn1_norm, m_ffn1_w_gate, m_ffn1_w_up, m_ffn1_w_down, m_mix_norm, m_w_in, m_sink_logits,
                            m_pool_w, m_pool_scale, m_w_out, m_ffn2_norm, m_ffn2_w_gate, m_ffn2_w_up, m_ffn2_w_down,
                            m_final_norm)))
    mom2 = dict(zip(names, (v_ffn1_norm, v_ffn1_w_gate, v_ffn1_w_up, v_ffn1_w_down, v_mix_norm, v_w_in, v_sink_logits,
                            v_pool_w, v_pool_scale, v_w_out, v_ffn2_norm, v_ffn2_w_gate, v_ffn2_w_up, v_ffn2_w_down,
                            v_final_norm)))
    chip = (2 * lax.axis_index("x") + lax.axis_index("y")).astype(jnp.int32).reshape(1)
    place = jnp.concatenate([lax.axis_index("c").astype(jnp.int32).reshape(1), chip])

    def rows_of(t, n):
        return jnp.swapaxes(t[n][0], 0, 1) if n in TRANSPOSED else t[n][0]

    bufs = [_pack(chip, [rows_of(weights, n) for n in GROUPS[0]], "pack_ffn1"),
            _pack(chip, [jnp.concatenate([rows_of(weights, n) for n in GROUPS[1]], axis=0)], "pack_mix"),
            _pack(chip, [rows_of(weights, n) for n in GROUPS[2]], "pack_ffn2")]

    small_w = {"ffn1_norm": ffn1_norm, "mix_norm": mix_norm, "ffn2_norm": ffn2_norm,
               "final_norm": final_norm.reshape(1, D_MODEL), "pool_scale": pool_scale, "sink_logits": sink_logits,
               "pool_w": pool_w[0]}
    loss, grad_x, group_grads, small_sum = _step(x[0], loss_target[0], bufs, small_w, place)

    out_g, out_d, out_m, out_v = {}, {}, {}, {}
    for members, g in zip(GROUPS, group_grads):
        row0 = 0
        for n in members:
            w = rows_of(weights, n)
            tile = FF_CHUNK // 4 if w.shape[0] == FF_CHUNK else math.gcd(IN_ROWS, OUT_ROWS)
            outs = _adamw(w, g, row0, rows_of(mom1, n), rows_of(mom2, n), tile, "adamw_" + n)
            row0 += w.shape[0]
            for dst, t in zip((out_g, out_d, out_m, out_v), outs):
                dst[n] = (jnp.swapaxes(t, 0, 1) if n in TRANSPOSED else t).reshape(weights[n].shape)
    small_outs = _adamw(_pack_small(weights), small_sum, 0, _pack_small(mom1), _pack_small(mom2), SMALL_ROWS, "adamw_small")
    for dst, packed in zip((out_g, out_d, out_m, out_v), small_outs):
        dst.update(_unpack_small(packed, weights))

    return (loss,grad_x.reshape(x.shape), *[out_g[n] for n in names], *[out_d[n] for n in names],
            *[out_m[n] for n in names], *[out_v[n] for n in names])
```

```python
import collections
import functools
import math

import jax
import jax.numpy as jnp
from jax import lax
from jax.experimental import pallas as pl
from jax.experimental.pallas import tpu as pltpu

F32, BF16 = jnp.float32, jnp.bfloat16
MESH = pl.DeviceIdType.MESH

D_MODEL = 1024
D_FF = 2816
N_CHIPS = 4
FF_CHUNK = D_FF // N_CHIPS
HEAD_DIM = 64
N_HEADS = 8
N_KV = 2
Q_PER_KV = N_HEADS // N_KV
KV_WIDTH = N_KV * HEAD_DIM
ATTN_WIDTH = N_HEADS * HEAD_DIM
POOL_WINDOWS = (2, 4, 8, 16)
N_POOL = len(POOL_WINDOWS)
POOL_GROUP = 128
POOL_WIDTH = N_POOL * POOL_GROUP
IN_WIDTH = ATTN_WIDTH + 2 * KV_WIDTH + POOL_WIDTH
WINDOW = 128
BLOCK = 128
BAND = 3 * BLOCK
ROPE_THETA = 500000.0
ROTARY_DIM = HEAD_DIM // 4
EPS = 1e-6
LANES = 128
Q_PAD = N_HEADS * LANES
U_PAD = Q_PAD + 2 * KV_WIDTH + POOL_WIDTH
SCALE = HEAD_DIM ** -0.5
NEG = -1e30

ADAM_LR, ADAM_B1, ADAM_B2, ADAM_EPS, ADAM_WD, ADAM_STEP = 0.001, 0.9, 0.999, 1e-08, 0.01, 10

V7X_VMEM_BYTES = 64 * 1024 * 1024
TOK_TILE = 512
SMALL_ROWS = 552


def _params(sem, vmem_mb):
    assert vmem_mb * 1024 * 1024 <= V7X_VMEM_BYTES
    return pltpu.CompilerParams(dimension_semantics=sem, vmem_limit_bytes=vmem_mb * 1024 * 1024)


def _dot(a, b):
    return lax.dot_general(a, b, (((1,), (0,)), ((), ())), preferred_element_type=F32)


def _dot_nt(a, b):
    return lax.dot_general(a, b, (((1,), (1,)), ((), ())), preferred_element_type=F32)


def _dot_tn(a, b):
    return lax.dot_general(a, b, (((0,), (0,)), ((), ())), preferred_element_type=F32)


def _rms_stats(h):
    r = lax.rsqrt(jnp.mean(h * h, axis=-1, keepdims=True) + EPS)
    return r, h * r


def _rms_bwd(dn, g, r, xh):
    gd = dn * g
    dh = r * (gd - xh * jnp.mean(gd * xh, axis=-1, keepdims=True))
    return dh, jnp.sum(dn * xh, axis=0, keepdims=True)


def _rope(x, c, s1, s2):
    return x * c + pltpu.roll(x, LANES - ROTARY_DIM // 2, 1) * s1 + pltpu.roll(x, ROTARY_DIM // 2, 1) * s2


def _rope_bwd(d, c, s1, s2):
    return d * c + pltpu.roll(d * s1, ROTARY_DIM // 2, 1) + pltpu.roll(d * s2, LANES - ROTARY_DIM // 2, 1)


def _sum_chunks(refs):
    terms = [ref[j].astype(F32) for ref in refs for j in range(ref.shape[0])]
    return functools.reduce(lambda a, b: a + b, terms)


def _chunk_rows(tile, k):
    return pl.BlockSpec((k, tile, D_MODEL), lambda i, *_: (0, i, 0))


def _full(shape):
    nd = len(shape)
    return pl.BlockSpec(shape, lambda *_: (0,) * nd)


def _rows(tile, cols):
    return pl.BlockSpec((tile, cols), lambda i, *_: (i, 0))


HBM_SPEC = pl.BlockSpec(memory_space=pltpu.HBM)

_Rider = collections.namedtuple("_Rider", "operands out_shapes aliases scratch start finish hooks", defaults=[()])


_NO_RIDER = _Rider([], [], {}, [], None, None)


def _call(body, name, grid, in_specs, out_specs, out_shape, scratch, vmem_mb, args, rider=None, prefetch=(),
          shares_rider_refs=False):
    rider = rider or _NO_RIDER
    n_pre, n_in, n_out, n_scr = len(prefetch), len(in_specs), len(out_specs), len(scratch)
    r_in, r_out = len(rider.operands), len(rider.out_shapes)

    def fused(*refs):
        pre, refs = refs[:n_pre], refs[n_pre:]
        ins, refs = refs[:n_in], refs[n_in:]
        r_ins, refs = refs[:r_in], refs[r_in:]
        outs, refs = refs[:n_out], refs[n_out:]
        r_outs, refs = refs[:r_out], refs[r_out:]
        scr, r_scr = refs[:n_scr], refs[n_scr:]
        ids = [pl.program_id(d) for d in range(len(grid))]
        if rider.start is not None:
            @pl.when(functools.reduce(jnp.logical_and, [i == 0 for i in ids]))
            def _():
                rider.start(r_ins, r_outs, r_scr)

        for at, hook in rider.hooks:
            @pl.when(functools.reduce(jnp.logical_and, [i == a for i, a in zip(ids, at)]))
            def _(hook=hook):
                hook(r_ins, r_outs, r_scr)

        if shares_rider_refs:
            body(*pre, *ins, *outs, *scr, rider_refs=r_outs)
        else:
            body(*pre, *ins, *outs, *scr)

        if rider.finish is not None:
            @pl.when(functools.reduce(jnp.logical_and, [i == g - 1 for i, g in zip(ids, grid)]))
            def _():
                rider.finish(r_ins, r_outs, r_scr)

    return pl.pallas_call(
        fused, name=name,
        grid_spec=pltpu.PrefetchScalarGridSpec(
            num_scalar_prefetch=n_pre, grid=grid,
            in_specs=list(in_specs) + [HBM_SPEC] * r_in, out_specs=list(out_specs) + [HBM_SPEC] * r_out,
            scratch_shapes=list(scratch) + list(rider.scratch)),
        out_shape=list(out_shape) + list(rider.out_shapes),
        input_output_aliases={n_pre + n_in + k: n_out + v for k, v in rider.aliases.items()},
        compiler_params=_params(("arbitrary",) * len(grid), vmem_mb),
    )(*prefetch, *args, *rider.operands)


def _comm_call(rider, name):
    r_in, r_out = len(rider.operands), len(rider.out_shapes)

    def body(*refs):
        r_ins, r_outs, r_scr = refs[:r_in], refs[r_in:r_in + r_out], refs[r_in + r_out:]
        rider.start(r_ins, r_outs, r_scr)
        rider.finish(r_ins, r_outs, r_scr)

    return pl.pallas_call(
        body, name=name, in_specs=[HBM_SPEC] * r_in, out_specs=[HBM_SPEC] * r_out, out_shape=list(rider.out_shapes),
        input_output_aliases=dict(rider.aliases), scratch_shapes=list(rider.scratch),
    )(*rider.operands)


def _ffn_fwd(order, h, gain, name, rider, group_at):
    S = h.shape[0]
    tile = min(TOK_TILE, S)
    nt = S // tile
    last = N_CHIPS - 1

    def body(order_ref, h_ref, g_ref, ho_ref, n_ref, gate_ref, up_ref, w_scr, w_sem, acc, n_scr, rider_refs):
        j, i = pl.program_id(0), pl.program_id(1)

        @pl.when(i == 0)
        def _():
            fetch = pltpu.make_async_copy(rider_refs[group_at].at[order_ref[j]], w_scr, w_sem)
            fetch.start()
            fetch.wait()

        at = pl.multiple_of(i * tile, tile)

        @pl.when(j == 0)
        def _():
            _, xh = _rms_stats(h_ref[...])
            n = (xh * g_ref[...]).astype(BF16)
            n_scr[pl.ds(at, tile), :] = n
            n_ref[...] = n
            acc[pl.ds(at, tile), :] = jnp.zeros((tile, D_MODEL), F32)

        half = tile // 2
        wg, wu, wd = (w_scr[part * FF_CHUNK:(part + 1) * FF_CHUNK, :] for part in range(3))
        ns = [n_scr[pl.ds(at + s * half, half), :] for s in range(2)]
        gates = [_dot_nt(n, wg) for n in ns]
        ups = [_dot_nt(n, wu) for n in ns]
        acts = [(g * jax.nn.sigmoid(g) * u).astype(BF16) for g, u in zip(gates, ups)]
        for s in range(2):
            gate_ref[0, s * half:(s + 1) * half, :] = gates[s].astype(BF16)
            up_ref[0, s * half:(s + 1) * half, :] = ups[s].astype(BF16)
            acc[pl.ds(at + s * half, half), :] += _dot(acts[s], wd)

        @pl.when(j == last)
        def _():
            ho_ref[...] = h_ref[...] + 0.5 * acc[pl.ds(at, tile), :]

    tok = pl.BlockSpec((tile, D_MODEL), lambda j, i, order_ref: (i, 0))
    hid = pl.BlockSpec((1, tile, FF_CHUNK), lambda j, i, order_ref: (order_ref[j], i, 0))
    return _call(
        body, name, (N_CHIPS, nt),
        [tok, pl.BlockSpec((1, D_MODEL), lambda j, i, order_ref: (0, 0))],
        [pl.BlockSpec((tile, D_MODEL), lambda j, i, order_ref: (jnp.where(j == last, i, 0), 0)),
         pl.BlockSpec((tile, D_MODEL), lambda j, i, order_ref: (jnp.where(j == 0, i, nt - 1), 0)),
         hid, hid],
        [jax.ShapeDtypeStruct((S, D_MODEL), F32), jax.ShapeDtypeStruct((S, D_MODEL), BF16),
         jax.ShapeDtypeStruct((N_CHIPS, S, FF_CHUNK), BF16), jax.ShapeDtypeStruct((N_CHIPS, S, FF_CHUNK), BF16)],
        [pltpu.VMEM((3 * FF_CHUNK, D_MODEL), BF16), pltpu.SemaphoreType.DMA, pltpu.VMEM((S, D_MODEL), F32),
         pltpu.VMEM((S, D_MODEL), BF16)], 58, (h, gain), rider, (order,), shares_rider_refs=True)


def _ffn_bwd(chunks, d_out, n, gate, up, group, name, rider=None):
    S = n.shape[0]
    n_chunks = chunks.shape[0]
    tile = min(TOK_TILE, S)
    nt = S // tile
    half_rows = 3 * FF_CHUNK // 2
    cut = FF_CHUNK // 2

    def body(chunks_ref, do_ref, n_ref, gate_ref, up_ref, wg_ref, wu_ref, wd_ref, dn_ref, dw_ref, acc_g, acc_u, acc_d):
        j, i = pl.program_id(0), pl.program_id(1)

        @pl.when(i == 0)
        def _():
            acc_g[...] = jnp.zeros_like(acc_g)
            acc_u[...] = jnp.zeros_like(acc_u)
            acc_d[...] = jnp.zeros_like(acc_d)

        halves = [pl.ds(s * (tile // 2), tile // 2) for s in range(2)]
        dos = [do_ref[rows, :] for rows in halves]
        d_acts = [_dot_nt(do, wd_ref[0]) for do in dos]
        gs = [gate_ref[0, rows, :].astype(F32) for rows in halves]
        us = [up_ref[0, rows, :].astype(F32) for rows in halves]
        sigs = [jax.nn.sigmoid(g) for g in gs]
        silus = [g * sig for g, sig in zip(gs, sigs)]
        d_ups = [(d_act * silu).astype(BF16) for d_act, silu in zip(d_acts, silus)]
        d_gates = [(d_act * u * (sig * (1.0 + g * (1.0 - sig)))).astype(BF16) for d_act, u, sig, g in zip(d_acts, us, sigs, gs)]
        for rows, d_gate, d_up in zip(halves, d_gates, d_ups):
            dn_ref[0, rows, :] = (_dot(d_gate, wg_ref[0]) + _dot(d_up, wu_ref[0])).astype(BF16)
        d_gate, d_up = jnp.concatenate(d_gates, axis=0), jnp.concatenate(d_ups, axis=0)
        act = jnp.concatenate([(silu * u).astype(BF16) for silu, u in zip(silus, us)], axis=0)
        nn = n_ref[...]
        acc_g[...] += _dot_tn(d_gate, nn)
        acc_u[...] += _dot_tn(d_up, nn)
        acc_d[...] += _dot_tn(act, do_ref[...])

        @pl.when(i == nt - 1)
        def _():
            dw_ref[0, 0, :FF_CHUNK, :] = acc_g[...].astype(BF16)
            dw_ref[0, 0, FF_CHUNK:, :] = acc_u[:cut, :].astype(BF16)
            dw_ref[1, 0, :cut, :] = acc_u[cut:, :].astype(BF16)
            dw_ref[1, 0, cut:, :] = acc_d[...].astype(BF16)

    tok = pl.BlockSpec((tile, D_MODEL), lambda j, i, chunks_ref: (i, 0))
    hid = pl.BlockSpec((1, tile, FF_CHUNK), lambda j, i, chunks_ref: (chunks_ref[j], i, 0))
    return _call(
        body, name, (n_chunks, nt),
        [tok, tok, hid, hid]
        + [pl.BlockSpec((1, FF_CHUNK, D_MODEL), functools.partial(lambda j, i, chunks_ref, part: (chunks_ref[j], part, 0), part=part))
           for part in range(3)],
        [pl.BlockSpec((1, tile, D_MODEL), lambda j, i, chunks_ref: (j, i, 0)),
         pl.BlockSpec((2, 1, half_rows, D_MODEL), lambda j, i, chunks_ref: (0, j, 0, 0))],
        [jax.ShapeDtypeStruct((n_chunks, S, D_MODEL), BF16), jax.ShapeDtypeStruct((2, n_chunks, half_rows, D_MODEL), BF16)],
        [pltpu.VMEM((FF_CHUNK, D_MODEL), F32)] * 3, 56, (d_out, n, gate, up, group, group, group), rider, (chunks,))


def _mix_in(h, gain, w_in, rc, rs1, rs2, name):
    S = h.shape[0]
    tile = min(TOK_TILE, S)

    def body(h_ref, g_ref, w_ref, c_ref, s1_ref, s2_ref, n_ref, q_ref, k_ref, v_ref, pc_ref):
        _, xh = _rms_stats(h_ref[...])
        n = (xh * g_ref[...]).astype(BF16)
        n_ref[...] = n
        u = _dot_nt(n, w_ref[...])
        c, s1, s2 = c_ref[...], s1_ref[...], s2_ref[...]
        q_ref[...] = jnp.concatenate([(_rope(u[:, hd * LANES:(hd + 1) * LANES], c, s1, s2) * SCALE).astype(BF16)
                                      for hd in range(N_HEADS)], axis=1)
        k_ref[...] = _rope(u[:, Q_PAD:Q_PAD + KV_WIDTH], c, s1, s2).astype(BF16)
        v_ref[...] = u[:, Q_PAD + KV_WIDTH:Q_PAD + 2 * KV_WIDTH].astype(BF16)
        pc_ref[...] = u[:, Q_PAD + 2 * KV_WIDTH:]

    return pl.pallas_call(
        body, name=name, grid=(S // tile,),
        in_specs=[_rows(tile, D_MODEL), _full((1, D_MODEL)), _full((U_PAD, D_MODEL)),
                  _rows(tile, LANES), _rows(tile, LANES), _rows(tile, LANES)],
        out_specs=[_rows(tile, D_MODEL), _rows(tile, Q_PAD), _rows(tile, KV_WIDTH), _rows(tile, KV_WIDTH),
                   _rows(tile, POOL_WIDTH)],
        out_shape=[jax.ShapeDtypeStruct((S, D_MODEL), BF16), jax.ShapeDtypeStruct((S, Q_PAD), BF16),
                   jax.ShapeDtypeStruct((S, KV_WIDTH), BF16), jax.ShapeDtypeStruct((S, KV_WIDTH), BF16),
                   jax.ShapeDtypeStruct((S, POOL_WIDTH), F32)],
        compiler_params=_params(("parallel",), 40),
    )(h, gain, w_in, rc, rs1, rs2)


def _band_start(i, S):
    return pl.multiple_of(jnp.clip((i - 1) * BLOCK, 0, S - BAND), BLOCK)


def _window_bias(off):
    r = lax.broadcasted_iota(jnp.int32, (BLOCK, 1), 0)
    c = lax.broadcasted_iota(jnp.int32, (1, BAND), 1)
    return jnp.where(jnp.abs(off + r - c) <= WINDOW, 0.0, NEG).astype(F32)


def _softmax_parts(qh, kb, bias, sink_h):
    s = _dot_nt(qh, kb) + bias
    m = jnp.maximum(jnp.max(s, axis=-1, keepdims=True), sink_h)
    p = jnp.exp(s - m)
    es = jnp.exp(sink_h - m)
    return p, es, 1.0 / (jnp.sum(p, axis=-1, keepdims=True) + es)


def _pool_matrix(t0, start, S, w):
    r = lax.broadcasted_iota(jnp.int32, (BLOCK, 1), 0) + t0
    c = lax.broadcasted_iota(jnp.int32, (1, BAND), 1) + start
    half = w // 2

    def window(lo, hi):
        a = jnp.maximum(lo, 0)
        b = jnp.minimum(hi + 1, S)
        return jnp.where((c >= a) & (c < b), 1.0 / (b - a).astype(F32), 0.0)

    return (0.5 * (window(r - half, r + half - 1) + window(r - half + 1, r + half))).astype(BF16)


def _pool_matrices(S):
    blocks = ((0, 0), (BLOCK, 0), (S - BLOCK, S - BAND))
    return jnp.stack([jnp.stack([_pool_matrix(t0, start, S, w) for w in POOL_WINDOWS]) for t0, start in blocks])


def _pool_spec(nb):
    return pl.BlockSpec((1, N_POOL, BLOCK, BAND), lambda i, *_: (jnp.where(i == 0, 0, jnp.where(i == nb - 1, 2, 1)), 0, 0, 0))


def _mix_core_fwd(q, k, v, pc, sink, pool_m, pool_w, pool_scale, name, rider=None):
    S = q.shape[0]
    nb = S // BLOCK

    def body(sink_ref, q_ref, k_ref, v_ref, pc_ref, pm_ref, pw_ref, ps_ref, a_ref, p_ref):
        i = pl.program_id(0)
        start = _band_start(i, S)
        band = pl.ds(start, BAND)
        bias = _window_bias(i * BLOCK - start)
        kb, vb = k_ref[band, :], v_ref[band, :]
        hs = range(N_HEADS)
        ss = [_dot_nt(q_ref[:, hd * LANES:(hd + 1) * LANES], kb) + bias for hd in hs]
        ms = [jnp.maximum(jnp.max(ss[hd], axis=-1, keepdims=True), sink_ref[0, hd]) for hd in hs]
        ps = [jnp.exp(ss[hd] - ms[hd]) for hd in hs]
        invs = [1.0 / (jnp.sum(ps[hd], axis=-1, keepdims=True) + jnp.exp(sink_ref[0, hd] - ms[hd])) for hd in hs]
        outs = [_dot(ps[hd].astype(BF16), vb) for hd in hs]
        a_ref[...] = jnp.concatenate([(outs[hd] * invs[hd]).astype(BF16) for hd in hs], axis=1)
        centre = pl.ds(pl.multiple_of(i * BLOCK, BLOCK), BLOCK)
        gs = range(N_POOL)
        sl = [slice(g * POOL_GROUP, (g + 1) * POOL_GROUP) for g in gs]
        means = [_dot(pm_ref[0, g], pc_ref[band, sl[g]].astype(BF16)) for g in gs]
        devs = [(means[g] - pc_ref[centre, sl[g]]).astype(BF16) for g in gs]
        p_ref[...] = (jnp.concatenate([_dot(devs[g], pw_ref[g]) for g in gs], axis=1) * ps_ref[...]).astype(BF16)

    return _call(
        body, name, (nb,),
        [pl.BlockSpec(memory_space=pltpu.SMEM), _rows(BLOCK, Q_PAD), _full((S, KV_WIDTH)), _full((S, KV_WIDTH)),
         _full((S, POOL_WIDTH)), _pool_spec(nb), _full((N_POOL, POOL_GROUP, POOL_GROUP)), _full((1, POOL_WIDTH))],
        [_rows(BLOCK, Q_PAD), _rows(BLOCK, POOL_WIDTH)],
        [jax.ShapeDtypeStruct((S, Q_PAD), BF16), jax.ShapeDtypeStruct((S, POOL_WIDTH), BF16)],
        [], 40, (sink, q, k, v, pc, pool_m, pool_w, pool_scale), rider)


def _mix_core_bwd(q, k, v, pc, da, dp, sink, pool_m, pool_w, pool_scale, rc, rs1, rs2, name, rider=None):
    S = q.shape[0]
    nb = S // BLOCK

    def body(sink_ref, q_ref, k_ref, v_ref, pc_ref, da_ref, dp_ref, pm_ref, pw_ref, ps_ref, c_ref, s1_ref, s2_ref,
             dq_ref, dk_ref, dv_ref, dpc_ref, dsink_ref, dpw_ref, dps_ref):
        i = pl.program_id(0)

        @pl.when(i == 0)
        def _():
            dk_ref[...] = jnp.zeros_like(dk_ref)
            dv_ref[...] = jnp.zeros_like(dv_ref)
            dpc_ref[...] = jnp.zeros_like(dpc_ref)
            dsink_ref[...] = jnp.zeros_like(dsink_ref)
            dpw_ref[...] = jnp.zeros_like(dpw_ref)
            dps_ref[...] = jnp.zeros_like(dps_ref)

        start = _band_start(i, S)
        band = pl.ds(start, BAND)
        bias = _window_bias(i * BLOCK - start)
        kb, vb = k_ref[band, :], v_ref[band, :]
        c, s1, s2 = c_ref[...], s1_ref[...], s2_ref[...]
        lane = lax.broadcasted_iota(jnp.int32, (1, LANES), 1)
        hs = range(N_HEADS)
        qs = [q_ref[:, hd * LANES:(hd + 1) * LANES] for hd in hs]
        das = [da_ref[:, hd * LANES:(hd + 1) * LANES] for hd in hs]
        ss = [_dot_nt(qs[hd], kb) + bias for hd in hs]
        d_probs = [_dot_nt(das[hd], vb) for hd in hs]
        ms = [jnp.maximum(jnp.max(ss[hd], axis=-1, keepdims=True), sink_ref[0, hd]) for hd in hs]
        ps = [jnp.exp(ss[hd] - ms[hd]) for hd in hs]
        ess = [jnp.exp(sink_ref[0, hd] - ms[hd]) for hd in hs]
        invs = [1.0 / (jnp.sum(ps[hd], axis=-1, keepdims=True) + ess[hd]) for hd in hs]
        probs = [ps[hd] * invs[hd] for hd in hs]
        deltas = [jnp.sum(probs[hd] * d_probs[hd], axis=-1, keepdims=True) for hd in hs]
        d_ss = [(probs[hd] * (d_probs[hd] - deltas[hd])).astype(BF16) for hd in hs]
        dqs = [_dot(d_ss[hd], kb) for hd in hs]
        dq_ref[...] = jnp.concatenate([_rope_bwd(dqs[hd] * SCALE, c, s1, s2).astype(BF16) for hd in hs], axis=1)
        dks = [_dot_tn(d_ss[hd], qs[hd]) for hd in hs]
        dvs = [_dot_tn(probs[hd].astype(BF16), das[hd]) for hd in hs]
        dk_ref[band, :] += functools.reduce(lambda a, b: a + b, dks)
        dv_ref[band, :] += functools.reduce(lambda a, b: a + b, dvs)
        dsink_ref[...] += functools.reduce(lambda a, b: a + b, [
            jnp.where(lane == hd, -jnp.sum(ess[hd] * invs[hd] * deltas[hd], axis=0, keepdims=True), 0.0) for hd in hs])

        centre = pl.ds(pl.multiple_of(i * BLOCK, BLOCK), BLOCK)
        gs = range(N_POOL)
        sl = [slice(g * POOL_GROUP, (g + 1) * POOL_GROUP) for g in gs]
        devs = [(_dot(pm_ref[0, g], pc_ref[band, sl[g]].astype(BF16)) - pc_ref[centre, sl[g]]).astype(BF16) for g in gs]
        dys = [dp_ref[:, sl[g]].astype(F32) for g in gs]
        zs = [_dot(devs[g], pw_ref[g]) for g in gs]
        dzs = [(dys[g] * ps_ref[:, sl[g]]).astype(BF16) for g in gs]
        d_devs = [_dot_nt(dzs[g], pw_ref[g]) for g in gs]
        dps_ref[...] += jnp.concatenate([jnp.sum(dys[g] * zs[g], axis=0, keepdims=True) for g in gs], axis=1)
        for g in gs:
            dpw_ref[g] += _dot_tn(devs[g], dzs[g])
        dpc_ref[band, :] += jnp.concatenate([_dot_tn(pm_ref[0, g], d_devs[g].astype(BF16)) for g in gs], axis=1)
        dpc_ref[centre, :] -= jnp.concatenate(d_devs, axis=1)

    return _call(
        body, name, (nb,),
        [pl.BlockSpec(memory_space=pltpu.SMEM), _rows(BLOCK, Q_PAD), _full((S, KV_WIDTH)), _full((S, KV_WIDTH)),
         _full((S, POOL_WIDTH)), _rows(BLOCK, Q_PAD), _rows(BLOCK, POOL_WIDTH), _pool_spec(nb),
         _full((N_POOL, POOL_GROUP, POOL_GROUP)), _full((1, POOL_WIDTH)),
         _rows(BLOCK, LANES), _rows(BLOCK, LANES), _rows(BLOCK, LANES)],
        [_rows(BLOCK, Q_PAD), _full((S, KV_WIDTH)), _full((S, KV_WIDTH)), _full((S, POOL_WIDTH)),
         _full((1, LANES)), _full((N_POOL, POOL_GROUP, POOL_GROUP)), _full((1, POOL_WIDTH))],
        [jax.ShapeDtypeStruct((S, Q_PAD), BF16), jax.ShapeDtypeStruct((S, KV_WIDTH), F32),
         jax.ShapeDtypeStruct((S, KV_WIDTH), F32), jax.ShapeDtypeStruct((S, POOL_WIDTH), F32),
         jax.ShapeDtypeStruct((1, LANES), F32), jax.ShapeDtypeStruct((N_POOL, POOL_GROUP, POOL_GROUP), F32),
         jax.ShapeDtypeStruct((1, POOL_WIDTH), F32)],
        [], 56, (sink, q, k, v, pc, da, dp, pool_m, pool_w, pool_scale, rc, rs1, rs2), rider)


def _mix_out(h, a, p, wa, wp, name):
    S = h.shape[0]
    tile = min(TOK_TILE, S)

    def body(h_ref, a_ref, p_ref, wa_ref, wp_ref, o_ref):
        o_ref[...] = h_ref[...] + _dot(a_ref[...], wa_ref[...]) + _dot(p_ref[...], wp_ref[...])

    return pl.pallas_call(
        body, name=name, grid=(S // tile,),
        in_specs=[_rows(tile, D_MODEL), _rows(tile, Q_PAD), _rows(tile, POOL_WIDTH),
                  _full((Q_PAD, D_MODEL)), _full((POOL_WIDTH, D_MODEL))],
        out_specs=_rows(tile, D_MODEL),
        out_shape=jax.ShapeDtypeStruct((S, D_MODEL), F32),
        compiler_params=_params(("parallel",), 40),
    )(h, a, p, wa, wp)


def _loss_head(h, target, gain, name):
    S = h.shape[0]
    tile = min(TOK_TILE, S)

    def body(h_ref, t_ref, g_ref, dh_ref, dhalf_ref, loss_ref, dg_ref):
        @pl.when(pl.program_id(0) == 0)
        def _():
            loss_ref[...] = jnp.zeros_like(loss_ref)
            dg_ref[...] = jnp.zeros_like(dg_ref)

        g = g_ref[...]
        r, xh = _rms_stats(h_ref[...])
        err = xh * g - t_ref[...]
        loss_ref[...] += (0.5 / D_MODEL) * jnp.sum(err * err, axis=0, keepdims=True)
        dh, dg = _rms_bwd(err * (1.0 / D_MODEL), g, r, xh)
        dg_ref[...] += dg
        dh_ref[...] = dh
        dhalf_ref[...] = (0.5 * dh).astype(BF16)

    return pl.pallas_call(
        body, name=name, grid=(S // tile,),
        in_specs=[_rows(tile, D_MODEL), _rows(tile, D_MODEL), _full((1, D_MODEL))],
        out_specs=[_rows(tile, D_MODEL), _rows(tile, D_MODEL), _full((1, D_MODEL)), _full((1, D_MODEL))],
        out_shape=[jax.ShapeDtypeStruct((S, D_MODEL), F32), jax.ShapeDtypeStruct((S, D_MODEL), BF16),
                   jax.ShapeDtypeStruct((1, D_MODEL), F32), jax.ShapeDtypeStruct((1, D_MODEL), F32)],
        compiler_params=_params(("arbitrary",), 40),
    )(h, target, gain)


def _mix_out_bwd(dh_out, dn, h, gain, a, p, wa, wp, name, rider=None):
    S = h.shape[0]
    tile = min(TOK_TILE, S)

    def body(do_ref, dn_ref, h_ref, g_ref, a_ref, p_ref, wa_ref, wp_ref, dh_ref, da_ref, dp_ref, dwa_ref, dwp_ref, dg_ref):
        @pl.when(pl.program_id(0) == 0)
        def _():
            dwa_ref[...] = jnp.zeros_like(dwa_ref)
            dwp_ref[...] = jnp.zeros_like(dwp_ref)
            dg_ref[...] = jnp.zeros_like(dg_ref)

        r, xh = _rms_stats(h_ref[...])
        dnorm, dg = _rms_bwd(_sum_chunks([dn_ref]), g_ref[...], r, xh)
        dh = do_ref[...] + dnorm
        dg_ref[...] += dg
        dh_ref[...] = dh
        dhb = dh.astype(BF16)
        da_ref[...] = _dot_nt(dhb, wa_ref[...]).astype(BF16)
        dp_ref[...] = _dot_nt(dhb, wp_ref[...]).astype(BF16)
        dwa_ref[...] += _dot_tn(a_ref[...], dhb)
        dwp_ref[...] += _dot_tn(p_ref[...], dhb)

    return _call(
        body, name, (S // tile,),
        [_rows(tile, D_MODEL), _chunk_rows(tile, dn.shape[0]), _rows(tile, D_MODEL), _full((1, D_MODEL)),
         _rows(tile, Q_PAD), _rows(tile, POOL_WIDTH), _full((Q_PAD, D_MODEL)), _full((POOL_WIDTH, D_MODEL))],
        [_rows(tile, D_MODEL), _rows(tile, Q_PAD), _rows(tile, POOL_WIDTH),
         _full((Q_PAD, D_MODEL)), _full((POOL_WIDTH, D_MODEL)), _full((1, D_MODEL))],
        [jax.ShapeDtypeStruct((S, D_MODEL), F32), jax.ShapeDtypeStruct((S, Q_PAD), BF16),
         jax.ShapeDtypeStruct((S, POOL_WIDTH), BF16), jax.ShapeDtypeStruct((Q_PAD, D_MODEL), F32),
         jax.ShapeDtypeStruct((POOL_WIDTH, D_MODEL), F32), jax.ShapeDtypeStruct((1, D_MODEL), F32)],
        [], 48, (dh_out, dn, h, gain, a, p, wa, wp), rider)


def _mix_in_bwd(dh_out, h, gain, n, dq, dk, dv, dpc, rc, rs1, rs2, w_in, name, rider=None):
    S = h.shape[0]
    tile = min(TOK_TILE, S)

    def body(do_ref, h_ref, g_ref, n_ref, dq_ref, dk_ref, dv_ref, dpc_ref, c_ref, s1_ref, s2_ref, w_ref,
             dh_ref, dhalf_ref, dw_ref, dg_ref):
        @pl.when(pl.program_id(0) == 0)
        def _():
            dw_ref[...] = jnp.zeros_like(dw_ref)
            dg_ref[...] = jnp.zeros_like(dg_ref)

        dk = _rope_bwd(dk_ref[...], c_ref[...], s1_ref[...], s2_ref[...]).astype(BF16)
        du = jnp.concatenate([dq_ref[...], dk, dv_ref[...].astype(BF16), dpc_ref[...].astype(BF16)], axis=1)
        dn = _dot(du, w_ref[...])
        dw_ref[...] += _dot_tn(du, n_ref[...])
        r, xh = _rms_stats(h_ref[...])
        dnorm, dg = _rms_bwd(dn, g_ref[...], r, xh)
        dh = do_ref[...] + dnorm
        dg_ref[...] += dg
        dh_ref[...] = dh
        dhalf_ref[...] = (0.5 * dh).astype(BF16)

    return _call(
        body, name, (S // tile,),
        [_rows(tile, D_MODEL), _rows(tile, D_MODEL), _full((1, D_MODEL)), _rows(tile, D_MODEL),
         _rows(tile, Q_PAD), _rows(tile, KV_WIDTH), _rows(tile, KV_WIDTH), _rows(tile, POOL_WIDTH),
         _rows(tile, LANES), _rows(tile, LANES), _rows(tile, LANES), _full((U_PAD, D_MODEL))],
        [_rows(tile, D_MODEL), _rows(tile, D_MODEL), _full((U_PAD, D_MODEL)), _full((1, D_MODEL))],
        [jax.ShapeDtypeStruct((S, D_MODEL), F32), jax.ShapeDtypeStruct((S, D_MODEL), BF16),
         jax.ShapeDtypeStruct((U_PAD, D_MODEL), F32), jax.ShapeDtypeStruct((1, D_MODEL), F32)],
        [], 56, (dh_out, h, gain, n, dq, dk, dv, dpc, rc, rs1, rs2, w_in), rider)


def _norm_bwd(dh_out, dns, h, gain, name, rider=None):
    S = h.shape[0]
    tile = min(TOK_TILE, S)
    n = len(dns)

    def body(do_ref, *refs):
        h_ref, g_ref, dh_ref, dg_ref = refs[n:]

        @pl.when(pl.program_id(0) == 0)
        def _():
            dg_ref[...] = jnp.zeros_like(dg_ref)

        r, xh = _rms_stats(h_ref[...])
        dnorm, dg = _rms_bwd(_sum_chunks(refs[:n]), g_ref[...], r, xh)
        dg_ref[...] += dg
        dh_ref[...] = do_ref[...] + dnorm

    return _call(
        body, name, (S // tile,),
        [_rows(tile, D_MODEL)] + [_chunk_rows(tile, dn.shape[0]) for dn in dns] + [_rows(tile, D_MODEL), _full((1, D_MODEL))],
        [_rows(tile, D_MODEL), _full((1, D_MODEL))],
        [jax.ShapeDtypeStruct((S, D_MODEL), F32), jax.ShapeDtypeStruct((1, D_MODEL), F32)],
        [], 40, (dh_out, *dns, h, gain), rider)


def _rope_tables(S):
    half = ROTARY_DIM // 2
    inv_freq = ROPE_THETA ** (-jnp.arange(0, ROTARY_DIM, 2, dtype=F32) / ROTARY_DIM)
    dim = jnp.arange(LANES) % HEAD_DIM
    ang = jnp.arange(S, dtype=F32)[:, None] * inv_freq[dim % half][None, :]
    lo, hi = (dim < half)[None, :], ((dim >= half) & (dim < ROTARY_DIM))[None, :]
    c = jnp.where(lo | hi, jnp.cos(ang), 1.0)
    s1 = jnp.where(lo, -jnp.sin(ang), 0.0)
    s2 = jnp.where(hi, jnp.sin(ang), 0.0)
    return c, s1, s2


def _pad_heads(w, axis):
    w = jnp.moveaxis(w, axis, 0)
    heads = w.reshape((N_HEADS, HEAD_DIM) + w.shape[1:])
    zero = jnp.zeros_like(heads)
    first = (jnp.arange(N_HEADS) < Q_PER_KV).reshape((N_HEADS, 1) + (1,) * (w.ndim - 1))
    lo = jnp.where(first, heads, zero)
    hi = jnp.where(first, zero, heads)
    padded = jnp.concatenate([lo, hi], axis=1).reshape((Q_PAD,) + w.shape[1:])
    return jnp.moveaxis(padded, 0, axis)


def _unpad_heads(w, axis):
    w = jnp.moveaxis(w, axis, 0)
    groups = w.reshape((N_HEADS, 2, HEAD_DIM) + w.shape[1:])
    first = (jnp.arange(N_HEADS) < Q_PER_KV).reshape((N_HEADS, 1) + (1,) * (w.ndim - 1))
    heads = jnp.where(first, groups[:, 0], groups[:, 1]).reshape((ATTN_WIDTH,) + w.shape[1:])
    return jnp.moveaxis(heads, 0, axis)


IN_ROWS = IN_WIDTH // N_CHIPS
OUT_ROWS = (ATTN_WIDTH + POOL_WIDTH) // N_CHIPS
MIX_ROWS = IN_ROWS + OUT_ROWS
FFN_ROWS = 3 * FF_CHUNK


def _step(x, target, bufs, small, place):
    S = x.shape[0]
    rc, rs1, rs2 = _rope_tables(S)
    mine = place[1]
    order = jnp.stack([mine, mine ^ 2, mine ^ 1, mine ^ 3])
    h1, n1, gate1, up1, ffn1, mix = _ffn_fwd(order, x, small["ffn1_norm"], "ffn1_fwd",
                                             _merge(_allgather(bufs[:1], in_passes=0), _allgather(bufs[1:2])), 0)
    w_in_t = mix[:, :IN_ROWS].reshape(IN_WIDTH, D_MODEL)
    w_in_pad = jnp.concatenate([_pad_heads(w_in_t[:ATTN_WIDTH], 0), w_in_t[ATTN_WIDTH:]], axis=0)
    w_out = mix[:, IN_ROWS:].reshape(ATTN_WIDTH + POOL_WIDTH, D_MODEL)
    wa = _pad_heads(w_out[:ATTN_WIDTH], 0)
    wp = w_out[ATTN_WIDTH:]
    pool_w = small["pool_w"].astype(BF16)

    n2, q, k, v, pc = _mix_in(h1, small["mix_norm"], w_in_pad, rc, rs1, rs2, "mix_in")
    pool_m = _pool_matrices(S)
    a, p = _mix_core_fwd(q, k, v, pc, small["sink_logits"], pool_m, pool_w, small["pool_scale"], "mix_core_fwd")
    h2 = _mix_out(h1, a, p, wa, wp, "mix_out")
    h3, n3, gate2, up2, ffn2 = _ffn_fwd(order, h2, small["ffn2_norm"], "ffn2_fwd", _allgather(bufs[2:], in_passes=0), 0)
    dh3, dhalf3, loss_lanes, d_final = _loss_head(h3, target, small["final_norm"], "loss_head")

    dn3, d_ffn2 = _ffn_bwd(jnp.arange(N_CHIPS, dtype=jnp.int32), dhalf3, n3, gate2, up2, ffn2, "ffn2_bwd")
    dh2, da, dp, dwa, dwp, d_ffn2_norm, received = _mix_out_bwd(dh3, dn3, h2, small["ffn2_norm"], a, p, wa, wp, "mix_out_bwd",
                                                                _sibling_exchange([d_ffn2]))
    pair = _pair_sum(place, d_ffn2, received, "grad_pair_sum_ffn2")
    dq, dk, dv, dpc, dsink, dpool_w, dpool_scale, stack = _mix_core_bwd(
        q, k, v, pc, da, dp, small["sink_logits"], pool_m, pool_w, small["pool_scale"], rc, rs1, rs2, "mix_core_bwd",
        _scatter([pair], peers=X_Y_DIAGONAL[:2]))
    dh1, dhalf1, dw_in_pad, d_mix_norm, stack = _mix_in_bwd(dh2, h1, small["mix_norm"], n2, dq, dk, dv, dpc, rc, rs1, rs2,
                                                            w_in_pad, "mix_in_bwd",
                                                            _scatter([pair], peers=X_Y_DIAGONAL[2:], stacks=[stack]))
    reduced_ffn2 = _chip_sum(place, pair, stack, 2, "grad_chip_sum_ffn2")
    dw_in_t = jnp.concatenate([_unpad_heads(dw_in_pad[:Q_PAD], 0), dw_in_pad[Q_PAD:]], axis=0)
    dw_out = jnp.concatenate([_unpad_heads(dwa, 0), dwp], axis=0)
    d_mix = jnp.concatenate([dw_in_t.reshape(N_CHIPS, IN_ROWS, D_MODEL), dw_out.reshape(N_CHIPS, OUT_ROWS, D_MODEL)], axis=1)
    d_mix = jnp.transpose(d_mix.reshape(N_CHIPS, 2, MIX_ROWS // 2, D_MODEL), (1, 0, 2, 3)).astype(BF16)
    small_g = {"ffn1_norm": jnp.zeros_like(d_mix_norm), "mix_norm": d_mix_norm, "ffn2_norm": d_ffn2_norm,
               "final_norm": d_final, "pool_scale": dpool_scale, "sink_logits": dsink[:, :N_HEADS], "pool_w": dpool_w}
    loss_row = jnp.sum(loss_lanes.reshape(D_MODEL // LANES, LANES), axis=0, keepdims=True)
    small_early = _pack_small(small_g, loss_row)

    chunk = [(place[1:] + 1 + p) % N_CHIPS for p in range(N_CHIPS)]
    ffn1_bwd = functools.partial(_ffn_bwd, d_out=dhalf1, n=n1, gate=gate1, up=up1, group=ffn1)
    stack = jnp.zeros((N_CHIPS, FFN_ROWS // 2, D_MODEL), BF16)
    dn_a, dw_a, recv_mix, small_all, g_ffn2 = ffn1_bwd(
        chunk[0], name="ffn1_bwd_0",
        rider=_merge(_merge(_sibling_exchange([d_mix]), _small_allgather(small_early)), _sibling_share([reduced_ffn2])))
    pair_mix = _pair_sum(place, d_mix, recv_mix, "grad_pair_sum_mix")
    dn_b, dw_b, recv_a, stack_mix = ffn1_bwd(chunk[1], name="ffn1_bwd_1",
                                             rider=_merge(_sibling_exchange([dw_a]), _scatter([pair_mix])))
    pair_a = _pair_sum(place, dw_a, recv_a, "grad_pair_sum_ffn1_0")
    reduced_mix = _chip_sum(place, pair_mix, stack_mix, 2, "grad_chip_sum_mix")
    dn_c, dw_c, recv_b, stack, g_mix = ffn1_bwd(
        chunk[2], name="ffn1_bwd_2",
        rider=_merge(_merge(_sibling_exchange([dw_b]), _scatter_step(pair_a, stack, 0)), _sibling_share([reduced_mix])))
    pair_b = _pair_sum(place, dw_b, recv_b, "grad_pair_sum_ffn1_1")
    dn_d, dw_d, recv_c, stack = ffn1_bwd(chunk[3], name="ffn1_bwd_3",
                                         rider=_merge(_sibling_exchange([dw_c]), _scatter_step(pair_b, stack, 1)))
    pair_c = _pair_sum(place, dw_c, recv_c, "grad_pair_sum_ffn1_2")
    grad_x, d_ffn1_norm, recv_d, stack = _norm_bwd(dh1, [dn_a, dn_b, dn_c, dn_d], x, small["ffn1_norm"], "norm1_bwd",
                                                   _merge(_sibling_exchange([dw_d]), _scatter_step(pair_c, stack, 2)))
    pair_d = _pair_sum(place, dw_d, recv_d, "grad_pair_sum_ffn1_3")
    reduced_ffn1 = _chip_sum(place, pair_d, stack, 2, "grad_chip_sum_ffn1")
    g_ffn1, gains = _comm_call(_merge(_sibling_share([reduced_ffn1]), _small_allgather(d_ffn1_norm.reshape(-1, LANES))),
                               "grad_share_tail")
    gain_sum = _sum_leading(gains, 1, "gain_grad_sum")
    small_sum = jnp.concatenate([gain_sum, _sum_leading(small_all, 1, "small_grad_sum")[gain_sum.shape[0]:]], axis=0)
    return jnp.sum(small_sum[SMALL_ROWS - 1]), grad_x, [g.reshape(-1, D_MODEL) for g in (g_ffn1, g_mix, g_ffn2)], small_sum


GROUPS =(("ffn1_w_gate", "ffn1_w_up", "ffn1_w_down"), ("w_in", "w_out"), ("ffn2_w_gate", "ffn2_w_up", "ffn2_w_down"))
TRANSPOSED = ("ffn1_w_gate", "ffn1_w_up", "w_in", "ffn2_w_gate", "ffn2_w_up")


def _place():
    x, y, c = lax.axis_index("x"), lax.axis_index("y"), lax.axis_index("c")
    chips = [(1 - x, y), (x, 1 - y), (1 - x, 1 - y)]
    return x, y, c, chips


def _remote(src, dst, send_sem, recv_sem, to):
    return pltpu.make_async_remote_copy(src_ref=src, dst_ref=dst, send_sem=send_sem, recv_sem=recv_sem,
                                        device_id=to, device_id_type=MESH)


def _pack(chip, members, name):
    rows = members[0].shape[0]
    n = len(members)

    def body(chip_ref, *refs):
        ins, out_ref, buf, sems = refs[:n], refs[n], refs[n + 1], refs[n + 2]
        copies = [pltpu.make_async_copy(ins[k], buf.at[k], sems.at[k]) for k in range(n)]
        for cp in copies:
            cp.start()
        for k in range(n):
            copies[k].wait()
            out_ref[0, k * rows:(k + 1) * rows, :] = buf[k].astype(BF16)

    return pl.pallas_call(
        body, name=name,
        grid_spec=pltpu.PrefetchScalarGridSpec(
            num_scalar_prefetch=1, grid=(1,),
            in_specs=[HBM_SPEC] * n,
            out_specs=pl.BlockSpec((1, n * rows, D_MODEL), lambda k, chip_ref: (chip_ref[0], 0, 0)),
            scratch_shapes=[pltpu.VMEM((n, rows, D_MODEL), F32), pltpu.SemaphoreType.DMA((n,))]),
        out_shape=jax.ShapeDtypeStruct((N_CHIPS, n * rows, D_MODEL), BF16),
        compiler_params=_params(("arbitrary",), 40),
    )(chip, *members)


def _same(arrays):
    return [jax.ShapeDtypeStruct(a.shape, a.dtype) for a in arrays]


X_Y_DIAGONAL = (0, 1, 2)


def _allgather(bufs, peers=X_Y_DIAGONAL, in_passes=None):
    n = len(bufs)

    def copy(kind, outs, send_sems, recv_sems, a, k):
        x, y, c, chips = _place()
        half = bufs[a].shape[1] // 2

        def rows(slot, core):
            return outs[a].at[slot, pl.ds(pl.multiple_of(core * half, 16), half)]

        me, slot = 2 * x + y, 2 * chips[k][0] + chips[k][1]
        over_ici = (send_sems.at[6 * a + k], recv_sems.at[6 * a + k])
        over_d2d = (send_sems.at[6 * a + 3 + k], recv_sems.at[6 * a + 3 + k])
        if kind == "first":
            return _remote(rows(me, c), rows(me, c), *over_ici, (*chips[k], c))
        if kind == "landed":
            return _remote(rows(me, c), rows(slot, c), *over_ici, (*chips[k], c))
        if kind == "passed":
            return _remote(rows(slot, c), rows(slot, c), *over_d2d, (x, y, 1 - c))
        return _remote(rows(me, c), rows(slot, 1 - c), *over_d2d, (x, y, 1 - c))

    def arrive(pairs):
        def hook(ins, outs, sems):
            for a, k in pairs:
                copy("landed", outs, *sems, a, k).wait_recv()
                copy("passed", outs, *sems, a, k).start()
            for a, k in pairs:
                copy("handed", outs, *sems, a, k).wait_recv()
        return hook

    everything = [(a, k) for a in range(n) for k in peers]
    early = [(a, k) for a, k in everything if a == in_passes]

    def start(ins, outs, sems):
        for a, k in everything:
            copy("first", outs, *sems, a, k).start()

    def finish(ins, outs, sems):
        arrive([pair for pair in everything if pair not in early])(ins, outs, sems)
        for a, k in everything:
            copy("first", outs, *sems, a, k).wait_send()
            copy("passed", outs, *sems, a, k).wait_send()

    hooks = tuple(((k + 1, 0), arrive([(a, k)])) for a, k in early)
    return _Rider(list(bufs), _same(bufs), {a: a for a in range(n)},
                  [pltpu.SemaphoreType.DMA((6 * n,)), pltpu.SemaphoreType.DMA((6 * n,))], start, finish, hooks)


def _sibling_exchange(parts):
    n = len(parts)

    def copies(ins, outs, send_sems, recv_sems):
        x, y, c, _ = _place()
        return [_remote(ins[a].at[1 - c], outs[a], send_sems.at[a], recv_sems.at[a], (x, y, 1 - c)) for a in range(n)]

    def start(ins, outs, sems):
        for cp in copies(ins, outs, *sems):
            cp.start()

    def finish(ins, outs, sems):
        for cp in copies(ins, outs, *sems):
            cp.wait_recv()
            cp.wait_send()

    return _Rider(list(parts), [jax.ShapeDtypeStruct(p.shape[1:], p.dtype) for p in parts], {},
                  [pltpu.SemaphoreType.DMA((n,)), pltpu.SemaphoreType.DMA((n,))], start, finish)


def _small_allgather(small):
    flips = [(fx, fy, fc) for fx in range(2) for fy in range(2) for fc in range(2)][1:]

    def copies(small_ref, gather_ref, send_sems, recv_sems, local_sem, started_only=False):
        x, y, c, _ = _place()
        me = 4 * x + 2 * y + c
        peers = [((1 - x) if fx else x, (1 - y) if fy else y, (1 - c) if fc else c) for fx, fy, fc in flips]
        own = pltpu.make_async_copy(small_ref, gather_ref.at[me], local_sem)
        sent = [_remote(small_ref, gather_ref.at[me], send_sems.at[k], recv_sems.at[k], peer) for k, peer in enumerate(peers)]
        if started_only:
            return own, sent
        landed = [_remote(small_ref, gather_ref.at[4 * px + 2 * py + pc], send_sems.at[k], recv_sems.at[k], (px, py, pc))
                  for k, (px, py, pc) in enumerate(peers)]
        return own, sent, landed

    def start(ins, outs, sems):
        own, sent = copies(ins[0], outs[0], *sems, started_only=True)
        own.start()
        for cp in sent:
            cp.start()

    def finish(ins, outs, sems):
        own, sent, landed = copies(ins[0], outs[0], *sems)
        for cp in landed:
            cp.wait_recv()
        for cp in sent:
            cp.wait_send()
        own.wait()

    return _Rider([small], [jax.ShapeDtypeStruct((2 * N_CHIPS,) + small.shape, small.dtype)], {},
                  [pltpu.SemaphoreType.DMA((7,)), pltpu.SemaphoreType.DMA((7,)), pltpu.SemaphoreType.DMA], start, finish)


def _merge(a, b):
    na, nao, nas = len(a.operands), len(a.out_shapes), len(a.scratch)

    def start(ins, outs, sems):
        a.start(ins[:na], outs[:nao], sems[:nas])
        b.start(ins[na:], outs[nao:], sems[nas:])

    def finish(ins, outs, sems):
        a.finish(ins[:na], outs[:nao], sems[:nas])
        b.finish(ins[na:], outs[nao:], sems[nas:])

    def of_a(fn):
        return lambda ins, outs, sems: fn(ins[:na], outs[:nao], sems[:nas])

    def of_b(fn):
        return lambda ins, outs, sems: fn(ins[na:], outs[nao:], sems[nas:])

    aliases = {**a.aliases, **{na + k: nao + v for k, v in b.aliases.items()}}
    hooks = tuple((at, of_a(fn)) for at, fn in a.hooks) + tuple((at, of_b(fn)) for at, fn in b.hooks)
    return _Rider(a.operands + b.operands, a.out_shapes + b.out_shapes, aliases, a.scratch + b.scratch, start, finish, hooks)


def _scatter_step(pair, stack, step):
    def copies(pair_ref, stack_ref, send_sem, recv_sem, started_only=False):
        x, y, c, _ = _place()
        me = 2 * x + y
        to = (me + 1 + step) % N_CHIPS
        frm = (me + N_CHIPS - 1 - step) % N_CHIPS
        sent = _remote(pair_ref.at[0], stack_ref.at[me], send_sem, recv_sem, (to // 2, to % 2, c))
        if started_only:
            return sent
        landed = _remote(pair_ref.at[0], stack_ref.at[frm], send_sem, recv_sem, (frm // 2, frm % 2, c))
        return sent, landed

    def start(ins, outs, sems):
        copies(ins[0], outs[0], *sems, started_only=True).start()

    def finish(ins, outs, sems):
        sent, landed = copies(ins[0], outs[0], *sems)
        landed.wait_recv()
        sent.wait_send()

    return _Rider([pair, stack], _same([stack]), {1: 0}, [pltpu.SemaphoreType.DMA, pltpu.SemaphoreType.DMA], start, finish)


def _scatter(sums, peers=X_Y_DIAGONAL, stacks=None):
    n = len(sums)

    def copies(ins, outs, send_sems, recv_sems, started_only=False):
        x, y, c, chips = _place()
        me = 2 * x + y
        slots = [2 * cx + cy for cx, cy in chips]
        sent = [_remote(ins[a].at[slots[k]], outs[a].at[me], send_sems.at[3 * a + k], recv_sems.at[3 * a + k], (*chips[k], c))
                for a in range(n) for k in peers]
        if started_only:
            return sent
        landed = [_remote(ins[a].at[slots[k]], outs[a].at[slots[k]], send_sems.at[3 * a + k], recv_sems.at[3 * a + k],
                          (*chips[k], c)) for a in range(n) for k in peers]
        return sent, landed

    def start(ins, outs, sems):
        for cp in copies(ins, outs, *sems, started_only=True):
            cp.start()

    def finish(ins, outs, sems):
        sent, landed = copies(ins, outs, *sems)
        for cp in landed:
            cp.wait_recv()
        for cp in sent:
            cp.wait_send()

    sems = [pltpu.SemaphoreType.DMA((3 * n,)), pltpu.SemaphoreType.DMA((3 * n,))]
    if stacks is None:
        return _Rider(list(sums), _same(sums), {}, sems, start, finish)
    return _Rider(list(sums) + list(stacks), _same(sums), {n + a: a for a in range(n)}, sems, start, finish)


def _sibling_share(bufs):
    n = len(bufs)

    def copies(outs, send_sems, recv_sems, started_only=False):
        x, y, c, _ = _place()
        sent = [_remote(outs[a].at[c], outs[a].at[c], send_sems.at[a], recv_sems.at[a], (x, y, 1 - c)) for a in range(n)]
        if started_only:
            return sent
        landed = [_remote(outs[a].at[c], outs[a].at[1 - c], send_sems.at[a], recv_sems.at[a], (x, y, 1 - c)) for a in range(n)]
        return sent, landed

    def start(ins, outs, sems):
        for cp in copies(outs, *sems, started_only=True):
            cp.start()

    def finish(ins, outs, sems):
        sent, landed = copies(outs, *sems)
        for cp in landed:
            cp.wait_recv()
        for cp in sent:
            cp.wait_send()

    return _Rider(list(bufs), _same(bufs), {a: a for a in range(n)},
                  [pltpu.SemaphoreType.DMA((n,)), pltpu.SemaphoreType.DMA((n,))], start, finish)


def _pair_sum(core, part, received, name):
    _, k, rh, cols = part.shape

    def body(core_ref, p_ref, r_ref, o_ref):
        o_ref[...] = (p_ref[0].astype(F32) + r_ref[...].astype(F32)).astype(BF16)

    return pl.pallas_call(
        body, name=name,
        grid_spec=pltpu.PrefetchScalarGridSpec(
            num_scalar_prefetch=1, grid=(k,),
            in_specs=[pl.BlockSpec((1, 1, rh, cols), lambda j, core_ref: (core_ref[0], j, 0, 0)),
                      pl.BlockSpec((1, rh, cols), lambda j, core_ref: (j, 0, 0))],
            out_specs=pl.BlockSpec((1, rh, cols), lambda j, core_ref: (j, 0, 0))),
        out_shape=jax.ShapeDtypeStruct((k, rh, cols), BF16),
        compiler_params=_params(("parallel",), 32),
    )(core, part, received)


def _sum_leading(stack, steps, name):
    k, rows, cols = stack.shape
    tile = rows // steps

    def body(s_ref, o_ref):
        total = s_ref[0].astype(F32)
        for d in range(1, k):
            total = total + s_ref[d].astype(F32)
        o_ref[...] = total

    return pl.pallas_call(
        body, name=name, grid=(steps,),
        in_specs=[pl.BlockSpec((k, tile, cols), lambda i: (0, i, 0))],
        out_specs=pl.BlockSpec((tile, cols), lambda i: (i, 0)),
        out_shape=jax.ShapeDtypeStruct((rows, cols), F32),
        compiler_params=_params(("parallel",), 32),
    )(stack)


def _chip_sum(place, own, stack, steps, name):
    k, rows, cols = stack.shape
    tile = rows // steps

    def body(place_ref, own_ref, *refs):
        chip = place_ref[1]
        total = None
        for d in range(k):
            term = jnp.where(chip == d, own_ref[0], refs[d][0]).astype(F32)
            total = term if total is None else total + term
        refs[k][0] = total

    def other(d):
        return lambda i, place_ref: (jnp.where(place_ref[1] == d, (d + 1) % k, d), i, 0)

    return pl.pallas_call(
        body, name=name,
        grid_spec=pltpu.PrefetchScalarGridSpec(
            num_scalar_prefetch=1, grid=(steps,),
            in_specs=[pl.BlockSpec((1, tile, cols), lambda i, place_ref: (place_ref[1] % own.shape[0], i, 0))]
            + [pl.BlockSpec((1, tile, cols), other(d)) for d in range(k)],
            out_specs=pl.BlockSpec((1, tile, cols), lambda i, place_ref: (place_ref[0], i, 0))),
        out_shape=jax.ShapeDtypeStruct((2, rows, cols), F32),
        compiler_params=_params(("arbitrary",), 32),
    )(place, own, *([stack] * k))


def _adamw(w, g, row0, m, v, tile, name):
    rows, cols = w.shape
    first = row0 // tile
    assert rows % tile == 0 and row0 % tile == 0
    bc1 = 1.0 - ADAM_B1 ** ADAM_STEP
    bc2 = 1.0 - ADAM_B2 ** ADAM_STEP

    def body(w_ref, g_ref, m_ref, v_ref, go_ref, d_ref, mo_ref, vo_ref):
        g = g_ref[...]
        m_new = ADAM_B1 * m_ref[...] + (1.0 - ADAM_B1) * g
        v_new = ADAM_B2 * v_ref[...] + (1.0 - ADAM_B2) * (g * g)
        go_ref[...] = g
        d_ref[...] = -ADAM_LR * ((m_new / bc1) / (jnp.sqrt(v_new / bc2) + ADAM_EPS) + ADAM_WD * w_ref[...])
        mo_ref[...] = m_new
        vo_ref[...] = v_new

    spec = pl.BlockSpec((tile, cols), lambda i: (i, 0))
    g_spec = pl.BlockSpec((tile, cols), lambda i: (first + i, 0))
    return pl.pallas_call(
        body, name=name, grid=(rows // tile,),
        in_specs=[spec, g_spec, spec, spec], out_specs=[spec] * 4,
        out_shape=[jax.ShapeDtypeStruct((rows, cols), F32)] * 4,
        compiler_params=_params(("parallel",), 32),
    )(w, g, m, v)


SMALL = ("ffn1_norm", "mix_norm", "ffn2_norm", "final_norm", "pool_scale", "sink_logits", "pool_w")


def _pack_small(d, last_row=None):
    sink = jnp.pad(d["sink_logits"].reshape(1, N_HEADS), ((0, 0), (0, LANES - N_HEADS)))
    rows = [d[n].reshape(-1, LANES) for n in SMALL[:5]] + [sink, d["pool_w"].reshape(-1, LANES)]
    used = sum(r.shape[0] for r in rows)
    last = jnp.zeros((1, LANES), F32) if last_row is None else last_row
    return jnp.concatenate(rows + [jnp.zeros((SMALL_ROWS - used - 1, LANES), F32), last], axis=0)


def _unpack_small(packed, like):
    out, row = {}, 0
    for n in SMALL:
        size = LANES if n == "sink_logits" else math.prod(like[n].shape)
        chunk = packed[row:row + size // LANES].reshape(-1)
        out[n] = (chunk[:N_HEADS] if n == "sink_logits" else chunk).reshape(like[n].shape)
        row += size // LANES
    return out


def kernel(x, ffn1_norm, ffn1_w_gate, ffn1_w_up, ffn1_w_down, mix_norm, w_in, sink_logits, pool_w, pool_scale, w_out, ffn2_norm, ffn2_w_gate, ffn2_w_up, ffn2_w_down, final_norm, loss_target, m_ffn1_norm, m_ffn1_w_gate, m_ffn1_w_up, m_ffn1_w_down, m_mix_norm, m_w_in, m_sink_logits, m_pool_w, m_pool_scale, m_w_out, m_ffn2_norm, m_ffn2_w_gate, m_ffn2_w_up, m_ffn2_w_down, m_final_norm, v_ffn1_norm, v_ffn1_w_gate, v_ffn1_w_up, v_ffn1_w_down, v_mix_norm, v_w_in, v_sink_logits, v_pool_w, v_pool_scale, v_w_out, v_ffn2_norm, v_ffn2_w_gate, v_ffn2_w_up, v_ffn2_w_down, v_final_norm):
    names = ("ffn1_norm", "ffn1_w_gate", "ffn1_w_up", "ffn1_w_down", "mix_norm", "w_in", "sink_logits", "pool_w",
             "pool_scale", "w_out", "ffn2_norm", "ffn2_w_gate", "ffn2_w_up", "ffn2_w_down", "final_norm")
    weights = dict(zip(names, (ffn1_norm, ffn1_w_gate, ffn1_w_up, ffn1_w_down, mix_norm, w_in, sink_logits, pool_w,
                               pool_scale, w_out, ffn2_norm, ffn2_w_gate, ffn2_w_up, ffn2_w_down, final_norm)))
    mom1 = dict(zip(names, (m_ffn1_norm, m_ffn1_w_gate, m_ffn1_w_up, m_ffn1_w_down, m_mix_norm, m_w_in, m_sink_logits,
                            m_pool_w, m_pool_scale, m_w_out, m_ffn2_norm, m_ffn2_w_gate, m_ffn2_w_up, m_ffn2_w_down,
                            m_final_norm)))
    mom2 = dict(zip(names, (v_ffn1_norm, v_ffn1_w_gate, v_ffn1_w_up, v_ffn1_w_down, v_mix_norm, v_w_in, v_sink_logits,
                            v_pool_w, v_pool_scale, v_w_out, v_ffn2_norm, v_ffn2_w_gate, v_ffn2_w_up, v_ffn2_w_down,
                            v_final_norm)))
    chip = (2 * lax.axis_index("x") + lax.axis_index("y")).astype(jnp.int32).reshape(1)
    place = jnp.concatenate([lax.axis_index("c").astype(jnp.int32).reshape(1), chip])

    def rows_of(t, n):
        return jnp.swapaxes(t[n][0], 0, 1) if n in TRANSPOSED else t[n][0]

    bufs = [_pack(chip, [rows_of(weights, n) for n in GROUPS[0]], "pack_ffn1"),
            _pack(chip, [jnp.concatenate([rows_of(weights, n) for n in GROUPS[1]], axis=0)], "pack_mix"),
            _pack(chip, [rows_of(weights, n) for n in GROUPS[2]], "pack_ffn2")]

    small_w = {"ffn1_norm": ffn1_norm, "mix_norm": mix_norm, "ffn2_norm": ffn2_norm,
               "final_norm": final_norm.reshape(1, D_MODEL), "pool_scale": pool_scale, "sink_logits": sink_logits,
               "pool_w": pool_w[0]}
    loss, grad_x, group_grads, small_sum = _step(x[0], loss_target[0], bufs, small_w, place)

    out_g, out_d, out_m, out_v = {}, {}, {}, {}
    for members, g in zip(GROUPS, group_grads):
        row0 = 0
        for n in members:
            w = rows_of(weights, n)
            tile = FF_CHUNK // 4 if w.shape[0] == FF_CHUNK else math.gcd(IN_ROWS, OUT_ROWS)
            outs = _adamw(w, g, row0, rows_of(mom1, n), rows_of(mom2, n), tile, "adamw_" + n)
            row0 += w.shape[0]
            for dst, t in zip((out_g, out_d, out_m, out_v), outs):
                dst[n] = (jnp.swapaxes(t, 0, 1) if n in TRANSPOSED else t).reshape(weights[n].shape)
    small_outs = _adamw(_pack_small(weights), small_sum, 0, _pack_small(mom1), _pack_small(mom2), SMALL_ROWS, "adamw_small")
    for dst, packed in zip((out_g, out_d, out_m, out_v), small_outs):
        dst.update(_unpack_small(packed, weights))

    return (loss,grad_x.reshape(x.shape), *[out_g[n] for n in names], *[out_d[n] for n in names],
            *[out_m[n] for n in names], *[out_v[n] for n in names])
```

```python
import collections
import functools
import math

import jax
import jax.numpy as jnp
from jax import lax
from jax.experimental import pallas as pl
from jax.experimental.pallas import tpu as pltpu

F32, BF16 = jnp.float32, jnp.bfloat16
MESH = pl.DeviceIdType.MESH

D_MODEL = 1024
D_FF = 2816
N_CHIPS = 4
FF_CHUNK = D_FF // N_CHIPS
HEAD_DIM = 64
N_HEADS = 8
N_KV = 2
Q_PER_KV = N_HEADS // N_KV
KV_WIDTH = N_KV * HEAD_DIM
ATTN_WIDTH = N_HEADS * HEAD_DIM
POOL_WINDOWS = (2, 4, 8, 16)
N_POOL = len(POOL_WINDOWS)
POOL_GROUP = 128
POOL_WIDTH = N_POOL * POOL_GROUP
IN_WIDTH = ATTN_WIDTH + 2 * KV_WIDTH + POOL_WIDTH
WINDOW = 128
BLOCK = 128
BAND = 3 * BLOCK
ROPE_THETA = 500000.0
ROTARY_DIM = HEAD_DIM // 4
EPS = 1e-6
LANES = 128
Q_PAD = N_HEADS * LANES
U_PAD = Q_PAD + 2 * KV_WIDTH + POOL_WIDTH
SCALE = HEAD_DIM ** -0.5
NEG = -1e30

ADAM_LR, ADAM_B1, ADAM_B2, ADAM_EPS, ADAM_WD, ADAM_STEP = 0.001, 0.9, 0.999, 1e-08, 0.01, 10

V7X_VMEM_BYTES = 64 * 1024 * 1024
TOK_TILE = 512
SMALL_ROWS = 552


def _params(sem, vmem_mb):
    assert vmem_mb * 1024 * 1024 <= V7X_VMEM_BYTES
    return pltpu.CompilerParams(dimension_semantics=sem, vmem_limit_bytes=vmem_mb * 1024 * 1024)


def _dot(a, b):
    return lax.dot_general(a, b, (((1,), (0,)), ((), ())), preferred_element_type=F32)


def _dot_nt(a, b):
    return lax.dot_general(a, b, (((1,), (1,)), ((), ())), preferred_element_type=F32)


def _dot_tn(a, b):
    return lax.dot_general(a, b, (((0,), (0,)), ((), ())), preferred_element_type=F32)


def _rms_stats(h):
    r = lax.rsqrt(jnp.mean(h * h, axis=-1, keepdims=True) + EPS)
    return r, h * r


def _rms_bwd(dn, g, r, xh):
    gd = dn * g
    dh = r * (gd - xh * jnp.mean(gd * xh, axis=-1, keepdims=True))
    return dh, jnp.sum(dn * xh, axis=0, keepdims=True)


def _rope(x, c, s1, s2):
    return x * c + pltpu.roll(x, LANES - ROTARY_DIM // 2, 1) * s1 + pltpu.roll(x, ROTARY_DIM // 2, 1) * s2


def _rope_bwd(d, c, s1, s2):
    return d * c + pltpu.roll(d * s1, ROTARY_DIM // 2, 1) + pltpu.roll(d * s2, LANES - ROTARY_DIM // 2, 1)


def _sum_chunks(refs):
    terms = [ref[j].astype(F32) for ref in refs for j in range(ref.shape[0])]
    return functools.reduce(lambda a, b: a + b, terms)


def _chunk_rows(tile, k):
    return pl.BlockSpec((k, tile, D_MODEL), lambda i, *_: (0, i, 0))


def _full(shape):
    nd = len(shape)
    return pl.BlockSpec(shape, lambda *_: (0,) * nd)


def _rows(tile, cols):
    return pl.BlockSpec((tile, cols), lambda i, *_: (i, 0))


HBM_SPEC = pl.BlockSpec(memory_space=pltpu.HBM)

_Rider = collections.namedtuple("_Rider", "operands out_shapes aliases scratch start finish hooks", defaults=[()])


_NO_RIDER = _Rider([], [], {}, [], None, None)


def _call(body, name, grid, in_specs, out_specs, out_shape, scratch, vmem_mb, args, rider=None, prefetch=(),
          shares_rider_refs=False):
    rider = rider or _NO_RIDER
    n_pre, n_in, n_out, n_scr = len(prefetch), len(in_specs), len(out_specs), len(scratch)
    r_in, r_out = len(rider.operands), len(rider.out_shapes)

    def fused(*refs):
        pre, refs = refs[:n_pre], refs[n_pre:]
        ins, refs = refs[:n_in], refs[n_in:]
        r_ins, refs = refs[:r_in], refs[r_in:]
        outs, refs = refs[:n_out], refs[n_out:]
        r_outs, refs = refs[:r_out], refs[r_out:]
        scr, r_scr = refs[:n_scr], refs[n_scr:]
        ids = [pl.program_id(d) for d in range(len(grid))]
        if rider.start is not None:
            @pl.when(functools.reduce(jnp.logical_and, [i == 0 for i in ids]))
            def _():
                rider.start(r_ins, r_outs, r_scr)

        for at, hook in rider.hooks:
            @pl.when(functools.reduce(jnp.logical_and, [i == a for i, a in zip(ids, at)]))
            def _(hook=hook):
                hook(r_ins, r_outs, r_scr)

        if shares_rider_refs:
            body(*pre, *ins, *outs, *scr, rider_refs=r_outs)
        else:
            body(*pre, *ins, *outs, *scr)

        if rider.finish is not None:
            @pl.when(functools.reduce(jnp.logical_and, [i == g - 1 for i, g in zip(ids, grid)]))
            def _():
                rider.finish(r_ins, r_outs, r_scr)

    return pl.pallas_call(
        fused, name=name,
        grid_spec=pltpu.PrefetchScalarGridSpec(
            num_scalar_prefetch=n_pre, grid=grid,
            in_specs=list(in_specs) + [HBM_SPEC] * r_in, out_specs=list(out_specs) + [HBM_SPEC] * r_out,
            scratch_shapes=list(scratch) + list(rider.scratch)),
        out_shape=list(out_shape) + list(rider.out_shapes),
        input_output_aliases={n_pre + n_in + k: n_out + v for k, v in rider.aliases.items()},
        compiler_params=_params(("arbitrary",) * len(grid), vmem_mb),
    )(*prefetch, *args, *rider.operands)


def _comm_call(rider, name):
    r_in, r_out = len(rider.operands), len(rider.out_shapes)

    def body(*refs):
        r_ins, r_outs, r_scr = refs[:r_in], refs[r_in:r_in + r_out], refs[r_in + r_out:]
        rider.start(r_ins, r_outs, r_scr)
        rider.finish(r_ins, r_outs, r_scr)

    return pl.pallas_call(
        body, name=name, in_specs=[HBM_SPEC] * r_in, out_specs=[HBM_SPEC] * r_out, out_shape=list(rider.out_shapes),
        input_output_aliases=dict(rider.aliases), scratch_shapes=list(rider.scratch),
    )(*rider.operands)


def _ffn_fwd(order, h, gain, name, rider, group_at):
    S = h.shape[0]
    tile = min(TOK_TILE, S)
    nt = S // tile
    last = N_CHIPS - 1

    def body(order_ref, h_ref, g_ref, ho_ref, n_ref, gate_ref, up_ref, w_scr, w_sem, acc, n_scr, rider_refs):
        j, i = pl.program_id(0), pl.program_id(1)

        @pl.when(i == 0)
        def _():
            fetch = pltpu.make_async_copy(rider_refs[group_at].at[order_ref[j]], w_scr, w_sem)
            fetch.start()
            fetch.wait()

        at = pl.multiple_of(i * tile, tile)

        @pl.when(j == 0)
        def _():
            _, xh = _rms_stats(h_ref[...])
            n = (xh * g_ref[...]).astype(BF16)
            n_scr[pl.ds(at, tile), :] = n
            n_ref[...] = n
            acc[pl.ds(at, tile), :] = jnp.zeros((tile, D_MODEL), F32)

        half = tile // 2
        wg, wu, wd = (w_scr[part * FF_CHUNK:(part + 1) * FF_CHUNK, :] for part in range(3))
        ns = [n_scr[pl.ds(at + s * half, half), :] for s in range(2)]
        gates = [_dot_nt(n, wg) for n in ns]
        ups = [_dot_nt(n, wu) for n in ns]
        acts = [(g * jax.nn.sigmoid(g) * u).astype(BF16) for g, u in zip(gates, ups)]
        for s in range(2):
            gate_ref[0, s * half:(s + 1) * half, :] = gates[s].astype(BF16)
            up_ref[0, s * half:(s + 1) * half, :] = ups[s].astype(BF16)
            acc[pl.ds(at + s * half, half), :] += _dot(acts[s], wd)

        @pl.when(j == last)
        def _():
            ho_ref[...] = h_ref[...] + 0.5 * acc[pl.ds(at, tile), :]

    tok = pl.BlockSpec((tile, D_MODEL), lambda j, i, order_ref: (i, 0))
    hid = pl.BlockSpec((1, tile, FF_CHUNK), lambda j, i, order_ref: (order_ref[j], i, 0))
    return _call(
        body, name, (N_CHIPS, nt),
        [tok, pl.BlockSpec((1, D_MODEL), lambda j, i, order_ref: (0, 0))],
        [pl.BlockSpec((tile, D_MODEL), lambda j, i, order_ref: (jnp.where(j == last, i, 0), 0)),
         pl.BlockSpec((tile, D_MODEL), lambda j, i, order_ref: (jnp.where(j == 0, i, nt - 1), 0)),
         hid, hid],
        [jax.ShapeDtypeStruct((S, D_MODEL), F32), jax.ShapeDtypeStruct((S, D_MODEL), BF16),
         jax.ShapeDtypeStruct((N_CHIPS, S, FF_CHUNK), BF16), jax.ShapeDtypeStruct((N_CHIPS, S, FF_CHUNK), BF16)],
        [pltpu.VMEM((3 * FF_CHUNK, D_MODEL), BF16), pltpu.SemaphoreType.DMA, pltpu.VMEM((S, D_MODEL), F32),
         pltpu.VMEM((S, D_MODEL), BF16)], 58, (h, gain), rider, (order,), shares_rider_refs=True)


def _ffn_bwd(chunks, d_out, n, gate, up, group, name, rider=None):
    S = n.shape[0]
    n_chunks = chunks.shape[0]
    tile = min(TOK_TILE, S)
    nt = S // tile
    half_rows = 3 * FF_CHUNK // 2
    cut = FF_CHUNK // 2

    def body(chunks_ref, do_ref, n_ref, gate_ref, up_ref, wg_ref, wu_ref, wd_ref, dn_ref, dw_ref, acc_g, acc_u, acc_d):
        j, i = pl.program_id(0), pl.program_id(1)

        @pl.when(i == 0)
        def _():
            acc_g[...] = jnp.zeros_like(acc_g)
            acc_u[...] = jnp.zeros_like(acc_u)
            acc_d[...] = jnp.zeros_like(acc_d)

        halves = [pl.ds(s * (tile // 2), tile // 2) for s in range(2)]
        dos = [do_ref[rows, :] for rows in halves]
        d_acts = [_dot_nt(do, wd_ref[0]) for do in dos]
        gs = [gate_ref[0, rows, :].astype(F32) for rows in halves]
        us = [up_ref[0, rows, :].astype(F32) for rows in halves]
        sigs = [jax.nn.sigmoid(g) for g in gs]
        silus = [g * sig for g, sig in zip(gs, sigs)]
        d_ups = [(d_act * silu).astype(BF16) for d_act, silu in zip(d_acts, silus)]
        d_gates = [(d_act * u * (sig * (1.0 + g * (1.0 - sig)))).astype(BF16) for d_act, u, sig, g in zip(d_acts, us, sigs, gs)]
        for rows, d_gate, d_up in zip(halves, d_gates, d_ups):
            dn_ref[0, rows, :] = (_dot(d_gate, wg_ref[0]) + _dot(d_up, wu_ref[0])).astype(BF16)
        d_gate, d_up = jnp.concatenate(d_gates, axis=0), jnp.concatenate(d_ups, axis=0)
        act = jnp.concatenate([(silu * u).astype(BF16) for silu, u in zip(silus, us)], axis=0)
        nn = n_ref[...]
        acc_g[...] += _dot_tn(d_gate, nn)
        acc_u[...] += _dot_tn(d_up, nn)
        acc_d[...] += _dot_tn(act, do_ref[...])

        @pl.when(i == nt - 1)
        def _():
            dw_ref[0, 0, :FF_CHUNK, :] = acc_g[...].astype(BF16)
            dw_ref[0, 0, FF_CHUNK:, :] = acc_u[:cut, :].astype(BF16)
            dw_ref[1, 0, :cut, :] = acc_u[cut:, :].astype(BF16)
            dw_ref[1, 0, cut:, :] = acc_d[...].astype(BF16)

    tok = pl.BlockSpec((tile, D_MODEL), lambda j, i, chunks_ref: (i, 0))
    hid = pl.BlockSpec((1, tile, FF_CHUNK), lambda j, i, chunks_ref: (chunks_ref[j], i, 0))
    return _call(
        body, name, (n_chunks, nt),
        [tok, tok, hid, hid]
        + [pl.BlockSpec((1, FF_CHUNK, D_MODEL), functools.partial(lambda j, i, chunks_ref, part: (chunks_ref[j], part, 0), part=part))
           for part in range(3)],
        [pl.BlockSpec((1, tile, D_MODEL), lambda j, i, chunks_ref: (j, i, 0)),
         pl.BlockSpec((2, 1, half_rows, D_MODEL), lambda j, i, chunks_ref: (0, j, 0, 0))],
        [jax.ShapeDtypeStruct((n_chunks, S, D_MODEL), BF16), jax.ShapeDtypeStruct((2, n_chunks, half_rows, D_MODEL), BF16)],
        [pltpu.VMEM((FF_CHUNK, D_MODEL), F32)] * 3, 56, (d_out, n, gate, up, group, group, group), rider, (chunks,))


def _mix_in(h, gain, w_in, rc, rs1, rs2, name):
    S = h.shape[0]
    tile = min(TOK_TILE, S)

    def body(h_ref, g_ref, w_ref, c_ref, s1_ref, s2_ref, n_ref, q_ref, k_ref, v_ref, pc_ref):
        _, xh = _rms_stats(h_ref[...])
        n = (xh * g_ref[...]).astype(BF16)
        n_ref[...] = n
        u = _dot_nt(n, w_ref[...])
        c, s1, s2 = c_ref[...], s1_ref[...], s2_ref[...]
        q_ref[...] = jnp.concatenate([(_rope(u[:, hd * LANES:(hd + 1) * LANES], c, s1, s2) * SCALE).astype(BF16)
                                      for hd in range(N_HEADS)], axis=1)
        k_ref[...] = _rope(u[:, Q_PAD:Q_PAD + KV_WIDTH], c, s1, s2).astype(BF16)
        v_ref[...] = u[:, Q_PAD + KV_WIDTH:Q_PAD + 2 * KV_WIDTH].astype(BF16)
        pc_ref[...] = u[:, Q_PAD + 2 * KV_WIDTH:]

    return pl.pallas_call(
        body, name=name, grid=(S // tile,),
        in_specs=[_rows(tile, D_MODEL), _full((1, D_MODEL)), _full((U_PAD, D_MODEL)),
                  _rows(tile, LANES), _rows(tile, LANES), _rows(tile, LANES)],
        out_specs=[_rows(tile, D_MODEL), _rows(tile, Q_PAD), _rows(tile, KV_WIDTH), _rows(tile, KV_WIDTH),
                   _rows(tile, POOL_WIDTH)],
        out_shape=[jax.ShapeDtypeStruct((S, D_MODEL), BF16), jax.ShapeDtypeStruct((S, Q_PAD), BF16),
                   jax.ShapeDtypeStruct((S, KV_WIDTH), BF16), jax.ShapeDtypeStruct((S, KV_WIDTH), BF16),
                   jax.ShapeDtypeStruct((S, POOL_WIDTH), F32)],
        compiler_params=_params(("parallel",), 40),
    )(h, gain, w_in, rc, rs1, rs2)


def _band_start(i, S):
    return pl.multiple_of(jnp.clip((i - 1) * BLOCK, 0, S - BAND), BLOCK)


def _window_bias(off):
    r = lax.broadcasted_iota(jnp.int32, (BLOCK, 1), 0)
    c = lax.broadcasted_iota(jnp.int32, (1, BAND), 1)
    return jnp.where(jnp.abs(off + r - c) <= WINDOW, 0.0, NEG).astype(F32)


def _softmax_parts(qh, kb, bias, sink_h):
    s = _dot_nt(qh, kb) + bias
    m = jnp.maximum(jnp.max(s, axis=-1, keepdims=True), sink_h)
    p = jnp.exp(s - m)
    es = jnp.exp(sink_h - m)
    return p, es, 1.0 / (jnp.sum(p, axis=-1, keepdims=True) + es)


def _pool_matrix(t0, start, S, w):
    r = lax.broadcasted_iota(jnp.int32, (BLOCK, 1), 0) + t0
    c = lax.broadcasted_iota(jnp.int32, (1, BAND), 1) + start
    half = w // 2

    def window(lo, hi):
        a = jnp.maximum(lo, 0)
        b = jnp.minimum(hi + 1, S)
        return jnp.where((c >= a) & (c < b), 1.0 / (b - a).astype(F32), 0.0)

    return (0.5 * (window(r - half, r + half - 1) + window(r - half + 1, r + half))).astype(BF16)


def _pool_matrices(S):
    blocks = ((0, 0), (BLOCK, 0), (S - BLOCK, S - BAND))
    return jnp.stack([jnp.stack([_pool_matrix(t0, start, S, w) for w in POOL_WINDOWS]) for t0, start in blocks])


def _pool_spec(nb):
    return pl.BlockSpec((1, N_POOL, BLOCK, BAND), lambda i, *_: (jnp.where(i == 0, 0, jnp.where(i == nb - 1, 2, 1)), 0, 0, 0))


def _mix_core_fwd(q, k, v, pc, sink, pool_m, pool_w, pool_scale, name, rider=None):
    S = q.shape[0]
    nb = S // BLOCK

    def body(sink_ref, q_ref, k_ref, v_ref, pc_ref, pm_ref, pw_ref, ps_ref, a_ref, p_ref):
        i = pl.program_id(0)
        start = _band_start(i, S)
        band = pl.ds(start, BAND)
        bias = _window_bias(i * BLOCK - start)
        kb, vb = k_ref[band, :], v_ref[band, :]
        hs = range(N_HEADS)
        ss = [_dot_nt(q_ref[:, hd * LANES:(hd + 1) * LANES], kb) + bias for hd in hs]
        ms = [jnp.maximum(jnp.max(ss[hd], axis=-1, keepdims=True), sink_ref[0, hd]) for hd in hs]
        ps = [jnp.exp(ss[hd] - ms[hd]) for hd in hs]
        invs = [1.0 / (jnp.sum(ps[hd], axis=-1, keepdims=True) + jnp.exp(sink_ref[0, hd] - ms[hd])) for hd in hs]
        outs = [_dot(ps[hd].astype(BF16), vb) for hd in hs]
        a_ref[...] = jnp.concatenate([(outs[hd] * invs[hd]).astype(BF16) for hd in hs], axis=1)
        centre = pl.ds(pl.multiple_of(i * BLOCK, BLOCK), BLOCK)
        gs = range(N_POOL)
        sl = [slice(g * POOL_GROUP, (g + 1) * POOL_GROUP) for g in gs]
        means = [_dot(pm_ref[0, g], pc_ref[band, sl[g]].astype(BF16)) for g in gs]
        devs = [(means[g] - pc_ref[centre, sl[g]]).astype(BF16) for g in gs]
        p_ref[...] = (jnp.concatenate([_dot(devs[g], pw_ref[g]) for g in gs], axis=1) * ps_ref[...]).astype(BF16)

    return _call(
        body, name, (nb,),
        [pl.BlockSpec(memory_space=pltpu.SMEM), _rows(BLOCK, Q_PAD), _full((S, KV_WIDTH)), _full((S, KV_WIDTH)),
         _full((S, POOL_WIDTH)), _pool_spec(nb), _full((N_POOL, POOL_GROUP, POOL_GROUP)), _full((1, POOL_WIDTH))],
        [_rows(BLOCK, Q_PAD), _rows(BLOCK, POOL_WIDTH)],
        [jax.ShapeDtypeStruct((S, Q_PAD), BF16), jax.ShapeDtypeStruct((S, POOL_WIDTH), BF16)],
        [], 40, (sink, q, k, v, pc, pool_m, pool_w, pool_scale), rider)


def _mix_core_bwd(q, k, v, pc, da, dp, sink, pool_m, pool_w, pool_scale, rc, rs1, rs2, name, rider=None):
    S = q.shape[0]
    nb = S // BLOCK

    def body(sink_ref, q_ref, k_ref, v_ref, pc_ref, da_ref, dp_ref, pm_ref, pw_ref, ps_ref, c_ref, s1_ref, s2_ref,
             dq_ref, dk_ref, dv_ref, dpc_ref, dsink_ref, dpw_ref, dps_ref):
        i = pl.program_id(0)

        @pl.when(i == 0)
        def _():
            dk_ref[...] = jnp.zeros_like(dk_ref)
            dv_ref[...] = jnp.zeros_like(dv_ref)
            dpc_ref[...] = jnp.zeros_like(dpc_ref)
            dsink_ref[...] = jnp.zeros_like(dsink_ref)
            dpw_ref[...] = jnp.zeros_like(dpw_ref)
            dps_ref[...] = jnp.zeros_like(dps_ref)

        start = _band_start(i, S)
        band = pl.ds(start, BAND)
        bias = _window_bias(i * BLOCK - start)
        kb, vb = k_ref[band, :], v_ref[band, :]
        c, s1, s2 = c_ref[...], s1_ref[...], s2_ref[...]
        lane = lax.broadcasted_iota(jnp.int32, (1, LANES), 1)
        hs = range(N_HEADS)
        qs = [q_ref[:, hd * LANES:(hd + 1) * LANES] for hd in hs]
        das = [da_ref[:, hd * LANES:(hd + 1) * LANES] for hd in hs]
        ss = [_dot_nt(qs[hd], kb) + bias for hd in hs]
        d_probs = [_dot_nt(das[hd], vb) for hd in hs]
        ms = [jnp.maximum(jnp.max(ss[hd], axis=-1, keepdims=True), sink_ref[0, hd]) for hd in hs]
        ps = [jnp.exp(ss[hd] - ms[hd]) for hd in hs]
        ess = [jnp.exp(sink_ref[0, hd] - ms[hd]) for hd in hs]
        invs = [1.0 / (jnp.sum(ps[hd], axis=-1, keepdims=True) + ess[hd]) for hd in hs]
        probs = [ps[hd] * invs[hd] for hd in hs]
        deltas = [jnp.sum(probs[hd] * d_probs[hd], axis=-1, keepdims=True) for hd in hs]
        d_ss = [(probs[hd] * (d_probs[hd] - deltas[hd])).astype(BF16) for hd in hs]
        dqs = [_dot(d_ss[hd], kb) for hd in hs]
        dq_ref[...] = jnp.concatenate([_rope_bwd(dqs[hd] * SCALE, c, s1, s2).astype(BF16) for hd in hs], axis=1)
        dks = [_dot_tn(d_ss[hd], qs[hd]) for hd in hs]
        dvs = [_dot_tn(probs[hd].astype(BF16), das[hd]) for hd in hs]
        dk_ref[band, :] += functools.reduce(lambda a, b: a + b, dks)
        dv_ref[band, :] += functools.reduce(lambda a, b: a + b, dvs)
        dsink_ref[...] += functools.reduce(lambda a, b: a + b, [
            jnp.where(lane == hd, -jnp.sum(ess[hd] * invs[hd] * deltas[hd], axis=0, keepdims=True), 0.0) for hd in hs])

        centre = pl.ds(pl.multiple_of(i * BLOCK, BLOCK), BLOCK)
        gs = range(N_POOL)
        sl = [slice(g * POOL_GROUP, (g + 1) * POOL_GROUP) for g in gs]
        devs = [(_dot(pm_ref[0, g], pc_ref[band, sl[g]].astype(BF16)) - pc_ref[centre, sl[g]]).astype(BF16) for g in gs]
        dys = [dp_ref[:, sl[g]].astype(F32) for g in gs]
        zs = [_dot(devs[g], pw_ref[g]) for g in gs]
        dzs = [(dys[g] * ps_ref[:, sl[g]]).astype(BF16) for g in gs]
        d_devs = [_dot_nt(dzs[g], pw_ref[g]) for g in gs]
        dps_ref[...] += jnp.concatenate([jnp.sum(dys[g] * zs[g], axis=0, keepdims=True) for g in gs], axis=1)
        for g in gs:
            dpw_ref[g] += _dot_tn(devs[g], dzs[g])
        dpc_ref[band, :] += jnp.concatenate([_dot_tn(pm_ref[0, g], d_devs[g].astype(BF16)) for g in gs], axis=1)
        dpc_ref[centre, :] -= jnp.concatenate(d_devs, axis=1)

    return _call(
        body, name, (nb,),
        [pl.BlockSpec(memory_space=pltpu.SMEM), _rows(BLOCK, Q_PAD), _full((S, KV_WIDTH)), _full((S, KV_WIDTH)),
         _full((S, POOL_WIDTH)), _rows(BLOCK, Q_PAD), _rows(BLOCK, POOL_WIDTH), _pool_spec(nb),
         _full((N_POOL, POOL_GROUP, POOL_GROUP)), _full((1, POOL_WIDTH)),
         _rows(BLOCK, LANES), _rows(BLOCK, LANES), _rows(BLOCK, LANES)],
        [_rows(BLOCK, Q_PAD), _full((S, KV_WIDTH)), _full((S, KV_WIDTH)), _full((S, POOL_WIDTH)),
         _full((1, LANES)), _full((N_POOL, POOL_GROUP, POOL_GROUP)), _full((1, POOL_WIDTH))],
        [jax.ShapeDtypeStruct((S, Q_PAD), BF16), jax.ShapeDtypeStruct((S, KV_WIDTH), F32),
         jax.ShapeDtypeStruct((S, KV_WIDTH), F32), jax.ShapeDtypeStruct((S, POOL_WIDTH), F32),
         jax.ShapeDtypeStruct((1, LANES), F32), jax.ShapeDtypeStruct((N_POOL, POOL_GROUP, POOL_GROUP), F32),
         jax.ShapeDtypeStruct((1, POOL_WIDTH), F32)],
        [], 56, (sink, q, k, v, pc, da, dp, pool_m, pool_w, pool_scale, rc, rs1, rs2), rider)


def _mix_out(h, a, p, wa, wp, name):
    S = h.shape[0]
    tile = min(TOK_TILE, S)

    def body(h_ref, a_ref, p_ref, wa_ref, wp_ref, o_ref):
        o_ref[...] = h_ref[...] + _dot(a_ref[...], wa_ref[...]) + _dot(p_ref[...], wp_ref[...])

    return pl.pallas_call(
        body, name=name, grid=(S // tile,),
        in_specs=[_rows(tile, D_MODEL), _rows(tile, Q_PAD), _rows(tile, POOL_WIDTH),
                  _full((Q_PAD, D_MODEL)), _full((POOL_WIDTH, D_MODEL))],
        out_specs=_rows(tile, D_MODEL),
        out_shape=jax.ShapeDtypeStruct((S, D_MODEL), F32),
        compiler_params=_params(("parallel",), 40),
    )(h, a, p, wa, wp)


def _loss_head(h, target, gain, name):
    S = h.shape[0]
    tile = min(TOK_TILE, S)

    def body(h_ref, t_ref, g_ref, dh_ref, dhalf_ref, loss_ref, dg_ref):
        @pl.when(pl.program_id(0) == 0)
        def _():
            loss_ref[...] = jnp.zeros_like(loss_ref)
            dg_ref[...] = jnp.zeros_like(dg_ref)

        g = g_ref[...]
        r, xh = _rms_stats(h_ref[...])
        err = xh * g - t_ref[...]
        loss_ref[...] += (0.5 / D_MODEL) * jnp.sum(err * err, axis=0, keepdims=True)
        dh, dg = _rms_bwd(err * (1.0 / D_MODEL), g, r, xh)
        dg_ref[...] += dg
        dh_ref[...] = dh
        dhalf_ref[...] = (0.5 * dh).astype(BF16)

    return pl.pallas_call(
        body, name=name, grid=(S // tile,),
        in_specs=[_rows(tile, D_MODEL), _rows(tile, D_MODEL), _full((1, D_MODEL))],
        out_specs=[_rows(tile, D_MODEL), _rows(tile, D_MODEL), _full((1, D_MODEL)), _full((1, D_MODEL))],
        out_shape=[jax.ShapeDtypeStruct((S, D_MODEL), F32), jax.ShapeDtypeStruct((S, D_MODEL), BF16),
                   jax.ShapeDtypeStruct((1, D_MODEL), F32), jax.ShapeDtypeStruct((1, D_MODEL), F32)],
        compiler_params=_params(("arbitrary",), 40),
    )(h, target, gain)


def _mix_out_bwd(dh_out, dn, h, gain, a, p, wa, wp, name, rider=None):
    S = h.shape[0]
    tile = min(TOK_TILE, S)

    def body(do_ref, dn_ref, h_ref, g_ref, a_ref, p_ref, wa_ref, wp_ref, dh_ref, da_ref, dp_ref, dwa_ref, dwp_ref, dg_ref):
        @pl.when(pl.program_id(0) == 0)
        def _():
            dwa_ref[...] = jnp.zeros_like(dwa_ref)
            dwp_ref[...] = jnp.zeros_like(dwp_ref)
            dg_ref[...] = jnp.zeros_like(dg_ref)

        r, xh = _rms_stats(h_ref[...])
        dnorm, dg = _rms_bwd(_sum_chunks([dn_ref]), g_ref[...], r, xh)
        dh = do_ref[...] + dnorm
        dg_ref[...] += dg
        dh_ref[...] = dh
        dhb = dh.astype(BF16)
        da_ref[...] = _dot_nt(dhb, wa_ref[...]).astype(BF16)
        dp_ref[...] = _dot_nt(dhb, wp_ref[...]).astype(BF16)
        dwa_ref[...] += _dot_tn(a_ref[...], dhb)
        dwp_ref[...] += _dot_tn(p_ref[...], dhb)

    return _call(
        body, name, (S // tile,),
        [_rows(tile, D_MODEL), _chunk_rows(tile, dn.shape[0]), _rows(tile, D_MODEL), _full((1, D_MODEL)),
         _rows(tile, Q_PAD), _rows(tile, POOL_WIDTH), _full((Q_PAD, D_MODEL)), _full((POOL_WIDTH, D_MODEL))],
        [_rows(tile, D_MODEL), _rows(tile, Q_PAD), _rows(tile, POOL_WIDTH),
         _full((Q_PAD, D_MODEL)), _full((POOL_WIDTH, D_MODEL)), _full((1, D_MODEL))],
        [jax.ShapeDtypeStruct((S, D_MODEL), F32), jax.ShapeDtypeStruct((S, Q_PAD), BF16),
         jax.ShapeDtypeStruct((S, POOL_WIDTH), BF16), jax.ShapeDtypeStruct((Q_PAD, D_MODEL), F32),
         jax.ShapeDtypeStruct((POOL_WIDTH, D_MODEL), F32), jax.ShapeDtypeStruct((1, D_MODEL), F32)],
        [], 48, (dh_out, dn, h, gain, a, p, wa, wp), rider)


def _mix_in_bwd(dh_out, h, gain, n, dq, dk, dv, dpc, rc, rs1, rs2, w_in, name, rider=None):
    S = h.shape[0]
    tile = min(TOK_TILE, S)

    def body(do_ref, h_ref, g_ref, n_ref, dq_ref, dk_ref, dv_ref, dpc_ref, c_ref, s1_ref, s2_ref, w_ref,
             dh_ref, dhalf_ref, dw_ref, dg_ref):
        @pl.when(pl.program_id(0) == 0)
        def _():
            dw_ref[...] = jnp.zeros_like(dw_ref)
            dg_ref[...] = jnp.zeros_like(dg_ref)

        dk = _rope_bwd(dk_ref[...], c_ref[...], s1_ref[...], s2_ref[...]).astype(BF16)
        du = jnp.concatenate([dq_ref[...], dk, dv_ref[...].astype(BF16), dpc_ref[...].astype(BF16)], axis=1)
        dn = _dot(du, w_ref[...])
        dw_ref[...] += _dot_tn(du, n_ref[...])
        r, xh = _rms_stats(h_ref[...])
        dnorm, dg = _rms_bwd(dn, g_ref[...], r, xh)
        dh = do_ref[...] + dnorm
        dg_ref[...] += dg
        dh_ref[...] = dh
        dhalf_ref[...] = (0.5 * dh).astype(BF16)

    return _call(
        body, name, (S // tile,),
        [_rows(tile, D_MODEL), _rows(tile, D_MODEL), _full((1, D_MODEL)), _rows(tile, D_MODEL),
         _rows(tile, Q_PAD), _rows(tile, KV_WIDTH), _rows(tile, KV_WIDTH), _rows(tile, POOL_WIDTH),
         _rows(tile, LANES), _rows(tile, LANES), _rows(tile, LANES), _full((U_PAD, D_MODEL))],
        [_rows(tile, D_MODEL), _rows(tile, D_MODEL), _full((U_PAD, D_MODEL)), _full((1, D_MODEL))],
        [jax.ShapeDtypeStruct((S, D_MODEL), F32), jax.ShapeDtypeStruct((S, D_MODEL), BF16),
         jax.ShapeDtypeStruct((U_PAD, D_MODEL), F32), jax.ShapeDtypeStruct((1, D_MODEL), F32)],
        [], 56, (dh_out, h, gain, n, dq, dk, dv, dpc, rc, rs1, rs2, w_in), rider)


def _norm_bwd(dh_out, dns, h, gain, name, rider=None):
    S = h.shape[0]
    tile = min(TOK_TILE, S)
    n = len(dns)

    def body(do_ref, *refs):
        h_ref, g_ref, dh_ref, dg_ref = refs[n:]

        @pl.when(pl.program_id(0) == 0)
        def _():
            dg_ref[...] = jnp.zeros_like(dg_ref)

        r, xh = _rms_stats(h_ref[...])
        dnorm, dg = _rms_bwd(_sum_chunks(refs[:n]), g_ref[...], r, xh)
        dg_ref[...] += dg
        dh_ref[...] = do_ref[...] + dnorm

    return _call(
        body, name, (S // tile,),
        [_rows(tile, D_MODEL)] + [_chunk_rows(tile, dn.shape[0]) for dn in dns] + [_rows(tile, D_MODEL), _full((1, D_MODEL))],
        [_rows(tile, D_MODEL), _full((1, D_MODEL))],
        [jax.ShapeDtypeStruct((S, D_MODEL), F32), jax.ShapeDtypeStruct((1, D_MODEL), F32)],
        [], 40, (dh_out, *dns, h, gain), rider)


def _rope_tables(S):
    half = ROTARY_DIM // 2
    inv_freq = ROPE_THETA ** (-jnp.arange(0, ROTARY_DIM, 2, dtype=F32) / ROTARY_DIM)
    dim = jnp.arange(LANES) % HEAD_DIM
    ang = jnp.arange(S, dtype=F32)[:, None] * inv_freq[dim % half][None, :]
    lo, hi = (dim < half)[None, :], ((dim >= half) & (dim < ROTARY_DIM))[None, :]
    c = jnp.where(lo | hi, jnp.cos(ang), 1.0)
    s1 = jnp.where(lo, -jnp.sin(ang), 0.0)
    s2 = jnp.where(hi, jnp.sin(ang), 0.0)
    return c, s1, s2


def _pad_heads(w, axis):
    w = jnp.moveaxis(w, axis, 0)
    heads = w.reshape((N_HEADS, HEAD_DIM) + w.shape[1:])
    zero = jnp.zeros_like(heads)
    first = (jnp.arange(N_HEADS) < Q_PER_KV).reshape((N_HEADS, 1) + (1,) * (w.ndim - 1))
    lo = jnp.where(first, heads, zero)
    hi = jnp.where(first, zero, heads)
    padded = jnp.concatenate([lo, hi], axis=1).reshape((Q_PAD,) + w.shape[1:])
    return jnp.moveaxis(padded, 0, axis)


def _unpad_heads(w, axis):
    w = jnp.moveaxis(w, axis, 0)
    groups = w.reshape((N_HEADS, 2, HEAD_DIM) + w.shape[1:])
    first = (jnp.arange(N_HEADS) < Q_PER_KV).reshape((N_HEADS, 1) + (1,) * (w.ndim - 1))
    heads = jnp.where(first, groups[:, 0], groups[:, 1]).reshape((ATTN_WIDTH,) + w.shape[1:])
    return jnp.moveaxis(heads, 0, axis)


IN_ROWS = IN_WIDTH // N_CHIPS
OUT_ROWS = (ATTN_WIDTH + POOL_WIDTH) // N_CHIPS
MIX_ROWS = IN_ROWS + OUT_ROWS
FFN_ROWS = 3 * FF_CHUNK


def _step(x, target, bufs, small, place):
    S = x.shape[0]
    rc, rs1, rs2 = _rope_tables(S)
    mine = place[1]
    order = jnp.stack([mine, mine ^ 2, mine ^ 1, mine ^ 3])
    h1, n1, gate1, up1, ffn1, mix, ffn2 = _ffn_fwd(
        order, x, small["ffn1_norm"], "ffn1_fwd",
        _merge(_merge(_allgather(bufs[:1], in_passes=0), _allgather(bufs[1:2])), _allgather(bufs[2:], peers=X_Y_DIAGONAL[:2])), 0)
    w_in_t = mix[:, :IN_ROWS].reshape(IN_WIDTH, D_MODEL)
    w_in_pad = jnp.concatenate([_pad_heads(w_in_t[:ATTN_WIDTH], 0), w_in_t[ATTN_WIDTH:]], axis=0)
    w_out = mix[:, IN_ROWS:].reshape(ATTN_WIDTH + POOL_WIDTH, D_MODEL)
    wa = _pad_heads(w_out[:ATTN_WIDTH], 0)
    wp = w_out[ATTN_WIDTH:]
    pool_w = small["pool_w"].astype(BF16)

    n2, q, k, v, pc = _mix_in(h1, small["mix_norm"], w_in_pad, rc, rs1, rs2, "mix_in")
    pool_m = _pool_matrices(S)
    a, p = _mix_core_fwd(q, k, v, pc, small["sink_logits"], pool_m, pool_w, small["pool_scale"], "mix_core_fwd")
    h2 = _mix_out(h1, a, p, wa, wp, "mix_out")
    h3, n3, gate2, up2, ffn2 = _ffn_fwd(order, h2, small["ffn2_norm"], "ffn2_fwd",
                                        _allgather([ffn2], peers=X_Y_DIAGONAL[2:], in_passes=0), 0)
    dh3, dhalf3, loss_lanes, d_final = _loss_head(h3, target, small["final_norm"], "loss_head")

    dn3, d_ffn2 = _ffn_bwd(jnp.arange(N_CHIPS, dtype=jnp.int32), dhalf3, n3, gate2, up2, ffn2, "ffn2_bwd")
    dh2, da, dp, dwa, dwp, d_ffn2_norm, received = _mix_out_bwd(dh3, dn3, h2, small["ffn2_norm"], a, p, wa, wp, "mix_out_bwd",
                                                                _sibling_exchange([d_ffn2]))
    pair = _pair_sum(place, d_ffn2, received, "grad_pair_sum_ffn2")
    dq, dk, dv, dpc, dsink, dpool_w, dpool_scale, stack = _mix_core_bwd(
        q, k, v, pc, da, dp, small["sink_logits"], pool_m, pool_w, small["pool_scale"], rc, rs1, rs2, "mix_core_bwd",
        _scatter([pair], peers=X_Y_DIAGONAL[:2]))
    dh1, dhalf1, dw_in_pad, d_mix_norm, stack = _mix_in_bwd(dh2, h1, small["mix_norm"], n2, dq, dk, dv, dpc, rc, rs1, rs2,
                                                            w_in_pad, "mix_in_bwd",
                                                            _scatter([pair], peers=X_Y_DIAGONAL[2:], stacks=[stack]))
    reduced_ffn2 = _chip_sum(place, pair, stack, 2, "grad_chip_sum_ffn2")
    dw_in_t = jnp.concatenate([_unpad_heads(dw_in_pad[:Q_PAD], 0), dw_in_pad[Q_PAD:]], axis=0)
    dw_out = jnp.concatenate([_unpad_heads(dwa, 0), dwp], axis=0)
    d_mix = jnp.concatenate([dw_in_t.reshape(N_CHIPS, IN_ROWS, D_MODEL), dw_out.reshape(N_CHIPS, OUT_ROWS, D_MODEL)], axis=1)
    d_mix = jnp.transpose(d_mix.reshape(N_CHIPS, 2, MIX_ROWS // 2, D_MODEL), (1, 0, 2, 3)).astype(BF16)
    small_g = {"ffn1_norm": jnp.zeros_like(d_mix_norm), "mix_norm": d_mix_norm, "ffn2_norm": d_ffn2_norm,
               "final_norm": d_final, "pool_scale": dpool_scale, "sink_logits": dsink[:, :N_HEADS], "pool_w": dpool_w}
    loss_row = jnp.sum(loss_lanes.reshape(D_MODEL // LANES, LANES), axis=0, keepdims=True)
    small_early = _pack_small(small_g, loss_row)

    chunk = [(place[1:] + 1 + p) % N_CHIPS for p in range(N_CHIPS)]
    ffn1_bwd = functools.partial(_ffn_bwd, d_out=dhalf1, n=n1, gate=gate1, up=up1, group=ffn1)
    dn_a, dw_a, recv_mix, small_all, g_ffn2 = ffn1_bwd(
        chunk[0], name="ffn1_bwd_0",
        rider=_merge(_merge(_sibling_exchange([d_mix]), _small_allgather(small_early)), _sibling_share([reduced_ffn2])))
    pair_mix = _pair_sum(place, d_mix, recv_mix, "grad_pair_sum_mix")
    dn_b, dw_b, recv_a, stack_mix = ffn1_bwd(chunk[1], name="ffn1_bwd_1",
                                             rider=_merge(_sibling_exchange([dw_a]), _scatter([pair_mix])))
    pair_a = _pair_sum(place, dw_a, recv_a, "grad_pair_sum_ffn1_0")
    reduced_mix = _chip_sum(place, pair_mix, stack_mix, 2, "grad_chip_sum_mix")
    dn_c, dw_c, recv_b, stack, g_mix = ffn1_bwd(
        chunk[2], name="ffn1_bwd_2",
        rider=_merge(_merge(_sibling_exchange([dw_b]), _scatter_step(pair_a, stack, 0)), _sibling_share([reduced_mix])))
    pair_b = _pair_sum(place, dw_b, recv_b, "grad_pair_sum_ffn1_1")
    dn_d, dw_d, recv_c, stack = ffn1_bwd(chunk[3], name="ffn1_bwd_3",
                                         rider=_merge(_sibling_exchange([dw_c]), _scatter_step(pair_b, stack, 1)))
    pair_c = _pair_sum(place, dw_c, recv_c, "grad_pair_sum_ffn1_2")
    grad_x, d_ffn1_norm, recv_d, stack = _norm_bwd(dh1, [dn_a, dn_b, dn_c, dn_d], x, small["ffn1_norm"], "norm1_bwd",
                                                   _merge(_sibling_exchange([dw_d]), _scatter_step(pair_c, stack, 2)))
    pair_d = _pair_sum(place, dw_d, recv_d, "grad_pair_sum_ffn1_3")
    reduced_ffn1 = _chip_sum(place, pair_d, stack, 2, "grad_chip_sum_ffn1")
    g_ffn1, gains = _comm_call(_merge(_sibling_share([reduced_ffn1]), _small_allgather(d_ffn1_norm.reshape(-1, LANES))),
                               "grad_share_tail")
    gain_sum = _sum_leading(gains, 1, "gain_grad_sum")
    small_sum = jnp.concatenate([gain_sum, _sum_leading(small_all, 1, "small_grad_sum")[gain_sum.shape[0]:]], axis=0)
    return jnp.sum(small_sum[SMALL_ROWS - 1]), grad_x, [g.reshape(-1, D_MODEL) for g in (g_ffn1, g_mix, g_ffn2)], small_sum


GROUPS =(("ffn1_w_gate", "ffn1_w_up", "ffn1_w_down"), ("w_in", "w_out"), ("ffn2_w_gate", "ffn2_w_up", "ffn2_w_down"))
TRANSPOSED = ("ffn1_w_gate", "ffn1_w_up", "w_in", "ffn2_w_gate", "ffn2_w_up")


def _place():
    x, y, c = lax.axis_index("x"), lax.axis_index("y"), lax.axis_index("c")
    chips = [(1 - x, y), (x, 1 - y), (1 - x, 1 - y)]
    return x, y, c, chips


def _remote(src, dst, send_sem, recv_sem, to):
    return pltpu.make_async_remote_copy(src_ref=src, dst_ref=dst, send_sem=send_sem, recv_sem=recv_sem,
                                        device_id=to, device_id_type=MESH)


def _pack(chip, members, name):
    rows = members[0].shape[0]
    n = len(members)

    def body(chip_ref, *refs):
        ins, out_ref, buf, sems = refs[:n], refs[n], refs[n + 1], refs[n + 2]
        copies = [pltpu.make_async_copy(ins[k], buf.at[k], sems.at[k]) for k in range(n)]
        for cp in copies:
            cp.start()
        for k in range(n):
            copies[k].wait()
            out_ref[0, k * rows:(k + 1) * rows, :] = buf[k].astype(BF16)

    return pl.pallas_call(
        body, name=name,
        grid_spec=pltpu.PrefetchScalarGridSpec(
            num_scalar_prefetch=1, grid=(1,),
            in_specs=[HBM_SPEC] * n,
            out_specs=pl.BlockSpec((1, n * rows, D_MODEL), lambda k, chip_ref: (chip_ref[0], 0, 0)),
            scratch_shapes=[pltpu.VMEM((n, rows, D_MODEL), F32), pltpu.SemaphoreType.DMA((n,))]),
        out_shape=jax.ShapeDtypeStruct((N_CHIPS, n * rows, D_MODEL), BF16),
        compiler_params=_params(("arbitrary",), 40),
    )(chip, *members)


def _same(arrays):
    return [jax.ShapeDtypeStruct(a.shape, a.dtype) for a in arrays]


X_Y_DIAGONAL = (0, 1, 2)


def _allgather(bufs, peers=X_Y_DIAGONAL, in_passes=None):
    n = len(bufs)

    def copy(kind, outs, send_sems, recv_sems, a, k):
        x, y, c, chips = _place()
        half = bufs[a].shape[1] // 2

        def rows(slot, core):
            return outs[a].at[slot, pl.ds(pl.multiple_of(core * half, 16), half)]

        me, slot = 2 * x + y, 2 * chips[k][0] + chips[k][1]
        over_ici = (send_sems.at[6 * a + k], recv_sems.at[6 * a + k])
        over_d2d = (send_sems.at[6 * a + 3 + k], recv_sems.at[6 * a + 3 + k])
        if kind == "first":
            return _remote(rows(me, c), rows(me, c), *over_ici, (*chips[k], c))
        if kind == "landed":
            return _remote(rows(me, c), rows(slot, c), *over_ici, (*chips[k], c))
        if kind == "passed":
            return _remote(rows(slot, c), rows(slot, c), *over_d2d, (x, y, 1 - c))
        return _remote(rows(me, c), rows(slot, 1 - c), *over_d2d, (x, y, 1 - c))

    def arrive(pairs):
        def hook(ins, outs, sems):
            for a, k in pairs:
                copy("landed", outs, *sems, a, k).wait_recv()
                copy("passed", outs, *sems, a, k).start()
            for a, k in pairs:
                copy("handed", outs, *sems, a, k).wait_recv()
        return hook

    everything = [(a, k) for a in range(n) for k in peers]
    early = [(a, k) for a, k in everything if a == in_passes]

    def start(ins, outs, sems):
        for a, k in everything:
            copy("first", outs, *sems, a, k).start()

    def finish(ins, outs, sems):
        arrive([pair for pair in everything if pair not in early])(ins, outs, sems)
        for a, k in everything:
            copy("first", outs, *sems, a, k).wait_send()
            copy("passed", outs, *sems, a, k).wait_send()

    hooks = tuple(((k + 1, 0), arrive([(a, k)])) for a, k in early)
    return _Rider(list(bufs), _same(bufs), {a: a for a in range(n)},
                  [pltpu.SemaphoreType.DMA((6 * n,)), pltpu.SemaphoreType.DMA((6 * n,))], start, finish, hooks)


def _sibling_exchange(parts):
    n = len(parts)

    def copies(ins, outs, send_sems, recv_sems):
        x, y, c, _ = _place()
        return [_remote(ins[a].at[1 - c], outs[a], send_sems.at[a], recv_sems.at[a], (x, y, 1 - c)) for a in range(n)]

    def start(ins, outs, sems):
        for cp in copies(ins, outs, *sems):
            cp.start()

    def finish(ins, outs, sems):
        for cp in copies(ins, outs, *sems):
            cp.wait_recv()
            cp.wait_send()

    return _Rider(list(parts), [jax.ShapeDtypeStruct(p.shape[1:], p.dtype) for p in parts], {},
                  [pltpu.SemaphoreType.DMA((n,)), pltpu.SemaphoreType.DMA((n,))], start, finish)


def _small_allgather(small):
    flips = [(fx, fy, fc) for fx in range(2) for fy in range(2) for fc in range(2)][1:]

    def copies(small_ref, gather_ref, send_sems, recv_sems, local_sem, started_only=False):
        x, y, c, _ = _place()
        me = 4 * x + 2 * y + c
        peers = [((1 - x) if fx else x, (1 - y) if fy else y, (1 - c) if fc else c) for fx, fy, fc in flips]
        own = pltpu.make_async_copy(small_ref, gather_ref.at[me], local_sem)
        sent = [_remote(small_ref, gather_ref.at[me], send_sems.at[k], recv_sems.at[k], peer) for k, peer in enumerate(peers)]
        if started_only:
            return own, sent
        landed = [_remote(small_ref, gather_ref.at[4 * px + 2 * py + pc], send_sems.at[k], recv_sems.at[k], (px, py, pc))
                  for k, (px, py, pc) in enumerate(peers)]
        return own, sent, landed

    def start(ins, outs, sems):
        own, sent = copies(ins[0], outs[0], *sems, started_only=True)
        own.start()
        for cp in sent:
            cp.start()

    def finish(ins, outs, sems):
        own, sent, landed = copies(ins[0], outs[0], *sems)
        for cp in landed:
            cp.wait_recv()
        for cp in sent:
            cp.wait_send()
        own.wait()

    return _Rider([small], [jax.ShapeDtypeStruct((2 * N_CHIPS,) + small.shape, small.dtype)], {},
                  [pltpu.SemaphoreType.DMA((7,)), pltpu.SemaphoreType.DMA((7,)), pltpu.SemaphoreType.DMA], start, finish)


def _merge(a, b):
    na, nao, nas = len(a.operands), len(a.out_shapes), len(a.scratch)

    def start(ins, outs, sems):
        a.start(ins[:na], outs[:nao], sems[:nas])
        b.start(ins[na:], outs[nao:], sems[nas:])

    def finish(ins, outs, sems):
        a.finish(ins[:na], outs[:nao], sems[:nas])
        b.finish(ins[na:], outs[nao:], sems[nas:])

    def of_a(fn):
        return lambda ins, outs, sems: fn(ins[:na], outs[:nao], sems[:nas])

    def of_b(fn):
        return lambda ins, outs, sems: fn(ins[na:], outs[nao:], sems[nas:])

    aliases = {**a.aliases, **{na + k: nao + v for k, v in b.aliases.items()}}
    hooks = tuple((at, of_a(fn)) for at, fn in a.hooks) + tuple((at, of_b(fn)) for at, fn in b.hooks)
    return _Rider(a.operands + b.operands, a.out_shapes + b.out_shapes, aliases, a.scratch + b.scratch, start, finish, hooks)


def _scatter_step(pair, stack, step):
    def copies(pair_ref, stack_ref, send_sem, recv_sem, started_only=False):
        x, y, c, _ = _place()
        me = 2 * x + y
        to = (me + 1 + step) % N_CHIPS
        frm = (me + N_CHIPS - 1 - step) % N_CHIPS
        sent = _remote(pair_ref.at[0], stack_ref.at[me], send_sem, recv_sem, (to // 2, to % 2, c))
        if started_only:
            return sent
        landed = _remote(pair_ref.at[0], stack_ref.at[frm], send_sem, recv_sem, (frm // 2, frm % 2, c))
        return sent, landed

    def start(ins, outs, sems):
        copies(ins[0], outs[0], *sems, started_only=True).start()

    def finish(ins, outs, sems):
        sent, landed = copies(ins[0], outs[0], *sems)
        landed.wait_recv()
        sent.wait_send()

    return _Rider([pair, stack], _same([stack]), {1: 0}, [pltpu.SemaphoreType.DMA, pltpu.SemaphoreType.DMA], start, finish)


def _scatter(sums, peers=X_Y_DIAGONAL, stacks=None):
    n = len(sums)

    def copies(ins, outs, send_sems, recv_sems, started_only=False):
        x, y, c, chips = _place()
        me = 2 * x + y
        slots = [2 * cx + cy for cx, cy in chips]
        sent = [_remote(ins[a].at[slots[k]], outs[a].at[me], send_sems.at[3 * a + k], recv_sems.at[3 * a + k], (*chips[k], c))
                for a in range(n) for k in peers]
        if started_only:
            return sent
        landed = [_remote(ins[a].at[slots[k]], outs[a].at[slots[k]], send_sems.at[3 * a + k], recv_sems.at[3 * a + k],
                          (*chips[k], c)) for a in range(n) for k in peers]
        return sent, landed

    def start(ins, outs, sems):
        for cp in copies(ins, outs, *sems, started_only=True):
            cp.start()

    def finish(ins, outs, sems):
        sent, landed = copies(ins, outs, *sems)
        for cp in landed:
            cp.wait_recv()
        for cp in sent:
            cp.wait_send()

    sems = [pltpu.SemaphoreType.DMA((3 * n,)), pltpu.SemaphoreType.DMA((3 * n,))]
    if stacks is None:
        return _Rider(list(sums), _same(sums), {}, sems, start, finish)
    return _Rider(list(sums) + list(stacks), _same(sums), {n + a: a for a in range(n)}, sems, start, finish)


def _sibling_share(bufs):
    n = len(bufs)

    def copies(outs, send_sems, recv_sems, started_only=False):
        x, y, c, _ = _place()
        sent = [_remote(outs[a].at[c], outs[a].at[c], send_sems.at[a], recv_sems.at[a], (x, y, 1 - c)) for a in range(n)]
        if started_only:
            return sent
        landed = [_remote(outs[a].at[c], outs[a].at[1 - c], send_sems.at[a], recv_sems.at[a], (x, y, 1 - c)) for a in range(n)]
        return sent, landed

    def start(ins, outs, sems):
        for cp in copies(outs, *sems, started_only=True):
            cp.start()

    def finish(ins, outs, sems):
        sent, landed = copies(outs, *sems)
        for cp in landed:
            cp.wait_recv()
        for cp in sent:
            cp.wait_send()

    return _Rider(list(bufs), _same(bufs), {a: a for a in range(n)},
                  [pltpu.SemaphoreType.DMA((n,)), pltpu.SemaphoreType.DMA((n,))], start, finish)


def _pair_sum(core, part, received, name):
    _, k, rh, cols = part.shape

    def body(core_ref, p_ref, r_ref, o_ref):
        o_ref[...] = (p_ref[0].astype(F32) + r_ref[...].astype(F32)).astype(BF16)

    return pl.pallas_call(
        body, name=name,
        grid_spec=pltpu.PrefetchScalarGridSpec(
            num_scalar_prefetch=1, grid=(k,),
            in_specs=[pl.BlockSpec((1, 1, rh, cols), lambda j, core_ref: (core_ref[0], j, 0, 0)),
                      pl.BlockSpec((1, rh, cols), lambda j, core_ref: (j, 0, 0))],
            out_specs=pl.BlockSpec((1, rh, cols), lambda j, core_ref: (j, 0, 0))),
        out_shape=jax.ShapeDtypeStruct((k, rh, cols), BF16),
        compiler_params=_params(("parallel",), 32),
    )(core, part, received)


def _sum_leading(stack, steps, name):
    k, rows, cols = stack.shape
    tile = rows // steps

    def body(s_ref, o_ref):
        total = s_ref[0].astype(F32)
        for d in range(1, k):
            total = total + s_ref[d].astype(F32)
        o_ref[...] = total

    return pl.pallas_call(
        body, name=name, grid=(steps,),
        in_specs=[pl.BlockSpec((k, tile, cols), lambda i: (0, i, 0))],
        out_specs=pl.BlockSpec((tile, cols), lambda i: (i, 0)),
        out_shape=jax.ShapeDtypeStruct((rows, cols), F32),
        compiler_params=_params(("parallel",), 32),
    )(stack)


def _chip_sum(place, own, stack, steps, name):
    k, rows, cols = stack.shape
    tile = rows // steps

    def body(place_ref, own_ref, *refs):
        chip = place_ref[1]
        total = None
        for d in range(k):
            term = jnp.where(chip == d, own_ref[0], refs[d][0]).astype(F32)
            total = term if total is None else total + term
        refs[k][0] = total

    def other(d):
        return lambda i, place_ref: (jnp.where(place_ref[1] == d, (d + 1) % k, d), i, 0)

    return pl.pallas_call(
        body, name=name,
        grid_spec=pltpu.PrefetchScalarGridSpec(
            num_scalar_prefetch=1, grid=(steps,),
            in_specs=[pl.BlockSpec((1, tile, cols), lambda i, place_ref: (place_ref[1] % own.shape[0], i, 0))]
            + [pl.BlockSpec((1, tile, cols), other(d)) for d in range(k)],
            out_specs=pl.BlockSpec((1, tile, cols), lambda i, place_ref: (place_ref[0], i, 0))),
        out_shape=jax.ShapeDtypeStruct((2, rows, cols), F32),
        compiler_params=_params(("arbitrary",), 32),
    )(place, own, *([stack] * k))


def _adamw(w, g, row0, m, v, tile, name):
    rows, cols = w.shape
    first = row0 // tile
    assert rows % tile == 0 and row0 % tile == 0
    bc1 = 1.0 - ADAM_B1 ** ADAM_STEP
    bc2 = 1.0 - ADAM_B2 ** ADAM_STEP

    def body(w_ref, g_ref, m_ref, v_ref, go_ref, d_ref, mo_ref, vo_ref):
        g = g_ref[...]
        m_new = ADAM_B1 * m_ref[...] + (1.0 - ADAM_B1) * g
        v_new = ADAM_B2 * v_ref[...] + (1.0 - ADAM_B2) * (g * g)
        go_ref[...] = g
        d_ref[...] = -ADAM_LR * ((m_new / bc1) / (jnp.sqrt(v_new / bc2) + ADAM_EPS) + ADAM_WD * w_ref[...])
        mo_ref[...] = m_new
        vo_ref[...] = v_new

    spec = pl.BlockSpec((tile, cols), lambda i: (i, 0))
    g_spec = pl.BlockSpec((tile, cols), lambda i: (first + i, 0))
    return pl.pallas_call(
        body, name=name, grid=(rows // tile,),
        in_specs=[spec, g_spec, spec, spec], out_specs=[spec] * 4,
        out_shape=[jax.ShapeDtypeStruct((rows, cols), F32)] * 4,
        compiler_params=_params(("parallel",), 32),
    )(w, g, m, v)


SMALL = ("ffn1_norm", "mix_norm", "ffn2_norm", "final_norm", "pool_scale", "sink_logits", "pool_w")


def _pack_small(d, last_row=None):
    sink = jnp.pad(d["sink_logits"].reshape(1, N_HEADS), ((0, 0), (0, LANES - N_HEADS)))
    rows = [d[n].reshape(-1, LANES) for n in SMALL[:5]] + [sink, d["pool_w"].reshape(-1, LANES)]
    used = sum(r.shape[0] for r in rows)
    last = jnp.zeros((1, LANES), F32) if last_row is None else last_row
    return jnp.concatenate(rows + [jnp.zeros((SMALL_ROWS - used - 1, LANES), F32), last], axis=0)


def _unpack_small(packed, like):
    out, row = {}, 0
    for n in SMALL:
        size = LANES if n == "sink_logits" else math.prod(like[n].shape)
        chunk = packed[row:row + size // LANES].reshape(-1)
        out[n] = (chunk[:N_HEADS] if n == "sink_logits" else chunk).reshape(like[n].shape)
        row += size // LANES
    return out


def kernel(x, ffn1_norm, ffn1_w_gate, ffn1_w_up, ffn1_w_down, mix_norm, w_in, sink_logits, pool_w, pool_scale, w_out, ffn2_norm, ffn2_w_gate, ffn2_w_up, ffn2_w_down, final_norm, loss_target, m_ffn1_norm, m_ffn1_w_gate, m_ffn1_w_up, m_ffn1_w_down, m_mix_norm, m_w_in, m_sink_logits, m_pool_w, m_pool_scale, m_w_out, m_ffn2_norm, m_ffn2_w_gate, m_ffn2_w_up, m_ffn2_w_down, m_final_norm, v_ffn1_norm, v_ffn1_w_gate, v_ffn1_w_up, v_ffn1_w_down, v_mix_norm, v_w_in, v_sink_logits, v_pool_w, v_pool_scale, v_w_out, v_ffn2_norm, v_ffn2_w_gate, v_ffn2_w_up, v_ffn2_w_down, v_final_norm):
    names = ("ffn1_norm", "ffn1_w_gate", "ffn1_w_up", "ffn1_w_down", "mix_norm", "w_in", "sink_logits", "pool_w",
             "pool_scale", "w_out", "ffn2_norm", "ffn2_w_gate", "ffn2_w_up", "ffn2_w_down", "final_norm")
    weights = dict(zip(names, (ffn1_norm, ffn1_w_gate, ffn1_w_up, ffn1_w_down, mix_norm, w_in, sink_logits, pool_w,
                               pool_scale, w_out, ffn2_norm, ffn2_w_gate, ffn2_w_up, ffn2_w_down, final_norm)))
    mom1 = dict(zip(names, (m_ffn1_norm, m_ffn1_w_gate, m_ffn1_w_up, m_ffn1_w_down, m_mix_norm, m_w_in, m_sink_logits,
                            m_pool_w, m_pool_scale, m_w_out, m_ffn2_norm, m_ffn2_w_gate, m_ffn2_w_up, m_ffn2_w_down,
                            m_final_norm)))
    mom2 = dict(zip(names, (v_ffn1_norm, v_ffn1_w_gate, v_ffn1_w_up, v_ffn1_w_down, v_mix_norm, v_w_in, v_sink_logits,
                            v_pool_w, v_pool_scale, v_w_out, v_ffn2_norm, v_ffn2_w_gate, v_ffn2_w_up, v_ffn2_w_down,
                            v_final_norm)))
    chip = (2 * lax.axis_index("x") + lax.axis_index("y")).astype(jnp.int32).reshape(1)
    place = jnp.concatenate([lax.axis_index("c").astype(jnp.int32).reshape(1), chip])

    def rows_of(t, n):
        return jnp.swapaxes(t[n][0], 0, 1) if n in TRANSPOSED else t[n][0]

    bufs = [_pack(chip, [rows_of(weights, n) for n in GROUPS[0]], "pack_ffn1"),
            _pack(chip, [jnp.concatenate([rows_of(weights, n) for n in GROUPS[1]], axis=0)], "pack_mix"),
            _pack(chip, [rows_of(weights, n) for n in GROUPS[2]], "pack_ffn2")]

    small_w = {"ffn1_norm": ffn1_norm, "mix_norm": mix_norm, "ffn2_norm": ffn2_norm,
               "final_norm": final_norm.reshape(1, D_MODEL), "pool_scale": pool_scale, "sink_logits": sink_logits,
               "pool_w": pool_w[0]}
    loss, grad_x, group_grads, small_sum = _step(x[0], loss_target[0], bufs, small_w, place)

    out_g, out_d, out_m, out_v = {}, {}, {}, {}
    for members, g in zip(GROUPS, group_grads):
        row0 = 0
        for n in members:
            w = rows_of(weights, n)
            tile = FF_CHUNK // 4 if w.shape[0] == FF_CHUNK else math.gcd(IN_ROWS, OUT_ROWS)
            outs = _adamw(w, g, row0, rows_of(mom1, n), rows_of(mom2, n), tile, "adamw_" + n)
            row0 += w.shape[0]
            for dst, t in zip((out_g, out_d, out_m, out_v), outs):
                dst[n] = (jnp.swapaxes(t, 0, 1) if n in TRANSPOSED else t).reshape(weights[n].shape)
    small_outs = _adamw(_pack_small(weights), small_sum, 0, _pack_small(mom1), _pack_small(mom2), SMALL_ROWS, "adamw_small")
    for dst, packed in zip((out_g, out_d, out_m, out_v), small_outs):
        dst.update(_unpack_small(packed, weights))

    return (loss,grad_x.reshape(x.shape), *[out_g[n] for n in names], *[out_d[n] for n in names],
            *[out_m[n] for n in names], *[out_v[n] for n in names])
```

```python
import collections
import functools
import math

import jax
import jax.numpy as jnp
from jax import lax
from jax.experimental import pallas as pl
from jax.experimental.pallas import tpu as pltpu

F32, BF16 = jnp.float32, jnp.bfloat16
MESH = pl.DeviceIdType.MESH

D_MODEL = 1024
D_FF = 2816
N_CHIPS = 4
FF_CHUNK = D_FF // N_CHIPS
HEAD_DIM = 64
N_HEADS = 8
N_KV = 2
Q_PER_KV = N_HEADS // N_KV
KV_WIDTH = N_KV * HEAD_DIM
ATTN_WIDTH = N_HEADS * HEAD_DIM
POOL_WINDOWS = (2, 4, 8, 16)
N_POOL = len(POOL_WINDOWS)
POOL_GROUP = 128
POOL_WIDTH = N_POOL * POOL_GROUP
IN_WIDTH = ATTN_WIDTH + 2 * KV_WIDTH + POOL_WIDTH
WINDOW = 128
BLOCK = 128
BAND = 3 * BLOCK
ROPE_THETA = 500000.0
ROTARY_DIM = HEAD_DIM // 4
EPS = 1e-6
LANES = 128
Q_PAD = N_HEADS * LANES
U_PAD = Q_PAD + 2 * KV_WIDTH + POOL_WIDTH
SCALE = HEAD_DIM ** -0.5
NEG = -1e30

ADAM_LR, ADAM_B1, ADAM_B2, ADAM_EPS, ADAM_WD, ADAM_STEP = 0.001, 0.9, 0.999, 1e-08, 0.01, 10

V7X_VMEM_BYTES = 64 * 1024 * 1024
TOK_TILE = 512
SMALL_ROWS = 552


def _params(sem, vmem_mb):
    assert vmem_mb * 1024 * 1024 <= V7X_VMEM_BYTES
    return pltpu.CompilerParams(dimension_semantics=sem, vmem_limit_bytes=vmem_mb * 1024 * 1024)


def _dot(a, b):
    return lax.dot_general(a, b, (((1,), (0,)), ((), ())), preferred_element_type=F32)


def _dot_nt(a, b):
    return lax.dot_general(a, b, (((1,), (1,)), ((), ())), preferred_element_type=F32)


def _dot_tn(a, b):
    return lax.dot_general(a, b, (((0,), (0,)), ((), ())), preferred_element_type=F32)


def _rms_stats(h):
    r = lax.rsqrt(jnp.mean(h * h, axis=-1, keepdims=True) + EPS)
    return r, h * r


def _rms_bwd(dn, g, r, xh):
    gd = dn * g
    dh = r * (gd - xh * jnp.mean(gd * xh, axis=-1, keepdims=True))
    return dh, jnp.sum(dn * xh, axis=0, keepdims=True)


def _rope(x, c, s1, s2):
    return x * c + pltpu.roll(x, LANES - ROTARY_DIM // 2, 1) * s1 + pltpu.roll(x, ROTARY_DIM // 2, 1) * s2


def _rope_bwd(d, c, s1, s2):
    return d * c + pltpu.roll(d * s1, ROTARY_DIM // 2, 1) + pltpu.roll(d * s2, LANES - ROTARY_DIM // 2, 1)


def _sum_chunks(refs):
    terms = [ref[j].astype(F32) for ref in refs for j in range(ref.shape[0])]
    return functools.reduce(lambda a, b: a + b, terms)


def _chunk_rows(tile, k):
    return pl.BlockSpec((k, tile, D_MODEL), lambda i, *_: (0, i, 0))


def _full(shape):
    nd = len(shape)
    return pl.BlockSpec(shape, lambda *_: (0,) * nd)


def _rows(tile, cols):
    return pl.BlockSpec((tile, cols), lambda i, *_: (i, 0))


HBM_SPEC = pl.BlockSpec(memory_space=pltpu.HBM)

_Rider = collections.namedtuple("_Rider", "operands out_shapes aliases scratch start finish hooks", defaults=[()])


_NO_RIDER = _Rider([], [], {}, [], None, None)


def _call(body, name, grid, in_specs, out_specs, out_shape, scratch, vmem_mb, args, rider=None, prefetch=(),
          shares_rider_refs=False):
    rider = rider or _NO_RIDER
    n_pre, n_in, n_out, n_scr = len(prefetch), len(in_specs), len(out_specs), len(scratch)
    r_in, r_out = len(rider.operands), len(rider.out_shapes)

    def fused(*refs):
        pre, refs = refs[:n_pre], refs[n_pre:]
        ins, refs = refs[:n_in], refs[n_in:]
        r_ins, refs = refs[:r_in], refs[r_in:]
        outs, refs = refs[:n_out], refs[n_out:]
        r_outs, refs = refs[:r_out], refs[r_out:]
        scr, r_scr = refs[:n_scr], refs[n_scr:]
        ids = [pl.program_id(d) for d in range(len(grid))]
        if rider.start is not None:
            @pl.when(functools.reduce(jnp.logical_and, [i == 0 for i in ids]))
            def _():
                rider.start(r_ins, r_outs, r_scr)

        for at, hook in rider.hooks:
            @pl.when(functools.reduce(jnp.logical_and, [i == a for i, a in zip(ids, at)]))
            def _(hook=hook):
                hook(r_ins, r_outs, r_scr)

        if shares_rider_refs:
            body(*pre, *ins, *outs, *scr, rider_refs=r_outs)
        else:
            body(*pre, *ins, *outs, *scr)

        if rider.finish is not None:
            @pl.when(functools.reduce(jnp.logical_and, [i == g - 1 for i, g in zip(ids, grid)]))
            def _():
                rider.finish(r_ins, r_outs, r_scr)

    return pl.pallas_call(
        fused, name=name,
        grid_spec=pltpu.PrefetchScalarGridSpec(
            num_scalar_prefetch=n_pre, grid=grid,
            in_specs=list(in_specs) + [HBM_SPEC] * r_in, out_specs=list(out_specs) + [HBM_SPEC] * r_out,
            scratch_shapes=list(scratch) + list(rider.scratch)),
        out_shape=list(out_shape) + list(rider.out_shapes),
        input_output_aliases={n_pre + n_in + k: n_out + v for k, v in rider.aliases.items()},
        compiler_params=_params(("arbitrary",) * len(grid), vmem_mb),
    )(*prefetch, *args, *rider.operands)


def _comm_call(rider, name):
    r_in, r_out = len(rider.operands), len(rider.out_shapes)

    def body(*refs):
        r_ins, r_outs, r_scr = refs[:r_in], refs[r_in:r_in + r_out], refs[r_in + r_out:]
        rider.start(r_ins, r_outs, r_scr)
        rider.finish(r_ins, r_outs, r_scr)

    return pl.pallas_call(
        body, name=name, in_specs=[HBM_SPEC] * r_in, out_specs=[HBM_SPEC] * r_out, out_shape=list(rider.out_shapes),
        input_output_aliases=dict(rider.aliases), scratch_shapes=list(rider.scratch),
    )(*rider.operands)


def _ffn_fwd(order, h, gain, name, rider, group_at, loss_head=None):
    S = h.shape[0]
    tile = min(TOK_TILE, S)
    nt = S // tile
    last = N_CHIPS - 1
    n_extra_in, n_head_out = (2, 4) if loss_head else (0, 1)

    def body(order_ref, h_ref, g_ref, *refs, rider_refs):
        extra_in, refs = refs[:n_extra_in], refs[n_extra_in:]
        head_out, (n_ref, gate_ref, up_ref, w_scr, w_sem, acc, n_scr) = refs[:n_head_out], refs[n_head_out:]
        j, i = pl.program_id(0), pl.program_id(1)

        @pl.when(i == 0)
        def _():
            fetch = pltpu.make_async_copy(rider_refs[group_at].at[order_ref[j]], w_scr, w_sem)
            fetch.start()
            fetch.wait()

        at = pl.multiple_of(i * tile, tile)

        @pl.when(j == 0)
        def _():
            _, xh = _rms_stats(h_ref[...])
            n = (xh * g_ref[...]).astype(BF16)
            n_scr[pl.ds(at, tile), :] = n
            n_ref[...] = n
            acc[pl.ds(at, tile), :] = jnp.zeros((tile, D_MODEL), F32)

        half = tile // 2
        wg, wu, wd = (w_scr[part * FF_CHUNK:(part + 1) * FF_CHUNK, :] for part in range(3))
        ns = [n_scr[pl.ds(at + s * half, half), :] for s in range(2)]
        gates = [_dot_nt(n, wg) for n in ns]
        ups = [_dot_nt(n, wu) for n in ns]
        acts = [(g * jax.nn.sigmoid(g) * u).astype(BF16) for g, u in zip(gates, ups)]
        for s in range(2):
            gate_ref[0, s * half:(s + 1) * half, :] = gates[s].astype(BF16)
            up_ref[0, s * half:(s + 1) * half, :] = ups[s].astype(BF16)
            acc[pl.ds(at + s * half, half), :] += _dot(acts[s], wd)

        @pl.when(j == last)
        def _():
            out = h_ref[...] + 0.5 * acc[pl.ds(at, tile), :]
            if not loss_head:
                head_out[0][...] = out
                return
            (t_ref, gf_ref), (dh_ref, dhalf_ref, loss_ref, dg_ref) = extra_in, head_out

            @pl.when(i == 0)
            def _():
                loss_ref[...] = jnp.zeros_like(loss_ref)
                dg_ref[...] = jnp.zeros_like(dg_ref)

            gf = gf_ref[...]
            r, xh = _rms_stats(out)
            err = xh * gf - t_ref[...]
            loss_ref[...] += (0.5 / D_MODEL) * jnp.sum(err * err, axis=0, keepdims=True)
            dh, dg = _rms_bwd(err * (1.0 / D_MODEL), gf, r, xh)
            dg_ref[...] += dg
            dh_ref[...] = dh
            dhalf_ref[...] = (0.5 * dh).astype(BF16)

    tok = pl.BlockSpec((tile, D_MODEL), lambda j, i, order_ref: (i, 0))
    hid = pl.BlockSpec((1, tile, FF_CHUNK), lambda j, i, order_ref: (order_ref[j], i, 0))
    row = pl.BlockSpec((1, D_MODEL), lambda j, i, order_ref: (0, 0))
    in_last = pl.BlockSpec((tile, D_MODEL), lambda j, i, order_ref: (jnp.where(j == last, i, 0), 0))
    in_first = pl.BlockSpec((tile, D_MODEL), lambda j, i, order_ref: (jnp.where(j == 0, i, nt - 1), 0))
    tok_f32, tok_bf16, lanes = (jax.ShapeDtypeStruct((S, D_MODEL), F32), jax.ShapeDtypeStruct((S, D_MODEL), BF16),
                                jax.ShapeDtypeStruct((1, D_MODEL), F32))
    hidden = jax.ShapeDtypeStruct((N_CHIPS, S, FF_CHUNK), BF16)
    if loss_head:
        extra_specs, extra_args = [in_last, row], list(loss_head)
        head_specs, head_shapes = [in_last, in_last, row, row], [tok_f32, tok_bf16, lanes, lanes]
    else:
        extra_specs, extra_args, head_specs, head_shapes = [], [], [in_last], [tok_f32]
    return _call(
        body, name, (N_CHIPS, nt), [tok, row] + extra_specs, head_specs + [in_first, hid, hid],
        head_shapes + [tok_bf16, hidden, hidden],
        [pltpu.VMEM((3 * FF_CHUNK, D_MODEL), BF16), pltpu.SemaphoreType.DMA, pltpu.VMEM((S, D_MODEL), F32),
         pltpu.VMEM((S, D_MODEL), BF16)], 58, (h, gain, *extra_args), rider, (order,), shares_rider_refs=True)


def _ffn_bwd(chunks, d_out, n, gate, up, group, name, rider=None):
    S = n.shape[0]
    n_chunks = chunks.shape[0]
    tile = min(TOK_TILE, S)
    nt = S // tile
    half_rows = 3 * FF_CHUNK // 2
    cut = FF_CHUNK // 2

    def body(chunks_ref, do_ref, n_ref, gate_ref, up_ref, wg_ref, wu_ref, wd_ref, dn_ref, dw_ref, acc_g, acc_u, acc_d):
        j, i = pl.program_id(0), pl.program_id(1)

        @pl.when(i == 0)
        def _():
            acc_g[...] = jnp.zeros_like(acc_g)
            acc_u[...] = jnp.zeros_like(acc_u)
            acc_d[...] = jnp.zeros_like(acc_d)

        halves = [pl.ds(s * (tile // 2), tile // 2) for s in range(2)]
        dos = [do_ref[rows, :] for rows in halves]
        d_acts = [_dot_nt(do, wd_ref[0]) for do in dos]
        gs = [gate_ref[0, rows, :].astype(F32) for rows in halves]
        us = [up_ref[0, rows, :].astype(F32) for rows in halves]
        sigs = [jax.nn.sigmoid(g) for g in gs]
        silus = [g * sig for g, sig in zip(gs, sigs)]
        d_ups = [(d_act * silu).astype(BF16) for d_act, silu in zip(d_acts, silus)]
        d_gates = [(d_act * u * (sig * (1.0 + g * (1.0 - sig)))).astype(BF16) for d_act, u, sig, g in zip(d_acts, us, sigs, gs)]
        for rows, d_gate, d_up in zip(halves, d_gates, d_ups):
            dn_ref[0, rows, :] = (_dot(d_gate, wg_ref[0]) + _dot(d_up, wu_ref[0])).astype(BF16)
        d_gate, d_up = jnp.concatenate(d_gates, axis=0), jnp.concatenate(d_ups, axis=0)
        act = jnp.concatenate([(silu * u).astype(BF16) for silu, u in zip(silus, us)], axis=0)
        nn = n_ref[...]
        acc_g[...] += _dot_tn(d_gate, nn)
        acc_u[...] += _dot_tn(d_up, nn)
        acc_d[...] += _dot_tn(act, do_ref[...])

        @pl.when(i == nt - 1)
        def _():
            dw_ref[0, 0, :FF_CHUNK, :] = acc_g[...].astype(BF16)
            dw_ref[0, 0, FF_CHUNK:, :] = acc_u[:cut, :].astype(BF16)
            dw_ref[1, 0, :cut, :] = acc_u[cut:, :].astype(BF16)
            dw_ref[1, 0, cut:, :] = acc_d[...].astype(BF16)

    tok = pl.BlockSpec((tile, D_MODEL), lambda j, i, chunks_ref: (i, 0))
    hid = pl.BlockSpec((1, tile, FF_CHUNK), lambda j, i, chunks_ref: (chunks_ref[j], i, 0))
    return _call(
        body, name, (n_chunks, nt),
        [tok, tok, hid, hid]
        + [pl.BlockSpec((1, FF_CHUNK, D_MODEL), functools.partial(lambda j, i, chunks_ref, part: (chunks_ref[j], part, 0), part=part))
           for part in range(3)],
        [pl.BlockSpec((1, tile, D_MODEL), lambda j, i, chunks_ref: (j, i, 0)),
         pl.BlockSpec((2, 1, half_rows, D_MODEL), lambda j, i, chunks_ref: (0, j, 0, 0))],
        [jax.ShapeDtypeStruct((n_chunks, S, D_MODEL), BF16), jax.ShapeDtypeStruct((2, n_chunks, half_rows, D_MODEL), BF16)],
        [pltpu.VMEM((FF_CHUNK, D_MODEL), F32)] * 3, 56, (d_out, n, gate, up, group, group, group), rider, (chunks,))


def _mix_in(h, gain, w_in, rc, rs1, rs2, name):
    S = h.shape[0]
    tile = min(TOK_TILE, S)

    def body(h_ref, g_ref, w_ref, c_ref, s1_ref, s2_ref, n_ref, q_ref, k_ref, v_ref, pc_ref):
        _, xh = _rms_stats(h_ref[...])
        n = (xh * g_ref[...]).astype(BF16)
        n_ref[...] = n
        u = _dot_nt(n, w_ref[...])
        c, s1, s2 = c_ref[...], s1_ref[...], s2_ref[...]
        q_ref[...] = jnp.concatenate([(_rope(u[:, hd * LANES:(hd + 1) * LANES], c, s1, s2) * SCALE).astype(BF16)
                                      for hd in range(N_HEADS)], axis=1)
        k_ref[...] = _rope(u[:, Q_PAD:Q_PAD + KV_WIDTH], c, s1, s2).astype(BF16)
        v_ref[...] = u[:, Q_PAD + KV_WIDTH:Q_PAD + 2 * KV_WIDTH].astype(BF16)
        pc_ref[...] = u[:, Q_PAD + 2 * KV_WIDTH:]

    return pl.pallas_call(
        body, name=name, grid=(S // tile,),
        in_specs=[_rows(tile, D_MODEL), _full((1, D_MODEL)), _full((U_PAD, D_MODEL)),
                  _rows(tile, LANES), _rows(tile, LANES), _rows(tile, LANES)],
        out_specs=[_rows(tile, D_MODEL), _rows(tile, Q_PAD), _rows(tile, KV_WIDTH), _rows(tile, KV_WIDTH),
                   _rows(tile, POOL_WIDTH)],
        out_shape=[jax.ShapeDtypeStruct((S, D_MODEL), BF16), jax.ShapeDtypeStruct((S, Q_PAD), BF16),
                   jax.ShapeDtypeStruct((S, KV_WIDTH), BF16), jax.ShapeDtypeStruct((S, KV_WIDTH), BF16),
                   jax.ShapeDtypeStruct((S, POOL_WIDTH), F32)],
        compiler_params=_params(("parallel",), 40),
    )(h, gain, w_in, rc, rs1, rs2)


def _band_start(i, S):
    return pl.multiple_of(jnp.clip((i - 1) * BLOCK, 0, S - BAND), BLOCK)


def _window_bias(off):
    r = lax.broadcasted_iota(jnp.int32, (BLOCK, 1), 0)
    c = lax.broadcasted_iota(jnp.int32, (1, BAND), 1)
    return jnp.where(jnp.abs(off + r - c) <= WINDOW, 0.0, NEG).astype(F32)


def _softmax_parts(qh, kb, bias, sink_h):
    s = _dot_nt(qh, kb) + bias
    m = jnp.maximum(jnp.max(s, axis=-1, keepdims=True), sink_h)
    p = jnp.exp(s - m)
    es = jnp.exp(sink_h - m)
    return p, es, 1.0 / (jnp.sum(p, axis=-1, keepdims=True) + es)


def _pool_matrix(t0, start, S, w):
    r = lax.broadcasted_iota(jnp.int32, (BLOCK, 1), 0) + t0
    c = lax.broadcasted_iota(jnp.int32, (1, BAND), 1) + start
    half = w // 2

    def window(lo, hi):
        a = jnp.maximum(lo, 0)
        b = jnp.minimum(hi + 1, S)
        return jnp.where((c >= a) & (c < b), 1.0 / (b - a).astype(F32), 0.0)

    return (0.5 * (window(r - half, r + half - 1) + window(r - half + 1, r + half))).astype(BF16)


def _pool_matrices(S):
    blocks = ((0, 0), (BLOCK, 0), (S - BLOCK, S - BAND))
    return jnp.stack([jnp.stack([_pool_matrix(t0, start, S, w) for w in POOL_WINDOWS]) for t0, start in blocks])


def _pool_spec(nb):
    return pl.BlockSpec((1, N_POOL, BLOCK, BAND), lambda i, *_: (jnp.where(i == 0, 0, jnp.where(i == nb - 1, 2, 1)), 0, 0, 0))


def _mix_core_fwd(q, k, v, pc, sink, pool_m, pool_w, pool_scale, name):
    S = q.shape[0]
    nb = S // BLOCK

    def body(sink_ref, q_ref, k_ref, v_ref, pc_ref, pm_ref, pw_ref, ps_ref, a_ref, p_ref):
        i = pl.program_id(0)
        start = _band_start(i, S)
        band = pl.ds(start, BAND)
        bias = _window_bias(i * BLOCK - start)
        kb, vb = k_ref[band, :], v_ref[band, :]
        hs = range(N_HEADS)
        ss = [_dot_nt(q_ref[:, hd * LANES:(hd + 1) * LANES], kb) + bias for hd in hs]
        ms = [jnp.maximum(jnp.max(ss[hd], axis=-1, keepdims=True), sink_ref[0, hd]) for hd in hs]
        ps = [jnp.exp(ss[hd] - ms[hd]) for hd in hs]
        invs = [1.0 / (jnp.sum(ps[hd], axis=-1, keepdims=True) + jnp.exp(sink_ref[0, hd] - ms[hd])) for hd in hs]
        outs = [_dot(ps[hd].astype(BF16), vb) for hd in hs]
        a_ref[...] = jnp.concatenate([(outs[hd] * invs[hd]).astype(BF16) for hd in hs], axis=1)
        centre = pl.ds(pl.multiple_of(i * BLOCK, BLOCK), BLOCK)
        gs = range(N_POOL)
        sl = [slice(g * POOL_GROUP, (g + 1) * POOL_GROUP) for g in gs]
        means = [_dot(pm_ref[0, g], pc_ref[band, sl[g]].astype(BF16)) for g in gs]
        devs = [(means[g] - pc_ref[centre, sl[g]]).astype(BF16) for g in gs]
        p_ref[...] = (jnp.concatenate([_dot(devs[g], pw_ref[g]) for g in gs], axis=1) * ps_ref[...]).astype(BF16)

    return _call(
        body, name, (nb,),
        [pl.BlockSpec(memory_space=pltpu.SMEM), _rows(BLOCK, Q_PAD), _full((S, KV_WIDTH)), _full((S, KV_WIDTH)),
         _full((S, POOL_WIDTH)), _pool_spec(nb), _full((N_POOL, POOL_GROUP, POOL_GROUP)), _full((1, POOL_WIDTH))],
        [_rows(BLOCK, Q_PAD), _rows(BLOCK, POOL_WIDTH)],
        [jax.ShapeDtypeStruct((S, Q_PAD), BF16), jax.ShapeDtypeStruct((S, POOL_WIDTH), BF16)],
        [], 40, (sink, q, k, v, pc, pool_m, pool_w, pool_scale))


def _mix_core_bwd(q, k, v, pc, da, dp, sink, pool_m, pool_w, pool_scale, rc, rs1, rs2, name, rider=None):
    S = q.shape[0]
    nb = S // BLOCK

    def body(sink_ref, q_ref, k_ref, v_ref, pc_ref, da_ref, dp_ref, pm_ref, pw_ref, ps_ref, c_ref, s1_ref, s2_ref,
             dq_ref, dk_ref, dv_ref, dpc_ref, dsink_ref, dpw_ref, dps_ref):
        i = pl.program_id(0)

        @pl.when(i == 0)
        def _():
            dk_ref[...] = jnp.zeros_like(dk_ref)
            dv_ref[...] = jnp.zeros_like(dv_ref)
            dpc_ref[...] = jnp.zeros_like(dpc_ref)
            dsink_ref[...] = jnp.zeros_like(dsink_ref)
            dpw_ref[...] = jnp.zeros_like(dpw_ref)
            dps_ref[...] = jnp.zeros_like(dps_ref)

        start = _band_start(i, S)
        band = pl.ds(start, BAND)
        bias = _window_bias(i * BLOCK - start)
        kb, vb = k_ref[band, :], v_ref[band, :]
        c, s1, s2 = c_ref[...], s1_ref[...], s2_ref[...]
        lane = lax.broadcasted_iota(jnp.int32, (1, LANES), 1)
        hs = range(N_HEADS)
        qs = [q_ref[:, hd * LANES:(hd + 1) * LANES] for hd in hs]
        das = [da_ref[:, hd * LANES:(hd + 1) * LANES] for hd in hs]
        ss = [_dot_nt(qs[hd], kb) + bias for hd in hs]
        d_probs = [_dot_nt(das[hd], vb) for hd in hs]
        ms = [jnp.maximum(jnp.max(ss[hd], axis=-1, keepdims=True), sink_ref[0, hd]) for hd in hs]
        ps = [jnp.exp(ss[hd] - ms[hd]) for hd in hs]
        ess = [jnp.exp(sink_ref[0, hd] - ms[hd]) for hd in hs]
        invs = [1.0 / (jnp.sum(ps[hd], axis=-1, keepdims=True) + ess[hd]) for hd in hs]
        probs = [ps[hd] * invs[hd] for hd in hs]
        deltas = [jnp.sum(probs[hd] * d_probs[hd], axis=-1, keepdims=True) for hd in hs]
        d_ss = [(probs[hd] * (d_probs[hd] - deltas[hd])).astype(BF16) for hd in hs]
        dqs = [_dot(d_ss[hd], kb) for hd in hs]
        dq_ref[...] = jnp.concatenate([_rope_bwd(dqs[hd] * SCALE, c, s1, s2).astype(BF16) for hd in hs], axis=1)
        dks = [_dot_tn(d_ss[hd], qs[hd]) for hd in hs]
        dvs = [_dot_tn(probs[hd].astype(BF16), das[hd]) for hd in hs]
        dk_ref[band, :] += functools.reduce(lambda a, b: a + b, dks)
        dv_ref[band, :] += functools.reduce(lambda a, b: a + b, dvs)
        dsink_ref[...] += functools.reduce(lambda a, b: a + b, [
            jnp.where(lane == hd, -jnp.sum(ess[hd] * invs[hd] * deltas[hd], axis=0, keepdims=True), 0.0) for hd in hs])

        centre = pl.ds(pl.multiple_of(i * BLOCK, BLOCK), BLOCK)
        gs = range(N_POOL)
        sl = [slice(g * POOL_GROUP, (g + 1) * POOL_GROUP) for g in gs]
        devs = [(_dot(pm_ref[0, g], pc_ref[band, sl[g]].astype(BF16)) - pc_ref[centre, sl[g]]).astype(BF16) for g in gs]
        dys = [dp_ref[:, sl[g]].astype(F32) for g in gs]
        zs = [_dot(devs[g], pw_ref[g]) for g in gs]
        dzs = [(dys[g] * ps_ref[:, sl[g]]).astype(BF16) for g in gs]
        d_devs = [_dot_nt(dzs[g], pw_ref[g]) for g in gs]
        dps_ref[...] += jnp.concatenate([jnp.sum(dys[g] * zs[g], axis=0, keepdims=True) for g in gs], axis=1)
        for g in gs:
            dpw_ref[g] += _dot_tn(devs[g], dzs[g])
        dpc_ref[band, :] += jnp.concatenate([_dot_tn(pm_ref[0, g], d_devs[g].astype(BF16)) for g in gs], axis=1)
        dpc_ref[centre, :] -= jnp.concatenate(d_devs, axis=1)

    return _call(
        body, name, (nb,),
        [pl.BlockSpec(memory_space=pltpu.SMEM), _rows(BLOCK, Q_PAD), _full((S, KV_WIDTH)), _full((S, KV_WIDTH)),
         _full((S, POOL_WIDTH)), _rows(BLOCK, Q_PAD), _rows(BLOCK, POOL_WIDTH), _pool_spec(nb),
         _full((N_POOL, POOL_GROUP, POOL_GROUP)), _full((1, POOL_WIDTH)),
         _rows(BLOCK, LANES), _rows(BLOCK, LANES), _rows(BLOCK, LANES)],
        [_rows(BLOCK, Q_PAD), _full((S, KV_WIDTH)), _full((S, KV_WIDTH)), _full((S, POOL_WIDTH)),
         _full((1, LANES)), _full((N_POOL, POOL_GROUP, POOL_GROUP)), _full((1, POOL_WIDTH))],
        [jax.ShapeDtypeStruct((S, Q_PAD), BF16), jax.ShapeDtypeStruct((S, KV_WIDTH), F32),
         jax.ShapeDtypeStruct((S, KV_WIDTH), F32), jax.ShapeDtypeStruct((S, POOL_WIDTH), F32),
         jax.ShapeDtypeStruct((1, LANES), F32), jax.ShapeDtypeStruct((N_POOL, POOL_GROUP, POOL_GROUP), F32),
         jax.ShapeDtypeStruct((1, POOL_WIDTH), F32)],
        [], 56, (sink, q, k, v, pc, da, dp, pool_m, pool_w, pool_scale, rc, rs1, rs2), rider)


def _mix_out(h, a, p, wa, wp, name):
    S = h.shape[0]
    tile = min(TOK_TILE, S)

    def body(h_ref, a_ref, p_ref, wa_ref, wp_ref, o_ref):
        o_ref[...] = h_ref[...] + _dot(a_ref[...], wa_ref[...]) + _dot(p_ref[...], wp_ref[...])

    return pl.pallas_call(
        body, name=name, grid=(S // tile,),
        in_specs=[_rows(tile, D_MODEL), _rows(tile, Q_PAD), _rows(tile, POOL_WIDTH),
                  _full((Q_PAD, D_MODEL)), _full((POOL_WIDTH, D_MODEL))],
        out_specs=_rows(tile, D_MODEL),
        out_shape=jax.ShapeDtypeStruct((S, D_MODEL), F32),
        compiler_params=_params(("parallel",), 40),
    )(h, a, p, wa, wp)


def _mix_out_bwd(dh_out, dn, h, gain, a, p, wa, wp, name, rider=None):
    S = h.shape[0]
    tile = min(TOK_TILE, S)

    def body(do_ref, dn_ref, h_ref, g_ref, a_ref, p_ref, wa_ref, wp_ref, dh_ref, da_ref, dp_ref, dwa_ref, dwp_ref, dg_ref):
        @pl.when(pl.program_id(0) == 0)
        def _():
            dwa_ref[...] = jnp.zeros_like(dwa_ref)
            dwp_ref[...] = jnp.zeros_like(dwp_ref)
            dg_ref[...] = jnp.zeros_like(dg_ref)

        r, xh = _rms_stats(h_ref[...])
        dnorm, dg = _rms_bwd(_sum_chunks([dn_ref]), g_ref[...], r, xh)
        dh = do_ref[...] + dnorm
        dg_ref[...] += dg
        dh_ref[...] = dh
        dhb = dh.astype(BF16)
        da_ref[...] = _dot_nt(dhb, wa_ref[...]).astype(BF16)
        dp_ref[...] = _dot_nt(dhb, wp_ref[...]).astype(BF16)
        dwa_ref[...] += _dot_tn(a_ref[...], dhb)
        dwp_ref[...] += _dot_tn(p_ref[...], dhb)

    return _call(
        body, name, (S // tile,),
        [_rows(tile, D_MODEL), _chunk_rows(tile, dn.shape[0]), _rows(tile, D_MODEL), _full((1, D_MODEL)),
         _rows(tile, Q_PAD), _rows(tile, POOL_WIDTH), _full((Q_PAD, D_MODEL)), _full((POOL_WIDTH, D_MODEL))],
        [_rows(tile, D_MODEL), _rows(tile, Q_PAD), _rows(tile, POOL_WIDTH),
         _full((Q_PAD, D_MODEL)), _full((POOL_WIDTH, D_MODEL)), _full((1, D_MODEL))],
        [jax.ShapeDtypeStruct((S, D_MODEL), F32), jax.ShapeDtypeStruct((S, Q_PAD), BF16),
         jax.ShapeDtypeStruct((S, POOL_WIDTH), BF16), jax.ShapeDtypeStruct((Q_PAD, D_MODEL), F32),
         jax.ShapeDtypeStruct((POOL_WIDTH, D_MODEL), F32), jax.ShapeDtypeStruct((1, D_MODEL), F32)],
        [], 48, (dh_out, dn, h, gain, a, p, wa, wp), rider)


def _mix_in_bwd(dh_out, h, gain, n, dq, dk, dv, dpc, rc, rs1, rs2, w_in, name, rider=None):
    S = h.shape[0]
    tile = min(TOK_TILE, S)

    def body(do_ref, h_ref, g_ref, n_ref, dq_ref, dk_ref, dv_ref, dpc_ref, c_ref, s1_ref, s2_ref, w_ref,
             dh_ref, dhalf_ref, dw_ref, dg_ref):
        @pl.when(pl.program_id(0) == 0)
        def _():
            dw_ref[...] = jnp.zeros_like(dw_ref)
            dg_ref[...] = jnp.zeros_like(dg_ref)

        dk = _rope_bwd(dk_ref[...], c_ref[...], s1_ref[...], s2_ref[...]).astype(BF16)
        du = jnp.concatenate([dq_ref[...], dk, dv_ref[...].astype(BF16), dpc_ref[...].astype(BF16)], axis=1)
        dn = _dot(du, w_ref[...])
        dw_ref[...] += _dot_tn(du, n_ref[...])
        r, xh = _rms_stats(h_ref[...])
        dnorm, dg = _rms_bwd(dn, g_ref[...], r, xh)
        dh = do_ref[...] + dnorm
        dg_ref[...] += dg
        dh_ref[...] = dh
        dhalf_ref[...] = (0.5 * dh).astype(BF16)

    return _call(
        body, name, (S // tile,),
        [_rows(tile, D_MODEL), _rows(tile, D_MODEL), _full((1, D_MODEL)), _rows(tile, D_MODEL),
         _rows(tile, Q_PAD), _rows(tile, KV_WIDTH), _rows(tile, KV_WIDTH), _rows(tile, POOL_WIDTH),
         _rows(tile, LANES), _rows(tile, LANES), _rows(tile, LANES), _full((U_PAD, D_MODEL))],
        [_rows(tile, D_MODEL), _rows(tile, D_MODEL), _full((U_PAD, D_MODEL)), _full((1, D_MODEL))],
        [jax.ShapeDtypeStruct((S, D_MODEL), F32), jax.ShapeDtypeStruct((S, D_MODEL), BF16),
         jax.ShapeDtypeStruct((U_PAD, D_MODEL), F32), jax.ShapeDtypeStruct((1, D_MODEL), F32)],
        [], 56, (dh_out, h, gain, n, dq, dk, dv, dpc, rc, rs1, rs2, w_in), rider)


def _norm_bwd(dh_out, dns, h, gain, name, rider=None):
    S = h.shape[0]
    tile = min(TOK_TILE, S)
    n = len(dns)

    def body(do_ref, *refs):
        h_ref, g_ref, dh_ref, dg_ref = refs[n:]

        @pl.when(pl.program_id(0) == 0)
        def _():
            dg_ref[...] = jnp.zeros_like(dg_ref)

        r, xh = _rms_stats(h_ref[...])
        dnorm, dg = _rms_bwd(_sum_chunks(refs[:n]), g_ref[...], r, xh)
        dg_ref[...] += dg
        dh_ref[...] = do_ref[...] + dnorm

    return _call(
        body, name, (S // tile,),
        [_rows(tile, D_MODEL)] + [_chunk_rows(tile, dn.shape[0]) for dn in dns] + [_rows(tile, D_MODEL), _full((1, D_MODEL))],
        [_rows(tile, D_MODEL), _full((1, D_MODEL))],
        [jax.ShapeDtypeStruct((S, D_MODEL), F32), jax.ShapeDtypeStruct((1, D_MODEL), F32)],
        [], 40, (dh_out, *dns, h, gain), rider)


def _rope_tables(S):
    half = ROTARY_DIM // 2
    inv_freq = ROPE_THETA ** (-jnp.arange(0, ROTARY_DIM, 2, dtype=F32) / ROTARY_DIM)
    dim = jnp.arange(LANES) % HEAD_DIM
    ang = jnp.arange(S, dtype=F32)[:, None] * inv_freq[dim % half][None, :]
    lo, hi = (dim < half)[None, :], ((dim >= half) & (dim < ROTARY_DIM))[None, :]
    c = jnp.where(lo | hi, jnp.cos(ang), 1.0)
    s1 = jnp.where(lo, -jnp.sin(ang), 0.0)
    s2 = jnp.where(hi, jnp.sin(ang), 0.0)
    return c, s1, s2


def _pad_heads(w, axis):
    w = jnp.moveaxis(w, axis, 0)
    heads = w.reshape((N_HEADS, HEAD_DIM) + w.shape[1:])
    zero = jnp.zeros_like(heads)
    first = (jnp.arange(N_HEADS) < Q_PER_KV).reshape((N_HEADS, 1) + (1,) * (w.ndim - 1))
    lo = jnp.where(first, heads, zero)
    hi = jnp.where(first, zero, heads)
    padded = jnp.concatenate([lo, hi], axis=1).reshape((Q_PAD,) + w.shape[1:])
    return jnp.moveaxis(padded, 0, axis)


def _unpad_heads(w, axis):
    w = jnp.moveaxis(w, axis, 0)
    groups = w.reshape((N_HEADS, 2, HEAD_DIM) + w.shape[1:])
    first = (jnp.arange(N_HEADS) < Q_PER_KV).reshape((N_HEADS, 1) + (1,) * (w.ndim - 1))
    heads = jnp.where(first, groups[:, 0], groups[:, 1]).reshape((ATTN_WIDTH,) + w.shape[1:])
    return jnp.moveaxis(heads, 0, axis)


IN_ROWS = IN_WIDTH // N_CHIPS
OUT_ROWS = (ATTN_WIDTH + POOL_WIDTH) // N_CHIPS
MIX_ROWS = IN_ROWS + OUT_ROWS
FFN_ROWS = 3 * FF_CHUNK


def _step(x, target, bufs, small, place):
    S = x.shape[0]
    rc, rs1, rs2 = _rope_tables(S)
    mine = place[1]
    order = jnp.stack([mine, mine ^ 2, mine ^ 1, mine ^ 3])
    h1, n1, gate1, up1, ffn1, mix, ffn2 = _ffn_fwd(
        order, x, small["ffn1_norm"], "ffn1_fwd",
        _merge(_merge(_allgather(bufs[:1], in_passes=0), _allgather(bufs[1:2])), _allgather(bufs[2:], peers=X_Y_DIAGONAL[:2])), 0)
    w_in_t = mix[:, :IN_ROWS].reshape(IN_WIDTH, D_MODEL)
    w_in_pad = jnp.concatenate([_pad_heads(w_in_t[:ATTN_WIDTH], 0), w_in_t[ATTN_WIDTH:]], axis=0)
    w_out = mix[:, IN_ROWS:].reshape(ATTN_WIDTH + POOL_WIDTH, D_MODEL)
    wa = _pad_heads(w_out[:ATTN_WIDTH], 0)
    wp = w_out[ATTN_WIDTH:]
    pool_w = small["pool_w"].astype(BF16)

    n2, q, k, v, pc = _mix_in(h1, small["mix_norm"], w_in_pad, rc, rs1, rs2, "mix_in")
    pool_m = _pool_matrices(S)
    a, p = _mix_core_fwd(q, k, v, pc, small["sink_logits"], pool_m, pool_w, small["pool_scale"], "mix_core_fwd")
    h2 = _mix_out(h1, a, p, wa, wp, "mix_out")
    dh3, dhalf3, loss_lanes, d_final, n3, gate2, up2, ffn2 = _ffn_fwd(
        order, h2, small["ffn2_norm"], "ffn2_fwd", _allgather([ffn2], peers=X_Y_DIAGONAL[2:], in_passes=0), 0,
        loss_head=(target, small["final_norm"]))

    dn3, d_ffn2 = _ffn_bwd(jnp.arange(N_CHIPS, dtype=jnp.int32), dhalf3, n3, gate2, up2, ffn2, "ffn2_bwd")
    dh2, da, dp, dwa, dwp, d_ffn2_norm, received = _mix_out_bwd(dh3, dn3, h2, small["ffn2_norm"], a, p, wa, wp, "mix_out_bwd",
                                                                _sibling_exchange([d_ffn2]))
    pair = _pair_sum(place, d_ffn2, received, "grad_pair_sum_ffn2")
    dq, dk, dv, dpc, dsink, dpool_w, dpool_scale, stack = _mix_core_bwd(
        q, k, v, pc, da, dp, small["sink_logits"], pool_m, pool_w, small["pool_scale"], rc, rs1, rs2, "mix_core_bwd",
        _scatter([pair], peers=X_Y_DIAGONAL[:2]))
    dh1, dhalf1, dw_in_pad, d_mix_norm, stack = _mix_in_bwd(dh2, h1, small["mix_norm"], n2, dq, dk, dv, dpc, rc, rs1, rs2,
                                                            w_in_pad, "mix_in_bwd",
                                                            _scatter([pair], peers=X_Y_DIAGONAL[2:], stacks=[stack]))
    reduced_ffn2 = _chip_sum(place, pair, stack, 2, "grad_chip_sum_ffn2")
    dw_in_t = jnp.concatenate([_unpad_heads(dw_in_pad[:Q_PAD], 0), dw_in_pad[Q_PAD:]], axis=0)
    dw_out = jnp.concatenate([_unpad_heads(dwa, 0), dwp], axis=0)
    d_mix = jnp.concatenate([dw_in_t.reshape(N_CHIPS, IN_ROWS, D_MODEL), dw_out.reshape(N_CHIPS, OUT_ROWS, D_MODEL)], axis=1)
    d_mix = jnp.transpose(d_mix.reshape(N_CHIPS, 2, MIX_ROWS // 2, D_MODEL), (1, 0, 2, 3)).astype(BF16)
    small_g = {"ffn1_norm": jnp.zeros_like(d_mix_norm), "mix_norm": d_mix_norm, "ffn2_norm": d_ffn2_norm,
               "final_norm": d_final, "pool_scale": dpool_scale, "sink_logits": dsink[:, :N_HEADS], "pool_w": dpool_w}
    loss_row = jnp.sum(loss_lanes.reshape(D_MODEL // LANES, LANES), axis=0, keepdims=True)
    small_early = _pack_small(small_g, loss_row)

    chunk = [(place[1:] + 1 + p) % N_CHIPS for p in range(N_CHIPS)]
    ffn1_bwd = functools.partial(_ffn_bwd, d_out=dhalf1, n=n1, gate=gate1, up=up1, group=ffn1)
    dn_a, dw_a, recv_mix, small_all, g_ffn2 = ffn1_bwd(
        chunk[0], name="ffn1_bwd_0",
        rider=_merge(_merge(_sibling_exchange([d_mix]), _small_allgather(small_early)), _sibling_share([reduced_ffn2])))
    pair_mix = _pair_sum(place, d_mix, recv_mix, "grad_pair_sum_mix")
    dn_b, dw_b, recv_a, stack_mix = ffn1_bwd(chunk[1], name="ffn1_bwd_1",
                                             rider=_merge(_sibling_exchange([dw_a]), _scatter([pair_mix])))
    pair_a = _pair_sum(place, dw_a, recv_a, "grad_pair_sum_ffn1_0")
    reduced_mix = _chip_sum(place, pair_mix, stack_mix, 2, "grad_chip_sum_mix")
    dn_c, dw_c, recv_b, stack, g_mix = ffn1_bwd(
        chunk[2], name="ffn1_bwd_2",
        rider=_merge(_merge(_sibling_exchange([dw_b]), _scatter_step(pair_a, stack, 0)), _sibling_share([reduced_mix])))
    pair_b = _pair_sum(place, dw_b, recv_b, "grad_pair_sum_ffn1_1")
    dn_d, dw_d, recv_c, stack = ffn1_bwd(chunk[3], name="ffn1_bwd_3",
                                         rider=_merge(_sibling_exchange([dw_c]), _scatter_step(pair_b, stack, 1)))
    pair_c = _pair_sum(place, dw_c, recv_c, "grad_pair_sum_ffn1_2")
    grad_x, d_ffn1_norm, recv_d, stack = _norm_bwd(dh1, [dn_a, dn_b, dn_c, dn_d], x, small["ffn1_norm"], "norm1_bwd",
                                                   _merge(_sibling_exchange([dw_d]), _scatter_step(pair_c, stack, 2)))
    pair_d = _pair_sum(place, dw_d, recv_d, "grad_pair_sum_ffn1_3")
    reduced_ffn1 = _chip_sum(place, pair_d, stack, 2, "grad_chip_sum_ffn1")
    g_ffn1, gains = _comm_call(_merge(_sibling_share([reduced_ffn1]), _small_allgather(d_ffn1_norm.reshape(-1, LANES))),
                               "grad_share_tail")
    gain_sum = _sum_leading(gains, 1, "gain_grad_sum")
    small_sum = jnp.concatenate([gain_sum, _sum_leading(small_all, 1, "small_grad_sum")[gain_sum.shape[0]:]], axis=0)
    return jnp.sum(small_sum[SMALL_ROWS - 1]), grad_x, [g.reshape(-1, D_MODEL) for g in (g_ffn1, g_mix, g_ffn2)], small_sum


GROUPS =(("ffn1_w_gate", "ffn1_w_up", "ffn1_w_down"), ("w_in", "w_out"), ("ffn2_w_gate", "ffn2_w_up", "ffn2_w_down"))
TRANSPOSED = ("ffn1_w_gate", "ffn1_w_up", "w_in", "ffn2_w_gate", "ffn2_w_up")


def _place():
    x, y, c = lax.axis_index("x"), lax.axis_index("y"), lax.axis_index("c")
    chips = [(1 - x, y), (x, 1 - y), (1 - x, 1 - y)]
    return x, y, c, chips


def _remote(src, dst, send_sem, recv_sem, to):
    return pltpu.make_async_remote_copy(src_ref=src, dst_ref=dst, send_sem=send_sem, recv_sem=recv_sem,
                                        device_id=to, device_id_type=MESH)


def _pack(chip, members, name):
    rows = members[0].shape[0]
    n = len(members)

    def body(chip_ref, *refs):
        ins, out_ref, buf, sems = refs[:n], refs[n], refs[n + 1], refs[n + 2]
        copies = [pltpu.make_async_copy(ins[k], buf.at[k], sems.at[k]) for k in range(n)]
        for cp in copies:
            cp.start()
        for k in range(n):
            copies[k].wait()
            out_ref[0, k * rows:(k + 1) * rows, :] = buf[k].astype(BF16)

    return pl.pallas_call(
        body, name=name,
        grid_spec=pltpu.PrefetchScalarGridSpec(
            num_scalar_prefetch=1, grid=(1,),
            in_specs=[HBM_SPEC] * n,
            out_specs=pl.BlockSpec((1, n * rows, D_MODEL), lambda k, chip_ref: (chip_ref[0], 0, 0)),
            scratch_shapes=[pltpu.VMEM((n, rows, D_MODEL), F32), pltpu.SemaphoreType.DMA((n,))]),
        out_shape=jax.ShapeDtypeStruct((N_CHIPS, n * rows, D_MODEL), BF16),
        compiler_params=_params(("arbitrary",), 40),
    )(chip, *members)


def _same(arrays):
    return [jax.ShapeDtypeStruct(a.shape, a.dtype) for a in arrays]


X_Y_DIAGONAL = (0, 1, 2)


def _allgather(bufs, peers=X_Y_DIAGONAL, in_passes=None):
    n = len(bufs)

    def copy(kind, outs, send_sems, recv_sems, a, k):
        x, y, c, chips = _place()
        half = bufs[a].shape[1] // 2

        def rows(slot, core):
            return outs[a].at[slot, pl.ds(pl.multiple_of(core * half, 16), half)]

        me, slot = 2 * x + y, 2 * chips[k][0] + chips[k][1]
        over_ici = (send_sems.at[6 * a + k], recv_sems.at[6 * a + k])
        over_d2d = (send_sems.at[6 * a + 3 + k], recv_sems.at[6 * a + 3 + k])
        if kind == "first":
            return _remote(rows(me, c), rows(me, c), *over_ici, (*chips[k], c))
        if kind == "landed":
            return _remote(rows(me, c), rows(slot, c), *over_ici, (*chips[k], c))
        if kind == "passed":
            return _remote(rows(slot, c), rows(slot, c), *over_d2d, (x, y, 1 - c))
        return _remote(rows(me, c), rows(slot, 1 - c), *over_d2d, (x, y, 1 - c))

    def arrive(pairs):
        def hook(ins, outs, sems):
            for a, k in pairs:
                copy("landed", outs, *sems, a, k).wait_recv()
                copy("passed", outs, *sems, a, k).start()
            for a, k in pairs:
                copy("handed", outs, *sems, a, k).wait_recv()
        return hook

    everything = [(a, k) for a in range(n) for k in peers]
    early = [(a, k) for a, k in everything if a == in_passes]

    def start(ins, outs, sems):
        for a, k in everything:
            copy("first", outs, *sems, a, k).start()

    def finish(ins, outs, sems):
        arrive([pair for pair in everything if pair not in early])(ins, outs, sems)
        for a, k in everything:
            copy("first", outs, *sems, a, k).wait_send()
            copy("passed", outs, *sems, a, k).wait_send()

    hooks = tuple(((k + 1, 0), arrive([(a, k)])) for a, k in early)
    return _Rider(list(bufs), _same(bufs), {a: a for a in range(n)},
                  [pltpu.SemaphoreType.DMA((6 * n,)), pltpu.SemaphoreType.DMA((6 * n,))], start, finish, hooks)


def _sibling_exchange(parts):
    n = len(parts)

    def copies(ins, outs, send_sems, recv_sems):
        x, y, c, _ = _place()
        return [_remote(ins[a].at[1 - c], outs[a], send_sems.at[a], recv_sems.at[a], (x, y, 1 - c)) for a in range(n)]

    def start(ins, outs, sems):
        for cp in copies(ins, outs, *sems):
            cp.start()

    def finish(ins, outs, sems):
        for cp in copies(ins, outs, *sems):
            cp.wait_recv()
            cp.wait_send()

    return _Rider(list(parts), [jax.ShapeDtypeStruct(p.shape[1:], p.dtype) for p in parts], {},
                  [pltpu.SemaphoreType.DMA((n,)), pltpu.SemaphoreType.DMA((n,))], start, finish)


def _small_allgather(small):
    flips = [(fx, fy, fc) for fx in range(2) for fy in range(2) for fc in range(2)][1:]

    def copies(small_ref, gather_ref, send_sems, recv_sems, local_sem, started_only=False):
        x, y, c, _ = _place()
        me = 4 * x + 2 * y + c
        peers = [((1 - x) if fx else x, (1 - y) if fy else y, (1 - c) if fc else c) for fx, fy, fc in flips]
        own = pltpu.make_async_copy(small_ref, gather_ref.at[me], local_sem)
        sent = [_remote(small_ref, gather_ref.at[me], send_sems.at[k], recv_sems.at[k], peer) for k, peer in enumerate(peers)]
        if started_only:
            return own, sent
        landed = [_remote(small_ref, gather_ref.at[4 * px + 2 * py + pc], send_sems.at[k], recv_sems.at[k], (px, py, pc))
                  for k, (px, py, pc) in enumerate(peers)]
        return own, sent, landed

    def start(ins, outs, sems):
        own, sent = copies(ins[0], outs[0], *sems, started_only=True)
        own.start()
        for cp in sent:
            cp.start()

    def finish(ins, outs, sems):
        own, sent, landed = copies(ins[0], outs[0], *sems)
        for cp in landed:
            cp.wait_recv()
        for cp in sent:
            cp.wait_send()
        own.wait()

    return _Rider([small], [jax.ShapeDtypeStruct((2 * N_CHIPS,) + small.shape, small.dtype)], {},
                  [pltpu.SemaphoreType.DMA((7,)), pltpu.SemaphoreType.DMA((7,)), pltpu.SemaphoreType.DMA], start, finish)


def _merge(a, b):
    na, nao, nas = len(a.operands), len(a.out_shapes), len(a.scratch)

    def start(ins, outs, sems):
        a.start(ins[:na], outs[:nao], sems[:nas])
        b.start(ins[na:], outs[nao:], sems[nas:])

    def finish(ins, outs, sems):
        a.finish(ins[:na], outs[:nao], sems[:nas])
        b.finish(ins[na:], outs[nao:], sems[nas:])

    def of_a(fn):
        return lambda ins, outs, sems: fn(ins[:na], outs[:nao], sems[:nas])

    def of_b(fn):
        return lambda ins, outs, sems: fn(ins[na:], outs[nao:], sems[nas:])

    aliases = {**a.aliases, **{na + k: nao + v for k, v in b.aliases.items()}}
    hooks = tuple((at, of_a(fn)) for at, fn in a.hooks) + tuple((at, of_b(fn)) for at, fn in b.hooks)
    return _Rider(a.operands + b.operands, a.out_shapes + b.out_shapes, aliases, a.scratch + b.scratch, start, finish, hooks)


def _scatter_step(pair, stack, step):
    def copies(pair_ref, stack_ref, send_sem, recv_sem, started_only=False):
        x, y, c, _ = _place()
        me = 2 * x + y
        to = (me + 1 + step) % N_CHIPS
        frm = (me + N_CHIPS - 1 - step) % N_CHIPS
        sent = _remote(pair_ref.at[0], stack_ref.at[me], send_sem, recv_sem, (to // 2, to % 2, c))
        if started_only:
            return sent
        landed = _remote(pair_ref.at[0], stack_ref.at[frm], send_sem, recv_sem, (frm // 2, frm % 2, c))
        return sent, landed

    def start(ins, outs, sems):
        copies(ins[0], outs[0], *sems, started_only=True).start()

    def finish(ins, outs, sems):
        sent, landed = copies(ins[0], outs[0], *sems)
        landed.wait_recv()
        sent.wait_send()

    return _Rider([pair, stack], _same([stack]), {1: 0}, [pltpu.SemaphoreType.DMA, pltpu.SemaphoreType.DMA], start, finish)


def _scatter(sums, peers=X_Y_DIAGONAL, stacks=None):
    n = len(sums)

    def copies(ins, outs, send_sems, recv_sems, started_only=False):
        x, y, c, chips = _place()
        me = 2 * x + y
        slots = [2 * cx + cy for cx, cy in chips]
        sent = [_remote(ins[a].at[slots[k]], outs[a].at[me], send_sems.at[3 * a + k], recv_sems.at[3 * a + k], (*chips[k], c))
                for a in range(n) for k in peers]
        if started_only:
            return sent
        landed = [_remote(ins[a].at[slots[k]], outs[a].at[slots[k]], send_sems.at[3 * a + k], recv_sems.at[3 * a + k],
                          (*chips[k], c)) for a in range(n) for k in peers]
        return sent, landed

    def start(ins, outs, sems):
        for cp in copies(ins, outs, *sems, started_only=True):
            cp.start()

    def finish(ins, outs, sems):
        sent, landed = copies(ins, outs, *sems)
        for cp in landed:
            cp.wait_recv()
        for cp in sent:
            cp.wait_send()

    sems = [pltpu.SemaphoreType.DMA((3 * n,)), pltpu.SemaphoreType.DMA((3 * n,))]
    if stacks is None:
        return _Rider(list(sums), _same(sums), {}, sems, start, finish)
    return _Rider(list(sums) + list(stacks), _same(sums), {n + a: a for a in range(n)}, sems, start, finish)


def _sibling_share(bufs):
    n = len(bufs)

    def copies(outs, send_sems, recv_sems, started_only=False):
        x, y, c, _ = _place()
        sent = [_remote(outs[a].at[c], outs[a].at[c], send_sems.at[a], recv_sems.at[a], (x, y, 1 - c)) for a in range(n)]
        if started_only:
            return sent
        landed = [_remote(outs[a].at[c], outs[a].at[1 - c], send_sems.at[a], recv_sems.at[a], (x, y, 1 - c)) for a in range(n)]
        return sent, landed

    def start(ins, outs, sems):
        for cp in copies(outs, *sems, started_only=True):
            cp.start()

    def finish(ins, outs, sems):
        sent, landed = copies(outs, *sems)
        for cp in landed:
            cp.wait_recv()
        for cp in sent:
            cp.wait_send()

    return _Rider(list(bufs), _same(bufs), {a: a for a in range(n)},
                  [pltpu.SemaphoreType.DMA((n,)), pltpu.SemaphoreType.DMA((n,))], start, finish)


def _pair_sum(core, part, received, name):
    _, k, rh, cols = part.shape

    def body(core_ref, p_ref, r_ref, o_ref):
        o_ref[...] = (p_ref[0].astype(F32) + r_ref[...].astype(F32)).astype(BF16)

    return pl.pallas_call(
        body, name=name,
        grid_spec=pltpu.PrefetchScalarGridSpec(
            num_scalar_prefetch=1, grid=(k,),
            in_specs=[pl.BlockSpec((1, 1, rh, cols), lambda j, core_ref: (core_ref[0], j, 0, 0)),
                      pl.BlockSpec((1, rh, cols), lambda j, core_ref: (j, 0, 0))],
            out_specs=pl.BlockSpec((1, rh, cols), lambda j, core_ref: (j, 0, 0))),
        out_shape=jax.ShapeDtypeStruct((k, rh, cols), BF16),
        compiler_params=_params(("parallel",), 32),
    )(core, part, received)


def _sum_leading(stack, steps, name):
    k, rows, cols = stack.shape
    tile = rows // steps

    def body(s_ref, o_ref):
        total = s_ref[0].astype(F32)
        for d in range(1, k):
            total = total + s_ref[d].astype(F32)
        o_ref[...] = total

    return pl.pallas_call(
        body, name=name, grid=(steps,),
        in_specs=[pl.BlockSpec((k, tile, cols), lambda i: (0, i, 0))],
        out_specs=pl.BlockSpec((tile, cols), lambda i: (i, 0)),
        out_shape=jax.ShapeDtypeStruct((rows, cols), F32),
        compiler_params=_params(("parallel",), 32),
    )(stack)


def _chip_sum(place, own, stack, steps, name):
    k, rows, cols = stack.shape
    tile = rows // steps

    def body(place_ref, own_ref, *refs):
        chip = place_ref[1]
        total = None
        for d in range(k):
            term = jnp.where(chip == d, own_ref[0], refs[d][0]).astype(F32)
            total = term if total is None else total + term
        refs[k][0] = total

    def other(d):
        return lambda i, place_ref: (jnp.where(place_ref[1] == d, (d + 1) % k, d), i, 0)

    return pl.pallas_call(
        body, name=name,
        grid_spec=pltpu.PrefetchScalarGridSpec(
            num_scalar_prefetch=1, grid=(steps,),
            in_specs=[pl.BlockSpec((1, tile, cols), lambda i, place_ref: (place_ref[1] % own.shape[0], i, 0))]
            + [pl.BlockSpec((1, tile, cols), other(d)) for d in range(k)],
            out_specs=pl.BlockSpec((1, tile, cols), lambda i, place_ref: (place_ref[0], i, 0))),
        out_shape=jax.ShapeDtypeStruct((2, rows, cols), F32),
        compiler_params=_params(("arbitrary",), 32),
    )(place, own, *([stack] * k))


def _adamw(w, g, row0, m, v, tile, name):
    rows, cols = w.shape
    first = row0 // tile
    assert rows % tile == 0 and row0 % tile == 0
    bc1 = 1.0 - ADAM_B1 ** ADAM_STEP
    bc2 = 1.0 - ADAM_B2 ** ADAM_STEP

    def body(w_ref, g_ref, m_ref, v_ref, go_ref, d_ref, mo_ref, vo_ref):
        g = g_ref[...]
        m_new = ADAM_B1 * m_ref[...] + (1.0 - ADAM_B1) * g
        v_new = ADAM_B2 * v_ref[...] + (1.0 - ADAM_B2) * (g * g)
        go_ref[...] = g
        d_ref[...] = -ADAM_LR * ((m_new / bc1) / (jnp.sqrt(v_new / bc2) + ADAM_EPS) + ADAM_WD * w_ref[...])
        mo_ref[...] = m_new
        vo_ref[...] = v_new

    spec = pl.BlockSpec((tile, cols), lambda i: (i, 0))
    g_spec = pl.BlockSpec((tile, cols), lambda i: (first + i, 0))
    return pl.pallas_call(
        body, name=name, grid=(rows // tile,),
        in_specs=[spec, g_spec, spec, spec], out_specs=[spec] * 4,
        out_shape=[jax.ShapeDtypeStruct((rows, cols), F32)] * 4,
        compiler_params=_params(("parallel",), 32),
    )(w, g, m, v)


SMALL = ("ffn1_norm", "mix_norm", "ffn2_norm", "final_norm", "pool_scale", "sink_logits", "pool_w")


def _pack_small(d, last_row=None):
    sink = jnp.pad(d["sink_logits"].reshape(1, N_HEADS), ((0, 0), (0, LANES - N_HEADS)))
    rows = [d[n].reshape(-1, LANES) for n in SMALL[:5]] + [sink, d["pool_w"].reshape(-1, LANES)]
    used = sum(r.shape[0] for r in rows)
    last = jnp.zeros((1, LANES), F32) if last_row is None else last_row
    return jnp.concatenate(rows + [jnp.zeros((SMALL_ROWS - used - 1, LANES), F32), last], axis=0)


def _unpack_small(packed, like):
    out, row = {}, 0
    for n in SMALL:
        size = LANES if n == "sink_logits" else math.prod(like[n].shape)
        chunk = packed[row:row + size // LANES].reshape(-1)
        out[n] = (chunk[:N_HEADS] if n == "sink_logits" else chunk).reshape(like[n].shape)
        row += size // LANES
    return out


def kernel(x, ffn1_norm, ffn1_w_gate, ffn1_w_up, ffn1_w_down, mix_norm, w_in, sink_logits, pool_w, pool_scale, w_out, ffn2_norm, ffn2_w_gate, ffn2_w_up, ffn2_w_down, final_norm, loss_target, m_ffn1_norm, m_ffn1_w_gate, m_ffn1_w_up, m_ffn1_w_down, m_mix_norm, m_w_in, m_sink_logits, m_pool_w, m_pool_scale, m_w_out, m_ffn2_norm, m_ffn2_w_gate, m_ffn2_w_up, m_ffn2_w_down, m_final_norm, v_ffn1_norm, v_ffn1_w_gate, v_ffn1_w_up, v_ffn1_w_down, v_mix_norm, v_w_in, v_sink_logits, v_pool_w, v_pool_scale, v_w_out, v_ffn2_norm, v_ffn2_w_gate, v_ffn2_w_up, v_ffn2_w_down, v_final_norm):
    names = ("ffn1_norm", "ffn1_w_gate", "ffn1_w_up", "ffn1_w_down", "mix_norm", "w_in", "sink_logits", "pool_w",
             "pool_scale", "w_out", "ffn2_norm", "ffn2_w_gate", "ffn2_w_up", "ffn2_w_down", "final_norm")
    weights = dict(zip(names, (ffn1_norm, ffn1_w_gate, ffn1_w_up, ffn1_w_down, mix_norm, w_in, sink_logits, pool_w,
                               pool_scale, w_out, ffn2_norm, ffn2_w_gate, ffn2_w_up, ffn2_w_down, final_norm)))
    mom1 = dict(zip(names, (m_ffn1_norm, m_ffn1_w_gate, m_ffn1_w_up, m_ffn1_w_down, m_mix_norm, m_w_in, m_sink_logits,
                            m_pool_w, m_pool_scale, m_w_out, m_ffn2_norm, m_ffn2_w_gate, m_ffn2_w_up, m_ffn2_w_down,
                            m_final_norm)))
    mom2 = dict(zip(names, (v_ffn1_norm, v_ffn1_w_gate, v_ffn1_w_up, v_ffn1_w_down, v_mix_norm, v_w_in, v_sink_logits,
                            v_pool_w, v_pool_scale, v_w_out, v_ffn2_norm, v_ffn2_w_gate, v_ffn2_w_up, v_ffn2_w_down,
                            v_final_norm)))
    chip = (2 * lax.axis_index("x") + lax.axis_index("y")).astype(jnp.int32).reshape(1)
    place = jnp.concatenate([lax.axis_index("c").astype(jnp.int32).reshape(1), chip])

    def rows_of(t, n):
        return jnp.swapaxes(t[n][0], 0, 1) if n in TRANSPOSED else t[n][0]

    bufs = [_pack(chip, [rows_of(weights, n) for n in GROUPS[0]], "pack_ffn1"),
            _pack(chip, [jnp.concatenate([rows_of(weights, n) for n in GROUPS[1]], axis=0)], "pack_mix"),
            _pack(chip, [rows_of(weights, n) for n in GROUPS[2]], "pack_ffn2")]

    small_w = {"ffn1_norm": ffn1_norm, "mix_norm": mix_norm, "ffn2_norm": ffn2_norm,
               "final_norm": final_norm.reshape(1, D_MODEL), "pool_scale": pool_scale, "sink_logits": sink_logits,
               "pool_w": pool_w[0]}
    loss, grad_x, group_grads, small_sum = _step(x[0], loss_target[0], bufs, small_w, place)

    out_g, out_d, out_m, out_v = {}, {}, {}, {}
    for members, g in zip(GROUPS, group_grads):
        row0 = 0
        for n in members:
            w = rows_of(weights, n)
            tile = FF_CHUNK // 4 if w.shape[0] == FF_CHUNK else math.gcd(IN_ROWS, OUT_ROWS)
            outs = _adamw(w, g, row0, rows_of(mom1, n), rows_of(mom2, n), tile, "adamw_" + n)
            row0 += w.shape[0]
            for dst, t in zip((out_g, out_d, out_m, out_v), outs):
                dst[n] = (jnp.swapaxes(t, 0, 1) if n in TRANSPOSED else t).reshape(weights[n].shape)
    small_outs = _adamw(_pack_small(weights), small_sum, 0, _pack_small(mom1), _pack_small(mom2), SMALL_ROWS, "adamw_small")
    for dst, packed in zip((out_g, out_d, out_m, out_v), small_outs):
        dst.update(_unpack_small(packed, weights))

    return (loss,grad_x.reshape(x.shape), *[out_g[n] for n in names], *[out_d[n] for n in names],
            *[out_m[n] for n in names], *[out_v[n] for n in names])
```

```python
import collections
import functools
import math

import jax
import jax.numpy as jnp
from jax import lax
from jax.experimental import pallas as pl
from jax.experimental.pallas import tpu as pltpu

F32, BF16 = jnp.float32, jnp.bfloat16
MESH = pl.DeviceIdType.MESH

D_MODEL = 1024
D_FF = 2816
N_CHIPS = 4
FF_CHUNK = D_FF // N_CHIPS
HEAD_DIM = 64
N_HEADS = 8
N_KV = 2
Q_PER_KV = N_HEADS // N_KV
KV_WIDTH = N_KV * HEAD_DIM
ATTN_WIDTH = N_HEADS * HEAD_DIM
POOL_WINDOWS = (2, 4, 8, 16)
N_POOL = len(POOL_WINDOWS)
POOL_GROUP = 128
POOL_WIDTH = N_POOL * POOL_GROUP
IN_WIDTH = ATTN_WIDTH + 2 * KV_WIDTH + POOL_WIDTH
WINDOW = 128
BLOCK = 128
BAND = 3 * BLOCK
ROPE_THETA = 500000.0
ROTARY_DIM = HEAD_DIM // 4
EPS = 1e-6
LANES = 128
Q_PAD = N_HEADS * LANES
U_PAD = Q_PAD + 2 * KV_WIDTH + POOL_WIDTH
SCALE = HEAD_DIM ** -0.5
NEG = -1e30

ADAM_LR, ADAM_B1, ADAM_B2, ADAM_EPS, ADAM_WD, ADAM_STEP = 0.001, 0.9, 0.999, 1e-08, 0.01, 10

V7X_VMEM_BYTES = 64 * 1024 * 1024
TOK_TILE = 512
SMALL_ROWS = 552


def _params(sem, vmem_mb):
    assert vmem_mb * 1024 * 1024 <= V7X_VMEM_BYTES
    return pltpu.CompilerParams(dimension_semantics=sem, vmem_limit_bytes=vmem_mb * 1024 * 1024)


def _dot(a, b):
    return lax.dot_general(a, b, (((1,), (0,)), ((), ())), preferred_element_type=F32)


def _dot_nt(a, b):
    return lax.dot_general(a, b, (((1,), (1,)), ((), ())), preferred_element_type=F32)


def _dot_tn(a, b):
    return lax.dot_general(a, b, (((0,), (0,)), ((), ())), preferred_element_type=F32)


def _rms_stats(h):
    r = lax.rsqrt(jnp.mean(h * h, axis=-1, keepdims=True) + EPS)
    return r, h * r


def _rms_bwd(dn, g, r, xh):
    gd = dn * g
    dh = r * (gd - xh * jnp.mean(gd * xh, axis=-1, keepdims=True))
    return dh, jnp.sum(dn * xh, axis=0, keepdims=True)


def _rope(x, c, s1, s2):
    return x * c + pltpu.roll(x, LANES - ROTARY_DIM // 2, 1) * s1 + pltpu.roll(x, ROTARY_DIM // 2, 1) * s2


def _rope_bwd(d, c, s1, s2):
    return d * c + pltpu.roll(d * s1, ROTARY_DIM // 2, 1) + pltpu.roll(d * s2, LANES - ROTARY_DIM // 2, 1)


def _sum_chunks(refs):
    terms = [ref[j].astype(F32) for ref in refs for j in range(ref.shape[0])]
    return functools.reduce(lambda a, b: a + b, terms)


def _chunk_rows(tile, k):
    return pl.BlockSpec((k, tile, D_MODEL), lambda i, *_: (0, i, 0))


def _full(shape):
    nd = len(shape)
    return pl.BlockSpec(shape, lambda *_: (0,) * nd)


def _rows(tile, cols):
    return pl.BlockSpec((tile, cols), lambda i, *_: (i, 0))


HBM_SPEC = pl.BlockSpec(memory_space=pltpu.HBM)

_Rider = collections.namedtuple("_Rider", "operands out_shapes aliases scratch start finish hooks", defaults=[()])


_NO_RIDER = _Rider([], [], {}, [], None, None)


def _call(body, name, grid, in_specs, out_specs, out_shape, scratch, vmem_mb, args, rider=None, prefetch=(),
          shares_rider_refs=False):
    rider = rider or _NO_RIDER
    n_pre, n_in, n_out, n_scr = len(prefetch), len(in_specs), len(out_specs), len(scratch)
    r_in, r_out = len(rider.operands), len(rider.out_shapes)

    def fused(*refs):
        pre, refs = refs[:n_pre], refs[n_pre:]
        ins, refs = refs[:n_in], refs[n_in:]
        r_ins, refs = refs[:r_in], refs[r_in:]
        outs, refs = refs[:n_out], refs[n_out:]
        r_outs, refs = refs[:r_out], refs[r_out:]
        scr, r_scr = refs[:n_scr], refs[n_scr:]
        ids = [pl.program_id(d) for d in range(len(grid))]
        if rider.start is not None:
            @pl.when(functools.reduce(jnp.logical_and, [i == 0 for i in ids]))
            def _():
                rider.start(r_ins, r_outs, r_scr)

        for at, hook in rider.hooks:
            @pl.when(functools.reduce(jnp.logical_and, [i == a for i, a in zip(ids, at)]))
            def _(hook=hook):
                hook(r_ins, r_outs, r_scr)

        if shares_rider_refs:
            body(*pre, *ins, *outs, *scr, rider_refs=r_outs)
        else:
            body(*pre, *ins, *outs, *scr)

        if rider.finish is not None:
            @pl.when(functools.reduce(jnp.logical_and, [i == g - 1 for i, g in zip(ids, grid)]))
            def _():
                rider.finish(r_ins, r_outs, r_scr)

    return pl.pallas_call(
        fused, name=name,
        grid_spec=pltpu.PrefetchScalarGridSpec(
            num_scalar_prefetch=n_pre, grid=grid,
            in_specs=list(in_specs) + [HBM_SPEC] * r_in, out_specs=list(out_specs) + [HBM_SPEC] * r_out,
            scratch_shapes=list(scratch) + list(rider.scratch)),
        out_shape=list(out_shape) + list(rider.out_shapes),
        input_output_aliases={n_pre + n_in + k: n_out + v for k, v in rider.aliases.items()},
        compiler_params=_params(("arbitrary",) * len(grid), vmem_mb),
    )(*prefetch, *args, *rider.operands)


def _comm_call(rider, name):
    r_in, r_out = len(rider.operands), len(rider.out_shapes)

    def body(*refs):
        r_ins, r_outs, r_scr = refs[:r_in], refs[r_in:r_in + r_out], refs[r_in + r_out:]
        rider.start(r_ins, r_outs, r_scr)
        rider.finish(r_ins, r_outs, r_scr)

    return pl.pallas_call(
        body, name=name, in_specs=[HBM_SPEC] * r_in, out_specs=[HBM_SPEC] * r_out, out_shape=list(rider.out_shapes),
        input_output_aliases=dict(rider.aliases), scratch_shapes=list(rider.scratch),
    )(*rider.operands)


def _ffn_fwd(order, h, gain, name, rider, group_at, loss_head=None):
    S = h.shape[0]
    tile = min(TOK_TILE, S)
    nt = S // tile
    last = N_CHIPS - 1
    n_extra_in, n_head_out = (2, 4) if loss_head else (0, 1)

    def body(order_ref, h_ref, g_ref, *refs, rider_refs):
        extra_in, refs = refs[:n_extra_in], refs[n_extra_in:]
        head_out, (n_ref, gate_ref, up_ref, w_scr, w_sem, acc, n_scr) = refs[:n_head_out], refs[n_head_out:]
        j, i = pl.program_id(0), pl.program_id(1)

        @pl.when(i == 0)
        def _():
            fetch = pltpu.make_async_copy(rider_refs[group_at].at[order_ref[j]], w_scr, w_sem)
            fetch.start()
            fetch.wait()

        at = pl.multiple_of(i * tile, tile)

        @pl.when(j == 0)
        def _():
            _, xh = _rms_stats(h_ref[...])
            n = (xh * g_ref[...]).astype(BF16)
            n_scr[pl.ds(at, tile), :] = n
            n_ref[...] = n
            acc[pl.ds(at, tile), :] = jnp.zeros((tile, D_MODEL), F32)

        half = tile // 2
        wg, wu, wd = (w_scr[part * FF_CHUNK:(part + 1) * FF_CHUNK, :] for part in range(3))
        ns = [n_scr[pl.ds(at + s * half, half), :] for s in range(2)]
        gates = [_dot_nt(n, wg) for n in ns]
        ups = [_dot_nt(n, wu) for n in ns]
        acts = [(g * jax.nn.sigmoid(g) * u).astype(BF16) for g, u in zip(gates, ups)]
        for s in range(2):
            gate_ref[0, s * half:(s + 1) * half, :] = gates[s].astype(BF16)
            up_ref[0, s * half:(s + 1) * half, :] = ups[s].astype(BF16)
            acc[pl.ds(at + s * half, half), :] += _dot(acts[s], wd)

        @pl.when(j == last)
        def _():
            out = h_ref[...] + 0.5 * acc[pl.ds(at, tile), :]
            if not loss_head:
                head_out[0][...] = out
                return
            (t_ref, gf_ref), (dh_ref, dhalf_ref, loss_ref, dg_ref) = extra_in, head_out

            @pl.when(i == 0)
            def _():
                loss_ref[...] = jnp.zeros_like(loss_ref)
                dg_ref[...] = jnp.zeros_like(dg_ref)

            gf = gf_ref[...]
            r, xh = _rms_stats(out)
            err = xh * gf - t_ref[...]
            loss_ref[...] += (0.5 / D_MODEL) * jnp.sum(err * err, axis=0, keepdims=True)
            dh, dg = _rms_bwd(err * (1.0 / D_MODEL), gf, r, xh)
            dg_ref[...] += dg
            dh_ref[...] = dh
            dhalf_ref[...] = (0.5 * dh).astype(BF16)

    tok = pl.BlockSpec((tile, D_MODEL), lambda j, i, order_ref: (i, 0))
    hid = pl.BlockSpec((1, tile, FF_CHUNK), lambda j, i, order_ref: (order_ref[j], i, 0))
    row = pl.BlockSpec((1, D_MODEL), lambda j, i, order_ref: (0, 0))
    in_last = pl.BlockSpec((tile, D_MODEL), lambda j, i, order_ref: (jnp.where(j == last, i, 0), 0))
    in_first = pl.BlockSpec((tile, D_MODEL), lambda j, i, order_ref: (jnp.where(j == 0, i, nt - 1), 0))
    tok_f32, tok_bf16, lanes = (jax.ShapeDtypeStruct((S, D_MODEL), F32), jax.ShapeDtypeStruct((S, D_MODEL), BF16),
                                jax.ShapeDtypeStruct((1, D_MODEL), F32))
    hidden = jax.ShapeDtypeStruct((N_CHIPS, S, FF_CHUNK), BF16)
    if loss_head:
        extra_specs, extra_args = [in_last, row], list(loss_head)
        head_specs, head_shapes = [in_last, in_last, row, row], [tok_f32, tok_bf16, lanes, lanes]
    else:
        extra_specs, extra_args, head_specs, head_shapes = [], [], [in_last], [tok_f32]
    return _call(
        body, name, (N_CHIPS, nt), [tok, row] + extra_specs, head_specs + [in_first, hid, hid],
        head_shapes + [tok_bf16, hidden, hidden],
        [pltpu.VMEM((3 * FF_CHUNK, D_MODEL), BF16), pltpu.SemaphoreType.DMA, pltpu.VMEM((S, D_MODEL), F32),
         pltpu.VMEM((S, D_MODEL), BF16)], 58, (h, gain, *extra_args), rider, (order,), shares_rider_refs=True)


def _ffn_bwd(chunks, d_out, n, gate, up, group, name, rider=None):
    S = n.shape[0]
    n_chunks = chunks.shape[0]
    tile = min(TOK_TILE, S)
    nt = S // tile
    half_rows = 3 * FF_CHUNK // 2
    cut = FF_CHUNK // 2

    def body(chunks_ref, do_ref, n_ref, gate_ref, up_ref, wg_ref, wu_ref, wd_ref, dn_ref, dw_ref, acc_g, acc_u, acc_d):
        j, i = pl.program_id(0), pl.program_id(1)

        @pl.when(i == 0)
        def _():
            acc_g[...] = jnp.zeros_like(acc_g)
            acc_u[...] = jnp.zeros_like(acc_u)
            acc_d[...] = jnp.zeros_like(acc_d)

        halves = [pl.ds(s * (tile // 2), tile // 2) for s in range(2)]
        dos = [do_ref[rows, :] for rows in halves]
        d_acts = [_dot_nt(do, wd_ref[0]) for do in dos]
        gs = [gate_ref[0, rows, :].astype(F32) for rows in halves]
        us = [up_ref[0, rows, :].astype(F32) for rows in halves]
        sigs = [jax.nn.sigmoid(g) for g in gs]
        silus = [g * sig for g, sig in zip(gs, sigs)]
        d_ups = [(d_act * silu).astype(BF16) for d_act, silu in zip(d_acts, silus)]
        d_gates = [(d_act * u * (sig * (1.0 + g * (1.0 - sig)))).astype(BF16) for d_act, u, sig, g in zip(d_acts, us, sigs, gs)]
        for rows, d_gate, d_up in zip(halves, d_gates, d_ups):
            dn_ref[0, rows, :] = (_dot(d_gate, wg_ref[0]) + _dot(d_up, wu_ref[0])).astype(BF16)
        d_gate, d_up = jnp.concatenate(d_gates, axis=0), jnp.concatenate(d_ups, axis=0)
        act = jnp.concatenate([(silu * u).astype(BF16) for silu, u in zip(silus, us)], axis=0)
        nn = n_ref[...]
        acc_g[...] += _dot_tn(d_gate, nn)
        acc_u[...] += _dot_tn(d_up, nn)
        acc_d[...] += _dot_tn(act, do_ref[...])

        @pl.when(i == nt - 1)
        def _():
            dw_ref[0, 0, :FF_CHUNK, :] = acc_g[...].astype(BF16)
            dw_ref[0, 0, FF_CHUNK:, :] = acc_u[:cut, :].astype(BF16)
            dw_ref[1, 0, :cut, :] = acc_u[cut:, :].astype(BF16)
            dw_ref[1, 0, cut:, :] = acc_d[...].astype(BF16)

    tok = pl.BlockSpec((tile, D_MODEL), lambda j, i, chunks_ref: (i, 0))
    hid = pl.BlockSpec((1, tile, FF_CHUNK), lambda j, i, chunks_ref: (chunks_ref[j], i, 0))
    return _call(
        body, name, (n_chunks, nt),
        [tok, tok, hid, hid]
        + [pl.BlockSpec((1, FF_CHUNK, D_MODEL), functools.partial(lambda j, i, chunks_ref, part: (chunks_ref[j], part, 0), part=part))
           for part in range(3)],
        [pl.BlockSpec((1, tile, D_MODEL), lambda j, i, chunks_ref: (j, i, 0)),
         pl.BlockSpec((2, 1, half_rows, D_MODEL), lambda j, i, chunks_ref: (0, j, 0, 0))],
        [jax.ShapeDtypeStruct((n_chunks, S, D_MODEL), BF16), jax.ShapeDtypeStruct((2, n_chunks, half_rows, D_MODEL), BF16)],
        [pltpu.VMEM((FF_CHUNK, D_MODEL), F32)] * 3, 56, (d_out, n, gate, up, group, group, group), rider, (chunks,))


def _mix_in(h, gain, w_in, rc, rs1, rs2, name):
    S = h.shape[0]
    tile = min(TOK_TILE, S)

    def body(h_ref, g_ref, w_ref, c_ref, s1_ref, s2_ref, n_ref, q_ref, k_ref, v_ref, pc_ref):
        _, xh = _rms_stats(h_ref[...])
        n = (xh * g_ref[...]).astype(BF16)
        n_ref[...] = n
        u = _dot_nt(n, w_ref[...])
        c, s1, s2 = c_ref[...], s1_ref[...], s2_ref[...]
        q_ref[...] = jnp.concatenate([(_rope(u[:, hd * LANES:(hd + 1) * LANES], c, s1, s2) * SCALE).astype(BF16)
                                      for hd in range(N_HEADS)], axis=1)
        k_ref[...] = _rope(u[:, Q_PAD:Q_PAD + KV_WIDTH], c, s1, s2).astype(BF16)
        v_ref[...] = u[:, Q_PAD + KV_WIDTH:Q_PAD + 2 * KV_WIDTH].astype(BF16)
        pc_ref[...] = u[:, Q_PAD + 2 * KV_WIDTH:]

    return pl.pallas_call(
        body, name=name, grid=(S // tile,),
        in_specs=[_rows(tile, D_MODEL), _full((1, D_MODEL)), _full((U_PAD, D_MODEL)),
                  _rows(tile, LANES), _rows(tile, LANES), _rows(tile, LANES)],
        out_specs=[_rows(tile, D_MODEL), _rows(tile, Q_PAD), _rows(tile, KV_WIDTH), _rows(tile, KV_WIDTH),
                   _rows(tile, POOL_WIDTH)],
        out_shape=[jax.ShapeDtypeStruct((S, D_MODEL), BF16), jax.ShapeDtypeStruct((S, Q_PAD), BF16),
                   jax.ShapeDtypeStruct((S, KV_WIDTH), BF16), jax.ShapeDtypeStruct((S, KV_WIDTH), BF16),
                   jax.ShapeDtypeStruct((S, POOL_WIDTH), F32)],
        compiler_params=_params(("parallel",), 40),
    )(h, gain, w_in, rc, rs1, rs2)


def _band_start(i, S):
    return pl.multiple_of(jnp.clip((i - 1) * BLOCK, 0, S - BAND), BLOCK)


def _window_bias(off):
    r = lax.broadcasted_iota(jnp.int32, (BLOCK, 1), 0)
    c = lax.broadcasted_iota(jnp.int32, (1, BAND), 1)
    return jnp.where(jnp.abs(off + r - c) <= WINDOW, 0.0, NEG).astype(F32)


def _softmax_parts(qh, kb, bias, sink_h):
    s = _dot_nt(qh, kb) + bias
    m = jnp.maximum(jnp.max(s, axis=-1, keepdims=True), sink_h)
    p = jnp.exp(s - m)
    es = jnp.exp(sink_h - m)
    return p, es, 1.0 / (jnp.sum(p, axis=-1, keepdims=True) + es)


def _pool_matrix(t0, start, S, w):
    r = lax.broadcasted_iota(jnp.int32, (BLOCK, 1), 0) + t0
    c = lax.broadcasted_iota(jnp.int32, (1, BAND), 1) + start
    half = w // 2

    def window(lo, hi):
        a = jnp.maximum(lo, 0)
        b = jnp.minimum(hi + 1, S)
        return jnp.where((c >= a) & (c < b), 1.0 / (b - a).astype(F32), 0.0)

    return (0.5 * (window(r - half, r + half - 1) + window(r - half + 1, r + half))).astype(BF16)


def _pool_matrices(S):
    blocks = ((0, 0), (BLOCK, 0), (S - BLOCK, S - BAND))
    return jnp.stack([jnp.stack([_pool_matrix(t0, start, S, w) for w in POOL_WINDOWS]) for t0, start in blocks])


def _pool_spec(nb):
    return pl.BlockSpec((1, N_POOL, BLOCK, BAND), lambda i, *_: (jnp.where(i == 0, 0, jnp.where(i == nb - 1, 2, 1)), 0, 0, 0))


def _mix_core_fwd(q, k, v, pc, sink, pool_m, pool_w, pool_scale, name):
    S = q.shape[0]
    nb = S // BLOCK

    def body(sink_ref, q_ref, k_ref, v_ref, pc_ref, pm_ref, pw_ref, ps_ref, a_ref, p_ref):
        i = pl.program_id(0)
        start = _band_start(i, S)
        band = pl.ds(start, BAND)
        bias = _window_bias(i * BLOCK - start)
        kb, vb = k_ref[band, :], v_ref[band, :]
        hs = range(N_HEADS)
        ss = [_dot_nt(q_ref[:, hd * LANES:(hd + 1) * LANES], kb) + bias for hd in hs]
        ms = [jnp.maximum(jnp.max(ss[hd], axis=-1, keepdims=True), sink_ref[0, hd]) for hd in hs]
        ps = [jnp.exp(ss[hd] - ms[hd]) for hd in hs]
        invs = [1.0 / (jnp.sum(ps[hd], axis=-1, keepdims=True) + jnp.exp(sink_ref[0, hd] - ms[hd])) for hd in hs]
        outs = [_dot(ps[hd].astype(BF16), vb) for hd in hs]
        a_ref[...] = jnp.concatenate([(outs[hd] * invs[hd]).astype(BF16) for hd in hs], axis=1)
        centre = pl.ds(pl.multiple_of(i * BLOCK, BLOCK), BLOCK)
        gs = range(N_POOL)
        sl = [slice(g * POOL_GROUP, (g + 1) * POOL_GROUP) for g in gs]
        means = [_dot(pm_ref[0, g], pc_ref[band, sl[g]].astype(BF16)) for g in gs]
        devs = [(means[g] - pc_ref[centre, sl[g]]).astype(BF16) for g in gs]
        p_ref[...] = (jnp.concatenate([_dot(devs[g], pw_ref[g]) for g in gs], axis=1) * ps_ref[...]).astype(BF16)

    return _call(
        body, name, (nb,),
        [pl.BlockSpec(memory_space=pltpu.SMEM), _rows(BLOCK, Q_PAD), _full((S, KV_WIDTH)), _full((S, KV_WIDTH)),
         _full((S, POOL_WIDTH)), _pool_spec(nb), _full((N_POOL, POOL_GROUP, POOL_GROUP)), _full((1, POOL_WIDTH))],
        [_rows(BLOCK, Q_PAD), _rows(BLOCK, POOL_WIDTH)],
        [jax.ShapeDtypeStruct((S, Q_PAD), BF16), jax.ShapeDtypeStruct((S, POOL_WIDTH), BF16)],
        [], 40, (sink, q, k, v, pc, pool_m, pool_w, pool_scale))


def _mix_core_bwd(q, k, v, pc, da, dp, sink, pool_m, pool_w, pool_scale, rc, rs1, rs2, name, rider=None):
    S = q.shape[0]
    nb = S // BLOCK

    def body(sink_ref, q_ref, k_ref, v_ref, pc_ref, da_ref, dp_ref, pm_ref, pw_ref, ps_ref, c_ref, s1_ref, s2_ref,
             dq_ref, dk_ref, dv_ref, dpc_ref, dsink_ref, dpw_ref, dps_ref):
        i = pl.program_id(0)

        @pl.when(i == 0)
        def _():
            dk_ref[...] = jnp.zeros_like(dk_ref)
            dv_ref[...] = jnp.zeros_like(dv_ref)
            dpc_ref[...] = jnp.zeros_like(dpc_ref)
            dsink_ref[...] = jnp.zeros_like(dsink_ref)
            dpw_ref[...] = jnp.zeros_like(dpw_ref)
            dps_ref[...] = jnp.zeros_like(dps_ref)

        start = _band_start(i, S)
        band = pl.ds(start, BAND)
        bias = _window_bias(i * BLOCK - start)
        kb, vb = k_ref[band, :], v_ref[band, :]
        c, s1, s2 = c_ref[...], s1_ref[...], s2_ref[...]
        lane = lax.broadcasted_iota(jnp.int32, (1, LANES), 1)
        hs = range(N_HEADS)
        qs = [q_ref[:, hd * LANES:(hd + 1) * LANES] for hd in hs]
        das = [da_ref[:, hd * LANES:(hd + 1) * LANES] for hd in hs]
        ss = [_dot_nt(qs[hd], kb) + bias for hd in hs]
        d_probs = [_dot_nt(das[hd], vb) for hd in hs]
        ms = [jnp.maximum(jnp.max(ss[hd], axis=-1, keepdims=True), sink_ref[0, hd]) for hd in hs]
        ps = [jnp.exp(ss[hd] - ms[hd]) for hd in hs]
        ess = [jnp.exp(sink_ref[0, hd] - ms[hd]) for hd in hs]
        invs = [1.0 / (jnp.sum(ps[hd], axis=-1, keepdims=True) + ess[hd]) for hd in hs]
        probs = [ps[hd] * invs[hd] for hd in hs]
        deltas = [jnp.sum(probs[hd] * d_probs[hd], axis=-1, keepdims=True) for hd in hs]
        d_ss = [(probs[hd] * (d_probs[hd] - deltas[hd])).astype(BF16) for hd in hs]
        dqs = [_dot(d_ss[hd], kb) for hd in hs]
        dq_ref[...] = jnp.concatenate([_rope_bwd(dqs[hd] * SCALE, c, s1, s2).astype(BF16) for hd in hs], axis=1)
        dks = [_dot_tn(d_ss[hd], qs[hd]) for hd in hs]
        dvs = [_dot_tn(probs[hd].astype(BF16), das[hd]) for hd in hs]
        dk_ref[band, :] += functools.reduce(lambda a, b: a + b, dks)
        dv_ref[band, :] += functools.reduce(lambda a, b: a + b, dvs)
        dsink_ref[...] += functools.reduce(lambda a, b: a + b, [
            jnp.where(lane == hd, -jnp.sum(ess[hd] * invs[hd] * deltas[hd], axis=0, keepdims=True), 0.0) for hd in hs])

        centre = pl.ds(pl.multiple_of(i * BLOCK, BLOCK), BLOCK)
        gs = range(N_POOL)
        sl = [slice(g * POOL_GROUP, (g + 1) * POOL_GROUP) for g in gs]
        devs = [(_dot(pm_ref[0, g], pc_ref[band, sl[g]].astype(BF16)) - pc_ref[centre, sl[g]]).astype(BF16) for g in gs]
        dys = [dp_ref[:, sl[g]].astype(F32) for g in gs]
        zs = [_dot(devs[g], pw_ref[g]) for g in gs]
        dzs = [(dys[g] * ps_ref[:, sl[g]]).astype(BF16) for g in gs]
        d_devs = [_dot_nt(dzs[g], pw_ref[g]) for g in gs]
        dps_ref[...] += jnp.concatenate([jnp.sum(dys[g] * zs[g], axis=0, keepdims=True) for g in gs], axis=1)
        for g in gs:
            dpw_ref[g] += _dot_tn(devs[g], dzs[g])
        dpc_ref[band, :] += jnp.concatenate([_dot_tn(pm_ref[0, g], d_devs[g].astype(BF16)) for g in gs], axis=1)
        dpc_ref[centre, :] -= jnp.concatenate(d_devs, axis=1)

    return _call(
        body, name, (nb,),
        [pl.BlockSpec(memory_space=pltpu.SMEM), _rows(BLOCK, Q_PAD), _full((S, KV_WIDTH)), _full((S, KV_WIDTH)),
         _full((S, POOL_WIDTH)), _rows(BLOCK, Q_PAD), _rows(BLOCK, POOL_WIDTH), _pool_spec(nb),
         _full((N_POOL, POOL_GROUP, POOL_GROUP)), _full((1, POOL_WIDTH)),
         _rows(BLOCK, LANES), _rows(BLOCK, LANES), _rows(BLOCK, LANES)],
        [_rows(BLOCK, Q_PAD), _full((S, KV_WIDTH)), _full((S, KV_WIDTH)), _full((S, POOL_WIDTH)),
         _full((1, LANES)), _full((N_POOL, POOL_GROUP, POOL_GROUP)), _full((1, POOL_WIDTH))],
        [jax.ShapeDtypeStruct((S, Q_PAD), BF16), jax.ShapeDtypeStruct((S, KV_WIDTH), F32),
         jax.ShapeDtypeStruct((S, KV_WIDTH), F32), jax.ShapeDtypeStruct((S, POOL_WIDTH), F32),
         jax.ShapeDtypeStruct((1, LANES), F32), jax.ShapeDtypeStruct((N_POOL, POOL_GROUP, POOL_GROUP), F32),
         jax.ShapeDtypeStruct((1, POOL_WIDTH), F32)],
        [], 56, (sink, q, k, v, pc, da, dp, pool_m, pool_w, pool_scale, rc, rs1, rs2), rider)


def _mix_out(h, a, p, wa, wp, name):
    S = h.shape[0]
    tile = min(TOK_TILE, S)

    def body(h_ref, a_ref, p_ref, wa_ref, wp_ref, o_ref):
        o_ref[...] = h_ref[...] + _dot(a_ref[...], wa_ref[...]) + _dot(p_ref[...], wp_ref[...])

    return pl.pallas_call(
        body, name=name, grid=(S // tile,),
        in_specs=[_rows(tile, D_MODEL), _rows(tile, Q_PAD), _rows(tile, POOL_WIDTH),
                  _full((Q_PAD, D_MODEL)), _full((POOL_WIDTH, D_MODEL))],
        out_specs=_rows(tile, D_MODEL),
        out_shape=jax.ShapeDtypeStruct((S, D_MODEL), F32),
        compiler_params=_params(("parallel",), 40),
    )(h, a, p, wa, wp)


def _mix_out_bwd(dh_out, dn, h, gain, a, p, wa, wp, name, rider=None):
    S = h.shape[0]
    tile = min(TOK_TILE, S)

    def body(do_ref, dn_ref, h_ref, g_ref, a_ref, p_ref, wa_ref, wp_ref, dh_ref, da_ref, dp_ref, dwa_ref, dwp_ref, dg_ref):
        @pl.when(pl.program_id(0) == 0)
        def _():
            dwa_ref[...] = jnp.zeros_like(dwa_ref)
            dwp_ref[...] = jnp.zeros_like(dwp_ref)
            dg_ref[...] = jnp.zeros_like(dg_ref)

        r, xh = _rms_stats(h_ref[...])
        dnorm, dg = _rms_bwd(_sum_chunks([dn_ref]), g_ref[...], r, xh)
        dh = do_ref[...] + dnorm
        dg_ref[...] += dg
        dh_ref[...] = dh
        dhb = dh.astype(BF16)
        da_ref[...] = _dot_nt(dhb, wa_ref[...]).astype(BF16)
        dp_ref[...] = _dot_nt(dhb, wp_ref[...]).astype(BF16)
        dwa_ref[...] += _dot_tn(a_ref[...], dhb)
        dwp_ref[...] += _dot_tn(p_ref[...], dhb)

    return _call(
        body, name, (S // tile,),
        [_rows(tile, D_MODEL), _chunk_rows(tile, dn.shape[0]), _rows(tile, D_MODEL), _full((1, D_MODEL)),
         _rows(tile, Q_PAD), _rows(tile, POOL_WIDTH), _full((Q_PAD, D_MODEL)), _full((POOL_WIDTH, D_MODEL))],
        [_rows(tile, D_MODEL), _rows(tile, Q_PAD), _rows(tile, POOL_WIDTH),
         _full((Q_PAD, D_MODEL)), _full((POOL_WIDTH, D_MODEL)), _full((1, D_MODEL))],
        [jax.ShapeDtypeStruct((S, D_MODEL), F32), jax.ShapeDtypeStruct((S, Q_PAD), BF16),
         jax.ShapeDtypeStruct((S, POOL_WIDTH), BF16), jax.ShapeDtypeStruct((Q_PAD, D_MODEL), F32),
         jax.ShapeDtypeStruct((POOL_WIDTH, D_MODEL), F32), jax.ShapeDtypeStruct((1, D_MODEL), F32)],
        [], 48, (dh_out, dn, h, gain, a, p, wa, wp), rider)


def _mix_in_bwd(dh_out, h, gain, n, dq, dk, dv, dpc, rc, rs1, rs2, w_in, name, rider=None):
    S = h.shape[0]
    tile = min(TOK_TILE, S)

    def body(do_ref, h_ref, g_ref, n_ref, dq_ref, dk_ref, dv_ref, dpc_ref, c_ref, s1_ref, s2_ref, w_ref,
             dh_ref, dhalf_ref, dw_ref, dg_ref):
        @pl.when(pl.program_id(0) == 0)
        def _():
            dw_ref[...] = jnp.zeros_like(dw_ref)
            dg_ref[...] = jnp.zeros_like(dg_ref)

        dk = _rope_bwd(dk_ref[...], c_ref[...], s1_ref[...], s2_ref[...]).astype(BF16)
        du = jnp.concatenate([dq_ref[...], dk, dv_ref[...].astype(BF16), dpc_ref[...].astype(BF16)], axis=1)
        dn = _dot(du, w_ref[...])
        dw_ref[...] += _dot_tn(du, n_ref[...])
        r, xh = _rms_stats(h_ref[...])
        dnorm, dg = _rms_bwd(dn, g_ref[...], r, xh)
        dh = do_ref[...] + dnorm
        dg_ref[...] += dg
        dh_ref[...] = dh
        dhalf_ref[...] = (0.5 * dh).astype(BF16)

    return _call(
        body, name, (S // tile,),
        [_rows(tile, D_MODEL), _rows(tile, D_MODEL), _full((1, D_MODEL)), _rows(tile, D_MODEL),
         _rows(tile, Q_PAD), _rows(tile, KV_WIDTH), _rows(tile, KV_WIDTH), _rows(tile, POOL_WIDTH),
         _rows(tile, LANES), _rows(tile, LANES), _rows(tile, LANES), _full((U_PAD, D_MODEL))],
        [_rows(tile, D_MODEL), _rows(tile, D_MODEL), _full((U_PAD, D_MODEL)), _full((1, D_MODEL))],
        [jax.ShapeDtypeStruct((S, D_MODEL), F32), jax.ShapeDtypeStruct((S, D_MODEL), BF16),
         jax.ShapeDtypeStruct((U_PAD, D_MODEL), F32), jax.ShapeDtypeStruct((1, D_MODEL), F32)],
        [], 56, (dh_out, h, gain, n, dq, dk, dv, dpc, rc, rs1, rs2, w_in), rider)


def _norm_bwd(dh_out, dns, h, gain, name, rider=None):
    S = h.shape[0]
    tile = min(TOK_TILE, S)
    n = len(dns)

    def body(do_ref, *refs):
        h_ref, g_ref, dh_ref, dg_ref = refs[n:]

        @pl.when(pl.program_id(0) == 0)
        def _():
            dg_ref[...] = jnp.zeros_like(dg_ref)

        r, xh = _rms_stats(h_ref[...])
        dnorm, dg = _rms_bwd(_sum_chunks(refs[:n]), g_ref[...], r, xh)
        dg_ref[...] += dg
        dh_ref[...] = do_ref[...] + dnorm

    return _call(
        body, name, (S // tile,),
        [_rows(tile, D_MODEL)] + [_chunk_rows(tile, dn.shape[0]) for dn in dns] + [_rows(tile, D_MODEL), _full((1, D_MODEL))],
        [_rows(tile, D_MODEL), _full((1, D_MODEL))],
        [jax.ShapeDtypeStruct((S, D_MODEL), F32), jax.ShapeDtypeStruct((1, D_MODEL), F32)],
        [], 40, (dh_out, *dns, h, gain), rider)


def _rope_tables(S):
    half = ROTARY_DIM // 2
    inv_freq = ROPE_THETA ** (-jnp.arange(0, ROTARY_DIM, 2, dtype=F32) / ROTARY_DIM)
    dim = jnp.arange(LANES) % HEAD_DIM
    ang = jnp.arange(S, dtype=F32)[:, None] * inv_freq[dim % half][None, :]
    lo, hi = (dim < half)[None, :], ((dim >= half) & (dim < ROTARY_DIM))[None, :]
    c = jnp.where(lo | hi, jnp.cos(ang), 1.0)
    s1 = jnp.where(lo, -jnp.sin(ang), 0.0)
    s2 = jnp.where(hi, jnp.sin(ang), 0.0)
    return c, s1, s2


def _pad_heads(w, axis):
    w = jnp.moveaxis(w, axis, 0)
    heads = w.reshape((N_HEADS, HEAD_DIM) + w.shape[1:])
    zero = jnp.zeros_like(heads)
    first = (jnp.arange(N_HEADS) < Q_PER_KV).reshape((N_HEADS, 1) + (1,) * (w.ndim - 1))
    lo = jnp.where(first, heads, zero)
    hi = jnp.where(first, zero, heads)
    padded = jnp.concatenate([lo, hi], axis=1).reshape((Q_PAD,) + w.shape[1:])
    return jnp.moveaxis(padded, 0, axis)


def _unpad_heads(w, axis):
    w = jnp.moveaxis(w, axis, 0)
    groups = w.reshape((N_HEADS, 2, HEAD_DIM) + w.shape[1:])
    first = (jnp.arange(N_HEADS) < Q_PER_KV).reshape((N_HEADS, 1) + (1,) * (w.ndim - 1))
    heads = jnp.where(first, groups[:, 0], groups[:, 1]).reshape((ATTN_WIDTH,) + w.shape[1:])
    return jnp.moveaxis(heads, 0, axis)


IN_ROWS = IN_WIDTH // N_CHIPS
OUT_ROWS = (ATTN_WIDTH + POOL_WIDTH) // N_CHIPS
MIX_ROWS = IN_ROWS + OUT_ROWS
FFN_ROWS = 3 * FF_CHUNK


def _step(x, target, bufs, small, place):
    S = x.shape[0]
    rc, rs1, rs2 = _rope_tables(S)
    mine = place[1]
    order = jnp.stack([mine, mine ^ 2, mine ^ 1, mine ^ 3])
    h1, n1, gate1, up1, ffn1, mix, ffn2 = _ffn_fwd(
        order, x, small["ffn1_norm"], "ffn1_fwd",
        _merge(_merge(_allgather_relayed(bufs[0]), _allgather(bufs[1:2])), _allgather(bufs[2:], peers=X_Y_DIAGONAL[:2])), 0)
    w_in_t = mix[:, :IN_ROWS].reshape(IN_WIDTH, D_MODEL)
    w_in_pad = jnp.concatenate([_pad_heads(w_in_t[:ATTN_WIDTH], 0), w_in_t[ATTN_WIDTH:]], axis=0)
    w_out = mix[:, IN_ROWS:].reshape(ATTN_WIDTH + POOL_WIDTH, D_MODEL)
    wa = _pad_heads(w_out[:ATTN_WIDTH], 0)
    wp = w_out[ATTN_WIDTH:]
    pool_w = small["pool_w"].astype(BF16)

    n2, q, k, v, pc = _mix_in(h1, small["mix_norm"], w_in_pad, rc, rs1, rs2, "mix_in")
    pool_m = _pool_matrices(S)
    a, p = _mix_core_fwd(q, k, v, pc, small["sink_logits"], pool_m, pool_w, small["pool_scale"], "mix_core_fwd")
    h2 = _mix_out(h1, a, p, wa, wp, "mix_out")
    dh3, dhalf3, loss_lanes, d_final, n3, gate2, up2, ffn2 = _ffn_fwd(
        order, h2, small["ffn2_norm"], "ffn2_fwd", _allgather([ffn2], peers=X_Y_DIAGONAL[2:], in_passes=0), 0,
        loss_head=(target, small["final_norm"]))

    dn3, d_ffn2 = _ffn_bwd(jnp.arange(N_CHIPS, dtype=jnp.int32), dhalf3, n3, gate2, up2, ffn2, "ffn2_bwd")
    dh2, da, dp, dwa, dwp, d_ffn2_norm, received = _mix_out_bwd(dh3, dn3, h2, small["ffn2_norm"], a, p, wa, wp, "mix_out_bwd",
                                                                _sibling_exchange([d_ffn2]))
    pair = _pair_sum(place, d_ffn2, received, "grad_pair_sum_ffn2")
    dq, dk, dv, dpc, dsink, dpool_w, dpool_scale, stack = _mix_core_bwd(
        q, k, v, pc, da, dp, small["sink_logits"], pool_m, pool_w, small["pool_scale"], rc, rs1, rs2, "mix_core_bwd",
        _scatter([pair], peers=X_Y_DIAGONAL[:2]))
    dh1, dhalf1, dw_in_pad, d_mix_norm, stack = _mix_in_bwd(dh2, h1, small["mix_norm"], n2, dq, dk, dv, dpc, rc, rs1, rs2,
                                                            w_in_pad, "mix_in_bwd",
                                                            _scatter([pair], peers=X_Y_DIAGONAL[2:], stacks=[stack]))
    reduced_ffn2 = _chip_sum(place, pair, stack, 2, "grad_chip_sum_ffn2")
    dw_in_t = jnp.concatenate([_unpad_heads(dw_in_pad[:Q_PAD], 0), dw_in_pad[Q_PAD:]], axis=0)
    dw_out = jnp.concatenate([_unpad_heads(dwa, 0), dwp], axis=0)
    d_mix = jnp.concatenate([dw_in_t.reshape(N_CHIPS, IN_ROWS, D_MODEL), dw_out.reshape(N_CHIPS, OUT_ROWS, D_MODEL)], axis=1)
    d_mix = jnp.transpose(d_mix.reshape(N_CHIPS, 2, MIX_ROWS // 2, D_MODEL), (1, 0, 2, 3)).astype(BF16)
    small_g = {"ffn1_norm": jnp.zeros_like(d_mix_norm), "mix_norm": d_mix_norm, "ffn2_norm": d_ffn2_norm,
               "final_norm": d_final, "pool_scale": dpool_scale, "sink_logits": dsink[:, :N_HEADS], "pool_w": dpool_w}
    loss_row = jnp.sum(loss_lanes.reshape(D_MODEL // LANES, LANES), axis=0, keepdims=True)
    small_early = _pack_small(small_g, loss_row)

    chunk = [(place[1:] + 1 + p) % N_CHIPS for p in range(N_CHIPS)]
    ffn1_bwd = functools.partial(_ffn_bwd, d_out=dhalf1, n=n1, gate=gate1, up=up1, group=ffn1)
    dn_a, dw_a, recv_mix, small_all, g_ffn2 = ffn1_bwd(
        chunk[0], name="ffn1_bwd_0",
        rider=_merge(_merge(_sibling_exchange([d_mix]), _small_allgather(small_early)), _sibling_share([reduced_ffn2])))
    pair_mix = _pair_sum(place, d_mix, recv_mix, "grad_pair_sum_mix")
    dn_b, dw_b, recv_a, stack_mix = ffn1_bwd(chunk[1], name="ffn1_bwd_1",
                                             rider=_merge(_sibling_exchange([dw_a]), _scatter([pair_mix])))
    pair_a = _pair_sum(place, dw_a, recv_a, "grad_pair_sum_ffn1_0")
    reduced_mix = _chip_sum(place, pair_mix, stack_mix, 2, "grad_chip_sum_mix")
    dn_c, dw_c, recv_b, stack, g_mix = ffn1_bwd(
        chunk[2], name="ffn1_bwd_2",
        rider=_merge(_merge(_sibling_exchange([dw_b]), _scatter_step(pair_a, stack, 0)), _sibling_share([reduced_mix])))
    pair_b = _pair_sum(place, dw_b, recv_b, "grad_pair_sum_ffn1_1")
    dn_d, dw_d, recv_c, stack = ffn1_bwd(chunk[3], name="ffn1_bwd_3",
                                         rider=_merge(_sibling_exchange([dw_c]), _scatter_step(pair_b, stack, 1)))
    pair_c = _pair_sum(place, dw_c, recv_c, "grad_pair_sum_ffn1_2")
    grad_x, d_ffn1_norm, recv_d, stack = _norm_bwd(dh1, [dn_a, dn_b, dn_c, dn_d], x, small["ffn1_norm"], "norm1_bwd",
                                                   _merge(_sibling_exchange([dw_d]), _scatter_step(pair_c, stack, 2)))
    pair_d = _pair_sum(place, dw_d, recv_d, "grad_pair_sum_ffn1_3")
    reduced_ffn1 = _chip_sum(place, pair_d, stack, 2, "grad_chip_sum_ffn1")
    g_ffn1, gains = _comm_call(_merge(_sibling_share([reduced_ffn1]), _small_allgather(d_ffn1_norm.reshape(-1, LANES))),
                               "grad_share_tail")
    gain_sum = _sum_leading(gains, 1, "gain_grad_sum")
    small_sum = jnp.concatenate([gain_sum, _sum_leading(small_all, 1, "small_grad_sum")[gain_sum.shape[0]:]], axis=0)
    return jnp.sum(small_sum[SMALL_ROWS - 1]), grad_x, [g.reshape(-1, D_MODEL) for g in (g_ffn1, g_mix, g_ffn2)], small_sum


GROUPS =(("ffn1_w_gate", "ffn1_w_up", "ffn1_w_down"), ("w_in", "w_out"), ("ffn2_w_gate", "ffn2_w_up", "ffn2_w_down"))
TRANSPOSED = ("ffn1_w_gate", "ffn1_w_up", "w_in", "ffn2_w_gate", "ffn2_w_up")


def _place():
    x, y, c = lax.axis_index("x"), lax.axis_index("y"), lax.axis_index("c")
    chips = [(1 - x, y), (x, 1 - y), (1 - x, 1 - y)]
    return x, y, c, chips


def _remote(src, dst, send_sem, recv_sem, to):
    return pltpu.make_async_remote_copy(src_ref=src, dst_ref=dst, send_sem=send_sem, recv_sem=recv_sem,
                                        device_id=to, device_id_type=MESH)


def _pack(chip, members, name):
    rows = members[0].shape[0]
    n = len(members)

    def body(chip_ref, *refs):
        ins, out_ref, buf, sems = refs[:n], refs[n], refs[n + 1], refs[n + 2]
        copies = [pltpu.make_async_copy(ins[k], buf.at[k], sems.at[k]) for k in range(n)]
        for cp in copies:
            cp.start()
        for k in range(n):
            copies[k].wait()
            out_ref[0, k * rows:(k + 1) * rows, :] = buf[k].astype(BF16)

    return pl.pallas_call(
        body, name=name,
        grid_spec=pltpu.PrefetchScalarGridSpec(
            num_scalar_prefetch=1, grid=(1,),
            in_specs=[HBM_SPEC] * n,
            out_specs=pl.BlockSpec((1, n * rows, D_MODEL), lambda k, chip_ref: (chip_ref[0], 0, 0)),
            scratch_shapes=[pltpu.VMEM((n, rows, D_MODEL), F32), pltpu.SemaphoreType.DMA((n,))]),
        out_shape=jax.ShapeDtypeStruct((N_CHIPS, n * rows, D_MODEL), BF16),
        compiler_params=_params(("arbitrary",), 40),
    )(chip, *members)


def _same(arrays):
    return [jax.ShapeDtypeStruct(a.shape, a.dtype) for a in arrays]


X_Y_DIAGONAL = (0, 1, 2)


def _allgather(bufs, peers=X_Y_DIAGONAL, in_passes=None):
    n = len(bufs)

    def copy(kind, outs, send_sems, recv_sems, a, k):
        x, y, c, chips = _place()
        half = bufs[a].shape[1] // 2

        def rows(slot, core):
            return outs[a].at[slot, pl.ds(pl.multiple_of(core * half, 16), half)]

        me, slot = 2 * x + y, 2 * chips[k][0] + chips[k][1]
        over_ici = (send_sems.at[6 * a + k], recv_sems.at[6 * a + k])
        over_d2d = (send_sems.at[6 * a + 3 + k], recv_sems.at[6 * a + 3 + k])
        if kind == "first":
            return _remote(rows(me, c), rows(me, c), *over_ici, (*chips[k], c))
        if kind == "landed":
            return _remote(rows(me, c), rows(slot, c), *over_ici, (*chips[k], c))
        if kind == "passed":
            return _remote(rows(slot, c), rows(slot, c), *over_d2d, (x, y, 1 - c))
        return _remote(rows(me, c), rows(slot, 1 - c), *over_d2d, (x, y, 1 - c))

    def arrive(pairs):
        def hook(ins, outs, sems):
            for a, k in pairs:
                copy("landed", outs, *sems, a, k).wait_recv()
                copy("passed", outs, *sems, a, k).start()
            for a, k in pairs:
                copy("handed", outs, *sems, a, k).wait_recv()
        return hook

    everything = [(a, k) for a in range(n) for k in peers]
    early = [(a, k) for a, k in everything if a == in_passes]

    def start(ins, outs, sems):
        for a, k in everything:
            copy("first", outs, *sems, a, k).start()

    def finish(ins, outs, sems):
        arrive([pair for pair in everything if pair not in early])(ins, outs, sems)
        for a, k in everything:
            copy("first", outs, *sems, a, k).wait_send()
            copy("passed", outs, *sems, a, k).wait_send()

    hooks = tuple(((k + 1, 0), arrive([(a, k)])) for a, k in early)
    return _Rider(list(bufs), _same(bufs), {a: a for a in range(n)},
                  [pltpu.SemaphoreType.DMA((6 * n,)), pltpu.SemaphoreType.DMA((6 * n,))], start, finish, hooks)


def _allgather_relayed(buf):
    half = buf.shape[1] // 2
    quarter = half // 2
    ici_x, ici_y, on_to_y, on_to_x, d2d_x, d2d_y, d2d_diagonal = range(7)

    def copy(kind, out, send_sems, recv_sems):
        x, y, c, chips = _place()
        me, (from_x, from_y, diagonal) = 2 * x + y, [2 * cx + cy for cx, cy in chips]
        to_x, to_y, sibling = (*chips[0], c), (*chips[1], c), (x, y, 1 - c)
        mine, theirs = c * half, (1 - c) * half

        def rows(slot, start, size):
            return out.at[slot, pl.ds(pl.multiple_of(start, 16), size)]

        src, dst, pair, to = {
            "first x": (rows(me, mine, half), rows(me, mine, half), ici_x, to_x),
            "first y": (rows(me, mine, half), rows(me, mine, half), ici_y, to_y),
            "landed x": (rows(me, mine, half), rows(from_x, mine, half), ici_x, to_x),
            "landed y": (rows(me, mine, half), rows(from_y, mine, half), ici_y, to_y),
            "on to y": (rows(from_x, mine, quarter), rows(from_x, mine, quarter), on_to_y, to_y),
            "on to x": (rows(from_y, mine + quarter, quarter), rows(from_y, mine + quarter, quarter), on_to_x, to_x),
            "landed diagonal 1": (rows(me, mine, quarter), rows(diagonal, mine, quarter), on_to_y, to_y),
            "landed diagonal 2": (rows(me, mine, quarter), rows(diagonal, mine + quarter, quarter), on_to_x, to_x),
            "passed x": (rows(from_x, mine, half), rows(from_x, mine, half), d2d_x, sibling),
            "passed y": (rows(from_y, mine, half), rows(from_y, mine, half), d2d_y, sibling),
            "passed diagonal": (rows(diagonal, mine, half), rows(diagonal, mine, half), d2d_diagonal, sibling),
            "handed x": (rows(me, mine, half), rows(from_x, theirs, half), d2d_x, sibling),
            "handed y": (rows(me, mine, half), rows(from_y, theirs, half), d2d_y, sibling),
            "handed diagonal": (rows(me, mine, half), rows(diagonal, theirs, half), d2d_diagonal, sibling),
        }[kind]
        return _remote(src, dst, send_sems.at[pair], recv_sems.at[pair], to)

    def arrive(landed, started, handed):
        def hook(ins, outs, sems):
            for kind in landed:
                copy(kind, outs[0], *sems).wait_recv()
            for kind in started:
                copy(kind, outs[0], *sems).start()
            copy(handed, outs[0], *sems).wait_recv()
        return hook

    def start(ins, outs, sems):
        copy("first x", outs[0], *sems).start()
        copy("first y", outs[0], *sems).start()

    def finish(ins, outs, sems):
        for kind in ("first x", "first y", "on to y", "on to x", "passed x", "passed y", "passed diagonal"):
            copy(kind, outs[0], *sems).wait_send()

    hooks = (((1, 0), arrive(["landed x"], ["on to y", "passed x"], "handed x")),
             ((2, 0), arrive(["landed y"], ["on to x", "passed y"], "handed y")),
             ((3, 0), arrive(["landed diagonal 1", "landed diagonal 2"], ["passed diagonal"], "handed diagonal")))
    return _Rider([buf], _same([buf]), {0: 0}, [pltpu.SemaphoreType.DMA((7,)), pltpu.SemaphoreType.DMA((7,))],
                  start, finish, hooks)


def _sibling_exchange(parts):
    n = len(parts)

    def copies(ins, outs, send_sems, recv_sems):
        x, y, c, _ = _place()
        return [_remote(ins[a].at[1 - c], outs[a], send_sems.at[a], recv_sems.at[a], (x, y, 1 - c)) for a in range(n)]

    def start(ins, outs, sems):
        for cp in copies(ins, outs, *sems):
            cp.start()

    def finish(ins, outs, sems):
        for cp in copies(ins, outs, *sems):
            cp.wait_recv()
            cp.wait_send()

    return _Rider(list(parts), [jax.ShapeDtypeStruct(p.shape[1:], p.dtype) for p in parts], {},
                  [pltpu.SemaphoreType.DMA((n,)), pltpu.SemaphoreType.DMA((n,))], start, finish)


def _small_allgather(small):
    flips = [(fx, fy, fc) for fx in range(2) for fy in range(2) for fc in range(2)][1:]

    def copies(small_ref, gather_ref, send_sems, recv_sems, local_sem, started_only=False):
        x, y, c, _ = _place()
        me = 4 * x + 2 * y + c
        peers = [((1 - x) if fx else x, (1 - y) if fy else y, (1 - c) if fc else c) for fx, fy, fc in flips]
        own = pltpu.make_async_copy(small_ref, gather_ref.at[me], local_sem)
        sent = [_remote(small_ref, gather_ref.at[me], send_sems.at[k], recv_sems.at[k], peer) for k, peer in enumerate(peers)]
        if started_only:
            return own, sent
        landed = [_remote(small_ref, gather_ref.at[4 * px + 2 * py + pc], send_sems.at[k], recv_sems.at[k], (px, py, pc))
                  for k, (px, py, pc) in enumerate(peers)]
        return own, sent, landed

    def start(ins, outs, sems):
        own, sent = copies(ins[0], outs[0], *sems, started_only=True)
        own.start()
        for cp in sent:
            cp.start()

    def finish(ins, outs, sems):
        own, sent, landed = copies(ins[0], outs[0], *sems)
        for cp in landed:
            cp.wait_recv()
        for cp in sent:
            cp.wait_send()
        own.wait()

    return _Rider([small], [jax.ShapeDtypeStruct((2 * N_CHIPS,) + small.shape, small.dtype)], {},
                  [pltpu.SemaphoreType.DMA((7,)), pltpu.SemaphoreType.DMA((7,)), pltpu.SemaphoreType.DMA], start, finish)


def _merge(a, b):
    na, nao, nas = len(a.operands), len(a.out_shapes), len(a.scratch)

    def start(ins, outs, sems):
        a.start(ins[:na], outs[:nao], sems[:nas])
        b.start(ins[na:], outs[nao:], sems[nas:])

    def finish(ins, outs, sems):
        a.finish(ins[:na], outs[:nao], sems[:nas])
        b.finish(ins[na:], outs[nao:], sems[nas:])

    def of_a(fn):
        return lambda ins, outs, sems: fn(ins[:na], outs[:nao], sems[:nas])

    def of_b(fn):
        return lambda ins, outs, sems: fn(ins[na:], outs[nao:], sems[nas:])

    aliases = {**a.aliases, **{na + k: nao + v for k, v in b.aliases.items()}}
    hooks = tuple((at, of_a(fn)) for at, fn in a.hooks) + tuple((at, of_b(fn)) for at, fn in b.hooks)
    return _Rider(a.operands + b.operands, a.out_shapes + b.out_shapes, aliases, a.scratch + b.scratch, start, finish, hooks)


def _scatter_step(pair, stack, step):
    def copies(pair_ref, stack_ref, send_sem, recv_sem, started_only=False):
        x, y, c, _ = _place()
        me = 2 * x + y
        to = (me + 1 + step) % N_CHIPS
        frm = (me + N_CHIPS - 1 - step) % N_CHIPS
        sent = _remote(pair_ref.at[0], stack_ref.at[me], send_sem, recv_sem, (to // 2, to % 2, c))
        if started_only:
            return sent
        landed = _remote(pair_ref.at[0], stack_ref.at[frm], send_sem, recv_sem, (frm // 2, frm % 2, c))
        return sent, landed

    def start(ins, outs, sems):
        copies(ins[0], outs[0], *sems, started_only=True).start()

    def finish(ins, outs, sems):
        sent, landed = copies(ins[0], outs[0], *sems)
        landed.wait_recv()
        sent.wait_send()

    return _Rider([pair, stack], _same([stack]), {1: 0}, [pltpu.SemaphoreType.DMA, pltpu.SemaphoreType.DMA], start, finish)


def _scatter(sums, peers=X_Y_DIAGONAL, stacks=None):
    n = len(sums)

    def copies(ins, outs, send_sems, recv_sems, started_only=False):
        x, y, c, chips = _place()
        me = 2 * x + y
        slots = [2 * cx + cy for cx, cy in chips]
        sent = [_remote(ins[a].at[slots[k]], outs[a].at[me], send_sems.at[3 * a + k], recv_sems.at[3 * a + k], (*chips[k], c))
                for a in range(n) for k in peers]
        if started_only:
            return sent
        landed = [_remote(ins[a].at[slots[k]], outs[a].at[slots[k]], send_sems.at[3 * a + k], recv_sems.at[3 * a + k],
                          (*chips[k], c)) for a in range(n) for k in peers]
        return sent, landed

    def start(ins, outs, sems):
        for cp in copies(ins, outs, *sems, started_only=True):
            cp.start()

    def finish(ins, outs, sems):
        sent, landed = copies(ins, outs, *sems)
        for cp in landed:
            cp.wait_recv()
        for cp in sent:
            cp.wait_send()

    sems = [pltpu.SemaphoreType.DMA((3 * n,)), pltpu.SemaphoreType.DMA((3 * n,))]
    if stacks is None:
        return _Rider(list(sums), _same(sums), {}, sems, start, finish)
    return _Rider(list(sums) + list(stacks), _same(sums), {n + a: a for a in range(n)}, sems, start, finish)


def _sibling_share(bufs):
    n = len(bufs)

    def copies(outs, send_sems, recv_sems, started_only=False):
        x, y, c, _ = _place()
        sent = [_remote(outs[a].at[c], outs[a].at[c], send_sems.at[a], recv_sems.at[a], (x, y, 1 - c)) for a in range(n)]
        if started_only:
            return sent
        landed = [_remote(outs[a].at[c], outs[a].at[1 - c], send_sems.at[a], recv_sems.at[a], (x, y, 1 - c)) for a in range(n)]
        return sent, landed

    def start(ins, outs, sems):
        for cp in copies(outs, *sems, started_only=True):
            cp.start()

    def finish(ins, outs, sems):
        sent, landed = copies(outs, *sems)
        for cp in landed:
            cp.wait_recv()
        for cp in sent:
            cp.wait_send()

    return _Rider(list(bufs), _same(bufs), {a: a for a in range(n)},
                  [pltpu.SemaphoreType.DMA((n,)), pltpu.SemaphoreType.DMA((n,))], start, finish)


def _pair_sum(core, part, received, name):
    _, k, rh, cols = part.shape

    def body(core_ref, p_ref, r_ref, o_ref):
        o_ref[...] = (p_ref[0].astype(F32) + r_ref[...].astype(F32)).astype(BF16)

    return pl.pallas_call(
        body, name=name,
        grid_spec=pltpu.PrefetchScalarGridSpec(
            num_scalar_prefetch=1, grid=(k,),
            in_specs=[pl.BlockSpec((1, 1, rh, cols), lambda j, core_ref: (core_ref[0], j, 0, 0)),
                      pl.BlockSpec((1, rh, cols), lambda j, core_ref: (j, 0, 0))],
            out_specs=pl.BlockSpec((1, rh, cols), lambda j, core_ref: (j, 0, 0))),
        out_shape=jax.ShapeDtypeStruct((k, rh, cols), BF16),
        compiler_params=_params(("parallel",), 32),
    )(core, part, received)


def _sum_leading(stack, steps, name):
    k, rows, cols = stack.shape
    tile = rows // steps

    def body(s_ref, o_ref):
        total = s_ref[0].astype(F32)
        for d in range(1, k):
            total = total + s_ref[d].astype(F32)
        o_ref[...] = total

    return pl.pallas_call(
        body, name=name, grid=(steps,),
        in_specs=[pl.BlockSpec((k, tile, cols), lambda i: (0, i, 0))],
        out_specs=pl.BlockSpec((tile, cols), lambda i: (i, 0)),
        out_shape=jax.ShapeDtypeStruct((rows, cols), F32),
        compiler_params=_params(("parallel",), 32),
    )(stack)


def _chip_sum(place, own, stack, steps, name):
    k, rows, cols = stack.shape
    tile = rows // steps

    def body(place_ref, own_ref, *refs):
        chip = place_ref[1]
        total = None
        for d in range(k):
            term = jnp.where(chip == d, own_ref[0], refs[d][0]).astype(F32)
            total = term if total is None else total + term
        refs[k][0] = total

    def other(d):
        return lambda i, place_ref: (jnp.where(place_ref[1] == d, (d + 1) % k, d), i, 0)

    return pl.pallas_call(
        body, name=name,
        grid_spec=pltpu.PrefetchScalarGridSpec(
            num_scalar_prefetch=1, grid=(steps,),
            in_specs=[pl.BlockSpec((1, tile, cols), lambda i, place_ref: (place_ref[1] % own.shape[0], i, 0))]
            + [pl.BlockSpec((1, tile, cols), other(d)) for d in range(k)],
            out_specs=pl.BlockSpec((1, tile, cols), lambda i, place_ref: (place_ref[0], i, 0))),
        out_shape=jax.ShapeDtypeStruct((2, rows, cols), F32),
        compiler_params=_params(("arbitrary",), 32),
    )(place, own, *([stack] * k))


def _adamw(w, g, row0, m, v, tile, name):
    rows, cols = w.shape
    first = row0 // tile
    assert rows % tile == 0 and row0 % tile == 0
    bc1 = 1.0 - ADAM_B1 ** ADAM_STEP
    bc2 = 1.0 - ADAM_B2 ** ADAM_STEP

    def body(w_ref, g_ref, m_ref, v_ref, go_ref, d_ref, mo_ref, vo_ref):
        g = g_ref[...]
        m_new = ADAM_B1 * m_ref[...] + (1.0 - ADAM_B1) * g
        v_new = ADAM_B2 * v_ref[...] + (1.0 - ADAM_B2) * (g * g)
        go_ref[...] = g
        d_ref[...] = -ADAM_LR * ((m_new / bc1) / (jnp.sqrt(v_new / bc2) + ADAM_EPS) + ADAM_WD * w_ref[...])
        mo_ref[...] = m_new
        vo_ref[...] = v_new

    spec = pl.BlockSpec((tile, cols), lambda i: (i, 0))
    g_spec = pl.BlockSpec((tile, cols), lambda i: (first + i, 0))
    return pl.pallas_call(
        body, name=name, grid=(rows // tile,),
        in_specs=[spec, g_spec, spec, spec], out_specs=[spec] * 4,
        out_shape=[jax.ShapeDtypeStruct((rows, cols), F32)] * 4,
        compiler_params=_params(("parallel",), 32),
    )(w, g, m, v)


SMALL = ("ffn1_norm", "mix_norm", "ffn2_norm", "final_norm", "pool_scale", "sink_logits", "pool_w")


def _pack_small(d, last_row=None):
    sink = jnp.pad(d["sink_logits"].reshape(1, N_HEADS), ((0, 0), (0, LANES - N_HEADS)))
    rows = [d[n].reshape(-1, LANES) for n in SMALL[:5]] + [sink, d["pool_w"].reshape(-1, LANES)]
    used = sum(r.shape[0] for r in rows)
    last = jnp.zeros((1, LANES), F32) if last_row is None else last_row
    return jnp.concatenate(rows + [jnp.zeros((SMALL_ROWS - used - 1, LANES), F32), last], axis=0)


def _unpack_small(packed, like):
    out, row = {}, 0
    for n in SMALL:
        size = LANES if n == "sink_logits" else math.prod(like[n].shape)
        chunk = packed[row:row + size // LANES].reshape(-1)
        out[n] = (chunk[:N_HEADS] if n == "sink_logits" else chunk).reshape(like[n].shape)
        row += size // LANES
    return out


def kernel(x, ffn1_norm, ffn1_w_gate, ffn1_w_up, ffn1_w_down, mix_norm, w_in, sink_logits, pool_w, pool_scale, w_out, ffn2_norm, ffn2_w_gate, ffn2_w_up, ffn2_w_down, final_norm, loss_target, m_ffn1_norm, m_ffn1_w_gate, m_ffn1_w_up, m_ffn1_w_down, m_mix_norm, m_w_in, m_sink_logits, m_pool_w, m_pool_scale, m_w_out, m_ffn2_norm, m_ffn2_w_gate, m_ffn2_w_up, m_ffn2_w_down, m_final_norm, v_ffn1_norm, v_ffn1_w_gate, v_ffn1_w_up, v_ffn1_w_down, v_mix_norm, v_w_in, v_sink_logits, v_pool_w, v_pool_scale, v_w_out, v_ffn2_norm, v_ffn2_w_gate, v_ffn2_w_up, v_ffn2_w_down, v_final_norm):
    names = ("ffn1_norm", "ffn1_w_gate", "ffn1_w_up", "ffn1_w_down", "mix_norm", "w_in", "sink_logits", "pool_w",
             "pool_scale", "w_out", "ffn2_norm", "ffn2_w_gate", "ffn2_w_up", "ffn2_w_down", "final_norm")
    weights = dict(zip(names, (ffn1_norm, ffn1_w_gate, ffn1_w_up, ffn1_w_down, mix_norm, w_in, sink_logits, pool_w,
                               pool_scale, w_out, ffn2_norm, ffn2_w_gate, ffn2_w_up, ffn2_w_down, final_norm)))
    mom1 = dict(zip(names, (m_ffn1_norm, m_ffn1_w_gate, m_ffn1_w_up, m_ffn1_w_down, m_mix_norm, m_w_in, m_sink_logits,
                            m_pool_w, m_pool_scale, m_w_out, m_ffn2_norm, m_ffn2_w_gate, m_ffn2_w_up, m_ffn2_w_down,
                            m_final_norm)))
    mom2 = dict(zip(names, (v_ffn1_norm, v_ffn1_w_gate, v_ffn1_w_up, v_ffn1_w_down, v_mix_norm, v_w_in, v_sink_logits,
                            v_pool_w, v_pool_scale, v_w_out, v_ffn2_norm, v_ffn2_w_gate, v_ffn2_w_up, v_ffn2_w_down,
                            v_final_norm)))
    chip = (2 * lax.axis_index("x") + lax.axis_index("y")).astype(jnp.int32).reshape(1)
    place = jnp.concatenate([lax.axis_index("c").astype(jnp.int32).reshape(1), chip])

    def rows_of(t, n):
        return jnp.swapaxes(t[n][0], 0, 1) if n in TRANSPOSED else t[n][0]

    bufs = [_pack(chip, [rows_of(weights, n) for n in GROUPS[0]], "pack_ffn1"),
            _pack(chip, [jnp.concatenate([rows_of(weights, n) for n in GROUPS[1]], axis=0)], "pack_mix"),
            _pack(chip, [rows_of(weights, n) for n in GROUPS[2]], "pack_ffn2")]

    small_w = {"ffn1_norm": ffn1_norm, "mix_norm": mix_norm, "ffn2_norm": ffn2_norm,
               "final_norm": final_norm.reshape(1, D_MODEL), "pool_scale": pool_scale, "sink_logits": sink_logits,
               "pool_w": pool_w[0]}
    loss, grad_x, group_grads, small_sum = _step(x[0], loss_target[0], bufs, small_w, place)

    out_g, out_d, out_m, out_v = {}, {}, {}, {}
    for members, g in zip(GROUPS, group_grads):
        row0 = 0
        for n in members:
            w = rows_of(weights, n)
            tile = FF_CHUNK // 4 if w.shape[0] == FF_CHUNK else math.gcd(IN_ROWS, OUT_ROWS)
            outs = _adamw(w, g, row0, rows_of(mom1, n), rows_of(mom2, n), tile, "adamw_" + n)
            row0 += w.shape[0]
            for dst, t in zip((out_g, out_d, out_m, out_v), outs):
                dst[n] = (jnp.swapaxes(t, 0, 1) if n in TRANSPOSED else t).reshape(weights[n].shape)
    small_outs = _adamw(_pack_small(weights), small_sum, 0, _pack_small(mom1), _pack_small(mom2), SMALL_ROWS, "adamw_small")
    for dst, packed in zip((out_g, out_d, out_m, out_v), small_outs):
        dst.update(_unpack_small(packed, weights))

    return (loss,grad_x.reshape(x.shape), *[out_g[n] for n in names], *[out_d[n] for n in names],
            *[out_m[n] for n in names], *[out_v[n] for n in names])
```

```python
import collections
import functools
import math

import jax
import jax.numpy as jnp
from jax import lax
from jax.experimental import pallas as pl
from jax.experimental.pallas import tpu as pltpu

F32, BF16 = jnp.float32, jnp.bfloat16
MESH = pl.DeviceIdType.MESH

D_MODEL = 1024
D_FF = 2816
N_CHIPS = 4
FF_CHUNK = D_FF // N_CHIPS
HEAD_DIM = 64
N_HEADS = 8
N_KV = 2
Q_PER_KV = N_HEADS // N_KV
KV_WIDTH = N_KV * HEAD_DIM
ATTN_WIDTH = N_HEADS * HEAD_DIM
POOL_WINDOWS = (2, 4, 8, 16)
N_POOL = len(POOL_WINDOWS)
POOL_GROUP = 128
POOL_WIDTH = N_POOL * POOL_GROUP
IN_WIDTH = ATTN_WIDTH + 2 * KV_WIDTH + POOL_WIDTH
WINDOW = 128
BLOCK = 128
BAND = 3 * BLOCK
ROPE_THETA = 500000.0
ROTARY_DIM = HEAD_DIM // 4
EPS = 1e-6
LANES = 128
Q_PAD = N_HEADS * LANES
U_PAD = Q_PAD + 2 * KV_WIDTH + POOL_WIDTH
SCALE = HEAD_DIM ** -0.5
NEG = -1e30

ADAM_LR, ADAM_B1, ADAM_B2, ADAM_EPS, ADAM_WD, ADAM_STEP = 0.001, 0.9, 0.999, 1e-08, 0.01, 10

V7X_VMEM_BYTES = 64 * 1024 * 1024
TOK_TILE = 512
SUBLANES = 8
SMALL_ROWS = 568


def _params(sem, vmem_mb):
    assert vmem_mb * 1024 * 1024 <= V7X_VMEM_BYTES
    return pltpu.CompilerParams(dimension_semantics=sem, vmem_limit_bytes=vmem_mb * 1024 * 1024)


def _dot(a, b):
    return lax.dot_general(a, b, (((1,), (0,)), ((), ())), preferred_element_type=F32)


def _dot_nt(a, b):
    return lax.dot_general(a, b, (((1,), (1,)), ((), ())), preferred_element_type=F32)


def _dot_tn(a, b):
    return lax.dot_general(a, b, (((0,), (0,)), ((), ())), preferred_element_type=F32)


def _rms_stats(h):
    r = lax.rsqrt(jnp.mean(h * h, axis=-1, keepdims=True) + EPS)
    return r, h * r


def _rms_bwd(dn, g, r, xh):
    gd = dn * g
    dh = r * (gd - xh * jnp.mean(gd * xh, axis=-1, keepdims=True))
    return dh, jnp.sum(dn * xh, axis=0, keepdims=True)


def _rope(x, c, s1, s2):
    return x * c + pltpu.roll(x, LANES - ROTARY_DIM // 2, 1) * s1 + pltpu.roll(x, ROTARY_DIM // 2, 1) * s2


def _rope_bwd(d, c, s1, s2):
    return d * c + pltpu.roll(d * s1, ROTARY_DIM // 2, 1) + pltpu.roll(d * s2, LANES - ROTARY_DIM // 2, 1)


def _sum_chunks(refs):
    terms = [ref[j].astype(F32) for ref in refs for j in range(ref.shape[0])]
    return functools.reduce(lambda a, b: a + b, terms)


def _chunk_rows(tile, k):
    return pl.BlockSpec((k, tile, D_MODEL), lambda i, *_: (0, i, 0))


def _full(shape):
    nd = len(shape)
    return pl.BlockSpec(shape, lambda *_: (0,) * nd)


def _rows(tile, cols):
    return pl.BlockSpec((tile, cols), lambda i, *_: (i, 0))


HBM_SPEC = pl.BlockSpec(memory_space=pltpu.HBM)

_Rider = collections.namedtuple("_Rider", "operands out_shapes aliases scratch start finish hooks", defaults=[()])


_NO_RIDER = _Rider([], [], {}, [], None, None)


def _call(body, name, grid, in_specs, out_specs, out_shape, scratch, vmem_mb, args, rider=None, prefetch=(),
          shares_rider_refs=False):
    rider = rider or _NO_RIDER
    n_pre, n_in, n_out, n_scr = len(prefetch), len(in_specs), len(out_specs), len(scratch)
    r_in, r_out = len(rider.operands), len(rider.out_shapes)

    def fused(*refs):
        pre, refs = refs[:n_pre], refs[n_pre:]
        ins, refs = refs[:n_in], refs[n_in:]
        r_ins, refs = refs[:r_in], refs[r_in:]
        outs, refs = refs[:n_out], refs[n_out:]
        r_outs, refs = refs[:r_out], refs[r_out:]
        scr, r_scr = refs[:n_scr], refs[n_scr:]
        ids = [pl.program_id(d) for d in range(len(grid))]
        if rider.start is not None:
            @pl.when(functools.reduce(jnp.logical_and, [i == 0 for i in ids]))
            def _():
                rider.start(r_ins, r_outs, r_scr)

        for at, hook in rider.hooks:
            @pl.when(functools.reduce(jnp.logical_and, [i == a for i, a in zip(ids, at)]))
            def _(hook=hook):
                hook(r_ins, r_outs, r_scr)

        if shares_rider_refs:
            body(*pre, *ins, *outs, *scr, rider_refs=r_outs)
        else:
            body(*pre, *ins, *outs, *scr)

        if rider.finish is not None:
            @pl.when(functools.reduce(jnp.logical_and, [i == g - 1 for i, g in zip(ids, grid)]))
            def _():
                rider.finish(r_ins, r_outs, r_scr)

    return pl.pallas_call(
        fused, name=name,
        grid_spec=pltpu.PrefetchScalarGridSpec(
            num_scalar_prefetch=n_pre, grid=grid,
            in_specs=list(in_specs) + [HBM_SPEC] * r_in, out_specs=list(out_specs) + [HBM_SPEC] * r_out,
            scratch_shapes=list(scratch) + list(rider.scratch)),
        out_shape=list(out_shape) + list(rider.out_shapes),
        input_output_aliases={n_pre + n_in + k: n_out + v for k, v in rider.aliases.items()},
        compiler_params=_params(("arbitrary",) * len(grid), vmem_mb),
    )(*prefetch, *args, *rider.operands)


def _comm_call(rider, name):
    r_in, r_out = len(rider.operands), len(rider.out_shapes)

    def body(*refs):
        r_ins, r_outs, r_scr = refs[:r_in], refs[r_in:r_in + r_out], refs[r_in + r_out:]
        rider.start(r_ins, r_outs, r_scr)
        rider.finish(r_ins, r_outs, r_scr)

    return pl.pallas_call(
        body, name=name, in_specs=[HBM_SPEC] * r_in, out_specs=[HBM_SPEC] * r_out, out_shape=list(rider.out_shapes),
        input_output_aliases=dict(rider.aliases), scratch_shapes=list(rider.scratch),
    )(*rider.operands)


def _ffn_fwd(order, h, gain, name, rider, group_at, loss_head=None):
    S = h.shape[0]
    tile = min(TOK_TILE, S)
    nt = S // tile
    last = N_CHIPS - 1
    n_extra_in, n_head_out = (2, 4) if loss_head else (0, 1)

    def body(order_ref, h_ref, g_ref, *refs, rider_refs):
        extra_in, refs = refs[:n_extra_in], refs[n_extra_in:]
        head_out, (n_ref, gate_ref, up_ref, w_scr, w_sem, acc, n_scr) = refs[:n_head_out], refs[n_head_out:]
        j, i = pl.program_id(0), pl.program_id(1)

        @pl.when(i == 0)
        def _():
            fetch = pltpu.make_async_copy(rider_refs[group_at].at[order_ref[j]], w_scr, w_sem)
            fetch.start()
            fetch.wait()

        at = pl.multiple_of(i * tile, tile)

        @pl.when(j == 0)
        def _():
            _, xh = _rms_stats(h_ref[...])
            n = (xh * g_ref[...]).astype(BF16)
            n_scr[pl.ds(at, tile), :] = n
            n_ref[...] = n
            acc[pl.ds(at, tile), :] = jnp.zeros((tile, D_MODEL), F32)

        half = tile // 2
        wg, wu, wd = (w_scr[part * FF_CHUNK:(part + 1) * FF_CHUNK, :] for part in range(3))
        ns = [n_scr[pl.ds(at + s * half, half), :] for s in range(2)]
        gates = [_dot_nt(n, wg) for n in ns]
        ups = [_dot_nt(n, wu) for n in ns]
        acts = [(g * jax.nn.sigmoid(g) * u).astype(BF16) for g, u in zip(gates, ups)]
        for s in range(2):
            gate_ref[0, s * half:(s + 1) * half, :] = gates[s].astype(BF16)
            up_ref[0, s * half:(s + 1) * half, :] = ups[s].astype(BF16)
            acc[pl.ds(at + s * half, half), :] += _dot(acts[s], wd)

        @pl.when(j == last)
        def _():
            out = h_ref[...] + 0.5 * acc[pl.ds(at, tile), :]
            if not loss_head:
                head_out[0][...] = out
                return
            (t_ref, gf_ref), (dh_ref, dhalf_ref, loss_ref, dg_ref) = extra_in, head_out

            @pl.when(i == 0)
            def _():
                loss_ref[...] = jnp.zeros_like(loss_ref)
                dg_ref[...] = jnp.zeros_like(dg_ref)

            gf = gf_ref[...]
            r, xh = _rms_stats(out)
            err = xh * gf - t_ref[...]
            loss_ref[...] += (0.5 / D_MODEL) * jnp.sum(err * err, axis=0, keepdims=True)
            dh, dg = _rms_bwd(err * (1.0 / D_MODEL), gf, r, xh)
            dg_ref[...] += dg
            dh_ref[...] = dh
            dhalf_ref[...] = (0.5 * dh).astype(BF16)

    tok = pl.BlockSpec((tile, D_MODEL), lambda j, i, order_ref: (i, 0))
    hid = pl.BlockSpec((1, tile, FF_CHUNK), lambda j, i, order_ref: (order_ref[j], i, 0))
    row = pl.BlockSpec((1, D_MODEL), lambda j, i, order_ref: (0, 0))
    in_last = pl.BlockSpec((tile, D_MODEL), lambda j, i, order_ref: (jnp.where(j == last, i, 0), 0))
    in_first = pl.BlockSpec((tile, D_MODEL), lambda j, i, order_ref: (jnp.where(j == 0, i, nt - 1), 0))
    tok_f32, tok_bf16, lanes = (jax.ShapeDtypeStruct((S, D_MODEL), F32), jax.ShapeDtypeStruct((S, D_MODEL), BF16),
                                jax.ShapeDtypeStruct((1, D_MODEL), F32))
    hidden = jax.ShapeDtypeStruct((N_CHIPS, S, FF_CHUNK), BF16)
    if loss_head:
        extra_specs, extra_args = [in_last, row], list(loss_head)
        head_specs, head_shapes = [in_last, in_last, row, row], [tok_f32, tok_bf16, lanes, lanes]
    else:
        extra_specs, extra_args, head_specs, head_shapes = [], [], [in_last], [tok_f32]
    return _call(
        body, name, (N_CHIPS, nt), [tok, row] + extra_specs, head_specs + [in_first, hid, hid],
        head_shapes + [tok_bf16, hidden, hidden],
        [pltpu.VMEM((3 * FF_CHUNK, D_MODEL), BF16), pltpu.SemaphoreType.DMA, pltpu.VMEM((S, D_MODEL), F32),
         pltpu.VMEM((S, D_MODEL), BF16)], 58, (h, gain, *extra_args), rider, (order,), shares_rider_refs=True)


def _ffn_bwd(chunks, d_out, n, gate, up, group, name, rider=None):
    S = n.shape[0]
    n_chunks = chunks.shape[0]
    tile = min(TOK_TILE, S)
    nt = S // tile
    half_rows = 3 * FF_CHUNK // 2
    cut = FF_CHUNK // 2

    def body(chunks_ref, do_ref, n_ref, gate_ref, up_ref, wg_ref, wu_ref, wd_ref, dn_ref, dw_ref, acc_g, acc_u, acc_d):
        j, i = pl.program_id(0), pl.program_id(1)

        @pl.when(i == 0)
        def _():
            acc_g[...] = jnp.zeros_like(acc_g)
            acc_u[...] = jnp.zeros_like(acc_u)
            acc_d[...] = jnp.zeros_like(acc_d)

        halves = [pl.ds(s * (tile // 2), tile // 2) for s in range(2)]
        dos = [do_ref[rows, :] for rows in halves]
        d_acts = [_dot_nt(do, wd_ref[0]) for do in dos]
        gs = [gate_ref[0, rows, :].astype(F32) for rows in halves]
        us = [up_ref[0, rows, :].astype(F32) for rows in halves]
        sigs = [jax.nn.sigmoid(g) for g in gs]
        silus = [g * sig for g, sig in zip(gs, sigs)]
        d_ups = [(d_act * silu).astype(BF16) for d_act, silu in zip(d_acts, silus)]
        d_gates = [(d_act * u * (sig * (1.0 + g * (1.0 - sig)))).astype(BF16) for d_act, u, sig, g in zip(d_acts, us, sigs, gs)]
        for rows, d_gate, d_up in zip(halves, d_gates, d_ups):
            dn_ref[0, rows, :] = (_dot(d_gate, wg_ref[0]) + _dot(d_up, wu_ref[0])).astype(BF16)
        d_gate, d_up = jnp.concatenate(d_gates, axis=0), jnp.concatenate(d_ups, axis=0)
        act = jnp.concatenate([(silu * u).astype(BF16) for silu, u in zip(silus, us)], axis=0)
        nn = n_ref[...]
        acc_g[...] += _dot_tn(d_gate, nn)
        acc_u[...] += _dot_tn(d_up, nn)
        acc_d[...] += _dot_tn(act, do_ref[...])

        @pl.when(i == nt - 1)
        def _():
            dw_ref[0, 0, :FF_CHUNK, :] = acc_g[...].astype(BF16)
            dw_ref[0, 0, FF_CHUNK:, :] = acc_u[:cut, :].astype(BF16)
            dw_ref[1, 0, :cut, :] = acc_u[cut:, :].astype(BF16)
            dw_ref[1, 0, cut:, :] = acc_d[...].astype(BF16)

    tok = pl.BlockSpec((tile, D_MODEL), lambda j, i, chunks_ref: (i, 0))
    hid = pl.BlockSpec((1, tile, FF_CHUNK), lambda j, i, chunks_ref: (chunks_ref[j], i, 0))
    return _call(
        body, name, (n_chunks, nt),
        [tok, tok, hid, hid]
        + [pl.BlockSpec((1, FF_CHUNK, D_MODEL), functools.partial(lambda j, i, chunks_ref, part: (chunks_ref[j], part, 0), part=part))
           for part in range(3)],
        [pl.BlockSpec((1, tile, D_MODEL), lambda j, i, chunks_ref: (j, i, 0)),
         pl.BlockSpec((2, 1, half_rows, D_MODEL), lambda j, i, chunks_ref: (0, j, 0, 0))],
        [jax.ShapeDtypeStruct((n_chunks, S, D_MODEL), BF16), jax.ShapeDtypeStruct((2, n_chunks, half_rows, D_MODEL), BF16)],
        [pltpu.VMEM((FF_CHUNK, D_MODEL), F32)] * 3, 56, (d_out, n, gate, up, group, group, group), rider, (chunks,))


def _mix_in(h, gain, w_in, rc, rs1, rs2, name):
    S = h.shape[0]
    tile = min(TOK_TILE, S)

    def body(h_ref, g_ref, w_ref, c_ref, s1_ref, s2_ref, n_ref, q_ref, k_ref, v_ref, pc_ref):
        _, xh = _rms_stats(h_ref[...])
        n = (xh * g_ref[...]).astype(BF16)
        n_ref[...] = n
        u = _dot_nt(n, w_ref[...])
        c, s1, s2 = c_ref[...], s1_ref[...], s2_ref[...]
        q_ref[...] = jnp.concatenate([(_rope(u[:, hd * LANES:(hd + 1) * LANES], c, s1, s2) * SCALE).astype(BF16)
                                      for hd in range(N_HEADS)], axis=1)
        k_ref[...] = _rope(u[:, Q_PAD:Q_PAD + KV_WIDTH], c, s1, s2).astype(BF16)
        v_ref[...] = u[:, Q_PAD + KV_WIDTH:Q_PAD + 2 * KV_WIDTH].astype(BF16)
        pc_ref[...] = u[:, Q_PAD + 2 * KV_WIDTH:]

    return pl.pallas_call(
        body, name=name, grid=(S // tile,),
        in_specs=[_rows(tile, D_MODEL), _full((1, D_MODEL)), _full((U_PAD, D_MODEL)),
                  _rows(tile, LANES), _rows(tile, LANES), _rows(tile, LANES)],
        out_specs=[_rows(tile, D_MODEL), _rows(tile, Q_PAD), _rows(tile, KV_WIDTH), _rows(tile, KV_WIDTH),
                   _rows(tile, POOL_WIDTH)],
        out_shape=[jax.ShapeDtypeStruct((S, D_MODEL), BF16), jax.ShapeDtypeStruct((S, Q_PAD), BF16),
                   jax.ShapeDtypeStruct((S, KV_WIDTH), BF16), jax.ShapeDtypeStruct((S, KV_WIDTH), BF16),
                   jax.ShapeDtypeStruct((S, POOL_WIDTH), F32)],
        compiler_params=_params(("parallel",), 40),
    )(h, gain, w_in, rc, rs1, rs2)


def _band_start(i, S):
    return pl.multiple_of(jnp.clip((i - 1) * BLOCK, 0, S - BAND), BLOCK)


def _window_bias(off):
    r = lax.broadcasted_iota(jnp.int32, (BLOCK, 1), 0)
    c = lax.broadcasted_iota(jnp.int32, (1, BAND), 1)
    return jnp.where(jnp.abs(off + r - c) <= WINDOW, 0.0, NEG).astype(F32)


def _softmax_parts(qh, kb, bias, sink_h):
    s = _dot_nt(qh, kb) + bias
    m = jnp.maximum(jnp.max(s, axis=-1, keepdims=True), sink_h)
    p = jnp.exp(s - m)
    es = jnp.exp(sink_h - m)
    return p, es, 1.0 / (jnp.sum(p, axis=-1, keepdims=True) + es)


def _pool_matrix(t0, start, S, w):
    r = lax.broadcasted_iota(jnp.int32, (BLOCK, 1), 0) + t0
    c = lax.broadcasted_iota(jnp.int32, (1, BAND), 1) + start
    half = w // 2

    def window(lo, hi):
        a = jnp.maximum(lo, 0)
        b = jnp.minimum(hi + 1, S)
        return jnp.where((c >= a) & (c < b), 1.0 / (b - a).astype(F32), 0.0)

    return (0.5 * (window(r - half, r + half - 1) + window(r - half + 1, r + half))).astype(BF16)


def _pool_matrices(S):
    blocks = ((0, 0), (BLOCK, 0), (S - BLOCK, S - BAND))
    return jnp.stack([jnp.stack([_pool_matrix(t0, start, S, w) for w in POOL_WINDOWS]) for t0, start in blocks])


def _pool_spec(nb):
    return pl.BlockSpec((1, N_POOL, BLOCK, BAND), lambda i, *_: (jnp.where(i == 0, 0, jnp.where(i == nb - 1, 2, 1)), 0, 0, 0))


def _mix_core_fwd(q, k, v, pc, sink, pool_m, pool_w, pool_scale, name):
    S = q.shape[0]
    nb = S // BLOCK

    def body(sink_ref, q_ref, k_ref, v_ref, pc_ref, pm_ref, pw_ref, ps_ref, a_ref, p_ref):
        i = pl.program_id(0)
        start = _band_start(i, S)
        band = pl.ds(start, BAND)
        bias = _window_bias(i * BLOCK - start)
        kb, vb = k_ref[band, :], v_ref[band, :]
        hs = range(N_HEADS)
        ss = [_dot_nt(q_ref[:, hd * LANES:(hd + 1) * LANES], kb) + bias for hd in hs]
        ms = [jnp.maximum(jnp.max(ss[hd], axis=-1, keepdims=True), sink_ref[0, hd]) for hd in hs]
        ps = [jnp.exp(ss[hd] - ms[hd]) for hd in hs]
        invs = [1.0 / (jnp.sum(ps[hd], axis=-1, keepdims=True) + jnp.exp(sink_ref[0, hd] - ms[hd])) for hd in hs]
        outs = [_dot(ps[hd].astype(BF16), vb) for hd in hs]
        a_ref[...] = jnp.concatenate([(outs[hd] * invs[hd]).astype(BF16) for hd in hs], axis=1)
        centre = pl.ds(pl.multiple_of(i * BLOCK, BLOCK), BLOCK)
        gs = range(N_POOL)
        sl = [slice(g * POOL_GROUP, (g + 1) * POOL_GROUP) for g in gs]
        means = [_dot(pm_ref[0, g], pc_ref[band, sl[g]].astype(BF16)) for g in gs]
        devs = [(means[g] - pc_ref[centre, sl[g]]).astype(BF16) for g in gs]
        p_ref[...] = (jnp.concatenate([_dot(devs[g], pw_ref[g]) for g in gs], axis=1) * ps_ref[...]).astype(BF16)

    return _call(
        body, name, (nb,),
        [pl.BlockSpec(memory_space=pltpu.SMEM), _rows(BLOCK, Q_PAD), _full((S, KV_WIDTH)), _full((S, KV_WIDTH)),
         _full((S, POOL_WIDTH)), _pool_spec(nb), _full((N_POOL, POOL_GROUP, POOL_GROUP)), _full((1, POOL_WIDTH))],
        [_rows(BLOCK, Q_PAD), _rows(BLOCK, POOL_WIDTH)],
        [jax.ShapeDtypeStruct((S, Q_PAD), BF16), jax.ShapeDtypeStruct((S, POOL_WIDTH), BF16)],
        [], 40, (sink, q, k, v, pc, pool_m, pool_w, pool_scale))


def _mix_core_bwd(q, k, v, pc, da, dp, sink, pool_m, pool_w, pool_scale, rc, rs1, rs2, name, rider=None):
    S = q.shape[0]
    nb = S // BLOCK

    def body(sink_ref, q_ref, k_ref, v_ref, pc_ref, da_ref, dp_ref, pm_ref, pw_ref, ps_ref, c_ref, s1_ref, s2_ref,
             dq_ref, dk_ref, dv_ref, dpc_ref, dsink_ref, dpw_ref, dps_ref):
        i = pl.program_id(0)

        @pl.when(i == 0)
        def _():
            dk_ref[...] = jnp.zeros_like(dk_ref)
            dv_ref[...] = jnp.zeros_like(dv_ref)
            dpc_ref[...] = jnp.zeros_like(dpc_ref)
            dsink_ref[...] = jnp.zeros_like(dsink_ref)
            dpw_ref[...] = jnp.zeros_like(dpw_ref)
            dps_ref[...] = jnp.zeros_like(dps_ref)

        start = _band_start(i, S)
        band = pl.ds(start, BAND)
        bias = _window_bias(i * BLOCK - start)
        kb, vb = k_ref[band, :], v_ref[band, :]
        c, s1, s2 = c_ref[...], s1_ref[...], s2_ref[...]
        lane = lax.broadcasted_iota(jnp.int32, (1, LANES), 1)
        hs = range(N_HEADS)
        qs = [q_ref[:, hd * LANES:(hd + 1) * LANES] for hd in hs]
        das = [da_ref[:, hd * LANES:(hd + 1) * LANES] for hd in hs]
        ss = [_dot_nt(qs[hd], kb) + bias for hd in hs]
        d_probs = [_dot_nt(das[hd], vb) for hd in hs]
        ms = [jnp.maximum(jnp.max(ss[hd], axis=-1, keepdims=True), sink_ref[0, hd]) for hd in hs]
        ps = [jnp.exp(ss[hd] - ms[hd]) for hd in hs]
        ess = [jnp.exp(sink_ref[0, hd] - ms[hd]) for hd in hs]
        invs = [1.0 / (jnp.sum(ps[hd], axis=-1, keepdims=True) + ess[hd]) for hd in hs]
        probs = [ps[hd] * invs[hd] for hd in hs]
        deltas = [jnp.sum(probs[hd] * d_probs[hd], axis=-1, keepdims=True) for hd in hs]
        d_ss = [(probs[hd] * (d_probs[hd] - deltas[hd])).astype(BF16) for hd in hs]
        dqs = [_dot(d_ss[hd], kb) for hd in hs]
        dq_ref[...] = jnp.concatenate([_rope_bwd(dqs[hd] * SCALE, c, s1, s2).astype(BF16) for hd in hs], axis=1)
        dks = [_dot_tn(d_ss[hd], qs[hd]) for hd in hs]
        dvs = [_dot_tn(probs[hd].astype(BF16), das[hd]) for hd in hs]
        dk_ref[band, :] += functools.reduce(lambda a, b: a + b, dks)
        dv_ref[band, :] += functools.reduce(lambda a, b: a + b, dvs)
        dsink_ref[...] += functools.reduce(lambda a, b: a + b, [
            jnp.where(lane == hd, -jnp.sum(ess[hd] * invs[hd] * deltas[hd], axis=0, keepdims=True), 0.0) for hd in hs])

        centre = pl.ds(pl.multiple_of(i * BLOCK, BLOCK), BLOCK)
        gs = range(N_POOL)
        sl = [slice(g * POOL_GROUP, (g + 1) * POOL_GROUP) for g in gs]
        devs = [(_dot(pm_ref[0, g], pc_ref[band, sl[g]].astype(BF16)) - pc_ref[centre, sl[g]]).astype(BF16) for g in gs]
        dys = [dp_ref[:, sl[g]].astype(F32) for g in gs]
        zs = [_dot(devs[g], pw_ref[g]) for g in gs]
        dzs = [(dys[g] * ps_ref[:, sl[g]]).astype(BF16) for g in gs]
        d_devs = [_dot_nt(dzs[g], pw_ref[g]) for g in gs]
        dps_ref[...] += jnp.concatenate([jnp.sum(dys[g] * zs[g], axis=0, keepdims=True) for g in gs], axis=1)
        for g in gs:
            dpw_ref[g] += _dot_tn(devs[g], dzs[g])
        dpc_ref[band, :] += jnp.concatenate([_dot_tn(pm_ref[0, g], d_devs[g].astype(BF16)) for g in gs], axis=1)
        dpc_ref[centre, :] -= jnp.concatenate(d_devs, axis=1)

    return _call(
        body, name, (nb,),
        [pl.BlockSpec(memory_space=pltpu.SMEM), _rows(BLOCK, Q_PAD), _full((S, KV_WIDTH)), _full((S, KV_WIDTH)),
         _full((S, POOL_WIDTH)), _rows(BLOCK, Q_PAD), _rows(BLOCK, POOL_WIDTH), _pool_spec(nb),
         _full((N_POOL, POOL_GROUP, POOL_GROUP)), _full((1, POOL_WIDTH)),
         _rows(BLOCK, LANES), _rows(BLOCK, LANES), _rows(BLOCK, LANES)],
        [_rows(BLOCK, Q_PAD), _full((S, KV_WIDTH)), _full((S, KV_WIDTH)), _full((S, POOL_WIDTH)),
         _full((1, LANES)), _full((N_POOL, POOL_GROUP, POOL_GROUP)), _full((1, POOL_WIDTH))],
        [jax.ShapeDtypeStruct((S, Q_PAD), BF16), jax.ShapeDtypeStruct((S, KV_WIDTH), F32),
         jax.ShapeDtypeStruct((S, KV_WIDTH), F32), jax.ShapeDtypeStruct((S, POOL_WIDTH), F32),
         jax.ShapeDtypeStruct((1, LANES), F32), jax.ShapeDtypeStruct((N_POOL, POOL_GROUP, POOL_GROUP), F32),
         jax.ShapeDtypeStruct((1, POOL_WIDTH), F32)],
        [], 56, (sink, q, k, v, pc, da, dp, pool_m, pool_w, pool_scale, rc, rs1, rs2), rider)


def _mix_out(h, a, p, wa, wp, name):
    S = h.shape[0]
    tile = min(TOK_TILE, S)

    def body(h_ref, a_ref, p_ref, wa_ref, wp_ref, o_ref):
        o_ref[...] = h_ref[...] + _dot(a_ref[...], wa_ref[...]) + _dot(p_ref[...], wp_ref[...])

    return pl.pallas_call(
        body, name=name, grid=(S // tile,),
        in_specs=[_rows(tile, D_MODEL), _rows(tile, Q_PAD), _rows(tile, POOL_WIDTH),
                  _full((Q_PAD, D_MODEL)), _full((POOL_WIDTH, D_MODEL))],
        out_specs=_rows(tile, D_MODEL),
        out_shape=jax.ShapeDtypeStruct((S, D_MODEL), F32),
        compiler_params=_params(("parallel",), 40),
    )(h, a, p, wa, wp)


def _mix_out_bwd(dh_out, dn, h, gain, a, p, wa, wp, name, rider=None):
    S = h.shape[0]
    tile = min(TOK_TILE, S)

    def body(do_ref, dn_ref, h_ref, g_ref, a_ref, p_ref, wa_ref, wp_ref, dh_ref, da_ref, dp_ref, dwa_ref, dwp_ref, dg_ref):
        @pl.when(pl.program_id(0) == 0)
        def _():
            dwa_ref[...] = jnp.zeros_like(dwa_ref)
            dwp_ref[...] = jnp.zeros_like(dwp_ref)
            dg_ref[...] = jnp.zeros_like(dg_ref)

        r, xh = _rms_stats(h_ref[...])
        dnorm, dg = _rms_bwd(_sum_chunks([dn_ref]), g_ref[...], r, xh)
        dh = do_ref[...] + dnorm
        dg_ref[...] += dg
        dh_ref[...] = dh
        dhb = dh.astype(BF16)
        da_ref[...] = _dot_nt(dhb, wa_ref[...]).astype(BF16)
        dp_ref[...] = _dot_nt(dhb, wp_ref[...]).astype(BF16)
        dwa_ref[...] += _dot_tn(a_ref[...], dhb)
        dwp_ref[...] += _dot_tn(p_ref[...], dhb)

    return _call(
        body, name, (S // tile,),
        [_rows(tile, D_MODEL), _chunk_rows(tile, dn.shape[0]), _rows(tile, D_MODEL), _full((1, D_MODEL)),
         _rows(tile, Q_PAD), _rows(tile, POOL_WIDTH), _full((Q_PAD, D_MODEL)), _full((POOL_WIDTH, D_MODEL))],
        [_rows(tile, D_MODEL), _rows(tile, Q_PAD), _rows(tile, POOL_WIDTH),
         _full((Q_PAD, D_MODEL)), _full((POOL_WIDTH, D_MODEL)), _full((1, D_MODEL))],
        [jax.ShapeDtypeStruct((S, D_MODEL), F32), jax.ShapeDtypeStruct((S, Q_PAD), BF16),
         jax.ShapeDtypeStruct((S, POOL_WIDTH), BF16), jax.ShapeDtypeStruct((Q_PAD, D_MODEL), F32),
         jax.ShapeDtypeStruct((POOL_WIDTH, D_MODEL), F32), jax.ShapeDtypeStruct((1, D_MODEL), F32)],
        [], 48, (dh_out, dn, h, gain, a, p, wa, wp), rider)


def _mix_in_bwd(dh_out, h, gain, n, dq, dk, dv, dpc, rc, rs1, rs2, w_in, name, rider=None):
    S = h.shape[0]
    tile = min(TOK_TILE, S)

    def body(do_ref, h_ref, g_ref, n_ref, dq_ref, dk_ref, dv_ref, dpc_ref, c_ref, s1_ref, s2_ref, w_ref,
             dh_ref, dhalf_ref, dw_ref, dg_ref):
        @pl.when(pl.program_id(0) == 0)
        def _():
            dw_ref[...] = jnp.zeros_like(dw_ref)
            dg_ref[...] = jnp.zeros_like(dg_ref)

        dk = _rope_bwd(dk_ref[...], c_ref[...], s1_ref[...], s2_ref[...]).astype(BF16)
        du = jnp.concatenate([dq_ref[...], dk, dv_ref[...].astype(BF16), dpc_ref[...].astype(BF16)], axis=1)
        dn = _dot(du, w_ref[...])
        dw_ref[...] += _dot_tn(du, n_ref[...])
        r, xh = _rms_stats(h_ref[...])
        dnorm, dg = _rms_bwd(dn, g_ref[...], r, xh)
        dh = do_ref[...] + dnorm
        dg_ref[...] += dg
        dh_ref[...] = dh
        dhalf_ref[...] = (0.5 * dh).astype(BF16)

    return _call(
        body, name, (S // tile,),
        [_rows(tile, D_MODEL), _rows(tile, D_MODEL), _full((1, D_MODEL)), _rows(tile, D_MODEL),
         _rows(tile, Q_PAD), _rows(tile, KV_WIDTH), _rows(tile, KV_WIDTH), _rows(tile, POOL_WIDTH),
         _rows(tile, LANES), _rows(tile, LANES), _rows(tile, LANES), _full((U_PAD, D_MODEL))],
        [_rows(tile, D_MODEL), _rows(tile, D_MODEL), _full((U_PAD, D_MODEL)), _full((1, D_MODEL))],
        [jax.ShapeDtypeStruct((S, D_MODEL), F32), jax.ShapeDtypeStruct((S, D_MODEL), BF16),
         jax.ShapeDtypeStruct((U_PAD, D_MODEL), F32), jax.ShapeDtypeStruct((1, D_MODEL), F32)],
        [], 56, (dh_out, h, gain, n, dq, dk, dv, dpc, rc, rs1, rs2, w_in), rider)


def _norm_bwd(dh_out, dns, h, gain, name, rider=None):
    S = h.shape[0]
    tile = min(TOK_TILE, S)
    n = len(dns)

    def body(do_ref, *refs):
        h_ref, g_ref, dh_ref, dg_ref = refs[n:]

        @pl.when(pl.program_id(0) == 0)
        def _():
            dg_ref[...] = jnp.zeros_like(dg_ref)

        r, xh = _rms_stats(h_ref[...])
        dnorm, dg = _rms_bwd(_sum_chunks(refs[:n]), g_ref[...], r, xh)
        dg_ref[...] += dg
        dh_ref[...] = do_ref[...] + dnorm

    return _call(
        body, name, (S // tile,),
        [_rows(tile, D_MODEL)] + [_chunk_rows(tile, dn.shape[0]) for dn in dns] + [_rows(tile, D_MODEL), _full((1, D_MODEL))],
        [_rows(tile, D_MODEL), _full((1, D_MODEL))],
        [jax.ShapeDtypeStruct((S, D_MODEL), F32), jax.ShapeDtypeStruct((1, D_MODEL), F32)],
        [], 40, (dh_out, *dns, h, gain), rider)


def _rope_tables(S):
    half = ROTARY_DIM // 2
    inv_freq = ROPE_THETA ** (-jnp.arange(0, ROTARY_DIM, 2, dtype=F32) / ROTARY_DIM)
    dim = jnp.arange(LANES) % HEAD_DIM
    ang = jnp.arange(S, dtype=F32)[:, None] * inv_freq[dim % half][None, :]
    lo, hi = (dim < half)[None, :], ((dim >= half) & (dim < ROTARY_DIM))[None, :]
    c = jnp.where(lo | hi, jnp.cos(ang), 1.0)
    s1 = jnp.where(lo, -jnp.sin(ang), 0.0)
    s2 = jnp.where(hi, jnp.sin(ang), 0.0)
    return c, s1, s2


def _pad_heads(w, axis):
    w = jnp.moveaxis(w, axis, 0)
    heads = w.reshape((N_HEADS, HEAD_DIM) + w.shape[1:])
    zero = jnp.zeros_like(heads)
    first = (jnp.arange(N_HEADS) < Q_PER_KV).reshape((N_HEADS, 1) + (1,) * (w.ndim - 1))
    lo = jnp.where(first, heads, zero)
    hi = jnp.where(first, zero, heads)
    padded = jnp.concatenate([lo, hi], axis=1).reshape((Q_PAD,) + w.shape[1:])
    return jnp.moveaxis(padded, 0, axis)


def _unpad_heads(w, axis):
    w = jnp.moveaxis(w, axis, 0)
    groups = w.reshape((N_HEADS, 2, HEAD_DIM) + w.shape[1:])
    first = (jnp.arange(N_HEADS) < Q_PER_KV).reshape((N_HEADS, 1) + (1,) * (w.ndim - 1))
    heads = jnp.where(first, groups[:, 0], groups[:, 1]).reshape((ATTN_WIDTH,) + w.shape[1:])
    return jnp.moveaxis(heads, 0, axis)


IN_ROWS = IN_WIDTH // N_CHIPS
OUT_ROWS = (ATTN_WIDTH + POOL_WIDTH) // N_CHIPS
MIX_ROWS = IN_ROWS + OUT_ROWS
FFN_ROWS = 3 * FF_CHUNK


def _step(x, target, bufs, small, place):
    S = x.shape[0]
    rc, rs1, rs2 = _rope_tables(S)
    mine = place[1]
    order = jnp.stack([mine, mine ^ 2, mine ^ 1, mine ^ 3])
    h1, n1, gate1, up1, ffn1, mix = _ffn_fwd(order, x, small["ffn1_norm"], "ffn1_fwd",
                                             _merge(_allgather(bufs[:1], in_passes=0), _allgather(bufs[1:2])), 0)
    in_flight, ffn2, token = _gather_start(bufs[2], X_Y_DIAGONAL[:2], "ffn2_gather_start")
    w_in_t = mix[:, :IN_ROWS].reshape(IN_WIDTH, D_MODEL)
    w_in_pad = jnp.concatenate([_pad_heads(w_in_t[:ATTN_WIDTH], 0), w_in_t[ATTN_WIDTH:]], axis=0)
    w_out = mix[:, IN_ROWS:].reshape(ATTN_WIDTH + POOL_WIDTH, D_MODEL)
    wa = _pad_heads(w_out[:ATTN_WIDTH], 0)
    wp = w_out[ATTN_WIDTH:]
    pool_w = small["pool_w"].astype(BF16)

    n2, q, k, v, pc = _mix_in(h1, small["mix_norm"] + token[0, 0], w_in_pad, rc, rs1, rs2, "mix_in")
    pool_m = _pool_matrices(S)
    a, p = _mix_core_fwd(q, k, v, pc, small["sink_logits"], pool_m, pool_w, small["pool_scale"], "mix_core_fwd")
    h2 = _mix_out(h1, a, p, wa, wp, "mix_out")
    ffn2 = _gather_wait(in_flight, ffn2, X_Y_DIAGONAL[:2], h2, "ffn2_gather_wait")
    dh3, dhalf3, loss_lanes, d_final, n3, gate2, up2, ffn2 = _ffn_fwd(
        order, h2, small["ffn2_norm"], "ffn2_fwd",
        _allgather([ffn2], peers=X_Y_DIAGONAL[2:], in_passes=0, landed_before=X_Y_DIAGONAL[:2]), 0,
        loss_head=(target, small["final_norm"]))

    dn3, d_ffn2 = _ffn_bwd(jnp.arange(N_CHIPS, dtype=jnp.int32), dhalf3, n3, gate2, up2, ffn2, "ffn2_bwd")
    dh2, da, dp, dwa, dwp, d_ffn2_norm, received = _mix_out_bwd(dh3, dn3, h2, small["ffn2_norm"], a, p, wa, wp, "mix_out_bwd",
                                                                _sibling_exchange([d_ffn2]))
    pair = _pair_sum(place, d_ffn2, received, "grad_pair_sum_ffn2")
    dq, dk, dv, dpc, dsink, dpool_w, dpool_scale, stack = _mix_core_bwd(
        q, k, v, pc, da, dp, small["sink_logits"], pool_m, pool_w, small["pool_scale"], rc, rs1, rs2, "mix_core_bwd",
        _scatter([pair], peers=X_Y_DIAGONAL[:2]))
    dh1, dhalf1, dw_in_pad, d_mix_norm, stack = _mix_in_bwd(dh2, h1, small["mix_norm"], n2, dq, dk, dv, dpc, rc, rs1, rs2,
                                                            w_in_pad, "mix_in_bwd",
                                                            _scatter([pair], peers=X_Y_DIAGONAL[2:], stacks=[stack]))
    reduced_ffn2 = _chip_sum(place, pair, stack, 2, "grad_chip_sum_ffn2")
    dw_in_t = jnp.concatenate([_unpad_heads(dw_in_pad[:Q_PAD], 0), dw_in_pad[Q_PAD:]], axis=0)
    dw_out = jnp.concatenate([_unpad_heads(dwa, 0), dwp], axis=0)
    d_mix = jnp.concatenate([dw_in_t.reshape(N_CHIPS, IN_ROWS, D_MODEL), dw_out.reshape(N_CHIPS, OUT_ROWS, D_MODEL)], axis=1)
    d_mix = jnp.transpose(d_mix.reshape(N_CHIPS, 2, MIX_ROWS // 2, D_MODEL), (1, 0, 2, 3)).astype(BF16)
    small_g = {"ffn1_norm": jnp.zeros_like(d_mix_norm), "mix_norm": d_mix_norm, "ffn2_norm": d_ffn2_norm,
               "final_norm": d_final, "pool_scale": dpool_scale, "sink_logits": dsink[:, :N_HEADS], "pool_w": dpool_w}
    loss_row = jnp.sum(loss_lanes.reshape(D_MODEL // LANES, LANES), axis=0, keepdims=True)
    small_early = _pack_small(small_g, loss_row)

    chunk = [(place[1:] + 1 + p) % N_CHIPS for p in range(N_CHIPS)]
    ffn1_bwd = functools.partial(_ffn_bwd, d_out=dhalf1, n=n1, gate=gate1, up=up1, group=ffn1)
    dn_a, dw_a, recv_mix, small_all, g_ffn2 = ffn1_bwd(
        chunk[0], name="ffn1_bwd_0",
        rider=_merge(_merge(_sibling_exchange([d_mix]), _small_allgather(small_early)), _sibling_share([reduced_ffn2])))
    pair_mix = _pair_sum(place, d_mix, recv_mix, "grad_pair_sum_mix")
    dn_b, dw_b, recv_a, stack_mix = ffn1_bwd(chunk[1], name="ffn1_bwd_1",
                                             rider=_merge(_sibling_exchange([dw_a]), _scatter([pair_mix])))
    pair_a = _pair_sum(place, dw_a, recv_a, "grad_pair_sum_ffn1_0")
    reduced_mix = _chip_sum(place, pair_mix, stack_mix, 2, "grad_chip_sum_mix")
    dn_c, dw_c, recv_b, stack, g_mix = ffn1_bwd(
        chunk[2], name="ffn1_bwd_2",
        rider=_merge(_merge(_sibling_exchange([dw_b]), _scatter_step(pair_a, stack, 0)), _sibling_share([reduced_mix])))
    pair_b = _pair_sum(place, dw_b, recv_b, "grad_pair_sum_ffn1_1")
    dn_d, dw_d, recv_c, stack = ffn1_bwd(chunk[3], name="ffn1_bwd_3",
                                         rider=_merge(_sibling_exchange([dw_c]), _scatter_step(pair_b, stack, 1)))
    pair_c = _pair_sum(place, dw_c, recv_c, "grad_pair_sum_ffn1_2")
    grad_x, d_ffn1_norm, recv_d, stack = _norm_bwd(dh1, [dn_a, dn_b, dn_c, dn_d], x, small["ffn1_norm"], "norm1_bwd",
                                                   _merge(_sibling_exchange([dw_d]), _scatter_step(pair_c, stack, 2)))
    pair_d = _pair_sum(place, dw_d, recv_d, "grad_pair_sum_ffn1_3")
    reduced_ffn1 = _chip_sum(place, pair_d, stack, 2, "grad_chip_sum_ffn1")
    g_ffn1, gains = _comm_call(_merge(_sibling_share([reduced_ffn1]), _small_allgather(d_ffn1_norm.reshape(-1, LANES))),
                               "grad_share_tail")
    gain_sum = _sum_leading(gains, 1, "gain_grad_sum")
    small_sum = jnp.concatenate([gain_sum, _sum_leading(small_all, 1, "small_grad_sum")[gain_sum.shape[0]:]], axis=0)
    return jnp.sum(small_sum[SMALL_ROWS - 1]), grad_x, [g.reshape(-1, D_MODEL) for g in (g_ffn1, g_mix, g_ffn2)], small_sum


GROUPS =(("ffn1_w_gate", "ffn1_w_up", "ffn1_w_down"), ("w_in", "w_out"), ("ffn2_w_gate", "ffn2_w_up", "ffn2_w_down"))
TRANSPOSED = ("ffn1_w_gate", "ffn1_w_up", "w_in", "ffn2_w_gate", "ffn2_w_up")


def _place():
    x, y, c = lax.axis_index("x"), lax.axis_index("y"), lax.axis_index("c")
    chips = [(1 - x, y), (x, 1 - y), (1 - x, 1 - y)]
    return x, y, c, chips


def _remote(src, dst, send_sem, recv_sem, to):
    return pltpu.make_async_remote_copy(src_ref=src, dst_ref=dst, send_sem=send_sem, recv_sem=recv_sem,
                                        device_id=to, device_id_type=MESH)


def _pack(chip, members, name):
    rows = members[0].shape[0]
    n = len(members)

    def body(chip_ref, *refs):
        ins, out_ref, buf, sems = refs[:n], refs[n], refs[n + 1], refs[n + 2]
        copies = [pltpu.make_async_copy(ins[k], buf.at[k], sems.at[k]) for k in range(n)]
        for cp in copies:
            cp.start()
        for k in range(n):
            copies[k].wait()
            out_ref[0, k * rows:(k + 1) * rows, :] = buf[k].astype(BF16)

    return pl.pallas_call(
        body, name=name,
        grid_spec=pltpu.PrefetchScalarGridSpec(
            num_scalar_prefetch=1, grid=(1,),
            in_specs=[HBM_SPEC] * n,
            out_specs=pl.BlockSpec((1, n * rows, D_MODEL), lambda k, chip_ref: (chip_ref[0], 0, 0)),
            scratch_shapes=[pltpu.VMEM((n, rows, D_MODEL), F32), pltpu.SemaphoreType.DMA((n,))]),
        out_shape=jax.ShapeDtypeStruct((N_CHIPS, n * rows, D_MODEL), BF16),
        compiler_params=_params(("arbitrary",), 40),
    )(chip, *members)


def _same(arrays):
    return [jax.ShapeDtypeStruct(a.shape, a.dtype) for a in arrays]


X_Y_DIAGONAL = (0, 1, 2)


def _allgather(bufs, peers=X_Y_DIAGONAL, in_passes=None, landed_before=()):
    n = len(bufs)

    def copy(kind, outs, send_sems, recv_sems, a, k):
        x, y, c, chips = _place()
        half = bufs[a].shape[1] // 2

        def rows(slot, core):
            return outs[a].at[slot, pl.ds(pl.multiple_of(core * half, 16), half)]

        me, slot = 2 * x + y, 2 * chips[k][0] + chips[k][1]
        over_ici = (send_sems.at[6 * a + k], recv_sems.at[6 * a + k])
        over_d2d = (send_sems.at[6 * a + 3 + k], recv_sems.at[6 * a + 3 + k])
        if kind == "first":
            return _remote(rows(me, c), rows(me, c), *over_ici, (*chips[k], c))
        if kind == "landed":
            return _remote(rows(me, c), rows(slot, c), *over_ici, (*chips[k], c))
        if kind == "passed":
            return _remote(rows(slot, c), rows(slot, c), *over_d2d, (x, y, 1 - c))
        return _remote(rows(me, c), rows(slot, 1 - c), *over_d2d, (x, y, 1 - c))

    def arrive(pairs):
        def hook(ins, outs, sems):
            for a, k in pairs:
                if k not in landed_before:
                    copy("landed", outs, *sems, a, k).wait_recv()
                copy("passed", outs, *sems, a, k).start()
            for a, k in pairs:
                copy("handed", outs, *sems, a, k).wait_recv()
        return hook

    sent = [(a, k) for a in range(n) for k in peers]
    everything = sent + [(in_passes, k) for k in landed_before]
    early = [(a, k) for a, k in everything if a == in_passes]

    def start(ins, outs, sems):
        for a, k in sent:
            copy("first", outs, *sems, a, k).start()

    def finish(ins, outs, sems):
        arrive([pair for pair in everything if pair not in early])(ins, outs, sems)
        for a, k in sent:
            copy("first", outs, *sems, a, k).wait_send()
        for a, k in everything:
            copy("passed", outs, *sems, a, k).wait_send()

    hooks = tuple(((k + 1, 0), arrive([(a, k)])) for a, k in early)
    return _Rider(list(bufs), _same(bufs), {a: a for a in range(n)},
                  [pltpu.SemaphoreType.DMA((6 * n,)), pltpu.SemaphoreType.DMA((6 * n,))], start, finish, hooks)


SEM_SPEC = pl.BlockSpec(memory_space=pltpu.SEMAPHORE)


def _gather_copies(buf_ref, sems, peers, landing):
    x, y, c, chips = _place()
    half = buf_ref.shape[1] // 2
    me = 2 * x + y

    def rows(slot):
        return buf_ref.at[slot, pl.ds(pl.multiple_of(c * half, 16), half)]

    return [_remote(rows(me), rows(2 * chips[k][0] + chips[k][1] if landing else me), sems[2 * j], sems[2 * j + 1], (*chips[k], c))
            for j, k in enumerate(peers)]


def _gather_start(buf, peers, name):
    n = 2 * len(peers)

    def body(buf_ref, *refs):
        sems, token = refs[:n], refs[n + 1]
        for cp in _gather_copies(refs[n], sems, peers, landing=False):
            cp.start()
        token[...] = jnp.zeros_like(token)

    outs = pl.pallas_call(
        body, name=name,
        out_shape=[pltpu.SemaphoreType.DMA(())] * n + [jax.ShapeDtypeStruct(buf.shape, buf.dtype),
                                                       jax.ShapeDtypeStruct((SUBLANES, LANES), F32)],
        in_specs=[HBM_SPEC], out_specs=[SEM_SPEC] * n + [HBM_SPEC, pl.BlockSpec(memory_space=pltpu.VMEM)],
        input_output_aliases={0: n},
        compiler_params=pltpu.CompilerParams(has_side_effects=pltpu.SideEffectType.DATAFLOW_SIDE_EFFECTING),
    )(buf)
    return outs[:n], outs[n], outs[n + 1]


def _gather_wait(sems, buf, peers, after, name):
    n = len(sems)

    def body(buf_ref, *refs):
        sem_refs, out_ref = refs[:n], refs[n + 1]
        for cp in _gather_copies(out_ref, sem_refs, peers, landing=True):
            cp.wait_send()
            cp.wait_recv()

    return pl.pallas_call(
        body, name=name, out_shape=jax.ShapeDtypeStruct(buf.shape, buf.dtype),
        in_specs=[HBM_SPEC] + [SEM_SPEC] * n + [pl.BlockSpec(memory_space=pl.ANY)], out_specs=HBM_SPEC,
        input_output_aliases={0: 0},
        compiler_params=pltpu.CompilerParams(has_side_effects=pltpu.SideEffectType.DATAFLOW_SIDE_EFFECTING),
    )(buf, *sems, after)


def _sibling_exchange(parts):
    n = len(parts)

    def copies(ins, outs, send_sems, recv_sems):
        x, y, c, _ = _place()
        return [_remote(ins[a].at[1 - c], outs[a], send_sems.at[a], recv_sems.at[a], (x, y, 1 - c)) for a in range(n)]

    def start(ins, outs, sems):
        for cp in copies(ins, outs, *sems):
            cp.start()

    def finish(ins, outs, sems):
        for cp in copies(ins, outs, *sems):
            cp.wait_recv()
            cp.wait_send()

    return _Rider(list(parts), [jax.ShapeDtypeStruct(p.shape[1:], p.dtype) for p in parts], {},
                  [pltpu.SemaphoreType.DMA((n,)), pltpu.SemaphoreType.DMA((n,))], start, finish)


def _small_allgather(small):
    flips = [(fx, fy, fc) for fx in range(2) for fy in range(2) for fc in range(2)][1:]

    def copies(small_ref, gather_ref, send_sems, recv_sems, local_sem, started_only=False):
        x, y, c, _ = _place()
        me = 4 * x + 2 * y + c
        peers = [((1 - x) if fx else x, (1 - y) if fy else y, (1 - c) if fc else c) for fx, fy, fc in flips]
        own = pltpu.make_async_copy(small_ref, gather_ref.at[me], local_sem)
        sent = [_remote(small_ref, gather_ref.at[me], send_sems.at[k], recv_sems.at[k], peer) for k, peer in enumerate(peers)]
        if started_only:
            return own, sent
        landed = [_remote(small_ref, gather_ref.at[4 * px + 2 * py + pc], send_sems.at[k], recv_sems.at[k], (px, py, pc))
                  for k, (px, py, pc) in enumerate(peers)]
        return own, sent, landed

    def start(ins, outs, sems):
        own, sent = copies(ins[0], outs[0], *sems, started_only=True)
        own.start()
        for cp in sent:
            cp.start()

    def finish(ins, outs, sems):
        own, sent, landed = copies(ins[0], outs[0], *sems)
        for cp in landed:
            cp.wait_recv()
        for cp in sent:
            cp.wait_send()
        own.wait()

    return _Rider([small], [jax.ShapeDtypeStruct((2 * N_CHIPS,) + small.shape, small.dtype)], {},
                  [pltpu.SemaphoreType.DMA((7,)), pltpu.SemaphoreType.DMA((7,)), pltpu.SemaphoreType.DMA], start, finish)


def _merge(a, b):
    na, nao, nas = len(a.operands), len(a.out_shapes), len(a.scratch)

    def start(ins, outs, sems):
        a.start(ins[:na], outs[:nao], sems[:nas])
        b.start(ins[na:], outs[nao:], sems[nas:])

    def finish(ins, outs, sems):
        a.finish(ins[:na], outs[:nao], sems[:nas])
        b.finish(ins[na:], outs[nao:], sems[nas:])

    def of_a(fn):
        return lambda ins, outs, sems: fn(ins[:na], outs[:nao], sems[:nas])

    def of_b(fn):
        return lambda ins, outs, sems: fn(ins[na:], outs[nao:], sems[nas:])

    aliases = {**a.aliases, **{na + k: nao + v for k, v in b.aliases.items()}}
    hooks = tuple((at, of_a(fn)) for at, fn in a.hooks) + tuple((at, of_b(fn)) for at, fn in b.hooks)
    return _Rider(a.operands + b.operands, a.out_shapes + b.out_shapes, aliases, a.scratch + b.scratch, start, finish, hooks)


def _scatter_step(pair, stack, step):
    def copies(pair_ref, stack_ref, send_sem, recv_sem, started_only=False):
        x, y, c, _ = _place()
        me = 2 * x + y
        to = (me + 1 + step) % N_CHIPS
        frm = (me + N_CHIPS - 1 - step) % N_CHIPS
        sent = _remote(pair_ref.at[0], stack_ref.at[me], send_sem, recv_sem, (to // 2, to % 2, c))
        if started_only:
            return sent
        landed = _remote(pair_ref.at[0], stack_ref.at[frm], send_sem, recv_sem, (frm // 2, frm % 2, c))
        return sent, landed

    def start(ins, outs, sems):
        copies(ins[0], outs[0], *sems, started_only=True).start()

    def finish(ins, outs, sems):
        sent, landed = copies(ins[0], outs[0], *sems)
        landed.wait_recv()
        sent.wait_send()

    return _Rider([pair, stack], _same([stack]), {1: 0}, [pltpu.SemaphoreType.DMA, pltpu.SemaphoreType.DMA], start, finish)


def _scatter(sums, peers=X_Y_DIAGONAL, stacks=None):
    n = len(sums)

    def copies(ins, outs, send_sems, recv_sems, started_only=False):
        x, y, c, chips = _place()
        me = 2 * x + y
        slots = [2 * cx + cy for cx, cy in chips]
        sent = [_remote(ins[a].at[slots[k]], outs[a].at[me], send_sems.at[3 * a + k], recv_sems.at[3 * a + k], (*chips[k], c))
                for a in range(n) for k in peers]
        if started_only:
            return sent
        landed = [_remote(ins[a].at[slots[k]], outs[a].at[slots[k]], send_sems.at[3 * a + k], recv_sems.at[3 * a + k],
                          (*chips[k], c)) for a in range(n) for k in peers]
        return sent, landed

    def start(ins, outs, sems):
        for cp in copies(ins, outs, *sems, started_only=True):
            cp.start()

    def finish(ins, outs, sems):
        sent, landed = copies(ins, outs, *sems)
        for cp in landed:
            cp.wait_recv()
        for cp in sent:
            cp.wait_send()

    sems = [pltpu.SemaphoreType.DMA((3 * n,)), pltpu.SemaphoreType.DMA((3 * n,))]
    if stacks is None:
        return _Rider(list(sums), _same(sums), {}, sems, start, finish)
    return _Rider(list(sums) + list(stacks), _same(sums), {n + a: a for a in range(n)}, sems, start, finish)


def _sibling_share(bufs):
    n = len(bufs)

    def copies(outs, send_sems, recv_sems, started_only=False):
        x, y, c, _ = _place()
        sent = [_remote(outs[a].at[c], outs[a].at[c], send_sems.at[a], recv_sems.at[a], (x, y, 1 - c)) for a in range(n)]
        if started_only:
            return sent
        landed = [_remote(outs[a].at[c], outs[a].at[1 - c], send_sems.at[a], recv_sems.at[a], (x, y, 1 - c)) for a in range(n)]
        return sent, landed

    def start(ins, outs, sems):
        for cp in copies(outs, *sems, started_only=True):
            cp.start()

    def finish(ins, outs, sems):
        sent, landed = copies(outs, *sems)
        for cp in landed:
            cp.wait_recv()
        for cp in sent:
            cp.wait_send()

    return _Rider(list(bufs), _same(bufs), {a: a for a in range(n)},
                  [pltpu.SemaphoreType.DMA((n,)), pltpu.SemaphoreType.DMA((n,))], start, finish)


def _pair_sum(core, part, received, name):
    _, k, rh, cols = part.shape

    def body(core_ref, p_ref, r_ref, o_ref):
        o_ref[...] = (p_ref[0].astype(F32) + r_ref[...].astype(F32)).astype(BF16)

    return pl.pallas_call(
        body, name=name,
        grid_spec=pltpu.PrefetchScalarGridSpec(
            num_scalar_prefetch=1, grid=(k,),
            in_specs=[pl.BlockSpec((1, 1, rh, cols), lambda j, core_ref: (core_ref[0], j, 0, 0)),
                      pl.BlockSpec((1, rh, cols), lambda j, core_ref: (j, 0, 0))],
            out_specs=pl.BlockSpec((1, rh, cols), lambda j, core_ref: (j, 0, 0))),
        out_shape=jax.ShapeDtypeStruct((k, rh, cols), BF16),
        compiler_params=_params(("parallel",), 32),
    )(core, part, received)


def _sum_leading(stack, steps, name):
    k, rows, cols = stack.shape
    tile = rows // steps

    def body(s_ref, o_ref):
        total = s_ref[0].astype(F32)
        for d in range(1, k):
            total = total + s_ref[d].astype(F32)
        o_ref[...] = total

    return pl.pallas_call(
        body, name=name, grid=(steps,),
        in_specs=[pl.BlockSpec((k, tile, cols), lambda i: (0, i, 0))],
        out_specs=pl.BlockSpec((tile, cols), lambda i: (i, 0)),
        out_shape=jax.ShapeDtypeStruct((rows, cols), F32),
        compiler_params=_params(("parallel",), 32),
    )(stack)


def _chip_sum(place, own, stack, steps, name):
    k, rows, cols = stack.shape
    tile = rows // steps

    def body(place_ref, own_ref, *refs):
        chip = place_ref[1]
        total = None
        for d in range(k):
            term = jnp.where(chip == d, own_ref[0], refs[d][0]).astype(F32)
            total = term if total is None else total + term
        refs[k][0] = total

    def other(d):
        return lambda i, place_ref: (jnp.where(place_ref[1] == d, (d + 1) % k, d), i, 0)

    return pl.pallas_call(
        body, name=name,
        grid_spec=pltpu.PrefetchScalarGridSpec(
            num_scalar_prefetch=1, grid=(steps,),
            in_specs=[pl.BlockSpec((1, tile, cols), lambda i, place_ref: (place_ref[1] % own.shape[0], i, 0))]
            + [pl.BlockSpec((1, tile, cols), other(d)) for d in range(k)],
            out_specs=pl.BlockSpec((1, tile, cols), lambda i, place_ref: (place_ref[0], i, 0))),
        out_shape=jax.ShapeDtypeStruct((2, rows, cols), F32),
        compiler_params=_params(("arbitrary",), 32),
    )(place, own, *([stack] * k))


def _adamw(w, g, row0, m, v, tile, name):
    rows, cols = w.shape
    first = row0 // tile
    assert rows % tile == 0 and row0 % tile == 0
    bc1 = 1.0 - ADAM_B1 ** ADAM_STEP
    bc2 = 1.0 - ADAM_B2 ** ADAM_STEP

    def body(w_ref, g_ref, m_ref, v_ref, go_ref, d_ref, mo_ref, vo_ref):
        g = g_ref[...]
        m_new = ADAM_B1 * m_ref[...] + (1.0 - ADAM_B1) * g
        v_new = ADAM_B2 * v_ref[...] + (1.0 - ADAM_B2) * (g * g)
        go_ref[...] = g
        d_ref[...] = -ADAM_LR * ((m_new / bc1) / (jnp.sqrt(v_new / bc2) + ADAM_EPS) + ADAM_WD * w_ref[...])
        mo_ref[...] = m_new
        vo_ref[...] = v_new

    spec = pl.BlockSpec((tile, cols), lambda i: (i, 0))
    g_spec = pl.BlockSpec((tile, cols), lambda i: (first + i, 0))
    return pl.pallas_call(
        body, name=name, grid=(rows // tile,),
        in_specs=[spec, g_spec, spec, spec], out_specs=[spec] * 4,
        out_shape=[jax.ShapeDtypeStruct((rows, cols), F32)] * 4,
        compiler_params=_params(("parallel",), 32),
    )(w, g, m, v)


SMALL = ("ffn1_norm", "mix_norm", "ffn2_norm", "final_norm", "pool_scale", "sink_logits", "pool_w")


def _pack_small(d, last_row=None):
    def part(n):
        flat = d[n].reshape(-1)
        flat = jnp.pad(flat, (0, -flat.shape[0] % (SUBLANES * LANES)))
        return flat.reshape(-1, LANES)

    last = jnp.zeros((SUBLANES, LANES), F32) if last_row is None else jnp.pad(last_row, ((SUBLANES - 1, 0), (0, 0)))
    packed = jnp.concatenate([part(n) for n in SMALL] + [last], axis=0)
    assert packed.shape[0] == SMALL_ROWS
    return packed


def _unpack_small(packed, like):
    out, row = {}, 0
    for n in SMALL:
        size = math.prod(like[n].shape)
        rows = -(-size // (SUBLANES * LANES)) * SUBLANES
        out[n] = packed[row:row + rows].reshape(-1)[:size].reshape(like[n].shape)
        row += rows
    return out


def kernel(x, ffn1_norm, ffn1_w_gate, ffn1_w_up, ffn1_w_down, mix_norm, w_in, sink_logits, pool_w, pool_scale, w_out, ffn2_norm, ffn2_w_gate, ffn2_w_up, ffn2_w_down, final_norm, loss_target, m_ffn1_norm, m_ffn1_w_gate, m_ffn1_w_up, m_ffn1_w_down, m_mix_norm, m_w_in, m_sink_logits, m_pool_w, m_pool_scale, m_w_out, m_ffn2_norm, m_ffn2_w_gate, m_ffn2_w_up, m_ffn2_w_down, m_final_norm, v_ffn1_norm, v_ffn1_w_gate, v_ffn1_w_up, v_ffn1_w_down, v_mix_norm, v_w_in, v_sink_logits, v_pool_w, v_pool_scale, v_w_out, v_ffn2_norm, v_ffn2_w_gate, v_ffn2_w_up, v_ffn2_w_down, v_final_norm):
    names = ("ffn1_norm", "ffn1_w_gate", "ffn1_w_up", "ffn1_w_down", "mix_norm", "w_in", "sink_logits", "pool_w",
             "pool_scale", "w_out", "ffn2_norm", "ffn2_w_gate", "ffn2_w_up", "ffn2_w_down", "final_norm")
    weights = dict(zip(names, (ffn1_norm, ffn1_w_gate, ffn1_w_up, ffn1_w_down, mix_norm, w_in, sink_logits, pool_w,
                               pool_scale, w_out, ffn2_norm, ffn2_w_gate, ffn2_w_up, ffn2_w_down, final_norm)))
    mom1 = dict(zip(names, (m_ffn1_norm, m_ffn1_w_gate, m_ffn1_w_up, m_ffn1_w_down, m_mix_norm, m_w_in, m_sink_logits,
                            m_pool_w, m_pool_scale, m_w_out, m_ffn2_norm, m_ffn2_w_gate, m_ffn2_w_up, m_ffn2_w_down,
                            m_final_norm)))
    mom2 = dict(zip(names, (v_ffn1_norm, v_ffn1_w_gate, v_ffn1_w_up, v_ffn1_w_down, v_mix_norm, v_w_in, v_sink_logits,
                            v_pool_w, v_pool_scale, v_w_out, v_ffn2_norm, v_ffn2_w_gate, v_ffn2_w_up, v_ffn2_w_down,
                            v_final_norm)))
    chip = (2 * lax.axis_index("x") + lax.axis_index("y")).astype(jnp.int32).reshape(1)
    place = jnp.concatenate([lax.axis_index("c").astype(jnp.int32).reshape(1), chip])

    def rows_of(t, n):
        return jnp.swapaxes(t[n][0], 0, 1) if n in TRANSPOSED else t[n][0]

    bufs = [_pack(chip, [rows_of(weights, n) for n in GROUPS[0]], "pack_ffn1"),
            _pack(chip, [jnp.concatenate([rows_of(weights, n) for n in GROUPS[1]], axis=0)], "pack_mix"),
            _pack(chip, [rows_of(weights, n) for n in GROUPS[2]], "pack_ffn2")]

    small_w = {"ffn1_norm": ffn1_norm, "mix_norm": mix_norm, "ffn2_norm": ffn2_norm,
               "final_norm": final_norm.reshape(1, D_MODEL), "pool_scale": pool_scale, "sink_logits": sink_logits,
               "pool_w": pool_w[0]}
    loss, grad_x, group_grads, small_sum = _step(x[0], loss_target[0], bufs, small_w, place)

    out_g, out_d, out_m, out_v = {}, {}, {}, {}
    for members, g in zip(GROUPS, group_grads):
        row0 = 0
        for n in members:
            w = rows_of(weights, n)
            tile = FF_CHUNK // 4 if w.shape[0] == FF_CHUNK else math.gcd(IN_ROWS, OUT_ROWS)
            outs = _adamw(w, g, row0, rows_of(mom1, n), rows_of(mom2, n), tile, "adamw_" + n)
            row0 += w.shape[0]
            for dst, t in zip((out_g, out_d, out_m, out_v), outs):
                dst[n] = (jnp.swapaxes(t, 0, 1) if n in TRANSPOSED else t).reshape(weights[n].shape)
    small_outs = _adamw(_pack_small(weights), small_sum, 0, _pack_small(mom1), _pack_small(mom2), SMALL_ROWS, "adamw_small")
    for dst, packed in zip((out_g, out_d, out_m, out_v), small_outs):
        dst.update(_unpack_small(packed, weights))

    return (loss,grad_x.reshape(x.shape), *[out_g[n] for n in names], *[out_d[n] for n in names],
            *[out_m[n] for n in names], *[out_v[n] for n in names])
```

```python
import collections
import functools
import math

import jax
import jax.numpy as jnp
from jax import lax
from jax.experimental import pallas as pl
from jax.experimental.pallas import tpu as pltpu

F32, BF16 = jnp.float32, jnp.bfloat16
MESH = pl.DeviceIdType.MESH

D_MODEL = 1024
D_FF = 2816
N_CHIPS = 4
FF_CHUNK = D_FF // N_CHIPS
HEAD_DIM = 64
N_HEADS = 8
N_KV = 2
Q_PER_KV = N_HEADS // N_KV
KV_WIDTH = N_KV * HEAD_DIM
ATTN_WIDTH = N_HEADS * HEAD_DIM
POOL_WINDOWS = (2, 4, 8, 16)
N_POOL = len(POOL_WINDOWS)
POOL_GROUP = 128
POOL_WIDTH = N_POOL * POOL_GROUP
IN_WIDTH = ATTN_WIDTH + 2 * KV_WIDTH + POOL_WIDTH
WINDOW = 128
BLOCK = 128
BAND = 3 * BLOCK
ROPE_THETA = 500000.0
ROTARY_DIM = HEAD_DIM // 4
EPS = 1e-6
LANES = 128
Q_PAD = N_HEADS * LANES
U_PAD = Q_PAD + 2 * KV_WIDTH + POOL_WIDTH
SCALE = HEAD_DIM ** -0.5
NEG = -1e30

ADAM_LR, ADAM_B1, ADAM_B2, ADAM_EPS, ADAM_WD, ADAM_STEP = 0.001, 0.9, 0.999, 1e-08, 0.01, 10

V7X_VMEM_BYTES = 64 * 1024 * 1024
TOK_TILE = 512
SUBLANES = 8
SMALL_ROWS = 568


def _params(sem, vmem_mb):
    assert vmem_mb * 1024 * 1024 <= V7X_VMEM_BYTES
    return pltpu.CompilerParams(dimension_semantics=sem, vmem_limit_bytes=vmem_mb * 1024 * 1024)


def _dot(a, b):
    return lax.dot_general(a, b, (((1,), (0,)), ((), ())), preferred_element_type=F32)


def _dot_nt(a, b):
    return lax.dot_general(a, b, (((1,), (1,)), ((), ())), preferred_element_type=F32)


def _dot_tn(a, b):
    return lax.dot_general(a, b, (((0,), (0,)), ((), ())), preferred_element_type=F32)


def _rms_stats(h):
    r = lax.rsqrt(jnp.mean(h * h, axis=-1, keepdims=True) + EPS)
    return r, h * r


def _rms_bwd(dn, g, r, xh):
    gd = dn * g
    dh = r * (gd - xh * jnp.mean(gd * xh, axis=-1, keepdims=True))
    return dh, jnp.sum(dn * xh, axis=0, keepdims=True)


def _rope(x, c, s1, s2):
    return x * c + pltpu.roll(x, LANES - ROTARY_DIM // 2, 1) * s1 + pltpu.roll(x, ROTARY_DIM // 2, 1) * s2


def _rope_bwd(d, c, s1, s2):
    return d * c + pltpu.roll(d * s1, ROTARY_DIM // 2, 1) + pltpu.roll(d * s2, LANES - ROTARY_DIM // 2, 1)


def _sum_chunks(refs):
    terms = [ref[j].astype(F32) for ref in refs for j in range(ref.shape[0])]
    return functools.reduce(lambda a, b: a + b, terms)


def _chunk_rows(tile, k):
    return pl.BlockSpec((k, tile, D_MODEL), lambda i, *_: (0, i, 0))


def _full(shape):
    nd = len(shape)
    return pl.BlockSpec(shape, lambda *_: (0,) * nd)


def _rows(tile, cols):
    return pl.BlockSpec((tile, cols), lambda i, *_: (i, 0))


HBM_SPEC = pl.BlockSpec(memory_space=pltpu.HBM)

_Rider = collections.namedtuple("_Rider", "operands out_shapes aliases scratch start finish hooks", defaults=[()])


_NO_RIDER = _Rider([], [], {}, [], None, None)


def _call(body, name, grid, in_specs, out_specs, out_shape, scratch, vmem_mb, args, rider=None, prefetch=(),
          shares_rider_refs=False):
    rider = rider or _NO_RIDER
    n_pre, n_in, n_out, n_scr = len(prefetch), len(in_specs), len(out_specs), len(scratch)
    r_in, r_out = len(rider.operands), len(rider.out_shapes)

    def fused(*refs):
        pre, refs = refs[:n_pre], refs[n_pre:]
        ins, refs = refs[:n_in], refs[n_in:]
        r_ins, refs = refs[:r_in], refs[r_in:]
        outs, refs = refs[:n_out], refs[n_out:]
        r_outs, refs = refs[:r_out], refs[r_out:]
        scr, r_scr = refs[:n_scr], refs[n_scr:]
        ids = [pl.program_id(d) for d in range(len(grid))]
        if rider.start is not None:
            @pl.when(functools.reduce(jnp.logical_and, [i == 0 for i in ids]))
            def _():
                rider.start(r_ins, r_outs, r_scr)

        for at, hook in rider.hooks:
            @pl.when(functools.reduce(jnp.logical_and, [i == a for i, a in zip(ids, at)]))
            def _(hook=hook):
                hook(r_ins, r_outs, r_scr)

        if shares_rider_refs:
            body(*pre, *ins, *outs, *scr, rider_refs=r_outs)
        else:
            body(*pre, *ins, *outs, *scr)

        if rider.finish is not None:
            @pl.when(functools.reduce(jnp.logical_and, [i == g - 1 for i, g in zip(ids, grid)]))
            def _():
                rider.finish(r_ins, r_outs, r_scr)

    return pl.pallas_call(
        fused, name=name,
        grid_spec=pltpu.PrefetchScalarGridSpec(
            num_scalar_prefetch=n_pre, grid=grid,
            in_specs=list(in_specs) + [HBM_SPEC] * r_in, out_specs=list(out_specs) + [HBM_SPEC] * r_out,
            scratch_shapes=list(scratch) + list(rider.scratch)),
        out_shape=list(out_shape) + list(rider.out_shapes),
        input_output_aliases={n_pre + n_in + k: n_out + v for k, v in rider.aliases.items()},
        compiler_params=_params(("arbitrary",) * len(grid), vmem_mb),
    )(*prefetch, *args, *rider.operands)


def _comm_call(rider, name):
    r_in, r_out = len(rider.operands), len(rider.out_shapes)

    def body(*refs):
        r_ins, r_outs, r_scr = refs[:r_in], refs[r_in:r_in + r_out], refs[r_in + r_out:]
        rider.start(r_ins, r_outs, r_scr)
        rider.finish(r_ins, r_outs, r_scr)

    return pl.pallas_call(
        body, name=name, in_specs=[HBM_SPEC] * r_in, out_specs=[HBM_SPEC] * r_out, out_shape=list(rider.out_shapes),
        input_output_aliases=dict(rider.aliases), scratch_shapes=list(rider.scratch),
    )(*rider.operands)


def _ffn_fwd(order, h, gain, name, rider, group_at, loss_head=None):
    S = h.shape[0]
    tile = min(TOK_TILE, S)
    nt = S // tile
    last = N_CHIPS - 1
    n_extra_in, n_head_out = (2, 4) if loss_head else (0, 1)

    def body(order_ref, h_ref, g_ref, *refs, rider_refs):
        extra_in, refs = refs[:n_extra_in], refs[n_extra_in:]
        head_out, (n_ref, gate_ref, up_ref, w_scr, w_sem, acc, n_scr) = refs[:n_head_out], refs[n_head_out:]
        j, i = pl.program_id(0), pl.program_id(1)

        @pl.when(i == 0)
        def _():
            fetch = pltpu.make_async_copy(rider_refs[group_at].at[order_ref[j]], w_scr, w_sem)
            fetch.start()
            fetch.wait()

        at = pl.multiple_of(i * tile, tile)

        @pl.when(j == 0)
        def _():
            _, xh = _rms_stats(h_ref[...])
            n = (xh * g_ref[...]).astype(BF16)
            n_scr[pl.ds(at, tile), :] = n
            n_ref[...] = n
            acc[pl.ds(at, tile), :] = jnp.zeros((tile, D_MODEL), F32)

        half = tile // 2
        wg, wu, wd = (w_scr[part * FF_CHUNK:(part + 1) * FF_CHUNK, :] for part in range(3))
        ns = [n_scr[pl.ds(at + s * half, half), :] for s in range(2)]
        gates = [_dot_nt(n, wg) for n in ns]
        ups = [_dot_nt(n, wu) for n in ns]
        acts = [(g * jax.nn.sigmoid(g) * u).astype(BF16) for g, u in zip(gates, ups)]
        for s in range(2):
            gate_ref[0, s * half:(s + 1) * half, :] = gates[s].astype(BF16)
            up_ref[0, s * half:(s + 1) * half, :] = ups[s].astype(BF16)
            acc[pl.ds(at + s * half, half), :] += _dot(acts[s], wd)

        @pl.when(j == last)
        def _():
            out = h_ref[...] + 0.5 * acc[pl.ds(at, tile), :]
            if not loss_head:
                head_out[0][...] = out
                return
            (t_ref, gf_ref), (dh_ref, dhalf_ref, loss_ref, dg_ref) = extra_in, head_out

            @pl.when(i == 0)
            def _():
                loss_ref[...] = jnp.zeros_like(loss_ref)
                dg_ref[...] = jnp.zeros_like(dg_ref)

            gf = gf_ref[...]
            r, xh = _rms_stats(out)
            err = xh * gf - t_ref[...]
            loss_ref[...] += (0.5 / D_MODEL) * jnp.sum(err * err, axis=0, keepdims=True)
            dh, dg = _rms_bwd(err * (1.0 / D_MODEL), gf, r, xh)
            dg_ref[...] += dg
            dh_ref[...] = dh
            dhalf_ref[...] = (0.5 * dh).astype(BF16)

    tok = pl.BlockSpec((tile, D_MODEL), lambda j, i, order_ref: (i, 0))
    hid = pl.BlockSpec((1, tile, FF_CHUNK), lambda j, i, order_ref: (order_ref[j], i, 0))
    row = pl.BlockSpec((1, D_MODEL), lambda j, i, order_ref: (0, 0))
    in_last = pl.BlockSpec((tile, D_MODEL), lambda j, i, order_ref: (jnp.where(j == last, i, 0), 0))
    in_first = pl.BlockSpec((tile, D_MODEL), lambda j, i, order_ref: (jnp.where(j == 0, i, nt - 1), 0))
    tok_f32, tok_bf16, lanes = (jax.ShapeDtypeStruct((S, D_MODEL), F32), jax.ShapeDtypeStruct((S, D_MODEL), BF16),
                                jax.ShapeDtypeStruct((1, D_MODEL), F32))
    hidden = jax.ShapeDtypeStruct((N_CHIPS, S, FF_CHUNK), BF16)
    if loss_head:
        extra_specs, extra_args = [in_last, row], list(loss_head)
        head_specs, head_shapes = [in_last, in_last, row, row], [tok_f32, tok_bf16, lanes, lanes]
    else:
        extra_specs, extra_args, head_specs, head_shapes = [], [], [in_last], [tok_f32]
    return _call(
        body, name, (N_CHIPS, nt), [tok, row] + extra_specs, head_specs + [in_first, hid, hid],
        head_shapes + [tok_bf16, hidden, hidden],
        [pltpu.VMEM((3 * FF_CHUNK, D_MODEL), BF16), pltpu.SemaphoreType.DMA, pltpu.VMEM((S, D_MODEL), F32),
         pltpu.VMEM((S, D_MODEL), BF16)], 58, (h, gain, *extra_args), rider, (order,), shares_rider_refs=True)


def _ffn_bwd(chunks, d_out, n, gate, up, group, name, rider=None):
    S = n.shape[0]
    n_chunks = chunks.shape[0]
    tile = min(TOK_TILE, S)
    nt = S // tile
    half_rows = 3 * FF_CHUNK // 2
    cut = FF_CHUNK // 2

    def body(chunks_ref, do_ref, n_ref, gate_ref, up_ref, wg_ref, wu_ref, wd_ref, dn_ref, dw_ref, acc_g, acc_u, acc_d):
        j, i = pl.program_id(0), pl.program_id(1)

        @pl.when(i == 0)
        def _():
            acc_g[...] = jnp.zeros_like(acc_g)
            acc_u[...] = jnp.zeros_like(acc_u)
            acc_d[...] = jnp.zeros_like(acc_d)

        halves = [pl.ds(s * (tile // 2), tile // 2) for s in range(2)]
        dos = [do_ref[rows, :] for rows in halves]
        d_acts = [_dot_nt(do, wd_ref[0]) for do in dos]
        gs = [gate_ref[0, rows, :].astype(F32) for rows in halves]
        us = [up_ref[0, rows, :].astype(F32) for rows in halves]
        sigs = [jax.nn.sigmoid(g) for g in gs]
        silus = [g * sig for g, sig in zip(gs, sigs)]
        d_ups = [(d_act * silu).astype(BF16) for d_act, silu in zip(d_acts, silus)]
        d_gates = [(d_act * u * (sig * (1.0 + g * (1.0 - sig)))).astype(BF16) for d_act, u, sig, g in zip(d_acts, us, sigs, gs)]
        for rows, d_gate, d_up in zip(halves, d_gates, d_ups):
            dn_ref[0, rows, :] = (_dot(d_gate, wg_ref[0]) + _dot(d_up, wu_ref[0])).astype(BF16)
        d_gate, d_up = jnp.concatenate(d_gates, axis=0), jnp.concatenate(d_ups, axis=0)
        act = jnp.concatenate([(silu * u).astype(BF16) for silu, u in zip(silus, us)], axis=0)
        nn = n_ref[...]
        acc_g[...] += _dot_tn(d_gate, nn)
        acc_u[...] += _dot_tn(d_up, nn)
        acc_d[...] += _dot_tn(act, do_ref[...])

        @pl.when(i == nt - 1)
        def _():
            dw_ref[0, 0, :FF_CHUNK, :] = acc_g[...].astype(BF16)
            dw_ref[0, 0, FF_CHUNK:, :] = acc_u[:cut, :].astype(BF16)
            dw_ref[1, 0, :cut, :] = acc_u[cut:, :].astype(BF16)
            dw_ref[1, 0, cut:, :] = acc_d[...].astype(BF16)

    tok = pl.BlockSpec((tile, D_MODEL), lambda j, i, chunks_ref: (i, 0))
    hid = pl.BlockSpec((1, tile, FF_CHUNK), lambda j, i, chunks_ref: (chunks_ref[j], i, 0))
    return _call(
        body, name, (n_chunks, nt),
        [tok, tok, hid, hid]
        + [pl.BlockSpec((1, FF_CHUNK, D_MODEL), functools.partial(lambda j, i, chunks_ref, part: (chunks_ref[j], part, 0), part=part))
           for part in range(3)],
        [pl.BlockSpec((1, tile, D_MODEL), lambda j, i, chunks_ref: (j, i, 0)),
         pl.BlockSpec((2, 1, half_rows, D_MODEL), lambda j, i, chunks_ref: (0, j, 0, 0))],
        [jax.ShapeDtypeStruct((n_chunks, S, D_MODEL), BF16), jax.ShapeDtypeStruct((2, n_chunks, half_rows, D_MODEL), BF16)],
        [pltpu.VMEM((FF_CHUNK, D_MODEL), F32)] * 3, 56, (d_out, n, gate, up, group, group, group), rider, (chunks,))


def _mix_in(h, gain, w_in, rc, rs1, rs2, name):
    S = h.shape[0]
    tile = min(TOK_TILE, S)

    def body(h_ref, g_ref, w_ref, c_ref, s1_ref, s2_ref, n_ref, q_ref, k_ref, v_ref, pc_ref):
        _, xh = _rms_stats(h_ref[...])
        n = (xh * g_ref[...]).astype(BF16)
        n_ref[...] = n
        u = _dot_nt(n, w_ref[...])
        c, s1, s2 = c_ref[...], s1_ref[...], s2_ref[...]
        q_ref[...] = jnp.concatenate([(_rope(u[:, hd * LANES:(hd + 1) * LANES], c, s1, s2) * SCALE).astype(BF16)
                                      for hd in range(N_HEADS)], axis=1)
        k_ref[...] = _rope(u[:, Q_PAD:Q_PAD + KV_WIDTH], c, s1, s2).astype(BF16)
        v_ref[...] = u[:, Q_PAD + KV_WIDTH:Q_PAD + 2 * KV_WIDTH].astype(BF16)
        pc_ref[...] = u[:, Q_PAD + 2 * KV_WIDTH:]

    return pl.pallas_call(
        body, name=name, grid=(S // tile,),
        in_specs=[_rows(tile, D_MODEL), _full((1, D_MODEL)), _full((U_PAD, D_MODEL)),
                  _rows(tile, LANES), _rows(tile, LANES), _rows(tile, LANES)],
        out_specs=[_rows(tile, D_MODEL), _rows(tile, Q_PAD), _rows(tile, KV_WIDTH), _rows(tile, KV_WIDTH),
                   _rows(tile, POOL_WIDTH)],
        out_shape=[jax.ShapeDtypeStruct((S, D_MODEL), BF16), jax.ShapeDtypeStruct((S, Q_PAD), BF16),
                   jax.ShapeDtypeStruct((S, KV_WIDTH), BF16), jax.ShapeDtypeStruct((S, KV_WIDTH), BF16),
                   jax.ShapeDtypeStruct((S, POOL_WIDTH), F32)],
        compiler_params=_params(("parallel",), 40),
    )(h, gain, w_in, rc, rs1, rs2)


def _band_start(i, S):
    return pl.multiple_of(jnp.clip((i - 1) * BLOCK, 0, S - BAND), BLOCK)


def _window_bias(off):
    r = lax.broadcasted_iota(jnp.int32, (BLOCK, 1), 0)
    c = lax.broadcasted_iota(jnp.int32, (1, BAND), 1)
    return jnp.where(jnp.abs(off + r - c) <= WINDOW, 0.0, NEG).astype(F32)


def _softmax_parts(qh, kb, bias, sink_h):
    s = _dot_nt(qh, kb) + bias
    m = jnp.maximum(jnp.max(s, axis=-1, keepdims=True), sink_h)
    p = jnp.exp(s - m)
    es = jnp.exp(sink_h - m)
    return p, es, 1.0 / (jnp.sum(p, axis=-1, keepdims=True) + es)


def _pool_matrix(t0, start, S, w):
    r = lax.broadcasted_iota(jnp.int32, (BLOCK, 1), 0) + t0
    c = lax.broadcasted_iota(jnp.int32, (1, BAND), 1) + start
    half = w // 2

    def window(lo, hi):
        a = jnp.maximum(lo, 0)
        b = jnp.minimum(hi + 1, S)
        return jnp.where((c >= a) & (c < b), 1.0 / (b - a).astype(F32), 0.0)

    return (0.5 * (window(r - half, r + half - 1) + window(r - half + 1, r + half))).astype(BF16)


def _pool_matrices(S):
    blocks = ((0, 0), (BLOCK, 0), (S - BLOCK, S - BAND))
    return jnp.stack([jnp.stack([_pool_matrix(t0, start, S, w) for w in POOL_WINDOWS]) for t0, start in blocks])


def _pool_spec(nb):
    return pl.BlockSpec((1, N_POOL, BLOCK, BAND), lambda i, *_: (jnp.where(i == 0, 0, jnp.where(i == nb - 1, 2, 1)), 0, 0, 0))


def _mix_core_fwd(q, k, v, pc, sink, pool_m, pool_w, pool_scale, name):
    S = q.shape[0]
    nb = S // BLOCK

    def body(sink_ref, q_ref, k_ref, v_ref, pc_ref, pm_ref, pw_ref, ps_ref, a_ref, p_ref):
        i = pl.program_id(0)
        start = _band_start(i, S)
        band = pl.ds(start, BAND)
        bias = _window_bias(i * BLOCK - start)
        kb, vb = k_ref[band, :], v_ref[band, :]
        hs = range(N_HEADS)
        ss = [_dot_nt(q_ref[:, hd * LANES:(hd + 1) * LANES], kb) + bias for hd in hs]
        ms = [jnp.maximum(jnp.max(ss[hd], axis=-1, keepdims=True), sink_ref[0, hd]) for hd in hs]
        ps = [jnp.exp(ss[hd] - ms[hd]) for hd in hs]
        invs = [1.0 / (jnp.sum(ps[hd], axis=-1, keepdims=True) + jnp.exp(sink_ref[0, hd] - ms[hd])) for hd in hs]
        outs = [_dot(ps[hd].astype(BF16), vb) for hd in hs]
        a_ref[...] = jnp.concatenate([(outs[hd] * invs[hd]).astype(BF16) for hd in hs], axis=1)
        centre = pl.ds(pl.multiple_of(i * BLOCK, BLOCK), BLOCK)
        gs = range(N_POOL)
        sl = [slice(g * POOL_GROUP, (g + 1) * POOL_GROUP) for g in gs]
        means = [_dot(pm_ref[0, g], pc_ref[band, sl[g]].astype(BF16)) for g in gs]
        devs = [(means[g] - pc_ref[centre, sl[g]]).astype(BF16) for g in gs]
        p_ref[...] = (jnp.concatenate([_dot(devs[g], pw_ref[g]) for g in gs], axis=1) * ps_ref[...]).astype(BF16)

    return _call(
        body, name, (nb,),
        [pl.BlockSpec(memory_space=pltpu.SMEM), _rows(BLOCK, Q_PAD), _full((S, KV_WIDTH)), _full((S, KV_WIDTH)),
         _full((S, POOL_WIDTH)), _pool_spec(nb), _full((N_POOL, POOL_GROUP, POOL_GROUP)), _full((1, POOL_WIDTH))],
        [_rows(BLOCK, Q_PAD), _rows(BLOCK, POOL_WIDTH)],
        [jax.ShapeDtypeStruct((S, Q_PAD), BF16), jax.ShapeDtypeStruct((S, POOL_WIDTH), BF16)],
        [], 40, (sink, q, k, v, pc, pool_m, pool_w, pool_scale))


def _mix_core_bwd(q, k, v, pc, da, dp, sink, pool_m, pool_w, pool_scale, rc, rs1, rs2, name, rider=None):
    S = q.shape[0]
    nb = S // BLOCK

    def body(sink_ref, q_ref, k_ref, v_ref, pc_ref, da_ref, dp_ref, pm_ref, pw_ref, ps_ref, c_ref, s1_ref, s2_ref,
             dq_ref, dk_ref, dv_ref, dpc_ref, dsink_ref, dpw_ref, dps_ref):
        i = pl.program_id(0)

        @pl.when(i == 0)
        def _():
            dk_ref[...] = jnp.zeros_like(dk_ref)
            dv_ref[...] = jnp.zeros_like(dv_ref)
            dpc_ref[...] = jnp.zeros_like(dpc_ref)
            dsink_ref[...] = jnp.zeros_like(dsink_ref)
            dpw_ref[...] = jnp.zeros_like(dpw_ref)
            dps_ref[...] = jnp.zeros_like(dps_ref)

        start = _band_start(i, S)
        band = pl.ds(start, BAND)
        bias = _window_bias(i * BLOCK - start)
        kb, vb = k_ref[band, :], v_ref[band, :]
        c, s1, s2 = c_ref[...], s1_ref[...], s2_ref[...]
        lane = lax.broadcasted_iota(jnp.int32, (1, LANES), 1)
        hs = range(N_HEADS)
        qs = [q_ref[:, hd * LANES:(hd + 1) * LANES] for hd in hs]
        das = [da_ref[:, hd * LANES:(hd + 1) * LANES] for hd in hs]
        ss = [_dot_nt(qs[hd], kb) + bias for hd in hs]
        d_probs = [_dot_nt(das[hd], vb) for hd in hs]
        ms = [jnp.maximum(jnp.max(ss[hd], axis=-1, keepdims=True), sink_ref[0, hd]) for hd in hs]
        ps = [jnp.exp(ss[hd] - ms[hd]) for hd in hs]
        ess = [jnp.exp(sink_ref[0, hd] - ms[hd]) for hd in hs]
        invs = [1.0 / (jnp.sum(ps[hd], axis=-1, keepdims=True) + ess[hd]) for hd in hs]
        probs = [ps[hd] * invs[hd] for hd in hs]
        deltas = [jnp.sum(probs[hd] * d_probs[hd], axis=-1, keepdims=True) for hd in hs]
        d_ss = [(probs[hd] * (d_probs[hd] - deltas[hd])).astype(BF16) for hd in hs]
        dqs = [_dot(d_ss[hd], kb) for hd in hs]
        dq_ref[...] = jnp.concatenate([_rope_bwd(dqs[hd] * SCALE, c, s1, s2).astype(BF16) for hd in hs], axis=1)
        dks = [_dot_tn(d_ss[hd], qs[hd]) for hd in hs]
        dvs = [_dot_tn(probs[hd].astype(BF16), das[hd]) for hd in hs]
        dk_ref[band, :] += functools.reduce(lambda a, b: a + b, dks)
        dv_ref[band, :] += functools.reduce(lambda a, b: a + b, dvs)
        dsink_ref[...] += functools.reduce(lambda a, b: a + b, [
            jnp.where(lane == hd, -jnp.sum(ess[hd] * invs[hd] * deltas[hd], axis=0, keepdims=True), 0.0) for hd in hs])

        centre = pl.ds(pl.multiple_of(i * BLOCK, BLOCK), BLOCK)
        gs = range(N_POOL)
        sl = [slice(g * POOL_GROUP, (g + 1) * POOL_GROUP) for g in gs]
        devs = [(_dot(pm_ref[0, g], pc_ref[band, sl[g]].astype(BF16)) - pc_ref[centre, sl[g]]).astype(BF16) for g in gs]
        dys = [dp_ref[:, sl[g]].astype(F32) for g in gs]
        zs = [_dot(devs[g], pw_ref[g]) for g in gs]
        dzs = [(dys[g] * ps_ref[:, sl[g]]).astype(BF16) for g in gs]
        d_devs = [_dot_nt(dzs[g], pw_ref[g]) for g in gs]
        dps_ref[...] += jnp.concatenate([jnp.sum(dys[g] * zs[g], axis=0, keepdims=True) for g in gs], axis=1)
        for g in gs:
            dpw_ref[g] += _dot_tn(devs[g], dzs[g])
        dpc_ref[band, :] += jnp.concatenate([_dot_tn(pm_ref[0, g], d_devs[g].astype(BF16)) for g in gs], axis=1)
        dpc_ref[centre, :] -= jnp.concatenate(d_devs, axis=1)

    return _call(
        body, name, (nb,),
        [pl.BlockSpec(memory_space=pltpu.SMEM), _rows(BLOCK, Q_PAD), _full((S, KV_WIDTH)), _full((S, KV_WIDTH)),
         _full((S, POOL_WIDTH)), _rows(BLOCK, Q_PAD), _rows(BLOCK, POOL_WIDTH), _pool_spec(nb),
         _full((N_POOL, POOL_GROUP, POOL_GROUP)), _full((1, POOL_WIDTH)),
         _rows(BLOCK, LANES), _rows(BLOCK, LANES), _rows(BLOCK, LANES)],
        [_rows(BLOCK, Q_PAD), _full((S, KV_WIDTH)), _full((S, KV_WIDTH)), _full((S, POOL_WIDTH)),
         _full((1, LANES)), _full((N_POOL, POOL_GROUP, POOL_GROUP)), _full((1, POOL_WIDTH))],
        [jax.ShapeDtypeStruct((S, Q_PAD), BF16), jax.ShapeDtypeStruct((S, KV_WIDTH), F32),
         jax.ShapeDtypeStruct((S, KV_WIDTH), F32), jax.ShapeDtypeStruct((S, POOL_WIDTH), F32),
         jax.ShapeDtypeStruct((1, LANES), F32), jax.ShapeDtypeStruct((N_POOL, POOL_GROUP, POOL_GROUP), F32),
         jax.ShapeDtypeStruct((1, POOL_WIDTH), F32)],
        [], 56, (sink, q, k, v, pc, da, dp, pool_m, pool_w, pool_scale, rc, rs1, rs2), rider)


def _mix_out(h, a, p, wa, wp, name):
    S = h.shape[0]
    tile = min(TOK_TILE, S)

    def body(h_ref, a_ref, p_ref, wa_ref, wp_ref, o_ref):
        o_ref[...] = h_ref[...] + _dot(a_ref[...], wa_ref[...]) + _dot(p_ref[...], wp_ref[...])

    return pl.pallas_call(
        body, name=name, grid=(S // tile,),
        in_specs=[_rows(tile, D_MODEL), _rows(tile, Q_PAD), _rows(tile, POOL_WIDTH),
                  _full((Q_PAD, D_MODEL)), _full((POOL_WIDTH, D_MODEL))],
        out_specs=_rows(tile, D_MODEL),
        out_shape=jax.ShapeDtypeStruct((S, D_MODEL), F32),
        compiler_params=_params(("parallel",), 40),
    )(h, a, p, wa, wp)


def _mix_out_bwd(dh_out, dn, h, gain, a, p, wa, wp, name, rider=None):
    S = h.shape[0]
    tile = min(TOK_TILE, S)

    def body(do_ref, dn_ref, h_ref, g_ref, a_ref, p_ref, wa_ref, wp_ref, dh_ref, da_ref, dp_ref, dwa_ref, dwp_ref, dg_ref):
        @pl.when(pl.program_id(0) == 0)
        def _():
            dwa_ref[...] = jnp.zeros_like(dwa_ref)
            dwp_ref[...] = jnp.zeros_like(dwp_ref)
            dg_ref[...] = jnp.zeros_like(dg_ref)

        r, xh = _rms_stats(h_ref[...])
        dnorm, dg = _rms_bwd(_sum_chunks([dn_ref]), g_ref[...], r, xh)
        dh = do_ref[...] + dnorm
        dg_ref[...] += dg
        dh_ref[...] = dh
        dhb = dh.astype(BF16)
        da_ref[...] = _dot_nt(dhb, wa_ref[...]).astype(BF16)
        dp_ref[...] = _dot_nt(dhb, wp_ref[...]).astype(BF16)
        dwa_ref[...] += _dot_tn(a_ref[...], dhb)
        dwp_ref[...] += _dot_tn(p_ref[...], dhb)

    return _call(
        body, name, (S // tile,),
        [_rows(tile, D_MODEL), _chunk_rows(tile, dn.shape[0]), _rows(tile, D_MODEL), _full((1, D_MODEL)),
         _rows(tile, Q_PAD), _rows(tile, POOL_WIDTH), _full((Q_PAD, D_MODEL)), _full((POOL_WIDTH, D_MODEL))],
        [_rows(tile, D_MODEL), _rows(tile, Q_PAD), _rows(tile, POOL_WIDTH),
         _full((Q_PAD, D_MODEL)), _full((POOL_WIDTH, D_MODEL)), _full((1, D_MODEL))],
        [jax.ShapeDtypeStruct((S, D_MODEL), F32), jax.ShapeDtypeStruct((S, Q_PAD), BF16),
         jax.ShapeDtypeStruct((S, POOL_WIDTH), BF16), jax.ShapeDtypeStruct((Q_PAD, D_MODEL), F32),
         jax.ShapeDtypeStruct((POOL_WIDTH, D_MODEL), F32), jax.ShapeDtypeStruct((1, D_MODEL), F32)],
        [], 48, (dh_out, dn, h, gain, a, p, wa, wp), rider)


def _mix_in_bwd(dh_out, h, gain, n, dq, dk, dv, dpc, rc, rs1, rs2, w_in, name, rider=None):
    S = h.shape[0]
    tile = min(TOK_TILE, S)

    def body(do_ref, h_ref, g_ref, n_ref, dq_ref, dk_ref, dv_ref, dpc_ref, c_ref, s1_ref, s2_ref, w_ref,
             dh_ref, dhalf_ref, dw_ref, dg_ref):
        @pl.when(pl.program_id(0) == 0)
        def _():
            dw_ref[...] = jnp.zeros_like(dw_ref)
            dg_ref[...] = jnp.zeros_like(dg_ref)

        dk = _rope_bwd(dk_ref[...], c_ref[...], s1_ref[...], s2_ref[...]).astype(BF16)
        du = jnp.concatenate([dq_ref[...], dk, dv_ref[...].astype(BF16), dpc_ref[...].astype(BF16)], axis=1)
        dn = _dot(du, w_ref[...])
        dw_ref[...] += _dot_tn(du, n_ref[...])
        r, xh = _rms_stats(h_ref[...])
        dnorm, dg = _rms_bwd(dn, g_ref[...], r, xh)
        dh = do_ref[...] + dnorm
        dg_ref[...] += dg
        dh_ref[...] = dh
        dhalf_ref[...] = (0.5 * dh).astype(BF16)

    return _call(
        body, name, (S // tile,),
        [_rows(tile, D_MODEL), _rows(tile, D_MODEL), _full((1, D_MODEL)), _rows(tile, D_MODEL),
         _rows(tile, Q_PAD), _rows(tile, KV_WIDTH), _rows(tile, KV_WIDTH), _rows(tile, POOL_WIDTH),
         _rows(tile, LANES), _rows(tile, LANES), _rows(tile, LANES), _full((U_PAD, D_MODEL))],
        [_rows(tile, D_MODEL), _rows(tile, D_MODEL), _full((U_PAD, D_MODEL)), _full((1, D_MODEL))],
        [jax.ShapeDtypeStruct((S, D_MODEL), F32), jax.ShapeDtypeStruct((S, D_MODEL), BF16),
         jax.ShapeDtypeStruct((U_PAD, D_MODEL), F32), jax.ShapeDtypeStruct((1, D_MODEL), F32)],
        [], 56, (dh_out, h, gain, n, dq, dk, dv, dpc, rc, rs1, rs2, w_in), rider)


def _norm_bwd(dh_out, dns, h, gain, name, rider=None):
    S = h.shape[0]
    tile = min(TOK_TILE, S)
    n = len(dns)

    def body(do_ref, *refs):
        h_ref, g_ref, dh_ref, dg_ref = refs[n:]

        @pl.when(pl.program_id(0) == 0)
        def _():
            dg_ref[...] = jnp.zeros_like(dg_ref)

        r, xh = _rms_stats(h_ref[...])
        dnorm, dg = _rms_bwd(_sum_chunks(refs[:n]), g_ref[...], r, xh)
        dg_ref[...] += dg
        dh_ref[...] = do_ref[...] + dnorm

    return _call(
        body, name, (S // tile,),
        [_rows(tile, D_MODEL)] + [_chunk_rows(tile, dn.shape[0]) for dn in dns] + [_rows(tile, D_MODEL), _full((1, D_MODEL))],
        [_rows(tile, D_MODEL), _full((1, D_MODEL))],
        [jax.ShapeDtypeStruct((S, D_MODEL), F32), jax.ShapeDtypeStruct((1, D_MODEL), F32)],
        [], 40, (dh_out, *dns, h, gain), rider)


def _rope_tables(S):
    half = ROTARY_DIM // 2
    inv_freq = ROPE_THETA ** (-jnp.arange(0, ROTARY_DIM, 2, dtype=F32) / ROTARY_DIM)
    dim = jnp.arange(LANES) % HEAD_DIM
    ang = jnp.arange(S, dtype=F32)[:, None] * inv_freq[dim % half][None, :]
    lo, hi = (dim < half)[None, :], ((dim >= half) & (dim < ROTARY_DIM))[None, :]
    c = jnp.where(lo | hi, jnp.cos(ang), 1.0)
    s1 = jnp.where(lo, -jnp.sin(ang), 0.0)
    s2 = jnp.where(hi, jnp.sin(ang), 0.0)
    return c, s1, s2


def _pad_heads(w, axis):
    w = jnp.moveaxis(w, axis, 0)
    heads = w.reshape((N_HEADS, HEAD_DIM) + w.shape[1:])
    zero = jnp.zeros_like(heads)
    first = (jnp.arange(N_HEADS) < Q_PER_KV).reshape((N_HEADS, 1) + (1,) * (w.ndim - 1))
    lo = jnp.where(first, heads, zero)
    hi = jnp.where(first, zero, heads)
    padded = jnp.concatenate([lo, hi], axis=1).reshape((Q_PAD,) + w.shape[1:])
    return jnp.moveaxis(padded, 0, axis)


def _unpad_heads(w, axis):
    w = jnp.moveaxis(w, axis, 0)
    groups = w.reshape((N_HEADS, 2, HEAD_DIM) + w.shape[1:])
    first = (jnp.arange(N_HEADS) < Q_PER_KV).reshape((N_HEADS, 1) + (1,) * (w.ndim - 1))
    heads = jnp.where(first, groups[:, 0], groups[:, 1]).reshape((ATTN_WIDTH,) + w.shape[1:])
    return jnp.moveaxis(heads, 0, axis)


IN_ROWS = IN_WIDTH // N_CHIPS
OUT_ROWS = (ATTN_WIDTH + POOL_WIDTH) // N_CHIPS
MIX_ROWS = IN_ROWS + OUT_ROWS
FFN_ROWS = 3 * FF_CHUNK


def _step(x, target, bufs, small, place):
    S = x.shape[0]
    rc, rs1, rs2 = _rope_tables(S)
    mine = place[1]
    order = jnp.stack([mine, mine ^ 2, mine ^ 1, mine ^ 3])
    h1, n1, gate1, up1, ffn1, mix = _ffn_fwd(order, x, small["ffn1_norm"], "ffn1_fwd",
                                             _merge(_allgather(bufs[:1], in_passes=0), _allgather(bufs[1:2])), 0)
    in_flight, ffn2, token = _gather_start(bufs[2], X_Y_DIAGONAL[:2], "ffn2_gather_start")
    w_in_t = mix[:, :IN_ROWS].reshape(IN_WIDTH, D_MODEL)
    w_in_pad = jnp.concatenate([_pad_heads(w_in_t[:ATTN_WIDTH], 0), w_in_t[ATTN_WIDTH:]], axis=0)
    w_out = mix[:, IN_ROWS:].reshape(ATTN_WIDTH + POOL_WIDTH, D_MODEL)
    wa = _pad_heads(w_out[:ATTN_WIDTH], 0)
    wp = w_out[ATTN_WIDTH:]
    pool_w = small["pool_w"].astype(BF16)

    n2, q, k, v, pc = _mix_in(h1, small["mix_norm"] + token[0, 0], w_in_pad, rc, rs1, rs2, "mix_in")
    pool_m = _pool_matrices(S)
    a, p = _mix_core_fwd(q, k, v, pc, small["sink_logits"], pool_m, pool_w, small["pool_scale"], "mix_core_fwd")
    h2 = _mix_out(h1, a, p, wa, wp, "mix_out")
    ffn2 = _gather_wait(in_flight, ffn2, X_Y_DIAGONAL[:2], h2, "ffn2_gather_wait")
    dh3, dhalf3, loss_lanes, d_final, n3, gate2, up2, ffn2 = _ffn_fwd(
        order, h2, small["ffn2_norm"], "ffn2_fwd",
        _allgather([ffn2], peers=X_Y_DIAGONAL[2:], in_passes=0, landed_before=X_Y_DIAGONAL[:2]), 0,
        loss_head=(target, small["final_norm"]))

    dn3, d_ffn2 = _ffn_bwd(jnp.arange(N_CHIPS, dtype=jnp.int32), dhalf3, n3, gate2, up2, ffn2, "ffn2_bwd")
    dh2, da, dp, dwa, dwp, d_ffn2_norm, received = _mix_out_bwd(dh3, dn3, h2, small["ffn2_norm"], a, p, wa, wp, "mix_out_bwd",
                                                                _sibling_exchange([d_ffn2]))
    pair = _pair_sum(place, d_ffn2, received, "grad_pair_sum_ffn2")
    in_flight, pair, stack, token = _scatter_start(pair, received, "ffn2_scatter_start")
    dq, dk, dv, dpc, dsink, dpool_w, dpool_scale = _mix_core_bwd(
        q, k, v, pc, da, dp, small["sink_logits"], pool_m, pool_w, small["pool_scale"] + token[0, 0], rc, rs1, rs2, "mix_core_bwd")
    dh1, dhalf1, dw_in_pad, d_mix_norm = _mix_in_bwd(dh2, h1, small["mix_norm"], n2, dq, dk, dv, dpc, rc, rs1, rs2, w_in_pad,
                                                     "mix_in_bwd")
    pair, stack = _scatter_wait(in_flight, pair, stack, d_mix_norm, "ffn2_scatter_wait")
    reduced_ffn2 = _chip_sum(place, pair, stack, 2, "grad_chip_sum_ffn2")
    dw_in_t = jnp.concatenate([_unpad_heads(dw_in_pad[:Q_PAD], 0), dw_in_pad[Q_PAD:]], axis=0)
    dw_out = jnp.concatenate([_unpad_heads(dwa, 0), dwp], axis=0)
    d_mix = jnp.concatenate([dw_in_t.reshape(N_CHIPS, IN_ROWS, D_MODEL), dw_out.reshape(N_CHIPS, OUT_ROWS, D_MODEL)], axis=1)
    d_mix = jnp.transpose(d_mix.reshape(N_CHIPS, 2, MIX_ROWS // 2, D_MODEL), (1, 0, 2, 3)).astype(BF16)
    small_g = {"ffn1_norm": jnp.zeros_like(d_mix_norm), "mix_norm": d_mix_norm, "ffn2_norm": d_ffn2_norm,
               "final_norm": d_final, "pool_scale": dpool_scale, "sink_logits": dsink[:, :N_HEADS], "pool_w": dpool_w}
    loss_row = jnp.sum(loss_lanes.reshape(D_MODEL // LANES, LANES), axis=0, keepdims=True)
    small_early = _pack_small(small_g, loss_row)

    chunk = [(place[1:] + 1 + p) % N_CHIPS for p in range(N_CHIPS)]
    ffn1_bwd = functools.partial(_ffn_bwd, d_out=dhalf1, n=n1, gate=gate1, up=up1, group=ffn1)
    dn_a, dw_a, recv_mix, small_all, g_ffn2 = ffn1_bwd(
        chunk[0], name="ffn1_bwd_0",
        rider=_merge(_merge(_sibling_exchange([d_mix]), _small_allgather(small_early)), _sibling_share([reduced_ffn2])))
    pair_mix = _pair_sum(place, d_mix, recv_mix, "grad_pair_sum_mix")
    dn_b, dw_b, recv_a, stack_mix = ffn1_bwd(chunk[1], name="ffn1_bwd_1",
                                             rider=_merge(_sibling_exchange([dw_a]), _scatter([pair_mix])))
    pair_a = _pair_sum(place, dw_a, recv_a, "grad_pair_sum_ffn1_0")
    reduced_mix = _chip_sum(place, pair_mix, stack_mix, 2, "grad_chip_sum_mix")
    dn_c, dw_c, recv_b, stack, g_mix = ffn1_bwd(
        chunk[2], name="ffn1_bwd_2",
        rider=_merge(_merge(_sibling_exchange([dw_b]), _scatter_step(pair_a, stack, 0)), _sibling_share([reduced_mix])))
    pair_b = _pair_sum(place, dw_b, recv_b, "grad_pair_sum_ffn1_1")
    dn_d, dw_d, recv_c, stack = ffn1_bwd(chunk[3], name="ffn1_bwd_3",
                                         rider=_merge(_sibling_exchange([dw_c]), _scatter_step(pair_b, stack, 1)))
    pair_c = _pair_sum(place, dw_c, recv_c, "grad_pair_sum_ffn1_2")
    grad_x, d_ffn1_norm, recv_d, stack = _norm_bwd(dh1, [dn_a, dn_b, dn_c, dn_d], x, small["ffn1_norm"], "norm1_bwd",
                                                   _merge(_sibling_exchange([dw_d]), _scatter_step(pair_c, stack, 2)))
    pair_d = _pair_sum(place, dw_d, recv_d, "grad_pair_sum_ffn1_3")
    reduced_ffn1 = _chip_sum(place, pair_d, stack, 2, "grad_chip_sum_ffn1")
    g_ffn1, gains = _comm_call(_merge(_sibling_share([reduced_ffn1]), _small_allgather(d_ffn1_norm.reshape(-1, LANES))),
                               "grad_share_tail")
    gain_sum = _sum_leading(gains, 1, "gain_grad_sum")
    small_sum = jnp.concatenate([gain_sum, _sum_leading(small_all, 1, "small_grad_sum")[gain_sum.shape[0]:]], axis=0)
    return jnp.sum(small_sum[SMALL_ROWS - 1]), grad_x, [g.reshape(-1, D_MODEL) for g in (g_ffn1, g_mix, g_ffn2)], small_sum


GROUPS =(("ffn1_w_gate", "ffn1_w_up", "ffn1_w_down"), ("w_in", "w_out"), ("ffn2_w_gate", "ffn2_w_up", "ffn2_w_down"))
TRANSPOSED = ("ffn1_w_gate", "ffn1_w_up", "w_in", "ffn2_w_gate", "ffn2_w_up")


def _place():
    x, y, c = lax.axis_index("x"), lax.axis_index("y"), lax.axis_index("c")
    chips = [(1 - x, y), (x, 1 - y), (1 - x, 1 - y)]
    return x, y, c, chips


def _remote(src, dst, send_sem, recv_sem, to):
    return pltpu.make_async_remote_copy(src_ref=src, dst_ref=dst, send_sem=send_sem, recv_sem=recv_sem,
                                        device_id=to, device_id_type=MESH)


def _pack(chip, members, name):
    rows = members[0].shape[0]
    n = len(members)

    def body(chip_ref, *refs):
        ins, out_ref, buf, sems = refs[:n], refs[n], refs[n + 1], refs[n + 2]
        copies = [pltpu.make_async_copy(ins[k], buf.at[k], sems.at[k]) for k in range(n)]
        for cp in copies:
            cp.start()
        for k in range(n):
            copies[k].wait()
            out_ref[0, k * rows:(k + 1) * rows, :] = buf[k].astype(BF16)

    return pl.pallas_call(
        body, name=name,
        grid_spec=pltpu.PrefetchScalarGridSpec(
            num_scalar_prefetch=1, grid=(1,),
            in_specs=[HBM_SPEC] * n,
            out_specs=pl.BlockSpec((1, n * rows, D_MODEL), lambda k, chip_ref: (chip_ref[0], 0, 0)),
            scratch_shapes=[pltpu.VMEM((n, rows, D_MODEL), F32), pltpu.SemaphoreType.DMA((n,))]),
        out_shape=jax.ShapeDtypeStruct((N_CHIPS, n * rows, D_MODEL), BF16),
        compiler_params=_params(("arbitrary",), 40),
    )(chip, *members)


def _same(arrays):
    return [jax.ShapeDtypeStruct(a.shape, a.dtype) for a in arrays]


X_Y_DIAGONAL = (0, 1, 2)


def _allgather(bufs, peers=X_Y_DIAGONAL, in_passes=None, landed_before=()):
    n = len(bufs)

    def copy(kind, outs, send_sems, recv_sems, a, k):
        x, y, c, chips = _place()
        half = bufs[a].shape[1] // 2

        def rows(slot, core):
            return outs[a].at[slot, pl.ds(pl.multiple_of(core * half, 16), half)]

        me, slot = 2 * x + y, 2 * chips[k][0] + chips[k][1]
        over_ici = (send_sems.at[6 * a + k], recv_sems.at[6 * a + k])
        over_d2d = (send_sems.at[6 * a + 3 + k], recv_sems.at[6 * a + 3 + k])
        if kind == "first":
            return _remote(rows(me, c), rows(me, c), *over_ici, (*chips[k], c))
        if kind == "landed":
            return _remote(rows(me, c), rows(slot, c), *over_ici, (*chips[k], c))
        if kind == "passed":
            return _remote(rows(slot, c), rows(slot, c), *over_d2d, (x, y, 1 - c))
        return _remote(rows(me, c), rows(slot, 1 - c), *over_d2d, (x, y, 1 - c))

    def arrive(pairs):
        def hook(ins, outs, sems):
            for a, k in pairs:
                if k not in landed_before:
                    copy("landed", outs, *sems, a, k).wait_recv()
                    copy("passed", outs, *sems, a, k).start()
            for a, k in pairs:
                copy("handed", outs, *sems, a, k).wait_recv()
        return hook

    sent = [(a, k) for a in range(n) for k in peers]
    everything = sent + [(in_passes, k) for k in landed_before]
    early = [(a, k) for a, k in everything if a == in_passes]

    def start(ins, outs, sems):
        for k in landed_before:
            copy("passed", outs, *sems, in_passes, k).start()
        for a, k in sent:
            copy("first", outs, *sems, a, k).start()

    def finish(ins, outs, sems):
        arrive([pair for pair in everything if pair not in early])(ins, outs, sems)
        for a, k in sent:
            copy("first", outs, *sems, a, k).wait_send()
        for a, k in everything:
            copy("passed", outs, *sems, a, k).wait_send()

    hooks = tuple(((k + 1, 0), arrive([(a, k)])) for a, k in early)
    return _Rider(list(bufs), _same(bufs), {a: a for a in range(n)},
                  [pltpu.SemaphoreType.DMA((6 * n,)), pltpu.SemaphoreType.DMA((6 * n,))], start, finish, hooks)


SEM_SPEC = pl.BlockSpec(memory_space=pltpu.SEMAPHORE)


def _gather_copies(buf_ref, sems, peers, landing):
    x, y, c, chips = _place()
    half = buf_ref.shape[1] // 2
    me = 2 * x + y

    def rows(slot):
        return buf_ref.at[slot, pl.ds(pl.multiple_of(c * half, 16), half)]

    return [_remote(rows(me), rows(2 * chips[k][0] + chips[k][1] if landing else me), sems[2 * j], sems[2 * j + 1], (*chips[k], c))
            for j, k in enumerate(peers)]


def _gather_start(buf, peers, name):
    n = 2 * len(peers)

    def body(buf_ref, *refs):
        sems, token = refs[:n], refs[n + 1]
        for cp in _gather_copies(refs[n], sems, peers, landing=False):
            cp.start()
        token[...] = jnp.zeros_like(token)

    outs = pl.pallas_call(
        body, name=name,
        out_shape=[pltpu.SemaphoreType.DMA(())] * n + [jax.ShapeDtypeStruct(buf.shape, buf.dtype),
                                                       jax.ShapeDtypeStruct((SUBLANES, LANES), F32)],
        in_specs=[HBM_SPEC], out_specs=[SEM_SPEC] * n + [HBM_SPEC, pl.BlockSpec(memory_space=pltpu.VMEM)],
        input_output_aliases={0: n},
        compiler_params=pltpu.CompilerParams(has_side_effects=pltpu.SideEffectType.DATAFLOW_SIDE_EFFECTING),
    )(buf)
    return outs[:n], outs[n], outs[n + 1]


def _gather_wait(sems, buf, peers, after, name):
    n = len(sems)

    def body(buf_ref, *refs):
        sem_refs, out_ref = refs[:n], refs[n + 1]
        for cp in _gather_copies(out_ref, sem_refs, peers, landing=True):
            cp.wait_send()
            cp.wait_recv()

    return pl.pallas_call(
        body, name=name, out_shape=jax.ShapeDtypeStruct(buf.shape, buf.dtype),
        in_specs=[HBM_SPEC] + [SEM_SPEC] * n + [pl.BlockSpec(memory_space=pl.ANY)], out_specs=HBM_SPEC,
        input_output_aliases={0: 0},
        compiler_params=pltpu.CompilerParams(has_side_effects=pltpu.SideEffectType.DATAFLOW_SIDE_EFFECTING),
    )(buf, *sems, after)


def _scatter_copies(sums_ref, stack_ref, sems, landing):
    x, y, c, chips = _place()
    me = 2 * x + y
    slots = [2 * cx + cy for cx, cy in chips]
    return [_remote(sums_ref.at[slots[k]], stack_ref.at[slots[k] if landing else me], sems[2 * k], sems[2 * k + 1], (*chips[k], c))
            for k in X_Y_DIAGONAL]


def _scatter_start(sums, stack, name):
    n = 2 * len(X_Y_DIAGONAL)

    def body(sums_ref, stack_ref, *refs):
        for cp in _scatter_copies(refs[n], refs[n + 1], refs[:n], landing=False):
            cp.start()
        refs[n + 2][...] = jnp.zeros_like(refs[n + 2])

    outs = pl.pallas_call(
        body, name=name,
        out_shape=[pltpu.SemaphoreType.DMA(())] * n + _same([sums, stack]) + [jax.ShapeDtypeStruct((SUBLANES, LANES), F32)],
        in_specs=[HBM_SPEC, HBM_SPEC], out_specs=[SEM_SPEC] * n + [HBM_SPEC, HBM_SPEC, pl.BlockSpec(memory_space=pltpu.VMEM)],
        input_output_aliases={0: n, 1: n + 1},
        compiler_params=pltpu.CompilerParams(has_side_effects=pltpu.SideEffectType.DATAFLOW_SIDE_EFFECTING),
    )(sums, stack)
    return outs[:n], outs[n], outs[n + 1], outs[n + 2]


def _scatter_wait(sems, sums, stack, after, name):
    n = len(sems)

    def body(sums_ref, stack_ref, *refs):
        for cp in _scatter_copies(refs[n + 1], refs[n + 2], refs[:n], landing=True):
            cp.wait_send()
            cp.wait_recv()

    return pl.pallas_call(
        body, name=name, out_shape=_same([sums, stack]),
        in_specs=[HBM_SPEC, HBM_SPEC] + [SEM_SPEC] * n + [pl.BlockSpec(memory_space=pl.ANY)], out_specs=[HBM_SPEC, HBM_SPEC],
        input_output_aliases={0: 0, 1: 1},
        compiler_params=pltpu.CompilerParams(has_side_effects=pltpu.SideEffectType.DATAFLOW_SIDE_EFFECTING),
    )(sums, stack, *sems, after)


def _sibling_exchange(parts):
    n = len(parts)

    def copies(ins, outs, send_sems, recv_sems):
        x, y, c, _ = _place()
        return [_remote(ins[a].at[1 - c], outs[a], send_sems.at[a], recv_sems.at[a], (x, y, 1 - c)) for a in range(n)]

    def start(ins, outs, sems):
        for cp in copies(ins, outs, *sems):
            cp.start()

    def finish(ins, outs, sems):
        for cp in copies(ins, outs, *sems):
            cp.wait_recv()
            cp.wait_send()

    return _Rider(list(parts), [jax.ShapeDtypeStruct(p.shape[1:], p.dtype) for p in parts], {},
                  [pltpu.SemaphoreType.DMA((n,)), pltpu.SemaphoreType.DMA((n,))], start, finish)


def _small_allgather(small):
    flips = [(fx, fy, fc) for fx in range(2) for fy in range(2) for fc in range(2)][1:]

    def copies(small_ref, gather_ref, send_sems, recv_sems, local_sem, started_only=False):
        x, y, c, _ = _place()
        me = 4 * x + 2 * y + c
        peers = [((1 - x) if fx else x, (1 - y) if fy else y, (1 - c) if fc else c) for fx, fy, fc in flips]
        own = pltpu.make_async_copy(small_ref, gather_ref.at[me], local_sem)
        sent = [_remote(small_ref, gather_ref.at[me], send_sems.at[k], recv_sems.at[k], peer) for k, peer in enumerate(peers)]
        if started_only:
            return own, sent
        landed = [_remote(small_ref, gather_ref.at[4 * px + 2 * py + pc], send_sems.at[k], recv_sems.at[k], (px, py, pc))
                  for k, (px, py, pc) in enumerate(peers)]
        return own, sent, landed

    def start(ins, outs, sems):
        own, sent = copies(ins[0], outs[0], *sems, started_only=True)
        own.start()
        for cp in sent:
            cp.start()

    def finish(ins, outs, sems):
        own, sent, landed = copies(ins[0], outs[0], *sems)
        for cp in landed:
            cp.wait_recv()
        for cp in sent:
            cp.wait_send()
        own.wait()

    return _Rider([small], [jax.ShapeDtypeStruct((2 * N_CHIPS,) + small.shape, small.dtype)], {},
                  [pltpu.SemaphoreType.DMA((7,)), pltpu.SemaphoreType.DMA((7,)), pltpu.SemaphoreType.DMA], start, finish)


def _merge(a, b):
    na, nao, nas = len(a.operands), len(a.out_shapes), len(a.scratch)

    def start(ins, outs, sems):
        a.start(ins[:na], outs[:nao], sems[:nas])
        b.start(ins[na:], outs[nao:], sems[nas:])

    def finish(ins, outs, sems):
        a.finish(ins[:na], outs[:nao], sems[:nas])
        b.finish(ins[na:], outs[nao:], sems[nas:])

    def of_a(fn):
        return lambda ins, outs, sems: fn(ins[:na], outs[:nao], sems[:nas])

    def of_b(fn):
        return lambda ins, outs, sems: fn(ins[na:], outs[nao:], sems[nas:])

    aliases = {**a.aliases, **{na + k: nao + v for k, v in b.aliases.items()}}
    hooks = tuple((at, of_a(fn)) for at, fn in a.hooks) + tuple((at, of_b(fn)) for at, fn in b.hooks)
    return _Rider(a.operands + b.operands, a.out_shapes + b.out_shapes, aliases, a.scratch + b.scratch, start, finish, hooks)


def _scatter_step(pair, stack, step):
    def copies(pair_ref, stack_ref, send_sem, recv_sem, started_only=False):
        x, y, c, _ = _place()
        me = 2 * x + y
        to = (me + 1 + step) % N_CHIPS
        frm = (me + N_CHIPS - 1 - step) % N_CHIPS
        sent = _remote(pair_ref.at[0], stack_ref.at[me], send_sem, recv_sem, (to // 2, to % 2, c))
        if started_only:
            return sent
        landed = _remote(pair_ref.at[0], stack_ref.at[frm], send_sem, recv_sem, (frm // 2, frm % 2, c))
        return sent, landed

    def start(ins, outs, sems):
        copies(ins[0], outs[0], *sems, started_only=True).start()

    def finish(ins, outs, sems):
        sent, landed = copies(ins[0], outs[0], *sems)
        landed.wait_recv()
        sent.wait_send()

    return _Rider([pair, stack], _same([stack]), {1: 0}, [pltpu.SemaphoreType.DMA, pltpu.SemaphoreType.DMA], start, finish)


def _scatter(sums, peers=X_Y_DIAGONAL, stacks=None):
    n = len(sums)

    def copies(ins, outs, send_sems, recv_sems, started_only=False):
        x, y, c, chips = _place()
        me = 2 * x + y
        slots = [2 * cx + cy for cx, cy in chips]
        sent = [_remote(ins[a].at[slots[k]], outs[a].at[me], send_sems.at[3 * a + k], recv_sems.at[3 * a + k], (*chips[k], c))
                for a in range(n) for k in peers]
        if started_only:
            return sent
        landed = [_remote(ins[a].at[slots[k]], outs[a].at[slots[k]], send_sems.at[3 * a + k], recv_sems.at[3 * a + k],
                          (*chips[k], c)) for a in range(n) for k in peers]
        return sent, landed

    def start(ins, outs, sems):
        for cp in copies(ins, outs, *sems, started_only=True):
            cp.start()

    def finish(ins, outs, sems):
        sent, landed = copies(ins, outs, *sems)
        for cp in landed:
            cp.wait_recv()
        for cp in sent:
            cp.wait_send()

    sems = [pltpu.SemaphoreType.DMA((3 * n,)), pltpu.SemaphoreType.DMA((3 * n,))]
    if stacks is None:
        return _Rider(list(sums), _same(sums), {}, sems, start, finish)
    return _Rider(list(sums) + list(stacks), _same(sums), {n + a: a for a in range(n)}, sems, start, finish)


def _sibling_share(bufs):
    n = len(bufs)

    def copies(outs, send_sems, recv_sems, started_only=False):
        x, y, c, _ = _place()
        sent = [_remote(outs[a].at[c], outs[a].at[c], send_sems.at[a], recv_sems.at[a], (x, y, 1 - c)) for a in range(n)]
        if started_only:
            return sent
        landed = [_remote(outs[a].at[c], outs[a].at[1 - c], send_sems.at[a], recv_sems.at[a], (x, y, 1 - c)) for a in range(n)]
        return sent, landed

    def start(ins, outs, sems):
        for cp in copies(outs, *sems, started_only=True):
            cp.start()

    def finish(ins, outs, sems):
        sent, landed = copies(outs, *sems)
        for cp in landed:
            cp.wait_recv()
        for cp in sent:
            cp.wait_send()

    return _Rider(list(bufs), _same(bufs), {a: a for a in range(n)},
                  [pltpu.SemaphoreType.DMA((n,)), pltpu.SemaphoreType.DMA((n,))], start, finish)


def _pair_sum(core, part, received, name):
    _, k, rh, cols = part.shape

    def body(core_ref, p_ref, r_ref, o_ref):
        o_ref[...] = (p_ref[0].astype(F32) + r_ref[...].astype(F32)).astype(BF16)

    return pl.pallas_call(
        body, name=name,
        grid_spec=pltpu.PrefetchScalarGridSpec(
            num_scalar_prefetch=1, grid=(k,),
            in_specs=[pl.BlockSpec((1, 1, rh, cols), lambda j, core_ref: (core_ref[0], j, 0, 0)),
                      pl.BlockSpec((1, rh, cols), lambda j, core_ref: (j, 0, 0))],
            out_specs=pl.BlockSpec((1, rh, cols), lambda j, core_ref: (j, 0, 0))),
        out_shape=jax.ShapeDtypeStruct((k, rh, cols), BF16),
        compiler_params=_params(("parallel",), 32),
    )(core, part, received)


def _sum_leading(stack, steps, name):
    k, rows, cols = stack.shape
    tile = rows // steps

    def body(s_ref, o_ref):
        total = s_ref[0].astype(F32)
        for d in range(1, k):
            total = total + s_ref[d].astype(F32)
        o_ref[...] = total

    return pl.pallas_call(
        body, name=name, grid=(steps,),
        in_specs=[pl.BlockSpec((k, tile, cols), lambda i: (0, i, 0))],
        out_specs=pl.BlockSpec((tile, cols), lambda i: (i, 0)),
        out_shape=jax.ShapeDtypeStruct((rows, cols), F32),
        compiler_params=_params(("parallel",), 32),
    )(stack)


def _chip_sum(place, own, stack, steps, name):
    k, rows, cols = stack.shape
    tile = rows // steps

    def body(place_ref, own_ref, *refs):
        chip = place_ref[1]
        total = None
        for d in range(k):
            term = jnp.where(chip == d, own_ref[0], refs[d][0]).astype(F32)
            total = term if total is None else total + term
        refs[k][0] = total

    def other(d):
        return lambda i, place_ref: (jnp.where(place_ref[1] == d, (d + 1) % k, d), i, 0)

    return pl.pallas_call(
        body, name=name,
        grid_spec=pltpu.PrefetchScalarGridSpec(
            num_scalar_prefetch=1, grid=(steps,),
            in_specs=[pl.BlockSpec((1, tile, cols), lambda i, place_ref: (place_ref[1] % own.shape[0], i, 0))]
            + [pl.BlockSpec((1, tile, cols), other(d)) for d in range(k)],
            out_specs=pl.BlockSpec((1, tile, cols), lambda i, place_ref: (place_ref[0], i, 0))),
        out_shape=jax.ShapeDtypeStruct((2, rows, cols), F32),
        compiler_params=_params(("arbitrary",), 32),
    )(place, own, *([stack] * k))


def _adamw(w, g, row0, m, v, tile, name):
    rows, cols = w.shape
    first = row0 // tile
    assert rows % tile == 0 and row0 % tile == 0
    bc1 = 1.0 - ADAM_B1 ** ADAM_STEP
    bc2 = 1.0 - ADAM_B2 ** ADAM_STEP

    def body(w_ref, g_ref, m_ref, v_ref, go_ref, d_ref, mo_ref, vo_ref):
        g = g_ref[...]
        m_new = ADAM_B1 * m_ref[...] + (1.0 - ADAM_B1) * g
        v_new = ADAM_B2 * v_ref[...] + (1.0 - ADAM_B2) * (g * g)
        go_ref[...] = g
        d_ref[...] = -ADAM_LR * ((m_new / bc1) / (jnp.sqrt(v_new / bc2) + ADAM_EPS) + ADAM_WD * w_ref[...])
        mo_ref[...] = m_new
        vo_ref[...] = v_new

    spec = pl.BlockSpec((tile, cols), lambda i: (i, 0))
    g_spec = pl.BlockSpec((tile, cols), lambda i: (first + i, 0))
    return pl.pallas_call(
        body, name=name, grid=(rows // tile,),
        in_specs=[spec, g_spec, spec, spec], out_specs=[spec] * 4,
        out_shape=[jax.ShapeDtypeStruct((rows, cols), F32)] * 4,
        compiler_params=_params(("parallel",), 32),
    )(w, g, m, v)


SMALL = ("ffn1_norm", "mix_norm", "ffn2_norm", "final_norm", "pool_scale", "sink_logits", "pool_w")


def _pack_small(d, last_row=None):
    def part(n):
        flat = d[n].reshape(-1)
        flat = jnp.pad(flat, (0, -flat.shape[0] % (SUBLANES * LANES)))
        return flat.reshape(-1, LANES)

    last = jnp.zeros((SUBLANES, LANES), F32) if last_row is None else jnp.pad(last_row, ((SUBLANES - 1, 0), (0, 0)))
    packed = jnp.concatenate([part(n) for n in SMALL] + [last], axis=0)
    assert packed.shape[0] == SMALL_ROWS
    return packed


def _unpack_small(packed, like):
    out, row = {}, 0
    for n in SMALL:
        size = math.prod(like[n].shape)
        rows = -(-size // (SUBLANES * LANES)) * SUBLANES
        out[n] = packed[row:row + rows].reshape(-1)[:size].reshape(like[n].shape)
        row += rows
    return out


def kernel(x, ffn1_norm, ffn1_w_gate, ffn1_w_up, ffn1_w_down, mix_norm, w_in, sink_logits, pool_w, pool_scale, w_out, ffn2_norm, ffn2_w_gate, ffn2_w_up, ffn2_w_down, final_norm, loss_target, m_ffn1_norm, m_ffn1_w_gate, m_ffn1_w_up, m_ffn1_w_down, m_mix_norm, m_w_in, m_sink_logits, m_pool_w, m_pool_scale, m_w_out, m_ffn2_norm, m_ffn2_w_gate, m_ffn2_w_up, m_ffn2_w_down, m_final_norm, v_ffn1_norm, v_ffn1_w_gate, v_ffn1_w_up, v_ffn1_w_down, v_mix_norm, v_w_in, v_sink_logits, v_pool_w, v_pool_scale, v_w_out, v_ffn2_norm, v_ffn2_w_gate, v_ffn2_w_up, v_ffn2_w_down, v_final_norm):
    names = ("ffn1_norm", "ffn1_w_gate", "ffn1_w_up", "ffn1_w_down", "mix_norm", "w_in", "sink_logits", "pool_w",
             "pool_scale", "w_out", "ffn2_norm", "ffn2_w_gate", "ffn2_w_up", "ffn2_w_down", "final_norm")
    weights = dict(zip(names, (ffn1_norm, ffn1_w_gate, ffn1_w_up, ffn1_w_down, mix_norm, w_in, sink_logits, pool_w,
                               pool_scale, w_out, ffn2_norm, ffn2_w_gate, ffn2_w_up, ffn2_w_down, final_norm)))
    mom1 = dict(zip(names, (m_ffn1_norm, m_ffn1_w_gate, m_ffn1_w_up, m_ffn1_w_down, m_mix_norm, m_w_in, m_sink_logits,
                            m_pool_w, m_pool_scale, m_w_out, m_ffn2_norm, m_ffn2_w_gate, m_ffn2_w_up, m_ffn2_w_down,
                            m_final_norm)))
    mom2 = dict(zip(names, (v_ffn1_norm, v_ffn1_w_gate, v_ffn1_w_up, v_ffn1_w_down, v_mix_norm, v_w_in, v_sink_logits,
                            v_pool_w, v_pool_scale, v_w_out, v_ffn2_norm, v_ffn2_w_gate, v_ffn2_w_up, v_ffn2_w_down,
                            v_final_norm)))
    chip = (2 * lax.axis_index("x") + lax.axis_index("y")).astype(jnp.int32).reshape(1)
    place = jnp.concatenate([lax.axis_index("c").astype(jnp.int32).reshape(1), chip])

    def rows_of(t, n):
        return jnp.swapaxes(t[n][0], 0, 1) if n in TRANSPOSED else t[n][0]

    bufs = [_pack(chip, [rows_of(weights, n) for n in GROUPS[0]], "pack_ffn1"),
            _pack(chip, [jnp.concatenate([rows_of(weights, n) for n in GROUPS[1]], axis=0)], "pack_mix"),
            _pack(chip, [rows_of(weights, n) for n in GROUPS[2]], "pack_ffn2")]

    small_w = {"ffn1_norm": ffn1_norm, "mix_norm": mix_norm, "ffn2_norm": ffn2_norm,
               "final_norm": final_norm.reshape(1, D_MODEL), "pool_scale": pool_scale, "sink_logits": sink_logits,
               "pool_w": pool_w[0]}
    loss, grad_x, group_grads, small_sum = _step(x[0], loss_target[0], bufs, small_w, place)

    out_g, out_d, out_m, out_v = {}, {}, {}, {}
    for members, g in zip(GROUPS, group_grads):
        row0 = 0
        for n in members:
            w = rows_of(weights, n)
            tile = FF_CHUNK // 4 if w.shape[0] == FF_CHUNK else math.gcd(IN_ROWS, OUT_ROWS)
            outs = _adamw(w, g, row0, rows_of(mom1, n), rows_of(mom2, n), tile, "adamw_" + n)
            row0 += w.shape[0]
            for dst, t in zip((out_g, out_d, out_m, out_v), outs):
                dst[n] = (jnp.swapaxes(t, 0, 1) if n in TRANSPOSED else t).reshape(weights[n].shape)
    small_outs = _adamw(_pack_small(weights), small_sum, 0, _pack_small(mom1), _pack_small(mom2), SMALL_ROWS, "adamw_small")
    for dst, packed in zip((out_g, out_d, out_m, out_v), small_outs):
        dst.update(_unpack_small(packed, weights))

    return (loss,grad_x.reshape(x.shape), *[out_g[n] for n in names], *[out_d[n] for n in names],
            *[out_m[n] for n in names], *[out_v[n] for n in names])
```

```python
import collections
import functools
import math

import jax
import jax.numpy as jnp
from jax import lax
from jax.experimental import pallas as pl
from jax.experimental.pallas import tpu as pltpu

F32, BF16 = jnp.float32, jnp.bfloat16
MESH = pl.DeviceIdType.MESH

D_MODEL = 1024
D_FF = 2816
N_CHIPS = 4
FF_CHUNK = D_FF // N_CHIPS
HEAD_DIM = 64
N_HEADS = 8
N_KV = 2
Q_PER_KV = N_HEADS // N_KV
KV_WIDTH = N_KV * HEAD_DIM
ATTN_WIDTH = N_HEADS * HEAD_DIM
POOL_WINDOWS = (2, 4, 8, 16)
N_POOL = len(POOL_WINDOWS)
POOL_GROUP = 128
POOL_WIDTH = N_POOL * POOL_GROUP
IN_WIDTH = ATTN_WIDTH + 2 * KV_WIDTH + POOL_WIDTH
WINDOW = 128
BLOCK = 128
BAND = 3 * BLOCK
ROPE_THETA = 500000.0
ROTARY_DIM = HEAD_DIM // 4
EPS = 1e-6
LANES = 128
Q_PAD = N_HEADS * LANES
U_PAD = Q_PAD + 2 * KV_WIDTH + POOL_WIDTH
SCALE = HEAD_DIM ** -0.5
NEG = -1e30

ADAM_LR, ADAM_B1, ADAM_B2, ADAM_EPS, ADAM_WD, ADAM_STEP = 0.001, 0.9, 0.999, 1e-08, 0.01, 10

V7X_VMEM_BYTES = 64 * 1024 * 1024
TOK_TILE = 512
SUBLANES = 8
SMALL_ROWS = 568


def _params(sem, vmem_mb):
    assert vmem_mb * 1024 * 1024 <= V7X_VMEM_BYTES
    return pltpu.CompilerParams(dimension_semantics=sem, vmem_limit_bytes=vmem_mb * 1024 * 1024)


def _dot(a, b):
    return lax.dot_general(a, b, (((1,), (0,)), ((), ())), preferred_element_type=F32)


def _dot_nt(a, b):
    return lax.dot_general(a, b, (((1,), (1,)), ((), ())), preferred_element_type=F32)


def _dot_tn(a, b):
    return lax.dot_general(a, b, (((0,), (0,)), ((), ())), preferred_element_type=F32)


def _rms_stats(h):
    r = lax.rsqrt(jnp.mean(h * h, axis=-1, keepdims=True) + EPS)
    return r, h * r


def _rms_bwd(dn, g, r, xh):
    gd = dn * g
    dh = r * (gd - xh * jnp.mean(gd * xh, axis=-1, keepdims=True))
    return dh, jnp.sum(dn * xh, axis=0, keepdims=True)


def _rope(x, c, s1, s2):
    return x * c + pltpu.roll(x, LANES - ROTARY_DIM // 2, 1) * s1 + pltpu.roll(x, ROTARY_DIM // 2, 1) * s2


def _rope_bwd(d, c, s1, s2):
    return d * c + pltpu.roll(d * s1, ROTARY_DIM // 2, 1) + pltpu.roll(d * s2, LANES - ROTARY_DIM // 2, 1)


def _sum_chunks(refs):
    terms = [ref[j].astype(F32) for ref in refs for j in range(ref.shape[0])]
    return functools.reduce(lambda a, b: a + b, terms)


def _chunk_rows(tile, k):
    return pl.BlockSpec((k, tile, D_MODEL), lambda i, *_: (0, i, 0))


def _full(shape):
    nd = len(shape)
    return pl.BlockSpec(shape, lambda *_: (0,) * nd)


def _rows(tile, cols):
    return pl.BlockSpec((tile, cols), lambda i, *_: (i, 0))


HBM_SPEC = pl.BlockSpec(memory_space=pltpu.HBM)

_Rider = collections.namedtuple("_Rider", "operands out_shapes aliases scratch start finish hooks", defaults=[()])


_NO_RIDER = _Rider([], [], {}, [], None, None)


def _call(body, name, grid, in_specs, out_specs, out_shape, scratch, vmem_mb, args, rider=None, prefetch=(),
          shares_rider_refs=False):
    rider = rider or _NO_RIDER
    n_pre, n_in, n_out, n_scr = len(prefetch), len(in_specs), len(out_specs), len(scratch)
    r_in, r_out = len(rider.operands), len(rider.out_shapes)

    def fused(*refs):
        pre, refs = refs[:n_pre], refs[n_pre:]
        ins, refs = refs[:n_in], refs[n_in:]
        r_ins, refs = refs[:r_in], refs[r_in:]
        outs, refs = refs[:n_out], refs[n_out:]
        r_outs, refs = refs[:r_out], refs[r_out:]
        scr, r_scr = refs[:n_scr], refs[n_scr:]
        ids = [pl.program_id(d) for d in range(len(grid))]
        if rider.start is not None:
            @pl.when(functools.reduce(jnp.logical_and, [i == 0 for i in ids]))
            def _():
                rider.start(r_ins, r_outs, r_scr)

        for at, hook in rider.hooks:
            @pl.when(functools.reduce(jnp.logical_and, [i == a for i, a in zip(ids, at)]))
            def _(hook=hook):
                hook(r_ins, r_outs, r_scr)

        if shares_rider_refs:
            body(*pre, *ins, *outs, *scr, rider_refs=r_outs)
        else:
            body(*pre, *ins, *outs, *scr)

        if rider.finish is not None:
            @pl.when(functools.reduce(jnp.logical_and, [i == g - 1 for i, g in zip(ids, grid)]))
            def _():
                rider.finish(r_ins, r_outs, r_scr)

    return pl.pallas_call(
        fused, name=name,
        grid_spec=pltpu.PrefetchScalarGridSpec(
            num_scalar_prefetch=n_pre, grid=grid,
            in_specs=list(in_specs) + [HBM_SPEC] * r_in, out_specs=list(out_specs) + [HBM_SPEC] * r_out,
            scratch_shapes=list(scratch) + list(rider.scratch)),
        out_shape=list(out_shape) + list(rider.out_shapes),
        input_output_aliases={n_pre + n_in + k: n_out + v for k, v in rider.aliases.items()},
        compiler_params=_params(("arbitrary",) * len(grid), vmem_mb),
    )(*prefetch, *args, *rider.operands)


def _after(token):
    return _Rider([token], [], {}, [], lambda *_: None, lambda *_: None)


def _comm_call(rider, name):
    r_in, r_out = len(rider.operands), len(rider.out_shapes)

    def body(*refs):
        r_ins, r_outs, r_scr = refs[:r_in], refs[r_in:r_in + r_out], refs[r_in + r_out:]
        rider.start(r_ins, r_outs, r_scr)
        rider.finish(r_ins, r_outs, r_scr)

    return pl.pallas_call(
        body, name=name, in_specs=[HBM_SPEC] * r_in, out_specs=[HBM_SPEC] * r_out, out_shape=list(rider.out_shapes),
        input_output_aliases=dict(rider.aliases), scratch_shapes=list(rider.scratch),
    )(*rider.operands)


def _ffn_fwd(order, h, gain, name, rider, group_at, loss_head=None):
    S = h.shape[0]
    tile = min(TOK_TILE, S)
    nt = S // tile
    last = N_CHIPS - 1
    n_extra_in, n_head_out = (2, 4) if loss_head else (0, 1)

    def body(order_ref, h_ref, g_ref, *refs, rider_refs):
        extra_in, refs = refs[:n_extra_in], refs[n_extra_in:]
        head_out, (n_ref, gate_ref, up_ref, w_scr, w_sem, acc, n_scr) = refs[:n_head_out], refs[n_head_out:]
        j, i = pl.program_id(0), pl.program_id(1)

        @pl.when(i == 0)
        def _():
            fetch = pltpu.make_async_copy(rider_refs[group_at].at[order_ref[j]], w_scr, w_sem)
            fetch.start()
            fetch.wait()

        at = pl.multiple_of(i * tile, tile)

        @pl.when(j == 0)
        def _():
            _, xh = _rms_stats(h_ref[...])
            n = (xh * g_ref[...]).astype(BF16)
            n_scr[pl.ds(at, tile), :] = n
            n_ref[...] = n
            acc[pl.ds(at, tile), :] = jnp.zeros((tile, D_MODEL), F32)

        half = tile // 2
        wg, wu, wd = (w_scr[part * FF_CHUNK:(part + 1) * FF_CHUNK, :] for part in range(3))
        ns = [n_scr[pl.ds(at + s * half, half), :] for s in range(2)]
        gates = [_dot_nt(n, wg) for n in ns]
        ups = [_dot_nt(n, wu) for n in ns]
        acts = [(g * jax.nn.sigmoid(g) * u).astype(BF16) for g, u in zip(gates, ups)]
        for s in range(2):
            gate_ref[0, s * half:(s + 1) * half, :] = gates[s].astype(BF16)
            up_ref[0, s * half:(s + 1) * half, :] = ups[s].astype(BF16)
            acc[pl.ds(at + s * half, half), :] += _dot(acts[s], wd)

        @pl.when(j == last)
        def _():
            out = h_ref[...] + 0.5 * acc[pl.ds(at, tile), :]
            if not loss_head:
                head_out[0][...] = out
                return
            (t_ref, gf_ref), (dh_ref, dhalf_ref, loss_ref, dg_ref) = extra_in, head_out

            @pl.when(i == 0)
            def _():
                loss_ref[...] = jnp.zeros_like(loss_ref)
                dg_ref[...] = jnp.zeros_like(dg_ref)

            gf = gf_ref[...]
            r, xh = _rms_stats(out)
            err = xh * gf - t_ref[...]
            loss_ref[...] += (0.5 / D_MODEL) * jnp.sum(err * err, axis=0, keepdims=True)
            dh, dg = _rms_bwd(err * (1.0 / D_MODEL), gf, r, xh)
            dg_ref[...] += dg
            dh_ref[...] = dh
            dhalf_ref[...] = (0.5 * dh).astype(BF16)

    tok = pl.BlockSpec((tile, D_MODEL), lambda j, i, order_ref: (i, 0))
    hid = pl.BlockSpec((1, tile, FF_CHUNK), lambda j, i, order_ref: (order_ref[j], i, 0))
    row = pl.BlockSpec((1, D_MODEL), lambda j, i, order_ref: (0, 0))
    in_last = pl.BlockSpec((tile, D_MODEL), lambda j, i, order_ref: (jnp.where(j == last, i, 0), 0))
    in_first = pl.BlockSpec((tile, D_MODEL), lambda j, i, order_ref: (jnp.where(j == 0, i, nt - 1), 0))
    tok_f32, tok_bf16, lanes = (jax.ShapeDtypeStruct((S, D_MODEL), F32), jax.ShapeDtypeStruct((S, D_MODEL), BF16),
                                jax.ShapeDtypeStruct((1, D_MODEL), F32))
    hidden = jax.ShapeDtypeStruct((N_CHIPS, S, FF_CHUNK), BF16)
    if loss_head:
        extra_specs, extra_args = [in_last, row], list(loss_head)
        head_specs, head_shapes = [in_last, in_last, row, row], [tok_f32, tok_bf16, lanes, lanes]
    else:
        extra_specs, extra_args, head_specs, head_shapes = [], [], [in_last], [tok_f32]
    return _call(
        body, name, (N_CHIPS, nt), [tok, row] + extra_specs, head_specs + [in_first, hid, hid],
        head_shapes + [tok_bf16, hidden, hidden],
        [pltpu.VMEM((3 * FF_CHUNK, D_MODEL), BF16), pltpu.SemaphoreType.DMA, pltpu.VMEM((S, D_MODEL), F32),
         pltpu.VMEM((S, D_MODEL), BF16)], 58, (h, gain, *extra_args), rider, (order,), shares_rider_refs=True)


def _ffn_bwd(chunks, d_out, n, gate, up, group, name, rider=None):
    S = n.shape[0]
    n_chunks = chunks.shape[0]
    tile = min(TOK_TILE, S)
    nt = S // tile
    half_rows = 3 * FF_CHUNK // 2
    cut = FF_CHUNK // 2

    def body(chunks_ref, do_ref, n_ref, gate_ref, up_ref, wg_ref, wu_ref, wd_ref, dn_ref, dw_ref, acc_g, acc_u, acc_d):
        j, i = pl.program_id(0), pl.program_id(1)

        @pl.when(i == 0)
        def _():
            acc_g[...] = jnp.zeros_like(acc_g)
            acc_u[...] = jnp.zeros_like(acc_u)
            acc_d[...] = jnp.zeros_like(acc_d)

        halves = [pl.ds(s * (tile // 2), tile // 2) for s in range(2)]
        dos = [do_ref[rows, :] for rows in halves]
        d_acts = [_dot_nt(do, wd_ref[0]) for do in dos]
        gs = [gate_ref[0, rows, :].astype(F32) for rows in halves]
        us = [up_ref[0, rows, :].astype(F32) for rows in halves]
        sigs = [jax.nn.sigmoid(g) for g in gs]
        silus = [g * sig for g, sig in zip(gs, sigs)]
        d_ups = [(d_act * silu).astype(BF16) for d_act, silu in zip(d_acts, silus)]
        d_gates = [(d_act * u * (sig * (1.0 + g * (1.0 - sig)))).astype(BF16) for d_act, u, sig, g in zip(d_acts, us, sigs, gs)]
        for rows, d_gate, d_up in zip(halves, d_gates, d_ups):
            dn_ref[0, rows, :] = (_dot(d_gate, wg_ref[0]) + _dot(d_up, wu_ref[0])).astype(BF16)
        d_gate, d_up = jnp.concatenate(d_gates, axis=0), jnp.concatenate(d_ups, axis=0)
        act = jnp.concatenate([(silu * u).astype(BF16) for silu, u in zip(silus, us)], axis=0)
        nn = n_ref[...]
        acc_g[...] += _dot_tn(d_gate, nn)
        acc_u[...] += _dot_tn(d_up, nn)
        acc_d[...] += _dot_tn(act, do_ref[...])

        @pl.when(i == nt - 1)
        def _():
            dw_ref[0, 0, :FF_CHUNK, :] = acc_g[...].astype(BF16)
            dw_ref[0, 0, FF_CHUNK:, :] = acc_u[:cut, :].astype(BF16)
            dw_ref[1, 0, :cut, :] = acc_u[cut:, :].astype(BF16)
            dw_ref[1, 0, cut:, :] = acc_d[...].astype(BF16)

    tok = pl.BlockSpec((tile, D_MODEL), lambda j, i, chunks_ref: (i, 0))
    hid = pl.BlockSpec((1, tile, FF_CHUNK), lambda j, i, chunks_ref: (chunks_ref[j], i, 0))
    return _call(
        body, name, (n_chunks, nt),
        [tok, tok, hid, hid]
        + [pl.BlockSpec((1, FF_CHUNK, D_MODEL), functools.partial(lambda j, i, chunks_ref, part: (chunks_ref[j], part, 0), part=part))
           for part in range(3)],
        [pl.BlockSpec((1, tile, D_MODEL), lambda j, i, chunks_ref: (j, i, 0)),
         pl.BlockSpec((2, 1, half_rows, D_MODEL), lambda j, i, chunks_ref: (0, j, 0, 0))],
        [jax.ShapeDtypeStruct((n_chunks, S, D_MODEL), BF16), jax.ShapeDtypeStruct((2, n_chunks, half_rows, D_MODEL), BF16)],
        [pltpu.VMEM((FF_CHUNK, D_MODEL), F32)] * 3, 56, (d_out, n, gate, up, group, group, group), rider, (chunks,))


def _mix_in(h, gain, w_in, rc, rs1, rs2, name):
    S = h.shape[0]
    tile = min(TOK_TILE, S)

    def body(h_ref, g_ref, w_ref, c_ref, s1_ref, s2_ref, n_ref, q_ref, k_ref, v_ref, pc_ref):
        _, xh = _rms_stats(h_ref[...])
        n = (xh * g_ref[...]).astype(BF16)
        n_ref[...] = n
        u = _dot_nt(n, w_ref[...])
        c, s1, s2 = c_ref[...], s1_ref[...], s2_ref[...]
        q_ref[...] = jnp.concatenate([(_rope(u[:, hd * LANES:(hd + 1) * LANES], c, s1, s2) * SCALE).astype(BF16)
                                      for hd in range(N_HEADS)], axis=1)
        k_ref[...] = _rope(u[:, Q_PAD:Q_PAD + KV_WIDTH], c, s1, s2).astype(BF16)
        v_ref[...] = u[:, Q_PAD + KV_WIDTH:Q_PAD + 2 * KV_WIDTH].astype(BF16)
        pc_ref[...] = u[:, Q_PAD + 2 * KV_WIDTH:]

    return pl.pallas_call(
        body, name=name, grid=(S // tile,),
        in_specs=[_rows(tile, D_MODEL), _full((1, D_MODEL)), _full((U_PAD, D_MODEL)),
                  _rows(tile, LANES), _rows(tile, LANES), _rows(tile, LANES)],
        out_specs=[_rows(tile, D_MODEL), _rows(tile, Q_PAD), _rows(tile, KV_WIDTH), _rows(tile, KV_WIDTH),
                   _rows(tile, POOL_WIDTH)],
        out_shape=[jax.ShapeDtypeStruct((S, D_MODEL), BF16), jax.ShapeDtypeStruct((S, Q_PAD), BF16),
                   jax.ShapeDtypeStruct((S, KV_WIDTH), BF16), jax.ShapeDtypeStruct((S, KV_WIDTH), BF16),
                   jax.ShapeDtypeStruct((S, POOL_WIDTH), F32)],
        compiler_params=_params(("parallel",), 40),
    )(h, gain, w_in, rc, rs1, rs2)


def _band_start(i, S):
    return pl.multiple_of(jnp.clip((i - 1) * BLOCK, 0, S - BAND), BLOCK)


def _window_bias(off):
    r = lax.broadcasted_iota(jnp.int32, (BLOCK, 1), 0)
    c = lax.broadcasted_iota(jnp.int32, (1, BAND), 1)
    return jnp.where(jnp.abs(off + r - c) <= WINDOW, 0.0, NEG).astype(F32)


def _softmax_parts(qh, kb, bias, sink_h):
    s = _dot_nt(qh, kb) + bias
    m = jnp.maximum(jnp.max(s, axis=-1, keepdims=True), sink_h)
    p = jnp.exp(s - m)
    es = jnp.exp(sink_h - m)
    return p, es, 1.0 / (jnp.sum(p, axis=-1, keepdims=True) + es)


def _pool_matrix(t0, start, S, w):
    r = lax.broadcasted_iota(jnp.int32, (BLOCK, 1), 0) + t0
    c = lax.broadcasted_iota(jnp.int32, (1, BAND), 1) + start
    half = w // 2

    def window(lo, hi):
        a = jnp.maximum(lo, 0)
        b = jnp.minimum(hi + 1, S)
        return jnp.where((c >= a) & (c < b), 1.0 / (b - a).astype(F32), 0.0)

    return (0.5 * (window(r - half, r + half - 1) + window(r - half + 1, r + half))).astype(BF16)


def _pool_matrices(S):
    blocks = ((0, 0), (BLOCK, 0), (S - BLOCK, S - BAND))
    return jnp.stack([jnp.stack([_pool_matrix(t0, start, S, w) for w in POOL_WINDOWS]) for t0, start in blocks])


def _pool_spec(nb):
    return pl.BlockSpec((1, N_POOL, BLOCK, BAND), lambda i, *_: (jnp.where(i == 0, 0, jnp.where(i == nb - 1, 2, 1)), 0, 0, 0))


def _mix_core_fwd(q, k, v, pc, sink, pool_m, pool_w, pool_scale, name):
    S = q.shape[0]
    nb = S // BLOCK

    def body(sink_ref, q_ref, k_ref, v_ref, pc_ref, pm_ref, pw_ref, ps_ref, a_ref, p_ref):
        i = pl.program_id(0)
        start = _band_start(i, S)
        band = pl.ds(start, BAND)
        bias = _window_bias(i * BLOCK - start)
        kb, vb = k_ref[band, :], v_ref[band, :]
        hs = range(N_HEADS)
        ss = [_dot_nt(q_ref[:, hd * LANES:(hd + 1) * LANES], kb) + bias for hd in hs]
        ms = [jnp.maximum(jnp.max(ss[hd], axis=-1, keepdims=True), sink_ref[0, hd]) for hd in hs]
        ps = [jnp.exp(ss[hd] - ms[hd]) for hd in hs]
        invs = [1.0 / (jnp.sum(ps[hd], axis=-1, keepdims=True) + jnp.exp(sink_ref[0, hd] - ms[hd])) for hd in hs]
        outs = [_dot(ps[hd].astype(BF16), vb) for hd in hs]
        a_ref[...] = jnp.concatenate([(outs[hd] * invs[hd]).astype(BF16) for hd in hs], axis=1)
        centre = pl.ds(pl.multiple_of(i * BLOCK, BLOCK), BLOCK)
        gs = range(N_POOL)
        sl = [slice(g * POOL_GROUP, (g + 1) * POOL_GROUP) for g in gs]
        means = [_dot(pm_ref[0, g], pc_ref[band, sl[g]].astype(BF16)) for g in gs]
        devs = [(means[g] - pc_ref[centre, sl[g]]).astype(BF16) for g in gs]
        p_ref[...] = (jnp.concatenate([_dot(devs[g], pw_ref[g]) for g in gs], axis=1) * ps_ref[...]).astype(BF16)

    return _call(
        body, name, (nb,),
        [pl.BlockSpec(memory_space=pltpu.SMEM), _rows(BLOCK, Q_PAD), _full((S, KV_WIDTH)), _full((S, KV_WIDTH)),
         _full((S, POOL_WIDTH)), _pool_spec(nb), _full((N_POOL, POOL_GROUP, POOL_GROUP)), _full((1, POOL_WIDTH))],
        [_rows(BLOCK, Q_PAD), _rows(BLOCK, POOL_WIDTH)],
        [jax.ShapeDtypeStruct((S, Q_PAD), BF16), jax.ShapeDtypeStruct((S, POOL_WIDTH), BF16)],
        [], 40, (sink, q, k, v, pc, pool_m, pool_w, pool_scale))


def _mix_core_bwd(q, k, v, pc, da, dp, sink, pool_m, pool_w, pool_scale, rc, rs1, rs2, name, rider=None):
    S = q.shape[0]
    nb = S // BLOCK

    def body(sink_ref, q_ref, k_ref, v_ref, pc_ref, da_ref, dp_ref, pm_ref, pw_ref, ps_ref, c_ref, s1_ref, s2_ref,
             dq_ref, dk_ref, dv_ref, dpc_ref, dsink_ref, dpw_ref, dps_ref):
        i = pl.program_id(0)

        @pl.when(i == 0)
        def _():
            dk_ref[...] = jnp.zeros_like(dk_ref)
            dv_ref[...] = jnp.zeros_like(dv_ref)
            dpc_ref[...] = jnp.zeros_like(dpc_ref)
            dsink_ref[...] = jnp.zeros_like(dsink_ref)
            dpw_ref[...] = jnp.zeros_like(dpw_ref)
            dps_ref[...] = jnp.zeros_like(dps_ref)

        start = _band_start(i, S)
        band = pl.ds(start, BAND)
        bias = _window_bias(i * BLOCK - start)
        kb, vb = k_ref[band, :], v_ref[band, :]
        c, s1, s2 = c_ref[...], s1_ref[...], s2_ref[...]
        lane = lax.broadcasted_iota(jnp.int32, (1, LANES), 1)
        hs = range(N_HEADS)
        qs = [q_ref[:, hd * LANES:(hd + 1) * LANES] for hd in hs]
        das = [da_ref[:, hd * LANES:(hd + 1) * LANES] for hd in hs]
        ss = [_dot_nt(qs[hd], kb) + bias for hd in hs]
        d_probs = [_dot_nt(das[hd], vb) for hd in hs]
        ms = [jnp.maximum(jnp.max(ss[hd], axis=-1, keepdims=True), sink_ref[0, hd]) for hd in hs]
        ps = [jnp.exp(ss[hd] - ms[hd]) for hd in hs]
        ess = [jnp.exp(sink_ref[0, hd] - ms[hd]) for hd in hs]
        invs = [1.0 / (jnp.sum(ps[hd], axis=-1, keepdims=True) + ess[hd]) for hd in hs]
        probs = [ps[hd] * invs[hd] for hd in hs]
        deltas = [jnp.sum(probs[hd] * d_probs[hd], axis=-1, keepdims=True) for hd in hs]
        d_ss = [(probs[hd] * (d_probs[hd] - deltas[hd])).astype(BF16) for hd in hs]
        dqs = [_dot(d_ss[hd], kb) for hd in hs]
        dq_ref[...] = jnp.concatenate([_rope_bwd(dqs[hd] * SCALE, c, s1, s2).astype(BF16) for hd in hs], axis=1)
        dks = [_dot_tn(d_ss[hd], qs[hd]) for hd in hs]
        dvs = [_dot_tn(probs[hd].astype(BF16), das[hd]) for hd in hs]
        dk_ref[band, :] += functools.reduce(lambda a, b: a + b, dks)
        dv_ref[band, :] += functools.reduce(lambda a, b: a + b, dvs)
        dsink_ref[...] += functools.reduce(lambda a, b: a + b, [
            jnp.where(lane == hd, -jnp.sum(ess[hd] * invs[hd] * deltas[hd], axis=0, keepdims=True), 0.0) for hd in hs])

        centre = pl.ds(pl.multiple_of(i * BLOCK, BLOCK), BLOCK)
        gs = range(N_POOL)
        sl = [slice(g * POOL_GROUP, (g + 1) * POOL_GROUP) for g in gs]
        devs = [(_dot(pm_ref[0, g], pc_ref[band, sl[g]].astype(BF16)) - pc_ref[centre, sl[g]]).astype(BF16) for g in gs]
        dys = [dp_ref[:, sl[g]].astype(F32) for g in gs]
        zs = [_dot(devs[g], pw_ref[g]) for g in gs]
        dzs = [(dys[g] * ps_ref[:, sl[g]]).astype(BF16) for g in gs]
        d_devs = [_dot_nt(dzs[g], pw_ref[g]) for g in gs]
        dps_ref[...] += jnp.concatenate([jnp.sum(dys[g] * zs[g], axis=0, keepdims=True) for g in gs], axis=1)
        for g in gs:
            dpw_ref[g] += _dot_tn(devs[g], dzs[g])
        dpc_ref[band, :] += jnp.concatenate([_dot_tn(pm_ref[0, g], d_devs[g].astype(BF16)) for g in gs], axis=1)
        dpc_ref[centre, :] -= jnp.concatenate(d_devs, axis=1)

    return _call(
        body, name, (nb,),
        [pl.BlockSpec(memory_space=pltpu.SMEM), _rows(BLOCK, Q_PAD), _full((S, KV_WIDTH)), _full((S, KV_WIDTH)),
         _full((S, POOL_WIDTH)), _rows(BLOCK, Q_PAD), _rows(BLOCK, POOL_WIDTH), _pool_spec(nb),
         _full((N_POOL, POOL_GROUP, POOL_GROUP)), _full((1, POOL_WIDTH)),
         _rows(BLOCK, LANES), _rows(BLOCK, LANES), _rows(BLOCK, LANES)],
        [_rows(BLOCK, Q_PAD), _full((S, KV_WIDTH)), _full((S, KV_WIDTH)), _full((S, POOL_WIDTH)),
         _full((1, LANES)), _full((N_POOL, POOL_GROUP, POOL_GROUP)), _full((1, POOL_WIDTH))],
        [jax.ShapeDtypeStruct((S, Q_PAD), BF16), jax.ShapeDtypeStruct((S, KV_WIDTH), F32),
         jax.ShapeDtypeStruct((S, KV_WIDTH), F32), jax.ShapeDtypeStruct((S, POOL_WIDTH), F32),
         jax.ShapeDtypeStruct((1, LANES), F32), jax.ShapeDtypeStruct((N_POOL, POOL_GROUP, POOL_GROUP), F32),
         jax.ShapeDtypeStruct((1, POOL_WIDTH), F32)],
        [], 56, (sink, q, k, v, pc, da, dp, pool_m, pool_w, pool_scale, rc, rs1, rs2), rider)


def _mix_out(h, a, p, wa, wp, name):
    S = h.shape[0]
    tile = min(TOK_TILE, S)

    def body(h_ref, a_ref, p_ref, wa_ref, wp_ref, o_ref):
        o_ref[...] = h_ref[...] + _dot(a_ref[...], wa_ref[...]) + _dot(p_ref[...], wp_ref[...])

    return pl.pallas_call(
        body, name=name, grid=(S // tile,),
        in_specs=[_rows(tile, D_MODEL), _rows(tile, Q_PAD), _rows(tile, POOL_WIDTH),
                  _full((Q_PAD, D_MODEL)), _full((POOL_WIDTH, D_MODEL))],
        out_specs=_rows(tile, D_MODEL),
        out_shape=jax.ShapeDtypeStruct((S, D_MODEL), F32),
        compiler_params=_params(("parallel",), 40),
    )(h, a, p, wa, wp)


def _mix_out_bwd(dh_out, dn, h, gain, a, p, wa, wp, name, rider=None):
    S = h.shape[0]
    tile = min(TOK_TILE, S)

    def body(do_ref, dn_ref, h_ref, g_ref, a_ref, p_ref, wa_ref, wp_ref, dh_ref, da_ref, dp_ref, dwa_ref, dwp_ref, dg_ref):
        @pl.when(pl.program_id(0) == 0)
        def _():
            dwa_ref[...] = jnp.zeros_like(dwa_ref)
            dwp_ref[...] = jnp.zeros_like(dwp_ref)
            dg_ref[...] = jnp.zeros_like(dg_ref)

        r, xh = _rms_stats(h_ref[...])
        dnorm, dg = _rms_bwd(_sum_chunks([dn_ref]), g_ref[...], r, xh)
        dh = do_ref[...] + dnorm
        dg_ref[...] += dg
        dh_ref[...] = dh
        dhb = dh.astype(BF16)
        da_ref[...] = _dot_nt(dhb, wa_ref[...]).astype(BF16)
        dp_ref[...] = _dot_nt(dhb, wp_ref[...]).astype(BF16)
        dwa_ref[...] += _dot_tn(a_ref[...], dhb)
        dwp_ref[...] += _dot_tn(p_ref[...], dhb)

    return _call(
        body, name, (S // tile,),
        [_rows(tile, D_MODEL), _chunk_rows(tile, dn.shape[0]), _rows(tile, D_MODEL), _full((1, D_MODEL)),
         _rows(tile, Q_PAD), _rows(tile, POOL_WIDTH), _full((Q_PAD, D_MODEL)), _full((POOL_WIDTH, D_MODEL))],
        [_rows(tile, D_MODEL), _rows(tile, Q_PAD), _rows(tile, POOL_WIDTH),
         _full((Q_PAD, D_MODEL)), _full((POOL_WIDTH, D_MODEL)), _full((1, D_MODEL))],
        [jax.ShapeDtypeStruct((S, D_MODEL), F32), jax.ShapeDtypeStruct((S, Q_PAD), BF16),
         jax.ShapeDtypeStruct((S, POOL_WIDTH), BF16), jax.ShapeDtypeStruct((Q_PAD, D_MODEL), F32),
         jax.ShapeDtypeStruct((POOL_WIDTH, D_MODEL), F32), jax.ShapeDtypeStruct((1, D_MODEL), F32)],
        [], 48, (dh_out, dn, h, gain, a, p, wa, wp), rider)


def _mix_in_bwd(dh_out, h, gain, n, dq, dk, dv, dpc, rc, rs1, rs2, w_in, name, rider=None):
    S = h.shape[0]
    tile = min(TOK_TILE, S)

    def body(do_ref, h_ref, g_ref, n_ref, dq_ref, dk_ref, dv_ref, dpc_ref, c_ref, s1_ref, s2_ref, w_ref,
             dh_ref, dhalf_ref, dw_ref, dg_ref):
        @pl.when(pl.program_id(0) == 0)
        def _():
            dw_ref[...] = jnp.zeros_like(dw_ref)
            dg_ref[...] = jnp.zeros_like(dg_ref)

        dk = _rope_bwd(dk_ref[...], c_ref[...], s1_ref[...], s2_ref[...]).astype(BF16)
        du = jnp.concatenate([dq_ref[...], dk, dv_ref[...].astype(BF16), dpc_ref[...].astype(BF16)], axis=1)
        dn = _dot(du, w_ref[...])
        dw_ref[...] += _dot_tn(du, n_ref[...])
        r, xh = _rms_stats(h_ref[...])
        dnorm, dg = _rms_bwd(dn, g_ref[...], r, xh)
        dh = do_ref[...] + dnorm
        dg_ref[...] += dg
        dh_ref[...] = dh
        dhalf_ref[...] = (0.5 * dh).astype(BF16)

    return _call(
        body, name, (S // tile,),
        [_rows(tile, D_MODEL), _rows(tile, D_MODEL), _full((1, D_MODEL)), _rows(tile, D_MODEL),
         _rows(tile, Q_PAD), _rows(tile, KV_WIDTH), _rows(tile, KV_WIDTH), _rows(tile, POOL_WIDTH),
         _rows(tile, LANES), _rows(tile, LANES), _rows(tile, LANES), _full((U_PAD, D_MODEL))],
        [_rows(tile, D_MODEL), _rows(tile, D_MODEL), _full((U_PAD, D_MODEL)), _full((1, D_MODEL))],
        [jax.ShapeDtypeStruct((S, D_MODEL), F32), jax.ShapeDtypeStruct((S, D_MODEL), BF16),
         jax.ShapeDtypeStruct((U_PAD, D_MODEL), F32), jax.ShapeDtypeStruct((1, D_MODEL), F32)],
        [], 56, (dh_out, h, gain, n, dq, dk, dv, dpc, rc, rs1, rs2, w_in), rider)


def _norm_bwd(dh_out, dns, h, gain, name, rider=None):
    S = h.shape[0]
    tile = min(TOK_TILE, S)
    n = len(dns)

    def body(do_ref, *refs):
        h_ref, g_ref, dh_ref, dg_ref = refs[n:]

        @pl.when(pl.program_id(0) == 0)
        def _():
            dg_ref[...] = jnp.zeros_like(dg_ref)

        r, xh = _rms_stats(h_ref[...])
        dnorm, dg = _rms_bwd(_sum_chunks(refs[:n]), g_ref[...], r, xh)
        dg_ref[...] += dg
        dh_ref[...] = do_ref[...] + dnorm

    return _call(
        body, name, (S // tile,),
        [_rows(tile, D_MODEL)] + [_chunk_rows(tile, dn.shape[0]) for dn in dns] + [_rows(tile, D_MODEL), _full((1, D_MODEL))],
        [_rows(tile, D_MODEL), _full((1, D_MODEL))],
        [jax.ShapeDtypeStruct((S, D_MODEL), F32), jax.ShapeDtypeStruct((1, D_MODEL), F32)],
        [], 40, (dh_out, *dns, h, gain), rider)


def _rope_tables(S):
    half = ROTARY_DIM // 2
    inv_freq = ROPE_THETA ** (-jnp.arange(0, ROTARY_DIM, 2, dtype=F32) / ROTARY_DIM)
    dim = jnp.arange(LANES) % HEAD_DIM
    ang = jnp.arange(S, dtype=F32)[:, None] * inv_freq[dim % half][None, :]
    lo, hi = (dim < half)[None, :], ((dim >= half) & (dim < ROTARY_DIM))[None, :]
    c = jnp.where(lo | hi, jnp.cos(ang), 1.0)
    s1 = jnp.where(lo, -jnp.sin(ang), 0.0)
    s2 = jnp.where(hi, jnp.sin(ang), 0.0)
    return c, s1, s2


def _pad_heads(w, axis):
    w = jnp.moveaxis(w, axis, 0)
    heads = w.reshape((N_HEADS, HEAD_DIM) + w.shape[1:])
    zero = jnp.zeros_like(heads)
    first = (jnp.arange(N_HEADS) < Q_PER_KV).reshape((N_HEADS, 1) + (1,) * (w.ndim - 1))
    lo = jnp.where(first, heads, zero)
    hi = jnp.where(first, zero, heads)
    padded = jnp.concatenate([lo, hi], axis=1).reshape((Q_PAD,) + w.shape[1:])
    return jnp.moveaxis(padded, 0, axis)


def _unpad_heads(w, axis):
    w = jnp.moveaxis(w, axis, 0)
    groups = w.reshape((N_HEADS, 2, HEAD_DIM) + w.shape[1:])
    first = (jnp.arange(N_HEADS) < Q_PER_KV).reshape((N_HEADS, 1) + (1,) * (w.ndim - 1))
    heads = jnp.where(first, groups[:, 0], groups[:, 1]).reshape((ATTN_WIDTH,) + w.shape[1:])
    return jnp.moveaxis(heads, 0, axis)


IN_ROWS = IN_WIDTH // N_CHIPS
OUT_ROWS = (ATTN_WIDTH + POOL_WIDTH) // N_CHIPS
MIX_ROWS = IN_ROWS + OUT_ROWS
FFN_ROWS = 3 * FF_CHUNK


def _step(x, target, bufs, small, place):
    S = x.shape[0]
    rc, rs1, rs2 = _rope_tables(S)
    mine = place[1]
    order = jnp.stack([mine, mine ^ 2, mine ^ 1, mine ^ 3])
    h1, n1, gate1, up1, ffn1, mix = _ffn_fwd(order, x, small["ffn1_norm"], "ffn1_fwd",
                                             _merge(_allgather(bufs[:1], in_passes=0), _allgather(bufs[1:2])), 0)
    in_flight, (ffn2,), token = _split_start(bufs[2:], 2, _gather_copies(X_Y_DIAGONAL[:2]), "ffn2_gather_start")
    w_in_t = mix[:, :IN_ROWS].reshape(IN_WIDTH, D_MODEL)
    w_in_pad = jnp.concatenate([_pad_heads(w_in_t[:ATTN_WIDTH], 0), w_in_t[ATTN_WIDTH:]], axis=0)
    w_out = mix[:, IN_ROWS:].reshape(ATTN_WIDTH + POOL_WIDTH, D_MODEL)
    wa = _pad_heads(w_out[:ATTN_WIDTH], 0)
    wp = w_out[ATTN_WIDTH:]
    pool_w = small["pool_w"].astype(BF16)

    n2, q, k, v, pc = _mix_in(h1, small["mix_norm"] + token[0, 0], w_in_pad, rc, rs1, rs2, "mix_in")
    pool_m = _pool_matrices(S)
    a, p = _mix_core_fwd(q, k, v, pc, small["sink_logits"], pool_m, pool_w, small["pool_scale"], "mix_core_fwd")
    h2 = _mix_out(h1, a, p, wa, wp, "mix_out")
    (ffn2,) = _split_wait(in_flight, [ffn2], _gather_copies(X_Y_DIAGONAL[:2]), h2, "ffn2_gather_wait")
    dh3, dhalf3, loss_lanes, d_final, n3, gate2, up2, ffn2 = _ffn_fwd(
        order, h2, small["ffn2_norm"], "ffn2_fwd",
        _allgather([ffn2], peers=X_Y_DIAGONAL[2:], in_passes=0, landed_before=X_Y_DIAGONAL[:2]), 0,
        loss_head=(target, small["final_norm"]))

    dn3, d_ffn2 = _ffn_bwd(jnp.arange(N_CHIPS, dtype=jnp.int32), dhalf3, n3, gate2, up2, ffn2, "ffn2_bwd")
    dh2, da, dp, dwa, dwp, d_ffn2_norm, received = _mix_out_bwd(dh3, dn3, h2, small["ffn2_norm"], a, p, wa, wp, "mix_out_bwd",
                                                                _sibling_exchange([d_ffn2]))
    pair = _pair_sum(place, d_ffn2, received, "grad_pair_sum_ffn2")
    in_flight, (pair, stack), token = _split_start([pair, received], 3, _scatter_copies, "ffn2_scatter_start")
    dq, dk, dv, dpc, dsink, dpool_w, dpool_scale = _mix_core_bwd(
        q, k, v, pc, da, dp, small["sink_logits"], pool_m, pool_w, small["pool_scale"] + token[0, 0], rc, rs1, rs2, "mix_core_bwd")
    dh1, dhalf1, dw_in_pad, d_mix_norm = _mix_in_bwd(dh2, h1, small["mix_norm"], n2, dq, dk, dv, dpc, rc, rs1, rs2, w_in_pad,
                                                     "mix_in_bwd")
    pair, stack = _split_wait(in_flight, [pair, stack], _scatter_copies, d_mix_norm, "ffn2_scatter_wait")
    reduced_ffn2 = _chip_sum(place, pair, stack, 2, "grad_chip_sum_ffn2")
    dw_in_t = jnp.concatenate([_unpad_heads(dw_in_pad[:Q_PAD], 0), dw_in_pad[Q_PAD:]], axis=0)
    dw_out = jnp.concatenate([_unpad_heads(dwa, 0), dwp], axis=0)
    d_mix = jnp.concatenate([dw_in_t.reshape(N_CHIPS, IN_ROWS, D_MODEL), dw_out.reshape(N_CHIPS, OUT_ROWS, D_MODEL)], axis=1)
    d_mix = jnp.transpose(d_mix.reshape(N_CHIPS, 2, MIX_ROWS // 2, D_MODEL), (1, 0, 2, 3)).astype(BF16)
    small_g = {"ffn1_norm": jnp.zeros_like(d_mix_norm), "mix_norm": d_mix_norm, "ffn2_norm": d_ffn2_norm,
               "final_norm": d_final, "pool_scale": dpool_scale, "sink_logits": dsink[:, :N_HEADS], "pool_w": dpool_w}
    loss_row = jnp.sum(loss_lanes.reshape(D_MODEL // LANES, LANES), axis=0, keepdims=True)
    small_early = _pack_small(small_g, loss_row)

    chunk = [(place[1:] + 1 + p) % N_CHIPS for p in range(N_CHIPS)]
    ffn1_bwd = functools.partial(_ffn_bwd, d_out=dhalf1, n=n1, gate=gate1, up=up1, group=ffn1)
    dn_a, dw_a, recv_mix, small_all, g_ffn2 = ffn1_bwd(
        chunk[0], name="ffn1_bwd_0",
        rider=_merge(_merge(_sibling_exchange([d_mix]), _small_allgather(small_early)), _sibling_share([reduced_ffn2])))
    pair_mix = _pair_sum(place, d_mix, recv_mix, "grad_pair_sum_mix")
    dn_b, dw_b, recv_a, stack_mix = ffn1_bwd(chunk[1], name="ffn1_bwd_1",
                                             rider=_merge(_sibling_exchange([dw_a]), _scatter([pair_mix])))
    pair_a = _pair_sum(place, dw_a, recv_a, "grad_pair_sum_ffn1_0")
    reduced_mix = _chip_sum(place, pair_mix, stack_mix, 2, "grad_chip_sum_mix")
    steps = [(_scatter_step_copies(p), (p, 3)) for p in range(3)]
    sems_a, (pair_a, stack), token = _split_start([pair_a, stack], 1, steps[0][0], "ffn1_scatter_start_0")
    dn_c, dw_c, recv_b, g_mix = ffn1_bwd(
        chunk[2], name="ffn1_bwd_2", rider=_merge(_merge(_sibling_exchange([dw_b]), _sibling_share([reduced_mix])), _after(token)))
    pair_b = _pair_sum(place, dw_b, recv_b, "grad_pair_sum_ffn1_1")
    sems_b, (pair_b, stack), token = _split_start([pair_b, stack], 1, steps[1][0], "ffn1_scatter_start_1")
    dn_d, dw_d, recv_c = ffn1_bwd(chunk[3], name="ffn1_bwd_3", rider=_merge(_sibling_exchange([dw_c]), _after(token)))
    pair_c = _pair_sum(place, dw_c, recv_c, "grad_pair_sum_ffn1_2")
    own = (_exchange_copies, (4, 5))
    sems_c, (pair_c, stack, dw_d, recv_d), token = _split_start(
        [pair_c, stack, dw_d, recv_c], 2, _joined([(steps[2][0], (0, 1)), (own[0], (2, 3))]), "ffn1_scatter_start_2")
    grad_x, d_ffn1_norm = _norm_bwd(dh1, [dn_a, dn_b, dn_c, dn_d], x, small["ffn1_norm"] + token[0, 0], "norm1_bwd")
    pair_a, pair_b, pair_c, stack, dw_d, recv_d = _split_wait(
        sems_a + sems_b + sems_c, [pair_a, pair_b, pair_c, stack, dw_d, recv_d], _joined(steps + [own]), d_ffn1_norm,
        "ffn1_scatter_wait")
    pair_d = _pair_sum(place, dw_d, recv_d, "grad_pair_sum_ffn1_3")
    reduced_ffn1 = _chip_sum(place, pair_d, stack, 2, "grad_chip_sum_ffn1")
    g_ffn1, gains = _comm_call(_merge(_sibling_share([reduced_ffn1]), _small_allgather(d_ffn1_norm.reshape(-1, LANES))),
                               "grad_share_tail")
    gain_sum = _sum_leading(gains, 1, "gain_grad_sum")
    small_sum = jnp.concatenate([gain_sum, _sum_leading(small_all, 1, "small_grad_sum")[gain_sum.shape[0]:]], axis=0)
    return jnp.sum(small_sum[SMALL_ROWS - 1]), grad_x, [g.reshape(-1, D_MODEL) for g in (g_ffn1, g_mix, g_ffn2)], small_sum


GROUPS =(("ffn1_w_gate", "ffn1_w_up", "ffn1_w_down"), ("w_in", "w_out"), ("ffn2_w_gate", "ffn2_w_up", "ffn2_w_down"))
TRANSPOSED = ("ffn1_w_gate", "ffn1_w_up", "w_in", "ffn2_w_gate", "ffn2_w_up")


def _place():
    x, y, c = lax.axis_index("x"), lax.axis_index("y"), lax.axis_index("c")
    chips = [(1 - x, y), (x, 1 - y), (1 - x, 1 - y)]
    return x, y, c, chips


def _remote(src, dst, send_sem, recv_sem, to):
    return pltpu.make_async_remote_copy(src_ref=src, dst_ref=dst, send_sem=send_sem, recv_sem=recv_sem,
                                        device_id=to, device_id_type=MESH)


def _pack(chip, members, name):
    rows = members[0].shape[0]
    n = len(members)

    def body(chip_ref, *refs):
        ins, out_ref, buf, sems = refs[:n], refs[n], refs[n + 1], refs[n + 2]
        copies = [pltpu.make_async_copy(ins[k], buf.at[k], sems.at[k]) for k in range(n)]
        for cp in copies:
            cp.start()
        for k in range(n):
            copies[k].wait()
            out_ref[0, k * rows:(k + 1) * rows, :] = buf[k].astype(BF16)

    return pl.pallas_call(
        body, name=name,
        grid_spec=pltpu.PrefetchScalarGridSpec(
            num_scalar_prefetch=1, grid=(1,),
            in_specs=[HBM_SPEC] * n,
            out_specs=pl.BlockSpec((1, n * rows, D_MODEL), lambda k, chip_ref: (chip_ref[0], 0, 0)),
            scratch_shapes=[pltpu.VMEM((n, rows, D_MODEL), F32), pltpu.SemaphoreType.DMA((n,))]),
        out_shape=jax.ShapeDtypeStruct((N_CHIPS, n * rows, D_MODEL), BF16),
        compiler_params=_params(("arbitrary",), 40),
    )(chip, *members)


def _same(arrays):
    return [jax.ShapeDtypeStruct(a.shape, a.dtype) for a in arrays]


X_Y_DIAGONAL = (0, 1, 2)


def _allgather(bufs, peers=X_Y_DIAGONAL, in_passes=None, landed_before=()):
    n = len(bufs)

    def copy(kind, outs, send_sems, recv_sems, a, k):
        x, y, c, chips = _place()
        half = bufs[a].shape[1] // 2

        def rows(slot, core):
            return outs[a].at[slot, pl.ds(pl.multiple_of(core * half, 16), half)]

        me, slot = 2 * x + y, 2 * chips[k][0] + chips[k][1]
        over_ici = (send_sems.at[6 * a + k], recv_sems.at[6 * a + k])
        over_d2d = (send_sems.at[6 * a + 3 + k], recv_sems.at[6 * a + 3 + k])
        if kind == "first":
            return _remote(rows(me, c), rows(me, c), *over_ici, (*chips[k], c))
        if kind == "landed":
            return _remote(rows(me, c), rows(slot, c), *over_ici, (*chips[k], c))
        if kind == "passed":
            return _remote(rows(slot, c), rows(slot, c), *over_d2d, (x, y, 1 - c))
        return _remote(rows(me, c), rows(slot, 1 - c), *over_d2d, (x, y, 1 - c))

    def arrive(pairs):
        def hook(ins, outs, sems):
            for a, k in pairs:
                if k not in landed_before:
                    copy("landed", outs, *sems, a, k).wait_recv()
                    copy("passed", outs, *sems, a, k).start()
            for a, k in pairs:
                copy("handed", outs, *sems, a, k).wait_recv()
        return hook

    sent = [(a, k) for a in range(n) for k in peers]
    everything = sent + [(in_passes, k) for k in landed_before]
    early = [(a, k) for a, k in everything if a == in_passes]

    def start(ins, outs, sems):
        for k in landed_before:
            copy("passed", outs, *sems, in_passes, k).start()
        for a, k in sent:
            copy("first", outs, *sems, a, k).start()

    def finish(ins, outs, sems):
        arrive([pair for pair in everything if pair not in early])(ins, outs, sems)
        for a, k in sent:
            copy("first", outs, *sems, a, k).wait_send()
        for a, k in everything:
            copy("passed", outs, *sems, a, k).wait_send()

    hooks = tuple(((k + 1, 0), arrive([(a, k)])) for a, k in early)
    return _Rider(list(bufs), _same(bufs), {a: a for a in range(n)},
                  [pltpu.SemaphoreType.DMA((6 * n,)), pltpu.SemaphoreType.DMA((6 * n,))], start, finish, hooks)


SEM_SPEC = pl.BlockSpec(memory_space=pltpu.SEMAPHORE)


def _split_start(arrays, n_copies, make_copies, name):
    m, n = len(arrays), 2 * n_copies

    def body(*refs):
        sems, thru, token = refs[m:m + n], refs[m + n:2 * m + n], refs[2 * m + n]
        for cp in make_copies(thru, sems, False):
            cp.start()
        token[...] = jnp.zeros_like(token)

    outs = pl.pallas_call(
        body, name=name,
        out_shape=[pltpu.SemaphoreType.DMA(())] * n + _same(arrays) + [jax.ShapeDtypeStruct((SUBLANES, LANES), F32)],
        in_specs=[HBM_SPEC] * m, out_specs=[SEM_SPEC] * n + [HBM_SPEC] * m + [pl.BlockSpec(memory_space=pltpu.VMEM)],
        input_output_aliases={a: n + a for a in range(m)},
        compiler_params=pltpu.CompilerParams(has_side_effects=pltpu.SideEffectType.DATAFLOW_SIDE_EFFECTING),
    )(*arrays)
    return list(outs[:n]), list(outs[n:n + m]), outs[n + m]


def _split_wait(sems, arrays, make_copies, after, name):
    m, n = len(arrays), len(sems)

    def body(*refs):
        for cp in make_copies(refs[m + n + 1:], refs[m:m + n], True):
            cp.wait_send()
            cp.wait_recv()

    return pl.pallas_call(
        body, name=name, out_shape=_same(arrays),
        in_specs=[HBM_SPEC] * m + [SEM_SPEC] * n + [pl.BlockSpec(memory_space=pl.ANY)], out_specs=[HBM_SPEC] * m,
        input_output_aliases={a: a for a in range(m)},
        compiler_params=pltpu.CompilerParams(has_side_effects=pltpu.SideEffectType.DATAFLOW_SIDE_EFFECTING),
    )(*arrays, *sems, after)


def _gather_copies(peers):
    def make(refs, sems, landing):
        x, y, c, chips = _place()
        half = refs[0].shape[1] // 2

        def rows(slot):
            return refs[0].at[slot, pl.ds(pl.multiple_of(c * half, 16), half)]

        me = 2 * x + y
        return [_remote(rows(me), rows(2 * chips[k][0] + chips[k][1] if landing else me), sems[2 * j], sems[2 * j + 1],
                        (*chips[k], c)) for j, k in enumerate(peers)]
    return make


def _scatter_copies(refs, sems, landing):
    x, y, c, chips = _place()
    me = 2 * x + y
    slots = [2 * cx + cy for cx, cy in chips]
    return [_remote(refs[0].at[slots[k]], refs[1].at[slots[k] if landing else me], sems[2 * k], sems[2 * k + 1], (*chips[k], c))
            for k in X_Y_DIAGONAL]


def _scatter_step_copies(step):
    def make(refs, sems, landing):
        x, y, c, _ = _place()
        me = 2 * x + y
        to = (me + 1 + step) % N_CHIPS
        frm = (me + N_CHIPS - 1 - step) % N_CHIPS
        peer = frm if landing else to
        return [_remote(refs[0].at[0], refs[1].at[frm if landing else me], sems[0], sems[1], (peer // 2, peer % 2, c))]
    return make


def _exchange_copies(refs, sems, landing):
    x, y, c, _ = _place()
    return [_remote(refs[0].at[1 - c], refs[1], sems[0], sems[1], (x, y, 1 - c))]


def _joined(makers):
    def make(refs, sems, landing):
        copies, at = [], 0
        for maker, places in makers:
            mine = maker([refs[p] for p in places], sems[at:], landing)
            copies += mine
            at += 2 * len(mine)
        return copies
    return make


def _sibling_exchange(parts):
    n = len(parts)

    def copies(ins, outs, send_sems, recv_sems):
        x, y, c, _ = _place()
        return [_remote(ins[a].at[1 - c], outs[a], send_sems.at[a], recv_sems.at[a], (x, y, 1 - c)) for a in range(n)]

    def start(ins, outs, sems):
        for cp in copies(ins, outs, *sems):
            cp.start()

    def finish(ins, outs, sems):
        for cp in copies(ins, outs, *sems):
            cp.wait_recv()
            cp.wait_send()

    return _Rider(list(parts), [jax.ShapeDtypeStruct(p.shape[1:], p.dtype) for p in parts], {},
                  [pltpu.SemaphoreType.DMA((n,)), pltpu.SemaphoreType.DMA((n,))], start, finish)


def _small_allgather(small):
    flips = [(fx, fy, fc) for fx in range(2) for fy in range(2) for fc in range(2)][1:]

    def copies(small_ref, gather_ref, send_sems, recv_sems, local_sem, started_only=False):
        x, y, c, _ = _place()
        me = 4 * x + 2 * y + c
        peers = [((1 - x) if fx else x, (1 - y) if fy else y, (1 - c) if fc else c) for fx, fy, fc in flips]
        own = pltpu.make_async_copy(small_ref, gather_ref.at[me], local_sem)
        sent = [_remote(small_ref, gather_ref.at[me], send_sems.at[k], recv_sems.at[k], peer) for k, peer in enumerate(peers)]
        if started_only:
            return own, sent
        landed = [_remote(small_ref, gather_ref.at[4 * px + 2 * py + pc], send_sems.at[k], recv_sems.at[k], (px, py, pc))
                  for k, (px, py, pc) in enumerate(peers)]
        return own, sent, landed

    def start(ins, outs, sems):
        own, sent = copies(ins[0], outs[0], *sems, started_only=True)
        own.start()
        for cp in sent:
            cp.start()

    def finish(ins, outs, sems):
        own, sent, landed = copies(ins[0], outs[0], *sems)
        for cp in landed:
            cp.wait_recv()
        for cp in sent:
            cp.wait_send()
        own.wait()

    return _Rider([small], [jax.ShapeDtypeStruct((2 * N_CHIPS,) + small.shape, small.dtype)], {},
                  [pltpu.SemaphoreType.DMA((7,)), pltpu.SemaphoreType.DMA((7,)), pltpu.SemaphoreType.DMA], start, finish)


def _merge(a, b):
    na, nao, nas = len(a.operands), len(a.out_shapes), len(a.scratch)

    def start(ins, outs, sems):
        a.start(ins[:na], outs[:nao], sems[:nas])
        b.start(ins[na:], outs[nao:], sems[nas:])

    def finish(ins, outs, sems):
        a.finish(ins[:na], outs[:nao], sems[:nas])
        b.finish(ins[na:], outs[nao:], sems[nas:])

    def of_a(fn):
        return lambda ins, outs, sems: fn(ins[:na], outs[:nao], sems[:nas])

    def of_b(fn):
        return lambda ins, outs, sems: fn(ins[na:], outs[nao:], sems[nas:])

    aliases = {**a.aliases, **{na + k: nao + v for k, v in b.aliases.items()}}
    hooks = tuple((at, of_a(fn)) for at, fn in a.hooks) + tuple((at, of_b(fn)) for at, fn in b.hooks)
    return _Rider(a.operands + b.operands, a.out_shapes + b.out_shapes, aliases, a.scratch + b.scratch, start, finish, hooks)


def _scatter_step(pair, stack, step):
    def copies(pair_ref, stack_ref, send_sem, recv_sem, started_only=False):
        x, y, c, _ = _place()
        me = 2 * x + y
        to = (me + 1 + step) % N_CHIPS
        frm = (me + N_CHIPS - 1 - step) % N_CHIPS
        sent = _remote(pair_ref.at[0], stack_ref.at[me], send_sem, recv_sem, (to // 2, to % 2, c))
        if started_only:
            return sent
        landed = _remote(pair_ref.at[0], stack_ref.at[frm], send_sem, recv_sem, (frm // 2, frm % 2, c))
        return sent, landed

    def start(ins, outs, sems):
        copies(ins[0], outs[0], *sems, started_only=True).start()

    def finish(ins, outs, sems):
        sent, landed = copies(ins[0], outs[0], *sems)
        landed.wait_recv()
        sent.wait_send()

    return _Rider([pair, stack], _same([stack]), {1: 0}, [pltpu.SemaphoreType.DMA, pltpu.SemaphoreType.DMA], start, finish)


def _scatter(sums, peers=X_Y_DIAGONAL, stacks=None):
    n = len(sums)

    def copies(ins, outs, send_sems, recv_sems, started_only=False):
        x, y, c, chips = _place()
        me = 2 * x + y
        slots = [2 * cx + cy for cx, cy in chips]
        sent = [_remote(ins[a].at[slots[k]], outs[a].at[me], send_sems.at[3 * a + k], recv_sems.at[3 * a + k], (*chips[k], c))
                for a in range(n) for k in peers]
        if started_only:
            return sent
        landed = [_remote(ins[a].at[slots[k]], outs[a].at[slots[k]], send_sems.at[3 * a + k], recv_sems.at[3 * a + k],
                          (*chips[k], c)) for a in range(n) for k in peers]
        return sent, landed

    def start(ins, outs, sems):
        for cp in copies(ins, outs, *sems, started_only=True):
            cp.start()

    def finish(ins, outs, sems):
        sent, landed = copies(ins, outs, *sems)
        for cp in landed:
            cp.wait_recv()
        for cp in sent:
            cp.wait_send()

    sems = [pltpu.SemaphoreType.DMA((3 * n,)), pltpu.SemaphoreType.DMA((3 * n,))]
    if stacks is None:
        return _Rider(list(sums), _same(sums), {}, sems, start, finish)
    return _Rider(list(sums) + list(stacks), _same(sums), {n + a: a for a in range(n)}, sems, start, finish)


def _sibling_share(bufs):
    n = len(bufs)

    def copies(outs, send_sems, recv_sems, started_only=False):
        x, y, c, _ = _place()
        sent = [_remote(outs[a].at[c], outs[a].at[c], send_sems.at[a], recv_sems.at[a], (x, y, 1 - c)) for a in range(n)]
        if started_only:
            return sent
        landed = [_remote(outs[a].at[c], outs[a].at[1 - c], send_sems.at[a], recv_sems.at[a], (x, y, 1 - c)) for a in range(n)]
        return sent, landed

    def start(ins, outs, sems):
        for cp in copies(outs, *sems, started_only=True):
            cp.start()

    def finish(ins, outs, sems):
        sent, landed = copies(outs, *sems)
        for cp in landed:
            cp.wait_recv()
        for cp in sent:
            cp.wait_send()

    return _Rider(list(bufs), _same(bufs), {a: a for a in range(n)},
                  [pltpu.SemaphoreType.DMA((n,)), pltpu.SemaphoreType.DMA((n,))], start, finish)


def _pair_sum(core, part, received, name):
    _, k, rh, cols = part.shape

    def body(core_ref, p_ref, r_ref, o_ref):
        o_ref[...] = (p_ref[0].astype(F32) + r_ref[...].astype(F32)).astype(BF16)

    return pl.pallas_call(
        body, name=name,
        grid_spec=pltpu.PrefetchScalarGridSpec(
            num_scalar_prefetch=1, grid=(k,),
            in_specs=[pl.BlockSpec((1, 1, rh, cols), lambda j, core_ref: (core_ref[0], j, 0, 0)),
                      pl.BlockSpec((1, rh, cols), lambda j, core_ref: (j, 0, 0))],
            out_specs=pl.BlockSpec((1, rh, cols), lambda j, core_ref: (j, 0, 0))),
        out_shape=jax.ShapeDtypeStruct((k, rh, cols), BF16),
        compiler_params=_params(("parallel",), 32),
    )(core, part, received)


def _sum_leading(stack, steps, name):
    k, rows, cols = stack.shape
    tile = rows // steps

    def body(s_ref, o_ref):
        total = s_ref[0].astype(F32)
        for d in range(1, k):
            total = total + s_ref[d].astype(F32)
        o_ref[...] = total

    return pl.pallas_call(
        body, name=name, grid=(steps,),
        in_specs=[pl.BlockSpec((k, tile, cols), lambda i: (0, i, 0))],
        out_specs=pl.BlockSpec((tile, cols), lambda i: (i, 0)),
        out_shape=jax.ShapeDtypeStruct((rows, cols), F32),
        compiler_params=_params(("parallel",), 32),
    )(stack)


def _chip_sum(place, own, stack, steps, name):
    k, rows, cols = stack.shape
    tile = rows // steps

    def body(place_ref, own_ref, *refs):
        chip = place_ref[1]
        total = None
        for d in range(k):
            term = jnp.where(chip == d, own_ref[0], refs[d][0]).astype(F32)
            total = term if total is None else total + term
        refs[k][0] = total

    def other(d):
        return lambda i, place_ref: (jnp.where(place_ref[1] == d, (d + 1) % k, d), i, 0)

    return pl.pallas_call(
        body, name=name,
        grid_spec=pltpu.PrefetchScalarGridSpec(
            num_scalar_prefetch=1, grid=(steps,),
            in_specs=[pl.BlockSpec((1, tile, cols), lambda i, place_ref: (place_ref[1] % own.shape[0], i, 0))]
            + [pl.BlockSpec((1, tile, cols), other(d)) for d in range(k)],
            out_specs=pl.BlockSpec((1, tile, cols), lambda i, place_ref: (place_ref[0], i, 0))),
        out_shape=jax.ShapeDtypeStruct((2, rows, cols), F32),
        compiler_params=_params(("arbitrary",), 32),
    )(place, own, *([stack] * k))


def _adamw(w, g, row0, m, v, tile, name):
    rows, cols = w.shape
    first = row0 // tile
    assert rows % tile == 0 and row0 % tile == 0
    bc1 = 1.0 - ADAM_B1 ** ADAM_STEP
    bc2 = 1.0 - ADAM_B2 ** ADAM_STEP

    def body(w_ref, g_ref, m_ref, v_ref, go_ref, d_ref, mo_ref, vo_ref):
        g = g_ref[...]
        m_new = ADAM_B1 * m_ref[...] + (1.0 - ADAM_B1) * g
        v_new = ADAM_B2 * v_ref[...] + (1.0 - ADAM_B2) * (g * g)
        go_ref[...] = g
        d_ref[...] = -ADAM_LR * ((m_new / bc1) / (jnp.sqrt(v_new / bc2) + ADAM_EPS) + ADAM_WD * w_ref[...])
        mo_ref[...] = m_new
        vo_ref[...] = v_new

    spec = pl.BlockSpec((tile, cols), lambda i: (i, 0))
    g_spec = pl.BlockSpec((tile, cols), lambda i: (first + i, 0))
    return pl.pallas_call(
        body, name=name, grid=(rows // tile,),
        in_specs=[spec, g_spec, spec, spec], out_specs=[spec] * 4,
        out_shape=[jax.ShapeDtypeStruct((rows, cols), F32)] * 4,
        compiler_params=_params(("parallel",), 32),
    )(w, g, m, v)


SMALL = ("ffn1_norm", "mix_norm", "ffn2_norm", "final_norm", "pool_scale", "sink_logits", "pool_w")


def _pack_small(d, last_row=None):
    def part(n):
        flat = d[n].reshape(-1)
        flat = jnp.pad(flat, (0, -flat.shape[0] % (SUBLANES * LANES)))
        return flat.reshape(-1, LANES)

    last = jnp.zeros((SUBLANES, LANES), F32) if last_row is None else jnp.pad(last_row, ((SUBLANES - 1, 0), (0, 0)))
    packed = jnp.concatenate([part(n) for n in SMALL] + [last], axis=0)
    assert packed.shape[0] == SMALL_ROWS
    return packed


def _unpack_small(packed, like):
    out, row = {}, 0
    for n in SMALL:
        size = math.prod(like[n].shape)
        rows = -(-size // (SUBLANES * LANES)) * SUBLANES
        out[n] = packed[row:row + rows].reshape(-1)[:size].reshape(like[n].shape)
        row += rows
    return out


def kernel(x, ffn1_norm, ffn1_w_gate, ffn1_w_up, ffn1_w_down, mix_norm, w_in, sink_logits, pool_w, pool_scale, w_out, ffn2_norm, ffn2_w_gate, ffn2_w_up, ffn2_w_down, final_norm, loss_target, m_ffn1_norm, m_ffn1_w_gate, m_ffn1_w_up, m_ffn1_w_down, m_mix_norm, m_w_in, m_sink_logits, m_pool_w, m_pool_scale, m_w_out, m_ffn2_norm, m_ffn2_w_gate, m_ffn2_w_up, m_ffn2_w_down, m_final_norm, v_ffn1_norm, v_ffn1_w_gate, v_ffn1_w_up, v_ffn1_w_down, v_mix_norm, v_w_in, v_sink_logits, v_pool_w, v_pool_scale, v_w_out, v_ffn2_norm, v_ffn2_w_gate, v_ffn2_w_up, v_ffn2_w_down, v_final_norm):
    names = ("ffn1_norm", "ffn1_w_gate", "ffn1_w_up", "ffn1_w_down", "mix_norm", "w_in", "sink_logits", "pool_w",
             "pool_scale", "w_out", "ffn2_norm", "ffn2_w_gate", "ffn2_w_up", "ffn2_w_down", "final_norm")
    weights = dict(zip(names, (ffn1_norm, ffn1_w_gate, ffn1_w_up, ffn1_w_down, mix_norm, w_in, sink_logits, pool_w,
                               pool_scale, w_out, ffn2_norm, ffn2_w_gate, ffn2_w_up, ffn2_w_down, final_norm)))
    mom1 = dict(zip(names, (m_ffn1_norm, m_ffn1_w_gate, m_ffn1_w_up, m_ffn1_w_down, m_mix_norm, m_w_in, m_sink_logits,
                            m_pool_w, m_pool_scale, m_w_out, m_ffn2_norm, m_ffn2_w_gate, m_ffn2_w_up, m_ffn2_w_down,
                            m_final_norm)))
    mom2 = dict(zip(names, (v_ffn1_norm, v_ffn1_w_gate, v_ffn1_w_up, v_ffn1_w_down, v_mix_norm, v_w_in, v_sink_logits,
                            v_pool_w, v_pool_scale, v_w_out, v_ffn2_norm, v_ffn2_w_gate, v_ffn2_w_up, v_ffn2_w_down,
                            v_final_norm)))
    chip = (2 * lax.axis_index("x") + lax.axis_index("y")).astype(jnp.int32).reshape(1)
    place = jnp.concatenate([lax.axis_index("c").astype(jnp.int32).reshape(1), chip])

    def rows_of(t, n):
        return jnp.swapaxes(t[n][0], 0, 1) if n in TRANSPOSED else t[n][0]

    bufs = [_pack(chip, [rows_of(weights, n) for n in GROUPS[0]], "pack_ffn1"),
            _pack(chip, [jnp.concatenate([rows_of(weights, n) for n in GROUPS[1]], axis=0)], "pack_mix"),
            _pack(chip, [rows_of(weights, n) for n in GROUPS[2]], "pack_ffn2")]

    small_w = {"ffn1_norm": ffn1_norm, "mix_norm": mix_norm, "ffn2_norm": ffn2_norm,
               "final_norm": final_norm.reshape(1, D_MODEL), "pool_scale": pool_scale, "sink_logits": sink_logits,
               "pool_w": pool_w[0]}
    loss, grad_x, group_grads, small_sum = _step(x[0], loss_target[0], bufs, small_w, place)

    out_g, out_d, out_m, out_v = {}, {}, {}, {}
    for members, g in zip(GROUPS, group_grads):
        row0 = 0
        for n in members:
            w = rows_of(weights, n)
            tile = FF_CHUNK // 4 if w.shape[0] == FF_CHUNK else math.gcd(IN_ROWS, OUT_ROWS)
            outs = _adamw(w, g, row0, rows_of(mom1, n), rows_of(mom2, n), tile, "adamw_" + n)
            row0 += w.shape[0]
            for dst, t in zip((out_g, out_d, out_m, out_v), outs):
                dst[n] = (jnp.swapaxes(t, 0, 1) if n in TRANSPOSED else t).reshape(weights[n].shape)
    small_outs = _adamw(_pack_small(weights), small_sum, 0, _pack_small(mom1), _pack_small(mom2), SMALL_ROWS, "adamw_small")
    for dst, packed in zip((out_g, out_d, out_m, out_v), small_outs):
        dst.update(_unpack_small(packed, weights))

    return (loss,grad_x.reshape(x.shape), *[out_g[n] for n in names], *[out_d[n] for n in names],
            *[out_m[n] for n in names], *[out_v[n] for n in names])
```

```python
import collections
import functools
import math

import jax
import jax.numpy as jnp
from jax import lax
from jax.experimental import pallas as pl
from jax.experimental.pallas import tpu as pltpu

F32, BF16 = jnp.float32, jnp.bfloat16
MESH = pl.DeviceIdType.MESH

D_MODEL = 1024
D_FF = 2816
N_CHIPS = 4
FF_CHUNK = D_FF // N_CHIPS
HEAD_DIM = 64
N_HEADS = 8
N_KV = 2
Q_PER_KV = N_HEADS // N_KV
KV_WIDTH = N_KV * HEAD_DIM
ATTN_WIDTH = N_HEADS * HEAD_DIM
POOL_WINDOWS = (2, 4, 8, 16)
N_POOL = len(POOL_WINDOWS)
POOL_GROUP = 128
POOL_WIDTH = N_POOL * POOL_GROUP
IN_WIDTH = ATTN_WIDTH + 2 * KV_WIDTH + POOL_WIDTH
WINDOW = 128
BLOCK = 128
BAND = 3 * BLOCK
ROPE_THETA = 500000.0
ROTARY_DIM = HEAD_DIM // 4
EPS = 1e-6
LANES = 128
Q_PAD = N_HEADS * LANES
U_PAD = Q_PAD + 2 * KV_WIDTH + POOL_WIDTH
SCALE = HEAD_DIM ** -0.5
NEG = -1e30

ADAM_LR, ADAM_B1, ADAM_B2, ADAM_EPS, ADAM_WD, ADAM_STEP = 0.001, 0.9, 0.999, 1e-08, 0.01, 10

V7X_VMEM_BYTES = 64 * 1024 * 1024
TOK_TILE = 512
SUBLANES = 8
SMALL_ROWS = 568


def _params(sem, vmem_mb):
    assert vmem_mb * 1024 * 1024 <= V7X_VMEM_BYTES
    return pltpu.CompilerParams(dimension_semantics=sem, vmem_limit_bytes=vmem_mb * 1024 * 1024)


def _dot(a, b):
    return lax.dot_general(a, b, (((1,), (0,)), ((), ())), preferred_element_type=F32)


def _dot_nt(a, b):
    return lax.dot_general(a, b, (((1,), (1,)), ((), ())), preferred_element_type=F32)


def _dot_tn(a, b):
    return lax.dot_general(a, b, (((0,), (0,)), ((), ())), preferred_element_type=F32)


def _rms_stats(h):
    r = lax.rsqrt(jnp.mean(h * h, axis=-1, keepdims=True) + EPS)
    return r, h * r


def _rms_bwd(dn, g, r, xh):
    gd = dn * g
    dh = r * (gd - xh * jnp.mean(gd * xh, axis=-1, keepdims=True))
    return dh, jnp.sum(dn * xh, axis=0, keepdims=True)


def _rope(x, c, s1, s2):
    return x * c + pltpu.roll(x, LANES - ROTARY_DIM // 2, 1) * s1 + pltpu.roll(x, ROTARY_DIM // 2, 1) * s2


def _rope_bwd(d, c, s1, s2):
    return d * c + pltpu.roll(d * s1, ROTARY_DIM // 2, 1) + pltpu.roll(d * s2, LANES - ROTARY_DIM // 2, 1)


def _sum_chunks(refs):
    terms = [ref[j].astype(F32) for ref in refs for j in range(ref.shape[0])]
    return functools.reduce(lambda a, b: a + b, terms)


def _chunk_rows(tile, k):
    return pl.BlockSpec((k, tile, D_MODEL), lambda i, *_: (0, i, 0))


def _full(shape):
    nd = len(shape)
    return pl.BlockSpec(shape, lambda *_: (0,) * nd)


def _rows(tile, cols):
    return pl.BlockSpec((tile, cols), lambda i, *_: (i, 0))


HBM_SPEC = pl.BlockSpec(memory_space=pltpu.HBM)

_Rider = collections.namedtuple("_Rider", "operands out_shapes aliases scratch start finish hooks", defaults=[()])


_NO_RIDER = _Rider([], [], {}, [], None, None)


def _call(body, name, grid, in_specs, out_specs, out_shape, scratch, vmem_mb, args, rider=None, prefetch=(),
          shares_rider_refs=False):
    rider = rider or _NO_RIDER
    n_pre, n_in, n_out, n_scr = len(prefetch), len(in_specs), len(out_specs), len(scratch)
    r_in, r_out = len(rider.operands), len(rider.out_shapes)

    def fused(*refs):
        pre, refs = refs[:n_pre], refs[n_pre:]
        ins, refs = refs[:n_in], refs[n_in:]
        r_ins, refs = refs[:r_in], refs[r_in:]
        outs, refs = refs[:n_out], refs[n_out:]
        r_outs, refs = refs[:r_out], refs[r_out:]
        scr, r_scr = refs[:n_scr], refs[n_scr:]
        ids = [pl.program_id(d) for d in range(len(grid))]
        if rider.start is not None:
            @pl.when(functools.reduce(jnp.logical_and, [i == 0 for i in ids]))
            def _():
                rider.start(r_ins, r_outs, r_scr)

        for at, hook in rider.hooks:
            @pl.when(functools.reduce(jnp.logical_and, [i == a for i, a in zip(ids, at)]))
            def _(hook=hook):
                hook(r_ins, r_outs, r_scr)

        if shares_rider_refs:
            body(*pre, *ins, *outs, *scr, rider_refs=r_outs)
        else:
            body(*pre, *ins, *outs, *scr)

        if rider.finish is not None:
            @pl.when(functools.reduce(jnp.logical_and, [i == g - 1 for i, g in zip(ids, grid)]))
            def _():
                rider.finish(r_ins, r_outs, r_scr)

    return pl.pallas_call(
        fused, name=name,
        grid_spec=pltpu.PrefetchScalarGridSpec(
            num_scalar_prefetch=n_pre, grid=grid,
            in_specs=list(in_specs) + [HBM_SPEC] * r_in, out_specs=list(out_specs) + [HBM_SPEC] * r_out,
            scratch_shapes=list(scratch) + list(rider.scratch)),
        out_shape=list(out_shape) + list(rider.out_shapes),
        input_output_aliases={n_pre + n_in + k: n_out + v for k, v in rider.aliases.items()},
        compiler_params=_params(("arbitrary",) * len(grid), vmem_mb),
    )(*prefetch, *args, *rider.operands)


def _after(token):
    return _Rider([token], [], {}, [], lambda *_: None, lambda *_: None)


def _comm_call(rider, name):
    r_in, r_out = len(rider.operands), len(rider.out_shapes)

    def body(*refs):
        r_ins, r_outs, r_scr = refs[:r_in], refs[r_in:r_in + r_out], refs[r_in + r_out:]
        rider.start(r_ins, r_outs, r_scr)
        rider.finish(r_ins, r_outs, r_scr)

    return pl.pallas_call(
        body, name=name, in_specs=[HBM_SPEC] * r_in, out_specs=[HBM_SPEC] * r_out, out_shape=list(rider.out_shapes),
        input_output_aliases=dict(rider.aliases), scratch_shapes=list(rider.scratch),
    )(*rider.operands)


def _ffn_fwd(order, h, gain, name, rider, group_at, loss_head=None):
    S = h.shape[0]
    tile = min(TOK_TILE, S)
    nt = S // tile
    last = N_CHIPS - 1
    n_extra_in, n_head_out = (2, 4) if loss_head else (0, 1)

    def body(order_ref, h_ref, g_ref, *refs, rider_refs):
        extra_in, refs = refs[:n_extra_in], refs[n_extra_in:]
        head_out, (n_ref, gate_ref, up_ref, w_scr, w_sem, acc, n_scr) = refs[:n_head_out], refs[n_head_out:]
        j, i = pl.program_id(0), pl.program_id(1)

        def fetch(chunk_pass):
            return pltpu.make_async_copy(rider_refs[group_at].at[order_ref[chunk_pass]], w_scr.at[chunk_pass % 2],
                                         w_sem.at[chunk_pass % 2])

        @pl.when(jnp.logical_and(j == 0, i == 0))
        def _():
            fetch(j).start()

        @pl.when(i == 0)
        def _():
            fetch(j).wait()

        at = pl.multiple_of(i * tile, tile)

        @pl.when(j == 0)
        def _():
            _, xh = _rms_stats(h_ref[...])
            n = (xh * g_ref[...]).astype(BF16)
            n_scr[pl.ds(at, tile), :] = n
            n_ref[...] = n
            acc[pl.ds(at, tile), :] = jnp.zeros((tile, D_MODEL), F32)

        half = tile // 2
        wg, wu, wd = (w_scr[j % 2, part * FF_CHUNK:(part + 1) * FF_CHUNK, :] for part in range(3))
        ns = [n_scr[pl.ds(at + s * half, half), :] for s in range(2)]
        gates = [_dot_nt(n, wg) for n in ns]
        ups = [_dot_nt(n, wu) for n in ns]
        acts = [(g * jax.nn.sigmoid(g) * u).astype(BF16) for g, u in zip(gates, ups)]
        for s in range(2):
            gate_ref[0, s * half:(s + 1) * half, :] = gates[s].astype(BF16)
            up_ref[0, s * half:(s + 1) * half, :] = ups[s].astype(BF16)
            acc[pl.ds(at + s * half, half), :] += _dot(acts[s], wd)

        @pl.when(jnp.logical_and(i == nt - 1, j < last))
        def _():
            fetch(j + 1).start()

        @pl.when(j == last)
        def _():
            out = h_ref[...] + 0.5 * acc[pl.ds(at, tile), :]
            if not loss_head:
                head_out[0][...] = out
                return
            (t_ref, gf_ref), (dh_ref, dhalf_ref, loss_ref, dg_ref) = extra_in, head_out

            @pl.when(i == 0)
            def _():
                loss_ref[...] = jnp.zeros_like(loss_ref)
                dg_ref[...] = jnp.zeros_like(dg_ref)

            gf = gf_ref[...]
            r, xh = _rms_stats(out)
            err = xh * gf - t_ref[...]
            loss_ref[...] += (0.5 / D_MODEL) * jnp.sum(err * err, axis=0, keepdims=True)
            dh, dg = _rms_bwd(err * (1.0 / D_MODEL), gf, r, xh)
            dg_ref[...] += dg
            dh_ref[...] = dh
            dhalf_ref[...] = (0.5 * dh).astype(BF16)

    tok = pl.BlockSpec((tile, D_MODEL), lambda j, i, order_ref: (i, 0))
    hid = pl.BlockSpec((1, tile, FF_CHUNK), lambda j, i, order_ref: (order_ref[j], i, 0))
    row = pl.BlockSpec((1, D_MODEL), lambda j, i, order_ref: (0, 0))
    in_last = pl.BlockSpec((tile, D_MODEL), lambda j, i, order_ref: (jnp.where(j == last, i, 0), 0))
    in_first = pl.BlockSpec((tile, D_MODEL), lambda j, i, order_ref: (jnp.where(j == 0, i, nt - 1), 0))
    tok_f32, tok_bf16, lanes = (jax.ShapeDtypeStruct((S, D_MODEL), F32), jax.ShapeDtypeStruct((S, D_MODEL), BF16),
                                jax.ShapeDtypeStruct((1, D_MODEL), F32))
    hidden = jax.ShapeDtypeStruct((N_CHIPS, S, FF_CHUNK), BF16)
    if loss_head:
        extra_specs, extra_args = [in_last, row], list(loss_head)
        head_specs, head_shapes = [in_last, in_last, row, row], [tok_f32, tok_bf16, lanes, lanes]
    else:
        extra_specs, extra_args, head_specs, head_shapes = [], [], [in_last], [tok_f32]
    return _call(
        body, name, (N_CHIPS, nt), [tok, row] + extra_specs, head_specs + [in_first, hid, hid],
        head_shapes + [tok_bf16, hidden, hidden],
        [pltpu.VMEM((2, 3 * FF_CHUNK, D_MODEL), BF16), pltpu.SemaphoreType.DMA((2,)), pltpu.VMEM((S, D_MODEL), F32),
         pltpu.VMEM((S, D_MODEL), BF16)], 62, (h, gain, *extra_args), rider, (order,), shares_rider_refs=True)


def _ffn_bwd(chunks, d_out, n, gate, up, group, name, rider=None):
    S = n.shape[0]
    n_chunks = chunks.shape[0]
    tile = min(TOK_TILE, S)
    nt = S // tile
    half_rows = 3 * FF_CHUNK // 2
    cut = FF_CHUNK // 2

    def body(chunks_ref, do_ref, n_ref, gate_ref, up_ref, wg_ref, wu_ref, wd_ref, dn_ref, dw_ref, acc_g, acc_u, acc_d):
        j, i = pl.program_id(0), pl.program_id(1)

        @pl.when(i == 0)
        def _():
            acc_g[...] = jnp.zeros_like(acc_g)
            acc_u[...] = jnp.zeros_like(acc_u)
            acc_d[...] = jnp.zeros_like(acc_d)

        halves = [pl.ds(s * (tile // 2), tile // 2) for s in range(2)]
        dos = [do_ref[rows, :] for rows in halves]
        d_acts = [_dot_nt(do, wd_ref[0]) for do in dos]
        gs = [gate_ref[0, rows, :].astype(F32) for rows in halves]
        us = [up_ref[0, rows, :].astype(F32) for rows in halves]
        sigs = [jax.nn.sigmoid(g) for g in gs]
        silus = [g * sig for g, sig in zip(gs, sigs)]
        d_ups = [(d_act * silu).astype(BF16) for d_act, silu in zip(d_acts, silus)]
        d_gates = [(d_act * u * (sig * (1.0 + g * (1.0 - sig)))).astype(BF16) for d_act, u, sig, g in zip(d_acts, us, sigs, gs)]
        for rows, d_gate, d_up in zip(halves, d_gates, d_ups):
            dn_ref[0, rows, :] = (_dot(d_gate, wg_ref[0]) + _dot(d_up, wu_ref[0])).astype(BF16)
        d_gate, d_up = jnp.concatenate(d_gates, axis=0), jnp.concatenate(d_ups, axis=0)
        act = jnp.concatenate([(silu * u).astype(BF16) for silu, u in zip(silus, us)], axis=0)
        nn = n_ref[...]
        acc_g[...] += _dot_tn(d_gate, nn)
        acc_u[...] += _dot_tn(d_up, nn)
        acc_d[...] += _dot_tn(act, do_ref[...])

        @pl.when(i == nt - 1)
        def _():
            dw_ref[0, 0, :FF_CHUNK, :] = acc_g[...].astype(BF16)
            dw_ref[0, 0, FF_CHUNK:, :] = acc_u[:cut, :].astype(BF16)
            dw_ref[1, 0, :cut, :] = acc_u[cut:, :].astype(BF16)
            dw_ref[1, 0, cut:, :] = acc_d[...].astype(BF16)

    tok = pl.BlockSpec((tile, D_MODEL), lambda j, i, chunks_ref: (i, 0))
    hid = pl.BlockSpec((1, tile, FF_CHUNK), lambda j, i, chunks_ref: (chunks_ref[j], i, 0))
    return _call(
        body, name, (n_chunks, nt),
        [tok, tok, hid, hid]
        + [pl.BlockSpec((1, FF_CHUNK, D_MODEL), functools.partial(lambda j, i, chunks_ref, part: (chunks_ref[j], part, 0), part=part))
           for part in range(3)],
        [pl.BlockSpec((1, tile, D_MODEL), lambda j, i, chunks_ref: (j, i, 0)),
         pl.BlockSpec((2, 1, half_rows, D_MODEL), lambda j, i, chunks_ref: (0, j, 0, 0))],
        [jax.ShapeDtypeStruct((n_chunks, S, D_MODEL), BF16), jax.ShapeDtypeStruct((2, n_chunks, half_rows, D_MODEL), BF16)],
        [pltpu.VMEM((FF_CHUNK, D_MODEL), F32)] * 3, 56, (d_out, n, gate, up, group, group, group), rider, (chunks,))


def _mix_in(h, gain, w_in, rc, rs1, rs2, name):
    S = h.shape[0]
    tile = min(TOK_TILE, S)

    def body(h_ref, g_ref, w_ref, c_ref, s1_ref, s2_ref, n_ref, q_ref, k_ref, v_ref, pc_ref):
        _, xh = _rms_stats(h_ref[...])
        n = (xh * g_ref[...]).astype(BF16)
        n_ref[...] = n
        u = _dot_nt(n, w_ref[...])
        c, s1, s2 = c_ref[...], s1_ref[...], s2_ref[...]
        q_ref[...] = jnp.concatenate([(_rope(u[:, hd * LANES:(hd + 1) * LANES], c, s1, s2) * SCALE).astype(BF16)
                                      for hd in range(N_HEADS)], axis=1)
        k_ref[...] = _rope(u[:, Q_PAD:Q_PAD + KV_WIDTH], c, s1, s2).astype(BF16)
        v_ref[...] = u[:, Q_PAD + KV_WIDTH:Q_PAD + 2 * KV_WIDTH].astype(BF16)
        pc_ref[...] = u[:, Q_PAD + 2 * KV_WIDTH:]

    return pl.pallas_call(
        body, name=name, grid=(S // tile,),
        in_specs=[_rows(tile, D_MODEL), _full((1, D_MODEL)), _full((U_PAD, D_MODEL)),
                  _rows(tile, LANES), _rows(tile, LANES), _rows(tile, LANES)],
        out_specs=[_rows(tile, D_MODEL), _rows(tile, Q_PAD), _rows(tile, KV_WIDTH), _rows(tile, KV_WIDTH),
                   _rows(tile, POOL_WIDTH)],
        out_shape=[jax.ShapeDtypeStruct((S, D_MODEL), BF16), jax.ShapeDtypeStruct((S, Q_PAD), BF16),
                   jax.ShapeDtypeStruct((S, KV_WIDTH), BF16), jax.ShapeDtypeStruct((S, KV_WIDTH), BF16),
                   jax.ShapeDtypeStruct((S, POOL_WIDTH), F32)],
        compiler_params=_params(("parallel",), 40),
    )(h, gain, w_in, rc, rs1, rs2)


def _band_start(i, S):
    return pl.multiple_of(jnp.clip((i - 1) * BLOCK, 0, S - BAND), BLOCK)


def _window_bias(off):
    r = lax.broadcasted_iota(jnp.int32, (BLOCK, 1), 0)
    c = lax.broadcasted_iota(jnp.int32, (1, BAND), 1)
    return jnp.where(jnp.abs(off + r - c) <= WINDOW, 0.0, NEG).astype(F32)


def _softmax_parts(qh, kb, bias, sink_h):
    s = _dot_nt(qh, kb) + bias
    m = jnp.maximum(jnp.max(s, axis=-1, keepdims=True), sink_h)
    p = jnp.exp(s - m)
    es = jnp.exp(sink_h - m)
    return p, es, 1.0 / (jnp.sum(p, axis=-1, keepdims=True) + es)


def _pool_matrix(t0, start, S, w):
    r = lax.broadcasted_iota(jnp.int32, (BLOCK, 1), 0) + t0
    c = lax.broadcasted_iota(jnp.int32, (1, BAND), 1) + start
    half = w // 2

    def window(lo, hi):
        a = jnp.maximum(lo, 0)
        b = jnp.minimum(hi + 1, S)
        return jnp.where((c >= a) & (c < b), 1.0 / (b - a).astype(F32), 0.0)

    return (0.5 * (window(r - half, r + half - 1) + window(r - half + 1, r + half))).astype(BF16)


def _pool_matrices(S):
    blocks = ((0, 0), (BLOCK, 0), (S - BLOCK, S - BAND))
    return jnp.stack([jnp.stack([_pool_matrix(t0, start, S, w) for w in POOL_WINDOWS]) for t0, start in blocks])


def _pool_spec(nb):
    return pl.BlockSpec((1, N_POOL, BLOCK, BAND), lambda i, *_: (jnp.where(i == 0, 0, jnp.where(i == nb - 1, 2, 1)), 0, 0, 0))


def _mix_core_fwd(q, k, v, pc, sink, pool_m, pool_w, pool_scale, name):
    S = q.shape[0]
    nb = S // BLOCK

    def body(sink_ref, q_ref, k_ref, v_ref, pc_ref, pm_ref, pw_ref, ps_ref, a_ref, p_ref):
        i = pl.program_id(0)
        start = _band_start(i, S)
        band = pl.ds(start, BAND)
        bias = _window_bias(i * BLOCK - start)
        kb, vb = k_ref[band, :], v_ref[band, :]
        hs = range(N_HEADS)
        ss = [_dot_nt(q_ref[:, hd * LANES:(hd + 1) * LANES], kb) + bias for hd in hs]
        ms = [jnp.maximum(jnp.max(ss[hd], axis=-1, keepdims=True), sink_ref[0, hd]) for hd in hs]
        ps = [jnp.exp(ss[hd] - ms[hd]) for hd in hs]
        invs = [1.0 / (jnp.sum(ps[hd], axis=-1, keepdims=True) + jnp.exp(sink_ref[0, hd] - ms[hd])) for hd in hs]
        outs = [_dot(ps[hd].astype(BF16), vb) for hd in hs]
        a_ref[...] = jnp.concatenate([(outs[hd] * invs[hd]).astype(BF16) for hd in hs], axis=1)
        centre = pl.ds(pl.multiple_of(i * BLOCK, BLOCK), BLOCK)
        gs = range(N_POOL)
        sl = [slice(g * POOL_GROUP, (g + 1) * POOL_GROUP) for g in gs]
        means = [_dot(pm_ref[0, g], pc_ref[band, sl[g]].astype(BF16)) for g in gs]
        devs = [(means[g] - pc_ref[centre, sl[g]]).astype(BF16) for g in gs]
        p_ref[...] = (jnp.concatenate([_dot(devs[g], pw_ref[g]) for g in gs], axis=1) * ps_ref[...]).astype(BF16)

    return _call(
        body, name, (nb,),
        [pl.BlockSpec(memory_space=pltpu.SMEM), _rows(BLOCK, Q_PAD), _full((S, KV_WIDTH)), _full((S, KV_WIDTH)),
         _full((S, POOL_WIDTH)), _pool_spec(nb), _full((N_POOL, POOL_GROUP, POOL_GROUP)), _full((1, POOL_WIDTH))],
        [_rows(BLOCK, Q_PAD), _rows(BLOCK, POOL_WIDTH)],
        [jax.ShapeDtypeStruct((S, Q_PAD), BF16), jax.ShapeDtypeStruct((S, POOL_WIDTH), BF16)],
        [], 40, (sink, q, k, v, pc, pool_m, pool_w, pool_scale))


def _mix_core_bwd(q, k, v, pc, da, dp, sink, pool_m, pool_w, pool_scale, rc, rs1, rs2, name):
    S = q.shape[0]
    nb = S // BLOCK

    def body(sink_ref, q_ref, k_ref, v_ref, pc_ref, da_ref, dp_ref, pm_ref, pw_ref, ps_ref, c_ref, s1_ref, s2_ref,
             dq_ref, dk_ref, dv_ref, dpc_ref, dsink_ref, dpw_ref, dps_ref):
        i = pl.program_id(0)

        @pl.when(i == 0)
        def _():
            dk_ref[...] = jnp.zeros_like(dk_ref)
            dv_ref[...] = jnp.zeros_like(dv_ref)
            dpc_ref[...] = jnp.zeros_like(dpc_ref)
            dsink_ref[...] = jnp.zeros_like(dsink_ref)
            dpw_ref[...] = jnp.zeros_like(dpw_ref)
            dps_ref[...] = jnp.zeros_like(dps_ref)

        start = _band_start(i, S)
        band = pl.ds(start, BAND)
        bias = _window_bias(i * BLOCK - start)
        kb, vb = k_ref[band, :], v_ref[band, :]
        c, s1, s2 = c_ref[...], s1_ref[...], s2_ref[...]
        lane = lax.broadcasted_iota(jnp.int32, (1, LANES), 1)
        hs = range(N_HEADS)
        qs = [q_ref[:, hd * LANES:(hd + 1) * LANES] for hd in hs]
        das = [da_ref[:, hd * LANES:(hd + 1) * LANES] for hd in hs]
        ss = [_dot_nt(qs[hd], kb) + bias for hd in hs]
        d_probs = [_dot_nt(das[hd], vb) for hd in hs]
        ms = [jnp.maximum(jnp.max(ss[hd], axis=-1, keepdims=True), sink_ref[0, hd]) for hd in hs]
        ps = [jnp.exp(ss[hd] - ms[hd]) for hd in hs]
        ess = [jnp.exp(sink_ref[0, hd] - ms[hd]) for hd in hs]
        invs = [1.0 / (jnp.sum(ps[hd], axis=-1, keepdims=True) + ess[hd]) for hd in hs]
        probs = [ps[hd] * invs[hd] for hd in hs]
        deltas = [jnp.sum(probs[hd] * d_probs[hd], axis=-1, keepdims=True) for hd in hs]
        d_ss = [(probs[hd] * (d_probs[hd] - deltas[hd])).astype(BF16) for hd in hs]
        dqs = [_dot(d_ss[hd], kb) for hd in hs]
        dq_ref[...] = jnp.concatenate([_rope_bwd(dqs[hd] * SCALE, c, s1, s2).astype(BF16) for hd in hs], axis=1)
        dks = [_dot_tn(d_ss[hd], qs[hd]) for hd in hs]
        dvs = [_dot_tn(probs[hd].astype(BF16), das[hd]) for hd in hs]
        dk_ref[band, :] += functools.reduce(lambda a, b: a + b, dks)
        dv_ref[band, :] += functools.reduce(lambda a, b: a + b, dvs)
        dsink_ref[...] += functools.reduce(lambda a, b: a + b, [
            jnp.where(lane == hd, -jnp.sum(ess[hd] * invs[hd] * deltas[hd], axis=0, keepdims=True), 0.0) for hd in hs])

        centre = pl.ds(pl.multiple_of(i * BLOCK, BLOCK), BLOCK)
        gs = range(N_POOL)
        sl = [slice(g * POOL_GROUP, (g + 1) * POOL_GROUP) for g in gs]
        devs = [(_dot(pm_ref[0, g], pc_ref[band, sl[g]].astype(BF16)) - pc_ref[centre, sl[g]]).astype(BF16) for g in gs]
        dys = [dp_ref[:, sl[g]].astype(F32) for g in gs]
        zs = [_dot(devs[g], pw_ref[g]) for g in gs]
        dzs = [(dys[g] * ps_ref[:, sl[g]]).astype(BF16) for g in gs]
        d_devs = [_dot_nt(dzs[g], pw_ref[g]) for g in gs]
        dps_ref[...] += jnp.concatenate([jnp.sum(dys[g] * zs[g], axis=0, keepdims=True) for g in gs], axis=1)
        for g in gs:
            dpw_ref[g] += _dot_tn(devs[g], dzs[g])
        dpc_ref[band, :] += jnp.concatenate([_dot_tn(pm_ref[0, g], d_devs[g].astype(BF16)) for g in gs], axis=1)
        dpc_ref[centre, :] -= jnp.concatenate(d_devs, axis=1)

    return _call(
        body, name, (nb,),
        [pl.BlockSpec(memory_space=pltpu.SMEM), _rows(BLOCK, Q_PAD), _full((S, KV_WIDTH)), _full((S, KV_WIDTH)),
         _full((S, POOL_WIDTH)), _rows(BLOCK, Q_PAD), _rows(BLOCK, POOL_WIDTH), _pool_spec(nb),
         _full((N_POOL, POOL_GROUP, POOL_GROUP)), _full((1, POOL_WIDTH)),
         _rows(BLOCK, LANES), _rows(BLOCK, LANES), _rows(BLOCK, LANES)],
        [_rows(BLOCK, Q_PAD), _full((S, KV_WIDTH)), _full((S, KV_WIDTH)), _full((S, POOL_WIDTH)),
         _full((1, LANES)), _full((N_POOL, POOL_GROUP, POOL_GROUP)), _full((1, POOL_WIDTH))],
        [jax.ShapeDtypeStruct((S, Q_PAD), BF16), jax.ShapeDtypeStruct((S, KV_WIDTH), F32),
         jax.ShapeDtypeStruct((S, KV_WIDTH), F32), jax.ShapeDtypeStruct((S, POOL_WIDTH), F32),
         jax.ShapeDtypeStruct((1, LANES), F32), jax.ShapeDtypeStruct((N_POOL, POOL_GROUP, POOL_GROUP), F32),
         jax.ShapeDtypeStruct((1, POOL_WIDTH), F32)],
        [], 56, (sink, q, k, v, pc, da, dp, pool_m, pool_w, pool_scale, rc, rs1, rs2))


def _mix_out(h, a, p, wa, wp, name):
    S = h.shape[0]
    tile = min(TOK_TILE, S)

    def body(h_ref, a_ref, p_ref, wa_ref, wp_ref, o_ref):
        o_ref[...] = h_ref[...] + _dot(a_ref[...], wa_ref[...]) + _dot(p_ref[...], wp_ref[...])

    return pl.pallas_call(
        body, name=name, grid=(S // tile,),
        in_specs=[_rows(tile, D_MODEL), _rows(tile, Q_PAD), _rows(tile, POOL_WIDTH),
                  _full((Q_PAD, D_MODEL)), _full((POOL_WIDTH, D_MODEL))],
        out_specs=_rows(tile, D_MODEL),
        out_shape=jax.ShapeDtypeStruct((S, D_MODEL), F32),
        compiler_params=_params(("parallel",), 40),
    )(h, a, p, wa, wp)


def _mix_out_bwd(dh_out, dn, h, gain, a, p, wa, wp, name, rider=None):
    S = h.shape[0]
    tile = min(TOK_TILE, S)

    def body(do_ref, dn_ref, h_ref, g_ref, a_ref, p_ref, wa_ref, wp_ref, dh_ref, da_ref, dp_ref, dwa_ref, dwp_ref, dg_ref):
        @pl.when(pl.program_id(0) == 0)
        def _():
            dwa_ref[...] = jnp.zeros_like(dwa_ref)
            dwp_ref[...] = jnp.zeros_like(dwp_ref)
            dg_ref[...] = jnp.zeros_like(dg_ref)

        r, xh = _rms_stats(h_ref[...])
        dnorm, dg = _rms_bwd(_sum_chunks([dn_ref]), g_ref[...], r, xh)
        dh = do_ref[...] + dnorm
        dg_ref[...] += dg
        dh_ref[...] = dh
        dhb = dh.astype(BF16)
        da_ref[...] = _dot_nt(dhb, wa_ref[...]).astype(BF16)
        dp_ref[...] = _dot_nt(dhb, wp_ref[...]).astype(BF16)
        dwa_ref[...] += _dot_tn(a_ref[...], dhb)
        dwp_ref[...] += _dot_tn(p_ref[...], dhb)

    return _call(
        body, name, (S // tile,),
        [_rows(tile, D_MODEL), _chunk_rows(tile, dn.shape[0]), _rows(tile, D_MODEL), _full((1, D_MODEL)),
         _rows(tile, Q_PAD), _rows(tile, POOL_WIDTH), _full((Q_PAD, D_MODEL)), _full((POOL_WIDTH, D_MODEL))],
        [_rows(tile, D_MODEL), _rows(tile, Q_PAD), _rows(tile, POOL_WIDTH),
         _full((Q_PAD, D_MODEL)), _full((POOL_WIDTH, D_MODEL)), _full((1, D_MODEL))],
        [jax.ShapeDtypeStruct((S, D_MODEL), F32), jax.ShapeDtypeStruct((S, Q_PAD), BF16),
         jax.ShapeDtypeStruct((S, POOL_WIDTH), BF16), jax.ShapeDtypeStruct((Q_PAD, D_MODEL), F32),
         jax.ShapeDtypeStruct((POOL_WIDTH, D_MODEL), F32), jax.ShapeDtypeStruct((1, D_MODEL), F32)],
        [], 48, (dh_out, dn, h, gain, a, p, wa, wp), rider)


def _mix_in_bwd(dh_out, h, gain, n, dq, dk, dv, dpc, rc, rs1, rs2, w_in, name):
    S = h.shape[0]
    tile = min(TOK_TILE, S)

    def body(do_ref, h_ref, g_ref, n_ref, dq_ref, dk_ref, dv_ref, dpc_ref, c_ref, s1_ref, s2_ref, w_ref,
             dh_ref, dhalf_ref, dw_ref, dg_ref):
        @pl.when(pl.program_id(0) == 0)
        def _():
            dw_ref[...] = jnp.zeros_like(dw_ref)
            dg_ref[...] = jnp.zeros_like(dg_ref)

        dk = _rope_bwd(dk_ref[...], c_ref[...], s1_ref[...], s2_ref[...]).astype(BF16)
        du = jnp.concatenate([dq_ref[...], dk, dv_ref[...].astype(BF16), dpc_ref[...].astype(BF16)], axis=1)
        dn = _dot(du, w_ref[...])
        dw_ref[...] += _dot_tn(du, n_ref[...])
        r, xh = _rms_stats(h_ref[...])
        dnorm, dg = _rms_bwd(dn, g_ref[...], r, xh)
        dh = do_ref[...] + dnorm
        dg_ref[...] += dg
        dh_ref[...] = dh
        dhalf_ref[...] = (0.5 * dh).astype(BF16)

    return _call(
        body, name, (S // tile,),
        [_rows(tile, D_MODEL), _rows(tile, D_MODEL), _full((1, D_MODEL)), _rows(tile, D_MODEL),
         _rows(tile, Q_PAD), _rows(tile, KV_WIDTH), _rows(tile, KV_WIDTH), _rows(tile, POOL_WIDTH),
         _rows(tile, LANES), _rows(tile, LANES), _rows(tile, LANES), _full((U_PAD, D_MODEL))],
        [_rows(tile, D_MODEL), _rows(tile, D_MODEL), _full((U_PAD, D_MODEL)), _full((1, D_MODEL))],
        [jax.ShapeDtypeStruct((S, D_MODEL), F32), jax.ShapeDtypeStruct((S, D_MODEL), BF16),
         jax.ShapeDtypeStruct((U_PAD, D_MODEL), F32), jax.ShapeDtypeStruct((1, D_MODEL), F32)],
        [], 56, (dh_out, h, gain, n, dq, dk, dv, dpc, rc, rs1, rs2, w_in))


def _norm_bwd(dh_out, dns, h, gain, name):
    S = h.shape[0]
    tile = min(TOK_TILE, S)
    n = len(dns)

    def body(do_ref, *refs):
        h_ref, g_ref, dh_ref, dg_ref = refs[n:]

        @pl.when(pl.program_id(0) == 0)
        def _():
            dg_ref[...] = jnp.zeros_like(dg_ref)

        r, xh = _rms_stats(h_ref[...])
        dnorm, dg = _rms_bwd(_sum_chunks(refs[:n]), g_ref[...], r, xh)
        dg_ref[...] += dg
        dh_ref[...] = do_ref[...] + dnorm

    return _call(
        body, name, (S // tile,),
        [_rows(tile, D_MODEL)] + [_chunk_rows(tile, dn.shape[0]) for dn in dns] + [_rows(tile, D_MODEL), _full((1, D_MODEL))],
        [_rows(tile, D_MODEL), _full((1, D_MODEL))],
        [jax.ShapeDtypeStruct((S, D_MODEL), F32), jax.ShapeDtypeStruct((1, D_MODEL), F32)],
        [], 40, (dh_out, *dns, h, gain))


def _rope_tables(S):
    half = ROTARY_DIM // 2
    inv_freq = ROPE_THETA ** (-jnp.arange(0, ROTARY_DIM, 2, dtype=F32) / ROTARY_DIM)
    dim = jnp.arange(LANES) % HEAD_DIM
    ang = jnp.arange(S, dtype=F32)[:, None] * inv_freq[dim % half][None, :]
    lo, hi = (dim < half)[None, :], ((dim >= half) & (dim < ROTARY_DIM))[None, :]
    c = jnp.where(lo | hi, jnp.cos(ang), 1.0)
    s1 = jnp.where(lo, -jnp.sin(ang), 0.0)
    s2 = jnp.where(hi, jnp.sin(ang), 0.0)
    return c, s1, s2


def _pad_heads(w, axis):
    w = jnp.moveaxis(w, axis, 0)
    heads = w.reshape((N_HEADS, HEAD_DIM) + w.shape[1:])
    zero = jnp.zeros_like(heads)
    first = (jnp.arange(N_HEADS) < Q_PER_KV).reshape((N_HEADS, 1) + (1,) * (w.ndim - 1))
    lo = jnp.where(first, heads, zero)
    hi = jnp.where(first, zero, heads)
    padded = jnp.concatenate([lo, hi], axis=1).reshape((Q_PAD,) + w.shape[1:])
    return jnp.moveaxis(padded, 0, axis)


def _unpad_heads(w, axis):
    w = jnp.moveaxis(w, axis, 0)
    groups = w.reshape((N_HEADS, 2, HEAD_DIM) + w.shape[1:])
    first = (jnp.arange(N_HEADS) < Q_PER_KV).reshape((N_HEADS, 1) + (1,) * (w.ndim - 1))
    heads = jnp.where(first, groups[:, 0], groups[:, 1]).reshape((ATTN_WIDTH,) + w.shape[1:])
    return jnp.moveaxis(heads, 0, axis)


IN_ROWS = IN_WIDTH // N_CHIPS
OUT_ROWS = (ATTN_WIDTH + POOL_WIDTH) // N_CHIPS
MIX_ROWS = IN_ROWS + OUT_ROWS
FFN_ROWS = 3 * FF_CHUNK


def _step(x, target, bufs, small, place):
    S = x.shape[0]
    rc, rs1, rs2 = _rope_tables(S)
    mine = place[1]
    order = jnp.stack([mine, mine ^ 2, mine ^ 1, mine ^ 3])
    steps = S // min(TOK_TILE, S)
    h1, n1, gate1, up1, ffn1, mix = _ffn_fwd(order, x, small["ffn1_norm"], "ffn1_fwd",
                                             _merge(_allgather(bufs[:1], in_passes=(0, steps)), _allgather(bufs[1:2])), 0)
    in_flight, (ffn2,), token = _split_start(bufs[2:], 2, _gather_copies(X_Y_DIAGONAL[:2]), "ffn2_gather_start")
    w_in_t = mix[:, :IN_ROWS].reshape(IN_WIDTH, D_MODEL)
    w_in_pad = jnp.concatenate([_pad_heads(w_in_t[:ATTN_WIDTH], 0), w_in_t[ATTN_WIDTH:]], axis=0)
    w_out = mix[:, IN_ROWS:].reshape(ATTN_WIDTH + POOL_WIDTH, D_MODEL)
    wa = _pad_heads(w_out[:ATTN_WIDTH], 0)
    wp = w_out[ATTN_WIDTH:]
    pool_w = small["pool_w"].astype(BF16)

    n2, q, k, v, pc = _mix_in(h1, small["mix_norm"] + token[0, 0], w_in_pad, rc, rs1, rs2, "mix_in")
    pool_m = _pool_matrices(S)
    a, p = _mix_core_fwd(q, k, v, pc, small["sink_logits"], pool_m, pool_w, small["pool_scale"], "mix_core_fwd")
    h2 = _mix_out(h1, a, p, wa, wp, "mix_out")
    (ffn2,) = _split_wait(in_flight, [ffn2], _gather_copies(X_Y_DIAGONAL[:2]), h2, "ffn2_gather_wait")
    dh3, dhalf3, loss_lanes, d_final, n3, gate2, up2, ffn2 = _ffn_fwd(
        order, h2, small["ffn2_norm"], "ffn2_fwd",
        _allgather([ffn2], peers=X_Y_DIAGONAL[2:], in_passes=(0, steps), landed_before=X_Y_DIAGONAL[:2]), 0,
        loss_head=(target, small["final_norm"]))

    dn3, d_ffn2 = _ffn_bwd(jnp.arange(N_CHIPS, dtype=jnp.int32), dhalf3, n3, gate2, up2, ffn2, "ffn2_bwd")
    dh2, da, dp, dwa, dwp, d_ffn2_norm, received = _mix_out_bwd(dh3, dn3, h2, small["ffn2_norm"], a, p, wa, wp, "mix_out_bwd",
                                                                _sibling_exchange([d_ffn2]))
    pair = _pair_sum(place, d_ffn2, received, "grad_pair_sum_ffn2")
    in_flight, (pair, stack), token = _split_start([pair, received], 3, _scatter_copies, "ffn2_scatter_start")
    dq, dk, dv, dpc, dsink, dpool_w, dpool_scale = _mix_core_bwd(
        q, k, v, pc, da, dp, small["sink_logits"], pool_m, pool_w, small["pool_scale"] + token[0, 0], rc, rs1, rs2, "mix_core_bwd")
    dh1, dhalf1, dw_in_pad, d_mix_norm = _mix_in_bwd(dh2, h1, small["mix_norm"], n2, dq, dk, dv, dpc, rc, rs1, rs2, w_in_pad,
                                                     "mix_in_bwd")
    pair, stack = _split_wait(in_flight, [pair, stack], _scatter_copies, d_mix_norm, "ffn2_scatter_wait")
    reduced_ffn2 = _chip_sum(place, pair, stack, 2, "grad_chip_sum_ffn2")
    dw_in_t = jnp.concatenate([_unpad_heads(dw_in_pad[:Q_PAD], 0), dw_in_pad[Q_PAD:]], axis=0)
    dw_out = jnp.concatenate([_unpad_heads(dwa, 0), dwp], axis=0)
    d_mix = jnp.concatenate([dw_in_t.reshape(N_CHIPS, IN_ROWS, D_MODEL), dw_out.reshape(N_CHIPS, OUT_ROWS, D_MODEL)], axis=1)
    d_mix = jnp.transpose(d_mix.reshape(N_CHIPS, 2, MIX_ROWS // 2, D_MODEL), (1, 0, 2, 3)).astype(BF16)
    small_g = {"ffn1_norm": jnp.zeros_like(d_mix_norm), "mix_norm": d_mix_norm, "ffn2_norm": d_ffn2_norm,
               "final_norm": d_final, "pool_scale": dpool_scale, "sink_logits": dsink[:, :N_HEADS], "pool_w": dpool_w}
    loss_row = jnp.sum(loss_lanes.reshape(D_MODEL // LANES, LANES), axis=0, keepdims=True)
    small_early = _pack_small(small_g, loss_row)

    chunk = [(place[1:] + 1 + p) % N_CHIPS for p in range(N_CHIPS)]
    ffn1_bwd = functools.partial(_ffn_bwd, d_out=dhalf1, n=n1, gate=gate1, up=up1, group=ffn1)
    dn_a, dw_a, recv_mix, small_all, g_ffn2 = ffn1_bwd(
        chunk[0], name="ffn1_bwd_0",
        rider=_merge(_merge(_sibling_exchange([d_mix]), _small_allgather(small_early)), _sibling_share([reduced_ffn2])))
    pair_mix = _pair_sum(place, d_mix, recv_mix, "grad_pair_sum_mix")
    dn_b, dw_b, recv_a, stack_mix = ffn1_bwd(chunk[1], name="ffn1_bwd_1",
                                             rider=_merge(_sibling_exchange([dw_a]), _scatter([pair_mix])))
    pair_a = _pair_sum(place, dw_a, recv_a, "grad_pair_sum_ffn1_0")
    reduced_mix = _chip_sum(place, pair_mix, stack_mix, 2, "grad_chip_sum_mix")
    steps = [(_scatter_step_copies(p), (p, 3)) for p in range(3)]
    sems_a, (pair_a, stack), token = _split_start([pair_a, stack], 1, steps[0][0], "ffn1_scatter_start_0")
    dn_c, dw_c, recv_b, g_mix = ffn1_bwd(
        chunk[2], name="ffn1_bwd_2", rider=_merge(_merge(_sibling_exchange([dw_b]), _sibling_share([reduced_mix])), _after(token)))
    pair_b = _pair_sum(place, dw_b, recv_b, "grad_pair_sum_ffn1_1")
    sems_b, (pair_b, stack), token = _split_start([pair_b, stack], 1, steps[1][0], "ffn1_scatter_start_1")
    dn_d, dw_d, recv_c = ffn1_bwd(chunk[3], name="ffn1_bwd_3", rider=_merge(_sibling_exchange([dw_c]), _after(token)))
    pair_c = _pair_sum(place, dw_c, recv_c, "grad_pair_sum_ffn1_2")
    own = (_exchange_copies, (4, 5))
    sems_c, (pair_c, stack, dw_d, recv_d), token = _split_start(
        [pair_c, stack, dw_d, recv_c], 2, _joined([(steps[2][0], (0, 1)), (own[0], (2, 3))]), "ffn1_scatter_start_2")
    grad_x, d_ffn1_norm = _norm_bwd(dh1, [dn_a, dn_b, dn_c, dn_d], x, small["ffn1_norm"] + token[0, 0], "norm1_bwd")
    pair_a, pair_b, pair_c, stack, dw_d, recv_d = _split_wait(
        sems_a + sems_b + sems_c, [pair_a, pair_b, pair_c, stack, dw_d, recv_d], _joined(steps + [own]), d_ffn1_norm,
        "ffn1_scatter_wait")
    pair_d = _pair_sum(place, dw_d, recv_d, "grad_pair_sum_ffn1_3")
    reduced_ffn1 = _chip_sum(place, pair_d, stack, 2, "grad_chip_sum_ffn1")
    g_ffn1, gains = _comm_call(_merge(_sibling_share([reduced_ffn1]), _small_allgather(d_ffn1_norm.reshape(-1, LANES))),
                               "grad_share_tail")
    gain_sum = _sum_leading(gains, 1, "gain_grad_sum")
    small_sum = jnp.concatenate([gain_sum, _sum_leading(small_all, 1, "small_grad_sum")[gain_sum.shape[0]:]], axis=0)
    return jnp.sum(small_sum[SMALL_ROWS - 1]), grad_x, [g.reshape(-1, D_MODEL) for g in (g_ffn1, g_mix, g_ffn2)], small_sum


GROUPS =(("ffn1_w_gate", "ffn1_w_up", "ffn1_w_down"), ("w_in", "w_out"), ("ffn2_w_gate", "ffn2_w_up", "ffn2_w_down"))
TRANSPOSED = ("ffn1_w_gate", "ffn1_w_up", "w_in", "ffn2_w_gate", "ffn2_w_up")


def _place():
    x, y, c = lax.axis_index("x"), lax.axis_index("y"), lax.axis_index("c")
    chips = [(1 - x, y), (x, 1 - y), (1 - x, 1 - y)]
    return x, y, c, chips


def _remote(src, dst, send_sem, recv_sem, to):
    return pltpu.make_async_remote_copy(src_ref=src, dst_ref=dst, send_sem=send_sem, recv_sem=recv_sem,
                                        device_id=to, device_id_type=MESH)


def _pack(chip, members, name):
    rows = members[0].shape[0]
    n = len(members)

    def body(chip_ref, *refs):
        ins, out_ref, buf, sems = refs[:n], refs[n], refs[n + 1], refs[n + 2]
        copies = [pltpu.make_async_copy(ins[k], buf.at[k], sems.at[k]) for k in range(n)]
        for cp in copies:
            cp.start()
        for k in range(n):
            copies[k].wait()
            out_ref[0, k * rows:(k + 1) * rows, :] = buf[k].astype(BF16)

    return pl.pallas_call(
        body, name=name,
        grid_spec=pltpu.PrefetchScalarGridSpec(
            num_scalar_prefetch=1, grid=(1,),
            in_specs=[HBM_SPEC] * n,
            out_specs=pl.BlockSpec((1, n * rows, D_MODEL), lambda k, chip_ref: (chip_ref[0], 0, 0)),
            scratch_shapes=[pltpu.VMEM((n, rows, D_MODEL), F32), pltpu.SemaphoreType.DMA((n,))]),
        out_shape=jax.ShapeDtypeStruct((N_CHIPS, n * rows, D_MODEL), BF16),
        compiler_params=_params(("arbitrary",), 40),
    )(chip, *members)


def _same(arrays):
    return [jax.ShapeDtypeStruct(a.shape, a.dtype) for a in arrays]


X_Y_DIAGONAL = (0, 1, 2)


def _allgather(bufs, peers=X_Y_DIAGONAL, in_passes=None, landed_before=()):
    n = len(bufs)

    def copy(kind, outs, send_sems, recv_sems, a, k):
        x, y, c, chips = _place()
        half = bufs[a].shape[1] // 2

        def rows(slot, core):
            return outs[a].at[slot, pl.ds(pl.multiple_of(core * half, 16), half)]

        me, slot = 2 * x + y, 2 * chips[k][0] + chips[k][1]
        over_ici = (send_sems.at[6 * a + k], recv_sems.at[6 * a + k])
        over_d2d = (send_sems.at[6 * a + 3 + k], recv_sems.at[6 * a + 3 + k])
        if kind == "first":
            return _remote(rows(me, c), rows(me, c), *over_ici, (*chips[k], c))
        if kind == "landed":
            return _remote(rows(me, c), rows(slot, c), *over_ici, (*chips[k], c))
        if kind == "passed":
            return _remote(rows(slot, c), rows(slot, c), *over_d2d, (x, y, 1 - c))
        return _remote(rows(me, c), rows(slot, 1 - c), *over_d2d, (x, y, 1 - c))

    def arrive(pairs):
        def hook(ins, outs, sems):
            for a, k in pairs:
                if k not in landed_before:
                    copy("landed", outs, *sems, a, k).wait_recv()
                    copy("passed", outs, *sems, a, k).start()
            for a, k in pairs:
                copy("handed", outs, *sems, a, k).wait_recv()
        return hook

    sent = [(a, k) for a in range(n) for k in peers]
    everything = sent + [(in_passes[0], k) for k in landed_before]
    early = [(a, k) for a, k in everything if in_passes and a == in_passes[0]]

    def start(ins, outs, sems):
        for k in landed_before:
            copy("passed", outs, *sems, in_passes[0], k).start()
        for a, k in sent:
            copy("first", outs, *sems, a, k).start()

    def finish(ins, outs, sems):
        arrive([pair for pair in everything if pair not in early])(ins, outs, sems)
        for a, k in sent:
            copy("first", outs, *sems, a, k).wait_send()
        for a, k in everything:
            copy("passed", outs, *sems, a, k).wait_send()

    hooks = tuple(((k, in_passes[1] - 1), arrive([(a, k)])) for a, k in early)
    return _Rider(list(bufs), _same(bufs), {a: a for a in range(n)},
                  [pltpu.SemaphoreType.DMA((6 * n,)), pltpu.SemaphoreType.DMA((6 * n,))], start, finish, hooks)


SEM_SPEC = pl.BlockSpec(memory_space=pltpu.SEMAPHORE)


def _split_start(arrays, n_copies, make_copies, name):
    m, n = len(arrays), 2 * n_copies

    def body(*refs):
        sems, thru, token = refs[m:m + n], refs[m + n:2 * m + n], refs[2 * m + n]
        for cp in make_copies(thru, sems, False):
            cp.start()
        token[...] = jnp.zeros_like(token)

    outs = pl.pallas_call(
        body, name=name,
        out_shape=[pltpu.SemaphoreType.DMA(())] * n + _same(arrays) + [jax.ShapeDtypeStruct((SUBLANES, LANES), F32)],
        in_specs=[HBM_SPEC] * m, out_specs=[SEM_SPEC] * n + [HBM_SPEC] * m + [pl.BlockSpec(memory_space=pltpu.VMEM)],
        input_output_aliases={a: n + a for a in range(m)},
        compiler_params=pltpu.CompilerParams(has_side_effects=pltpu.SideEffectType.DATAFLOW_SIDE_EFFECTING),
    )(*arrays)
    return list(outs[:n]), list(outs[n:n + m]), outs[n + m]


def _split_wait(sems, arrays, make_copies, after, name):
    m, n = len(arrays), len(sems)

    def body(*refs):
        for cp in make_copies(refs[m + n + 1:], refs[m:m + n], True):
            cp.wait_send()
            cp.wait_recv()

    return pl.pallas_call(
        body, name=name, out_shape=_same(arrays),
        in_specs=[HBM_SPEC] * m + [SEM_SPEC] * n + [pl.BlockSpec(memory_space=pl.ANY)], out_specs=[HBM_SPEC] * m,
        input_output_aliases={a: a for a in range(m)},
        compiler_params=pltpu.CompilerParams(has_side_effects=pltpu.SideEffectType.DATAFLOW_SIDE_EFFECTING),
    )(*arrays, *sems, after)


def _gather_copies(peers):
    def make(refs, sems, landing):
        x, y, c, chips = _place()
        half = refs[0].shape[1] // 2

        def rows(slot):
            return refs[0].at[slot, pl.ds(pl.multiple_of(c * half, 16), half)]

        me = 2 * x + y
        return [_remote(rows(me), rows(2 * chips[k][0] + chips[k][1] if landing else me), sems[2 * j], sems[2 * j + 1],
                        (*chips[k], c)) for j, k in enumerate(peers)]
    return make


def _scatter_copies(refs, sems, landing):
    x, y, c, chips = _place()
    me = 2 * x + y
    slots = [2 * cx + cy for cx, cy in chips]
    return [_remote(refs[0].at[slots[k]], refs[1].at[slots[k] if landing else me], sems[2 * k], sems[2 * k + 1], (*chips[k], c))
            for k in X_Y_DIAGONAL]


def _scatter_step_copies(step):
    def make(refs, sems, landing):
        x, y, c, _ = _place()
        me = 2 * x + y
        to = (me + 1 + step) % N_CHIPS
        frm = (me + N_CHIPS - 1 - step) % N_CHIPS
        peer = frm if landing else to
        return [_remote(refs[0].at[0], refs[1].at[frm if landing else me], sems[0], sems[1], (peer // 2, peer % 2, c))]
    return make


def _exchange_copies(refs, sems, landing):
    x, y, c, _ = _place()
    return [_remote(refs[0].at[1 - c], refs[1], sems[0], sems[1], (x, y, 1 - c))]


def _joined(makers):
    def make(refs, sems, landing):
        copies, at = [], 0
        for maker, places in makers:
            mine = maker([refs[p] for p in places], sems[at:], landing)
            copies += mine
            at += 2 * len(mine)
        return copies
    return make


def _sibling_exchange(parts):
    n = len(parts)

    def copies(ins, outs, send_sems, recv_sems):
        x, y, c, _ = _place()
        return [_remote(ins[a].at[1 - c], outs[a], send_sems.at[a], recv_sems.at[a], (x, y, 1 - c)) for a in range(n)]

    def start(ins, outs, sems):
        for cp in copies(ins, outs, *sems):
            cp.start()

    def finish(ins, outs, sems):
        for cp in copies(ins, outs, *sems):
            cp.wait_recv()
            cp.wait_send()

    return _Rider(list(parts), [jax.ShapeDtypeStruct(p.shape[1:], p.dtype) for p in parts], {},
                  [pltpu.SemaphoreType.DMA((n,)), pltpu.SemaphoreType.DMA((n,))], start, finish)


def _small_allgather(small):
    flips = [(fx, fy, fc) for fx in range(2) for fy in range(2) for fc in range(2)][1:]

    def copies(small_ref, gather_ref, send_sems, recv_sems, local_sem, started_only=False):
        x, y, c, _ = _place()
        me = 4 * x + 2 * y + c
        peers = [((1 - x) if fx else x, (1 - y) if fy else y, (1 - c) if fc else c) for fx, fy, fc in flips]
        own = pltpu.make_async_copy(small_ref, gather_ref.at[me], local_sem)
        sent = [_remote(small_ref, gather_ref.at[me], send_sems.at[k], recv_sems.at[k], peer) for k, peer in enumerate(peers)]
        if started_only:
            return own, sent
        landed = [_remote(small_ref, gather_ref.at[4 * px + 2 * py + pc], send_sems.at[k], recv_sems.at[k], (px, py, pc))
                  for k, (px, py, pc) in enumerate(peers)]
        return own, sent, landed

    def start(ins, outs, sems):
        own, sent = copies(ins[0], outs[0], *sems, started_only=True)
        own.start()
        for cp in sent:
            cp.start()

    def finish(ins, outs, sems):
        own, sent, landed = copies(ins[0], outs[0], *sems)
        for cp in landed:
            cp.wait_recv()
        for cp in sent:
            cp.wait_send()
        own.wait()

    return _Rider([small], [jax.ShapeDtypeStruct((2 * N_CHIPS,) + small.shape, small.dtype)], {},
                  [pltpu.SemaphoreType.DMA((7,)), pltpu.SemaphoreType.DMA((7,)), pltpu.SemaphoreType.DMA], start, finish)


def _merge(a, b):
    na, nao, nas = len(a.operands), len(a.out_shapes), len(a.scratch)

    def start(ins, outs, sems):
        a.start(ins[:na], outs[:nao], sems[:nas])
        b.start(ins[na:], outs[nao:], sems[nas:])

    def finish(ins, outs, sems):
        a.finish(ins[:na], outs[:nao], sems[:nas])
        b.finish(ins[na:], outs[nao:], sems[nas:])

    def of_a(fn):
        return lambda ins, outs, sems: fn(ins[:na], outs[:nao], sems[:nas])

    def of_b(fn):
        return lambda ins, outs, sems: fn(ins[na:], outs[nao:], sems[nas:])

    aliases = {**a.aliases, **{na + k: nao + v for k, v in b.aliases.items()}}
    hooks = tuple((at, of_a(fn)) for at, fn in a.hooks) + tuple((at, of_b(fn)) for at, fn in b.hooks)
    return _Rider(a.operands + b.operands, a.out_shapes + b.out_shapes, aliases, a.scratch + b.scratch, start, finish, hooks)


def _scatter(sums):
    n = len(sums)

    def copies(ins, outs, send_sems, recv_sems, started_only=False):
        x, y, c, chips = _place()
        me = 2 * x + y
        slots = [2 * cx + cy for cx, cy in chips]
        sent = [_remote(ins[a].at[slots[k]], outs[a].at[me], send_sems.at[3 * a + k], recv_sems.at[3 * a + k], (*chips[k], c))
                for a in range(n) for k in X_Y_DIAGONAL]
        if started_only:
            return sent
        landed = [_remote(ins[a].at[slots[k]], outs[a].at[slots[k]], send_sems.at[3 * a + k], recv_sems.at[3 * a + k],
                          (*chips[k], c)) for a in range(n) for k in X_Y_DIAGONAL]
        return sent, landed

    def start(ins, outs, sems):
        for cp in copies(ins, outs, *sems, started_only=True):
            cp.start()

    def finish(ins, outs, sems):
        sent, landed = copies(ins, outs, *sems)
        for cp in landed:
            cp.wait_recv()
        for cp in sent:
            cp.wait_send()

    return _Rider(list(sums), _same(sums), {}, [pltpu.SemaphoreType.DMA((3 * n,)), pltpu.SemaphoreType.DMA((3 * n,))],
                  start, finish)


def _sibling_share(bufs):
    n = len(bufs)

    def copies(outs, send_sems, recv_sems, started_only=False):
        x, y, c, _ = _place()
        sent = [_remote(outs[a].at[c], outs[a].at[c], send_sems.at[a], recv_sems.at[a], (x, y, 1 - c)) for a in range(n)]
        if started_only:
            return sent
        landed = [_remote(outs[a].at[c], outs[a].at[1 - c], send_sems.at[a], recv_sems.at[a], (x, y, 1 - c)) for a in range(n)]
        return sent, landed

    def start(ins, outs, sems):
        for cp in copies(outs, *sems, started_only=True):
            cp.start()

    def finish(ins, outs, sems):
        sent, landed = copies(outs, *sems)
        for cp in landed:
            cp.wait_recv()
        for cp in sent:
            cp.wait_send()

    return _Rider(list(bufs), _same(bufs), {a: a for a in range(n)},
                  [pltpu.SemaphoreType.DMA((n,)), pltpu.SemaphoreType.DMA((n,))], start, finish)


def _pair_sum(core, part, received, name):
    _, k, rh, cols = part.shape

    def body(core_ref, p_ref, r_ref, o_ref):
        o_ref[...] = (p_ref[0].astype(F32) + r_ref[...].astype(F32)).astype(BF16)

    return pl.pallas_call(
        body, name=name,
        grid_spec=pltpu.PrefetchScalarGridSpec(
            num_scalar_prefetch=1, grid=(k,),
            in_specs=[pl.BlockSpec((1, 1, rh, cols), lambda j, core_ref: (core_ref[0], j, 0, 0)),
                      pl.BlockSpec((1, rh, cols), lambda j, core_ref: (j, 0, 0))],
            out_specs=pl.BlockSpec((1, rh, cols), lambda j, core_ref: (j, 0, 0))),
        out_shape=jax.ShapeDtypeStruct((k, rh, cols), BF16),
        compiler_params=_params(("parallel",), 32),
    )(core, part, received)


def _sum_leading(stack, steps, name):
    k, rows, cols = stack.shape
    tile = rows // steps

    def body(s_ref, o_ref):
        total = s_ref[0].astype(F32)
        for d in range(1, k):
            total = total + s_ref[d].astype(F32)
        o_ref[...] = total

    return pl.pallas_call(
        body, name=name, grid=(steps,),
        in_specs=[pl.BlockSpec((k, tile, cols), lambda i: (0, i, 0))],
        out_specs=pl.BlockSpec((tile, cols), lambda i: (i, 0)),
        out_shape=jax.ShapeDtypeStruct((rows, cols), F32),
        compiler_params=_params(("parallel",), 32),
    )(stack)


def _chip_sum(place, own, stack, steps, name):
    k, rows, cols = stack.shape
    tile = rows // steps

    def body(place_ref, own_ref, *refs):
        chip = place_ref[1]
        total = None
        for d in range(k):
            term = jnp.where(chip == d, own_ref[0], refs[d][0]).astype(F32)
            total = term if total is None else total + term
        refs[k][0] = total

    def other(d):
        return lambda i, place_ref: (jnp.where(place_ref[1] == d, (d + 1) % k, d), i, 0)

    return pl.pallas_call(
        body, name=name,
        grid_spec=pltpu.PrefetchScalarGridSpec(
            num_scalar_prefetch=1, grid=(steps,),
            in_specs=[pl.BlockSpec((1, tile, cols), lambda i, place_ref: (place_ref[1] % own.shape[0], i, 0))]
            + [pl.BlockSpec((1, tile, cols), other(d)) for d in range(k)],
            out_specs=pl.BlockSpec((1, tile, cols), lambda i, place_ref: (place_ref[0], i, 0))),
        out_shape=jax.ShapeDtypeStruct((2, rows, cols), F32),
        compiler_params=_params(("arbitrary",), 32),
    )(place, own, *([stack] * k))


def _adamw(w, g, row0, m, v, tile, name):
    rows, cols = w.shape
    first = row0 // tile
    assert rows % tile == 0 and row0 % tile == 0
    bc1 = 1.0 - ADAM_B1 ** ADAM_STEP
    bc2 = 1.0 - ADAM_B2 ** ADAM_STEP

    def body(w_ref, g_ref, m_ref, v_ref, go_ref, d_ref, mo_ref, vo_ref):
        g = g_ref[...]
        m_new = ADAM_B1 * m_ref[...] + (1.0 - ADAM_B1) * g
        v_new = ADAM_B2 * v_ref[...] + (1.0 - ADAM_B2) * (g * g)
        go_ref[...] = g
        d_ref[...] = -ADAM_LR * ((m_new / bc1) / (jnp.sqrt(v_new / bc2) + ADAM_EPS) + ADAM_WD * w_ref[...])
        mo_ref[...] = m_new
        vo_ref[...] = v_new

    spec = pl.BlockSpec((tile, cols), lambda i: (i, 0))
    g_spec = pl.BlockSpec((tile, cols), lambda i: (first + i, 0))
    return pl.pallas_call(
        body, name=name, grid=(rows // tile,),
        in_specs=[spec, g_spec, spec, spec], out_specs=[spec] * 4,
        out_shape=[jax.ShapeDtypeStruct((rows, cols), F32)] * 4,
        compiler_params=_params(("parallel",), 32),
    )(w, g, m, v)


SMALL = ("ffn1_norm", "mix_norm", "ffn2_norm", "final_norm", "pool_scale", "sink_logits", "pool_w")


def _pack_small(d, last_row=None):
    def part(n):
        flat = d[n].reshape(-1)
        flat = jnp.pad(flat, (0, -flat.shape[0] % (SUBLANES * LANES)))
        return flat.reshape(-1, LANES)

    last = jnp.zeros((SUBLANES, LANES), F32) if last_row is None else jnp.pad(last_row, ((SUBLANES - 1, 0), (0, 0)))
    packed = jnp.concatenate([part(n) for n in SMALL] + [last], axis=0)
    assert packed.shape[0] == SMALL_ROWS
    return packed


def _unpack_small(packed, like):
    out, row = {}, 0
    for n in SMALL:
        size = math.prod(like[n].shape)
        rows = -(-size // (SUBLANES * LANES)) * SUBLANES
        out[n] = packed[row:row + rows].reshape(-1)[:size].reshape(like[n].shape)
        row += rows
    return out


def kernel(x, ffn1_norm, ffn1_w_gate, ffn1_w_up, ffn1_w_down, mix_norm, w_in, sink_logits, pool_w, pool_scale, w_out, ffn2_norm, ffn2_w_gate, ffn2_w_up, ffn2_w_down, final_norm, loss_target, m_ffn1_norm, m_ffn1_w_gate, m_ffn1_w_up, m_ffn1_w_down, m_mix_norm, m_w_in, m_sink_logits, m_pool_w, m_pool_scale, m_w_out, m_ffn2_norm, m_ffn2_w_gate, m_ffn2_w_up, m_ffn2_w_down, m_final_norm, v_ffn1_norm, v_ffn1_w_gate, v_ffn1_w_up, v_ffn1_w_down, v_mix_norm, v_w_in, v_sink_logits, v_pool_w, v_pool_scale, v_w_out, v_ffn2_norm, v_ffn2_w_gate, v_ffn2_w_up, v_ffn2_w_down, v_final_norm):
    names = ("ffn1_norm", "ffn1_w_gate", "ffn1_w_up", "ffn1_w_down", "mix_norm", "w_in", "sink_logits", "pool_w",
             "pool_scale", "w_out", "ffn2_norm", "ffn2_w_gate", "ffn2_w_up", "ffn2_w_down", "final_norm")
    weights = dict(zip(names, (ffn1_norm, ffn1_w_gate, ffn1_w_up, ffn1_w_down, mix_norm, w_in, sink_logits, pool_w,
                               pool_scale, w_out, ffn2_norm, ffn2_w_gate, ffn2_w_up, ffn2_w_down, final_norm)))
    mom1 = dict(zip(names, (m_ffn1_norm, m_ffn1_w_gate, m_ffn1_w_up, m_ffn1_w_down, m_mix_norm, m_w_in, m_sink_logits,
                            m_pool_w, m_pool_scale, m_w_out, m_ffn2_norm, m_ffn2_w_gate, m_ffn2_w_up, m_ffn2_w_down,
                            m_final_norm)))
    mom2 = dict(zip(names, (v_ffn1_norm, v_ffn1_w_gate, v_ffn1_w_up, v_ffn1_w_down, v_mix_norm, v_w_in, v_sink_logits,
                            v_pool_w, v_pool_scale, v_w_out, v_ffn2_norm, v_ffn2_w_gate, v_ffn2_w_up, v_ffn2_w_down,
                            v_final_norm)))
    chip = (2 * lax.axis_index("x") + lax.axis_index("y")).astype(jnp.int32).reshape(1)
    place = jnp.concatenate([lax.axis_index("c").astype(jnp.int32).reshape(1), chip])

    def rows_of(t, n):
        return jnp.swapaxes(t[n][0], 0, 1) if n in TRANSPOSED else t[n][0]

    bufs = [_pack(chip, [rows_of(weights, n) for n in GROUPS[0]], "pack_ffn1"),
            _pack(chip, [jnp.concatenate([rows_of(weights, n) for n in GROUPS[1]], axis=0)], "pack_mix"),
            _pack(chip, [rows_of(weights, n) for n in GROUPS[2]], "pack_ffn2")]

    small_w = {"ffn1_norm": ffn1_norm, "mix_norm": mix_norm, "ffn2_norm": ffn2_norm,
               "final_norm": final_norm.reshape(1, D_MODEL), "pool_scale": pool_scale, "sink_logits": sink_logits,
               "pool_w": pool_w[0]}
    loss, grad_x, group_grads, small_sum = _step(x[0], loss_target[0], bufs, small_w, place)

    out_g, out_d, out_m, out_v = {}, {}, {}, {}
    for members, g in zip(GROUPS, group_grads):
        row0 = 0
        for n in members:
            w = rows_of(weights, n)
            tile = FF_CHUNK // 4 if w.shape[0] == FF_CHUNK else math.gcd(IN_ROWS, OUT_ROWS)
            outs = _adamw(w, g, row0, rows_of(mom1, n), rows_of(mom2, n), tile, "adamw_" + n)
            row0 += w.shape[0]
            for dst, t in zip((out_g, out_d, out_m, out_v), outs):
                dst[n] = (jnp.swapaxes(t, 0, 1) if n in TRANSPOSED else t).reshape(weights[n].shape)
    small_outs = _adamw(_pack_small(weights), small_sum, 0, _pack_small(mom1), _pack_small(mom2), SMALL_ROWS, "adamw_small")
    for dst, packed in zip((out_g, out_d, out_m, out_v), small_outs):
        dst.update(_unpack_small(packed, weights))

    return (loss,grad_x.reshape(x.shape), *[out_g[n] for n in names], *[out_d[n] for n in names],
            *[out_m[n] for n in names], *[out_v[n] for n in names])
```

```python
import collections
import functools
import math

import jax
import jax.numpy as jnp
from jax import lax
from jax.experimental import pallas as pl
from jax.experimental.pallas import tpu as pltpu

F32, BF16 = jnp.float32, jnp.bfloat16
MESH = pl.DeviceIdType.MESH

D_MODEL = 1024
D_FF = 2816
N_CHIPS = 4
FF_CHUNK = D_FF // N_CHIPS
HEAD_DIM = 64
N_HEADS = 8
N_KV = 2
Q_PER_KV = N_HEADS // N_KV
KV_WIDTH = N_KV * HEAD_DIM
ATTN_WIDTH = N_HEADS * HEAD_DIM
POOL_WINDOWS = (2, 4, 8, 16)
N_POOL = len(POOL_WINDOWS)
POOL_GROUP = 128
POOL_WIDTH = N_POOL * POOL_GROUP
IN_WIDTH = ATTN_WIDTH + 2 * KV_WIDTH + POOL_WIDTH
WINDOW = 128
BLOCK = 128
BAND = 3 * BLOCK
ROPE_THETA = 500000.0
ROTARY_DIM = HEAD_DIM // 4
EPS = 1e-6
LANES = 128
Q_PAD = N_HEADS * LANES
U_PAD = Q_PAD + 2 * KV_WIDTH + POOL_WIDTH
SCALE = HEAD_DIM ** -0.5
NEG = -1e30

ADAM_LR, ADAM_B1, ADAM_B2, ADAM_EPS, ADAM_WD, ADAM_STEP = 0.001, 0.9, 0.999, 1e-08, 0.01, 10

V7X_VMEM_BYTES = 64 * 1024 * 1024
TOK_TILE = 512
SUBLANES = 8
SMALL_ROWS = 568


def _params(sem, vmem_mb):
    assert vmem_mb * 1024 * 1024 <= V7X_VMEM_BYTES
    return pltpu.CompilerParams(dimension_semantics=sem, vmem_limit_bytes=vmem_mb * 1024 * 1024)


def _dot(a, b):
    return lax.dot_general(a, b, (((1,), (0,)), ((), ())), preferred_element_type=F32)


def _dot_nt(a, b):
    return lax.dot_general(a, b, (((1,), (1,)), ((), ())), preferred_element_type=F32)


def _dot_tn(a, b):
    return lax.dot_general(a, b, (((0,), (0,)), ((), ())), preferred_element_type=F32)


def _rms_stats(h):
    r = lax.rsqrt(jnp.mean(h * h, axis=-1, keepdims=True) + EPS)
    return r, h * r


def _rms_bwd(dn, g, r, xh):
    gd = dn * g
    dh = r * (gd - xh * jnp.mean(gd * xh, axis=-1, keepdims=True))
    return dh, jnp.sum(dn * xh, axis=0, keepdims=True)


def _rope(x, c, s1, s2):
    return x * c + pltpu.roll(x, LANES - ROTARY_DIM // 2, 1) * s1 + pltpu.roll(x, ROTARY_DIM // 2, 1) * s2


def _rope_bwd(d, c, s1, s2):
    return d * c + pltpu.roll(d * s1, ROTARY_DIM // 2, 1) + pltpu.roll(d * s2, LANES - ROTARY_DIM // 2, 1)


def _sum_chunks(refs):
    terms = [ref[j].astype(F32) for ref in refs for j in range(ref.shape[0])]
    return functools.reduce(lambda a, b: a + b, terms)


def _chunk_rows(tile, k):
    return pl.BlockSpec((k, tile, D_MODEL), lambda i, *_: (0, i, 0))


def _full(shape):
    nd = len(shape)
    return pl.BlockSpec(shape, lambda *_: (0,) * nd)


def _rows(tile, cols):
    return pl.BlockSpec((tile, cols), lambda i, *_: (i, 0))


HBM_SPEC = pl.BlockSpec(memory_space=pltpu.HBM)

_Rider = collections.namedtuple("_Rider", "operands out_shapes aliases scratch start finish hooks", defaults=[()])


_NO_RIDER = _Rider([], [], {}, [], None, None)


def _call(body, name, grid, in_specs, out_specs, out_shape, scratch, vmem_mb, args, rider=None, prefetch=(),
          shares_rider_refs=False):
    rider = rider or _NO_RIDER
    n_pre, n_in, n_out, n_scr = len(prefetch), len(in_specs), len(out_specs), len(scratch)
    r_in, r_out = len(rider.operands), len(rider.out_shapes)

    def fused(*refs):
        pre, refs = refs[:n_pre], refs[n_pre:]
        ins, refs = refs[:n_in], refs[n_in:]
        r_ins, refs = refs[:r_in], refs[r_in:]
        outs, refs = refs[:n_out], refs[n_out:]
        r_outs, refs = refs[:r_out], refs[r_out:]
        scr, r_scr = refs[:n_scr], refs[n_scr:]
        ids = [pl.program_id(d) for d in range(len(grid))]
        if rider.start is not None:
            @pl.when(functools.reduce(jnp.logical_and, [i == 0 for i in ids]))
            def _():
                rider.start(r_ins, r_outs, r_scr)

        for at, hook in rider.hooks:
            @pl.when(functools.reduce(jnp.logical_and, [i == a for i, a in zip(ids, at)]))
            def _(hook=hook):
                hook(r_ins, r_outs, r_scr)

        if shares_rider_refs:
            body(*pre, *ins, *outs, *scr, rider_refs=r_outs)
        else:
            body(*pre, *ins, *outs, *scr)

        if rider.finish is not None:
            @pl.when(functools.reduce(jnp.logical_and, [i == g - 1 for i, g in zip(ids, grid)]))
            def _():
                rider.finish(r_ins, r_outs, r_scr)

    return pl.pallas_call(
        fused, name=name,
        grid_spec=pltpu.PrefetchScalarGridSpec(
            num_scalar_prefetch=n_pre, grid=grid,
            in_specs=list(in_specs) + [HBM_SPEC] * r_in, out_specs=list(out_specs) + [HBM_SPEC] * r_out,
            scratch_shapes=list(scratch) + list(rider.scratch)),
        out_shape=list(out_shape) + list(rider.out_shapes),
        input_output_aliases={n_pre + n_in + k: n_out + v for k, v in rider.aliases.items()},
        compiler_params=_params(("arbitrary",) * len(grid), vmem_mb),
    )(*prefetch, *args, *rider.operands)


def _after(token):
    return _Rider([token], [], {}, [], lambda *_: None, lambda *_: None)


def _comm_call(rider, name):
    r_in, r_out = len(rider.operands), len(rider.out_shapes)

    def body(*refs):
        r_ins, r_outs, r_scr = refs[:r_in], refs[r_in:r_in + r_out], refs[r_in + r_out:]
        rider.start(r_ins, r_outs, r_scr)
        rider.finish(r_ins, r_outs, r_scr)

    return pl.pallas_call(
        body, name=name, in_specs=[HBM_SPEC] * r_in, out_specs=[HBM_SPEC] * r_out, out_shape=list(rider.out_shapes),
        input_output_aliases=dict(rider.aliases), scratch_shapes=list(rider.scratch),
    )(*rider.operands)


def _ffn_fwd(order, h, gain, name, rider, group_at, loss_head=None):
    S = h.shape[0]
    tile = min(TOK_TILE, S)
    nt = S // tile
    last = N_CHIPS - 1
    n_extra_in, n_head_out = (2, 4) if loss_head else (0, 1)

    def body(order_ref, h_ref, g_ref, *refs, rider_refs):
        extra_in, refs = refs[:n_extra_in], refs[n_extra_in:]
        head_out, (n_ref, gate_ref, up_ref, w_scr, w_sem, acc, n_scr) = refs[:n_head_out], refs[n_head_out:]
        j, i = pl.program_id(0), pl.program_id(1)

        @pl.when(i == 0)
        def _():
            fetch = pltpu.make_async_copy(rider_refs[group_at].at[order_ref[j]], w_scr, w_sem)
            fetch.start()
            fetch.wait()

        at = pl.multiple_of(i * tile, tile)

        @pl.when(j == 0)
        def _():
            _, xh = _rms_stats(h_ref[...])
            n = (xh * g_ref[...]).astype(BF16)
            n_scr[pl.ds(at, tile), :] = n
            n_ref[...] = n
            acc[pl.ds(at, tile), :] = jnp.zeros((tile, D_MODEL), F32)

        half = tile // 2
        wg, wu, wd = (w_scr[part * FF_CHUNK:(part + 1) * FF_CHUNK, :] for part in range(3))
        ns = [n_scr[pl.ds(at + s * half, half), :] for s in range(2)]
        gates = [_dot_nt(n, wg) for n in ns]
        ups = [_dot_nt(n, wu) for n in ns]
        acts = [(g * jax.nn.sigmoid(g) * u).astype(BF16) for g, u in zip(gates, ups)]
        for s in range(2):
            gate_ref[0, s * half:(s + 1) * half, :] = gates[s].astype(BF16)
            up_ref[0, s * half:(s + 1) * half, :] = ups[s].astype(BF16)
            acc[pl.ds(at + s * half, half), :] += _dot(acts[s], wd)

        @pl.when(j == last)
        def _():
            out = h_ref[...] + 0.5 * acc[pl.ds(at, tile), :]
            if not loss_head:
                head_out[0][...] = out
                return
            (t_ref, gf_ref), (dh_ref, dhalf_ref, loss_ref, dg_ref) = extra_in, head_out

            @pl.when(i == 0)
            def _():
                loss_ref[...] = jnp.zeros_like(loss_ref)
                dg_ref[...] = jnp.zeros_like(dg_ref)

            gf = gf_ref[...]
            r, xh = _rms_stats(out)
            err = xh * gf - t_ref[...]
            loss_ref[...] += (0.5 / D_MODEL) * jnp.sum(err * err, axis=0, keepdims=True)
            dh, dg = _rms_bwd(err * (1.0 / D_MODEL), gf, r, xh)
            dg_ref[...] += dg
            dh_ref[...] = dh
            dhalf_ref[...] = (0.5 * dh).astype(BF16)

    tok = pl.BlockSpec((tile, D_MODEL), lambda j, i, order_ref: (i, 0))
    hid = pl.BlockSpec((1, tile, FF_CHUNK), lambda j, i, order_ref: (order_ref[j], i, 0))
    row = pl.BlockSpec((1, D_MODEL), lambda j, i, order_ref: (0, 0))
    in_last = pl.BlockSpec((tile, D_MODEL), lambda j, i, order_ref: (jnp.where(j == last, i, 0), 0))
    in_first = pl.BlockSpec((tile, D_MODEL), lambda j, i, order_ref: (jnp.where(j == 0, i, nt - 1), 0))
    tok_f32, tok_bf16, lanes = (jax.ShapeDtypeStruct((S, D_MODEL), F32), jax.ShapeDtypeStruct((S, D_MODEL), BF16),
                                jax.ShapeDtypeStruct((1, D_MODEL), F32))
    hidden = jax.ShapeDtypeStruct((N_CHIPS, S, FF_CHUNK), BF16)
    if loss_head:
        extra_specs, extra_args = [in_last, row], list(loss_head)
        head_specs, head_shapes = [in_last, in_last, row, row], [tok_f32, tok_bf16, lanes, lanes]
    else:
        extra_specs, extra_args, head_specs, head_shapes = [], [], [in_last], [tok_f32]
    return _call(
        body, name, (N_CHIPS, nt), [tok, row] + extra_specs, head_specs + [in_first, hid, hid],
        head_shapes + [tok_bf16, hidden, hidden],
        [pltpu.VMEM((3 * FF_CHUNK, D_MODEL), BF16), pltpu.SemaphoreType.DMA, pltpu.VMEM((S, D_MODEL), F32),
         pltpu.VMEM((S, D_MODEL), BF16)], 58, (h, gain, *extra_args), rider, (order,), shares_rider_refs=True)


def _ffn_bwd(chunks, d_out, n, gate, up, group, name, rider=None):
    S = n.shape[0]
    n_chunks = chunks.shape[0]
    tile = min(TOK_TILE, S)
    nt = S // tile
    half_rows = 3 * FF_CHUNK // 2
    cut = FF_CHUNK // 2

    def body(chunks_ref, do_ref, n_ref, gate_ref, up_ref, wg_ref, wu_ref, wd_ref, dn_ref, dw_ref, acc_g, acc_u, acc_d):
        j, i = pl.program_id(0), pl.program_id(1)

        @pl.when(i == 0)
        def _():
            acc_g[...] = jnp.zeros_like(acc_g)
            acc_u[...] = jnp.zeros_like(acc_u)
            acc_d[...] = jnp.zeros_like(acc_d)

        halves = [pl.ds(s * (tile // 2), tile // 2) for s in range(2)]
        dos = [do_ref[rows, :] for rows in halves]
        d_acts = [_dot_nt(do, wd_ref[0]) for do in dos]
        gs = [gate_ref[0, rows, :].astype(F32) for rows in halves]
        us = [up_ref[0, rows, :].astype(F32) for rows in halves]
        sigs = [jax.nn.sigmoid(g) for g in gs]
        silus = [g * sig for g, sig in zip(gs, sigs)]
        d_ups = [(d_act * silu).astype(BF16) for d_act, silu in zip(d_acts, silus)]
        d_gates = [(d_act * u * (sig * (1.0 + g * (1.0 - sig)))).astype(BF16) for d_act, u, sig, g in zip(d_acts, us, sigs, gs)]
        for rows, d_gate, d_up in zip(halves, d_gates, d_ups):
            dn_ref[0, rows, :] = (_dot(d_gate, wg_ref[0]) + _dot(d_up, wu_ref[0])).astype(BF16)
        d_gate, d_up = jnp.concatenate(d_gates, axis=0), jnp.concatenate(d_ups, axis=0)
        act = jnp.concatenate([(silu * u).astype(BF16) for silu, u in zip(silus, us)], axis=0)
        nn = n_ref[...]
        acc_g[...] += _dot_tn(d_gate, nn)
        acc_u[...] += _dot_tn(d_up, nn)
        acc_d[...] += _dot_tn(act, do_ref[...])

        @pl.when(i == nt - 1)
        def _():
            dw_ref[0, 0, :FF_CHUNK, :] = acc_g[...].astype(BF16)
            dw_ref[0, 0, FF_CHUNK:, :] = acc_u[:cut, :].astype(BF16)
            dw_ref[1, 0, :cut, :] = acc_u[cut:, :].astype(BF16)
            dw_ref[1, 0, cut:, :] = acc_d[...].astype(BF16)

    tok = pl.BlockSpec((tile, D_MODEL), lambda j, i, chunks_ref: (i, 0))
    hid = pl.BlockSpec((1, tile, FF_CHUNK), lambda j, i, chunks_ref: (chunks_ref[j], i, 0))
    return _call(
        body, name, (n_chunks, nt),
        [tok, tok, hid, hid]
        + [pl.BlockSpec((1, FF_CHUNK, D_MODEL), functools.partial(lambda j, i, chunks_ref, part: (chunks_ref[j], part, 0), part=part))
           for part in range(3)],
        [pl.BlockSpec((1, tile, D_MODEL), lambda j, i, chunks_ref: (j, i, 0)),
         pl.BlockSpec((2, 1, half_rows, D_MODEL), lambda j, i, chunks_ref: (0, j, 0, 0))],
        [jax.ShapeDtypeStruct((n_chunks, S, D_MODEL), BF16), jax.ShapeDtypeStruct((2, n_chunks, half_rows, D_MODEL), BF16)],
        [pltpu.VMEM((FF_CHUNK, D_MODEL), F32)] * 3, 56, (d_out, n, gate, up, group, group, group), rider, (chunks,))


def _mix_in(h, gain, w_in, rc, rs1, rs2, name):
    S = h.shape[0]
    tile = min(TOK_TILE, S)

    def body(h_ref, g_ref, w_ref, c_ref, s1_ref, s2_ref, n_ref, q_ref, k_ref, v_ref, pc_ref):
        _, xh = _rms_stats(h_ref[...])
        n = (xh * g_ref[...]).astype(BF16)
        n_ref[...] = n
        u = _dot_nt(n, w_ref[...])
        c, s1, s2 = c_ref[...], s1_ref[...], s2_ref[...]
        q_ref[...] = jnp.concatenate([(_rope(u[:, hd * LANES:(hd + 1) * LANES], c, s1, s2) * SCALE).astype(BF16)
                                      for hd in range(N_HEADS)], axis=1)
        k_ref[...] = _rope(u[:, Q_PAD:Q_PAD + KV_WIDTH], c, s1, s2).astype(BF16)
        v_ref[...] = u[:, Q_PAD + KV_WIDTH:Q_PAD + 2 * KV_WIDTH].astype(BF16)
        pc_ref[...] = u[:, Q_PAD + 2 * KV_WIDTH:]

    return pl.pallas_call(
        body, name=name, grid=(S // tile,),
        in_specs=[_rows(tile, D_MODEL), _full((1, D_MODEL)), _full((U_PAD, D_MODEL)),
                  _rows(tile, LANES), _rows(tile, LANES), _rows(tile, LANES)],
        out_specs=[_rows(tile, D_MODEL), _rows(tile, Q_PAD), _rows(tile, KV_WIDTH), _rows(tile, KV_WIDTH),
                   _rows(tile, POOL_WIDTH)],
        out_shape=[jax.ShapeDtypeStruct((S, D_MODEL), BF16), jax.ShapeDtypeStruct((S, Q_PAD), BF16),
                   jax.ShapeDtypeStruct((S, KV_WIDTH), BF16), jax.ShapeDtypeStruct((S, KV_WIDTH), BF16),
                   jax.ShapeDtypeStruct((S, POOL_WIDTH), F32)],
        compiler_params=_params(("parallel",), 40),
    )(h, gain, w_in, rc, rs1, rs2)


def _band_start(i, S):
    return pl.multiple_of(jnp.clip((i - 1) * BLOCK, 0, S - BAND), BLOCK)


def _window_bias(off):
    r = lax.broadcasted_iota(jnp.int32, (BLOCK, 1), 0)
    c = lax.broadcasted_iota(jnp.int32, (1, BAND), 1)
    return jnp.where(jnp.abs(off + r - c) <= WINDOW, 0.0, NEG).astype(F32)


def _softmax_parts(qh, kb, bias, sink_h):
    s = _dot_nt(qh, kb) + bias
    m = jnp.maximum(jnp.max(s, axis=-1, keepdims=True), sink_h)
    p = jnp.exp(s - m)
    es = jnp.exp(sink_h - m)
    return p, es, 1.0 / (jnp.sum(p, axis=-1, keepdims=True) + es)


def _pool_matrix(t0, start, S, w):
    r = lax.broadcasted_iota(jnp.int32, (BLOCK, 1), 0) + t0
    c = lax.broadcasted_iota(jnp.int32, (1, BAND), 1) + start
    half = w // 2

    def window(lo, hi):
        a = jnp.maximum(lo, 0)
        b = jnp.minimum(hi + 1, S)
        return jnp.where((c >= a) & (c < b), 1.0 / (b - a).astype(F32), 0.0)

    return (0.5 * (window(r - half, r + half - 1) + window(r - half + 1, r + half))).astype(BF16)


def _pool_matrices(S):
    blocks = ((0, 0), (BLOCK, 0), (S - BLOCK, S - BAND))
    return jnp.stack([jnp.stack([_pool_matrix(t0, start, S, w) for w in POOL_WINDOWS]) for t0, start in blocks])


def _pool_spec(nb):
    return pl.BlockSpec((1, N_POOL, BLOCK, BAND), lambda i, *_: (jnp.where(i == 0, 0, jnp.where(i == nb - 1, 2, 1)), 0, 0, 0))


def _mix_core_fwd(q, k, v, pc, sink, pool_m, pool_w, pool_scale, name):
    S = q.shape[0]
    nb = S // BLOCK

    def body(sink_ref, q_ref, k_ref, v_ref, pc_ref, pm_ref, pw_ref, ps_ref, a_ref, p_ref):
        i = pl.program_id(0)
        start = _band_start(i, S)
        band = pl.ds(start, BAND)
        bias = _window_bias(i * BLOCK - start)
        kb, vb = k_ref[band, :], v_ref[band, :]
        hs = range(N_HEADS)
        ss = [_dot_nt(q_ref[:, hd * LANES:(hd + 1) * LANES], kb) + bias for hd in hs]
        ms = [jnp.maximum(jnp.max(ss[hd], axis=-1, keepdims=True), sink_ref[0, hd]) for hd in hs]
        ps = [jnp.exp(ss[hd] - ms[hd]) for hd in hs]
        invs = [1.0 / (jnp.sum(ps[hd], axis=-1, keepdims=True) + jnp.exp(sink_ref[0, hd] - ms[hd])) for hd in hs]
        outs = [_dot(ps[hd].astype(BF16), vb) for hd in hs]
        a_ref[...] = jnp.concatenate([(outs[hd] * invs[hd]).astype(BF16) for hd in hs], axis=1)
        centre = pl.ds(pl.multiple_of(i * BLOCK, BLOCK), BLOCK)
        gs = range(N_POOL)
        sl = [slice(g * POOL_GROUP, (g + 1) * POOL_GROUP) for g in gs]
        means = [_dot(pm_ref[0, g], pc_ref[band, sl[g]].astype(BF16)) for g in gs]
        devs = [(means[g] - pc_ref[centre, sl[g]]).astype(BF16) for g in gs]
        p_ref[...] = (jnp.concatenate([_dot(devs[g], pw_ref[g]) for g in gs], axis=1) * ps_ref[...]).astype(BF16)

    return _call(
        body, name, (nb,),
        [pl.BlockSpec(memory_space=pltpu.SMEM), _rows(BLOCK, Q_PAD), _full((S, KV_WIDTH)), _full((S, KV_WIDTH)),
         _full((S, POOL_WIDTH)), _pool_spec(nb), _full((N_POOL, POOL_GROUP, POOL_GROUP)), _full((1, POOL_WIDTH))],
        [_rows(BLOCK, Q_PAD), _rows(BLOCK, POOL_WIDTH)],
        [jax.ShapeDtypeStruct((S, Q_PAD), BF16), jax.ShapeDtypeStruct((S, POOL_WIDTH), BF16)],
        [], 40, (sink, q, k, v, pc, pool_m, pool_w, pool_scale))


def _mix_core_bwd(q, k, v, pc, da, dp, sink, pool_m, pool_w, pool_scale, rc, rs1, rs2, name):
    S = q.shape[0]
    nb = S // BLOCK

    def body(sink_ref, q_ref, k_ref, v_ref, pc_ref, da_ref, dp_ref, pm_ref, pw_ref, ps_ref, c_ref, s1_ref, s2_ref,
             dq_ref, dk_ref, dv_ref, dpc_ref, dsink_ref, dpw_ref, dps_ref):
        i = pl.program_id(0)

        @pl.when(i == 0)
        def _():
            dk_ref[...] = jnp.zeros_like(dk_ref)
            dv_ref[...] = jnp.zeros_like(dv_ref)
            dpc_ref[...] = jnp.zeros_like(dpc_ref)
            dsink_ref[...] = jnp.zeros_like(dsink_ref)
            dpw_ref[...] = jnp.zeros_like(dpw_ref)
            dps_ref[...] = jnp.zeros_like(dps_ref)

        start = _band_start(i, S)
        band = pl.ds(start, BAND)
        bias = _window_bias(i * BLOCK - start)
        kb, vb = k_ref[band, :], v_ref[band, :]
        c, s1, s2 = c_ref[...], s1_ref[...], s2_ref[...]
        lane = lax.broadcasted_iota(jnp.int32, (1, LANES), 1)
        hs = range(N_HEADS)
        qs = [q_ref[:, hd * LANES:(hd + 1) * LANES] for hd in hs]
        das = [da_ref[:, hd * LANES:(hd + 1) * LANES] for hd in hs]
        ss = [_dot_nt(qs[hd], kb) + bias for hd in hs]
        d_probs = [_dot_nt(das[hd], vb) for hd in hs]
        ms = [jnp.maximum(jnp.max(ss[hd], axis=-1, keepdims=True), sink_ref[0, hd]) for hd in hs]
        ps = [jnp.exp(ss[hd] - ms[hd]) for hd in hs]
        ess = [jnp.exp(sink_ref[0, hd] - ms[hd]) for hd in hs]
        invs = [1.0 / (jnp.sum(ps[hd], axis=-1, keepdims=True) + ess[hd]) for hd in hs]
        probs = [ps[hd] * invs[hd] for hd in hs]
        deltas = [jnp.sum(probs[hd] * d_probs[hd], axis=-1, keepdims=True) for hd in hs]
        d_ss = [(probs[hd] * (d_probs[hd] - deltas[hd])).astype(BF16) for hd in hs]
        dqs = [_dot(d_ss[hd], kb) for hd in hs]
        dq_ref[...] = jnp.concatenate([_rope_bwd(dqs[hd] * SCALE, c, s1, s2).astype(BF16) for hd in hs], axis=1)
        dks = [_dot_tn(d_ss[hd], qs[hd]) for hd in hs]
        dvs = [_dot_tn(probs[hd].astype(BF16), das[hd]) for hd in hs]
        dk_ref[band, :] += functools.reduce(lambda a, b: a + b, dks)
        dv_ref[band, :] += functools.reduce(lambda a, b: a + b, dvs)
        dsink_ref[...] += functools.reduce(lambda a, b: a + b, [
            jnp.where(lane == hd, -jnp.sum(ess[hd] * invs[hd] * deltas[hd], axis=0, keepdims=True), 0.0) for hd in hs])

        centre = pl.ds(pl.multiple_of(i * BLOCK, BLOCK), BLOCK)
        gs = range(N_POOL)
        sl = [slice(g * POOL_GROUP, (g + 1) * POOL_GROUP) for g in gs]
        devs = [(_dot(pm_ref[0, g], pc_ref[band, sl[g]].astype(BF16)) - pc_ref[centre, sl[g]]).astype(BF16) for g in gs]
        dys = [dp_ref[:, sl[g]].astype(F32) for g in gs]
        zs = [_dot(devs[g], pw_ref[g]) for g in gs]
        dzs = [(dys[g] * ps_ref[:, sl[g]]).astype(BF16) for g in gs]
        d_devs = [_dot_nt(dzs[g], pw_ref[g]) for g in gs]
        dps_ref[...] += jnp.concatenate([jnp.sum(dys[g] * zs[g], axis=0, keepdims=True) for g in gs], axis=1)
        for g in gs:
            dpw_ref[g] += _dot_tn(devs[g], dzs[g])
        dpc_ref[band, :] += jnp.concatenate([_dot_tn(pm_ref[0, g], d_devs[g].astype(BF16)) for g in gs], axis=1)
        dpc_ref[centre, :] -= jnp.concatenate(d_devs, axis=1)

    return _call(
        body, name, (nb,),
        [pl.BlockSpec(memory_space=pltpu.SMEM), _rows(BLOCK, Q_PAD), _full((S, KV_WIDTH)), _full((S, KV_WIDTH)),
         _full((S, POOL_WIDTH)), _rows(BLOCK, Q_PAD), _rows(BLOCK, POOL_WIDTH), _pool_spec(nb),
         _full((N_POOL, POOL_GROUP, POOL_GROUP)), _full((1, POOL_WIDTH)),
         _rows(BLOCK, LANES), _rows(BLOCK, LANES), _rows(BLOCK, LANES)],
        [_rows(BLOCK, Q_PAD), _full((S, KV_WIDTH)), _full((S, KV_WIDTH)), _full((S, POOL_WIDTH)),
         _full((1, LANES)), _full((N_POOL, POOL_GROUP, POOL_GROUP)), _full((1, POOL_WIDTH))],
        [jax.ShapeDtypeStruct((S, Q_PAD), BF16), jax.ShapeDtypeStruct((S, KV_WIDTH), F32),
         jax.ShapeDtypeStruct((S, KV_WIDTH), F32), jax.ShapeDtypeStruct((S, POOL_WIDTH), F32),
         jax.ShapeDtypeStruct((1, LANES), F32), jax.ShapeDtypeStruct((N_POOL, POOL_GROUP, POOL_GROUP), F32),
         jax.ShapeDtypeStruct((1, POOL_WIDTH), F32)],
        [], 56, (sink, q, k, v, pc, da, dp, pool_m, pool_w, pool_scale, rc, rs1, rs2))


def _mix_out(h, a, p, wa, wp, name):
    S = h.shape[0]
    tile = min(TOK_TILE, S)

    def body(h_ref, a_ref, p_ref, wa_ref, wp_ref, o_ref):
        o_ref[...] = h_ref[...] + _dot(a_ref[...], wa_ref[...]) + _dot(p_ref[...], wp_ref[...])

    return pl.pallas_call(
        body, name=name, grid=(S // tile,),
        in_specs=[_rows(tile, D_MODEL), _rows(tile, Q_PAD), _rows(tile, POOL_WIDTH),
                  _full((Q_PAD, D_MODEL)), _full((POOL_WIDTH, D_MODEL))],
        out_specs=_rows(tile, D_MODEL),
        out_shape=jax.ShapeDtypeStruct((S, D_MODEL), F32),
        compiler_params=_params(("parallel",), 40),
    )(h, a, p, wa, wp)


def _mix_out_bwd(dh_out, dn, h, gain, a, p, wa, wp, name, rider=None):
    S = h.shape[0]
    tile = min(TOK_TILE, S)

    def body(do_ref, dn_ref, h_ref, g_ref, a_ref, p_ref, wa_ref, wp_ref, dh_ref, da_ref, dp_ref, dwa_ref, dwp_ref, dg_ref):
        @pl.when(pl.program_id(0) == 0)
        def _():
            dwa_ref[...] = jnp.zeros_like(dwa_ref)
            dwp_ref[...] = jnp.zeros_like(dwp_ref)
            dg_ref[...] = jnp.zeros_like(dg_ref)

        r, xh = _rms_stats(h_ref[...])
        dnorm, dg = _rms_bwd(_sum_chunks([dn_ref]), g_ref[...], r, xh)
        dh = do_ref[...] + dnorm
        dg_ref[...] += dg
        dh_ref[...] = dh
        dhb = dh.astype(BF16)
        da_ref[...] = _dot_nt(dhb, wa_ref[...]).astype(BF16)
        dp_ref[...] = _dot_nt(dhb, wp_ref[...]).astype(BF16)
        dwa_ref[...] += _dot_tn(a_ref[...], dhb)
        dwp_ref[...] += _dot_tn(p_ref[...], dhb)

    return _call(
        body, name, (S // tile,),
        [_rows(tile, D_MODEL), _chunk_rows(tile, dn.shape[0]), _rows(tile, D_MODEL), _full((1, D_MODEL)),
         _rows(tile, Q_PAD), _rows(tile, POOL_WIDTH), _full((Q_PAD, D_MODEL)), _full((POOL_WIDTH, D_MODEL))],
        [_rows(tile, D_MODEL), _rows(tile, Q_PAD), _rows(tile, POOL_WIDTH),
         _full((Q_PAD, D_MODEL)), _full((POOL_WIDTH, D_MODEL)), _full((1, D_MODEL))],
        [jax.ShapeDtypeStruct((S, D_MODEL), F32), jax.ShapeDtypeStruct((S, Q_PAD), BF16),
         jax.ShapeDtypeStruct((S, POOL_WIDTH), BF16), jax.ShapeDtypeStruct((Q_PAD, D_MODEL), F32),
         jax.ShapeDtypeStruct((POOL_WIDTH, D_MODEL), F32), jax.ShapeDtypeStruct((1, D_MODEL), F32)],
        [], 48, (dh_out, dn, h, gain, a, p, wa, wp), rider)


def _mix_in_bwd(dh_out, h, gain, n, dq, dk, dv, dpc, rc, rs1, rs2, w_in, name):
    S = h.shape[0]
    tile = min(TOK_TILE, S)

    def body(do_ref, h_ref, g_ref, n_ref, dq_ref, dk_ref, dv_ref, dpc_ref, c_ref, s1_ref, s2_ref, w_ref,
             dh_ref, dhalf_ref, dw_ref, dg_ref):
        @pl.when(pl.program_id(0) == 0)
        def _():
            dw_ref[...] = jnp.zeros_like(dw_ref)
            dg_ref[...] = jnp.zeros_like(dg_ref)

        dk = _rope_bwd(dk_ref[...], c_ref[...], s1_ref[...], s2_ref[...]).astype(BF16)
        du = jnp.concatenate([dq_ref[...], dk, dv_ref[...].astype(BF16), dpc_ref[...].astype(BF16)], axis=1)
        dn = _dot(du, w_ref[...])
        dw_ref[...] += _dot_tn(du, n_ref[...])
        r, xh = _rms_stats(h_ref[...])
        dnorm, dg = _rms_bwd(dn, g_ref[...], r, xh)
        dh = do_ref[...] + dnorm
        dg_ref[...] += dg
        dh_ref[...] = dh
        dhalf_ref[...] = (0.5 * dh).astype(BF16)

    return _call(
        body, name, (S // tile,),
        [_rows(tile, D_MODEL), _rows(tile, D_MODEL), _full((1, D_MODEL)), _rows(tile, D_MODEL),
         _rows(tile, Q_PAD), _rows(tile, KV_WIDTH), _rows(tile, KV_WIDTH), _rows(tile, POOL_WIDTH),
         _rows(tile, LANES), _rows(tile, LANES), _rows(tile, LANES), _full((U_PAD, D_MODEL))],
        [_rows(tile, D_MODEL), _rows(tile, D_MODEL), _full((U_PAD, D_MODEL)), _full((1, D_MODEL))],
        [jax.ShapeDtypeStruct((S, D_MODEL), F32), jax.ShapeDtypeStruct((S, D_MODEL), BF16),
         jax.ShapeDtypeStruct((U_PAD, D_MODEL), F32), jax.ShapeDtypeStruct((1, D_MODEL), F32)],
        [], 56, (dh_out, h, gain, n, dq, dk, dv, dpc, rc, rs1, rs2, w_in))


def _norm_bwd(dh_out, dns, h, gain, name):
    S = h.shape[0]
    tile = min(TOK_TILE, S)
    n = len(dns)

    def body(do_ref, *refs):
        h_ref, g_ref, dh_ref, dg_ref = refs[n:]

        @pl.when(pl.program_id(0) == 0)
        def _():
            dg_ref[...] = jnp.zeros_like(dg_ref)

        r, xh = _rms_stats(h_ref[...])
        dnorm, dg = _rms_bwd(_sum_chunks(refs[:n]), g_ref[...], r, xh)
        dg_ref[...] += dg
        dh_ref[...] = do_ref[...] + dnorm

    return _call(
        body, name, (S // tile,),
        [_rows(tile, D_MODEL)] + [_chunk_rows(tile, dn.shape[0]) for dn in dns] + [_rows(tile, D_MODEL), _full((1, D_MODEL))],
        [_rows(tile, D_MODEL), _full((1, D_MODEL))],
        [jax.ShapeDtypeStruct((S, D_MODEL), F32), jax.ShapeDtypeStruct((1, D_MODEL), F32)],
        [], 40, (dh_out, *dns, h, gain))


def _rope_tables(S):
    half = ROTARY_DIM // 2
    inv_freq = ROPE_THETA ** (-jnp.arange(0, ROTARY_DIM, 2, dtype=F32) / ROTARY_DIM)
    dim = jnp.arange(LANES) % HEAD_DIM
    ang = jnp.arange(S, dtype=F32)[:, None] * inv_freq[dim % half][None, :]
    lo, hi = (dim < half)[None, :], ((dim >= half) & (dim < ROTARY_DIM))[None, :]
    c = jnp.where(lo | hi, jnp.cos(ang), 1.0)
    s1 = jnp.where(lo, -jnp.sin(ang), 0.0)
    s2 = jnp.where(hi, jnp.sin(ang), 0.0)
    return c, s1, s2


def _pad_heads(w, axis):
    w = jnp.moveaxis(w, axis, 0)
    heads = w.reshape((N_HEADS, HEAD_DIM) + w.shape[1:])
    zero = jnp.zeros_like(heads)
    first = (jnp.arange(N_HEADS) < Q_PER_KV).reshape((N_HEADS, 1) + (1,) * (w.ndim - 1))
    lo = jnp.where(first, heads, zero)
    hi = jnp.where(first, zero, heads)
    padded = jnp.concatenate([lo, hi], axis=1).reshape((Q_PAD,) + w.shape[1:])
    return jnp.moveaxis(padded, 0, axis)


def _unpad_heads(w, axis):
    w = jnp.moveaxis(w, axis, 0)
    groups = w.reshape((N_HEADS, 2, HEAD_DIM) + w.shape[1:])
    first = (jnp.arange(N_HEADS) < Q_PER_KV).reshape((N_HEADS, 1) + (1,) * (w.ndim - 1))
    heads = jnp.where(first, groups[:, 0], groups[:, 1]).reshape((ATTN_WIDTH,) + w.shape[1:])
    return jnp.moveaxis(heads, 0, axis)


IN_ROWS = IN_WIDTH // N_CHIPS
OUT_ROWS = (ATTN_WIDTH + POOL_WIDTH) // N_CHIPS
MIX_ROWS = IN_ROWS + OUT_ROWS
FFN_ROWS = 3 * FF_CHUNK


def _step(x, target, bufs, small, place):
    S = x.shape[0]
    rc, rs1, rs2 = _rope_tables(S)
    mine = place[1]
    order = jnp.stack([mine, mine ^ 2, mine ^ 1, mine ^ 3])
    h1, n1, gate1, up1, ffn1, mix = _ffn_fwd(order, x, small["ffn1_norm"], "ffn1_fwd",
                                             _merge(_allgather(bufs[:1], in_passes=0), _allgather(bufs[1:2])), 0)
    in_flight, (ffn2,), token = _split_start(bufs[2:], 2, _gather_copies(X_Y_DIAGONAL[:2]), "ffn2_gather_start")
    w_in_t = mix[:, :IN_ROWS].reshape(IN_WIDTH, D_MODEL)
    w_in_pad = jnp.concatenate([_pad_heads(w_in_t[:ATTN_WIDTH], 0), w_in_t[ATTN_WIDTH:]], axis=0)
    w_out = mix[:, IN_ROWS:].reshape(ATTN_WIDTH + POOL_WIDTH, D_MODEL)
    wa = _pad_heads(w_out[:ATTN_WIDTH], 0)
    wp = w_out[ATTN_WIDTH:]
    pool_w = small["pool_w"].astype(BF16)

    n2, q, k, v, pc = _mix_in(h1, small["mix_norm"] + token[0, 0], w_in_pad, rc, rs1, rs2, "mix_in")
    pool_m = _pool_matrices(S)
    a, p = _mix_core_fwd(q, k, v, pc, small["sink_logits"], pool_m, pool_w, small["pool_scale"], "mix_core_fwd")
    h2 = _mix_out(h1, a, p, wa, wp, "mix_out")
    (ffn2,) = _split_wait(in_flight, [ffn2], _gather_copies(X_Y_DIAGONAL[:2]), h2, "ffn2_gather_wait")
    dh3, dhalf3, loss_lanes, d_final, n3, gate2, up2, ffn2 = _ffn_fwd(
        order, h2, small["ffn2_norm"], "ffn2_fwd",
        _allgather([ffn2], peers=X_Y_DIAGONAL[2:], in_passes=0, landed_before=X_Y_DIAGONAL[:2]), 0,
        loss_head=(target, small["final_norm"]))

    dn3, d_ffn2 = _ffn_bwd(jnp.arange(N_CHIPS, dtype=jnp.int32), dhalf3, n3, gate2, up2, ffn2, "ffn2_bwd")
    dh2, da, dp, dwa, dwp, d_ffn2_norm, received = _mix_out_bwd(dh3, dn3, h2, small["ffn2_norm"], a, p, wa, wp, "mix_out_bwd",
                                                                _sibling_exchange([d_ffn2]))
    pair = _pair_sum(place, d_ffn2, received, "grad_pair_sum_ffn2")
    in_flight, (pair, stack), token = _split_start([pair, received], 3, _scatter_copies, "ffn2_scatter_start")
    dq, dk, dv, dpc, dsink, dpool_w, dpool_scale = _mix_core_bwd(
        q, k, v, pc, da, dp, small["sink_logits"], pool_m, pool_w, small["pool_scale"] + token[0, 0], rc, rs1, rs2, "mix_core_bwd")
    dh1, dhalf1, dw_in_pad, d_mix_norm = _mix_in_bwd(dh2, h1, small["mix_norm"], n2, dq, dk, dv, dpc, rc, rs1, rs2, w_in_pad,
                                                     "mix_in_bwd")
    pair, stack = _split_wait(in_flight, [pair, stack], _scatter_copies, d_mix_norm, "ffn2_scatter_wait")
    reduced_ffn2 = _chip_sum(place, pair, stack, 2, "grad_chip_sum_ffn2")
    dw_in_t = jnp.concatenate([_unpad_heads(dw_in_pad[:Q_PAD], 0), dw_in_pad[Q_PAD:]], axis=0)
    dw_out = jnp.concatenate([_unpad_heads(dwa, 0), dwp], axis=0)
    d_mix = jnp.concatenate([dw_in_t.reshape(N_CHIPS, IN_ROWS, D_MODEL), dw_out.reshape(N_CHIPS, OUT_ROWS, D_MODEL)], axis=1)
    d_mix = jnp.transpose(d_mix.reshape(N_CHIPS, 2, MIX_ROWS // 2, D_MODEL), (1, 0, 2, 3)).astype(BF16)
    small_g = {"ffn1_norm": jnp.zeros_like(d_mix_norm), "mix_norm": d_mix_norm, "ffn2_norm": d_ffn2_norm,
               "final_norm": d_final, "pool_scale": dpool_scale, "sink_logits": dsink[:, :N_HEADS], "pool_w": dpool_w}
    loss_row = jnp.sum(loss_lanes.reshape(D_MODEL // LANES, LANES), axis=0, keepdims=True)
    small_early = _pack_small(small_g, loss_row)

    chunk = [(place[1:] + 1 + p) % N_CHIPS for p in range(N_CHIPS)]
    ffn1_bwd = functools.partial(_ffn_bwd, d_out=dhalf1, n=n1, gate=gate1, up=up1, group=ffn1)
    dn_a, dw_a, recv_mix, small_all, g_ffn2 = ffn1_bwd(
        chunk[0], name="ffn1_bwd_0",
        rider=_merge(_merge(_sibling_exchange([d_mix]), _small_allgather(small_early)), _sibling_share([reduced_ffn2])))
    pair_mix = _pair_sum(place, d_mix, recv_mix, "grad_pair_sum_mix")
    dn_b, dw_b, recv_a, stack_mix = ffn1_bwd(chunk[1], name="ffn1_bwd_1",
                                             rider=_merge(_sibling_exchange([dw_a]), _scatter([pair_mix])))
    pair_a = _pair_sum(place, dw_a, recv_a, "grad_pair_sum_ffn1_0")
    reduced_mix = _chip_sum(place, pair_mix, stack_mix, 2, "grad_chip_sum_mix")
    steps = [(_scatter_step_copies(p), (p, 3)) for p in range(3)]
    sems_a, (pair_a, stack), token = _split_start([pair_a, stack], 1, steps[0][0], "ffn1_scatter_start_0")
    dn_c, dw_c, recv_b, g_mix = ffn1_bwd(
        chunk[2], name="ffn1_bwd_2", rider=_merge(_merge(_sibling_exchange([dw_b]), _sibling_share([reduced_mix])), _after(token)))
    pair_b = _pair_sum(place, dw_b, recv_b, "grad_pair_sum_ffn1_1")
    sems_b, (pair_b, stack), token = _split_start([pair_b, stack], 1, steps[1][0], "ffn1_scatter_start_1")
    dn_d, dw_d, recv_c = ffn1_bwd(chunk[3], name="ffn1_bwd_3", rider=_merge(_sibling_exchange([dw_c]), _after(token)))
    pair_c = _pair_sum(place, dw_c, recv_c, "grad_pair_sum_ffn1_2")
    own = (_exchange_copies, (4, 5))
    sems_c, (pair_c, stack, dw_d, recv_d), token = _split_start(
        [pair_c, stack, dw_d, recv_c], 2, _joined([(steps[2][0], (0, 1)), (own[0], (2, 3))]), "ffn1_scatter_start_2")
    grad_x, d_ffn1_norm = _norm_bwd(dh1, [dn_a, dn_b, dn_c, dn_d], x, small["ffn1_norm"] + token[0, 0], "norm1_bwd")
    pair_a, pair_b, pair_c, stack, dw_d, recv_d = _split_wait(
        sems_a + sems_b + sems_c, [pair_a, pair_b, pair_c, stack, dw_d, recv_d], _joined(steps + [own]), d_ffn1_norm,
        "ffn1_scatter_wait")
    pair_d = _pair_sum(place, dw_d, recv_d, "grad_pair_sum_ffn1_3")
    reduced_ffn1 = _chip_sum(place, pair_d, stack, 2, "grad_chip_sum_ffn1")
    g_ffn1, gains = _comm_call(_merge(_sibling_share([reduced_ffn1]), _small_allgather(d_ffn1_norm.reshape(-1, LANES))),
                               "grad_share_tail")
    gain_sum = _sum_leading(gains, 1, "gain_grad_sum")
    small_sum = jnp.concatenate([gain_sum, _sum_leading(small_all, 1, "small_grad_sum")[gain_sum.shape[0]:]], axis=0)
    return jnp.sum(small_sum[SMALL_ROWS - 1]), grad_x, [g.reshape(-1, D_MODEL) for g in (g_ffn1, g_mix, g_ffn2)], small_sum


GROUPS =(("ffn1_w_gate", "ffn1_w_up", "ffn1_w_down"), ("w_in", "w_out"), ("ffn2_w_gate", "ffn2_w_up", "ffn2_w_down"))
TRANSPOSED = ("ffn1_w_gate", "ffn1_w_up", "w_in", "ffn2_w_gate", "ffn2_w_up")


def _place():
    x, y, c = lax.axis_index("x"), lax.axis_index("y"), lax.axis_index("c")
    chips = [(1 - x, y), (x, 1 - y), (1 - x, 1 - y)]
    return x, y, c, chips


def _remote(src, dst, send_sem, recv_sem, to):
    return pltpu.make_async_remote_copy(src_ref=src, dst_ref=dst, send_sem=send_sem, recv_sem=recv_sem,
                                        device_id=to, device_id_type=MESH)


def _pack(chip, members, name):
    rows = members[0].shape[0]
    n = len(members)

    def body(chip_ref, *refs):
        ins, out_ref, buf, sems = refs[:n], refs[n], refs[n + 1], refs[n + 2]
        copies = [pltpu.make_async_copy(ins[k], buf.at[k], sems.at[k]) for k in range(n)]
        for cp in copies:
            cp.start()
        for k in range(n):
            copies[k].wait()
            out_ref[0, k * rows:(k + 1) * rows, :] = buf[k].astype(BF16)

    return pl.pallas_call(
        body, name=name,
        grid_spec=pltpu.PrefetchScalarGridSpec(
            num_scalar_prefetch=1, grid=(1,),
            in_specs=[HBM_SPEC] * n,
            out_specs=pl.BlockSpec((1, n * rows, D_MODEL), lambda k, chip_ref: (chip_ref[0], 0, 0)),
            scratch_shapes=[pltpu.VMEM((n, rows, D_MODEL), F32), pltpu.SemaphoreType.DMA((n,))]),
        out_shape=jax.ShapeDtypeStruct((N_CHIPS, n * rows, D_MODEL), BF16),
        compiler_params=_params(("arbitrary",), 40),
    )(chip, *members)


def _same(arrays):
    return [jax.ShapeDtypeStruct(a.shape, a.dtype) for a in arrays]


X_Y_DIAGONAL = (0, 1, 2)


def _allgather(bufs, peers=X_Y_DIAGONAL, in_passes=None, landed_before=()):
    n = len(bufs)

    def copy(kind, outs, send_sems, recv_sems, a, k):
        x, y, c, chips = _place()
        half = bufs[a].shape[1] // 2

        def rows(slot, core):
            return outs[a].at[slot, pl.ds(pl.multiple_of(core * half, 16), half)]

        me, slot = 2 * x + y, 2 * chips[k][0] + chips[k][1]
        over_ici = (send_sems.at[6 * a + k], recv_sems.at[6 * a + k])
        over_d2d = (send_sems.at[6 * a + 3 + k], recv_sems.at[6 * a + 3 + k])
        if kind == "first":
            return _remote(rows(me, c), rows(me, c), *over_ici, (*chips[k], c))
        if kind == "landed":
            return _remote(rows(me, c), rows(slot, c), *over_ici, (*chips[k], c))
        if kind == "passed":
            return _remote(rows(slot, c), rows(slot, c), *over_d2d, (x, y, 1 - c))
        return _remote(rows(me, c), rows(slot, 1 - c), *over_d2d, (x, y, 1 - c))

    def arrive(pairs):
        def hook(ins, outs, sems):
            for a, k in pairs:
                if k not in landed_before:
                    copy("landed", outs, *sems, a, k).wait_recv()
                    copy("passed", outs, *sems, a, k).start()
            for a, k in pairs:
                copy("handed", outs, *sems, a, k).wait_recv()
        return hook

    sent = [(a, k) for a in range(n) for k in peers]
    everything = sent + [(in_passes, k) for k in landed_before]
    early = [(a, k) for a, k in everything if a == in_passes]

    def start(ins, outs, sems):
        for k in landed_before:
            copy("passed", outs, *sems, in_passes, k).start()
        for a, k in sent:
            copy("first", outs, *sems, a, k).start()

    def finish(ins, outs, sems):
        arrive([pair for pair in everything if pair not in early])(ins, outs, sems)
        for a, k in sent:
            copy("first", outs, *sems, a, k).wait_send()
        for a, k in everything:
            copy("passed", outs, *sems, a, k).wait_send()

    hooks = tuple(((k + 1, 0), arrive([(a, k)])) for a, k in early)
    return _Rider(list(bufs), _same(bufs), {a: a for a in range(n)},
                  [pltpu.SemaphoreType.DMA((6 * n,)), pltpu.SemaphoreType.DMA((6 * n,))], start, finish, hooks)


SEM_SPEC = pl.BlockSpec(memory_space=pltpu.SEMAPHORE)


def _split_start(arrays, n_copies, make_copies, name):
    m, n = len(arrays), 2 * n_copies

    def body(*refs):
        sems, thru, token = refs[m:m + n], refs[m + n:2 * m + n], refs[2 * m + n]
        for cp in make_copies(thru, sems, False):
            cp.start()
        token[...] = jnp.zeros_like(token)

    outs = pl.pallas_call(
        body, name=name,
        out_shape=[pltpu.SemaphoreType.DMA(())] * n + _same(arrays) + [jax.ShapeDtypeStruct((SUBLANES, LANES), F32)],
        in_specs=[HBM_SPEC] * m, out_specs=[SEM_SPEC] * n + [HBM_SPEC] * m + [pl.BlockSpec(memory_space=pltpu.VMEM)],
        input_output_aliases={a: n + a for a in range(m)},
        compiler_params=pltpu.CompilerParams(has_side_effects=pltpu.SideEffectType.DATAFLOW_SIDE_EFFECTING),
    )(*arrays)
    return list(outs[:n]), list(outs[n:n + m]), outs[n + m]


def _split_wait(sems, arrays, make_copies, after, name):
    m, n = len(arrays), len(sems)

    def body(*refs):
        for cp in make_copies(refs[m + n + 1:], refs[m:m + n], True):
            cp.wait_send()
            cp.wait_recv()

    return pl.pallas_call(
        body, name=name, out_shape=_same(arrays),
        in_specs=[HBM_SPEC] * m + [SEM_SPEC] * n + [pl.BlockSpec(memory_space=pl.ANY)], out_specs=[HBM_SPEC] * m,
        input_output_aliases={a: a for a in range(m)},
        compiler_params=pltpu.CompilerParams(has_side_effects=pltpu.SideEffectType.DATAFLOW_SIDE_EFFECTING),
    )(*arrays, *sems, after)


def _gather_copies(peers):
    def make(refs, sems, landing):
        x, y, c, chips = _place()
        half = refs[0].shape[1] // 2

        def rows(slot):
            return refs[0].at[slot, pl.ds(pl.multiple_of(c * half, 16), half)]

        me = 2 * x + y
        return [_remote(rows(me), rows(2 * chips[k][0] + chips[k][1] if landing else me), sems[2 * j], sems[2 * j + 1],
                        (*chips[k], c)) for j, k in enumerate(peers)]
    return make


def _scatter_copies(refs, sems, landing):
    x, y, c, chips = _place()
    me = 2 * x + y
    slots = [2 * cx + cy for cx, cy in chips]
    return [_remote(refs[0].at[slots[k]], refs[1].at[slots[k] if landing else me], sems[2 * k], sems[2 * k + 1], (*chips[k], c))
            for k in X_Y_DIAGONAL]


def _scatter_step_copies(step):
    def make(refs, sems, landing):
        x, y, c, _ = _place()
        me = 2 * x + y
        to = (me + 1 + step) % N_CHIPS
        frm = (me + N_CHIPS - 1 - step) % N_CHIPS
        peer = frm if landing else to
        return [_remote(refs[0].at[0], refs[1].at[frm if landing else me], sems[0], sems[1], (peer // 2, peer % 2, c))]
    return make


def _exchange_copies(refs, sems, landing):
    x, y, c, _ = _place()
    return [_remote(refs[0].at[1 - c], refs[1], sems[0], sems[1], (x, y, 1 - c))]


def _joined(makers):
    def make(refs, sems, landing):
        copies, at = [], 0
        for maker, places in makers:
            mine = maker([refs[p] for p in places], sems[at:], landing)
            copies += mine
            at += 2 * len(mine)
        return copies
    return make


def _sibling_exchange(parts):
    n = len(parts)

    def copies(ins, outs, send_sems, recv_sems):
        x, y, c, _ = _place()
        return [_remote(ins[a].at[1 - c], outs[a], send_sems.at[a], recv_sems.at[a], (x, y, 1 - c)) for a in range(n)]

    def start(ins, outs, sems):
        for cp in copies(ins, outs, *sems):
            cp.start()

    def finish(ins, outs, sems):
        for cp in copies(ins, outs, *sems):
            cp.wait_recv()
            cp.wait_send()

    return _Rider(list(parts), [jax.ShapeDtypeStruct(p.shape[1:], p.dtype) for p in parts], {},
                  [pltpu.SemaphoreType.DMA((n,)), pltpu.SemaphoreType.DMA((n,))], start, finish)


def _small_allgather(small):
    flips = [(fx, fy, fc) for fx in range(2) for fy in range(2) for fc in range(2)][1:]

    def copies(small_ref, gather_ref, send_sems, recv_sems, local_sem, started_only=False):
        x, y, c, _ = _place()
        me = 4 * x + 2 * y + c
        peers = [((1 - x) if fx else x, (1 - y) if fy else y, (1 - c) if fc else c) for fx, fy, fc in flips]
        own = pltpu.make_async_copy(small_ref, gather_ref.at[me], local_sem)
        sent = [_remote(small_ref, gather_ref.at[me], send_sems.at[k], recv_sems.at[k], peer) for k, peer in enumerate(peers)]
        if started_only:
            return own, sent
        landed = [_remote(small_ref, gather_ref.at[4 * px + 2 * py + pc], send_sems.at[k], recv_sems.at[k], (px, py, pc))
                  for k, (px, py, pc) in enumerate(peers)]
        return own, sent, landed

    def start(ins, outs, sems):
        own, sent = copies(ins[0], outs[0], *sems, started_only=True)
        own.start()
        for cp in sent:
            cp.start()

    def finish(ins, outs, sems):
        own, sent, landed = copies(ins[0], outs[0], *sems)
        for cp in landed:
            cp.wait_recv()
        for cp in sent:
            cp.wait_send()
        own.wait()

    return _Rider([small], [jax.ShapeDtypeStruct((2 * N_CHIPS,) + small.shape, small.dtype)], {},
                  [pltpu.SemaphoreType.DMA((7,)), pltpu.SemaphoreType.DMA((7,)), pltpu.SemaphoreType.DMA], start, finish)


def _merge(a, b):
    na, nao, nas = len(a.operands), len(a.out_shapes), len(a.scratch)

    def start(ins, outs, sems):
        a.start(ins[:na], outs[:nao], sems[:nas])
        b.start(ins[na:], outs[nao:], sems[nas:])

    def finish(ins, outs, sems):
        a.finish(ins[:na], outs[:nao], sems[:nas])
        b.finish(ins[na:], outs[nao:], sems[nas:])

    def of_a(fn):
        return lambda ins, outs, sems: fn(ins[:na], outs[:nao], sems[:nas])

    def of_b(fn):
        return lambda ins, outs, sems: fn(ins[na:], outs[nao:], sems[nas:])

    aliases = {**a.aliases, **{na + k: nao + v for k, v in b.aliases.items()}}
    hooks = tuple((at, of_a(fn)) for at, fn in a.hooks) + tuple((at, of_b(fn)) for at, fn in b.hooks)
    return _Rider(a.operands + b.operands, a.out_shapes + b.out_shapes, aliases, a.scratch + b.scratch, start, finish, hooks)


def _scatter(sums):
    n = len(sums)

    def copies(ins, outs, send_sems, recv_sems, started_only=False):
        x, y, c, chips = _place()
        me = 2 * x + y
        slots = [2 * cx + cy for cx, cy in chips]
        sent = [_remote(ins[a].at[slots[k]], outs[a].at[me], send_sems.at[3 * a + k], recv_sems.at[3 * a + k], (*chips[k], c))
                for a in range(n) for k in X_Y_DIAGONAL]
        if started_only:
            return sent
        landed = [_remote(ins[a].at[slots[k]], outs[a].at[slots[k]], send_sems.at[3 * a + k], recv_sems.at[3 * a + k],
                          (*chips[k], c)) for a in range(n) for k in X_Y_DIAGONAL]
        return sent, landed

    def start(ins, outs, sems):
        for cp in copies(ins, outs, *sems, started_only=True):
            cp.start()

    def finish(ins, outs, sems):
        sent, landed = copies(ins, outs, *sems)
        for cp in landed:
            cp.wait_recv()
        for cp in sent:
            cp.wait_send()

    return _Rider(list(sums), _same(sums), {}, [pltpu.SemaphoreType.DMA((3 * n,)), pltpu.SemaphoreType.DMA((3 * n,))],
                  start, finish)


def _sibling_share(bufs):
    n = len(bufs)

    def copies(outs, send_sems, recv_sems, started_only=False):
        x, y, c, _ = _place()
        sent = [_remote(outs[a].at[c], outs[a].at[c], send_sems.at[a], recv_sems.at[a], (x, y, 1 - c)) for a in range(n)]
        if started_only:
            return sent
        landed = [_remote(outs[a].at[c], outs[a].at[1 - c], send_sems.at[a], recv_sems.at[a], (x, y, 1 - c)) for a in range(n)]
        return sent, landed

    def start(ins, outs, sems):
        for cp in copies(outs, *sems, started_only=True):
            cp.start()

    def finish(ins, outs, sems):
        sent, landed = copies(outs, *sems)
        for cp in landed:
            cp.wait_recv()
        for cp in sent:
            cp.wait_send()

    return _Rider(list(bufs), _same(bufs), {a: a for a in range(n)},
                  [pltpu.SemaphoreType.DMA((n,)), pltpu.SemaphoreType.DMA((n,))], start, finish)


def _pair_sum(core, part, received, name):
    _, k, rh, cols = part.shape

    def body(core_ref, p_ref, r_ref, o_ref):
        o_ref[...] = (p_ref[0].astype(F32) + r_ref[...].astype(F32)).astype(BF16)

    return pl.pallas_call(
        body, name=name,
        grid_spec=pltpu.PrefetchScalarGridSpec(
            num_scalar_prefetch=1, grid=(k,),
            in_specs=[pl.BlockSpec((1, 1, rh, cols), lambda j, core_ref: (core_ref[0], j, 0, 0)),
                      pl.BlockSpec((1, rh, cols), lambda j, core_ref: (j, 0, 0))],
            out_specs=pl.BlockSpec((1, rh, cols), lambda j, core_ref: (j, 0, 0))),
        out_shape=jax.ShapeDtypeStruct((k, rh, cols), BF16),
        compiler_params=_params(("parallel",), 60),
    )(core, part, received)


def _sum_leading(stack, steps, name):
    k, rows, cols = stack.shape
    tile = rows // steps

    def body(s_ref, o_ref):
        total = s_ref[0].astype(F32)
        for d in range(1, k):
            total = total + s_ref[d].astype(F32)
        o_ref[...] = total

    return pl.pallas_call(
        body, name=name, grid=(steps,),
        in_specs=[pl.BlockSpec((k, tile, cols), lambda i: (0, i, 0))],
        out_specs=pl.BlockSpec((tile, cols), lambda i: (i, 0)),
        out_shape=jax.ShapeDtypeStruct((rows, cols), F32),
        compiler_params=_params(("parallel",), 32),
    )(stack)


def _chip_sum(place, own, stack, steps, name):
    k, rows, cols = stack.shape
    tile = rows // steps

    def body(place_ref, own_ref, *refs):
        chip = place_ref[1]
        total = None
        for d in range(k):
            term = jnp.where(chip == d, own_ref[0], refs[d][0]).astype(F32)
            total = term if total is None else total + term
        refs[k][0] = total

    def other(d):
        return lambda i, place_ref: (jnp.where(place_ref[1] == d, (d + 1) % k, d), i, 0)

    return pl.pallas_call(
        body, name=name,
        grid_spec=pltpu.PrefetchScalarGridSpec(
            num_scalar_prefetch=1, grid=(steps,),
            in_specs=[pl.BlockSpec((1, tile, cols), lambda i, place_ref: (place_ref[1] % own.shape[0], i, 0))]
            + [pl.BlockSpec((1, tile, cols), other(d)) for d in range(k)],
            out_specs=pl.BlockSpec((1, tile, cols), lambda i, place_ref: (place_ref[0], i, 0))),
        out_shape=jax.ShapeDtypeStruct((2, rows, cols), F32),
        compiler_params=_params(("arbitrary",), 60),
    )(place, own, *([stack] * k))


def _adamw(w, g, row0, m, v, tile, name):
    rows, cols = w.shape
    first = row0 // tile
    assert rows % tile == 0 and row0 % tile == 0
    bc1 = 1.0 - ADAM_B1 ** ADAM_STEP
    bc2 = 1.0 - ADAM_B2 ** ADAM_STEP

    def body(w_ref, g_ref, m_ref, v_ref, go_ref, d_ref, mo_ref, vo_ref):
        g = g_ref[...]
        m_new = ADAM_B1 * m_ref[...] + (1.0 - ADAM_B1) * g
        v_new = ADAM_B2 * v_ref[...] + (1.0 - ADAM_B2) * (g * g)
        go_ref[...] = g
        d_ref[...] = -ADAM_LR * ((m_new / bc1) / (jnp.sqrt(v_new / bc2) + ADAM_EPS) + ADAM_WD * w_ref[...])
        mo_ref[...] = m_new
        vo_ref[...] = v_new

    spec = pl.BlockSpec((tile, cols), lambda i: (i, 0))
    g_spec = pl.BlockSpec((tile, cols), lambda i: (first + i, 0))
    return pl.pallas_call(
        body, name=name, grid=(rows // tile,),
        in_specs=[spec, g_spec, spec, spec], out_specs=[spec] * 4,
        out_shape=[jax.ShapeDtypeStruct((rows, cols), F32)] * 4,
        compiler_params=_params(("parallel",), 32),
    )(w, g, m, v)


SMALL = ("ffn1_norm", "mix_norm", "ffn2_norm", "final_norm", "pool_scale", "sink_logits", "pool_w")


def _pack_small(d, last_row=None):
    def part(n):
        flat = d[n].reshape(-1)
        flat = jnp.pad(flat, (0, -flat.shape[0] % (SUBLANES * LANES)))
        return flat.reshape(-1, LANES)

    last = jnp.zeros((SUBLANES, LANES), F32) if last_row is None else jnp.pad(last_row, ((SUBLANES - 1, 0), (0, 0)))
    packed = jnp.concatenate([part(n) for n in SMALL] + [last], axis=0)
    assert packed.shape[0] == SMALL_ROWS
    return packed


def _unpack_small(packed, like):
    out, row = {}, 0
    for n in SMALL:
        size = math.prod(like[n].shape)
        rows = -(-size // (SUBLANES * LANES)) * SUBLANES
        out[n] = packed[row:row + rows].reshape(-1)[:size].reshape(like[n].shape)
        row += rows
    return out


def kernel(x, ffn1_norm, ffn1_w_gate, ffn1_w_up, ffn1_w_down, mix_norm, w_in, sink_logits, pool_w, pool_scale, w_out, ffn2_norm, ffn2_w_gate, ffn2_w_up, ffn2_w_down, final_norm, loss_target, m_ffn1_norm, m_ffn1_w_gate, m_ffn1_w_up, m_ffn1_w_down, m_mix_norm, m_w_in, m_sink_logits, m_pool_w, m_pool_scale, m_w_out, m_ffn2_norm, m_ffn2_w_gate, m_ffn2_w_up, m_ffn2_w_down, m_final_norm, v_ffn1_norm, v_ffn1_w_gate, v_ffn1_w_up, v_ffn1_w_down, v_mix_norm, v_w_in, v_sink_logits, v_pool_w, v_pool_scale, v_w_out, v_ffn2_norm, v_ffn2_w_gate, v_ffn2_w_up, v_ffn2_w_down, v_final_norm):
    names = ("ffn1_norm", "ffn1_w_gate", "ffn1_w_up", "ffn1_w_down", "mix_norm", "w_in", "sink_logits", "pool_w",
             "pool_scale", "w_out", "ffn2_norm", "ffn2_w_gate", "ffn2_w_up", "ffn2_w_down", "final_norm")
    weights = dict(zip(names, (ffn1_norm, ffn1_w_gate, ffn1_w_up, ffn1_w_down, mix_norm, w_in, sink_logits, pool_w,
                               pool_scale, w_out, ffn2_norm, ffn2_w_gate, ffn2_w_up, ffn2_w_down, final_norm)))
    mom1 = dict(zip(names, (m_ffn1_norm, m_ffn1_w_gate, m_ffn1_w_up, m_ffn1_w_down, m_mix_norm, m_w_in, m_sink_logits,
                            m_pool_w, m_pool_scale, m_w_out, m_ffn2_norm, m_ffn2_w_gate, m_ffn2_w_up, m_ffn2_w_down,
                            m_final_norm)))
    mom2 = dict(zip(names, (v_ffn1_norm, v_ffn1_w_gate, v_ffn1_w_up, v_ffn1_w_down, v_mix_norm, v_w_in, v_sink_logits,
                            v_pool_w, v_pool_scale, v_w_out, v_ffn2_norm, v_ffn2_w_gate, v_ffn2_w_up, v_ffn2_w_down,
                            v_final_norm)))
    chip = (2 * lax.axis_index("x") + lax.axis_index("y")).astype(jnp.int32).reshape(1)
    place = jnp.concatenate([lax.axis_index("c").astype(jnp.int32).reshape(1), chip])

    def rows_of(t, n):
        return jnp.swapaxes(t[n][0], 0, 1) if n in TRANSPOSED else t[n][0]

    bufs = [_pack(chip, [rows_of(weights, n) for n in GROUPS[0]], "pack_ffn1"),
            _pack(chip, [jnp.concatenate([rows_of(weights, n) for n in GROUPS[1]], axis=0)], "pack_mix"),
            _pack(chip, [rows_of(weights, n) for n in GROUPS[2]], "pack_ffn2")]

    small_w = {"ffn1_norm": ffn1_norm, "mix_norm": mix_norm, "ffn2_norm": ffn2_norm,
               "final_norm": final_norm.reshape(1, D_MODEL), "pool_scale": pool_scale, "sink_logits": sink_logits,
               "pool_w": pool_w[0]}
    loss, grad_x, group_grads, small_sum = _step(x[0], loss_target[0], bufs, small_w, place)

    out_g, out_d, out_m, out_v = {}, {}, {}, {}
    for members, g in zip(GROUPS, group_grads):
        row0 = 0
        for n in members:
            w = rows_of(weights, n)
            tile = FF_CHUNK // 4 if w.shape[0] == FF_CHUNK else math.gcd(IN_ROWS, OUT_ROWS)
            outs = _adamw(w, g, row0, rows_of(mom1, n), rows_of(mom2, n), tile, "adamw_" + n)
            row0 += w.shape[0]
            for dst, t in zip((out_g, out_d, out_m, out_v), outs):
                dst[n] = (jnp.swapaxes(t, 0, 1) if n in TRANSPOSED else t).reshape(weights[n].shape)
    small_outs = _adamw(_pack_small(weights), small_sum, 0, _pack_small(mom1), _pack_small(mom2), SMALL_ROWS, "adamw_small")
    for dst, packed in zip((out_g, out_d, out_m, out_v), small_outs):
        dst.update(_unpack_small(packed, weights))

    return (loss,grad_x.reshape(x.shape), *[out_g[n] for n in names], *[out_d[n] for n in names],
            *[out_m[n] for n in names], *[out_v[n] for n in names])
```

```python
import collections
import functools
import math

import jax
import jax.numpy as jnp
from jax import lax
from jax.experimental import pallas as pl
from jax.experimental.pallas import tpu as pltpu

F32, BF16 = jnp.float32, jnp.bfloat16
MESH = pl.DeviceIdType.MESH

D_MODEL = 1024
D_FF = 2816
N_CHIPS = 4
FF_CHUNK = D_FF // N_CHIPS
HEAD_DIM = 64
N_HEADS = 8
N_KV = 2
Q_PER_KV = N_HEADS // N_KV
KV_WIDTH = N_KV * HEAD_DIM
ATTN_WIDTH = N_HEADS * HEAD_DIM
POOL_WINDOWS = (2, 4, 8, 16)
N_POOL = len(POOL_WINDOWS)
POOL_GROUP = 128
POOL_WIDTH = N_POOL * POOL_GROUP
IN_WIDTH = ATTN_WIDTH + 2 * KV_WIDTH + POOL_WIDTH
WINDOW = 128
BLOCK = 128
BAND = 3 * BLOCK
ROPE_THETA = 500000.0
ROTARY_DIM = HEAD_DIM // 4
EPS = 1e-6
LANES = 128
Q_PAD = N_HEADS * LANES
U_PAD = Q_PAD + 2 * KV_WIDTH + POOL_WIDTH
SCALE = HEAD_DIM ** -0.5
NEG = -1e30

ADAM_LR, ADAM_B1, ADAM_B2, ADAM_EPS, ADAM_WD, ADAM_STEP = 0.001, 0.9, 0.999, 1e-08, 0.01, 10

V7X_VMEM_BYTES = 64 * 1024 * 1024
TOK_TILE = 512
SUBLANES = 8
SMALL_ROWS = 568


def _params(sem, vmem_mb):
    assert vmem_mb * 1024 * 1024 <= V7X_VMEM_BYTES
    return pltpu.CompilerParams(dimension_semantics=sem, vmem_limit_bytes=vmem_mb * 1024 * 1024)


def _dot(a, b):
    return lax.dot_general(a, b, (((1,), (0,)), ((), ())), preferred_element_type=F32)


def _dot_nt(a, b):
    return lax.dot_general(a, b, (((1,), (1,)), ((), ())), preferred_element_type=F32)


def _dot_tn(a, b):
    return lax.dot_general(a, b, (((0,), (0,)), ((), ())), preferred_element_type=F32)


def _rms_stats(h):
    r = lax.rsqrt(jnp.mean(h * h, axis=-1, keepdims=True) + EPS)
    return r, h * r


def _rms_bwd(dn, g, r, xh):
    gd = dn * g
    dh = r * (gd - xh * jnp.mean(gd * xh, axis=-1, keepdims=True))
    return dh, jnp.sum(dn * xh, axis=0, keepdims=True)


def _rope(x, c, s1, s2):
    return x * c + pltpu.roll(x, LANES - ROTARY_DIM // 2, 1) * s1 + pltpu.roll(x, ROTARY_DIM // 2, 1) * s2


def _rope_bwd(d, c, s1, s2):
    return d * c + pltpu.roll(d * s1, ROTARY_DIM // 2, 1) + pltpu.roll(d * s2, LANES - ROTARY_DIM // 2, 1)


def _sum_chunks(refs):
    terms = [ref[j].astype(F32) for ref in refs for j in range(ref.shape[0])]
    return functools.reduce(lambda a, b: a + b, terms)


def _chunk_rows(tile, k):
    return pl.BlockSpec((k, tile, D_MODEL), lambda i, *_: (0, i, 0))


def _full(shape):
    nd = len(shape)
    return pl.BlockSpec(shape, lambda *_: (0,) * nd)


def _rows(tile, cols):
    return pl.BlockSpec((tile, cols), lambda i, *_: (i, 0))


HBM_SPEC = pl.BlockSpec(memory_space=pltpu.HBM)

_Rider = collections.namedtuple("_Rider", "operands out_shapes aliases scratch start finish hooks", defaults=[()])


_NO_RIDER = _Rider([], [], {}, [], None, None)


def _call(body, name, grid, in_specs, out_specs, out_shape, scratch, vmem_mb, args, rider=None, prefetch=(),
          shares_rider_refs=False):
    rider = rider or _NO_RIDER
    n_pre, n_in, n_out, n_scr = len(prefetch), len(in_specs), len(out_specs), len(scratch)
    r_in, r_out = len(rider.operands), len(rider.out_shapes)

    def fused(*refs):
        pre, refs = refs[:n_pre], refs[n_pre:]
        ins, refs = refs[:n_in], refs[n_in:]
        r_ins, refs = refs[:r_in], refs[r_in:]
        outs, refs = refs[:n_out], refs[n_out:]
        r_outs, refs = refs[:r_out], refs[r_out:]
        scr, r_scr = refs[:n_scr], refs[n_scr:]
        ids = [pl.program_id(d) for d in range(len(grid))]
        if rider.start is not None:
            @pl.when(functools.reduce(jnp.logical_and, [i == 0 for i in ids]))
            def _():
                rider.start(r_ins, r_outs, r_scr)

        for at, hook in rider.hooks:
            @pl.when(functools.reduce(jnp.logical_and, [i == a for i, a in zip(ids, at)]))
            def _(hook=hook):
                hook(r_ins, r_outs, r_scr)

        if shares_rider_refs:
            body(*pre, *ins, *outs, *scr, rider_refs=r_outs)
        else:
            body(*pre, *ins, *outs, *scr)

        if rider.finish is not None:
            @pl.when(functools.reduce(jnp.logical_and, [i == g - 1 for i, g in zip(ids, grid)]))
            def _():
                rider.finish(r_ins, r_outs, r_scr)

    return pl.pallas_call(
        fused, name=name,
        grid_spec=pltpu.PrefetchScalarGridSpec(
            num_scalar_prefetch=n_pre, grid=grid,
            in_specs=list(in_specs) + [HBM_SPEC] * r_in, out_specs=list(out_specs) + [HBM_SPEC] * r_out,
            scratch_shapes=list(scratch) + list(rider.scratch)),
        out_shape=list(out_shape) + list(rider.out_shapes),
        input_output_aliases={n_pre + n_in + k: n_out + v for k, v in rider.aliases.items()},
        compiler_params=_params(("arbitrary",) * len(grid), vmem_mb),
    )(*prefetch, *args, *rider.operands)


def _after(token):
    return _Rider([token], [], {}, [], lambda *_: None, lambda *_: None)


def _comm_call(rider, name):
    r_in, r_out = len(rider.operands), len(rider.out_shapes)

    def body(*refs):
        r_ins, r_outs, r_scr = refs[:r_in], refs[r_in:r_in + r_out], refs[r_in + r_out:]
        rider.start(r_ins, r_outs, r_scr)
        rider.finish(r_ins, r_outs, r_scr)

    return pl.pallas_call(
        body, name=name, in_specs=[HBM_SPEC] * r_in, out_specs=[HBM_SPEC] * r_out, out_shape=list(rider.out_shapes),
        input_output_aliases=dict(rider.aliases), scratch_shapes=list(rider.scratch),
    )(*rider.operands)


def _ffn_fwd(order, h, gain, name, rider, group_at, loss_head=None):
    S = h.shape[0]
    tile = min(TOK_TILE, S)
    nt = S // tile
    last = N_CHIPS - 1
    n_extra_in, n_head_out = (2, 4) if loss_head else (0, 1)

    def body(order_ref, h_ref, g_ref, *refs, rider_refs):
        extra_in, refs = refs[:n_extra_in], refs[n_extra_in:]
        head_out, (n_ref, gate_ref, up_ref, w_scr, w_sem, acc, n_scr) = refs[:n_head_out], refs[n_head_out:]
        j, i = pl.program_id(0), pl.program_id(1)

        @pl.when(i == 0)
        def _():
            fetch = pltpu.make_async_copy(rider_refs[group_at].at[order_ref[j]], w_scr, w_sem)
            fetch.start()
            fetch.wait()

        at = pl.multiple_of(i * tile, tile)

        @pl.when(j == 0)
        def _():
            _, xh = _rms_stats(h_ref[...])
            n = (xh * g_ref[...]).astype(BF16)
            n_scr[pl.ds(at, tile), :] = n
            n_ref[...] = n
            acc[pl.ds(at, tile), :] = jnp.zeros((tile, D_MODEL), F32)

        half = tile // 2
        wg, wu, wd = (w_scr[part * FF_CHUNK:(part + 1) * FF_CHUNK, :] for part in range(3))
        ns = [n_scr[pl.ds(at + s * half, half), :] for s in range(2)]
        gates = [_dot_nt(n, wg) for n in ns]
        ups = [_dot_nt(n, wu) for n in ns]
        acts = [(g * jax.nn.sigmoid(g) * u).astype(BF16) for g, u in zip(gates, ups)]
        for s in range(2):
            gate_ref[0, s * half:(s + 1) * half, :] = gates[s].astype(BF16)
            up_ref[0, s * half:(s + 1) * half, :] = ups[s].astype(BF16)
            acc[pl.ds(at + s * half, half), :] += _dot(acts[s], wd)

        @pl.when(j == last)
        def _():
            out = h_ref[...] + 0.5 * acc[pl.ds(at, tile), :]
            if not loss_head:
                head_out[0][...] = out
                return
            (t_ref, gf_ref), (dh_ref, dhalf_ref, loss_ref, dg_ref) = extra_in, head_out

            @pl.when(i == 0)
            def _():
                loss_ref[...] = jnp.zeros_like(loss_ref)
                dg_ref[...] = jnp.zeros_like(dg_ref)

            gf = gf_ref[...]
            r, xh = _rms_stats(out)
            err = xh * gf - t_ref[...]
            loss_ref[...] += (0.5 / D_MODEL) * jnp.sum(err * err, axis=0, keepdims=True)
            dh, dg = _rms_bwd(err * (1.0 / D_MODEL), gf, r, xh)
            dg_ref[...] += dg
            dh_ref[...] = dh
            dhalf_ref[...] = (0.5 * dh).astype(BF16)

    tok = pl.BlockSpec((tile, D_MODEL), lambda j, i, order_ref: (i, 0))
    hid = pl.BlockSpec((1, tile, FF_CHUNK), lambda j, i, order_ref: (order_ref[j], i, 0))
    row = pl.BlockSpec((1, D_MODEL), lambda j, i, order_ref: (0, 0))
    in_last = pl.BlockSpec((tile, D_MODEL), lambda j, i, order_ref: (jnp.where(j == last, i, 0), 0))
    in_first = pl.BlockSpec((tile, D_MODEL), lambda j, i, order_ref: (jnp.where(j == 0, i, nt - 1), 0))
    tok_f32, tok_bf16, lanes = (jax.ShapeDtypeStruct((S, D_MODEL), F32), jax.ShapeDtypeStruct((S, D_MODEL), BF16),
                                jax.ShapeDtypeStruct((1, D_MODEL), F32))
    hidden = jax.ShapeDtypeStruct((N_CHIPS, S, FF_CHUNK), BF16)
    if loss_head:
        extra_specs, extra_args = [in_last, row], list(loss_head)
        head_specs, head_shapes = [in_last, in_last, row, row], [tok_f32, tok_bf16, lanes, lanes]
    else:
        extra_specs, extra_args, head_specs, head_shapes = [], [], [in_last], [tok_f32]
    return _call(
        body, name, (N_CHIPS, nt), [tok, row] + extra_specs, head_specs + [in_first, hid, hid],
        head_shapes + [tok_bf16, hidden, hidden],
        [pltpu.VMEM((3 * FF_CHUNK, D_MODEL), BF16), pltpu.SemaphoreType.DMA, pltpu.VMEM((S, D_MODEL), F32),
         pltpu.VMEM((S, D_MODEL), BF16)], 58, (h, gain, *extra_args), rider, (order,), shares_rider_refs=True)


def _ffn_bwd(chunks, d_out, n, gate, up, group, name, rider=None):
    S = n.shape[0]
    n_chunks = chunks.shape[0]
    tile = min(TOK_TILE, S)
    nt = S // tile
    half_rows = 3 * FF_CHUNK // 2
    cut = FF_CHUNK // 2

    def body(chunks_ref, do_ref, n_ref, gate_ref, up_ref, wg_ref, wu_ref, wd_ref, dn_ref, dw_ref, acc_g, acc_u, acc_d):
        j, i = pl.program_id(0), pl.program_id(1)

        @pl.when(i == 0)
        def _():
            acc_g[...] = jnp.zeros_like(acc_g)
            acc_u[...] = jnp.zeros_like(acc_u)
            acc_d[...] = jnp.zeros_like(acc_d)

        halves = [pl.ds(s * (tile // 2), tile // 2) for s in range(2)]
        dos = [do_ref[rows, :] for rows in halves]
        d_acts = [_dot_nt(do, wd_ref[0]) for do in dos]
        gs = [gate_ref[0, rows, :].astype(F32) for rows in halves]
        us = [up_ref[0, rows, :].astype(F32) for rows in halves]
        sigs = [jax.nn.sigmoid(g) for g in gs]
        silus = [g * sig for g, sig in zip(gs, sigs)]
        d_ups = [(d_act * silu).astype(BF16) for d_act, silu in zip(d_acts, silus)]
        d_gates = [(d_act * u * (sig * (1.0 + g * (1.0 - sig)))).astype(BF16) for d_act, u, sig, g in zip(d_acts, us, sigs, gs)]
        for rows, d_gate, d_up in zip(halves, d_gates, d_ups):
            dn_ref[0, rows, :] = (_dot(d_gate, wg_ref[0]) + _dot(d_up, wu_ref[0])).astype(BF16)
        d_gate, d_up = jnp.concatenate(d_gates, axis=0), jnp.concatenate(d_ups, axis=0)
        act = jnp.concatenate([(silu * u).astype(BF16) for silu, u in zip(silus, us)], axis=0)
        nn = n_ref[...]
        acc_g[...] += _dot_tn(d_gate, nn)
        acc_u[...] += _dot_tn(d_up, nn)
        acc_d[...] += _dot_tn(act, do_ref[...])

        @pl.when(i == nt - 1)
        def _():
            dw_ref[0, 0, :FF_CHUNK, :] = acc_g[...].astype(BF16)
            dw_ref[0, 0, FF_CHUNK:, :] = acc_u[:cut, :].astype(BF16)
            dw_ref[1, 0, :cut, :] = acc_u[cut:, :].astype(BF16)
            dw_ref[1, 0, cut:, :] = acc_d[...].astype(BF16)

    tok = pl.BlockSpec((tile, D_MODEL), lambda j, i, chunks_ref: (i, 0))
    hid = pl.BlockSpec((1, tile, FF_CHUNK), lambda j, i, chunks_ref: (chunks_ref[j], i, 0))
    return _call(
        body, name, (n_chunks, nt),
        [tok, tok, hid, hid]
        + [pl.BlockSpec((1, FF_CHUNK, D_MODEL), functools.partial(lambda j, i, chunks_ref, part: (chunks_ref[j], part, 0), part=part))
           for part in range(3)],
        [pl.BlockSpec((1, tile, D_MODEL), lambda j, i, chunks_ref: (j, i, 0)),
         pl.BlockSpec((2, 1, half_rows, D_MODEL), lambda j, i, chunks_ref: (0, j, 0, 0))],
        [jax.ShapeDtypeStruct((n_chunks, S, D_MODEL), BF16), jax.ShapeDtypeStruct((2, n_chunks, half_rows, D_MODEL), BF16)],
        [pltpu.VMEM((FF_CHUNK, D_MODEL), F32)] * 3, 56, (d_out, n, gate, up, group, group, group), rider, (chunks,))


def _mix_in(h, gain, w_in, rc, rs1, rs2, name):
    S = h.shape[0]
    tile = min(TOK_TILE, S)

    def body(h_ref, g_ref, w_ref, c_ref, s1_ref, s2_ref, n_ref, q_ref, k_ref, v_ref, pc_ref):
        _, xh = _rms_stats(h_ref[...])
        n = (xh * g_ref[...]).astype(BF16)
        n_ref[...] = n
        u = _dot_nt(n, w_ref[...])
        c, s1, s2 = c_ref[...], s1_ref[...], s2_ref[...]
        q_ref[...] = jnp.concatenate([(_rope(u[:, hd * LANES:(hd + 1) * LANES], c, s1, s2) * SCALE).astype(BF16)
                                      for hd in range(N_HEADS)], axis=1)
        k_ref[...] = _rope(u[:, Q_PAD:Q_PAD + KV_WIDTH], c, s1, s2).astype(BF16)
        v_ref[...] = u[:, Q_PAD + KV_WIDTH:Q_PAD + 2 * KV_WIDTH].astype(BF16)
        pc_ref[...] = u[:, Q_PAD + 2 * KV_WIDTH:]

    return pl.pallas_call(
        body, name=name, grid=(S // tile,),
        in_specs=[_rows(tile, D_MODEL), _full((1, D_MODEL)), _full((U_PAD, D_MODEL)),
                  _rows(tile, LANES), _rows(tile, LANES), _rows(tile, LANES)],
        out_specs=[_rows(tile, D_MODEL), _rows(tile, Q_PAD), _rows(tile, KV_WIDTH), _rows(tile, KV_WIDTH),
                   _rows(tile, POOL_WIDTH)],
        out_shape=[jax.ShapeDtypeStruct((S, D_MODEL), BF16), jax.ShapeDtypeStruct((S, Q_PAD), BF16),
                   jax.ShapeDtypeStruct((S, KV_WIDTH), BF16), jax.ShapeDtypeStruct((S, KV_WIDTH), BF16),
                   jax.ShapeDtypeStruct((S, POOL_WIDTH), F32)],
        compiler_params=_params(("parallel",), 40),
    )(h, gain, w_in, rc, rs1, rs2)


def _band_start(i, S):
    return pl.multiple_of(jnp.clip((i - 1) * BLOCK, 0, S - BAND), BLOCK)


def _window_bias(off):
    r = lax.broadcasted_iota(jnp.int32, (BLOCK, 1), 0)
    c = lax.broadcasted_iota(jnp.int32, (1, BAND), 1)
    return jnp.where(jnp.abs(off + r - c) <= WINDOW, 0.0, NEG).astype(F32)


def _softmax_parts(qh, kb, bias, sink_h):
    s = _dot_nt(qh, kb) + bias
    m = jnp.maximum(jnp.max(s, axis=-1, keepdims=True), sink_h)
    p = jnp.exp(s - m)
    es = jnp.exp(sink_h - m)
    return p, es, 1.0 / (jnp.sum(p, axis=-1, keepdims=True) + es)


def _pool_matrix(t0, start, S, w):
    r = lax.broadcasted_iota(jnp.int32, (BLOCK, 1), 0) + t0
    c = lax.broadcasted_iota(jnp.int32, (1, BAND), 1) + start
    half = w // 2

    def window(lo, hi):
        a = jnp.maximum(lo, 0)
        b = jnp.minimum(hi + 1, S)
        return jnp.where((c >= a) & (c < b), 1.0 / (b - a).astype(F32), 0.0)

    return (0.5 * (window(r - half, r + half - 1) + window(r - half + 1, r + half))).astype(BF16)


def _pool_matrices(S):
    blocks = ((0, 0), (BLOCK, 0), (S - BLOCK, S - BAND))
    return jnp.stack([jnp.stack([_pool_matrix(t0, start, S, w) for w in POOL_WINDOWS]) for t0, start in blocks])


def _pool_spec(nb):
    return pl.BlockSpec((1, N_POOL, BLOCK, BAND), lambda i, *_: (jnp.where(i == 0, 0, jnp.where(i == nb - 1, 2, 1)), 0, 0, 0))


def _mix_core_fwd(q, k, v, pc, sink, pool_m, pool_w, pool_scale, name):
    S = q.shape[0]
    nb = S // BLOCK

    def body(sink_ref, q_ref, k_ref, v_ref, pc_ref, pm_ref, pw_ref, ps_ref, a_ref, p_ref):
        i = pl.program_id(0)
        start = _band_start(i, S)
        band = pl.ds(start, BAND)
        bias = _window_bias(i * BLOCK - start)
        kb, vb = k_ref[band, :], v_ref[band, :]
        hs = range(N_HEADS)
        ss = [_dot_nt(q_ref[:, hd * LANES:(hd + 1) * LANES], kb) + bias for hd in hs]
        ms = [jnp.maximum(jnp.max(ss[hd], axis=-1, keepdims=True), sink_ref[0, hd]) for hd in hs]
        ps = [jnp.exp(ss[hd] - ms[hd]) for hd in hs]
        invs = [1.0 / (jnp.sum(ps[hd], axis=-1, keepdims=True) + jnp.exp(sink_ref[0, hd] - ms[hd])) for hd in hs]
        outs = [_dot(ps[hd].astype(BF16), vb) for hd in hs]
        a_ref[...] = jnp.concatenate([(outs[hd] * invs[hd]).astype(BF16) for hd in hs], axis=1)
        centre = pl.ds(pl.multiple_of(i * BLOCK, BLOCK), BLOCK)
        gs = range(N_POOL)
        sl = [slice(g * POOL_GROUP, (g + 1) * POOL_GROUP) for g in gs]
        means = [_dot(pm_ref[0, g], pc_ref[band, sl[g]].astype(BF16)) for g in gs]
        devs = [(means[g] - pc_ref[centre, sl[g]]).astype(BF16) for g in gs]
        p_ref[...] = (jnp.concatenate([_dot(devs[g], pw_ref[g]) for g in gs], axis=1) * ps_ref[...]).astype(BF16)

    return _call(
        body, name, (nb,),
        [pl.BlockSpec(memory_space=pltpu.SMEM), _rows(BLOCK, Q_PAD), _full((S, KV_WIDTH)), _full((S, KV_WIDTH)),
         _full((S, POOL_WIDTH)), _pool_spec(nb), _full((N_POOL, POOL_GROUP, POOL_GROUP)), _full((1, POOL_WIDTH))],
        [_rows(BLOCK, Q_PAD), _rows(BLOCK, POOL_WIDTH)],
        [jax.ShapeDtypeStruct((S, Q_PAD), BF16), jax.ShapeDtypeStruct((S, POOL_WIDTH), BF16)],
        [], 40, (sink, q, k, v, pc, pool_m, pool_w, pool_scale))


def _mix_core_bwd(q, k, v, pc, da, dp, sink, pool_m, pool_w, pool_scale, rc, rs1, rs2, name):
    S = q.shape[0]
    nb = S // BLOCK

    def body(sink_ref, q_ref, k_ref, v_ref, pc_ref, da_ref, dp_ref, pm_ref, pw_ref, ps_ref, c_ref, s1_ref, s2_ref,
             dq_ref, dk_ref, dv_ref, dpc_ref, dsink_ref, dpw_ref, dps_ref):
        i = pl.program_id(0)

        @pl.when(i == 0)
        def _():
            dk_ref[...] = jnp.zeros_like(dk_ref)
            dv_ref[...] = jnp.zeros_like(dv_ref)
            dpc_ref[...] = jnp.zeros_like(dpc_ref)
            dsink_ref[...] = jnp.zeros_like(dsink_ref)
            dpw_ref[...] = jnp.zeros_like(dpw_ref)
            dps_ref[...] = jnp.zeros_like(dps_ref)

        start = _band_start(i, S)
        band = pl.ds(start, BAND)
        bias = _window_bias(i * BLOCK - start)
        kb, vb = k_ref[band, :], v_ref[band, :]
        c, s1, s2 = c_ref[...], s1_ref[...], s2_ref[...]
        lane = lax.broadcasted_iota(jnp.int32, (1, LANES), 1)
        hs = range(N_HEADS)
        qs = [q_ref[:, hd * LANES:(hd + 1) * LANES] for hd in hs]
        das = [da_ref[:, hd * LANES:(hd + 1) * LANES] for hd in hs]
        ss = [_dot_nt(qs[hd], kb) + bias for hd in hs]
        d_probs = [_dot_nt(das[hd], vb) for hd in hs]
        ms = [jnp.maximum(jnp.max(ss[hd], axis=-1, keepdims=True), sink_ref[0, hd]) for hd in hs]
        ps = [jnp.exp(ss[hd] - ms[hd]) for hd in hs]
        ess = [jnp.exp(sink_ref[0, hd] - ms[hd]) for hd in hs]
        invs = [1.0 / (jnp.sum(ps[hd], axis=-1, keepdims=True) + ess[hd]) for hd in hs]
        probs = [ps[hd] * invs[hd] for hd in hs]
        deltas = [jnp.sum(probs[hd] * d_probs[hd], axis=-1, keepdims=True) for hd in hs]
        d_ss = [(probs[hd] * (d_probs[hd] - deltas[hd])).astype(BF16) for hd in hs]
        dqs = [_dot(d_ss[hd], kb) for hd in hs]
        dq_ref[...] = jnp.concatenate([_rope_bwd(dqs[hd] * SCALE, c, s1, s2).astype(BF16) for hd in hs], axis=1)
        dks = [_dot_tn(d_ss[hd], qs[hd]) for hd in hs]
        dvs = [_dot_tn(probs[hd].astype(BF16), das[hd]) for hd in hs]
        dk_ref[band, :] += functools.reduce(lambda a, b: a + b, dks)
        dv_ref[band, :] += functools.reduce(lambda a, b: a + b, dvs)
        dsink_ref[...] += functools.reduce(lambda a, b: a + b, [
            jnp.where(lane == hd, -jnp.sum(ess[hd] * invs[hd] * deltas[hd], axis=0, keepdims=True), 0.0) for hd in hs])

        centre = pl.ds(pl.multiple_of(i * BLOCK, BLOCK), BLOCK)
        gs = range(N_POOL)
        sl = [slice(g * POOL_GROUP, (g + 1) * POOL_GROUP) for g in gs]
        devs = [(_dot(pm_ref[0, g], pc_ref[band, sl[g]].astype(BF16)) - pc_ref[centre, sl[g]]).astype(BF16) for g in gs]
        dys = [dp_ref[:, sl[g]].astype(F32) for g in gs]
        zs = [_dot(devs[g], pw_ref[g]) for g in gs]
        dzs = [(dys[g] * ps_ref[:, sl[g]]).astype(BF16) for g in gs]
        d_devs = [_dot_nt(dzs[g], pw_ref[g]) for g in gs]
        dps_ref[...] += jnp.concatenate([jnp.sum(dys[g] * zs[g], axis=0, keepdims=True) for g in gs], axis=1)
        for g in gs:
            dpw_ref[g] += _dot_tn(devs[g], dzs[g])
        dpc_ref[band, :] += jnp.concatenate([_dot_tn(pm_ref[0, g], d_devs[g].astype(BF16)) for g in gs], axis=1)
        dpc_ref[centre, :] -= jnp.concatenate(d_devs, axis=1)

    return _call(
        body, name, (nb,),
        [pl.BlockSpec(memory_space=pltpu.SMEM), _rows(BLOCK, Q_PAD), _full((S, KV_WIDTH)), _full((S, KV_WIDTH)),
         _full((S, POOL_WIDTH)), _rows(BLOCK, Q_PAD), _rows(BLOCK, POOL_WIDTH), _pool_spec(nb),
         _full((N_POOL, POOL_GROUP, POOL_GROUP)), _full((1, POOL_WIDTH)),
         _rows(BLOCK, LANES), _rows(BLOCK, LANES), _rows(BLOCK, LANES)],
        [_rows(BLOCK, Q_PAD), _full((S, KV_WIDTH)), _full((S, KV_WIDTH)), _full((S, POOL_WIDTH)),
         _full((1, LANES)), _full((N_POOL, POOL_GROUP, POOL_GROUP)), _full((1, POOL_WIDTH))],
        [jax.ShapeDtypeStruct((S, Q_PAD), BF16), jax.ShapeDtypeStruct((S, KV_WIDTH), F32),
         jax.ShapeDtypeStruct((S, KV_WIDTH), F32), jax.ShapeDtypeStruct((S, POOL_WIDTH), F32),
         jax.ShapeDtypeStruct((1, LANES), F32), jax.ShapeDtypeStruct((N_POOL, POOL_GROUP, POOL_GROUP), F32),
         jax.ShapeDtypeStruct((1, POOL_WIDTH), F32)],
        [], 56, (sink, q, k, v, pc, da, dp, pool_m, pool_w, pool_scale, rc, rs1, rs2))


def _mix_out(h, a, p, wa, wp, name):
    S = h.shape[0]
    tile = min(TOK_TILE, S)

    def body(h_ref, a_ref, p_ref, wa_ref, wp_ref, o_ref):
        o_ref[...] = h_ref[...] + _dot(a_ref[...], wa_ref[...]) + _dot(p_ref[...], wp_ref[...])

    return pl.pallas_call(
        body, name=name, grid=(S // tile,),
        in_specs=[_rows(tile, D_MODEL), _rows(tile, Q_PAD), _rows(tile, POOL_WIDTH),
                  _full((Q_PAD, D_MODEL)), _full((POOL_WIDTH, D_MODEL))],
        out_specs=_rows(tile, D_MODEL),
        out_shape=jax.ShapeDtypeStruct((S, D_MODEL), F32),
        compiler_params=_params(("parallel",), 40),
    )(h, a, p, wa, wp)


def _mix_out_bwd(dh_out, dn, h, gain, a, p, wa, wp, name, rider=None):
    S = h.shape[0]
    tile = min(TOK_TILE, S)

    def body(do_ref, dn_ref, h_ref, g_ref, a_ref, p_ref, wa_ref, wp_ref, dh_ref, da_ref, dp_ref, dwa_ref, dwp_ref, dg_ref):
        @pl.when(pl.program_id(0) == 0)
        def _():
            dwa_ref[...] = jnp.zeros_like(dwa_ref)
            dwp_ref[...] = jnp.zeros_like(dwp_ref)
            dg_ref[...] = jnp.zeros_like(dg_ref)

        r, xh = _rms_stats(h_ref[...])
        dnorm, dg = _rms_bwd(_sum_chunks([dn_ref]), g_ref[...], r, xh)
        dh = do_ref[...] + dnorm
        dg_ref[...] += dg
        dh_ref[...] = dh
        dhb = dh.astype(BF16)
        da_ref[...] = _dot_nt(dhb, wa_ref[...]).astype(BF16)
        dp_ref[...] = _dot_nt(dhb, wp_ref[...]).astype(BF16)
        dwa_ref[...] += _dot_tn(a_ref[...], dhb)
        dwp_ref[...] += _dot_tn(p_ref[...], dhb)

    return _call(
        body, name, (S // tile,),
        [_rows(tile, D_MODEL), _chunk_rows(tile, dn.shape[0]), _rows(tile, D_MODEL), _full((1, D_MODEL)),
         _rows(tile, Q_PAD), _rows(tile, POOL_WIDTH), _full((Q_PAD, D_MODEL)), _full((POOL_WIDTH, D_MODEL))],
        [_rows(tile, D_MODEL), _rows(tile, Q_PAD), _rows(tile, POOL_WIDTH),
         _full((Q_PAD, D_MODEL)), _full((POOL_WIDTH, D_MODEL)), _full((1, D_MODEL))],
        [jax.ShapeDtypeStruct((S, D_MODEL), F32), jax.ShapeDtypeStruct((S, Q_PAD), BF16),
         jax.ShapeDtypeStruct((S, POOL_WIDTH), BF16), jax.ShapeDtypeStruct((Q_PAD, D_MODEL), F32),
         jax.ShapeDtypeStruct((POOL_WIDTH, D_MODEL), F32), jax.ShapeDtypeStruct((1, D_MODEL), F32)],
        [], 48, (dh_out, dn, h, gain, a, p, wa, wp), rider)


def _mix_in_bwd(dh_out, h, gain, n, dq, dk, dv, dpc, rc, rs1, rs2, w_in, name):
    S = h.shape[0]
    tile = min(TOK_TILE, S)

    def body(do_ref, h_ref, g_ref, n_ref, dq_ref, dk_ref, dv_ref, dpc_ref, c_ref, s1_ref, s2_ref, w_ref,
             dh_ref, dhalf_ref, dw_ref, dg_ref):
        @pl.when(pl.program_id(0) == 0)
        def _():
            dw_ref[...] = jnp.zeros_like(dw_ref)
            dg_ref[...] = jnp.zeros_like(dg_ref)

        dk = _rope_bwd(dk_ref[...], c_ref[...], s1_ref[...], s2_ref[...]).astype(BF16)
        du = jnp.concatenate([dq_ref[...], dk, dv_ref[...].astype(BF16), dpc_ref[...].astype(BF16)], axis=1)
        dn = _dot(du, w_ref[...])
        dw_ref[...] += _dot_tn(du, n_ref[...])
        r, xh = _rms_stats(h_ref[...])
        dnorm, dg = _rms_bwd(dn, g_ref[...], r, xh)
        dh = do_ref[...] + dnorm
        dg_ref[...] += dg
        dh_ref[...] = dh
        dhalf_ref[...] = (0.5 * dh).astype(BF16)

    return _call(
        body, name, (S // tile,),
        [_rows(tile, D_MODEL), _rows(tile, D_MODEL), _full((1, D_MODEL)), _rows(tile, D_MODEL),
         _rows(tile, Q_PAD), _rows(tile, KV_WIDTH), _rows(tile, KV_WIDTH), _rows(tile, POOL_WIDTH),
         _rows(tile, LANES), _rows(tile, LANES), _rows(tile, LANES), _full((U_PAD, D_MODEL))],
        [_rows(tile, D_MODEL), _rows(tile, D_MODEL), _full((U_PAD, D_MODEL)), _full((1, D_MODEL))],
        [jax.ShapeDtypeStruct((S, D_MODEL), F32), jax.ShapeDtypeStruct((S, D_MODEL), BF16),
         jax.ShapeDtypeStruct((U_PAD, D_MODEL), F32), jax.ShapeDtypeStruct((1, D_MODEL), F32)],
        [], 56, (dh_out, h, gain, n, dq, dk, dv, dpc, rc, rs1, rs2, w_in))


def _norm_bwd(dh_out, dns, h, gain, name):
    S = h.shape[0]
    tile = min(TOK_TILE, S)
    n = len(dns)

    def body(do_ref, *refs):
        h_ref, g_ref, dh_ref, dg_ref = refs[n:]

        @pl.when(pl.program_id(0) == 0)
        def _():
            dg_ref[...] = jnp.zeros_like(dg_ref)

        r, xh = _rms_stats(h_ref[...])
        dnorm, dg = _rms_bwd(_sum_chunks(refs[:n]), g_ref[...], r, xh)
        dg_ref[...] += dg
        dh_ref[...] = do_ref[...] + dnorm

    return _call(
        body, name, (S // tile,),
        [_rows(tile, D_MODEL)] + [_chunk_rows(tile, dn.shape[0]) for dn in dns] + [_rows(tile, D_MODEL), _full((1, D_MODEL))],
        [_rows(tile, D_MODEL), _full((1, D_MODEL))],
        [jax.ShapeDtypeStruct((S, D_MODEL), F32), jax.ShapeDtypeStruct((1, D_MODEL), F32)],
        [], 40, (dh_out, *dns, h, gain))


def _rope_tables(S):
    half = ROTARY_DIM // 2
    inv_freq = ROPE_THETA ** (-jnp.arange(0, ROTARY_DIM, 2, dtype=F32) / ROTARY_DIM)
    dim = jnp.arange(LANES) % HEAD_DIM
    ang = jnp.arange(S, dtype=F32)[:, None] * inv_freq[dim % half][None, :]
    lo, hi = (dim < half)[None, :], ((dim >= half) & (dim < ROTARY_DIM))[None, :]
    c = jnp.where(lo | hi, jnp.cos(ang), 1.0)
    s1 = jnp.where(lo, -jnp.sin(ang), 0.0)
    s2 = jnp.where(hi, jnp.sin(ang), 0.0)
    return c, s1, s2


def _pad_heads(w, axis):
    w = jnp.moveaxis(w, axis, 0)
    heads = w.reshape((N_HEADS, HEAD_DIM) + w.shape[1:])
    zero = jnp.zeros_like(heads)
    first = (jnp.arange(N_HEADS) < Q_PER_KV).reshape((N_HEADS, 1) + (1,) * (w.ndim - 1))
    lo = jnp.where(first, heads, zero)
    hi = jnp.where(first, zero, heads)
    padded = jnp.concatenate([lo, hi], axis=1).reshape((Q_PAD,) + w.shape[1:])
    return jnp.moveaxis(padded, 0, axis)


def _unpad_heads(w, axis):
    w = jnp.moveaxis(w, axis, 0)
    groups = w.reshape((N_HEADS, 2, HEAD_DIM) + w.shape[1:])
    first = (jnp.arange(N_HEADS) < Q_PER_KV).reshape((N_HEADS, 1) + (1,) * (w.ndim - 1))
    heads = jnp.where(first, groups[:, 0], groups[:, 1]).reshape((ATTN_WIDTH,) + w.shape[1:])
    return jnp.moveaxis(heads, 0, axis)


IN_ROWS = IN_WIDTH // N_CHIPS
OUT_ROWS = (ATTN_WIDTH + POOL_WIDTH) // N_CHIPS
MIX_ROWS = IN_ROWS + OUT_ROWS
FFN_ROWS = 3 * FF_CHUNK


def _step(x, target, bufs, small, place):
    S = x.shape[0]
    rc, rs1, rs2 = _rope_tables(S)
    mine = place[1]
    order = jnp.stack([mine, mine ^ 2, mine ^ 1, mine ^ 3])
    h1, n1, gate1, up1, ffn1, mix = _ffn_fwd(order, x, small["ffn1_norm"], "ffn1_fwd",
                                             _merge(_allgather(bufs[:1], in_passes=0), _allgather(bufs[1:2])), 0)
    in_flight, (ffn2,), token = _split_start(bufs[2:], 2, _gather_copies(X_Y_DIAGONAL[:2]), "ffn2_gather_start")
    w_in_t = mix[:, :IN_ROWS].reshape(IN_WIDTH, D_MODEL)
    w_in_pad = jnp.concatenate([_pad_heads(w_in_t[:ATTN_WIDTH], 0), w_in_t[ATTN_WIDTH:]], axis=0)
    w_out = mix[:, IN_ROWS:].reshape(ATTN_WIDTH + POOL_WIDTH, D_MODEL)
    wa = _pad_heads(w_out[:ATTN_WIDTH], 0)
    wp = w_out[ATTN_WIDTH:]
    pool_w = small["pool_w"].astype(BF16)

    n2, q, k, v, pc = _mix_in(h1, small["mix_norm"] + token[0, 0], w_in_pad, rc, rs1, rs2, "mix_in")
    pool_m = _pool_matrices(S)
    a, p = _mix_core_fwd(q, k, v, pc, small["sink_logits"], pool_m, pool_w, small["pool_scale"], "mix_core_fwd")
    h2 = _mix_out(h1, a, p, wa, wp, "mix_out")
    (ffn2,) = _split_wait(in_flight, [ffn2], _gather_copies(X_Y_DIAGONAL[:2]), h2, "ffn2_gather_wait")
    dh3, dhalf3, loss_lanes, d_final, n3, gate2, up2, ffn2 = _ffn_fwd(
        order, h2, small["ffn2_norm"], "ffn2_fwd",
        _allgather([ffn2], peers=X_Y_DIAGONAL[2:], in_passes=0, landed_before=X_Y_DIAGONAL[:2]), 0,
        loss_head=(target, small["final_norm"]))

    dn3, d_ffn2 = _ffn_bwd(jnp.arange(N_CHIPS, dtype=jnp.int32), dhalf3, n3, gate2, up2, ffn2, "ffn2_bwd")
    dh2, da, dp, dwa, dwp, d_ffn2_norm, received = _mix_out_bwd(dh3, dn3, h2, small["ffn2_norm"], a, p, wa, wp, "mix_out_bwd",
                                                                _sibling_exchange([d_ffn2]))
    pair = _pair_sum(place, d_ffn2, received, "grad_pair_sum_ffn2")
    in_flight, (pair, stack), token = _split_start([pair, received], 3, _scatter_copies, "ffn2_scatter_start")
    dq, dk, dv, dpc, dsink, dpool_w, dpool_scale = _mix_core_bwd(
        q, k, v, pc, da, dp, small["sink_logits"], pool_m, pool_w, small["pool_scale"] + token[0, 0], rc, rs1, rs2, "mix_core_bwd")
    dh1, dhalf1, dw_in_pad, d_mix_norm = _mix_in_bwd(dh2, h1, small["mix_norm"], n2, dq, dk, dv, dpc, rc, rs1, rs2, w_in_pad,
                                                     "mix_in_bwd")
    pair, stack = _split_wait(in_flight, [pair, stack], _scatter_copies, d_mix_norm, "ffn2_scatter_wait")
    reduced_ffn2 = _chip_sum(place, pair, stack, "grad_chip_sum_ffn2")
    dw_in_t = jnp.concatenate([_unpad_heads(dw_in_pad[:Q_PAD], 0), dw_in_pad[Q_PAD:]], axis=0)
    dw_out = jnp.concatenate([_unpad_heads(dwa, 0), dwp], axis=0)
    d_mix = jnp.concatenate([dw_in_t.reshape(N_CHIPS, IN_ROWS, D_MODEL), dw_out.reshape(N_CHIPS, OUT_ROWS, D_MODEL)], axis=1)
    d_mix = jnp.transpose(d_mix.reshape(N_CHIPS, 2, MIX_ROWS // 2, D_MODEL), (1, 0, 2, 3)).astype(BF16)
    small_g = {"ffn1_norm": jnp.zeros_like(d_mix_norm), "mix_norm": d_mix_norm, "ffn2_norm": d_ffn2_norm,
               "final_norm": d_final, "pool_scale": dpool_scale, "sink_logits": dsink[:, :N_HEADS], "pool_w": dpool_w}
    loss_row = jnp.sum(loss_lanes.reshape(D_MODEL // LANES, LANES), axis=0, keepdims=True)
    small_early = _pack_small(small_g, loss_row)

    chunk = [(place[1:] + 1 + p) % N_CHIPS for p in range(N_CHIPS)]
    ffn1_bwd = functools.partial(_ffn_bwd, d_out=dhalf1, n=n1, gate=gate1, up=up1, group=ffn1)
    dn_a, dw_a, recv_mix, small_all, g_ffn2 = ffn1_bwd(
        chunk[0], name="ffn1_bwd_0",
        rider=_merge(_merge(_sibling_exchange([d_mix]), _small_allgather(small_early)), _sibling_share([reduced_ffn2])))
    pair_mix = _pair_sum(place, d_mix, recv_mix, "grad_pair_sum_mix")
    dn_b, dw_b, recv_a, stack_mix = ffn1_bwd(chunk[1], name="ffn1_bwd_1",
                                             rider=_merge(_sibling_exchange([dw_a]), _scatter([pair_mix])))
    pair_a = _pair_sum(place, dw_a, recv_a, "grad_pair_sum_ffn1_0")
    reduced_mix = _chip_sum(place, pair_mix, stack_mix, "grad_chip_sum_mix")
    steps = [(_scatter_step_copies(p), (p, 3)) for p in range(3)]
    sems_a, (pair_a, stack), token = _split_start([pair_a, stack], 1, steps[0][0], "ffn1_scatter_start_0")
    dn_c, dw_c, recv_b, g_mix = ffn1_bwd(
        chunk[2], name="ffn1_bwd_2", rider=_merge(_merge(_sibling_exchange([dw_b]), _sibling_share([reduced_mix])), _after(token)))
    pair_b = _pair_sum(place, dw_b, recv_b, "grad_pair_sum_ffn1_1")
    sems_b, (pair_b, stack), token = _split_start([pair_b, stack], 1, steps[1][0], "ffn1_scatter_start_1")
    dn_d, dw_d, recv_c = ffn1_bwd(chunk[3], name="ffn1_bwd_3", rider=_merge(_sibling_exchange([dw_c]), _after(token)))
    pair_c = _pair_sum(place, dw_c, recv_c, "grad_pair_sum_ffn1_2")
    own = (_exchange_copies, (4, 5))
    sems_c, (pair_c, stack, dw_d, recv_d), token = _split_start(
        [pair_c, stack, dw_d, recv_c], 2, _joined([(steps[2][0], (0, 1)), (own[0], (2, 3))]), "ffn1_scatter_start_2")
    grad_x, d_ffn1_norm = _norm_bwd(dh1, [dn_a, dn_b, dn_c, dn_d], x, small["ffn1_norm"] + token[0, 0], "norm1_bwd")
    pair_a, pair_b, pair_c, stack, dw_d, recv_d = _split_wait(
        sems_a + sems_b + sems_c, [pair_a, pair_b, pair_c, stack, dw_d, recv_d], _joined(steps + [own]), d_ffn1_norm,
        "ffn1_scatter_wait")
    pair_d = _pair_sum(place, dw_d, recv_d, "grad_pair_sum_ffn1_3")
    reduced_ffn1 = _chip_sum(place, pair_d, stack, "grad_chip_sum_ffn1")
    g_ffn1, gains = _comm_call(_merge(_sibling_share([reduced_ffn1]), _small_allgather(d_ffn1_norm.reshape(-1, LANES))),
                               "grad_share_tail")
    gain_sum = _sum_leading(gains, 1, "gain_grad_sum")
    small_sum = jnp.concatenate([gain_sum, _sum_leading(small_all, 1, "small_grad_sum")[gain_sum.shape[0]:]], axis=0)
    return jnp.sum(small_sum[SMALL_ROWS - 1]), grad_x, [g.reshape(-1, D_MODEL) for g in (g_ffn1, g_mix, g_ffn2)], small_sum


GROUPS =(("ffn1_w_gate", "ffn1_w_up", "ffn1_w_down"), ("w_in", "w_out"), ("ffn2_w_gate", "ffn2_w_up", "ffn2_w_down"))
TRANSPOSED = ("ffn1_w_gate", "ffn1_w_up", "w_in", "ffn2_w_gate", "ffn2_w_up")


def _place():
    x, y, c = lax.axis_index("x"), lax.axis_index("y"), lax.axis_index("c")
    chips = [(1 - x, y), (x, 1 - y), (1 - x, 1 - y)]
    return x, y, c, chips


def _remote(src, dst, send_sem, recv_sem, to):
    return pltpu.make_async_remote_copy(src_ref=src, dst_ref=dst, send_sem=send_sem, recv_sem=recv_sem,
                                        device_id=to, device_id_type=MESH)


def _pack(chip, members, name):
    rows = members[0].shape[0]
    n = len(members)

    def body(chip_ref, *refs):
        ins, out_ref, buf, sems = refs[:n], refs[n], refs[n + 1], refs[n + 2]
        copies = [pltpu.make_async_copy(ins[k], buf.at[k], sems.at[k]) for k in range(n)]
        for cp in copies:
            cp.start()
        for k in range(n):
            copies[k].wait()
            out_ref[0, k * rows:(k + 1) * rows, :] = buf[k].astype(BF16)

    return pl.pallas_call(
        body, name=name,
        grid_spec=pltpu.PrefetchScalarGridSpec(
            num_scalar_prefetch=1, grid=(1,),
            in_specs=[HBM_SPEC] * n,
            out_specs=pl.BlockSpec((1, n * rows, D_MODEL), lambda k, chip_ref: (chip_ref[0], 0, 0)),
            scratch_shapes=[pltpu.VMEM((n, rows, D_MODEL), F32), pltpu.SemaphoreType.DMA((n,))]),
        out_shape=jax.ShapeDtypeStruct((N_CHIPS, n * rows, D_MODEL), BF16),
        compiler_params=_params(("arbitrary",), 40),
    )(chip, *members)


def _same(arrays):
    return [jax.ShapeDtypeStruct(a.shape, a.dtype) for a in arrays]


X_Y_DIAGONAL = (0, 1, 2)


def _allgather(bufs, peers=X_Y_DIAGONAL, in_passes=None, landed_before=()):
    n = len(bufs)

    def copy(kind, outs, send_sems, recv_sems, a, k):
        x, y, c, chips = _place()
        half = bufs[a].shape[1] // 2

        def rows(slot, core):
            return outs[a].at[slot, pl.ds(pl.multiple_of(core * half, 16), half)]

        me, slot = 2 * x + y, 2 * chips[k][0] + chips[k][1]
        over_ici = (send_sems.at[6 * a + k], recv_sems.at[6 * a + k])
        over_d2d = (send_sems.at[6 * a + 3 + k], recv_sems.at[6 * a + 3 + k])
        if kind == "first":
            return _remote(rows(me, c), rows(me, c), *over_ici, (*chips[k], c))
        if kind == "landed":
            return _remote(rows(me, c), rows(slot, c), *over_ici, (*chips[k], c))
        if kind == "passed":
            return _remote(rows(slot, c), rows(slot, c), *over_d2d, (x, y, 1 - c))
        return _remote(rows(me, c), rows(slot, 1 - c), *over_d2d, (x, y, 1 - c))

    def arrive(pairs):
        def hook(ins, outs, sems):
            for a, k in pairs:
                if k not in landed_before:
                    copy("landed", outs, *sems, a, k).wait_recv()
                    copy("passed", outs, *sems, a, k).start()
            for a, k in pairs:
                copy("handed", outs, *sems, a, k).wait_recv()
        return hook

    sent = [(a, k) for a in range(n) for k in peers]
    everything = sent + [(in_passes, k) for k in landed_before]
    early = [(a, k) for a, k in everything if a == in_passes]

    def start(ins, outs, sems):
        for k in landed_before:
            copy("passed", outs, *sems, in_passes, k).start()
        for a, k in sent:
            copy("first", outs, *sems, a, k).start()

    def finish(ins, outs, sems):
        arrive([pair for pair in everything if pair not in early])(ins, outs, sems)
        for a, k in sent:
            copy("first", outs, *sems, a, k).wait_send()
        for a, k in everything:
            copy("passed", outs, *sems, a, k).wait_send()

    hooks = tuple(((k + 1, 0), arrive([(a, k)])) for a, k in early)
    return _Rider(list(bufs), _same(bufs), {a: a for a in range(n)},
                  [pltpu.SemaphoreType.DMA((6 * n,)), pltpu.SemaphoreType.DMA((6 * n,))], start, finish, hooks)


SEM_SPEC = pl.BlockSpec(memory_space=pltpu.SEMAPHORE)


def _split_start(arrays, n_copies, make_copies, name):
    m, n = len(arrays), 2 * n_copies

    def body(*refs):
        sems, thru, token = refs[m:m + n], refs[m + n:2 * m + n], refs[2 * m + n]
        for cp in make_copies(thru, sems, False):
            cp.start()
        token[...] = jnp.zeros_like(token)

    outs = pl.pallas_call(
        body, name=name,
        out_shape=[pltpu.SemaphoreType.DMA(())] * n + _same(arrays) + [jax.ShapeDtypeStruct((SUBLANES, LANES), F32)],
        in_specs=[HBM_SPEC] * m, out_specs=[SEM_SPEC] * n + [HBM_SPEC] * m + [pl.BlockSpec(memory_space=pltpu.VMEM)],
        input_output_aliases={a: n + a for a in range(m)},
        compiler_params=pltpu.CompilerParams(has_side_effects=pltpu.SideEffectType.DATAFLOW_SIDE_EFFECTING),
    )(*arrays)
    return list(outs[:n]), list(outs[n:n + m]), outs[n + m]


def _split_wait(sems, arrays, make_copies, after, name):
    m, n = len(arrays), len(sems)

    def body(*refs):
        for cp in make_copies(refs[m + n + 1:], refs[m:m + n], True):
            cp.wait_send()
            cp.wait_recv()

    return pl.pallas_call(
        body, name=name, out_shape=_same(arrays),
        in_specs=[HBM_SPEC] * m + [SEM_SPEC] * n + [pl.BlockSpec(memory_space=pl.ANY)], out_specs=[HBM_SPEC] * m,
        input_output_aliases={a: a for a in range(m)},
        compiler_params=pltpu.CompilerParams(has_side_effects=pltpu.SideEffectType.DATAFLOW_SIDE_EFFECTING),
    )(*arrays, *sems, after)


def _gather_copies(peers):
    def make(refs, sems, landing):
        x, y, c, chips = _place()
        half = refs[0].shape[1] // 2

        def rows(slot):
            return refs[0].at[slot, pl.ds(pl.multiple_of(c * half, 16), half)]

        me = 2 * x + y
        return [_remote(rows(me), rows(2 * chips[k][0] + chips[k][1] if landing else me), sems[2 * j], sems[2 * j + 1],
                        (*chips[k], c)) for j, k in enumerate(peers)]
    return make


def _scatter_copies(refs, sems, landing):
    x, y, c, chips = _place()
    me = 2 * x + y
    slots = [2 * cx + cy for cx, cy in chips]
    return [_remote(refs[0].at[slots[k]], refs[1].at[slots[k] if landing else me], sems[2 * k], sems[2 * k + 1], (*chips[k], c))
            for k in X_Y_DIAGONAL]


def _scatter_step_copies(step):
    def make(refs, sems, landing):
        x, y, c, _ = _place()
        me = 2 * x + y
        to = (me + 1 + step) % N_CHIPS
        frm = (me + N_CHIPS - 1 - step) % N_CHIPS
        peer = frm if landing else to
        return [_remote(refs[0].at[0], refs[1].at[frm if landing else me], sems[0], sems[1], (peer // 2, peer % 2, c))]
    return make


def _exchange_copies(refs, sems, landing):
    x, y, c, _ = _place()
    return [_remote(refs[0].at[1 - c], refs[1], sems[0], sems[1], (x, y, 1 - c))]


def _joined(makers):
    def make(refs, sems, landing):
        copies, at = [], 0
        for maker, places in makers:
            mine = maker([refs[p] for p in places], sems[at:], landing)
            copies += mine
            at += 2 * len(mine)
        return copies
    return make


def _sibling_exchange(parts):
    n = len(parts)

    def copies(ins, outs, send_sems, recv_sems):
        x, y, c, _ = _place()
        return [_remote(ins[a].at[1 - c], outs[a], send_sems.at[a], recv_sems.at[a], (x, y, 1 - c)) for a in range(n)]

    def start(ins, outs, sems):
        for cp in copies(ins, outs, *sems):
            cp.start()

    def finish(ins, outs, sems):
        for cp in copies(ins, outs, *sems):
            cp.wait_recv()
            cp.wait_send()

    return _Rider(list(parts), [jax.ShapeDtypeStruct(p.shape[1:], p.dtype) for p in parts], {},
                  [pltpu.SemaphoreType.DMA((n,)), pltpu.SemaphoreType.DMA((n,))], start, finish)


def _small_allgather(small):
    flips = [(fx, fy, fc) for fx in range(2) for fy in range(2) for fc in range(2)][1:]

    def copies(small_ref, gather_ref, send_sems, recv_sems, local_sem, started_only=False):
        x, y, c, _ = _place()
        me = 4 * x + 2 * y + c
        peers = [((1 - x) if fx else x, (1 - y) if fy else y, (1 - c) if fc else c) for fx, fy, fc in flips]
        own = pltpu.make_async_copy(small_ref, gather_ref.at[me], local_sem)
        sent = [_remote(small_ref, gather_ref.at[me], send_sems.at[k], recv_sems.at[k], peer) for k, peer in enumerate(peers)]
        if started_only:
            return own, sent
        landed = [_remote(small_ref, gather_ref.at[4 * px + 2 * py + pc], send_sems.at[k], recv_sems.at[k], (px, py, pc))
                  for k, (px, py, pc) in enumerate(peers)]
        return own, sent, landed

    def start(ins, outs, sems):
        own, sent = copies(ins[0], outs[0], *sems, started_only=True)
        own.start()
        for cp in sent:
            cp.start()

    def finish(ins, outs, sems):
        own, sent, landed = copies(ins[0], outs[0], *sems)
        for cp in landed:
            cp.wait_recv()
        for cp in sent:
            cp.wait_send()
        own.wait()

    return _Rider([small], [jax.ShapeDtypeStruct((2 * N_CHIPS,) + small.shape, small.dtype)], {},
                  [pltpu.SemaphoreType.DMA((7,)), pltpu.SemaphoreType.DMA((7,)), pltpu.SemaphoreType.DMA], start, finish)


def _merge(a, b):
    na, nao, nas = len(a.operands), len(a.out_shapes), len(a.scratch)

    def start(ins, outs, sems):
        a.start(ins[:na], outs[:nao], sems[:nas])
        b.start(ins[na:], outs[nao:], sems[nas:])

    def finish(ins, outs, sems):
        a.finish(ins[:na], outs[:nao], sems[:nas])
        b.finish(ins[na:], outs[nao:], sems[nas:])

    def of_a(fn):
        return lambda ins, outs, sems: fn(ins[:na], outs[:nao], sems[:nas])

    def of_b(fn):
        return lambda ins, outs, sems: fn(ins[na:], outs[nao:], sems[nas:])

    aliases = {**a.aliases, **{na + k: nao + v for k, v in b.aliases.items()}}
    hooks = tuple((at, of_a(fn)) for at, fn in a.hooks) + tuple((at, of_b(fn)) for at, fn in b.hooks)
    return _Rider(a.operands + b.operands, a.out_shapes + b.out_shapes, aliases, a.scratch + b.scratch, start, finish, hooks)


def _scatter(sums):
    n = len(sums)

    def copies(ins, outs, send_sems, recv_sems, started_only=False):
        x, y, c, chips = _place()
        me = 2 * x + y
        slots = [2 * cx + cy for cx, cy in chips]
        sent = [_remote(ins[a].at[slots[k]], outs[a].at[me], send_sems.at[3 * a + k], recv_sems.at[3 * a + k], (*chips[k], c))
                for a in range(n) for k in X_Y_DIAGONAL]
        if started_only:
            return sent
        landed = [_remote(ins[a].at[slots[k]], outs[a].at[slots[k]], send_sems.at[3 * a + k], recv_sems.at[3 * a + k],
                          (*chips[k], c)) for a in range(n) for k in X_Y_DIAGONAL]
        return sent, landed

    def start(ins, outs, sems):
        for cp in copies(ins, outs, *sems, started_only=True):
            cp.start()

    def finish(ins, outs, sems):
        sent, landed = copies(ins, outs, *sems)
        for cp in landed:
            cp.wait_recv()
        for cp in sent:
            cp.wait_send()

    return _Rider(list(sums), _same(sums), {}, [pltpu.SemaphoreType.DMA((3 * n,)), pltpu.SemaphoreType.DMA((3 * n,))],
                  start, finish)


def _sibling_share(bufs):
    n = len(bufs)

    def copies(outs, send_sems, recv_sems, started_only=False):
        x, y, c, _ = _place()
        sent = [_remote(outs[a].at[c], outs[a].at[c], send_sems.at[a], recv_sems.at[a], (x, y, 1 - c)) for a in range(n)]
        if started_only:
            return sent
        landed = [_remote(outs[a].at[c], outs[a].at[1 - c], send_sems.at[a], recv_sems.at[a], (x, y, 1 - c)) for a in range(n)]
        return sent, landed

    def start(ins, outs, sems):
        for cp in copies(outs, *sems, started_only=True):
            cp.start()

    def finish(ins, outs, sems):
        sent, landed = copies(outs, *sems)
        for cp in landed:
            cp.wait_recv()
        for cp in sent:
            cp.wait_send()

    return _Rider(list(bufs), _same(bufs), {a: a for a in range(n)},
                  [pltpu.SemaphoreType.DMA((n,)), pltpu.SemaphoreType.DMA((n,))], start, finish)


def _sum_tile(rows):
    tile = next(t for t in (FF_CHUNK // 4, 96) if rows % t == 0)
    assert tile % (2 * SUBLANES) == 0
    return tile


def _pair_sum(core, part, received, name):
    _, k, rh, cols = part.shape
    tile = _sum_tile(rh)

    def body(core_ref, p_ref, r_ref, o_ref):
        o_ref[...] = (p_ref[0].astype(F32) + r_ref[...].astype(F32)).astype(BF16)

    return pl.pallas_call(
        body, name=name,
        grid_spec=pltpu.PrefetchScalarGridSpec(
            num_scalar_prefetch=1, grid=(k, rh // tile),
            in_specs=[pl.BlockSpec((1, 1, tile, cols), lambda j, i, core_ref: (core_ref[0], j, i, 0)),
                      pl.BlockSpec((1, tile, cols), lambda j, i, core_ref: (j, i, 0))],
            out_specs=pl.BlockSpec((1, tile, cols), lambda j, i, core_ref: (j, i, 0))),
        out_shape=jax.ShapeDtypeStruct((k, rh, cols), BF16),
        compiler_params=_params(("parallel", "parallel"), 60),
    )(core, part, received)


def _sum_leading(stack, steps, name):
    k, rows, cols = stack.shape
    tile = rows // steps

    def body(s_ref, o_ref):
        total = s_ref[0].astype(F32)
        for d in range(1, k):
            total = total + s_ref[d].astype(F32)
        o_ref[...] = total

    return pl.pallas_call(
        body, name=name, grid=(steps,),
        in_specs=[pl.BlockSpec((k, tile, cols), lambda i: (0, i, 0))],
        out_specs=pl.BlockSpec((tile, cols), lambda i: (i, 0)),
        out_shape=jax.ShapeDtypeStruct((rows, cols), F32),
        compiler_params=_params(("parallel",), 32),
    )(stack)


def _chip_sum(place, own, stack, name):
    k, rows, cols = stack.shape
    tile = _sum_tile(rows)

    def body(place_ref, own_ref, *refs):
        chip = place_ref[1]
        total = None
        for d in range(k):
            term = jnp.where(chip == d, own_ref[0], refs[d][0]).astype(F32)
            total = term if total is None else total + term
        refs[k][0] = total

    def other(d):
        return lambda i, place_ref: (jnp.where(place_ref[1] == d, (d + 1) % k, d), i, 0)

    return pl.pallas_call(
        body, name=name,
        grid_spec=pltpu.PrefetchScalarGridSpec(
            num_scalar_prefetch=1, grid=(rows // tile,),
            in_specs=[pl.BlockSpec((1, tile, cols), lambda i, place_ref: (place_ref[1] % own.shape[0], i, 0))]
            + [pl.BlockSpec((1, tile, cols), other(d)) for d in range(k)],
            out_specs=pl.BlockSpec((1, tile, cols), lambda i, place_ref: (place_ref[0], i, 0))),
        out_shape=jax.ShapeDtypeStruct((2, rows, cols), F32),
        compiler_params=_params(("arbitrary",), 60),
    )(place, own, *([stack] * k))


def _adamw(w, g, row0, m, v, tile, name):
    rows, cols = w.shape
    first = row0 // tile
    assert rows % tile == 0 and row0 % tile == 0
    bc1 = 1.0 - ADAM_B1 ** ADAM_STEP
    bc2 = 1.0 - ADAM_B2 ** ADAM_STEP

    def body(w_ref, g_ref, m_ref, v_ref, go_ref, d_ref, mo_ref, vo_ref):
        g = g_ref[...]
        m_new = ADAM_B1 * m_ref[...] + (1.0 - ADAM_B1) * g
        v_new = ADAM_B2 * v_ref[...] + (1.0 - ADAM_B2) * (g * g)
        go_ref[...] = g
        d_ref[...] = -ADAM_LR * ((m_new / bc1) / (jnp.sqrt(v_new / bc2) + ADAM_EPS) + ADAM_WD * w_ref[...])
        mo_ref[...] = m_new
        vo_ref[...] = v_new

    spec = pl.BlockSpec((tile, cols), lambda i: (i, 0))
    g_spec = pl.BlockSpec((tile, cols), lambda i: (first + i, 0))
    return pl.pallas_call(
        body, name=name, grid=(rows // tile,),
        in_specs=[spec, g_spec, spec, spec], out_specs=[spec] * 4,
        out_shape=[jax.ShapeDtypeStruct((rows, cols), F32)] * 4,
        compiler_params=_params(("parallel",), 32),
    )(w, g, m, v)


SMALL = ("ffn1_norm", "mix_norm", "ffn2_norm", "final_norm", "pool_scale", "sink_logits", "pool_w")


def _pack_small(d, last_row=None):
    def part(n):
        flat = d[n].reshape(-1)
        flat = jnp.pad(flat, (0, -flat.shape[0] % (SUBLANES * LANES)))
        return flat.reshape(-1, LANES)

    last = jnp.zeros((SUBLANES, LANES), F32) if last_row is None else jnp.pad(last_row, ((SUBLANES - 1, 0), (0, 0)))
    packed = jnp.concatenate([part(n) for n in SMALL] + [last], axis=0)
    assert packed.shape[0] == SMALL_ROWS
    return packed


def _unpack_small(packed, like):
    out, row = {}, 0
    for n in SMALL:
        size = math.prod(like[n].shape)
        rows = -(-size // (SUBLANES * LANES)) * SUBLANES
        out[n] = packed[row:row + rows].reshape(-1)[:size].reshape(like[n].shape)
        row += rows
    return out


def kernel(x, ffn1_norm, ffn1_w_gate, ffn1_w_up, ffn1_w_down, mix_norm, w_in, sink_logits, pool_w, pool_scale, w_out, ffn2_norm, ffn2_w_gate, ffn2_w_up, ffn2_w_down, final_norm, loss_target, m_ffn1_norm, m_ffn1_w_gate, m_ffn1_w_up, m_ffn1_w_down, m_mix_norm, m_w_in, m_sink_logits, m_pool_w, m_pool_scale, m_w_out, m_ffn2_norm, m_ffn2_w_gate, m_ffn2_w_up, m_ffn2_w_down, m_final_norm, v_ffn1_norm, v_ffn1_w_gate, v_ffn1_w_up, v_ffn1_w_down, v_mix_norm, v_w_in, v_sink_logits, v_pool_w, v_pool_scale, v_w_out, v_ffn2_norm, v_ffn2_w_gate, v_ffn2_w_up, v_ffn2_w_down, v_final_norm):
    names = ("ffn1_norm", "ffn1_w_gate", "ffn1_w_up", "ffn1_w_down", "mix_norm", "w_in", "sink_logits", "pool_w",
             "pool_scale", "w_out", "ffn2_norm", "ffn2_w_gate", "ffn2_w_up", "ffn2_w_down", "final_norm")
    weights = dict(zip(names, (ffn1_norm, ffn1_w_gate, ffn1_w_up, ffn1_w_down, mix_norm, w_in, sink_logits, pool_w,
                               pool_scale, w_out, ffn2_norm, ffn2_w_gate, ffn2_w_up, ffn2_w_down, final_norm)))
    mom1 = dict(zip(names, (m_ffn1_norm, m_ffn1_w_gate, m_ffn1_w_up, m_ffn1_w_down, m_mix_norm, m_w_in, m_sink_logits,
                            m_pool_w, m_pool_scale, m_w_out, m_ffn2_norm, m_ffn2_w_gate, m_ffn2_w_up, m_ffn2_w_down,
                            m_final_norm)))
    mom2 = dict(zip(names, (v_ffn1_norm, v_ffn1_w_gate, v_ffn1_w_up, v_ffn1_w_down, v_mix_norm, v_w_in, v_sink_logits,
                            v_pool_w, v_pool_scale, v_w_out, v_ffn2_norm, v_ffn2_w_gate, v_ffn2_w_up, v_ffn2_w_down,
                            v_final_norm)))
    chip = (2 * lax.axis_index("x") + lax.axis_index("y")).astype(jnp.int32).reshape(1)
    place = jnp.concatenate([lax.axis_index("c").astype(jnp.int32).reshape(1), chip])

    def rows_of(t, n):
        return jnp.swapaxes(t[n][0], 0, 1) if n in TRANSPOSED else t[n][0]

    bufs = [_pack(chip, [rows_of(weights, n) for n in GROUPS[0]], "pack_ffn1"),
            _pack(chip, [jnp.concatenate([rows_of(weights, n) for n in GROUPS[1]], axis=0)], "pack_mix"),
            _pack(chip, [rows_of(weights, n) for n in GROUPS[2]], "pack_ffn2")]

    small_w = {"ffn1_norm": ffn1_norm, "mix_norm": mix_norm, "ffn2_norm": ffn2_norm,
               "final_norm": final_norm.reshape(1, D_MODEL), "pool_scale": pool_scale, "sink_logits": sink_logits,
               "pool_w": pool_w[0]}
    loss, grad_x, group_grads, small_sum = _step(x[0], loss_target[0], bufs, small_w, place)

    out_g, out_d, out_m, out_v = {}, {}, {}, {}
    for members, g in zip(GROUPS, group_grads):
        row0 = 0
        for n in members:
            w = rows_of(weights, n)
            tile = FF_CHUNK // 4 if w.shape[0] == FF_CHUNK else math.gcd(IN_ROWS, OUT_ROWS)
            outs = _adamw(w, g, row0, rows_of(mom1, n), rows_of(mom2, n), tile, "adamw_" + n)
            row0 += w.shape[0]
            for dst, t in zip((out_g, out_d, out_m, out_v), outs):
                dst[n] = (jnp.swapaxes(t, 0, 1) if n in TRANSPOSED else t).reshape(weights[n].shape)
    small_outs = _adamw(_pack_small(weights), small_sum, 0, _pack_small(mom1), _pack_small(mom2), SMALL_ROWS, "adamw_small")
    for dst, packed in zip((out_g, out_d, out_m, out_v), small_outs):
        dst.update(_unpack_small(packed, weights))

    return (loss,grad_x.reshape(x.shape), *[out_g[n] for n in names], *[out_d[n] for n in names],
            *[out_m[n] for n in names], *[out_v[n] for n in names])
```

```python
import collections
import functools
import math

import jax
import jax.numpy as jnp
from jax import lax
from jax.experimental import pallas as pl
from jax.experimental.pallas import tpu as pltpu

F32, BF16 = jnp.float32, jnp.bfloat16
MESH = pl.DeviceIdType.MESH

D_MODEL = 1024
D_FF = 2816
N_CHIPS = 4
FF_CHUNK = D_FF // N_CHIPS
HEAD_DIM = 64
N_HEADS = 8
N_KV = 2
Q_PER_KV = N_HEADS // N_KV
KV_WIDTH = N_KV * HEAD_DIM
ATTN_WIDTH = N_HEADS * HEAD_DIM
POOL_WINDOWS = (2, 4, 8, 16)
N_POOL = len(POOL_WINDOWS)
POOL_GROUP = 128
POOL_WIDTH = N_POOL * POOL_GROUP
IN_WIDTH = ATTN_WIDTH + 2 * KV_WIDTH + POOL_WIDTH
WINDOW = 128
BLOCK = 128
BAND = 3 * BLOCK
ROPE_THETA = 500000.0
ROTARY_DIM = HEAD_DIM // 4
EPS = 1e-6
LANES = 128
Q_PAD = N_HEADS * LANES
U_PAD = Q_PAD + 2 * KV_WIDTH + POOL_WIDTH
SCALE = HEAD_DIM ** -0.5
NEG = -1e30

ADAM_LR, ADAM_B1, ADAM_B2, ADAM_EPS, ADAM_WD, ADAM_STEP = 0.001, 0.9, 0.999, 1e-08, 0.01, 10

V7X_VMEM_BYTES = 64 * 1024 * 1024
TOK_TILE = 512
SUBLANES = 8
SMALL_ROWS = 568


def _params(sem, vmem_mb):
    assert vmem_mb * 1024 * 1024 <= V7X_VMEM_BYTES
    return pltpu.CompilerParams(dimension_semantics=sem, vmem_limit_bytes=vmem_mb * 1024 * 1024)


def _dot(a, b):
    return lax.dot_general(a, b, (((1,), (0,)), ((), ())), preferred_element_type=F32)


def _dot_nt(a, b):
    return lax.dot_general(a, b, (((1,), (1,)), ((), ())), preferred_element_type=F32)


def _dot_tn(a, b):
    return lax.dot_general(a, b, (((0,), (0,)), ((), ())), preferred_element_type=F32)


def _rms_stats(h):
    r = lax.rsqrt(jnp.mean(h * h, axis=-1, keepdims=True) + EPS)
    return r, h * r


def _rms_bwd(dn, g, r, xh):
    gd = dn * g
    dh = r * (gd - xh * jnp.mean(gd * xh, axis=-1, keepdims=True))
    return dh, jnp.sum(dn * xh, axis=0, keepdims=True)


def _rope(x, c, s1, s2):
    return x * c + pltpu.roll(x, LANES - ROTARY_DIM // 2, 1) * s1 + pltpu.roll(x, ROTARY_DIM // 2, 1) * s2


def _rope_bwd(d, c, s1, s2):
    return d * c + pltpu.roll(d * s1, ROTARY_DIM // 2, 1) + pltpu.roll(d * s2, LANES - ROTARY_DIM // 2, 1)


def _sum_chunks(refs):
    terms = [ref[j].astype(F32) for ref in refs for j in range(ref.shape[0])]
    return functools.reduce(lambda a, b: a + b, terms)


def _chunk_rows(tile, k):
    return pl.BlockSpec((k, tile, D_MODEL), lambda i, *_: (0, i, 0))


def _full(shape):
    nd = len(shape)
    return pl.BlockSpec(shape, lambda *_: (0,) * nd)


def _rows(tile, cols):
    return pl.BlockSpec((tile, cols), lambda i, *_: (i, 0))


HBM_SPEC = pl.BlockSpec(memory_space=pltpu.HBM)

_Rider = collections.namedtuple("_Rider", "operands out_shapes aliases scratch start finish hooks", defaults=[()])


_NO_RIDER = _Rider([], [], {}, [], None, None)


def _call(body, name, grid, in_specs, out_specs, out_shape, scratch, vmem_mb, args, rider=None, prefetch=(),
          shares_rider_refs=False):
    rider = rider or _NO_RIDER
    n_pre, n_in, n_out, n_scr = len(prefetch), len(in_specs), len(out_specs), len(scratch)
    r_in, r_out = len(rider.operands), len(rider.out_shapes)

    def fused(*refs):
        pre, refs = refs[:n_pre], refs[n_pre:]
        ins, refs = refs[:n_in], refs[n_in:]
        r_ins, refs = refs[:r_in], refs[r_in:]
        outs, refs = refs[:n_out], refs[n_out:]
        r_outs, refs = refs[:r_out], refs[r_out:]
        scr, r_scr = refs[:n_scr], refs[n_scr:]
        ids = [pl.program_id(d) for d in range(len(grid))]
        if rider.start is not None:
            @pl.when(functools.reduce(jnp.logical_and, [i == 0 for i in ids]))
            def _():
                rider.start(r_ins, r_outs, r_scr)

        for at, hook in rider.hooks:
            @pl.when(functools.reduce(jnp.logical_and, [i == a for i, a in zip(ids, at)]))
            def _(hook=hook):
                hook(r_ins, r_outs, r_scr)

        if shares_rider_refs:
            body(*pre, *ins, *outs, *scr, rider_refs=r_outs)
        else:
            body(*pre, *ins, *outs, *scr)

        if rider.finish is not None:
            @pl.when(functools.reduce(jnp.logical_and, [i == g - 1 for i, g in zip(ids, grid)]))
            def _():
                rider.finish(r_ins, r_outs, r_scr)

    return pl.pallas_call(
        fused, name=name,
        grid_spec=pltpu.PrefetchScalarGridSpec(
            num_scalar_prefetch=n_pre, grid=grid,
            in_specs=list(in_specs) + [HBM_SPEC] * r_in, out_specs=list(out_specs) + [HBM_SPEC] * r_out,
            scratch_shapes=list(scratch) + list(rider.scratch)),
        out_shape=list(out_shape) + list(rider.out_shapes),
        input_output_aliases={n_pre + n_in + k: n_out + v for k, v in rider.aliases.items()},
        compiler_params=_params(("arbitrary",) * len(grid), vmem_mb),
    )(*prefetch, *args, *rider.operands)


def _after(token):
    return _Rider([token], [], {}, [], lambda *_: None, lambda *_: None)


def _comm_call(rider, name):
    r_in, r_out = len(rider.operands), len(rider.out_shapes)

    def body(*refs):
        r_ins, r_outs, r_scr = refs[:r_in], refs[r_in:r_in + r_out], refs[r_in + r_out:]
        rider.start(r_ins, r_outs, r_scr)
        rider.finish(r_ins, r_outs, r_scr)

    return pl.pallas_call(
        body, name=name, in_specs=[HBM_SPEC] * r_in, out_specs=[HBM_SPEC] * r_out, out_shape=list(rider.out_shapes),
        input_output_aliases=dict(rider.aliases), scratch_shapes=list(rider.scratch),
    )(*rider.operands)


def _ffn_fwd(order, h, gain, name, rider, group_at, loss_head=None):
    S = h.shape[0]
    tile = min(TOK_TILE, S)
    nt = S // tile
    last = N_CHIPS - 1
    n_extra_in, n_head_out = (2, 4) if loss_head else (0, 1)

    def body(order_ref, h_ref, g_ref, *refs, rider_refs):
        extra_in, refs = refs[:n_extra_in], refs[n_extra_in:]
        head_out, (n_ref, gate_ref, up_ref, w_scr, w_sem, acc, n_scr) = refs[:n_head_out], refs[n_head_out:]
        j, i = pl.program_id(0), pl.program_id(1)

        @pl.when(i == 0)
        def _():
            fetch = pltpu.make_async_copy(rider_refs[group_at].at[order_ref[j]], w_scr, w_sem)
            fetch.start()
            fetch.wait()

        at = pl.multiple_of(i * tile, tile)

        @pl.when(j == 0)
        def _():
            _, xh = _rms_stats(h_ref[...])
            n = (xh * g_ref[...]).astype(BF16)
            n_scr[pl.ds(at, tile), :] = n
            n_ref[...] = n
            acc[pl.ds(at, tile), :] = jnp.zeros((tile, D_MODEL), F32)

        half = tile // 2
        wg, wu, wd = (w_scr[part * FF_CHUNK:(part + 1) * FF_CHUNK, :] for part in range(3))
        ns = [n_scr[pl.ds(at + s * half, half), :] for s in range(2)]
        gates = [_dot_nt(n, wg) for n in ns]
        ups = [_dot_nt(n, wu) for n in ns]
        acts = [(g * jax.nn.sigmoid(g) * u).astype(BF16) for g, u in zip(gates, ups)]
        for s in range(2):
            gate_ref[0, s * half:(s + 1) * half, :] = gates[s].astype(BF16)
            up_ref[0, s * half:(s + 1) * half, :] = ups[s].astype(BF16)
            acc[pl.ds(at + s * half, half), :] += _dot(acts[s], wd)

        @pl.when(j == last)
        def _():
            out = h_ref[...] + 0.5 * acc[pl.ds(at, tile), :]
            if not loss_head:
                head_out[0][...] = out
                return
            (t_ref, gf_ref), (dh_ref, dhalf_ref, loss_ref, dg_ref) = extra_in, head_out

            @pl.when(i == 0)
            def _():
                loss_ref[...] = jnp.zeros_like(loss_ref)
                dg_ref[...] = jnp.zeros_like(dg_ref)

            gf = gf_ref[...]
            r, xh = _rms_stats(out)
            err = xh * gf - t_ref[...]
            loss_ref[...] += (0.5 / D_MODEL) * jnp.sum(err * err, axis=0, keepdims=True)
            dh, dg = _rms_bwd(err * (1.0 / D_MODEL), gf, r, xh)
            dg_ref[...] += dg
            dh_ref[...] = dh
            dhalf_ref[...] = (0.5 * dh).astype(BF16)

    tok = pl.BlockSpec((tile, D_MODEL), lambda j, i, order_ref: (i, 0))
    hid = pl.BlockSpec((1, tile, FF_CHUNK), lambda j, i, order_ref: (order_ref[j], i, 0))
    row = pl.BlockSpec((1, D_MODEL), lambda j, i, order_ref: (0, 0))
    in_last = pl.BlockSpec((tile, D_MODEL), lambda j, i, order_ref: (jnp.where(j == last, i, 0), 0))
    in_first = pl.BlockSpec((tile, D_MODEL), lambda j, i, order_ref: (jnp.where(j == 0, i, nt - 1), 0))
    tok_f32, tok_bf16, lanes = (jax.ShapeDtypeStruct((S, D_MODEL), F32), jax.ShapeDtypeStruct((S, D_MODEL), BF16),
                                jax.ShapeDtypeStruct((1, D_MODEL), F32))
    hidden = jax.ShapeDtypeStruct((N_CHIPS, S, FF_CHUNK), BF16)
    if loss_head:
        extra_specs, extra_args = [in_last, row], list(loss_head)
        head_specs, head_shapes = [in_last, in_last, row, row], [tok_f32, tok_bf16, lanes, lanes]
    else:
        extra_specs, extra_args, head_specs, head_shapes = [], [], [in_last], [tok_f32]
    return _call(
        body, name, (N_CHIPS, nt), [tok, row] + extra_specs, head_specs + [in_first, hid, hid],
        head_shapes + [tok_bf16, hidden, hidden],
        [pltpu.VMEM((3 * FF_CHUNK, D_MODEL), BF16), pltpu.SemaphoreType.DMA, pltpu.VMEM((S, D_MODEL), F32),
         pltpu.VMEM((S, D_MODEL), BF16)], 58, (h, gain, *extra_args), rider, (order,), shares_rider_refs=True)


def _ffn_bwd(chunks, d_out, n, gate, up, group, name, rider=None):
    S = n.shape[0]
    n_chunks = chunks.shape[0]
    tile = min(TOK_TILE, S)
    nt = S // tile
    half_rows = 3 * FF_CHUNK // 2
    cut = FF_CHUNK // 2

    def body(chunks_ref, do_ref, n_ref, gate_ref, up_ref, wg_ref, wu_ref, wd_ref, dn_ref, dw_ref, acc_g, acc_u, acc_d):
        j, i = pl.program_id(0), pl.program_id(1)

        @pl.when(i == 0)
        def _():
            acc_g[...] = jnp.zeros_like(acc_g)
            acc_u[...] = jnp.zeros_like(acc_u)
            acc_d[...] = jnp.zeros_like(acc_d)

        halves = [pl.ds(s * (tile // 2), tile // 2) for s in range(2)]
        dos = [do_ref[rows, :] for rows in halves]
        d_acts = [_dot_nt(do, wd_ref[0]) for do in dos]
        gs = [gate_ref[0, rows, :].astype(F32) for rows in halves]
        us = [up_ref[0, rows, :].astype(F32) for rows in halves]
        sigs = [jax.nn.sigmoid(g) for g in gs]
        silus = [g * sig for g, sig in zip(gs, sigs)]
        d_ups = [(d_act * silu).astype(BF16) for d_act, silu in zip(d_acts, silus)]
        d_gates = [(d_act * u * (sig * (1.0 + g * (1.0 - sig)))).astype(BF16) for d_act, u, sig, g in zip(d_acts, us, sigs, gs)]
        for rows, d_gate, d_up in zip(halves, d_gates, d_ups):
            dn_ref[0, rows, :] = (_dot(d_gate, wg_ref[0]) + _dot(d_up, wu_ref[0])).astype(BF16)
        d_gate, d_up = jnp.concatenate(d_gates, axis=0), jnp.concatenate(d_ups, axis=0)
        act = jnp.concatenate([(silu * u).astype(BF16) for silu, u in zip(silus, us)], axis=0)
        nn = n_ref[...]
        acc_g[...] += _dot_tn(d_gate, nn)
        acc_u[...] += _dot_tn(d_up, nn)
        acc_d[...] += _dot_tn(act, do_ref[...])

        @pl.when(i == nt - 1)
        def _():
            dw_ref[0, 0, :FF_CHUNK, :] = acc_g[...].astype(BF16)
            dw_ref[0, 0, FF_CHUNK:, :] = acc_u[:cut, :].astype(BF16)
            dw_ref[1, 0, :cut, :] = acc_u[cut:, :].astype(BF16)
            dw_ref[1, 0, cut:, :] = acc_d[...].astype(BF16)

    tok = pl.BlockSpec((tile, D_MODEL), lambda j, i, chunks_ref: (i, 0))
    hid = pl.BlockSpec((1, tile, FF_CHUNK), lambda j, i, chunks_ref: (chunks_ref[j], i, 0))
    return _call(
        body, name, (n_chunks, nt),
        [tok, tok, hid, hid]
        + [pl.BlockSpec((1, FF_CHUNK, D_MODEL), functools.partial(lambda j, i, chunks_ref, part: (chunks_ref[j], part, 0), part=part))
           for part in range(3)],
        [pl.BlockSpec((1, tile, D_MODEL), lambda j, i, chunks_ref: (j, i, 0)),
         pl.BlockSpec((2, 1, half_rows, D_MODEL), lambda j, i, chunks_ref: (0, j, 0, 0))],
        [jax.ShapeDtypeStruct((n_chunks, S, D_MODEL), BF16), jax.ShapeDtypeStruct((2, n_chunks, half_rows, D_MODEL), BF16)],
        [pltpu.VMEM((FF_CHUNK, D_MODEL), F32)] * 3, 56, (d_out, n, gate, up, group, group, group), rider, (chunks,))


def _mix_in(h, gain, w_in, rc, rs1, rs2, name):
    S = h.shape[0]
    tile = min(TOK_TILE, S)

    def body(h_ref, g_ref, w_ref, c_ref, s1_ref, s2_ref, n_ref, q_ref, k_ref, v_ref, pc_ref):
        _, xh = _rms_stats(h_ref[...])
        n = (xh * g_ref[...]).astype(BF16)
        n_ref[...] = n
        u = _dot_nt(n, w_ref[...])
        c, s1, s2 = c_ref[...], s1_ref[...], s2_ref[...]
        q_ref[...] = jnp.concatenate([(_rope(u[:, hd * LANES:(hd + 1) * LANES], c, s1, s2) * SCALE).astype(BF16)
                                      for hd in range(N_HEADS)], axis=1)
        k_ref[...] = _rope(u[:, Q_PAD:Q_PAD + KV_WIDTH], c, s1, s2).astype(BF16)
        v_ref[...] = u[:, Q_PAD + KV_WIDTH:Q_PAD + 2 * KV_WIDTH].astype(BF16)
        pc_ref[...] = u[:, Q_PAD + 2 * KV_WIDTH:]

    return pl.pallas_call(
        body, name=name, grid=(S // tile,),
        in_specs=[_rows(tile, D_MODEL), _full((1, D_MODEL)), _full((U_PAD, D_MODEL)),
                  _rows(tile, LANES), _rows(tile, LANES), _rows(tile, LANES)],
        out_specs=[_rows(tile, D_MODEL), _rows(tile, Q_PAD), _rows(tile, KV_WIDTH), _rows(tile, KV_WIDTH),
                   _rows(tile, POOL_WIDTH)],
        out_shape=[jax.ShapeDtypeStruct((S, D_MODEL), BF16), jax.ShapeDtypeStruct((S, Q_PAD), BF16),
                   jax.ShapeDtypeStruct((S, KV_WIDTH), BF16), jax.ShapeDtypeStruct((S, KV_WIDTH), BF16),
                   jax.ShapeDtypeStruct((S, POOL_WIDTH), F32)],
        compiler_params=_params(("parallel",), 40),
    )(h, gain, w_in, rc, rs1, rs2)


def _band_start(i, S):
    return pl.multiple_of(jnp.clip((i - 1) * BLOCK, 0, S - BAND), BLOCK)


def _window_bias(off):
    r = lax.broadcasted_iota(jnp.int32, (BLOCK, 1), 0)
    c = lax.broadcasted_iota(jnp.int32, (1, BAND), 1)
    return jnp.where(jnp.abs(off + r - c) <= WINDOW, 0.0, NEG).astype(F32)


def _softmax_parts(qh, kb, bias, sink_h):
    s = _dot_nt(qh, kb) + bias
    m = jnp.maximum(jnp.max(s, axis=-1, keepdims=True), sink_h)
    p = jnp.exp(s - m)
    es = jnp.exp(sink_h - m)
    return p, es, 1.0 / (jnp.sum(p, axis=-1, keepdims=True) + es)


def _pool_matrix(t0, start, S, w):
    r = lax.broadcasted_iota(jnp.int32, (BLOCK, 1), 0) + t0
    c = lax.broadcasted_iota(jnp.int32, (1, BAND), 1) + start
    half = w // 2

    def window(lo, hi):
        a = jnp.maximum(lo, 0)
        b = jnp.minimum(hi + 1, S)
        return jnp.where((c >= a) & (c < b), 1.0 / (b - a).astype(F32), 0.0)

    return (0.5 * (window(r - half, r + half - 1) + window(r - half + 1, r + half))).astype(BF16)


def _pool_matrices(S):
    blocks = ((0, 0), (BLOCK, 0), (S - BLOCK, S - BAND))
    return jnp.stack([jnp.stack([_pool_matrix(t0, start, S, w) for w in POOL_WINDOWS]) for t0, start in blocks])


def _pool_spec(nb):
    return pl.BlockSpec((1, N_POOL, BLOCK, BAND), lambda i, *_: (jnp.where(i == 0, 0, jnp.where(i == nb - 1, 2, 1)), 0, 0, 0))


def _mix_core_fwd(q, k, v, pc, sink, pool_m, pool_w, pool_scale, name):
    S = q.shape[0]
    nb = S // BLOCK

    def body(sink_ref, q_ref, k_ref, v_ref, pc_ref, pm_ref, pw_ref, ps_ref, a_ref, p_ref):
        i = pl.program_id(0)
        start = _band_start(i, S)
        band = pl.ds(start, BAND)
        bias = _window_bias(i * BLOCK - start)
        kb, vb = k_ref[band, :], v_ref[band, :]
        hs = range(N_HEADS)
        ss = [_dot_nt(q_ref[:, hd * LANES:(hd + 1) * LANES], kb) + bias for hd in hs]
        ms = [jnp.maximum(jnp.max(ss[hd], axis=-1, keepdims=True), sink_ref[0, hd]) for hd in hs]
        ps = [jnp.exp(ss[hd] - ms[hd]) for hd in hs]
        invs = [1.0 / (jnp.sum(ps[hd], axis=-1, keepdims=True) + jnp.exp(sink_ref[0, hd] - ms[hd])) for hd in hs]
        outs = [_dot(ps[hd].astype(BF16), vb) for hd in hs]
        a_ref[...] = jnp.concatenate([(outs[hd] * invs[hd]).astype(BF16) for hd in hs], axis=1)
        centre = pl.ds(pl.multiple_of(i * BLOCK, BLOCK), BLOCK)
        gs = range(N_POOL)
        sl = [slice(g * POOL_GROUP, (g + 1) * POOL_GROUP) for g in gs]
        means = [_dot(pm_ref[0, g], pc_ref[band, sl[g]].astype(BF16)) for g in gs]
        devs = [(means[g] - pc_ref[centre, sl[g]]).astype(BF16) for g in gs]
        p_ref[...] = (jnp.concatenate([_dot(devs[g], pw_ref[g]) for g in gs], axis=1) * ps_ref[...]).astype(BF16)

    return _call(
        body, name, (nb,),
        [pl.BlockSpec(memory_space=pltpu.SMEM), _rows(BLOCK, Q_PAD), _full((S, KV_WIDTH)), _full((S, KV_WIDTH)),
         _full((S, POOL_WIDTH)), _pool_spec(nb), _full((N_POOL, POOL_GROUP, POOL_GROUP)), _full((1, POOL_WIDTH))],
        [_rows(BLOCK, Q_PAD), _rows(BLOCK, POOL_WIDTH)],
        [jax.ShapeDtypeStruct((S, Q_PAD), BF16), jax.ShapeDtypeStruct((S, POOL_WIDTH), BF16)],
        [], 40, (sink, q, k, v, pc, pool_m, pool_w, pool_scale))


def _mix_core_bwd(q, k, v, pc, da, dp, sink, pool_m, pool_w, pool_scale, rc, rs1, rs2, name):
    S = q.shape[0]
    nb = S // BLOCK

    def body(sink_ref, q_ref, k_ref, v_ref, pc_ref, da_ref, dp_ref, pm_ref, pw_ref, ps_ref, c_ref, s1_ref, s2_ref,
             dq_ref, dk_ref, dv_ref, dpc_ref, dsink_ref, dpw_ref, dps_ref):
        i = pl.program_id(0)

        @pl.when(i == 0)
        def _():
            dk_ref[...] = jnp.zeros_like(dk_ref)
            dv_ref[...] = jnp.zeros_like(dv_ref)
            dpc_ref[...] = jnp.zeros_like(dpc_ref)
            dsink_ref[...] = jnp.zeros_like(dsink_ref)
            dpw_ref[...] = jnp.zeros_like(dpw_ref)
            dps_ref[...] = jnp.zeros_like(dps_ref)

        start = _band_start(i, S)
        band = pl.ds(start, BAND)
        bias = _window_bias(i * BLOCK - start)
        kb, vb = k_ref[band, :], v_ref[band, :]
        c, s1, s2 = c_ref[...], s1_ref[...], s2_ref[...]
        lane = lax.broadcasted_iota(jnp.int32, (1, LANES), 1)
        hs = range(N_HEADS)
        qs = [q_ref[:, hd * LANES:(hd + 1) * LANES] for hd in hs]
        das = [da_ref[:, hd * LANES:(hd + 1) * LANES] for hd in hs]
        ss = [_dot_nt(qs[hd], kb) + bias for hd in hs]
        d_probs = [_dot_nt(das[hd], vb) for hd in hs]
        ms = [jnp.maximum(jnp.max(ss[hd], axis=-1, keepdims=True), sink_ref[0, hd]) for hd in hs]
        ps = [jnp.exp(ss[hd] - ms[hd]) for hd in hs]
        ess = [jnp.exp(sink_ref[0, hd] - ms[hd]) for hd in hs]
        invs = [1.0 / (jnp.sum(ps[hd], axis=-1, keepdims=True) + ess[hd]) for hd in hs]
        probs = [ps[hd] * invs[hd] for hd in hs]
        deltas = [jnp.sum(probs[hd] * d_probs[hd], axis=-1, keepdims=True) for hd in hs]
        d_ss = [(probs[hd] * (d_probs[hd] - deltas[hd])).astype(BF16) for hd in hs]
        dqs = [_dot(d_ss[hd], kb) for hd in hs]
        dq_ref[...] = jnp.concatenate([_rope_bwd(dqs[hd] * SCALE, c, s1, s2).astype(BF16) for hd in hs], axis=1)
        dks = [_dot_tn(d_ss[hd], qs[hd]) for hd in hs]
        dvs = [_dot_tn(probs[hd].astype(BF16), das[hd]) for hd in hs]
        dk_ref[band, :] += functools.reduce(lambda a, b: a + b, dks)
        dv_ref[band, :] += functools.reduce(lambda a, b: a + b, dvs)
        dsink_ref[...] += functools.reduce(lambda a, b: a + b, [
            jnp.where(lane == hd, -jnp.sum(ess[hd] * invs[hd] * deltas[hd], axis=0, keepdims=True), 0.0) for hd in hs])

        centre = pl.ds(pl.multiple_of(i * BLOCK, BLOCK), BLOCK)
        gs = range(N_POOL)
        sl = [slice(g * POOL_GROUP, (g + 1) * POOL_GROUP) for g in gs]
        devs = [(_dot(pm_ref[0, g], pc_ref[band, sl[g]].astype(BF16)) - pc_ref[centre, sl[g]]).astype(BF16) for g in gs]
        dys = [dp_ref[:, sl[g]].astype(F32) for g in gs]
        zs = [_dot(devs[g], pw_ref[g]) for g in gs]
        dzs = [(dys[g] * ps_ref[:, sl[g]]).astype(BF16) for g in gs]
        d_devs = [_dot_nt(dzs[g], pw_ref[g]) for g in gs]
        dps_ref[...] += jnp.concatenate([jnp.sum(dys[g] * zs[g], axis=0, keepdims=True) for g in gs], axis=1)
        for g in gs:
            dpw_ref[g] += _dot_tn(devs[g], dzs[g])
        dpc_ref[band, :] += jnp.concatenate([_dot_tn(pm_ref[0, g], d_devs[g].astype(BF16)) for g in gs], axis=1)
        dpc_ref[centre, :] -= jnp.concatenate(d_devs, axis=1)

    return _call(
        body, name, (nb,),
        [pl.BlockSpec(memory_space=pltpu.SMEM), _rows(BLOCK, Q_PAD), _full((S, KV_WIDTH)), _full((S, KV_WIDTH)),
         _full((S, POOL_WIDTH)), _rows(BLOCK, Q_PAD), _rows(BLOCK, POOL_WIDTH), _pool_spec(nb),
         _full((N_POOL, POOL_GROUP, POOL_GROUP)), _full((1, POOL_WIDTH)),
         _rows(BLOCK, LANES), _rows(BLOCK, LANES), _rows(BLOCK, LANES)],
        [_rows(BLOCK, Q_PAD), _full((S, KV_WIDTH)), _full((S, KV_WIDTH)), _full((S, POOL_WIDTH)),
         _full((1, LANES)), _full((N_POOL, POOL_GROUP, POOL_GROUP)), _full((1, POOL_WIDTH))],
        [jax.ShapeDtypeStruct((S, Q_PAD), BF16), jax.ShapeDtypeStruct((S, KV_WIDTH), F32),
         jax.ShapeDtypeStruct((S, KV_WIDTH), F32), jax.ShapeDtypeStruct((S, POOL_WIDTH), F32),
         jax.ShapeDtypeStruct((1, LANES), F32), jax.ShapeDtypeStruct((N_POOL, POOL_GROUP, POOL_GROUP), F32),
         jax.ShapeDtypeStruct((1, POOL_WIDTH), F32)],
        [], 56, (sink, q, k, v, pc, da, dp, pool_m, pool_w, pool_scale, rc, rs1, rs2))


def _mix_out(h, a, p, wa, wp, name):
    S = h.shape[0]
    tile = min(TOK_TILE, S)

    def body(h_ref, a_ref, p_ref, wa_ref, wp_ref, o_ref):
        o_ref[...] = h_ref[...] + _dot(a_ref[...], wa_ref[...]) + _dot(p_ref[...], wp_ref[...])

    return pl.pallas_call(
        body, name=name, grid=(S // tile,),
        in_specs=[_rows(tile, D_MODEL), _rows(tile, Q_PAD), _rows(tile, POOL_WIDTH),
                  _full((Q_PAD, D_MODEL)), _full((POOL_WIDTH, D_MODEL))],
        out_specs=_rows(tile, D_MODEL),
        out_shape=jax.ShapeDtypeStruct((S, D_MODEL), F32),
        compiler_params=_params(("parallel",), 40),
    )(h, a, p, wa, wp)


def _mix_out_bwd(dh_out, dn, h, gain, a, p, wa, wp, name, rider=None):
    S = h.shape[0]
    tile = min(TOK_TILE, S)

    def body(do_ref, dn_ref, h_ref, g_ref, a_ref, p_ref, wa_ref, wp_ref, dh_ref, da_ref, dp_ref, dwa_ref, dwp_ref, dg_ref):
        @pl.when(pl.program_id(0) == 0)
        def _():
            dwa_ref[...] = jnp.zeros_like(dwa_ref)
            dwp_ref[...] = jnp.zeros_like(dwp_ref)
            dg_ref[...] = jnp.zeros_like(dg_ref)

        r, xh = _rms_stats(h_ref[...])
        dnorm, dg = _rms_bwd(_sum_chunks([dn_ref]), g_ref[...], r, xh)
        dh = do_ref[...] + dnorm
        dg_ref[...] += dg
        dh_ref[...] = dh
        dhb = dh.astype(BF16)
        da_ref[...] = _dot_nt(dhb, wa_ref[...]).astype(BF16)
        dp_ref[...] = _dot_nt(dhb, wp_ref[...]).astype(BF16)
        dwa_ref[...] += _dot_tn(a_ref[...], dhb)
        dwp_ref[...] += _dot_tn(p_ref[...], dhb)

    return _call(
        body, name, (S // tile,),
        [_rows(tile, D_MODEL), _chunk_rows(tile, dn.shape[0]), _rows(tile, D_MODEL), _full((1, D_MODEL)),
         _rows(tile, Q_PAD), _rows(tile, POOL_WIDTH), _full((Q_PAD, D_MODEL)), _full((POOL_WIDTH, D_MODEL))],
        [_rows(tile, D_MODEL), _rows(tile, Q_PAD), _rows(tile, POOL_WIDTH),
         _full((Q_PAD, D_MODEL)), _full((POOL_WIDTH, D_MODEL)), _full((1, D_MODEL))],
        [jax.ShapeDtypeStruct((S, D_MODEL), F32), jax.ShapeDtypeStruct((S, Q_PAD), BF16),
         jax.ShapeDtypeStruct((S, POOL_WIDTH), BF16), jax.ShapeDtypeStruct((Q_PAD, D_MODEL), F32),
         jax.ShapeDtypeStruct((POOL_WIDTH, D_MODEL), F32), jax.ShapeDtypeStruct((1, D_MODEL), F32)],
        [], 48, (dh_out, dn, h, gain, a, p, wa, wp), rider)


def _mix_in_bwd(dh_out, h, gain, n, dq, dk, dv, dpc, rc, rs1, rs2, w_in, name):
    S = h.shape[0]
    tile = min(TOK_TILE, S)

    def body(do_ref, h_ref, g_ref, n_ref, dq_ref, dk_ref, dv_ref, dpc_ref, c_ref, s1_ref, s2_ref, w_ref,
             dh_ref, dhalf_ref, dw_ref, dg_ref):
        @pl.when(pl.program_id(0) == 0)
        def _():
            dw_ref[...] = jnp.zeros_like(dw_ref)
            dg_ref[...] = jnp.zeros_like(dg_ref)

        dk = _rope_bwd(dk_ref[...], c_ref[...], s1_ref[...], s2_ref[...]).astype(BF16)
        du = jnp.concatenate([dq_ref[...], dk, dv_ref[...].astype(BF16), dpc_ref[...].astype(BF16)], axis=1)
        dn = _dot(du, w_ref[...])
        dw_ref[...] += _dot_tn(du, n_ref[...])
        r, xh = _rms_stats(h_ref[...])
        dnorm, dg = _rms_bwd(dn, g_ref[...], r, xh)
        dh = do_ref[...] + dnorm
        dg_ref[...] += dg
        dh_ref[...] = dh
        dhalf_ref[...] = (0.5 * dh).astype(BF16)

    return _call(
        body, name, (S // tile,),
        [_rows(tile, D_MODEL), _rows(tile, D_MODEL), _full((1, D_MODEL)), _rows(tile, D_MODEL),
         _rows(tile, Q_PAD), _rows(tile, KV_WIDTH), _rows(tile, KV_WIDTH), _rows(tile, POOL_WIDTH),
         _rows(tile, LANES), _rows(tile, LANES), _rows(tile, LANES), _full((U_PAD, D_MODEL))],
        [_rows(tile, D_MODEL), _rows(tile, D_MODEL), _full((U_PAD, D_MODEL)), _full((1, D_MODEL))],
        [jax.ShapeDtypeStruct((S, D_MODEL), F32), jax.ShapeDtypeStruct((S, D_MODEL), BF16),
         jax.ShapeDtypeStruct((U_PAD, D_MODEL), F32), jax.ShapeDtypeStruct((1, D_MODEL), F32)],
        [], 56, (dh_out, h, gain, n, dq, dk, dv, dpc, rc, rs1, rs2, w_in))


def _norm_bwd(dh_out, dns, h, gain, name):
    S = h.shape[0]
    tile = min(TOK_TILE, S)
    n = len(dns)

    def body(do_ref, *refs):
        h_ref, g_ref, dh_ref, dg_ref = refs[n:]

        @pl.when(pl.program_id(0) == 0)
        def _():
            dg_ref[...] = jnp.zeros_like(dg_ref)

        r, xh = _rms_stats(h_ref[...])
        dnorm, dg = _rms_bwd(_sum_chunks(refs[:n]), g_ref[...], r, xh)
        dg_ref[...] += dg
        dh_ref[...] = do_ref[...] + dnorm

    return _call(
        body, name, (S // tile,),
        [_rows(tile, D_MODEL)] + [_chunk_rows(tile, dn.shape[0]) for dn in dns] + [_rows(tile, D_MODEL), _full((1, D_MODEL))],
        [_rows(tile, D_MODEL), _full((1, D_MODEL))],
        [jax.ShapeDtypeStruct((S, D_MODEL), F32), jax.ShapeDtypeStruct((1, D_MODEL), F32)],
        [], 40, (dh_out, *dns, h, gain))


def _rope_tables(S):
    half = ROTARY_DIM // 2
    inv_freq = ROPE_THETA ** (-jnp.arange(0, ROTARY_DIM, 2, dtype=F32) / ROTARY_DIM)
    dim = jnp.arange(LANES) % HEAD_DIM
    ang = jnp.arange(S, dtype=F32)[:, None] * inv_freq[dim % half][None, :]
    lo, hi = (dim < half)[None, :], ((dim >= half) & (dim < ROTARY_DIM))[None, :]
    c = jnp.where(lo | hi, jnp.cos(ang), 1.0)
    s1 = jnp.where(lo, -jnp.sin(ang), 0.0)
    s2 = jnp.where(hi, jnp.sin(ang), 0.0)
    return c, s1, s2


def _pad_heads(w, axis):
    w = jnp.moveaxis(w, axis, 0)
    heads = w.reshape((N_HEADS, HEAD_DIM) + w.shape[1:])
    zero = jnp.zeros_like(heads)
    first = (jnp.arange(N_HEADS) < Q_PER_KV).reshape((N_HEADS, 1) + (1,) * (w.ndim - 1))
    lo = jnp.where(first, heads, zero)
    hi = jnp.where(first, zero, heads)
    padded = jnp.concatenate([lo, hi], axis=1).reshape((Q_PAD,) + w.shape[1:])
    return jnp.moveaxis(padded, 0, axis)


def _unpad_heads(w, axis):
    w = jnp.moveaxis(w, axis, 0)
    groups = w.reshape((N_HEADS, 2, HEAD_DIM) + w.shape[1:])
    first = (jnp.arange(N_HEADS) < Q_PER_KV).reshape((N_HEADS, 1) + (1,) * (w.ndim - 1))
    heads = jnp.where(first, groups[:, 0], groups[:, 1]).reshape((ATTN_WIDTH,) + w.shape[1:])
    return jnp.moveaxis(heads, 0, axis)


IN_ROWS = IN_WIDTH // N_CHIPS
OUT_ROWS = (ATTN_WIDTH + POOL_WIDTH) // N_CHIPS
MIX_ROWS = IN_ROWS + OUT_ROWS
FFN_ROWS = 3 * FF_CHUNK


def _step(x, target, bufs, small, place, update):
    S = x.shape[0]
    rc, rs1, rs2 = _rope_tables(S)
    mine = place[1]
    order = jnp.stack([mine, mine ^ 2, mine ^ 1, mine ^ 3])
    h1, n1, gate1, up1, ffn1, mix = _ffn_fwd(order, x, small["ffn1_norm"], "ffn1_fwd",
                                             _merge(_allgather(bufs[:1], in_passes=0), _allgather(bufs[1:2])), 0)
    in_flight, (ffn2,), token = _split_start(bufs[2:], 2, _gather_copies(X_Y_DIAGONAL[:2]), "ffn2_gather_start")
    w_in_t = mix[:, :IN_ROWS].reshape(IN_WIDTH, D_MODEL)
    w_in_pad = jnp.concatenate([_pad_heads(w_in_t[:ATTN_WIDTH], 0), w_in_t[ATTN_WIDTH:]], axis=0)
    w_out = mix[:, IN_ROWS:].reshape(ATTN_WIDTH + POOL_WIDTH, D_MODEL)
    wa = _pad_heads(w_out[:ATTN_WIDTH], 0)
    wp = w_out[ATTN_WIDTH:]
    pool_w = small["pool_w"].astype(BF16)

    n2, q, k, v, pc = _mix_in(h1, small["mix_norm"] + token[0, 0], w_in_pad, rc, rs1, rs2, "mix_in")
    pool_m = _pool_matrices(S)
    a, p = _mix_core_fwd(q, k, v, pc, small["sink_logits"], pool_m, pool_w, small["pool_scale"], "mix_core_fwd")
    h2 = _mix_out(h1, a, p, wa, wp, "mix_out")
    (ffn2,) = _split_wait(in_flight, [ffn2], _gather_copies(X_Y_DIAGONAL[:2]), [h2], "ffn2_gather_wait")
    dh3, dhalf3, loss_lanes, d_final, n3, gate2, up2, ffn2 = _ffn_fwd(
        order, h2, small["ffn2_norm"], "ffn2_fwd",
        _allgather([ffn2], peers=X_Y_DIAGONAL[2:], in_passes=0, landed_before=X_Y_DIAGONAL[:2]), 0,
        loss_head=(target, small["final_norm"]))

    dn3, d_ffn2 = _ffn_bwd(jnp.arange(N_CHIPS, dtype=jnp.int32), dhalf3, n3, gate2, up2, ffn2, "ffn2_bwd")
    dh2, da, dp, dwa, dwp, d_ffn2_norm, received = _mix_out_bwd(dh3, dn3, h2, small["ffn2_norm"], a, p, wa, wp, "mix_out_bwd",
                                                                _sibling_exchange([d_ffn2]))
    pair = _pair_sum(place, d_ffn2, received, "grad_pair_sum_ffn2")
    in_flight, (pair, stack), token = _split_start([pair, received], 3, _scatter_copies, "ffn2_scatter_start")
    dq, dk, dv, dpc, dsink, dpool_w, dpool_scale = _mix_core_bwd(
        q, k, v, pc, da, dp, small["sink_logits"], pool_m, pool_w, small["pool_scale"] + token[0, 0], rc, rs1, rs2, "mix_core_bwd")
    dh1, dhalf1, dw_in_pad, d_mix_norm = _mix_in_bwd(dh2, h1, small["mix_norm"], n2, dq, dk, dv, dpc, rc, rs1, rs2, w_in_pad,
                                                     "mix_in_bwd")
    pair, stack = _split_wait(in_flight, [pair, stack], _scatter_copies, [d_mix_norm], "ffn2_scatter_wait")
    reduced_ffn2 = _chip_sum(place, pair, stack, 2, "grad_chip_sum_ffn2")
    dw_in_t = jnp.concatenate([_unpad_heads(dw_in_pad[:Q_PAD], 0), dw_in_pad[Q_PAD:]], axis=0)
    dw_out = jnp.concatenate([_unpad_heads(dwa, 0), dwp], axis=0)
    d_mix = jnp.concatenate([dw_in_t.reshape(N_CHIPS, IN_ROWS, D_MODEL), dw_out.reshape(N_CHIPS, OUT_ROWS, D_MODEL)], axis=1)
    d_mix = jnp.transpose(d_mix.reshape(N_CHIPS, 2, MIX_ROWS // 2, D_MODEL), (1, 0, 2, 3)).astype(BF16)
    small_g = {"ffn1_norm": jnp.zeros_like(d_mix_norm), "mix_norm": d_mix_norm, "ffn2_norm": d_ffn2_norm,
               "final_norm": d_final, "pool_scale": dpool_scale, "sink_logits": dsink[:, :N_HEADS], "pool_w": dpool_w}
    loss_row = jnp.sum(loss_lanes.reshape(D_MODEL // LANES, LANES), axis=0, keepdims=True)
    small_early = _pack_small(small_g, loss_row)

    chunk = [(place[1:] + 1 + p) % N_CHIPS for p in range(N_CHIPS)]
    ffn1_bwd = functools.partial(_ffn_bwd, d_out=dhalf1, n=n1, gate=gate1, up=up1, group=ffn1)
    dn_a, dw_a, recv_mix, small_all, g_ffn2 = ffn1_bwd(
        chunk[0], name="ffn1_bwd_0",
        rider=_merge(_merge(_sibling_exchange([d_mix]), _small_allgather(small_early)), _sibling_share([reduced_ffn2])))
    pair_mix = _pair_sum(place, d_mix, recv_mix, "grad_pair_sum_mix")
    dn_b, dw_b, recv_a, stack_mix = ffn1_bwd(chunk[1], name="ffn1_bwd_1",
                                             rider=_merge(_sibling_exchange([dw_a]), _scatter([pair_mix])))
    pair_a = _pair_sum(place, dw_a, recv_a, "grad_pair_sum_ffn1_0")
    reduced_mix = _chip_sum(place, pair_mix, stack_mix, 2, "grad_chip_sum_mix")
    steps = [(_scatter_step_copies(p), (p, 3)) for p in range(3)]
    sems_a, (pair_a, stack), token = _split_start([pair_a, stack], 1, steps[0][0], "ffn1_scatter_start_0")
    dn_c, dw_c, recv_b, g_mix = ffn1_bwd(
        chunk[2], name="ffn1_bwd_2", rider=_merge(_merge(_sibling_exchange([dw_b]), _sibling_share([reduced_mix])), _after(token)))
    pair_b = _pair_sum(place, dw_b, recv_b, "grad_pair_sum_ffn1_1")
    sems_b, (pair_b, stack), token = _split_start([pair_b, stack], 1, steps[1][0], "ffn1_scatter_start_1")
    dn_d, dw_d, recv_c = ffn1_bwd(chunk[3], name="ffn1_bwd_3", rider=_merge(_sibling_exchange([dw_c]), _after(token)))
    pair_c = _pair_sum(place, dw_c, recv_c, "grad_pair_sum_ffn1_2")
    own = (_exchange_copies, (4, 5))
    sems_c, (pair_c, stack, dw_d, recv_d), token = _split_start(
        [pair_c, stack, dw_d, recv_c], 2, _joined([(steps[2][0], (0, 1)), (own[0], (2, 3))]), "ffn1_scatter_start_2")
    grad_x, d_ffn1_norm = _norm_bwd(dh1, [dn_a, dn_b, dn_c, dn_d], x, small["ffn1_norm"] + token[0, 0], "norm1_bwd")
    updated = (update(GROUPS[2], g_ffn2.reshape(-1, D_MODEL), d_ffn1_norm)
               + update(GROUPS[1], g_mix.reshape(-1, D_MODEL), d_ffn1_norm))
    pair_a, pair_b, pair_c, stack, dw_d, recv_d = _split_wait(
        sems_a + sems_b + sems_c, [pair_a, pair_b, pair_c, stack, dw_d, recv_d], _joined(steps + [own]), updated,
        "ffn1_scatter_wait")
    pair_d = _pair_sum(place, dw_d, recv_d, "grad_pair_sum_ffn1_3")
    reduced_ffn1 = _chip_sum(place, pair_d, stack, 2, "grad_chip_sum_ffn1")
    g_ffn1, gains = _comm_call(_merge(_sibling_share([reduced_ffn1]), _small_allgather(d_ffn1_norm.reshape(-1, LANES))),
                               "grad_share_tail")
    gain_sum = _sum_leading(gains, 1, "gain_grad_sum")
    small_sum = jnp.concatenate([gain_sum, _sum_leading(small_all, 1, "small_grad_sum")[gain_sum.shape[0]:]], axis=0)
    update(GROUPS[0], g_ffn1.reshape(-1, D_MODEL))
    return jnp.sum(small_sum[SMALL_ROWS - 1]), grad_x, small_sum


GROUPS =(("ffn1_w_gate", "ffn1_w_up", "ffn1_w_down"), ("w_in", "w_out"), ("ffn2_w_gate", "ffn2_w_up", "ffn2_w_down"))
TRANSPOSED = ("ffn1_w_gate", "ffn1_w_up", "w_in", "ffn2_w_gate", "ffn2_w_up")


def _place():
    x, y, c = lax.axis_index("x"), lax.axis_index("y"), lax.axis_index("c")
    chips = [(1 - x, y), (x, 1 - y), (1 - x, 1 - y)]
    return x, y, c, chips


def _remote(src, dst, send_sem, recv_sem, to):
    return pltpu.make_async_remote_copy(src_ref=src, dst_ref=dst, send_sem=send_sem, recv_sem=recv_sem,
                                        device_id=to, device_id_type=MESH)


def _pack(chip, members, name):
    rows = members[0].shape[0]
    n = len(members)

    def body(chip_ref, *refs):
        ins, out_ref, buf, sems = refs[:n], refs[n], refs[n + 1], refs[n + 2]
        copies = [pltpu.make_async_copy(ins[k], buf.at[k], sems.at[k]) for k in range(n)]
        for cp in copies:
            cp.start()
        for k in range(n):
            copies[k].wait()
            out_ref[0, k * rows:(k + 1) * rows, :] = buf[k].astype(BF16)

    return pl.pallas_call(
        body, name=name,
        grid_spec=pltpu.PrefetchScalarGridSpec(
            num_scalar_prefetch=1, grid=(1,),
            in_specs=[HBM_SPEC] * n,
            out_specs=pl.BlockSpec((1, n * rows, D_MODEL), lambda k, chip_ref: (chip_ref[0], 0, 0)),
            scratch_shapes=[pltpu.VMEM((n, rows, D_MODEL), F32), pltpu.SemaphoreType.DMA((n,))]),
        out_shape=jax.ShapeDtypeStruct((N_CHIPS, n * rows, D_MODEL), BF16),
        compiler_params=_params(("arbitrary",), 40),
    )(chip, *members)


def _same(arrays):
    return [jax.ShapeDtypeStruct(a.shape, a.dtype) for a in arrays]


X_Y_DIAGONAL = (0, 1, 2)


def _allgather(bufs, peers=X_Y_DIAGONAL, in_passes=None, landed_before=()):
    n = len(bufs)

    def copy(kind, outs, send_sems, recv_sems, a, k):
        x, y, c, chips = _place()
        half = bufs[a].shape[1] // 2

        def rows(slot, core):
            return outs[a].at[slot, pl.ds(pl.multiple_of(core * half, 16), half)]

        me, slot = 2 * x + y, 2 * chips[k][0] + chips[k][1]
        over_ici = (send_sems.at[6 * a + k], recv_sems.at[6 * a + k])
        over_d2d = (send_sems.at[6 * a + 3 + k], recv_sems.at[6 * a + 3 + k])
        if kind == "first":
            return _remote(rows(me, c), rows(me, c), *over_ici, (*chips[k], c))
        if kind == "landed":
            return _remote(rows(me, c), rows(slot, c), *over_ici, (*chips[k], c))
        if kind == "passed":
            return _remote(rows(slot, c), rows(slot, c), *over_d2d, (x, y, 1 - c))
        return _remote(rows(me, c), rows(slot, 1 - c), *over_d2d, (x, y, 1 - c))

    def arrive(pairs):
        def hook(ins, outs, sems):
            for a, k in pairs:
                if k not in landed_before:
                    copy("landed", outs, *sems, a, k).wait_recv()
                    copy("passed", outs, *sems, a, k).start()
            for a, k in pairs:
                copy("handed", outs, *sems, a, k).wait_recv()
        return hook

    sent = [(a, k) for a in range(n) for k in peers]
    everything = sent + [(in_passes, k) for k in landed_before]
    early = [(a, k) for a, k in everything if a == in_passes]

    def start(ins, outs, sems):
        for k in landed_before:
            copy("passed", outs, *sems, in_passes, k).start()
        for a, k in sent:
            copy("first", outs, *sems, a, k).start()

    def finish(ins, outs, sems):
        arrive([pair for pair in everything if pair not in early])(ins, outs, sems)
        for a, k in sent:
            copy("first", outs, *sems, a, k).wait_send()
        for a, k in everything:
            copy("passed", outs, *sems, a, k).wait_send()

    hooks = tuple(((k + 1, 0), arrive([(a, k)])) for a, k in early)
    return _Rider(list(bufs), _same(bufs), {a: a for a in range(n)},
                  [pltpu.SemaphoreType.DMA((6 * n,)), pltpu.SemaphoreType.DMA((6 * n,))], start, finish, hooks)


SEM_SPEC = pl.BlockSpec(memory_space=pltpu.SEMAPHORE)


def _split_start(arrays, n_copies, make_copies, name):
    m, n = len(arrays), 2 * n_copies

    def body(*refs):
        sems, thru, token = refs[m:m + n], refs[m + n:2 * m + n], refs[2 * m + n]
        for cp in make_copies(thru, sems, False):
            cp.start()
        token[...] = jnp.zeros_like(token)

    outs = pl.pallas_call(
        body, name=name,
        out_shape=[pltpu.SemaphoreType.DMA(())] * n + _same(arrays) + [jax.ShapeDtypeStruct((SUBLANES, LANES), F32)],
        in_specs=[HBM_SPEC] * m, out_specs=[SEM_SPEC] * n + [HBM_SPEC] * m + [pl.BlockSpec(memory_space=pltpu.VMEM)],
        input_output_aliases={a: n + a for a in range(m)},
        compiler_params=pltpu.CompilerParams(has_side_effects=pltpu.SideEffectType.DATAFLOW_SIDE_EFFECTING),
    )(*arrays)
    return list(outs[:n]), list(outs[n:n + m]), outs[n + m]


def _split_wait(sems, arrays, make_copies, after, name):
    m, n = len(arrays), len(sems)

    def body(*refs):
        for cp in make_copies(refs[m + n + len(after):], refs[m:m + n], True):
            cp.wait_send()
            cp.wait_recv()

    return pl.pallas_call(
        body, name=name, out_shape=_same(arrays),
        in_specs=[HBM_SPEC] * m + [SEM_SPEC] * n + [pl.BlockSpec(memory_space=pl.ANY)] * len(after),
        out_specs=[HBM_SPEC] * m, input_output_aliases={a: a for a in range(m)},
        compiler_params=pltpu.CompilerParams(has_side_effects=pltpu.SideEffectType.DATAFLOW_SIDE_EFFECTING),
    )(*arrays, *sems, *after)


def _gather_copies(peers):
    def make(refs, sems, landing):
        x, y, c, chips = _place()
        half = refs[0].shape[1] // 2

        def rows(slot):
            return refs[0].at[slot, pl.ds(pl.multiple_of(c * half, 16), half)]

        me = 2 * x + y
        return [_remote(rows(me), rows(2 * chips[k][0] + chips[k][1] if landing else me), sems[2 * j], sems[2 * j + 1],
                        (*chips[k], c)) for j, k in enumerate(peers)]
    return make


def _scatter_copies(refs, sems, landing):
    x, y, c, chips = _place()
    me = 2 * x + y
    slots = [2 * cx + cy for cx, cy in chips]
    return [_remote(refs[0].at[slots[k]], refs[1].at[slots[k] if landing else me], sems[2 * k], sems[2 * k + 1], (*chips[k], c))
            for k in X_Y_DIAGONAL]


def _scatter_step_copies(step):
    def make(refs, sems, landing):
        x, y, c, _ = _place()
        me = 2 * x + y
        to = (me + 1 + step) % N_CHIPS
        frm = (me + N_CHIPS - 1 - step) % N_CHIPS
        peer = frm if landing else to
        return [_remote(refs[0].at[0], refs[1].at[frm if landing else me], sems[0], sems[1], (peer // 2, peer % 2, c))]
    return make


def _exchange_copies(refs, sems, landing):
    x, y, c, _ = _place()
    return [_remote(refs[0].at[1 - c], refs[1], sems[0], sems[1], (x, y, 1 - c))]


def _joined(makers):
    def make(refs, sems, landing):
        copies, at = [], 0
        for maker, places in makers:
            mine = maker([refs[p] for p in places], sems[at:], landing)
            copies += mine
            at += 2 * len(mine)
        return copies
    return make


def _sibling_exchange(parts):
    n = len(parts)

    def copies(ins, outs, send_sems, recv_sems):
        x, y, c, _ = _place()
        return [_remote(ins[a].at[1 - c], outs[a], send_sems.at[a], recv_sems.at[a], (x, y, 1 - c)) for a in range(n)]

    def start(ins, outs, sems):
        for cp in copies(ins, outs, *sems):
            cp.start()

    def finish(ins, outs, sems):
        for cp in copies(ins, outs, *sems):
            cp.wait_recv()
            cp.wait_send()

    return _Rider(list(parts), [jax.ShapeDtypeStruct(p.shape[1:], p.dtype) for p in parts], {},
                  [pltpu.SemaphoreType.DMA((n,)), pltpu.SemaphoreType.DMA((n,))], start, finish)


def _small_allgather(small):
    flips = [(fx, fy, fc) for fx in range(2) for fy in range(2) for fc in range(2)][1:]

    def copies(small_ref, gather_ref, send_sems, recv_sems, local_sem, started_only=False):
        x, y, c, _ = _place()
        me = 4 * x + 2 * y + c
        peers = [((1 - x) if fx else x, (1 - y) if fy else y, (1 - c) if fc else c) for fx, fy, fc in flips]
        own = pltpu.make_async_copy(small_ref, gather_ref.at[me], local_sem)
        sent = [_remote(small_ref, gather_ref.at[me], send_sems.at[k], recv_sems.at[k], peer) for k, peer in enumerate(peers)]
        if started_only:
            return own, sent
        landed = [_remote(small_ref, gather_ref.at[4 * px + 2 * py + pc], send_sems.at[k], recv_sems.at[k], (px, py, pc))
                  for k, (px, py, pc) in enumerate(peers)]
        return own, sent, landed

    def start(ins, outs, sems):
        own, sent = copies(ins[0], outs[0], *sems, started_only=True)
        own.start()
        for cp in sent:
            cp.start()

    def finish(ins, outs, sems):
        own, sent, landed = copies(ins[0], outs[0], *sems)
        for cp in landed:
            cp.wait_recv()
        for cp in sent:
            cp.wait_send()
        own.wait()

    return _Rider([small], [jax.ShapeDtypeStruct((2 * N_CHIPS,) + small.shape, small.dtype)], {},
                  [pltpu.SemaphoreType.DMA((7,)), pltpu.SemaphoreType.DMA((7,)), pltpu.SemaphoreType.DMA], start, finish)


def _merge(a, b):
    na, nao, nas = len(a.operands), len(a.out_shapes), len(a.scratch)

    def start(ins, outs, sems):
        a.start(ins[:na], outs[:nao], sems[:nas])
        b.start(ins[na:], outs[nao:], sems[nas:])

    def finish(ins, outs, sems):
        a.finish(ins[:na], outs[:nao], sems[:nas])
        b.finish(ins[na:], outs[nao:], sems[nas:])

    def of_a(fn):
        return lambda ins, outs, sems: fn(ins[:na], outs[:nao], sems[:nas])

    def of_b(fn):
        return lambda ins, outs, sems: fn(ins[na:], outs[nao:], sems[nas:])

    aliases = {**a.aliases, **{na + k: nao + v for k, v in b.aliases.items()}}
    hooks = tuple((at, of_a(fn)) for at, fn in a.hooks) + tuple((at, of_b(fn)) for at, fn in b.hooks)
    return _Rider(a.operands + b.operands, a.out_shapes + b.out_shapes, aliases, a.scratch + b.scratch, start, finish, hooks)


def _scatter(sums):
    n = len(sums)

    def copies(ins, outs, send_sems, recv_sems, started_only=False):
        x, y, c, chips = _place()
        me = 2 * x + y
        slots = [2 * cx + cy for cx, cy in chips]
        sent = [_remote(ins[a].at[slots[k]], outs[a].at[me], send_sems.at[3 * a + k], recv_sems.at[3 * a + k], (*chips[k], c))
                for a in range(n) for k in X_Y_DIAGONAL]
        if started_only:
            return sent
        landed = [_remote(ins[a].at[slots[k]], outs[a].at[slots[k]], send_sems.at[3 * a + k], recv_sems.at[3 * a + k],
                          (*chips[k], c)) for a in range(n) for k in X_Y_DIAGONAL]
        return sent, landed

    def start(ins, outs, sems):
        for cp in copies(ins, outs, *sems, started_only=True):
            cp.start()

    def finish(ins, outs, sems):
        sent, landed = copies(ins, outs, *sems)
        for cp in landed:
            cp.wait_recv()
        for cp in sent:
            cp.wait_send()

    return _Rider(list(sums), _same(sums), {}, [pltpu.SemaphoreType.DMA((3 * n,)), pltpu.SemaphoreType.DMA((3 * n,))],
                  start, finish)


def _sibling_share(bufs):
    n = len(bufs)

    def copies(outs, send_sems, recv_sems, started_only=False):
        x, y, c, _ = _place()
        sent = [_remote(outs[a].at[c], outs[a].at[c], send_sems.at[a], recv_sems.at[a], (x, y, 1 - c)) for a in range(n)]
        if started_only:
            return sent
        landed = [_remote(outs[a].at[c], outs[a].at[1 - c], send_sems.at[a], recv_sems.at[a], (x, y, 1 - c)) for a in range(n)]
        return sent, landed

    def start(ins, outs, sems):
        for cp in copies(outs, *sems, started_only=True):
            cp.start()

    def finish(ins, outs, sems):
        sent, landed = copies(outs, *sems)
        for cp in landed:
            cp.wait_recv()
        for cp in sent:
            cp.wait_send()

    return _Rider(list(bufs), _same(bufs), {a: a for a in range(n)},
                  [pltpu.SemaphoreType.DMA((n,)), pltpu.SemaphoreType.DMA((n,))], start, finish)


def _pair_sum(core, part, received, name):
    _, k, rh, cols = part.shape

    def body(core_ref, p_ref, r_ref, o_ref):
        o_ref[...] = (p_ref[0].astype(F32) + r_ref[...].astype(F32)).astype(BF16)

    return pl.pallas_call(
        body, name=name,
        grid_spec=pltpu.PrefetchScalarGridSpec(
            num_scalar_prefetch=1, grid=(k,),
            in_specs=[pl.BlockSpec((1, 1, rh, cols), lambda j, core_ref: (core_ref[0], j, 0, 0)),
                      pl.BlockSpec((1, rh, cols), lambda j, core_ref: (j, 0, 0))],
            out_specs=pl.BlockSpec((1, rh, cols), lambda j, core_ref: (j, 0, 0))),
        out_shape=jax.ShapeDtypeStruct((k, rh, cols), BF16),
        compiler_params=_params(("parallel",), 32),
    )(core, part, received)


def _sum_leading(stack, steps, name):
    k, rows, cols = stack.shape
    tile = rows // steps

    def body(s_ref, o_ref):
        total = s_ref[0].astype(F32)
        for d in range(1, k):
            total = total + s_ref[d].astype(F32)
        o_ref[...] = total

    return pl.pallas_call(
        body, name=name, grid=(steps,),
        in_specs=[pl.BlockSpec((k, tile, cols), lambda i: (0, i, 0))],
        out_specs=pl.BlockSpec((tile, cols), lambda i: (i, 0)),
        out_shape=jax.ShapeDtypeStruct((rows, cols), F32),
        compiler_params=_params(("parallel",), 32),
    )(stack)


def _chip_sum(place, own, stack, steps, name):
    k, rows, cols = stack.shape
    tile = rows // steps

    def body(place_ref, own_ref, *refs):
        chip = place_ref[1]
        total = None
        for d in range(k):
            term = jnp.where(chip == d, own_ref[0], refs[d][0]).astype(F32)
            total = term if total is None else total + term
        refs[k][0] = total

    def other(d):
        return lambda i, place_ref: (jnp.where(place_ref[1] == d, (d + 1) % k, d), i, 0)

    return pl.pallas_call(
        body, name=name,
        grid_spec=pltpu.PrefetchScalarGridSpec(
            num_scalar_prefetch=1, grid=(steps,),
            in_specs=[pl.BlockSpec((1, tile, cols), lambda i, place_ref: (place_ref[1] % own.shape[0], i, 0))]
            + [pl.BlockSpec((1, tile, cols), other(d)) for d in range(k)],
            out_specs=pl.BlockSpec((1, tile, cols), lambda i, place_ref: (place_ref[0], i, 0))),
        out_shape=jax.ShapeDtypeStruct((2, rows, cols), F32),
        compiler_params=_params(("arbitrary",), 32),
    )(place, own, *([stack] * k))


def _adamw(w, g, row0, m, v, tile, name, after=None):
    rows, cols = w.shape
    first = row0 // tile
    assert rows % tile == 0 and row0 % tile == 0
    bc1 = 1.0 - ADAM_B1 ** ADAM_STEP
    bc2 = 1.0 - ADAM_B2 ** ADAM_STEP

    def body(w_ref, g_ref, m_ref, v_ref, *refs):
        go_ref, d_ref, mo_ref, vo_ref = refs[-4:]
        g = g_ref[...]
        m_new = ADAM_B1 * m_ref[...] + (1.0 - ADAM_B1) * g
        v_new = ADAM_B2 * v_ref[...] + (1.0 - ADAM_B2) * (g * g)
        go_ref[...] = g
        d_ref[...] = -ADAM_LR * ((m_new / bc1) / (jnp.sqrt(v_new / bc2) + ADAM_EPS) + ADAM_WD * w_ref[...])
        mo_ref[...] = m_new
        vo_ref[...] = v_new

    later = [] if after is None else [after]
    spec = pl.BlockSpec((tile, cols), lambda i: (i, 0))
    g_spec = pl.BlockSpec((tile, cols), lambda i: (first + i, 0))
    return pl.pallas_call(
        body, name=name, grid=(rows // tile,),
        in_specs=[spec, g_spec, spec, spec] + [pl.BlockSpec(memory_space=pl.ANY)] * len(later), out_specs=[spec] * 4,
        out_shape=[jax.ShapeDtypeStruct((rows, cols), F32)] * 4,
        compiler_params=_params(("parallel",), 32),
    )(w, g, m, v, *later)


SMALL = ("ffn1_norm", "mix_norm", "ffn2_norm", "final_norm", "pool_scale", "sink_logits", "pool_w")


def _pack_small(d, last_row=None):
    def part(n):
        flat = d[n].reshape(-1)
        flat = jnp.pad(flat, (0, -flat.shape[0] % (SUBLANES * LANES)))
        return flat.reshape(-1, LANES)

    last = jnp.zeros((SUBLANES, LANES), F32) if last_row is None else jnp.pad(last_row, ((SUBLANES - 1, 0), (0, 0)))
    packed = jnp.concatenate([part(n) for n in SMALL] + [last], axis=0)
    assert packed.shape[0] == SMALL_ROWS
    return packed


def _unpack_small(packed, like):
    out, row = {}, 0
    for n in SMALL:
        size = math.prod(like[n].shape)
        rows = -(-size // (SUBLANES * LANES)) * SUBLANES
        out[n] = packed[row:row + rows].reshape(-1)[:size].reshape(like[n].shape)
        row += rows
    return out


def kernel(x, ffn1_norm, ffn1_w_gate, ffn1_w_up, ffn1_w_down, mix_norm, w_in, sink_logits, pool_w, pool_scale, w_out, ffn2_norm, ffn2_w_gate, ffn2_w_up, ffn2_w_down, final_norm, loss_target, m_ffn1_norm, m_ffn1_w_gate, m_ffn1_w_up, m_ffn1_w_down, m_mix_norm, m_w_in, m_sink_logits, m_pool_w, m_pool_scale, m_w_out, m_ffn2_norm, m_ffn2_w_gate, m_ffn2_w_up, m_ffn2_w_down, m_final_norm, v_ffn1_norm, v_ffn1_w_gate, v_ffn1_w_up, v_ffn1_w_down, v_mix_norm, v_w_in, v_sink_logits, v_pool_w, v_pool_scale, v_w_out, v_ffn2_norm, v_ffn2_w_gate, v_ffn2_w_up, v_ffn2_w_down, v_final_norm):
    names = ("ffn1_norm", "ffn1_w_gate", "ffn1_w_up", "ffn1_w_down", "mix_norm", "w_in", "sink_logits", "pool_w",
             "pool_scale", "w_out", "ffn2_norm", "ffn2_w_gate", "ffn2_w_up", "ffn2_w_down", "final_norm")
    weights = dict(zip(names, (ffn1_norm, ffn1_w_gate, ffn1_w_up, ffn1_w_down, mix_norm, w_in, sink_logits, pool_w,
                               pool_scale, w_out, ffn2_norm, ffn2_w_gate, ffn2_w_up, ffn2_w_down, final_norm)))
    mom1 = dict(zip(names, (m_ffn1_norm, m_ffn1_w_gate, m_ffn1_w_up, m_ffn1_w_down, m_mix_norm, m_w_in, m_sink_logits,
                            m_pool_w, m_pool_scale, m_w_out, m_ffn2_norm, m_ffn2_w_gate, m_ffn2_w_up, m_ffn2_w_down,
                            m_final_norm)))
    mom2 = dict(zip(names, (v_ffn1_norm, v_ffn1_w_gate, v_ffn1_w_up, v_ffn1_w_down, v_mix_norm, v_w_in, v_sink_logits,
                            v_pool_w, v_pool_scale, v_w_out, v_ffn2_norm, v_ffn2_w_gate, v_ffn2_w_up, v_ffn2_w_down,
                            v_final_norm)))
    chip = (2 * lax.axis_index("x") + lax.axis_index("y")).astype(jnp.int32).reshape(1)
    place = jnp.concatenate([lax.axis_index("c").astype(jnp.int32).reshape(1), chip])

    def rows_of(t, n):
        return jnp.swapaxes(t[n][0], 0, 1) if n in TRANSPOSED else t[n][0]

    bufs = [_pack(chip, [rows_of(weights, n) for n in GROUPS[0]], "pack_ffn1"),
            _pack(chip, [jnp.concatenate([rows_of(weights, n) for n in GROUPS[1]], axis=0)], "pack_mix"),
            _pack(chip, [rows_of(weights, n) for n in GROUPS[2]], "pack_ffn2")]

    small_w = {"ffn1_norm": ffn1_norm, "mix_norm": mix_norm, "ffn2_norm": ffn2_norm,
               "final_norm": final_norm.reshape(1, D_MODEL), "pool_scale": pool_scale, "sink_logits": sink_logits,
               "pool_w": pool_w[0]}
    out_g, out_d, out_m, out_v = {}, {}, {}, {}

    def update(members, g, after=None):
        row0, done = 0, []
        for n in members:
            w = rows_of(weights, n)
            tile = FF_CHUNK // 4 if w.shape[0] == FF_CHUNK else math.gcd(IN_ROWS, OUT_ROWS)
            outs = _adamw(w, g, row0, rows_of(mom1, n), rows_of(mom2, n), tile, "adamw_" + n, after)
            row0 += w.shape[0]
            done.append(outs[0])
            for dst, t in zip((out_g, out_d, out_m, out_v), outs):
                dst[n] = (jnp.swapaxes(t, 0, 1) if n in TRANSPOSED else t).reshape(weights[n].shape)
        return done

    loss, grad_x, small_sum = _step(x[0], loss_target[0], bufs, small_w, place, update)
    small_outs = _adamw(_pack_small(weights), small_sum, 0, _pack_small(mom1), _pack_small(mom2), SMALL_ROWS, "adamw_small")
    for dst, packed in zip((out_g, out_d, out_m, out_v), small_outs):
        dst.update(_unpack_small(packed, weights))

    return (loss,grad_x.reshape(x.shape), *[out_g[n] for n in names], *[out_d[n] for n in names],
            *[out_m[n] for n in names], *[out_v[n] for n in names])
```

```python
import collections
import functools
import math

import jax
import jax.numpy as jnp
from jax import lax
from jax.experimental import pallas as pl
from jax.experimental.pallas import tpu as pltpu

F32, BF16 = jnp.float32, jnp.bfloat16
MESH = pl.DeviceIdType.MESH

D_MODEL = 1024
D_FF = 2816
N_CHIPS = 4
FF_CHUNK = D_FF // N_CHIPS
HEAD_DIM = 64
N_HEADS = 8
N_KV = 2
Q_PER_KV = N_HEADS // N_KV
KV_WIDTH = N_KV * HEAD_DIM
ATTN_WIDTH = N_HEADS * HEAD_DIM
POOL_WINDOWS = (2, 4, 8, 16)
N_POOL = len(POOL_WINDOWS)
POOL_GROUP = 128
POOL_WIDTH = N_POOL * POOL_GROUP
IN_WIDTH = ATTN_WIDTH + 2 * KV_WIDTH + POOL_WIDTH
WINDOW = 128
BLOCK = 128
BAND = 3 * BLOCK
ROPE_THETA = 500000.0
ROTARY_DIM = HEAD_DIM // 4
EPS = 1e-6
LANES = 128
Q_PAD = N_HEADS * LANES
U_PAD = Q_PAD + 2 * KV_WIDTH + POOL_WIDTH
SCALE = HEAD_DIM ** -0.5
NEG = -1e30

ADAM_LR, ADAM_B1, ADAM_B2, ADAM_EPS, ADAM_WD, ADAM_STEP = 0.001, 0.9, 0.999, 1e-08, 0.01, 10

V7X_VMEM_BYTES = 64 * 1024 * 1024
TOK_TILE = 512
SUBLANES = 8
SMALL_ROWS = 568


def _params(sem, vmem_mb):
    assert vmem_mb * 1024 * 1024 <= V7X_VMEM_BYTES
    return pltpu.CompilerParams(dimension_semantics=sem, vmem_limit_bytes=vmem_mb * 1024 * 1024)


def _dot(a, b):
    return lax.dot_general(a, b, (((1,), (0,)), ((), ())), preferred_element_type=F32)


def _dot_nt(a, b):
    return lax.dot_general(a, b, (((1,), (1,)), ((), ())), preferred_element_type=F32)


def _dot_tn(a, b):
    return lax.dot_general(a, b, (((0,), (0,)), ((), ())), preferred_element_type=F32)


def _rms_stats(h):
    r = lax.rsqrt(jnp.mean(h * h, axis=-1, keepdims=True) + EPS)
    return r, h * r


def _rms_bwd(dn, g, r, xh):
    gd = dn * g
    dh = r * (gd - xh * jnp.mean(gd * xh, axis=-1, keepdims=True))
    return dh, jnp.sum(dn * xh, axis=0, keepdims=True)


def _rope(x, c, s1, s2):
    return x * c + pltpu.roll(x, LANES - ROTARY_DIM // 2, 1) * s1 + pltpu.roll(x, ROTARY_DIM // 2, 1) * s2


def _rope_bwd(d, c, s1, s2):
    return d * c + pltpu.roll(d * s1, ROTARY_DIM // 2, 1) + pltpu.roll(d * s2, LANES - ROTARY_DIM // 2, 1)


def _sum_chunks(refs):
    terms = [ref[j].astype(F32) for ref in refs for j in range(ref.shape[0])]
    return functools.reduce(lambda a, b: a + b, terms)


def _chunk_rows(tile, k):
    return pl.BlockSpec((k, tile, D_MODEL), lambda i, *_: (0, i, 0))


def _full(shape):
    nd = len(shape)
    return pl.BlockSpec(shape, lambda *_: (0,) * nd)


def _rows(tile, cols):
    return pl.BlockSpec((tile, cols), lambda i, *_: (i, 0))


HBM_SPEC = pl.BlockSpec(memory_space=pltpu.HBM)

_Rider = collections.namedtuple("_Rider", "operands out_shapes aliases scratch start finish hooks", defaults=[()])


_NO_RIDER = _Rider([], [], {}, [], None, None)


def _call(body, name, grid, in_specs, out_specs, out_shape, scratch, vmem_mb, args, rider=None, prefetch=(),
          shares_rider_refs=False):
    rider = rider or _NO_RIDER
    n_pre, n_in, n_out, n_scr = len(prefetch), len(in_specs), len(out_specs), len(scratch)
    r_in, r_out = len(rider.operands), len(rider.out_shapes)

    def fused(*refs):
        pre, refs = refs[:n_pre], refs[n_pre:]
        ins, refs = refs[:n_in], refs[n_in:]
        r_ins, refs = refs[:r_in], refs[r_in:]
        outs, refs = refs[:n_out], refs[n_out:]
        r_outs, refs = refs[:r_out], refs[r_out:]
        scr, r_scr = refs[:n_scr], refs[n_scr:]
        ids = [pl.program_id(d) for d in range(len(grid))]
        if rider.start is not None:
            @pl.when(functools.reduce(jnp.logical_and, [i == 0 for i in ids]))
            def _():
                rider.start(r_ins, r_outs, r_scr)

        for at, hook in rider.hooks:
            @pl.when(functools.reduce(jnp.logical_and, [i == a for i, a in zip(ids, at)]))
            def _(hook=hook):
                hook(r_ins, r_outs, r_scr)

        if shares_rider_refs:
            body(*pre, *ins, *outs, *scr, rider_refs=r_outs)
        else:
            body(*pre, *ins, *outs, *scr)

        if rider.finish is not None:
            @pl.when(functools.reduce(jnp.logical_and, [i == g - 1 for i, g in zip(ids, grid)]))
            def _():
                rider.finish(r_ins, r_outs, r_scr)

    return pl.pallas_call(
        fused, name=name,
        grid_spec=pltpu.PrefetchScalarGridSpec(
            num_scalar_prefetch=n_pre, grid=grid,
            in_specs=list(in_specs) + [HBM_SPEC] * r_in, out_specs=list(out_specs) + [HBM_SPEC] * r_out,
            scratch_shapes=list(scratch) + list(rider.scratch)),
        out_shape=list(out_shape) + list(rider.out_shapes),
        input_output_aliases={n_pre + n_in + k: n_out + v for k, v in rider.aliases.items()},
        compiler_params=_params(("arbitrary",) * len(grid), vmem_mb),
    )(*prefetch, *args, *rider.operands)


def _after(token):
    return _Rider([token], [], {}, [], lambda *_: None, lambda *_: None)


def _comm_call(rider, name):
    r_in, r_out = len(rider.operands), len(rider.out_shapes)

    def body(*refs):
        r_ins, r_outs, r_scr = refs[:r_in], refs[r_in:r_in + r_out], refs[r_in + r_out:]
        rider.start(r_ins, r_outs, r_scr)
        rider.finish(r_ins, r_outs, r_scr)

    return pl.pallas_call(
        body, name=name, in_specs=[HBM_SPEC] * r_in, out_specs=[HBM_SPEC] * r_out, out_shape=list(rider.out_shapes),
        input_output_aliases=dict(rider.aliases), scratch_shapes=list(rider.scratch),
    )(*rider.operands)


def _ffn_fwd(order, h, gain, name, rider, group_at, loss_head=None):
    S = h.shape[0]
    tile = min(TOK_TILE, S)
    nt = S // tile
    last = N_CHIPS - 1
    n_extra_in, n_head_out = (2, 4) if loss_head else (0, 1)

    def body(order_ref, h_ref, g_ref, *refs, rider_refs):
        extra_in, refs = refs[:n_extra_in], refs[n_extra_in:]
        head_out, (n_ref, gate_ref, up_ref, w_scr, w_sem, acc, n_scr) = refs[:n_head_out], refs[n_head_out:]
        j, i = pl.program_id(0), pl.program_id(1)

        @pl.when(i == 0)
        def _():
            fetch = pltpu.make_async_copy(rider_refs[group_at].at[order_ref[j]], w_scr, w_sem)
            fetch.start()
            fetch.wait()

        at = pl.multiple_of(i * tile, tile)

        @pl.when(j == 0)
        def _():
            _, xh = _rms_stats(h_ref[...])
            n = (xh * g_ref[...]).astype(BF16)
            n_scr[pl.ds(at, tile), :] = n
            n_ref[...] = n
            acc[pl.ds(at, tile), :] = jnp.zeros((tile, D_MODEL), F32)

        half = tile // 2
        wg, wu, wd = (w_scr[part * FF_CHUNK:(part + 1) * FF_CHUNK, :] for part in range(3))
        ns = [n_scr[pl.ds(at + s * half, half), :] for s in range(2)]
        gates = [_dot_nt(n, wg) for n in ns]
        ups = [_dot_nt(n, wu) for n in ns]
        acts = [(g * jax.nn.sigmoid(g) * u).astype(BF16) for g, u in zip(gates, ups)]
        for s in range(2):
            gate_ref[0, s * half:(s + 1) * half, :] = gates[s].astype(BF16)
            up_ref[0, s * half:(s + 1) * half, :] = ups[s].astype(BF16)
            acc[pl.ds(at + s * half, half), :] += _dot(acts[s], wd)

        @pl.when(j == last)
        def _():
            out = h_ref[...] + 0.5 * acc[pl.ds(at, tile), :]
            if not loss_head:
                head_out[0][...] = out
                return
            (t_ref, gf_ref), (dh_ref, dhalf_ref, loss_ref, dg_ref) = extra_in, head_out

            @pl.when(i == 0)
            def _():
                loss_ref[...] = jnp.zeros_like(loss_ref)
                dg_ref[...] = jnp.zeros_like(dg_ref)

            gf = gf_ref[...]
            r, xh = _rms_stats(out)
            err = xh * gf - t_ref[...]
            loss_ref[...] += (0.5 / D_MODEL) * jnp.sum(err * err, axis=0, keepdims=True)
            dh, dg = _rms_bwd(err * (1.0 / D_MODEL), gf, r, xh)
            dg_ref[...] += dg
            dh_ref[...] = dh
            dhalf_ref[...] = (0.5 * dh).astype(BF16)

    tok = pl.BlockSpec((tile, D_MODEL), lambda j, i, order_ref: (i, 0))
    hid = pl.BlockSpec((1, tile, FF_CHUNK), lambda j, i, order_ref: (order_ref[j], i, 0))
    row = pl.BlockSpec((1, D_MODEL), lambda j, i, order_ref: (0, 0))
    in_last = pl.BlockSpec((tile, D_MODEL), lambda j, i, order_ref: (jnp.where(j == last, i, 0), 0))
    in_first = pl.BlockSpec((tile, D_MODEL), lambda j, i, order_ref: (jnp.where(j == 0, i, nt - 1), 0))
    tok_f32, tok_bf16, lanes = (jax.ShapeDtypeStruct((S, D_MODEL), F32), jax.ShapeDtypeStruct((S, D_MODEL), BF16),
                                jax.ShapeDtypeStruct((1, D_MODEL), F32))
    hidden = jax.ShapeDtypeStruct((N_CHIPS, S, FF_CHUNK), BF16)
    if loss_head:
        extra_specs, extra_args = [in_last, row], list(loss_head)
        head_specs, head_shapes = [in_last, in_last, row, row], [tok_f32, tok_bf16, lanes, lanes]
    else:
        extra_specs, extra_args, head_specs, head_shapes = [], [], [in_last], [tok_f32]
    return _call(
        body, name, (N_CHIPS, nt), [tok, row] + extra_specs, head_specs + [in_first, hid, hid],
        head_shapes + [tok_bf16, hidden, hidden],
        [pltpu.VMEM((3 * FF_CHUNK, D_MODEL), BF16), pltpu.SemaphoreType.DMA, pltpu.VMEM((S, D_MODEL), F32),
         pltpu.VMEM((S, D_MODEL), BF16)], 58, (h, gain, *extra_args), rider, (order,), shares_rider_refs=True)


def _ffn_bwd(chunks, d_out, n, gate, up, group, name, rider=None):
    S = n.shape[0]
    n_chunks = chunks.shape[0]
    tile = min(TOK_TILE, S)
    nt = S // tile
    half_rows = 3 * FF_CHUNK // 2
    cut = FF_CHUNK // 2

    def body(chunks_ref, do_ref, n_ref, gate_ref, up_ref, wg_ref, wu_ref, wd_ref, dn_ref, dw_ref, acc_g, acc_u, acc_d):
        j, i = pl.program_id(0), pl.program_id(1)

        @pl.when(i == 0)
        def _():
            acc_g[...] = jnp.zeros_like(acc_g)
            acc_u[...] = jnp.zeros_like(acc_u)
            acc_d[...] = jnp.zeros_like(acc_d)

        halves = [pl.ds(s * (tile // 2), tile // 2) for s in range(2)]
        dos = [do_ref[rows, :] for rows in halves]
        d_acts = [_dot_nt(do, wd_ref[0]) for do in dos]
        gs = [gate_ref[0, rows, :].astype(F32) for rows in halves]
        us = [up_ref[0, rows, :].astype(F32) for rows in halves]
        sigs = [jax.nn.sigmoid(g) for g in gs]
        silus = [g * sig for g, sig in zip(gs, sigs)]
        d_ups = [(d_act * silu).astype(BF16) for d_act, silu in zip(d_acts, silus)]
        d_gates = [(d_act * u * (sig * (1.0 + g * (1.0 - sig)))).astype(BF16) for d_act, u, sig, g in zip(d_acts, us, sigs, gs)]
        for rows, d_gate, d_up in zip(halves, d_gates, d_ups):
            dn_ref[0, rows, :] = (_dot(d_gate, wg_ref[0]) + _dot(d_up, wu_ref[0])).astype(BF16)
        d_gate, d_up = jnp.concatenate(d_gates, axis=0), jnp.concatenate(d_ups, axis=0)
        act = jnp.concatenate([(silu * u).astype(BF16) for silu, u in zip(silus, us)], axis=0)
        nn = n_ref[...]
        acc_g[...] += _dot_tn(d_gate, nn)
        acc_u[...] += _dot_tn(d_up, nn)
        acc_d[...] += _dot_tn(act, do_ref[...])

        @pl.when(i == nt - 1)
        def _():
            dw_ref[0, 0, :FF_CHUNK, :] = acc_g[...].astype(BF16)
            dw_ref[0, 0, FF_CHUNK:, :] = acc_u[:cut, :].astype(BF16)
            dw_ref[1, 0, :cut, :] = acc_u[cut:, :].astype(BF16)
            dw_ref[1, 0, cut:, :] = acc_d[...].astype(BF16)

    tok = pl.BlockSpec((tile, D_MODEL), lambda j, i, chunks_ref: (i, 0))
    hid = pl.BlockSpec((1, tile, FF_CHUNK), lambda j, i, chunks_ref: (chunks_ref[j], i, 0))
    return _call(
        body, name, (n_chunks, nt),
        [tok, tok, hid, hid]
        + [pl.BlockSpec((1, FF_CHUNK, D_MODEL), functools.partial(lambda j, i, chunks_ref, part: (chunks_ref[j], part, 0), part=part))
           for part in range(3)],
        [pl.BlockSpec((1, tile, D_MODEL), lambda j, i, chunks_ref: (j, i, 0)),
         pl.BlockSpec((2, 1, half_rows, D_MODEL), lambda j, i, chunks_ref: (0, j, 0, 0))],
        [jax.ShapeDtypeStruct((n_chunks, S, D_MODEL), BF16), jax.ShapeDtypeStruct((2, n_chunks, half_rows, D_MODEL), BF16)],
        [pltpu.VMEM((FF_CHUNK, D_MODEL), F32)] * 3, 56, (d_out, n, gate, up, group, group, group), rider, (chunks,))


def _mix_in(h, gain, w_in, rc, rs1, rs2, name):
    S = h.shape[0]
    tile = min(TOK_TILE, S)

    def body(h_ref, g_ref, w_ref, c_ref, s1_ref, s2_ref, n_ref, q_ref, k_ref, v_ref, pc_ref):
        _, xh = _rms_stats(h_ref[...])
        n = (xh * g_ref[...]).astype(BF16)
        n_ref[...] = n
        u = _dot_nt(n, w_ref[...])
        c, s1, s2 = c_ref[...], s1_ref[...], s2_ref[...]
        q_ref[...] = jnp.concatenate([(_rope(u[:, hd * LANES:(hd + 1) * LANES], c, s1, s2) * SCALE).astype(BF16)
                                      for hd in range(N_HEADS)], axis=1)
        k_ref[...] = _rope(u[:, Q_PAD:Q_PAD + KV_WIDTH], c, s1, s2).astype(BF16)
        v_ref[...] = u[:, Q_PAD + KV_WIDTH:Q_PAD + 2 * KV_WIDTH].astype(BF16)
        pc_ref[...] = u[:, Q_PAD + 2 * KV_WIDTH:]

    return pl.pallas_call(
        body, name=name, grid=(S // tile,),
        in_specs=[_rows(tile, D_MODEL), _full((1, D_MODEL)), _full((U_PAD, D_MODEL)),
                  _rows(tile, LANES), _rows(tile, LANES), _rows(tile, LANES)],
        out_specs=[_rows(tile, D_MODEL), _rows(tile, Q_PAD), _rows(tile, KV_WIDTH), _rows(tile, KV_WIDTH),
                   _rows(tile, POOL_WIDTH)],
        out_shape=[jax.ShapeDtypeStruct((S, D_MODEL), BF16), jax.ShapeDtypeStruct((S, Q_PAD), BF16),
                   jax.ShapeDtypeStruct((S, KV_WIDTH), BF16), jax.ShapeDtypeStruct((S, KV_WIDTH), BF16),
                   jax.ShapeDtypeStruct((S, POOL_WIDTH), F32)],
        compiler_params=_params(("parallel",), 40),
    )(h, gain, w_in, rc, rs1, rs2)


def _band_start(i, S):
    return pl.multiple_of(jnp.clip((i - 1) * BLOCK, 0, S - BAND), BLOCK)


def _window_bias(off):
    r = lax.broadcasted_iota(jnp.int32, (BLOCK, 1), 0)
    c = lax.broadcasted_iota(jnp.int32, (1, BAND), 1)
    return jnp.where(jnp.abs(off + r - c) <= WINDOW, 0.0, NEG).astype(F32)


def _softmax_parts(qh, kb, bias, sink_h):
    s = _dot_nt(qh, kb) + bias
    m = jnp.maximum(jnp.max(s, axis=-1, keepdims=True), sink_h)
    p = jnp.exp(s - m)
    es = jnp.exp(sink_h - m)
    return p, es, 1.0 / (jnp.sum(p, axis=-1, keepdims=True) + es)


def _pool_matrix(t0, start, S, w):
    r = lax.broadcasted_iota(jnp.int32, (BLOCK, 1), 0) + t0
    c = lax.broadcasted_iota(jnp.int32, (1, BAND), 1) + start
    half = w // 2

    def window(lo, hi):
        a = jnp.maximum(lo, 0)
        b = jnp.minimum(hi + 1, S)
        return jnp.where((c >= a) & (c < b), 1.0 / (b - a).astype(F32), 0.0)

    return (0.5 * (window(r - half, r + half - 1) + window(r - half + 1, r + half))).astype(BF16)


def _pool_matrices(S):
    blocks = ((0, 0), (BLOCK, 0), (S - BLOCK, S - BAND))
    return jnp.stack([jnp.stack([_pool_matrix(t0, start, S, w) for w in POOL_WINDOWS]) for t0, start in blocks])


def _pool_spec(nb):
    return pl.BlockSpec((1, N_POOL, BLOCK, BAND), lambda i, *_: (jnp.where(i == 0, 0, jnp.where(i == nb - 1, 2, 1)), 0, 0, 0))


def _mix_core_fwd(q, k, v, pc, sink, pool_m, pool_w, pool_scale, name):
    S = q.shape[0]
    nb = S // BLOCK

    def body(sink_ref, q_ref, k_ref, v_ref, pc_ref, pm_ref, pw_ref, ps_ref, a_ref, p_ref):
        i = pl.program_id(0)
        start = _band_start(i, S)
        band = pl.ds(start, BAND)
        bias = _window_bias(i * BLOCK - start)
        kb, vb = k_ref[band, :], v_ref[band, :]
        hs = range(N_HEADS)
        ss = [_dot_nt(q_ref[:, hd * LANES:(hd + 1) * LANES], kb) + bias for hd in hs]
        ms = [jnp.maximum(jnp.max(ss[hd], axis=-1, keepdims=True), sink_ref[0, hd]) for hd in hs]
        ps = [jnp.exp(ss[hd] - ms[hd]) for hd in hs]
        invs = [1.0 / (jnp.sum(ps[hd], axis=-1, keepdims=True) + jnp.exp(sink_ref[0, hd] - ms[hd])) for hd in hs]
        outs = [_dot(ps[hd].astype(BF16), vb) for hd in hs]
        a_ref[...] = jnp.concatenate([(outs[hd] * invs[hd]).astype(BF16) for hd in hs], axis=1)
        centre = pl.ds(pl.multiple_of(i * BLOCK, BLOCK), BLOCK)
        gs = range(N_POOL)
        sl = [slice(g * POOL_GROUP, (g + 1) * POOL_GROUP) for g in gs]
        means = [_dot(pm_ref[0, g], pc_ref[band, sl[g]].astype(BF16)) for g in gs]
        devs = [(means[g] - pc_ref[centre, sl[g]]).astype(BF16) for g in gs]
        p_ref[...] = (jnp.concatenate([_dot(devs[g], pw_ref[g]) for g in gs], axis=1) * ps_ref[...]).astype(BF16)

    return _call(
        body, name, (nb,),
        [pl.BlockSpec(memory_space=pltpu.SMEM), _rows(BLOCK, Q_PAD), _full((S, KV_WIDTH)), _full((S, KV_WIDTH)),
         _full((S, POOL_WIDTH)), _pool_spec(nb), _full((N_POOL, POOL_GROUP, POOL_GROUP)), _full((1, POOL_WIDTH))],
        [_rows(BLOCK, Q_PAD), _rows(BLOCK, POOL_WIDTH)],
        [jax.ShapeDtypeStruct((S, Q_PAD), BF16), jax.ShapeDtypeStruct((S, POOL_WIDTH), BF16)],
        [], 40, (sink, q, k, v, pc, pool_m, pool_w, pool_scale))


def _mix_core_bwd(q, k, v, pc, da, dp, sink, pool_m, pool_w, pool_scale, rc, rs1, rs2, name, rider=None):
    S = q.shape[0]
    nb = S // BLOCK

    def body(sink_ref, q_ref, k_ref, v_ref, pc_ref, da_ref, dp_ref, pm_ref, pw_ref, ps_ref, c_ref, s1_ref, s2_ref,
             dq_ref, dk_ref, dv_ref, dpc_ref, dsink_ref, dpw_ref, dps_ref):
        i = pl.program_id(0)

        @pl.when(i == 0)
        def _():
            dk_ref[...] = jnp.zeros_like(dk_ref)
            dv_ref[...] = jnp.zeros_like(dv_ref)
            dpc_ref[...] = jnp.zeros_like(dpc_ref)
            dsink_ref[...] = jnp.zeros_like(dsink_ref)
            dpw_ref[...] = jnp.zeros_like(dpw_ref)
            dps_ref[...] = jnp.zeros_like(dps_ref)

        start = _band_start(i, S)
        band = pl.ds(start, BAND)
        bias = _window_bias(i * BLOCK - start)
        kb, vb = k_ref[band, :], v_ref[band, :]
        c, s1, s2 = c_ref[...], s1_ref[...], s2_ref[...]
        lane = lax.broadcasted_iota(jnp.int32, (1, LANES), 1)
        hs = range(N_HEADS)
        qs = [q_ref[:, hd * LANES:(hd + 1) * LANES] for hd in hs]
        das = [da_ref[:, hd * LANES:(hd + 1) * LANES] for hd in hs]
        ss = [_dot_nt(qs[hd], kb) + bias for hd in hs]
        d_probs = [_dot_nt(das[hd], vb) for hd in hs]
        ms = [jnp.maximum(jnp.max(ss[hd], axis=-1, keepdims=True), sink_ref[0, hd]) for hd in hs]
        ps = [jnp.exp(ss[hd] - ms[hd]) for hd in hs]
        ess = [jnp.exp(sink_ref[0, hd] - ms[hd]) for hd in hs]
        invs = [1.0 / (jnp.sum(ps[hd], axis=-1, keepdims=True) + ess[hd]) for hd in hs]
        probs = [ps[hd] * invs[hd] for hd in hs]
        deltas = [jnp.sum(probs[hd] * d_probs[hd], axis=-1, keepdims=True) for hd in hs]
        d_ss = [(probs[hd] * (d_probs[hd] - deltas[hd])).astype(BF16) for hd in hs]
        dqs = [_dot(d_ss[hd], kb) for hd in hs]
        dq_ref[...] = jnp.concatenate([_rope_bwd(dqs[hd] * SCALE, c, s1, s2).astype(BF16) for hd in hs], axis=1)
        dks = [_dot_tn(d_ss[hd], qs[hd]) for hd in hs]
        dvs = [_dot_tn(probs[hd].astype(BF16), das[hd]) for hd in hs]
        dk_ref[band, :] += functools.reduce(lambda a, b: a + b, dks)
        dv_ref[band, :] += functools.reduce(lambda a, b: a + b, dvs)
        dsink_ref[...] += functools.reduce(lambda a, b: a + b, [
            jnp.where(lane == hd, -jnp.sum(ess[hd] * invs[hd] * deltas[hd], axis=0, keepdims=True), 0.0) for hd in hs])

        centre = pl.ds(pl.multiple_of(i * BLOCK, BLOCK), BLOCK)
        gs = range(N_POOL)
        sl = [slice(g * POOL_GROUP, (g + 1) * POOL_GROUP) for g in gs]
        devs = [(_dot(pm_ref[0, g], pc_ref[band, sl[g]].astype(BF16)) - pc_ref[centre, sl[g]]).astype(BF16) for g in gs]
        dys = [dp_ref[:, sl[g]].astype(F32) for g in gs]
        zs = [_dot(devs[g], pw_ref[g]) for g in gs]
        dzs = [(dys[g] * ps_ref[:, sl[g]]).astype(BF16) for g in gs]
        d_devs = [_dot_nt(dzs[g], pw_ref[g]) for g in gs]
        dps_ref[...] += jnp.concatenate([jnp.sum(dys[g] * zs[g], axis=0, keepdims=True) for g in gs], axis=1)
        for g in gs:
            dpw_ref[g] += _dot_tn(devs[g], dzs[g])
        dpc_ref[band, :] += jnp.concatenate([_dot_tn(pm_ref[0, g], d_devs[g].astype(BF16)) for g in gs], axis=1)
        dpc_ref[centre, :] -= jnp.concatenate(d_devs, axis=1)

    return _call(
        body, name, (nb,),
        [pl.BlockSpec(memory_space=pltpu.SMEM), _rows(BLOCK, Q_PAD), _full((S, KV_WIDTH)), _full((S, KV_WIDTH)),
         _full((S, POOL_WIDTH)), _rows(BLOCK, Q_PAD), _rows(BLOCK, POOL_WIDTH), _pool_spec(nb),
         _full((N_POOL, POOL_GROUP, POOL_GROUP)), _full((1, POOL_WIDTH)),
         _rows(BLOCK, LANES), _rows(BLOCK, LANES), _rows(BLOCK, LANES)],
        [_rows(BLOCK, Q_PAD), _full((S, KV_WIDTH)), _full((S, KV_WIDTH)), _full((S, POOL_WIDTH)),
         _full((1, LANES)), _full((N_POOL, POOL_GROUP, POOL_GROUP)), _full((1, POOL_WIDTH))],
        [jax.ShapeDtypeStruct((S, Q_PAD), BF16), jax.ShapeDtypeStruct((S, KV_WIDTH), F32),
         jax.ShapeDtypeStruct((S, KV_WIDTH), F32), jax.ShapeDtypeStruct((S, POOL_WIDTH), F32),
         jax.ShapeDtypeStruct((1, LANES), F32), jax.ShapeDtypeStruct((N_POOL, POOL_GROUP, POOL_GROUP), F32),
         jax.ShapeDtypeStruct((1, POOL_WIDTH), F32)],
        [], 56, (sink, q, k, v, pc, da, dp, pool_m, pool_w, pool_scale, rc, rs1, rs2), rider)


def _mix_out(h, a, p, wa, wp, name):
    S = h.shape[0]
    tile = min(TOK_TILE, S)

    def body(h_ref, a_ref, p_ref, wa_ref, wp_ref, o_ref):
        o_ref[...] = h_ref[...] + _dot(a_ref[...], wa_ref[...]) + _dot(p_ref[...], wp_ref[...])

    return pl.pallas_call(
        body, name=name, grid=(S // tile,),
        in_specs=[_rows(tile, D_MODEL), _rows(tile, Q_PAD), _rows(tile, POOL_WIDTH),
                  _full((Q_PAD, D_MODEL)), _full((POOL_WIDTH, D_MODEL))],
        out_specs=_rows(tile, D_MODEL),
        out_shape=jax.ShapeDtypeStruct((S, D_MODEL), F32),
        compiler_params=_params(("parallel",), 40),
    )(h, a, p, wa, wp)


def _mix_out_bwd(dh_out, dn, h, gain, a, p, wa, wp, name):
    S = h.shape[0]
    tile = min(TOK_TILE, S)

    def body(do_ref, dn_ref, h_ref, g_ref, a_ref, p_ref, wa_ref, wp_ref, dh_ref, da_ref, dp_ref, dwa_ref, dwp_ref, dg_ref):
        @pl.when(pl.program_id(0) == 0)
        def _():
            dwa_ref[...] = jnp.zeros_like(dwa_ref)
            dwp_ref[...] = jnp.zeros_like(dwp_ref)
            dg_ref[...] = jnp.zeros_like(dg_ref)

        r, xh = _rms_stats(h_ref[...])
        dnorm, dg = _rms_bwd(_sum_chunks([dn_ref]), g_ref[...], r, xh)
        dh = do_ref[...] + dnorm
        dg_ref[...] += dg
        dh_ref[...] = dh
        dhb = dh.astype(BF16)
        da_ref[...] = _dot_nt(dhb, wa_ref[...]).astype(BF16)
        dp_ref[...] = _dot_nt(dhb, wp_ref[...]).astype(BF16)
        dwa_ref[...] += _dot_tn(a_ref[...], dhb)
        dwp_ref[...] += _dot_tn(p_ref[...], dhb)

    return _call(
        body, name, (S // tile,),
        [_rows(tile, D_MODEL), _chunk_rows(tile, dn.shape[0]), _rows(tile, D_MODEL), _full((1, D_MODEL)),
         _rows(tile, Q_PAD), _rows(tile, POOL_WIDTH), _full((Q_PAD, D_MODEL)), _full((POOL_WIDTH, D_MODEL))],
        [_rows(tile, D_MODEL), _rows(tile, Q_PAD), _rows(tile, POOL_WIDTH),
         _full((Q_PAD, D_MODEL)), _full((POOL_WIDTH, D_MODEL)), _full((1, D_MODEL))],
        [jax.ShapeDtypeStruct((S, D_MODEL), F32), jax.ShapeDtypeStruct((S, Q_PAD), BF16),
         jax.ShapeDtypeStruct((S, POOL_WIDTH), BF16), jax.ShapeDtypeStruct((Q_PAD, D_MODEL), F32),
         jax.ShapeDtypeStruct((POOL_WIDTH, D_MODEL), F32), jax.ShapeDtypeStruct((1, D_MODEL), F32)],
        [], 48, (dh_out, dn, h, gain, a, p, wa, wp))


def _mix_in_bwd(dh_out, h, gain, n, dq, dk, dv, dpc, rc, rs1, rs2, w_in, name):
    S = h.shape[0]
    tile = min(TOK_TILE, S)

    def body(do_ref, h_ref, g_ref, n_ref, dq_ref, dk_ref, dv_ref, dpc_ref, c_ref, s1_ref, s2_ref, w_ref,
             dh_ref, dhalf_ref, dw_ref, dg_ref):
        @pl.when(pl.program_id(0) == 0)
        def _():
            dw_ref[...] = jnp.zeros_like(dw_ref)
            dg_ref[...] = jnp.zeros_like(dg_ref)

        dk = _rope_bwd(dk_ref[...], c_ref[...], s1_ref[...], s2_ref[...]).astype(BF16)
        du = jnp.concatenate([dq_ref[...], dk, dv_ref[...].astype(BF16), dpc_ref[...].astype(BF16)], axis=1)
        dn = _dot(du, w_ref[...])
        dw_ref[...] += _dot_tn(du, n_ref[...])
        r, xh = _rms_stats(h_ref[...])
        dnorm, dg = _rms_bwd(dn, g_ref[...], r, xh)
        dh = do_ref[...] + dnorm
        dg_ref[...] += dg
        dh_ref[...] = dh
        dhalf_ref[...] = (0.5 * dh).astype(BF16)

    return _call(
        body, name, (S // tile,),
        [_rows(tile, D_MODEL), _rows(tile, D_MODEL), _full((1, D_MODEL)), _rows(tile, D_MODEL),
         _rows(tile, Q_PAD), _rows(tile, KV_WIDTH), _rows(tile, KV_WIDTH), _rows(tile, POOL_WIDTH),
         _rows(tile, LANES), _rows(tile, LANES), _rows(tile, LANES), _full((U_PAD, D_MODEL))],
        [_rows(tile, D_MODEL), _rows(tile, D_MODEL), _full((U_PAD, D_MODEL)), _full((1, D_MODEL))],
        [jax.ShapeDtypeStruct((S, D_MODEL), F32), jax.ShapeDtypeStruct((S, D_MODEL), BF16),
         jax.ShapeDtypeStruct((U_PAD, D_MODEL), F32), jax.ShapeDtypeStruct((1, D_MODEL), F32)],
        [], 56, (dh_out, h, gain, n, dq, dk, dv, dpc, rc, rs1, rs2, w_in))


def _norm_bwd(dh_out, dns, h, gain, name):
    S = h.shape[0]
    tile = min(TOK_TILE, S)
    n = len(dns)

    def body(do_ref, *refs):
        h_ref, g_ref, dh_ref, dg_ref = refs[n:]

        @pl.when(pl.program_id(0) == 0)
        def _():
            dg_ref[...] = jnp.zeros_like(dg_ref)

        r, xh = _rms_stats(h_ref[...])
        dnorm, dg = _rms_bwd(_sum_chunks(refs[:n]), g_ref[...], r, xh)
        dg_ref[...] += dg
        dh_ref[...] = do_ref[...] + dnorm

    return _call(
        body, name, (S // tile,),
        [_rows(tile, D_MODEL)] + [_chunk_rows(tile, dn.shape[0]) for dn in dns] + [_rows(tile, D_MODEL), _full((1, D_MODEL))],
        [_rows(tile, D_MODEL), _full((1, D_MODEL))],
        [jax.ShapeDtypeStruct((S, D_MODEL), F32), jax.ShapeDtypeStruct((1, D_MODEL), F32)],
        [], 40, (dh_out, *dns, h, gain))


def _rope_tables(S):
    half = ROTARY_DIM // 2
    inv_freq = ROPE_THETA ** (-jnp.arange(0, ROTARY_DIM, 2, dtype=F32) / ROTARY_DIM)
    dim = jnp.arange(LANES) % HEAD_DIM
    ang = jnp.arange(S, dtype=F32)[:, None] * inv_freq[dim % half][None, :]
    lo, hi = (dim < half)[None, :], ((dim >= half) & (dim < ROTARY_DIM))[None, :]
    c = jnp.where(lo | hi, jnp.cos(ang), 1.0)
    s1 = jnp.where(lo, -jnp.sin(ang), 0.0)
    s2 = jnp.where(hi, jnp.sin(ang), 0.0)
    return c, s1, s2


def _pad_heads(w, axis):
    w = jnp.moveaxis(w, axis, 0)
    heads = w.reshape((N_HEADS, HEAD_DIM) + w.shape[1:])
    zero = jnp.zeros_like(heads)
    first = (jnp.arange(N_HEADS) < Q_PER_KV).reshape((N_HEADS, 1) + (1,) * (w.ndim - 1))
    lo = jnp.where(first, heads, zero)
    hi = jnp.where(first, zero, heads)
    padded = jnp.concatenate([lo, hi], axis=1).reshape((Q_PAD,) + w.shape[1:])
    return jnp.moveaxis(padded, 0, axis)


def _unpad_heads(w, axis):
    w = jnp.moveaxis(w, axis, 0)
    groups = w.reshape((N_HEADS, 2, HEAD_DIM) + w.shape[1:])
    first = (jnp.arange(N_HEADS) < Q_PER_KV).reshape((N_HEADS, 1) + (1,) * (w.ndim - 1))
    heads = jnp.where(first, groups[:, 0], groups[:, 1]).reshape((ATTN_WIDTH,) + w.shape[1:])
    return jnp.moveaxis(heads, 0, axis)


IN_ROWS = IN_WIDTH // N_CHIPS
OUT_ROWS = (ATTN_WIDTH + POOL_WIDTH) // N_CHIPS
MIX_ROWS = IN_ROWS + OUT_ROWS
FFN_ROWS = 3 * FF_CHUNK


def _step(x, target, bufs, small, place, update):
    S = x.shape[0]
    rc, rs1, rs2 = _rope_tables(S)
    mine = place[1]
    order = jnp.stack([mine, mine ^ 2, mine ^ 1, mine ^ 3])
    h1, n1, gate1, up1, ffn1, mix = _ffn_fwd(order, x, small["ffn1_norm"], "ffn1_fwd",
                                             _merge(_allgather(bufs[:1], in_passes=0), _allgather(bufs[1:2])), 0)
    in_flight, (ffn2,), token = _split_start(bufs[2:], 2, _gather_copies(X_Y_DIAGONAL[:2]), "ffn2_gather_start")
    w_in_t = mix[:, :IN_ROWS].reshape(IN_WIDTH, D_MODEL)
    w_in_pad = jnp.concatenate([_pad_heads(w_in_t[:ATTN_WIDTH], 0), w_in_t[ATTN_WIDTH:]], axis=0)
    w_out = mix[:, IN_ROWS:].reshape(ATTN_WIDTH + POOL_WIDTH, D_MODEL)
    wa = _pad_heads(w_out[:ATTN_WIDTH], 0)
    wp = w_out[ATTN_WIDTH:]
    pool_w = small["pool_w"].astype(BF16)

    n2, q, k, v, pc = _mix_in(h1, small["mix_norm"] + token[0, 0], w_in_pad, rc, rs1, rs2, "mix_in")
    pool_m = _pool_matrices(S)
    a, p = _mix_core_fwd(q, k, v, pc, small["sink_logits"], pool_m, pool_w, small["pool_scale"], "mix_core_fwd")
    h2 = _mix_out(h1, a, p, wa, wp, "mix_out")
    (ffn2,) = _split_wait(in_flight, [ffn2], _gather_copies(X_Y_DIAGONAL[:2]), [h2], "ffn2_gather_wait")
    dh3, dhalf3, loss_lanes, d_final, n3, gate2, up2, ffn2 = _ffn_fwd(
        order, h2, small["ffn2_norm"], "ffn2_fwd",
        _allgather([ffn2], peers=X_Y_DIAGONAL[2:], in_passes=0, landed_before=X_Y_DIAGONAL[:2]), 0,
        loss_head=(target, small["final_norm"]))

    dn3, d_ffn2 = _ffn_bwd(jnp.arange(N_CHIPS, dtype=jnp.int32), dhalf3, n3, gate2, up2, ffn2, "ffn2_bwd")
    dh2, da, dp, dwa, dwp, d_ffn2_norm = _mix_out_bwd(dh3, dn3, h2, small["ffn2_norm"], a, p, wa, wp, "mix_out_bwd")
    dq, dk, dv, dpc, dsink, dpool_w, dpool_scale, received = _mix_core_bwd(
        q, k, v, pc, da, dp, small["sink_logits"], pool_m, pool_w, small["pool_scale"], rc, rs1, rs2, "mix_core_bwd",
        _sibling_exchange([d_ffn2]))
    pair = _pair_sum(place, d_ffn2, received, "grad_pair_sum_ffn2")
    in_flight, (pair, stack), token = _split_start([pair, received], 3, _scatter_copies, "ffn2_scatter_start")
    dh1, dhalf1, dw_in_pad, d_mix_norm = _mix_in_bwd(dh2, h1, small["mix_norm"] + token[0, 0], n2, dq, dk, dv, dpc, rc, rs1, rs2,
                                                     w_in_pad, "mix_in_bwd")
    dw_in_t = jnp.concatenate([_unpad_heads(dw_in_pad[:Q_PAD], 0), dw_in_pad[Q_PAD:]], axis=0)
    dw_out = jnp.concatenate([_unpad_heads(dwa, 0), dwp], axis=0)
    d_mix = jnp.concatenate([dw_in_t.reshape(N_CHIPS, IN_ROWS, D_MODEL), dw_out.reshape(N_CHIPS, OUT_ROWS, D_MODEL)], axis=1)
    d_mix = jnp.transpose(d_mix.reshape(N_CHIPS, 2, MIX_ROWS // 2, D_MODEL), (1, 0, 2, 3)).astype(BF16)
    small_g = {"ffn1_norm": jnp.zeros_like(d_mix_norm), "mix_norm": d_mix_norm, "ffn2_norm": d_ffn2_norm,
               "final_norm": d_final, "pool_scale": dpool_scale, "sink_logits": dsink[:, :N_HEADS], "pool_w": dpool_w}
    loss_row = jnp.sum(loss_lanes.reshape(D_MODEL // LANES, LANES), axis=0, keepdims=True)
    small_early = _pack_small(small_g, loss_row)

    chunk = [(place[1:] + 1 + p) % N_CHIPS for p in range(N_CHIPS)]
    ffn1_bwd = functools.partial(_ffn_bwd, d_out=dhalf1, n=n1, gate=gate1, up=up1, group=ffn1)
    dn_a, dw_a, recv_mix, small_all = ffn1_bwd(
        chunk[0], name="ffn1_bwd_0", rider=_merge(_sibling_exchange([d_mix]), _small_allgather(small_early)))
    pair, stack = _split_wait(in_flight, [pair, stack], _scatter_copies, [dn_a], "ffn2_scatter_wait")
    reduced_ffn2 = _chip_sum(place, pair, stack, 2, "grad_chip_sum_ffn2")
    pair_mix = _pair_sum(place, d_mix, recv_mix, "grad_pair_sum_mix")
    dn_b, dw_b, recv_a, stack_mix, g_ffn2 = ffn1_bwd(
        chunk[1], name="ffn1_bwd_1",
        rider=_merge(_merge(_sibling_exchange([dw_a]), _scatter([pair_mix])), _sibling_share([reduced_ffn2])))
    pair_a = _pair_sum(place, dw_a, recv_a, "grad_pair_sum_ffn1_0")
    reduced_mix = _chip_sum(place, pair_mix, stack_mix, 2, "grad_chip_sum_mix")
    steps = [(_scatter_step_copies(p), (p, 3)) for p in range(3)]
    sems_a, (pair_a, stack), token = _split_start([pair_a, stack], 1, steps[0][0], "ffn1_scatter_start_0")
    dn_c, dw_c, recv_b, g_mix = ffn1_bwd(
        chunk[2], name="ffn1_bwd_2", rider=_merge(_merge(_sibling_exchange([dw_b]), _sibling_share([reduced_mix])), _after(token)))
    pair_b = _pair_sum(place, dw_b, recv_b, "grad_pair_sum_ffn1_1")
    sems_b, (pair_b, stack), token = _split_start([pair_b, stack], 1, steps[1][0], "ffn1_scatter_start_1")
    dn_d, dw_d, recv_c = ffn1_bwd(chunk[3], name="ffn1_bwd_3", rider=_merge(_sibling_exchange([dw_c]), _after(token)))
    pair_c = _pair_sum(place, dw_c, recv_c, "grad_pair_sum_ffn1_2")
    own = (_exchange_copies, (4, 5))
    sems_c, (pair_c, stack, dw_d, recv_d), token = _split_start(
        [pair_c, stack, dw_d, recv_c], 2, _joined([(steps[2][0], (0, 1)), (own[0], (2, 3))]), "ffn1_scatter_start_2")
    grad_x, d_ffn1_norm = _norm_bwd(dh1, [dn_a, dn_b, dn_c, dn_d], x, small["ffn1_norm"] + token[0, 0], "norm1_bwd")
    updated = (update(GROUPS[2], g_ffn2.reshape(-1, D_MODEL), d_ffn1_norm)
               + update(GROUPS[1], g_mix.reshape(-1, D_MODEL), d_ffn1_norm))
    pair_a, pair_b, pair_c, stack, dw_d, recv_d = _split_wait(
        sems_a + sems_b + sems_c, [pair_a, pair_b, pair_c, stack, dw_d, recv_d], _joined(steps + [own]), updated,
        "ffn1_scatter_wait")
    pair_d = _pair_sum(place, dw_d, recv_d, "grad_pair_sum_ffn1_3")
    reduced_ffn1 = _chip_sum(place, pair_d, stack, 2, "grad_chip_sum_ffn1")
    g_ffn1, gains = _comm_call(_merge(_sibling_share([reduced_ffn1]), _small_allgather(d_ffn1_norm.reshape(-1, LANES))),
                               "grad_share_tail")
    gain_sum = _sum_leading(gains, 1, "gain_grad_sum")
    small_sum = jnp.concatenate([gain_sum, _sum_leading(small_all, 1, "small_grad_sum")[gain_sum.shape[0]:]], axis=0)
    update(GROUPS[0], g_ffn1.reshape(-1, D_MODEL))
    return jnp.sum(small_sum[SMALL_ROWS - 1]), grad_x, small_sum


GROUPS =(("ffn1_w_gate", "ffn1_w_up", "ffn1_w_down"), ("w_in", "w_out"), ("ffn2_w_gate", "ffn2_w_up", "ffn2_w_down"))
TRANSPOSED = ("ffn1_w_gate", "ffn1_w_up", "w_in", "ffn2_w_gate", "ffn2_w_up")


def _place():
    x, y, c = lax.axis_index("x"), lax.axis_index("y"), lax.axis_index("c")
    chips = [(1 - x, y), (x, 1 - y), (1 - x, 1 - y)]
    return x, y, c, chips


def _remote(src, dst, send_sem, recv_sem, to):
    return pltpu.make_async_remote_copy(src_ref=src, dst_ref=dst, send_sem=send_sem, recv_sem=recv_sem,
                                        device_id=to, device_id_type=MESH)


def _pack(chip, members, name):
    rows = members[0].shape[0]
    n = len(members)

    def body(chip_ref, *refs):
        ins, out_ref, buf, sems = refs[:n], refs[n], refs[n + 1], refs[n + 2]
        copies = [pltpu.make_async_copy(ins[k], buf.at[k], sems.at[k]) for k in range(n)]
        for cp in copies:
            cp.start()
        for k in range(n):
            copies[k].wait()
            out_ref[0, k * rows:(k + 1) * rows, :] = buf[k].astype(BF16)

    return pl.pallas_call(
        body, name=name,
        grid_spec=pltpu.PrefetchScalarGridSpec(
            num_scalar_prefetch=1, grid=(1,),
            in_specs=[HBM_SPEC] * n,
            out_specs=pl.BlockSpec((1, n * rows, D_MODEL), lambda k, chip_ref: (chip_ref[0], 0, 0)),
            scratch_shapes=[pltpu.VMEM((n, rows, D_MODEL), F32), pltpu.SemaphoreType.DMA((n,))]),
        out_shape=jax.ShapeDtypeStruct((N_CHIPS, n * rows, D_MODEL), BF16),
        compiler_params=_params(("arbitrary",), 40),
    )(chip, *members)


def _same(arrays):
    return [jax.ShapeDtypeStruct(a.shape, a.dtype) for a in arrays]


X_Y_DIAGONAL = (0, 1, 2)


def _allgather(bufs, peers=X_Y_DIAGONAL, in_passes=None, landed_before=()):
    n = len(bufs)

    def copy(kind, outs, send_sems, recv_sems, a, k):
        x, y, c, chips = _place()
        half = bufs[a].shape[1] // 2

        def rows(slot, core):
            return outs[a].at[slot, pl.ds(pl.multiple_of(core * half, 16), half)]

        me, slot = 2 * x + y, 2 * chips[k][0] + chips[k][1]
        over_ici = (send_sems.at[6 * a + k], recv_sems.at[6 * a + k])
        over_d2d = (send_sems.at[6 * a + 3 + k], recv_sems.at[6 * a + 3 + k])
        if kind == "first":
            return _remote(rows(me, c), rows(me, c), *over_ici, (*chips[k], c))
        if kind == "landed":
            return _remote(rows(me, c), rows(slot, c), *over_ici, (*chips[k], c))
        if kind == "passed":
            return _remote(rows(slot, c), rows(slot, c), *over_d2d, (x, y, 1 - c))
        return _remote(rows(me, c), rows(slot, 1 - c), *over_d2d, (x, y, 1 - c))

    def arrive(pairs):
        def hook(ins, outs, sems):
            for a, k in pairs:
                if k not in landed_before:
                    copy("landed", outs, *sems, a, k).wait_recv()
                    copy("passed", outs, *sems, a, k).start()
            for a, k in pairs:
                copy("handed", outs, *sems, a, k).wait_recv()
        return hook

    sent = [(a, k) for a in range(n) for k in peers]
    everything = sent + [(in_passes, k) for k in landed_before]
    early = [(a, k) for a, k in everything if a == in_passes]

    def start(ins, outs, sems):
        for k in landed_before:
            copy("passed", outs, *sems, in_passes, k).start()
        for a, k in sent:
            copy("first", outs, *sems, a, k).start()

    def finish(ins, outs, sems):
        arrive([pair for pair in everything if pair not in early])(ins, outs, sems)
        for a, k in sent:
            copy("first", outs, *sems, a, k).wait_send()
        for a, k in everything:
            copy("passed", outs, *sems, a, k).wait_send()

    hooks = tuple(((k + 1, 0), arrive([(a, k)])) for a, k in early)
    return _Rider(list(bufs), _same(bufs), {a: a for a in range(n)},
                  [pltpu.SemaphoreType.DMA((6 * n,)), pltpu.SemaphoreType.DMA((6 * n,))], start, finish, hooks)


SEM_SPEC = pl.BlockSpec(memory_space=pltpu.SEMAPHORE)


def _split_start(arrays, n_copies, make_copies, name):
    m, n = len(arrays), 2 * n_copies

    def body(*refs):
        sems, thru, token = refs[m:m + n], refs[m + n:2 * m + n], refs[2 * m + n]
        for cp in make_copies(thru, sems, False):
            cp.start()
        token[...] = jnp.zeros_like(token)

    outs = pl.pallas_call(
        body, name=name,
        out_shape=[pltpu.SemaphoreType.DMA(())] * n + _same(arrays) + [jax.ShapeDtypeStruct((SUBLANES, LANES), F32)],
        in_specs=[HBM_SPEC] * m, out_specs=[SEM_SPEC] * n + [HBM_SPEC] * m + [pl.BlockSpec(memory_space=pltpu.VMEM)],
        input_output_aliases={a: n + a for a in range(m)},
        compiler_params=pltpu.CompilerParams(has_side_effects=pltpu.SideEffectType.DATAFLOW_SIDE_EFFECTING),
    )(*arrays)
    return list(outs[:n]), list(outs[n:n + m]), outs[n + m]


def _split_wait(sems, arrays, make_copies, after, name):
    m, n = len(arrays), len(sems)

    def body(*refs):
        for cp in make_copies(refs[m + n + len(after):], refs[m:m + n], True):
            cp.wait_send()
            cp.wait_recv()

    return pl.pallas_call(
        body, name=name, out_shape=_same(arrays),
        in_specs=[HBM_SPEC] * m + [SEM_SPEC] * n + [pl.BlockSpec(memory_space=pl.ANY)] * len(after),
        out_specs=[HBM_SPEC] * m, input_output_aliases={a: a for a in range(m)},
        compiler_params=pltpu.CompilerParams(has_side_effects=pltpu.SideEffectType.DATAFLOW_SIDE_EFFECTING),
    )(*arrays, *sems, *after)


def _gather_copies(peers):
    def make(refs, sems, landing):
        x, y, c, chips = _place()
        half = refs[0].shape[1] // 2

        def rows(slot):
            return refs[0].at[slot, pl.ds(pl.multiple_of(c * half, 16), half)]

        me = 2 * x + y
        return [_remote(rows(me), rows(2 * chips[k][0] + chips[k][1] if landing else me), sems[2 * j], sems[2 * j + 1],
                        (*chips[k], c)) for j, k in enumerate(peers)]
    return make


def _scatter_copies(refs, sems, landing):
    x, y, c, chips = _place()
    me = 2 * x + y
    slots = [2 * cx + cy for cx, cy in chips]
    return [_remote(refs[0].at[slots[k]], refs[1].at[slots[k] if landing else me], sems[2 * k], sems[2 * k + 1], (*chips[k], c))
            for k in X_Y_DIAGONAL]


def _scatter_step_copies(step):
    def make(refs, sems, landing):
        x, y, c, _ = _place()
        me = 2 * x + y
        to = (me + 1 + step) % N_CHIPS
        frm = (me + N_CHIPS - 1 - step) % N_CHIPS
        peer = frm if landing else to
        return [_remote(refs[0].at[0], refs[1].at[frm if landing else me], sems[0], sems[1], (peer // 2, peer % 2, c))]
    return make


def _exchange_copies(refs, sems, landing):
    x, y, c, _ = _place()
    return [_remote(refs[0].at[1 - c], refs[1], sems[0], sems[1], (x, y, 1 - c))]


def _joined(makers):
    def make(refs, sems, landing):
        copies, at = [], 0
        for maker, places in makers:
            mine = maker([refs[p] for p in places], sems[at:], landing)
            copies += mine
            at += 2 * len(mine)
        return copies
    return make


def _sibling_exchange(parts):
    n = len(parts)

    def copies(ins, outs, send_sems, recv_sems):
        x, y, c, _ = _place()
        return [_remote(ins[a].at[1 - c], outs[a], send_sems.at[a], recv_sems.at[a], (x, y, 1 - c)) for a in range(n)]

    def start(ins, outs, sems):
        for cp in copies(ins, outs, *sems):
            cp.start()

    def finish(ins, outs, sems):
        for cp in copies(ins, outs, *sems):
            cp.wait_recv()
            cp.wait_send()

    return _Rider(list(parts), [jax.ShapeDtypeStruct(p.shape[1:], p.dtype) for p in parts], {},
                  [pltpu.SemaphoreType.DMA((n,)), pltpu.SemaphoreType.DMA((n,))], start, finish)


def _small_allgather(small):
    flips = [(fx, fy, fc) for fx in range(2) for fy in range(2) for fc in range(2)][1:]

    def copies(small_ref, gather_ref, send_sems, recv_sems, local_sem, started_only=False):
        x, y, c, _ = _place()
        me = 4 * x + 2 * y + c
        peers = [((1 - x) if fx else x, (1 - y) if fy else y, (1 - c) if fc else c) for fx, fy, fc in flips]
        own = pltpu.make_async_copy(small_ref, gather_ref.at[me], local_sem)
        sent = [_remote(small_ref, gather_ref.at[me], send_sems.at[k], recv_sems.at[k], peer) for k, peer in enumerate(peers)]
        if started_only:
            return own, sent
        landed = [_remote(small_ref, gather_ref.at[4 * px + 2 * py + pc], send_sems.at[k], recv_sems.at[k], (px, py, pc))
                  for k, (px, py, pc) in enumerate(peers)]
        return own, sent, landed

    def start(ins, outs, sems):
        own, sent = copies(ins[0], outs[0], *sems, started_only=True)
        own.start()
        for cp in sent:
            cp.start()

    def finish(ins, outs, sems):
        own, sent, landed = copies(ins[0], outs[0], *sems)
        for cp in landed:
            cp.wait_recv()
        for cp in sent:
            cp.wait_send()
        own.wait()

    return _Rider([small], [jax.ShapeDtypeStruct((2 * N_CHIPS,) + small.shape, small.dtype)], {},
                  [pltpu.SemaphoreType.DMA((7,)), pltpu.SemaphoreType.DMA((7,)), pltpu.SemaphoreType.DMA], start, finish)


def _merge(a, b):
    na, nao, nas = len(a.operands), len(a.out_shapes), len(a.scratch)

    def start(ins, outs, sems):
        a.start(ins[:na], outs[:nao], sems[:nas])
        b.start(ins[na:], outs[nao:], sems[nas:])

    def finish(ins, outs, sems):
        a.finish(ins[:na], outs[:nao], sems[:nas])
        b.finish(ins[na:], outs[nao:], sems[nas:])

    def of_a(fn):
        return lambda ins, outs, sems: fn(ins[:na], outs[:nao], sems[:nas])

    def of_b(fn):
        return lambda ins, outs, sems: fn(ins[na:], outs[nao:], sems[nas:])

    aliases = {**a.aliases, **{na + k: nao + v for k, v in b.aliases.items()}}
    hooks = tuple((at, of_a(fn)) for at, fn in a.hooks) + tuple((at, of_b(fn)) for at, fn in b.hooks)
    return _Rider(a.operands + b.operands, a.out_shapes + b.out_shapes, aliases, a.scratch + b.scratch, start, finish, hooks)


def _scatter(sums):
    n = len(sums)

    def copies(ins, outs, send_sems, recv_sems, started_only=False):
        x, y, c, chips = _place()
        me = 2 * x + y
        slots = [2 * cx + cy for cx, cy in chips]
        sent = [_remote(ins[a].at[slots[k]], outs[a].at[me], send_sems.at[3 * a + k], recv_sems.at[3 * a + k], (*chips[k], c))
                for a in range(n) for k in X_Y_DIAGONAL]
        if started_only:
            return sent
        landed = [_remote(ins[a].at[slots[k]], outs[a].at[slots[k]], send_sems.at[3 * a + k], recv_sems.at[3 * a + k],
                          (*chips[k], c)) for a in range(n) for k in X_Y_DIAGONAL]
        return sent, landed

    def start(ins, outs, sems):
        for cp in copies(ins, outs, *sems, started_only=True):
            cp.start()

    def finish(ins, outs, sems):
        sent, landed = copies(ins, outs, *sems)
        for cp in landed:
            cp.wait_recv()
        for cp in sent:
            cp.wait_send()

    return _Rider(list(sums), _same(sums), {}, [pltpu.SemaphoreType.DMA((3 * n,)), pltpu.SemaphoreType.DMA((3 * n,))],
                  start, finish)


def _sibling_share(bufs):
    n = len(bufs)

    def copies(outs, send_sems, recv_sems, started_only=False):
        x, y, c, _ = _place()
        sent = [_remote(outs[a].at[c], outs[a].at[c], send_sems.at[a], recv_sems.at[a], (x, y, 1 - c)) for a in range(n)]
        if started_only:
            return sent
        landed = [_remote(outs[a].at[c], outs[a].at[1 - c], send_sems.at[a], recv_sems.at[a], (x, y, 1 - c)) for a in range(n)]
        return sent, landed

    def start(ins, outs, sems):
        for cp in copies(outs, *sems, started_only=True):
            cp.start()

    def finish(ins, outs, sems):
        sent, landed = copies(outs, *sems)
        for cp in landed:
            cp.wait_recv()
        for cp in sent:
            cp.wait_send()

    return _Rider(list(bufs), _same(bufs), {a: a for a in range(n)},
                  [pltpu.SemaphoreType.DMA((n,)), pltpu.SemaphoreType.DMA((n,))], start, finish)


def _pair_sum(core, part, received, name):
    _, k, rh, cols = part.shape

    def body(core_ref, p_ref, r_ref, o_ref):
        o_ref[...] = (p_ref[0].astype(F32) + r_ref[...].astype(F32)).astype(BF16)

    return pl.pallas_call(
        body, name=name,
        grid_spec=pltpu.PrefetchScalarGridSpec(
            num_scalar_prefetch=1, grid=(k,),
            in_specs=[pl.BlockSpec((1, 1, rh, cols), lambda j, core_ref: (core_ref[0], j, 0, 0)),
                      pl.BlockSpec((1, rh, cols), lambda j, core_ref: (j, 0, 0))],
            out_specs=pl.BlockSpec((1, rh, cols), lambda j, core_ref: (j, 0, 0))),
        out_shape=jax.ShapeDtypeStruct((k, rh, cols), BF16),
        compiler_params=_params(("parallel",), 32),
    )(core, part, received)


def _sum_leading(stack, steps, name):
    k, rows, cols = stack.shape
    tile = rows // steps

    def body(s_ref, o_ref):
        total = s_ref[0].astype(F32)
        for d in range(1, k):
            total = total + s_ref[d].astype(F32)
        o_ref[...] = total

    return pl.pallas_call(
        body, name=name, grid=(steps,),
        in_specs=[pl.BlockSpec((k, tile, cols), lambda i: (0, i, 0))],
        out_specs=pl.BlockSpec((tile, cols), lambda i: (i, 0)),
        out_shape=jax.ShapeDtypeStruct((rows, cols), F32),
        compiler_params=_params(("parallel",), 32),
    )(stack)


def _chip_sum(place, own, stack, steps, name):
    k, rows, cols = stack.shape
    tile = rows // steps

    def body(place_ref, own_ref, *refs):
        chip = place_ref[1]
        total = None
        for d in range(k):
            term = jnp.where(chip == d, own_ref[0], refs[d][0]).astype(F32)
            total = term if total is None else total + term
        refs[k][0] = total

    def other(d):
        return lambda i, place_ref: (jnp.where(place_ref[1] == d, (d + 1) % k, d), i, 0)

    return pl.pallas_call(
        body, name=name,
        grid_spec=pltpu.PrefetchScalarGridSpec(
            num_scalar_prefetch=1, grid=(steps,),
            in_specs=[pl.BlockSpec((1, tile, cols), lambda i, place_ref: (place_ref[1] % own.shape[0], i, 0))]
            + [pl.BlockSpec((1, tile, cols), other(d)) for d in range(k)],
            out_specs=pl.BlockSpec((1, tile, cols), lambda i, place_ref: (place_ref[0], i, 0))),
        out_shape=jax.ShapeDtypeStruct((2, rows, cols), F32),
        compiler_params=_params(("arbitrary",), 32),
    )(place, own, *([stack] * k))


def _adamw(w, g, row0, m, v, tile, name, after=None):
    rows, cols = w.shape
    first = row0 // tile
    assert rows % tile == 0 and row0 % tile == 0
    bc1 = 1.0 - ADAM_B1 ** ADAM_STEP
    bc2 = 1.0 - ADAM_B2 ** ADAM_STEP

    def body(w_ref, g_ref, m_ref, v_ref, *refs):
        go_ref, d_ref, mo_ref, vo_ref = refs[-4:]
        g = g_ref[...]
        m_new = ADAM_B1 * m_ref[...] + (1.0 - ADAM_B1) * g
        v_new = ADAM_B2 * v_ref[...] + (1.0 - ADAM_B2) * (g * g)
        go_ref[...] = g
        d_ref[...] = -ADAM_LR * ((m_new / bc1) / (jnp.sqrt(v_new / bc2) + ADAM_EPS) + ADAM_WD * w_ref[...])
        mo_ref[...] = m_new
        vo_ref[...] = v_new

    later = [] if after is None else [after]
    spec = pl.BlockSpec((tile, cols), lambda i: (i, 0))
    g_spec = pl.BlockSpec((tile, cols), lambda i: (first + i, 0))
    return pl.pallas_call(
        body, name=name, grid=(rows // tile,),
        in_specs=[spec, g_spec, spec, spec] + [pl.BlockSpec(memory_space=pl.ANY)] * len(later), out_specs=[spec] * 4,
        out_shape=[jax.ShapeDtypeStruct((rows, cols), F32)] * 4,
        compiler_params=_params(("parallel",), 32),
    )(w, g, m, v, *later)


SMALL = ("ffn1_norm", "mix_norm", "ffn2_norm", "final_norm", "pool_scale", "sink_logits", "pool_w")


def _pack_small(d, last_row=None):
    def part(n):
        flat = d[n].reshape(-1)
        flat = jnp.pad(flat, (0, -flat.shape[0] % (SUBLANES * LANES)))
        return flat.reshape(-1, LANES)

    last = jnp.zeros((SUBLANES, LANES), F32) if last_row is None else jnp.pad(last_row, ((SUBLANES - 1, 0), (0, 0)))
    packed = jnp.concatenate([part(n) for n in SMALL] + [last], axis=0)
    assert packed.shape[0] == SMALL_ROWS
    return packed


def _unpack_small(packed, like):
    out, row = {}, 0
    for n in SMALL:
        size = math.prod(like[n].shape)
        rows = -(-size // (SUBLANES * LANES)) * SUBLANES
        out[n] = packed[row:row + rows].reshape(-1)[:size].reshape(like[n].shape)
        row += rows
    return out


def kernel(x, ffn1_norm, ffn1_w_gate, ffn1_w_up, ffn1_w_down, mix_norm, w_in, sink_logits, pool_w, pool_scale, w_out, ffn2_norm, ffn2_w_gate, ffn2_w_up, ffn2_w_down, final_norm, loss_target, m_ffn1_norm, m_ffn1_w_gate, m_ffn1_w_up, m_ffn1_w_down, m_mix_norm, m_w_in, m_sink_logits, m_pool_w, m_pool_scale, m_w_out, m_ffn2_norm, m_ffn2_w_gate, m_ffn2_w_up, m_ffn2_w_down, m_final_norm, v_ffn1_norm, v_ffn1_w_gate, v_ffn1_w_up, v_ffn1_w_down, v_mix_norm, v_w_in, v_sink_logits, v_pool_w, v_pool_scale, v_w_out, v_ffn2_norm, v_ffn2_w_gate, v_ffn2_w_up, v_ffn2_w_down, v_final_norm):
    names = ("ffn1_norm", "ffn1_w_gate", "ffn1_w_up", "ffn1_w_down", "mix_norm", "w_in", "sink_logits", "pool_w",
             "pool_scale", "w_out", "ffn2_norm", "ffn2_w_gate", "ffn2_w_up", "ffn2_w_down", "final_norm")
    weights = dict(zip(names, (ffn1_norm, ffn1_w_gate, ffn1_w_up, ffn1_w_down, mix_norm, w_in, sink_logits, pool_w,
                               pool_scale, w_out, ffn2_norm, ffn2_w_gate, ffn2_w_up, ffn2_w_down, final_norm)))
    mom1 = dict(zip(names, (m_ffn1_norm, m_ffn1_w_gate, m_ffn1_w_up, m_ffn1_w_down, m_mix_norm, m_w_in, m_sink_logits,
                            m_pool_w, m_pool_scale, m_w_out, m_ffn2_norm, m_ffn2_w_gate, m_ffn2_w_up, m_ffn2_w_down,
                            m_final_norm)))
    mom2 = dict(zip(names, (v_ffn1_norm, v_ffn1_w_gate, v_ffn1_w_up, v_ffn1_w_down, v_mix_norm, v_w_in, v_sink_logits,
                            v_pool_w, v_pool_scale, v_w_out, v_ffn2_norm, v_ffn2_w_gate, v_ffn2_w_up, v_ffn2_w_down,
                            v_final_norm)))
    chip = (2 * lax.axis_index("x") + lax.axis_index("y")).astype(jnp.int32).reshape(1)
    place = jnp.concatenate([lax.axis_index("c").astype(jnp.int32).reshape(1), chip])

    def rows_of(t, n):
        return jnp.swapaxes(t[n][0], 0, 1) if n in TRANSPOSED else t[n][0]

    bufs = [_pack(chip, [rows_of(weights, n) for n in GROUPS[0]], "pack_ffn1"),
            _pack(chip, [jnp.concatenate([rows_of(weights, n) for n in GROUPS[1]], axis=0)], "pack_mix"),
            _pack(chip, [rows_of(weights, n) for n in GROUPS[2]], "pack_ffn2")]

    small_w = {"ffn1_norm": ffn1_norm, "mix_norm": mix_norm, "ffn2_norm": ffn2_norm,
               "final_norm": final_norm.reshape(1, D_MODEL), "pool_scale": pool_scale, "sink_logits": sink_logits,
               "pool_w": pool_w[0]}
    out_g, out_d, out_m, out_v = {}, {}, {}, {}

    def update(members, g, after=None):
        row0, done = 0, []
        for n in members:
            w = rows_of(weights, n)
            tile = FF_CHUNK // 4 if w.shape[0] == FF_CHUNK else math.gcd(IN_ROWS, OUT_ROWS)
            outs = _adamw(w, g, row0, rows_of(mom1, n), rows_of(mom2, n), tile, "adamw_" + n, after)
            row0 += w.shape[0]
            done.append(outs[0])
            for dst, t in zip((out_g, out_d, out_m, out_v), outs):
                dst[n] = (jnp.swapaxes(t, 0, 1) if n in TRANSPOSED else t).reshape(weights[n].shape)
        return done

    loss, grad_x, small_sum = _step(x[0], loss_target[0], bufs, small_w, place, update)
    small_outs = _adamw(_pack_small(weights), small_sum, 0, _pack_small(mom1), _pack_small(mom2), SMALL_ROWS, "adamw_small")
    for dst, packed in zip((out_g, out_d, out_m, out_v), small_outs):
        dst.update(_unpack_small(packed, weights))

    return (loss,grad_x.reshape(x.shape), *[out_g[n] for n in names], *[out_d[n] for n in names],
            *[out_m[n] for n in names], *[out_v[n] for n in names])
```

```python
import collections
import functools
import math

import jax
import jax.numpy as jnp
from jax import lax
from jax.experimental import pallas as pl
from jax.experimental.pallas import tpu as pltpu

F32, BF16 = jnp.float32, jnp.bfloat16
MESH = pl.DeviceIdType.MESH

D_MODEL = 1024
D_FF = 2816
N_CHIPS = 4
FF_CHUNK = D_FF // N_CHIPS
HEAD_DIM = 64
N_HEADS = 8
N_KV = 2
Q_PER_KV = N_HEADS // N_KV
KV_WIDTH = N_KV * HEAD_DIM
ATTN_WIDTH = N_HEADS * HEAD_DIM
POOL_WINDOWS = (2, 4, 8, 16)
N_POOL = len(POOL_WINDOWS)
POOL_GROUP = 128
POOL_WIDTH = N_POOL * POOL_GROUP
IN_WIDTH = ATTN_WIDTH + 2 * KV_WIDTH + POOL_WIDTH
WINDOW = 128
BLOCK = 128
BAND = 3 * BLOCK
ROPE_THETA = 500000.0
ROTARY_DIM = HEAD_DIM // 4
EPS = 1e-6
LANES = 128
Q_PAD = N_HEADS * LANES
U_PAD = Q_PAD + 2 * KV_WIDTH + POOL_WIDTH
SCALE = HEAD_DIM ** -0.5
NEG = -1e30

ADAM_LR, ADAM_B1, ADAM_B2, ADAM_EPS, ADAM_WD, ADAM_STEP = 0.001, 0.9, 0.999, 1e-08, 0.01, 10

V7X_VMEM_BYTES = 64 * 1024 * 1024
TOK_TILE = 512
SUBLANES = 8
SMALL_ROWS = 568


def _params(sem, vmem_mb):
    assert vmem_mb * 1024 * 1024 <= V7X_VMEM_BYTES
    return pltpu.CompilerParams(dimension_semantics=sem, vmem_limit_bytes=vmem_mb * 1024 * 1024)


def _dot(a, b):
    return lax.dot_general(a, b, (((1,), (0,)), ((), ())), preferred_element_type=F32)


def _dot_nt(a, b):
    return lax.dot_general(a, b, (((1,), (1,)), ((), ())), preferred_element_type=F32)


def _dot_tn(a, b):
    return lax.dot_general(a, b, (((0,), (0,)), ((), ())), preferred_element_type=F32)


def _rms_stats(h):
    r = lax.rsqrt(jnp.mean(h * h, axis=-1, keepdims=True) + EPS)
    return r, h * r


def _rms_bwd(dn, g, r, xh):
    gd = dn * g
    dh = r * (gd - xh * jnp.mean(gd * xh, axis=-1, keepdims=True))
    return dh, jnp.sum(dn * xh, axis=0, keepdims=True)


def _rope(x, c, s1, s2):
    return x * c + pltpu.roll(x, LANES - ROTARY_DIM // 2, 1) * s1 + pltpu.roll(x, ROTARY_DIM // 2, 1) * s2


def _rope_bwd(d, c, s1, s2):
    return d * c + pltpu.roll(d * s1, ROTARY_DIM // 2, 1) + pltpu.roll(d * s2, LANES - ROTARY_DIM // 2, 1)


def _sum_chunks(refs):
    terms = [ref[j].astype(F32) for ref in refs for j in range(ref.shape[0])]
    return functools.reduce(lambda a, b: a + b, terms)


def _chunk_rows(tile, k):
    return pl.BlockSpec((k, tile, D_MODEL), lambda i, *_: (0, i, 0))


def _full(shape):
    nd = len(shape)
    return pl.BlockSpec(shape, lambda *_: (0,) * nd)


def _rows(tile, cols):
    return pl.BlockSpec((tile, cols), lambda i, *_: (i, 0))


HBM_SPEC = pl.BlockSpec(memory_space=pltpu.HBM)

_Rider = collections.namedtuple("_Rider", "operands out_shapes aliases scratch start finish hooks", defaults=[()])


_NO_RIDER = _Rider([], [], {}, [], None, None)


def _call(body, name, grid, in_specs, out_specs, out_shape, scratch, vmem_mb, args, rider=None, prefetch=(),
          shares_rider_refs=False):
    rider = rider or _NO_RIDER
    n_pre, n_in, n_out, n_scr = len(prefetch), len(in_specs), len(out_specs), len(scratch)
    r_in, r_out = len(rider.operands), len(rider.out_shapes)

    def fused(*refs):
        pre, refs = refs[:n_pre], refs[n_pre:]
        ins, refs = refs[:n_in], refs[n_in:]
        r_ins, refs = refs[:r_in], refs[r_in:]
        outs, refs = refs[:n_out], refs[n_out:]
        r_outs, refs = refs[:r_out], refs[r_out:]
        scr, r_scr = refs[:n_scr], refs[n_scr:]
        ids = [pl.program_id(d) for d in range(len(grid))]
        if rider.start is not None:
            @pl.when(functools.reduce(jnp.logical_and, [i == 0 for i in ids]))
            def _():
                rider.start(r_ins, r_outs, r_scr)

        for at, hook in rider.hooks:
            @pl.when(functools.reduce(jnp.logical_and, [i == a % g for i, a, g in zip(ids, at, grid)]))
            def _(hook=hook):
                hook(r_ins, r_outs, r_scr)

        if shares_rider_refs:
            body(*pre, *ins, *outs, *scr, rider_refs=r_outs)
        else:
            body(*pre, *ins, *outs, *scr)

        if rider.finish is not None:
            @pl.when(functools.reduce(jnp.logical_and, [i == g - 1 for i, g in zip(ids, grid)]))
            def _():
                rider.finish(r_ins, r_outs, r_scr)

    return pl.pallas_call(
        fused, name=name,
        grid_spec=pltpu.PrefetchScalarGridSpec(
            num_scalar_prefetch=n_pre, grid=grid,
            in_specs=list(in_specs) + [HBM_SPEC] * r_in, out_specs=list(out_specs) + [HBM_SPEC] * r_out,
            scratch_shapes=list(scratch) + list(rider.scratch)),
        out_shape=list(out_shape) + list(rider.out_shapes),
        input_output_aliases={n_pre + n_in + k: n_out + v for k, v in rider.aliases.items()},
        compiler_params=_params(("arbitrary",) * len(grid), vmem_mb),
    )(*prefetch, *args, *rider.operands)


def _after(token):
    return _Rider([token], [], {}, [], lambda *_: None, lambda *_: None)


def _comm_call(rider, name):
    r_in, r_out = len(rider.operands), len(rider.out_shapes)

    def body(*refs):
        r_ins, r_outs, r_scr = refs[:r_in], refs[r_in:r_in + r_out], refs[r_in + r_out:]
        rider.start(r_ins, r_outs, r_scr)
        rider.finish(r_ins, r_outs, r_scr)

    return pl.pallas_call(
        body, name=name, in_specs=[HBM_SPEC] * r_in, out_specs=[HBM_SPEC] * r_out, out_shape=list(rider.out_shapes),
        input_output_aliases=dict(rider.aliases), scratch_shapes=list(rider.scratch),
    )(*rider.operands)


def _ffn_fwd(order, h, gain, name, rider, group_at, loss_head=None):
    S = h.shape[0]
    tile = min(TOK_TILE, S)
    nt = S // tile
    last = N_CHIPS - 1
    n_extra_in, n_head_out = (2, 4) if loss_head else (0, 1)

    def body(order_ref, h_ref, g_ref, *refs, rider_refs):
        extra_in, refs = refs[:n_extra_in], refs[n_extra_in:]
        head_out, (n_ref, gate_ref, up_ref, w_scr, w_sem, acc, n_scr) = refs[:n_head_out], refs[n_head_out:]
        j, i = pl.program_id(0), pl.program_id(1)

        @pl.when(i == 0)
        def _():
            fetch = pltpu.make_async_copy(rider_refs[group_at].at[order_ref[j]], w_scr, w_sem)
            fetch.start()
            fetch.wait()

        at = pl.multiple_of(i * tile, tile)

        @pl.when(j == 0)
        def _():
            _, xh = _rms_stats(h_ref[...])
            n = (xh * g_ref[...]).astype(BF16)
            n_scr[pl.ds(at, tile), :] = n
            n_ref[...] = n
            acc[pl.ds(at, tile), :] = jnp.zeros((tile, D_MODEL), F32)

        half = tile // 2
        wg, wu, wd = (w_scr[part * FF_CHUNK:(part + 1) * FF_CHUNK, :] for part in range(3))
        ns = [n_scr[pl.ds(at + s * half, half), :] for s in range(2)]
        gates = [_dot_nt(n, wg) for n in ns]
        ups = [_dot_nt(n, wu) for n in ns]
        acts = [(g * jax.nn.sigmoid(g) * u).astype(BF16) for g, u in zip(gates, ups)]
        for s in range(2):
            gate_ref[0, s * half:(s + 1) * half, :] = gates[s].astype(BF16)
            up_ref[0, s * half:(s + 1) * half, :] = ups[s].astype(BF16)
            acc[pl.ds(at + s * half, half), :] += _dot(acts[s], wd)

        @pl.when(j == last)
        def _():
            out = h_ref[...] + 0.5 * acc[pl.ds(at, tile), :]
            if not loss_head:
                head_out[0][...] = out
                return
            (t_ref, gf_ref), (dh_ref, dhalf_ref, loss_ref, dg_ref) = extra_in, head_out

            @pl.when(i == 0)
            def _():
                loss_ref[...] = jnp.zeros_like(loss_ref)
                dg_ref[...] = jnp.zeros_like(dg_ref)

            gf = gf_ref[...]
            r, xh = _rms_stats(out)
            err = xh * gf - t_ref[...]
            loss_ref[...] += (0.5 / D_MODEL) * jnp.sum(err * err, axis=0, keepdims=True)
            dh, dg = _rms_bwd(err * (1.0 / D_MODEL), gf, r, xh)
            dg_ref[...] += dg
            dh_ref[...] = dh
            dhalf_ref[...] = (0.5 * dh).astype(BF16)

    tok = pl.BlockSpec((tile, D_MODEL), lambda j, i, order_ref: (i, 0))
    hid = pl.BlockSpec((1, tile, FF_CHUNK), lambda j, i, order_ref: (order_ref[j], i, 0))
    row = pl.BlockSpec((1, D_MODEL), lambda j, i, order_ref: (0, 0))
    in_last = pl.BlockSpec((tile, D_MODEL), lambda j, i, order_ref: (jnp.where(j == last, i, 0), 0))
    in_first = pl.BlockSpec((tile, D_MODEL), lambda j, i, order_ref: (jnp.where(j == 0, i, nt - 1), 0))
    tok_f32, tok_bf16, lanes = (jax.ShapeDtypeStruct((S, D_MODEL), F32), jax.ShapeDtypeStruct((S, D_MODEL), BF16),
                                jax.ShapeDtypeStruct((1, D_MODEL), F32))
    hidden = jax.ShapeDtypeStruct((N_CHIPS, S, FF_CHUNK), BF16)
    if loss_head:
        extra_specs, extra_args = [in_last, row], list(loss_head)
        head_specs, head_shapes = [in_last, in_last, row, row], [tok_f32, tok_bf16, lanes, lanes]
    else:
        extra_specs, extra_args, head_specs, head_shapes = [], [], [in_last], [tok_f32]
    return _call(
        body, name, (N_CHIPS, nt), [tok, row] + extra_specs, head_specs + [in_first, hid, hid],
        head_shapes + [tok_bf16, hidden, hidden],
        [pltpu.VMEM((3 * FF_CHUNK, D_MODEL), BF16), pltpu.SemaphoreType.DMA, pltpu.VMEM((S, D_MODEL), F32),
         pltpu.VMEM((S, D_MODEL), BF16)], 58, (h, gain, *extra_args), rider, (order,), shares_rider_refs=True)


def _ffn_bwd(chunks, d_out, n, gate, up, group, name, rider=None):
    S = n.shape[0]
    n_chunks = chunks.shape[0]
    tile = min(TOK_TILE, S)
    nt = S // tile
    half_rows = 3 * FF_CHUNK // 2
    cut = FF_CHUNK // 2

    def body(chunks_ref, do_ref, n_ref, gate_ref, up_ref, wg_ref, wu_ref, wd_ref, dn_ref, dw_ref, acc_g, acc_u, acc_d):
        j, i = pl.program_id(0), pl.program_id(1)

        @pl.when(i == 0)
        def _():
            acc_g[...] = jnp.zeros_like(acc_g)
            acc_u[...] = jnp.zeros_like(acc_u)
            acc_d[...] = jnp.zeros_like(acc_d)

        halves = [pl.ds(s * (tile // 2), tile // 2) for s in range(2)]
        dos = [do_ref[rows, :] for rows in halves]
        d_acts = [_dot_nt(do, wd_ref[0]) for do in dos]
        gs = [gate_ref[0, rows, :].astype(F32) for rows in halves]
        us = [up_ref[0, rows, :].astype(F32) for rows in halves]
        sigs = [jax.nn.sigmoid(g) for g in gs]
        silus = [g * sig for g, sig in zip(gs, sigs)]
        d_ups = [(d_act * silu).astype(BF16) for d_act, silu in zip(d_acts, silus)]
        d_gates = [(d_act * u * (sig * (1.0 + g * (1.0 - sig)))).astype(BF16) for d_act, u, sig, g in zip(d_acts, us, sigs, gs)]
        for rows, d_gate, d_up in zip(halves, d_gates, d_ups):
            dn_ref[0, rows, :] = (_dot(d_gate, wg_ref[0]) + _dot(d_up, wu_ref[0])).astype(BF16)
        d_gate, d_up = jnp.concatenate(d_gates, axis=0), jnp.concatenate(d_ups, axis=0)
        act = jnp.concatenate([(silu * u).astype(BF16) for silu, u in zip(silus, us)], axis=0)
        nn = n_ref[...]
        acc_g[...] += _dot_tn(d_gate, nn)
        acc_u[...] += _dot_tn(d_up, nn)
        acc_d[...] += _dot_tn(act, do_ref[...])

        @pl.when(i == nt - 1)
        def _():
            dw_ref[0, 0, :FF_CHUNK, :] = acc_g[...].astype(BF16)
            dw_ref[0, 0, FF_CHUNK:, :] = acc_u[:cut, :].astype(BF16)
            dw_ref[1, 0, :cut, :] = acc_u[cut:, :].astype(BF16)
            dw_ref[1, 0, cut:, :] = acc_d[...].astype(BF16)

    tok = pl.BlockSpec((tile, D_MODEL), lambda j, i, chunks_ref: (i, 0))
    hid = pl.BlockSpec((1, tile, FF_CHUNK), lambda j, i, chunks_ref: (chunks_ref[j], i, 0))
    return _call(
        body, name, (n_chunks, nt),
        [tok, tok, hid, hid]
        + [pl.BlockSpec((1, FF_CHUNK, D_MODEL), functools.partial(lambda j, i, chunks_ref, part: (chunks_ref[j], part, 0), part=part))
           for part in range(3)],
        [pl.BlockSpec((1, tile, D_MODEL), lambda j, i, chunks_ref: (j, i, 0)),
         pl.BlockSpec((2, 1, half_rows, D_MODEL), lambda j, i, chunks_ref: (0, j, 0, 0))],
        [jax.ShapeDtypeStruct((n_chunks, S, D_MODEL), BF16), jax.ShapeDtypeStruct((2, n_chunks, half_rows, D_MODEL), BF16)],
        [pltpu.VMEM((FF_CHUNK, D_MODEL), F32)] * 3, 56, (d_out, n, gate, up, group, group, group), rider, (chunks,))


def _mix_in(h, gain, w_in, rc, rs1, rs2, name):
    S = h.shape[0]
    tile = min(TOK_TILE, S)

    def body(h_ref, g_ref, w_ref, c_ref, s1_ref, s2_ref, n_ref, q_ref, k_ref, v_ref, pc_ref):
        _, xh = _rms_stats(h_ref[...])
        n = (xh * g_ref[...]).astype(BF16)
        n_ref[...] = n
        u = _dot_nt(n, w_ref[...])
        c, s1, s2 = c_ref[...], s1_ref[...], s2_ref[...]
        q_ref[...] = jnp.concatenate([(_rope(u[:, hd * LANES:(hd + 1) * LANES], c, s1, s2) * SCALE).astype(BF16)
                                      for hd in range(N_HEADS)], axis=1)
        k_ref[...] = _rope(u[:, Q_PAD:Q_PAD + KV_WIDTH], c, s1, s2).astype(BF16)
        v_ref[...] = u[:, Q_PAD + KV_WIDTH:Q_PAD + 2 * KV_WIDTH].astype(BF16)
        pc_ref[...] = u[:, Q_PAD + 2 * KV_WIDTH:]

    return pl.pallas_call(
        body, name=name, grid=(S // tile,),
        in_specs=[_rows(tile, D_MODEL), _full((1, D_MODEL)), _full((U_PAD, D_MODEL)),
                  _rows(tile, LANES), _rows(tile, LANES), _rows(tile, LANES)],
        out_specs=[_rows(tile, D_MODEL), _rows(tile, Q_PAD), _rows(tile, KV_WIDTH), _rows(tile, KV_WIDTH),
                   _rows(tile, POOL_WIDTH)],
        out_shape=[jax.ShapeDtypeStruct((S, D_MODEL), BF16), jax.ShapeDtypeStruct((S, Q_PAD), BF16),
                   jax.ShapeDtypeStruct((S, KV_WIDTH), BF16), jax.ShapeDtypeStruct((S, KV_WIDTH), BF16),
                   jax.ShapeDtypeStruct((S, POOL_WIDTH), F32)],
        compiler_params=_params(("parallel",), 40),
    )(h, gain, w_in, rc, rs1, rs2)


def _band_start(i, S):
    return pl.multiple_of(jnp.clip((i - 1) * BLOCK, 0, S - BAND), BLOCK)


def _window_bias(off):
    r = lax.broadcasted_iota(jnp.int32, (BLOCK, 1), 0)
    c = lax.broadcasted_iota(jnp.int32, (1, BAND), 1)
    return jnp.where(jnp.abs(off + r - c) <= WINDOW, 0.0, NEG).astype(F32)


def _softmax_parts(qh, kb, bias, sink_h):
    s = _dot_nt(qh, kb) + bias
    m = jnp.maximum(jnp.max(s, axis=-1, keepdims=True), sink_h)
    p = jnp.exp(s - m)
    es = jnp.exp(sink_h - m)
    return p, es, 1.0 / (jnp.sum(p, axis=-1, keepdims=True) + es)


def _pool_matrix(t0, start, S, w):
    r = lax.broadcasted_iota(jnp.int32, (BLOCK, 1), 0) + t0
    c = lax.broadcasted_iota(jnp.int32, (1, BAND), 1) + start
    half = w // 2

    def window(lo, hi):
        a = jnp.maximum(lo, 0)
        b = jnp.minimum(hi + 1, S)
        return jnp.where((c >= a) & (c < b), 1.0 / (b - a).astype(F32), 0.0)

    return (0.5 * (window(r - half, r + half - 1) + window(r - half + 1, r + half))).astype(BF16)


def _pool_matrices(S):
    blocks = ((0, 0), (BLOCK, 0), (S - BLOCK, S - BAND))
    return jnp.stack([jnp.stack([_pool_matrix(t0, start, S, w) for w in POOL_WINDOWS]) for t0, start in blocks])


def _pool_spec(nb):
    return pl.BlockSpec((1, N_POOL, BLOCK, BAND), lambda i, *_: (jnp.where(i == 0, 0, jnp.where(i == nb - 1, 2, 1)), 0, 0, 0))


def _mix_core_fwd(q, k, v, pc, sink, pool_m, pool_w, pool_scale, name):
    S = q.shape[0]
    nb = S // BLOCK

    def body(sink_ref, q_ref, k_ref, v_ref, pc_ref, pm_ref, pw_ref, ps_ref, a_ref, p_ref):
        i = pl.program_id(0)
        start = _band_start(i, S)
        band = pl.ds(start, BAND)
        bias = _window_bias(i * BLOCK - start)
        kb, vb = k_ref[band, :], v_ref[band, :]
        hs = range(N_HEADS)
        ss = [_dot_nt(q_ref[:, hd * LANES:(hd + 1) * LANES], kb) + bias for hd in hs]
        ms = [jnp.maximum(jnp.max(ss[hd], axis=-1, keepdims=True), sink_ref[0, hd]) for hd in hs]
        ps = [jnp.exp(ss[hd] - ms[hd]) for hd in hs]
        invs = [1.0 / (jnp.sum(ps[hd], axis=-1, keepdims=True) + jnp.exp(sink_ref[0, hd] - ms[hd])) for hd in hs]
        outs = [_dot(ps[hd].astype(BF16), vb) for hd in hs]
        a_ref[...] = jnp.concatenate([(outs[hd] * invs[hd]).astype(BF16) for hd in hs], axis=1)
        centre = pl.ds(pl.multiple_of(i * BLOCK, BLOCK), BLOCK)
        gs = range(N_POOL)
        sl = [slice(g * POOL_GROUP, (g + 1) * POOL_GROUP) for g in gs]
        means = [_dot(pm_ref[0, g], pc_ref[band, sl[g]].astype(BF16)) for g in gs]
        devs = [(means[g] - pc_ref[centre, sl[g]]).astype(BF16) for g in gs]
        p_ref[...] = (jnp.concatenate([_dot(devs[g], pw_ref[g]) for g in gs], axis=1) * ps_ref[...]).astype(BF16)

    return _call(
        body, name, (nb,),
        [pl.BlockSpec(memory_space=pltpu.SMEM), _rows(BLOCK, Q_PAD), _full((S, KV_WIDTH)), _full((S, KV_WIDTH)),
         _full((S, POOL_WIDTH)), _pool_spec(nb), _full((N_POOL, POOL_GROUP, POOL_GROUP)), _full((1, POOL_WIDTH))],
        [_rows(BLOCK, Q_PAD), _rows(BLOCK, POOL_WIDTH)],
        [jax.ShapeDtypeStruct((S, Q_PAD), BF16), jax.ShapeDtypeStruct((S, POOL_WIDTH), BF16)],
        [], 40, (sink, q, k, v, pc, pool_m, pool_w, pool_scale))


def _mix_core_bwd(q, k, v, pc, da, dp, sink, pool_m, pool_w, pool_scale, rc, rs1, rs2, name, rider=None):
    S = q.shape[0]
    nb = S // BLOCK

    def body(sink_ref, q_ref, k_ref, v_ref, pc_ref, da_ref, dp_ref, pm_ref, pw_ref, ps_ref, c_ref, s1_ref, s2_ref,
             dq_ref, dk_ref, dv_ref, dpc_ref, dsink_ref, dpw_ref, dps_ref):
        i = pl.program_id(0)

        @pl.when(i == 0)
        def _():
            dk_ref[...] = jnp.zeros_like(dk_ref)
            dv_ref[...] = jnp.zeros_like(dv_ref)
            dpc_ref[...] = jnp.zeros_like(dpc_ref)
            dsink_ref[...] = jnp.zeros_like(dsink_ref)
            dpw_ref[...] = jnp.zeros_like(dpw_ref)
            dps_ref[...] = jnp.zeros_like(dps_ref)

        start = _band_start(i, S)
        band = pl.ds(start, BAND)
        bias = _window_bias(i * BLOCK - start)
        kb, vb = k_ref[band, :], v_ref[band, :]
        c, s1, s2 = c_ref[...], s1_ref[...], s2_ref[...]
        lane = lax.broadcasted_iota(jnp.int32, (1, LANES), 1)
        hs = range(N_HEADS)
        qs = [q_ref[:, hd * LANES:(hd + 1) * LANES] for hd in hs]
        das = [da_ref[:, hd * LANES:(hd + 1) * LANES] for hd in hs]
        ss = [_dot_nt(qs[hd], kb) + bias for hd in hs]
        d_probs = [_dot_nt(das[hd], vb) for hd in hs]
        ms = [jnp.maximum(jnp.max(ss[hd], axis=-1, keepdims=True), sink_ref[0, hd]) for hd in hs]
        ps = [jnp.exp(ss[hd] - ms[hd]) for hd in hs]
        ess = [jnp.exp(sink_ref[0, hd] - ms[hd]) for hd in hs]
        invs = [1.0 / (jnp.sum(ps[hd], axis=-1, keepdims=True) + ess[hd]) for hd in hs]
        probs = [ps[hd] * invs[hd] for hd in hs]
        deltas = [jnp.sum(probs[hd] * d_probs[hd], axis=-1, keepdims=True) for hd in hs]
        d_ss = [(probs[hd] * (d_probs[hd] - deltas[hd])).astype(BF16) for hd in hs]
        dqs = [_dot(d_ss[hd], kb) for hd in hs]
        dq_ref[...] = jnp.concatenate([_rope_bwd(dqs[hd] * SCALE, c, s1, s2).astype(BF16) for hd in hs], axis=1)
        dks = [_dot_tn(d_ss[hd], qs[hd]) for hd in hs]
        dvs = [_dot_tn(probs[hd].astype(BF16), das[hd]) for hd in hs]
        dk_ref[band, :] += functools.reduce(lambda a, b: a + b, dks)
        dv_ref[band, :] += functools.reduce(lambda a, b: a + b, dvs)
        dsink_ref[...] += functools.reduce(lambda a, b: a + b, [
            jnp.where(lane == hd, -jnp.sum(ess[hd] * invs[hd] * deltas[hd], axis=0, keepdims=True), 0.0) for hd in hs])

        centre = pl.ds(pl.multiple_of(i * BLOCK, BLOCK), BLOCK)
        gs = range(N_POOL)
        sl = [slice(g * POOL_GROUP, (g + 1) * POOL_GROUP) for g in gs]
        devs = [(_dot(pm_ref[0, g], pc_ref[band, sl[g]].astype(BF16)) - pc_ref[centre, sl[g]]).astype(BF16) for g in gs]
        dys = [dp_ref[:, sl[g]].astype(F32) for g in gs]
        zs = [_dot(devs[g], pw_ref[g]) for g in gs]
        dzs = [(dys[g] * ps_ref[:, sl[g]]).astype(BF16) for g in gs]
        d_devs = [_dot_nt(dzs[g], pw_ref[g]) for g in gs]
        dps_ref[...] += jnp.concatenate([jnp.sum(dys[g] * zs[g], axis=0, keepdims=True) for g in gs], axis=1)
        for g in gs:
            dpw_ref[g] += _dot_tn(devs[g], dzs[g])
        dpc_ref[band, :] += jnp.concatenate([_dot_tn(pm_ref[0, g], d_devs[g].astype(BF16)) for g in gs], axis=1)
        dpc_ref[centre, :] -= jnp.concatenate(d_devs, axis=1)

    return _call(
        body, name, (nb,),
        [pl.BlockSpec(memory_space=pltpu.SMEM), _rows(BLOCK, Q_PAD), _full((S, KV_WIDTH)), _full((S, KV_WIDTH)),
         _full((S, POOL_WIDTH)), _rows(BLOCK, Q_PAD), _rows(BLOCK, POOL_WIDTH), _pool_spec(nb),
         _full((N_POOL, POOL_GROUP, POOL_GROUP)), _full((1, POOL_WIDTH)),
         _rows(BLOCK, LANES), _rows(BLOCK, LANES), _rows(BLOCK, LANES)],
        [_rows(BLOCK, Q_PAD), _full((S, KV_WIDTH)), _full((S, KV_WIDTH)), _full((S, POOL_WIDTH)),
         _full((1, LANES)), _full((N_POOL, POOL_GROUP, POOL_GROUP)), _full((1, POOL_WIDTH))],
        [jax.ShapeDtypeStruct((S, Q_PAD), BF16), jax.ShapeDtypeStruct((S, KV_WIDTH), F32),
         jax.ShapeDtypeStruct((S, KV_WIDTH), F32), jax.ShapeDtypeStruct((S, POOL_WIDTH), F32),
         jax.ShapeDtypeStruct((1, LANES), F32), jax.ShapeDtypeStruct((N_POOL, POOL_GROUP, POOL_GROUP), F32),
         jax.ShapeDtypeStruct((1, POOL_WIDTH), F32)],
        [], 56, (sink, q, k, v, pc, da, dp, pool_m, pool_w, pool_scale, rc, rs1, rs2), rider)


def _mix_out(h, a, p, wa, wp, name):
    S = h.shape[0]
    tile = min(TOK_TILE, S)

    def body(h_ref, a_ref, p_ref, wa_ref, wp_ref, o_ref):
        o_ref[...] = h_ref[...] + _dot(a_ref[...], wa_ref[...]) + _dot(p_ref[...], wp_ref[...])

    return pl.pallas_call(
        body, name=name, grid=(S // tile,),
        in_specs=[_rows(tile, D_MODEL), _rows(tile, Q_PAD), _rows(tile, POOL_WIDTH),
                  _full((Q_PAD, D_MODEL)), _full((POOL_WIDTH, D_MODEL))],
        out_specs=_rows(tile, D_MODEL),
        out_shape=jax.ShapeDtypeStruct((S, D_MODEL), F32),
        compiler_params=_params(("parallel",), 40),
    )(h, a, p, wa, wp)


def _mix_out_bwd(dh_out, dn, h, gain, a, p, wa, wp, name):
    S = h.shape[0]
    tile = min(TOK_TILE, S)

    def body(do_ref, dn_ref, h_ref, g_ref, a_ref, p_ref, wa_ref, wp_ref, dh_ref, da_ref, dp_ref, dwa_ref, dwp_ref, dg_ref):
        @pl.when(pl.program_id(0) == 0)
        def _():
            dwa_ref[...] = jnp.zeros_like(dwa_ref)
            dwp_ref[...] = jnp.zeros_like(dwp_ref)
            dg_ref[...] = jnp.zeros_like(dg_ref)

        r, xh = _rms_stats(h_ref[...])
        dnorm, dg = _rms_bwd(_sum_chunks([dn_ref]), g_ref[...], r, xh)
        dh = do_ref[...] + dnorm
        dg_ref[...] += dg
        dh_ref[...] = dh
        dhb = dh.astype(BF16)
        da_ref[...] = _dot_nt(dhb, wa_ref[...]).astype(BF16)
        dp_ref[...] = _dot_nt(dhb, wp_ref[...]).astype(BF16)
        dwa_ref[...] += _dot_tn(a_ref[...], dhb)
        dwp_ref[...] += _dot_tn(p_ref[...], dhb)

    return _call(
        body, name, (S // tile,),
        [_rows(tile, D_MODEL), _chunk_rows(tile, dn.shape[0]), _rows(tile, D_MODEL), _full((1, D_MODEL)),
         _rows(tile, Q_PAD), _rows(tile, POOL_WIDTH), _full((Q_PAD, D_MODEL)), _full((POOL_WIDTH, D_MODEL))],
        [_rows(tile, D_MODEL), _rows(tile, Q_PAD), _rows(tile, POOL_WIDTH),
         _full((Q_PAD, D_MODEL)), _full((POOL_WIDTH, D_MODEL)), _full((1, D_MODEL))],
        [jax.ShapeDtypeStruct((S, D_MODEL), F32), jax.ShapeDtypeStruct((S, Q_PAD), BF16),
         jax.ShapeDtypeStruct((S, POOL_WIDTH), BF16), jax.ShapeDtypeStruct((Q_PAD, D_MODEL), F32),
         jax.ShapeDtypeStruct((POOL_WIDTH, D_MODEL), F32), jax.ShapeDtypeStruct((1, D_MODEL), F32)],
        [], 48, (dh_out, dn, h, gain, a, p, wa, wp))


def _mix_in_bwd(dh_out, h, gain, n, dq, dk, dv, dpc, rc, rs1, rs2, w_in, name):
    S = h.shape[0]
    tile = min(TOK_TILE, S)

    def body(do_ref, h_ref, g_ref, n_ref, dq_ref, dk_ref, dv_ref, dpc_ref, c_ref, s1_ref, s2_ref, w_ref,
             dh_ref, dhalf_ref, dw_ref, dg_ref):
        @pl.when(pl.program_id(0) == 0)
        def _():
            dw_ref[...] = jnp.zeros_like(dw_ref)
            dg_ref[...] = jnp.zeros_like(dg_ref)

        dk = _rope_bwd(dk_ref[...], c_ref[...], s1_ref[...], s2_ref[...]).astype(BF16)
        du = jnp.concatenate([dq_ref[...], dk, dv_ref[...].astype(BF16), dpc_ref[...].astype(BF16)], axis=1)
        dn = _dot(du, w_ref[...])
        dw_ref[...] += _dot_tn(du, n_ref[...])
        r, xh = _rms_stats(h_ref[...])
        dnorm, dg = _rms_bwd(dn, g_ref[...], r, xh)
        dh = do_ref[...] + dnorm
        dg_ref[...] += dg
        dh_ref[...] = dh
        dhalf_ref[...] = (0.5 * dh).astype(BF16)

    return _call(
        body, name, (S // tile,),
        [_rows(tile, D_MODEL), _rows(tile, D_MODEL), _full((1, D_MODEL)), _rows(tile, D_MODEL),
         _rows(tile, Q_PAD), _rows(tile, KV_WIDTH), _rows(tile, KV_WIDTH), _rows(tile, POOL_WIDTH),
         _rows(tile, LANES), _rows(tile, LANES), _rows(tile, LANES), _full((U_PAD, D_MODEL))],
        [_rows(tile, D_MODEL), _rows(tile, D_MODEL), _full((U_PAD, D_MODEL)), _full((1, D_MODEL))],
        [jax.ShapeDtypeStruct((S, D_MODEL), F32), jax.ShapeDtypeStruct((S, D_MODEL), BF16),
         jax.ShapeDtypeStruct((U_PAD, D_MODEL), F32), jax.ShapeDtypeStruct((1, D_MODEL), F32)],
        [], 56, (dh_out, h, gain, n, dq, dk, dv, dpc, rc, rs1, rs2, w_in))


def _norm_bwd(dh_out, dns, h, gain, name):
    S = h.shape[0]
    tile = min(TOK_TILE, S)
    n = len(dns)

    def body(do_ref, *refs):
        h_ref, g_ref, dh_ref, dg_ref = refs[n:]

        @pl.when(pl.program_id(0) == 0)
        def _():
            dg_ref[...] = jnp.zeros_like(dg_ref)

        r, xh = _rms_stats(h_ref[...])
        dnorm, dg = _rms_bwd(_sum_chunks(refs[:n]), g_ref[...], r, xh)
        dg_ref[...] += dg
        dh_ref[...] = do_ref[...] + dnorm

    return _call(
        body, name, (S // tile,),
        [_rows(tile, D_MODEL)] + [_chunk_rows(tile, dn.shape[0]) for dn in dns] + [_rows(tile, D_MODEL), _full((1, D_MODEL))],
        [_rows(tile, D_MODEL), _full((1, D_MODEL))],
        [jax.ShapeDtypeStruct((S, D_MODEL), F32), jax.ShapeDtypeStruct((1, D_MODEL), F32)],
        [], 40, (dh_out, *dns, h, gain))


def _rope_tables(S):
    half = ROTARY_DIM // 2
    inv_freq = ROPE_THETA ** (-jnp.arange(0, ROTARY_DIM, 2, dtype=F32) / ROTARY_DIM)
    dim = jnp.arange(LANES) % HEAD_DIM
    ang = jnp.arange(S, dtype=F32)[:, None] * inv_freq[dim % half][None, :]
    lo, hi = (dim < half)[None, :], ((dim >= half) & (dim < ROTARY_DIM))[None, :]
    c = jnp.where(lo | hi, jnp.cos(ang), 1.0)
    s1 = jnp.where(lo, -jnp.sin(ang), 0.0)
    s2 = jnp.where(hi, jnp.sin(ang), 0.0)
    return c, s1, s2


def _pad_heads(w, axis):
    w = jnp.moveaxis(w, axis, 0)
    heads = w.reshape((N_HEADS, HEAD_DIM) + w.shape[1:])
    zero = jnp.zeros_like(heads)
    first = (jnp.arange(N_HEADS) < Q_PER_KV).reshape((N_HEADS, 1) + (1,) * (w.ndim - 1))
    lo = jnp.where(first, heads, zero)
    hi = jnp.where(first, zero, heads)
    padded = jnp.concatenate([lo, hi], axis=1).reshape((Q_PAD,) + w.shape[1:])
    return jnp.moveaxis(padded, 0, axis)


def _unpad_heads(w, axis):
    w = jnp.moveaxis(w, axis, 0)
    groups = w.reshape((N_HEADS, 2, HEAD_DIM) + w.shape[1:])
    first = (jnp.arange(N_HEADS) < Q_PER_KV).reshape((N_HEADS, 1) + (1,) * (w.ndim - 1))
    heads = jnp.where(first, groups[:, 0], groups[:, 1]).reshape((ATTN_WIDTH,) + w.shape[1:])
    return jnp.moveaxis(heads, 0, axis)


IN_ROWS = IN_WIDTH // N_CHIPS
OUT_ROWS = (ATTN_WIDTH + POOL_WIDTH) // N_CHIPS
MIX_ROWS = IN_ROWS + OUT_ROWS
FFN_ROWS = 3 * FF_CHUNK


def _step(x, target, bufs, small, place, update):
    S = x.shape[0]
    rc, rs1, rs2 = _rope_tables(S)
    mine = place[1]
    order = jnp.stack([mine, mine ^ 2, mine ^ 1, mine ^ 3])
    h1, n1, gate1, up1, ffn1, mix = _ffn_fwd(order, x, small["ffn1_norm"], "ffn1_fwd",
                                             _merge(_allgather(bufs[:1], in_passes=0), _allgather(bufs[1:2])), 0)
    in_flight, (ffn2,), token = _split_start(bufs[2:], 2, _gather_copies(X_Y_DIAGONAL[:2]), "ffn2_gather_start")
    w_in_t = mix[:, :IN_ROWS].reshape(IN_WIDTH, D_MODEL)
    w_in_pad = jnp.concatenate([_pad_heads(w_in_t[:ATTN_WIDTH], 0), w_in_t[ATTN_WIDTH:]], axis=0)
    w_out = mix[:, IN_ROWS:].reshape(ATTN_WIDTH + POOL_WIDTH, D_MODEL)
    wa = _pad_heads(w_out[:ATTN_WIDTH], 0)
    wp = w_out[ATTN_WIDTH:]
    pool_w = small["pool_w"].astype(BF16)

    n2, q, k, v, pc = _mix_in(h1, small["mix_norm"] + token[0, 0], w_in_pad, rc, rs1, rs2, "mix_in")
    pool_m = _pool_matrices(S)
    a, p = _mix_core_fwd(q, k, v, pc, small["sink_logits"], pool_m, pool_w, small["pool_scale"], "mix_core_fwd")
    h2 = _mix_out(h1, a, p, wa, wp, "mix_out")
    (ffn2,) = _split_wait(in_flight, [ffn2], _gather_copies(X_Y_DIAGONAL[:2]), [h2], "ffn2_gather_wait")
    dh3, dhalf3, loss_lanes, d_final, n3, gate2, up2, ffn2 = _ffn_fwd(
        order, h2, small["ffn2_norm"], "ffn2_fwd",
        _allgather([ffn2], peers=X_Y_DIAGONAL[2:], in_passes=0, landed_before=X_Y_DIAGONAL[:2]), 0,
        loss_head=(target, small["final_norm"]))

    dn3, d_ffn2 = _ffn_bwd(jnp.arange(N_CHIPS, dtype=jnp.int32), dhalf3, n3, gate2, up2, ffn2, "ffn2_bwd")
    dh2, da, dp, dwa, dwp, d_ffn2_norm = _mix_out_bwd(dh3, dn3, h2, small["ffn2_norm"], a, p, wa, wp, "mix_out_bwd")
    dq, dk, dv, dpc, dsink, dpool_w, dpool_scale, received = _mix_core_bwd(
        q, k, v, pc, da, dp, small["sink_logits"], pool_m, pool_w, small["pool_scale"], rc, rs1, rs2, "mix_core_bwd",
        _sibling_exchange([d_ffn2]))
    pair = _pair_sum(place, d_ffn2, received, "grad_pair_sum_ffn2")
    in_flight, (pair, stack), token = _split_start([pair, received], 3, _scatter_copies, "ffn2_scatter_start")
    dh1, dhalf1, dw_in_pad, d_mix_norm = _mix_in_bwd(dh2, h1, small["mix_norm"] + token[0, 0], n2, dq, dk, dv, dpc, rc, rs1, rs2,
                                                     w_in_pad, "mix_in_bwd")
    dw_in_t = jnp.concatenate([_unpad_heads(dw_in_pad[:Q_PAD], 0), dw_in_pad[Q_PAD:]], axis=0)
    dw_out = jnp.concatenate([_unpad_heads(dwa, 0), dwp], axis=0)
    d_mix = jnp.concatenate([dw_in_t.reshape(N_CHIPS, IN_ROWS, D_MODEL), dw_out.reshape(N_CHIPS, OUT_ROWS, D_MODEL)], axis=1)
    d_mix = jnp.transpose(d_mix.reshape(N_CHIPS, 2, MIX_ROWS // 2, D_MODEL), (1, 0, 2, 3)).astype(BF16)
    small_g = {"ffn1_norm": jnp.zeros_like(d_mix_norm), "mix_norm": d_mix_norm, "ffn2_norm": d_ffn2_norm,
               "final_norm": d_final, "pool_scale": dpool_scale, "sink_logits": dsink[:, :N_HEADS], "pool_w": dpool_w}
    loss_row = jnp.sum(loss_lanes.reshape(D_MODEL // LANES, LANES), axis=0, keepdims=True)
    small_early = _pack_small(small_g, loss_row)

    chunk = [(place[1:] + 1 + p) % N_CHIPS for p in range(N_CHIPS)]
    ffn1_bwd = functools.partial(_ffn_bwd, d_out=dhalf1, n=n1, gate=gate1, up=up1, group=ffn1)
    dn_a, dw_a, recv_mix, small_all = ffn1_bwd(
        chunk[0], name="ffn1_bwd_0", rider=_merge(_sibling_exchange([d_mix]), _small_allgather(small_early)))
    pair, stack = _split_wait(in_flight, [pair, stack], _scatter_copies, [dn_a], "ffn2_scatter_wait")
    reduced_ffn2 = _chip_sum(place, pair, stack, 2, "grad_chip_sum_ffn2")
    pair_mix = _pair_sum(place, d_mix, recv_mix, "grad_pair_sum_mix")
    dn_b, dw_b, recv_a, stack_mix, g_ffn2 = ffn1_bwd(
        chunk[1], name="ffn1_bwd_1",
        rider=_merge(_merge(_sibling_exchange([dw_a]), _scatter([pair_mix])), _sibling_share([reduced_ffn2])))
    pair_a = _pair_sum(place, dw_a, recv_a, "grad_pair_sum_ffn1_0")
    reduced_mix = _chip_sum(place, pair_mix, stack_mix, 2, "grad_chip_sum_mix")
    steps = [(_scatter_step_copies(p), (p, 3)) for p in range(3)]
    sems_a, (pair_a, stack), token = _split_start([pair_a, stack], 1, steps[0][0], "ffn1_scatter_start_0")
    dn_c, dw_c, recv_b, g_mix = ffn1_bwd(
        chunk[2], name="ffn1_bwd_2", rider=_merge(_merge(_sibling_exchange([dw_b]), _sibling_share([reduced_mix])), _after(token)))
    pair_b = _pair_sum(place, dw_b, recv_b, "grad_pair_sum_ffn1_1")
    sems_b, (pair_b, stack), token = _split_start([pair_b, stack], 1, steps[1][0], "ffn1_scatter_start_1")
    dn_d, dw_d, recv_c = ffn1_bwd(chunk[3], name="ffn1_bwd_3", rider=_merge(_sibling_exchange([dw_c]), _after(token)))
    pair_c = _pair_sum(place, dw_c, recv_c, "grad_pair_sum_ffn1_2")
    own = (_exchange_copies, (4, 5))
    sems_c, (pair_c, stack, dw_d, recv_d), token = _split_start(
        [pair_c, stack, dw_d, recv_c], 2, _joined([(steps[2][0], (0, 1)), (own[0], (2, 3))]), "ffn1_scatter_start_2")
    grad_x, d_ffn1_norm = _norm_bwd(dh1, [dn_a, dn_b, dn_c, dn_d], x, small["ffn1_norm"] + token[0, 0], "norm1_bwd")
    updated = (update(GROUPS[2], g_ffn2.reshape(-1, D_MODEL), d_ffn1_norm)
               + update(GROUPS[1], g_mix.reshape(-1, D_MODEL), d_ffn1_norm))
    pair_a, pair_b, pair_c, stack, dw_d, recv_d = _split_wait(
        sems_a + sems_b + sems_c, [pair_a, pair_b, pair_c, stack, dw_d, recv_d], _joined(steps + [own]), updated,
        "ffn1_scatter_wait")
    pair_d = _pair_sum(place, dw_d, recv_d, "grad_pair_sum_ffn1_3")
    reduced_ffn1 = _chip_sum(place, pair_d, stack, 2, "grad_chip_sum_ffn1")
    g_ffn1, gains = _comm_call(_merge(_sibling_share([reduced_ffn1]), _small_allgather(d_ffn1_norm.reshape(-1, LANES))),
                               "grad_share_tail")
    gain_sum = _sum_leading(gains, 1, "gain_grad_sum")
    small_sum = jnp.concatenate([gain_sum, _sum_leading(small_all, 1, "small_grad_sum")[gain_sum.shape[0]:]], axis=0)
    update(GROUPS[0], g_ffn1.reshape(-1, D_MODEL))
    return jnp.sum(small_sum[SMALL_ROWS - 1]), grad_x, small_sum


GROUPS =(("ffn1_w_gate", "ffn1_w_up", "ffn1_w_down"), ("w_in", "w_out"), ("ffn2_w_gate", "ffn2_w_up", "ffn2_w_down"))
TRANSPOSED = ("ffn1_w_gate", "ffn1_w_up", "w_in", "ffn2_w_gate", "ffn2_w_up")


def _place():
    x, y, c = lax.axis_index("x"), lax.axis_index("y"), lax.axis_index("c")
    chips = [(1 - x, y), (x, 1 - y), (1 - x, 1 - y)]
    return x, y, c, chips


def _remote(src, dst, send_sem, recv_sem, to):
    return pltpu.make_async_remote_copy(src_ref=src, dst_ref=dst, send_sem=send_sem, recv_sem=recv_sem,
                                        device_id=to, device_id_type=MESH)


def _pack(chip, members, name):
    rows = members[0].shape[0]
    n = len(members)

    def body(chip_ref, *refs):
        ins, out_ref, buf, sems = refs[:n], refs[n], refs[n + 1], refs[n + 2]
        copies = [pltpu.make_async_copy(ins[k], buf.at[k], sems.at[k]) for k in range(n)]
        for cp in copies:
            cp.start()
        for k in range(n):
            copies[k].wait()
            out_ref[0, k * rows:(k + 1) * rows, :] = buf[k].astype(BF16)

    return pl.pallas_call(
        body, name=name,
        grid_spec=pltpu.PrefetchScalarGridSpec(
            num_scalar_prefetch=1, grid=(1,),
            in_specs=[HBM_SPEC] * n,
            out_specs=pl.BlockSpec((1, n * rows, D_MODEL), lambda k, chip_ref: (chip_ref[0], 0, 0)),
            scratch_shapes=[pltpu.VMEM((n, rows, D_MODEL), F32), pltpu.SemaphoreType.DMA((n,))]),
        out_shape=jax.ShapeDtypeStruct((N_CHIPS, n * rows, D_MODEL), BF16),
        compiler_params=_params(("arbitrary",), 40),
    )(chip, *members)


def _same(arrays):
    return [jax.ShapeDtypeStruct(a.shape, a.dtype) for a in arrays]


X_Y_DIAGONAL = (0, 1, 2)


def _allgather(bufs, peers=X_Y_DIAGONAL, in_passes=None, landed_before=()):
    n = len(bufs)

    def copy(kind, outs, send_sems, recv_sems, a, k):
        x, y, c, chips = _place()
        half = bufs[a].shape[1] // 2

        def rows(slot, core):
            return outs[a].at[slot, pl.ds(pl.multiple_of(core * half, 16), half)]

        me, slot = 2 * x + y, 2 * chips[k][0] + chips[k][1]
        over_ici = (send_sems.at[6 * a + k], recv_sems.at[6 * a + k])
        over_d2d = (send_sems.at[6 * a + 3 + k], recv_sems.at[6 * a + 3 + k])
        if kind == "first":
            return _remote(rows(me, c), rows(me, c), *over_ici, (*chips[k], c))
        if kind == "landed":
            return _remote(rows(me, c), rows(slot, c), *over_ici, (*chips[k], c))
        if kind == "passed":
            return _remote(rows(slot, c), rows(slot, c), *over_d2d, (x, y, 1 - c))
        return _remote(rows(me, c), rows(slot, 1 - c), *over_d2d, (x, y, 1 - c))

    def land(pairs):
        def hook(ins, outs, sems):
            for a, k in pairs:
                if k not in landed_before:
                    copy("landed", outs, *sems, a, k).wait_recv()
                    copy("passed", outs, *sems, a, k).start()
        return hook

    def handed(pairs):
        def hook(ins, outs, sems):
            for a, k in pairs:
                copy("handed", outs, *sems, a, k).wait_recv()
        return hook

    sent = [(a, k) for a in range(n) for k in peers]
    everything = sent + [(in_passes, k) for k in landed_before]
    early = [(a, k) for a, k in everything if a == in_passes]

    def start(ins, outs, sems):
        for k in landed_before:
            copy("passed", outs, *sems, in_passes, k).start()
        for a, k in sent:
            copy("first", outs, *sems, a, k).start()

    def finish(ins, outs, sems):
        late = [pair for pair in everything if pair not in early]
        land(late)(ins, outs, sems)
        handed(late)(ins, outs, sems)
        for a, k in sent:
            copy("first", outs, *sems, a, k).wait_send()
        for a, k in everything:
            copy("passed", outs, *sems, a, k).wait_send()

    hooks = (tuple(((k, -2), land([(a, k)])) for a, k in early if k not in landed_before)
             + tuple(((k + 1, 0), handed([(a, k)])) for a, k in early))
    return _Rider(list(bufs), _same(bufs), {a: a for a in range(n)},
                  [pltpu.SemaphoreType.DMA((6 * n,)), pltpu.SemaphoreType.DMA((6 * n,))], start, finish, hooks)


SEM_SPEC = pl.BlockSpec(memory_space=pltpu.SEMAPHORE)


def _split_start(arrays, n_copies, make_copies, name):
    m, n = len(arrays), 2 * n_copies

    def body(*refs):
        sems, thru, token = refs[m:m + n], refs[m + n:2 * m + n], refs[2 * m + n]
        for cp in make_copies(thru, sems, False):
            cp.start()
        token[...] = jnp.zeros_like(token)

    outs = pl.pallas_call(
        body, name=name,
        out_shape=[pltpu.SemaphoreType.DMA(())] * n + _same(arrays) + [jax.ShapeDtypeStruct((SUBLANES, LANES), F32)],
        in_specs=[HBM_SPEC] * m, out_specs=[SEM_SPEC] * n + [HBM_SPEC] * m + [pl.BlockSpec(memory_space=pltpu.VMEM)],
        input_output_aliases={a: n + a for a in range(m)},
        compiler_params=pltpu.CompilerParams(has_side_effects=pltpu.SideEffectType.DATAFLOW_SIDE_EFFECTING),
    )(*arrays)
    return list(outs[:n]), list(outs[n:n + m]), outs[n + m]


def _split_wait(sems, arrays, make_copies, after, name):
    m, n = len(arrays), len(sems)

    def body(*refs):
        for cp in make_copies(refs[m + n + len(after):], refs[m:m + n], True):
            cp.wait_send()
            cp.wait_recv()

    return pl.pallas_call(
        body, name=name, out_shape=_same(arrays),
        in_specs=[HBM_SPEC] * m + [SEM_SPEC] * n + [pl.BlockSpec(memory_space=pl.ANY)] * len(after),
        out_specs=[HBM_SPEC] * m, input_output_aliases={a: a for a in range(m)},
        compiler_params=pltpu.CompilerParams(has_side_effects=pltpu.SideEffectType.DATAFLOW_SIDE_EFFECTING),
    )(*arrays, *sems, *after)


def _gather_copies(peers):
    def make(refs, sems, landing):
        x, y, c, chips = _place()
        half = refs[0].shape[1] // 2

        def rows(slot):
            return refs[0].at[slot, pl.ds(pl.multiple_of(c * half, 16), half)]

        me = 2 * x + y
        return [_remote(rows(me), rows(2 * chips[k][0] + chips[k][1] if landing else me), sems[2 * j], sems[2 * j + 1],
                        (*chips[k], c)) for j, k in enumerate(peers)]
    return make


def _scatter_copies(refs, sems, landing):
    x, y, c, chips = _place()
    me = 2 * x + y
    slots = [2 * cx + cy for cx, cy in chips]
    return [_remote(refs[0].at[slots[k]], refs[1].at[slots[k] if landing else me], sems[2 * k], sems[2 * k + 1], (*chips[k], c))
            for k in X_Y_DIAGONAL]


def _scatter_step_copies(step):
    def make(refs, sems, landing):
        x, y, c, _ = _place()
        me = 2 * x + y
        to = (me + 1 + step) % N_CHIPS
        frm = (me + N_CHIPS - 1 - step) % N_CHIPS
        peer = frm if landing else to
        return [_remote(refs[0].at[0], refs[1].at[frm if landing else me], sems[0], sems[1], (peer // 2, peer % 2, c))]
    return make


def _exchange_copies(refs, sems, landing):
    x, y, c, _ = _place()
    return [_remote(refs[0].at[1 - c], refs[1], sems[0], sems[1], (x, y, 1 - c))]


def _joined(makers):
    def make(refs, sems, landing):
        copies, at = [], 0
        for maker, places in makers:
            mine = maker([refs[p] for p in places], sems[at:], landing)
            copies += mine
            at += 2 * len(mine)
        return copies
    return make


def _sibling_exchange(parts):
    n = len(parts)

    def copies(ins, outs, send_sems, recv_sems):
        x, y, c, _ = _place()
        return [_remote(ins[a].at[1 - c], outs[a], send_sems.at[a], recv_sems.at[a], (x, y, 1 - c)) for a in range(n)]

    def start(ins, outs, sems):
        for cp in copies(ins, outs, *sems):
            cp.start()

    def finish(ins, outs, sems):
        for cp in copies(ins, outs, *sems):
            cp.wait_recv()
            cp.wait_send()

    return _Rider(list(parts), [jax.ShapeDtypeStruct(p.shape[1:], p.dtype) for p in parts], {},
                  [pltpu.SemaphoreType.DMA((n,)), pltpu.SemaphoreType.DMA((n,))], start, finish)


def _small_allgather(small):
    flips = [(fx, fy, fc) for fx in range(2) for fy in range(2) for fc in range(2)][1:]

    def copies(small_ref, gather_ref, send_sems, recv_sems, local_sem, started_only=False):
        x, y, c, _ = _place()
        me = 4 * x + 2 * y + c
        peers = [((1 - x) if fx else x, (1 - y) if fy else y, (1 - c) if fc else c) for fx, fy, fc in flips]
        own = pltpu.make_async_copy(small_ref, gather_ref.at[me], local_sem)
        sent = [_remote(small_ref, gather_ref.at[me], send_sems.at[k], recv_sems.at[k], peer) for k, peer in enumerate(peers)]
        if started_only:
            return own, sent
        landed = [_remote(small_ref, gather_ref.at[4 * px + 2 * py + pc], send_sems.at[k], recv_sems.at[k], (px, py, pc))
                  for k, (px, py, pc) in enumerate(peers)]
        return own, sent, landed

    def start(ins, outs, sems):
        own, sent = copies(ins[0], outs[0], *sems, started_only=True)
        own.start()
        for cp in sent:
            cp.start()

    def finish(ins, outs, sems):
        own, sent, landed = copies(ins[0], outs[0], *sems)
        for cp in landed:
            cp.wait_recv()
        for cp in sent:
            cp.wait_send()
        own.wait()

    return _Rider([small], [jax.ShapeDtypeStruct((2 * N_CHIPS,) + small.shape, small.dtype)], {},
                  [pltpu.SemaphoreType.DMA((7,)), pltpu.SemaphoreType.DMA((7,)), pltpu.SemaphoreType.DMA], start, finish)


def _merge(a, b):
    na, nao, nas = len(a.operands), len(a.out_shapes), len(a.scratch)

    def start(ins, outs, sems):
        a.start(ins[:na], outs[:nao], sems[:nas])
        b.start(ins[na:], outs[nao:], sems[nas:])

    def finish(ins, outs, sems):
        a.finish(ins[:na], outs[:nao], sems[:nas])
        b.finish(ins[na:], outs[nao:], sems[nas:])

    def of_a(fn):
        return lambda ins, outs, sems: fn(ins[:na], outs[:nao], sems[:nas])

    def of_b(fn):
        return lambda ins, outs, sems: fn(ins[na:], outs[nao:], sems[nas:])

    aliases = {**a.aliases, **{na + k: nao + v for k, v in b.aliases.items()}}
    hooks = tuple((at, of_a(fn)) for at, fn in a.hooks) + tuple((at, of_b(fn)) for at, fn in b.hooks)
    return _Rider(a.operands + b.operands, a.out_shapes + b.out_shapes, aliases, a.scratch + b.scratch, start, finish, hooks)


def _scatter(sums):
    n = len(sums)

    def copies(ins, outs, send_sems, recv_sems, started_only=False):
        x, y, c, chips = _place()
        me = 2 * x + y
        slots = [2 * cx + cy for cx, cy in chips]
        sent = [_remote(ins[a].at[slots[k]], outs[a].at[me], send_sems.at[3 * a + k], recv_sems.at[3 * a + k], (*chips[k], c))
                for a in range(n) for k in X_Y_DIAGONAL]
        if started_only:
            return sent
        landed = [_remote(ins[a].at[slots[k]], outs[a].at[slots[k]], send_sems.at[3 * a + k], recv_sems.at[3 * a + k],
                          (*chips[k], c)) for a in range(n) for k in X_Y_DIAGONAL]
        return sent, landed

    def start(ins, outs, sems):
        for cp in copies(ins, outs, *sems, started_only=True):
            cp.start()

    def finish(ins, outs, sems):
        sent, landed = copies(ins, outs, *sems)
        for cp in landed:
            cp.wait_recv()
        for cp in sent:
            cp.wait_send()

    return _Rider(list(sums), _same(sums), {}, [pltpu.SemaphoreType.DMA((3 * n,)), pltpu.SemaphoreType.DMA((3 * n,))],
                  start, finish)


def _sibling_share(bufs):
    n = len(bufs)

    def copies(outs, send_sems, recv_sems, started_only=False):
        x, y, c, _ = _place()
        sent = [_remote(outs[a].at[c], outs[a].at[c], send_sems.at[a], recv_sems.at[a], (x, y, 1 - c)) for a in range(n)]
        if started_only:
            return sent
        landed = [_remote(outs[a].at[c], outs[a].at[1 - c], send_sems.at[a], recv_sems.at[a], (x, y, 1 - c)) for a in range(n)]
        return sent, landed

    def start(ins, outs, sems):
        for cp in copies(outs, *sems, started_only=True):
            cp.start()

    def finish(ins, outs, sems):
        sent, landed = copies(outs, *sems)
        for cp in landed:
            cp.wait_recv()
        for cp in sent:
            cp.wait_send()

    return _Rider(list(bufs), _same(bufs), {a: a for a in range(n)},
                  [pltpu.SemaphoreType.DMA((n,)), pltpu.SemaphoreType.DMA((n,))], start, finish)


def _pair_sum(core, part, received, name):
    _, k, rh, cols = part.shape

    def body(core_ref, p_ref, r_ref, o_ref):
        o_ref[...] = (p_ref[0].astype(F32) + r_ref[...].astype(F32)).astype(BF16)

    return pl.pallas_call(
        body, name=name,
        grid_spec=pltpu.PrefetchScalarGridSpec(
            num_scalar_prefetch=1, grid=(k,),
            in_specs=[pl.BlockSpec((1, 1, rh, cols), lambda j, core_ref: (core_ref[0], j, 0, 0)),
                      pl.BlockSpec((1, rh, cols), lambda j, core_ref: (j, 0, 0))],
            out_specs=pl.BlockSpec((1, rh, cols), lambda j, core_ref: (j, 0, 0))),
        out_shape=jax.ShapeDtypeStruct((k, rh, cols), BF16),
        compiler_params=_params(("parallel",), 32),
    )(core, part, received)


def _sum_leading(stack, steps, name):
    k, rows, cols = stack.shape
    tile = rows // steps

    def body(s_ref, o_ref):
        total = s_ref[0].astype(F32)
        for d in range(1, k):
            total = total + s_ref[d].astype(F32)
        o_ref[...] = total

    return pl.pallas_call(
        body, name=name, grid=(steps,),
        in_specs=[pl.BlockSpec((k, tile, cols), lambda i: (0, i, 0))],
        out_specs=pl.BlockSpec((tile, cols), lambda i: (i, 0)),
        out_shape=jax.ShapeDtypeStruct((rows, cols), F32),
        compiler_params=_params(("parallel",), 32),
    )(stack)


def _chip_sum(place, own, stack, steps, name):
    k, rows, cols = stack.shape
    tile = rows // steps

    def body(place_ref, own_ref, *refs):
        chip = place_ref[1]
        total = None
        for d in range(k):
            term = jnp.where(chip == d, own_ref[0], refs[d][0]).astype(F32)
            total = term if total is None else total + term
        refs[k][0] = total

    def other(d):
        return lambda i, place_ref: (jnp.where(place_ref[1] == d, (d + 1) % k, d), i, 0)

    return pl.pallas_call(
        body, name=name,
        grid_spec=pltpu.PrefetchScalarGridSpec(
            num_scalar_prefetch=1, grid=(steps,),
            in_specs=[pl.BlockSpec((1, tile, cols), lambda i, place_ref: (place_ref[1] % own.shape[0], i, 0))]
            + [pl.BlockSpec((1, tile, cols), other(d)) for d in range(k)],
            out_specs=pl.BlockSpec((1, tile, cols), lambda i, place_ref: (place_ref[0], i, 0))),
        out_shape=jax.ShapeDtypeStruct((2, rows, cols), F32),
        compiler_params=_params(("arbitrary",), 32),
    )(place, own, *([stack] * k))


def _adamw(w, g, row0, m, v, tile, name, after=None):
    rows, cols = w.shape
    first = row0 // tile
    assert rows % tile == 0 and row0 % tile == 0
    bc1 = 1.0 - ADAM_B1 ** ADAM_STEP
    bc2 = 1.0 - ADAM_B2 ** ADAM_STEP

    def body(w_ref, g_ref, m_ref, v_ref, *refs):
        go_ref, d_ref, mo_ref, vo_ref = refs[-4:]
        g = g_ref[...]
        m_new = ADAM_B1 * m_ref[...] + (1.0 - ADAM_B1) * g
        v_new = ADAM_B2 * v_ref[...] + (1.0 - ADAM_B2) * (g * g)
        go_ref[...] = g
        d_ref[...] = -ADAM_LR * ((m_new / bc1) / (jnp.sqrt(v_new / bc2) + ADAM_EPS) + ADAM_WD * w_ref[...])
        mo_ref[...] = m_new
        vo_ref[...] = v_new

    later = [] if after is None else [after]
    spec = pl.BlockSpec((tile, cols), lambda i: (i, 0))
    g_spec = pl.BlockSpec((tile, cols), lambda i: (first + i, 0))
    return pl.pallas_call(
        body, name=name, grid=(rows // tile,),
        in_specs=[spec, g_spec, spec, spec] + [pl.BlockSpec(memory_space=pl.ANY)] * len(later), out_specs=[spec] * 4,
        out_shape=[jax.ShapeDtypeStruct((rows, cols), F32)] * 4,
        compiler_params=_params(("parallel",), 32),
    )(w, g, m, v, *later)


SMALL = ("ffn1_norm", "mix_norm", "ffn2_norm", "final_norm", "pool_scale", "sink_logits", "pool_w")


def _pack_small(d, last_row=None):
    def part(n):
        flat = d[n].reshape(-1)
        flat = jnp.pad(flat, (0, -flat.shape[0] % (SUBLANES * LANES)))
        return flat.reshape(-1, LANES)

    last = jnp.zeros((SUBLANES, LANES), F32) if last_row is None else jnp.pad(last_row, ((SUBLANES - 1, 0), (0, 0)))
    packed = jnp.concatenate([part(n) for n in SMALL] + [last], axis=0)
    assert packed.shape[0] == SMALL_ROWS
    return packed


def _unpack_small(packed, like):
    out, row = {}, 0
    for n in SMALL:
        size = math.prod(like[n].shape)
        rows = -(-size // (SUBLANES * LANES)) * SUBLANES
        out[n] = packed[row:row + rows].reshape(-1)[:size].reshape(like[n].shape)
        row += rows
    return out


def kernel(x, ffn1_norm, ffn1_w_gate, ffn1_w_up, ffn1_w_down, mix_norm, w_in, sink_logits, pool_w, pool_scale, w_out, ffn2_norm, ffn2_w_gate, ffn2_w_up, ffn2_w_down, final_norm, loss_target, m_ffn1_norm, m_ffn1_w_gate, m_ffn1_w_up, m_ffn1_w_down, m_mix_norm, m_w_in, m_sink_logits, m_pool_w, m_pool_scale, m_w_out, m_ffn2_norm, m_ffn2_w_gate, m_ffn2_w_up, m_ffn2_w_down, m_final_norm, v_ffn1_norm, v_ffn1_w_gate, v_ffn1_w_up, v_ffn1_w_down, v_mix_norm, v_w_in, v_sink_logits, v_pool_w, v_pool_scale, v_w_out, v_ffn2_norm, v_ffn2_w_gate, v_ffn2_w_up, v_ffn2_w_down, v_final_norm):
    names = ("ffn1_norm", "ffn1_w_gate", "ffn1_w_up", "ffn1_w_down", "mix_norm", "w_in", "sink_logits", "pool_w",
             "pool_scale", "w_out", "ffn2_norm", "ffn2_w_gate", "ffn2_w_up", "ffn2_w_down", "final_norm")
    weights = dict(zip(names, (ffn1_norm, ffn1_w_gate, ffn1_w_up, ffn1_w_down, mix_norm, w_in, sink_logits, pool_w,
                               pool_scale, w_out, ffn2_norm, ffn2_w_gate, ffn2_w_up, ffn2_w_down, final_norm)))
    mom1 = dict(zip(names, (m_ffn1_norm, m_ffn1_w_gate, m_ffn1_w_up, m_ffn1_w_down, m_mix_norm, m_w_in, m_sink_logits,
                            m_pool_w, m_pool_scale, m_w_out, m_ffn2_norm, m_ffn2_w_gate, m_ffn2_w_up, m_ffn2_w_down,
                            m_final_norm)))
    mom2 = dict(zip(names, (v_ffn1_norm, v_ffn1_w_gate, v_ffn1_w_up, v_ffn1_w_down, v_mix_norm, v_w_in, v_sink_logits,
                            v_pool_w, v_pool_scale, v_w_out, v_ffn2_norm, v_ffn2_w_gate, v_ffn2_w_up, v_ffn2_w_down,
                            v_final_norm)))
    chip = (2 * lax.axis_index("x") + lax.axis_index("y")).astype(jnp.int32).reshape(1)
    place = jnp.concatenate([lax.axis_index("c").astype(jnp.int32).reshape(1), chip])

    def rows_of(t, n):
        return jnp.swapaxes(t[n][0], 0, 1) if n in TRANSPOSED else t[n][0]

    bufs = [_pack(chip, [rows_of(weights, n) for n in GROUPS[0]], "pack_ffn1"),
            _pack(chip, [jnp.concatenate([rows_of(weights, n) for n in GROUPS[1]], axis=0)], "pack_mix"),
            _pack(chip, [rows_of(weights, n) for n in GROUPS[2]], "pack_ffn2")]

    small_w = {"ffn1_norm": ffn1_norm, "mix_norm": mix_norm, "ffn2_norm": ffn2_norm,
               "final_norm": final_norm.reshape(1, D_MODEL), "pool_scale": pool_scale, "sink_logits": sink_logits,
               "pool_w": pool_w[0]}
    out_g, out_d, out_m, out_v = {}, {}, {}, {}

    def update(members, g, after=None):
        row0, done = 0, []
        for n in members:
            w = rows_of(weights, n)
            tile = FF_CHUNK // 4 if w.shape[0] == FF_CHUNK else math.gcd(IN_ROWS, OUT_ROWS)
            outs = _adamw(w, g, row0, rows_of(mom1, n), rows_of(mom2, n), tile, "adamw_" + n, after)
            row0 += w.shape[0]
            done.append(outs[0])
            for dst, t in zip((out_g, out_d, out_m, out_v), outs):
                dst[n] = (jnp.swapaxes(t, 0, 1) if n in TRANSPOSED else t).reshape(weights[n].shape)
        return done

    loss, grad_x, small_sum = _step(x[0], loss_target[0], bufs, small_w, place, update)
    small_outs = _adamw(_pack_small(weights), small_sum, 0, _pack_small(mom1), _pack_small(mom2), SMALL_ROWS, "adamw_small")
    for dst, packed in zip((out_g, out_d, out_m, out_v), small_outs):
        dst.update(_unpack_small(packed, weights))

    return (loss,grad_x.reshape(x.shape), *[out_g[n] for n in names], *[out_d[n] for n in names],
            *[out_m[n] for n in names], *[out_v[n] for n in names])
```

```python
import collections
import functools
import math

import jax
import jax.numpy as jnp
from jax import lax
from jax.experimental import pallas as pl
from jax.experimental.pallas import tpu as pltpu

F32, BF16 = jnp.float32, jnp.bfloat16
MESH = pl.DeviceIdType.MESH

D_MODEL = 1024
D_FF = 2816
N_CHIPS = 4
FF_CHUNK = D_FF // N_CHIPS
HEAD_DIM = 64
N_HEADS = 8
N_KV = 2
Q_PER_KV = N_HEADS // N_KV
KV_WIDTH = N_KV * HEAD_DIM
ATTN_WIDTH = N_HEADS * HEAD_DIM
POOL_WINDOWS = (2, 4, 8, 16)
N_POOL = len(POOL_WINDOWS)
POOL_GROUP = 128
POOL_WIDTH = N_POOL * POOL_GROUP
IN_WIDTH = ATTN_WIDTH + 2 * KV_WIDTH + POOL_WIDTH
WINDOW = 128
BLOCK = 128
BAND = 3 * BLOCK
ROPE_THETA = 500000.0
ROTARY_DIM = HEAD_DIM // 4
EPS = 1e-6
LANES = 128
Q_PAD = N_HEADS * LANES
U_PAD = Q_PAD + 2 * KV_WIDTH + POOL_WIDTH
SCALE = HEAD_DIM ** -0.5
NEG = -1e30

ADAM_LR, ADAM_B1, ADAM_B2, ADAM_EPS, ADAM_WD, ADAM_STEP = 0.001, 0.9, 0.999, 1e-08, 0.01, 10

V7X_VMEM_BYTES = 64 * 1024 * 1024
TOK_TILE = 512
SUBLANES = 8
SMALL_ROWS = 568


def _params(sem, vmem_mb):
    assert vmem_mb * 1024 * 1024 <= V7X_VMEM_BYTES
    return pltpu.CompilerParams(dimension_semantics=sem, vmem_limit_bytes=vmem_mb * 1024 * 1024)


def _dot(a, b):
    return lax.dot_general(a, b, (((1,), (0,)), ((), ())), preferred_element_type=F32)


def _dot_nt(a, b):
    return lax.dot_general(a, b, (((1,), (1,)), ((), ())), preferred_element_type=F32)


def _dot_tn(a, b):
    return lax.dot_general(a, b, (((0,), (0,)), ((), ())), preferred_element_type=F32)


def _rms_stats(h):
    r = lax.rsqrt(jnp.mean(h * h, axis=-1, keepdims=True) + EPS)
    return r, h * r


def _rms_bwd(dn, g, r, xh):
    gd = dn * g
    dh = r * (gd - xh * jnp.mean(gd * xh, axis=-1, keepdims=True))
    return dh, jnp.sum(dn * xh, axis=0, keepdims=True)


def _rope(x, c, s1, s2):
    return x * c + pltpu.roll(x, LANES - ROTARY_DIM // 2, 1) * s1 + pltpu.roll(x, ROTARY_DIM // 2, 1) * s2


def _rope_bwd(d, c, s1, s2):
    return d * c + pltpu.roll(d * s1, ROTARY_DIM // 2, 1) + pltpu.roll(d * s2, LANES - ROTARY_DIM // 2, 1)


def _sum_chunks(refs):
    terms = [ref[j].astype(F32) for ref in refs for j in range(ref.shape[0])]
    return functools.reduce(lambda a, b: a + b, terms)


def _chunk_rows(tile, k):
    return pl.BlockSpec((k, tile, D_MODEL), lambda i, *_: (0, i, 0))


def _full(shape):
    nd = len(shape)
    return pl.BlockSpec(shape, lambda *_: (0,) * nd)


def _rows(tile, cols):
    return pl.BlockSpec((tile, cols), lambda i, *_: (i, 0))


HBM_SPEC = pl.BlockSpec(memory_space=pltpu.HBM)

_Rider = collections.namedtuple("_Rider", "operands out_shapes aliases scratch start finish hooks", defaults=[()])


_NO_RIDER = _Rider([], [], {}, [], None, None)


def _call(body, name, grid, in_specs, out_specs, out_shape, scratch, vmem_mb, args, rider=None, prefetch=(),
          shares_rider_refs=False):
    rider = rider or _NO_RIDER
    n_pre, n_in, n_out, n_scr = len(prefetch), len(in_specs), len(out_specs), len(scratch)
    r_in, r_out = len(rider.operands), len(rider.out_shapes)

    def fused(*refs):
        pre, refs = refs[:n_pre], refs[n_pre:]
        ins, refs = refs[:n_in], refs[n_in:]
        r_ins, refs = refs[:r_in], refs[r_in:]
        outs, refs = refs[:n_out], refs[n_out:]
        r_outs, refs = refs[:r_out], refs[r_out:]
        scr, r_scr = refs[:n_scr], refs[n_scr:]
        ids = [pl.program_id(d) for d in range(len(grid))]
        if rider.start is not None:
            @pl.when(functools.reduce(jnp.logical_and, [i == 0 for i in ids]))
            def _():
                rider.start(r_ins, r_outs, r_scr)

        for at, hook in rider.hooks:
            @pl.when(functools.reduce(jnp.logical_and, [i == a % g for i, a, g in zip(ids, at, grid)]))
            def _(hook=hook):
                hook(r_ins, r_outs, r_scr)

        if shares_rider_refs:
            body(*pre, *ins, *outs, *scr, rider_refs=r_outs)
        else:
            body(*pre, *ins, *outs, *scr)

        if rider.finish is not None:
            @pl.when(functools.reduce(jnp.logical_and, [i == g - 1 for i, g in zip(ids, grid)]))
            def _():
                rider.finish(r_ins, r_outs, r_scr)

    return pl.pallas_call(
        fused, name=name,
        grid_spec=pltpu.PrefetchScalarGridSpec(
            num_scalar_prefetch=n_pre, grid=grid,
            in_specs=list(in_specs) + [HBM_SPEC] * r_in, out_specs=list(out_specs) + [HBM_SPEC] * r_out,
            scratch_shapes=list(scratch) + list(rider.scratch)),
        out_shape=list(out_shape) + list(rider.out_shapes),
        input_output_aliases={n_pre + n_in + k: n_out + v for k, v in rider.aliases.items()},
        compiler_params=_params(("arbitrary",) * len(grid), vmem_mb),
    )(*prefetch, *args, *rider.operands)


def _after(token):
    return _Rider([token], [], {}, [], lambda *_: None, lambda *_: None)


def _comm_call(rider, name):
    r_in, r_out = len(rider.operands), len(rider.out_shapes)

    def body(*refs):
        r_ins, r_outs, r_scr = refs[:r_in], refs[r_in:r_in + r_out], refs[r_in + r_out:]
        rider.start(r_ins, r_outs, r_scr)
        rider.finish(r_ins, r_outs, r_scr)

    return pl.pallas_call(
        body, name=name, in_specs=[HBM_SPEC] * r_in, out_specs=[HBM_SPEC] * r_out, out_shape=list(rider.out_shapes),
        input_output_aliases=dict(rider.aliases), scratch_shapes=list(rider.scratch),
    )(*rider.operands)


def _ffn_fwd(order, h, gain, name, rider, group_at, loss_head=None):
    S = h.shape[0]
    tile = min(TOK_TILE, S)
    nt = S // tile
    last = N_CHIPS - 1
    n_extra_in, n_head_out = (2, 4) if loss_head else (0, 1)

    def body(order_ref, h_ref, g_ref, *refs, rider_refs):
        extra_in, refs = refs[:n_extra_in], refs[n_extra_in:]
        head_out, (n_ref, gate_ref, up_ref, w_scr, w_sem, acc, n_scr) = refs[:n_head_out], refs[n_head_out:]
        j, i = pl.program_id(0), pl.program_id(1)

        @pl.when(i == 0)
        def _():
            fetch = pltpu.make_async_copy(rider_refs[group_at].at[order_ref[j]], w_scr, w_sem)
            fetch.start()
            fetch.wait()

        at = pl.multiple_of(i * tile, tile)

        @pl.when(j == 0)
        def _():
            _, xh = _rms_stats(h_ref[...])
            n = (xh * g_ref[...]).astype(BF16)
            n_scr[pl.ds(at, tile), :] = n
            n_ref[...] = n
            acc[pl.ds(at, tile), :] = jnp.zeros((tile, D_MODEL), F32)

        half = tile // 2
        wg, wu, wd = (w_scr[part * FF_CHUNK:(part + 1) * FF_CHUNK, :] for part in range(3))
        ns = [n_scr[pl.ds(at + s * half, half), :] for s in range(2)]
        gates = [_dot_nt(n, wg) for n in ns]
        ups = [_dot_nt(n, wu) for n in ns]
        acts = [(g * jax.nn.sigmoid(g) * u).astype(BF16) for g, u in zip(gates, ups)]
        for s in range(2):
            gate_ref[0, s * half:(s + 1) * half, :] = gates[s].astype(BF16)
            up_ref[0, s * half:(s + 1) * half, :] = ups[s].astype(BF16)
            acc[pl.ds(at + s * half, half), :] += _dot(acts[s], wd)

        @pl.when(j == last)
        def _():
            out = h_ref[...] + 0.5 * acc[pl.ds(at, tile), :]
            if not loss_head:
                head_out[0][...] = out
                return
            (t_ref, gf_ref), (dh_ref, dhalf_ref, loss_ref, dg_ref) = extra_in, head_out

            @pl.when(i == 0)
            def _():
                loss_ref[...] = jnp.zeros_like(loss_ref)
                dg_ref[...] = jnp.zeros_like(dg_ref)

            gf = gf_ref[...]
            r, xh = _rms_stats(out)
            err = xh * gf - t_ref[...]
            loss_ref[...] += (0.5 / D_MODEL) * jnp.sum(err * err, axis=0, keepdims=True)
            dh, dg = _rms_bwd(err * (1.0 / D_MODEL), gf, r, xh)
            dg_ref[...] += dg
            dh_ref[...] = dh
            dhalf_ref[...] = (0.5 * dh).astype(BF16)

    tok = pl.BlockSpec((tile, D_MODEL), lambda j, i, order_ref: (i, 0))
    hid = pl.BlockSpec((1, tile, FF_CHUNK), lambda j, i, order_ref: (order_ref[j], i, 0))
    row = pl.BlockSpec((1, D_MODEL), lambda j, i, order_ref: (0, 0))
    in_last = pl.BlockSpec((tile, D_MODEL), lambda j, i, order_ref: (jnp.where(j == last, i, 0), 0))
    in_first = pl.BlockSpec((tile, D_MODEL), lambda j, i, order_ref: (jnp.where(j == 0, i, nt - 1), 0))
    tok_f32, tok_bf16, lanes = (jax.ShapeDtypeStruct((S, D_MODEL), F32), jax.ShapeDtypeStruct((S, D_MODEL), BF16),
                                jax.ShapeDtypeStruct((1, D_MODEL), F32))
    hidden = jax.ShapeDtypeStruct((N_CHIPS, S, FF_CHUNK), BF16)
    if loss_head:
        extra_specs, extra_args = [in_last, row], list(loss_head)
        head_specs, head_shapes = [in_last, in_last, row, row], [tok_f32, tok_bf16, lanes, lanes]
    else:
        extra_specs, extra_args, head_specs, head_shapes = [], [], [in_last], [tok_f32]
    return _call(
        body, name, (N_CHIPS, nt), [tok, row] + extra_specs, head_specs + [in_first, hid, hid],
        head_shapes + [tok_bf16, hidden, hidden],
        [pltpu.VMEM((3 * FF_CHUNK, D_MODEL), BF16), pltpu.SemaphoreType.DMA, pltpu.VMEM((S, D_MODEL), F32),
         pltpu.VMEM((S, D_MODEL), BF16)], 58, (h, gain, *extra_args), rider, (order,), shares_rider_refs=True)


def _ffn_bwd(chunks, d_out, n, gate, up, group, name, rider=None):
    S = n.shape[0]
    n_chunks = chunks.shape[0]
    tile = min(TOK_TILE, S)
    nt = S // tile
    half_rows = 3 * FF_CHUNK // 2
    cut = FF_CHUNK // 2

    def body(chunks_ref, do_ref, n_ref, gate_ref, up_ref, wg_ref, wu_ref, wd_ref, dn_ref, dw_ref, acc_g, acc_u, acc_d):
        j, i = pl.program_id(0), pl.program_id(1)

        @pl.when(i == 0)
        def _():
            acc_g[...] = jnp.zeros_like(acc_g)
            acc_u[...] = jnp.zeros_like(acc_u)
            acc_d[...] = jnp.zeros_like(acc_d)

        halves = [pl.ds(s * (tile // 2), tile // 2) for s in range(2)]
        dos = [do_ref[rows, :] for rows in halves]
        d_acts = [_dot_nt(do, wd_ref[0]) for do in dos]
        gs = [gate_ref[0, rows, :].astype(F32) for rows in halves]
        us = [up_ref[0, rows, :].astype(F32) for rows in halves]
        sigs = [jax.nn.sigmoid(g) for g in gs]
        silus = [g * sig for g, sig in zip(gs, sigs)]
        d_ups = [(d_act * silu).astype(BF16) for d_act, silu in zip(d_acts, silus)]
        d_gates = [(d_act * u * (sig * (1.0 + g * (1.0 - sig)))).astype(BF16) for d_act, u, sig, g in zip(d_acts, us, sigs, gs)]
        for rows, d_gate, d_up in zip(halves, d_gates, d_ups):
            dn_ref[0, rows, :] = (_dot(d_gate, wg_ref[0]) + _dot(d_up, wu_ref[0])).astype(BF16)
        d_gate, d_up = jnp.concatenate(d_gates, axis=0), jnp.concatenate(d_ups, axis=0)
        act = jnp.concatenate([(silu * u).astype(BF16) for silu, u in zip(silus, us)], axis=0)
        nn = n_ref[...]
        acc_g[...] += _dot_tn(d_gate, nn)
        acc_u[...] += _dot_tn(d_up, nn)
        acc_d[...] += _dot_tn(act, do_ref[...])

        @pl.when(i == nt - 1)
        def _():
            dw_ref[0, 0, :FF_CHUNK, :] = acc_g[...].astype(BF16)
            dw_ref[0, 0, FF_CHUNK:, :] = acc_u[:cut, :].astype(BF16)
            dw_ref[1, 0, :cut, :] = acc_u[cut:, :].astype(BF16)
            dw_ref[1, 0, cut:, :] = acc_d[...].astype(BF16)

    tok = pl.BlockSpec((tile, D_MODEL), lambda j, i, chunks_ref: (i, 0))
    hid = pl.BlockSpec((1, tile, FF_CHUNK), lambda j, i, chunks_ref: (chunks_ref[j], i, 0))
    return _call(
        body, name, (n_chunks, nt),
        [tok, tok, hid, hid]
        + [pl.BlockSpec((1, FF_CHUNK, D_MODEL), functools.partial(lambda j, i, chunks_ref, part: (chunks_ref[j], part, 0), part=part))
           for part in range(3)],
        [pl.BlockSpec((1, tile, D_MODEL), lambda j, i, chunks_ref: (j, i, 0)),
         pl.BlockSpec((2, 1, half_rows, D_MODEL), lambda j, i, chunks_ref: (0, j, 0, 0))],
        [jax.ShapeDtypeStruct((n_chunks, S, D_MODEL), BF16), jax.ShapeDtypeStruct((2, n_chunks, half_rows, D_MODEL), BF16)],
        [pltpu.VMEM((FF_CHUNK, D_MODEL), F32)] * 3, 56, (d_out, n, gate, up, group, group, group), rider, (chunks,))


def _mix_in(h, gain, w_in, rc, rs1, rs2, name, rider=None):
    S = h.shape[0]
    tile = min(TOK_TILE, S)

    def body(h_ref, g_ref, w_ref, c_ref, s1_ref, s2_ref, n_ref, q_ref, k_ref, v_ref, pc_ref):
        _, xh = _rms_stats(h_ref[...])
        n = (xh * g_ref[...]).astype(BF16)
        n_ref[...] = n
        u = _dot_nt(n, w_ref[...])
        c, s1, s2 = c_ref[...], s1_ref[...], s2_ref[...]
        q_ref[...] = jnp.concatenate([(_rope(u[:, hd * LANES:(hd + 1) * LANES], c, s1, s2) * SCALE).astype(BF16)
                                      for hd in range(N_HEADS)], axis=1)
        k_ref[...] = _rope(u[:, Q_PAD:Q_PAD + KV_WIDTH], c, s1, s2).astype(BF16)
        v_ref[...] = u[:, Q_PAD + KV_WIDTH:Q_PAD + 2 * KV_WIDTH].astype(BF16)
        pc_ref[...] = u[:, Q_PAD + 2 * KV_WIDTH:]

    return _call(
        body, name, (S // tile,),
        [_rows(tile, D_MODEL), _full((1, D_MODEL)), _full((U_PAD, D_MODEL)),
         _rows(tile, LANES), _rows(tile, LANES), _rows(tile, LANES)],
        [_rows(tile, D_MODEL), _rows(tile, Q_PAD), _rows(tile, KV_WIDTH), _rows(tile, KV_WIDTH), _rows(tile, POOL_WIDTH)],
        [jax.ShapeDtypeStruct((S, D_MODEL), BF16), jax.ShapeDtypeStruct((S, Q_PAD), BF16),
         jax.ShapeDtypeStruct((S, KV_WIDTH), BF16), jax.ShapeDtypeStruct((S, KV_WIDTH), BF16),
         jax.ShapeDtypeStruct((S, POOL_WIDTH), F32)],
        [], 40, (h, gain, w_in, rc, rs1, rs2), rider)


def _band_start(i, S):
    return pl.multiple_of(jnp.clip((i - 1) * BLOCK, 0, S - BAND), BLOCK)


def _window_bias(off):
    r = lax.broadcasted_iota(jnp.int32, (BLOCK, 1), 0)
    c = lax.broadcasted_iota(jnp.int32, (1, BAND), 1)
    return jnp.where(jnp.abs(off + r - c) <= WINDOW, 0.0, NEG).astype(F32)


def _softmax_parts(qh, kb, bias, sink_h):
    s = _dot_nt(qh, kb) + bias
    m = jnp.maximum(jnp.max(s, axis=-1, keepdims=True), sink_h)
    p = jnp.exp(s - m)
    es = jnp.exp(sink_h - m)
    return p, es, 1.0 / (jnp.sum(p, axis=-1, keepdims=True) + es)


def _pool_matrix(t0, start, S, w):
    r = lax.broadcasted_iota(jnp.int32, (BLOCK, 1), 0) + t0
    c = lax.broadcasted_iota(jnp.int32, (1, BAND), 1) + start
    half = w // 2

    def window(lo, hi):
        a = jnp.maximum(lo, 0)
        b = jnp.minimum(hi + 1, S)
        return jnp.where((c >= a) & (c < b), 1.0 / (b - a).astype(F32), 0.0)

    return (0.5 * (window(r - half, r + half - 1) + window(r - half + 1, r + half))).astype(BF16)


def _pool_matrices(S):
    blocks = ((0, 0), (BLOCK, 0), (S - BLOCK, S - BAND))
    return jnp.stack([jnp.stack([_pool_matrix(t0, start, S, w) for w in POOL_WINDOWS]) for t0, start in blocks])


def _pool_spec(nb):
    return pl.BlockSpec((1, N_POOL, BLOCK, BAND), lambda i, *_: (jnp.where(i == 0, 0, jnp.where(i == nb - 1, 2, 1)), 0, 0, 0))


def _mix_core_fwd(q, k, v, pc, sink, pool_m, pool_w, pool_scale, name):
    S = q.shape[0]
    nb = S // BLOCK

    def body(sink_ref, q_ref, k_ref, v_ref, pc_ref, pm_ref, pw_ref, ps_ref, a_ref, p_ref):
        i = pl.program_id(0)
        start = _band_start(i, S)
        band = pl.ds(start, BAND)
        bias = _window_bias(i * BLOCK - start)
        kb, vb = k_ref[band, :], v_ref[band, :]
        hs = range(N_HEADS)
        ss = [_dot_nt(q_ref[:, hd * LANES:(hd + 1) * LANES], kb) + bias for hd in hs]
        ms = [jnp.maximum(jnp.max(ss[hd], axis=-1, keepdims=True), sink_ref[0, hd]) for hd in hs]
        ps = [jnp.exp(ss[hd] - ms[hd]) for hd in hs]
        invs = [1.0 / (jnp.sum(ps[hd], axis=-1, keepdims=True) + jnp.exp(sink_ref[0, hd] - ms[hd])) for hd in hs]
        outs = [_dot(ps[hd].astype(BF16), vb) for hd in hs]
        a_ref[...] = jnp.concatenate([(outs[hd] * invs[hd]).astype(BF16) for hd in hs], axis=1)
        centre = pl.ds(pl.multiple_of(i * BLOCK, BLOCK), BLOCK)
        gs = range(N_POOL)
        sl = [slice(g * POOL_GROUP, (g + 1) * POOL_GROUP) for g in gs]
        means = [_dot(pm_ref[0, g], pc_ref[band, sl[g]].astype(BF16)) for g in gs]
        devs = [(means[g] - pc_ref[centre, sl[g]]).astype(BF16) for g in gs]
        p_ref[...] = (jnp.concatenate([_dot(devs[g], pw_ref[g]) for g in gs], axis=1) * ps_ref[...]).astype(BF16)

    return _call(
        body, name, (nb,),
        [pl.BlockSpec(memory_space=pltpu.SMEM), _rows(BLOCK, Q_PAD), _full((S, KV_WIDTH)), _full((S, KV_WIDTH)),
         _full((S, POOL_WIDTH)), _pool_spec(nb), _full((N_POOL, POOL_GROUP, POOL_GROUP)), _full((1, POOL_WIDTH))],
        [_rows(BLOCK, Q_PAD), _rows(BLOCK, POOL_WIDTH)],
        [jax.ShapeDtypeStruct((S, Q_PAD), BF16), jax.ShapeDtypeStruct((S, POOL_WIDTH), BF16)],
        [], 40, (sink, q, k, v, pc, pool_m, pool_w, pool_scale))


def _mix_core_bwd(q, k, v, pc, da, dp, sink, pool_m, pool_w, pool_scale, rc, rs1, rs2, name, rider=None):
    S = q.shape[0]
    nb = S // BLOCK

    def body(sink_ref, q_ref, k_ref, v_ref, pc_ref, da_ref, dp_ref, pm_ref, pw_ref, ps_ref, c_ref, s1_ref, s2_ref,
             dq_ref, dk_ref, dv_ref, dpc_ref, dsink_ref, dpw_ref, dps_ref):
        i = pl.program_id(0)

        @pl.when(i == 0)
        def _():
            dk_ref[...] = jnp.zeros_like(dk_ref)
            dv_ref[...] = jnp.zeros_like(dv_ref)
            dpc_ref[...] = jnp.zeros_like(dpc_ref)
            dsink_ref[...] = jnp.zeros_like(dsink_ref)
            dpw_ref[...] = jnp.zeros_like(dpw_ref)
            dps_ref[...] = jnp.zeros_like(dps_ref)

        start = _band_start(i, S)
        band = pl.ds(start, BAND)
        bias = _window_bias(i * BLOCK - start)
        kb, vb = k_ref[band, :], v_ref[band, :]
        c, s1, s2 = c_ref[...], s1_ref[...], s2_ref[...]
        lane = lax.broadcasted_iota(jnp.int32, (1, LANES), 1)
        hs = range(N_HEADS)
        qs = [q_ref[:, hd * LANES:(hd + 1) * LANES] for hd in hs]
        das = [da_ref[:, hd * LANES:(hd + 1) * LANES] for hd in hs]
        ss = [_dot_nt(qs[hd], kb) + bias for hd in hs]
        d_probs = [_dot_nt(das[hd], vb) for hd in hs]
        ms = [jnp.maximum(jnp.max(ss[hd], axis=-1, keepdims=True), sink_ref[0, hd]) for hd in hs]
        ps = [jnp.exp(ss[hd] - ms[hd]) for hd in hs]
        ess = [jnp.exp(sink_ref[0, hd] - ms[hd]) for hd in hs]
        invs = [1.0 / (jnp.sum(ps[hd], axis=-1, keepdims=True) + ess[hd]) for hd in hs]
        probs = [ps[hd] * invs[hd] for hd in hs]
        deltas = [jnp.sum(probs[hd] * d_probs[hd], axis=-1, keepdims=True) for hd in hs]
        d_ss = [(probs[hd] * (d_probs[hd] - deltas[hd])).astype(BF16) for hd in hs]
        dqs = [_dot(d_ss[hd], kb) for hd in hs]
        dq_ref[...] = jnp.concatenate([_rope_bwd(dqs[hd] * SCALE, c, s1, s2).astype(BF16) for hd in hs], axis=1)
        dks = [_dot_tn(d_ss[hd], qs[hd]) for hd in hs]
        dvs = [_dot_tn(probs[hd].astype(BF16), das[hd]) for hd in hs]
        dk_ref[band, :] += functools.reduce(lambda a, b: a + b, dks)
        dv_ref[band, :] += functools.reduce(lambda a, b: a + b, dvs)
        dsink_ref[...] += functools.reduce(lambda a, b: a + b, [
            jnp.where(lane == hd, -jnp.sum(ess[hd] * invs[hd] * deltas[hd], axis=0, keepdims=True), 0.0) for hd in hs])

        centre = pl.ds(pl.multiple_of(i * BLOCK, BLOCK), BLOCK)
        gs = range(N_POOL)
        sl = [slice(g * POOL_GROUP, (g + 1) * POOL_GROUP) for g in gs]
        devs = [(_dot(pm_ref[0, g], pc_ref[band, sl[g]].astype(BF16)) - pc_ref[centre, sl[g]]).astype(BF16) for g in gs]
        dys = [dp_ref[:, sl[g]].astype(F32) for g in gs]
        zs = [_dot(devs[g], pw_ref[g]) for g in gs]
        dzs = [(dys[g] * ps_ref[:, sl[g]]).astype(BF16) for g in gs]
        d_devs = [_dot_nt(dzs[g], pw_ref[g]) for g in gs]
        dps_ref[...] += jnp.concatenate([jnp.sum(dys[g] * zs[g], axis=0, keepdims=True) for g in gs], axis=1)
        for g in gs:
            dpw_ref[g] += _dot_tn(devs[g], dzs[g])
        dpc_ref[band, :] += jnp.concatenate([_dot_tn(pm_ref[0, g], d_devs[g].astype(BF16)) for g in gs], axis=1)
        dpc_ref[centre, :] -= jnp.concatenate(d_devs, axis=1)

    return _call(
        body, name, (nb,),
        [pl.BlockSpec(memory_space=pltpu.SMEM), _rows(BLOCK, Q_PAD), _full((S, KV_WIDTH)), _full((S, KV_WIDTH)),
         _full((S, POOL_WIDTH)), _rows(BLOCK, Q_PAD), _rows(BLOCK, POOL_WIDTH), _pool_spec(nb),
         _full((N_POOL, POOL_GROUP, POOL_GROUP)), _full((1, POOL_WIDTH)),
         _rows(BLOCK, LANES), _rows(BLOCK, LANES), _rows(BLOCK, LANES)],
        [_rows(BLOCK, Q_PAD), _full((S, KV_WIDTH)), _full((S, KV_WIDTH)), _full((S, POOL_WIDTH)),
         _full((1, LANES)), _full((N_POOL, POOL_GROUP, POOL_GROUP)), _full((1, POOL_WIDTH))],
        [jax.ShapeDtypeStruct((S, Q_PAD), BF16), jax.ShapeDtypeStruct((S, KV_WIDTH), F32),
         jax.ShapeDtypeStruct((S, KV_WIDTH), F32), jax.ShapeDtypeStruct((S, POOL_WIDTH), F32),
         jax.ShapeDtypeStruct((1, LANES), F32), jax.ShapeDtypeStruct((N_POOL, POOL_GROUP, POOL_GROUP), F32),
         jax.ShapeDtypeStruct((1, POOL_WIDTH), F32)],
        [], 56, (sink, q, k, v, pc, da, dp, pool_m, pool_w, pool_scale, rc, rs1, rs2), rider)


def _mix_out(h, a, p, wa, wp, name):
    S = h.shape[0]
    tile = min(TOK_TILE, S)

    def body(h_ref, a_ref, p_ref, wa_ref, wp_ref, o_ref):
        o_ref[...] = h_ref[...] + _dot(a_ref[...], wa_ref[...]) + _dot(p_ref[...], wp_ref[...])

    return pl.pallas_call(
        body, name=name, grid=(S // tile,),
        in_specs=[_rows(tile, D_MODEL), _rows(tile, Q_PAD), _rows(tile, POOL_WIDTH),
                  _full((Q_PAD, D_MODEL)), _full((POOL_WIDTH, D_MODEL))],
        out_specs=_rows(tile, D_MODEL),
        out_shape=jax.ShapeDtypeStruct((S, D_MODEL), F32),
        compiler_params=_params(("parallel",), 40),
    )(h, a, p, wa, wp)


def _mix_out_bwd(dh_out, dn, h, gain, a, p, wa, wp, name):
    S = h.shape[0]
    tile = min(TOK_TILE, S)

    def body(do_ref, dn_ref, h_ref, g_ref, a_ref, p_ref, wa_ref, wp_ref, dh_ref, da_ref, dp_ref, dwa_ref, dwp_ref, dg_ref):
        @pl.when(pl.program_id(0) == 0)
        def _():
            dwa_ref[...] = jnp.zeros_like(dwa_ref)
            dwp_ref[...] = jnp.zeros_like(dwp_ref)
            dg_ref[...] = jnp.zeros_like(dg_ref)

        r, xh = _rms_stats(h_ref[...])
        dnorm, dg = _rms_bwd(_sum_chunks([dn_ref]), g_ref[...], r, xh)
        dh = do_ref[...] + dnorm
        dg_ref[...] += dg
        dh_ref[...] = dh
        dhb = dh.astype(BF16)
        da_ref[...] = _dot_nt(dhb, wa_ref[...]).astype(BF16)
        dp_ref[...] = _dot_nt(dhb, wp_ref[...]).astype(BF16)
        dwa_ref[...] += _dot_tn(a_ref[...], dhb)
        dwp_ref[...] += _dot_tn(p_ref[...], dhb)

    return _call(
        body, name, (S // tile,),
        [_rows(tile, D_MODEL), _chunk_rows(tile, dn.shape[0]), _rows(tile, D_MODEL), _full((1, D_MODEL)),
         _rows(tile, Q_PAD), _rows(tile, POOL_WIDTH), _full((Q_PAD, D_MODEL)), _full((POOL_WIDTH, D_MODEL))],
        [_rows(tile, D_MODEL), _rows(tile, Q_PAD), _rows(tile, POOL_WIDTH),
         _full((Q_PAD, D_MODEL)), _full((POOL_WIDTH, D_MODEL)), _full((1, D_MODEL))],
        [jax.ShapeDtypeStruct((S, D_MODEL), F32), jax.ShapeDtypeStruct((S, Q_PAD), BF16),
         jax.ShapeDtypeStruct((S, POOL_WIDTH), BF16), jax.ShapeDtypeStruct((Q_PAD, D_MODEL), F32),
         jax.ShapeDtypeStruct((POOL_WIDTH, D_MODEL), F32), jax.ShapeDtypeStruct((1, D_MODEL), F32)],
        [], 48, (dh_out, dn, h, gain, a, p, wa, wp))


def _mix_in_bwd(dh_out, h, gain, n, dq, dk, dv, dpc, rc, rs1, rs2, w_in, name, rider=None):
    S = h.shape[0]
    tile = min(TOK_TILE, S)

    def body(do_ref, h_ref, g_ref, n_ref, dq_ref, dk_ref, dv_ref, dpc_ref, c_ref, s1_ref, s2_ref, w_ref,
             dh_ref, dhalf_ref, dw_ref, dg_ref):
        @pl.when(pl.program_id(0) == 0)
        def _():
            dw_ref[...] = jnp.zeros_like(dw_ref)
            dg_ref[...] = jnp.zeros_like(dg_ref)

        dk = _rope_bwd(dk_ref[...], c_ref[...], s1_ref[...], s2_ref[...]).astype(BF16)
        du = jnp.concatenate([dq_ref[...], dk, dv_ref[...].astype(BF16), dpc_ref[...].astype(BF16)], axis=1)
        dn = _dot(du, w_ref[...])
        dw_ref[...] += _dot_tn(du, n_ref[...])
        r, xh = _rms_stats(h_ref[...])
        dnorm, dg = _rms_bwd(dn, g_ref[...], r, xh)
        dh = do_ref[...] + dnorm
        dg_ref[...] += dg
        dh_ref[...] = dh
        dhalf_ref[...] = (0.5 * dh).astype(BF16)

    return _call(
        body, name, (S // tile,),
        [_rows(tile, D_MODEL), _rows(tile, D_MODEL), _full((1, D_MODEL)), _rows(tile, D_MODEL),
         _rows(tile, Q_PAD), _rows(tile, KV_WIDTH), _rows(tile, KV_WIDTH), _rows(tile, POOL_WIDTH),
         _rows(tile, LANES), _rows(tile, LANES), _rows(tile, LANES), _full((U_PAD, D_MODEL))],
        [_rows(tile, D_MODEL), _rows(tile, D_MODEL), _full((U_PAD, D_MODEL)), _full((1, D_MODEL))],
        [jax.ShapeDtypeStruct((S, D_MODEL), F32), jax.ShapeDtypeStruct((S, D_MODEL), BF16),
         jax.ShapeDtypeStruct((U_PAD, D_MODEL), F32), jax.ShapeDtypeStruct((1, D_MODEL), F32)],
        [], 56, (dh_out, h, gain, n, dq, dk, dv, dpc, rc, rs1, rs2, w_in), rider)


def _norm_bwd(dh_out, dns, h, gain, name, rider=None):
    S = h.shape[0]
    tile = min(TOK_TILE, S)
    n = len(dns)

    def body(do_ref, *refs):
        h_ref, g_ref, dh_ref, dg_ref = refs[n:]

        @pl.when(pl.program_id(0) == 0)
        def _():
            dg_ref[...] = jnp.zeros_like(dg_ref)

        r, xh = _rms_stats(h_ref[...])
        dnorm, dg = _rms_bwd(_sum_chunks(refs[:n]), g_ref[...], r, xh)
        dg_ref[...] += dg
        dh_ref[...] = do_ref[...] + dnorm

    return _call(
        body, name, (S // tile,),
        [_rows(tile, D_MODEL)] + [_chunk_rows(tile, dn.shape[0]) for dn in dns] + [_rows(tile, D_MODEL), _full((1, D_MODEL))],
        [_rows(tile, D_MODEL), _full((1, D_MODEL))],
        [jax.ShapeDtypeStruct((S, D_MODEL), F32), jax.ShapeDtypeStruct((1, D_MODEL), F32)],
        [], 40, (dh_out, *dns, h, gain), rider)


def _rope_tables(S):
    half = ROTARY_DIM // 2
    inv_freq = ROPE_THETA ** (-jnp.arange(0, ROTARY_DIM, 2, dtype=F32) / ROTARY_DIM)
    dim = jnp.arange(LANES) % HEAD_DIM
    ang = jnp.arange(S, dtype=F32)[:, None] * inv_freq[dim % half][None, :]
    lo, hi = (dim < half)[None, :], ((dim >= half) & (dim < ROTARY_DIM))[None, :]
    c = jnp.where(lo | hi, jnp.cos(ang), 1.0)
    s1 = jnp.where(lo, -jnp.sin(ang), 0.0)
    s2 = jnp.where(hi, jnp.sin(ang), 0.0)
    return c, s1, s2


def _pad_heads(w, axis):
    w = jnp.moveaxis(w, axis, 0)
    heads = w.reshape((N_HEADS, HEAD_DIM) + w.shape[1:])
    zero = jnp.zeros_like(heads)
    first = (jnp.arange(N_HEADS) < Q_PER_KV).reshape((N_HEADS, 1) + (1,) * (w.ndim - 1))
    lo = jnp.where(first, heads, zero)
    hi = jnp.where(first, zero, heads)
    padded = jnp.concatenate([lo, hi], axis=1).reshape((Q_PAD,) + w.shape[1:])
    return jnp.moveaxis(padded, 0, axis)


def _unpad_heads(w, axis):
    w = jnp.moveaxis(w, axis, 0)
    groups = w.reshape((N_HEADS, 2, HEAD_DIM) + w.shape[1:])
    first = (jnp.arange(N_HEADS) < Q_PER_KV).reshape((N_HEADS, 1) + (1,) * (w.ndim - 1))
    heads = jnp.where(first, groups[:, 0], groups[:, 1]).reshape((ATTN_WIDTH,) + w.shape[1:])
    return jnp.moveaxis(heads, 0, axis)


IN_ROWS = IN_WIDTH // N_CHIPS
OUT_ROWS = (ATTN_WIDTH + POOL_WIDTH) // N_CHIPS
MIX_ROWS = IN_ROWS + OUT_ROWS
FFN_ROWS = 3 * FF_CHUNK


def _step(x, target, bufs, small, place, update):
    S = x.shape[0]
    rc, rs1, rs2 = _rope_tables(S)
    mine = place[1]
    order = jnp.stack([mine, mine ^ 2, mine ^ 1, mine ^ 3])
    h1, n1, gate1, up1, ffn1, mix = _ffn_fwd(order, x, small["ffn1_norm"], "ffn1_fwd",
                                             _merge(_allgather(bufs[:1], in_passes=0), _allgather(bufs[1:2])), 0)
    in_flight, (ffn2,), token = _split_start(bufs[2:], 2, _gather_copies(X_Y_DIAGONAL[:2]), "ffn2_gather_start")
    w_in_t = mix[:, :IN_ROWS].reshape(IN_WIDTH, D_MODEL)
    w_in_pad = jnp.concatenate([_pad_heads(w_in_t[:ATTN_WIDTH], 0), w_in_t[ATTN_WIDTH:]], axis=0)
    w_out = mix[:, IN_ROWS:].reshape(ATTN_WIDTH + POOL_WIDTH, D_MODEL)
    wa = _pad_heads(w_out[:ATTN_WIDTH], 0)
    wp = w_out[ATTN_WIDTH:]
    pool_w = small["pool_w"].astype(BF16)

    n2, q, k, v, pc = _mix_in(h1, small["mix_norm"], w_in_pad, rc, rs1, rs2, "mix_in", _after(token))
    pool_m = _pool_matrices(S)
    a, p = _mix_core_fwd(q, k, v, pc, small["sink_logits"], pool_m, pool_w, small["pool_scale"], "mix_core_fwd")
    h2 = _mix_out(h1, a, p, wa, wp, "mix_out")
    (ffn2,) = _split_wait(in_flight, [ffn2], _gather_copies(X_Y_DIAGONAL[:2]), [h2], "ffn2_gather_wait")
    dh3, dhalf3, loss_lanes, d_final, n3, gate2, up2, ffn2 = _ffn_fwd(
        order, h2, small["ffn2_norm"], "ffn2_fwd",
        _allgather([ffn2], peers=X_Y_DIAGONAL[2:], in_passes=0, landed_before=X_Y_DIAGONAL[:2]), 0,
        loss_head=(target, small["final_norm"]))

    dn3, d_ffn2 = _ffn_bwd(jnp.arange(N_CHIPS, dtype=jnp.int32), dhalf3, n3, gate2, up2, ffn2, "ffn2_bwd")
    dh2, da, dp, dwa, dwp, d_ffn2_norm = _mix_out_bwd(dh3, dn3, h2, small["ffn2_norm"], a, p, wa, wp, "mix_out_bwd")
    dq, dk, dv, dpc, dsink, dpool_w, dpool_scale, received = _mix_core_bwd(
        q, k, v, pc, da, dp, small["sink_logits"], pool_m, pool_w, small["pool_scale"], rc, rs1, rs2, "mix_core_bwd",
        _sibling_exchange([d_ffn2]))
    pair = _pair_sum(place, d_ffn2, received, "grad_pair_sum_ffn2")
    in_flight, (pair, stack), token = _split_start([pair, received], 3, _scatter_copies, "ffn2_scatter_start")
    dh1, dhalf1, dw_in_pad, d_mix_norm = _mix_in_bwd(dh2, h1, small["mix_norm"], n2, dq, dk, dv, dpc, rc, rs1, rs2, w_in_pad,
                                                     "mix_in_bwd", _after(token))
    dw_in_t = jnp.concatenate([_unpad_heads(dw_in_pad[:Q_PAD], 0), dw_in_pad[Q_PAD:]], axis=0)
    dw_out = jnp.concatenate([_unpad_heads(dwa, 0), dwp], axis=0)
    d_mix = jnp.concatenate([dw_in_t.reshape(N_CHIPS, IN_ROWS, D_MODEL), dw_out.reshape(N_CHIPS, OUT_ROWS, D_MODEL)], axis=1)
    d_mix = jnp.transpose(d_mix.reshape(N_CHIPS, 2, MIX_ROWS // 2, D_MODEL), (1, 0, 2, 3)).astype(BF16)
    small_g = {"ffn1_norm": jnp.zeros_like(d_mix_norm), "mix_norm": d_mix_norm, "ffn2_norm": d_ffn2_norm,
               "final_norm": d_final, "pool_scale": dpool_scale, "sink_logits": dsink[:, :N_HEADS], "pool_w": dpool_w}
    loss_row = jnp.sum(loss_lanes.reshape(D_MODEL // LANES, LANES), axis=0, keepdims=True)
    small_early = _pack_small(small_g, loss_row)

    chunk = [(place[1:] + 1 + p) % N_CHIPS for p in range(N_CHIPS)]
    ffn1_bwd = functools.partial(_ffn_bwd, d_out=dhalf1, n=n1, gate=gate1, up=up1, group=ffn1)
    dn_a, dw_a, recv_mix, small_all = ffn1_bwd(
        chunk[0], name="ffn1_bwd_0", rider=_merge(_sibling_exchange([d_mix]), _small_allgather(small_early)))
    pair, stack = _split_wait(in_flight, [pair, stack], _scatter_copies, [dn_a], "ffn2_scatter_wait")
    reduced_ffn2 = _chip_sum(place, pair, stack, 2, "grad_chip_sum_ffn2")
    pair_mix = _pair_sum(place, d_mix, recv_mix, "grad_pair_sum_mix")
    dn_b, dw_b, recv_a, stack_mix, g_ffn2 = ffn1_bwd(
        chunk[1], name="ffn1_bwd_1",
        rider=_merge(_merge(_sibling_exchange([dw_a]), _scatter([pair_mix])), _sibling_share([reduced_ffn2])))
    pair_a = _pair_sum(place, dw_a, recv_a, "grad_pair_sum_ffn1_0")
    reduced_mix = _chip_sum(place, pair_mix, stack_mix, 2, "grad_chip_sum_mix")
    steps = [(_scatter_step_copies(p), (p, 3)) for p in range(3)]
    sems_a, (pair_a, stack), token = _split_start([pair_a, stack], 1, steps[0][0], "ffn1_scatter_start_0")
    dn_c, dw_c, recv_b, g_mix = ffn1_bwd(
        chunk[2], name="ffn1_bwd_2", rider=_merge(_merge(_sibling_exchange([dw_b]), _sibling_share([reduced_mix])), _after(token)))
    pair_b = _pair_sum(place, dw_b, recv_b, "grad_pair_sum_ffn1_1")
    sems_b, (pair_b, stack), token = _split_start([pair_b, stack], 1, steps[1][0], "ffn1_scatter_start_1")
    dn_d, dw_d, recv_c = ffn1_bwd(chunk[3], name="ffn1_bwd_3", rider=_merge(_sibling_exchange([dw_c]), _after(token)))
    pair_c = _pair_sum(place, dw_c, recv_c, "grad_pair_sum_ffn1_2")
    own = (_exchange_copies, (4, 5))
    sems_c, (pair_c, stack, dw_d, recv_d), token = _split_start(
        [pair_c, stack, dw_d, recv_c], 2, _joined([(steps[2][0], (0, 1)), (own[0], (2, 3))]), "ffn1_scatter_start_2")
    grad_x, d_ffn1_norm = _norm_bwd(dh1, [dn_a, dn_b, dn_c, dn_d], x, small["ffn1_norm"], "norm1_bwd", _after(token))
    updated = (update(GROUPS[2], g_ffn2.reshape(-1, D_MODEL), d_ffn1_norm)
               + update(GROUPS[1], g_mix.reshape(-1, D_MODEL), d_ffn1_norm))
    pair_a, pair_b, pair_c, stack, dw_d, recv_d = _split_wait(
        sems_a + sems_b + sems_c, [pair_a, pair_b, pair_c, stack, dw_d, recv_d], _joined(steps + [own]), updated,
        "ffn1_scatter_wait")
    pair_d = _pair_sum(place, dw_d, recv_d, "grad_pair_sum_ffn1_3")
    reduced_ffn1 = _chip_sum(place, pair_d, stack, 2, "grad_chip_sum_ffn1")
    g_ffn1, gains = _comm_call(_merge(_sibling_share([reduced_ffn1]), _small_allgather(d_ffn1_norm.reshape(-1, LANES))),
                               "grad_share_tail")
    gain_sum = _sum_leading(gains, 1, "gain_grad_sum")
    small_sum = jnp.concatenate([gain_sum, _sum_leading(small_all, 1, "small_grad_sum")[gain_sum.shape[0]:]], axis=0)
    update(GROUPS[0], g_ffn1.reshape(-1, D_MODEL))
    return jnp.sum(small_sum[SMALL_ROWS - 1]), grad_x, small_sum


GROUPS =(("ffn1_w_gate", "ffn1_w_up", "ffn1_w_down"), ("w_in", "w_out"), ("ffn2_w_gate", "ffn2_w_up", "ffn2_w_down"))
TRANSPOSED = ("ffn1_w_gate", "ffn1_w_up", "w_in", "ffn2_w_gate", "ffn2_w_up")


def _place():
    x, y, c = lax.axis_index("x"), lax.axis_index("y"), lax.axis_index("c")
    chips = [(1 - x, y), (x, 1 - y), (1 - x, 1 - y)]
    return x, y, c, chips


def _remote(src, dst, send_sem, recv_sem, to):
    return pltpu.make_async_remote_copy(src_ref=src, dst_ref=dst, send_sem=send_sem, recv_sem=recv_sem,
                                        device_id=to, device_id_type=MESH)


def _pack(chip, members, name):
    rows = members[0].shape[0]
    n = len(members)

    def body(chip_ref, *refs):
        ins, out_ref, buf, sems = refs[:n], refs[n], refs[n + 1], refs[n + 2]
        copies = [pltpu.make_async_copy(ins[k], buf.at[k], sems.at[k]) for k in range(n)]
        for cp in copies:
            cp.start()
        for k in range(n):
            copies[k].wait()
            out_ref[0, k * rows:(k + 1) * rows, :] = buf[k].astype(BF16)

    return pl.pallas_call(
        body, name=name,
        grid_spec=pltpu.PrefetchScalarGridSpec(
            num_scalar_prefetch=1, grid=(1,),
            in_specs=[HBM_SPEC] * n,
            out_specs=pl.BlockSpec((1, n * rows, D_MODEL), lambda k, chip_ref: (chip_ref[0], 0, 0)),
            scratch_shapes=[pltpu.VMEM((n, rows, D_MODEL), F32), pltpu.SemaphoreType.DMA((n,))]),
        out_shape=jax.ShapeDtypeStruct((N_CHIPS, n * rows, D_MODEL), BF16),
        compiler_params=_params(("arbitrary",), 40),
    )(chip, *members)


def _same(arrays):
    return [jax.ShapeDtypeStruct(a.shape, a.dtype) for a in arrays]


X_Y_DIAGONAL = (0, 1, 2)


def _allgather(bufs, peers=X_Y_DIAGONAL, in_passes=None, landed_before=()):
    n = len(bufs)

    def copy(kind, outs, send_sems, recv_sems, a, k):
        x, y, c, chips = _place()
        half = bufs[a].shape[1] // 2

        def rows(slot, core):
            return outs[a].at[slot, pl.ds(pl.multiple_of(core * half, 16), half)]

        me, slot = 2 * x + y, 2 * chips[k][0] + chips[k][1]
        over_ici = (send_sems.at[6 * a + k], recv_sems.at[6 * a + k])
        over_d2d = (send_sems.at[6 * a + 3 + k], recv_sems.at[6 * a + 3 + k])
        if kind == "first":
            return _remote(rows(me, c), rows(me, c), *over_ici, (*chips[k], c))
        if kind == "landed":
            return _remote(rows(me, c), rows(slot, c), *over_ici, (*chips[k], c))
        if kind == "passed":
            return _remote(rows(slot, c), rows(slot, c), *over_d2d, (x, y, 1 - c))
        return _remote(rows(me, c), rows(slot, 1 - c), *over_d2d, (x, y, 1 - c))

    def land(pairs):
        def hook(ins, outs, sems):
            for a, k in pairs:
                if k not in landed_before:
                    copy("landed", outs, *sems, a, k).wait_recv()
                    copy("passed", outs, *sems, a, k).start()
        return hook

    def handed(pairs):
        def hook(ins, outs, sems):
            for a, k in pairs:
                copy("handed", outs, *sems, a, k).wait_recv()
        return hook

    sent = [(a, k) for a in range(n) for k in peers]
    everything = sent + [(in_passes, k) for k in landed_before]
    early = [(a, k) for a, k in everything if a == in_passes]

    def start(ins, outs, sems):
        for k in landed_before:
            copy("passed", outs, *sems, in_passes, k).start()
        for a, k in sent:
            copy("first", outs, *sems, a, k).start()

    def finish(ins, outs, sems):
        late = [pair for pair in everything if pair not in early]
        land(late)(ins, outs, sems)
        handed(late)(ins, outs, sems)
        for a, k in sent:
            copy("first", outs, *sems, a, k).wait_send()
        for a, k in everything:
            copy("passed", outs, *sems, a, k).wait_send()

    hooks = (tuple(((k, -2), land([(a, k)])) for a, k in early if k not in landed_before)
             + tuple(((k + 1, 0), handed([(a, k)])) for a, k in early))
    return _Rider(list(bufs), _same(bufs), {a: a for a in range(n)},
                  [pltpu.SemaphoreType.DMA((6 * n,)), pltpu.SemaphoreType.DMA((6 * n,))], start, finish, hooks)


SEM_SPEC = pl.BlockSpec(memory_space=pltpu.SEMAPHORE)


def _split_start(arrays, n_copies, make_copies, name):
    m, n = len(arrays), 2 * n_copies

    def body(*refs):
        sems, thru, token = refs[m:m + n], refs[m + n:2 * m + n], refs[2 * m + n]
        for cp in make_copies(thru, sems, False):
            cp.start()
        token[...] = jnp.zeros_like(token)

    outs = pl.pallas_call(
        body, name=name,
        out_shape=[pltpu.SemaphoreType.DMA(())] * n + _same(arrays) + [jax.ShapeDtypeStruct((SUBLANES, LANES), F32)],
        in_specs=[HBM_SPEC] * m, out_specs=[SEM_SPEC] * n + [HBM_SPEC] * m + [pl.BlockSpec(memory_space=pltpu.VMEM)],
        input_output_aliases={a: n + a for a in range(m)},
        compiler_params=pltpu.CompilerParams(has_side_effects=pltpu.SideEffectType.DATAFLOW_SIDE_EFFECTING),
    )(*arrays)
    return list(outs[:n]), list(outs[n:n + m]), outs[n + m]


def _split_wait(sems, arrays, make_copies, after, name):
    m, n = len(arrays), len(sems)

    def body(*refs):
        for cp in make_copies(refs[m + n + len(after):], refs[m:m + n], True):
            cp.wait_send()
            cp.wait_recv()

    return pl.pallas_call(
        body, name=name, out_shape=_same(arrays),
        in_specs=[HBM_SPEC] * m + [SEM_SPEC] * n + [pl.BlockSpec(memory_space=pl.ANY)] * len(after),
        out_specs=[HBM_SPEC] * m, input_output_aliases={a: a for a in range(m)},
        compiler_params=pltpu.CompilerParams(has_side_effects=pltpu.SideEffectType.DATAFLOW_SIDE_EFFECTING),
    )(*arrays, *sems, *after)


def _gather_copies(peers):
    def make(refs, sems, landing):
        x, y, c, chips = _place()
        half = refs[0].shape[1] // 2

        def rows(slot):
            return refs[0].at[slot, pl.ds(pl.multiple_of(c * half, 16), half)]

        me = 2 * x + y
        return [_remote(rows(me), rows(2 * chips[k][0] + chips[k][1] if landing else me), sems[2 * j], sems[2 * j + 1],
                        (*chips[k], c)) for j, k in enumerate(peers)]
    return make


def _scatter_copies(refs, sems, landing):
    x, y, c, chips = _place()
    me = 2 * x + y
    slots = [2 * cx + cy for cx, cy in chips]
    return [_remote(refs[0].at[slots[k]], refs[1].at[slots[k] if landing else me], sems[2 * k], sems[2 * k + 1], (*chips[k], c))
            for k in X_Y_DIAGONAL]


def _scatter_step_copies(step):
    def make(refs, sems, landing):
        x, y, c, _ = _place()
        me = 2 * x + y
        to = (me + 1 + step) % N_CHIPS
        frm = (me + N_CHIPS - 1 - step) % N_CHIPS
        peer = frm if landing else to
        return [_remote(refs[0].at[0], refs[1].at[frm if landing else me], sems[0], sems[1], (peer // 2, peer % 2, c))]
    return make


def _exchange_copies(refs, sems, landing):
    x, y, c, _ = _place()
    return [_remote(refs[0].at[1 - c], refs[1], sems[0], sems[1], (x, y, 1 - c))]


def _joined(makers):
    def make(refs, sems, landing):
        copies, at = [], 0
        for maker, places in makers:
            mine = maker([refs[p] for p in places], sems[at:], landing)
            copies += mine
            at += 2 * len(mine)
        return copies
    return make


def _sibling_exchange(parts):
    n = len(parts)

    def copies(ins, outs, send_sems, recv_sems):
        x, y, c, _ = _place()
        return [_remote(ins[a].at[1 - c], outs[a], send_sems.at[a], recv_sems.at[a], (x, y, 1 - c)) for a in range(n)]

    def start(ins, outs, sems):
        for cp in copies(ins, outs, *sems):
            cp.start()

    def finish(ins, outs, sems):
        for cp in copies(ins, outs, *sems):
            cp.wait_recv()
            cp.wait_send()

    return _Rider(list(parts), [jax.ShapeDtypeStruct(p.shape[1:], p.dtype) for p in parts], {},
                  [pltpu.SemaphoreType.DMA((n,)), pltpu.SemaphoreType.DMA((n,))], start, finish)


def _small_allgather(small):
    flips = [(fx, fy, fc) for fx in range(2) for fy in range(2) for fc in range(2)][1:]

    def copies(small_ref, gather_ref, send_sems, recv_sems, local_sem, started_only=False):
        x, y, c, _ = _place()
        me = 4 * x + 2 * y + c
        peers = [((1 - x) if fx else x, (1 - y) if fy else y, (1 - c) if fc else c) for fx, fy, fc in flips]
        own = pltpu.make_async_copy(small_ref, gather_ref.at[me], local_sem)
        sent = [_remote(small_ref, gather_ref.at[me], send_sems.at[k], recv_sems.at[k], peer) for k, peer in enumerate(peers)]
        if started_only:
            return own, sent
        landed = [_remote(small_ref, gather_ref.at[4 * px + 2 * py + pc], send_sems.at[k], recv_sems.at[k], (px, py, pc))
                  for k, (px, py, pc) in enumerate(peers)]
        return own, sent, landed

    def start(ins, outs, sems):
        own, sent = copies(ins[0], outs[0], *sems, started_only=True)
        own.start()
        for cp in sent:
            cp.start()

    def finish(ins, outs, sems):
        own, sent, landed = copies(ins[0], outs[0], *sems)
        for cp in landed:
            cp.wait_recv()
        for cp in sent:
            cp.wait_send()
        own.wait()

    return _Rider([small], [jax.ShapeDtypeStruct((2 * N_CHIPS,) + small.shape, small.dtype)], {},
                  [pltpu.SemaphoreType.DMA((7,)), pltpu.SemaphoreType.DMA((7,)), pltpu.SemaphoreType.DMA], start, finish)


def _merge(a, b):
    na, nao, nas = len(a.operands), len(a.out_shapes), len(a.scratch)

    def start(ins, outs, sems):
        a.start(ins[:na], outs[:nao], sems[:nas])
        b.start(ins[na:], outs[nao:], sems[nas:])

    def finish(ins, outs, sems):
        a.finish(ins[:na], outs[:nao], sems[:nas])
        b.finish(ins[na:], outs[nao:], sems[nas:])

    def of_a(fn):
        return lambda ins, outs, sems: fn(ins[:na], outs[:nao], sems[:nas])

    def of_b(fn):
        return lambda ins, outs, sems: fn(ins[na:], outs[nao:], sems[nas:])

    aliases = {**a.aliases, **{na + k: nao + v for k, v in b.aliases.items()}}
    hooks = tuple((at, of_a(fn)) for at, fn in a.hooks) + tuple((at, of_b(fn)) for at, fn in b.hooks)
    return _Rider(a.operands + b.operands, a.out_shapes + b.out_shapes, aliases, a.scratch + b.scratch, start, finish, hooks)


def _scatter(sums):
    n = len(sums)

    def copies(ins, outs, send_sems, recv_sems, started_only=False):
        x, y, c, chips = _place()
        me = 2 * x + y
        slots = [2 * cx + cy for cx, cy in chips]
        sent = [_remote(ins[a].at[slots[k]], outs[a].at[me], send_sems.at[3 * a + k], recv_sems.at[3 * a + k], (*chips[k], c))
                for a in range(n) for k in X_Y_DIAGONAL]
        if started_only:
            return sent
        landed = [_remote(ins[a].at[slots[k]], outs[a].at[slots[k]], send_sems.at[3 * a + k], recv_sems.at[3 * a + k],
                          (*chips[k], c)) for a in range(n) for k in X_Y_DIAGONAL]
        return sent, landed

    def start(ins, outs, sems):
        for cp in copies(ins, outs, *sems, started_only=True):
            cp.start()

    def finish(ins, outs, sems):
        sent, landed = copies(ins, outs, *sems)
        for cp in landed:
            cp.wait_recv()
        for cp in sent:
            cp.wait_send()

    return _Rider(list(sums), _same(sums), {}, [pltpu.SemaphoreType.DMA((3 * n,)), pltpu.SemaphoreType.DMA((3 * n,))],
                  start, finish)


def _sibling_share(bufs):
    n = len(bufs)

    def copies(outs, send_sems, recv_sems, started_only=False):
        x, y, c, _ = _place()
        sent = [_remote(outs[a].at[c], outs[a].at[c], send_sems.at[a], recv_sems.at[a], (x, y, 1 - c)) for a in range(n)]
        if started_only:
            return sent
        landed = [_remote(outs[a].at[c], outs[a].at[1 - c], send_sems.at[a], recv_sems.at[a], (x, y, 1 - c)) for a in range(n)]
        return sent, landed

    def start(ins, outs, sems):
        for cp in copies(outs, *sems, started_only=True):
            cp.start()

    def finish(ins, outs, sems):
        sent, landed = copies(outs, *sems)
        for cp in landed:
            cp.wait_recv()
        for cp in sent:
            cp.wait_send()

    return _Rider(list(bufs), _same(bufs), {a: a for a in range(n)},
                  [pltpu.SemaphoreType.DMA((n,)), pltpu.SemaphoreType.DMA((n,))], start, finish)


def _pair_sum(core, part, received, name):
    _, k, rh, cols = part.shape

    def body(core_ref, p_ref, r_ref, o_ref):
        o_ref[...] = (p_ref[0].astype(F32) + r_ref[...].astype(F32)).astype(BF16)

    return pl.pallas_call(
        body, name=name,
        grid_spec=pltpu.PrefetchScalarGridSpec(
            num_scalar_prefetch=1, grid=(k,),
            in_specs=[pl.BlockSpec((1, 1, rh, cols), lambda j, core_ref: (core_ref[0], j, 0, 0)),
                      pl.BlockSpec((1, rh, cols), lambda j, core_ref: (j, 0, 0))],
            out_specs=pl.BlockSpec((1, rh, cols), lambda j, core_ref: (j, 0, 0))),
        out_shape=jax.ShapeDtypeStruct((k, rh, cols), BF16),
        compiler_params=_params(("parallel",), 32),
    )(core, part, received)


def _sum_leading(stack, steps, name):
    k, rows, cols = stack.shape
    tile = rows // steps

    def body(s_ref, o_ref):
        total = s_ref[0].astype(F32)
        for d in range(1, k):
            total = total + s_ref[d].astype(F32)
        o_ref[...] = total

    return pl.pallas_call(
        body, name=name, grid=(steps,),
        in_specs=[pl.BlockSpec((k, tile, cols), lambda i: (0, i, 0))],
        out_specs=pl.BlockSpec((tile, cols), lambda i: (i, 0)),
        out_shape=jax.ShapeDtypeStruct((rows, cols), F32),
        compiler_params=_params(("parallel",), 32),
    )(stack)


def _chip_sum(place, own, stack, steps, name):
    k, rows, cols = stack.shape
    tile = rows // steps

    def body(place_ref, own_ref, *refs):
        chip = place_ref[1]
        total = None
        for d in range(k):
            term = jnp.where(chip == d, own_ref[0], refs[d][0]).astype(F32)
            total = term if total is None else total + term
        refs[k][0] = total

    def other(d):
        return lambda i, place_ref: (jnp.where(place_ref[1] == d, (d + 1) % k, d), i, 0)

    return pl.pallas_call(
        body, name=name,
        grid_spec=pltpu.PrefetchScalarGridSpec(
            num_scalar_prefetch=1, grid=(steps,),
            in_specs=[pl.BlockSpec((1, tile, cols), lambda i, place_ref: (place_ref[1] % own.shape[0], i, 0))]
            + [pl.BlockSpec((1, tile, cols), other(d)) for d in range(k)],
            out_specs=pl.BlockSpec((1, tile, cols), lambda i, place_ref: (place_ref[0], i, 0))),
        out_shape=jax.ShapeDtypeStruct((2, rows, cols), F32),
        compiler_params=_params(("arbitrary",), 32),
    )(place, own, *([stack] * k))


def _adamw(w, g, row0, m, v, tile, name, after=None):
    rows, cols = w.shape
    first = row0 // tile
    assert rows % tile == 0 and row0 % tile == 0
    bc1 = 1.0 - ADAM_B1 ** ADAM_STEP
    bc2 = 1.0 - ADAM_B2 ** ADAM_STEP

    def body(w_ref, g_ref, m_ref, v_ref, *refs):
        go_ref, d_ref, mo_ref, vo_ref = refs[-4:]
        g = g_ref[...]
        m_new = ADAM_B1 * m_ref[...] + (1.0 - ADAM_B1) * g
        v_new = ADAM_B2 * v_ref[...] + (1.0 - ADAM_B2) * (g * g)
        go_ref[...] = g
        d_ref[...] = -ADAM_LR * ((m_new / bc1) / (jnp.sqrt(v_new / bc2) + ADAM_EPS) + ADAM_WD * w_ref[...])
        mo_ref[...] = m_new
        vo_ref[...] = v_new

    later = [] if after is None else [after]
    spec = pl.BlockSpec((tile, cols), lambda i: (i, 0))
    g_spec = pl.BlockSpec((tile, cols), lambda i: (first + i, 0))
    return pl.pallas_call(
        body, name=name, grid=(rows // tile,),
        in_specs=[spec, g_spec, spec, spec] + [pl.BlockSpec(memory_space=pl.ANY)] * len(later), out_specs=[spec] * 4,
        out_shape=[jax.ShapeDtypeStruct((rows, cols), F32)] * 4,
        compiler_params=_params(("parallel",), 32),
    )(w, g, m, v, *later)


SMALL = ("ffn1_norm", "mix_norm", "ffn2_norm", "final_norm", "pool_scale", "sink_logits", "pool_w")


def _pack_small(d, last_row=None):
    def part(n):
        flat = d[n].reshape(-1)
        flat = jnp.pad(flat, (0, -flat.shape[0] % (SUBLANES * LANES)))
        return flat.reshape(-1, LANES)

    last = jnp.zeros((SUBLANES, LANES), F32) if last_row is None else jnp.pad(last_row, ((SUBLANES - 1, 0), (0, 0)))
    packed = jnp.concatenate([part(n) for n in SMALL] + [last], axis=0)
    assert packed.shape[0] == SMALL_ROWS
    return packed


def _unpack_small(packed, like):
    out, row = {}, 0
    for n in SMALL:
        size = math.prod(like[n].shape)
        rows = -(-size // (SUBLANES * LANES)) * SUBLANES
        out[n] = packed[row:row + rows].reshape(-1)[:size].reshape(like[n].shape)
        row += rows
    return out


def kernel(x, ffn1_norm, ffn1_w_gate, ffn1_w_up, ffn1_w_down, mix_norm, w_in, sink_logits, pool_w, pool_scale, w_out, ffn2_norm, ffn2_w_gate, ffn2_w_up, ffn2_w_down, final_norm, loss_target, m_ffn1_norm, m_ffn1_w_gate, m_ffn1_w_up, m_ffn1_w_down, m_mix_norm, m_w_in, m_sink_logits, m_pool_w, m_pool_scale, m_w_out, m_ffn2_norm, m_ffn2_w_gate, m_ffn2_w_up, m_ffn2_w_down, m_final_norm, v_ffn1_norm, v_ffn1_w_gate, v_ffn1_w_up, v_ffn1_w_down, v_mix_norm, v_w_in, v_sink_logits, v_pool_w, v_pool_scale, v_w_out, v_ffn2_norm, v_ffn2_w_gate, v_ffn2_w_up, v_ffn2_w_down, v_final_norm):
    names = ("ffn1_norm", "ffn1_w_gate", "ffn1_w_up", "ffn1_w_down", "mix_norm", "w_in", "sink_logits", "pool_w",
             "pool_scale", "w_out", "ffn2_norm", "ffn2_w_gate", "ffn2_w_up", "ffn2_w_down", "final_norm")
    weights = dict(zip(names, (ffn1_norm, ffn1_w_gate, ffn1_w_up, ffn1_w_down, mix_norm, w_in, sink_logits, pool_w,
                               pool_scale, w_out, ffn2_norm, ffn2_w_gate, ffn2_w_up, ffn2_w_down, final_norm)))
    mom1 = dict(zip(names, (m_ffn1_norm, m_ffn1_w_gate, m_ffn1_w_up, m_ffn1_w_down, m_mix_norm, m_w_in, m_sink_logits,
                            m_pool_w, m_pool_scale, m_w_out, m_ffn2_norm, m_ffn2_w_gate, m_ffn2_w_up, m_ffn2_w_down,
                            m_final_norm)))
    mom2 = dict(zip(names, (v_ffn1_norm, v_ffn1_w_gate, v_ffn1_w_up, v_ffn1_w_down, v_mix_norm, v_w_in, v_sink_logits,
                            v_pool_w, v_pool_scale, v_w_out, v_ffn2_norm, v_ffn2_w_gate, v_ffn2_w_up, v_ffn2_w_down,
                            v_final_norm)))
    chip = (2 * lax.axis_index("x") + lax.axis_index("y")).astype(jnp.int32).reshape(1)
    place = jnp.concatenate([lax.axis_index("c").astype(jnp.int32).reshape(1), chip])

    def rows_of(t, n):
        return jnp.swapaxes(t[n][0], 0, 1) if n in TRANSPOSED else t[n][0]

    bufs = [_pack(chip, [rows_of(weights, n) for n in GROUPS[0]], "pack_ffn1"),
            _pack(chip, [jnp.concatenate([rows_of(weights, n) for n in GROUPS[1]], axis=0)], "pack_mix"),
            _pack(chip, [rows_of(weights, n) for n in GROUPS[2]], "pack_ffn2")]

    small_w = {"ffn1_norm": ffn1_norm, "mix_norm": mix_norm, "ffn2_norm": ffn2_norm,
               "final_norm": final_norm.reshape(1, D_MODEL), "pool_scale": pool_scale, "sink_logits": sink_logits,
               "pool_w": pool_w[0]}
    out_g, out_d, out_m, out_v = {}, {}, {}, {}

    def update(members, g, after=None):
        row0, done = 0, []
        for n in members:
            w = rows_of(weights, n)
            tile = FF_CHUNK // 4 if w.shape[0] == FF_CHUNK else math.gcd(IN_ROWS, OUT_ROWS)
            outs = _adamw(w, g, row0, rows_of(mom1, n), rows_of(mom2, n), tile, "adamw_" + n, after)
            row0 += w.shape[0]
            done.append(outs[0])
            for dst, t in zip((out_g, out_d, out_m, out_v), outs):
                dst[n] = (jnp.swapaxes(t, 0, 1) if n in TRANSPOSED else t).reshape(weights[n].shape)
        return done

    loss, grad_x, small_sum = _step(x[0], loss_target[0], bufs, small_w, place, update)
    small_outs = _adamw(_pack_small(weights), small_sum, 0, _pack_small(mom1), _pack_small(mom2), SMALL_ROWS, "adamw_small")
    for dst, packed in zip((out_g, out_d, out_m, out_v), small_outs):
        dst.update(_unpack_small(packed, weights))

    return (loss,grad_x.reshape(x.shape), *[out_g[n] for n in names], *[out_d[n] for n in names],
            *[out_m[n] for n in names], *[out_v[n] for n in names])
```

```python
import collections
import functools
import math

import jax
import jax.numpy as jnp
from jax import lax
from jax.experimental import pallas as pl
from jax.experimental.pallas import tpu as pltpu

F32, BF16 = jnp.float32, jnp.bfloat16
MESH = pl.DeviceIdType.MESH

D_MODEL = 1024
D_FF = 2816
N_CHIPS = 4
FF_CHUNK = D_FF // N_CHIPS
HEAD_DIM = 64
N_HEADS = 8
N_KV = 2
Q_PER_KV = N_HEADS // N_KV
KV_WIDTH = N_KV * HEAD_DIM
ATTN_WIDTH = N_HEADS * HEAD_DIM
POOL_WINDOWS = (2, 4, 8, 16)
N_POOL = len(POOL_WINDOWS)
POOL_GROUP = 128
POOL_WIDTH = N_POOL * POOL_GROUP
IN_WIDTH = ATTN_WIDTH + 2 * KV_WIDTH + POOL_WIDTH
WINDOW = 128
BLOCK = 128
BAND = 3 * BLOCK
ROPE_THETA = 500000.0
ROTARY_DIM = HEAD_DIM // 4
EPS = 1e-6
LANES = 128
Q_PAD = N_HEADS * LANES
U_PAD = Q_PAD + 2 * KV_WIDTH + POOL_WIDTH
SCALE = HEAD_DIM ** -0.5
NEG = -1e30

ADAM_LR, ADAM_B1, ADAM_B2, ADAM_EPS, ADAM_WD, ADAM_STEP = 0.001, 0.9, 0.999, 1e-08, 0.01, 10

V7X_VMEM_BYTES = 64 * 1024 * 1024
TOK_TILE = 512
SUBLANES = 8
SMALL_ROWS = 568


def _params(sem, vmem_mb):
    assert vmem_mb * 1024 * 1024 <= V7X_VMEM_BYTES
    return pltpu.CompilerParams(dimension_semantics=sem, vmem_limit_bytes=vmem_mb * 1024 * 1024)


def _dot(a, b):
    return lax.dot_general(a, b, (((1,), (0,)), ((), ())), preferred_element_type=F32)


def _dot_nt(a, b):
    return lax.dot_general(a, b, (((1,), (1,)), ((), ())), preferred_element_type=F32)


def _dot_tn(a, b):
    return lax.dot_general(a, b, (((0,), (0,)), ((), ())), preferred_element_type=F32)


def _rms_stats(h):
    r = lax.rsqrt(jnp.mean(h * h, axis=-1, keepdims=True) + EPS)
    return r, h * r


def _rms_bwd(dn, g, r, xh):
    gd = dn * g
    dh = r * (gd - xh * jnp.mean(gd * xh, axis=-1, keepdims=True))
    return dh, jnp.sum(dn * xh, axis=0, keepdims=True)


def _rope(x, c, s1, s2):
    return x * c + pltpu.roll(x, LANES - ROTARY_DIM // 2, 1) * s1 + pltpu.roll(x, ROTARY_DIM // 2, 1) * s2


def _rope_bwd(d, c, s1, s2):
    return d * c + pltpu.roll(d * s1, ROTARY_DIM // 2, 1) + pltpu.roll(d * s2, LANES - ROTARY_DIM // 2, 1)


def _sum_chunks(refs):
    terms = [ref[j].astype(F32) for ref in refs for j in range(ref.shape[0])]
    return functools.reduce(lambda a, b: a + b, terms)


def _chunk_rows(tile, k):
    return pl.BlockSpec((k, tile, D_MODEL), lambda i, *_: (0, i, 0))


def _full(shape):
    nd = len(shape)
    return pl.BlockSpec(shape, lambda *_: (0,) * nd)


def _rows(tile, cols):
    return pl.BlockSpec((tile, cols), lambda i, *_: (i, 0))


HBM_SPEC = pl.BlockSpec(memory_space=pltpu.HBM)

_Rider = collections.namedtuple("_Rider", "operands out_shapes aliases scratch start finish hooks", defaults=[()])


_NO_RIDER = _Rider([], [], {}, [], None, None)


def _call(body, name, grid, in_specs, out_specs, out_shape, scratch, vmem_mb, args, rider=None, prefetch=(),
          shares_rider_refs=False):
    rider = rider or _NO_RIDER
    n_pre, n_in, n_out, n_scr = len(prefetch), len(in_specs), len(out_specs), len(scratch)
    r_in, r_out = len(rider.operands), len(rider.out_shapes)

    def fused(*refs):
        pre, refs = refs[:n_pre], refs[n_pre:]
        ins, refs = refs[:n_in], refs[n_in:]
        r_ins, refs = refs[:r_in], refs[r_in:]
        outs, refs = refs[:n_out], refs[n_out:]
        r_outs, refs = refs[:r_out], refs[r_out:]
        scr, r_scr = refs[:n_scr], refs[n_scr:]
        ids = [pl.program_id(d) for d in range(len(grid))]
        if rider.start is not None:
            @pl.when(functools.reduce(jnp.logical_and, [i == 0 for i in ids]))
            def _():
                rider.start(r_ins, r_outs, r_scr)

        for at, hook in rider.hooks:
            @pl.when(functools.reduce(jnp.logical_and, [i == a % g for i, a, g in zip(ids, at, grid)]))
            def _(hook=hook):
                hook(r_ins, r_outs, r_scr)

        if shares_rider_refs:
            body(*pre, *ins, *outs, *scr, rider_refs=r_outs)
        else:
            body(*pre, *ins, *outs, *scr)

        if rider.finish is not None:
            @pl.when(functools.reduce(jnp.logical_and, [i == g - 1 for i, g in zip(ids, grid)]))
            def _():
                rider.finish(r_ins, r_outs, r_scr)

    return pl.pallas_call(
        fused, name=name,
        grid_spec=pltpu.PrefetchScalarGridSpec(
            num_scalar_prefetch=n_pre, grid=grid,
            in_specs=list(in_specs) + [HBM_SPEC] * r_in, out_specs=list(out_specs) + [HBM_SPEC] * r_out,
            scratch_shapes=list(scratch) + list(rider.scratch)),
        out_shape=list(out_shape) + list(rider.out_shapes),
        input_output_aliases={n_pre + n_in + k: n_out + v for k, v in rider.aliases.items()},
        compiler_params=_params(("arbitrary",) * len(grid), vmem_mb),
    )(*prefetch, *args, *rider.operands)


def _after(token):
    return _Rider([token], [], {}, [], lambda *_: None, lambda *_: None)


def _comm_call(rider, name):
    r_in, r_out = len(rider.operands), len(rider.out_shapes)

    def body(*refs):
        r_ins, r_outs, r_scr = refs[:r_in], refs[r_in:r_in + r_out], refs[r_in + r_out:]
        rider.start(r_ins, r_outs, r_scr)
        rider.finish(r_ins, r_outs, r_scr)

    return pl.pallas_call(
        body, name=name, in_specs=[HBM_SPEC] * r_in, out_specs=[HBM_SPEC] * r_out, out_shape=list(rider.out_shapes),
        input_output_aliases=dict(rider.aliases), scratch_shapes=list(rider.scratch),
    )(*rider.operands)


def _ffn_fwd(order, h, gain, name, rider, group_at, loss_head=None):
    S = h.shape[0]
    tile = min(TOK_TILE, S)
    nt = S // tile
    last = N_CHIPS - 1
    n_extra_in, n_head_out = (2, 4) if loss_head else (0, 1)

    def body(order_ref, h_ref, g_ref, *refs, rider_refs):
        extra_in, refs = refs[:n_extra_in], refs[n_extra_in:]
        head_out, (n_ref, gate_ref, up_ref, w_scr, w_sem, acc, n_scr) = refs[:n_head_out], refs[n_head_out:]
        j, i = pl.program_id(0), pl.program_id(1)

        def fetch(chunk_pass):
            return pltpu.make_async_copy(rider_refs[group_at].at[order_ref[chunk_pass]], w_scr.at[chunk_pass % 2],
                                         w_sem.at[chunk_pass % 2])

        @pl.when(jnp.logical_and(j == 0, i == 0))
        def _():
            fetch(j).start()

        @pl.when(jnp.logical_and(i == nt - 1, j < last))
        def _():
            fetch(j + 1).start()

        @pl.when(i == 0)
        def _():
            fetch(j).wait()

        at = pl.multiple_of(i * tile, tile)

        @pl.when(j == 0)
        def _():
            _, xh = _rms_stats(h_ref[...])
            n = (xh * g_ref[...]).astype(BF16)
            n_scr[pl.ds(at, tile), :] = n
            n_ref[...] = n
            acc[pl.ds(at, tile), :] = jnp.zeros((tile, D_MODEL), F32)

        half = tile // 2
        wg, wu, wd = (w_scr[j % 2, part * FF_CHUNK:(part + 1) * FF_CHUNK, :] for part in range(3))
        ns = [n_scr[pl.ds(at + s * half, half), :] for s in range(2)]
        gates = [_dot_nt(n, wg) for n in ns]
        ups = [_dot_nt(n, wu) for n in ns]
        acts = [(g * jax.nn.sigmoid(g) * u).astype(BF16) for g, u in zip(gates, ups)]
        for s in range(2):
            gate_ref[0, s * half:(s + 1) * half, :] = gates[s].astype(BF16)
            up_ref[0, s * half:(s + 1) * half, :] = ups[s].astype(BF16)
            acc[pl.ds(at + s * half, half), :] += _dot(acts[s], wd)

        @pl.when(j == last)
        def _():
            out = h_ref[...] + 0.5 * acc[pl.ds(at, tile), :]
            if not loss_head:
                head_out[0][...] = out
                return
            (t_ref, gf_ref), (dh_ref, dhalf_ref, loss_ref, dg_ref) = extra_in, head_out

            @pl.when(i == 0)
            def _():
                loss_ref[...] = jnp.zeros_like(loss_ref)
                dg_ref[...] = jnp.zeros_like(dg_ref)

            gf = gf_ref[...]
            r, xh = _rms_stats(out)
            err = xh * gf - t_ref[...]
            loss_ref[...] += (0.5 / D_MODEL) * jnp.sum(err * err, axis=0, keepdims=True)
            dh, dg = _rms_bwd(err * (1.0 / D_MODEL), gf, r, xh)
            dg_ref[...] += dg
            dh_ref[...] = dh
            dhalf_ref[...] = (0.5 * dh).astype(BF16)

    tok = pl.BlockSpec((tile, D_MODEL), lambda j, i, order_ref: (i, 0))
    hid = pl.BlockSpec((1, tile, FF_CHUNK), lambda j, i, order_ref: (order_ref[j], i, 0))
    row = pl.BlockSpec((1, D_MODEL), lambda j, i, order_ref: (0, 0))
    in_last = pl.BlockSpec((tile, D_MODEL), lambda j, i, order_ref: (jnp.where(j == last, i, 0), 0))
    in_first = pl.BlockSpec((tile, D_MODEL), lambda j, i, order_ref: (jnp.where(j == 0, i, nt - 1), 0))
    tok_f32, tok_bf16, lanes = (jax.ShapeDtypeStruct((S, D_MODEL), F32), jax.ShapeDtypeStruct((S, D_MODEL), BF16),
                                jax.ShapeDtypeStruct((1, D_MODEL), F32))
    hidden = jax.ShapeDtypeStruct((N_CHIPS, S, FF_CHUNK), BF16)
    if loss_head:
        extra_specs, extra_args = [in_last, row], list(loss_head)
        head_specs, head_shapes = [in_last, in_last, row, row], [tok_f32, tok_bf16, lanes, lanes]
    else:
        extra_specs, extra_args, head_specs, head_shapes = [], [], [in_last], [tok_f32]
    return _call(
        body, name, (N_CHIPS, nt), [tok, row] + extra_specs, head_specs + [in_first, hid, hid],
        head_shapes + [tok_bf16, hidden, hidden],
        [pltpu.VMEM((2, 3 * FF_CHUNK, D_MODEL), BF16), pltpu.SemaphoreType.DMA((2,)), pltpu.VMEM((S, D_MODEL), F32),
         pltpu.VMEM((S, D_MODEL), BF16)], 62, (h, gain, *extra_args), rider, (order,), shares_rider_refs=True)


def _ffn_bwd(chunks, d_out, n, gate, up, group, name, rider=None):
    S = n.shape[0]
    n_chunks = chunks.shape[0]
    tile = min(TOK_TILE, S)
    nt = S // tile
    half_rows = 3 * FF_CHUNK // 2
    cut = FF_CHUNK // 2

    def body(chunks_ref, do_ref, n_ref, gate_ref, up_ref, wg_ref, wu_ref, wd_ref, dn_ref, dw_ref, acc_g, acc_u, acc_d):
        j, i = pl.program_id(0), pl.program_id(1)

        @pl.when(i == 0)
        def _():
            acc_g[...] = jnp.zeros_like(acc_g)
            acc_u[...] = jnp.zeros_like(acc_u)
            acc_d[...] = jnp.zeros_like(acc_d)

        halves = [pl.ds(s * (tile // 2), tile // 2) for s in range(2)]
        dos = [do_ref[rows, :] for rows in halves]
        d_acts = [_dot_nt(do, wd_ref[0]) for do in dos]
        gs = [gate_ref[0, rows, :].astype(F32) for rows in halves]
        us = [up_ref[0, rows, :].astype(F32) for rows in halves]
        sigs = [jax.nn.sigmoid(g) for g in gs]
        silus = [g * sig for g, sig in zip(gs, sigs)]
        d_ups = [(d_act * silu).astype(BF16) for d_act, silu in zip(d_acts, silus)]
        d_gates = [(d_act * u * (sig * (1.0 + g * (1.0 - sig)))).astype(BF16) for d_act, u, sig, g in zip(d_acts, us, sigs, gs)]
        for rows, d_gate, d_up in zip(halves, d_gates, d_ups):
            dn_ref[0, rows, :] = (_dot(d_gate, wg_ref[0]) + _dot(d_up, wu_ref[0])).astype(BF16)
        d_gate, d_up = jnp.concatenate(d_gates, axis=0), jnp.concatenate(d_ups, axis=0)
        act = jnp.concatenate([(silu * u).astype(BF16) for silu, u in zip(silus, us)], axis=0)
        nn = n_ref[...]
        acc_g[...] += _dot_tn(d_gate, nn)
        acc_u[...] += _dot_tn(d_up, nn)
        acc_d[...] += _dot_tn(act, do_ref[...])

        @pl.when(i == nt - 1)
        def _():
            dw_ref[0, 0, :FF_CHUNK, :] = acc_g[...].astype(BF16)
            dw_ref[0, 0, FF_CHUNK:, :] = acc_u[:cut, :].astype(BF16)
            dw_ref[1, 0, :cut, :] = acc_u[cut:, :].astype(BF16)
            dw_ref[1, 0, cut:, :] = acc_d[...].astype(BF16)

    tok = pl.BlockSpec((tile, D_MODEL), lambda j, i, chunks_ref: (i, 0))
    hid = pl.BlockSpec((1, tile, FF_CHUNK), lambda j, i, chunks_ref: (chunks_ref[j], i, 0))
    return _call(
        body, name, (n_chunks, nt),
        [tok, tok, hid, hid]
        + [pl.BlockSpec((1, FF_CHUNK, D_MODEL), functools.partial(lambda j, i, chunks_ref, part: (chunks_ref[j], part, 0), part=part))
           for part in range(3)],
        [pl.BlockSpec((1, tile, D_MODEL), lambda j, i, chunks_ref: (j, i, 0)),
         pl.BlockSpec((2, 1, half_rows, D_MODEL), lambda j, i, chunks_ref: (0, j, 0, 0))],
        [jax.ShapeDtypeStruct((n_chunks, S, D_MODEL), BF16), jax.ShapeDtypeStruct((2, n_chunks, half_rows, D_MODEL), BF16)],
        [pltpu.VMEM((FF_CHUNK, D_MODEL), F32)] * 3, 56, (d_out, n, gate, up, group, group, group), rider, (chunks,))


def _mix_in(h, gain, w_in, rc, rs1, rs2, name, rider=None):
    S = h.shape[0]
    tile = min(TOK_TILE, S)

    def body(h_ref, g_ref, w_ref, c_ref, s1_ref, s2_ref, n_ref, q_ref, k_ref, v_ref, pc_ref):
        _, xh = _rms_stats(h_ref[...])
        n = (xh * g_ref[...]).astype(BF16)
        n_ref[...] = n
        u = _dot_nt(n, w_ref[...])
        c, s1, s2 = c_ref[...], s1_ref[...], s2_ref[...]
        q_ref[...] = jnp.concatenate([(_rope(u[:, hd * LANES:(hd + 1) * LANES], c, s1, s2) * SCALE).astype(BF16)
                                      for hd in range(N_HEADS)], axis=1)
        k_ref[...] = _rope(u[:, Q_PAD:Q_PAD + KV_WIDTH], c, s1, s2).astype(BF16)
        v_ref[...] = u[:, Q_PAD + KV_WIDTH:Q_PAD + 2 * KV_WIDTH].astype(BF16)
        pc_ref[...] = u[:, Q_PAD + 2 * KV_WIDTH:]

    return _call(
        body, name, (S // tile,),
        [_rows(tile, D_MODEL), _full((1, D_MODEL)), _full((U_PAD, D_MODEL)),
         _rows(tile, LANES), _rows(tile, LANES), _rows(tile, LANES)],
        [_rows(tile, D_MODEL), _rows(tile, Q_PAD), _rows(tile, KV_WIDTH), _rows(tile, KV_WIDTH), _rows(tile, POOL_WIDTH)],
        [jax.ShapeDtypeStruct((S, D_MODEL), BF16), jax.ShapeDtypeStruct((S, Q_PAD), BF16),
         jax.ShapeDtypeStruct((S, KV_WIDTH), BF16), jax.ShapeDtypeStruct((S, KV_WIDTH), BF16),
         jax.ShapeDtypeStruct((S, POOL_WIDTH), F32)],
        [], 40, (h, gain, w_in, rc, rs1, rs2), rider)


def _band_start(i, S):
    return pl.multiple_of(jnp.clip((i - 1) * BLOCK, 0, S - BAND), BLOCK)


def _window_bias(off):
    r = lax.broadcasted_iota(jnp.int32, (BLOCK, 1), 0)
    c = lax.broadcasted_iota(jnp.int32, (1, BAND), 1)
    return jnp.where(jnp.abs(off + r - c) <= WINDOW, 0.0, NEG).astype(F32)


def _softmax_parts(qh, kb, bias, sink_h):
    s = _dot_nt(qh, kb) + bias
    m = jnp.maximum(jnp.max(s, axis=-1, keepdims=True), sink_h)
    p = jnp.exp(s - m)
    es = jnp.exp(sink_h - m)
    return p, es, 1.0 / (jnp.sum(p, axis=-1, keepdims=True) + es)


def _pool_matrix(t0, start, S, w):
    r = lax.broadcasted_iota(jnp.int32, (BLOCK, 1), 0) + t0
    c = lax.broadcasted_iota(jnp.int32, (1, BAND), 1) + start
    half = w // 2

    def window(lo, hi):
        a = jnp.maximum(lo, 0)
        b = jnp.minimum(hi + 1, S)
        return jnp.where((c >= a) & (c < b), 1.0 / (b - a).astype(F32), 0.0)

    return (0.5 * (window(r - half, r + half - 1) + window(r - half + 1, r + half))).astype(BF16)


def _pool_matrices(S):
    blocks = ((0, 0), (BLOCK, 0), (S - BLOCK, S - BAND))
    return jnp.stack([jnp.stack([_pool_matrix(t0, start, S, w) for w in POOL_WINDOWS]) for t0, start in blocks])


def _pool_spec(nb):
    return pl.BlockSpec((1, N_POOL, BLOCK, BAND), lambda i, *_: (jnp.where(i == 0, 0, jnp.where(i == nb - 1, 2, 1)), 0, 0, 0))


def _mix_core_fwd(q, k, v, pc, sink, pool_m, pool_w, pool_scale, name):
    S = q.shape[0]
    nb = S // BLOCK

    def body(sink_ref, q_ref, k_ref, v_ref, pc_ref, pm_ref, pw_ref, ps_ref, a_ref, p_ref):
        i = pl.program_id(0)
        start = _band_start(i, S)
        band = pl.ds(start, BAND)
        bias = _window_bias(i * BLOCK - start)
        kb, vb = k_ref[band, :], v_ref[band, :]
        hs = range(N_HEADS)
        ss = [_dot_nt(q_ref[:, hd * LANES:(hd + 1) * LANES], kb) + bias for hd in hs]
        ms = [jnp.maximum(jnp.max(ss[hd], axis=-1, keepdims=True), sink_ref[0, hd]) for hd in hs]
        ps = [jnp.exp(ss[hd] - ms[hd]) for hd in hs]
        invs = [1.0 / (jnp.sum(ps[hd], axis=-1, keepdims=True) + jnp.exp(sink_ref[0, hd] - ms[hd])) for hd in hs]
        outs = [_dot(ps[hd].astype(BF16), vb) for hd in hs]
        a_ref[...] = jnp.concatenate([(outs[hd] * invs[hd]).astype(BF16) for hd in hs], axis=1)
        centre = pl.ds(pl.multiple_of(i * BLOCK, BLOCK), BLOCK)
        gs = range(N_POOL)
        sl = [slice(g * POOL_GROUP, (g + 1) * POOL_GROUP) for g in gs]
        means = [_dot(pm_ref[0, g], pc_ref[band, sl[g]].astype(BF16)) for g in gs]
        devs = [(means[g] - pc_ref[centre, sl[g]]).astype(BF16) for g in gs]
        p_ref[...] = (jnp.concatenate([_dot(devs[g], pw_ref[g]) for g in gs], axis=1) * ps_ref[...]).astype(BF16)

    return _call(
        body, name, (nb,),
        [pl.BlockSpec(memory_space=pltpu.SMEM), _rows(BLOCK, Q_PAD), _full((S, KV_WIDTH)), _full((S, KV_WIDTH)),
         _full((S, POOL_WIDTH)), _pool_spec(nb), _full((N_POOL, POOL_GROUP, POOL_GROUP)), _full((1, POOL_WIDTH))],
        [_rows(BLOCK, Q_PAD), _rows(BLOCK, POOL_WIDTH)],
        [jax.ShapeDtypeStruct((S, Q_PAD), BF16), jax.ShapeDtypeStruct((S, POOL_WIDTH), BF16)],
        [], 40, (sink, q, k, v, pc, pool_m, pool_w, pool_scale))


def _mix_core_bwd(q, k, v, pc, da, dp, sink, pool_m, pool_w, pool_scale, rc, rs1, rs2, name, rider=None):
    S = q.shape[0]
    nb = S // BLOCK

    def body(sink_ref, q_ref, k_ref, v_ref, pc_ref, da_ref, dp_ref, pm_ref, pw_ref, ps_ref, c_ref, s1_ref, s2_ref,
             dq_ref, dk_ref, dv_ref, dpc_ref, dsink_ref, dpw_ref, dps_ref):
        i = pl.program_id(0)

        @pl.when(i == 0)
        def _():
            dk_ref[...] = jnp.zeros_like(dk_ref)
            dv_ref[...] = jnp.zeros_like(dv_ref)
            dpc_ref[...] = jnp.zeros_like(dpc_ref)
            dsink_ref[...] = jnp.zeros_like(dsink_ref)
            dpw_ref[...] = jnp.zeros_like(dpw_ref)
            dps_ref[...] = jnp.zeros_like(dps_ref)

        start = _band_start(i, S)
        band = pl.ds(start, BAND)
        bias = _window_bias(i * BLOCK - start)
        kb, vb = k_ref[band, :], v_ref[band, :]
        c, s1, s2 = c_ref[...], s1_ref[...], s2_ref[...]
        lane = lax.broadcasted_iota(jnp.int32, (1, LANES), 1)
        hs = range(N_HEADS)
        qs = [q_ref[:, hd * LANES:(hd + 1) * LANES] for hd in hs]
        das = [da_ref[:, hd * LANES:(hd + 1) * LANES] for hd in hs]
        ss = [_dot_nt(qs[hd], kb) + bias for hd in hs]
        d_probs = [_dot_nt(das[hd], vb) for hd in hs]
        ms = [jnp.maximum(jnp.max(ss[hd], axis=-1, keepdims=True), sink_ref[0, hd]) for hd in hs]
        ps = [jnp.exp(ss[hd] - ms[hd]) for hd in hs]
        ess = [jnp.exp(sink_ref[0, hd] - ms[hd]) for hd in hs]
        invs = [1.0 / (jnp.sum(ps[hd], axis=-1, keepdims=True) + ess[hd]) for hd in hs]
        probs = [ps[hd] * invs[hd] for hd in hs]
        deltas = [jnp.sum(probs[hd] * d_probs[hd], axis=-1, keepdims=True) for hd in hs]
        d_ss = [(probs[hd] * (d_probs[hd] - deltas[hd])).astype(BF16) for hd in hs]
        dqs = [_dot(d_ss[hd], kb) for hd in hs]
        dq_ref[...] = jnp.concatenate([_rope_bwd(dqs[hd] * SCALE, c, s1, s2).astype(BF16) for hd in hs], axis=1)
        dks = [_dot_tn(d_ss[hd], qs[hd]) for hd in hs]
        dvs = [_dot_tn(probs[hd].astype(BF16), das[hd]) for hd in hs]
        dk_ref[band, :] += functools.reduce(lambda a, b: a + b, dks)
        dv_ref[band, :] += functools.reduce(lambda a, b: a + b, dvs)
        dsink_ref[...] += functools.reduce(lambda a, b: a + b, [
            jnp.where(lane == hd, -jnp.sum(ess[hd] * invs[hd] * deltas[hd], axis=0, keepdims=True), 0.0) for hd in hs])

        centre = pl.ds(pl.multiple_of(i * BLOCK, BLOCK), BLOCK)
        gs = range(N_POOL)
        sl = [slice(g * POOL_GROUP, (g + 1) * POOL_GROUP) for g in gs]
        devs = [(_dot(pm_ref[0, g], pc_ref[band, sl[g]].astype(BF16)) - pc_ref[centre, sl[g]]).astype(BF16) for g in gs]
        dys = [dp_ref[:, sl[g]].astype(F32) for g in gs]
        zs = [_dot(devs[g], pw_ref[g]) for g in gs]
        dzs = [(dys[g] * ps_ref[:, sl[g]]).astype(BF16) for g in gs]
        d_devs = [_dot_nt(dzs[g], pw_ref[g]) for g in gs]
        dps_ref[...] += jnp.concatenate([jnp.sum(dys[g] * zs[g], axis=0, keepdims=True) for g in gs], axis=1)
        for g in gs:
            dpw_ref[g] += _dot_tn(devs[g], dzs[g])
        dpc_ref[band, :] += jnp.concatenate([_dot_tn(pm_ref[0, g], d_devs[g].astype(BF16)) for g in gs], axis=1)
        dpc_ref[centre, :] -= jnp.concatenate(d_devs, axis=1)

    return _call(
        body, name, (nb,),
        [pl.BlockSpec(memory_space=pltpu.SMEM), _rows(BLOCK, Q_PAD), _full((S, KV_WIDTH)), _full((S, KV_WIDTH)),
         _full((S, POOL_WIDTH)), _rows(BLOCK, Q_PAD), _rows(BLOCK, POOL_WIDTH), _pool_spec(nb),
         _full((N_POOL, POOL_GROUP, POOL_GROUP)), _full((1, POOL_WIDTH)),
         _rows(BLOCK, LANES), _rows(BLOCK, LANES), _rows(BLOCK, LANES)],
        [_rows(BLOCK, Q_PAD), _full((S, KV_WIDTH)), _full((S, KV_WIDTH)), _full((S, POOL_WIDTH)),
         _full((1, LANES)), _full((N_POOL, POOL_GROUP, POOL_GROUP)), _full((1, POOL_WIDTH))],
        [jax.ShapeDtypeStruct((S, Q_PAD), BF16), jax.ShapeDtypeStruct((S, KV_WIDTH), F32),
         jax.ShapeDtypeStruct((S, KV_WIDTH), F32), jax.ShapeDtypeStruct((S, POOL_WIDTH), F32),
         jax.ShapeDtypeStruct((1, LANES), F32), jax.ShapeDtypeStruct((N_POOL, POOL_GROUP, POOL_GROUP), F32),
         jax.ShapeDtypeStruct((1, POOL_WIDTH), F32)],
        [], 56, (sink, q, k, v, pc, da, dp, pool_m, pool_w, pool_scale, rc, rs1, rs2), rider)


def _mix_out(h, a, p, wa, wp, name):
    S = h.shape[0]
    tile = min(TOK_TILE, S)

    def body(h_ref, a_ref, p_ref, wa_ref, wp_ref, o_ref):
        o_ref[...] = h_ref[...] + _dot(a_ref[...], wa_ref[...]) + _dot(p_ref[...], wp_ref[...])

    return pl.pallas_call(
        body, name=name, grid=(S // tile,),
        in_specs=[_rows(tile, D_MODEL), _rows(tile, Q_PAD), _rows(tile, POOL_WIDTH),
                  _full((Q_PAD, D_MODEL)), _full((POOL_WIDTH, D_MODEL))],
        out_specs=_rows(tile, D_MODEL),
        out_shape=jax.ShapeDtypeStruct((S, D_MODEL), F32),
        compiler_params=_params(("parallel",), 40),
    )(h, a, p, wa, wp)


def _mix_out_bwd(dh_out, dn, h, gain, a, p, wa, wp, name):
    S = h.shape[0]
    tile = min(TOK_TILE, S)

    def body(do_ref, dn_ref, h_ref, g_ref, a_ref, p_ref, wa_ref, wp_ref, dh_ref, da_ref, dp_ref, dwa_ref, dwp_ref, dg_ref):
        @pl.when(pl.program_id(0) == 0)
        def _():
            dwa_ref[...] = jnp.zeros_like(dwa_ref)
            dwp_ref[...] = jnp.zeros_like(dwp_ref)
            dg_ref[...] = jnp.zeros_like(dg_ref)

        r, xh = _rms_stats(h_ref[...])
        dnorm, dg = _rms_bwd(_sum_chunks([dn_ref]), g_ref[...], r, xh)
        dh = do_ref[...] + dnorm
        dg_ref[...] += dg
        dh_ref[...] = dh
        dhb = dh.astype(BF16)
        da_ref[...] = _dot_nt(dhb, wa_ref[...]).astype(BF16)
        dp_ref[...] = _dot_nt(dhb, wp_ref[...]).astype(BF16)
        dwa_ref[...] += _dot_tn(a_ref[...], dhb)
        dwp_ref[...] += _dot_tn(p_ref[...], dhb)

    return _call(
        body, name, (S // tile,),
        [_rows(tile, D_MODEL), _chunk_rows(tile, dn.shape[0]), _rows(tile, D_MODEL), _full((1, D_MODEL)),
         _rows(tile, Q_PAD), _rows(tile, POOL_WIDTH), _full((Q_PAD, D_MODEL)), _full((POOL_WIDTH, D_MODEL))],
        [_rows(tile, D_MODEL), _rows(tile, Q_PAD), _rows(tile, POOL_WIDTH),
         _full((Q_PAD, D_MODEL)), _full((POOL_WIDTH, D_MODEL)), _full((1, D_MODEL))],
        [jax.ShapeDtypeStruct((S, D_MODEL), F32), jax.ShapeDtypeStruct((S, Q_PAD), BF16),
         jax.ShapeDtypeStruct((S, POOL_WIDTH), BF16), jax.ShapeDtypeStruct((Q_PAD, D_MODEL), F32),
         jax.ShapeDtypeStruct((POOL_WIDTH, D_MODEL), F32), jax.ShapeDtypeStruct((1, D_MODEL), F32)],
        [], 48, (dh_out, dn, h, gain, a, p, wa, wp))


def _mix_in_bwd(dh_out, h, gain, n, dq, dk, dv, dpc, rc, rs1, rs2, w_in, name, rider=None):
    S = h.shape[0]
    tile = min(TOK_TILE, S)

    def body(do_ref, h_ref, g_ref, n_ref, dq_ref, dk_ref, dv_ref, dpc_ref, c_ref, s1_ref, s2_ref, w_ref,
             dh_ref, dhalf_ref, dw_ref, dg_ref):
        @pl.when(pl.program_id(0) == 0)
        def _():
            dw_ref[...] = jnp.zeros_like(dw_ref)
            dg_ref[...] = jnp.zeros_like(dg_ref)

        dk = _rope_bwd(dk_ref[...], c_ref[...], s1_ref[...], s2_ref[...]).astype(BF16)
        du = jnp.concatenate([dq_ref[...], dk, dv_ref[...].astype(BF16), dpc_ref[...].astype(BF16)], axis=1)
        dn = _dot(du, w_ref[...])
        dw_ref[...] += _dot_tn(du, n_ref[...])
        r, xh = _rms_stats(h_ref[...])
        dnorm, dg = _rms_bwd(dn, g_ref[...], r, xh)
        dh = do_ref[...] + dnorm
        dg_ref[...] += dg
        dh_ref[...] = dh
        dhalf_ref[...] = (0.5 * dh).astype(BF16)

    return _call(
        body, name, (S // tile,),
        [_rows(tile, D_MODEL), _rows(tile, D_MODEL), _full((1, D_MODEL)), _rows(tile, D_MODEL),
         _rows(tile, Q_PAD), _rows(tile, KV_WIDTH), _rows(tile, KV_WIDTH), _rows(tile, POOL_WIDTH),
         _rows(tile, LANES), _rows(tile, LANES), _rows(tile, LANES), _full((U_PAD, D_MODEL))],
        [_rows(tile, D_MODEL), _rows(tile, D_MODEL), _full((U_PAD, D_MODEL)), _full((1, D_MODEL))],
        [jax.ShapeDtypeStruct((S, D_MODEL), F32), jax.ShapeDtypeStruct((S, D_MODEL), BF16),
         jax.ShapeDtypeStruct((U_PAD, D_MODEL), F32), jax.ShapeDtypeStruct((1, D_MODEL), F32)],
        [], 56, (dh_out, h, gain, n, dq, dk, dv, dpc, rc, rs1, rs2, w_in), rider)


def _norm_bwd(dh_out, dns, h, gain, name, rider=None):
    S = h.shape[0]
    tile = min(TOK_TILE, S)
    n = len(dns)

    def body(do_ref, *refs):
        h_ref, g_ref, dh_ref, dg_ref = refs[n:]

        @pl.when(pl.program_id(0) == 0)
        def _():
            dg_ref[...] = jnp.zeros_like(dg_ref)

        r, xh = _rms_stats(h_ref[...])
        dnorm, dg = _rms_bwd(_sum_chunks(refs[:n]), g_ref[...], r, xh)
        dg_ref[...] += dg
        dh_ref[...] = do_ref[...] + dnorm

    return _call(
        body, name, (S // tile,),
        [_rows(tile, D_MODEL)] + [_chunk_rows(tile, dn.shape[0]) for dn in dns] + [_rows(tile, D_MODEL), _full((1, D_MODEL))],
        [_rows(tile, D_MODEL), _full((1, D_MODEL))],
        [jax.ShapeDtypeStruct((S, D_MODEL), F32), jax.ShapeDtypeStruct((1, D_MODEL), F32)],
        [], 40, (dh_out, *dns, h, gain), rider)


def _rope_tables(S):
    half = ROTARY_DIM // 2
    inv_freq = ROPE_THETA ** (-jnp.arange(0, ROTARY_DIM, 2, dtype=F32) / ROTARY_DIM)
    dim = jnp.arange(LANES) % HEAD_DIM
    ang = jnp.arange(S, dtype=F32)[:, None] * inv_freq[dim % half][None, :]
    lo, hi = (dim < half)[None, :], ((dim >= half) & (dim < ROTARY_DIM))[None, :]
    c = jnp.where(lo | hi, jnp.cos(ang), 1.0)
    s1 = jnp.where(lo, -jnp.sin(ang), 0.0)
    s2 = jnp.where(hi, jnp.sin(ang), 0.0)
    return c, s1, s2


def _pad_heads(w, axis):
    w = jnp.moveaxis(w, axis, 0)
    heads = w.reshape((N_HEADS, HEAD_DIM) + w.shape[1:])
    zero = jnp.zeros_like(heads)
    first = (jnp.arange(N_HEADS) < Q_PER_KV).reshape((N_HEADS, 1) + (1,) * (w.ndim - 1))
    lo = jnp.where(first, heads, zero)
    hi = jnp.where(first, zero, heads)
    padded = jnp.concatenate([lo, hi], axis=1).reshape((Q_PAD,) + w.shape[1:])
    return jnp.moveaxis(padded, 0, axis)


def _unpad_heads(w, axis):
    w = jnp.moveaxis(w, axis, 0)
    groups = w.reshape((N_HEADS, 2, HEAD_DIM) + w.shape[1:])
    first = (jnp.arange(N_HEADS) < Q_PER_KV).reshape((N_HEADS, 1) + (1,) * (w.ndim - 1))
    heads = jnp.where(first, groups[:, 0], groups[:, 1]).reshape((ATTN_WIDTH,) + w.shape[1:])
    return jnp.moveaxis(heads, 0, axis)


IN_ROWS = IN_WIDTH // N_CHIPS
OUT_ROWS = (ATTN_WIDTH + POOL_WIDTH) // N_CHIPS
MIX_ROWS = IN_ROWS + OUT_ROWS
FFN_ROWS = 3 * FF_CHUNK


def _step(x, target, bufs, small, place, update):
    S = x.shape[0]
    rc, rs1, rs2 = _rope_tables(S)
    mine = place[1]
    order = jnp.stack([mine, mine ^ 2, mine ^ 1, mine ^ 3])
    h1, n1, gate1, up1, ffn1, mix = _ffn_fwd(order, x, small["ffn1_norm"], "ffn1_fwd",
                                             _merge(_allgather(bufs[:1], in_passes=0), _allgather(bufs[1:2])), 0)
    in_flight, (ffn2,), token = _split_start(bufs[2:], 2, _gather_copies(X_Y_DIAGONAL[:2]), "ffn2_gather_start")
    w_in_t = mix[:, :IN_ROWS].reshape(IN_WIDTH, D_MODEL)
    w_in_pad = jnp.concatenate([_pad_heads(w_in_t[:ATTN_WIDTH], 0), w_in_t[ATTN_WIDTH:]], axis=0)
    w_out = mix[:, IN_ROWS:].reshape(ATTN_WIDTH + POOL_WIDTH, D_MODEL)
    wa = _pad_heads(w_out[:ATTN_WIDTH], 0)
    wp = w_out[ATTN_WIDTH:]
    pool_w = small["pool_w"].astype(BF16)

    n2, q, k, v, pc = _mix_in(h1, small["mix_norm"], w_in_pad, rc, rs1, rs2, "mix_in", _after(token))
    pool_m = _pool_matrices(S)
    a, p = _mix_core_fwd(q, k, v, pc, small["sink_logits"], pool_m, pool_w, small["pool_scale"], "mix_core_fwd")
    h2 = _mix_out(h1, a, p, wa, wp, "mix_out")
    (ffn2,) = _split_wait(in_flight, [ffn2], _gather_copies(X_Y_DIAGONAL[:2]), [h2], "ffn2_gather_wait")
    dh3, dhalf3, loss_lanes, d_final, n3, gate2, up2, ffn2 = _ffn_fwd(
        order, h2, small["ffn2_norm"], "ffn2_fwd",
        _allgather([ffn2], peers=X_Y_DIAGONAL[2:], in_passes=0, landed_before=X_Y_DIAGONAL[:2]), 0,
        loss_head=(target, small["final_norm"]))

    dn3, d_ffn2 = _ffn_bwd(jnp.arange(N_CHIPS, dtype=jnp.int32), dhalf3, n3, gate2, up2, ffn2, "ffn2_bwd")
    dh2, da, dp, dwa, dwp, d_ffn2_norm = _mix_out_bwd(dh3, dn3, h2, small["ffn2_norm"], a, p, wa, wp, "mix_out_bwd")
    dq, dk, dv, dpc, dsink, dpool_w, dpool_scale, received = _mix_core_bwd(
        q, k, v, pc, da, dp, small["sink_logits"], pool_m, pool_w, small["pool_scale"], rc, rs1, rs2, "mix_core_bwd",
        _sibling_exchange([d_ffn2]))
    pair = _pair_sum(place, d_ffn2, received, "grad_pair_sum_ffn2")
    in_flight, (pair, stack), token = _split_start([pair, received], 3, _scatter_copies, "ffn2_scatter_start")
    dh1, dhalf1, dw_in_pad, d_mix_norm = _mix_in_bwd(dh2, h1, small["mix_norm"], n2, dq, dk, dv, dpc, rc, rs1, rs2, w_in_pad,
                                                     "mix_in_bwd", _after(token))
    dw_in_t = jnp.concatenate([_unpad_heads(dw_in_pad[:Q_PAD], 0), dw_in_pad[Q_PAD:]], axis=0)
    dw_out = jnp.concatenate([_unpad_heads(dwa, 0), dwp], axis=0)
    d_mix = jnp.concatenate([dw_in_t.reshape(N_CHIPS, IN_ROWS, D_MODEL), dw_out.reshape(N_CHIPS, OUT_ROWS, D_MODEL)], axis=1)
    d_mix = jnp.transpose(d_mix.reshape(N_CHIPS, 2, MIX_ROWS // 2, D_MODEL), (1, 0, 2, 3)).astype(BF16)
    small_g = {"ffn1_norm": jnp.zeros_like(d_mix_norm), "mix_norm": d_mix_norm, "ffn2_norm": d_ffn2_norm,
               "final_norm": d_final, "pool_scale": dpool_scale, "sink_logits": dsink[:, :N_HEADS], "pool_w": dpool_w}
    loss_row = jnp.sum(loss_lanes.reshape(D_MODEL // LANES, LANES), axis=0, keepdims=True)
    small_early = _pack_small(small_g, loss_row)

    chunk = [(place[1:] + 1 + p) % N_CHIPS for p in range(N_CHIPS)]
    ffn1_bwd = functools.partial(_ffn_bwd, d_out=dhalf1, n=n1, gate=gate1, up=up1, group=ffn1)
    dn_a, dw_a, recv_mix, small_all = ffn1_bwd(
        chunk[0], name="ffn1_bwd_0", rider=_merge(_sibling_exchange([d_mix]), _small_allgather(small_early)))
    pair, stack = _split_wait(in_flight, [pair, stack], _scatter_copies, [dn_a], "ffn2_scatter_wait")
    reduced_ffn2 = _chip_sum(place, pair, stack, 2, "grad_chip_sum_ffn2")
    pair_mix = _pair_sum(place, d_mix, recv_mix, "grad_pair_sum_mix")
    dn_b, dw_b, recv_a, stack_mix, g_ffn2 = ffn1_bwd(
        chunk[1], name="ffn1_bwd_1",
        rider=_merge(_merge(_sibling_exchange([dw_a]), _scatter([pair_mix])), _sibling_share([reduced_ffn2])))
    pair_a = _pair_sum(place, dw_a, recv_a, "grad_pair_sum_ffn1_0")
    reduced_mix = _chip_sum(place, pair_mix, stack_mix, 2, "grad_chip_sum_mix")
    steps = [(_scatter_step_copies(p), (p, 3)) for p in range(3)]
    sems_a, (pair_a, stack), token = _split_start([pair_a, stack], 1, steps[0][0], "ffn1_scatter_start_0")
    dn_c, dw_c, recv_b, g_mix = ffn1_bwd(
        chunk[2], name="ffn1_bwd_2", rider=_merge(_merge(_sibling_exchange([dw_b]), _sibling_share([reduced_mix])), _after(token)))
    pair_b = _pair_sum(place, dw_b, recv_b, "grad_pair_sum_ffn1_1")
    sems_b, (pair_b, stack), token = _split_start([pair_b, stack], 1, steps[1][0], "ffn1_scatter_start_1")
    dn_d, dw_d, recv_c = ffn1_bwd(chunk[3], name="ffn1_bwd_3", rider=_merge(_sibling_exchange([dw_c]), _after(token)))
    pair_c = _pair_sum(place, dw_c, recv_c, "grad_pair_sum_ffn1_2")
    own = (_exchange_copies, (4, 5))
    sems_c, (pair_c, stack, dw_d, recv_d), token = _split_start(
        [pair_c, stack, dw_d, recv_c], 2, _joined([(steps[2][0], (0, 1)), (own[0], (2, 3))]), "ffn1_scatter_start_2")
    grad_x, d_ffn1_norm = _norm_bwd(dh1, [dn_a, dn_b, dn_c, dn_d], x, small["ffn1_norm"], "norm1_bwd", _after(token))
    updated = (update(GROUPS[2], g_ffn2.reshape(-1, D_MODEL), d_ffn1_norm)
               + update(GROUPS[1], g_mix.reshape(-1, D_MODEL), d_ffn1_norm))
    pair_a, pair_b, pair_c, stack, dw_d, recv_d = _split_wait(
        sems_a + sems_b + sems_c, [pair_a, pair_b, pair_c, stack, dw_d, recv_d], _joined(steps + [own]), updated,
        "ffn1_scatter_wait")
    pair_d = _pair_sum(place, dw_d, recv_d, "grad_pair_sum_ffn1_3")
    reduced_ffn1 = _chip_sum(place, pair_d, stack, 2, "grad_chip_sum_ffn1")
    g_ffn1, gains = _comm_call(_merge(_sibling_share([reduced_ffn1]), _small_allgather(d_ffn1_norm.reshape(-1, LANES))),
                               "grad_share_tail")
    gain_sum = _sum_leading(gains, 1, "gain_grad_sum")
    small_sum = jnp.concatenate([gain_sum, _sum_leading(small_all, 1, "small_grad_sum")[gain_sum.shape[0]:]], axis=0)
    update(GROUPS[0], g_ffn1.reshape(-1, D_MODEL))
    return jnp.sum(small_sum[SMALL_ROWS - 1]), grad_x, small_sum


GROUPS =(("ffn1_w_gate", "ffn1_w_up", "ffn1_w_down"), ("w_in", "w_out"), ("ffn2_w_gate", "ffn2_w_up", "ffn2_w_down"))
TRANSPOSED = ("ffn1_w_gate", "ffn1_w_up", "w_in", "ffn2_w_gate", "ffn2_w_up")


def _place():
    x, y, c = lax.axis_index("x"), lax.axis_index("y"), lax.axis_index("c")
    chips = [(1 - x, y), (x, 1 - y), (1 - x, 1 - y)]
    return x, y, c, chips


def _remote(src, dst, send_sem, recv_sem, to):
    return pltpu.make_async_remote_copy(src_ref=src, dst_ref=dst, send_sem=send_sem, recv_sem=recv_sem,
                                        device_id=to, device_id_type=MESH)


def _pack(chip, members, name):
    rows = members[0].shape[0]
    n = len(members)

    def body(chip_ref, *refs):
        ins, out_ref, buf, sems = refs[:n], refs[n], refs[n + 1], refs[n + 2]
        copies = [pltpu.make_async_copy(ins[k], buf.at[k], sems.at[k]) for k in range(n)]
        for cp in copies:
            cp.start()
        for k in range(n):
            copies[k].wait()
            out_ref[0, k * rows:(k + 1) * rows, :] = buf[k].astype(BF16)

    return pl.pallas_call(
        body, name=name,
        grid_spec=pltpu.PrefetchScalarGridSpec(
            num_scalar_prefetch=1, grid=(1,),
            in_specs=[HBM_SPEC] * n,
            out_specs=pl.BlockSpec((1, n * rows, D_MODEL), lambda k, chip_ref: (chip_ref[0], 0, 0)),
            scratch_shapes=[pltpu.VMEM((n, rows, D_MODEL), F32), pltpu.SemaphoreType.DMA((n,))]),
        out_shape=jax.ShapeDtypeStruct((N_CHIPS, n * rows, D_MODEL), BF16),
        compiler_params=_params(("arbitrary",), 40),
    )(chip, *members)


def _same(arrays):
    return [jax.ShapeDtypeStruct(a.shape, a.dtype) for a in arrays]


X_Y_DIAGONAL = (0, 1, 2)


def _allgather(bufs, peers=X_Y_DIAGONAL, in_passes=None, landed_before=()):
    n = len(bufs)

    def copy(kind, outs, send_sems, recv_sems, a, k):
        x, y, c, chips = _place()
        half = bufs[a].shape[1] // 2

        def rows(slot, core):
            return outs[a].at[slot, pl.ds(pl.multiple_of(core * half, 16), half)]

        me, slot = 2 * x + y, 2 * chips[k][0] + chips[k][1]
        over_ici = (send_sems.at[6 * a + k], recv_sems.at[6 * a + k])
        over_d2d = (send_sems.at[6 * a + 3 + k], recv_sems.at[6 * a + 3 + k])
        if kind == "first":
            return _remote(rows(me, c), rows(me, c), *over_ici, (*chips[k], c))
        if kind == "landed":
            return _remote(rows(me, c), rows(slot, c), *over_ici, (*chips[k], c))
        if kind == "passed":
            return _remote(rows(slot, c), rows(slot, c), *over_d2d, (x, y, 1 - c))
        return _remote(rows(me, c), rows(slot, 1 - c), *over_d2d, (x, y, 1 - c))

    def land(pairs):
        def hook(ins, outs, sems):
            for a, k in pairs:
                if k not in landed_before:
                    copy("landed", outs, *sems, a, k).wait_recv()
                    copy("passed", outs, *sems, a, k).start()
        return hook

    def handed(pairs):
        def hook(ins, outs, sems):
            for a, k in pairs:
                copy("handed", outs, *sems, a, k).wait_recv()
        return hook

    sent = [(a, k) for a in range(n) for k in peers]
    everything = sent + [(in_passes, k) for k in landed_before]
    early = [(a, k) for a, k in everything if a == in_passes]

    def start(ins, outs, sems):
        for k in landed_before:
            copy("passed", outs, *sems, in_passes, k).start()
        for a, k in sent:
            copy("first", outs, *sems, a, k).start()

    def finish(ins, outs, sems):
        late = [pair for pair in everything if pair not in early]
        land(late)(ins, outs, sems)
        handed(late)(ins, outs, sems)
        for a, k in sent:
            copy("first", outs, *sems, a, k).wait_send()
        for a, k in everything:
            copy("passed", outs, *sems, a, k).wait_send()

    hooks = (tuple(((k, -3), land([(a, k)])) for a, k in early if k not in landed_before)
             + tuple(((k, -1), handed([(a, k)])) for a, k in early))
    return _Rider(list(bufs), _same(bufs), {a: a for a in range(n)},
                  [pltpu.SemaphoreType.DMA((6 * n,)), pltpu.SemaphoreType.DMA((6 * n,))], start, finish, hooks)


SEM_SPEC = pl.BlockSpec(memory_space=pltpu.SEMAPHORE)


def _split_start(arrays, n_copies, make_copies, name):
    m, n = len(arrays), 2 * n_copies

    def body(*refs):
        sems, thru, token = refs[m:m + n], refs[m + n:2 * m + n], refs[2 * m + n]
        for cp in make_copies(thru, sems, False):
            cp.start()
        token[...] = jnp.zeros_like(token)

    outs = pl.pallas_call(
        body, name=name,
        out_shape=[pltpu.SemaphoreType.DMA(())] * n + _same(arrays) + [jax.ShapeDtypeStruct((SUBLANES, LANES), F32)],
        in_specs=[HBM_SPEC] * m, out_specs=[SEM_SPEC] * n + [HBM_SPEC] * m + [pl.BlockSpec(memory_space=pltpu.VMEM)],
        input_output_aliases={a: n + a for a in range(m)},
        compiler_params=pltpu.CompilerParams(has_side_effects=pltpu.SideEffectType.DATAFLOW_SIDE_EFFECTING),
    )(*arrays)
    return list(outs[:n]), list(outs[n:n + m]), outs[n + m]


def _split_wait(sems, arrays, make_copies, after, name):
    m, n = len(arrays), len(sems)

    def body(*refs):
        for cp in make_copies(refs[m + n + len(after):], refs[m:m + n], True):
            cp.wait_send()
            cp.wait_recv()

    return pl.pallas_call(
        body, name=name, out_shape=_same(arrays),
        in_specs=[HBM_SPEC] * m + [SEM_SPEC] * n + [pl.BlockSpec(memory_space=pl.ANY)] * len(after),
        out_specs=[HBM_SPEC] * m, input_output_aliases={a: a for a in range(m)},
        compiler_params=pltpu.CompilerParams(has_side_effects=pltpu.SideEffectType.DATAFLOW_SIDE_EFFECTING),
    )(*arrays, *sems, *after)


def _gather_copies(peers):
    def make(refs, sems, landing):
        x, y, c, chips = _place()
        half = refs[0].shape[1] // 2

        def rows(slot):
            return refs[0].at[slot, pl.ds(pl.multiple_of(c * half, 16), half)]

        me = 2 * x + y
        return [_remote(rows(me), rows(2 * chips[k][0] + chips[k][1] if landing else me), sems[2 * j], sems[2 * j + 1],
                        (*chips[k], c)) for j, k in enumerate(peers)]
    return make


def _scatter_copies(refs, sems, landing):
    x, y, c, chips = _place()
    me = 2 * x + y
    slots = [2 * cx + cy for cx, cy in chips]
    return [_remote(refs[0].at[slots[k]], refs[1].at[slots[k] if landing else me], sems[2 * k], sems[2 * k + 1], (*chips[k], c))
            for k in X_Y_DIAGONAL]


def _scatter_step_copies(step):
    def make(refs, sems, landing):
        x, y, c, _ = _place()
        me = 2 * x + y
        to = (me + 1 + step) % N_CHIPS
        frm = (me + N_CHIPS - 1 - step) % N_CHIPS
        peer = frm if landing else to
        return [_remote(refs[0].at[0], refs[1].at[frm if landing else me], sems[0], sems[1], (peer // 2, peer % 2, c))]
    return make


def _exchange_copies(refs, sems, landing):
    x, y, c, _ = _place()
    return [_remote(refs[0].at[1 - c], refs[1], sems[0], sems[1], (x, y, 1 - c))]


def _joined(makers):
    def make(refs, sems, landing):
        copies, at = [], 0
        for maker, places in makers:
            mine = maker([refs[p] for p in places], sems[at:], landing)
            copies += mine
            at += 2 * len(mine)
        return copies
    return make


def _sibling_exchange(parts):
    n = len(parts)

    def copies(ins, outs, send_sems, recv_sems):
        x, y, c, _ = _place()
        return [_remote(ins[a].at[1 - c], outs[a], send_sems.at[a], recv_sems.at[a], (x, y, 1 - c)) for a in range(n)]

    def start(ins, outs, sems):
        for cp in copies(ins, outs, *sems):
            cp.start()

    def finish(ins, outs, sems):
        for cp in copies(ins, outs, *sems):
            cp.wait_recv()
            cp.wait_send()

    return _Rider(list(parts), [jax.ShapeDtypeStruct(p.shape[1:], p.dtype) for p in parts], {},
                  [pltpu.SemaphoreType.DMA((n,)), pltpu.SemaphoreType.DMA((n,))], start, finish)


def _small_allgather(small):
    flips = [(fx, fy, fc) for fx in range(2) for fy in range(2) for fc in range(2)][1:]

    def copies(small_ref, gather_ref, send_sems, recv_sems, local_sem, started_only=False):
        x, y, c, _ = _place()
        me = 4 * x + 2 * y + c
        peers = [((1 - x) if fx else x, (1 - y) if fy else y, (1 - c) if fc else c) for fx, fy, fc in flips]
        own = pltpu.make_async_copy(small_ref, gather_ref.at[me], local_sem)
        sent = [_remote(small_ref, gather_ref.at[me], send_sems.at[k], recv_sems.at[k], peer) for k, peer in enumerate(peers)]
        if started_only:
            return own, sent
        landed = [_remote(small_ref, gather_ref.at[4 * px + 2 * py + pc], send_sems.at[k], recv_sems.at[k], (px, py, pc))
                  for k, (px, py, pc) in enumerate(peers)]
        return own, sent, landed

    def start(ins, outs, sems):
        own, sent = copies(ins[0], outs[0], *sems, started_only=True)
        own.start()
        for cp in sent:
            cp.start()

    def finish(ins, outs, sems):
        own, sent, landed = copies(ins[0], outs[0], *sems)
        for cp in landed:
            cp.wait_recv()
        for cp in sent:
            cp.wait_send()
        own.wait()

    return _Rider([small], [jax.ShapeDtypeStruct((2 * N_CHIPS,) + small.shape, small.dtype)], {},
                  [pltpu.SemaphoreType.DMA((7,)), pltpu.SemaphoreType.DMA((7,)), pltpu.SemaphoreType.DMA], start, finish)


def _merge(a, b):
    na, nao, nas = len(a.operands), len(a.out_shapes), len(a.scratch)

    def start(ins, outs, sems):
        a.start(ins[:na], outs[:nao], sems[:nas])
        b.start(ins[na:], outs[nao:], sems[nas:])

    def finish(ins, outs, sems):
        a.finish(ins[:na], outs[:nao], sems[:nas])
        b.finish(ins[na:], outs[nao:], sems[nas:])

    def of_a(fn):
        return lambda ins, outs, sems: fn(ins[:na], outs[:nao], sems[:nas])

    def of_b(fn):
        return lambda ins, outs, sems: fn(ins[na:], outs[nao:], sems[nas:])

    aliases = {**a.aliases, **{na + k: nao + v for k, v in b.aliases.items()}}
    hooks = tuple((at, of_a(fn)) for at, fn in a.hooks) + tuple((at, of_b(fn)) for at, fn in b.hooks)
    return _Rider(a.operands + b.operands, a.out_shapes + b.out_shapes, aliases, a.scratch + b.scratch, start, finish, hooks)


def _scatter(sums):
    n = len(sums)

    def copies(ins, outs, send_sems, recv_sems, started_only=False):
        x, y, c, chips = _place()
        me = 2 * x + y
        slots = [2 * cx + cy for cx, cy in chips]
        sent = [_remote(ins[a].at[slots[k]], outs[a].at[me], send_sems.at[3 * a + k], recv_sems.at[3 * a + k], (*chips[k], c))
                for a in range(n) for k in X_Y_DIAGONAL]
        if started_only:
            return sent
        landed = [_remote(ins[a].at[slots[k]], outs[a].at[slots[k]], send_sems.at[3 * a + k], recv_sems.at[3 * a + k],
                          (*chips[k], c)) for a in range(n) for k in X_Y_DIAGONAL]
        return sent, landed

    def start(ins, outs, sems):
        for cp in copies(ins, outs, *sems, started_only=True):
            cp.start()

    def finish(ins, outs, sems):
        sent, landed = copies(ins, outs, *sems)
        for cp in landed:
            cp.wait_recv()
        for cp in sent:
            cp.wait_send()

    return _Rider(list(sums), _same(sums), {}, [pltpu.SemaphoreType.DMA((3 * n,)), pltpu.SemaphoreType.DMA((3 * n,))],
                  start, finish)


def _sibling_share(bufs):
    n = len(bufs)

    def copies(outs, send_sems, recv_sems, started_only=False):
        x, y, c, _ = _place()
        sent = [_remote(outs[a].at[c], outs[a].at[c], send_sems.at[a], recv_sems.at[a], (x, y, 1 - c)) for a in range(n)]
        if started_only:
            return sent
        landed = [_remote(outs[a].at[c], outs[a].at[1 - c], send_sems.at[a], recv_sems.at[a], (x, y, 1 - c)) for a in range(n)]
        return sent, landed

    def start(ins, outs, sems):
        for cp in copies(outs, *sems, started_only=True):
            cp.start()

    def finish(ins, outs, sems):
        sent, landed = copies(outs, *sems)
        for cp in landed:
            cp.wait_recv()
        for cp in sent:
            cp.wait_send()

    return _Rider(list(bufs), _same(bufs), {a: a for a in range(n)},
                  [pltpu.SemaphoreType.DMA((n,)), pltpu.SemaphoreType.DMA((n,))], start, finish)


def _pair_sum(core, part, received, name):
    _, k, rh, cols = part.shape

    def body(core_ref, p_ref, r_ref, o_ref):
        o_ref[...] = (p_ref[0].astype(F32) + r_ref[...].astype(F32)).astype(BF16)

    return pl.pallas_call(
        body, name=name,
        grid_spec=pltpu.PrefetchScalarGridSpec(
            num_scalar_prefetch=1, grid=(k,),
            in_specs=[pl.BlockSpec((1, 1, rh, cols), lambda j, core_ref: (core_ref[0], j, 0, 0)),
                      pl.BlockSpec((1, rh, cols), lambda j, core_ref: (j, 0, 0))],
            out_specs=pl.BlockSpec((1, rh, cols), lambda j, core_ref: (j, 0, 0))),
        out_shape=jax.ShapeDtypeStruct((k, rh, cols), BF16),
        compiler_params=_params(("parallel",), 32),
    )(core, part, received)


def _sum_leading(stack, steps, name):
    k, rows, cols = stack.shape
    tile = rows // steps

    def body(s_ref, o_ref):
        total = s_ref[0].astype(F32)
        for d in range(1, k):
            total = total + s_ref[d].astype(F32)
        o_ref[...] = total

    return pl.pallas_call(
        body, name=name, grid=(steps,),
        in_specs=[pl.BlockSpec((k, tile, cols), lambda i: (0, i, 0))],
        out_specs=pl.BlockSpec((tile, cols), lambda i: (i, 0)),
        out_shape=jax.ShapeDtypeStruct((rows, cols), F32),
        compiler_params=_params(("parallel",), 32),
    )(stack)


def _chip_sum(place, own, stack, steps, name):
    k, rows, cols = stack.shape
    tile = rows // steps

    def body(place_ref, own_ref, *refs):
        chip = place_ref[1]
        total = None
        for d in range(k):
            term = jnp.where(chip == d, own_ref[0], refs[d][0]).astype(F32)
            total = term if total is None else total + term
        refs[k][0] = total

    def other(d):
        return lambda i, place_ref: (jnp.where(place_ref[1] == d, (d + 1) % k, d), i, 0)

    return pl.pallas_call(
        body, name=name,
        grid_spec=pltpu.PrefetchScalarGridSpec(
            num_scalar_prefetch=1, grid=(steps,),
            in_specs=[pl.BlockSpec((1, tile, cols), lambda i, place_ref: (place_ref[1] % own.shape[0], i, 0))]
            + [pl.BlockSpec((1, tile, cols), other(d)) for d in range(k)],
            out_specs=pl.BlockSpec((1, tile, cols), lambda i, place_ref: (place_ref[0], i, 0))),
        out_shape=jax.ShapeDtypeStruct((2, rows, cols), F32),
        compiler_params=_params(("arbitrary",), 32),
    )(place, own, *([stack] * k))


def _adamw(w, g, row0, m, v, tile, name, after=None):
    rows, cols = w.shape
    first = row0 // tile
    assert rows % tile == 0 and row0 % tile == 0
    bc1 = 1.0 - ADAM_B1 ** ADAM_STEP
    bc2 = 1.0 - ADAM_B2 ** ADAM_STEP

    def body(w_ref, g_ref, m_ref, v_ref, *refs):
        go_ref, d_ref, mo_ref, vo_ref = refs[-4:]
        g = g_ref[...]
        m_new = ADAM_B1 * m_ref[...] + (1.0 - ADAM_B1) * g
        v_new = ADAM_B2 * v_ref[...] + (1.0 - ADAM_B2) * (g * g)
        go_ref[...] = g
        d_ref[...] = -ADAM_LR * ((m_new / bc1) / (jnp.sqrt(v_new / bc2) + ADAM_EPS) + ADAM_WD * w_ref[...])
        mo_ref[...] = m_new
        vo_ref[...] = v_new

    later = [] if after is None else [after]
    spec = pl.BlockSpec((tile, cols), lambda i: (i, 0))
    g_spec = pl.BlockSpec((tile, cols), lambda i: (first + i, 0))
    return pl.pallas_call(
        body, name=name, grid=(rows // tile,),
        in_specs=[spec, g_spec, spec, spec] + [pl.BlockSpec(memory_space=pl.ANY)] * len(later), out_specs=[spec] * 4,
        out_shape=[jax.ShapeDtypeStruct((rows, cols), F32)] * 4,
        compiler_params=_params(("parallel",), 32),
    )(w, g, m, v, *later)


SMALL = ("ffn1_norm", "mix_norm", "ffn2_norm", "final_norm", "pool_scale", "sink_logits", "pool_w")


def _pack_small(d, last_row=None):
    def part(n):
        flat = d[n].reshape(-1)
        flat = jnp.pad(flat, (0, -flat.shape[0] % (SUBLANES * LANES)))
        return flat.reshape(-1, LANES)

    last = jnp.zeros((SUBLANES, LANES), F32) if last_row is None else jnp.pad(last_row, ((SUBLANES - 1, 0), (0, 0)))
    packed = jnp.concatenate([part(n) for n in SMALL] + [last], axis=0)
    assert packed.shape[0] == SMALL_ROWS
    return packed


def _unpack_small(packed, like):
    out, row = {}, 0
    for n in SMALL:
        size = math.prod(like[n].shape)
        rows = -(-size // (SUBLANES * LANES)) * SUBLANES
        out[n] = packed[row:row + rows].reshape(-1)[:size].reshape(like[n].shape)
        row += rows
    return out


def kernel(x, ffn1_norm, ffn1_w_gate, ffn1_w_up, ffn1_w_down, mix_norm, w_in, sink_logits, pool_w, pool_scale, w_out, ffn2_norm, ffn2_w_gate, ffn2_w_up, ffn2_w_down, final_norm, loss_target, m_ffn1_norm, m_ffn1_w_gate, m_ffn1_w_up, m_ffn1_w_down, m_mix_norm, m_w_in, m_sink_logits, m_pool_w, m_pool_scale, m_w_out, m_ffn2_norm, m_ffn2_w_gate, m_ffn2_w_up, m_ffn2_w_down, m_final_norm, v_ffn1_norm, v_ffn1_w_gate, v_ffn1_w_up, v_ffn1_w_down, v_mix_norm, v_w_in, v_sink_logits, v_pool_w, v_pool_scale, v_w_out, v_ffn2_norm, v_ffn2_w_gate, v_ffn2_w_up, v_ffn2_w_down, v_final_norm):
    names = ("ffn1_norm", "ffn1_w_gate", "ffn1_w_up", "ffn1_w_down", "mix_norm", "w_in", "sink_logits", "pool_w",
             "pool_scale", "w_out", "ffn2_norm", "ffn2_w_gate", "ffn2_w_up", "ffn2_w_down", "final_norm")
    weights = dict(zip(names, (ffn1_norm, ffn1_w_gate, ffn1_w_up, ffn1_w_down, mix_norm, w_in, sink_logits, pool_w,
                               pool_scale, w_out, ffn2_norm, ffn2_w_gate, ffn2_w_up, ffn2_w_down, final_norm)))
    mom1 = dict(zip(names, (m_ffn1_norm, m_ffn1_w_gate, m_ffn1_w_up, m_ffn1_w_down, m_mix_norm, m_w_in, m_sink_logits,
                            m_pool_w, m_pool_scale, m_w_out, m_ffn2_norm, m_ffn2_w_gate, m_ffn2_w_up, m_ffn2_w_down,
                            m_final_norm)))
    mom2 = dict(zip(names, (v_ffn1_norm, v_ffn1_w_gate, v_ffn1_w_up, v_ffn1_w_down, v_mix_norm, v_w_in, v_sink_logits,
                            v_pool_w, v_pool_scale, v_w_out, v_ffn2_norm, v_ffn2_w_gate, v_ffn2_w_up, v_ffn2_w_down,
                            v_final_norm)))
    chip = (2 * lax.axis_index("x") + lax.axis_index("y")).astype(jnp.int32).reshape(1)
    place = jnp.concatenate([lax.axis_index("c").astype(jnp.int32).reshape(1), chip])

    def rows_of(t, n):
        return jnp.swapaxes(t[n][0], 0, 1) if n in TRANSPOSED else t[n][0]

    bufs = [_pack(chip, [rows_of(weights, n) for n in GROUPS[0]], "pack_ffn1"),
            _pack(chip, [jnp.concatenate([rows_of(weights, n) for n in GROUPS[1]], axis=0)], "pack_mix"),
            _pack(chip, [rows_of(weights, n) for n in GROUPS[2]], "pack_ffn2")]

    small_w = {"ffn1_norm": ffn1_norm, "mix_norm": mix_norm, "ffn2_norm": ffn2_norm,
               "final_norm": final_norm.reshape(1, D_MODEL), "pool_scale": pool_scale, "sink_logits": sink_logits,
               "pool_w": pool_w[0]}
    out_g, out_d, out_m, out_v = {}, {}, {}, {}

    def update(members, g, after=None):
        row0, done = 0, []
        for n in members:
            w = rows_of(weights, n)
            tile = FF_CHUNK // 4 if w.shape[0] == FF_CHUNK else math.gcd(IN_ROWS, OUT_ROWS)
            outs = _adamw(w, g, row0, rows_of(mom1, n), rows_of(mom2, n), tile, "adamw_" + n, after)
            row0 += w.shape[0]
            done.append(outs[0])
            for dst, t in zip((out_g, out_d, out_m, out_v), outs):
                dst[n] = (jnp.swapaxes(t, 0, 1) if n in TRANSPOSED else t).reshape(weights[n].shape)
        return done

    loss, grad_x, small_sum = _step(x[0], loss_target[0], bufs, small_w, place, update)
    small_outs = _adamw(_pack_small(weights), small_sum, 0, _pack_small(mom1), _pack_small(mom2), SMALL_ROWS, "adamw_small")
    for dst, packed in zip((out_g, out_d, out_m, out_v), small_outs):
        dst.update(_unpack_small(packed, weights))

    return (loss,grad_x.reshape(x.shape), *[out_g[n] for n in names], *[out_d[n] for n in names],
            *[out_m[n] for n in names], *[out_v[n] for n in names])
```
